```python
import jax, jax.numpy as jnp
from jax import lax
import numpy as np

D_MODEL = 1024
BATCH = 8
SEQ = 4096
DEPTH = 1

PLE_DIM = 256
HEAD_DIM = 64
DILATED_CFG = ((128, 1), (512, 4), (2048, 16))
N_GROUPS = 3
HEADS_PER_GROUP = 8
ATT_HEADS = N_GROUPS * HEADS_PER_GROUP
ATT_QKV = ATT_HEADS * HEAD_DIM
ATT_OUT = HEADS_PER_GROUP * HEAD_DIM
ROT_DIM = HEAD_DIM // 4
ROPE_THETA = 500000.0
GLA_HEADS = 4
GLA_DK = 128
GLA_DV = 256
GLA_KEY = GLA_HEADS * GLA_DK
GLA_VAL = GLA_HEADS * GLA_DV
GLA_GATE_RANK = 16
GLA_TAU = 16.0
GLA_CHUNK = 64
EPS = 1e-6
IN_SPLITS = (ATT_QKV, ATT_QKV, ATT_QKV, ATT_OUT,
             GLA_KEY, GLA_KEY, GLA_VAL, GLA_GATE_RANK, GLA_VAL,
             D_MODEL, D_MODEL)
IN_COLS = sum(IN_SPLITS)

kernel_name = "hybrid_dilated_swa_gla_gated_merge"


def _rmsnorm(x, g):
    xf = x.astype(jnp.float32)
    y = xf * lax.rsqrt(jnp.mean(xf * xf, axis=-1, keepdims=True) + EPS)
    return (y * g.astype(jnp.float32)).astype(x.dtype)


def _head_rmsnorm(t, g):
    return t * lax.rsqrt(jnp.mean(t * t, axis=-1, keepdims=True) + EPS) * g.astype(jnp.float32)


def _rotary_partial(t, pos):
    half = ROT_DIM // 2
    inv = jnp.power(jnp.float32(ROPE_THETA), -jnp.arange(half, dtype=jnp.float32) * 2.0 / ROT_DIM)
    ang = pos[..., None] * inv
    cos = jnp.cos(ang)[:, :, None, :]
    sin = jnp.sin(ang)[:, :, None, :]
    t1 = t[..., :half]
    t2 = t[..., half:ROT_DIM]
    return jnp.concatenate([t1 * cos - t2 * sin, t2 * cos + t1 * sin, t[..., ROT_DIM:]], axis=-1)


def _strided_window_attention(q, k, v, n_steps):
    N, L, H, hd = q.shape
    blk = n_steps
    nb = -(-L // blk)
    pad = nb * blk - L
    if pad:
        pw = ((0, 0), (0, pad), (0, 0), (0, 0))
        q, k, v = jnp.pad(q, pw), jnp.pad(k, pw), jnp.pad(v, pw)
    qb = q.reshape(N, nb, blk, H, hd)
    kb = k.reshape(N, nb, blk, H, hd)
    vb = v.reshape(N, nb, blk, H, hd)
    shift = ((0, 0), (1, 0), (0, 0), (0, 0), (0, 0))
    k2 = jnp.concatenate([jnp.pad(kb, shift)[:, :-1], kb], axis=2)
    v2 = jnp.concatenate([jnp.pad(vb, shift)[:, :-1], vb], axis=2)
    s = jnp.einsum('nbqhd,nbkhd->nbhqk', qb, k2) * (hd ** -0.5)
    a = jnp.arange(blk)[:, None]
    c = jnp.arange(2 * blk)[None, :]
    dist = blk + a - c
    in_band = (dist >= 0) & (dist <= n_steps)
    exists = (jnp.arange(nb)[:, None, None] > 0) | (c >= blk)[None]
    mask = in_band[None] & exists
    s = jnp.where(mask[None, :, None], s, -jnp.inf)
    m = jnp.max(s, axis=-1)
    pr = jnp.exp(s - m[..., None])
    l = jnp.sum(pr, axis=-1)
    o = jnp.einsum('nbhqk,nbkhd->nbqhd', pr, v2) / jnp.swapaxes(l, 2, 3)[..., None]
    o = o.reshape(N, nb * blk, H, hd)[:, :L]
    m = jnp.swapaxes(m, 2, 3).reshape(N, nb * blk, H)[:, :L]
    l = jnp.swapaxes(l, 2, 3).reshape(N, nb * blk, H)[:, :L]
    return o, m, l


def _dilated_group(q, k, v, window, dilation):
    B, S, H, hd = q.shape
    L = S // dilation

    def to_res(t):
        return t.reshape(B, L, dilation, H, hd).transpose(0, 2, 1, 3, 4).reshape(B * dilation, L, H, hd)

    o, m, l = _strided_window_attention(to_res(q), to_res(k), to_res(v), window // dilation)
    o = o.reshape(B, dilation, L, H, hd).transpose(0, 2, 1, 3, 4).reshape(B, S, H, hd)
    m = m.reshape(B, dilation, L, H).transpose(0, 2, 1, 3).reshape(B, S, H)
    l = l.reshape(B, dilation, L, H).transpose(0, 2, 1, 3).reshape(B, S, H)
    return o, m, l


def _dilated_attention(qa, ka, va, pos, gq, gk):
    B, S, _ = qa.shape
    q = _head_rmsnorm(qa.astype(jnp.float32).reshape(B, S, ATT_HEADS, HEAD_DIM), gq)
    k = _head_rmsnorm(ka.astype(jnp.float32).reshape(B, S, ATT_HEADS, HEAD_DIM), gk)
    v = va.astype(jnp.float32).reshape(B, S, ATT_HEADS, HEAD_DIM)
    q = _rotary_partial(q, pos)
    k = _rotary_partial(k, pos)
    q = q.reshape(B, S, N_GROUPS, HEADS_PER_GROUP, HEAD_DIM)
    k = k.reshape(B, S, N_GROUPS, HEADS_PER_GROUP, HEAD_DIM)
    v = v.reshape(B, S, N_GROUPS, HEADS_PER_GROUP, HEAD_DIM)
    outs, maxs, dens = [], [], []
    for g, (window, dilation) in enumerate(DILATED_CFG):
        o, m, l = _dilated_group(q[:, :, g], k[:, :, g], v[:, :, g], window, dilation)
        outs.append(o); maxs.append(m); dens.append(l)
    o = jnp.stack(outs)
    m = jnp.stack(maxs)
    l = jnp.stack(dens)
    w = l * jnp.exp(m - jnp.max(m, axis=0, keepdims=True))
    out = jnp.sum(w[..., None] * o, axis=0) / jnp.sum(w, axis=0)[..., None]
    return out.reshape(B, S, ATT_OUT)


def _gla_chunk_step(state, inp):
    q, k, v, lg = inp
    cum = jnp.cumsum(lg, axis=2)
    o_inter = jnp.einsum('bhid,bhde->bhie', q * jnp.exp(cum), state)
    C = q.shape[2]
    causal = jnp.tril(jnp.ones((C, C), dtype=bool))
    diff = cum[:, :, :, None, :] - cum[:, :, None, :, :]
    decay = jnp.exp(jnp.where(causal[:, :, None], diff, -jnp.inf))
    att = jnp.einsum('bhid,bhjd,bhijd->bhij', q, k, decay)
    o_intra = jnp.einsum('bhij,bhje->bhie', att, v)
    last = cum[:, :, -1:, :]
    new_state = jnp.exp(last[:, :, 0, :])[..., None] * state + \
        jnp.einsum('bhjd,bhje->bhde', k * jnp.exp(last - cum), v)
    return new_state, o_intra + o_inter


def _gla(qg, kg, vg, glr, w_g2, b_g, g_norm):
    B, S, _ = qg.shape
    nc = S // GLA_CHUNK
    q = qg.astype(jnp.float32).reshape(B, S, GLA_HEADS, GLA_DK) * (GLA_DK ** -0.5)
    k = kg.astype(jnp.float32).reshape(B, S, GLA_HEADS, GLA_DK)
    v = vg.astype(jnp.float32).reshape(B, S, GLA_HEADS, GLA_DV)
    logit = (glr @ w_g2 + b_g).astype(jnp.float32)
    lg = (jax.nn.log_sigmoid(logit) / GLA_TAU).reshape(B, S, GLA_HEADS, GLA_DK)

    def chunks(t):
        return t.reshape(B, nc, GLA_CHUNK, GLA_HEADS, t.shape[-1]).transpose(1, 0, 3, 2, 4)

    s0 = jnp.zeros((B, GLA_HEADS, GLA_DK, GLA_DV), jnp.float32)
    _, o = lax.scan(_gla_chunk_step, s0, (chunks(q), chunks(k), chunks(v), chunks(lg)))
    o = o.transpose(1, 0, 3, 2, 4).reshape(B, S, GLA_HEADS, GLA_DV)
    o = _head_rmsnorm(o, g_norm)
    return o.reshape(B, S, GLA_VAL)


def _fwd_setup_inputs(seed: int = 0) -> dict:
    key = jax.random.key(seed)
    ks = jax.random.split(key, 18)
    f32 = jnp.float32

    def nrm(k, shape, fan_in):
        return jax.random.normal(k, shape, f32) * (fan_in ** -0.5)

    def gain(k, shape):
        return 1.0 + 0.1 * jax.random.normal(k, shape, f32)

    return {
        "x": jax.random.normal(ks[0], (BATCH, SEQ, D_MODEL), f32),
        "p": jax.random.normal(ks[1], (DEPTH, BATCH, SEQ, PLE_DIM), f32),
        "positions": jnp.broadcast_to(jnp.arange(SEQ, dtype=jnp.int32), (BATCH, SEQ)),
        "norm_g": gain(ks[2], (DEPTH, D_MODEL)),
        "w_in": nrm(ks[3], (DEPTH, D_MODEL, IN_COLS), D_MODEL),
        "qk_norm_q": gain(ks[4], (DEPTH, HEAD_DIM)),
        "qk_norm_k": gain(ks[5], (DEPTH, HEAD_DIM)),
        "gla_gate_w2": nrm(ks[6], (DEPTH, GLA_GATE_RANK, GLA_KEY), GLA_GATE_RANK),
        "gla_gate_b": 0.1 * jax.random.normal(ks[7], (DEPTH, GLA_KEY), f32),
        "gla_norm_g": gain(ks[8], (DEPTH, GLA_DV)),
        "w_att_proj": nrm(ks[9], (DEPTH, ATT_OUT, D_MODEL), ATT_OUT),
        "w_gla_proj": nrm(ks[10], (DEPTH, GLA_VAL, D_MODEL), GLA_VAL),
        "w_out": nrm(ks[11], (DEPTH, D_MODEL, D_MODEL), D_MODEL),
        "ple_norm_g": gain(ks[12], (DEPTH, D_MODEL)),
        "w_ple_gate": nrm(ks[13], (DEPTH, D_MODEL, D_MODEL), D_MODEL),
        "w_ple": nrm(ks[14], (DEPTH, PLE_DIM, D_MODEL), PLE_DIM),
    }


def _fwd_reference(x, p, positions, norm_g, w_in, qk_norm_q, qk_norm_k, gla_gate_w2, gla_gate_b,
              gla_norm_g, w_att_proj, w_gla_proj, w_out, ple_norm_g, w_ple_gate, w_ple):
    pos = positions.astype(jnp.float32)
    split_idx = tuple(int(s) for s in np.cumsum(IN_SPLITS)[:-1])
    for i in range(DEPTH):
        h = _rmsnorm(x, norm_g[i])
        proj = h @ w_in[i]
        qa, ka, va, za, qg, kg, vg, glr, zg, gate_a, gate_b = jnp.split(proj, split_idx, axis=-1)
        att = _dilated_attention(qa, ka, va, pos, qk_norm_q[i], qk_norm_k[i]).astype(x.dtype)
        y_a = (att * jax.nn.silu(za)) @ w_att_proj[i]
        lin = _gla(qg, kg, vg, glr, gla_gate_w2[i], gla_gate_b[i], gla_norm_g[i]).astype(x.dtype)
        y_b = (lin * jax.nn.silu(zg)) @ w_gla_proj[i]
        y = jax.nn.sigmoid(gate_a) * y_a + jax.nn.sigmoid(gate_b) * y_b
        x = x + y @ w_out[i]
        ple_gate = jax.nn.sigmoid(_rmsnorm(x, ple_norm_g[i]) @ w_ple_gate[i])
        x = x + (p[i] @ w_ple[i]) * ple_gate
    return x


import jax as _jax
import jax.numpy as _jnp

TWIN_FORMAT = 'train_step'
FWD_PARAMS = ['x', 'p', 'positions', 'norm_g', 'w_in', 'qk_norm_q', 'qk_norm_k', 'gla_gate_w2', 'gla_gate_b', 'gla_norm_g', 'w_att_proj', 'w_gla_proj', 'w_out', 'ple_norm_g', 'w_ple_gate', 'w_ple']
TWIN_WEIGHTS = ['norm_g', 'w_in', 'qk_norm_q', 'qk_norm_k', 'gla_gate_w2', 'gla_gate_b', 'gla_norm_g', 'w_att_proj', 'w_gla_proj', 'w_out', 'ple_norm_g', 'w_ple_gate', 'w_ple']
TWIN_DIFF_INPUT = 'x'
TWIN_INPUTS = ['x', 'p', 'positions', 'norm_g', 'w_in', 'qk_norm_q', 'qk_norm_k', 'gla_gate_w2', 'gla_gate_b', 'gla_norm_g', 'w_att_proj', 'w_gla_proj', 'w_out', 'ple_norm_g', 'w_ple_gate', 'w_ple', 'loss_target', 'm_norm_g', 'm_w_in', 'm_qk_norm_q', 'm_qk_norm_k', 'm_gla_gate_w2', 'm_gla_gate_b', 'm_gla_norm_g', 'm_w_att_proj', 'm_w_gla_proj', 'm_w_out', 'm_ple_norm_g', 'm_w_ple_gate', 'm_w_ple', 'v_norm_g', 'v_w_in', 'v_qk_norm_q', 'v_qk_norm_k', 'v_gla_gate_w2', 'v_gla_gate_b', 'v_gla_norm_g', 'v_w_att_proj', 'v_w_gla_proj', 'v_w_out', 'v_ple_norm_g', 'v_w_ple_gate', 'v_w_ple']
TWIN_OUTPUTS = ['loss', 'grad_x', 'grad_norm_g', 'grad_w_in', 'grad_qk_norm_q', 'grad_qk_norm_k', 'grad_gla_gate_w2', 'grad_gla_gate_b', 'grad_gla_norm_g', 'grad_w_att_proj', 'grad_w_gla_proj', 'grad_w_out', 'grad_ple_norm_g', 'grad_w_ple_gate', 'grad_w_ple', 'delta_norm_g', 'delta_w_in', 'delta_qk_norm_q', 'delta_qk_norm_k', 'delta_gla_gate_w2', 'delta_gla_gate_b', 'delta_gla_norm_g', 'delta_w_att_proj', 'delta_w_gla_proj', 'delta_w_out', 'delta_ple_norm_g', 'delta_w_ple_gate', 'delta_w_ple', 'new_m_norm_g', 'new_m_w_in', 'new_m_qk_norm_q', 'new_m_qk_norm_k', 'new_m_gla_gate_w2', 'new_m_gla_gate_b', 'new_m_gla_norm_g', 'new_m_w_att_proj', 'new_m_w_gla_proj', 'new_m_w_out', 'new_m_ple_norm_g', 'new_m_w_ple_gate', 'new_m_w_ple', 'new_v_norm_g', 'new_v_w_in', 'new_v_qk_norm_q', 'new_v_qk_norm_k', 'new_v_gla_gate_w2', 'new_v_gla_gate_b', 'new_v_gla_norm_g', 'new_v_w_att_proj', 'new_v_w_gla_proj', 'new_v_w_out', 'new_v_ple_norm_g', 'new_v_w_ple_gate', 'new_v_w_ple']
TWIN_LEAF_KINDS = {'loss': 'loss', 'grad_x': 'grad_x', 'grad_norm_g': 'grad_w', 'grad_w_in': 'grad_w', 'grad_qk_norm_q': 'grad_w', 'grad_qk_norm_k': 'grad_w', 'grad_gla_gate_w2': 'grad_w', 'grad_gla_gate_b': 'grad_w', 'grad_gla_norm_g': 'grad_w', 'grad_w_att_proj': 'grad_w', 'grad_w_gla_proj': 'grad_w', 'grad_w_out': 'grad_w', 'grad_ple_norm_g': 'grad_w', 'grad_w_ple_gate': 'grad_w', 'grad_w_ple': 'grad_w', 'delta_norm_g': 'delta_w', 'delta_w_in': 'delta_w', 'delta_qk_norm_q': 'delta_w', 'delta_qk_norm_k': 'delta_w', 'delta_gla_gate_w2': 'delta_w', 'delta_gla_gate_b': 'delta_w', 'delta_gla_norm_g': 'delta_w', 'delta_w_att_proj': 'delta_w', 'delta_w_gla_proj': 'delta_w', 'delta_w_out': 'delta_w', 'delta_ple_norm_g': 'delta_w', 'delta_w_ple_gate': 'delta_w', 'delta_w_ple': 'delta_w', 'new_m_norm_g': 'new_m', 'new_m_w_in': 'new_m', 'new_m_qk_norm_q': 'new_m', 'new_m_qk_norm_k': 'new_m', 'new_m_gla_gate_w2': 'new_m', 'new_m_gla_gate_b': 'new_m', 'new_m_gla_norm_g': 'new_m', 'new_m_w_att_proj': 'new_m', 'new_m_w_gla_proj': 'new_m', 'new_m_w_out': 'new_m', 'new_m_ple_norm_g': 'new_m', 'new_m_w_ple_gate': 'new_m', 'new_m_w_ple': 'new_m', 'new_v_norm_g': 'new_v', 'new_v_w_in': 'new_v', 'new_v_qk_norm_q': 'new_v', 'new_v_qk_norm_k': 'new_v', 'new_v_gla_gate_w2': 'new_v', 'new_v_gla_gate_b': 'new_v', 'new_v_gla_norm_g': 'new_v', 'new_v_w_att_proj': 'new_v', 'new_v_w_gla_proj': 'new_v', 'new_v_w_out': 'new_v', 'new_v_ple_norm_g': 'new_v', 'new_v_w_ple_gate': 'new_v', 'new_v_w_ple': 'new_v'}


def _forward(args):
    return _fwd_reference(*[args[k] for k in FWD_PARAMS])


def _output_shape():
    def fwd():
        inp = _fwd_setup_inputs(0)
        return _fwd_reference(*[inp[k] for k in FWD_PARAMS])
    out = _jax.eval_shape(fwd)
    return out.shape, out.dtype

N_MICROBATCH = 1
ADAM_LR = 0.001
ADAM_B1 = 0.9
ADAM_B2 = 0.999
ADAM_EPS = 1e-08
ADAM_WD = 0.01
ADAM_STEP = 10
PER_EXAMPLE_BATCH_AXIS = {'x': 0, 'p': 1, 'positions': 0, 'loss_target': 0}
SHARED_INPUTS = []
_WEIGHT_DTYPES = {'norm_g': _jnp.float32, 'w_in': _jnp.float32, 'qk_norm_q': _jnp.float32, 'qk_norm_k': _jnp.float32, 'gla_gate_w2': _jnp.float32, 'gla_gate_b': _jnp.float32, 'gla_norm_g': _jnp.float32, 'w_att_proj': _jnp.float32, 'w_gla_proj': _jnp.float32, 'w_out': _jnp.float32, 'ple_norm_g': _jnp.float32, 'w_ple_gate': _jnp.float32, 'w_ple': _jnp.float32}
MOMENT_SCALE = {'norm_g': 4.291794e+00, 'w_in': 7.591949e-02, 'qk_norm_q': 3.665909e-01, 'qk_norm_k': 3.518877e-01, 'gla_gate_w2': 1.696899e-02, 'gla_gate_b': 7.395914e-02, 'gla_norm_g': 1.359970e+01, 'w_att_proj': 1.793336e-02, 'w_gla_proj': 1.411135e-01, 'w_out': 1.379903e-01, 'ple_norm_g': 9.820474e-01, 'w_ple_gate': 6.824426e-02, 'w_ple': 5.175526e-01}


def _to_microbatches(a, axis):
    t = _jnp.moveaxis(a, axis, 0)
    t = t.reshape((N_MICROBATCH, t.shape[0] // N_MICROBATCH) + t.shape[1:])
    return _jnp.moveaxis(t, 1, axis + 1)


def setup_inputs(seed: int = 0) -> dict:
    inp = _fwd_setup_inputs(seed)
    key = _jax.random.fold_in(_jax.random.key(seed), 7919)
    shape, _ = _output_shape()
    out = dict(inp)
    out["loss_target"] = _jax.random.normal(_jax.random.fold_in(key, 0), shape, _jnp.float32)
    for i, name in enumerate(TWIN_WEIGHTS):
        w = inp[name].astype(_jnp.float32)
        if MOMENT_SCALE is None:
            s = _jnp.sqrt(_jnp.mean(_jnp.square(w)) + 1e-30)
        else:
            s = MOMENT_SCALE[name]
        km, kv = _jax.random.split(_jax.random.fold_in(key, i + 1))
        out[name] = w
        out["m_" + name] = s * _jax.random.normal(km, w.shape, _jnp.float32)
        out["v_" + name] = (s * s) * _jax.random.uniform(kv, w.shape, _jnp.float32, 0.5, 1.5)
    if N_MICROBATCH > 1:
        for name, axis in PER_EXAMPLE_BATCH_AXIS.items():
            out[name] = _to_microbatches(out[name], axis)
    return {'x': out['x'], 'p': out['p'], 'positions': out['positions'], 'norm_g': out['norm_g'], 'w_in': out['w_in'], 'qk_norm_q': out['qk_norm_q'], 'qk_norm_k': out['qk_norm_k'], 'gla_gate_w2': out['gla_gate_w2'], 'gla_gate_b': out['gla_gate_b'], 'gla_norm_g': out['gla_norm_g'], 'w_att_proj': out['w_att_proj'], 'w_gla_proj': out['w_gla_proj'], 'w_out': out['w_out'], 'ple_norm_g': out['ple_norm_g'], 'w_ple_gate': out['w_ple_gate'], 'w_ple': out['w_ple'], 'loss_target': out['loss_target'], 'm_norm_g': out['m_norm_g'], 'm_w_in': out['m_w_in'], 'm_qk_norm_q': out['m_qk_norm_q'], 'm_qk_norm_k': out['m_qk_norm_k'], 'm_gla_gate_w2': out['m_gla_gate_w2'], 'm_gla_gate_b': out['m_gla_gate_b'], 'm_gla_norm_g': out['m_gla_norm_g'], 'm_w_att_proj': out['m_w_att_proj'], 'm_w_gla_proj': out['m_w_gla_proj'], 'm_w_out': out['m_w_out'], 'm_ple_norm_g': out['m_ple_norm_g'], 'm_w_ple_gate': out['m_w_ple_gate'], 'm_w_ple': out['m_w_ple'], 'v_norm_g': out['v_norm_g'], 'v_w_in': out['v_w_in'], 'v_qk_norm_q': out['v_qk_norm_q'], 'v_qk_norm_k': out['v_qk_norm_k'], 'v_gla_gate_w2': out['v_gla_gate_w2'], 'v_gla_gate_b': out['v_gla_gate_b'], 'v_gla_norm_g': out['v_gla_norm_g'], 'v_w_att_proj': out['v_w_att_proj'], 'v_w_gla_proj': out['v_w_gla_proj'], 'v_w_out': out['v_w_out'], 'v_ple_norm_g': out['v_ple_norm_g'], 'v_w_ple_gate': out['v_w_ple_gate'], 'v_w_ple': out['v_w_ple']}


def _loss(weights, diff, rest, loss_target):
    with _jax.named_scope("forward"):
        args = {**rest, TWIN_DIFF_INPUT: diff, **{k: w.astype(_WEIGHT_DTYPES[k]) for k, w in weights.items()}}
        y = _forward(args)
    with _jax.named_scope("loss_head"):
        err = _jnp.square(y.astype(_jnp.float32) - loss_target)
        return 0.5 * _jnp.sum(_jnp.mean(err, axis=-1)) if err.ndim else 0.5 * err


def _adamw(w, g, m, v):
    m = ADAM_B1 * m + (1.0 - ADAM_B1) * g
    v = ADAM_B2 * v + (1.0 - ADAM_B2) * _jnp.square(g)
    m_hat = m / (1.0 - ADAM_B1 ** ADAM_STEP)
    v_hat = v / (1.0 - ADAM_B2 ** ADAM_STEP)
    delta = -ADAM_LR * (m_hat / (_jnp.sqrt(v_hat) + ADAM_EPS) + ADAM_WD * w)
    return delta, m, v


def reference(x, p, positions, norm_g, w_in, qk_norm_q, qk_norm_k, gla_gate_w2, gla_gate_b, gla_norm_g, w_att_proj, w_gla_proj, w_out, ple_norm_g, w_ple_gate, w_ple, loss_target, m_norm_g, m_w_in, m_qk_norm_q, m_qk_norm_k, m_gla_gate_w2, m_gla_gate_b, m_gla_norm_g, m_w_att_proj, m_w_gla_proj, m_w_out, m_ple_norm_g, m_w_ple_gate, m_w_ple, v_norm_g, v_w_in, v_qk_norm_q, v_qk_norm_k, v_gla_gate_w2, v_gla_gate_b, v_gla_norm_g, v_w_att_proj, v_w_gla_proj, v_w_out, v_ple_norm_g, v_w_ple_gate, v_w_ple):
    given = dict(x=x, p=p, positions=positions, norm_g=norm_g, w_in=w_in, qk_norm_q=qk_norm_q, qk_norm_k=qk_norm_k, gla_gate_w2=gla_gate_w2, gla_gate_b=gla_gate_b, gla_norm_g=gla_norm_g, w_att_proj=w_att_proj, w_gla_proj=w_gla_proj, w_out=w_out, ple_norm_g=ple_norm_g, w_ple_gate=w_ple_gate, w_ple=w_ple, loss_target=loss_target, m_norm_g=m_norm_g, m_w_in=m_w_in, m_qk_norm_q=m_qk_norm_q, m_qk_norm_k=m_qk_norm_k, m_gla_gate_w2=m_gla_gate_w2, m_gla_gate_b=m_gla_gate_b, m_gla_norm_g=m_gla_norm_g, m_w_att_proj=m_w_att_proj, m_w_gla_proj=m_w_gla_proj, m_w_out=m_w_out, m_ple_norm_g=m_ple_norm_g, m_w_ple_gate=m_w_ple_gate, m_w_ple=m_w_ple, v_norm_g=v_norm_g, v_w_in=v_w_in, v_qk_norm_q=v_qk_norm_q, v_qk_norm_k=v_qk_norm_k, v_gla_gate_w2=v_gla_gate_w2, v_gla_gate_b=v_gla_gate_b, v_gla_norm_g=v_gla_norm_g, v_w_att_proj=v_w_att_proj, v_w_gla_proj=v_w_gla_proj, v_w_out=v_w_out, v_ple_norm_g=v_ple_norm_g, v_w_ple_gate=v_w_ple_gate, v_w_ple=v_w_ple)
    weights = {n: given[n] for n in TWIN_WEIGHTS}
    shared = {n: given[n] for n in SHARED_INPUTS}
    per_example = {n: given[n] for n in ['x', 'p', 'positions']}
    grad_fn = _jax.value_and_grad(_loss, argnums=(0, 1))

    def one_microbatch(ex, loss_target):
        ex = dict(ex)
        diff = ex.pop(TWIN_DIFF_INPUT)
        return grad_fn(weights, diff, {**shared, **ex}, loss_target)

    if N_MICROBATCH == 1:
        loss, (grad_w, grad_x) = one_microbatch(per_example, given["loss_target"])
    else:
        def body(carry, xs):
            loss_sum, grad_sum = carry
            l_k, (gw_k, gx_k) = one_microbatch(xs[0], xs[1])
            with _jax.named_scope("update"):
                return (loss_sum + l_k, _jax.tree.map(_jnp.add, grad_sum, gw_k)), gx_k

        init = (_jnp.zeros((), _jnp.float32), _jax.tree.map(_jnp.zeros_like, weights))
        (loss, grad_w), grad_x = _jax.lax.scan(body, init, (per_example, given["loss_target"]))
    with _jax.named_scope("update"):
        delta_w, new_m, new_v = {}, {}, {}
        for n in TWIN_WEIGHTS:
            delta_w[n], new_m[n], new_v[n] = _adamw(weights[n], grad_w[n], given["m_" + n], given["v_" + n])
    return (loss, grad_x, *[grad_w[n] for n in TWIN_WEIGHTS], *[delta_w[n] for n in TWIN_WEIGHTS],
            *[new_m[n] for n in TWIN_WEIGHTS], *[new_v[n] for n in TWIN_WEIGHTS])
```

```python
import functools

import jax
import jax.numpy as jnp
from jax import lax
from jax.experimental import pallas as pl
from jax.experimental.pallas import tpu as pltpu

F32 = jnp.float32
BF16 = jnp.bfloat16
S = jax.ShapeDtypeStruct

T = 4096
D = 1024
NDEV = 8
HD = 64
ATT_W = 512
ATT_QKV = 1536
DILATIONS = (1, 4, 16)
BLK = 128
GH, GDK, GDV = 4, 128, 256
GLA_C = 128
PLE = 256
EPS = 1e-6
ROT_DIM = 16
ROPE_THETA = 500000.0
GLA_TAU = 16.0
W_IN_COLS = 10256
W_IN_SHARD = 1282

C_QA, C_KA, C_VA, C_ZA, C_QG, C_KG, C_VG, C_ZG, C_GA, C_GB, C_GLR = (
    0, 1536, 3072, 4608, 5120, 5632, 6144, 7168, 8192, 9216, 10240)
NCOL = 10368
GLR_ORIG = 7168
GLR_N = 16

ADAM_LR, ADAM_B1, ADAM_B2, ADAM_EPS, ADAM_WD, ADAM_STEP = 0.001, 0.9, 0.999, 1e-08, 0.01, 10

MESH = pl.DeviceIdType.MESH


def _sigmoid(z):
    return 1.0 / (1.0 + jnp.exp(-z))


def _dot(a, b, dims):
    return lax.dot_general(a, b, (dims, ((), ())), preferred_element_type=F32)


def _nn(a, b):
    return _dot(a, b, ((1,), (0,)))


def _nt(a, b):
    return _dot(a, b, ((1,), (1,)))


def _tn(a, b):
    return _dot(a, b, ((0,), (0,)))


def _mm(a, b, *, mode, name, tm, tn, tk, out_dtype=F32, res=None):
    if mode == "nn":
        (m, k), n = a.shape, b.shape[1]
        a_spec = pl.BlockSpec((tm, tk), lambda i, j, l: (i, l))
        b_spec = pl.BlockSpec((tk, tn), lambda i, j, l: (l, j))
        dot = _nn
    elif mode == "nt":
        (m, k), n = a.shape, b.shape[0]
        a_spec = pl.BlockSpec((tm, tk), lambda i, j, l: (i, l))
        b_spec = pl.BlockSpec((tn, tk), lambda i, j, l: (j, l))
        dot = _nt
    else:
        (k, m), n = a.shape, b.shape[1]
        a_spec = pl.BlockSpec((tk, tm), lambda i, j, l: (l, i))
        b_spec = pl.BlockSpec((tk, tn), lambda i, j, l: (l, j))
        dot = _tn
    assert m % tm == 0 and n % tn == 0 and k % tk == 0, (name, m, n, k)
    nk = k // tk
    o_spec = pl.BlockSpec((tm, tn), lambda i, j, l: (i, j))
    in_specs = [a_spec, b_spec]
    args = [a, b]
    if res is not None:
        in_specs.append(o_spec)
        args.append(res)

    def body(*refs):
        a_ref, b_ref = refs[0], refs[1]
        r_ref = refs[2] if res is not None else None
        o_ref = refs[3] if res is not None else refs[2]
        part = dot(a_ref[...].astype(BF16), b_ref[...].astype(BF16))

        def finish(val):
            if r_ref is not None:
                val = val + r_ref[...]
            o_ref[...] = val.astype(out_dtype)

        if nk == 1:
            finish(part)
        else:
            acc = refs[-1]
            l = pl.program_id(2)

            @pl.when(l == 0)
            def _():
                acc[...] = part

            @pl.when(l > 0)
            def _():
                acc[...] += part

            @pl.when(l == nk - 1)
            def _():
                finish(acc[...])

    return pl.pallas_call(
        body, name=name, grid=(m // tm, n // tn, nk),
        in_specs=in_specs, out_specs=o_spec, out_shape=S((m, n), out_dtype),
        scratch_shapes=[pltpu.VMEM((tm, tn), F32)] if nk > 1 else [],
        compiler_params=pltpu.CompilerParams(dimension_semantics=("parallel", "parallel", "arbitrary")),
    )(*args)


def _rows(arr, width=None, cblk=0):
    return ("rows", arr, arr.shape[1] if width is None else width, cblk)


def _whole(arr):
    return ("whole", arr)


def _rowcall(body, name, tt, ins, outs, scratch=()):
    in_specs, args = [], []
    for spec in ins:
        if spec[0] == "rows":
            _, arr, width, cblk = spec
            in_specs.append(pl.BlockSpec((tt, width), functools.partial(lambda i, c: (i, c), c=cblk)))
        else:
            arr = spec[1]
            in_specs.append(pl.BlockSpec(arr.shape, functools.partial(lambda i, nd: (0,) * nd, nd=arr.ndim)))
        args.append(arr)
    out_specs, out_shape = [], []
    for kind, shape, dtype in outs:
        if kind == "rows":
            out_specs.append(pl.BlockSpec((tt, shape), lambda i: (i, 0)))
            out_shape.append(S((T, shape), dtype))
        else:
            out_specs.append(pl.BlockSpec(shape, functools.partial(lambda i, nd: (0,) * nd, nd=len(shape))))
            out_shape.append(S(shape, dtype))
    return pl.pallas_call(
        body, name=name, grid=(T // tt,), in_specs=in_specs, out_specs=out_specs, out_shape=out_shape,
        scratch_shapes=list(scratch),
        compiler_params=pltpu.CompilerParams(dimension_semantics=("arbitrary",)),
    )(*args)


def _rms_fwd(x, g, name):
    def body(x_ref, g_ref, h_ref):
        xf = x_ref[...]
        r = lax.rsqrt(jnp.mean(xf * xf, axis=-1, keepdims=True) + EPS)
        h_ref[...] = (xf * r * g_ref[...]).astype(BF16)

    return _rowcall(body, name, 512, [_rows(x), _whole(g)], [("rows", D, BF16)])[0]


def _rms_bwd(dn, x, g, skip, name):
    def body(dn_ref, x_ref, g_ref, s_ref, dx_ref, dxb_ref, dg_ref):
        xf = x_ref[...]
        r = lax.rsqrt(jnp.mean(xf * xf, axis=-1, keepdims=True) + EPS)
        dn_ = dn_ref[...]
        u = dn_ * g_ref[...]
        dx = s_ref[...] + r * u - xf * (r * r * r) * jnp.mean(u * xf, axis=-1, keepdims=True)
        dx_ref[...] = dx
        dxb_ref[...] = dx.astype(BF16)
        part = jnp.sum(dn_ * xf * r, axis=0, keepdims=True)

        @pl.when(pl.program_id(0) == 0)
        def _():
            dg_ref[...] = part

        @pl.when(pl.program_id(0) > 0)
        def _():
            dg_ref[...] += part

    return _rowcall(body, name, 256, [_rows(dn), _rows(x), _whole(g), _rows(skip)],
                    [("rows", D, F32), ("rows", D, BF16), ("acc", (1, D), F32)])


def _rot_tables(pos_ref, inv_ref):
    lane = lax.broadcasted_iota(jnp.int32, (1, 128), 1) % HD
    ang = pos_ref[...] * inv_ref[...]
    cos, sin = jnp.cos(ang), jnp.sin(ang)
    c = jnp.where(lane < ROT_DIM, cos, 1.0)
    sp = jnp.where((lane >= ROT_DIM // 2) & (lane < ROT_DIM), sin, 0.0)
    sm = jnp.where(lane < ROT_DIM // 2, -sin, 0.0)
    return c, sp, sm


def _pair_norm(t):
    low = lax.broadcasted_iota(jnp.int32, (1, 128), 1) < HD
    sq = t * t
    s_all = jnp.sum(sq, axis=-1, keepdims=True)
    s_low = jnp.sum(jnp.where(low, sq, 0.0), axis=-1, keepdims=True)
    ms = jnp.where(low, s_low, s_all - s_low) * (1.0 / HD)
    return lax.rsqrt(ms + EPS)


def _pair_mean(t):
    low = lax.broadcasted_iota(jnp.int32, (1, 128), 1) < HD
    s_all = jnp.sum(t, axis=-1, keepdims=True)
    s_low = jnp.sum(jnp.where(low, t, 0.0), axis=-1, keepdims=True)
    return jnp.where(low, s_low, s_all - s_low) * (1.0 / HD)


def _qk_prep(proj, pos, inv, gq, gk):
    def body(q_ref, k_ref, v_ref, pos_ref, inv_ref, gq_ref, gk_ref, qo_ref, ko_ref, vo_ref):
        c, sp, sm = _rot_tables(pos_ref, inv_ref)
        for src, g_ref, dst in ((q_ref, gq_ref, qo_ref), (k_ref, gk_ref, ko_ref)):
            for j in range(ATT_QKV // 128):
                cols = slice(j * 128, (j + 1) * 128)
                t = src[:, cols]
                n = t * _pair_norm(t) * g_ref[...]
                dst[:, cols] = (n * c + pltpu.roll(n, 8, 1) * sp + pltpu.roll(n, 120, 1) * sm).astype(BF16)
        vo_ref[...] = v_ref[...].astype(BF16)

    return _rowcall(
        body, "qk_prep", 256,
        [_rows(proj, ATT_QKV, 0), _rows(proj, ATT_QKV, 1), _rows(proj, ATT_QKV, 2), _rows(pos), _whole(inv),
         _whole(gq), _whole(gk)],
        [("rows", ATT_QKV, BF16)] * 3)


def _qk_bwd(proj, pos, inv, gq, gk, dqs, dks, dvs):
    def body(q_ref, k_ref, pos_ref, inv_ref, gq_ref, gk_ref, dq0, dq1, dq2, dk0, dk1, dk2, dv0, dv1, dv2,
             dqa_ref, dka_ref, dva_ref, dgq_ref, dgk_ref):
        c, sp, sm = _rot_tables(pos_ref, inv_ref)
        first = pl.program_id(0) == 0
        for src, g_ref, drefs, dst, dg_ref in ((q_ref, gq_ref, (dq0, dq1, dq2), dqa_ref, dgq_ref),
                                               (k_ref, gk_ref, (dk0, dk1, dk2), dka_ref, dgk_ref)):
            dg = jnp.zeros((1, 128), F32)
            for j in range(ATT_QKV // 128):
                cols = slice(j * 128, (j + 1) * 128)
                grp, sub = divmod(j * 128, ATT_W)
                d_rot = drefs[grp][:, sub:sub + 128]
                dn = d_rot * c + pltpu.roll(d_rot * sp, 120, 1) + pltpu.roll(d_rot * sm, 8, 1)
                t = src[:, cols]
                r = _pair_norm(t)
                u = dn * g_ref[...]
                dst[:, cols] = (r * u - t * (r * r * r) * _pair_mean(u * t)).astype(BF16)
                dg = dg + jnp.sum(dn * t * r, axis=0, keepdims=True)
            dg = dg + pltpu.roll(dg, HD, 1)

            @pl.when(first)
            def _():
                dg_ref[...] = dg

            @pl.when(jnp.logical_not(first))
            def _():
                dg_ref[...] += dg

        for grp, dref in enumerate((dv0, dv1, dv2)):
            dva_ref[:, grp * ATT_W:(grp + 1) * ATT_W] = dref[...].astype(BF16)

    return _rowcall(
        body, "qk_bwd", 256,
        [_rows(proj, ATT_QKV, 0), _rows(proj, ATT_QKV, 1), _rows(pos), _whole(inv), _whole(gq), _whole(gk)]
        + [_rows(a) for a in (*dqs, *dks, *dvs)],
        [("rows", ATT_QKV, BF16)] * 3 + [("acc", (1, 128), F32)] * 2)


def _split_heads(t):
    low = lax.broadcasted_iota(jnp.int32, (1, 128), 1) < HD
    zero = jnp.zeros_like(t)
    return jnp.concatenate([jnp.where(low, t, zero), jnp.where(low, zero, t)], axis=0)


def _join_heads(t2):
    low = lax.broadcasted_iota(jnp.int32, (1, 128), 1) < HD
    n = t2.shape[0] // 2
    return jnp.where(low, t2[:n], t2[n:])


def _band_masks():
    row = lax.broadcasted_iota(jnp.int32, (BLK, 2 * BLK), 0)
    col = lax.broadcasted_iota(jnp.int32, (BLK, 2 * BLK), 1) % BLK
    return col <= row, col >= row


def _per_head(col_a, col_b):
    first = lax.broadcasted_iota(jnp.int32, (1, 2 * BLK), 1) < BLK
    return jnp.where(first, col_a, col_b)


def _att_fwd(qn, kn, vb, grp, name):
    dil = DILATIONS[grp]
    seq = T // dil
    nb = seq // BLK
    view = lambda a: a.reshape(seq, dil * ATT_QKV)
    scale = HD ** -0.5

    def body(q_ref, kp_ref, kc_ref, vp_ref, vc_ref, o_ref, lse_ref):
        has_prev = pl.program_id(1) > 0
        m_cur, m_prev = _band_masks()
        m_prev = m_prev & has_prev
        for j in range(ATT_W // 128):
            cols = slice(j * 128, (j + 1) * 128)
            q = q_ref[:, cols]
            s_c = jnp.where(m_cur, _nt(q, _split_heads(kc_ref[:, cols])) * scale, -jnp.inf)
            s_p = jnp.where(m_prev, _nt(q, _split_heads(kp_ref[:, cols])) * scale, -jnp.inf)
            mx = []
            for half in (slice(0, BLK), slice(BLK, 2 * BLK)):
                mx.append(jnp.maximum(jnp.max(s_c[:, half], axis=-1, keepdims=True),
                                      jnp.max(s_p[:, half], axis=-1, keepdims=True)))
            m2 = _per_head(*mx)
            p_c = jnp.exp(s_c - m2)
            p_p = jnp.exp(s_p - m2)
            den = []
            for half in (slice(0, BLK), slice(BLK, 2 * BLK)):
                den.append(jnp.sum(p_c[:, half], axis=-1, keepdims=True) + jnp.sum(p_p[:, half], axis=-1, keepdims=True))
            acc = _nn(p_c.astype(BF16), _split_heads(vc_ref[:, cols])) + _nn(p_p.astype(BF16), _split_heads(vp_ref[:, cols]))
            low = lax.broadcasted_iota(jnp.int32, (1, 128), 1) < HD
            o_ref[:, cols] = acc / jnp.where(low, den[0], den[1])
            lse_ref[:, cols] = jnp.where(low, mx[0] + jnp.log(den[0]), mx[1] + jnp.log(den[1]))

    cur = lambda r, i: (i, r * 3 + grp)
    prev = lambda r, i: (jnp.maximum(i - 1, 0), r * 3 + grp)
    blk = lambda f: pl.BlockSpec((BLK, ATT_W), f)
    out = pl.BlockSpec((BLK, ATT_W), lambda r, i: (i, r))
    o, lse = pl.pallas_call(
        body, name=name, grid=(dil, nb),
        in_specs=[blk(cur), blk(prev), blk(cur), blk(prev), blk(cur)],
        out_specs=[out, out], out_shape=[S((seq, dil * ATT_W), F32)] * 2,
        compiler_params=pltpu.CompilerParams(dimension_semantics=("parallel", "arbitrary")),
    )(view(qn), view(kn), view(kn), view(vb), view(vb))
    return o.reshape(T, ATT_W), lse.reshape(T, ATT_W)


def _att_bwd(qn, kn, vb, datt, att, lse, grp, name):
    dil = DILATIONS[grp]
    seq = T // dil
    nb = seq // BLK
    view3 = lambda a: a.reshape(seq, dil * ATT_QKV)
    view1 = lambda a: a.reshape(seq, dil * ATT_W)
    scale = HD ** -0.5

    def body(q0_ref, q1_ref, kp_ref, kc_ref, vp_ref, vc_ref, do0_ref, do1_ref, o0_ref, o1_ref, l0_ref, l1_ref,
             dq_ref, dk_ref, dv_ref):
        i = pl.program_id(1)
        has_prev = i > 0
        has_next = i < nb - 1
        m_cur, m_prev = _band_masks()
        low = lax.broadcasted_iota(jnp.int32, (1, 128), 1) < HD

        def stats(do_ref, o_ref, l_ref, cols):
            prod = do_ref[:, cols] * o_ref[:, cols]
            d_all = jnp.sum(prod, axis=-1, keepdims=True)
            d_low = jnp.sum(jnp.where(low, prod, 0.0), axis=-1, keepdims=True)
            lse_t = l_ref[:, cols]
            return _per_head(d_low, d_all - d_low), _per_head(lse_t[:, 0:1], lse_t[:, HD:HD + 1])

        def pair(q, k2, v2, do, dsum, lse2, mask):
            s = _nt(q, k2) * scale
            p = jnp.where(mask, jnp.exp(s - lse2), 0.0)
            ds = p * (_nt(do, v2) - dsum) * scale
            return p.astype(BF16), ds.astype(BF16)

        for j in range(ATT_W // 128):
            cols = slice(j * 128, (j + 1) * 128)
            q0, q1 = q0_ref[:, cols], q1_ref[:, cols]
            do0, do1 = do0_ref[:, cols].astype(BF16), do1_ref[:, cols].astype(BF16)
            kc2, kp2 = _split_heads(kc_ref[:, cols]), _split_heads(kp_ref[:, cols])
            vc2, vp2 = _split_heads(vc_ref[:, cols]), _split_heads(vp_ref[:, cols])
            d0, l0 = stats(do0_ref, o0_ref, l0_ref, cols)
            d1, l1 = stats(do1_ref, o1_ref, l1_ref, cols)
            p_a, ds_a = pair(q0, kc2, vc2, do0, d0, l0, m_cur)
            _, ds_b = pair(q0, kp2, vp2, do0, d0, l0, m_prev & has_prev)
            p_c, ds_c = pair(q1, kc2, vc2, do1, d1, l1, m_prev & has_next)
            dq_ref[:, cols] = _nn(ds_a, kc2) + _nn(ds_b, kp2)
            dk_ref[:, cols] = _join_heads(_tn(ds_a, q0) + _tn(ds_c, q1))
            dv_ref[:, cols] = _join_heads(_tn(p_a, do0) + _tn(p_c, do1))

    def spec(shift, width_blocks, g):
        def f(r, i):
            row = jnp.clip(i + shift, 0, nb - 1)
            return (row, r * width_blocks + g)
        return pl.BlockSpec((BLK, ATT_W), f)

    q0, q1 = spec(0, 3, grp), spec(1, 3, grp)
    kp, kc = spec(-1, 3, grp), spec(0, 3, grp)
    a0, a1 = spec(0, 1, 0), spec(1, 1, 0)
    out = pl.BlockSpec((BLK, ATT_W), lambda r, i: (i, r))
    dq, dk, dv = pl.pallas_call(
        body, name=name, grid=(dil, nb),
        in_specs=[q0, q1, kp, kc, kp, kc, a0, a1, a0, a1, a0, a1],
        out_specs=[out] * 3, out_shape=[S((seq, dil * ATT_W), F32)] * 3,
        compiler_params=pltpu.CompilerParams(dimension_semantics=("parallel", "arbitrary")),
    )(view3(qn), view3(qn), view3(kn), view3(kn), view3(vb), view3(vb),
      view1(datt), view1(datt), view1(att), view1(att), view1(lse), view1(lse))
    return dq.reshape(T, ATT_W), dk.reshape(T, ATT_W), dv.reshape(T, ATT_W)


def _att_merge(os_, lses, proj):
    def body(o0, o1, o2, l0, l1, l2, za_ref, att_ref, lse_ref, ain_ref):
        la, lb, lc = l0[...], l1[...], l2[...]
        m = jnp.maximum(jnp.maximum(la, lb), lc)
        wa, wb, wc = jnp.exp(la - m), jnp.exp(lb - m), jnp.exp(lc - m)
        tot = wa + wb + wc
        att = (wa * o0[...] + wb * o1[...] + wc * o2[...]) / tot
        att_ref[...] = att
        lse_ref[...] = m + jnp.log(tot)
        za = za_ref[...]
        ain_ref[...] = (att * za * _sigmoid(za)).astype(BF16)

    return _rowcall(body, "att_merge", 512,
                    [_rows(a) for a in (*os_, *lses)] + [_rows(proj, ATT_W, C_ZA // ATT_W)],
                    [("rows", ATT_W, F32), ("rows", ATT_W, F32), ("rows", ATT_W, BF16)])


def _att_gate_bwd(dain, att, proj):
    def body(d_ref, att_ref, za_ref, datt_ref, dza_ref):
        za = za_ref[...]
        sg = _sigmoid(za)
        d = d_ref[...]
        datt_ref[...] = d * za * sg
        dza_ref[...] = (d * att_ref[...] * sg * (1.0 + za * (1.0 - sg))).astype(BF16)

    return _rowcall(body, "att_gate_bwd", 512, [_rows(dain), _rows(att), _rows(proj, ATT_W, C_ZA // ATT_W)],
                    [("rows", ATT_W, F32), ("rows", ATT_W, BF16)])


def _split3(v):
    hi = v.astype(BF16)
    r1 = v - hi.astype(F32)
    mid = r1.astype(BF16)
    lo = (r1 - mid.astype(F32)).astype(BF16)
    return hi, mid, lo


def _tri_sum(v, upper):
    n = v.shape[0]
    row = lax.broadcasted_iota(jnp.int32, (n, n), 0)
    col = lax.broadcasted_iota(jnp.int32, (n, n), 1)
    tri = jnp.where(col >= row if upper else col <= row, 1.0, 0.0).astype(BF16)
    hi, mid, lo = _split3(v)
    return _nn(tri, hi) + _nn(tri, mid) + _nn(tri, lo)


def _gla_gates(glr_ref, w2_ref, b_ref):
    logit = _nn(glr_ref[...].astype(BF16), w2_ref[...]) + b_ref[...]
    lg = (jnp.minimum(logit, 0.0) - jnp.log(1.0 + jnp.exp(-jnp.abs(logit)))) * (1.0 / GLA_TAU)
    return logit, _tri_sum(lg, upper=False)


def _gla_head(cum, q_ref, k_ref, h):
    cols = slice(h * GDK, (h + 1) * GDK)
    b = cum[:, cols]
    last = b[GLA_C - 1:GLA_C, :]
    e_pos = jnp.exp(b)
    e_neg = jnp.exp(-b)
    e_end = jnp.exp(last - b)
    qt = q_ref[:, cols] * (GDK ** -0.5) * e_pos
    kt = k_ref[:, cols] * e_neg
    kh = k_ref[:, cols] * e_end
    return b, last, e_pos, e_neg, e_end, qt, kt, kh


def _causal(n):
    return lax.broadcasted_iota(jnp.int32, (n, n), 1) <= lax.broadcasted_iota(jnp.int32, (n, n), 0)


def _gla_fwd(proj, w2p, bg, gn):
    nc = T // GLA_C

    def body(q_ref, k_ref, v_ref, glr_ref, zg_ref, w2_ref, b_ref, gn_ref, o_ref, bin_ref, st_ref, state):
        @pl.when(pl.program_id(0) == 0)
        def _():
            state[...] = jnp.zeros_like(state)

        _, cum = _gla_gates(glr_ref, w2_ref, b_ref)
        for h in range(GH):
            _, last, _, _, _, qt, kt, kh = _gla_head(cum, q_ref, k_ref, h)
            vcols = slice(h * GDV, (h + 1) * GDV)
            st = state[h]
            st_ref[0, h] = st
            v = v_ref[:, vcols].astype(BF16)
            qb = qt.astype(BF16)
            a = jnp.where(_causal(GLA_C), _nt(qb, kt.astype(BF16)), 0.0)
            o = _nt(qb, st.astype(BF16)) + _nn(a.astype(BF16), v)
            state[h] = st * jnp.exp(last) + _tn(v, kh.astype(BF16))
            o_ref[:, vcols] = o
            r = lax.rsqrt(jnp.mean(o * o, axis=-1, keepdims=True) + EPS)
            zg = zg_ref[:, vcols]
            bin_ref[:, vcols] = (o * r * gn_ref[...] * zg * _sigmoid(zg)).astype(BF16)

    row = lambda width, cblk: pl.BlockSpec((GLA_C, width), functools.partial(lambda i, c: (i, c), c=cblk))
    full = lambda a: pl.BlockSpec(a.shape, functools.partial(lambda i, nd: (0,) * nd, nd=a.ndim))
    return pl.pallas_call(
        body, name="gla_fwd", grid=(nc,),
        in_specs=[row(512, C_QG // 512), row(512, C_KG // 512), row(1024, C_VG // 1024), row(128, C_GLR // 128),
                  row(1024, C_ZG // 1024), full(w2p), full(bg), full(gn)],
        out_specs=[pl.BlockSpec((GLA_C, GH * GDV), lambda i: (i, 0)), pl.BlockSpec((GLA_C, GH * GDV), lambda i: (i, 0)),
                   pl.BlockSpec((1, GH, GDV, GDK), lambda i: (i, 0, 0, 0))],
        out_shape=[S((T, GH * GDV), F32), S((T, GH * GDV), BF16), S((nc, GH, GDV, GDK), F32)],
        scratch_shapes=[pltpu.VMEM((GH, GDV, GDK), F32)],
        compiler_params=pltpu.CompilerParams(dimension_semantics=("arbitrary",)),
    )(proj, proj, proj, proj, proj, w2p, bg, gn)


def _gla_bwd(proj, w2p, bg, gn, o_gla, states, dbin):
    nc = T // GLA_C

    def body(q_ref, k_ref, v_ref, glr_ref, zg_ref, w2_ref, b_ref, gn_ref, o_ref, st_ref, dbin_ref,
             dq_ref, dk_ref, dv_ref, dglr_ref, dzg_ref, dw2_ref, dbg_ref, dgn_ref, dstate, dlogit):
        first = pl.program_id(0) == 0

        @pl.when(first)
        def _():
            dstate[...] = jnp.zeros_like(dstate)

        logit, cum = _gla_gates(glr_ref, w2_ref, b_ref)
        is_last = lax.broadcasted_iota(jnp.int32, (GLA_C, 1), 0) == GLA_C - 1
        dgn = jnp.zeros((1, GDV), F32)
        for h in range(GH):
            _, last, e_pos, e_neg, e_end, qt, kt, kh = _gla_head(cum, q_ref, k_ref, h)
            cols = slice(h * GDK, (h + 1) * GDK)
            vcols = slice(h * GDV, (h + 1) * GDV)
            o = o_ref[:, vcols]
            r = lax.rsqrt(jnp.mean(o * o, axis=-1, keepdims=True) + EPS)
            zg = zg_ref[:, vcols]
            sg = _sigmoid(zg)
            db_ = dbin_ref[:, vcols]
            dlin = db_ * zg * sg
            dzg_ref[:, vcols] = (db_ * (o * r * gn_ref[...]) * sg * (1.0 + zg * (1.0 - sg))).astype(BF16)
            u = dlin * gn_ref[...]
            do = (r * u - o * (r * r * r) * jnp.mean(u * o, axis=-1, keepdims=True)).astype(BF16)
            dgn = dgn + jnp.sum(dlin * o * r, axis=0, keepdims=True)
            st = st_ref[0, h]
            dst = dstate[h]
            v = v_ref[:, vcols].astype(BF16)
            qb, kb, khb = qt.astype(BF16), kt.astype(BF16), kh.astype(BF16)
            dstb = dst.astype(BF16)
            causal = _causal(GLA_C)
            a = jnp.where(causal, _nt(qb, kb), 0.0).astype(BF16)
            da = jnp.where(causal, _nt(do, v), 0.0).astype(BF16)
            dqt = _nn(do, st.astype(BF16)) + _nn(da, kb)
            dkt = _tn(da, qb)
            dkh = _nn(v, dstb)
            dv_ref[:, vcols] = (_tn(a, do) + _nt(khb, dstb)).astype(BF16)
            lam = jnp.exp(last)
            dlam = jnp.sum(dst * st, axis=0, keepdims=True)
            dstate[h] = dst * lam + _tn(do, qb)
            dq_ref[:, cols] = (dqt * e_pos * (GDK ** -0.5)).astype(BF16)
            dk_ref[:, cols] = (dkt * e_neg + dkh * e_end).astype(BF16)
            dkh_kh = dkh * kh
            dcum = dqt * qt - dkt * kt - dkh_kh
            dlast = jnp.sum(dkh_kh, axis=0, keepdims=True) + dlam * lam
            dcum = jnp.where(is_last, dcum + dlast, dcum)
            dlg = _tri_sum(dcum, upper=True)
            dlogit[:, cols] = dlg * (1.0 / GLA_TAU) * (1.0 - _sigmoid(logit[:, cols]))

        dl = dlogit[...]
        dlb = dl.astype(BF16)
        dglr_ref[...] = _nt(dlb, w2_ref[...]).astype(BF16)
        dw2 = _tn(glr_ref[...].astype(BF16), dlb)
        dbg = jnp.sum(dl, axis=0, keepdims=True)

        @pl.when(first)
        def _():
            dw2_ref[...] = dw2
            dbg_ref[...] = dbg
            dgn_ref[...] = dgn

        @pl.when(jnp.logical_not(first))
        def _():
            dw2_ref[...] += dw2
            dbg_ref[...] += dbg
            dgn_ref[...] += dgn

    rev = lambda i: nc - 1 - i
    row = lambda width, cblk: pl.BlockSpec((GLA_C, width), functools.partial(lambda i, c: (rev(i), c), c=cblk))
    full = lambda a: pl.BlockSpec(a.shape, functools.partial(lambda i, nd: (0,) * nd, nd=a.ndim))
    keep = lambda shape: pl.BlockSpec(shape, functools.partial(lambda i, nd: (0,) * nd, nd=len(shape)))
    return pl.pallas_call(
        body, name="gla_bwd", grid=(nc,),
        in_specs=[row(512, C_QG // 512), row(512, C_KG // 512), row(1024, C_VG // 1024), row(128, C_GLR // 128),
                  row(1024, C_ZG // 1024), full(w2p), full(bg), full(gn), row(GH * GDV, 0),
                  pl.BlockSpec((1, GH, GDV, GDK), lambda i: (rev(i), 0, 0, 0)), row(GH * GDV, 0)],
        out_specs=[row(512, 0), row(512, 0), row(1024, 0), row(128, 0), row(1024, 0),
                   keep((128, 512)), keep((1, 512)), keep((1, GDV))],
        out_shape=[S((T, 512), BF16), S((T, 512), BF16), S((T, 1024), BF16), S((T, 128), BF16), S((T, 1024), BF16),
                   S((128, 512), F32), S((1, 512), F32), S((1, GDV), F32)],
        scratch_shapes=[pltpu.VMEM((GH, GDV, GDK), F32), pltpu.VMEM((GLA_C, GH * GDK), F32)],
        compiler_params=pltpu.CompilerParams(dimension_semantics=("arbitrary",)),
    )(proj, proj, proj, proj, proj, w2p, bg, gn, o_gla, states, dbin)


def _merge_fwd(ya, yb, proj):
    def body(ya_ref, yb_ref, ga_ref, gb_ref, y_ref):
        y_ref[...] = (_sigmoid(ga_ref[...]) * ya_ref[...] + _sigmoid(gb_ref[...]) * yb_ref[...]).astype(BF16)

    return _rowcall(body, "merge_fwd", 512,
                    [_rows(ya), _rows(yb), _rows(proj, D, C_GA // D), _rows(proj, D, C_GB // D)],
                    [("rows", D, BF16)])[0]


def _merge_bwd(dy, ya, yb, proj):
    def body(dy_ref, ya_ref, yb_ref, ga_ref, gb_ref, dga_ref, dgb_ref, dya_ref, dyb_ref):
        dy_ = dy_ref[...]
        sa, sb = _sigmoid(ga_ref[...]), _sigmoid(gb_ref[...])
        dga_ref[...] = (dy_ * ya_ref[...] * sa * (1.0 - sa)).astype(BF16)
        dgb_ref[...] = (dy_ * yb_ref[...] * sb * (1.0 - sb)).astype(BF16)
        dya_ref[...] = (dy_ * sa).astype(BF16)
        dyb_ref[...] = (dy_ * sb).astype(BF16)

    return _rowcall(body, "merge_bwd", 512,
                    [_rows(dy), _rows(ya), _rows(yb), _rows(proj, D, C_GA // D), _rows(proj, D, C_GB // D)],
                    [("rows", D, BF16)] * 4)


def _loss_head(x1, e, u, target):
    def body(x1_ref, e_ref, u_ref, t_ref, loss_ref, dout_ref, de_ref, du_ref, acc):
        first = pl.program_id(0) == 0
        pg = _sigmoid(u_ref[...])
        e_ = e_ref[...]
        diff = x1_ref[...] + e_ * pg - t_ref[...]
        part = jnp.sum(diff * diff, axis=0, keepdims=True)

        @pl.when(first)
        def _():
            acc[...] = part

        @pl.when(jnp.logical_not(first))
        def _():
            acc[...] += part

        dout = diff * (1.0 / D)
        dout_ref[...] = dout
        de_ref[...] = (dout * pg).astype(BF16)
        du_ref[...] = (dout * e_ * pg * (1.0 - pg)).astype(BF16)
        loss_ref[...] = jnp.zeros((1, 128), F32) + jnp.sum(acc[...], axis=-1, keepdims=True) * (0.5 / D)

    return _rowcall(body, "loss_head", 256, [_rows(x1), _rows(e), _rows(u), _rows(target)],
                    [("acc", (1, 128), F32), ("rows", D, F32), ("rows", D, BF16), ("rows", D, BF16)],
                    scratch=[pltpu.VMEM((1, D), F32)])


def _peer(k):
    x, y, c = lax.axis_index("x"), lax.axis_index("y"), lax.axis_index("c")
    return (x ^ ((k >> 2) & 1), y ^ ((k >> 1) & 1), c ^ (k & 1))


def _my_index():
    return 4 * lax.axis_index("x") + 2 * lax.axis_index("y") + lax.axis_index("c")


def _peer_index(k):
    px, py, pc = _peer(k)
    return 4 * px + 2 * py + pc


def _all_gather(arrs, name):
    n = len(arrs)

    def body(*refs):
        ins, outs = refs[:n], refs[n:2 * n]
        send, recv, local = refs[2 * n:]
        me = _my_index()
        copies = []
        for a in range(n):
            own = pltpu.make_async_copy(ins[a], outs[a].at[me], local.at[a])
            own.start()
            copies.append(own)
        rdmas = []
        for k in range(1, NDEV):
            for a in range(n):
                cp = pltpu.make_async_remote_copy(
                    src_ref=ins[a], dst_ref=outs[a].at[me], send_sem=send.at[k - 1, a], recv_sem=recv.at[k - 1, a],
                    device_id=_peer(k), device_id_type=MESH)
                cp.start()
                rdmas.append(cp)
        for k in range(1, NDEV):
            for a in range(n):
                pltpu.make_async_remote_copy(
                    src_ref=ins[a], dst_ref=outs[a].at[_peer_index(k)], send_sem=send.at[k - 1, a],
                    recv_sem=recv.at[k - 1, a], device_id=_peer(k), device_id_type=MESH).wait_recv()
        for cp in rdmas:
            cp.wait_send()
        for cp in copies:
            cp.wait()

    hbm = pl.BlockSpec(memory_space=pl.ANY)
    return pl.pallas_call(
        body, name=name, in_specs=[hbm] * n, out_specs=[hbm] * n,
        out_shape=[S((NDEV,) + a.shape, a.dtype) for a in arrs],
        scratch_shapes=[pltpu.SemaphoreType.DMA((NDEV - 1, n)), pltpu.SemaphoreType.DMA((NDEV - 1, n)),
                        pltpu.SemaphoreType.DMA((n,))],
    )(*arrs)


def _exchange(arrs, name):
    n = len(arrs)

    def body(*refs):
        ins, outs = refs[:n], refs[n:2 * n]
        send, recv, local = refs[2 * n:]
        me = _my_index()
        copies = []
        for a in range(n):
            own = pltpu.make_async_copy(ins[a].at[me], outs[a].at[me], local.at[a])
            own.start()
            copies.append(own)
        rdmas = []
        for k in range(1, NDEV):
            for a in range(n):
                cp = pltpu.make_async_remote_copy(
                    src_ref=ins[a].at[_peer_index(k)], dst_ref=outs[a].at[me], send_sem=send.at[k - 1, a],
                    recv_sem=recv.at[k - 1, a], device_id=_peer(k), device_id_type=MESH)
                cp.start()
                rdmas.append(cp)
        for k in range(1, NDEV):
            for a in range(n):
                pltpu.make_async_remote_copy(
                    src_ref=ins[a].at[me], dst_ref=outs[a].at[_peer_index(k)], send_sem=send.at[k - 1, a],
                    recv_sem=recv.at[k - 1, a], device_id=_peer(k), device_id_type=MESH).wait_recv()
        for cp in rdmas:
            cp.wait_send()
        for cp in copies:
            cp.wait()

    hbm = pl.BlockSpec(memory_space=pl.ANY)
    return pl.pallas_call(
        body, name=name, in_specs=[hbm] * n, out_specs=[hbm] * n,
        out_shape=[S(a.shape, a.dtype) for a in arrs],
        scratch_shapes=[pltpu.SemaphoreType.DMA((NDEV - 1, n)), pltpu.SemaphoreType.DMA((NDEV - 1, n)),
                        pltpu.SemaphoreType.DMA((n,))],
    )(*arrs)


def _adamw(parts, w, m, v, name, tr):
    rows, cols = w.shape
    assert rows % tr == 0
    c1 = 1.0 - ADAM_B1 ** ADAM_STEP
    c2 = 1.0 - ADAM_B2 ** ADAM_STEP

    def body(p_ref, w_ref, m_ref, v_ref, g_ref, d_ref, mo_ref, vo_ref):
        g = p_ref[0].astype(F32)
        for s in range(1, NDEV):
            g = g + p_ref[s].astype(F32)
        m_new = ADAM_B1 * m_ref[...] + (1.0 - ADAM_B1) * g
        v_new = ADAM_B2 * v_ref[...] + (1.0 - ADAM_B2) * (g * g)
        g_ref[...] = g
        mo_ref[...] = m_new
        vo_ref[...] = v_new
        d_ref[...] = -ADAM_LR * ((m_new / c1) / (jnp.sqrt(v_new / c2) + ADAM_EPS) + ADAM_WD * w_ref[...])

    blk = pl.BlockSpec((tr, cols), lambda i: (i, 0))
    return pl.pallas_call(
        body, name=name, grid=(rows // tr,),
        in_specs=[pl.BlockSpec((NDEV, tr, cols), lambda i: (0, i, 0)), blk, blk, blk],
        out_specs=[blk] * 4, out_shape=[S((rows, cols), F32)] * 4,
        compiler_params=pltpu.CompilerParams(dimension_semantics=("parallel",)),
    )(parts, w, m, v)


def _to_aligned(w):
    pad = jnp.zeros((w.shape[0], NCOL - W_IN_COLS), w.dtype)
    return jnp.concatenate([w[:, :GLR_ORIG], w[:, GLR_ORIG + GLR_N:], w[:, GLR_ORIG:GLR_ORIG + GLR_N], pad], axis=1)


def _from_aligned(w):
    return jnp.concatenate([w[:, :C_ZG], w[:, C_GLR:C_GLR + GLR_N], w[:, C_ZG:C_GLR]], axis=1)


def _col_blocks(w, width):
    return w.reshape(w.shape[0], NDEV, width).transpose(1, 0, 2)


def _from_col_blocks(w):
    return w.transpose(1, 0, 2).reshape(w.shape[1], NDEV * w.shape[2])


SMALL = (("norm_g", D), ("qk_norm_q", HD), ("qk_norm_k", HD), ("gla_gate_b", 512), ("gla_norm_g", GDV),
         ("ple_norm_g", D))
SMALL_PAD = 4096


def _local_step(x2, p2, pos, tgt, norm_g, qk_norm_q, qk_norm_k, gla_gate_b, gla_norm_g, ple_norm_g,
                w_al, w2p, w_att_f, w_gla_f, w_out_f, w_pg_f, w_ple_f):
    half = ROT_DIM // 2
    inv8 = jnp.power(jnp.float32(ROPE_THETA), -jnp.arange(half, dtype=F32) * 2.0 / ROT_DIM)
    inv = jnp.tile(jnp.concatenate([inv8, inv8, jnp.zeros((HD - ROT_DIM,), F32)]), 2).reshape(1, 128)
    gq = jnp.tile(qk_norm_q, (1, 2))
    gk = jnp.tile(qk_norm_k, (1, 2))

    h = _rms_fwd(x2, norm_g, "rms1_fwd")
    proj = _mm(h, w_al, mode="nn", name="proj", tm=1024, tn=1152, tk=D)
    qn, kn, vb = _qk_prep(proj, pos, inv, gq, gk)
    fwd = [_att_fwd(qn, kn, vb, g, f"att_fwd{g}") for g in range(3)]
    att, lse, ain = _att_merge([f[0] for f in fwd], [f[1] for f in fwd], proj)
    o_gla, bin_, states = _gla_fwd(proj, w2p, gla_gate_b, gla_norm_g)
    ya = _mm(ain, w_att_f, mode="nn", name="ya", tm=1024, tn=D, tk=512)
    yb = _mm(bin_, w_gla_f, mode="nn", name="yb", tm=1024, tn=D, tk=D)
    y = _merge_fwd(ya, yb, proj)
    x1 = _mm(y, w_out_f, mode="nn", name="x1", tm=1024, tn=D, tk=D, res=x2)
    n2 = _rms_fwd(x1, ple_norm_g, "rms2_fwd")
    u = _mm(n2, w_pg_f, mode="nn", name="ple_u", tm=1024, tn=D, tk=D)
    e = _mm(p2, w_ple_f, mode="nn", name="ple_e", tm=1024, tn=D, tk=PLE)
    loss_v, dout, de, du = _loss_head(x1, e, u, tgt)

    dw_ple = _mm(p2, de, mode="tn", name="dw_ple", tm=PLE, tn=D, tk=512)
    dw_pg = _mm(n2, du, mode="tn", name="dw_pg", tm=D, tn=D, tk=512)
    dn2 = _mm(du, w_pg_f, mode="nt", name="dn2", tm=1024, tn=D, tk=D)
    dx1, dx1b, dg_ple = _rms_bwd(dn2, x1, ple_norm_g, dout, "rms2_bwd")
    dw_out = _mm(y, dx1b, mode="tn", name="dw_out", tm=D, tn=D, tk=512)
    dy = _mm(dx1b, w_out_f, mode="nt", name="dy", tm=1024, tn=D, tk=D)
    dga, dgb, dya, dyb = _merge_bwd(dy, ya, yb, proj)
    dw_att = _mm(ain, dya, mode="tn", name="dw_att", tm=512, tn=D, tk=512)
    dain = _mm(dya, w_att_f, mode="nt", name="dain", tm=1024, tn=512, tk=D)
    dw_gla = _mm(bin_, dyb, mode="tn", name="dw_gla", tm=D, tn=D, tk=512)
    dbin = _mm(dyb, w_gla_f, mode="nt", name="dbin", tm=1024, tn=D, tk=D)
    datt, dza = _att_gate_bwd(dain, att, proj)
    bwd = [_att_bwd(qn, kn, vb, datt, att, lse, g, f"att_bwd{g}") for g in range(3)]
    dqa, dka, dva, dgq, dgk = _qk_bwd(proj, pos, inv, gq, gk, [b[0] for b in bwd], [b[1] for b in bwd],
                                      [b[2] for b in bwd])
    dqg, dkg, dvg, dglr, dzg, dw2, dbg, dgn = _gla_bwd(proj, w2p, gla_gate_b, gla_norm_g, o_gla, states, dbin)
    dproj = jnp.concatenate([dqa, dka, dva, dza, dqg, dkg, dvg, dzg, dga, dgb, dglr], axis=1)
    dw_al = _mm(h, dproj, mode="tn", name="dw_in", tm=D, tn=1152, tk=512)
    dh = _mm(dproj, w_al, mode="nt", name="dh", tm=1024, tn=D, tk=1152)
    grad_x, _, dg_norm = _rms_bwd(dh, x2, norm_g, dx1, "rms1_bwd")

    return dict(loss=loss_v, grad_x=grad_x, dw_al=dw_al, dw2=dw2, dw_att=dw_att, dw_gla=dw_gla, dw_out=dw_out,
                dw_pg=dw_pg, dw_ple=dw_ple, dg_norm=dg_norm, dgq=dgq, dgk=dgk, dbg=dbg, dgn=dgn, dg_ple=dg_ple)


def kernel(x, p, positions, norm_g, w_in, qk_norm_q, qk_norm_k, gla_gate_w2, gla_gate_b, gla_norm_g, w_att_proj, w_gla_proj, w_out, ple_norm_g, w_ple_gate, w_ple, loss_target, m_norm_g, m_w_in, m_qk_norm_q, m_qk_norm_k, m_gla_gate_w2, m_gla_gate_b, m_gla_norm_g, m_w_att_proj, m_w_gla_proj, m_w_out, m_ple_norm_g, m_w_ple_gate, m_w_ple, v_norm_g, v_w_in, v_qk_norm_q, v_qk_norm_k, v_gla_gate_w2, v_gla_gate_b, v_gla_norm_g, v_w_att_proj, v_w_gla_proj, v_w_out, v_ple_norm_g, v_w_ple_gate, v_w_ple):
    x2, p2, tgt = x[0], p[0, 0], loss_target[0]
    pos = positions.astype(F32).reshape(T, 1)

    rows3 = jnp.stack([w_gla_proj[0], w_out[0], w_ple_gate[0]]).astype(BF16)
    cols3 = jnp.concatenate([w_att_proj[0], w_ple[0], jnp.pad(gla_gate_w2[0], ((0, 0), (0, 64)))], axis=0).astype(BF16)
    g_in, g_rows, g_cols = _all_gather([w_in[0].astype(BF16), rows3, cols3], "gather_weights")
    w_al = _to_aligned(_from_col_blocks(g_in))
    w_gla_f = g_rows[:, 0].reshape(D, D)
    w_out_f = g_rows[:, 1].reshape(D, D)
    w_pg_f = g_rows[:, 2].reshape(D, D)
    w_att_f = _from_col_blocks(g_cols[:, :512])
    w_ple_f = _from_col_blocks(g_cols[:, 512:768])
    w2_f = _from_col_blocks(g_cols[:, 768:784, :64])
    w2p = jnp.pad(w2_f, ((0, 128 - GLR_N), (0, 0)))

    loc = _local_step(x2, p2, pos, tgt, norm_g, qk_norm_q, qk_norm_k, gla_gate_b, gla_norm_g, ple_norm_g,
                      w_al, w2p, w_att_f, w_gla_f, w_out_f, w_pg_f, w_ple_f)
    loss_v, grad_x, dw_al, dw2, dw_att, dw_gla, dw_out, dw_pg, dw_ple = (
        loc[k] for k in ("loss", "grad_x", "dw_al", "dw2", "dw_att", "dw_gla", "dw_out", "dw_pg", "dw_ple"))
    dg_norm, dgq, dgk, dbg, dgn, dg_ple = (loc[k] for k in ("dg_norm", "dgq", "dgk", "dbg", "dgn", "dg_ple"))

    s_in = _col_blocks(_from_aligned(dw_al), W_IN_SHARD).astype(BF16)
    s_rows = jnp.stack([dw_gla.reshape(NDEV, 128, D), dw_out.reshape(NDEV, 128, D), dw_pg.reshape(NDEV, 128, D)],
                       axis=1).astype(BF16)
    s_cols = jnp.concatenate([_col_blocks(dw_att, 128), _col_blocks(dw_ple, 128),
                              jnp.pad(_col_blocks(dw2[:GLR_N], 64), ((0, 0), (0, 0), (0, 64)))], axis=1).astype(BF16)
    small = jnp.concatenate([dg_norm[0], dgq[0, :HD], dgk[0, :HD], dbg[0], dgn[0], dg_ple[0]])
    small = jnp.pad(small, (0, SMALL_PAD - small.shape[0])).reshape(1, 8, SMALL_PAD // 8)
    r_in, r_rows, r_cols = _exchange([s_in, s_rows, s_cols], "exchange_grads")
    (r_small,) = _all_gather([small], "gather_small")

    outs = {}

    def adam(nm, parts, w, m, v, tr):
        outs[nm] = _adamw(parts, w, m, v, "adam_" + nm, tr)

    adam("w_in", r_in, w_in[0], m_w_in[0], v_w_in[0], 128)
    adam("w_gla_proj", r_rows[:, 0], w_gla_proj[0], m_w_gla_proj[0], v_w_gla_proj[0], 128)
    adam("w_out", r_rows[:, 1], w_out[0], m_w_out[0], v_w_out[0], 128)
    adam("w_ple_gate", r_rows[:, 2], w_ple_gate[0], m_w_ple_gate[0], v_w_ple_gate[0], 128)
    adam("w_att_proj", r_cols[:, :512], w_att_proj[0], m_w_att_proj[0], v_w_att_proj[0], 512)
    adam("w_ple", r_cols[:, 512:768], w_ple[0], m_w_ple[0], v_w_ple[0], 256)
    adam("gla_gate_w2", r_cols[:, 768:784, :64], gla_gate_w2[0], m_gla_gate_w2[0], v_gla_gate_w2[0], 16)
    given = dict(norm_g=(norm_g, m_norm_g, v_norm_g), qk_norm_q=(qk_norm_q, m_qk_norm_q, v_qk_norm_q),
                 qk_norm_k=(qk_norm_k, m_qk_norm_k, v_qk_norm_k), gla_gate_b=(gla_gate_b, m_gla_gate_b, v_gla_gate_b),
                 gla_norm_g=(gla_norm_g, m_gla_norm_g, v_gla_norm_g), ple_norm_g=(ple_norm_g, m_ple_norm_g, v_ple_norm_g))

    def pack(i):
        flat = jnp.concatenate([given[nm][i][0] for nm, _ in SMALL])
        return jnp.pad(flat, (0, SMALL_PAD - flat.shape[0])).reshape(8, SMALL_PAD // 8)

    sm = _adamw(r_small.reshape(NDEV, 8, SMALL_PAD // 8), pack(0), pack(1), pack(2), "adam_small", 8)
    off = 0
    for nm, width in SMALL:
        outs[nm] = [o.reshape(-1)[off:off + width] for o in sm]
        off += width

    loss = lax.psum(loss_v[0, 0], ("x", "y", "c"))
    order = ["norm_g", "w_in", "qk_norm_q", "qk_norm_k", "gla_gate_w2", "gla_gate_b", "gla_norm_g", "w_att_proj",
             "w_gla_proj", "w_out", "ple_norm_g", "w_ple_gate", "w_ple"]
    result = [loss, grad_x[None]]
    for i in range(4):
        result += [outs[nm][i][None] for nm in order]
    return tuple(result)
```

```python
import functools

import jax
import jax.numpy as jnp
from jax import lax
from jax.experimental import pallas as pl
from jax.experimental.pallas import tpu as pltpu

F32 = jnp.float32
BF16 = jnp.bfloat16
S = jax.ShapeDtypeStruct

T = 4096
D = 1024
NDEV = 8
HD = 64
ATT_W = 512
ATT_QKV = 1536
DILATIONS = (1, 4, 16)
BLK = 128
GH, GDK, GDV = 4, 128, 256
GLA_C = 128
PLE = 256
EPS = 1e-6
ROT_DIM = 16
ROPE_THETA = 500000.0
GLA_TAU = 16.0
W_IN_COLS = 10256
W_IN_SHARD = 1282

C_QA, C_KA, C_VA, C_ZA, C_QG, C_KG, C_VG, C_ZG, C_GA, C_GB, C_GLR = (
    0, 1536, 3072, 4608, 5120, 5632, 6144, 7168, 8192, 9216, 10240)
NCOL = 10368
GLR_ORIG = 7168
GLR_N = 16

ADAM_LR, ADAM_B1, ADAM_B2, ADAM_EPS, ADAM_WD, ADAM_STEP = 0.001, 0.9, 0.999, 1e-08, 0.01, 10

MESH = pl.DeviceIdType.MESH


def _sigmoid(z):
    return 1.0 / (1.0 + jnp.exp(-z))


def _dot(a, b, dims):
    return lax.dot_general(a, b, (dims, ((), ())), preferred_element_type=F32)


def _nn(a, b):
    return _dot(a, b, ((1,), (0,)))


def _nt(a, b):
    return _dot(a, b, ((1,), (1,)))


def _tn(a, b):
    return _dot(a, b, ((0,), (0,)))


def _mm(a, b, *, mode, name, tm, tn, tk, out_dtype=F32, res=None):
    if mode == "nn":
        (m, k), n = a.shape, b.shape[1]
        a_spec = pl.BlockSpec((tm, tk), lambda i, j, l: (i, l))
        b_spec = pl.BlockSpec((tk, tn), lambda i, j, l: (l, j))
        dot = _nn
    elif mode == "nt":
        (m, k), n = a.shape, b.shape[0]
        a_spec = pl.BlockSpec((tm, tk), lambda i, j, l: (i, l))
        b_spec = pl.BlockSpec((tn, tk), lambda i, j, l: (j, l))
        dot = _nt
    else:
        (k, m), n = a.shape, b.shape[1]
        a_spec = pl.BlockSpec((tk, tm), lambda i, j, l: (l, i))
        b_spec = pl.BlockSpec((tk, tn), lambda i, j, l: (l, j))
        dot = _tn
    assert m % tm == 0 and n % tn == 0 and k % tk == 0, (name, m, n, k)
    nk = k // tk
    o_spec = pl.BlockSpec((tm, tn), lambda i, j, l: (i, j))
    in_specs = [a_spec, b_spec]
    args = [a, b]
    if res is not None:
        in_specs.append(o_spec)
        args.append(res)

    def body(*refs):
        a_ref, b_ref = refs[0], refs[1]
        r_ref = refs[2] if res is not None else None
        o_ref = refs[3] if res is not None else refs[2]
        part = dot(a_ref[...].astype(BF16), b_ref[...].astype(BF16))

        def finish(val):
            if r_ref is not None:
                val = val + r_ref[...]
            o_ref[...] = val.astype(out_dtype)

        if nk == 1:
            finish(part)
        else:
            acc = refs[-1]
            l = pl.program_id(2)

            @pl.when(l == 0)
            def _():
                acc[...] = part

            @pl.when(l > 0)
            def _():
                acc[...] += part

            @pl.when(l == nk - 1)
            def _():
                finish(acc[...])

    return pl.pallas_call(
        body, name=name, grid=(m // tm, n // tn, nk),
        in_specs=in_specs, out_specs=o_spec, out_shape=S((m, n), out_dtype),
        scratch_shapes=[pltpu.VMEM((tm, tn), F32)] if nk > 1 else [],
        compiler_params=pltpu.CompilerParams(dimension_semantics=("parallel", "parallel", "arbitrary")),
    )(*args)


def _rows(arr, width=None, cblk=0):
    return ("rows", arr, arr.shape[1] if width is None else width, cblk)


def _whole(arr):
    return ("whole", arr)


def _rowcall(body, name, tt, ins, outs, scratch=()):
    in_specs, args = [], []
    for spec in ins:
        if spec[0] == "rows":
            _, arr, width, cblk = spec
            in_specs.append(pl.BlockSpec((tt, width), functools.partial(lambda i, c: (i, c), c=cblk)))
        else:
            arr = spec[1]
            in_specs.append(pl.BlockSpec(arr.shape, functools.partial(lambda i, nd: (0,) * nd, nd=arr.ndim)))
        args.append(arr)
    out_specs, out_shape = [], []
    for kind, shape, dtype in outs:
        if kind == "rows":
            out_specs.append(pl.BlockSpec((tt, shape), lambda i: (i, 0)))
            out_shape.append(S((T, shape), dtype))
        else:
            out_specs.append(pl.BlockSpec(shape, functools.partial(lambda i, nd: (0,) * nd, nd=len(shape))))
            out_shape.append(S(shape, dtype))
    return pl.pallas_call(
        body, name=name, grid=(T // tt,), in_specs=in_specs, out_specs=out_specs, out_shape=out_shape,
        scratch_shapes=list(scratch),
        compiler_params=pltpu.CompilerParams(dimension_semantics=("arbitrary",)),
    )(*args)


def _rms_fwd(x, g, name):
    def body(x_ref, g_ref, h_ref):
        xf = x_ref[...]
        r = lax.rsqrt(jnp.mean(xf * xf, axis=-1, keepdims=True) + EPS)
        h_ref[...] = (xf * r * g_ref[...]).astype(BF16)

    return _rowcall(body, name, 512, [_rows(x), _whole(g)], [("rows", D, BF16)])[0]


def _rms_bwd(dn, x, g, skip, name):
    def body(dn_ref, x_ref, g_ref, s_ref, dx_ref, dxb_ref, dg_ref):
        xf = x_ref[...]
        r = lax.rsqrt(jnp.mean(xf * xf, axis=-1, keepdims=True) + EPS)
        dn_ = dn_ref[...]
        u = dn_ * g_ref[...]
        dx = s_ref[...] + r * u - xf * (r * r * r) * jnp.mean(u * xf, axis=-1, keepdims=True)
        dx_ref[...] = dx
        dxb_ref[...] = dx.astype(BF16)
        part = jnp.sum(dn_ * xf * r, axis=0, keepdims=True)

        @pl.when(pl.program_id(0) == 0)
        def _():
            dg_ref[...] = part

        @pl.when(pl.program_id(0) > 0)
        def _():
            dg_ref[...] += part

    return _rowcall(body, name, 256, [_rows(dn), _rows(x), _whole(g), _rows(skip)],
                    [("rows", D, F32), ("rows", D, BF16), ("acc", (1, D), F32)])


def _rot_tables(pos_ref, inv_ref):
    lane = lax.broadcasted_iota(jnp.int32, (1, 128), 1) % HD
    ang = pos_ref[...] * inv_ref[...]
    cos, sin = jnp.cos(ang), jnp.sin(ang)
    c = jnp.where(lane < ROT_DIM, cos, 1.0)
    sp = jnp.where((lane >= ROT_DIM // 2) & (lane < ROT_DIM), sin, 0.0)
    sm = jnp.where(lane < ROT_DIM // 2, -sin, 0.0)
    return c, sp, sm


def _head_sums(v):
    same = (lax.broadcasted_iota(jnp.int32, (128, 128), 0) < HD) == (lax.broadcasted_iota(jnp.int32, (128, 128), 1) < HD)
    ones = jnp.where(same, 1.0, 0.0).astype(BF16)
    hi = v.astype(BF16)
    lo = (v - hi.astype(F32)).astype(BF16)
    return _nn(hi, ones) + _nn(lo, ones)


def _pair_norm(t):
    return lax.rsqrt(_head_sums(t * t) * (1.0 / HD) + EPS)


def _pair_mean(t):
    return _head_sums(t) * (1.0 / HD)


TT = 256
NCH = ATT_QKV // 128


def _res_shape(grp, dtype):
    return S((DILATIONS[grp], T // DILATIONS[grp], ATT_W), dtype)


def _res_spec(grp):
    dil = DILATIONS[grp]
    return pl.BlockSpec((dil, TT // dil, ATT_W), lambda i: (0, i, 0))


def _to_residues(sc, j, dst_ref, dil, cols):
    n = TT // dil
    for r in range(dil):
        rows = sc[j] if dil == 1 else sc.at[j][pl.ds(r, n, stride=dil), :]
        dst_ref[r, :, cols] = rows.astype(dst_ref.dtype)


def _from_residues(src_ref, cols, sc, j, dil):
    n = TT // dil
    for r in range(dil):
        if dil == 1:
            sc[j] = src_ref[r, :, cols]
        else:
            sc.at[j][pl.ds(r, n, stride=dil), :] = src_ref[r, :, cols]


def _tok_spec(width, cblk=0):
    return pl.BlockSpec((TT, width), functools.partial(lambda i, c: (i, c), c=cblk))


def _const_spec(arr_or_shape):
    shape = arr_or_shape if isinstance(arr_or_shape, tuple) else arr_or_shape.shape
    return pl.BlockSpec(shape, functools.partial(lambda i, nd: (0,) * nd, nd=len(shape)))


def _qk_prep(proj, pos, inv, gq, gk):
    def body(q_ref, k_ref, v_ref, pos_ref, inv_ref, gq_ref, gk_ref, *rest):
        outs, sc = rest[:9], rest[9]
        c, sp, sm = _rot_tables(pos_ref, inv_ref)
        for which, (src, g_ref) in enumerate(((q_ref, gq_ref), (k_ref, gk_ref), (v_ref, None))):
            for j in range(NCH):
                t = src[:, j * 128:(j + 1) * 128]
                if g_ref is not None:
                    n = t * _pair_norm(t) * g_ref[...]
                    t = n * c + pltpu.roll(n, 8, 1) * sp + pltpu.roll(n, 120, 1) * sm
                sc[j] = t
            for j in range(NCH):
                grp, sub = divmod(j * 128, ATT_W)
                _to_residues(sc, j, outs[which * 3 + grp], DILATIONS[grp], slice(sub, sub + 128))

    return pl.pallas_call(
        body, name="qk_prep", grid=(T // TT,),
        in_specs=[_tok_spec(ATT_QKV, 0), _tok_spec(ATT_QKV, 1), _tok_spec(ATT_QKV, 2), _tok_spec(1), _const_spec(inv),
                  _const_spec(gq), _const_spec(gk)],
        out_specs=[_res_spec(g) for _ in range(3) for g in range(3)],
        out_shape=[_res_shape(g, BF16) for _ in range(3) for g in range(3)],
        scratch_shapes=[pltpu.VMEM((NCH, TT, 128), F32)],
        compiler_params=pltpu.CompilerParams(dimension_semantics=("arbitrary",)),
    )(proj, proj, proj, pos, inv, gq, gk)


def _qk_bwd(proj, pos, inv, gq, gk, dqs, dks, dvs):
    def body(q_ref, k_ref, pos_ref, inv_ref, gq_ref, gk_ref, dq0, dq1, dq2, dk0, dk1, dk2, dv0, dv1, dv2,
             dqa_ref, dka_ref, dva_ref, dgq_ref, dgk_ref, sc):
        c, sp, sm = _rot_tables(pos_ref, inv_ref)
        first = pl.program_id(0) == 0

        def gather(drefs):
            for j in range(NCH):
                grp, sub = divmod(j * 128, ATT_W)
                _from_residues(drefs[grp], slice(sub, sub + 128), sc, j, DILATIONS[grp])

        for src, g_ref, drefs, dst, dg_ref in ((q_ref, gq_ref, (dq0, dq1, dq2), dqa_ref, dgq_ref),
                                               (k_ref, gk_ref, (dk0, dk1, dk2), dka_ref, dgk_ref)):
            gather(drefs)
            dg = jnp.zeros((1, 128), F32)
            for j in range(NCH):
                cols = slice(j * 128, (j + 1) * 128)
                d_rot = sc[j]
                dn = d_rot * c + pltpu.roll(d_rot * sp, 120, 1) + pltpu.roll(d_rot * sm, 8, 1)
                t = src[:, cols]
                r = _pair_norm(t)
                u = dn * g_ref[...]
                dst[:, cols] = (r * u - t * (r * r * r) * _pair_mean(u * t)).astype(BF16)
                dg = dg + jnp.sum(dn * t * r, axis=0, keepdims=True)
            dg = dg + pltpu.roll(dg, HD, 1)

            @pl.when(first)
            def _():
                dg_ref[...] = dg

            @pl.when(jnp.logical_not(first))
            def _():
                dg_ref[...] += dg

        gather((dv0, dv1, dv2))
        for j in range(NCH):
            dva_ref[:, j * 128:(j + 1) * 128] = sc[j].astype(BF16)

    return pl.pallas_call(
        body, name="qk_bwd", grid=(T // TT,),
        in_specs=[_tok_spec(ATT_QKV, 0), _tok_spec(ATT_QKV, 1), _tok_spec(1), _const_spec(inv), _const_spec(gq),
                  _const_spec(gk)] + [_res_spec(g) for _ in range(3) for g in range(3)],
        out_specs=[_tok_spec(ATT_QKV)] * 3 + [_const_spec((1, 128))] * 2,
        out_shape=[S((T, ATT_QKV), BF16)] * 3 + [S((1, 128), F32)] * 2,
        scratch_shapes=[pltpu.VMEM((NCH, TT, 128), F32)],
        compiler_params=pltpu.CompilerParams(dimension_semantics=("arbitrary",)),
    )(proj, proj, pos, inv, gq, gk, *dqs, *dks, *dvs)


def _split_heads(t):
    low = lax.broadcasted_iota(jnp.int32, (1, 128), 1) < HD
    zero = jnp.zeros_like(t)
    return jnp.concatenate([jnp.where(low, t, zero), jnp.where(low, zero, t)], axis=0)


def _join_heads(t2):
    low = lax.broadcasted_iota(jnp.int32, (1, 128), 1) < HD
    n = t2.shape[0] // 2
    return jnp.where(low, t2[:n], t2[n:])


def _band_masks():
    row = lax.broadcasted_iota(jnp.int32, (BLK, 2 * BLK), 0)
    col = lax.broadcasted_iota(jnp.int32, (BLK, 2 * BLK), 1) % BLK
    return col <= row, col >= row


def _per_head(col_a, col_b):
    first = lax.broadcasted_iota(jnp.int32, (1, 2 * BLK), 1) < BLK
    return jnp.where(first, col_a, col_b)


def _att_fwd(q, k, v, grp, name):
    dil = DILATIONS[grp]
    nb = T // dil // BLK
    scale = HD ** -0.5

    def body(q_ref, kp_ref, kc_ref, vp_ref, vc_ref, o_ref, lse_ref):
        has_prev = pl.program_id(1) > 0
        m_cur, m_prev = _band_masks()
        m_prev = m_prev & has_prev
        for j in range(ATT_W // 128):
            cols = slice(j * 128, (j + 1) * 128)
            q = q_ref[:, cols]
            s_c = jnp.where(m_cur, _nt(q, _split_heads(kc_ref[:, cols])) * scale, -jnp.inf)
            s_p = jnp.where(m_prev, _nt(q, _split_heads(kp_ref[:, cols])) * scale, -jnp.inf)
            mx = []
            for half in (slice(0, BLK), slice(BLK, 2 * BLK)):
                mx.append(jnp.maximum(jnp.max(s_c[:, half], axis=-1, keepdims=True),
                                      jnp.max(s_p[:, half], axis=-1, keepdims=True)))
            m2 = _per_head(*mx)
            p_c = jnp.exp(s_c - m2)
            p_p = jnp.exp(s_p - m2)
            den = []
            for half in (slice(0, BLK), slice(BLK, 2 * BLK)):
                den.append(jnp.sum(p_c[:, half], axis=-1, keepdims=True) + jnp.sum(p_p[:, half], axis=-1, keepdims=True))
            acc = _nn(p_c.astype(BF16), _split_heads(vc_ref[:, cols])) + _nn(p_p.astype(BF16), _split_heads(vp_ref[:, cols]))
            low = lax.broadcasted_iota(jnp.int32, (1, 128), 1) < HD
            o_ref[:, cols] = acc / jnp.where(low, den[0], den[1])
            lse_ref[:, cols] = jnp.where(low, mx[0] + jnp.log(den[0]), mx[1] + jnp.log(den[1]))

    cur = pl.BlockSpec((None, BLK, ATT_W), lambda r, i: (r, i, 0))
    prev = pl.BlockSpec((None, BLK, ATT_W), lambda r, i: (r, jnp.maximum(i - 1, 0), 0))
    return pl.pallas_call(
        body, name=name, grid=(dil, nb),
        in_specs=[cur, prev, cur, prev, cur],
        out_specs=[cur, cur], out_shape=[_res_shape(grp, F32)] * 2,
        compiler_params=pltpu.CompilerParams(dimension_semantics=("parallel", "arbitrary")),
    )(q, k, k, v, v)


def _att_bwd(q, k, v, datt, att, lse, grp, name):
    dil = DILATIONS[grp]
    nb = T // dil // BLK
    scale = HD ** -0.5

    def body(q0_ref, q1_ref, kp_ref, kc_ref, vp_ref, vc_ref, do0_ref, do1_ref, o0_ref, o1_ref, l0_ref, l1_ref,
             dq_ref, dk_ref, dv_ref):
        i = pl.program_id(1)
        has_prev = i > 0
        has_next = i < nb - 1
        m_cur, m_prev = _band_masks()
        low = lax.broadcasted_iota(jnp.int32, (1, 128), 1) < HD

        def stats(do_ref, o_ref, l_ref, cols):
            prod = do_ref[:, cols] * o_ref[:, cols]
            d_all = jnp.sum(prod, axis=-1, keepdims=True)
            d_low = jnp.sum(jnp.where(low, prod, 0.0), axis=-1, keepdims=True)
            lse_t = l_ref[:, cols]
            return _per_head(d_low, d_all - d_low), _per_head(lse_t[:, 0:1], lse_t[:, HD:HD + 1])

        def pair(q, k2, v2, do, dsum, lse2, mask):
            s = _nt(q, k2) * scale
            p = jnp.where(mask, jnp.exp(s - lse2), 0.0)
            ds = p * (_nt(do, v2) - dsum) * scale
            return p.astype(BF16), ds.astype(BF16)

        for j in range(ATT_W // 128):
            cols = slice(j * 128, (j + 1) * 128)
            q0, q1 = q0_ref[:, cols], q1_ref[:, cols]
            do0, do1 = do0_ref[:, cols].astype(BF16), do1_ref[:, cols].astype(BF16)
            kc2, kp2 = _split_heads(kc_ref[:, cols]), _split_heads(kp_ref[:, cols])
            vc2, vp2 = _split_heads(vc_ref[:, cols]), _split_heads(vp_ref[:, cols])
            d0, l0 = stats(do0_ref, o0_ref, l0_ref, cols)
            d1, l1 = stats(do1_ref, o1_ref, l1_ref, cols)
            p_a, ds_a = pair(q0, kc2, vc2, do0, d0, l0, m_cur)
            _, ds_b = pair(q0, kp2, vp2, do0, d0, l0, m_prev & has_prev)
            p_c, ds_c = pair(q1, kc2, vc2, do1, d1, l1, m_prev & has_next)
            dq_ref[:, cols] = _nn(ds_a, kc2) + _nn(ds_b, kp2)
            dk_ref[:, cols] = _join_heads(_tn(ds_a, q0) + _tn(ds_c, q1))
            dv_ref[:, cols] = _join_heads(_tn(p_a, do0) + _tn(p_c, do1))

    def spec(shift):
        return pl.BlockSpec((None, BLK, ATT_W), lambda r, i: (r, jnp.clip(i + shift, 0, nb - 1), 0))

    here, after, before = spec(0), spec(1), spec(-1)
    return pl.pallas_call(
        body, name=name, grid=(dil, nb),
        in_specs=[here, after, before, here, before, here, here, after, here, after, here, after],
        out_specs=[here] * 3, out_shape=[_res_shape(grp, F32)] * 3,
        compiler_params=pltpu.CompilerParams(dimension_semantics=("parallel", "arbitrary")),
    )(q, q, k, k, v, v, datt, datt, att, att, lse, lse)


def _att_merge(os_, lses, proj):
    nq = ATT_W // 128

    def body(o0, o1, o2, l0, l1, l2, za_ref, att_ref, lse_ref, ain_ref, sc):
        for a, ref in enumerate((o0, o1, o2, l0, l1, l2)):
            for j in range(nq):
                _from_residues(ref, slice(j * 128, (j + 1) * 128), sc, a * nq + j, DILATIONS[a % 3])
        for j in range(nq):
            cols = slice(j * 128, (j + 1) * 128)
            oa, ob, oc = (sc[a * nq + j] for a in range(3))
            la, lb, lc = (sc[(3 + a) * nq + j] for a in range(3))
            m = jnp.maximum(jnp.maximum(la, lb), lc)
            wa, wb, wc = jnp.exp(la - m), jnp.exp(lb - m), jnp.exp(lc - m)
            tot = wa + wb + wc
            att = (wa * oa + wb * ob + wc * oc) / tot
            att_ref[:, cols] = att
            lse_ref[:, cols] = m + jnp.log(tot)
            za = za_ref[:, cols]
            ain_ref[:, cols] = (att * za * _sigmoid(za)).astype(BF16)

    return pl.pallas_call(
        body, name="att_merge", grid=(T // TT,),
        in_specs=[_res_spec(g) for _ in range(2) for g in range(3)] + [_tok_spec(ATT_W, C_ZA // ATT_W)],
        out_specs=[_tok_spec(ATT_W)] * 3,
        out_shape=[S((T, ATT_W), F32), S((T, ATT_W), F32), S((T, ATT_W), BF16)],
        scratch_shapes=[pltpu.VMEM((6 * nq, TT, 128), F32)],
        compiler_params=pltpu.CompilerParams(dimension_semantics=("arbitrary",)),
    )(*os_, *lses, proj)


def _att_gate_bwd(dain, att, lse, proj):
    nq = ATT_W // 128

    def body(d_ref, att_ref, lse_ref, za_ref, dza_ref, da0, da1, da2, at1, at2, ls1, ls2, sc):
        for j in range(nq):
            cols = slice(j * 128, (j + 1) * 128)
            za = za_ref[:, cols]
            sg = _sigmoid(za)
            d = d_ref[:, cols]
            att_ = att_ref[:, cols]
            dza_ref[:, cols] = (d * att_ * sg * (1.0 + za * (1.0 - sg))).astype(BF16)
            sc[j] = d * za * sg
            sc[nq + j] = att_
            sc[2 * nq + j] = lse_ref[:, cols]
        for j in range(nq):
            cols = slice(j * 128, (j + 1) * 128)
            for grp, dst in enumerate((da0, da1, da2)):
                _to_residues(sc, j, dst, DILATIONS[grp], cols)
            for grp, dst in ((1, at1), (2, at2)):
                _to_residues(sc, nq + j, dst, DILATIONS[grp], cols)
            for grp, dst in ((1, ls1), (2, ls2)):
                _to_residues(sc, 2 * nq + j, dst, DILATIONS[grp], cols)

    res = (0, 1, 2, 1, 2, 1, 2)
    return pl.pallas_call(
        body, name="att_gate_bwd", grid=(T // TT,),
        in_specs=[_tok_spec(ATT_W)] * 3 + [_tok_spec(ATT_W, C_ZA // ATT_W)],
        out_specs=[_tok_spec(ATT_W)] + [_res_spec(g) for g in res],
        out_shape=[S((T, ATT_W), BF16)] + [_res_shape(g, F32) for g in res],
        scratch_shapes=[pltpu.VMEM((3 * nq, TT, 128), F32)],
        compiler_params=pltpu.CompilerParams(dimension_semantics=("arbitrary",)),
    )(dain, att, lse, proj)


def _split3(v):
    hi = v.astype(BF16)
    r1 = v - hi.astype(F32)
    mid = r1.astype(BF16)
    lo = (r1 - mid.astype(F32)).astype(BF16)
    return hi, mid, lo


def _tri_sum(v, upper):
    n = v.shape[0]
    row = lax.broadcasted_iota(jnp.int32, (n, n), 0)
    col = lax.broadcasted_iota(jnp.int32, (n, n), 1)
    tri = jnp.where(col >= row if upper else col <= row, 1.0, 0.0).astype(BF16)
    hi, mid, lo = _split3(v)
    return _nn(tri, hi) + _nn(tri, mid) + _nn(tri, lo)


def _gla_gates(glr_ref, w2_ref, b_ref):
    logit = _nn(glr_ref[...].astype(BF16), w2_ref[...]) + b_ref[...]
    lg = (jnp.minimum(logit, 0.0) - jnp.log(1.0 + jnp.exp(-jnp.abs(logit)))) * (1.0 / GLA_TAU)
    return logit, _tri_sum(lg, upper=False)


def _gla_head(cum, q_ref, k_ref, h):
    cols = slice(h * GDK, (h + 1) * GDK)
    b = cum[:, cols]
    last = b[GLA_C - 1:GLA_C, :]
    e_pos = jnp.exp(b)
    e_neg = jnp.exp(-b)
    e_end = jnp.exp(last - b)
    qt = q_ref[:, cols] * (GDK ** -0.5) * e_pos
    kt = k_ref[:, cols] * e_neg
    kh = k_ref[:, cols] * e_end
    return b, last, e_pos, e_neg, e_end, qt, kt, kh


def _causal(n):
    return lax.broadcasted_iota(jnp.int32, (n, n), 1) <= lax.broadcasted_iota(jnp.int32, (n, n), 0)


def _gla_fwd(proj, w2p, bg, gn):
    nc = T // GLA_C

    def body(q_ref, k_ref, v_ref, glr_ref, zg_ref, w2_ref, b_ref, gn_ref, o_ref, bin_ref, st_ref, state):
        @pl.when(pl.program_id(0) == 0)
        def _():
            state[...] = jnp.zeros_like(state)

        _, cum = _gla_gates(glr_ref, w2_ref, b_ref)
        for h in range(GH):
            _, last, _, _, _, qt, kt, kh = _gla_head(cum, q_ref, k_ref, h)
            vcols = slice(h * GDV, (h + 1) * GDV)
            st = state[h]
            st_ref[0, h] = st
            v = v_ref[:, vcols].astype(BF16)
            qb = qt.astype(BF16)
            a = jnp.where(_causal(GLA_C), _nt(qb, kt.astype(BF16)), 0.0)
            o = _nt(qb, st.astype(BF16)) + _nn(a.astype(BF16), v)
            state[h] = st * jnp.exp(last) + _tn(v, kh.astype(BF16))
            o_ref[:, vcols] = o
            r = lax.rsqrt(jnp.mean(o * o, axis=-1, keepdims=True) + EPS)
            zg = zg_ref[:, vcols]
            bin_ref[:, vcols] = (o * r * gn_ref[...] * zg * _sigmoid(zg)).astype(BF16)

    row = lambda width, cblk: pl.BlockSpec((GLA_C, width), functools.partial(lambda i, c: (i, c), c=cblk))
    full = lambda a: pl.BlockSpec(a.shape, functools.partial(lambda i, nd: (0,) * nd, nd=a.ndim))
    return pl.pallas_call(
        body, name="gla_fwd", grid=(nc,),
        in_specs=[row(512, C_QG // 512), row(512, C_KG // 512), row(1024, C_VG // 1024), row(128, C_GLR // 128),
                  row(1024, C_ZG // 1024), full(w2p), full(bg), full(gn)],
        out_specs=[pl.BlockSpec((GLA_C, GH * GDV), lambda i: (i, 0)), pl.BlockSpec((GLA_C, GH * GDV), lambda i: (i, 0)),
                   pl.BlockSpec((1, GH, GDV, GDK), lambda i: (i, 0, 0, 0))],
        out_shape=[S((T, GH * GDV), F32), S((T, GH * GDV), BF16), S((nc, GH, GDV, GDK), F32)],
        scratch_shapes=[pltpu.VMEM((GH, GDV, GDK), F32)],
        compiler_params=pltpu.CompilerParams(dimension_semantics=("arbitrary",)),
    )(proj, proj, proj, proj, proj, w2p, bg, gn)


def _gla_bwd(proj, w2p, bg, gn, o_gla, states, dbin):
    nc = T // GLA_C

    def body(q_ref, k_ref, v_ref, glr_ref, zg_ref, w2_ref, b_ref, gn_ref, o_ref, st_ref, dbin_ref,
             dq_ref, dk_ref, dv_ref, dglr_ref, dzg_ref, dw2_ref, dbg_ref, dgn_ref, dstate, dlogit):
        first = pl.program_id(0) == 0

        @pl.when(first)
        def _():
            dstate[...] = jnp.zeros_like(dstate)

        logit, cum = _gla_gates(glr_ref, w2_ref, b_ref)
        is_last = lax.broadcasted_iota(jnp.int32, (GLA_C, 1), 0) == GLA_C - 1
        dgn = jnp.zeros((1, GDV), F32)
        for h in range(GH):
            _, last, e_pos, e_neg, e_end, qt, kt, kh = _gla_head(cum, q_ref, k_ref, h)
            cols = slice(h * GDK, (h + 1) * GDK)
            vcols = slice(h * GDV, (h + 1) * GDV)
            o = o_ref[:, vcols]
            r = lax.rsqrt(jnp.mean(o * o, axis=-1, keepdims=True) + EPS)
            zg = zg_ref[:, vcols]
            sg = _sigmoid(zg)
            db_ = dbin_ref[:, vcols]
            dlin = db_ * zg * sg
            dzg_ref[:, vcols] = (db_ * (o * r * gn_ref[...]) * sg * (1.0 + zg * (1.0 - sg))).astype(BF16)
            u = dlin * gn_ref[...]
            do = (r * u - o * (r * r * r) * jnp.mean(u * o, axis=-1, keepdims=True)).astype(BF16)
            dgn = dgn + jnp.sum(dlin * o * r, axis=0, keepdims=True)
            st = st_ref[0, h]
            dst = dstate[h]
            v = v_ref[:, vcols].astype(BF16)
            qb, kb, khb = qt.astype(BF16), kt.astype(BF16), kh.astype(BF16)
            dstb = dst.astype(BF16)
            causal = _causal(GLA_C)
            a = jnp.where(causal, _nt(qb, kb), 0.0).astype(BF16)
            da = jnp.where(causal, _nt(do, v), 0.0).astype(BF16)
            dqt = _nn(do, st.astype(BF16)) + _nn(da, kb)
            dkt = _tn(da, qb)
            dkh = _nn(v, dstb)
            dv_ref[:, vcols] = (_tn(a, do) + _nt(khb, dstb)).astype(BF16)
            lam = jnp.exp(last)
            dlam = jnp.sum(dst * st, axis=0, keepdims=True)
            dstate[h] = dst * lam + _tn(do, qb)
            dq_ref[:, cols] = (dqt * e_pos * (GDK ** -0.5)).astype(BF16)
            dk_ref[:, cols] = (dkt * e_neg + dkh * e_end).astype(BF16)
            dkh_kh = dkh * kh
            dcum = dqt * qt - dkt * kt - dkh_kh
            dlast = jnp.sum(dkh_kh, axis=0, keepdims=True) + dlam * lam
            dcum = jnp.where(is_last, dcum + dlast, dcum)
            dlg = _tri_sum(dcum, upper=True)
            dlogit[:, cols] = dlg * (1.0 / GLA_TAU) * (1.0 - _sigmoid(logit[:, cols]))

        dl = dlogit[...]
        dlb = dl.astype(BF16)
        dglr_ref[...] = _nt(dlb, w2_ref[...]).astype(BF16)
        dw2 = _tn(glr_ref[...].astype(BF16), dlb)
        dbg = jnp.sum(dl, axis=0, keepdims=True)

        @pl.when(first)
        def _():
            dw2_ref[...] = dw2
            dbg_ref[...] = dbg
            dgn_ref[...] = dgn

        @pl.when(jnp.logical_not(first))
        def _():
            dw2_ref[...] += dw2
            dbg_ref[...] += dbg
            dgn_ref[...] += dgn

    rev = lambda i: nc - 1 - i
    row = lambda width, cblk: pl.BlockSpec((GLA_C, width), functools.partial(lambda i, c: (rev(i), c), c=cblk))
    full = lambda a: pl.BlockSpec(a.shape, functools.partial(lambda i, nd: (0,) * nd, nd=a.ndim))
    keep = lambda shape: pl.BlockSpec(shape, functools.partial(lambda i, nd: (0,) * nd, nd=len(shape)))
    return pl.pallas_call(
        body, name="gla_bwd", grid=(nc,),
        in_specs=[row(512, C_QG // 512), row(512, C_KG // 512), row(1024, C_VG // 1024), row(128, C_GLR // 128),
                  row(1024, C_ZG // 1024), full(w2p), full(bg), full(gn), row(GH * GDV, 0),
                  pl.BlockSpec((1, GH, GDV, GDK), lambda i: (rev(i), 0, 0, 0)), row(GH * GDV, 0)],
        out_specs=[row(512, 0), row(512, 0), row(1024, 0), row(128, 0), row(1024, 0),
                   keep((128, 512)), keep((1, 512)), keep((1, GDV))],
        out_shape=[S((T, 512), BF16), S((T, 512), BF16), S((T, 1024), BF16), S((T, 128), BF16), S((T, 1024), BF16),
                   S((128, 512), F32), S((1, 512), F32), S((1, GDV), F32)],
        scratch_shapes=[pltpu.VMEM((GH, GDV, GDK), F32), pltpu.VMEM((GLA_C, GH * GDK), F32)],
        compiler_params=pltpu.CompilerParams(dimension_semantics=("arbitrary",)),
    )(proj, proj, proj, proj, proj, w2p, bg, gn, o_gla, states, dbin)


def _merge_fwd(ya, yb, proj):
    def body(ya_ref, yb_ref, ga_ref, gb_ref, y_ref):
        y_ref[...] = (_sigmoid(ga_ref[...]) * ya_ref[...] + _sigmoid(gb_ref[...]) * yb_ref[...]).astype(BF16)

    return _rowcall(body, "merge_fwd", 512,
                    [_rows(ya), _rows(yb), _rows(proj, D, C_GA // D), _rows(proj, D, C_GB // D)],
                    [("rows", D, BF16)])[0]


def _merge_bwd(dy, ya, yb, proj):
    def body(dy_ref, ya_ref, yb_ref, ga_ref, gb_ref, dga_ref, dgb_ref, dya_ref, dyb_ref):
        dy_ = dy_ref[...]
        sa, sb = _sigmoid(ga_ref[...]), _sigmoid(gb_ref[...])
        dga_ref[...] = (dy_ * ya_ref[...] * sa * (1.0 - sa)).astype(BF16)
        dgb_ref[...] = (dy_ * yb_ref[...] * sb * (1.0 - sb)).astype(BF16)
        dya_ref[...] = (dy_ * sa).astype(BF16)
        dyb_ref[...] = (dy_ * sb).astype(BF16)

    return _rowcall(body, "merge_bwd", 512,
                    [_rows(dy), _rows(ya), _rows(yb), _rows(proj, D, C_GA // D), _rows(proj, D, C_GB // D)],
                    [("rows", D, BF16)] * 4)


def _loss_head(x1, e, u, target):
    def body(x1_ref, e_ref, u_ref, t_ref, loss_ref, dout_ref, de_ref, du_ref, acc):
        first = pl.program_id(0) == 0
        pg = _sigmoid(u_ref[...])
        e_ = e_ref[...]
        diff = x1_ref[...] + e_ * pg - t_ref[...]
        part = jnp.sum(diff * diff, axis=0, keepdims=True)

        @pl.when(first)
        def _():
            acc[...] = part

        @pl.when(jnp.logical_not(first))
        def _():
            acc[...] += part

        dout = diff * (1.0 / D)
        dout_ref[...] = dout
        de_ref[...] = (dout * pg).astype(BF16)
        du_ref[...] = (dout * e_ * pg * (1.0 - pg)).astype(BF16)
        loss_ref[...] = jnp.zeros((1, 128), F32) + jnp.sum(acc[...], axis=-1, keepdims=True) * (0.5 / D)

    return _rowcall(body, "loss_head", 256, [_rows(x1), _rows(e), _rows(u), _rows(target)],
                    [("acc", (1, 128), F32), ("rows", D, F32), ("rows", D, BF16), ("rows", D, BF16)],
                    scratch=[pltpu.VMEM((1, D), F32)])


def _peer(k):
    x, y, c = lax.axis_index("x"), lax.axis_index("y"), lax.axis_index("c")
    return (x ^ ((k >> 2) & 1), y ^ ((k >> 1) & 1), c ^ (k & 1))


def _my_index():
    return 4 * lax.axis_index("x") + 2 * lax.axis_index("y") + lax.axis_index("c")


def _peer_index(k):
    px, py, pc = _peer(k)
    return 4 * px + 2 * py + pc


def _all_gather(arrs, name):
    n = len(arrs)

    def body(*refs):
        ins, outs = refs[:n], refs[n:2 * n]
        send, recv, local = refs[2 * n:]
        me = _my_index()
        copies = []
        for a in range(n):
            own = pltpu.make_async_copy(ins[a], outs[a].at[me], local.at[a])
            own.start()
            copies.append(own)
        rdmas = []
        for k in range(1, NDEV):
            for a in range(n):
                cp = pltpu.make_async_remote_copy(
                    src_ref=ins[a], dst_ref=outs[a].at[me], send_sem=send.at[k - 1, a], recv_sem=recv.at[k - 1, a],
                    device_id=_peer(k), device_id_type=MESH)
                cp.start()
                rdmas.append(cp)
        for k in range(1, NDEV):
            for a in range(n):
                pltpu.make_async_remote_copy(
                    src_ref=ins[a], dst_ref=outs[a].at[_peer_index(k)], send_sem=send.at[k - 1, a],
                    recv_sem=recv.at[k - 1, a], device_id=_peer(k), device_id_type=MESH).wait_recv()
        for cp in rdmas:
            cp.wait_send()
        for cp in copies:
            cp.wait()

    hbm = pl.BlockSpec(memory_space=pl.ANY)
    return pl.pallas_call(
        body, name=name, in_specs=[hbm] * n, out_specs=[hbm] * n,
        out_shape=[S((NDEV,) + a.shape, a.dtype) for a in arrs],
        scratch_shapes=[pltpu.SemaphoreType.DMA((NDEV - 1, n)), pltpu.SemaphoreType.DMA((NDEV - 1, n)),
                        pltpu.SemaphoreType.DMA((n,))],
    )(*arrs)


def _all_gather_by_chip(arrs, name):
    n = len(arrs)

    def body(*refs):
        ins, outs = refs[:n], refs[n:2 * n]
        send, recv, local = refs[2 * n:]
        x, y, c = lax.axis_index("x"), lax.axis_index("y"), lax.axis_index("c")
        me, sibling = (x, y, c), (x, y, 1 - c)
        chips = [(1 - x, y), (x, 1 - y), (1 - x, 1 - y)]

        def copy(k, a, block, to, src=None):
            px, py, pc = block
            slot = outs[a].at[4 * px + 2 * py + pc]
            return pltpu.make_async_remote_copy(
                src_ref=slot if src is None else src, dst_ref=slot, send_sem=send.at[k, a], recv_sem=recv.at[k, a],
                device_id=to, device_id_type=MESH)

        mine = [pltpu.make_async_copy(ins[a], outs[a].at[4 * x + 2 * y + c], local.at[a]) for a in range(n)]
        first = []
        for a in range(n):
            first.append(copy(0, a, me, sibling, src=ins[a]))
            first += [copy(1 + j, a, me, (*chip, c), src=ins[a]) for j, chip in enumerate(chips)]
        for cp in mine + first:
            cp.start()
        passed = []
        for j, chip in enumerate(chips):
            for a in range(n):
                copy(1 + j, a, (*chip, c), me).wait_recv()
                passed.append(copy(4 + j, a, (*chip, c), sibling))
                passed[-1].start()
        for a in range(n):
            copy(0, a, sibling, me).wait_recv()
        for j, chip in enumerate(chips):
            for a in range(n):
                copy(4 + j, a, (*chip, 1 - c), me).wait_recv()
        for cp in first + passed:
            cp.wait_send()
        for cp in mine:
            cp.wait()

    hbm = pl.BlockSpec(memory_space=pl.ANY)
    return pl.pallas_call(
        body, name=name, in_specs=[hbm] * n, out_specs=[hbm] * n,
        out_shape=[S((NDEV,) + a.shape, a.dtype) for a in arrs],
        scratch_shapes=[pltpu.SemaphoreType.DMA((NDEV - 1, n)), pltpu.SemaphoreType.DMA((NDEV - 1, n)),
                        pltpu.SemaphoreType.DMA((n,))],
    )(*arrs)


NCHIP = 4


def _exchange_sibling(arrs, name):
    n = len(arrs)

    def body(*refs):
        ins, outs = refs[:n], refs[n:2 * n]
        send, recv = refs[2 * n:]
        x, y, c = lax.axis_index("x"), lax.axis_index("y"), lax.axis_index("c")
        copies = []
        for q in range(NCHIP):
            for a in range(n):
                copies.append(pltpu.make_async_remote_copy(
                    src_ref=ins[a].at[2 * q + (1 - c)], dst_ref=outs[a].at[q], send_sem=send.at[q, a],
                    recv_sem=recv.at[q, a], device_id=(x, y, 1 - c), device_id_type=MESH))
        for cp in copies:
            cp.start()
        for cp in copies:
            cp.wait_recv()
        for cp in copies:
            cp.wait_send()

    hbm = pl.BlockSpec(memory_space=pl.ANY)
    return pl.pallas_call(
        body, name=name, in_specs=[hbm] * n, out_specs=[hbm] * n,
        out_shape=[S((NCHIP,) + a.shape[1:], a.dtype) for a in arrs],
        scratch_shapes=[pltpu.SemaphoreType.DMA((NCHIP, n)), pltpu.SemaphoreType.DMA((NCHIP, n))],
    )(*arrs)


def _pair_add(mine, got, core, name):
    _, rows, cols = mine.shape
    tr = rows if rows * cols <= 1 << 19 else 128
    assert rows % tr == 0

    def body(core_ref, a_ref, b_ref, o_ref):
        o_ref[...] = (a_ref[...].astype(F32) + b_ref[...].astype(F32)).astype(BF16)

    return pl.pallas_call(
        body, name=name,
        grid_spec=pltpu.PrefetchScalarGridSpec(
            num_scalar_prefetch=1, grid=(NCHIP, rows // tr),
            in_specs=[pl.BlockSpec((None, tr, cols), lambda q, i, core_ref: (2 * q + core_ref[0], i, 0)),
                      pl.BlockSpec((None, tr, cols), lambda q, i, core_ref: (q, i, 0))],
            out_specs=pl.BlockSpec((None, tr, cols), lambda q, i, core_ref: (q, i, 0))),
        out_shape=S((NCHIP, rows, cols), BF16),
    )(core, mine, got)


def _exchange_chips(arrs, name):
    n = len(arrs)

    def body(*refs):
        ins, outs = refs[:n], refs[n:2 * n]
        send, recv, local = refs[2 * n:]
        x, y, c = lax.axis_index("x"), lax.axis_index("y"), lax.axis_index("c")
        here = 2 * x + y
        chips = [(1 - x, y), (x, 1 - y), (1 - x, 1 - y)]
        own = [pltpu.make_async_copy(ins[a].at[here], outs[a].at[here], local.at[a]) for a in range(n)]
        sent = []
        for j, (cx, cy) in enumerate(chips):
            for a in range(n):
                sent.append(pltpu.make_async_remote_copy(
                    src_ref=ins[a].at[2 * cx + cy], dst_ref=outs[a].at[here], send_sem=send.at[j, a],
                    recv_sem=recv.at[j, a], device_id=(cx, cy, c), device_id_type=MESH))
        for cp in own + sent:
            cp.start()
        for j, (cx, cy) in enumerate(chips):
            for a in range(n):
                pltpu.make_async_remote_copy(
                    src_ref=ins[a].at[here], dst_ref=outs[a].at[2 * cx + cy], send_sem=send.at[j, a],
                    recv_sem=recv.at[j, a], device_id=(cx, cy, c), device_id_type=MESH).wait_recv()
        for cp in sent:
            cp.wait_send()
        for cp in own:
            cp.wait()

    hbm = pl.BlockSpec(memory_space=pl.ANY)
    return pl.pallas_call(
        body, name=name, in_specs=[hbm] * n, out_specs=[hbm] * n,
        out_shape=[S(a.shape, a.dtype) for a in arrs],
        scratch_shapes=[pltpu.SemaphoreType.DMA((NCHIP - 1, n)), pltpu.SemaphoreType.DMA((NCHIP - 1, n)),
                        pltpu.SemaphoreType.DMA((n,))],
    )(*arrs)


def _adamw(parts, w, m, v, name, tr):
    rows, cols = w.shape
    assert rows % tr == 0
    c1 = 1.0 - ADAM_B1 ** ADAM_STEP
    c2 = 1.0 - ADAM_B2 ** ADAM_STEP

    nparts = parts.shape[0]

    def body(p_ref, w_ref, m_ref, v_ref, g_ref, d_ref, mo_ref, vo_ref):
        g = p_ref[0].astype(F32)
        for s in range(1, nparts):
            g = g + p_ref[s].astype(F32)
        m_new = ADAM_B1 * m_ref[...] + (1.0 - ADAM_B1) * g
        v_new = ADAM_B2 * v_ref[...] + (1.0 - ADAM_B2) * (g * g)
        g_ref[...] = g
        mo_ref[...] = m_new
        vo_ref[...] = v_new
        d_ref[...] = -ADAM_LR * ((m_new / c1) / (jnp.sqrt(v_new / c2) + ADAM_EPS) + ADAM_WD * w_ref[...])

    blk = pl.BlockSpec((tr, cols), lambda i: (i, 0))
    return pl.pallas_call(
        body, name=name, grid=(rows // tr,),
        in_specs=[pl.BlockSpec((nparts, tr, cols), lambda i: (0, i, 0)), blk, blk, blk],
        out_specs=[blk] * 4, out_shape=[S((rows, cols), F32)] * 4,
        compiler_params=pltpu.CompilerParams(dimension_semantics=("parallel",)),
    )(parts, w, m, v)


def _to_aligned(w):
    pad = jnp.zeros((w.shape[0], NCOL - W_IN_COLS), w.dtype)
    return jnp.concatenate([w[:, :GLR_ORIG], w[:, GLR_ORIG + GLR_N:], w[:, GLR_ORIG:GLR_ORIG + GLR_N], pad], axis=1)


def _from_aligned(w):
    return jnp.concatenate([w[:, :C_ZG], w[:, C_GLR:C_GLR + GLR_N], w[:, C_ZG:C_GLR]], axis=1)


def _col_blocks(w, width):
    return w.reshape(w.shape[0], NDEV, width).transpose(1, 0, 2)


def _from_col_blocks(w):
    return w.transpose(1, 0, 2).reshape(w.shape[1], NDEV * w.shape[2])


SMALL = (("norm_g", D), ("qk_norm_q", HD), ("qk_norm_k", HD), ("gla_gate_b", 512), ("gla_norm_g", GDV),
         ("ple_norm_g", D))
SMALL_PAD = 4096


def _local_step(x2, p2, pos, tgt, norm_g, qk_norm_q, qk_norm_k, gla_gate_b, gla_norm_g, ple_norm_g,
                w_al, w2p, w_att_f, w_gla_f, w_out_f, w_pg_f, w_ple_f):
    half = ROT_DIM // 2
    inv8 = jnp.power(jnp.float32(ROPE_THETA), -jnp.arange(half, dtype=F32) * 2.0 / ROT_DIM)
    inv = jnp.tile(jnp.concatenate([inv8, inv8, jnp.zeros((HD - ROT_DIM,), F32)]), 2).reshape(1, 128)
    gq = jnp.tile(qk_norm_q, (1, 2))
    gk = jnp.tile(qk_norm_k, (1, 2))

    h = _rms_fwd(x2, norm_g, "rms1_fwd")
    proj = _mm(h, w_al, mode="nn", name="proj", tm=1024, tn=1152, tk=D)
    qkv = _qk_prep(proj, pos, inv, gq, gk)
    fwd = [_att_fwd(qkv[g], qkv[3 + g], qkv[6 + g], g, f"att_fwd{g}") for g in range(3)]
    att, lse, ain = _att_merge([f[0] for f in fwd], [f[1] for f in fwd], proj)
    o_gla, bin_, states = _gla_fwd(proj, w2p, gla_gate_b, gla_norm_g)
    ya = _mm(ain, w_att_f, mode="nn", name="ya", tm=1024, tn=D, tk=512)
    yb = _mm(bin_, w_gla_f, mode="nn", name="yb", tm=1024, tn=D, tk=D)
    y = _merge_fwd(ya, yb, proj)
    x1 = _mm(y, w_out_f, mode="nn", name="x1", tm=1024, tn=D, tk=D, res=x2)
    n2 = _rms_fwd(x1, ple_norm_g, "rms2_fwd")
    u = _mm(n2, w_pg_f, mode="nn", name="ple_u", tm=1024, tn=D, tk=D)
    e = _mm(p2, w_ple_f, mode="nn", name="ple_e", tm=1024, tn=D, tk=PLE)
    loss_v, dout, de, du = _loss_head(x1, e, u, tgt)

    dw_ple = _mm(p2, de, mode="tn", name="dw_ple", tm=PLE, tn=D, tk=512)
    dw_pg = _mm(n2, du, mode="tn", name="dw_pg", tm=D, tn=D, tk=512)
    dn2 = _mm(du, w_pg_f, mode="nt", name="dn2", tm=1024, tn=D, tk=D)
    dx1, dx1b, dg_ple = _rms_bwd(dn2, x1, ple_norm_g, dout, "rms2_bwd")
    dw_out = _mm(y, dx1b, mode="tn", name="dw_out", tm=D, tn=D, tk=512)
    dy = _mm(dx1b, w_out_f, mode="nt", name="dy", tm=1024, tn=D, tk=D)
    dga, dgb, dya, dyb = _merge_bwd(dy, ya, yb, proj)
    dw_att = _mm(ain, dya, mode="tn", name="dw_att", tm=512, tn=D, tk=512)
    dain = _mm(dya, w_att_f, mode="nt", name="dain", tm=1024, tn=512, tk=D)
    dw_gla = _mm(bin_, dyb, mode="tn", name="dw_gla", tm=D, tn=D, tk=512)
    dbin = _mm(dyb, w_gla_f, mode="nt", name="dbin", tm=1024, tn=D, tk=D)
    dza, da0, da1, da2, at1, at2, ls1, ls2 = _att_gate_bwd(dain, att, lse, proj)
    datts, atts, lses = (da0, da1, da2), (att[None], at1, at2), (lse[None], ls1, ls2)
    bwd = [_att_bwd(qkv[g], qkv[3 + g], qkv[6 + g], datts[g], atts[g], lses[g], g, f"att_bwd{g}") for g in range(3)]
    dqa, dka, dva, dgq, dgk = _qk_bwd(proj, pos, inv, gq, gk, [b[0] for b in bwd], [b[1] for b in bwd],
                                      [b[2] for b in bwd])
    dqg, dkg, dvg, dglr, dzg, dw2, dbg, dgn = _gla_bwd(proj, w2p, gla_gate_b, gla_norm_g, o_gla, states, dbin)
    dproj = jnp.concatenate([dqa, dka, dva, dza, dqg, dkg, dvg, dzg, dga, dgb, dglr], axis=1)
    dw_al = _mm(h, dproj, mode="tn", name="dw_in", tm=D, tn=1152, tk=512)
    dh = _mm(dproj, w_al, mode="nt", name="dh", tm=1024, tn=D, tk=1152)
    grad_x, _, dg_norm = _rms_bwd(dh, x2, norm_g, dx1, "rms1_bwd")

    return dict(loss=loss_v, grad_x=grad_x, dw_al=dw_al, dw2=dw2, dw_att=dw_att, dw_gla=dw_gla, dw_out=dw_out,
                dw_pg=dw_pg, dw_ple=dw_ple, dg_norm=dg_norm, dgq=dgq, dgk=dgk, dbg=dbg, dgn=dgn, dg_ple=dg_ple)


def kernel(x, p, positions, norm_g, w_in, qk_norm_q, qk_norm_k, gla_gate_w2, gla_gate_b, gla_norm_g, w_att_proj, w_gla_proj, w_out, ple_norm_g, w_ple_gate, w_ple, loss_target, m_norm_g, m_w_in, m_qk_norm_q, m_qk_norm_k, m_gla_gate_w2, m_gla_gate_b, m_gla_norm_g, m_w_att_proj, m_w_gla_proj, m_w_out, m_ple_norm_g, m_w_ple_gate, m_w_ple, v_norm_g, v_w_in, v_qk_norm_q, v_qk_norm_k, v_gla_gate_w2, v_gla_gate_b, v_gla_norm_g, v_w_att_proj, v_w_gla_proj, v_w_out, v_ple_norm_g, v_w_ple_gate, v_w_ple):
    x2, p2, tgt = x[0], p[0, 0], loss_target[0]
    pos = positions.astype(F32).reshape(T, 1)

    rows3 = jnp.stack([w_gla_proj[0], w_out[0], w_ple_gate[0]]).astype(BF16)
    cols3 = jnp.concatenate([w_att_proj[0], w_ple[0], jnp.pad(gla_gate_w2[0], ((0, 0), (0, 64)))], axis=0).astype(BF16)
    g_in, g_rows, g_cols = _all_gather_by_chip([w_in[0].astype(BF16), rows3, cols3], "gather_weights")
    w_al = _to_aligned(_from_col_blocks(g_in))
    w_gla_f = g_rows[:, 0].reshape(D, D)
    w_out_f = g_rows[:, 1].reshape(D, D)
    w_pg_f = g_rows[:, 2].reshape(D, D)
    w_att_f = _from_col_blocks(g_cols[:, :512])
    w_ple_f = _from_col_blocks(g_cols[:, 512:768])
    w2_f = _from_col_blocks(g_cols[:, 768:784, :64])
    w2p = jnp.pad(w2_f, ((0, 128 - GLR_N), (0, 0)))

    loc = _local_step(x2, p2, pos, tgt, norm_g, qk_norm_q, qk_norm_k, gla_gate_b, gla_norm_g, ple_norm_g,
                      w_al, w2p, w_att_f, w_gla_f, w_out_f, w_pg_f, w_ple_f)
    loss_v, grad_x, dw_al, dw2, dw_att, dw_gla, dw_out, dw_pg, dw_ple = (
        loc[k] for k in ("loss", "grad_x", "dw_al", "dw2", "dw_att", "dw_gla", "dw_out", "dw_pg", "dw_ple"))
    dg_norm, dgq, dgk, dbg, dgn, dg_ple = (loc[k] for k in ("dg_norm", "dgq", "dgk", "dbg", "dgn", "dg_ple"))

    s_in = _col_blocks(_from_aligned(dw_al), W_IN_SHARD).astype(BF16)
    s_rows = jnp.concatenate([dw_gla.reshape(NDEV, 128, D), dw_out.reshape(NDEV, 128, D), dw_pg.reshape(NDEV, 128, D)],
                             axis=1).astype(BF16)
    s_cols = jnp.concatenate([_col_blocks(dw_att, 128), _col_blocks(dw_ple, 128),
                              jnp.pad(_col_blocks(dw2[:GLR_N], 64), ((0, 0), (0, 0), (0, 64)))], axis=1).astype(BF16)
    small = jnp.concatenate([dg_norm[0], dgq[0, :HD], dgk[0, :HD], dbg[0], dgn[0], dg_ple[0]])
    small = jnp.pad(small, (0, SMALL_PAD - small.shape[0])).reshape(1, 8, SMALL_PAD // 8)
    mine = [s_in, s_rows, s_cols]
    got = _exchange_sibling(mine, "exchange_sibling")
    core = lax.axis_index("c").astype(jnp.int32).reshape(1)
    sums = [_pair_add(a, b, core, f"pair_add{i}") for i, (a, b) in enumerate(zip(mine, got))]
    r_in, r_rows, r_cols = _exchange_chips(sums, "exchange_chips")
    (r_small,) = _all_gather([small], "gather_small")

    outs = {}

    def adam(nm, parts, w, m, v, tr):
        outs[nm] = _adamw(parts, w, m, v, "adam_" + nm, tr)

    adam("w_in", r_in, w_in[0], m_w_in[0], v_w_in[0], 128)
    adam("w_gla_proj", r_rows[:, :128], w_gla_proj[0], m_w_gla_proj[0], v_w_gla_proj[0], 128)
    adam("w_out", r_rows[:, 128:256], w_out[0], m_w_out[0], v_w_out[0], 128)
    adam("w_ple_gate", r_rows[:, 256:], w_ple_gate[0], m_w_ple_gate[0], v_w_ple_gate[0], 128)
    adam("w_att_proj", r_cols[:, :512], w_att_proj[0], m_w_att_proj[0], v_w_att_proj[0], 512)
    adam("w_ple", r_cols[:, 512:768], w_ple[0], m_w_ple[0], v_w_ple[0], 256)
    adam("gla_gate_w2", r_cols[:, 768:784, :64], gla_gate_w2[0], m_gla_gate_w2[0], v_gla_gate_w2[0], 16)
    given = dict(norm_g=(norm_g, m_norm_g, v_norm_g), qk_norm_q=(qk_norm_q, m_qk_norm_q, v_qk_norm_q),
                 qk_norm_k=(qk_norm_k, m_qk_norm_k, v_qk_norm_k), gla_gate_b=(gla_gate_b, m_gla_gate_b, v_gla_gate_b),
                 gla_norm_g=(gla_norm_g, m_gla_norm_g, v_gla_norm_g), ple_norm_g=(ple_norm_g, m_ple_norm_g, v_ple_norm_g))

    def pack(i):
        flat = jnp.concatenate([given[nm][i][0] for nm, _ in SMALL])
        return jnp.pad(flat, (0, SMALL_PAD - flat.shape[0])).reshape(8, SMALL_PAD // 8)

    sm = _adamw(r_small.reshape(NDEV, 8, SMALL_PAD // 8), pack(0), pack(1), pack(2), "adam_small", 8)
    off = 0
    for nm, width in SMALL:
        outs[nm] = [o.reshape(-1)[off:off + width] for o in sm]
        off += width

    loss = lax.psum(loss_v[0, 0], ("x", "y", "c"))
    order = ["norm_g", "w_in", "qk_norm_q", "qk_norm_k", "gla_gate_w2", "gla_gate_b", "gla_norm_g", "w_att_proj",
             "w_gla_proj", "w_out", "ple_norm_g", "w_ple_gate", "w_ple"]
    result = [loss, grad_x[None]]
    for i in range(4):
        result += [outs[nm][i][None] for nm in order]
    return tuple(result)
```

```python
import functools

import jax
import jax.numpy as jnp
from jax import lax
from jax.experimental import pallas as pl
from jax.experimental.pallas import tpu as pltpu

F32 = jnp.float32
BF16 = jnp.bfloat16
S = jax.ShapeDtypeStruct

T = 4096
D = 1024
NDEV = 8
HD = 64
ATT_W = 512
ATT_QKV = 1536
DILATIONS = (1, 4, 16)
BLK = 128
GH, GDK, GDV = 4, 128, 256
GLA_C = 128
PLE = 256
EPS = 1e-6
ROT_DIM = 16
ROPE_THETA = 500000.0
GLA_TAU = 16.0
W_IN_COLS = 10256
W_IN_SHARD = 1282

C_QG, C_KG, C_VG, C_ZG, C_GLR, C_ZA, C_GA, C_GB, C_QA, C_KA, C_VA = (
    0, 512, 1024, 2048, 3072, 3584, 4096, 5120, 6144, 7680, 9216)
GLA_GROUP_W = 3584
GLR_W = 512
NCOL = 10752
GLR_N = 16
O_QA, O_ZA, O_QG, O_GLR, O_ZG, O_GA, O_END = 0, 4608, 5120, 7168, 7184, 8208, 10256

ADAM_LR, ADAM_B1, ADAM_B2, ADAM_EPS, ADAM_WD, ADAM_STEP = 0.001, 0.9, 0.999, 1e-08, 0.01, 10

MESH = pl.DeviceIdType.MESH


def _sigmoid(z):
    return 1.0 / (1.0 + jnp.exp(-z))


def _dot(a, b, dims):
    return lax.dot_general(a, b, (dims, ((), ())), preferred_element_type=F32)


def _nn(a, b):
    return _dot(a, b, ((1,), (0,)))


def _nt(a, b):
    return _dot(a, b, ((1,), (1,)))


def _tn(a, b):
    return _dot(a, b, ((0,), (0,)))


def _mm(a, b, *, mode, name, tm, tn, tk, out_dtype=F32, res=None, side=None):
    if mode == "nn":
        (m, k), n = a.shape, b.shape[1]
        a_spec = pl.BlockSpec((tm, tk), lambda i, j, l: (i, l))
        b_spec = pl.BlockSpec((tk, tn), lambda i, j, l: (l, j))
        dot = _nn
    elif mode == "nt":
        (m, k), n = a.shape, b.shape[0]
        a_spec = pl.BlockSpec((tm, tk), lambda i, j, l: (i, l))
        b_spec = pl.BlockSpec((tn, tk), lambda i, j, l: (j, l))
        dot = _nt
    else:
        (k, m), n = a.shape, b.shape[1]
        a_spec = pl.BlockSpec((tk, tm), lambda i, j, l: (l, i))
        b_spec = pl.BlockSpec((tk, tn), lambda i, j, l: (l, j))
        dot = _tn
    assert m % tm == 0 and n % tn == 0 and k % tk == 0, (name, m, n, k)
    grid = (m // tm, n // tn, k // tk)
    nk = grid[2]
    o_spec = pl.BlockSpec((tm, tn), lambda i, j, l: (i, j))
    in_specs = [a_spec, b_spec]
    args = [a, b]
    if res is not None:
        in_specs.append(o_spec)
        args.append(res)
    n_in = len(args)
    n_side = 0 if side is None else len(side["arrs"])
    hbm = pl.BlockSpec(memory_space=pl.ANY)

    def body(*refs):
        a_ref, b_ref = refs[0], refs[1]
        r_ref = refs[2] if res is not None else None
        o_ref = refs[n_in + n_side]
        scratch = refs[n_in + 2 * n_side + 1:]
        if side is not None:
            start, finish_side = side["plan"](refs[n_in:n_in + n_side], refs[n_in + n_side + 1:n_in + 2 * n_side + 1],
                                              *scratch[1 if nk > 1 else 0:])
            ids = [pl.program_id(d) for d in range(3)]

            @pl.when((ids[0] == 0) & (ids[1] == 0) & (ids[2] == 0))
            def _():
                start()

        part = dot(a_ref[...].astype(BF16), b_ref[...].astype(BF16))

        def finish(val):
            if r_ref is not None:
                val = val + r_ref[...]
            o_ref[...] = val.astype(out_dtype)

        if nk == 1:
            finish(part)
        else:
            acc = scratch[0]
            l = pl.program_id(2)

            @pl.when(l == 0)
            def _():
                acc[...] = part

            @pl.when(l > 0)
            def _():
                acc[...] += part

            @pl.when(l == nk - 1)
            def _():
                finish(acc[...])

        if side is not None:
            @pl.when((ids[0] == grid[0] - 1) & (ids[1] == grid[1] - 1) & (ids[2] == grid[2] - 1))
            def _():
                finish_side()

    sems = [] if side is None else side["scratch"]
    outs = pl.pallas_call(
        body, name=name, grid=grid,
        in_specs=in_specs + [hbm] * n_side, out_specs=[o_spec] + [hbm] * n_side,
        out_shape=[S((m, n), out_dtype)] + ([] if side is None else side["out_shape"]),
        scratch_shapes=([pltpu.VMEM((tm, tn), F32)] if nk > 1 else []) + sems,
        compiler_params=pltpu.CompilerParams(
            dimension_semantics=("arbitrary",) * 3 if side is not None else ("parallel", "parallel", "arbitrary")),
    )(*args, *([] if side is None else side["arrs"]))
    return outs[0] if side is None else (outs[0], outs[1:])


def _rows(arr, width=None, cblk=0):
    return ("rows", arr, arr.shape[1] if width is None else width, cblk)


def _whole(arr):
    return ("whole", arr)


def _rowcall(body, name, tt, ins, outs, scratch=()):
    in_specs, args = [], []
    for spec in ins:
        if spec[0] == "rows":
            _, arr, width, cblk = spec
            in_specs.append(pl.BlockSpec((tt, width), functools.partial(lambda i, c: (i, c), c=cblk)))
        else:
            arr = spec[1]
            in_specs.append(pl.BlockSpec(arr.shape, functools.partial(lambda i, nd: (0,) * nd, nd=arr.ndim)))
        args.append(arr)
    out_specs, out_shape = [], []
    for kind, shape, dtype in outs:
        if kind == "rows":
            out_specs.append(pl.BlockSpec((tt, shape), lambda i: (i, 0)))
            out_shape.append(S((T, shape), dtype))
        else:
            out_specs.append(pl.BlockSpec(shape, functools.partial(lambda i, nd: (0,) * nd, nd=len(shape))))
            out_shape.append(S(shape, dtype))
    return pl.pallas_call(
        body, name=name, grid=(T // tt,), in_specs=in_specs, out_specs=out_specs, out_shape=out_shape,
        scratch_shapes=list(scratch),
        compiler_params=pltpu.CompilerParams(dimension_semantics=("arbitrary",)),
    )(*args)


def _rms_fwd(x, g, name):
    def body(x_ref, g_ref, h_ref):
        xf = x_ref[...]
        r = lax.rsqrt(jnp.mean(xf * xf, axis=-1, keepdims=True) + EPS)
        h_ref[...] = (xf * r * g_ref[...]).astype(BF16)

    return _rowcall(body, name, 512, [_rows(x), _whole(g)], [("rows", D, BF16)])[0]


def _rms_bwd(dn, x, g, skip, name):
    def body(dn_ref, x_ref, g_ref, s_ref, dx_ref, dxb_ref, dg_ref):
        xf = x_ref[...]
        r = lax.rsqrt(jnp.mean(xf * xf, axis=-1, keepdims=True) + EPS)
        dn_ = dn_ref[...]
        u = dn_ * g_ref[...]
        dx = s_ref[...] + r * u - xf * (r * r * r) * jnp.mean(u * xf, axis=-1, keepdims=True)
        dx_ref[...] = dx
        dxb_ref[...] = dx.astype(BF16)
        part = jnp.sum(dn_ * xf * r, axis=0, keepdims=True)

        @pl.when(pl.program_id(0) == 0)
        def _():
            dg_ref[...] = part

        @pl.when(pl.program_id(0) > 0)
        def _():
            dg_ref[...] += part

    return _rowcall(body, name, 256, [_rows(dn), _rows(x), _whole(g), _rows(skip)],
                    [("rows", D, F32), ("rows", D, BF16), ("acc", (1, D), F32)])


def _rot_tables(pos_ref, inv_ref):
    lane = lax.broadcasted_iota(jnp.int32, (1, 128), 1) % HD
    ang = pos_ref[...] * inv_ref[...]
    cos, sin = jnp.cos(ang), jnp.sin(ang)
    c = jnp.where(lane < ROT_DIM, cos, 1.0)
    sp = jnp.where((lane >= ROT_DIM // 2) & (lane < ROT_DIM), sin, 0.0)
    sm = jnp.where(lane < ROT_DIM // 2, -sin, 0.0)
    return c, sp, sm


def _head_sums(v):
    same = (lax.broadcasted_iota(jnp.int32, (128, 128), 0) < HD) == (lax.broadcasted_iota(jnp.int32, (128, 128), 1) < HD)
    ones = jnp.where(same, 1.0, 0.0).astype(BF16)
    hi = v.astype(BF16)
    lo = (v - hi.astype(F32)).astype(BF16)
    return _nn(hi, ones) + _nn(lo, ones)


def _pair_norm(t):
    return lax.rsqrt(_head_sums(t * t) * (1.0 / HD) + EPS)


def _pair_mean(t):
    return _head_sums(t) * (1.0 / HD)


TT = 256
NCH = ATT_QKV // 128


def _res_shape(grp, dtype):
    return S((DILATIONS[grp], T // DILATIONS[grp], ATT_W), dtype)


def _res_spec(grp):
    dil = DILATIONS[grp]
    return pl.BlockSpec((dil, TT // dil, ATT_W), lambda i: (0, i, 0))


def _to_residues(sc, j, dst_ref, dil, cols):
    n = TT // dil
    for r in range(dil):
        rows = sc[j] if dil == 1 else sc.at[j][pl.ds(r, n, stride=dil), :]
        dst_ref[r, :, cols] = rows.astype(dst_ref.dtype)


def _from_residues(src_ref, cols, sc, j, dil):
    n = TT // dil
    for r in range(dil):
        if dil == 1:
            sc[j] = src_ref[r, :, cols]
        else:
            sc.at[j][pl.ds(r, n, stride=dil), :] = src_ref[r, :, cols]


def _tok_spec(width, cblk=0):
    return pl.BlockSpec((TT, width), functools.partial(lambda i, c: (i, c), c=cblk))


def _const_spec(arr_or_shape):
    shape = arr_or_shape if isinstance(arr_or_shape, tuple) else arr_or_shape.shape
    return pl.BlockSpec(shape, functools.partial(lambda i, nd: (0,) * nd, nd=len(shape)))


def _qk_prep(proj, pos, inv, gq, gk):
    def body(q_ref, k_ref, v_ref, pos_ref, inv_ref, gq_ref, gk_ref, *rest):
        outs, sc = rest[:9], rest[9]
        c, sp, sm = _rot_tables(pos_ref, inv_ref)
        for which, (src, g_ref) in enumerate(((q_ref, gq_ref), (k_ref, gk_ref), (v_ref, None))):
            for j in range(NCH):
                t = src[:, j * 128:(j + 1) * 128]
                if g_ref is not None:
                    n = t * _pair_norm(t) * g_ref[...]
                    t = n * c + pltpu.roll(n, 8, 1) * sp + pltpu.roll(n, 120, 1) * sm
                sc[j] = t
            for j in range(NCH):
                grp, sub = divmod(j * 128, ATT_W)
                _to_residues(sc, j, outs[which * 3 + grp], DILATIONS[grp], slice(sub, sub + 128))

    return pl.pallas_call(
        body, name="qk_prep", grid=(T // TT,),
        in_specs=[_tok_spec(ATT_QKV, C_QA // ATT_QKV), _tok_spec(ATT_QKV, C_KA // ATT_QKV),
                  _tok_spec(ATT_QKV, C_VA // ATT_QKV), _tok_spec(1), _const_spec(inv), _const_spec(gq), _const_spec(gk)],
        out_specs=[_res_spec(g) for _ in range(3) for g in range(3)],
        out_shape=[_res_shape(g, BF16) for _ in range(3) for g in range(3)],
        scratch_shapes=[pltpu.VMEM((NCH, TT, 128), F32)],
        compiler_params=pltpu.CompilerParams(dimension_semantics=("arbitrary",)),
    )(proj, proj, proj, pos, inv, gq, gk)


def _qk_bwd(proj, pos, inv, gq, gk, dqs, dks, dvs, dproj):
    const = lambda a: pl.BlockSpec(a.shape, functools.partial(lambda i, p, nd: (0,) * nd, nd=a.ndim))
    res = lambda g: pl.BlockSpec((DILATIONS[g], TT // DILATIONS[g], ATT_W), lambda i, p: (0, i, 0))
    base = C_QA // ATT_QKV

    def body(t_ref, pos_ref, inv_ref, gq_ref, gk_ref, dq0, dq1, dq2, dk0, dk1, dk2, dv0, dv1, dv2, buf_ref,
             out_ref, dgq_ref, dgk_ref, sc):
        del buf_ref
        part = pl.program_id(1)
        first = pl.program_id(0) == 0

        def gather(drefs):
            for j in range(NCH):
                grp, sub = divmod(j * 128, ATT_W)
                _from_residues(drefs[grp], slice(sub, sub + 128), sc, j, DILATIONS[grp])

        def normed(g_ref, drefs, dg_ref):
            c, sp, sm = _rot_tables(pos_ref, inv_ref)
            gather(drefs)
            dg = jnp.zeros((1, 128), F32)
            for j in range(NCH):
                cols = slice(j * 128, (j + 1) * 128)
                d_rot = sc[j]
                dn = d_rot * c + pltpu.roll(d_rot * sp, 120, 1) + pltpu.roll(d_rot * sm, 8, 1)
                t = t_ref[:, cols]
                r = _pair_norm(t)
                u = dn * g_ref[...]
                out_ref[:, cols] = (r * u - t * (r * r * r) * _pair_mean(u * t)).astype(BF16)
                dg = dg + jnp.sum(dn * t * r, axis=0, keepdims=True)
            dg = dg + pltpu.roll(dg, HD, 1)

            @pl.when(first)
            def _():
                dg_ref[...] = dg

            @pl.when(jnp.logical_not(first))
            def _():
                dg_ref[...] += dg

        @pl.when(part == 0)
        def _():
            normed(gq_ref, (dq0, dq1, dq2), dgq_ref)

        @pl.when(part == 1)
        def _():
            normed(gk_ref, (dk0, dk1, dk2), dgk_ref)

        @pl.when(part == 2)
        def _():
            gather((dv0, dv1, dv2))
            for j in range(NCH):
                out_ref[:, j * 128:(j + 1) * 128] = sc[j].astype(BF16)

    keep = pl.BlockSpec((1, 128), lambda i, p: (0, 0))
    return pl.pallas_call(
        body, name="qk_bwd", grid=(T // TT, 3),
        in_specs=[pl.BlockSpec((TT, ATT_QKV), lambda i, p: (i, base + jnp.minimum(p, 1))),
                  pl.BlockSpec((TT, 1), lambda i, p: (i, 0)), const(inv), const(gq), const(gk)]
        + [res(g) for _ in range(3) for g in range(3)] + [pl.BlockSpec(memory_space=pl.ANY)],
        out_specs=[pl.BlockSpec((TT, ATT_QKV), lambda i, p: (i, base + p)), keep, keep],
        out_shape=[S(dproj.shape, dproj.dtype), S((1, 128), F32), S((1, 128), F32)],
        input_output_aliases={14: 0},
        scratch_shapes=[pltpu.VMEM((NCH, TT, 128), F32)],
        compiler_params=pltpu.CompilerParams(dimension_semantics=("arbitrary", "arbitrary")),
    )(proj, pos, inv, gq, gk, *dqs, *dks, *dvs, dproj)


def _split_heads(t):
    low = lax.broadcasted_iota(jnp.int32, (1, 128), 1) < HD
    zero = jnp.zeros_like(t)
    return jnp.concatenate([jnp.where(low, t, zero), jnp.where(low, zero, t)], axis=0)


def _join_heads(t2):
    low = lax.broadcasted_iota(jnp.int32, (1, 128), 1) < HD
    n = t2.shape[0] // 2
    return jnp.where(low, t2[:n], t2[n:])


def _band_masks():
    row = lax.broadcasted_iota(jnp.int32, (BLK, 2 * BLK), 0)
    col = lax.broadcasted_iota(jnp.int32, (BLK, 2 * BLK), 1) % BLK
    return col <= row, col >= row


def _per_head(col_a, col_b):
    first = lax.broadcasted_iota(jnp.int32, (1, 2 * BLK), 1) < BLK
    return jnp.where(first, col_a, col_b)


def _att_fwd(q, k, v, grp, name):
    dil = DILATIONS[grp]
    nb = T // dil // BLK
    scale = HD ** -0.5

    def body(q_ref, kp_ref, kc_ref, vp_ref, vc_ref, o_ref, lse_ref):
        has_prev = pl.program_id(1) > 0
        m_cur, m_prev = _band_masks()
        m_prev = m_prev & has_prev
        for j in range(ATT_W // 128):
            cols = slice(j * 128, (j + 1) * 128)
            q = q_ref[:, cols]
            s_c = jnp.where(m_cur, _nt(q, _split_heads(kc_ref[:, cols])) * scale, -jnp.inf)
            s_p = jnp.where(m_prev, _nt(q, _split_heads(kp_ref[:, cols])) * scale, -jnp.inf)
            mx = []
            for half in (slice(0, BLK), slice(BLK, 2 * BLK)):
                mx.append(jnp.maximum(jnp.max(s_c[:, half], axis=-1, keepdims=True),
                                      jnp.max(s_p[:, half], axis=-1, keepdims=True)))
            m2 = _per_head(*mx)
            p_c = jnp.exp(s_c - m2)
            p_p = jnp.exp(s_p - m2)
            den = []
            for half in (slice(0, BLK), slice(BLK, 2 * BLK)):
                den.append(jnp.sum(p_c[:, half], axis=-1, keepdims=True) + jnp.sum(p_p[:, half], axis=-1, keepdims=True))
            acc = _nn(p_c.astype(BF16), _split_heads(vc_ref[:, cols])) + _nn(p_p.astype(BF16), _split_heads(vp_ref[:, cols]))
            low = lax.broadcasted_iota(jnp.int32, (1, 128), 1) < HD
            o_ref[:, cols] = acc / jnp.where(low, den[0], den[1])
            lse_ref[:, cols] = jnp.where(low, mx[0] + jnp.log(den[0]), mx[1] + jnp.log(den[1]))

    cur = pl.BlockSpec((None, BLK, ATT_W), lambda r, i: (r, i, 0))
    prev = pl.BlockSpec((None, BLK, ATT_W), lambda r, i: (r, jnp.maximum(i - 1, 0), 0))
    return pl.pallas_call(
        body, name=name, grid=(dil, nb),
        in_specs=[cur, prev, cur, prev, cur],
        out_specs=[cur, cur], out_shape=[_res_shape(grp, F32)] * 2,
        compiler_params=pltpu.CompilerParams(dimension_semantics=("parallel", "arbitrary")),
    )(q, k, k, v, v)


def _att_bwd(q, k, v, datt, att, lse, grp, name):
    dil = DILATIONS[grp]
    nb = T // dil // BLK
    scale = HD ** -0.5

    def body(q0_ref, q1_ref, kp_ref, kc_ref, vp_ref, vc_ref, do0_ref, do1_ref, o0_ref, o1_ref, l0_ref, l1_ref,
             dq_ref, dk_ref, dv_ref):
        i = pl.program_id(1)
        has_prev = i > 0
        has_next = i < nb - 1
        m_cur, m_prev = _band_masks()
        low = lax.broadcasted_iota(jnp.int32, (1, 128), 1) < HD

        def stats(do_ref, o_ref, l_ref, cols):
            prod = do_ref[:, cols] * o_ref[:, cols]
            d_all = jnp.sum(prod, axis=-1, keepdims=True)
            d_low = jnp.sum(jnp.where(low, prod, 0.0), axis=-1, keepdims=True)
            lse_t = l_ref[:, cols]
            return _per_head(d_low, d_all - d_low), _per_head(lse_t[:, 0:1], lse_t[:, HD:HD + 1])

        def pair(q, k2, v2, do, dsum, lse2, mask):
            s = _nt(q, k2) * scale
            p = jnp.where(mask, jnp.exp(s - lse2), 0.0)
            ds = p * (_nt(do, v2) - dsum) * scale
            return p.astype(BF16), ds.astype(BF16)

        for j in range(ATT_W // 128):
            cols = slice(j * 128, (j + 1) * 128)
            q0, q1 = q0_ref[:, cols], q1_ref[:, cols]
            do0, do1 = do0_ref[:, cols].astype(BF16), do1_ref[:, cols].astype(BF16)
            kc2, kp2 = _split_heads(kc_ref[:, cols]), _split_heads(kp_ref[:, cols])
            vc2, vp2 = _split_heads(vc_ref[:, cols]), _split_heads(vp_ref[:, cols])
            d0, l0 = stats(do0_ref, o0_ref, l0_ref, cols)
            d1, l1 = stats(do1_ref, o1_ref, l1_ref, cols)
            p_a, ds_a = pair(q0, kc2, vc2, do0, d0, l0, m_cur)
            _, ds_b = pair(q0, kp2, vp2, do0, d0, l0, m_prev & has_prev)
            p_c, ds_c = pair(q1, kc2, vc2, do1, d1, l1, m_prev & has_next)
            dq_ref[:, cols] = _nn(ds_a, kc2) + _nn(ds_b, kp2)
            dk_ref[:, cols] = _join_heads(_tn(ds_a, q0) + _tn(ds_c, q1))
            dv_ref[:, cols] = _join_heads(_tn(p_a, do0) + _tn(p_c, do1))

    def spec(shift):
        return pl.BlockSpec((None, BLK, ATT_W), lambda r, i: (r, jnp.clip(i + shift, 0, nb - 1), 0))

    here, after, before = spec(0), spec(1), spec(-1)
    return pl.pallas_call(
        body, name=name, grid=(dil, nb),
        in_specs=[here, after, before, here, before, here, here, after, here, after, here, after],
        out_specs=[here] * 3, out_shape=[_res_shape(grp, F32)] * 3,
        compiler_params=pltpu.CompilerParams(dimension_semantics=("parallel", "arbitrary")),
    )(q, q, k, k, v, v, datt, datt, att, att, lse, lse)


def _att_merge(os_, lses, proj):
    nq = ATT_W // 128

    def body(o0, o1, o2, l0, l1, l2, za_ref, att_ref, lse_ref, ain_ref, sc):
        for a, ref in enumerate((o0, o1, o2, l0, l1, l2)):
            for j in range(nq):
                _from_residues(ref, slice(j * 128, (j + 1) * 128), sc, a * nq + j, DILATIONS[a % 3])
        for j in range(nq):
            cols = slice(j * 128, (j + 1) * 128)
            oa, ob, oc = (sc[a * nq + j] for a in range(3))
            la, lb, lc = (sc[(3 + a) * nq + j] for a in range(3))
            m = jnp.maximum(jnp.maximum(la, lb), lc)
            wa, wb, wc = jnp.exp(la - m), jnp.exp(lb - m), jnp.exp(lc - m)
            tot = wa + wb + wc
            att = (wa * oa + wb * ob + wc * oc) / tot
            att_ref[:, cols] = att
            lse_ref[:, cols] = m + jnp.log(tot)
            za = za_ref[:, cols]
            ain_ref[:, cols] = (att * za * _sigmoid(za)).astype(BF16)

    return pl.pallas_call(
        body, name="att_merge", grid=(T // TT,),
        in_specs=[_res_spec(g) for _ in range(2) for g in range(3)] + [_tok_spec(ATT_W, C_ZA // ATT_W)],
        out_specs=[_tok_spec(ATT_W)] * 3,
        out_shape=[S((T, ATT_W), F32), S((T, ATT_W), F32), S((T, ATT_W), BF16)],
        scratch_shapes=[pltpu.VMEM((6 * nq, TT, 128), F32)],
        compiler_params=pltpu.CompilerParams(dimension_semantics=("arbitrary",)),
    )(*os_, *lses, proj)


def _att_gate_bwd(dain, att, lse, proj, dproj):
    nq = ATT_W // 128

    def body(d_ref, att_ref, lse_ref, za_ref, buf_ref, dza_ref, da0, da1, da2, at1, at2, ls1, ls2, sc):
        del buf_ref
        for j in range(nq):
            cols = slice(j * 128, (j + 1) * 128)
            za = za_ref[:, cols]
            sg = _sigmoid(za)
            d = d_ref[:, cols]
            att_ = att_ref[:, cols]
            dza_ref[:, cols] = (d * att_ * sg * (1.0 + za * (1.0 - sg))).astype(BF16)
            sc[j] = d * za * sg
            sc[nq + j] = att_
            sc[2 * nq + j] = lse_ref[:, cols]
        for j in range(nq):
            cols = slice(j * 128, (j + 1) * 128)
            for grp, dst in enumerate((da0, da1, da2)):
                _to_residues(sc, j, dst, DILATIONS[grp], cols)
            for grp, dst in ((1, at1), (2, at2)):
                _to_residues(sc, nq + j, dst, DILATIONS[grp], cols)
            for grp, dst in ((1, ls1), (2, ls2)):
                _to_residues(sc, 2 * nq + j, dst, DILATIONS[grp], cols)

    res = (0, 1, 2, 1, 2, 1, 2)
    return pl.pallas_call(
        body, name="att_gate_bwd", grid=(T // TT,),
        in_specs=[_tok_spec(ATT_W)] * 3 + [_tok_spec(ATT_W, C_ZA // ATT_W), pl.BlockSpec(memory_space=pl.ANY)],
        out_specs=[_tok_spec(ATT_W, C_ZA // ATT_W)] + [_res_spec(g) for g in res],
        out_shape=[S(dproj.shape, dproj.dtype)] + [_res_shape(g, F32) for g in res],
        input_output_aliases={4: 0},
        scratch_shapes=[pltpu.VMEM((3 * nq, TT, 128), F32)],
        compiler_params=pltpu.CompilerParams(dimension_semantics=("arbitrary",)),
    )(dain, att, lse, proj, dproj)


def _split3(v):
    hi = v.astype(BF16)
    r1 = v - hi.astype(F32)
    mid = r1.astype(BF16)
    lo = (r1 - mid.astype(F32)).astype(BF16)
    return hi, mid, lo


def _tri_sum(v, upper):
    n = v.shape[0]
    row = lax.broadcasted_iota(jnp.int32, (n, n), 0)
    col = lax.broadcasted_iota(jnp.int32, (n, n), 1)
    tri = jnp.where(col >= row if upper else col <= row, 1.0, 0.0).astype(BF16)
    hi, mid, lo = _split3(v)
    return _nn(tri, hi) + _nn(tri, mid) + _nn(tri, lo)


def _gla_gates(glr_ref, w2_ref, b_ref):
    logit = _nn(glr_ref[...].astype(BF16), w2_ref[...]) + b_ref[...]
    lg = (jnp.minimum(logit, 0.0) - jnp.log(1.0 + jnp.exp(-jnp.abs(logit)))) * (1.0 / GLA_TAU)
    return logit, _tri_sum(lg, upper=False)


def _gla_head(cum, q_ref, k_ref, h):
    cols = slice(h * GDK, (h + 1) * GDK)
    b = cum[:, cols]
    last = b[GLA_C - 1:GLA_C, :]
    e_pos = jnp.exp(b)
    e_neg = jnp.exp(-b)
    e_end = jnp.exp(last - b)
    qt = q_ref[:, cols] * (GDK ** -0.5) * e_pos
    kt = k_ref[:, cols] * e_neg
    kh = k_ref[:, cols] * e_end
    return b, last, e_pos, e_neg, e_end, qt, kt, kh


def _causal(n):
    return lax.broadcasted_iota(jnp.int32, (n, n), 1) <= lax.broadcasted_iota(jnp.int32, (n, n), 0)


def _gla_fwd(proj, w2p, bg, gn):
    nc = T // GLA_C

    def body(q_ref, k_ref, v_ref, glr_ref, zg_ref, w2_ref, b_ref, gn_ref, o_ref, bin_ref, st_ref, state):
        @pl.when(pl.program_id(0) == 0)
        def _():
            state[...] = jnp.zeros_like(state)

        _, cum = _gla_gates(glr_ref, w2_ref, b_ref)
        for h in range(GH):
            _, last, _, _, _, qt, kt, kh = _gla_head(cum, q_ref, k_ref, h)
            vcols = slice(h * GDV, (h + 1) * GDV)
            st = state[h]
            st_ref[0, h] = st
            v = v_ref[:, vcols].astype(BF16)
            qb = qt.astype(BF16)
            a = jnp.where(_causal(GLA_C), _nt(qb, kt.astype(BF16)), 0.0)
            o = _nt(qb, st.astype(BF16)) + _nn(a.astype(BF16), v)
            state[h] = st * jnp.exp(last) + _tn(v, kh.astype(BF16))
            o_ref[:, vcols] = o
            r = lax.rsqrt(jnp.mean(o * o, axis=-1, keepdims=True) + EPS)
            zg = zg_ref[:, vcols]
            bin_ref[:, vcols] = (o * r * gn_ref[...] * zg * _sigmoid(zg)).astype(BF16)

    row = lambda width, cblk: pl.BlockSpec((GLA_C, width), functools.partial(lambda i, c: (i, c), c=cblk))
    full = lambda a: pl.BlockSpec(a.shape, functools.partial(lambda i, nd: (0,) * nd, nd=a.ndim))
    return pl.pallas_call(
        body, name="gla_fwd", grid=(nc,),
        in_specs=[row(512, C_QG // 512), row(512, C_KG // 512), row(1024, C_VG // 1024), row(GLR_W, C_GLR // GLR_W),
                  row(1024, C_ZG // 1024), full(w2p), full(bg), full(gn)],
        out_specs=[pl.BlockSpec((GLA_C, GH * GDV), lambda i: (i, 0)), pl.BlockSpec((GLA_C, GH * GDV), lambda i: (i, 0)),
                   pl.BlockSpec((1, GH, GDV, GDK), lambda i: (i, 0, 0, 0))],
        out_shape=[S((T, GH * GDV), F32), S((T, GH * GDV), BF16), S((nc, GH, GDV, GDK), F32)],
        scratch_shapes=[pltpu.VMEM((GH, GDV, GDK), F32)],
        compiler_params=pltpu.CompilerParams(dimension_semantics=("arbitrary",)),
    )(proj, proj, proj, proj, proj, w2p, bg, gn)


def _gla_bwd(proj, w2p, bg, gn, o_gla, states, dbin, dproj):
    nc = T // GLA_C

    def body(q_ref, k_ref, v_ref, glr_ref, zg_ref, w2_ref, b_ref, gn_ref, o_ref, st_ref, dbin_ref, buf_ref,
             out_ref, dw2_ref, dbg_ref, dgn_ref, dstate, dlogit):
        del buf_ref
        dq_ref = out_ref.at[:, C_QG:C_KG]
        dk_ref = out_ref.at[:, C_KG:C_VG]
        dv_ref = out_ref.at[:, C_VG:C_ZG]
        dzg_ref = out_ref.at[:, C_ZG:C_GLR]
        dglr_ref = out_ref.at[:, C_GLR:C_GLR + GLR_W]
        first = pl.program_id(0) == 0

        @pl.when(first)
        def _():
            dstate[...] = jnp.zeros_like(dstate)

        logit, cum = _gla_gates(glr_ref, w2_ref, b_ref)
        is_last = lax.broadcasted_iota(jnp.int32, (GLA_C, 1), 0) == GLA_C - 1
        dgn = jnp.zeros((1, GDV), F32)
        for h in range(GH):
            _, last, e_pos, e_neg, e_end, qt, kt, kh = _gla_head(cum, q_ref, k_ref, h)
            cols = slice(h * GDK, (h + 1) * GDK)
            vcols = slice(h * GDV, (h + 1) * GDV)
            o = o_ref[:, vcols]
            r = lax.rsqrt(jnp.mean(o * o, axis=-1, keepdims=True) + EPS)
            zg = zg_ref[:, vcols]
            sg = _sigmoid(zg)
            db_ = dbin_ref[:, vcols]
            dlin = db_ * zg * sg
            dzg_ref[:, vcols] = (db_ * (o * r * gn_ref[...]) * sg * (1.0 + zg * (1.0 - sg))).astype(BF16)
            u = dlin * gn_ref[...]
            do = (r * u - o * (r * r * r) * jnp.mean(u * o, axis=-1, keepdims=True)).astype(BF16)
            dgn = dgn + jnp.sum(dlin * o * r, axis=0, keepdims=True)
            st = st_ref[0, h]
            dst = dstate[h]
            v = v_ref[:, vcols].astype(BF16)
            qb, kb, khb = qt.astype(BF16), kt.astype(BF16), kh.astype(BF16)
            dstb = dst.astype(BF16)
            causal = _causal(GLA_C)
            a = jnp.where(causal, _nt(qb, kb), 0.0).astype(BF16)
            da = jnp.where(causal, _nt(do, v), 0.0).astype(BF16)
            dqt = _nn(do, st.astype(BF16)) + _nn(da, kb)
            dkt = _tn(da, qb)
            dkh = _nn(v, dstb)
            dv_ref[:, vcols] = (_tn(a, do) + _nt(khb, dstb)).astype(BF16)
            lam = jnp.exp(last)
            dlam = jnp.sum(dst * st, axis=0, keepdims=True)
            dstate[h] = dst * lam + _tn(do, qb)
            dq_ref[:, cols] = (dqt * e_pos * (GDK ** -0.5)).astype(BF16)
            dk_ref[:, cols] = (dkt * e_neg + dkh * e_end).astype(BF16)
            dkh_kh = dkh * kh
            dcum = dqt * qt - dkt * kt - dkh_kh
            dlast = jnp.sum(dkh_kh, axis=0, keepdims=True) + dlam * lam
            dcum = jnp.where(is_last, dcum + dlast, dcum)
            dlg = _tri_sum(dcum, upper=True)
            dlogit[:, cols] = dlg * (1.0 / GLA_TAU) * (1.0 - _sigmoid(logit[:, cols]))

        dl = dlogit[...]
        dlb = dl.astype(BF16)
        dglr_ref[...] = _nt(dlb, w2_ref[...]).astype(BF16)
        dw2 = _tn(glr_ref[...].astype(BF16), dlb)
        dbg = jnp.sum(dl, axis=0, keepdims=True)

        @pl.when(first)
        def _():
            dw2_ref[...] = dw2
            dbg_ref[...] = dbg
            dgn_ref[...] = dgn

        @pl.when(jnp.logical_not(first))
        def _():
            dw2_ref[...] += dw2
            dbg_ref[...] += dbg
            dgn_ref[...] += dgn

    rev = lambda i: nc - 1 - i
    row = lambda width, cblk: pl.BlockSpec((GLA_C, width), functools.partial(lambda i, c: (rev(i), c), c=cblk))
    full = lambda a: pl.BlockSpec(a.shape, functools.partial(lambda i, nd: (0,) * nd, nd=a.ndim))
    keep = lambda shape: pl.BlockSpec(shape, functools.partial(lambda i, nd: (0,) * nd, nd=len(shape)))
    return pl.pallas_call(
        body, name="gla_bwd", grid=(nc,),
        in_specs=[row(512, C_QG // 512), row(512, C_KG // 512), row(1024, C_VG // 1024), row(GLR_W, C_GLR // GLR_W),
                  row(1024, C_ZG // 1024), full(w2p), full(bg), full(gn), row(GH * GDV, 0),
                  pl.BlockSpec((1, GH, GDV, GDK), lambda i: (rev(i), 0, 0, 0)), row(GH * GDV, 0),
                  pl.BlockSpec(memory_space=pl.ANY)],
        out_specs=[row(GLA_GROUP_W, 0), keep((GLR_W, 512)), keep((1, 512)), keep((1, GDV))],
        out_shape=[S(dproj.shape, dproj.dtype), S((GLR_W, 512), F32), S((1, 512), F32), S((1, GDV), F32)],
        input_output_aliases={11: 0},
        scratch_shapes=[pltpu.VMEM((GH, GDV, GDK), F32), pltpu.VMEM((GLA_C, GH * GDK), F32)],
        compiler_params=pltpu.CompilerParams(dimension_semantics=("arbitrary",)),
    )(proj, proj, proj, proj, proj, w2p, bg, gn, o_gla, states, dbin, dproj)


def _merge_fwd(ya, yb, proj):
    def body(ya_ref, yb_ref, ga_ref, gb_ref, y_ref):
        y_ref[...] = (_sigmoid(ga_ref[...]) * ya_ref[...] + _sigmoid(gb_ref[...]) * yb_ref[...]).astype(BF16)

    return _rowcall(body, "merge_fwd", 512,
                    [_rows(ya), _rows(yb), _rows(proj, D, C_GA // D), _rows(proj, D, C_GB // D)],
                    [("rows", D, BF16)])[0]


def _merge_bwd(dy, ya, yb, proj):
    tt = 512

    def body(dy_ref, ya_ref, yb_ref, g_ref, dg_ref, dya_ref, dyb_ref):
        dy_ = dy_ref[...]
        sa, sb = _sigmoid(g_ref[:, :D]), _sigmoid(g_ref[:, D:])
        dg_ref[:, :D] = (dy_ * ya_ref[...] * sa * (1.0 - sa)).astype(BF16)
        dg_ref[:, D:] = (dy_ * yb_ref[...] * sb * (1.0 - sb)).astype(BF16)
        dya_ref[...] = (dy_ * sa).astype(BF16)
        dyb_ref[...] = (dy_ * sb).astype(BF16)

    tok = pl.BlockSpec((tt, D), lambda i: (i, 0))
    gates = pl.BlockSpec((tt, 2 * D), lambda i: (i, C_GA // (2 * D)))
    return pl.pallas_call(
        body, name="merge_bwd", grid=(T // tt,),
        in_specs=[tok, tok, tok, gates], out_specs=[gates, tok, tok],
        out_shape=[S((T, NCOL), BF16), S((T, D), BF16), S((T, D), BF16)],
        compiler_params=pltpu.CompilerParams(dimension_semantics=("arbitrary",)),
    )(dy, ya, yb, proj)


def _loss_head(x1, e, u, target):
    def body(x1_ref, e_ref, u_ref, t_ref, loss_ref, dout_ref, de_ref, du_ref, acc):
        first = pl.program_id(0) == 0
        pg = _sigmoid(u_ref[...])
        e_ = e_ref[...]
        diff = x1_ref[...] + e_ * pg - t_ref[...]
        part = jnp.sum(diff * diff, axis=0, keepdims=True)

        @pl.when(first)
        def _():
            acc[...] = part

        @pl.when(jnp.logical_not(first))
        def _():
            acc[...] += part

        dout = diff * (1.0 / D)
        dout_ref[...] = dout
        de_ref[...] = (dout * pg).astype(BF16)
        du_ref[...] = (dout * e_ * pg * (1.0 - pg)).astype(BF16)
        loss_ref[...] = jnp.zeros((1, 128), F32) + jnp.sum(acc[...], axis=-1, keepdims=True) * (0.5 / D)

    return _rowcall(body, "loss_head", 256, [_rows(x1), _rows(e), _rows(u), _rows(target)],
                    [("acc", (1, 128), F32), ("rows", D, F32), ("rows", D, BF16), ("rows", D, BF16)],
                    scratch=[pltpu.VMEM((1, D), F32)])


def _peer(k):
    x, y, c = lax.axis_index("x"), lax.axis_index("y"), lax.axis_index("c")
    return (x ^ ((k >> 2) & 1), y ^ ((k >> 1) & 1), c ^ (k & 1))


def _my_index():
    return 4 * lax.axis_index("x") + 2 * lax.axis_index("y") + lax.axis_index("c")


def _peer_index(k):
    px, py, pc = _peer(k)
    return 4 * px + 2 * py + pc


def _pairwise_plan(src_of, dst_of, landed_of, own_src, own_dst):
    def plan(ins, outs, send, recv, local):
        n = len(ins)

        def own():
            return [pltpu.make_async_copy(own_src(ins[a]), own_dst(outs[a]), local.at[a]) for a in range(n)]

        def remote(k, a, src, dst):
            return pltpu.make_async_remote_copy(src_ref=src, dst_ref=dst, send_sem=send.at[k - 1, a],
                                                recv_sem=recv.at[k - 1, a], device_id=_peer(k), device_id_type=MESH)

        def sent():
            return [remote(k, a, src_of(ins[a], k), dst_of(outs[a])) for k in range(1, NDEV) for a in range(n)]

        def start():
            for cp in own() + sent():
                cp.start()

        def finish():
            for k in range(1, NDEV):
                for a in range(n):
                    remote(k, a, own_src(ins[a]), landed_of(outs[a], k)).wait_recv()
            for cp in sent():
                cp.wait_send()
            for cp in own():
                cp.wait()

        return start, finish

    return plan


def _pairwise_sems(n):
    return [pltpu.SemaphoreType.DMA((NDEV - 1, n)), pltpu.SemaphoreType.DMA((NDEV - 1, n)),
            pltpu.SemaphoreType.DMA((n,))]


def _gather_side(arrs):
    plan = _pairwise_plan(src_of=lambda i, k: i, dst_of=lambda o: o.at[_my_index()],
                          landed_of=lambda o, k: o.at[_peer_index(k)],
                          own_src=lambda i: i, own_dst=lambda o: o.at[_my_index()])
    return dict(arrs=arrs, out_shape=[S((NDEV,) + a.shape, a.dtype) for a in arrs],
                scratch=_pairwise_sems(len(arrs)), plan=plan)


def _exchange_side(arrs):
    plan = _pairwise_plan(src_of=lambda i, k: i.at[_peer_index(k)], dst_of=lambda o: o.at[_my_index()],
                          landed_of=lambda o, k: o.at[_peer_index(k)],
                          own_src=lambda i: i.at[_my_index()], own_dst=lambda o: o.at[_my_index()])
    return dict(arrs=arrs, out_shape=[S(a.shape, a.dtype) for a in arrs], scratch=_pairwise_sems(len(arrs)), plan=plan)


def _comm_call(side, name):
    n = len(side["arrs"])

    def body(*refs):
        start, finish = side["plan"](refs[:n], refs[n:2 * n], *refs[2 * n:])
        start()
        finish()

    hbm = pl.BlockSpec(memory_space=pl.ANY)
    return pl.pallas_call(body, name=name, in_specs=[hbm] * n, out_specs=[hbm] * n, out_shape=side["out_shape"],
                          scratch_shapes=side["scratch"])(*side["arrs"])


def _all_gather_by_chip(arrs, name):
    n = len(arrs)

    def body(*refs):
        ins, outs = refs[:n], refs[n:2 * n]
        send, recv, local = refs[2 * n:]
        x, y, c = lax.axis_index("x"), lax.axis_index("y"), lax.axis_index("c")
        me, sibling = (x, y, c), (x, y, 1 - c)
        chips = [(1 - x, y), (x, 1 - y), (1 - x, 1 - y)]

        def copy(k, a, block, to, src=None):
            px, py, pc = block
            slot = outs[a].at[4 * px + 2 * py + pc]
            return pltpu.make_async_remote_copy(
                src_ref=slot if src is None else src, dst_ref=slot, send_sem=send.at[k, a], recv_sem=recv.at[k, a],
                device_id=to, device_id_type=MESH)

        mine = [pltpu.make_async_copy(ins[a], outs[a].at[4 * x + 2 * y + c], local.at[a]) for a in range(n)]
        first = []
        for a in range(n):
            first.append(copy(0, a, me, sibling, src=ins[a]))
            first += [copy(1 + j, a, me, (*chip, c), src=ins[a]) for j, chip in enumerate(chips)]
        for cp in mine + first:
            cp.start()
        passed = []
        for j, chip in enumerate(chips):
            for a in range(n):
                copy(1 + j, a, (*chip, c), me).wait_recv()
                passed.append(copy(4 + j, a, (*chip, c), sibling))
                passed[-1].start()
        for a in range(n):
            copy(0, a, sibling, me).wait_recv()
        for j, chip in enumerate(chips):
            for a in range(n):
                copy(4 + j, a, (*chip, 1 - c), me).wait_recv()
        for cp in first + passed:
            cp.wait_send()
        for cp in mine:
            cp.wait()

    hbm = pl.BlockSpec(memory_space=pl.ANY)
    return pl.pallas_call(
        body, name=name, in_specs=[hbm] * n, out_specs=[hbm] * n,
        out_shape=[S((NDEV,) + a.shape, a.dtype) for a in arrs],
        scratch_shapes=[pltpu.SemaphoreType.DMA((NDEV - 1, n)), pltpu.SemaphoreType.DMA((NDEV - 1, n)),
                        pltpu.SemaphoreType.DMA((n,))],
    )(*arrs)


NCHIP = 4


def _exchange_sibling(arrs, name):
    n = len(arrs)

    def body(*refs):
        ins, outs = refs[:n], refs[n:2 * n]
        send, recv = refs[2 * n:]
        x, y, c = lax.axis_index("x"), lax.axis_index("y"), lax.axis_index("c")
        copies = []
        for q in range(NCHIP):
            for a in range(n):
                copies.append(pltpu.make_async_remote_copy(
                    src_ref=ins[a].at[2 * q + (1 - c)], dst_ref=outs[a].at[q], send_sem=send.at[q, a],
                    recv_sem=recv.at[q, a], device_id=(x, y, 1 - c), device_id_type=MESH))
        for cp in copies:
            cp.start()
        for cp in copies:
            cp.wait_recv()
        for cp in copies:
            cp.wait_send()

    hbm = pl.BlockSpec(memory_space=pl.ANY)
    return pl.pallas_call(
        body, name=name, in_specs=[hbm] * n, out_specs=[hbm] * n,
        out_shape=[S((NCHIP,) + a.shape[1:], a.dtype) for a in arrs],
        scratch_shapes=[pltpu.SemaphoreType.DMA((NCHIP, n)), pltpu.SemaphoreType.DMA((NCHIP, n))],
    )(*arrs)


def _pair_add(mine, got, core, name):
    _, rows, cols = mine.shape
    tr = rows if rows * cols <= 1 << 19 else 128
    assert rows % tr == 0

    def body(core_ref, a_ref, b_ref, o_ref):
        o_ref[...] = (a_ref[...].astype(F32) + b_ref[...].astype(F32)).astype(BF16)

    return pl.pallas_call(
        body, name=name,
        grid_spec=pltpu.PrefetchScalarGridSpec(
            num_scalar_prefetch=1, grid=(NCHIP, rows // tr),
            in_specs=[pl.BlockSpec((None, tr, cols), lambda q, i, core_ref: (2 * q + core_ref[0], i, 0)),
                      pl.BlockSpec((None, tr, cols), lambda q, i, core_ref: (q, i, 0))],
            out_specs=pl.BlockSpec((None, tr, cols), lambda q, i, core_ref: (q, i, 0))),
        out_shape=S((NCHIP, rows, cols), BF16),
    )(core, mine, got)


def _chips_side(arrs):
    def plan(ins, outs, send, recv, local):
        n = len(ins)

        def places():
            x, y, c = lax.axis_index("x"), lax.axis_index("y"), lax.axis_index("c")
            return 2 * x + y, c, [(1 - x, y), (x, 1 - y), (1 - x, 1 - y)]

        def own():
            here, _, _ = places()
            return [pltpu.make_async_copy(ins[a].at[here], outs[a].at[here], local.at[a]) for a in range(n)]

        def remote(j, a, src_slot, dst_slot):
            _, c, chips = places()
            cx, cy = chips[j]
            return pltpu.make_async_remote_copy(
                src_ref=ins[a].at[src_slot], dst_ref=outs[a].at[dst_slot], send_sem=send.at[j, a],
                recv_sem=recv.at[j, a], device_id=(cx, cy, c), device_id_type=MESH)

        def sent():
            here, _, chips = places()
            return [remote(j, a, 2 * cx + cy, here) for j, (cx, cy) in enumerate(chips) for a in range(n)]

        def start():
            for cp in own() + sent():
                cp.start()

        def finish():
            here, _, chips = places()
            for j, (cx, cy) in enumerate(chips):
                for a in range(n):
                    remote(j, a, here, 2 * cx + cy).wait_recv()
            for cp in sent():
                cp.wait_send()
            for cp in own():
                cp.wait()

        return start, finish

    n = len(arrs)
    return dict(arrs=arrs, out_shape=[S(a.shape, a.dtype) for a in arrs],
                scratch=[pltpu.SemaphoreType.DMA((NCHIP - 1, n)), pltpu.SemaphoreType.DMA((NCHIP - 1, n)),
                         pltpu.SemaphoreType.DMA((n,))], plan=plan)


def _adamw(parts, w, m, v, name, tr):
    rows, cols = w.shape
    assert rows % tr == 0
    c1 = 1.0 - ADAM_B1 ** ADAM_STEP
    c2 = 1.0 - ADAM_B2 ** ADAM_STEP

    nparts = parts.shape[0]

    def body(p_ref, w_ref, m_ref, v_ref, g_ref, d_ref, mo_ref, vo_ref):
        g = p_ref[0].astype(F32)
        for s in range(1, nparts):
            g = g + p_ref[s].astype(F32)
        m_new = ADAM_B1 * m_ref[...] + (1.0 - ADAM_B1) * g
        v_new = ADAM_B2 * v_ref[...] + (1.0 - ADAM_B2) * (g * g)
        g_ref[...] = g
        mo_ref[...] = m_new
        vo_ref[...] = v_new
        d_ref[...] = -ADAM_LR * ((m_new / c1) / (jnp.sqrt(v_new / c2) + ADAM_EPS) + ADAM_WD * w_ref[...])

    blk = pl.BlockSpec((tr, cols), lambda i: (i, 0))
    return pl.pallas_call(
        body, name=name, grid=(rows // tr,),
        in_specs=[pl.BlockSpec((nparts, tr, cols), lambda i: (0, i, 0)), blk, blk, blk],
        out_specs=[blk] * 4, out_shape=[S((rows, cols), F32)] * 4,
        compiler_params=pltpu.CompilerParams(dimension_semantics=("parallel",)),
    )(parts, w, m, v)


def _to_aligned(w):
    pad = jnp.zeros((w.shape[0], GLR_W - GLR_N), w.dtype)
    return jnp.concatenate([w[:, O_QG:O_GLR], w[:, O_ZG:O_GA], w[:, O_GLR:O_ZG], pad, w[:, O_ZA:O_QG],
                            w[:, O_GA:O_END], w[:, O_QA:O_ZA]], axis=1)


def _from_aligned(w):
    return jnp.concatenate([w[:, C_QA:], w[:, C_ZA:C_GA], w[:, C_QG:C_ZG], w[:, C_GLR:C_GLR + GLR_N], w[:, C_ZG:C_GLR],
                            w[:, C_GA:C_QA]], axis=1)


def _col_blocks(w, width):
    return w.reshape(w.shape[0], NDEV, width).transpose(1, 0, 2)


def _from_col_blocks(w):
    return w.transpose(1, 0, 2).reshape(w.shape[1], NDEV * w.shape[2])


SMALL = (("norm_g", D), ("qk_norm_q", HD), ("qk_norm_k", HD), ("gla_gate_b", 512), ("gla_norm_g", GDV),
         ("ple_norm_g", D))
SMALL_PAD = 4096


def _local_step(x2, p2, pos, tgt, norm_g, qk_norm_q, qk_norm_k, gla_gate_b, gla_norm_g, ple_norm_g, w_al,
                weights=None, proj_side=None, unpack=None, dw_side_of=None, dh_side_of=None):
    half = ROT_DIM // 2
    inv8 = jnp.power(jnp.float32(ROPE_THETA), -jnp.arange(half, dtype=F32) * 2.0 / ROT_DIM)
    inv = jnp.tile(jnp.concatenate([inv8, inv8, jnp.zeros((HD - ROT_DIM,), F32)]), 2).reshape(1, 128)
    gq = jnp.tile(qk_norm_q, (1, 2))
    gk = jnp.tile(qk_norm_k, (1, 2))

    h = _rms_fwd(x2, norm_g, "rms1_fwd")
    if proj_side is None:
        proj = _mm(h, w_al, mode="nn", name="proj", tm=1024, tn=1536, tk=D)
    else:
        proj, got = _mm(h, w_al, mode="nn", name="proj", tm=1024, tn=1536, tk=D, side=proj_side)
        weights = unpack(got)
    w2p, w_att_f, w_gla_f, w_out_f, w_pg_f, w_ple_f = weights
    qkv = _qk_prep(proj, pos, inv, gq, gk)
    fwd = [_att_fwd(qkv[g], qkv[3 + g], qkv[6 + g], g, f"att_fwd{g}") for g in range(3)]
    att, lse, ain = _att_merge([f[0] for f in fwd], [f[1] for f in fwd], proj)
    o_gla, bin_, states = _gla_fwd(proj, w2p, gla_gate_b, gla_norm_g)
    ya = _mm(ain, w_att_f, mode="nn", name="ya", tm=1024, tn=D, tk=512)
    yb = _mm(bin_, w_gla_f, mode="nn", name="yb", tm=1024, tn=D, tk=D)
    y = _merge_fwd(ya, yb, proj)
    x1 = _mm(y, w_out_f, mode="nn", name="x1", tm=1024, tn=D, tk=D, res=x2)
    n2 = _rms_fwd(x1, ple_norm_g, "rms2_fwd")
    u = _mm(n2, w_pg_f, mode="nn", name="ple_u", tm=1024, tn=D, tk=D)
    e = _mm(p2, w_ple_f, mode="nn", name="ple_e", tm=1024, tn=D, tk=PLE)
    loss_v, dout, de, du = _loss_head(x1, e, u, tgt)

    dw_ple = _mm(p2, de, mode="tn", name="dw_ple", tm=PLE, tn=D, tk=512)
    dw_pg = _mm(n2, du, mode="tn", name="dw_pg", tm=D, tn=D, tk=512)
    dn2 = _mm(du, w_pg_f, mode="nt", name="dn2", tm=1024, tn=D, tk=D)
    dx1, dx1b, dg_ple = _rms_bwd(dn2, x1, ple_norm_g, dout, "rms2_bwd")
    dw_out = _mm(y, dx1b, mode="tn", name="dw_out", tm=D, tn=D, tk=512)
    dy = _mm(dx1b, w_out_f, mode="nt", name="dy", tm=1024, tn=D, tk=D)
    dproj, dya, dyb = _merge_bwd(dy, ya, yb, proj)
    dw_att = _mm(ain, dya, mode="tn", name="dw_att", tm=512, tn=D, tk=512)
    dain = _mm(dya, w_att_f, mode="nt", name="dain", tm=1024, tn=512, tk=D)
    dw_gla = _mm(bin_, dyb, mode="tn", name="dw_gla", tm=D, tn=D, tk=512)
    dbin = _mm(dyb, w_gla_f, mode="nt", name="dbin", tm=1024, tn=D, tk=D)
    dproj, da0, da1, da2, at1, at2, ls1, ls2 = _att_gate_bwd(dain, att, lse, proj, dproj)
    datts, atts, lses = (da0, da1, da2), (att[None], at1, at2), (lse[None], ls1, ls2)
    dproj, dw2, dbg, dgn = _gla_bwd(proj, w2p, gla_gate_b, gla_norm_g, o_gla, states, dbin, dproj)
    bwd = [_att_bwd(qkv[g], qkv[3 + g], qkv[6 + g], datts[g], atts[g], lses[g], g, f"att_bwd{g}") for g in range(3)]
    dproj, dgq, dgk = _qk_bwd(proj, pos, inv, gq, gk, [b[0] for b in bwd], [b[1] for b in bwd],
                              [b[2] for b in bwd], dproj)
    out = dict(loss=loss_v, dw2=dw2, dw_att=dw_att, dw_gla=dw_gla, dw_out=dw_out, dw_pg=dw_pg, dw_ple=dw_ple,
               dgq=dgq, dgk=dgk, dbg=dbg, dgn=dgn, dg_ple=dg_ple)
    if dw_side_of is None:
        dw_al = _mm(h, dproj, mode="tn", name="dw_in", tm=D, tn=1536, tk=2048)
    else:
        dw_al, out["dw_side"] = _mm(h, dproj, mode="tn", name="dw_in", tm=D, tn=1536, tk=2048, side=dw_side_of(out))
    if dh_side_of is None:
        dh = _mm(dproj, w_al, mode="nt", name="dh", tm=1024, tn=D, tk=3584)
    else:
        dh, out["dh_side"] = _mm(dproj, w_al, mode="nt", name="dh", tm=1024, tn=D, tk=3584, side=dh_side_of(dw_al))
    grad_x, _, dg_norm = _rms_bwd(dh, x2, norm_g, dx1, "rms1_bwd")
    out.update(grad_x=grad_x, dw_al=dw_al, dg_norm=dg_norm)
    return out


def kernel(x, p, positions, norm_g, w_in, qk_norm_q, qk_norm_k, gla_gate_w2, gla_gate_b, gla_norm_g, w_att_proj, w_gla_proj, w_out, ple_norm_g, w_ple_gate, w_ple, loss_target, m_norm_g, m_w_in, m_qk_norm_q, m_qk_norm_k, m_gla_gate_w2, m_gla_gate_b, m_gla_norm_g, m_w_att_proj, m_w_gla_proj, m_w_out, m_ple_norm_g, m_w_ple_gate, m_w_ple, v_norm_g, v_w_in, v_qk_norm_q, v_qk_norm_k, v_gla_gate_w2, v_gla_gate_b, v_gla_norm_g, v_w_att_proj, v_w_gla_proj, v_w_out, v_ple_norm_g, v_w_ple_gate, v_w_ple):
    x2, p2, tgt = x[0], p[0, 0], loss_target[0]
    pos = positions.astype(F32).reshape(T, 1)

    rows3 = jnp.stack([w_gla_proj[0], w_out[0], w_ple_gate[0]]).astype(BF16)
    cols3 = jnp.concatenate([w_att_proj[0], w_ple[0], jnp.pad(gla_gate_w2[0], ((0, 0), (0, 64)))], axis=0).astype(BF16)
    (g_in,) = _all_gather_by_chip([w_in[0].astype(BF16)], "gather_w_in")
    w_al = _to_aligned(_from_col_blocks(g_in))

    def unpack(got):
        g_rows, g_cols = got
        w2_f = _from_col_blocks(g_cols[:, 768:784, :64])
        return (jnp.pad(w2_f, ((0, GLR_W - GLR_N), (0, 0))), _from_col_blocks(g_cols[:, :512]),
                g_rows[:, 0].reshape(D, D), g_rows[:, 1].reshape(D, D), g_rows[:, 2].reshape(D, D),
                _from_col_blocks(g_cols[:, 512:768]))

    def dw_side_of(g):
        s_rows = jnp.concatenate([g[k].reshape(NDEV, 128, D) for k in ("dw_gla", "dw_out", "dw_pg")], axis=1)
        s_cols = jnp.concatenate([_col_blocks(g["dw_att"], 128), _col_blocks(g["dw_ple"], 128),
                                  jnp.pad(_col_blocks(g["dw2"][:GLR_N], 64), ((0, 0), (0, 0), (0, 64)))], axis=1)
        return _exchange_side([s_rows.astype(BF16), s_cols.astype(BF16)])

    def dh_side_of(dw_al):
        s_in = _col_blocks(_from_aligned(dw_al), W_IN_SHARD).astype(BF16)
        (from_sibling,) = _exchange_sibling([s_in], "exchange_sibling")
        core = lax.axis_index("c").astype(jnp.int32).reshape(1)
        return _chips_side([_pair_add(s_in, from_sibling, core, "pair_add")])

    loc = _local_step(x2, p2, pos, tgt, norm_g, qk_norm_q, qk_norm_k, gla_gate_b, gla_norm_g, ple_norm_g, w_al,
                      proj_side=_gather_side([rows3, cols3]), unpack=unpack, dw_side_of=dw_side_of,
                      dh_side_of=dh_side_of)
    loss_v, grad_x = loc["loss"], loc["grad_x"]
    dg_norm, dgq, dgk, dbg, dgn, dg_ple = (loc[k] for k in ("dg_norm", "dgq", "dgk", "dbg", "dgn", "dg_ple"))
    r_rows, r_cols = loc["dw_side"]
    (r_in,) = loc["dh_side"]

    small = jnp.concatenate([dg_norm[0], dgq[0, :HD], dgk[0, :HD], dbg[0], dgn[0], dg_ple[0]])
    small = jnp.pad(small, (0, SMALL_PAD - small.shape[0])).reshape(1, 8, SMALL_PAD // 8)
    (r_small,) = _comm_call(_gather_side([small]), "gather_small")

    outs = {}

    def adam(nm, parts, w, m, v, tr):
        outs[nm] = _adamw(parts, w, m, v, "adam_" + nm, tr)

    adam("w_in", r_in, w_in[0], m_w_in[0], v_w_in[0], 128)
    adam("w_gla_proj", r_rows[:, :128], w_gla_proj[0], m_w_gla_proj[0], v_w_gla_proj[0], 128)
    adam("w_out", r_rows[:, 128:256], w_out[0], m_w_out[0], v_w_out[0], 128)
    adam("w_ple_gate", r_rows[:, 256:], w_ple_gate[0], m_w_ple_gate[0], v_w_ple_gate[0], 128)
    adam("w_att_proj", r_cols[:, :512], w_att_proj[0], m_w_att_proj[0], v_w_att_proj[0], 512)
    adam("w_ple", r_cols[:, 512:768], w_ple[0], m_w_ple[0], v_w_ple[0], 256)
    adam("gla_gate_w2", r_cols[:, 768:784, :64], gla_gate_w2[0], m_gla_gate_w2[0], v_gla_gate_w2[0], 16)
    given = dict(norm_g=(norm_g, m_norm_g, v_norm_g), qk_norm_q=(qk_norm_q, m_qk_norm_q, v_qk_norm_q),
                 qk_norm_k=(qk_norm_k, m_qk_norm_k, v_qk_norm_k), gla_gate_b=(gla_gate_b, m_gla_gate_b, v_gla_gate_b),
                 gla_norm_g=(gla_norm_g, m_gla_norm_g, v_gla_norm_g), ple_norm_g=(ple_norm_g, m_ple_norm_g, v_ple_norm_g))

    def pack(i):
        flat = jnp.concatenate([given[nm][i][0] for nm, _ in SMALL])
        return jnp.pad(flat, (0, SMALL_PAD - flat.shape[0])).reshape(8, SMALL_PAD // 8)

    sm = _adamw(r_small.reshape(NDEV, 8, SMALL_PAD // 8), pack(0), pack(1), pack(2), "adam_small", 8)
    off = 0
    for nm, width in SMALL:
        outs[nm] = [o.reshape(-1)[off:off + width] for o in sm]
        off += width

    loss = lax.psum(loss_v[0, 0], ("x", "y", "c"))
    order = ["norm_g", "w_in", "qk_norm_q", "qk_norm_k", "gla_gate_w2", "gla_gate_b", "gla_norm_g", "w_att_proj",
             "w_gla_proj", "w_out", "ple_norm_g", "w_ple_gate", "w_ple"]
    result = [loss, grad_x[None]]
    for i in range(4):
        result += [outs[nm][i][None] for nm in order]
    return tuple(result)
```

```python
import functools

import jax
import jax.numpy as jnp
from jax import lax
from jax.experimental import pallas as pl
from jax.experimental.pallas import tpu as pltpu

F32 = jnp.float32
BF16 = jnp.bfloat16
S = jax.ShapeDtypeStruct

T = 4096
D = 1024
NDEV = 8
HD = 64
ATT_W = 512
ATT_QKV = 1536
DILATIONS = (1, 4, 16)
BLK = 128
GH, GDK, GDV = 4, 128, 256
GLA_C = 128
PLE = 256
EPS = 1e-6
ROT_DIM = 16
ROPE_THETA = 500000.0
GLA_TAU = 16.0
W_IN_COLS = 10256
W_IN_SHARD = 1282

C_QG, C_KG, C_VG, C_ZG, C_GLR, C_ZA, C_GA, C_GB, C_QA, C_KA, C_VA = (
    0, 512, 1024, 2048, 3072, 3584, 4096, 5120, 6144, 7680, 9216)
GLA_GROUP_W = 3584
GLR_W = 512
NCOL = 10752
GLR_N = 16
O_QA, O_ZA, O_QG, O_GLR, O_ZG, O_GA, O_END = 0, 4608, 5120, 7168, 7184, 8208, 10256

ADAM_LR, ADAM_B1, ADAM_B2, ADAM_EPS, ADAM_WD, ADAM_STEP = 0.001, 0.9, 0.999, 1e-08, 0.01, 10

MESH = pl.DeviceIdType.MESH


def _sigmoid(z):
    return 1.0 / (1.0 + jnp.exp(-z))


def _dot(a, b, dims):
    return lax.dot_general(a, b, (dims, ((), ())), preferred_element_type=F32)


def _nn(a, b):
    return _dot(a, b, ((1,), (0,)))


def _nt(a, b):
    return _dot(a, b, ((1,), (1,)))


def _tn(a, b):
    return _dot(a, b, ((0,), (0,)))


def _mm(a, b, *, mode, name, tm, tn, tk, out_dtype=F32, res=None, side=None):
    if mode == "nn":
        (m, k), n = a.shape, b.shape[1]
        a_spec = pl.BlockSpec((tm, tk), lambda i, j, l: (i, l))
        b_spec = pl.BlockSpec((tk, tn), lambda i, j, l: (l, j))
        dot = _nn
    elif mode == "nt":
        (m, k), n = a.shape, b.shape[0]
        a_spec = pl.BlockSpec((tm, tk), lambda i, j, l: (i, l))
        b_spec = pl.BlockSpec((tn, tk), lambda i, j, l: (j, l))
        dot = _nt
    else:
        (k, m), n = a.shape, b.shape[1]
        a_spec = pl.BlockSpec((tk, tm), lambda i, j, l: (l, i))
        b_spec = pl.BlockSpec((tk, tn), lambda i, j, l: (l, j))
        dot = _tn
    assert m % tm == 0 and n % tn == 0 and k % tk == 0, (name, m, n, k)
    grid = (m // tm, n // tn, k // tk)
    nk = grid[2]
    o_spec = pl.BlockSpec((tm, tn), lambda i, j, l: (i, j))
    in_specs = [a_spec, b_spec]
    args = [a, b]
    if res is not None:
        in_specs.append(o_spec)
        args.append(res)
    n_in = len(args)
    n_side = 0 if side is None else len(side["arrs"])
    hbm = pl.BlockSpec(memory_space=pl.ANY)

    def body(*refs):
        a_ref, b_ref = refs[0], refs[1]
        r_ref = refs[2] if res is not None else None
        o_ref = refs[n_in + n_side]
        scratch = refs[n_in + 2 * n_side + 1:]
        if side is not None:
            start, finish_side = side["plan"](refs[n_in:n_in + n_side], refs[n_in + n_side + 1:n_in + 2 * n_side + 1],
                                              *scratch[1 if nk > 1 else 0:])
            ids = [pl.program_id(d) for d in range(3)]

            @pl.when((ids[0] == 0) & (ids[1] == 0) & (ids[2] == 0))
            def _():
                start()

        part = dot(a_ref[...].astype(BF16), b_ref[...].astype(BF16))

        def finish(val):
            if r_ref is not None:
                val = val + r_ref[...]
            o_ref[...] = val.astype(out_dtype)

        if nk == 1:
            finish(part)
        else:
            acc = scratch[0]
            l = pl.program_id(2)

            @pl.when(l == 0)
            def _():
                acc[...] = part

            @pl.when(l > 0)
            def _():
                acc[...] += part

            @pl.when(l == nk - 1)
            def _():
                finish(acc[...])

        if side is not None:
            @pl.when((ids[0] == grid[0] - 1) & (ids[1] == grid[1] - 1) & (ids[2] == grid[2] - 1))
            def _():
                finish_side()

    sems = [] if side is None else side["scratch"]
    outs = pl.pallas_call(
        body, name=name, grid=grid,
        in_specs=in_specs + [hbm] * n_side, out_specs=[o_spec] + [hbm] * n_side,
        out_shape=[S((m, n), out_dtype)] + ([] if side is None else side["out_shape"]),
        scratch_shapes=([pltpu.VMEM((tm, tn), F32)] if nk > 1 else []) + sems,
        compiler_params=pltpu.CompilerParams(
            dimension_semantics=("arbitrary",) * 3 if side is not None else ("parallel", "parallel", "arbitrary")),
    )(*args, *([] if side is None else side["arrs"]))
    return outs[0] if side is None else (outs[0], outs[1:])


def _rows(arr, width=None, cblk=0):
    return ("rows", arr, arr.shape[1] if width is None else width, cblk)


def _whole(arr):
    return ("whole", arr)


def _rowcall(body, name, tt, ins, outs, scratch=()):
    in_specs, args = [], []
    for spec in ins:
        if spec[0] == "rows":
            _, arr, width, cblk = spec
            in_specs.append(pl.BlockSpec((tt, width), functools.partial(lambda i, c: (i, c), c=cblk)))
        else:
            arr = spec[1]
            in_specs.append(pl.BlockSpec(arr.shape, functools.partial(lambda i, nd: (0,) * nd, nd=arr.ndim)))
        args.append(arr)
    out_specs, out_shape = [], []
    for kind, shape, dtype in outs:
        if kind == "rows":
            out_specs.append(pl.BlockSpec((tt, shape), lambda i: (i, 0)))
            out_shape.append(S((T, shape), dtype))
        else:
            out_specs.append(pl.BlockSpec(shape, functools.partial(lambda i, nd: (0,) * nd, nd=len(shape))))
            out_shape.append(S(shape, dtype))
    return pl.pallas_call(
        body, name=name, grid=(T // tt,), in_specs=in_specs, out_specs=out_specs, out_shape=out_shape,
        scratch_shapes=list(scratch),
        compiler_params=pltpu.CompilerParams(dimension_semantics=("arbitrary",)),
    )(*args)


def _rms_fwd(x, g, name):
    def body(x_ref, g_ref, h_ref):
        xf = x_ref[...]
        r = lax.rsqrt(jnp.mean(xf * xf, axis=-1, keepdims=True) + EPS)
        h_ref[...] = (xf * r * g_ref[...]).astype(BF16)

    return _rowcall(body, name, 512, [_rows(x), _whole(g)], [("rows", D, BF16)])[0]


def _rms_bwd(dn, x, g, skip, name):
    def body(dn_ref, x_ref, g_ref, s_ref, dx_ref, dxb_ref, dg_ref):
        xf = x_ref[...]
        r = lax.rsqrt(jnp.mean(xf * xf, axis=-1, keepdims=True) + EPS)
        dn_ = dn_ref[...]
        u = dn_ * g_ref[...]
        dx = s_ref[...] + r * u - xf * (r * r * r) * jnp.mean(u * xf, axis=-1, keepdims=True)
        dx_ref[...] = dx
        dxb_ref[...] = dx.astype(BF16)
        part = jnp.sum(dn_ * xf * r, axis=0, keepdims=True)

        @pl.when(pl.program_id(0) == 0)
        def _():
            dg_ref[...] = part

        @pl.when(pl.program_id(0) > 0)
        def _():
            dg_ref[...] += part

    return _rowcall(body, name, 256, [_rows(dn), _rows(x), _whole(g), _rows(skip)],
                    [("rows", D, F32), ("rows", D, BF16), ("acc", (1, D), F32)])


def _rot_tables(pos_ref, inv_ref):
    lane = lax.broadcasted_iota(jnp.int32, (1, 128), 1) % HD
    ang = pos_ref[...] * inv_ref[...]
    cos, sin = jnp.cos(ang), jnp.sin(ang)
    c = jnp.where(lane < ROT_DIM, cos, 1.0)
    sp = jnp.where((lane >= ROT_DIM // 2) & (lane < ROT_DIM), sin, 0.0)
    sm = jnp.where(lane < ROT_DIM // 2, -sin, 0.0)
    return c, sp, sm


def _head_sums(v):
    same = (lax.broadcasted_iota(jnp.int32, (128, 128), 0) < HD) == (lax.broadcasted_iota(jnp.int32, (128, 128), 1) < HD)
    ones = jnp.where(same, 1.0, 0.0).astype(BF16)
    hi = v.astype(BF16)
    lo = (v - hi.astype(F32)).astype(BF16)
    return _nn(hi, ones) + _nn(lo, ones)


def _pair_norm(t):
    return lax.rsqrt(_head_sums(t * t) * (1.0 / HD) + EPS)


def _pair_mean(t):
    return _head_sums(t) * (1.0 / HD)


TT = 256
NCH = ATT_QKV // 128


def _res_shape(grp, dtype):
    return S((DILATIONS[grp], T // DILATIONS[grp], ATT_W), dtype)


def _res_spec(grp):
    dil = DILATIONS[grp]
    return pl.BlockSpec((dil, TT // dil, ATT_W), lambda i: (0, i, 0))


def _to_residues(sc, j, dst_ref, dil, cols):
    n = TT // dil
    for r in range(dil):
        rows = sc[j] if dil == 1 else sc.at[j][pl.ds(r, n, stride=dil), :]
        dst_ref[r, :, cols] = rows.astype(dst_ref.dtype)


def _from_residues(src_ref, cols, sc, j, dil):
    n = TT // dil
    for r in range(dil):
        if dil == 1:
            sc[j] = src_ref[r, :, cols]
        else:
            sc.at[j][pl.ds(r, n, stride=dil), :] = src_ref[r, :, cols]


def _tok_spec(width, cblk=0):
    return pl.BlockSpec((TT, width), functools.partial(lambda i, c: (i, c), c=cblk))


def _const_spec(arr_or_shape):
    shape = arr_or_shape if isinstance(arr_or_shape, tuple) else arr_or_shape.shape
    return pl.BlockSpec(shape, functools.partial(lambda i, nd: (0,) * nd, nd=len(shape)))


def _qk_prep(proj, pos, inv, gq, gk):
    def body(q_ref, k_ref, v_ref, pos_ref, inv_ref, gq_ref, gk_ref, *rest):
        outs, sc = rest[:9], rest[9]
        c, sp, sm = _rot_tables(pos_ref, inv_ref)
        for which, (src, g_ref) in enumerate(((q_ref, gq_ref), (k_ref, gk_ref), (v_ref, None))):
            for j in range(NCH):
                t = src[:, j * 128:(j + 1) * 128]
                if g_ref is not None:
                    n = t * _pair_norm(t) * g_ref[...]
                    t = n * c + pltpu.roll(n, 8, 1) * sp + pltpu.roll(n, 120, 1) * sm
                sc[j] = t
            for j in range(NCH):
                grp, sub = divmod(j * 128, ATT_W)
                _to_residues(sc, j, outs[which * 3 + grp], DILATIONS[grp], slice(sub, sub + 128))

    return pl.pallas_call(
        body, name="qk_prep", grid=(T // TT,),
        in_specs=[_tok_spec(ATT_QKV, C_QA // ATT_QKV), _tok_spec(ATT_QKV, C_KA // ATT_QKV),
                  _tok_spec(ATT_QKV, C_VA // ATT_QKV), _tok_spec(1), _const_spec(inv), _const_spec(gq), _const_spec(gk)],
        out_specs=[_res_spec(g) for _ in range(3) for g in range(3)],
        out_shape=[_res_shape(g, BF16) for _ in range(3) for g in range(3)],
        scratch_shapes=[pltpu.VMEM((NCH, TT, 128), F32)],
        compiler_params=pltpu.CompilerParams(dimension_semantics=("arbitrary",)),
    )(proj, proj, proj, pos, inv, gq, gk)


def _qk_bwd(proj, pos, inv, gq, gk, dqs, dks, dvs, dproj):
    const = lambda a: pl.BlockSpec(a.shape, functools.partial(lambda i, p, nd: (0,) * nd, nd=a.ndim))
    res = lambda g: pl.BlockSpec((DILATIONS[g], TT // DILATIONS[g], ATT_W), lambda i, p: (0, i, 0))
    base = C_QA // ATT_QKV

    def body(t_ref, pos_ref, inv_ref, gq_ref, gk_ref, dq0, dq1, dq2, dk0, dk1, dk2, dv0, dv1, dv2, buf_ref,
             out_ref, dgq_ref, dgk_ref, sc):
        del buf_ref
        part = pl.program_id(1)
        first = pl.program_id(0) == 0

        def gather(drefs):
            for j in range(NCH):
                grp, sub = divmod(j * 128, ATT_W)
                _from_residues(drefs[grp], slice(sub, sub + 128), sc, j, DILATIONS[grp])

        def normed(g_ref, drefs, dg_ref):
            c, sp, sm = _rot_tables(pos_ref, inv_ref)
            gather(drefs)
            dg = jnp.zeros((1, 128), F32)
            for j in range(NCH):
                cols = slice(j * 128, (j + 1) * 128)
                d_rot = sc[j]
                dn = d_rot * c + pltpu.roll(d_rot * sp, 120, 1) + pltpu.roll(d_rot * sm, 8, 1)
                t = t_ref[:, cols]
                r = _pair_norm(t)
                u = dn * g_ref[...]
                out_ref[:, cols] = (r * u - t * (r * r * r) * _pair_mean(u * t)).astype(BF16)
                dg = dg + jnp.sum(dn * t * r, axis=0, keepdims=True)
            dg = dg + pltpu.roll(dg, HD, 1)

            @pl.when(first)
            def _():
                dg_ref[...] = dg

            @pl.when(jnp.logical_not(first))
            def _():
                dg_ref[...] += dg

        @pl.when(part == 0)
        def _():
            gather((dv0, dv1, dv2))
            for j in range(NCH):
                out_ref[:, j * 128:(j + 1) * 128] = sc[j].astype(BF16)

        @pl.when(part == 1)
        def _():
            normed(gq_ref, (dq0, dq1, dq2), dgq_ref)

        @pl.when(part == 2)
        def _():
            normed(gk_ref, (dk0, dk1, dk2), dgk_ref)

    keep = pl.BlockSpec((1, 128), lambda i, p: (0, 0))
    return pl.pallas_call(
        body, name="qk_bwd", grid=(T // TT, 3),
        in_specs=[pl.BlockSpec((TT, ATT_QKV), lambda i, p: (i, base + jnp.maximum(p - 1, 0))),
                  pl.BlockSpec((TT, 1), lambda i, p: (i, 0)), const(inv), const(gq), const(gk)]
        + [res(g) for _ in range(3) for g in range(3)] + [pl.BlockSpec(memory_space=pl.ANY)],
        out_specs=[pl.BlockSpec((TT, ATT_QKV), lambda i, p: (i, base + jnp.where(p == 0, 2, p - 1))), keep, keep],
        out_shape=[S(dproj.shape, dproj.dtype), S((1, 128), F32), S((1, 128), F32)],
        input_output_aliases={14: 0},
        scratch_shapes=[pltpu.VMEM((NCH, TT, 128), F32)],
        compiler_params=pltpu.CompilerParams(dimension_semantics=("arbitrary", "arbitrary")),
    )(proj, pos, inv, gq, gk, *dqs, *dks, *dvs, dproj)


def _split_heads(t):
    low = lax.broadcasted_iota(jnp.int32, (1, 128), 1) < HD
    zero = jnp.zeros_like(t)
    return jnp.concatenate([jnp.where(low, t, zero), jnp.where(low, zero, t)], axis=0)


def _join_heads(t2):
    low = lax.broadcasted_iota(jnp.int32, (1, 128), 1) < HD
    n = t2.shape[0] // 2
    return jnp.where(low, t2[:n], t2[n:])


def _band_masks():
    row = lax.broadcasted_iota(jnp.int32, (BLK, 2 * BLK), 0)
    col = lax.broadcasted_iota(jnp.int32, (BLK, 2 * BLK), 1) % BLK
    return col <= row, col >= row


def _per_head(col_a, col_b):
    first = lax.broadcasted_iota(jnp.int32, (1, 2 * BLK), 1) < BLK
    return jnp.where(first, col_a, col_b)


def _att_fwd(q, k, v, grp, name):
    dil = DILATIONS[grp]
    nb = T // dil // BLK
    scale = HD ** -0.5

    def body(q_ref, kp_ref, kc_ref, vp_ref, vc_ref, o_ref, lse_ref):
        has_prev = pl.program_id(1) > 0
        m_cur, m_prev = _band_masks()
        m_prev = m_prev & has_prev
        for j in range(ATT_W // 128):
            cols = slice(j * 128, (j + 1) * 128)
            q = q_ref[:, cols]
            s_c = jnp.where(m_cur, _nt(q, _split_heads(kc_ref[:, cols])) * scale, -jnp.inf)
            s_p = jnp.where(m_prev, _nt(q, _split_heads(kp_ref[:, cols])) * scale, -jnp.inf)
            mx = []
            for half in (slice(0, BLK), slice(BLK, 2 * BLK)):
                mx.append(jnp.maximum(jnp.max(s_c[:, half], axis=-1, keepdims=True),
                                      jnp.max(s_p[:, half], axis=-1, keepdims=True)))
            m2 = _per_head(*mx)
            p_c = jnp.exp(s_c - m2)
            p_p = jnp.exp(s_p - m2)
            den = []
            for half in (slice(0, BLK), slice(BLK, 2 * BLK)):
                den.append(jnp.sum(p_c[:, half], axis=-1, keepdims=True) + jnp.sum(p_p[:, half], axis=-1, keepdims=True))
            acc = _nn(p_c.astype(BF16), _split_heads(vc_ref[:, cols])) + _nn(p_p.astype(BF16), _split_heads(vp_ref[:, cols]))
            low = lax.broadcasted_iota(jnp.int32, (1, 128), 1) < HD
            o_ref[:, cols] = acc / jnp.where(low, den[0], den[1])
            lse_ref[:, cols] = jnp.where(low, mx[0] + jnp.log(den[0]), mx[1] + jnp.log(den[1]))

    cur = pl.BlockSpec((None, BLK, ATT_W), lambda r, i: (r, i, 0))
    prev = pl.BlockSpec((None, BLK, ATT_W), lambda r, i: (r, jnp.maximum(i - 1, 0), 0))
    return pl.pallas_call(
        body, name=name, grid=(dil, nb),
        in_specs=[cur, prev, cur, prev, cur],
        out_specs=[cur, cur], out_shape=[_res_shape(grp, F32)] * 2,
        compiler_params=pltpu.CompilerParams(dimension_semantics=("parallel", "arbitrary")),
    )(q, k, k, v, v)


def _att_bwd(q, k, v, datt, att, lse, grp, name):
    dil = DILATIONS[grp]
    nb = T // dil // BLK
    scale = HD ** -0.5

    def body(q0_ref, q1_ref, kp_ref, kc_ref, vp_ref, vc_ref, do0_ref, do1_ref, o0_ref, o1_ref, l0_ref, l1_ref,
             dq_ref, dk_ref, dv_ref):
        i = pl.program_id(1)
        has_prev = i > 0
        has_next = i < nb - 1
        m_cur, m_prev = _band_masks()
        low = lax.broadcasted_iota(jnp.int32, (1, 128), 1) < HD

        def stats(do_ref, o_ref, l_ref, cols):
            prod = do_ref[:, cols] * o_ref[:, cols]
            d_all = jnp.sum(prod, axis=-1, keepdims=True)
            d_low = jnp.sum(jnp.where(low, prod, 0.0), axis=-1, keepdims=True)
            lse_t = l_ref[:, cols]
            return _per_head(d_low, d_all - d_low), _per_head(lse_t[:, 0:1], lse_t[:, HD:HD + 1])

        def pair(q, k2, v2, do, dsum, lse2, mask):
            s = _nt(q, k2) * scale
            p = jnp.where(mask, jnp.exp(s - lse2), 0.0)
            ds = p * (_nt(do, v2) - dsum) * scale
            return p.astype(BF16), ds.astype(BF16)

        for j in range(ATT_W // 128):
            cols = slice(j * 128, (j + 1) * 128)
            q0, q1 = q0_ref[:, cols], q1_ref[:, cols]
            do0, do1 = do0_ref[:, cols].astype(BF16), do1_ref[:, cols].astype(BF16)
            kc2, kp2 = _split_heads(kc_ref[:, cols]), _split_heads(kp_ref[:, cols])
            vc2, vp2 = _split_heads(vc_ref[:, cols]), _split_heads(vp_ref[:, cols])
            d0, l0 = stats(do0_ref, o0_ref, l0_ref, cols)
            d1, l1 = stats(do1_ref, o1_ref, l1_ref, cols)
            p_a, ds_a = pair(q0, kc2, vc2, do0, d0, l0, m_cur)
            _, ds_b = pair(q0, kp2, vp2, do0, d0, l0, m_prev & has_prev)
            p_c, ds_c = pair(q1, kc2, vc2, do1, d1, l1, m_prev & has_next)
            dq_ref[:, cols] = _nn(ds_a, kc2) + _nn(ds_b, kp2)
            dk_ref[:, cols] = _join_heads(_tn(ds_a, q0) + _tn(ds_c, q1))
            dv_ref[:, cols] = _join_heads(_tn(p_a, do0) + _tn(p_c, do1))

    def spec(shift):
        return pl.BlockSpec((None, BLK, ATT_W), lambda r, i: (r, jnp.clip(i + shift, 0, nb - 1), 0))

    here, after, before = spec(0), spec(1), spec(-1)
    return pl.pallas_call(
        body, name=name, grid=(dil, nb),
        in_specs=[here, after, before, here, before, here, here, after, here, after, here, after],
        out_specs=[here] * 3, out_shape=[_res_shape(grp, F32)] * 3,
        compiler_params=pltpu.CompilerParams(dimension_semantics=("parallel", "arbitrary")),
    )(q, q, k, k, v, v, datt, datt, att, att, lse, lse)


def _att_merge(os_, lses, proj):
    nq = ATT_W // 128

    def body(o0, o1, o2, l0, l1, l2, za_ref, att_ref, lse_ref, ain_ref, sc):
        for a, ref in enumerate((o0, o1, o2, l0, l1, l2)):
            for j in range(nq):
                _from_residues(ref, slice(j * 128, (j + 1) * 128), sc, a * nq + j, DILATIONS[a % 3])
        for j in range(nq):
            cols = slice(j * 128, (j + 1) * 128)
            oa, ob, oc = (sc[a * nq + j] for a in range(3))
            la, lb, lc = (sc[(3 + a) * nq + j] for a in range(3))
            m = jnp.maximum(jnp.maximum(la, lb), lc)
            wa, wb, wc = jnp.exp(la - m), jnp.exp(lb - m), jnp.exp(lc - m)
            tot = wa + wb + wc
            att = (wa * oa + wb * ob + wc * oc) / tot
            att_ref[:, cols] = att
            lse_ref[:, cols] = m + jnp.log(tot)
            za = za_ref[:, cols]
            ain_ref[:, cols] = (att * za * _sigmoid(za)).astype(BF16)

    return pl.pallas_call(
        body, name="att_merge", grid=(T // TT,),
        in_specs=[_res_spec(g) for _ in range(2) for g in range(3)] + [_tok_spec(ATT_W, C_ZA // ATT_W)],
        out_specs=[_tok_spec(ATT_W)] * 3,
        out_shape=[S((T, ATT_W), F32), S((T, ATT_W), F32), S((T, ATT_W), BF16)],
        scratch_shapes=[pltpu.VMEM((6 * nq, TT, 128), F32)],
        compiler_params=pltpu.CompilerParams(dimension_semantics=("arbitrary",)),
    )(*os_, *lses, proj)


def _att_gate_bwd(dain, att, lse, proj, dproj):
    nq = ATT_W // 128

    def body(d_ref, att_ref, lse_ref, za_ref, buf_ref, dza_ref, da0, da1, da2, at1, at2, ls1, ls2, sc):
        del buf_ref
        for j in range(nq):
            cols = slice(j * 128, (j + 1) * 128)
            za = za_ref[:, cols]
            sg = _sigmoid(za)
            d = d_ref[:, cols]
            att_ = att_ref[:, cols]
            dza_ref[:, cols] = (d * att_ * sg * (1.0 + za * (1.0 - sg))).astype(BF16)
            sc[j] = d * za * sg
            sc[nq + j] = att_
            sc[2 * nq + j] = lse_ref[:, cols]
        for j in range(nq):
            cols = slice(j * 128, (j + 1) * 128)
            for grp, dst in enumerate((da0, da1, da2)):
                _to_residues(sc, j, dst, DILATIONS[grp], cols)
            for grp, dst in ((1, at1), (2, at2)):
                _to_residues(sc, nq + j, dst, DILATIONS[grp], cols)
            for grp, dst in ((1, ls1), (2, ls2)):
                _to_residues(sc, 2 * nq + j, dst, DILATIONS[grp], cols)

    res = (0, 1, 2, 1, 2, 1, 2)
    return pl.pallas_call(
        body, name="att_gate_bwd", grid=(T // TT,),
        in_specs=[_tok_spec(ATT_W)] * 3 + [_tok_spec(ATT_W, C_ZA // ATT_W), pl.BlockSpec(memory_space=pl.ANY)],
        out_specs=[_tok_spec(ATT_W, C_ZA // ATT_W)] + [_res_spec(g) for g in res],
        out_shape=[S(dproj.shape, dproj.dtype)] + [_res_shape(g, F32) for g in res],
        input_output_aliases={4: 0},
        scratch_shapes=[pltpu.VMEM((3 * nq, TT, 128), F32)],
        compiler_params=pltpu.CompilerParams(dimension_semantics=("arbitrary",)),
    )(dain, att, lse, proj, dproj)


def _split3(v):
    hi = v.astype(BF16)
    r1 = v - hi.astype(F32)
    mid = r1.astype(BF16)
    lo = (r1 - mid.astype(F32)).astype(BF16)
    return hi, mid, lo


def _tri_sum(v, upper):
    n = v.shape[0]
    row = lax.broadcasted_iota(jnp.int32, (n, n), 0)
    col = lax.broadcasted_iota(jnp.int32, (n, n), 1)
    tri = jnp.where(col >= row if upper else col <= row, 1.0, 0.0).astype(BF16)
    hi, mid, lo = _split3(v)
    return _nn(tri, hi) + _nn(tri, mid) + _nn(tri, lo)


def _gla_gates(glr_ref, w2_ref, b_ref):
    logit = _nn(glr_ref[...].astype(BF16), w2_ref[...]) + b_ref[...]
    lg = (jnp.minimum(logit, 0.0) - jnp.log(1.0 + jnp.exp(-jnp.abs(logit)))) * (1.0 / GLA_TAU)
    return logit, _tri_sum(lg, upper=False)


def _gla_head(cum, q_ref, k_ref, h):
    cols = slice(h * GDK, (h + 1) * GDK)
    b = cum[:, cols]
    last = b[GLA_C - 1:GLA_C, :]
    e_pos = jnp.exp(b)
    e_neg = jnp.exp(-b)
    e_end = jnp.exp(last - b)
    qt = q_ref[:, cols] * (GDK ** -0.5) * e_pos
    kt = k_ref[:, cols] * e_neg
    kh = k_ref[:, cols] * e_end
    return b, last, e_pos, e_neg, e_end, qt, kt, kh


def _causal(n):
    return lax.broadcasted_iota(jnp.int32, (n, n), 1) <= lax.broadcasted_iota(jnp.int32, (n, n), 0)


def _gla_fwd(proj, w2p, bg, gn):
    nc = T // GLA_C

    def body(q_ref, k_ref, v_ref, glr_ref, zg_ref, w2_ref, b_ref, gn_ref, o_ref, bin_ref, st_ref, state):
        @pl.when(pl.program_id(0) == 0)
        def _():
            state[...] = jnp.zeros_like(state)

        _, cum = _gla_gates(glr_ref, w2_ref, b_ref)
        for h in range(GH):
            _, last, _, _, _, qt, kt, kh = _gla_head(cum, q_ref, k_ref, h)
            vcols = slice(h * GDV, (h + 1) * GDV)
            st = state[h]
            st_ref[0, h] = st
            v = v_ref[:, vcols].astype(BF16)
            qb = qt.astype(BF16)
            a = jnp.where(_causal(GLA_C), _nt(qb, kt.astype(BF16)), 0.0)
            o = _nt(qb, st.astype(BF16)) + _nn(a.astype(BF16), v)
            state[h] = st * jnp.exp(last) + _tn(v, kh.astype(BF16))
            o_ref[:, vcols] = o
            r = lax.rsqrt(jnp.mean(o * o, axis=-1, keepdims=True) + EPS)
            zg = zg_ref[:, vcols]
            bin_ref[:, vcols] = (o * r * gn_ref[...] * zg * _sigmoid(zg)).astype(BF16)

    row = lambda width, cblk: pl.BlockSpec((GLA_C, width), functools.partial(lambda i, c: (i, c), c=cblk))
    full = lambda a: pl.BlockSpec(a.shape, functools.partial(lambda i, nd: (0,) * nd, nd=a.ndim))
    return pl.pallas_call(
        body, name="gla_fwd", grid=(nc,),
        in_specs=[row(512, C_QG // 512), row(512, C_KG // 512), row(1024, C_VG // 1024), row(GLR_W, C_GLR // GLR_W),
                  row(1024, C_ZG // 1024), full(w2p), full(bg), full(gn)],
        out_specs=[pl.BlockSpec((GLA_C, GH * GDV), lambda i: (i, 0)), pl.BlockSpec((GLA_C, GH * GDV), lambda i: (i, 0)),
                   pl.BlockSpec((1, GH, GDV, GDK), lambda i: (i, 0, 0, 0))],
        out_shape=[S((T, GH * GDV), F32), S((T, GH * GDV), BF16), S((nc, GH, GDV, GDK), F32)],
        scratch_shapes=[pltpu.VMEM((GH, GDV, GDK), F32)],
        compiler_params=pltpu.CompilerParams(dimension_semantics=("arbitrary",)),
    )(proj, proj, proj, proj, proj, w2p, bg, gn)


def _gla_bwd(proj, w2p, bg, gn, o_gla, states, dbin, dproj):
    nc = T // GLA_C

    def body(q_ref, k_ref, v_ref, glr_ref, zg_ref, w2_ref, b_ref, gn_ref, o_ref, st_ref, dbin_ref, buf_ref,
             out_ref, dw2_ref, dbg_ref, dgn_ref, dstate, dlogit):
        del buf_ref
        dq_ref = out_ref.at[:, C_QG:C_KG]
        dk_ref = out_ref.at[:, C_KG:C_VG]
        dv_ref = out_ref.at[:, C_VG:C_ZG]
        dzg_ref = out_ref.at[:, C_ZG:C_GLR]
        dglr_ref = out_ref.at[:, C_GLR:C_GLR + GLR_W]
        first = pl.program_id(0) == 0

        @pl.when(first)
        def _():
            dstate[...] = jnp.zeros_like(dstate)

        logit, cum = _gla_gates(glr_ref, w2_ref, b_ref)
        is_last = lax.broadcasted_iota(jnp.int32, (GLA_C, 1), 0) == GLA_C - 1
        dgn = jnp.zeros((1, GDV), F32)
        for h in range(GH):
            _, last, e_pos, e_neg, e_end, qt, kt, kh = _gla_head(cum, q_ref, k_ref, h)
            cols = slice(h * GDK, (h + 1) * GDK)
            vcols = slice(h * GDV, (h + 1) * GDV)
            o = o_ref[:, vcols]
            r = lax.rsqrt(jnp.mean(o * o, axis=-1, keepdims=True) + EPS)
            zg = zg_ref[:, vcols]
            sg = _sigmoid(zg)
            db_ = dbin_ref[:, vcols]
            dlin = db_ * zg * sg
            dzg_ref[:, vcols] = (db_ * (o * r * gn_ref[...]) * sg * (1.0 + zg * (1.0 - sg))).astype(BF16)
            u = dlin * gn_ref[...]
            do = (r * u - o * (r * r * r) * jnp.mean(u * o, axis=-1, keepdims=True)).astype(BF16)
            dgn = dgn + jnp.sum(dlin * o * r, axis=0, keepdims=True)
            st = st_ref[0, h]
            dst = dstate[h]
            v = v_ref[:, vcols].astype(BF16)
            qb, kb, khb = qt.astype(BF16), kt.astype(BF16), kh.astype(BF16)
            dstb = dst.astype(BF16)
            causal = _causal(GLA_C)
            a = jnp.where(causal, _nt(qb, kb), 0.0).astype(BF16)
            da = jnp.where(causal, _nt(do, v), 0.0).astype(BF16)
            dqt = _nn(do, st.astype(BF16)) + _nn(da, kb)
            dkt = _tn(da, qb)
            dkh = _nn(v, dstb)
            dv_ref[:, vcols] = (_tn(a, do) + _nt(khb, dstb)).astype(BF16)
            lam = jnp.exp(last)
            dlam = jnp.sum(dst * st, axis=0, keepdims=True)
            dstate[h] = dst * lam + _tn(do, qb)
            dq_ref[:, cols] = (dqt * e_pos * (GDK ** -0.5)).astype(BF16)
            dk_ref[:, cols] = (dkt * e_neg + dkh * e_end).astype(BF16)
            dkh_kh = dkh * kh
            dcum = dqt * qt - dkt * kt - dkh_kh
            dlast = jnp.sum(dkh_kh, axis=0, keepdims=True) + dlam * lam
            dcum = jnp.where(is_last, dcum + dlast, dcum)
            dlg = _tri_sum(dcum, upper=True)
            dlogit[:, cols] = dlg * (1.0 / GLA_TAU) * (1.0 - _sigmoid(logit[:, cols]))

        dl = dlogit[...]
        dlb = dl.astype(BF16)
        dglr_ref[...] = _nt(dlb, w2_ref[...]).astype(BF16)
        dw2 = _tn(glr_ref[...].astype(BF16), dlb)
        dbg = jnp.sum(dl, axis=0, keepdims=True)

        @pl.when(first)
        def _():
            dw2_ref[...] = dw2
            dbg_ref[...] = dbg
            dgn_ref[...] = dgn

        @pl.when(jnp.logical_not(first))
        def _():
            dw2_ref[...] += dw2
            dbg_ref[...] += dbg
            dgn_ref[...] += dgn

    rev = lambda i: nc - 1 - i
    row = lambda width, cblk: pl.BlockSpec((GLA_C, width), functools.partial(lambda i, c: (rev(i), c), c=cblk))
    full = lambda a: pl.BlockSpec(a.shape, functools.partial(lambda i, nd: (0,) * nd, nd=a.ndim))
    keep = lambda shape: pl.BlockSpec(shape, functools.partial(lambda i, nd: (0,) * nd, nd=len(shape)))
    return pl.pallas_call(
        body, name="gla_bwd", grid=(nc,),
        in_specs=[row(512, C_QG // 512), row(512, C_KG // 512), row(1024, C_VG // 1024), row(GLR_W, C_GLR // GLR_W),
                  row(1024, C_ZG // 1024), full(w2p), full(bg), full(gn), row(GH * GDV, 0),
                  pl.BlockSpec((1, GH, GDV, GDK), lambda i: (rev(i), 0, 0, 0)), row(GH * GDV, 0),
                  pl.BlockSpec(memory_space=pl.ANY)],
        out_specs=[row(GLA_GROUP_W, 0), keep((GLR_W, 512)), keep((1, 512)), keep((1, GDV))],
        out_shape=[S(dproj.shape, dproj.dtype), S((GLR_W, 512), F32), S((1, 512), F32), S((1, GDV), F32)],
        input_output_aliases={11: 0},
        scratch_shapes=[pltpu.VMEM((GH, GDV, GDK), F32), pltpu.VMEM((GLA_C, GH * GDK), F32)],
        compiler_params=pltpu.CompilerParams(dimension_semantics=("arbitrary",)),
    )(proj, proj, proj, proj, proj, w2p, bg, gn, o_gla, states, dbin, dproj)


def _merge_fwd(ya, yb, proj):
    def body(ya_ref, yb_ref, ga_ref, gb_ref, y_ref):
        y_ref[...] = (_sigmoid(ga_ref[...]) * ya_ref[...] + _sigmoid(gb_ref[...]) * yb_ref[...]).astype(BF16)

    return _rowcall(body, "merge_fwd", 512,
                    [_rows(ya), _rows(yb), _rows(proj, D, C_GA // D), _rows(proj, D, C_GB // D)],
                    [("rows", D, BF16)])[0]


def _merge_bwd(dy, ya, yb, proj):
    tt = 512

    def body(dy_ref, ya_ref, yb_ref, g_ref, dg_ref, dya_ref, dyb_ref):
        dy_ = dy_ref[...]
        sa, sb = _sigmoid(g_ref[:, :D]), _sigmoid(g_ref[:, D:])
        dg_ref[:, :D] = (dy_ * ya_ref[...] * sa * (1.0 - sa)).astype(BF16)
        dg_ref[:, D:] = (dy_ * yb_ref[...] * sb * (1.0 - sb)).astype(BF16)
        dya_ref[...] = (dy_ * sa).astype(BF16)
        dyb_ref[...] = (dy_ * sb).astype(BF16)

    tok = pl.BlockSpec((tt, D), lambda i: (i, 0))
    gates = pl.BlockSpec((tt, 2 * D), lambda i: (i, C_GA // (2 * D)))
    return pl.pallas_call(
        body, name="merge_bwd", grid=(T // tt,),
        in_specs=[tok, tok, tok, gates], out_specs=[gates, tok, tok],
        out_shape=[S((T, NCOL), BF16), S((T, D), BF16), S((T, D), BF16)],
        compiler_params=pltpu.CompilerParams(dimension_semantics=("arbitrary",)),
    )(dy, ya, yb, proj)


def _loss_head(x1, e, u, target):
    def body(x1_ref, e_ref, u_ref, t_ref, loss_ref, dout_ref, de_ref, du_ref, acc):
        first = pl.program_id(0) == 0
        pg = _sigmoid(u_ref[...])
        e_ = e_ref[...]
        diff = x1_ref[...] + e_ * pg - t_ref[...]
        part = jnp.sum(diff * diff, axis=0, keepdims=True)

        @pl.when(first)
        def _():
            acc[...] = part

        @pl.when(jnp.logical_not(first))
        def _():
            acc[...] += part

        dout = diff * (1.0 / D)
        dout_ref[...] = dout
        de_ref[...] = (dout * pg).astype(BF16)
        du_ref[...] = (dout * e_ * pg * (1.0 - pg)).astype(BF16)
        loss_ref[...] = jnp.zeros((1, 128), F32) + jnp.sum(acc[...], axis=-1, keepdims=True) * (0.5 / D)

    return _rowcall(body, "loss_head", 256, [_rows(x1), _rows(e), _rows(u), _rows(target)],
                    [("acc", (1, 128), F32), ("rows", D, F32), ("rows", D, BF16), ("rows", D, BF16)],
                    scratch=[pltpu.VMEM((1, D), F32)])


def _peer(k):
    x, y, c = lax.axis_index("x"), lax.axis_index("y"), lax.axis_index("c")
    return (x ^ ((k >> 2) & 1), y ^ ((k >> 1) & 1), c ^ (k & 1))


def _my_index():
    return 4 * lax.axis_index("x") + 2 * lax.axis_index("y") + lax.axis_index("c")


def _peer_index(k):
    px, py, pc = _peer(k)
    return 4 * px + 2 * py + pc


def _pairwise_plan(src_of, dst_of, landed_of, own_src, own_dst):
    def plan(ins, outs, send, recv, local):
        n = len(ins)

        def own():
            return [pltpu.make_async_copy(own_src(ins[a]), own_dst(outs[a]), local.at[a]) for a in range(n)]

        def remote(k, a, src, dst):
            return pltpu.make_async_remote_copy(src_ref=src, dst_ref=dst, send_sem=send.at[k - 1, a],
                                                recv_sem=recv.at[k - 1, a], device_id=_peer(k), device_id_type=MESH)

        def sent():
            return [remote(k, a, src_of(ins[a], k), dst_of(outs[a])) for k in range(1, NDEV) for a in range(n)]

        def start():
            for cp in own() + sent():
                cp.start()

        def finish():
            for k in range(1, NDEV):
                for a in range(n):
                    remote(k, a, own_src(ins[a]), landed_of(outs[a], k)).wait_recv()
            for cp in sent():
                cp.wait_send()
            for cp in own():
                cp.wait()

        return start, finish

    return plan


def _pairwise_sems(n):
    return [pltpu.SemaphoreType.DMA((NDEV - 1, n)), pltpu.SemaphoreType.DMA((NDEV - 1, n)),
            pltpu.SemaphoreType.DMA((n,))]


def _gather_side(arrs):
    plan = _pairwise_plan(src_of=lambda i, k: i, dst_of=lambda o: o.at[_my_index()],
                          landed_of=lambda o, k: o.at[_peer_index(k)],
                          own_src=lambda i: i, own_dst=lambda o: o.at[_my_index()])
    return dict(arrs=arrs, out_shape=[S((NDEV,) + a.shape, a.dtype) for a in arrs],
                scratch=_pairwise_sems(len(arrs)), plan=plan)


def _exchange_side(arrs):
    plan = _pairwise_plan(src_of=lambda i, k: i.at[_peer_index(k)], dst_of=lambda o: o.at[_my_index()],
                          landed_of=lambda o, k: o.at[_peer_index(k)],
                          own_src=lambda i: i.at[_my_index()], own_dst=lambda o: o.at[_my_index()])
    return dict(arrs=arrs, out_shape=[S(a.shape, a.dtype) for a in arrs], scratch=_pairwise_sems(len(arrs)), plan=plan)


def _comm_call(side, name):
    n = len(side["arrs"])

    def body(*refs):
        start, finish = side["plan"](refs[:n], refs[n:2 * n], *refs[2 * n:])
        start()
        finish()

    hbm = pl.BlockSpec(memory_space=pl.ANY)
    return pl.pallas_call(body, name=name, in_specs=[hbm] * n, out_specs=[hbm] * n, out_shape=side["out_shape"],
                          scratch_shapes=side["scratch"])(*side["arrs"])


def _all_gather_by_chip(arrs, name):
    n = len(arrs)

    def body(*refs):
        ins, outs = refs[:n], refs[n:2 * n]
        send, recv, local = refs[2 * n:]
        x, y, c = lax.axis_index("x"), lax.axis_index("y"), lax.axis_index("c")
        me, sibling = (x, y, c), (x, y, 1 - c)
        chips = [(1 - x, y), (x, 1 - y), (1 - x, 1 - y)]

        def copy(k, a, block, to, src=None):
            px, py, pc = block
            slot = outs[a].at[4 * px + 2 * py + pc]
            return pltpu.make_async_remote_copy(
                src_ref=slot if src is None else src, dst_ref=slot, send_sem=send.at[k, a], recv_sem=recv.at[k, a],
                device_id=to, device_id_type=MESH)

        mine = [pltpu.make_async_copy(ins[a], outs[a].at[4 * x + 2 * y + c], local.at[a]) for a in range(n)]
        first = []
        for a in range(n):
            first.append(copy(0, a, me, sibling, src=ins[a]))
            first += [copy(1 + j, a, me, (*chip, c), src=ins[a]) for j, chip in enumerate(chips)]
        for cp in mine + first:
            cp.start()
        passed = []
        for j, chip in enumerate(chips):
            for a in range(n):
                copy(1 + j, a, (*chip, c), me).wait_recv()
                passed.append(copy(4 + j, a, (*chip, c), sibling))
                passed[-1].start()
        for a in range(n):
            copy(0, a, sibling, me).wait_recv()
        for j, chip in enumerate(chips):
            for a in range(n):
                copy(4 + j, a, (*chip, 1 - c), me).wait_recv()
        for cp in first + passed:
            cp.wait_send()
        for cp in mine:
            cp.wait()

    hbm = pl.BlockSpec(memory_space=pl.ANY)
    return pl.pallas_call(
        body, name=name, in_specs=[hbm] * n, out_specs=[hbm] * n,
        out_shape=[S((NDEV,) + a.shape, a.dtype) for a in arrs],
        scratch_shapes=[pltpu.SemaphoreType.DMA((NDEV - 1, n)), pltpu.SemaphoreType.DMA((NDEV - 1, n)),
                        pltpu.SemaphoreType.DMA((n,))],
    )(*arrs)


NCHIP = 4


def _exchange_sibling(arrs, name):
    n = len(arrs)

    def body(*refs):
        ins, outs = refs[:n], refs[n:2 * n]
        send, recv = refs[2 * n:]
        x, y, c = lax.axis_index("x"), lax.axis_index("y"), lax.axis_index("c")
        copies = []
        for q in range(NCHIP):
            for a in range(n):
                copies.append(pltpu.make_async_remote_copy(
                    src_ref=ins[a].at[2 * q + (1 - c)], dst_ref=outs[a].at[q], send_sem=send.at[q, a],
                    recv_sem=recv.at[q, a], device_id=(x, y, 1 - c), device_id_type=MESH))
        for cp in copies:
            cp.start()
        for cp in copies:
            cp.wait_recv()
        for cp in copies:
            cp.wait_send()

    hbm = pl.BlockSpec(memory_space=pl.ANY)
    return pl.pallas_call(
        body, name=name, in_specs=[hbm] * n, out_specs=[hbm] * n,
        out_shape=[S((NCHIP,) + a.shape[1:], a.dtype) for a in arrs],
        scratch_shapes=[pltpu.SemaphoreType.DMA((NCHIP, n)), pltpu.SemaphoreType.DMA((NCHIP, n))],
    )(*arrs)


def _pair_add(mine, got, core, name):
    _, rows, cols = mine.shape
    tc = 256
    assert cols % tc == 0

    def body(core_ref, a_ref, b_ref, o_ref):
        o_ref[...] = (a_ref[...].astype(F32) + b_ref[...].astype(F32)).astype(BF16)

    return pl.pallas_call(
        body, name=name,
        grid_spec=pltpu.PrefetchScalarGridSpec(
            num_scalar_prefetch=1, grid=(NCHIP, cols // tc),
            in_specs=[pl.BlockSpec((None, rows, tc), lambda q, i, core_ref: (2 * q + core_ref[0], 0, i)),
                      pl.BlockSpec((None, rows, tc), lambda q, i, core_ref: (q, 0, i))],
            out_specs=pl.BlockSpec((None, rows, tc), lambda q, i, core_ref: (q, 0, i))),
        out_shape=S((NCHIP, rows, cols), BF16),
    )(core, mine, got)


def _chips_side(arrs):
    def plan(ins, outs, send, recv, local):
        n = len(ins)

        def places():
            x, y, c = lax.axis_index("x"), lax.axis_index("y"), lax.axis_index("c")
            return 2 * x + y, c, [(1 - x, y), (x, 1 - y), (1 - x, 1 - y)]

        def own():
            here, _, _ = places()
            return [pltpu.make_async_copy(ins[a].at[here], outs[a].at[here], local.at[a]) for a in range(n)]

        def remote(j, a, src_slot, dst_slot):
            _, c, chips = places()
            cx, cy = chips[j]
            return pltpu.make_async_remote_copy(
                src_ref=ins[a].at[src_slot], dst_ref=outs[a].at[dst_slot], send_sem=send.at[j, a],
                recv_sem=recv.at[j, a], device_id=(cx, cy, c), device_id_type=MESH)

        def sent():
            here, _, chips = places()
            return [remote(j, a, 2 * cx + cy, here) for j, (cx, cy) in enumerate(chips) for a in range(n)]

        def start():
            for cp in own() + sent():
                cp.start()

        def finish():
            here, _, chips = places()
            for j, (cx, cy) in enumerate(chips):
                for a in range(n):
                    remote(j, a, here, 2 * cx + cy).wait_recv()
            for cp in sent():
                cp.wait_send()
            for cp in own():
                cp.wait()

        return start, finish

    n = len(arrs)
    return dict(arrs=arrs, out_shape=[S(a.shape, a.dtype) for a in arrs],
                scratch=[pltpu.SemaphoreType.DMA((NCHIP - 1, n)), pltpu.SemaphoreType.DMA((NCHIP - 1, n)),
                         pltpu.SemaphoreType.DMA((n,))], plan=plan)


def _adamw(parts, w, m, v, name, tr, tc=None):
    rows, cols = w.shape
    if tc is None:
        assert rows % tr == 0
        grid, shape, at = (rows // tr,), (tr, cols), (lambda i: (i, 0))
    else:
        assert cols % tc == 0
        grid, shape, at = (cols // tc,), (rows, tc), (lambda i: (0, i))
    c1 = 1.0 - ADAM_B1 ** ADAM_STEP
    c2 = 1.0 - ADAM_B2 ** ADAM_STEP

    nparts = parts.shape[0]

    def body(p_ref, w_ref, m_ref, v_ref, g_ref, d_ref, mo_ref, vo_ref):
        g = p_ref[0].astype(F32)
        for s in range(1, nparts):
            g = g + p_ref[s].astype(F32)
        m_new = ADAM_B1 * m_ref[...] + (1.0 - ADAM_B1) * g
        v_new = ADAM_B2 * v_ref[...] + (1.0 - ADAM_B2) * (g * g)
        g_ref[...] = g
        mo_ref[...] = m_new
        vo_ref[...] = v_new
        d_ref[...] = -ADAM_LR * ((m_new / c1) / (jnp.sqrt(v_new / c2) + ADAM_EPS) + ADAM_WD * w_ref[...])

    blk = pl.BlockSpec(shape, at)
    return pl.pallas_call(
        body, name=name, grid=grid,
        in_specs=[pl.BlockSpec((nparts,) + shape, lambda i: (0,) + at(i)), blk, blk, blk],
        out_specs=[blk] * 4, out_shape=[S((rows, cols), F32)] * 4,
        compiler_params=pltpu.CompilerParams(dimension_semantics=("parallel",)),
    )(parts, w, m, v)


def _to_aligned(wt):
    pad = jnp.zeros((GLR_W - GLR_N, wt.shape[1]), wt.dtype)
    return jnp.concatenate([wt[O_QG:O_GLR], wt[O_ZG:O_GA], wt[O_GLR:O_ZG], pad, wt[O_ZA:O_QG], wt[O_GA:O_END],
                            wt[O_QA:O_ZA]], axis=0)


def _from_aligned(wt):
    return jnp.concatenate([wt[C_QA:], wt[C_ZA:C_GA], wt[C_QG:C_ZG], wt[C_GLR:C_GLR + GLR_N], wt[C_ZG:C_GLR],
                            wt[C_GA:C_QA]], axis=0)


def _col_blocks(w, width):
    return w.reshape(w.shape[0], NDEV, width).transpose(1, 0, 2)


def _from_col_blocks(w):
    return w.transpose(1, 0, 2).reshape(w.shape[1], NDEV * w.shape[2])


SMALL = (("norm_g", D), ("qk_norm_q", HD), ("qk_norm_k", HD), ("gla_gate_b", 512), ("gla_norm_g", GDV),
         ("ple_norm_g", D))
SMALL_PAD = 4096


def _local_step(x2, p2, pos, tgt, norm_g, qk_norm_q, qk_norm_k, gla_gate_b, gla_norm_g, ple_norm_g, w_al,
                weights=None, proj_side=None, unpack=None, dw_side_of=None, dh_side_of=None):
    half = ROT_DIM // 2
    inv8 = jnp.power(jnp.float32(ROPE_THETA), -jnp.arange(half, dtype=F32) * 2.0 / ROT_DIM)
    inv = jnp.tile(jnp.concatenate([inv8, inv8, jnp.zeros((HD - ROT_DIM,), F32)]), 2).reshape(1, 128)
    gq = jnp.tile(qk_norm_q, (1, 2))
    gk = jnp.tile(qk_norm_k, (1, 2))

    h = _rms_fwd(x2, norm_g, "rms1_fwd")
    if proj_side is None:
        proj = _mm(h, w_al, mode="nt", name="proj", tm=1024, tn=1536, tk=D)
    else:
        proj, got = _mm(h, w_al, mode="nt", name="proj", tm=1024, tn=1536, tk=D, side=proj_side)
        weights = unpack(got)
    w2p, w_att_f, w_gla_f, w_out_f, w_pg_f, w_ple_f = weights
    qkv = _qk_prep(proj, pos, inv, gq, gk)
    fwd = [_att_fwd(qkv[g], qkv[3 + g], qkv[6 + g], g, f"att_fwd{g}") for g in range(3)]
    att, lse, ain = _att_merge([f[0] for f in fwd], [f[1] for f in fwd], proj)
    o_gla, bin_, states = _gla_fwd(proj, w2p, gla_gate_b, gla_norm_g)
    ya = _mm(ain, w_att_f, mode="nn", name="ya", tm=1024, tn=D, tk=512)
    yb = _mm(bin_, w_gla_f, mode="nn", name="yb", tm=1024, tn=D, tk=D)
    y = _merge_fwd(ya, yb, proj)
    x1 = _mm(y, w_out_f, mode="nn", name="x1", tm=1024, tn=D, tk=D, res=x2)
    n2 = _rms_fwd(x1, ple_norm_g, "rms2_fwd")
    u = _mm(n2, w_pg_f, mode="nn", name="ple_u", tm=1024, tn=D, tk=D)
    e = _mm(p2, w_ple_f, mode="nn", name="ple_e", tm=1024, tn=D, tk=PLE)
    loss_v, dout, de, du = _loss_head(x1, e, u, tgt)

    dw_ple = _mm(p2, de, mode="tn", name="dw_ple", tm=PLE, tn=D, tk=512)
    dw_pg = _mm(n2, du, mode="tn", name="dw_pg", tm=D, tn=D, tk=512)
    dn2 = _mm(du, w_pg_f, mode="nt", name="dn2", tm=1024, tn=D, tk=D)
    dx1, dx1b, dg_ple = _rms_bwd(dn2, x1, ple_norm_g, dout, "rms2_bwd")
    dw_out = _mm(y, dx1b, mode="tn", name="dw_out", tm=D, tn=D, tk=512)
    dy = _mm(dx1b, w_out_f, mode="nt", name="dy", tm=1024, tn=D, tk=D)
    dproj, dya, dyb = _merge_bwd(dy, ya, yb, proj)
    dw_att = _mm(ain, dya, mode="tn", name="dw_att", tm=512, tn=D, tk=512)
    dain = _mm(dya, w_att_f, mode="nt", name="dain", tm=1024, tn=512, tk=D)
    dw_gla = _mm(bin_, dyb, mode="tn", name="dw_gla", tm=D, tn=D, tk=512)
    dbin = _mm(dyb, w_gla_f, mode="nt", name="dbin", tm=1024, tn=D, tk=D)
    dproj, da0, da1, da2, at1, at2, ls1, ls2 = _att_gate_bwd(dain, att, lse, proj, dproj)
    datts, atts, lses = (da0, da1, da2), (att[None], at1, at2), (lse[None], ls1, ls2)
    dproj, dw2, dbg, dgn = _gla_bwd(proj, w2p, gla_gate_b, gla_norm_g, o_gla, states, dbin, dproj)
    bwd = [_att_bwd(qkv[g], qkv[3 + g], qkv[6 + g], datts[g], atts[g], lses[g], g, f"att_bwd{g}") for g in range(3)]
    dproj, dgq, dgk = _qk_bwd(proj, pos, inv, gq, gk, [b[0] for b in bwd], [b[1] for b in bwd],
                              [b[2] for b in bwd], dproj)
    out = dict(loss=loss_v, dw2=dw2, dw_att=dw_att, dw_gla=dw_gla, dw_out=dw_out, dw_pg=dw_pg, dw_ple=dw_ple,
               dgq=dgq, dgk=dgk, dbg=dbg, dgn=dgn, dg_ple=dg_ple)
    if dw_side_of is None:
        dw_al = _mm(dproj, h, mode="tn", name="dw_in", tm=1536, tn=D, tk=2048)
    else:
        dw_al, out["dw_side"] = _mm(dproj, h, mode="tn", name="dw_in", tm=1536, tn=D, tk=2048, side=dw_side_of(out))
    if dh_side_of is None:
        dh = _mm(dproj, w_al, mode="nn", name="dh", tm=1024, tn=D, tk=3584)
    else:
        dh, out["dh_side"] = _mm(dproj, w_al, mode="nn", name="dh", tm=1024, tn=D, tk=3584, side=dh_side_of(dw_al))
    grad_x, _, dg_norm = _rms_bwd(dh, x2, norm_g, dx1, "rms1_bwd")
    out.update(grad_x=grad_x, dw_al=dw_al, dg_norm=dg_norm)
    return out


def kernel(x, p, positions, norm_g, w_in, qk_norm_q, qk_norm_k, gla_gate_w2, gla_gate_b, gla_norm_g, w_att_proj, w_gla_proj, w_out, ple_norm_g, w_ple_gate, w_ple, loss_target, m_norm_g, m_w_in, m_qk_norm_q, m_qk_norm_k, m_gla_gate_w2, m_gla_gate_b, m_gla_norm_g, m_w_att_proj, m_w_gla_proj, m_w_out, m_ple_norm_g, m_w_ple_gate, m_w_ple, v_norm_g, v_w_in, v_qk_norm_q, v_qk_norm_k, v_gla_gate_w2, v_gla_gate_b, v_gla_norm_g, v_w_att_proj, v_w_gla_proj, v_w_out, v_ple_norm_g, v_w_ple_gate, v_w_ple):
    x2, p2, tgt = x[0], p[0, 0], loss_target[0]
    pos = positions.astype(F32).reshape(T, 1)

    rows3 = jnp.stack([w_gla_proj[0], w_out[0], w_ple_gate[0]]).astype(BF16)
    cols3 = jnp.concatenate([w_att_proj[0], w_ple[0], jnp.pad(gla_gate_w2[0], ((0, 0), (0, 64)))], axis=0).astype(BF16)
    wt, mt, vt = w_in[0].T, m_w_in[0].T, v_w_in[0].T
    (g_in,) = _all_gather_by_chip([wt.astype(BF16)], "gather_w_in")
    w_al = _to_aligned(g_in.reshape(W_IN_COLS, D))

    def unpack(got):
        g_rows, g_cols = got
        w2_f = _from_col_blocks(g_cols[:, 768:784, :64])
        return (jnp.pad(w2_f, ((0, GLR_W - GLR_N), (0, 0))), _from_col_blocks(g_cols[:, :512]),
                g_rows[:, 0].reshape(D, D), g_rows[:, 1].reshape(D, D), g_rows[:, 2].reshape(D, D),
                _from_col_blocks(g_cols[:, 512:768]))

    def dw_side_of(g):
        s_rows = jnp.concatenate([g[k].reshape(NDEV, 128, D) for k in ("dw_gla", "dw_out", "dw_pg")], axis=1)
        s_cols = jnp.concatenate([_col_blocks(g["dw_att"], 128), _col_blocks(g["dw_ple"], 128),
                                  jnp.pad(_col_blocks(g["dw2"][:GLR_N], 64), ((0, 0), (0, 0), (0, 64)))], axis=1)
        return _exchange_side([s_rows.astype(BF16), s_cols.astype(BF16)])

    def dh_side_of(dw_al):
        s_in = _from_aligned(dw_al).astype(BF16).reshape(NDEV, W_IN_SHARD, D)
        (from_sibling,) = _exchange_sibling([s_in], "exchange_sibling")
        core = lax.axis_index("c").astype(jnp.int32).reshape(1)
        return _chips_side([_pair_add(s_in, from_sibling, core, "pair_add")])

    loc = _local_step(x2, p2, pos, tgt, norm_g, qk_norm_q, qk_norm_k, gla_gate_b, gla_norm_g, ple_norm_g, w_al,
                      proj_side=_gather_side([rows3, cols3]), unpack=unpack, dw_side_of=dw_side_of,
                      dh_side_of=dh_side_of)
    loss_v, grad_x = loc["loss"], loc["grad_x"]
    dg_norm, dgq, dgk, dbg, dgn, dg_ple = (loc[k] for k in ("dg_norm", "dgq", "dgk", "dbg", "dgn", "dg_ple"))
    r_rows, r_cols = loc["dw_side"]
    (r_in,) = loc["dh_side"]

    small = jnp.concatenate([dg_norm[0], dgq[0, :HD], dgk[0, :HD], dbg[0], dgn[0], dg_ple[0]])
    small = jnp.pad(small, (0, SMALL_PAD - small.shape[0])).reshape(1, 8, SMALL_PAD // 8)
    (r_small,) = _comm_call(_gather_side([small]), "gather_small")

    outs = {}

    def adam(nm, parts, w, m, v, tr):
        outs[nm] = _adamw(parts, w, m, v, "adam_" + nm, tr)

    outs["w_in"] = [o.T for o in _adamw(r_in, wt, mt, vt, "adam_w_in", None, tc=128)]
    adam("w_gla_proj", r_rows[:, :128], w_gla_proj[0], m_w_gla_proj[0], v_w_gla_proj[0], 128)
    adam("w_out", r_rows[:, 128:256], w_out[0], m_w_out[0], v_w_out[0], 128)
    adam("w_ple_gate", r_rows[:, 256:], w_ple_gate[0], m_w_ple_gate[0], v_w_ple_gate[0], 128)
    adam("w_att_proj", r_cols[:, :512], w_att_proj[0], m_w_att_proj[0], v_w_att_proj[0], 512)
    adam("w_ple", r_cols[:, 512:768], w_ple[0], m_w_ple[0], v_w_ple[0], 256)
    adam("gla_gate_w2", r_cols[:, 768:784, :64], gla_gate_w2[0], m_gla_gate_w2[0], v_gla_gate_w2[0], 16)
    given = dict(norm_g=(norm_g, m_norm_g, v_norm_g), qk_norm_q=(qk_norm_q, m_qk_norm_q, v_qk_norm_q),
                 qk_norm_k=(qk_norm_k, m_qk_norm_k, v_qk_norm_k), gla_gate_b=(gla_gate_b, m_gla_gate_b, v_gla_gate_b),
                 gla_norm_g=(gla_norm_g, m_gla_norm_g, v_gla_norm_g), ple_norm_g=(ple_norm_g, m_ple_norm_g, v_ple_norm_g))

    def pack(i):
        flat = jnp.concatenate([given[nm][i][0] for nm, _ in SMALL])
        return jnp.pad(flat, (0, SMALL_PAD - flat.shape[0])).reshape(8, SMALL_PAD // 8)

    sm = _adamw(r_small.reshape(NDEV, 8, SMALL_PAD // 8), pack(0), pack(1), pack(2), "adam_small", 8)
    off = 0
    for nm, width in SMALL:
        outs[nm] = [o.reshape(-1)[off:off + width] for o in sm]
        off += width

    loss = lax.psum(loss_v[0, 0], ("x", "y", "c"))
    order = ["norm_g", "w_in", "qk_norm_q", "qk_norm_k", "gla_gate_w2", "gla_gate_b", "gla_norm_g", "w_att_proj",
             "w_gla_proj", "w_out", "ple_norm_g", "w_ple_gate", "w_ple"]
    result = [loss, grad_x[None]]
    for i in range(4):
        result += [outs[nm][i][None] for nm in order]
    return tuple(result)
```

```python
import functools

import jax
import jax.numpy as jnp
from jax import lax
from jax.experimental import pallas as pl
from jax.experimental.pallas import tpu as pltpu

F32 = jnp.float32
BF16 = jnp.bfloat16
S = jax.ShapeDtypeStruct

T = 4096
D = 1024
NDEV = 8
HD = 64
ATT_W = 512
ATT_QKV = 1536
DILATIONS = (1, 4, 16)
BLK = 128
GH, GDK, GDV = 4, 128, 256
GLA_C = 128
PLE = 256
EPS = 1e-6
ROT_DIM = 16
ROPE_THETA = 500000.0
GLA_TAU = 16.0
W_IN_COLS = 10256
W_IN_SHARD = 1282

C_QG, C_KG, C_VG, C_ZG, C_GLR, C_ZA, C_GA, C_GB, C_QA, C_KA, C_VA = (
    0, 512, 1024, 2048, 3072, 3584, 4096, 5120, 6144, 7680, 9216)
GLA_GROUP_W = 3584
GLR_W = 512
NCOL = 10752
GLR_N = 16
O_QA, O_ZA, O_QG, O_GLR, O_ZG, O_GA, O_END = 0, 4608, 5120, 7168, 7184, 8208, 10256

ADAM_LR, ADAM_B1, ADAM_B2, ADAM_EPS, ADAM_WD, ADAM_STEP = 0.001, 0.9, 0.999, 1e-08, 0.01, 10

MESH = pl.DeviceIdType.MESH


def _sigmoid(z):
    return 1.0 / (1.0 + jnp.exp(-z))


def _dot(a, b, dims):
    return lax.dot_general(a, b, (dims, ((), ())), preferred_element_type=F32)


def _nn(a, b):
    return _dot(a, b, ((1,), (0,)))


def _nt(a, b):
    return _dot(a, b, ((1,), (1,)))


def _tn(a, b):
    return _dot(a, b, ((0,), (0,)))


def _mm(a, b, *, mode, name, tm, tn, tk, out_dtype=F32, res=None, side=None):
    if mode == "nn":
        (m, k), n = a.shape, b.shape[1]
        a_spec = pl.BlockSpec((tm, tk), lambda i, j, l: (i, l))
        b_spec = pl.BlockSpec((tk, tn), lambda i, j, l: (l, j))
        dot = _nn
    elif mode == "nt":
        (m, k), n = a.shape, b.shape[0]
        a_spec = pl.BlockSpec((tm, tk), lambda i, j, l: (i, l))
        b_spec = pl.BlockSpec((tn, tk), lambda i, j, l: (j, l))
        dot = _nt
    else:
        (k, m), n = a.shape, b.shape[1]
        a_spec = pl.BlockSpec((tk, tm), lambda i, j, l: (l, i))
        b_spec = pl.BlockSpec((tk, tn), lambda i, j, l: (l, j))
        dot = _tn
    assert m % tm == 0 and n % tn == 0 and k % tk == 0, (name, m, n, k)
    grid = (m // tm, n // tn, k // tk)
    nk = grid[2]
    o_spec = pl.BlockSpec((tm, tn), lambda i, j, l: (i, j))
    in_specs = [a_spec, b_spec]
    args = [a, b]
    if res is not None:
        in_specs.append(o_spec)
        args.append(res)
    n_in = len(args)
    n_side = 0 if side is None else len(side["arrs"])
    hbm = pl.BlockSpec(memory_space=pl.ANY)

    def body(*refs):
        a_ref, b_ref = refs[0], refs[1]
        r_ref = refs[2] if res is not None else None
        o_ref = refs[n_in + n_side]
        scratch = refs[n_in + 2 * n_side + 1:]
        if side is not None:
            start, finish_side = side["plan"](refs[n_in:n_in + n_side], refs[n_in + n_side + 1:n_in + 2 * n_side + 1],
                                              *scratch[1 if nk > 1 else 0:])
            ids = [pl.program_id(d) for d in range(3)]

            @pl.when((ids[0] == 0) & (ids[1] == 0) & (ids[2] == 0))
            def _():
                start()

        part = dot(a_ref[...].astype(BF16), b_ref[...].astype(BF16))

        def finish(val):
            if r_ref is not None:
                val = val + r_ref[...]
            o_ref[...] = val.astype(out_dtype)

        if nk == 1:
            finish(part)
        else:
            acc = scratch[0]
            l = pl.program_id(2)

            @pl.when(l == 0)
            def _():
                acc[...] = part

            @pl.when(l > 0)
            def _():
                acc[...] += part

            @pl.when(l == nk - 1)
            def _():
                finish(acc[...])

        if side is not None:
            @pl.when((ids[0] == grid[0] - 1) & (ids[1] == grid[1] - 1) & (ids[2] == grid[2] - 1))
            def _():
                finish_side()

    sems = [] if side is None else side["scratch"]
    outs = pl.pallas_call(
        body, name=name, grid=grid,
        in_specs=in_specs + [hbm] * n_side, out_specs=[o_spec] + [hbm] * n_side,
        out_shape=[S((m, n), out_dtype)] + ([] if side is None else side["out_shape"]),
        scratch_shapes=([pltpu.VMEM((tm, tn), F32)] if nk > 1 else []) + sems,
        compiler_params=pltpu.CompilerParams(
            dimension_semantics=("arbitrary",) * 3 if side is not None else ("parallel", "parallel", "arbitrary")),
    )(*args, *([] if side is None else side["arrs"]))
    return outs[0] if side is None else (outs[0], outs[1:])


def _rows(arr, width=None, cblk=0):
    return ("rows", arr, arr.shape[1] if width is None else width, cblk)


def _whole(arr):
    return ("whole", arr)


def _rowcall(body, name, tt, ins, outs, scratch=()):
    in_specs, args = [], []
    for spec in ins:
        if spec[0] == "rows":
            _, arr, width, cblk = spec
            in_specs.append(pl.BlockSpec((tt, width), functools.partial(lambda i, c: (i, c), c=cblk)))
        else:
            arr = spec[1]
            in_specs.append(pl.BlockSpec(arr.shape, functools.partial(lambda i, nd: (0,) * nd, nd=arr.ndim)))
        args.append(arr)
    out_specs, out_shape = [], []
    for kind, shape, dtype in outs:
        if kind == "rows":
            out_specs.append(pl.BlockSpec((tt, shape), lambda i: (i, 0)))
            out_shape.append(S((T, shape), dtype))
        else:
            out_specs.append(pl.BlockSpec(shape, functools.partial(lambda i, nd: (0,) * nd, nd=len(shape))))
            out_shape.append(S(shape, dtype))
    return pl.pallas_call(
        body, name=name, grid=(T // tt,), in_specs=in_specs, out_specs=out_specs, out_shape=out_shape,
        scratch_shapes=list(scratch),
        compiler_params=pltpu.CompilerParams(dimension_semantics=("arbitrary",)),
    )(*args)


def _rms_fwd(x, g, name):
    def body(x_ref, g_ref, h_ref):
        xf = x_ref[...]
        r = lax.rsqrt(jnp.mean(xf * xf, axis=-1, keepdims=True) + EPS)
        h_ref[...] = (xf * r * g_ref[...]).astype(BF16)

    return _rowcall(body, name, 512, [_rows(x), _whole(g)], [("rows", D, BF16)])[0]


def _rms_bwd(dn, x, g, skip, name):
    def body(dn_ref, x_ref, g_ref, s_ref, dx_ref, dxb_ref, dg_ref):
        xf = x_ref[...]
        r = lax.rsqrt(jnp.mean(xf * xf, axis=-1, keepdims=True) + EPS)
        dn_ = dn_ref[...]
        u = dn_ * g_ref[...]
        dx = s_ref[...] + r * u - xf * (r * r * r) * jnp.mean(u * xf, axis=-1, keepdims=True)
        dx_ref[...] = dx
        dxb_ref[...] = dx.astype(BF16)
        part = jnp.sum(dn_ * xf * r, axis=0, keepdims=True)

        @pl.when(pl.program_id(0) == 0)
        def _():
            dg_ref[...] = part

        @pl.when(pl.program_id(0) > 0)
        def _():
            dg_ref[...] += part

    return _rowcall(body, name, 256, [_rows(dn), _rows(x), _whole(g), _rows(skip)],
                    [("rows", D, F32), ("rows", D, BF16), ("acc", (1, D), F32)])


def _rot_tables(pos_ref, inv_ref):
    lane = lax.broadcasted_iota(jnp.int32, (1, 128), 1) % HD
    ang = pos_ref[...] * inv_ref[...]
    cos, sin = jnp.cos(ang), jnp.sin(ang)
    c = jnp.where(lane < ROT_DIM, cos, 1.0)
    sp = jnp.where((lane >= ROT_DIM // 2) & (lane < ROT_DIM), sin, 0.0)
    sm = jnp.where(lane < ROT_DIM // 2, -sin, 0.0)
    return c, sp, sm


def _head_sums(v):
    same = (lax.broadcasted_iota(jnp.int32, (128, 128), 0) < HD) == (lax.broadcasted_iota(jnp.int32, (128, 128), 1) < HD)
    ones = jnp.where(same, 1.0, 0.0).astype(BF16)
    hi = v.astype(BF16)
    lo = (v - hi.astype(F32)).astype(BF16)
    return _nn(hi, ones) + _nn(lo, ones)


def _pair_norm(t):
    return lax.rsqrt(_head_sums(t * t) * (1.0 / HD) + EPS)


def _pair_mean(t):
    return _head_sums(t) * (1.0 / HD)


TT = 256
NCH = ATT_QKV // 128


def _res_shape(grp, dtype):
    return S((DILATIONS[grp], T // DILATIONS[grp], ATT_W), dtype)


def _res_spec(grp):
    dil = DILATIONS[grp]
    return pl.BlockSpec((dil, TT // dil, ATT_W), lambda i: (0, i, 0))


def _to_residues(sc, j, dst_ref, dil, cols):
    n = TT // dil
    for r in range(dil):
        rows = sc[j] if dil == 1 else sc.at[j][pl.ds(r, n, stride=dil), :]
        dst_ref[r, :, cols] = rows.astype(dst_ref.dtype)


def _from_residues(src_ref, cols, sc, j, dil):
    n = TT // dil
    for r in range(dil):
        if dil == 1:
            sc[j] = src_ref[r, :, cols]
        else:
            sc.at[j][pl.ds(r, n, stride=dil), :] = src_ref[r, :, cols]


def _tok_spec(width, cblk=0):
    return pl.BlockSpec((TT, width), functools.partial(lambda i, c: (i, c), c=cblk))


def _const_spec(arr_or_shape):
    shape = arr_or_shape if isinstance(arr_or_shape, tuple) else arr_or_shape.shape
    return pl.BlockSpec(shape, functools.partial(lambda i, nd: (0,) * nd, nd=len(shape)))


def _qk_prep(proj, pos, inv, gq, gk):
    def body(q_ref, k_ref, v_ref, pos_ref, inv_ref, gq_ref, gk_ref, *rest):
        outs, sc = rest[:9], rest[9]
        c, sp, sm = _rot_tables(pos_ref, inv_ref)
        for which, (src, g_ref) in enumerate(((q_ref, gq_ref), (k_ref, gk_ref), (v_ref, None))):
            for j in range(NCH):
                t = src[:, j * 128:(j + 1) * 128]
                if g_ref is not None:
                    n = t * _pair_norm(t) * g_ref[...]
                    t = n * c + pltpu.roll(n, 8, 1) * sp + pltpu.roll(n, 120, 1) * sm
                sc[j] = t
            for j in range(NCH):
                grp, sub = divmod(j * 128, ATT_W)
                _to_residues(sc, j, outs[which * 3 + grp], DILATIONS[grp], slice(sub, sub + 128))

    return pl.pallas_call(
        body, name="qk_prep", grid=(T // TT,),
        in_specs=[_tok_spec(ATT_QKV, C_QA // ATT_QKV), _tok_spec(ATT_QKV, C_KA // ATT_QKV),
                  _tok_spec(ATT_QKV, C_VA // ATT_QKV), _tok_spec(1), _const_spec(inv), _const_spec(gq), _const_spec(gk)],
        out_specs=[_res_spec(g) for _ in range(3) for g in range(3)],
        out_shape=[_res_shape(g, BF16) for _ in range(3) for g in range(3)],
        scratch_shapes=[pltpu.VMEM((NCH, TT, 128), F32)],
        compiler_params=pltpu.CompilerParams(dimension_semantics=("arbitrary",)),
    )(proj, proj, proj, pos, inv, gq, gk)


def _qk_bwd(proj, pos, inv, gq, gk, dqs, dks, dvs, dproj):
    const = lambda a: pl.BlockSpec(a.shape, functools.partial(lambda i, p, nd: (0,) * nd, nd=a.ndim))
    res = lambda g: pl.BlockSpec((DILATIONS[g], TT // DILATIONS[g], ATT_W), lambda i, p: (0, i, 0))
    base = C_QA // ATT_QKV

    def body(t_ref, pos_ref, inv_ref, gq_ref, gk_ref, dq0, dq1, dq2, dk0, dk1, dk2, dv0, dv1, dv2, buf_ref,
             out_ref, dgq_ref, dgk_ref, sc):
        del buf_ref
        part = pl.program_id(1)
        first = pl.program_id(0) == 0

        def gather(drefs):
            for j in range(NCH):
                grp, sub = divmod(j * 128, ATT_W)
                _from_residues(drefs[grp], slice(sub, sub + 128), sc, j, DILATIONS[grp])

        def normed(g_ref, drefs, dg_ref):
            c, sp, sm = _rot_tables(pos_ref, inv_ref)
            gather(drefs)
            dg = jnp.zeros((1, 128), F32)
            for j in range(NCH):
                cols = slice(j * 128, (j + 1) * 128)
                d_rot = sc[j]
                dn = d_rot * c + pltpu.roll(d_rot * sp, 120, 1) + pltpu.roll(d_rot * sm, 8, 1)
                t = t_ref[:, cols]
                r = _pair_norm(t)
                u = dn * g_ref[...]
                out_ref[:, cols] = (r * u - t * (r * r * r) * _pair_mean(u * t)).astype(BF16)
                dg = dg + jnp.sum(dn * t * r, axis=0, keepdims=True)
            dg = dg + pltpu.roll(dg, HD, 1)

            @pl.when(first)
            def _():
                dg_ref[...] = dg

            @pl.when(jnp.logical_not(first))
            def _():
                dg_ref[...] += dg

        @pl.when(part == 0)
        def _():
            gather((dv0, dv1, dv2))
            for j in range(NCH):
                out_ref[:, j * 128:(j + 1) * 128] = sc[j].astype(BF16)

        @pl.when(part == 1)
        def _():
            normed(gq_ref, (dq0, dq1, dq2), dgq_ref)

        @pl.when(part == 2)
        def _():
            normed(gk_ref, (dk0, dk1, dk2), dgk_ref)

    keep = pl.BlockSpec((1, 128), lambda i, p: (0, 0))
    return pl.pallas_call(
        body, name="qk_bwd", grid=(T // TT, 3),
        in_specs=[pl.BlockSpec((TT, ATT_QKV), lambda i, p: (i, base + jnp.maximum(p - 1, 0))),
                  pl.BlockSpec((TT, 1), lambda i, p: (i, 0)), const(inv), const(gq), const(gk)]
        + [res(g) for _ in range(3) for g in range(3)] + [pl.BlockSpec(memory_space=pl.ANY)],
        out_specs=[pl.BlockSpec((TT, ATT_QKV), lambda i, p: (i, base + jnp.where(p == 0, 2, p - 1))), keep, keep],
        out_shape=[S(dproj.shape, dproj.dtype), S((1, 128), F32), S((1, 128), F32)],
        input_output_aliases={14: 0},
        scratch_shapes=[pltpu.VMEM((NCH, TT, 128), F32)],
        compiler_params=pltpu.CompilerParams(dimension_semantics=("arbitrary", "arbitrary")),
    )(proj, pos, inv, gq, gk, *dqs, *dks, *dvs, dproj)


def _split_heads(t):
    low = lax.broadcasted_iota(jnp.int32, (1, 128), 1) < HD
    zero = jnp.zeros_like(t)
    return jnp.concatenate([jnp.where(low, t, zero), jnp.where(low, zero, t)], axis=0)


def _join_heads(t2):
    low = lax.broadcasted_iota(jnp.int32, (1, 128), 1) < HD
    n = t2.shape[0] // 2
    return jnp.where(low, t2[:n], t2[n:])


def _band_mask4(has_before, has_own):
    row = lax.broadcasted_iota(jnp.int32, (BLK, 4 * BLK), 0)
    lane = lax.broadcasted_iota(jnp.int32, (BLK, 4 * BLK), 1)
    key = lane & (BLK - 1)
    own = lane >= 2 * BLK
    return (own & (key <= row) & has_own) | (jnp.logical_not(own) & (key >= row) & has_before)


def _band_mask_before(has_before):
    row = lax.broadcasted_iota(jnp.int32, (BLK, 2 * BLK), 0)
    key = lax.broadcasted_iota(jnp.int32, (BLK, 2 * BLK), 1) & (BLK - 1)
    return (key >= row) & has_before


def _is_head_a4():
    lane = lax.broadcasted_iota(jnp.int32, (1, 4 * BLK), 1)
    return (lane & BLK) == 0


def _per_head4(col_a, col_b):
    return jnp.where(_is_head_a4(), col_a, col_b)


def _spread(t, pick, width):
    src = lax.broadcasted_iota(jnp.int32, (128, width), 0)
    dst_a = (lax.broadcasted_iota(jnp.int32, (128, width), 1) & BLK) == 0
    if pick:
        sel = (dst_a & (src == 0)) | (jnp.logical_not(dst_a) & (src == HD))
    else:
        sel = dst_a == (src < HD)
    sel = jnp.where(sel, 1.0, 0.0).astype(BF16)
    hi, mid, lo = _split3(t)
    return _nn(hi, sel) + _nn(mid, sel) + _nn(lo, sel)


def _att_fwd(q, k, v, grp, name):
    dil = DILATIONS[grp]
    nb = T // dil // BLK
    scale = HD ** -0.5

    def body(q_ref, kp_ref, kc_ref, vp_ref, vc_ref, o_ref, lse_ref):
        mask = _band_mask4(pl.program_id(1) > 0, True)
        low = lax.broadcasted_iota(jnp.int32, (1, 128), 1) < HD
        ones = jnp.ones((BLK, 128), BF16)
        ones4 = jnp.concatenate([_split_heads(ones), _split_heads(ones)], axis=0)
        for j in range(ATT_W // 128):
            cols = slice(j * 128, (j + 1) * 128)
            k4 = jnp.concatenate([_split_heads(kp_ref[:, cols]), _split_heads(kc_ref[:, cols])], axis=0)
            v4 = jnp.concatenate([_split_heads(vp_ref[:, cols]), _split_heads(vc_ref[:, cols])], axis=0)
            s = jnp.where(mask, _nt(q_ref[:, cols], k4) * scale, -jnp.inf)
            mx = [jnp.maximum(jnp.max(s[:, h * BLK:(h + 1) * BLK], axis=-1, keepdims=True),
                              jnp.max(s[:, (h + 2) * BLK:(h + 3) * BLK], axis=-1, keepdims=True)) for h in range(2)]
            p = jnp.exp(s - _per_head4(*mx)).astype(BF16)
            den = _nn(p, ones4)
            o_ref[:, cols] = _nn(p, v4) / den
            lse_ref[:, cols] = jnp.where(low, mx[0], mx[1]) + jnp.log(den)

    cur = pl.BlockSpec((None, BLK, ATT_W), lambda r, i: (r, i, 0))
    prev = pl.BlockSpec((None, BLK, ATT_W), lambda r, i: (r, jnp.maximum(i - 1, 0), 0))
    return pl.pallas_call(
        body, name=name, grid=(dil, nb),
        in_specs=[cur, prev, cur, prev, cur],
        out_specs=[cur, cur], out_shape=[_res_shape(grp, F32)] * 2,
        compiler_params=pltpu.CompilerParams(dimension_semantics=("parallel", "arbitrary")),
    )(q, k, k, v, v)


def _att_bwd(q, k, v, datt, att, lse, grp, name):
    dil = DILATIONS[grp]
    nb = T // dil // BLK
    scale = HD ** -0.5

    def body(q0_ref, q1_ref, kp_ref, kc_ref, vp_ref, vc_ref, do0_ref, do1_ref, o0_ref, o1_ref, l0_ref, l1_ref,
             dq_ref, dk_ref, dv_ref):
        i = pl.program_id(1)
        mask_mine = _band_mask4(i > 0, True)
        mask_next = _band_mask_before(i < nb - 1)

        def pair(q, keys, vals, do_ref, o_ref, l_ref, cols, mask):
            width = mask.shape[1]
            dsum = _spread(do_ref[:, cols] * o_ref[:, cols], False, width)
            lse_ = _spread(l_ref[:, cols], True, width)
            p = jnp.where(mask, jnp.exp(_nt(q, keys) * scale - lse_), 0.0)
            ds = p * (_nt(do_ref[:, cols].astype(BF16), vals) - dsum) * scale
            return p.astype(BF16), ds.astype(BF16)

        for j in range(ATT_W // 128):
            cols = slice(j * 128, (j + 1) * 128)
            q0, q1 = q0_ref[:, cols], q1_ref[:, cols]
            kc2, vc2 = _split_heads(kc_ref[:, cols]), _split_heads(vc_ref[:, cols])
            k4 = jnp.concatenate([_split_heads(kp_ref[:, cols]), kc2], axis=0)
            v4 = jnp.concatenate([_split_heads(vp_ref[:, cols]), vc2], axis=0)
            p0, ds0 = pair(q0, k4, v4, do0_ref, o0_ref, l0_ref, cols, mask_mine)
            p1, ds1 = pair(q1, kc2, vc2, do1_ref, o1_ref, l1_ref, cols, mask_next)
            dq_ref[:, cols] = _nn(ds0, k4)
            both = lambda a0, a1: jnp.concatenate([a0[:, 2 * BLK:], a1], axis=0)
            qq = jnp.concatenate([q0, q1], axis=0)
            dd = jnp.concatenate([do0_ref[:, cols], do1_ref[:, cols]], axis=0).astype(BF16)
            dk_ref[:, cols] = _join_heads(_tn(both(ds0, ds1), qq))
            dv_ref[:, cols] = _join_heads(_tn(both(p0, p1), dd))

    def spec(shift):
        return pl.BlockSpec((None, BLK, ATT_W), lambda r, i: (r, jnp.clip(i + shift, 0, nb - 1), 0))

    here, after, before = spec(0), spec(1), spec(-1)
    return pl.pallas_call(
        body, name=name, grid=(dil, nb),
        in_specs=[here, after, before, here, before, here, here, after, here, after, here, after],
        out_specs=[here] * 3, out_shape=[_res_shape(grp, F32)] * 3,
        compiler_params=pltpu.CompilerParams(dimension_semantics=("parallel", "arbitrary")),
    )(q, q, k, k, v, v, datt, datt, att, att, lse, lse)


def _att_merge(os_, lses, proj):
    nq = ATT_W // 128

    def body(o0, o1, o2, l0, l1, l2, za_ref, att_ref, lse_ref, ain_ref, sc):
        for a, ref in enumerate((o0, o1, o2, l0, l1, l2)):
            for j in range(nq):
                _from_residues(ref, slice(j * 128, (j + 1) * 128), sc, a * nq + j, DILATIONS[a % 3])
        for j in range(nq):
            cols = slice(j * 128, (j + 1) * 128)
            oa, ob, oc = (sc[a * nq + j] for a in range(3))
            la, lb, lc = (sc[(3 + a) * nq + j] for a in range(3))
            m = jnp.maximum(jnp.maximum(la, lb), lc)
            wa, wb, wc = jnp.exp(la - m), jnp.exp(lb - m), jnp.exp(lc - m)
            tot = wa + wb + wc
            att = (wa * oa + wb * ob + wc * oc) / tot
            att_ref[:, cols] = att
            lse_ref[:, cols] = m + jnp.log(tot)
            za = za_ref[:, cols]
            ain_ref[:, cols] = (att * za * _sigmoid(za)).astype(BF16)

    return pl.pallas_call(
        body, name="att_merge", grid=(T // TT,),
        in_specs=[_res_spec(g) for _ in range(2) for g in range(3)] + [_tok_spec(ATT_W, C_ZA // ATT_W)],
        out_specs=[_tok_spec(ATT_W)] * 3,
        out_shape=[S((T, ATT_W), F32), S((T, ATT_W), F32), S((T, ATT_W), BF16)],
        scratch_shapes=[pltpu.VMEM((6 * nq, TT, 128), F32)],
        compiler_params=pltpu.CompilerParams(dimension_semantics=("arbitrary",)),
    )(*os_, *lses, proj)


def _att_gate_bwd(dain, att, lse, proj, dproj):
    nq = ATT_W // 128

    def body(d_ref, att_ref, lse_ref, za_ref, buf_ref, dza_ref, da0, da1, da2, at1, at2, ls1, ls2, sc):
        del buf_ref
        for j in range(nq):
            cols = slice(j * 128, (j + 1) * 128)
            za = za_ref[:, cols]
            sg = _sigmoid(za)
            d = d_ref[:, cols].astype(F32)
            att_ = att_ref[:, cols]
            dza_ref[:, cols] = (d * att_ * sg * (1.0 + za * (1.0 - sg))).astype(BF16)
            sc[j] = d * za * sg
            sc[nq + j] = att_
            sc[2 * nq + j] = lse_ref[:, cols]
        for j in range(nq):
            cols = slice(j * 128, (j + 1) * 128)
            for grp, dst in enumerate((da0, da1, da2)):
                _to_residues(sc, j, dst, DILATIONS[grp], cols)
            for grp, dst in ((1, at1), (2, at2)):
                _to_residues(sc, nq + j, dst, DILATIONS[grp], cols)
            for grp, dst in ((1, ls1), (2, ls2)):
                _to_residues(sc, 2 * nq + j, dst, DILATIONS[grp], cols)

    res = (0, 1, 2, 1, 2, 1, 2)
    return pl.pallas_call(
        body, name="att_gate_bwd", grid=(T // TT,),
        in_specs=[_tok_spec(ATT_W)] * 3 + [_tok_spec(ATT_W, C_ZA // ATT_W), pl.BlockSpec(memory_space=pl.ANY)],
        out_specs=[_tok_spec(ATT_W, C_ZA // ATT_W)] + [_res_spec(g) for g in res],
        out_shape=[S(dproj.shape, dproj.dtype)] + [_res_shape(g, F32) for g in res],
        input_output_aliases={4: 0},
        scratch_shapes=[pltpu.VMEM((3 * nq, TT, 128), F32)],
        compiler_params=pltpu.CompilerParams(dimension_semantics=("arbitrary",)),
    )(dain, att, lse, proj, dproj)


def _split3(v):
    hi = v.astype(BF16)
    r1 = v - hi.astype(F32)
    mid = r1.astype(BF16)
    lo = (r1 - mid.astype(F32)).astype(BF16)
    return hi, mid, lo


def _tri_sum(v, upper):
    n = v.shape[0]
    row = lax.broadcasted_iota(jnp.int32, (n, n), 0)
    col = lax.broadcasted_iota(jnp.int32, (n, n), 1)
    tri = jnp.where(col >= row if upper else col <= row, 1.0, 0.0).astype(BF16)
    hi, mid, lo = _split3(v)
    return _nn(tri, hi) + _nn(tri, mid) + _nn(tri, lo)


def _gla_gates(glr_ref, w2_ref, b_ref):
    logit = _nn(glr_ref[...].astype(BF16), w2_ref[...]) + b_ref[...]
    lg = (jnp.minimum(logit, 0.0) - jnp.log(1.0 + jnp.exp(-jnp.abs(logit)))) * (1.0 / GLA_TAU)
    return logit, _tri_sum(lg, upper=False)


def _gla_head(cum, q_ref, k_ref, h):
    cols = slice(h * GDK, (h + 1) * GDK)
    b = cum[:, cols]
    last = b[GLA_C - 1:GLA_C, :]
    e_pos = jnp.exp(b)
    e_neg = jnp.exp(-b)
    e_end = jnp.exp(last - b)
    qt = q_ref[:, cols] * (GDK ** -0.5) * e_pos
    kt = k_ref[:, cols] * e_neg
    kh = k_ref[:, cols] * e_end
    return b, last, e_pos, e_neg, e_end, qt, kt, kh


def _causal(n):
    return lax.broadcasted_iota(jnp.int32, (n, n), 1) <= lax.broadcasted_iota(jnp.int32, (n, n), 0)


def _gla_fwd(proj, w2p, bg, gn):
    nc = T // GLA_C

    def body(q_ref, k_ref, v_ref, glr_ref, zg_ref, w2_ref, b_ref, gn_ref, o_ref, bin_ref, st_ref, state):
        @pl.when(pl.program_id(0) == 0)
        def _():
            state[...] = jnp.zeros_like(state)

        _, cum = _gla_gates(glr_ref, w2_ref, b_ref)
        for h in range(GH):
            _, last, _, _, _, qt, kt, kh = _gla_head(cum, q_ref, k_ref, h)
            vcols = slice(h * GDV, (h + 1) * GDV)
            st = state[h]
            st_ref[0, h] = st
            v = v_ref[:, vcols].astype(BF16)
            qb = qt.astype(BF16)
            a = jnp.where(_causal(GLA_C), _nt(qb, kt.astype(BF16)), 0.0)
            o = _nt(qb, st.astype(BF16)) + _nn(a.astype(BF16), v)
            state[h] = st * jnp.exp(last) + _tn(v, kh.astype(BF16))
            o_ref[:, vcols] = o
            r = lax.rsqrt(jnp.mean(o * o, axis=-1, keepdims=True) + EPS)
            zg = zg_ref[:, vcols]
            bin_ref[:, vcols] = (o * r * gn_ref[...] * zg * _sigmoid(zg)).astype(BF16)

    row = lambda width, cblk: pl.BlockSpec((GLA_C, width), functools.partial(lambda i, c: (i, c), c=cblk))
    full = lambda a: pl.BlockSpec(a.shape, functools.partial(lambda i, nd: (0,) * nd, nd=a.ndim))
    return pl.pallas_call(
        body, name="gla_fwd", grid=(nc,),
        in_specs=[row(512, C_QG // 512), row(512, C_KG // 512), row(1024, C_VG // 1024), row(GLR_W, C_GLR // GLR_W),
                  row(1024, C_ZG // 1024), full(w2p), full(bg), full(gn)],
        out_specs=[pl.BlockSpec((GLA_C, GH * GDV), lambda i: (i, 0)), pl.BlockSpec((GLA_C, GH * GDV), lambda i: (i, 0)),
                   pl.BlockSpec((1, GH, GDV, GDK), lambda i: (i, 0, 0, 0))],
        out_shape=[S((T, GH * GDV), F32), S((T, GH * GDV), BF16), S((nc, GH, GDV, GDK), F32)],
        scratch_shapes=[pltpu.VMEM((GH, GDV, GDK), F32)],
        compiler_params=pltpu.CompilerParams(dimension_semantics=("arbitrary",)),
    )(proj, proj, proj, proj, proj, w2p, bg, gn)


def _gla_bwd(proj, w2p, bg, gn, o_gla, states, dbin, dproj):
    nc = T // GLA_C

    def body(q_ref, k_ref, v_ref, glr_ref, zg_ref, w2_ref, b_ref, gn_ref, o_ref, st_ref, dbin_ref, buf_ref,
             out_ref, dw2_ref, dbg_ref, dgn_ref, dstate, dlogit):
        del buf_ref
        dq_ref = out_ref.at[:, C_QG:C_KG]
        dk_ref = out_ref.at[:, C_KG:C_VG]
        dv_ref = out_ref.at[:, C_VG:C_ZG]
        dzg_ref = out_ref.at[:, C_ZG:C_GLR]
        dglr_ref = out_ref.at[:, C_GLR:C_GLR + GLR_W]
        first = pl.program_id(0) == 0

        @pl.when(first)
        def _():
            dstate[...] = jnp.zeros_like(dstate)

        logit, cum = _gla_gates(glr_ref, w2_ref, b_ref)
        is_last = lax.broadcasted_iota(jnp.int32, (GLA_C, 1), 0) == GLA_C - 1
        dgn = jnp.zeros((1, GDV), F32)
        for h in range(GH):
            _, last, e_pos, e_neg, e_end, qt, kt, kh = _gla_head(cum, q_ref, k_ref, h)
            cols = slice(h * GDK, (h + 1) * GDK)
            vcols = slice(h * GDV, (h + 1) * GDV)
            o = o_ref[:, vcols]
            r = lax.rsqrt(jnp.mean(o * o, axis=-1, keepdims=True) + EPS)
            zg = zg_ref[:, vcols]
            sg = _sigmoid(zg)
            db_ = dbin_ref[:, vcols].astype(F32)
            dlin = db_ * zg * sg
            dzg_ref[:, vcols] = (db_ * (o * r * gn_ref[...]) * sg * (1.0 + zg * (1.0 - sg))).astype(BF16)
            u = dlin * gn_ref[...]
            do = (r * u - o * (r * r * r) * jnp.mean(u * o, axis=-1, keepdims=True)).astype(BF16)
            dgn = dgn + jnp.sum(dlin * o * r, axis=0, keepdims=True)
            st = st_ref[0, h]
            dst = dstate[h]
            v = v_ref[:, vcols].astype(BF16)
            qb, kb, khb = qt.astype(BF16), kt.astype(BF16), kh.astype(BF16)
            dstb = dst.astype(BF16)
            causal = _causal(GLA_C)
            a = jnp.where(causal, _nt(qb, kb), 0.0).astype(BF16)
            da = jnp.where(causal, _nt(do, v), 0.0).astype(BF16)
            dqt = _nn(do, st.astype(BF16)) + _nn(da, kb)
            dkt = _tn(da, qb)
            dkh = _nn(v, dstb)
            dv_ref[:, vcols] = (_tn(a, do) + _nt(khb, dstb)).astype(BF16)
            lam = jnp.exp(last)
            dlam = jnp.sum(dst * st, axis=0, keepdims=True)
            dstate[h] = dst * lam + _tn(do, qb)
            dq_ref[:, cols] = (dqt * e_pos * (GDK ** -0.5)).astype(BF16)
            dk_ref[:, cols] = (dkt * e_neg + dkh * e_end).astype(BF16)
            dkh_kh = dkh * kh
            dcum = dqt * qt - dkt * kt - dkh_kh
            dlast = jnp.sum(dkh_kh, axis=0, keepdims=True) + dlam * lam
            dcum = jnp.where(is_last, dcum + dlast, dcum)
            dlg = _tri_sum(dcum, upper=True)
            dlogit[:, cols] = dlg * (1.0 / GLA_TAU) * (1.0 - _sigmoid(logit[:, cols]))

        dl = dlogit[...]
        dlb = dl.astype(BF16)
        dglr_ref[...] = _nt(dlb, w2_ref[...]).astype(BF16)
        dw2 = _tn(glr_ref[...].astype(BF16), dlb)
        dbg = jnp.sum(dl, axis=0, keepdims=True)

        @pl.when(first)
        def _():
            dw2_ref[...] = dw2
            dbg_ref[...] = dbg
            dgn_ref[...] = dgn

        @pl.when(jnp.logical_not(first))
        def _():
            dw2_ref[...] += dw2
            dbg_ref[...] += dbg
            dgn_ref[...] += dgn

    rev = lambda i: nc - 1 - i
    row = lambda width, cblk: pl.BlockSpec((GLA_C, width), functools.partial(lambda i, c: (rev(i), c), c=cblk))
    full = lambda a: pl.BlockSpec(a.shape, functools.partial(lambda i, nd: (0,) * nd, nd=a.ndim))
    keep = lambda shape: pl.BlockSpec(shape, functools.partial(lambda i, nd: (0,) * nd, nd=len(shape)))
    return pl.pallas_call(
        body, name="gla_bwd", grid=(nc,),
        in_specs=[row(512, C_QG // 512), row(512, C_KG // 512), row(1024, C_VG // 1024), row(GLR_W, C_GLR // GLR_W),
                  row(1024, C_ZG // 1024), full(w2p), full(bg), full(gn), row(GH * GDV, 0),
                  pl.BlockSpec((1, GH, GDV, GDK), lambda i: (rev(i), 0, 0, 0)), row(GH * GDV, 0),
                  pl.BlockSpec(memory_space=pl.ANY)],
        out_specs=[row(GLA_GROUP_W, 0), keep((GLR_W, 512)), keep((1, 512)), keep((1, GDV))],
        out_shape=[S(dproj.shape, dproj.dtype), S((GLR_W, 512), F32), S((1, 512), F32), S((1, GDV), F32)],
        input_output_aliases={11: 0},
        scratch_shapes=[pltpu.VMEM((GH, GDV, GDK), F32), pltpu.VMEM((GLA_C, GH * GDK), F32)],
        compiler_params=pltpu.CompilerParams(dimension_semantics=("arbitrary",)),
    )(proj, proj, proj, proj, proj, w2p, bg, gn, o_gla, states, dbin, dproj)


def _merge_fwd(ya, yb, proj):
    def body(ya_ref, yb_ref, ga_ref, gb_ref, y_ref):
        y_ref[...] = (_sigmoid(ga_ref[...]) * ya_ref[...].astype(F32)
                      + _sigmoid(gb_ref[...]) * yb_ref[...].astype(F32)).astype(BF16)

    return _rowcall(body, "merge_fwd", 512,
                    [_rows(ya), _rows(yb), _rows(proj, D, C_GA // D), _rows(proj, D, C_GB // D)],
                    [("rows", D, BF16)])[0]


def _merge_bwd(dy, ya, yb, proj):
    tt = 512

    def body(dy_ref, ya_ref, yb_ref, g_ref, dg_ref, dya_ref, dyb_ref):
        dy_ = dy_ref[...].astype(F32)
        sa, sb = _sigmoid(g_ref[:, :D]), _sigmoid(g_ref[:, D:])
        dg_ref[:, :D] = (dy_ * ya_ref[...].astype(F32) * sa * (1.0 - sa)).astype(BF16)
        dg_ref[:, D:] = (dy_ * yb_ref[...].astype(F32) * sb * (1.0 - sb)).astype(BF16)
        dya_ref[...] = (dy_ * sa).astype(BF16)
        dyb_ref[...] = (dy_ * sb).astype(BF16)

    tok = pl.BlockSpec((tt, D), lambda i: (i, 0))
    gates = pl.BlockSpec((tt, 2 * D), lambda i: (i, C_GA // (2 * D)))
    return pl.pallas_call(
        body, name="merge_bwd", grid=(T // tt,),
        in_specs=[tok, tok, tok, gates], out_specs=[gates, tok, tok],
        out_shape=[S((T, NCOL), BF16), S((T, D), BF16), S((T, D), BF16)],
        compiler_params=pltpu.CompilerParams(dimension_semantics=("arbitrary",)),
    )(dy, ya, yb, proj)


def _loss_head(x1, e, u, target):
    def body(x1_ref, e_ref, u_ref, t_ref, loss_ref, dout_ref, de_ref, du_ref, acc):
        first = pl.program_id(0) == 0
        pg = _sigmoid(u_ref[...])
        e_ = e_ref[...]
        diff = x1_ref[...] + e_ * pg - t_ref[...]
        part = jnp.sum(diff * diff, axis=0, keepdims=True)

        @pl.when(first)
        def _():
            acc[...] = part

        @pl.when(jnp.logical_not(first))
        def _():
            acc[...] += part

        dout = diff * (1.0 / D)
        dout_ref[...] = dout
        de_ref[...] = (dout * pg).astype(BF16)
        du_ref[...] = (dout * e_ * pg * (1.0 - pg)).astype(BF16)
        loss_ref[...] = jnp.zeros((1, 128), F32) + jnp.sum(acc[...], axis=-1, keepdims=True) * (0.5 / D)

    return _rowcall(body, "loss_head", 256, [_rows(x1), _rows(e), _rows(u), _rows(target)],
                    [("acc", (1, 128), F32), ("rows", D, F32), ("rows", D, BF16), ("rows", D, BF16)],
                    scratch=[pltpu.VMEM((1, D), F32)])


def _peer(k):
    x, y, c = lax.axis_index("x"), lax.axis_index("y"), lax.axis_index("c")
    return (x ^ ((k >> 2) & 1), y ^ ((k >> 1) & 1), c ^ (k & 1))


def _my_index():
    return 4 * lax.axis_index("x") + 2 * lax.axis_index("y") + lax.axis_index("c")


def _peer_index(k):
    px, py, pc = _peer(k)
    return 4 * px + 2 * py + pc


def _pairwise_plan(src_of, dst_of, landed_of, own_src, own_dst):
    def plan(ins, outs, send, recv, local):
        n = len(ins)

        def own():
            return [pltpu.make_async_copy(own_src(ins[a]), own_dst(outs[a]), local.at[a]) for a in range(n)]

        def remote(k, a, src, dst):
            return pltpu.make_async_remote_copy(src_ref=src, dst_ref=dst, send_sem=send.at[k - 1, a],
                                                recv_sem=recv.at[k - 1, a], device_id=_peer(k), device_id_type=MESH)

        def sent():
            return [remote(k, a, src_of(ins[a], k), dst_of(outs[a])) for k in range(1, NDEV) for a in range(n)]

        def start():
            for cp in own() + sent():
                cp.start()

        def finish():
            for k in range(1, NDEV):
                for a in range(n):
                    remote(k, a, own_src(ins[a]), landed_of(outs[a], k)).wait_recv()
            for cp in sent():
                cp.wait_send()
            for cp in own():
                cp.wait()

        return start, finish

    return plan


def _pairwise_sems(n):
    return [pltpu.SemaphoreType.DMA((NDEV - 1, n)), pltpu.SemaphoreType.DMA((NDEV - 1, n)),
            pltpu.SemaphoreType.DMA((n,))]


def _gather_side(arrs):
    plan = _pairwise_plan(src_of=lambda i, k: i, dst_of=lambda o: o.at[_my_index()],
                          landed_of=lambda o, k: o.at[_peer_index(k)],
                          own_src=lambda i: i, own_dst=lambda o: o.at[_my_index()])
    return dict(arrs=arrs, out_shape=[S((NDEV,) + a.shape, a.dtype) for a in arrs],
                scratch=_pairwise_sems(len(arrs)), plan=plan)


def _exchange_side(arrs):
    plan = _pairwise_plan(src_of=lambda i, k: i.at[_peer_index(k)], dst_of=lambda o: o.at[_my_index()],
                          landed_of=lambda o, k: o.at[_peer_index(k)],
                          own_src=lambda i: i.at[_my_index()], own_dst=lambda o: o.at[_my_index()])
    return dict(arrs=arrs, out_shape=[S(a.shape, a.dtype) for a in arrs], scratch=_pairwise_sems(len(arrs)), plan=plan)


def _comm_call(side, name):
    n = len(side["arrs"])

    def body(*refs):
        start, finish = side["plan"](refs[:n], refs[n:2 * n], *refs[2 * n:])
        start()
        finish()

    hbm = pl.BlockSpec(memory_space=pl.ANY)
    return pl.pallas_call(body, name=name, in_specs=[hbm] * n, out_specs=[hbm] * n, out_shape=side["out_shape"],
                          scratch_shapes=side["scratch"])(*side["arrs"])


def _all_gather_by_chip(arrs, name):
    n = len(arrs)

    def body(*refs):
        ins, outs = refs[:n], refs[n:2 * n]
        send, recv, local = refs[2 * n:]
        x, y, c = lax.axis_index("x"), lax.axis_index("y"), lax.axis_index("c")
        me, sibling = (x, y, c), (x, y, 1 - c)
        chips = [(1 - x, y), (x, 1 - y), (1 - x, 1 - y)]

        def copy(k, a, block, to, src=None):
            px, py, pc = block
            slot = outs[a].at[4 * px + 2 * py + pc]
            return pltpu.make_async_remote_copy(
                src_ref=slot if src is None else src, dst_ref=slot, send_sem=send.at[k, a], recv_sem=recv.at[k, a],
                device_id=to, device_id_type=MESH)

        mine = [pltpu.make_async_copy(ins[a], outs[a].at[4 * x + 2 * y + c], local.at[a]) for a in range(n)]
        first = []
        for a in range(n):
            first.append(copy(0, a, me, sibling, src=ins[a]))
            first += [copy(1 + j, a, me, (*chip, c), src=ins[a]) for j, chip in enumerate(chips)]
        for cp in mine + first:
            cp.start()
        passed = []
        for j, chip in enumerate(chips):
            for a in range(n):
                copy(1 + j, a, (*chip, c), me).wait_recv()
                passed.append(copy(4 + j, a, (*chip, c), sibling))
                passed[-1].start()
        for a in range(n):
            copy(0, a, sibling, me).wait_recv()
        for j, chip in enumerate(chips):
            for a in range(n):
                copy(4 + j, a, (*chip, 1 - c), me).wait_recv()
        for cp in first + passed:
            cp.wait_send()
        for cp in mine:
            cp.wait()

    hbm = pl.BlockSpec(memory_space=pl.ANY)
    return pl.pallas_call(
        body, name=name, in_specs=[hbm] * n, out_specs=[hbm] * n,
        out_shape=[S((NDEV,) + a.shape, a.dtype) for a in arrs],
        scratch_shapes=[pltpu.SemaphoreType.DMA((NDEV - 1, n)), pltpu.SemaphoreType.DMA((NDEV - 1, n)),
                        pltpu.SemaphoreType.DMA((n,))],
    )(*arrs)


NCHIP = 4


def _exchange_sibling(arrs, name):
    n = len(arrs)

    def body(*refs):
        ins, outs = refs[:n], refs[n:2 * n]
        send, recv = refs[2 * n:]
        x, y, c = lax.axis_index("x"), lax.axis_index("y"), lax.axis_index("c")
        copies = []
        for q in range(NCHIP):
            for a in range(n):
                copies.append(pltpu.make_async_remote_copy(
                    src_ref=ins[a].at[2 * q + (1 - c)], dst_ref=outs[a].at[q], send_sem=send.at[q, a],
                    recv_sem=recv.at[q, a], device_id=(x, y, 1 - c), device_id_type=MESH))
        for cp in copies:
            cp.start()
        for cp in copies:
            cp.wait_recv()
        for cp in copies:
            cp.wait_send()

    hbm = pl.BlockSpec(memory_space=pl.ANY)
    return pl.pallas_call(
        body, name=name, in_specs=[hbm] * n, out_specs=[hbm] * n,
        out_shape=[S((NCHIP,) + a.shape[1:], a.dtype) for a in arrs],
        scratch_shapes=[pltpu.SemaphoreType.DMA((NCHIP, n)), pltpu.SemaphoreType.DMA((NCHIP, n))],
    )(*arrs)


def _pair_add(mine, got, core, name):
    _, rows, cols = mine.shape
    tc = 256
    assert cols % tc == 0

    def body(core_ref, a_ref, b_ref, o_ref):
        o_ref[...] = (a_ref[...].astype(F32) + b_ref[...].astype(F32)).astype(BF16)

    return pl.pallas_call(
        body, name=name,
        grid_spec=pltpu.PrefetchScalarGridSpec(
            num_scalar_prefetch=1, grid=(NCHIP, cols // tc),
            in_specs=[pl.BlockSpec((None, rows, tc), lambda q, i, core_ref: (2 * q + core_ref[0], 0, i)),
                      pl.BlockSpec((None, rows, tc), lambda q, i, core_ref: (q, 0, i))],
            out_specs=pl.BlockSpec((None, rows, tc), lambda q, i, core_ref: (q, 0, i))),
        out_shape=S((NCHIP, rows, cols), BF16),
    )(core, mine, got)


def _chips_side(arrs):
    def plan(ins, outs, send, recv, local):
        n = len(ins)

        def places():
            x, y, c = lax.axis_index("x"), lax.axis_index("y"), lax.axis_index("c")
            return 2 * x + y, c, [(1 - x, y), (x, 1 - y), (1 - x, 1 - y)]

        def own():
            here, _, _ = places()
            return [pltpu.make_async_copy(ins[a].at[here], outs[a].at[here], local.at[a]) for a in range(n)]

        def remote(j, a, src_slot, dst_slot):
            _, c, chips = places()
            cx, cy = chips[j]
            return pltpu.make_async_remote_copy(
                src_ref=ins[a].at[src_slot], dst_ref=outs[a].at[dst_slot], send_sem=send.at[j, a],
                recv_sem=recv.at[j, a], device_id=(cx, cy, c), device_id_type=MESH)

        def sent():
            here, _, chips = places()
            return [remote(j, a, 2 * cx + cy, here) for j, (cx, cy) in enumerate(chips) for a in range(n)]

        def start():
            for cp in own() + sent():
                cp.start()

        def finish():
            here, _, chips = places()
            for j, (cx, cy) in enumerate(chips):
                for a in range(n):
                    remote(j, a, here, 2 * cx + cy).wait_recv()
            for cp in sent():
                cp.wait_send()
            for cp in own():
                cp.wait()

        return start, finish

    n = len(arrs)
    return dict(arrs=arrs, out_shape=[S(a.shape, a.dtype) for a in arrs],
                scratch=[pltpu.SemaphoreType.DMA((NCHIP - 1, n)), pltpu.SemaphoreType.DMA((NCHIP - 1, n)),
                         pltpu.SemaphoreType.DMA((n,))], plan=plan)


def _adamw(parts, w, m, v, name, tr, tc=None):
    rows, cols = w.shape
    if tc is None:
        assert rows % tr == 0
        grid, shape, at = (rows // tr,), (tr, cols), (lambda i: (i, 0))
    else:
        assert cols % tc == 0
        grid, shape, at = (cols // tc,), (rows, tc), (lambda i: (0, i))
    c1 = 1.0 - ADAM_B1 ** ADAM_STEP
    c2 = 1.0 - ADAM_B2 ** ADAM_STEP

    nparts = parts.shape[0]

    def body(p_ref, w_ref, m_ref, v_ref, g_ref, d_ref, mo_ref, vo_ref):
        g = p_ref[0].astype(F32)
        for s in range(1, nparts):
            g = g + p_ref[s].astype(F32)
        m_new = ADAM_B1 * m_ref[...] + (1.0 - ADAM_B1) * g
        v_new = ADAM_B2 * v_ref[...] + (1.0 - ADAM_B2) * (g * g)
        g_ref[...] = g
        mo_ref[...] = m_new
        vo_ref[...] = v_new
        d_ref[...] = -ADAM_LR * ((m_new / c1) / (jnp.sqrt(v_new / c2) + ADAM_EPS) + ADAM_WD * w_ref[...])

    blk = pl.BlockSpec(shape, at)
    return pl.pallas_call(
        body, name=name, grid=grid,
        in_specs=[pl.BlockSpec((nparts,) + shape, lambda i: (0,) + at(i)), blk, blk, blk],
        out_specs=[blk] * 4, out_shape=[S((rows, cols), F32)] * 4,
        compiler_params=pltpu.CompilerParams(dimension_semantics=("parallel",)),
    )(parts, w, m, v)


def _to_aligned(wt):
    pad = jnp.zeros((GLR_W - GLR_N, wt.shape[1]), wt.dtype)
    return jnp.concatenate([wt[O_QG:O_GLR], wt[O_ZG:O_GA], wt[O_GLR:O_ZG], pad, wt[O_ZA:O_QG], wt[O_GA:O_END],
                            wt[O_QA:O_ZA]], axis=0)


def _from_aligned(wt):
    return jnp.concatenate([wt[C_QA:], wt[C_ZA:C_GA], wt[C_QG:C_ZG], wt[C_GLR:C_GLR + GLR_N], wt[C_ZG:C_GLR],
                            wt[C_GA:C_QA]], axis=0)


def _col_blocks(w, width):
    return w.reshape(w.shape[0], NDEV, width).transpose(1, 0, 2)


def _from_col_blocks(w):
    return w.transpose(1, 0, 2).reshape(w.shape[1], NDEV * w.shape[2])


SMALL = (("norm_g", D), ("qk_norm_q", HD), ("qk_norm_k", HD), ("gla_gate_b", 512), ("gla_norm_g", GDV),
         ("ple_norm_g", D))
SMALL_PAD = 4096


def _local_step(x2, p2, pos, tgt, norm_g, qk_norm_q, qk_norm_k, gla_gate_b, gla_norm_g, ple_norm_g, w_al,
                weights=None, proj_side=None, unpack=None, dw_side_of=None, dh_side_of=None):
    half = ROT_DIM // 2
    inv8 = jnp.power(jnp.float32(ROPE_THETA), -jnp.arange(half, dtype=F32) * 2.0 / ROT_DIM)
    inv = jnp.tile(jnp.concatenate([inv8, inv8, jnp.zeros((HD - ROT_DIM,), F32)]), 2).reshape(1, 128)
    gq = jnp.tile(qk_norm_q, (1, 2))
    gk = jnp.tile(qk_norm_k, (1, 2))

    h = _rms_fwd(x2, norm_g, "rms1_fwd")
    if proj_side is None:
        proj = _mm(h, w_al, mode="nt", name="proj", tm=1024, tn=1536, tk=D)
    else:
        proj, got = _mm(h, w_al, mode="nt", name="proj", tm=1024, tn=1536, tk=D, side=proj_side)
        weights = unpack(got)
    w2p, w_att_f, w_gla_f, w_out_f, w_pg_f, w_ple_f = weights
    qkv = _qk_prep(proj, pos, inv, gq, gk)
    fwd = [_att_fwd(qkv[g], qkv[3 + g], qkv[6 + g], g, f"att_fwd{g}") for g in range(3)]
    att, lse, ain = _att_merge([f[0] for f in fwd], [f[1] for f in fwd], proj)
    o_gla, bin_, states = _gla_fwd(proj, w2p, gla_gate_b, gla_norm_g)
    ya = _mm(ain, w_att_f, mode="nn", name="ya", tm=512, tn=D, tk=512, out_dtype=BF16)
    yb = _mm(bin_, w_gla_f, mode="nn", name="yb", tm=512, tn=D, tk=D, out_dtype=BF16)
    y = _merge_fwd(ya, yb, proj)
    x1 = _mm(y, w_out_f, mode="nn", name="x1", tm=512, tn=D, tk=D, res=x2)
    n2 = _rms_fwd(x1, ple_norm_g, "rms2_fwd")
    u = _mm(n2, w_pg_f, mode="nn", name="ple_u", tm=512, tn=D, tk=D)
    e = _mm(p2, w_ple_f, mode="nn", name="ple_e", tm=512, tn=D, tk=PLE)
    loss_v, dout, de, du = _loss_head(x1, e, u, tgt)

    dw_ple = _mm(p2, de, mode="tn", name="dw_ple", tm=PLE, tn=D, tk=512)
    dw_pg = _mm(n2, du, mode="tn", name="dw_pg", tm=D, tn=D, tk=512)
    dn2 = _mm(du, w_pg_f, mode="nt", name="dn2", tm=512, tn=D, tk=D)
    dx1, dx1b, dg_ple = _rms_bwd(dn2, x1, ple_norm_g, dout, "rms2_bwd")
    dw_out = _mm(y, dx1b, mode="tn", name="dw_out", tm=D, tn=D, tk=512)
    dy = _mm(dx1b, w_out_f, mode="nt", name="dy", tm=512, tn=D, tk=D, out_dtype=BF16)
    dproj, dya, dyb = _merge_bwd(dy, ya, yb, proj)
    dw_att = _mm(ain, dya, mode="tn", name="dw_att", tm=512, tn=D, tk=512)
    dain = _mm(dya, w_att_f, mode="nt", name="dain", tm=512, tn=512, tk=D, out_dtype=BF16)
    dw_gla = _mm(bin_, dyb, mode="tn", name="dw_gla", tm=D, tn=D, tk=512)
    dbin = _mm(dyb, w_gla_f, mode="nt", name="dbin", tm=512, tn=D, tk=D, out_dtype=BF16)
    dproj, da0, da1, da2, at1, at2, ls1, ls2 = _att_gate_bwd(dain, att, lse, proj, dproj)
    datts, atts, lses = (da0, da1, da2), (att[None], at1, at2), (lse[None], ls1, ls2)
    dproj, dw2, dbg, dgn = _gla_bwd(proj, w2p, gla_gate_b, gla_norm_g, o_gla, states, dbin, dproj)
    bwd = [_att_bwd(qkv[g], qkv[3 + g], qkv[6 + g], datts[g], atts[g], lses[g], g, f"att_bwd{g}") for g in range(3)]
    dproj, dgq, dgk = _qk_bwd(proj, pos, inv, gq, gk, [b[0] for b in bwd], [b[1] for b in bwd],
                              [b[2] for b in bwd], dproj)
    out = dict(loss=loss_v, dw2=dw2, dw_att=dw_att, dw_gla=dw_gla, dw_out=dw_out, dw_pg=dw_pg, dw_ple=dw_ple,
               dgq=dgq, dgk=dgk, dbg=dbg, dgn=dgn, dg_ple=dg_ple)
    if dw_side_of is None:
        dw_al = _mm(dproj, h, mode="tn", name="dw_in", tm=1536, tn=D, tk=2048)
    else:
        dw_al, out["dw_side"] = _mm(dproj, h, mode="tn", name="dw_in", tm=1536, tn=D, tk=2048, side=dw_side_of(out))
    if dh_side_of is None:
        dh = _mm(dproj, w_al, mode="nn", name="dh", tm=1024, tn=D, tk=3584)
    else:
        dh, out["dh_side"] = _mm(dproj, w_al, mode="nn", name="dh", tm=1024, tn=D, tk=3584, side=dh_side_of(dw_al))
    grad_x, _, dg_norm = _rms_bwd(dh, x2, norm_g, dx1, "rms1_bwd")
    out.update(grad_x=grad_x, dw_al=dw_al, dg_norm=dg_norm)
    return out


def kernel(x, p, positions, norm_g, w_in, qk_norm_q, qk_norm_k, gla_gate_w2, gla_gate_b, gla_norm_g, w_att_proj, w_gla_proj, w_out, ple_norm_g, w_ple_gate, w_ple, loss_target, m_norm_g, m_w_in, m_qk_norm_q, m_qk_norm_k, m_gla_gate_w2, m_gla_gate_b, m_gla_norm_g, m_w_att_proj, m_w_gla_proj, m_w_out, m_ple_norm_g, m_w_ple_gate, m_w_ple, v_norm_g, v_w_in, v_qk_norm_q, v_qk_norm_k, v_gla_gate_w2, v_gla_gate_b, v_gla_norm_g, v_w_att_proj, v_w_gla_proj, v_w_out, v_ple_norm_g, v_w_ple_gate, v_w_ple):
    x2, p2, tgt = x[0], p[0, 0], loss_target[0]
    pos = positions.astype(F32).reshape(T, 1)

    rows3 = jnp.stack([w_gla_proj[0], w_out[0], w_ple_gate[0]]).astype(BF16)
    cols3 = jnp.concatenate([w_att_proj[0], w_ple[0], jnp.pad(gla_gate_w2[0], ((0, 0), (0, 64)))], axis=0).astype(BF16)
    wt, mt, vt = w_in[0].T, m_w_in[0].T, v_w_in[0].T
    (g_in,) = _all_gather_by_chip([wt.astype(BF16)], "gather_w_in")
    w_al = _to_aligned(g_in.reshape(W_IN_COLS, D))

    def unpack(got):
        g_rows, g_cols = got
        w2_f = _from_col_blocks(g_cols[:, 768:784, :64])
        return (jnp.pad(w2_f, ((0, GLR_W - GLR_N), (0, 0))), _from_col_blocks(g_cols[:, :512]),
                g_rows[:, 0].reshape(D, D), g_rows[:, 1].reshape(D, D), g_rows[:, 2].reshape(D, D),
                _from_col_blocks(g_cols[:, 512:768]))

    def dw_side_of(g):
        s_rows = jnp.concatenate([g[k].reshape(NDEV, 128, D) for k in ("dw_gla", "dw_out", "dw_pg")], axis=1)
        s_cols = jnp.concatenate([_col_blocks(g["dw_att"], 128), _col_blocks(g["dw_ple"], 128),
                                  jnp.pad(_col_blocks(g["dw2"][:GLR_N], 64), ((0, 0), (0, 0), (0, 64)))], axis=1)
        return _exchange_side([s_rows.astype(BF16), s_cols.astype(BF16)])

    def dh_side_of(dw_al):
        s_in = _from_aligned(dw_al).astype(BF16).reshape(NDEV, W_IN_SHARD, D)
        (from_sibling,) = _exchange_sibling([s_in], "exchange_sibling")
        core = lax.axis_index("c").astype(jnp.int32).reshape(1)
        return _chips_side([_pair_add(s_in, from_sibling, core, "pair_add")])

    loc = _local_step(x2, p2, pos, tgt, norm_g, qk_norm_q, qk_norm_k, gla_gate_b, gla_norm_g, ple_norm_g, w_al,
                      proj_side=_gather_side([rows3, cols3]), unpack=unpack, dw_side_of=dw_side_of,
                      dh_side_of=dh_side_of)
    loss_v, grad_x = loc["loss"], loc["grad_x"]
    dg_norm, dgq, dgk, dbg, dgn, dg_ple = (loc[k] for k in ("dg_norm", "dgq", "dgk", "dbg", "dgn", "dg_ple"))
    r_rows, r_cols = loc["dw_side"]
    (r_in,) = loc["dh_side"]

    small = jnp.concatenate([dg_norm[0], dgq[0, :HD], dgk[0, :HD], dbg[0], dgn[0], dg_ple[0]])
    small = jnp.pad(small, (0, SMALL_PAD - small.shape[0])).reshape(1, 8, SMALL_PAD // 8)
    (r_small,) = _comm_call(_gather_side([small]), "gather_small")

    outs = {}

    def adam(nm, parts, w, m, v, tr):
        outs[nm] = _adamw(parts, w, m, v, "adam_" + nm, tr)

    outs["w_in"] = [o.T for o in _adamw(r_in, wt, mt, vt, "adam_w_in", None, tc=128)]
    adam("w_gla_proj", r_rows[:, :128], w_gla_proj[0], m_w_gla_proj[0], v_w_gla_proj[0], 128)
    adam("w_out", r_rows[:, 128:256], w_out[0], m_w_out[0], v_w_out[0], 128)
    adam("w_ple_gate", r_rows[:, 256:], w_ple_gate[0], m_w_ple_gate[0], v_w_ple_gate[0], 128)
    adam("w_att_proj", r_cols[:, :512], w_att_proj[0], m_w_att_proj[0], v_w_att_proj[0], 512)
    adam("w_ple", r_cols[:, 512:768], w_ple[0], m_w_ple[0], v_w_ple[0], 256)
    adam("gla_gate_w2", r_cols[:, 768:784, :64], gla_gate_w2[0], m_gla_gate_w2[0], v_gla_gate_w2[0], 16)
    given = dict(norm_g=(norm_g, m_norm_g, v_norm_g), qk_norm_q=(qk_norm_q, m_qk_norm_q, v_qk_norm_q),
                 qk_norm_k=(qk_norm_k, m_qk_norm_k, v_qk_norm_k), gla_gate_b=(gla_gate_b, m_gla_gate_b, v_gla_gate_b),
                 gla_norm_g=(gla_norm_g, m_gla_norm_g, v_gla_norm_g), ple_norm_g=(ple_norm_g, m_ple_norm_g, v_ple_norm_g))

    def pack(i):
        flat = jnp.concatenate([given[nm][i][0] for nm, _ in SMALL])
        return jnp.pad(flat, (0, SMALL_PAD - flat.shape[0])).reshape(8, SMALL_PAD // 8)

    sm = _adamw(r_small.reshape(NDEV, 8, SMALL_PAD // 8), pack(0), pack(1), pack(2), "adam_small", 8)
    off = 0
    for nm, width in SMALL:
        outs[nm] = [o.reshape(-1)[off:off + width] for o in sm]
        off += width

    loss = lax.psum(loss_v[0, 0], ("x", "y", "c"))
    order = ["norm_g", "w_in", "qk_norm_q", "qk_norm_k", "gla_gate_w2", "gla_gate_b", "gla_norm_g", "w_att_proj",
             "w_gla_proj", "w_out", "ple_norm_g", "w_ple_gate", "w_ple"]
    result = [loss, grad_x[None]]
    for i in range(4):
        result += [outs[nm][i][None] for nm in order]
    return tuple(result)
```

```python
import functools

import jax
import jax.numpy as jnp
from jax import lax
from jax.experimental import pallas as pl
from jax.experimental.pallas import tpu as pltpu

F32 = jnp.float32
BF16 = jnp.bfloat16
S = jax.ShapeDtypeStruct

T = 4096
D = 1024
NDEV = 8
HD = 64
ATT_W = 512
ATT_QKV = 1536
DILATIONS = (1, 4, 16)
BLK = 128
GH, GDK, GDV = 4, 128, 256
GLA_C = 128
PLE = 256
EPS = 1e-6
ROT_DIM = 16
ROPE_THETA = 500000.0
GLA_TAU = 16.0
W_IN_COLS = 10256
W_IN_SHARD = 1282

C_QG, C_KG, C_VG, C_ZG, C_GLR, C_ZA, C_GA, C_GB, C_QA, C_KA, C_VA = (
    0, 512, 1024, 2048, 3072, 3584, 4096, 5120, 6144, 7680, 9216)
GLA_GROUP_W = 3584
GLR_W = 512
NCOL = 10752
GLR_N = 16
O_QA, O_ZA, O_QG, O_GLR, O_ZG, O_GA, O_END = 0, 4608, 5120, 7168, 7184, 8208, 10256

ADAM_LR, ADAM_B1, ADAM_B2, ADAM_EPS, ADAM_WD, ADAM_STEP = 0.001, 0.9, 0.999, 1e-08, 0.01, 10

MESH = pl.DeviceIdType.MESH


def _sigmoid(z):
    return 1.0 / (1.0 + jnp.exp(-z))


def _dot(a, b, dims):
    return lax.dot_general(a, b, (dims, ((), ())), preferred_element_type=F32)


def _nn(a, b):
    return _dot(a, b, ((1,), (0,)))


def _nt(a, b):
    return _dot(a, b, ((1,), (1,)))


def _tn(a, b):
    return _dot(a, b, ((0,), (0,)))


def _mm(a, b, *, mode, name, tm, tn, tk, out_dtype=F32, res=None, side=None):
    if mode == "nn":
        (m, k), n = a.shape, b.shape[1]
        a_spec = pl.BlockSpec((tm, tk), lambda i, j, l: (i, l))
        b_spec = pl.BlockSpec((tk, tn), lambda i, j, l: (l, j))
        dot = _nn
    elif mode == "nt":
        (m, k), n = a.shape, b.shape[0]
        a_spec = pl.BlockSpec((tm, tk), lambda i, j, l: (i, l))
        b_spec = pl.BlockSpec((tn, tk), lambda i, j, l: (j, l))
        dot = _nt
    else:
        (k, m), n = a.shape, b.shape[1]
        a_spec = pl.BlockSpec((tk, tm), lambda i, j, l: (l, i))
        b_spec = pl.BlockSpec((tk, tn), lambda i, j, l: (l, j))
        dot = _tn
    assert m % tm == 0 and n % tn == 0 and k % tk == 0, (name, m, n, k)
    grid = (m // tm, n // tn, k // tk)
    nk = grid[2]
    o_spec = pl.BlockSpec((tm, tn), lambda i, j, l: (i, j))
    in_specs = [a_spec, b_spec]
    args = [a, b]
    if res is not None:
        in_specs.append(o_spec)
        args.append(res)
    n_in = len(args)
    n_side = 0 if side is None else len(side["arrs"])
    hbm = pl.BlockSpec(memory_space=pl.ANY)

    def body(*refs):
        a_ref, b_ref = refs[0], refs[1]
        r_ref = refs[2] if res is not None else None
        o_ref = refs[n_in + n_side]
        scratch = refs[n_in + 2 * n_side + 1:]
        if side is not None:
            start, finish_side = side["plan"](refs[n_in:n_in + n_side], refs[n_in + n_side + 1:n_in + 2 * n_side + 1],
                                              *scratch[1 if nk > 1 else 0:])
            ids = [pl.program_id(d) for d in range(3)]

            @pl.when((ids[0] == 0) & (ids[1] == 0) & (ids[2] == 0))
            def _():
                start()

        part = dot(a_ref[...].astype(BF16), b_ref[...].astype(BF16))

        def finish(val):
            if r_ref is not None:
                val = val + r_ref[...]
            o_ref[...] = val.astype(out_dtype)

        if nk == 1:
            finish(part)
        else:
            acc = scratch[0]
            l = pl.program_id(2)

            @pl.when(l == 0)
            def _():
                acc[...] = part

            @pl.when(l > 0)
            def _():
                acc[...] += part

            @pl.when(l == nk - 1)
            def _():
                finish(acc[...])

        if side is not None:
            @pl.when((ids[0] == grid[0] - 1) & (ids[1] == grid[1] - 1) & (ids[2] == grid[2] - 1))
            def _():
                finish_side()

    sems = [] if side is None else side["scratch"]
    outs = pl.pallas_call(
        body, name=name, grid=grid,
        in_specs=in_specs + [hbm] * n_side, out_specs=[o_spec] + [hbm] * n_side,
        out_shape=[S((m, n), out_dtype)] + ([] if side is None else side["out_shape"]),
        scratch_shapes=([pltpu.VMEM((tm, tn), F32)] if nk > 1 else []) + sems,
        compiler_params=pltpu.CompilerParams(
            dimension_semantics=("arbitrary",) * 3 if side is not None else ("parallel", "parallel", "arbitrary")),
    )(*args, *([] if side is None else side["arrs"]))
    return outs[0] if side is None else (outs[0], outs[1:])


def _rows(arr, width=None, cblk=0):
    return ("rows", arr, arr.shape[1] if width is None else width, cblk)


def _whole(arr):
    return ("whole", arr)


def _rowcall(body, name, tt, ins, outs, scratch=()):
    in_specs, args = [], []
    for spec in ins:
        if spec[0] == "rows":
            _, arr, width, cblk = spec
            in_specs.append(pl.BlockSpec((tt, width), functools.partial(lambda i, c: (i, c), c=cblk)))
        else:
            arr = spec[1]
            in_specs.append(pl.BlockSpec(arr.shape, functools.partial(lambda i, nd: (0,) * nd, nd=arr.ndim)))
        args.append(arr)
    out_specs, out_shape = [], []
    for kind, shape, dtype in outs:
        if kind == "rows":
            out_specs.append(pl.BlockSpec((tt, shape), lambda i: (i, 0)))
            out_shape.append(S((T, shape), dtype))
        else:
            out_specs.append(pl.BlockSpec(shape, functools.partial(lambda i, nd: (0,) * nd, nd=len(shape))))
            out_shape.append(S(shape, dtype))
    return pl.pallas_call(
        body, name=name, grid=(T // tt,), in_specs=in_specs, out_specs=out_specs, out_shape=out_shape,
        scratch_shapes=list(scratch),
        compiler_params=pltpu.CompilerParams(dimension_semantics=("arbitrary",)),
    )(*args)


def _rms_fwd(x, g, name):
    def body(x_ref, g_ref, h_ref):
        xf = x_ref[...]
        r = lax.rsqrt(jnp.mean(xf * xf, axis=-1, keepdims=True) + EPS)
        h_ref[...] = (xf * r * g_ref[...]).astype(BF16)

    return _rowcall(body, name, 512, [_rows(x), _whole(g)], [("rows", D, BF16)])[0]


def _rms_bwd(dn, x, g, skip, name):
    def body(dn_ref, x_ref, g_ref, s_ref, dx_ref, dxb_ref, dg_ref):
        xf = x_ref[...]
        r = lax.rsqrt(jnp.mean(xf * xf, axis=-1, keepdims=True) + EPS)
        dn_ = dn_ref[...]
        u = dn_ * g_ref[...]
        dx = s_ref[...] + r * u - xf * (r * r * r) * jnp.mean(u * xf, axis=-1, keepdims=True)
        dx_ref[...] = dx
        dxb_ref[...] = dx.astype(BF16)
        part = jnp.sum(dn_ * xf * r, axis=0, keepdims=True)

        @pl.when(pl.program_id(0) == 0)
        def _():
            dg_ref[...] = part

        @pl.when(pl.program_id(0) > 0)
        def _():
            dg_ref[...] += part

    return _rowcall(body, name, 256, [_rows(dn), _rows(x), _whole(g), _rows(skip)],
                    [("rows", D, F32), ("rows", D, BF16), ("acc", (1, D), F32)])


def _rot_tables(pos_ref, inv_ref):
    lane = lax.broadcasted_iota(jnp.int32, (1, 128), 1) % HD
    ang = pos_ref[...] * inv_ref[...]
    cos, sin = jnp.cos(ang), jnp.sin(ang)
    c = jnp.where(lane < ROT_DIM, cos, 1.0)
    sp = jnp.where((lane >= ROT_DIM // 2) & (lane < ROT_DIM), sin, 0.0)
    sm = jnp.where(lane < ROT_DIM // 2, -sin, 0.0)
    return c, sp, sm


def _head_sums(v):
    same = (lax.broadcasted_iota(jnp.int32, (128, 128), 0) < HD) == (lax.broadcasted_iota(jnp.int32, (128, 128), 1) < HD)
    ones = jnp.where(same, 1.0, 0.0).astype(BF16)
    hi = v.astype(BF16)
    lo = (v - hi.astype(F32)).astype(BF16)
    return _nn(hi, ones) + _nn(lo, ones)


def _pair_norm(t):
    return lax.rsqrt(_head_sums(t * t) * (1.0 / HD) + EPS)


def _pair_mean(t):
    return _head_sums(t) * (1.0 / HD)


TT = 256
NCH = ATT_QKV // 128


def _res_shape(grp, dtype):
    return S((DILATIONS[grp], T // DILATIONS[grp], ATT_W), dtype)


def _res_spec(grp):
    dil = DILATIONS[grp]
    return pl.BlockSpec((dil, TT // dil, ATT_W), lambda i: (0, i, 0))


def _to_residues(sc, j, dst_ref, dil, cols):
    n = TT // dil
    for r in range(dil):
        rows = sc[j] if dil == 1 else sc.at[j][pl.ds(r, n, stride=dil), :]
        dst_ref[r, :, cols] = rows.astype(dst_ref.dtype)


def _from_residues(src_ref, cols, sc, j, dil):
    n = TT // dil
    for r in range(dil):
        if dil == 1:
            sc[j] = src_ref[r, :, cols]
        else:
            sc.at[j][pl.ds(r, n, stride=dil), :] = src_ref[r, :, cols]


def _tok_spec(width, cblk=0):
    return pl.BlockSpec((TT, width), functools.partial(lambda i, c: (i, c), c=cblk))


def _const_spec(arr_or_shape):
    shape = arr_or_shape if isinstance(arr_or_shape, tuple) else arr_or_shape.shape
    return pl.BlockSpec(shape, functools.partial(lambda i, nd: (0,) * nd, nd=len(shape)))


def _qk_prep(proj, pos, inv, gq, gk):
    def body(q_ref, k_ref, v_ref, pos_ref, inv_ref, gq_ref, gk_ref, *rest):
        outs, sc = rest[:9], rest[9]
        c, sp, sm = _rot_tables(pos_ref, inv_ref)
        for which, (src, g_ref) in enumerate(((q_ref, gq_ref), (k_ref, gk_ref), (v_ref, None))):
            if g_ref is not None:
                g = jnp.broadcast_to(g_ref[...] * ((HD ** -0.5) if which == 0 else 1.0), c.shape)
                cg, spg, smg = c * g, sp * pltpu.roll(g, 8, 1), sm * pltpu.roll(g, 120, 1)
            for j in range(NCH):
                t = src[:, j * 128:(j + 1) * 128]
                if g_ref is not None:
                    t = _pair_norm(t) * (t * cg + pltpu.roll(t, 8, 1) * spg + pltpu.roll(t, 120, 1) * smg)
                sc[j] = t
            for j in range(NCH):
                grp, sub = divmod(j * 128, ATT_W)
                _to_residues(sc, j, outs[which * 3 + grp], DILATIONS[grp], slice(sub, sub + 128))

    return pl.pallas_call(
        body, name="qk_prep", grid=(T // TT,),
        in_specs=[_tok_spec(ATT_QKV, C_QA // ATT_QKV), _tok_spec(ATT_QKV, C_KA // ATT_QKV),
                  _tok_spec(ATT_QKV, C_VA // ATT_QKV), _tok_spec(1), _const_spec(inv), _const_spec(gq), _const_spec(gk)],
        out_specs=[_res_spec(g) for _ in range(3) for g in range(3)],
        out_shape=[_res_shape(g, BF16) for _ in range(3) for g in range(3)],
        scratch_shapes=[pltpu.VMEM((NCH, TT, 128), F32)],
        compiler_params=pltpu.CompilerParams(dimension_semantics=("arbitrary",)),
    )(proj, proj, proj, pos, inv, gq, gk)


def _qk_bwd(proj, pos, inv, gq, gk, dqs, dks, dvs, dproj):
    const = lambda a: pl.BlockSpec(a.shape, functools.partial(lambda i, p, nd: (0,) * nd, nd=a.ndim))
    res = lambda g: pl.BlockSpec((DILATIONS[g], TT // DILATIONS[g], ATT_W), lambda i, p: (0, i, 0))
    base = C_QA // ATT_QKV

    def body(t_ref, pos_ref, inv_ref, gq_ref, gk_ref, dq0, dq1, dq2, dk0, dk1, dk2, dv0, dv1, dv2, buf_ref,
             out_ref, dgq_ref, dgk_ref, sc):
        del buf_ref
        part = pl.program_id(1)
        first = pl.program_id(0) == 0

        def gather(drefs):
            for j in range(NCH):
                grp, sub = divmod(j * 128, ATT_W)
                _from_residues(drefs[grp], slice(sub, sub + 128), sc, j, DILATIONS[grp])

        def normed(g_ref, drefs, dg_ref):
            c, sp, sm = _rot_tables(pos_ref, inv_ref)
            gather(drefs)
            dg = jnp.zeros((1, 128), F32)
            for j in range(NCH):
                cols = slice(j * 128, (j + 1) * 128)
                d_rot = sc[j]
                dn = d_rot * c + pltpu.roll(d_rot * sp, 120, 1) + pltpu.roll(d_rot * sm, 8, 1)
                t = t_ref[:, cols]
                r = _pair_norm(t)
                u = dn * g_ref[...]
                out_ref[:, cols] = (r * u - t * (r * r * r) * _pair_mean(u * t)).astype(BF16)
                dg = dg + jnp.sum(dn * t * r, axis=0, keepdims=True)
            dg = dg + pltpu.roll(dg, HD, 1)

            @pl.when(first)
            def _():
                dg_ref[...] = dg

            @pl.when(jnp.logical_not(first))
            def _():
                dg_ref[...] += dg

        @pl.when(part == 0)
        def _():
            gather((dv0, dv1, dv2))
            for j in range(NCH):
                out_ref[:, j * 128:(j + 1) * 128] = sc[j].astype(BF16)

        @pl.when(part == 1)
        def _():
            normed(gq_ref, (dq0, dq1, dq2), dgq_ref)

        @pl.when(part == 2)
        def _():
            normed(gk_ref, (dk0, dk1, dk2), dgk_ref)

    keep = pl.BlockSpec((1, 128), lambda i, p: (0, 0))
    return pl.pallas_call(
        body, name="qk_bwd", grid=(T // TT, 3),
        in_specs=[pl.BlockSpec((TT, ATT_QKV), lambda i, p: (i, base + jnp.maximum(p - 1, 0))),
                  pl.BlockSpec((TT, 1), lambda i, p: (i, 0)), const(inv), const(gq), const(gk)]
        + [res(g) for _ in range(3) for g in range(3)] + [pl.BlockSpec(memory_space=pl.ANY)],
        out_specs=[pl.BlockSpec((TT, ATT_QKV), lambda i, p: (i, base + jnp.where(p == 0, 2, p - 1))), keep, keep],
        out_shape=[S(dproj.shape, dproj.dtype), S((1, 128), F32), S((1, 128), F32)],
        input_output_aliases={14: 0},
        scratch_shapes=[pltpu.VMEM((NCH, TT, 128), F32)],
        compiler_params=pltpu.CompilerParams(dimension_semantics=("arbitrary", "arbitrary")),
    )(proj, pos, inv, gq, gk, *dqs, *dks, *dvs, dproj)


def _split_heads(t):
    low = lax.broadcasted_iota(jnp.int32, (1, 128), 1) < HD
    zero = jnp.zeros_like(t)
    return jnp.concatenate([jnp.where(low, t, zero), jnp.where(low, zero, t)], axis=0)


def _join_heads(t2):
    low = lax.broadcasted_iota(jnp.int32, (1, 128), 1) < HD
    n = t2.shape[0] // 2
    return jnp.where(low, t2[:n], t2[n:])


def _band_mask4(has_before, has_own):
    row = lax.broadcasted_iota(jnp.int32, (BLK, 4 * BLK), 0)
    lane = lax.broadcasted_iota(jnp.int32, (BLK, 4 * BLK), 1)
    key = lane & (BLK - 1)
    own = lane >= 2 * BLK
    return (own & (key <= row) & has_own) | (jnp.logical_not(own) & (key >= row) & has_before)


def _band_mask_before(has_before):
    row = lax.broadcasted_iota(jnp.int32, (BLK, 2 * BLK), 0)
    key = lax.broadcasted_iota(jnp.int32, (BLK, 2 * BLK), 1) & (BLK - 1)
    return (key >= row) & has_before


def _per_head(width, col_a, col_b):
    lane = lax.broadcasted_iota(jnp.int32, (1, width), 1)
    return jnp.where((lane & BLK) == 0, col_a, col_b)


NQ = ATT_W // 128


def _att_fwd(q, k, v, grp, name):
    dil = DILATIONS[grp]
    nb = T // dil // BLK

    def body(q_ref, kp_ref, kc_ref, vp_ref, vc_ref, o_ref, lse_ref, s_sc, p_sc):
        mask = _band_mask4(pl.program_id(1) > 0, True)
        low = lax.broadcasted_iota(jnp.int32, (1, 128), 1) < HD
        halves = lambda ref, j, h: (ref[j, :, h * BLK:(h + 1) * BLK], ref[j, :, (h + 2) * BLK:(h + 3) * BLK])
        for j in range(NQ):
            cols = slice(j * 128, (j + 1) * 128)
            k4 = jnp.concatenate([_split_heads(kp_ref[:, cols]), _split_heads(kc_ref[:, cols])], axis=0)
            s_sc[j] = jnp.where(mask, _nt(q_ref[:, cols], k4), -jnp.inf)
        mxs = [[jnp.maximum(*(jnp.max(t, axis=-1, keepdims=True) for t in halves(s_sc, j, h))) for h in range(2)]
               for j in range(NQ)]
        dens = []
        for j in range(NQ):
            p = jnp.exp(s_sc[j] - _per_head(4 * BLK, *mxs[j]))
            p_sc[j] = p.astype(BF16)
            dens.append([jnp.sum(p[:, h * BLK:(h + 1) * BLK], axis=-1, keepdims=True)
                         + jnp.sum(p[:, (h + 2) * BLK:(h + 3) * BLK], axis=-1, keepdims=True) for h in range(2)])
        for j in range(NQ):
            cols = slice(j * 128, (j + 1) * 128)
            v4 = jnp.concatenate([_split_heads(vp_ref[:, cols]), _split_heads(vc_ref[:, cols])], axis=0)
            o_ref[:, cols] = _nn(p_sc[j], v4) / jnp.where(low, dens[j][0], dens[j][1])
            lse_ref[:, cols] = jnp.where(low, mxs[j][0] + jnp.log(dens[j][0]), mxs[j][1] + jnp.log(dens[j][1]))

    cur = pl.BlockSpec((None, BLK, ATT_W), lambda r, i: (r, i, 0))
    prev = pl.BlockSpec((None, BLK, ATT_W), lambda r, i: (r, jnp.maximum(i - 1, 0), 0))
    return pl.pallas_call(
        body, name=name, grid=(dil, nb),
        in_specs=[cur, prev, cur, prev, cur],
        out_specs=[cur, cur], out_shape=[_res_shape(grp, F32)] * 2,
        scratch_shapes=[pltpu.VMEM((NQ, BLK, 4 * BLK), F32), pltpu.VMEM((NQ, BLK, 4 * BLK), BF16)],
        compiler_params=pltpu.CompilerParams(dimension_semantics=("parallel", "arbitrary")),
    )(q, k, k, v, v)


def _att_bwd(q, k, v, datt, att, lse, grp, name):
    dil = DILATIONS[grp]
    nb = T // dil // BLK
    scale = HD ** -0.5

    def body(q0_ref, q1_ref, kp_ref, kc_ref, vp_ref, vc_ref, do0_ref, do1_ref, o0_ref, o1_ref, l0_ref, l1_ref,
             dq_ref, dk_ref, dv_ref, k4_sc, v4_sc, s0_sc, s1_sc, dp0_sc, dp1_sc, p_sc, ds_sc):
        i = pl.program_id(1)
        mask_mine = _band_mask4(i > 0, True)
        mask_next = _band_mask_before(i < nb - 1)
        low = lax.broadcasted_iota(jnp.int32, (1, 128), 1) < HD
        for j in range(NQ):
            cols = slice(j * 128, (j + 1) * 128)
            k4_sc[j, :2 * BLK] = _split_heads(kp_ref[:, cols])
            k4_sc[j, 2 * BLK:] = _split_heads(kc_ref[:, cols])
            v4_sc[j, :2 * BLK] = _split_heads(vp_ref[:, cols])
            v4_sc[j, 2 * BLK:] = _split_heads(vc_ref[:, cols])
        for j in range(NQ):
            cols = slice(j * 128, (j + 1) * 128)
            s0_sc[j] = _nt(q0_ref[:, cols], k4_sc[j])
            s1_sc[j] = _nt(q1_ref[:, cols], k4_sc[j, 2 * BLK:])
            dp0_sc[j] = _nt(do0_ref[:, cols].astype(BF16), v4_sc[j])
            dp1_sc[j] = _nt(do1_ref[:, cols].astype(BF16), v4_sc[j, 2 * BLK:])
        stats = []
        for j in range(NQ):
            cols = slice(j * 128, (j + 1) * 128)
            for do_ref, o_ref, l_ref in ((do0_ref, o0_ref, l0_ref), (do1_ref, o1_ref, l1_ref)):
                prod = do_ref[:, cols] * o_ref[:, cols]
                d_all = jnp.sum(prod, axis=-1, keepdims=True)
                d_low = jnp.sum(jnp.where(low, prod, 0.0), axis=-1, keepdims=True)
                lse_t = l_ref[:, cols]
                stats.append((d_low, d_all - d_low, lse_t[:, 0:1], lse_t[:, HD:HD + 1]))
        for j in range(NQ):
            (da, db, la, lb), (da1, db1, la1, lb1) = stats[2 * j], stats[2 * j + 1]
            p0 = jnp.where(mask_mine, jnp.exp(s0_sc[j] - _per_head(4 * BLK, la, lb)), 0.0)
            ds0 = p0 * (dp0_sc[j] - _per_head(4 * BLK, da, db))
            p1 = jnp.where(mask_next, jnp.exp(s1_sc[j] - _per_head(2 * BLK, la1, lb1)), 0.0)
            ds1 = p1 * (dp1_sc[j] - _per_head(2 * BLK, da1, db1))
            p_sc[j, :BLK] = p0.astype(BF16)
            ds_sc[j, :BLK] = ds0.astype(BF16)
            p_sc[j, BLK:, 2 * BLK:] = p1.astype(BF16)
            ds_sc[j, BLK:, 2 * BLK:] = ds1.astype(BF16)
        for j in range(NQ):
            cols = slice(j * 128, (j + 1) * 128)
            dq_ref[:, cols] = _nn(ds_sc[j, :BLK], k4_sc[j]) * scale
            qq = jnp.concatenate([q0_ref[:, cols], q1_ref[:, cols]], axis=0)
            dd = jnp.concatenate([do0_ref[:, cols], do1_ref[:, cols]], axis=0).astype(BF16)
            dk_ref[:, cols] = _join_heads(_tn(ds_sc[j, :, 2 * BLK:], qq))
            dv_ref[:, cols] = _join_heads(_tn(p_sc[j, :, 2 * BLK:], dd))

    def spec(shift):
        return pl.BlockSpec((None, BLK, ATT_W), lambda r, i: (r, jnp.clip(i + shift, 0, nb - 1), 0))

    here, after, before = spec(0), spec(1), spec(-1)
    vm = pltpu.VMEM
    return pl.pallas_call(
        body, name=name, grid=(dil, nb),
        in_specs=[here, after, before, here, before, here, here, after, here, after, here, after],
        out_specs=[here] * 3, out_shape=[_res_shape(grp, F32)] * 3,
        scratch_shapes=[vm((NQ, 4 * BLK, 128), BF16), vm((NQ, 4 * BLK, 128), BF16), vm((NQ, BLK, 4 * BLK), F32),
                        vm((NQ, BLK, 2 * BLK), F32), vm((NQ, BLK, 4 * BLK), F32), vm((NQ, BLK, 2 * BLK), F32),
                        vm((NQ, 2 * BLK, 4 * BLK), BF16), vm((NQ, 2 * BLK, 4 * BLK), BF16)],
        compiler_params=pltpu.CompilerParams(dimension_semantics=("parallel", "arbitrary")),
    )(q, q, k, k, v, v, datt, datt, att, att, lse, lse)


def _att_merge(os_, lses, proj):
    nq = ATT_W // 128

    def body(o0, o1, o2, l0, l1, l2, za_ref, att_ref, lse_ref, ain_ref, sc):
        for a, ref in enumerate((o0, o1, o2, l0, l1, l2)):
            for j in range(nq):
                _from_residues(ref, slice(j * 128, (j + 1) * 128), sc, a * nq + j, DILATIONS[a % 3])
        for j in range(nq):
            cols = slice(j * 128, (j + 1) * 128)
            oa, ob, oc = (sc[a * nq + j] for a in range(3))
            la, lb, lc = (sc[(3 + a) * nq + j] for a in range(3))
            m = jnp.maximum(jnp.maximum(la, lb), lc)
            wa, wb, wc = jnp.exp(la - m), jnp.exp(lb - m), jnp.exp(lc - m)
            tot = wa + wb + wc
            att = (wa * oa + wb * ob + wc * oc) / tot
            att_ref[:, cols] = att
            lse_ref[:, cols] = m + jnp.log(tot)
            za = za_ref[:, cols]
            ain_ref[:, cols] = (att * za * _sigmoid(za)).astype(BF16)

    return pl.pallas_call(
        body, name="att_merge", grid=(T // TT,),
        in_specs=[_res_spec(g) for _ in range(2) for g in range(3)] + [_tok_spec(ATT_W, C_ZA // ATT_W)],
        out_specs=[_tok_spec(ATT_W)] * 3,
        out_shape=[S((T, ATT_W), F32), S((T, ATT_W), F32), S((T, ATT_W), BF16)],
        scratch_shapes=[pltpu.VMEM((6 * nq, TT, 128), F32)],
        compiler_params=pltpu.CompilerParams(dimension_semantics=("arbitrary",)),
    )(*os_, *lses, proj)


def _att_gate_bwd(dain, att, lse, proj, dproj):
    nq = ATT_W // 128

    def body(d_ref, att_ref, lse_ref, za_ref, buf_ref, dza_ref, da0, da1, da2, at1, at2, ls1, ls2, sc):
        del buf_ref
        for j in range(nq):
            cols = slice(j * 128, (j + 1) * 128)
            za = za_ref[:, cols]
            sg = _sigmoid(za)
            d = d_ref[:, cols].astype(F32)
            att_ = att_ref[:, cols]
            dza_ref[:, cols] = (d * att_ * sg * (1.0 + za * (1.0 - sg))).astype(BF16)
            sc[j] = d * za * sg
            sc[nq + j] = att_
            sc[2 * nq + j] = lse_ref[:, cols]
        for j in range(nq):
            cols = slice(j * 128, (j + 1) * 128)
            for grp, dst in enumerate((da0, da1, da2)):
                _to_residues(sc, j, dst, DILATIONS[grp], cols)
            for grp, dst in ((1, at1), (2, at2)):
                _to_residues(sc, nq + j, dst, DILATIONS[grp], cols)
            for grp, dst in ((1, ls1), (2, ls2)):
                _to_residues(sc, 2 * nq + j, dst, DILATIONS[grp], cols)

    res = (0, 1, 2, 1, 2, 1, 2)
    return pl.pallas_call(
        body, name="att_gate_bwd", grid=(T // TT,),
        in_specs=[_tok_spec(ATT_W)] * 3 + [_tok_spec(ATT_W, C_ZA // ATT_W), pl.BlockSpec(memory_space=pl.ANY)],
        out_specs=[_tok_spec(ATT_W, C_ZA // ATT_W)] + [_res_spec(g) for g in res],
        out_shape=[S(dproj.shape, dproj.dtype)] + [_res_shape(g, F32) for g in res],
        input_output_aliases={4: 0},
        scratch_shapes=[pltpu.VMEM((3 * nq, TT, 128), F32)],
        compiler_params=pltpu.CompilerParams(dimension_semantics=("arbitrary",)),
    )(dain, att, lse, proj, dproj)


def _split3(v):
    hi = v.astype(BF16)
    r1 = v - hi.astype(F32)
    mid = r1.astype(BF16)
    lo = (r1 - mid.astype(F32)).astype(BF16)
    return hi, mid, lo


def _tri_sum(v, upper):
    n = v.shape[0]
    row = lax.broadcasted_iota(jnp.int32, (n, n), 0)
    col = lax.broadcasted_iota(jnp.int32, (n, n), 1)
    tri = jnp.where(col >= row if upper else col <= row, 1.0, 0.0).astype(BF16)
    hi, mid, lo = _split3(v)
    return _nn(tri, hi) + _nn(tri, mid) + _nn(tri, lo)


def _gla_gates(glr_ref, w2_ref, b_ref):
    logit = _nn(glr_ref[...].astype(BF16), w2_ref[...]) + b_ref[...]
    lg = (jnp.minimum(logit, 0.0) - jnp.log(1.0 + jnp.exp(-jnp.abs(logit)))) * (1.0 / GLA_TAU)
    return logit, _tri_sum(lg, upper=False)


def _gla_head(cum, q_ref, k_ref, h):
    cols = slice(h * GDK, (h + 1) * GDK)
    b = cum[:, cols]
    last = b[GLA_C - 1:GLA_C, :]
    e_pos = jnp.exp(b)
    e_neg = jnp.exp(-b)
    e_end = jnp.exp(last - b)
    qt = q_ref[:, cols] * (GDK ** -0.5) * e_pos
    kt = k_ref[:, cols] * e_neg
    kh = k_ref[:, cols] * e_end
    return b, last, e_pos, e_neg, e_end, qt, kt, kh


def _causal(n):
    return lax.broadcasted_iota(jnp.int32, (n, n), 1) <= lax.broadcasted_iota(jnp.int32, (n, n), 0)


def _gla_fwd(proj, w2p, bg, gn):
    nc = T // GLA_C

    def body(q_ref, k_ref, v_ref, glr_ref, zg_ref, w2_ref, b_ref, gn_ref, o_ref, bin_ref, st_ref, state):
        @pl.when(pl.program_id(0) == 0)
        def _():
            state[...] = jnp.zeros_like(state)

        _, cum = _gla_gates(glr_ref, w2_ref, b_ref)
        for h in range(GH):
            _, last, _, _, _, qt, kt, kh = _gla_head(cum, q_ref, k_ref, h)
            vcols = slice(h * GDV, (h + 1) * GDV)
            st = state[h]
            st_ref[0, h] = st
            v = v_ref[:, vcols].astype(BF16)
            qb = qt.astype(BF16)
            a = jnp.where(_causal(GLA_C), _nt(qb, kt.astype(BF16)), 0.0)
            o = _nt(qb, st.astype(BF16)) + _nn(a.astype(BF16), v)
            state[h] = st * jnp.exp(last) + _tn(v, kh.astype(BF16))
            o_ref[:, vcols] = o
            r = lax.rsqrt(jnp.mean(o * o, axis=-1, keepdims=True) + EPS)
            zg = zg_ref[:, vcols]
            bin_ref[:, vcols] = (o * r * gn_ref[...] * zg * _sigmoid(zg)).astype(BF16)

    row = lambda width, cblk: pl.BlockSpec((GLA_C, width), functools.partial(lambda i, c: (i, c), c=cblk))
    full = lambda a: pl.BlockSpec(a.shape, functools.partial(lambda i, nd: (0,) * nd, nd=a.ndim))
    return pl.pallas_call(
        body, name="gla_fwd", grid=(nc,),
        in_specs=[row(512, C_QG // 512), row(512, C_KG // 512), row(1024, C_VG // 1024), row(GLR_W, C_GLR // GLR_W),
                  row(1024, C_ZG // 1024), full(w2p), full(bg), full(gn)],
        out_specs=[pl.BlockSpec((GLA_C, GH * GDV), lambda i: (i, 0)), pl.BlockSpec((GLA_C, GH * GDV), lambda i: (i, 0)),
                   pl.BlockSpec((1, GH, GDV, GDK), lambda i: (i, 0, 0, 0))],
        out_shape=[S((T, GH * GDV), F32), S((T, GH * GDV), BF16), S((nc, GH, GDV, GDK), F32)],
        scratch_shapes=[pltpu.VMEM((GH, GDV, GDK), F32)],
        compiler_params=pltpu.CompilerParams(dimension_semantics=("arbitrary",)),
    )(proj, proj, proj, proj, proj, w2p, bg, gn)


def _gla_bwd(proj, w2p, bg, gn, o_gla, states, dbin, dproj):
    nc = T // GLA_C

    def body(q_ref, k_ref, v_ref, glr_ref, zg_ref, w2_ref, b_ref, gn_ref, o_ref, st_ref, dbin_ref, buf_ref,
             out_ref, dw2_ref, dbg_ref, dgn_ref, dstate, dlogit):
        del buf_ref
        dq_ref = out_ref.at[:, C_QG:C_KG]
        dk_ref = out_ref.at[:, C_KG:C_VG]
        dv_ref = out_ref.at[:, C_VG:C_ZG]
        dzg_ref = out_ref.at[:, C_ZG:C_GLR]
        dglr_ref = out_ref.at[:, C_GLR:C_GLR + GLR_W]
        first = pl.program_id(0) == 0

        @pl.when(first)
        def _():
            dstate[...] = jnp.zeros_like(dstate)

        logit, cum = _gla_gates(glr_ref, w2_ref, b_ref)
        is_last = lax.broadcasted_iota(jnp.int32, (GLA_C, 1), 0) == GLA_C - 1
        dgn = jnp.zeros((1, GDV), F32)
        for h in range(GH):
            _, last, e_pos, e_neg, e_end, qt, kt, kh = _gla_head(cum, q_ref, k_ref, h)
            cols = slice(h * GDK, (h + 1) * GDK)
            vcols = slice(h * GDV, (h + 1) * GDV)
            o = o_ref[:, vcols]
            r = lax.rsqrt(jnp.mean(o * o, axis=-1, keepdims=True) + EPS)
            zg = zg_ref[:, vcols]
            sg = _sigmoid(zg)
            db_ = dbin_ref[:, vcols].astype(F32)
            dlin = db_ * zg * sg
            dzg_ref[:, vcols] = (db_ * (o * r * gn_ref[...]) * sg * (1.0 + zg * (1.0 - sg))).astype(BF16)
            u = dlin * gn_ref[...]
            do = (r * u - o * (r * r * r) * jnp.mean(u * o, axis=-1, keepdims=True)).astype(BF16)
            dgn = dgn + jnp.sum(dlin * o * r, axis=0, keepdims=True)
            st = st_ref[0, h]
            dst = dstate[h]
            v = v_ref[:, vcols].astype(BF16)
            qb, kb, khb = qt.astype(BF16), kt.astype(BF16), kh.astype(BF16)
            dstb = dst.astype(BF16)
            causal = _causal(GLA_C)
            a = jnp.where(causal, _nt(qb, kb), 0.0).astype(BF16)
            da = jnp.where(causal, _nt(do, v), 0.0).astype(BF16)
            dqt = _nn(do, st.astype(BF16)) + _nn(da, kb)
            dkt = _tn(da, qb)
            dkh = _nn(v, dstb)
            dv_ref[:, vcols] = (_tn(a, do) + _nt(khb, dstb)).astype(BF16)
            lam = jnp.exp(last)
            dlam = jnp.sum(dst * st, axis=0, keepdims=True)
            dstate[h] = dst * lam + _tn(do, qb)
            dq_ref[:, cols] = (dqt * e_pos * (GDK ** -0.5)).astype(BF16)
            dk_ref[:, cols] = (dkt * e_neg + dkh * e_end).astype(BF16)
            dkh_kh = dkh * kh
            dcum = dqt * qt - dkt * kt - dkh_kh
            dlast = jnp.sum(dkh_kh, axis=0, keepdims=True) + dlam * lam
            dcum = jnp.where(is_last, dcum + dlast, dcum)
            dlg = _tri_sum(dcum, upper=True)
            dlogit[:, cols] = dlg * (1.0 / GLA_TAU) * (1.0 - _sigmoid(logit[:, cols]))

        dl = dlogit[...]
        dlb = dl.astype(BF16)
        dglr_ref[...] = _nt(dlb, w2_ref[...]).astype(BF16)
        dw2 = _tn(glr_ref[...].astype(BF16), dlb)
        dbg = jnp.sum(dl, axis=0, keepdims=True)

        @pl.when(first)
        def _():
            dw2_ref[...] = dw2
            dbg_ref[...] = dbg
            dgn_ref[...] = dgn

        @pl.when(jnp.logical_not(first))
        def _():
            dw2_ref[...] += dw2
            dbg_ref[...] += dbg
            dgn_ref[...] += dgn

    rev = lambda i: nc - 1 - i
    row = lambda width, cblk: pl.BlockSpec((GLA_C, width), functools.partial(lambda i, c: (rev(i), c), c=cblk))
    full = lambda a: pl.BlockSpec(a.shape, functools.partial(lambda i, nd: (0,) * nd, nd=a.ndim))
    keep = lambda shape: pl.BlockSpec(shape, functools.partial(lambda i, nd: (0,) * nd, nd=len(shape)))
    return pl.pallas_call(
        body, name="gla_bwd", grid=(nc,),
        in_specs=[row(512, C_QG // 512), row(512, C_KG // 512), row(1024, C_VG // 1024), row(GLR_W, C_GLR // GLR_W),
                  row(1024, C_ZG // 1024), full(w2p), full(bg), full(gn), row(GH * GDV, 0),
                  pl.BlockSpec((1, GH, GDV, GDK), lambda i: (rev(i), 0, 0, 0)), row(GH * GDV, 0),
                  pl.BlockSpec(memory_space=pl.ANY)],
        out_specs=[row(GLA_GROUP_W, 0), keep((GLR_W, 512)), keep((1, 512)), keep((1, GDV))],
        out_shape=[S(dproj.shape, dproj.dtype), S((GLR_W, 512), F32), S((1, 512), F32), S((1, GDV), F32)],
        input_output_aliases={11: 0},
        scratch_shapes=[pltpu.VMEM((GH, GDV, GDK), F32), pltpu.VMEM((GLA_C, GH * GDK), F32)],
        compiler_params=pltpu.CompilerParams(dimension_semantics=("arbitrary",)),
    )(proj, proj, proj, proj, proj, w2p, bg, gn, o_gla, states, dbin, dproj)


def _merge_fwd(ya, yb, proj):
    def body(ya_ref, yb_ref, ga_ref, gb_ref, y_ref):
        y_ref[...] = (_sigmoid(ga_ref[...]) * ya_ref[...].astype(F32)
                      + _sigmoid(gb_ref[...]) * yb_ref[...].astype(F32)).astype(BF16)

    return _rowcall(body, "merge_fwd", 512,
                    [_rows(ya), _rows(yb), _rows(proj, D, C_GA // D), _rows(proj, D, C_GB // D)],
                    [("rows", D, BF16)])[0]


def _merge_bwd(dy, ya, yb, proj):
    tt = 512

    def body(dy_ref, ya_ref, yb_ref, g_ref, dg_ref, dya_ref, dyb_ref):
        dy_ = dy_ref[...].astype(F32)
        sa, sb = _sigmoid(g_ref[:, :D]), _sigmoid(g_ref[:, D:])
        dg_ref[:, :D] = (dy_ * ya_ref[...].astype(F32) * sa * (1.0 - sa)).astype(BF16)
        dg_ref[:, D:] = (dy_ * yb_ref[...].astype(F32) * sb * (1.0 - sb)).astype(BF16)
        dya_ref[...] = (dy_ * sa).astype(BF16)
        dyb_ref[...] = (dy_ * sb).astype(BF16)

    tok = pl.BlockSpec((tt, D), lambda i: (i, 0))
    gates = pl.BlockSpec((tt, 2 * D), lambda i: (i, C_GA // (2 * D)))
    return pl.pallas_call(
        body, name="merge_bwd", grid=(T // tt,),
        in_specs=[tok, tok, tok, gates], out_specs=[gates, tok, tok],
        out_shape=[S((T, NCOL), BF16), S((T, D), BF16), S((T, D), BF16)],
        compiler_params=pltpu.CompilerParams(dimension_semantics=("arbitrary",)),
    )(dy, ya, yb, proj)


def _loss_head(x1, e, u, target):
    def body(x1_ref, e_ref, u_ref, t_ref, loss_ref, dout_ref, de_ref, du_ref, acc):
        first = pl.program_id(0) == 0
        pg = _sigmoid(u_ref[...])
        e_ = e_ref[...]
        diff = x1_ref[...] + e_ * pg - t_ref[...]
        part = jnp.sum(diff * diff, axis=0, keepdims=True)

        @pl.when(first)
        def _():
            acc[...] = part

        @pl.when(jnp.logical_not(first))
        def _():
            acc[...] += part

        dout = diff * (1.0 / D)
        dout_ref[...] = dout
        de_ref[...] = (dout * pg).astype(BF16)
        du_ref[...] = (dout * e_ * pg * (1.0 - pg)).astype(BF16)
        loss_ref[...] = jnp.zeros((1, 128), F32) + jnp.sum(acc[...], axis=-1, keepdims=True) * (0.5 / D)

    return _rowcall(body, "loss_head", 256, [_rows(x1), _rows(e), _rows(u), _rows(target)],
                    [("acc", (1, 128), F32), ("rows", D, F32), ("rows", D, BF16), ("rows", D, BF16)],
                    scratch=[pltpu.VMEM((1, D), F32)])


def _peer(k):
    x, y, c = lax.axis_index("x"), lax.axis_index("y"), lax.axis_index("c")
    return (x ^ ((k >> 2) & 1), y ^ ((k >> 1) & 1), c ^ (k & 1))


def _my_index():
    return 4 * lax.axis_index("x") + 2 * lax.axis_index("y") + lax.axis_index("c")


def _peer_index(k):
    px, py, pc = _peer(k)
    return 4 * px + 2 * py + pc


def _pairwise_plan(src_of, dst_of, landed_of, own_src, own_dst):
    def plan(ins, outs, send, recv, local):
        n = len(ins)

        def own():
            return [pltpu.make_async_copy(own_src(ins[a]), own_dst(outs[a]), local.at[a]) for a in range(n)]

        def remote(k, a, src, dst):
            return pltpu.make_async_remote_copy(src_ref=src, dst_ref=dst, send_sem=send.at[k - 1, a],
                                                recv_sem=recv.at[k - 1, a], device_id=_peer(k), device_id_type=MESH)

        def sent():
            return [remote(k, a, src_of(ins[a], k), dst_of(outs[a])) for k in range(1, NDEV) for a in range(n)]

        def start():
            for cp in own() + sent():
                cp.start()

        def finish():
            for k in range(1, NDEV):
                for a in range(n):
                    remote(k, a, own_src(ins[a]), landed_of(outs[a], k)).wait_recv()
            for cp in sent():
                cp.wait_send()
            for cp in own():
                cp.wait()

        return start, finish

    return plan


def _pairwise_sems(n):
    return [pltpu.SemaphoreType.DMA((NDEV - 1, n)), pltpu.SemaphoreType.DMA((NDEV - 1, n)),
            pltpu.SemaphoreType.DMA((n,))]


def _gather_side(arrs):
    plan = _pairwise_plan(src_of=lambda i, k: i, dst_of=lambda o: o.at[_my_index()],
                          landed_of=lambda o, k: o.at[_peer_index(k)],
                          own_src=lambda i: i, own_dst=lambda o: o.at[_my_index()])
    return dict(arrs=arrs, out_shape=[S((NDEV,) + a.shape, a.dtype) for a in arrs],
                scratch=_pairwise_sems(len(arrs)), plan=plan)


def _exchange_side(arrs):
    plan = _pairwise_plan(src_of=lambda i, k: i.at[_peer_index(k)], dst_of=lambda o: o.at[_my_index()],
                          landed_of=lambda o, k: o.at[_peer_index(k)],
                          own_src=lambda i: i.at[_my_index()], own_dst=lambda o: o.at[_my_index()])
    return dict(arrs=arrs, out_shape=[S(a.shape, a.dtype) for a in arrs], scratch=_pairwise_sems(len(arrs)), plan=plan)


def _comm_call(side, name):
    n = len(side["arrs"])

    def body(*refs):
        start, finish = side["plan"](refs[:n], refs[n:2 * n], *refs[2 * n:])
        start()
        finish()

    hbm = pl.BlockSpec(memory_space=pl.ANY)
    return pl.pallas_call(body, name=name, in_specs=[hbm] * n, out_specs=[hbm] * n, out_shape=side["out_shape"],
                          scratch_shapes=side["scratch"])(*side["arrs"])


def _all_gather_by_chip(arrs, name):
    n = len(arrs)

    def body(*refs):
        ins, outs = refs[:n], refs[n:2 * n]
        send, recv, local = refs[2 * n:]
        x, y, c = lax.axis_index("x"), lax.axis_index("y"), lax.axis_index("c")
        me, sibling = (x, y, c), (x, y, 1 - c)
        chips = [(1 - x, y), (x, 1 - y), (1 - x, 1 - y)]

        def copy(k, a, block, to, src=None):
            px, py, pc = block
            slot = outs[a].at[4 * px + 2 * py + pc]
            return pltpu.make_async_remote_copy(
                src_ref=slot if src is None else src, dst_ref=slot, send_sem=send.at[k, a], recv_sem=recv.at[k, a],
                device_id=to, device_id_type=MESH)

        mine = [pltpu.make_async_copy(ins[a], outs[a].at[4 * x + 2 * y + c], local.at[a]) for a in range(n)]
        first = []
        for a in range(n):
            first.append(copy(0, a, me, sibling, src=ins[a]))
            first += [copy(1 + j, a, me, (*chip, c), src=ins[a]) for j, chip in enumerate(chips)]
        for cp in mine + first:
            cp.start()
        passed = []
        for j, chip in enumerate(chips):
            for a in range(n):
                copy(1 + j, a, (*chip, c), me).wait_recv()
                passed.append(copy(4 + j, a, (*chip, c), sibling))
                passed[-1].start()
        for a in range(n):
            copy(0, a, sibling, me).wait_recv()
        for j, chip in enumerate(chips):
            for a in range(n):
                copy(4 + j, a, (*chip, 1 - c), me).wait_recv()
        for cp in first + passed:
            cp.wait_send()
        for cp in mine:
            cp.wait()

    hbm = pl.BlockSpec(memory_space=pl.ANY)
    return pl.pallas_call(
        body, name=name, in_specs=[hbm] * n, out_specs=[hbm] * n,
        out_shape=[S((NDEV,) + a.shape, a.dtype) for a in arrs],
        scratch_shapes=[pltpu.SemaphoreType.DMA((NDEV - 1, n)), pltpu.SemaphoreType.DMA((NDEV - 1, n)),
                        pltpu.SemaphoreType.DMA((n,))],
    )(*arrs)


NCHIP = 4


def _exchange_sibling(arrs, name):
    n = len(arrs)

    def body(*refs):
        ins, outs = refs[:n], refs[n:2 * n]
        send, recv = refs[2 * n:]
        x, y, c = lax.axis_index("x"), lax.axis_index("y"), lax.axis_index("c")
        copies = []
        for q in range(NCHIP):
            for a in range(n):
                copies.append(pltpu.make_async_remote_copy(
                    src_ref=ins[a].at[2 * q + (1 - c)], dst_ref=outs[a].at[q], send_sem=send.at[q, a],
                    recv_sem=recv.at[q, a], device_id=(x, y, 1 - c), device_id_type=MESH))
        for cp in copies:
            cp.start()
        for cp in copies:
            cp.wait_recv()
        for cp in copies:
            cp.wait_send()

    hbm = pl.BlockSpec(memory_space=pl.ANY)
    return pl.pallas_call(
        body, name=name, in_specs=[hbm] * n, out_specs=[hbm] * n,
        out_shape=[S((NCHIP,) + a.shape[1:], a.dtype) for a in arrs],
        scratch_shapes=[pltpu.SemaphoreType.DMA((NCHIP, n)), pltpu.SemaphoreType.DMA((NCHIP, n))],
    )(*arrs)


def _pair_add(mine, got, core, name):
    _, rows, cols = mine.shape
    tc = 256
    assert cols % tc == 0

    def body(core_ref, a_ref, b_ref, o_ref):
        o_ref[...] = (a_ref[...].astype(F32) + b_ref[...].astype(F32)).astype(BF16)

    return pl.pallas_call(
        body, name=name,
        grid_spec=pltpu.PrefetchScalarGridSpec(
            num_scalar_prefetch=1, grid=(NCHIP, cols // tc),
            in_specs=[pl.BlockSpec((None, rows, tc), lambda q, i, core_ref: (2 * q + core_ref[0], 0, i)),
                      pl.BlockSpec((None, rows, tc), lambda q, i, core_ref: (q, 0, i))],
            out_specs=pl.BlockSpec((None, rows, tc), lambda q, i, core_ref: (q, 0, i))),
        out_shape=S((NCHIP, rows, cols), BF16),
    )(core, mine, got)


def _chips_side(arrs):
    def plan(ins, outs, send, recv, local):
        n = len(ins)

        def places():
            x, y, c = lax.axis_index("x"), lax.axis_index("y"), lax.axis_index("c")
            return 2 * x + y, c, [(1 - x, y), (x, 1 - y), (1 - x, 1 - y)]

        def own():
            here, _, _ = places()
            return [pltpu.make_async_copy(ins[a].at[here], outs[a].at[here], local.at[a]) for a in range(n)]

        def remote(j, a, src_slot, dst_slot):
            _, c, chips = places()
            cx, cy = chips[j]
            return pltpu.make_async_remote_copy(
                src_ref=ins[a].at[src_slot], dst_ref=outs[a].at[dst_slot], send_sem=send.at[j, a],
                recv_sem=recv.at[j, a], device_id=(cx, cy, c), device_id_type=MESH)

        def sent():
            here, _, chips = places()
            return [remote(j, a, 2 * cx + cy, here) for j, (cx, cy) in enumerate(chips) for a in range(n)]

        def start():
            for cp in own() + sent():
                cp.start()

        def finish():
            here, _, chips = places()
            for j, (cx, cy) in enumerate(chips):
                for a in range(n):
                    remote(j, a, here, 2 * cx + cy).wait_recv()
            for cp in sent():
                cp.wait_send()
            for cp in own():
                cp.wait()

        return start, finish

    n = len(arrs)
    return dict(arrs=arrs, out_shape=[S(a.shape, a.dtype) for a in arrs],
                scratch=[pltpu.SemaphoreType.DMA((NCHIP - 1, n)), pltpu.SemaphoreType.DMA((NCHIP - 1, n)),
                         pltpu.SemaphoreType.DMA((n,))], plan=plan)


def _adamw(parts, w, m, v, name, tr, tc=None):
    rows, cols = w.shape
    if tc is None:
        assert rows % tr == 0
        grid, shape, at = (rows // tr,), (tr, cols), (lambda i: (i, 0))
    else:
        assert cols % tc == 0
        grid, shape, at = (cols // tc,), (rows, tc), (lambda i: (0, i))
    c1 = 1.0 - ADAM_B1 ** ADAM_STEP
    c2 = 1.0 - ADAM_B2 ** ADAM_STEP

    nparts = parts.shape[0]

    def body(p_ref, w_ref, m_ref, v_ref, g_ref, d_ref, mo_ref, vo_ref):
        g = p_ref[0].astype(F32)
        for s in range(1, nparts):
            g = g + p_ref[s].astype(F32)
        m_new = ADAM_B1 * m_ref[...] + (1.0 - ADAM_B1) * g
        v_new = ADAM_B2 * v_ref[...] + (1.0 - ADAM_B2) * (g * g)
        g_ref[...] = g
        mo_ref[...] = m_new
        vo_ref[...] = v_new
        d_ref[...] = -ADAM_LR * ((m_new / c1) / (jnp.sqrt(v_new / c2) + ADAM_EPS) + ADAM_WD * w_ref[...])

    blk = pl.BlockSpec(shape, at)
    return pl.pallas_call(
        body, name=name, grid=grid,
        in_specs=[pl.BlockSpec((nparts,) + shape, lambda i: (0,) + at(i)), blk, blk, blk],
        out_specs=[blk] * 4, out_shape=[S((rows, cols), F32)] * 4,
        compiler_params=pltpu.CompilerParams(dimension_semantics=("parallel",)),
    )(parts, w, m, v)


def _to_aligned(wt):
    pad = jnp.zeros((GLR_W - GLR_N, wt.shape[1]), wt.dtype)
    return jnp.concatenate([wt[O_QG:O_GLR], wt[O_ZG:O_GA], wt[O_GLR:O_ZG], pad, wt[O_ZA:O_QG], wt[O_GA:O_END],
                            wt[O_QA:O_ZA]], axis=0)


def _from_aligned(wt):
    return jnp.concatenate([wt[C_QA:], wt[C_ZA:C_GA], wt[C_QG:C_ZG], wt[C_GLR:C_GLR + GLR_N], wt[C_ZG:C_GLR],
                            wt[C_GA:C_QA]], axis=0)


def _col_blocks(w, width):
    return w.reshape(w.shape[0], NDEV, width).transpose(1, 0, 2)


def _from_col_blocks(w):
    return w.transpose(1, 0, 2).reshape(w.shape[1], NDEV * w.shape[2])


SMALL = (("norm_g", D), ("qk_norm_q", HD), ("qk_norm_k", HD), ("gla_gate_b", 512), ("gla_norm_g", GDV),
         ("ple_norm_g", D))
SMALL_PAD = 4096


def _local_step(x2, p2, pos, tgt, norm_g, qk_norm_q, qk_norm_k, gla_gate_b, gla_norm_g, ple_norm_g, w_al,
                weights=None, proj_side=None, unpack=None, dw_side_of=None, dh_side_of=None):
    half = ROT_DIM // 2
    inv8 = jnp.power(jnp.float32(ROPE_THETA), -jnp.arange(half, dtype=F32) * 2.0 / ROT_DIM)
    inv = jnp.tile(jnp.concatenate([inv8, inv8, jnp.zeros((HD - ROT_DIM,), F32)]), 2).reshape(1, 128)
    gq = jnp.tile(qk_norm_q, (1, 2))
    gk = jnp.tile(qk_norm_k, (1, 2))

    h = _rms_fwd(x2, norm_g, "rms1_fwd")
    if proj_side is None:
        proj = _mm(h, w_al, mode="nt", name="proj", tm=1024, tn=1536, tk=D)
    else:
        proj, got = _mm(h, w_al, mode="nt", name="proj", tm=1024, tn=1536, tk=D, side=proj_side)
        weights = unpack(got)
    w2p, w_att_f, w_gla_f, w_out_f, w_pg_f, w_ple_f = weights
    qkv = _qk_prep(proj, pos, inv, gq, gk)
    fwd = [_att_fwd(qkv[g], qkv[3 + g], qkv[6 + g], g, f"att_fwd{g}") for g in range(3)]
    att, lse, ain = _att_merge([f[0] for f in fwd], [f[1] for f in fwd], proj)
    o_gla, bin_, states = _gla_fwd(proj, w2p, gla_gate_b, gla_norm_g)
    ya = _mm(ain, w_att_f, mode="nn", name="ya", tm=1024, tn=D, tk=512, out_dtype=BF16)
    yb = _mm(bin_, w_gla_f, mode="nn", name="yb", tm=1024, tn=D, tk=D, out_dtype=BF16)
    y = _merge_fwd(ya, yb, proj)
    x1 = _mm(y, w_out_f, mode="nn", name="x1", tm=1024, tn=D, tk=D, res=x2)
    n2 = _rms_fwd(x1, ple_norm_g, "rms2_fwd")
    u = _mm(n2, w_pg_f, mode="nn", name="ple_u", tm=1024, tn=D, tk=D)
    e = _mm(p2, w_ple_f, mode="nn", name="ple_e", tm=1024, tn=D, tk=PLE)
    loss_v, dout, de, du = _loss_head(x1, e, u, tgt)

    dw_ple = _mm(p2, de, mode="tn", name="dw_ple", tm=PLE, tn=D, tk=512)
    dw_pg = _mm(n2, du, mode="tn", name="dw_pg", tm=D, tn=D, tk=512)
    dn2 = _mm(du, w_pg_f, mode="nt", name="dn2", tm=1024, tn=D, tk=D)
    dx1, dx1b, dg_ple = _rms_bwd(dn2, x1, ple_norm_g, dout, "rms2_bwd")
    dw_out = _mm(y, dx1b, mode="tn", name="dw_out", tm=D, tn=D, tk=512)
    dy = _mm(dx1b, w_out_f, mode="nt", name="dy", tm=1024, tn=D, tk=D, out_dtype=BF16)
    dproj, dya, dyb = _merge_bwd(dy, ya, yb, proj)
    dw_att = _mm(ain, dya, mode="tn", name="dw_att", tm=512, tn=D, tk=512)
    dain = _mm(dya, w_att_f, mode="nt", name="dain", tm=1024, tn=512, tk=D, out_dtype=BF16)
    dw_gla = _mm(bin_, dyb, mode="tn", name="dw_gla", tm=D, tn=D, tk=512)
    dbin = _mm(dyb, w_gla_f, mode="nt", name="dbin", tm=1024, tn=D, tk=D, out_dtype=BF16)
    dproj, da0, da1, da2, at1, at2, ls1, ls2 = _att_gate_bwd(dain, att, lse, proj, dproj)
    datts, atts, lses = (da0, da1, da2), (att[None], at1, at2), (lse[None], ls1, ls2)
    dproj, dw2, dbg, dgn = _gla_bwd(proj, w2p, gla_gate_b, gla_norm_g, o_gla, states, dbin, dproj)
    bwd = [_att_bwd(qkv[g], qkv[3 + g], qkv[6 + g], datts[g], atts[g], lses[g], g, f"att_bwd{g}") for g in range(3)]
    dproj, dgq, dgk = _qk_bwd(proj, pos, inv, gq, gk, [b[0] for b in bwd], [b[1] for b in bwd],
                              [b[2] for b in bwd], dproj)
    out = dict(loss=loss_v, dw2=dw2, dw_att=dw_att, dw_gla=dw_gla, dw_out=dw_out, dw_pg=dw_pg, dw_ple=dw_ple,
               dgq=dgq, dgk=dgk, dbg=dbg, dgn=dgn, dg_ple=dg_ple)
    if dw_side_of is None:
        dw_al = _mm(dproj, h, mode="tn", name="dw_in", tm=1536, tn=D, tk=2048)
    else:
        dw_al, out["dw_side"] = _mm(dproj, h, mode="tn", name="dw_in", tm=1536, tn=D, tk=2048, side=dw_side_of(out))
    if dh_side_of is None:
        dh = _mm(dproj, w_al, mode="nn", name="dh", tm=1024, tn=D, tk=3584)
    else:
        dh, out["dh_side"] = _mm(dproj, w_al, mode="nn", name="dh", tm=1024, tn=D, tk=3584, side=dh_side_of(dw_al))
    grad_x, _, dg_norm = _rms_bwd(dh, x2, norm_g, dx1, "rms1_bwd")
    out.update(grad_x=grad_x, dw_al=dw_al, dg_norm=dg_norm)
    return out


def kernel(x, p, positions, norm_g, w_in, qk_norm_q, qk_norm_k, gla_gate_w2, gla_gate_b, gla_norm_g, w_att_proj, w_gla_proj, w_out, ple_norm_g, w_ple_gate, w_ple, loss_target, m_norm_g, m_w_in, m_qk_norm_q, m_qk_norm_k, m_gla_gate_w2, m_gla_gate_b, m_gla_norm_g, m_w_att_proj, m_w_gla_proj, m_w_out, m_ple_norm_g, m_w_ple_gate, m_w_ple, v_norm_g, v_w_in, v_qk_norm_q, v_qk_norm_k, v_gla_gate_w2, v_gla_gate_b, v_gla_norm_g, v_w_att_proj, v_w_gla_proj, v_w_out, v_ple_norm_g, v_w_ple_gate, v_w_ple):
    x2, p2, tgt = x[0], p[0, 0], loss_target[0]
    pos = positions.astype(F32).reshape(T, 1)

    rows3 = jnp.stack([w_gla_proj[0], w_out[0], w_ple_gate[0]]).astype(BF16)
    cols3 = jnp.concatenate([w_att_proj[0], w_ple[0], jnp.pad(gla_gate_w2[0], ((0, 0), (0, 64)))], axis=0).astype(BF16)
    wt, mt, vt = w_in[0].T, m_w_in[0].T, v_w_in[0].T
    (g_in,) = _all_gather_by_chip([wt.astype(BF16)], "gather_w_in")
    w_al = _to_aligned(g_in.reshape(W_IN_COLS, D))

    def unpack(got):
        g_rows, g_cols = got
        w2_f = _from_col_blocks(g_cols[:, 768:784, :64])
        return (jnp.pad(w2_f, ((0, GLR_W - GLR_N), (0, 0))), _from_col_blocks(g_cols[:, :512]),
                g_rows[:, 0].reshape(D, D), g_rows[:, 1].reshape(D, D), g_rows[:, 2].reshape(D, D),
                _from_col_blocks(g_cols[:, 512:768]))

    def dw_side_of(g):
        s_rows = jnp.concatenate([g[k].reshape(NDEV, 128, D) for k in ("dw_gla", "dw_out", "dw_pg")], axis=1)
        s_cols = jnp.concatenate([_col_blocks(g["dw_att"], 128), _col_blocks(g["dw_ple"], 128),
                                  jnp.pad(_col_blocks(g["dw2"][:GLR_N], 64), ((0, 0), (0, 0), (0, 64)))], axis=1)
        return _exchange_side([s_rows.astype(BF16), s_cols.astype(BF16)])

    def dh_side_of(dw_al):
        s_in = _from_aligned(dw_al).astype(BF16).reshape(NDEV, W_IN_SHARD, D)
        (from_sibling,) = _exchange_sibling([s_in], "exchange_sibling")
        core = lax.axis_index("c").astype(jnp.int32).reshape(1)
        return _chips_side([_pair_add(s_in, from_sibling, core, "pair_add")])

    loc = _local_step(x2, p2, pos, tgt, norm_g, qk_norm_q, qk_norm_k, gla_gate_b, gla_norm_g, ple_norm_g, w_al,
                      proj_side=_gather_side([rows3, cols3]), unpack=unpack, dw_side_of=dw_side_of,
                      dh_side_of=dh_side_of)
    loss_v, grad_x = loc["loss"], loc["grad_x"]
    dg_norm, dgq, dgk, dbg, dgn, dg_ple = (loc[k] for k in ("dg_norm", "dgq", "dgk", "dbg", "dgn", "dg_ple"))
    r_rows, r_cols = loc["dw_side"]
    (r_in,) = loc["dh_side"]

    small = jnp.concatenate([dg_norm[0], dgq[0, :HD], dgk[0, :HD], dbg[0], dgn[0], dg_ple[0]])
    small = jnp.pad(small, (0, SMALL_PAD - small.shape[0])).reshape(1, 8, SMALL_PAD // 8)
    (r_small,) = _comm_call(_gather_side([small]), "gather_small")

    outs = {}

    def adam(nm, parts, w, m, v, tr):
        outs[nm] = _adamw(parts, w, m, v, "adam_" + nm, tr)

    outs["w_in"] = [o.T for o in _adamw(r_in, wt, mt, vt, "adam_w_in", None, tc=128)]
    adam("w_gla_proj", r_rows[:, :128], w_gla_proj[0], m_w_gla_proj[0], v_w_gla_proj[0], 128)
    adam("w_out", r_rows[:, 128:256], w_out[0], m_w_out[0], v_w_out[0], 128)
    adam("w_ple_gate", r_rows[:, 256:], w_ple_gate[0], m_w_ple_gate[0], v_w_ple_gate[0], 128)
    adam("w_att_proj", r_cols[:, :512], w_att_proj[0], m_w_att_proj[0], v_w_att_proj[0], 512)
    adam("w_ple", r_cols[:, 512:768], w_ple[0], m_w_ple[0], v_w_ple[0], 256)
    adam("gla_gate_w2", r_cols[:, 768:784, :64], gla_gate_w2[0], m_gla_gate_w2[0], v_gla_gate_w2[0], 16)
    given = dict(norm_g=(norm_g, m_norm_g, v_norm_g), qk_norm_q=(qk_norm_q, m_qk_norm_q, v_qk_norm_q),
                 qk_norm_k=(qk_norm_k, m_qk_norm_k, v_qk_norm_k), gla_gate_b=(gla_gate_b, m_gla_gate_b, v_gla_gate_b),
                 gla_norm_g=(gla_norm_g, m_gla_norm_g, v_gla_norm_g), ple_norm_g=(ple_norm_g, m_ple_norm_g, v_ple_norm_g))

    def pack(i):
        flat = jnp.concatenate([given[nm][i][0] for nm, _ in SMALL])
        return jnp.pad(flat, (0, SMALL_PAD - flat.shape[0])).reshape(8, SMALL_PAD // 8)

    sm = _adamw(r_small.reshape(NDEV, 8, SMALL_PAD // 8), pack(0), pack(1), pack(2), "adam_small", 8)
    off = 0
    for nm, width in SMALL:
        outs[nm] = [o.reshape(-1)[off:off + width] for o in sm]
        off += width

    loss = lax.psum(loss_v[0, 0], ("x", "y", "c"))
    order = ["norm_g", "w_in", "qk_norm_q", "qk_norm_k", "gla_gate_w2", "gla_gate_b", "gla_norm_g", "w_att_proj",
             "w_gla_proj", "w_out", "ple_norm_g", "w_ple_gate", "w_ple"]
    result = [loss, grad_x[None]]
    for i in range(4):
        result += [outs[nm][i][None] for nm in order]
    return tuple(result)
```

```python
import functools

import jax
import jax.numpy as jnp
from jax import lax
from jax.experimental import pallas as pl
from jax.experimental.pallas import tpu as pltpu

F32 = jnp.float32
BF16 = jnp.bfloat16
S = jax.ShapeDtypeStruct

T = 4096
D = 1024
NDEV = 8
HD = 64
ATT_W = 512
ATT_QKV = 1536
DILATIONS = (1, 4, 16)
BLK = 128
GH, GDK, GDV = 4, 128, 256
GLA_C = 128
PLE = 256
EPS = 1e-6
ROT_DIM = 16
ROPE_THETA = 500000.0
GLA_TAU = 16.0
W_IN_COLS = 10256
W_IN_SHARD = 1282

C_QG, C_KG, C_VG, C_ZG, C_GLR, C_ZA, C_GA, C_GB, C_QA, C_KA, C_VA = (
    0, 512, 1024, 2048, 3072, 3584, 4096, 5120, 6144, 7680, 9216)
GLA_GROUP_W = 3584
GLR_W = 512
NCOL = 10752
GLR_N = 16
O_QA, O_ZA, O_QG, O_GLR, O_ZG, O_GA, O_END = 0, 4608, 5120, 7168, 7184, 8208, 10256

ADAM_LR, ADAM_B1, ADAM_B2, ADAM_EPS, ADAM_WD, ADAM_STEP = 0.001, 0.9, 0.999, 1e-08, 0.01, 10

MESH = pl.DeviceIdType.MESH


def _sigmoid(z):
    return 1.0 / (1.0 + jnp.exp(-z))


def _dot(a, b, dims):
    return lax.dot_general(a, b, (dims, ((), ())), preferred_element_type=F32)


def _nn(a, b):
    return _dot(a, b, ((1,), (0,)))


def _nt(a, b):
    return _dot(a, b, ((1,), (1,)))


def _tn(a, b):
    return _dot(a, b, ((0,), (0,)))


def _mm(a, b, *, mode, name, tm, tn, tk, out_dtype=F32, res=None, side=None):
    if mode == "nn":
        (m, k), n = a.shape, b.shape[1]
        a_spec = pl.BlockSpec((tm, tk), lambda i, j, l: (i, l))
        b_spec = pl.BlockSpec((tk, tn), lambda i, j, l: (l, j))
        dot = _nn
    elif mode == "nt":
        (m, k), n = a.shape, b.shape[0]
        a_spec = pl.BlockSpec((tm, tk), lambda i, j, l: (i, l))
        b_spec = pl.BlockSpec((tn, tk), lambda i, j, l: (j, l))
        dot = _nt
    else:
        (k, m), n = a.shape, b.shape[1]
        a_spec = pl.BlockSpec((tk, tm), lambda i, j, l: (l, i))
        b_spec = pl.BlockSpec((tk, tn), lambda i, j, l: (l, j))
        dot = _tn
    assert m % tm == 0 and n % tn == 0 and k % tk == 0, (name, m, n, k)
    grid = (m // tm, n // tn, k // tk)
    nk = grid[2]
    o_spec = pl.BlockSpec((tm, tn), lambda i, j, l: (i, j))
    in_specs = [a_spec, b_spec]
    args = [a, b]
    if res is not None:
        in_specs.append(o_spec)
        args.append(res)
    n_in = len(args)
    n_side = 0 if side is None else len(side["arrs"])
    hbm = pl.BlockSpec(memory_space=pl.ANY)

    def body(*refs):
        a_ref, b_ref = refs[0], refs[1]
        r_ref = refs[2] if res is not None else None
        o_ref = refs[n_in + n_side]
        scratch = refs[n_in + 2 * n_side + 1:]
        if side is not None:
            start, finish_side = side["plan"](refs[n_in:n_in + n_side], refs[n_in + n_side + 1:n_in + 2 * n_side + 1],
                                              *scratch[1 if nk > 1 else 0:])
            ids = [pl.program_id(d) for d in range(3)]

            @pl.when((ids[0] == 0) & (ids[1] == 0) & (ids[2] == 0))
            def _():
                start()

        part = dot(a_ref[...].astype(BF16), b_ref[...].astype(BF16))

        def finish(val):
            if r_ref is not None:
                val = val + r_ref[...]
            o_ref[...] = val.astype(out_dtype)

        if nk == 1:
            finish(part)
        else:
            acc = scratch[0]
            l = pl.program_id(2)

            @pl.when(l == 0)
            def _():
                acc[...] = part

            @pl.when(l > 0)
            def _():
                acc[...] += part

            @pl.when(l == nk - 1)
            def _():
                finish(acc[...])

        if side is not None:
            @pl.when((ids[0] == grid[0] - 1) & (ids[1] == grid[1] - 1) & (ids[2] == grid[2] - 1))
            def _():
                finish_side()

    sems = [] if side is None else side["scratch"]
    outs = pl.pallas_call(
        body, name=name, grid=grid,
        in_specs=in_specs + [hbm] * n_side, out_specs=[o_spec] + [hbm] * n_side,
        out_shape=[S((m, n), out_dtype)] + ([] if side is None else side["out_shape"]),
        scratch_shapes=([pltpu.VMEM((tm, tn), F32)] if nk > 1 else []) + sems,
        compiler_params=pltpu.CompilerParams(
            dimension_semantics=("arbitrary",) * 3 if side is not None else ("parallel", "parallel", "arbitrary")),
    )(*args, *([] if side is None else side["arrs"]))
    return outs[0] if side is None else (outs[0], outs[1:])


def _rows(arr, width=None, cblk=0):
    return ("rows", arr, arr.shape[1] if width is None else width, cblk)


def _whole(arr):
    return ("whole", arr)


def _rowcall(body, name, tt, ins, outs, scratch=()):
    in_specs, args = [], []
    for spec in ins:
        if spec[0] == "rows":
            _, arr, width, cblk = spec
            in_specs.append(pl.BlockSpec((tt, width), functools.partial(lambda i, c: (i, c), c=cblk)))
        else:
            arr = spec[1]
            in_specs.append(pl.BlockSpec(arr.shape, functools.partial(lambda i, nd: (0,) * nd, nd=arr.ndim)))
        args.append(arr)
    out_specs, out_shape = [], []
    for kind, shape, dtype in outs:
        if kind == "rows":
            out_specs.append(pl.BlockSpec((tt, shape), lambda i: (i, 0)))
            out_shape.append(S((T, shape), dtype))
        else:
            out_specs.append(pl.BlockSpec(shape, functools.partial(lambda i, nd: (0,) * nd, nd=len(shape))))
            out_shape.append(S(shape, dtype))
    return pl.pallas_call(
        body, name=name, grid=(T // tt,), in_specs=in_specs, out_specs=out_specs, out_shape=out_shape,
        scratch_shapes=list(scratch),
        compiler_params=pltpu.CompilerParams(dimension_semantics=("arbitrary",)),
    )(*args)


def _rms_fwd(x, g, name):
    def body(x_ref, g_ref, h_ref):
        xf = x_ref[...]
        r = lax.rsqrt(jnp.mean(xf * xf, axis=-1, keepdims=True) + EPS)
        h_ref[...] = (xf * r * g_ref[...]).astype(BF16)

    return _rowcall(body, name, 512, [_rows(x), _whole(g)], [("rows", D, BF16)])[0]


def _rms_bwd(dn, x, g, skip, name):
    def body(dn_ref, x_ref, g_ref, s_ref, dx_ref, dxb_ref, dg_ref):
        xf = x_ref[...]
        r = lax.rsqrt(jnp.mean(xf * xf, axis=-1, keepdims=True) + EPS)
        dn_ = dn_ref[...]
        u = dn_ * g_ref[...]
        dx = s_ref[...] + r * u - xf * (r * r * r) * jnp.mean(u * xf, axis=-1, keepdims=True)
        dx_ref[...] = dx
        dxb_ref[...] = dx.astype(BF16)
        part = jnp.sum(dn_ * xf * r, axis=0, keepdims=True)

        @pl.when(pl.program_id(0) == 0)
        def _():
            dg_ref[...] = part

        @pl.when(pl.program_id(0) > 0)
        def _():
            dg_ref[...] += part

    return _rowcall(body, name, 256, [_rows(dn), _rows(x), _whole(g), _rows(skip)],
                    [("rows", D, F32), ("rows", D, BF16), ("acc", (1, D), F32)])


def _rot_tables(pos_ref, inv_ref):
    lane = lax.broadcasted_iota(jnp.int32, (1, 128), 1) % HD
    ang = pos_ref[...] * inv_ref[...]
    cos, sin = jnp.cos(ang), jnp.sin(ang)
    c = jnp.where(lane < ROT_DIM, cos, 1.0)
    sp = jnp.where((lane >= ROT_DIM // 2) & (lane < ROT_DIM), sin, 0.0)
    sm = jnp.where(lane < ROT_DIM // 2, -sin, 0.0)
    return c, sp, sm


def _head_sums(v):
    same = (lax.broadcasted_iota(jnp.int32, (128, 128), 0) < HD) == (lax.broadcasted_iota(jnp.int32, (128, 128), 1) < HD)
    ones = jnp.where(same, 1.0, 0.0).astype(BF16)
    hi = v.astype(BF16)
    lo = (v - hi.astype(F32)).astype(BF16)
    return _nn(hi, ones) + _nn(lo, ones)


def _pair_norm(t):
    return lax.rsqrt(_head_sums(t * t) * (1.0 / HD) + EPS)


def _pair_mean(t):
    return _head_sums(t) * (1.0 / HD)


TT = 256
NCH = ATT_QKV // 128


def _res_shape(grp, dtype):
    return S((DILATIONS[grp], T // DILATIONS[grp], ATT_W), dtype)


def _res_spec(grp):
    dil = DILATIONS[grp]
    return pl.BlockSpec((dil, TT // dil, ATT_W), lambda i: (0, i, 0))


def _to_residues(sc, j, dst_ref, dil, cols):
    n = TT // dil
    for r in range(dil):
        rows = sc[j] if dil == 1 else sc.at[j][pl.ds(r, n, stride=dil), :]
        dst_ref[r, :, cols] = rows.astype(dst_ref.dtype)


def _from_residues(src_ref, cols, sc, j, dil):
    n = TT // dil
    for r in range(dil):
        if dil == 1:
            sc[j] = src_ref[r, :, cols]
        else:
            sc.at[j][pl.ds(r, n, stride=dil), :] = src_ref[r, :, cols]


def _tok_spec(width, cblk=0):
    return pl.BlockSpec((TT, width), functools.partial(lambda i, c: (i, c), c=cblk))


def _const_spec(arr_or_shape):
    shape = arr_or_shape if isinstance(arr_or_shape, tuple) else arr_or_shape.shape
    return pl.BlockSpec(shape, functools.partial(lambda i, nd: (0,) * nd, nd=len(shape)))


def _qk_prep(proj, pos, inv, gq, gk):
    def body(q_ref, k_ref, v_ref, pos_ref, inv_ref, gq_ref, gk_ref, *rest):
        outs, sc = rest[:9], rest[9]
        c, sp, sm = _rot_tables(pos_ref, inv_ref)
        for which, (src, g_ref) in enumerate(((q_ref, gq_ref), (k_ref, gk_ref), (v_ref, None))):
            if g_ref is not None:
                g = jnp.broadcast_to(g_ref[...] * ((HD ** -0.5) if which == 0 else 1.0), c.shape)
                cg, spg, smg = c * g, sp * pltpu.roll(g, 8, 1), sm * pltpu.roll(g, 120, 1)
            for j in range(NCH):
                t = src[:, j * 128:(j + 1) * 128]
                if g_ref is not None:
                    t = _pair_norm(t) * (t * cg + pltpu.roll(t, 8, 1) * spg + pltpu.roll(t, 120, 1) * smg)
                sc[j] = t
            for j in range(NCH):
                grp, sub = divmod(j * 128, ATT_W)
                _to_residues(sc, j, outs[which * 3 + grp], DILATIONS[grp], slice(sub, sub + 128))

    return pl.pallas_call(
        body, name="qk_prep", grid=(T // TT,),
        in_specs=[_tok_spec(ATT_QKV, C_QA // ATT_QKV), _tok_spec(ATT_QKV, C_KA // ATT_QKV),
                  _tok_spec(ATT_QKV, C_VA // ATT_QKV), _tok_spec(1), _const_spec(inv), _const_spec(gq), _const_spec(gk)],
        out_specs=[_res_spec(g) for _ in range(3) for g in range(3)],
        out_shape=[_res_shape(g, BF16) for _ in range(3) for g in range(3)],
        scratch_shapes=[pltpu.VMEM((NCH, TT, 128), F32)],
        compiler_params=pltpu.CompilerParams(dimension_semantics=("arbitrary",)),
    )(proj, proj, proj, pos, inv, gq, gk)


def _qk_bwd(proj, pos, inv, gq, gk, dqs, dks, dvs, dproj):
    const = lambda a: pl.BlockSpec(a.shape, functools.partial(lambda i, p, nd: (0,) * nd, nd=a.ndim))
    res = lambda g: pl.BlockSpec((DILATIONS[g], TT // DILATIONS[g], ATT_W), lambda i, p: (0, i, 0))
    base = C_QA // ATT_QKV

    def body(t_ref, pos_ref, inv_ref, gq_ref, gk_ref, dq0, dq1, dq2, dk0, dk1, dk2, dv0, dv1, dv2, buf_ref,
             out_ref, dgq_ref, dgk_ref, sc):
        del buf_ref
        part = pl.program_id(1)
        first = pl.program_id(0) == 0

        def gather(drefs):
            for j in range(NCH):
                grp, sub = divmod(j * 128, ATT_W)
                _from_residues(drefs[grp], slice(sub, sub + 128), sc, j, DILATIONS[grp])

        def normed(g_ref, drefs, dg_ref):
            c, sp, sm = _rot_tables(pos_ref, inv_ref)
            gather(drefs)
            dg = jnp.zeros((1, 128), F32)
            for j in range(NCH):
                cols = slice(j * 128, (j + 1) * 128)
                d_rot = sc[j]
                dn = d_rot * c + pltpu.roll(d_rot * sp, 120, 1) + pltpu.roll(d_rot * sm, 8, 1)
                t = t_ref[:, cols]
                r = _pair_norm(t)
                u = dn * g_ref[...]
                out_ref[:, cols] = (r * u - t * (r * r * r) * _pair_mean(u * t)).astype(BF16)
                dg = dg + jnp.sum(dn * t * r, axis=0, keepdims=True)
            dg = dg + pltpu.roll(dg, HD, 1)

            @pl.when(first)
            def _():
                dg_ref[...] = dg

            @pl.when(jnp.logical_not(first))
            def _():
                dg_ref[...] += dg

        @pl.when(part == 0)
        def _():
            gather((dv0, dv1, dv2))
            for j in range(NCH):
                out_ref[:, j * 128:(j + 1) * 128] = sc[j].astype(BF16)

        @pl.when(part == 1)
        def _():
            normed(gq_ref, (dq0, dq1, dq2), dgq_ref)

        @pl.when(part == 2)
        def _():
            normed(gk_ref, (dk0, dk1, dk2), dgk_ref)

    keep = pl.BlockSpec((1, 128), lambda i, p: (0, 0))
    return pl.pallas_call(
        body, name="qk_bwd", grid=(T // TT, 3),
        in_specs=[pl.BlockSpec((TT, ATT_QKV), lambda i, p: (i, base + jnp.maximum(p - 1, 0))),
                  pl.BlockSpec((TT, 1), lambda i, p: (i, 0)), const(inv), const(gq), const(gk)]
        + [res(g) for _ in range(3) for g in range(3)] + [pl.BlockSpec(memory_space=pl.ANY)],
        out_specs=[pl.BlockSpec((TT, ATT_QKV), lambda i, p: (i, base + jnp.where(p == 0, 2, p - 1))), keep, keep],
        out_shape=[S(dproj.shape, dproj.dtype), S((1, 128), F32), S((1, 128), F32)],
        input_output_aliases={14: 0},
        scratch_shapes=[pltpu.VMEM((NCH, TT, 128), F32)],
        compiler_params=pltpu.CompilerParams(dimension_semantics=("arbitrary", "arbitrary")),
    )(proj, pos, inv, gq, gk, *dqs, *dks, *dvs, dproj)


def _split_heads(t):
    low = lax.broadcasted_iota(jnp.int32, (1, 128), 1) < HD
    zero = jnp.zeros_like(t)
    return jnp.concatenate([jnp.where(low, t, zero), jnp.where(low, zero, t)], axis=0)


def _join_heads(t2):
    low = lax.broadcasted_iota(jnp.int32, (1, 128), 1) < HD
    n = t2.shape[0] // 2
    return jnp.where(low, t2[:n], t2[n:])


def _band_mask4(has_before, has_own):
    row = lax.broadcasted_iota(jnp.int32, (BLK, 4 * BLK), 0)
    lane = lax.broadcasted_iota(jnp.int32, (BLK, 4 * BLK), 1)
    key = lane & (BLK - 1)
    own = lane >= 2 * BLK
    return (own & (key <= row) & has_own) | (jnp.logical_not(own) & (key >= row) & has_before)


def _band_mask_before(has_before):
    row = lax.broadcasted_iota(jnp.int32, (BLK, 2 * BLK), 0)
    key = lax.broadcasted_iota(jnp.int32, (BLK, 2 * BLK), 1) & (BLK - 1)
    return (key >= row) & has_before


def _per_head(width, col_a, col_b):
    lane = lax.broadcasted_iota(jnp.int32, (1, width), 1)
    return jnp.where((lane & BLK) == 0, col_a, col_b)


NQ = ATT_W // 128


def _att_fwd(q, k, v, grp, name):
    dil = DILATIONS[grp]
    nb = T // dil // BLK

    def body(q_ref, kp_ref, kc_ref, vp_ref, vc_ref, o_ref, lse_ref, s_sc, p_sc):
        mask = _band_mask4(pl.program_id(1) > 0, True)
        low = lax.broadcasted_iota(jnp.int32, (1, 128), 1) < HD
        halves = lambda ref, j, h: (ref[j, :, h * BLK:(h + 1) * BLK], ref[j, :, (h + 2) * BLK:(h + 3) * BLK])
        for j in range(NQ):
            cols = slice(j * 128, (j + 1) * 128)
            k4 = jnp.concatenate([_split_heads(kp_ref[:, cols]), _split_heads(kc_ref[:, cols])], axis=0)
            s_sc[j] = jnp.where(mask, _nt(q_ref[:, cols], k4), -jnp.inf)
        mxs = [[jnp.maximum(*(jnp.max(t, axis=-1, keepdims=True) for t in halves(s_sc, j, h))) for h in range(2)]
               for j in range(NQ)]
        dens = []
        for j in range(NQ):
            p = jnp.exp(s_sc[j] - _per_head(4 * BLK, *mxs[j]))
            p_sc[j] = p.astype(BF16)
            dens.append([jnp.sum(p[:, h * BLK:(h + 1) * BLK], axis=-1, keepdims=True)
                         + jnp.sum(p[:, (h + 2) * BLK:(h + 3) * BLK], axis=-1, keepdims=True) for h in range(2)])
        for j in range(NQ):
            cols = slice(j * 128, (j + 1) * 128)
            v4 = jnp.concatenate([_split_heads(vp_ref[:, cols]), _split_heads(vc_ref[:, cols])], axis=0)
            o_ref[:, cols] = _nn(p_sc[j], v4) / jnp.where(low, dens[j][0], dens[j][1])
            lse_ref[:, cols] = jnp.where(low, mxs[j][0] + jnp.log(dens[j][0]), mxs[j][1] + jnp.log(dens[j][1]))

    cur = pl.BlockSpec((None, BLK, ATT_W), lambda r, i: (r, i, 0))
    prev = pl.BlockSpec((None, BLK, ATT_W), lambda r, i: (r, jnp.maximum(i - 1, 0), 0))
    return pl.pallas_call(
        body, name=name, grid=(dil, nb),
        in_specs=[cur, prev, cur, prev, cur],
        out_specs=[cur, cur], out_shape=[_res_shape(grp, F32)] * 2,
        scratch_shapes=[pltpu.VMEM((NQ, BLK, 4 * BLK), F32), pltpu.VMEM((NQ, BLK, 4 * BLK), BF16)],
        compiler_params=pltpu.CompilerParams(dimension_semantics=("parallel", "arbitrary")),
    )(q, k, k, v, v)


def _att_bwd(q, k, v, datt, att, lse, grp, name):
    dil = DILATIONS[grp]
    nb = T // dil // BLK
    scale = HD ** -0.5

    def body(q0_ref, q1_ref, kp_ref, kc_ref, vp_ref, vc_ref, do0_ref, do1_ref, o0_ref, o1_ref, l0_ref, l1_ref,
             dq_ref, dk_ref, dv_ref, k4_sc, v4_sc, s0_sc, s1_sc, dp0_sc, dp1_sc, p_sc, ds_sc):
        i = pl.program_id(1)
        mask_mine = _band_mask4(i > 0, True)
        mask_next = _band_mask_before(i < nb - 1)
        low = lax.broadcasted_iota(jnp.int32, (1, 128), 1) < HD
        for j in range(NQ):
            cols = slice(j * 128, (j + 1) * 128)
            k4_sc[j, :2 * BLK] = _split_heads(kp_ref[:, cols])
            k4_sc[j, 2 * BLK:] = _split_heads(kc_ref[:, cols])
            v4_sc[j, :2 * BLK] = _split_heads(vp_ref[:, cols])
            v4_sc[j, 2 * BLK:] = _split_heads(vc_ref[:, cols])
        for j in range(NQ):
            cols = slice(j * 128, (j + 1) * 128)
            s0_sc[j] = _nt(q0_ref[:, cols], k4_sc[j])
            s1_sc[j] = _nt(q1_ref[:, cols], k4_sc[j, 2 * BLK:])
            dp0_sc[j] = _nt(do0_ref[:, cols].astype(BF16), v4_sc[j])
            dp1_sc[j] = _nt(do1_ref[:, cols].astype(BF16), v4_sc[j, 2 * BLK:])
        stats = []
        for j in range(NQ):
            cols = slice(j * 128, (j + 1) * 128)
            for do_ref, o_ref, l_ref in ((do0_ref, o0_ref, l0_ref), (do1_ref, o1_ref, l1_ref)):
                prod = do_ref[:, cols].astype(F32) * o_ref[:, cols].astype(F32)
                d_all = jnp.sum(prod, axis=-1, keepdims=True)
                d_low = jnp.sum(jnp.where(low, prod, 0.0), axis=-1, keepdims=True)
                lse_t = l_ref[:, cols]
                stats.append((d_low, d_all - d_low, lse_t[:, 0:1], lse_t[:, HD:HD + 1]))
        for j in range(NQ):
            (da, db, la, lb), (da1, db1, la1, lb1) = stats[2 * j], stats[2 * j + 1]
            p0 = jnp.where(mask_mine, jnp.exp(s0_sc[j] - _per_head(4 * BLK, la, lb)), 0.0)
            ds0 = p0 * (dp0_sc[j] - _per_head(4 * BLK, da, db))
            p1 = jnp.where(mask_next, jnp.exp(s1_sc[j] - _per_head(2 * BLK, la1, lb1)), 0.0)
            ds1 = p1 * (dp1_sc[j] - _per_head(2 * BLK, da1, db1))
            p_sc[j, :BLK] = p0.astype(BF16)
            ds_sc[j, :BLK] = ds0.astype(BF16)
            p_sc[j, BLK:, 2 * BLK:] = p1.astype(BF16)
            ds_sc[j, BLK:, 2 * BLK:] = ds1.astype(BF16)
        for j in range(NQ):
            cols = slice(j * 128, (j + 1) * 128)
            dq_ref[:, cols] = _nn(ds_sc[j, :BLK], k4_sc[j]) * scale
            qq = jnp.concatenate([q0_ref[:, cols], q1_ref[:, cols]], axis=0)
            dd = jnp.concatenate([do0_ref[:, cols], do1_ref[:, cols]], axis=0).astype(BF16)
            dk_ref[:, cols] = _join_heads(_tn(ds_sc[j, :, 2 * BLK:], qq))
            dv_ref[:, cols] = _join_heads(_tn(p_sc[j, :, 2 * BLK:], dd))

    def spec(shift):
        return pl.BlockSpec((None, BLK, ATT_W), lambda r, i: (r, jnp.clip(i + shift, 0, nb - 1), 0))

    here, after, before = spec(0), spec(1), spec(-1)
    vm = pltpu.VMEM
    return pl.pallas_call(
        body, name=name, grid=(dil, nb),
        in_specs=[here, after, before, here, before, here, here, after, here, after, here, after],
        out_specs=[here] * 3, out_shape=[_res_shape(grp, F32)] * 3,
        scratch_shapes=[vm((NQ, 4 * BLK, 128), BF16), vm((NQ, 4 * BLK, 128), BF16), vm((NQ, BLK, 4 * BLK), F32),
                        vm((NQ, BLK, 2 * BLK), F32), vm((NQ, BLK, 4 * BLK), F32), vm((NQ, BLK, 2 * BLK), F32),
                        vm((NQ, 2 * BLK, 4 * BLK), BF16), vm((NQ, 2 * BLK, 4 * BLK), BF16)],
        compiler_params=pltpu.CompilerParams(dimension_semantics=("parallel", "arbitrary")),
    )(q, q, k, k, v, v, datt, datt, att, att, lse, lse)


def _att_merge(os_, lses, proj):
    nq = ATT_W // 128

    def body(o0, o1, o2, l0, l1, l2, za_ref, att_ref, lse_ref, ain_ref, sc):
        for a, ref in enumerate((o0, o1, o2, l0, l1, l2)):
            for j in range(nq):
                _from_residues(ref, slice(j * 128, (j + 1) * 128), sc, a * nq + j, DILATIONS[a % 3])
        for j in range(nq):
            cols = slice(j * 128, (j + 1) * 128)
            oa, ob, oc = (sc[a * nq + j] for a in range(3))
            la, lb, lc = (sc[(3 + a) * nq + j] for a in range(3))
            m = jnp.maximum(jnp.maximum(la, lb), lc)
            wa, wb, wc = jnp.exp(la - m), jnp.exp(lb - m), jnp.exp(lc - m)
            tot = wa + wb + wc
            att = (wa * oa + wb * ob + wc * oc) / tot
            att_ref[:, cols] = att
            lse_ref[:, cols] = m + jnp.log(tot)
            za = za_ref[:, cols]
            ain_ref[:, cols] = (att * za * _sigmoid(za)).astype(BF16)

    return pl.pallas_call(
        body, name="att_merge", grid=(T // TT,),
        in_specs=[_res_spec(g) for _ in range(2) for g in range(3)] + [_tok_spec(ATT_W, C_ZA // ATT_W)],
        out_specs=[_tok_spec(ATT_W)] * 3,
        out_shape=[S((T, ATT_W), F32), S((T, ATT_W), F32), S((T, ATT_W), BF16)],
        scratch_shapes=[pltpu.VMEM((6 * nq, TT, 128), F32)],
        compiler_params=pltpu.CompilerParams(dimension_semantics=("arbitrary",)),
    )(*os_, *lses, proj)


def _att_gate_bwd(dain, att, lse, proj, dproj):
    nq = ATT_W // 128

    def body(d_ref, att_ref, lse_ref, za_ref, buf_ref, dza_ref, da0, da1, da2, at1, at2, ls1, ls2, sc):
        del buf_ref
        for j in range(nq):
            cols = slice(j * 128, (j + 1) * 128)
            za = za_ref[:, cols]
            sg = _sigmoid(za)
            d = d_ref[:, cols].astype(F32)
            att_ = att_ref[:, cols]
            dza_ref[:, cols] = (d * att_ * sg * (1.0 + za * (1.0 - sg))).astype(BF16)
            sc[j] = d * za * sg
            sc[nq + j] = att_
            sc[2 * nq + j] = lse_ref[:, cols]
        for j in range(nq):
            cols = slice(j * 128, (j + 1) * 128)
            for grp, dst in enumerate((da0, da1, da2)):
                _to_residues(sc, j, dst, DILATIONS[grp], cols)
            for grp, dst in ((1, at1), (2, at2)):
                _to_residues(sc, nq + j, dst, DILATIONS[grp], cols)
            for grp, dst in ((1, ls1), (2, ls2)):
                _to_residues(sc, 2 * nq + j, dst, DILATIONS[grp], cols)

    res = (0, 1, 2, 1, 2, 1, 2)
    return pl.pallas_call(
        body, name="att_gate_bwd", grid=(T // TT,),
        in_specs=[_tok_spec(ATT_W)] * 3 + [_tok_spec(ATT_W, C_ZA // ATT_W), pl.BlockSpec(memory_space=pl.ANY)],
        out_specs=[_tok_spec(ATT_W, C_ZA // ATT_W)] + [_res_spec(g) for g in res],
        out_shape=[S(dproj.shape, dproj.dtype)] + [_res_shape(g, BF16) for g in res[:5]]
        + [_res_shape(g, F32) for g in res[5:]],
        input_output_aliases={4: 0},
        scratch_shapes=[pltpu.VMEM((3 * nq, TT, 128), F32)],
        compiler_params=pltpu.CompilerParams(dimension_semantics=("arbitrary",)),
    )(dain, att, lse, proj, dproj)


def _split3(v):
    hi = v.astype(BF16)
    r1 = v - hi.astype(F32)
    mid = r1.astype(BF16)
    lo = (r1 - mid.astype(F32)).astype(BF16)
    return hi, mid, lo


def _tri_sum(v, upper):
    n = v.shape[0]
    row = lax.broadcasted_iota(jnp.int32, (n, n), 0)
    col = lax.broadcasted_iota(jnp.int32, (n, n), 1)
    tri = jnp.where(col >= row if upper else col <= row, 1.0, 0.0).astype(BF16)
    hi, mid, lo = _split3(v)
    return _nn(tri, hi) + _nn(tri, mid) + _nn(tri, lo)


def _gla_gates(glr_ref, w2_ref, b_ref):
    logit = _nn(glr_ref[...].astype(BF16), w2_ref[...]) + b_ref[...]
    lg = (jnp.minimum(logit, 0.0) - jnp.log(1.0 + jnp.exp(-jnp.abs(logit)))) * (1.0 / GLA_TAU)
    return logit, _tri_sum(lg, upper=False)


def _gla_head(cum, q_ref, k_ref, h):
    cols = slice(h * GDK, (h + 1) * GDK)
    b = cum[:, cols]
    last = b[GLA_C - 1:GLA_C, :]
    e_pos = jnp.exp(b)
    e_neg = jnp.exp(-b)
    e_end = jnp.exp(last - b)
    qt = q_ref[:, cols] * (GDK ** -0.5) * e_pos
    kt = k_ref[:, cols] * e_neg
    kh = k_ref[:, cols] * e_end
    return b, last, e_pos, e_neg, e_end, qt, kt, kh


def _causal(n):
    return lax.broadcasted_iota(jnp.int32, (n, n), 1) <= lax.broadcasted_iota(jnp.int32, (n, n), 0)


def _gla_fwd(proj, w2p, bg, gn):
    nc = T // GLA_C

    def body(q_ref, k_ref, v_ref, glr_ref, zg_ref, w2_ref, b_ref, gn_ref, o_ref, bin_ref, st_ref, state):
        @pl.when(pl.program_id(0) == 0)
        def _():
            state[...] = jnp.zeros_like(state)

        _, cum = _gla_gates(glr_ref, w2_ref, b_ref)
        for h in range(GH):
            _, last, _, _, _, qt, kt, kh = _gla_head(cum, q_ref, k_ref, h)
            vcols = slice(h * GDV, (h + 1) * GDV)
            st = state[h]
            st_ref[0, h] = st
            v = v_ref[:, vcols].astype(BF16)
            qb = qt.astype(BF16)
            a = jnp.where(_causal(GLA_C), _nt(qb, kt.astype(BF16)), 0.0)
            o = _nt(qb, st.astype(BF16)) + _nn(a.astype(BF16), v)
            state[h] = st * jnp.exp(last) + _tn(v, kh.astype(BF16))
            o_ref[:, vcols] = o
            r = lax.rsqrt(jnp.mean(o * o, axis=-1, keepdims=True) + EPS)
            zg = zg_ref[:, vcols]
            bin_ref[:, vcols] = (o * r * gn_ref[...] * zg * _sigmoid(zg)).astype(BF16)

    row = lambda width, cblk: pl.BlockSpec((GLA_C, width), functools.partial(lambda i, c: (i, c), c=cblk))
    full = lambda a: pl.BlockSpec(a.shape, functools.partial(lambda i, nd: (0,) * nd, nd=a.ndim))
    return pl.pallas_call(
        body, name="gla_fwd", grid=(nc,),
        in_specs=[row(512, C_QG // 512), row(512, C_KG // 512), row(1024, C_VG // 1024), row(GLR_W, C_GLR // GLR_W),
                  row(1024, C_ZG // 1024), full(w2p), full(bg), full(gn)],
        out_specs=[pl.BlockSpec((GLA_C, GH * GDV), lambda i: (i, 0)), pl.BlockSpec((GLA_C, GH * GDV), lambda i: (i, 0)),
                   pl.BlockSpec((1, GH, GDV, GDK), lambda i: (i, 0, 0, 0))],
        out_shape=[S((T, GH * GDV), F32), S((T, GH * GDV), BF16), S((nc, GH, GDV, GDK), F32)],
        scratch_shapes=[pltpu.VMEM((GH, GDV, GDK), F32)],
        compiler_params=pltpu.CompilerParams(dimension_semantics=("arbitrary",)),
    )(proj, proj, proj, proj, proj, w2p, bg, gn)


def _gla_bwd(proj, w2p, bg, gn, o_gla, states, dbin, dproj):
    nc = T // GLA_C

    def body(q_ref, k_ref, v_ref, glr_ref, zg_ref, w2_ref, b_ref, gn_ref, o_ref, st_ref, dbin_ref, buf_ref,
             out_ref, dw2_ref, dbg_ref, dgn_ref, dstate, dlogit):
        del buf_ref
        dq_ref = out_ref.at[:, C_QG:C_KG]
        dk_ref = out_ref.at[:, C_KG:C_VG]
        dv_ref = out_ref.at[:, C_VG:C_ZG]
        dzg_ref = out_ref.at[:, C_ZG:C_GLR]
        dglr_ref = out_ref.at[:, C_GLR:C_GLR + GLR_W]
        first = pl.program_id(0) == 0

        @pl.when(first)
        def _():
            dstate[...] = jnp.zeros_like(dstate)

        logit, cum = _gla_gates(glr_ref, w2_ref, b_ref)
        is_last = lax.broadcasted_iota(jnp.int32, (GLA_C, 1), 0) == GLA_C - 1
        dgn = jnp.zeros((1, GDV), F32)
        for h in range(GH):
            _, last, e_pos, e_neg, e_end, qt, kt, kh = _gla_head(cum, q_ref, k_ref, h)
            cols = slice(h * GDK, (h + 1) * GDK)
            vcols = slice(h * GDV, (h + 1) * GDV)
            o = o_ref[:, vcols]
            r = lax.rsqrt(jnp.mean(o * o, axis=-1, keepdims=True) + EPS)
            zg = zg_ref[:, vcols]
            sg = _sigmoid(zg)
            db_ = dbin_ref[:, vcols].astype(F32)
            dlin = db_ * zg * sg
            dzg_ref[:, vcols] = (db_ * (o * r * gn_ref[...]) * sg * (1.0 + zg * (1.0 - sg))).astype(BF16)
            u = dlin * gn_ref[...]
            do = (r * u - o * (r * r * r) * jnp.mean(u * o, axis=-1, keepdims=True)).astype(BF16)
            dgn = dgn + jnp.sum(dlin * o * r, axis=0, keepdims=True)
            st = st_ref[0, h]
            dst = dstate[h]
            v = v_ref[:, vcols].astype(BF16)
            qb, kb, khb = qt.astype(BF16), kt.astype(BF16), kh.astype(BF16)
            dstb = dst.astype(BF16)
            causal = _causal(GLA_C)
            a = jnp.where(causal, _nt(qb, kb), 0.0).astype(BF16)
            da = jnp.where(causal, _nt(do, v), 0.0).astype(BF16)
            dqt = _nn(do, st.astype(BF16)) + _nn(da, kb)
            dkt = _tn(da, qb)
            dkh = _nn(v, dstb)
            dv_ref[:, vcols] = (_tn(a, do) + _nt(khb, dstb)).astype(BF16)
            lam = jnp.exp(last)
            dlam = jnp.sum(dst * st, axis=0, keepdims=True)
            dstate[h] = dst * lam + _tn(do, qb)
            dq_ref[:, cols] = (dqt * e_pos * (GDK ** -0.5)).astype(BF16)
            dk_ref[:, cols] = (dkt * e_neg + dkh * e_end).astype(BF16)
            dkh_kh = dkh * kh
            dcum = dqt * qt - dkt * kt - dkh_kh
            dlast = jnp.sum(dkh_kh, axis=0, keepdims=True) + dlam * lam
            dcum = jnp.where(is_last, dcum + dlast, dcum)
            dlg = _tri_sum(dcum, upper=True)
            dlogit[:, cols] = dlg * (1.0 / GLA_TAU) * (1.0 - _sigmoid(logit[:, cols]))

        dl = dlogit[...]
        dlb = dl.astype(BF16)
        dglr_ref[...] = _nt(dlb, w2_ref[...]).astype(BF16)
        dw2 = _tn(glr_ref[...].astype(BF16), dlb)
        dbg = jnp.sum(dl, axis=0, keepdims=True)

        @pl.when(first)
        def _():
            dw2_ref[...] = dw2
            dbg_ref[...] = dbg
            dgn_ref[...] = dgn

        @pl.when(jnp.logical_not(first))
        def _():
            dw2_ref[...] += dw2
            dbg_ref[...] += dbg
            dgn_ref[...] += dgn

    rev = lambda i: nc - 1 - i
    row = lambda width, cblk: pl.BlockSpec((GLA_C, width), functools.partial(lambda i, c: (rev(i), c), c=cblk))
    full = lambda a: pl.BlockSpec(a.shape, functools.partial(lambda i, nd: (0,) * nd, nd=a.ndim))
    keep = lambda shape: pl.BlockSpec(shape, functools.partial(lambda i, nd: (0,) * nd, nd=len(shape)))
    return pl.pallas_call(
        body, name="gla_bwd", grid=(nc,),
        in_specs=[row(512, C_QG // 512), row(512, C_KG // 512), row(1024, C_VG // 1024), row(GLR_W, C_GLR // GLR_W),
                  row(1024, C_ZG // 1024), full(w2p), full(bg), full(gn), row(GH * GDV, 0),
                  pl.BlockSpec((1, GH, GDV, GDK), lambda i: (rev(i), 0, 0, 0)), row(GH * GDV, 0),
                  pl.BlockSpec(memory_space=pl.ANY)],
        out_specs=[row(GLA_GROUP_W, 0), keep((GLR_W, 512)), keep((1, 512)), keep((1, GDV))],
        out_shape=[S(dproj.shape, dproj.dtype), S((GLR_W, 512), F32), S((1, 512), F32), S((1, GDV), F32)],
        input_output_aliases={11: 0},
        scratch_shapes=[pltpu.VMEM((GH, GDV, GDK), F32), pltpu.VMEM((GLA_C, GH * GDK), F32)],
        compiler_params=pltpu.CompilerParams(dimension_semantics=("arbitrary",)),
    )(proj, proj, proj, proj, proj, w2p, bg, gn, o_gla, states, dbin, dproj)


def _merge_fwd(ya, yb, proj):
    def body(ya_ref, yb_ref, ga_ref, gb_ref, y_ref):
        y_ref[...] = (_sigmoid(ga_ref[...]) * ya_ref[...].astype(F32)
                      + _sigmoid(gb_ref[...]) * yb_ref[...].astype(F32)).astype(BF16)

    return _rowcall(body, "merge_fwd", 512,
                    [_rows(ya), _rows(yb), _rows(proj, D, C_GA // D), _rows(proj, D, C_GB // D)],
                    [("rows", D, BF16)])[0]


def _merge_bwd(dy, ya, yb, proj):
    tt = 512

    def body(dy_ref, ya_ref, yb_ref, g_ref, dg_ref, dya_ref, dyb_ref):
        dy_ = dy_ref[...].astype(F32)
        sa, sb = _sigmoid(g_ref[:, :D]), _sigmoid(g_ref[:, D:])
        dg_ref[:, :D] = (dy_ * ya_ref[...].astype(F32) * sa * (1.0 - sa)).astype(BF16)
        dg_ref[:, D:] = (dy_ * yb_ref[...].astype(F32) * sb * (1.0 - sb)).astype(BF16)
        dya_ref[...] = (dy_ * sa).astype(BF16)
        dyb_ref[...] = (dy_ * sb).astype(BF16)

    tok = pl.BlockSpec((tt, D), lambda i: (i, 0))
    gates = pl.BlockSpec((tt, 2 * D), lambda i: (i, C_GA // (2 * D)))
    return pl.pallas_call(
        body, name="merge_bwd", grid=(T // tt,),
        in_specs=[tok, tok, tok, gates], out_specs=[gates, tok, tok],
        out_shape=[S((T, NCOL), BF16), S((T, D), BF16), S((T, D), BF16)],
        compiler_params=pltpu.CompilerParams(dimension_semantics=("arbitrary",)),
    )(dy, ya, yb, proj)


def _loss_head(x1, e, u, target):
    def body(x1_ref, e_ref, u_ref, t_ref, loss_ref, dout_ref, de_ref, du_ref, acc):
        first = pl.program_id(0) == 0
        pg = _sigmoid(u_ref[...])
        e_ = e_ref[...]
        diff = x1_ref[...] + e_ * pg - t_ref[...]
        part = jnp.sum(diff * diff, axis=0, keepdims=True)

        @pl.when(first)
        def _():
            acc[...] = part

        @pl.when(jnp.logical_not(first))
        def _():
            acc[...] += part

        dout = diff * (1.0 / D)
        dout_ref[...] = dout
        de_ref[...] = (dout * pg).astype(BF16)
        du_ref[...] = (dout * e_ * pg * (1.0 - pg)).astype(BF16)
        loss_ref[...] = jnp.zeros((1, 128), F32) + jnp.sum(acc[...], axis=-1, keepdims=True) * (0.5 / D)

    return _rowcall(body, "loss_head", 256, [_rows(x1), _rows(e), _rows(u), _rows(target)],
                    [("acc", (1, 128), F32), ("rows", D, F32), ("rows", D, BF16), ("rows", D, BF16)],
                    scratch=[pltpu.VMEM((1, D), F32)])


def _peer(k):
    x, y, c = lax.axis_index("x"), lax.axis_index("y"), lax.axis_index("c")
    return (x ^ ((k >> 2) & 1), y ^ ((k >> 1) & 1), c ^ (k & 1))


def _my_index():
    return 4 * lax.axis_index("x") + 2 * lax.axis_index("y") + lax.axis_index("c")


def _peer_index(k):
    px, py, pc = _peer(k)
    return 4 * px + 2 * py + pc


def _pairwise_plan(src_of, dst_of, landed_of, own_src, own_dst):
    def plan(ins, outs, send, recv, local):
        n = len(ins)

        def own():
            return [pltpu.make_async_copy(own_src(ins[a]), own_dst(outs[a]), local.at[a]) for a in range(n)]

        def remote(k, a, src, dst):
            return pltpu.make_async_remote_copy(src_ref=src, dst_ref=dst, send_sem=send.at[k - 1, a],
                                                recv_sem=recv.at[k - 1, a], device_id=_peer(k), device_id_type=MESH)

        def sent():
            return [remote(k, a, src_of(ins[a], k), dst_of(outs[a])) for k in range(1, NDEV) for a in range(n)]

        def start():
            for cp in own() + sent():
                cp.start()

        def finish():
            for k in range(1, NDEV):
                for a in range(n):
                    remote(k, a, own_src(ins[a]), landed_of(outs[a], k)).wait_recv()
            for cp in sent():
                cp.wait_send()
            for cp in own():
                cp.wait()

        return start, finish

    return plan


def _pairwise_sems(n):
    return [pltpu.SemaphoreType.DMA((NDEV - 1, n)), pltpu.SemaphoreType.DMA((NDEV - 1, n)),
            pltpu.SemaphoreType.DMA((n,))]


def _gather_side(arrs):
    plan = _pairwise_plan(src_of=lambda i, k: i, dst_of=lambda o: o.at[_my_index()],
                          landed_of=lambda o, k: o.at[_peer_index(k)],
                          own_src=lambda i: i, own_dst=lambda o: o.at[_my_index()])
    return dict(arrs=arrs, out_shape=[S((NDEV,) + a.shape, a.dtype) for a in arrs],
                scratch=_pairwise_sems(len(arrs)), plan=plan)


def _exchange_side(arrs):
    plan = _pairwise_plan(src_of=lambda i, k: i.at[_peer_index(k)], dst_of=lambda o: o.at[_my_index()],
                          landed_of=lambda o, k: o.at[_peer_index(k)],
                          own_src=lambda i: i.at[_my_index()], own_dst=lambda o: o.at[_my_index()])
    return dict(arrs=arrs, out_shape=[S(a.shape, a.dtype) for a in arrs], scratch=_pairwise_sems(len(arrs)), plan=plan)


def _comm_call(side, name):
    n = len(side["arrs"])

    def body(*refs):
        start, finish = side["plan"](refs[:n], refs[n:2 * n], *refs[2 * n:])
        start()
        finish()

    hbm = pl.BlockSpec(memory_space=pl.ANY)
    return pl.pallas_call(body, name=name, in_specs=[hbm] * n, out_specs=[hbm] * n, out_shape=side["out_shape"],
                          scratch_shapes=side["scratch"])(*side["arrs"])


def _all_gather_by_chip(arrs, name):
    n = len(arrs)

    def body(*refs):
        ins, outs = refs[:n], refs[n:2 * n]
        send, recv, local = refs[2 * n:]
        x, y, c = lax.axis_index("x"), lax.axis_index("y"), lax.axis_index("c")
        me, sibling = (x, y, c), (x, y, 1 - c)
        chips = [(1 - x, y), (x, 1 - y), (1 - x, 1 - y)]

        def copy(k, a, block, to, src=None):
            px, py, pc = block
            slot = outs[a].at[4 * px + 2 * py + pc]
            return pltpu.make_async_remote_copy(
                src_ref=slot if src is None else src, dst_ref=slot, send_sem=send.at[k, a], recv_sem=recv.at[k, a],
                device_id=to, device_id_type=MESH)

        mine = [pltpu.make_async_copy(ins[a], outs[a].at[4 * x + 2 * y + c], local.at[a]) for a in range(n)]
        first = []
        for a in range(n):
            first.append(copy(0, a, me, sibling, src=ins[a]))
            first += [copy(1 + j, a, me, (*chip, c), src=ins[a]) for j, chip in enumerate(chips)]
        for cp in mine + first:
            cp.start()
        passed = []
        for j, chip in enumerate(chips):
            for a in range(n):
                copy(1 + j, a, (*chip, c), me).wait_recv()
                passed.append(copy(4 + j, a, (*chip, c), sibling))
                passed[-1].start()
        for a in range(n):
            copy(0, a, sibling, me).wait_recv()
        for j, chip in enumerate(chips):
            for a in range(n):
                copy(4 + j, a, (*chip, 1 - c), me).wait_recv()
        for cp in first + passed:
            cp.wait_send()
        for cp in mine:
            cp.wait()

    hbm = pl.BlockSpec(memory_space=pl.ANY)
    return pl.pallas_call(
        body, name=name, in_specs=[hbm] * n, out_specs=[hbm] * n,
        out_shape=[S((NDEV,) + a.shape, a.dtype) for a in arrs],
        scratch_shapes=[pltpu.SemaphoreType.DMA((NDEV - 1, n)), pltpu.SemaphoreType.DMA((NDEV - 1, n)),
                        pltpu.SemaphoreType.DMA((n,))],
    )(*arrs)


NCHIP = 4


def _exchange_sibling(arrs, name):
    n = len(arrs)

    def body(*refs):
        ins, outs = refs[:n], refs[n:2 * n]
        send, recv = refs[2 * n:]
        x, y, c = lax.axis_index("x"), lax.axis_index("y"), lax.axis_index("c")
        copies = []
        for q in range(NCHIP):
            for a in range(n):
                copies.append(pltpu.make_async_remote_copy(
                    src_ref=ins[a].at[2 * q + (1 - c)], dst_ref=outs[a].at[q], send_sem=send.at[q, a],
                    recv_sem=recv.at[q, a], device_id=(x, y, 1 - c), device_id_type=MESH))
        for cp in copies:
            cp.start()
        for cp in copies:
            cp.wait_recv()
        for cp in copies:
            cp.wait_send()

    hbm = pl.BlockSpec(memory_space=pl.ANY)
    return pl.pallas_call(
        body, name=name, in_specs=[hbm] * n, out_specs=[hbm] * n,
        out_shape=[S((NCHIP,) + a.shape[1:], a.dtype) for a in arrs],
        scratch_shapes=[pltpu.SemaphoreType.DMA((NCHIP, n)), pltpu.SemaphoreType.DMA((NCHIP, n))],
    )(*arrs)


def _pair_add(mine, got, core, name):
    _, rows, cols = mine.shape
    tc = 256
    assert cols % tc == 0

    def body(core_ref, a_ref, b_ref, o_ref):
        o_ref[...] = (a_ref[...].astype(F32) + b_ref[...].astype(F32)).astype(BF16)

    return pl.pallas_call(
        body, name=name,
        grid_spec=pltpu.PrefetchScalarGridSpec(
            num_scalar_prefetch=1, grid=(NCHIP, cols // tc),
            in_specs=[pl.BlockSpec((None, rows, tc), lambda q, i, core_ref: (2 * q + core_ref[0], 0, i)),
                      pl.BlockSpec((None, rows, tc), lambda q, i, core_ref: (q, 0, i))],
            out_specs=pl.BlockSpec((None, rows, tc), lambda q, i, core_ref: (q, 0, i))),
        out_shape=S((NCHIP, rows, cols), BF16),
    )(core, mine, got)


def _chips_side(arrs):
    def plan(ins, outs, send, recv, local):
        n = len(ins)

        def places():
            x, y, c = lax.axis_index("x"), lax.axis_index("y"), lax.axis_index("c")
            return 2 * x + y, c, [(1 - x, y), (x, 1 - y), (1 - x, 1 - y)]

        def own():
            here, _, _ = places()
            return [pltpu.make_async_copy(ins[a].at[here], outs[a].at[here], local.at[a]) for a in range(n)]

        def remote(j, a, src_slot, dst_slot):
            _, c, chips = places()
            cx, cy = chips[j]
            return pltpu.make_async_remote_copy(
                src_ref=ins[a].at[src_slot], dst_ref=outs[a].at[dst_slot], send_sem=send.at[j, a],
                recv_sem=recv.at[j, a], device_id=(cx, cy, c), device_id_type=MESH)

        def sent():
            here, _, chips = places()
            return [remote(j, a, 2 * cx + cy, here) for j, (cx, cy) in enumerate(chips) for a in range(n)]

        def start():
            for cp in own() + sent():
                cp.start()

        def finish():
            here, _, chips = places()
            for j, (cx, cy) in enumerate(chips):
                for a in range(n):
                    remote(j, a, here, 2 * cx + cy).wait_recv()
            for cp in sent():
                cp.wait_send()
            for cp in own():
                cp.wait()

        return start, finish

    n = len(arrs)
    return dict(arrs=arrs, out_shape=[S(a.shape, a.dtype) for a in arrs],
                scratch=[pltpu.SemaphoreType.DMA((NCHIP - 1, n)), pltpu.SemaphoreType.DMA((NCHIP - 1, n)),
                         pltpu.SemaphoreType.DMA((n,))], plan=plan)


def _adamw(parts, w, m, v, name, tr, tc=None):
    rows, cols = w.shape
    if tc is None:
        assert rows % tr == 0
        grid, shape, at = (rows // tr,), (tr, cols), (lambda i: (i, 0))
    else:
        assert cols % tc == 0
        grid, shape, at = (cols // tc,), (rows, tc), (lambda i: (0, i))
    c1 = 1.0 - ADAM_B1 ** ADAM_STEP
    c2 = 1.0 - ADAM_B2 ** ADAM_STEP

    nparts = parts.shape[0]

    def body(p_ref, w_ref, m_ref, v_ref, g_ref, d_ref, mo_ref, vo_ref):
        g = p_ref[0].astype(F32)
        for s in range(1, nparts):
            g = g + p_ref[s].astype(F32)
        m_new = ADAM_B1 * m_ref[...] + (1.0 - ADAM_B1) * g
        v_new = ADAM_B2 * v_ref[...] + (1.0 - ADAM_B2) * (g * g)
        g_ref[...] = g
        mo_ref[...] = m_new
        vo_ref[...] = v_new
        d_ref[...] = -ADAM_LR * ((m_new / c1) / (jnp.sqrt(v_new / c2) + ADAM_EPS) + ADAM_WD * w_ref[...])

    blk = pl.BlockSpec(shape, at)
    return pl.pallas_call(
        body, name=name, grid=grid,
        in_specs=[pl.BlockSpec((nparts,) + shape, lambda i: (0,) + at(i)), blk, blk, blk],
        out_specs=[blk] * 4, out_shape=[S((rows, cols), F32)] * 4,
        compiler_params=pltpu.CompilerParams(dimension_semantics=("parallel",)),
    )(parts, w, m, v)


def _adamw_rows(parts, w, m, v, name, tc=128):
    rows, _, cols = w.shape
    nparts = parts.shape[0]
    c1 = 1.0 - ADAM_B1 ** ADAM_STEP
    c2 = 1.0 - ADAM_B2 ** ADAM_STEP

    def body(p_ref, w_ref, m_ref, v_ref, g_ref, d_ref, mo_ref, vo_ref):
        flat = lambda ref: ref[...].reshape(rows, tc)
        g = p_ref[0].astype(F32)
        for s in range(1, nparts):
            g = g + p_ref[s].astype(F32)
        m_new = ADAM_B1 * flat(m_ref) + (1.0 - ADAM_B1) * g
        v_new = ADAM_B2 * flat(v_ref) + (1.0 - ADAM_B2) * (g * g)
        delta = -ADAM_LR * ((m_new / c1) / (jnp.sqrt(v_new / c2) + ADAM_EPS) + ADAM_WD * flat(w_ref))
        for ref, val in ((g_ref, g), (d_ref, delta), (mo_ref, m_new), (vo_ref, v_new)):
            ref[...] = val.reshape(rows, 1, tc)

    blk = pl.BlockSpec((rows, 1, tc), lambda i: (0, 0, i))
    return pl.pallas_call(
        body, name=name, grid=(cols // tc,),
        in_specs=[pl.BlockSpec((nparts, rows, tc), lambda i: (0, 0, i)), blk, blk, blk],
        out_specs=[blk] * 4, out_shape=[S((rows, 1, cols), F32)] * 4,
        compiler_params=pltpu.CompilerParams(dimension_semantics=("parallel",)),
    )(parts, w, m, v)


def _to_aligned(wt):
    pad = jnp.zeros((GLR_W - GLR_N, wt.shape[1]), wt.dtype)
    return jnp.concatenate([wt[O_QG:O_GLR], wt[O_ZG:O_GA], wt[O_GLR:O_ZG], pad, wt[O_ZA:O_QG], wt[O_GA:O_END],
                            wt[O_QA:O_ZA]], axis=0)


def _from_aligned(wt):
    return jnp.concatenate([wt[C_QA:], wt[C_ZA:C_GA], wt[C_QG:C_ZG], wt[C_GLR:C_GLR + GLR_N], wt[C_ZG:C_GLR],
                            wt[C_GA:C_QA]], axis=0)


def _col_blocks(w, width):
    return w.reshape(w.shape[0], NDEV, width).transpose(1, 0, 2)


def _from_col_blocks(w):
    return w.transpose(1, 0, 2).reshape(w.shape[1], NDEV * w.shape[2])


SMALL = (("norm_g", D), ("qk_norm_q", HD), ("qk_norm_k", HD), ("gla_gate_b", 512), ("gla_norm_g", GDV),
         ("ple_norm_g", D))
SMALL_PAD = 4096


def _local_step(x2, p2, pos, tgt, norm_g, qk_norm_q, qk_norm_k, gla_gate_b, gla_norm_g, ple_norm_g, w_al,
                weights=None, proj_side=None, unpack=None, dw_side_of=None, dh_side_of=None):
    half = ROT_DIM // 2
    inv8 = jnp.power(jnp.float32(ROPE_THETA), -jnp.arange(half, dtype=F32) * 2.0 / ROT_DIM)
    inv = jnp.tile(jnp.concatenate([inv8, inv8, jnp.zeros((HD - ROT_DIM,), F32)]), 2).reshape(1, 128)
    gq = jnp.tile(qk_norm_q, (1, 2))
    gk = jnp.tile(qk_norm_k, (1, 2))

    h = _rms_fwd(x2, norm_g, "rms1_fwd")
    if proj_side is None:
        proj = _mm(h, w_al, mode="nt", name="proj", tm=1024, tn=1536, tk=D)
    else:
        proj, got = _mm(h, w_al, mode="nt", name="proj", tm=1024, tn=1536, tk=D, side=proj_side)
        weights = unpack(got)
    w2p, w_att_f, w_gla_f, w_out_f, w_pg_f, w_ple_f = weights
    qkv = _qk_prep(proj, pos, inv, gq, gk)
    fwd = [_att_fwd(qkv[g], qkv[3 + g], qkv[6 + g], g, f"att_fwd{g}") for g in range(3)]
    att, lse, ain = _att_merge([f[0] for f in fwd], [f[1] for f in fwd], proj)
    o_gla, bin_, states = _gla_fwd(proj, w2p, gla_gate_b, gla_norm_g)
    ya = _mm(ain, w_att_f, mode="nn", name="ya", tm=1024, tn=D, tk=512, out_dtype=BF16)
    yb = _mm(bin_, w_gla_f, mode="nn", name="yb", tm=1024, tn=D, tk=D, out_dtype=BF16)
    y = _merge_fwd(ya, yb, proj)
    x1 = _mm(y, w_out_f, mode="nn", name="x1", tm=1024, tn=D, tk=D, res=x2)
    n2 = _rms_fwd(x1, ple_norm_g, "rms2_fwd")
    u = _mm(n2, w_pg_f, mode="nn", name="ple_u", tm=1024, tn=D, tk=D)
    e = _mm(p2, w_ple_f, mode="nn", name="ple_e", tm=1024, tn=D, tk=PLE)
    loss_v, dout, de, du = _loss_head(x1, e, u, tgt)

    dw_ple = _mm(p2, de, mode="tn", name="dw_ple", tm=PLE, tn=D, tk=512)
    dw_pg = _mm(n2, du, mode="tn", name="dw_pg", tm=D, tn=D, tk=512)
    dn2 = _mm(du, w_pg_f, mode="nt", name="dn2", tm=1024, tn=D, tk=D)
    dx1, dx1b, dg_ple = _rms_bwd(dn2, x1, ple_norm_g, dout, "rms2_bwd")
    dw_out = _mm(y, dx1b, mode="tn", name="dw_out", tm=D, tn=D, tk=512)
    dy = _mm(dx1b, w_out_f, mode="nt", name="dy", tm=1024, tn=D, tk=D, out_dtype=BF16)
    dproj, dya, dyb = _merge_bwd(dy, ya, yb, proj)
    dw_att = _mm(ain, dya, mode="tn", name="dw_att", tm=512, tn=D, tk=512)
    dain = _mm(dya, w_att_f, mode="nt", name="dain", tm=1024, tn=512, tk=D, out_dtype=BF16)
    dw_gla = _mm(bin_, dyb, mode="tn", name="dw_gla", tm=D, tn=D, tk=512)
    dbin = _mm(dyb, w_gla_f, mode="nt", name="dbin", tm=1024, tn=D, tk=D, out_dtype=BF16)
    dproj, da0, da1, da2, at1, at2, ls1, ls2 = _att_gate_bwd(dain, att, lse, proj, dproj)
    datts, atts, lses = (da0, da1, da2), (att[None], at1, at2), (lse[None], ls1, ls2)
    dproj, dw2, dbg, dgn = _gla_bwd(proj, w2p, gla_gate_b, gla_norm_g, o_gla, states, dbin, dproj)
    bwd = [_att_bwd(qkv[g], qkv[3 + g], qkv[6 + g], datts[g], atts[g], lses[g], g, f"att_bwd{g}") for g in range(3)]
    dproj, dgq, dgk = _qk_bwd(proj, pos, inv, gq, gk, [b[0] for b in bwd], [b[1] for b in bwd],
                              [b[2] for b in bwd], dproj)
    out = dict(loss=loss_v, dw2=dw2, dw_att=dw_att, dw_gla=dw_gla, dw_out=dw_out, dw_pg=dw_pg, dw_ple=dw_ple,
               dgq=dgq, dgk=dgk, dbg=dbg, dgn=dgn, dg_ple=dg_ple)
    if dw_side_of is None:
        dw_al = _mm(dproj, h, mode="tn", name="dw_in", tm=1536, tn=D, tk=2048, out_dtype=BF16)
    else:
        dw_al, out["dw_side"] = _mm(dproj, h, mode="tn", name="dw_in", tm=1536, tn=D, tk=2048, out_dtype=BF16,
                                    side=dw_side_of(out))
    if dh_side_of is None:
        dh = _mm(dproj, w_al, mode="nn", name="dh", tm=1024, tn=D, tk=3584)
    else:
        dh, out["dh_side"] = _mm(dproj, w_al, mode="nn", name="dh", tm=1024, tn=D, tk=3584, side=dh_side_of(dw_al))
    grad_x, _, dg_norm = _rms_bwd(dh, x2, norm_g, dx1, "rms1_bwd")
    out.update(grad_x=grad_x, dw_al=dw_al, dg_norm=dg_norm)
    return out


def kernel(x, p, positions, norm_g, w_in, qk_norm_q, qk_norm_k, gla_gate_w2, gla_gate_b, gla_norm_g, w_att_proj, w_gla_proj, w_out, ple_norm_g, w_ple_gate, w_ple, loss_target, m_norm_g, m_w_in, m_qk_norm_q, m_qk_norm_k, m_gla_gate_w2, m_gla_gate_b, m_gla_norm_g, m_w_att_proj, m_w_gla_proj, m_w_out, m_ple_norm_g, m_w_ple_gate, m_w_ple, v_norm_g, v_w_in, v_qk_norm_q, v_qk_norm_k, v_gla_gate_w2, v_gla_gate_b, v_gla_norm_g, v_w_att_proj, v_w_gla_proj, v_w_out, v_ple_norm_g, v_w_ple_gate, v_w_ple):
    x2, p2, tgt = x[0], p[0, 0], loss_target[0]
    pos = positions.astype(F32).reshape(T, 1)

    rows3 = jnp.stack([w_gla_proj[0], w_out[0], w_ple_gate[0]]).astype(BF16)
    cols3 = jnp.concatenate([w_att_proj[0], w_ple[0], jnp.pad(gla_gate_w2[0], ((0, 0), (0, 64)))], axis=0).astype(BF16)
    (g_in,) = _all_gather_by_chip([w_in[0].T.astype(BF16)], "gather_w_in")
    w_al = _to_aligned(g_in.reshape(W_IN_COLS, D))

    def unpack(got):
        g_rows, g_cols = got
        w2_f = _from_col_blocks(g_cols[:, 768:784, :64])
        return (jnp.pad(w2_f, ((0, GLR_W - GLR_N), (0, 0))), _from_col_blocks(g_cols[:, :512]),
                g_rows[:, 0].reshape(D, D), g_rows[:, 1].reshape(D, D), g_rows[:, 2].reshape(D, D),
                _from_col_blocks(g_cols[:, 512:768]))

    def dw_side_of(g):
        s_rows = jnp.concatenate([g[k].reshape(NDEV, 128, D) for k in ("dw_gla", "dw_out", "dw_pg")], axis=1)
        s_cols = jnp.concatenate([_col_blocks(g["dw_att"], 128), _col_blocks(g["dw_ple"], 128),
                                  jnp.pad(_col_blocks(g["dw2"][:GLR_N], 64), ((0, 0), (0, 0), (0, 64)))], axis=1)
        return _exchange_side([s_rows.astype(BF16), s_cols.astype(BF16)])

    def dh_side_of(dw_al):
        s_in = _from_aligned(dw_al).astype(BF16).reshape(NDEV, W_IN_SHARD, D)
        (from_sibling,) = _exchange_sibling([s_in], "exchange_sibling")
        core = lax.axis_index("c").astype(jnp.int32).reshape(1)
        return _chips_side([_pair_add(s_in, from_sibling, core, "pair_add")])

    loc = _local_step(x2, p2, pos, tgt, norm_g, qk_norm_q, qk_norm_k, gla_gate_b, gla_norm_g, ple_norm_g, w_al,
                      proj_side=_gather_side([rows3, cols3]), unpack=unpack, dw_side_of=dw_side_of,
                      dh_side_of=dh_side_of)
    loss_v, grad_x = loc["loss"], loc["grad_x"]
    dg_norm, dgq, dgk, dbg, dgn, dg_ple = (loc[k] for k in ("dg_norm", "dgq", "dgk", "dbg", "dgn", "dg_ple"))
    r_rows, r_cols = loc["dw_side"]
    (r_in,) = loc["dh_side"]

    small = jnp.concatenate([dg_norm[0], dgq[0, :HD], dgk[0, :HD], dbg[0], dgn[0], dg_ple[0]])
    small = jnp.pad(small, (0, SMALL_PAD - small.shape[0])).reshape(1, 8, SMALL_PAD // 8)
    (r_small,) = _comm_call(_gather_side([small]), "gather_small")

    outs = {}

    def adam(nm, parts, w, m, v, tr):
        outs[nm] = _adamw(parts, w, m, v, "adam_" + nm, tr)

    rows_of = lambda a: jnp.transpose(a, (2, 0, 1))
    outs["w_in"] = [jnp.transpose(o, (1, 2, 0))[0] for o in
                    _adamw_rows(r_in, rows_of(w_in), rows_of(m_w_in), rows_of(v_w_in), "adam_w_in")]
    adam("w_gla_proj", r_rows[:, :128], w_gla_proj[0], m_w_gla_proj[0], v_w_gla_proj[0], 128)
    adam("w_out", r_rows[:, 128:256], w_out[0], m_w_out[0], v_w_out[0], 128)
    adam("w_ple_gate", r_rows[:, 256:], w_ple_gate[0], m_w_ple_gate[0], v_w_ple_gate[0], 128)
    adam("w_att_proj", r_cols[:, :512], w_att_proj[0], m_w_att_proj[0], v_w_att_proj[0], 512)
    adam("w_ple", r_cols[:, 512:768], w_ple[0], m_w_ple[0], v_w_ple[0], 256)
    adam("gla_gate_w2", r_cols[:, 768:784, :64], gla_gate_w2[0], m_gla_gate_w2[0], v_gla_gate_w2[0], 16)
    given = dict(norm_g=(norm_g, m_norm_g, v_norm_g), qk_norm_q=(qk_norm_q, m_qk_norm_q, v_qk_norm_q),
                 qk_norm_k=(qk_norm_k, m_qk_norm_k, v_qk_norm_k), gla_gate_b=(gla_gate_b, m_gla_gate_b, v_gla_gate_b),
                 gla_norm_g=(gla_norm_g, m_gla_norm_g, v_gla_norm_g), ple_norm_g=(ple_norm_g, m_ple_norm_g, v_ple_norm_g))

    def pack(i):
        flat = jnp.concatenate([given[nm][i][0] for nm, _ in SMALL])
        return jnp.pad(flat, (0, SMALL_PAD - flat.shape[0])).reshape(8, SMALL_PAD // 8)

    sm = _adamw(r_small.reshape(NDEV, 8, SMALL_PAD // 8), pack(0), pack(1), pack(2), "adam_small", 8)
    off = 0
    for nm, width in SMALL:
        outs[nm] = [o.reshape(-1)[off:off + width] for o in sm]
        off += width

    loss = lax.psum(loss_v[0, 0], ("x", "y", "c"))
    order = ["norm_g", "w_in", "qk_norm_q", "qk_norm_k", "gla_gate_w2", "gla_gate_b", "gla_norm_g", "w_att_proj",
             "w_gla_proj", "w_out", "ple_norm_g", "w_ple_gate", "w_ple"]
    result = [loss, grad_x[None]]
    for i in range(4):
        result += [outs[nm][i][None] for nm in order]
    return tuple(result)
```

```python
import functools

import jax
import jax.numpy as jnp
from jax import lax
from jax.experimental import pallas as pl
from jax.experimental.pallas import tpu as pltpu

F32 = jnp.float32
BF16 = jnp.bfloat16
S = jax.ShapeDtypeStruct

T = 4096
D = 1024
NDEV = 8
HD = 64
ATT_W = 512
ATT_QKV = 1536
DILATIONS = (1, 4, 16)
BLK = 128
GH, GDK, GDV = 4, 128, 256
GLA_C = 128
PLE = 256
EPS = 1e-6
ROT_DIM = 16
ROPE_THETA = 500000.0
GLA_TAU = 16.0
W_IN_COLS = 10256
W_IN_SHARD = 1282

C_QG, C_KG, C_VG, C_ZG, C_GLR, C_ZA, C_GA, C_GB, C_QA, C_KA, C_VA = (
    0, 512, 1024, 2048, 3072, 3584, 4096, 5120, 6144, 7680, 9216)
GLA_GROUP_W = 3584
GLR_W = 512
NCOL = 10752
GLR_N = 16
O_QA, O_ZA, O_QG, O_GLR, O_ZG, O_GA, O_END = 0, 4608, 5120, 7168, 7184, 8208, 10256

ADAM_LR, ADAM_B1, ADAM_B2, ADAM_EPS, ADAM_WD, ADAM_STEP = 0.001, 0.9, 0.999, 1e-08, 0.01, 10

MESH = pl.DeviceIdType.MESH


def _sigmoid(z):
    return 1.0 / (1.0 + jnp.exp(-z))


def _dot(a, b, dims):
    return lax.dot_general(a, b, (dims, ((), ())), preferred_element_type=F32)


def _nn(a, b):
    return _dot(a, b, ((1,), (0,)))


def _nt(a, b):
    return _dot(a, b, ((1,), (1,)))


def _tn(a, b):
    return _dot(a, b, ((0,), (0,)))


def _mm(a, b, *, mode, name, tm, tn, tk, out_dtype=F32, res=None, side=None):
    if mode == "nn":
        (m, k), n = a.shape, b.shape[1]
        a_spec = pl.BlockSpec((tm, tk), lambda i, j, l: (i, l))
        b_spec = pl.BlockSpec((tk, tn), lambda i, j, l: (l, j))
        dot = _nn
    elif mode == "nt":
        (m, k), n = a.shape, b.shape[0]
        a_spec = pl.BlockSpec((tm, tk), lambda i, j, l: (i, l))
        b_spec = pl.BlockSpec((tn, tk), lambda i, j, l: (j, l))
        dot = _nt
    else:
        (k, m), n = a.shape, b.shape[1]
        a_spec = pl.BlockSpec((tk, tm), lambda i, j, l: (l, i))
        b_spec = pl.BlockSpec((tk, tn), lambda i, j, l: (l, j))
        dot = _tn
    assert m % tm == 0 and n % tn == 0 and k % tk == 0, (name, m, n, k)
    grid = (m // tm, n // tn, k // tk)
    nk = grid[2]
    o_spec = pl.BlockSpec((tm, tn), lambda i, j, l: (i, j))
    in_specs = [a_spec, b_spec]
    args = [a, b]
    if res is not None:
        in_specs.append(o_spec)
        args.append(res)
    n_in = len(args)
    n_side = 0 if side is None else len(side["arrs"])
    hbm = pl.BlockSpec(memory_space=pl.ANY)

    def body(*refs):
        a_ref, b_ref = refs[0], refs[1]
        r_ref = refs[2] if res is not None else None
        o_ref = refs[n_in + n_side]
        scratch = refs[n_in + 2 * n_side + 1:]
        if side is not None:
            start, finish_side = side["plan"](refs[n_in:n_in + n_side], refs[n_in + n_side + 1:n_in + 2 * n_side + 1],
                                              *scratch[1 if nk > 1 else 0:])
            ids = [pl.program_id(d) for d in range(3)]

            @pl.when((ids[0] == 0) & (ids[1] == 0) & (ids[2] == 0))
            def _():
                start()

        part = dot(a_ref[...].astype(BF16), b_ref[...].astype(BF16))

        def finish(val):
            if r_ref is not None:
                val = val + r_ref[...]
            o_ref[...] = val.astype(out_dtype)

        if nk == 1:
            finish(part)
        else:
            acc = scratch[0]
            l = pl.program_id(2)

            @pl.when(l == 0)
            def _():
                acc[...] = part

            @pl.when(l > 0)
            def _():
                acc[...] += part

            @pl.when(l == nk - 1)
            def _():
                finish(acc[...])

        if side is not None:
            @pl.when((ids[0] == grid[0] - 1) & (ids[1] == grid[1] - 1) & (ids[2] == grid[2] - 1))
            def _():
                finish_side()

    sems = [] if side is None else side["scratch"]
    outs = pl.pallas_call(
        body, name=name, grid=grid,
        in_specs=in_specs + [hbm] * n_side, out_specs=[o_spec] + [hbm] * n_side,
        out_shape=[S((m, n), out_dtype)] + ([] if side is None else side["out_shape"]),
        scratch_shapes=([pltpu.VMEM((tm, tn), F32)] if nk > 1 else []) + sems,
        compiler_params=pltpu.CompilerParams(
            dimension_semantics=("arbitrary",) * 3 if side is not None else ("parallel", "parallel", "arbitrary")),
    )(*args, *([] if side is None else side["arrs"]))
    return outs[0] if side is None else (outs[0], outs[1:])


def _rows(arr, width=None, cblk=0):
    return ("rows", arr, arr.shape[1] if width is None else width, cblk)


def _whole(arr):
    return ("whole", arr)


def _rowcall(body, name, tt, ins, outs, scratch=()):
    in_specs, args = [], []
    for spec in ins:
        if spec[0] == "rows":
            _, arr, width, cblk = spec
            in_specs.append(pl.BlockSpec((tt, width), functools.partial(lambda i, c: (i, c), c=cblk)))
        else:
            arr = spec[1]
            in_specs.append(pl.BlockSpec(arr.shape, functools.partial(lambda i, nd: (0,) * nd, nd=arr.ndim)))
        args.append(arr)
    out_specs, out_shape = [], []
    for kind, shape, dtype in outs:
        if kind == "rows":
            out_specs.append(pl.BlockSpec((tt, shape), lambda i: (i, 0)))
            out_shape.append(S((T, shape), dtype))
        else:
            out_specs.append(pl.BlockSpec(shape, functools.partial(lambda i, nd: (0,) * nd, nd=len(shape))))
            out_shape.append(S(shape, dtype))
    return pl.pallas_call(
        body, name=name, grid=(T // tt,), in_specs=in_specs, out_specs=out_specs, out_shape=out_shape,
        scratch_shapes=list(scratch),
        compiler_params=pltpu.CompilerParams(dimension_semantics=("arbitrary",)),
    )(*args)


def _rms_fwd(x, g, name):
    def body(x_ref, g_ref, h_ref):
        xf = x_ref[...]
        r = lax.rsqrt(jnp.mean(xf * xf, axis=-1, keepdims=True) + EPS)
        h_ref[...] = (xf * r * g_ref[...]).astype(BF16)

    return _rowcall(body, name, 512, [_rows(x), _whole(g)], [("rows", D, BF16)])[0]


def _rms_bwd(dn, x, g, skip, name):
    def body(dn_ref, x_ref, g_ref, s_ref, dx_ref, dxb_ref, dg_ref):
        xf = x_ref[...]
        r = lax.rsqrt(jnp.mean(xf * xf, axis=-1, keepdims=True) + EPS)
        dn_ = dn_ref[...]
        u = dn_ * g_ref[...]
        dx = s_ref[...] + r * u - xf * (r * r * r) * jnp.mean(u * xf, axis=-1, keepdims=True)
        dx_ref[...] = dx
        dxb_ref[...] = dx.astype(BF16)
        part = jnp.sum(dn_ * xf * r, axis=0, keepdims=True)

        @pl.when(pl.program_id(0) == 0)
        def _():
            dg_ref[...] = part

        @pl.when(pl.program_id(0) > 0)
        def _():
            dg_ref[...] += part

    return _rowcall(body, name, 256, [_rows(dn), _rows(x), _whole(g), _rows(skip)],
                    [("rows", D, F32), ("rows", D, BF16), ("acc", (1, D), F32)])


def _rot_tables(pos_ref, inv_ref):
    lane = lax.broadcasted_iota(jnp.int32, (1, 128), 1) % HD
    ang = pos_ref[...] * inv_ref[...]
    cos, sin = jnp.cos(ang), jnp.sin(ang)
    c = jnp.where(lane < ROT_DIM, cos, 1.0)
    sp = jnp.where((lane >= ROT_DIM // 2) & (lane < ROT_DIM), sin, 0.0)
    sm = jnp.where(lane < ROT_DIM // 2, -sin, 0.0)
    return c, sp, sm


def _head_sums(v):
    same = (lax.broadcasted_iota(jnp.int32, (128, 128), 0) < HD) == (lax.broadcasted_iota(jnp.int32, (128, 128), 1) < HD)
    ones = jnp.where(same, 1.0, 0.0).astype(BF16)
    hi = v.astype(BF16)
    lo = (v - hi.astype(F32)).astype(BF16)
    return _nn(hi, ones) + _nn(lo, ones)


def _pair_norm(t):
    return lax.rsqrt(_head_sums(t * t) * (1.0 / HD) + EPS)


def _pair_mean(t):
    return _head_sums(t) * (1.0 / HD)


TT = 256
NCH = ATT_QKV // 128


def _res_shape(grp, dtype):
    return S((DILATIONS[grp], T // DILATIONS[grp], ATT_W), dtype)


def _res_spec(grp):
    dil = DILATIONS[grp]
    return pl.BlockSpec((dil, TT // dil, ATT_W), lambda i: (0, i, 0))


def _to_residues(sc, j, dst_ref, dil, cols):
    n = TT // dil
    for r in range(dil):
        rows = sc[j] if dil == 1 else sc.at[j][pl.ds(r, n, stride=dil), :]
        dst_ref[r, :, cols] = rows.astype(dst_ref.dtype)


def _from_residues(src_ref, cols, sc, j, dil):
    n = TT // dil
    for r in range(dil):
        if dil == 1:
            sc[j] = src_ref[r, :, cols]
        else:
            sc.at[j][pl.ds(r, n, stride=dil), :] = src_ref[r, :, cols]


def _tok_spec(width, cblk=0):
    return pl.BlockSpec((TT, width), functools.partial(lambda i, c: (i, c), c=cblk))


def _const_spec(arr_or_shape):
    shape = arr_or_shape if isinstance(arr_or_shape, tuple) else arr_or_shape.shape
    return pl.BlockSpec(shape, functools.partial(lambda i, nd: (0,) * nd, nd=len(shape)))


def _qk_prep(proj, pos, inv, gq, gk):
    def body(q_ref, k_ref, v_ref, pos_ref, inv_ref, gq_ref, gk_ref, *rest):
        outs, sc = rest[:9], rest[9]
        c, sp, sm = _rot_tables(pos_ref, inv_ref)
        for which, (src, g_ref) in enumerate(((q_ref, gq_ref), (k_ref, gk_ref), (v_ref, None))):
            if g_ref is not None:
                g = jnp.broadcast_to(g_ref[...] * ((HD ** -0.5) if which == 0 else 1.0), c.shape)
                cg, spg, smg = c * g, sp * pltpu.roll(g, 8, 1), sm * pltpu.roll(g, 120, 1)
            for j in range(NCH):
                t = src[:, j * 128:(j + 1) * 128]
                if g_ref is not None:
                    t = _pair_norm(t) * (t * cg + pltpu.roll(t, 8, 1) * spg + pltpu.roll(t, 120, 1) * smg)
                sc[j] = t
            for j in range(NCH):
                grp, sub = divmod(j * 128, ATT_W)
                _to_residues(sc, j, outs[which * 3 + grp], DILATIONS[grp], slice(sub, sub + 128))

    return pl.pallas_call(
        body, name="qk_prep", grid=(T // TT,),
        in_specs=[_tok_spec(ATT_QKV, C_QA // ATT_QKV), _tok_spec(ATT_QKV, C_KA // ATT_QKV),
                  _tok_spec(ATT_QKV, C_VA // ATT_QKV), _tok_spec(1), _const_spec(inv), _const_spec(gq), _const_spec(gk)],
        out_specs=[_res_spec(g) for _ in range(3) for g in range(3)],
        out_shape=[_res_shape(g, BF16) for _ in range(3) for g in range(3)],
        scratch_shapes=[pltpu.VMEM((NCH, TT, 128), F32)],
        compiler_params=pltpu.CompilerParams(dimension_semantics=("arbitrary",)),
    )(proj, proj, proj, pos, inv, gq, gk)


def _qk_bwd(proj, pos, inv, gq, gk, dqs, dks, dvs, dproj):
    const = lambda a: pl.BlockSpec(a.shape, functools.partial(lambda i, p, nd: (0,) * nd, nd=a.ndim))
    res = lambda g: pl.BlockSpec((DILATIONS[g], TT // DILATIONS[g], ATT_W), lambda i, p: (0, i, 0))
    base = C_QA // ATT_QKV

    def body(t_ref, pos_ref, inv_ref, gq_ref, gk_ref, dq0, dq1, dq2, dk0, dk1, dk2, dv0, dv1, dv2, buf_ref,
             out_ref, dgq_ref, dgk_ref, sc):
        del buf_ref
        part = pl.program_id(1)
        first = pl.program_id(0) == 0

        def gather(drefs):
            for j in range(NCH):
                grp, sub = divmod(j * 128, ATT_W)
                _from_residues(drefs[grp], slice(sub, sub + 128), sc, j, DILATIONS[grp])

        def normed(g_ref, drefs, dg_ref):
            c, sp, sm = _rot_tables(pos_ref, inv_ref)
            gather(drefs)
            dg = jnp.zeros((1, 128), F32)
            for j in range(NCH):
                cols = slice(j * 128, (j + 1) * 128)
                d_rot = sc[j]
                dn = d_rot * c + pltpu.roll(d_rot * sp, 120, 1) + pltpu.roll(d_rot * sm, 8, 1)
                t = t_ref[:, cols]
                r = _pair_norm(t)
                u = dn * g_ref[...]
                out_ref[:, cols] = (r * u - t * (r * r * r) * _pair_mean(u * t)).astype(BF16)
                dg = dg + jnp.sum(dn * t * r, axis=0, keepdims=True)
            dg = dg + pltpu.roll(dg, HD, 1)

            @pl.when(first)
            def _():
                dg_ref[...] = dg

            @pl.when(jnp.logical_not(first))
            def _():
                dg_ref[...] += dg

        @pl.when(part == 0)
        def _():
            gather((dv0, dv1, dv2))
            for j in range(NCH):
                out_ref[:, j * 128:(j + 1) * 128] = sc[j].astype(BF16)

        @pl.when(part == 1)
        def _():
            normed(gq_ref, (dq0, dq1, dq2), dgq_ref)

        @pl.when(part == 2)
        def _():
            normed(gk_ref, (dk0, dk1, dk2), dgk_ref)

    keep = pl.BlockSpec((1, 128), lambda i, p: (0, 0))
    return pl.pallas_call(
        body, name="qk_bwd", grid=(T // TT, 3),
        in_specs=[pl.BlockSpec((TT, ATT_QKV), lambda i, p: (i, base + jnp.maximum(p - 1, 0))),
                  pl.BlockSpec((TT, 1), lambda i, p: (i, 0)), const(inv), const(gq), const(gk)]
        + [res(g) for _ in range(3) for g in range(3)] + [pl.BlockSpec(memory_space=pl.ANY)],
        out_specs=[pl.BlockSpec((TT, ATT_QKV), lambda i, p: (i, base + jnp.where(p == 0, 2, p - 1))), keep, keep],
        out_shape=[S(dproj.shape, dproj.dtype), S((1, 128), F32), S((1, 128), F32)],
        input_output_aliases={14: 0},
        scratch_shapes=[pltpu.VMEM((NCH, TT, 128), F32)],
        compiler_params=pltpu.CompilerParams(dimension_semantics=("arbitrary", "arbitrary")),
    )(proj, pos, inv, gq, gk, *dqs, *dks, *dvs, dproj)


def _split_heads(t):
    low = lax.broadcasted_iota(jnp.int32, (1, 128), 1) < HD
    zero = jnp.zeros_like(t)
    return jnp.concatenate([jnp.where(low, t, zero), jnp.where(low, zero, t)], axis=0)


def _join_heads(t2):
    low = lax.broadcasted_iota(jnp.int32, (1, 128), 1) < HD
    n = t2.shape[0] // 2
    return jnp.where(low, t2[:n], t2[n:])


def _band_mask4(has_before, has_own):
    row = lax.broadcasted_iota(jnp.int32, (BLK, 4 * BLK), 0)
    lane = lax.broadcasted_iota(jnp.int32, (BLK, 4 * BLK), 1)
    key = lane & (BLK - 1)
    own = lane >= 2 * BLK
    return (own & (key <= row) & has_own) | (jnp.logical_not(own) & (key >= row) & has_before)


def _band_mask_before(has_before):
    row = lax.broadcasted_iota(jnp.int32, (BLK, 2 * BLK), 0)
    key = lax.broadcasted_iota(jnp.int32, (BLK, 2 * BLK), 1) & (BLK - 1)
    return (key >= row) & has_before


def _per_head(width, col_a, col_b):
    lane = lax.broadcasted_iota(jnp.int32, (1, width), 1)
    return jnp.where((lane & BLK) == 0, col_a, col_b)


NQ = ATT_W // 128


def _att_fwd(q, k, v, grp, name):
    dil = DILATIONS[grp]
    nb = T // dil // BLK

    def body(q_ref, kp_ref, kc_ref, vp_ref, vc_ref, o_ref, lse_ref, s_sc, p_sc):
        mask = _band_mask4(pl.program_id(1) > 0, True)
        low = lax.broadcasted_iota(jnp.int32, (1, 128), 1) < HD
        halves = lambda ref, j, h: (ref[j, :, h * BLK:(h + 1) * BLK], ref[j, :, (h + 2) * BLK:(h + 3) * BLK])
        for j in range(NQ):
            cols = slice(j * 128, (j + 1) * 128)
            k4 = jnp.concatenate([_split_heads(kp_ref[:, cols]), _split_heads(kc_ref[:, cols])], axis=0)
            s_sc[j] = jnp.where(mask, _nt(q_ref[:, cols], k4), -jnp.inf)
        mxs = [[jnp.maximum(*(jnp.max(t, axis=-1, keepdims=True) for t in halves(s_sc, j, h))) for h in range(2)]
               for j in range(NQ)]
        dens = []
        for j in range(NQ):
            p = jnp.exp(s_sc[j] - _per_head(4 * BLK, *mxs[j]))
            p_sc[j] = p.astype(BF16)
            dens.append([jnp.sum(p[:, h * BLK:(h + 1) * BLK], axis=-1, keepdims=True)
                         + jnp.sum(p[:, (h + 2) * BLK:(h + 3) * BLK], axis=-1, keepdims=True) for h in range(2)])
        for j in range(NQ):
            cols = slice(j * 128, (j + 1) * 128)
            v4 = jnp.concatenate([_split_heads(vp_ref[:, cols]), _split_heads(vc_ref[:, cols])], axis=0)
            o_ref[:, cols] = _nn(p_sc[j], v4) / jnp.where(low, dens[j][0], dens[j][1])
            lse_ref[:, cols] = jnp.where(low, mxs[j][0] + jnp.log(dens[j][0]), mxs[j][1] + jnp.log(dens[j][1]))

    cur = pl.BlockSpec((None, BLK, ATT_W), lambda r, i: (r, i, 0))
    prev = pl.BlockSpec((None, BLK, ATT_W), lambda r, i: (r, jnp.maximum(i - 1, 0), 0))
    return pl.pallas_call(
        body, name=name, grid=(dil, nb),
        in_specs=[cur, prev, cur, prev, cur],
        out_specs=[cur, cur], out_shape=[_res_shape(grp, F32)] * 2,
        scratch_shapes=[pltpu.VMEM((NQ, BLK, 4 * BLK), F32), pltpu.VMEM((NQ, BLK, 4 * BLK), BF16)],
        compiler_params=pltpu.CompilerParams(dimension_semantics=("parallel", "arbitrary")),
    )(q, k, k, v, v)


def _att_bwd(q, k, v, datt, att, lse, grp, name):
    dil = DILATIONS[grp]
    nb = T // dil // BLK
    scale = HD ** -0.5

    def body(q0_ref, q1_ref, kp_ref, kc_ref, vp_ref, vc_ref, do0_ref, do1_ref, o0_ref, o1_ref, l0_ref, l1_ref,
             dq_ref, dk_ref, dv_ref, k4_sc, v4_sc, s0_sc, s1_sc, dp0_sc, dp1_sc, p_sc, ds_sc):
        i = pl.program_id(1)
        mask_mine = _band_mask4(i > 0, True)
        mask_next = _band_mask_before(i < nb - 1)
        low = lax.broadcasted_iota(jnp.int32, (1, 128), 1) < HD
        for j in range(NQ):
            cols = slice(j * 128, (j + 1) * 128)
            k4_sc[j, :2 * BLK] = _split_heads(kp_ref[:, cols])
            k4_sc[j, 2 * BLK:] = _split_heads(kc_ref[:, cols])
            v4_sc[j, :2 * BLK] = _split_heads(vp_ref[:, cols])
            v4_sc[j, 2 * BLK:] = _split_heads(vc_ref[:, cols])
        for j in range(NQ):
            cols = slice(j * 128, (j + 1) * 128)
            s0_sc[j] = _nt(q0_ref[:, cols], k4_sc[j])
            s1_sc[j] = _nt(q1_ref[:, cols], k4_sc[j, 2 * BLK:])
            dp0_sc[j] = _nt(do0_ref[:, cols].astype(BF16), v4_sc[j])
            dp1_sc[j] = _nt(do1_ref[:, cols].astype(BF16), v4_sc[j, 2 * BLK:])
        stats = []
        for j in range(NQ):
            cols = slice(j * 128, (j + 1) * 128)
            for do_ref, o_ref, l_ref in ((do0_ref, o0_ref, l0_ref), (do1_ref, o1_ref, l1_ref)):
                prod = do_ref[:, cols].astype(F32) * o_ref[:, cols].astype(F32)
                d_all = jnp.sum(prod, axis=-1, keepdims=True)
                d_low = jnp.sum(jnp.where(low, prod, 0.0), axis=-1, keepdims=True)
                lse_t = l_ref[:, cols]
                stats.append((d_low, d_all - d_low, lse_t[:, 0:1], lse_t[:, HD:HD + 1]))
        for j in range(NQ):
            (da, db, la, lb), (da1, db1, la1, lb1) = stats[2 * j], stats[2 * j + 1]
            p0 = jnp.where(mask_mine, jnp.exp(s0_sc[j] - _per_head(4 * BLK, la, lb)), 0.0)
            ds0 = p0 * (dp0_sc[j] - _per_head(4 * BLK, da, db))
            p1 = jnp.where(mask_next, jnp.exp(s1_sc[j] - _per_head(2 * BLK, la1, lb1)), 0.0)
            ds1 = p1 * (dp1_sc[j] - _per_head(2 * BLK, da1, db1))
            p_sc[j, :BLK] = p0.astype(BF16)
            ds_sc[j, :BLK] = ds0.astype(BF16)
            p_sc[j, BLK:, 2 * BLK:] = p1.astype(BF16)
            ds_sc[j, BLK:, 2 * BLK:] = ds1.astype(BF16)
        for j in range(NQ):
            cols = slice(j * 128, (j + 1) * 128)
            dq_ref[:, cols] = _nn(ds_sc[j, :BLK], k4_sc[j]) * scale
            qq = jnp.concatenate([q0_ref[:, cols], q1_ref[:, cols]], axis=0)
            dd = jnp.concatenate([do0_ref[:, cols], do1_ref[:, cols]], axis=0).astype(BF16)
            dk_ref[:, cols] = _join_heads(_tn(ds_sc[j, :, 2 * BLK:], qq))
            dv_ref[:, cols] = _join_heads(_tn(p_sc[j, :, 2 * BLK:], dd))

    def spec(shift):
        return pl.BlockSpec((None, BLK, ATT_W), lambda r, i: (r, jnp.clip(i + shift, 0, nb - 1), 0))

    here, after, before = spec(0), spec(1), spec(-1)
    vm = pltpu.VMEM
    return pl.pallas_call(
        body, name=name, grid=(dil, nb),
        in_specs=[here, after, before, here, before, here, here, after, here, after, here, after],
        out_specs=[here] * 3, out_shape=[_res_shape(grp, F32)] * 3,
        scratch_shapes=[vm((NQ, 4 * BLK, 128), BF16), vm((NQ, 4 * BLK, 128), BF16), vm((NQ, BLK, 4 * BLK), F32),
                        vm((NQ, BLK, 2 * BLK), F32), vm((NQ, BLK, 4 * BLK), F32), vm((NQ, BLK, 2 * BLK), F32),
                        vm((NQ, 2 * BLK, 4 * BLK), BF16), vm((NQ, 2 * BLK, 4 * BLK), BF16)],
        compiler_params=pltpu.CompilerParams(dimension_semantics=("parallel", "arbitrary")),
    )(q, q, k, k, v, v, datt, datt, att, att, lse, lse)


def _att_merge(os_, lses, proj):
    nq = ATT_W // 128

    def body(o0, o1, o2, l0, l1, l2, za_ref, att_ref, lse_ref, ain_ref, sc):
        for a, ref in enumerate((o0, o1, o2, l0, l1, l2)):
            for j in range(nq):
                _from_residues(ref, slice(j * 128, (j + 1) * 128), sc, a * nq + j, DILATIONS[a % 3])
        for j in range(nq):
            cols = slice(j * 128, (j + 1) * 128)
            oa, ob, oc = (sc[a * nq + j] for a in range(3))
            la, lb, lc = (sc[(3 + a) * nq + j] for a in range(3))
            m = jnp.maximum(jnp.maximum(la, lb), lc)
            wa, wb, wc = jnp.exp(la - m), jnp.exp(lb - m), jnp.exp(lc - m)
            tot = wa + wb + wc
            att = (wa * oa + wb * ob + wc * oc) / tot
            att_ref[:, cols] = att
            lse_ref[:, cols] = m + jnp.log(tot)
            za = za_ref[:, cols]
            ain_ref[:, cols] = (att * za * _sigmoid(za)).astype(BF16)

    return pl.pallas_call(
        body, name="att_merge", grid=(T // TT,),
        in_specs=[_res_spec(g) for _ in range(2) for g in range(3)] + [_tok_spec(ATT_W, C_ZA // ATT_W)],
        out_specs=[_tok_spec(ATT_W)] * 3,
        out_shape=[S((T, ATT_W), F32), S((T, ATT_W), F32), S((T, ATT_W), BF16)],
        scratch_shapes=[pltpu.VMEM((6 * nq, TT, 128), F32)],
        compiler_params=pltpu.CompilerParams(dimension_semantics=("arbitrary",)),
    )(*os_, *lses, proj)


def _att_gate_bwd(dain, att, lse, proj, dproj):
    nq = ATT_W // 128

    def body(d_ref, att_ref, lse_ref, za_ref, buf_ref, dza_ref, da0, da1, da2, at1, at2, ls1, ls2, sc):
        del buf_ref
        for j in range(nq):
            cols = slice(j * 128, (j + 1) * 128)
            za = za_ref[:, cols]
            sg = _sigmoid(za)
            d = d_ref[:, cols].astype(F32)
            att_ = att_ref[:, cols]
            dza_ref[:, cols] = (d * att_ * sg * (1.0 + za * (1.0 - sg))).astype(BF16)
            sc[j] = d * za * sg
            sc[nq + j] = att_
            sc[2 * nq + j] = lse_ref[:, cols]
        for j in range(nq):
            cols = slice(j * 128, (j + 1) * 128)
            for grp, dst in enumerate((da0, da1, da2)):
                _to_residues(sc, j, dst, DILATIONS[grp], cols)
            for grp, dst in ((1, at1), (2, at2)):
                _to_residues(sc, nq + j, dst, DILATIONS[grp], cols)
            for grp, dst in ((1, ls1), (2, ls2)):
                _to_residues(sc, 2 * nq + j, dst, DILATIONS[grp], cols)

    res = (0, 1, 2, 1, 2, 1, 2)
    return pl.pallas_call(
        body, name="att_gate_bwd", grid=(T // TT,),
        in_specs=[_tok_spec(ATT_W)] * 3 + [_tok_spec(ATT_W, C_ZA // ATT_W), pl.BlockSpec(memory_space=pl.ANY)],
        out_specs=[_tok_spec(ATT_W, C_ZA // ATT_W)] + [_res_spec(g) for g in res],
        out_shape=[S(dproj.shape, dproj.dtype)] + [_res_shape(g, BF16) for g in res[:5]]
        + [_res_shape(g, F32) for g in res[5:]],
        input_output_aliases={4: 0},
        scratch_shapes=[pltpu.VMEM((3 * nq, TT, 128), F32)],
        compiler_params=pltpu.CompilerParams(dimension_semantics=("arbitrary",)),
    )(dain, att, lse, proj, dproj)


def _split3(v):
    hi = v.astype(BF16)
    r1 = v - hi.astype(F32)
    mid = r1.astype(BF16)
    lo = (r1 - mid.astype(F32)).astype(BF16)
    return hi, mid, lo


def _tri_sum(v, upper):
    n = v.shape[0]
    row = lax.broadcasted_iota(jnp.int32, (n, n), 0)
    col = lax.broadcasted_iota(jnp.int32, (n, n), 1)
    tri = jnp.where(col >= row if upper else col <= row, 1.0, 0.0).astype(BF16)
    hi, mid, lo = _split3(v)
    return _nn(tri, hi) + _nn(tri, mid) + _nn(tri, lo)


def _gla_gates(glr_ref, w2_ref, b_ref):
    logit = _nn(glr_ref[...].astype(BF16), w2_ref[...]) + b_ref[...]
    lg = (jnp.minimum(logit, 0.0) - jnp.log(1.0 + jnp.exp(-jnp.abs(logit)))) * (1.0 / GLA_TAU)
    return logit, _tri_sum(lg, upper=False)


def _gla_head(cum, q_ref, k_ref, h):
    cols = slice(h * GDK, (h + 1) * GDK)
    b = cum[:, cols]
    last = b[GLA_C - 1:GLA_C, :]
    e_pos = jnp.exp(b)
    e_neg = jnp.exp(-b)
    e_end = jnp.exp(last - b)
    qt = q_ref[:, cols] * (GDK ** -0.5) * e_pos
    kt = k_ref[:, cols] * e_neg
    kh = k_ref[:, cols] * e_end
    return b, last, e_pos, e_neg, e_end, qt, kt, kh


def _causal(n):
    return lax.broadcasted_iota(jnp.int32, (n, n), 1) <= lax.broadcasted_iota(jnp.int32, (n, n), 0)


def _gla_fwd(proj, w2p, bg, gn):
    nc = T // GLA_C

    def body(q_ref, k_ref, v_ref, glr_ref, zg_ref, w2_ref, b_ref, gn_ref, o_ref, bin_ref, st_ref, state):
        @pl.when(pl.program_id(0) == 0)
        def _():
            state[...] = jnp.zeros_like(state)

        _, cum = _gla_gates(glr_ref, w2_ref, b_ref)
        for h in range(GH):
            _, last, _, _, _, qt, kt, kh = _gla_head(cum, q_ref, k_ref, h)
            vcols = slice(h * GDV, (h + 1) * GDV)
            st = state[h]
            st_ref[0, h] = st
            v = v_ref[:, vcols].astype(BF16)
            qb = qt.astype(BF16)
            a = jnp.where(_causal(GLA_C), _nt(qb, kt.astype(BF16)), 0.0)
            o = _nt(qb, st.astype(BF16)) + _nn(a.astype(BF16), v)
            state[h] = st * jnp.exp(last) + _tn(v, kh.astype(BF16))
            o_ref[:, vcols] = o
            r = lax.rsqrt(jnp.mean(o * o, axis=-1, keepdims=True) + EPS)
            zg = zg_ref[:, vcols]
            bin_ref[:, vcols] = (o * r * gn_ref[...] * zg * _sigmoid(zg)).astype(BF16)

    row = lambda width, cblk: pl.BlockSpec((GLA_C, width), functools.partial(lambda i, c: (i, c), c=cblk))
    full = lambda a: pl.BlockSpec(a.shape, functools.partial(lambda i, nd: (0,) * nd, nd=a.ndim))
    return pl.pallas_call(
        body, name="gla_fwd", grid=(nc,),
        in_specs=[row(512, C_QG // 512), row(512, C_KG // 512), row(1024, C_VG // 1024), row(GLR_W, C_GLR // GLR_W),
                  row(1024, C_ZG // 1024), full(w2p), full(bg), full(gn)],
        out_specs=[pl.BlockSpec((GLA_C, GH * GDV), lambda i: (i, 0)), pl.BlockSpec((GLA_C, GH * GDV), lambda i: (i, 0)),
                   pl.BlockSpec((1, GH, GDV, GDK), lambda i: (i, 0, 0, 0))],
        out_shape=[S((T, GH * GDV), F32), S((T, GH * GDV), BF16), S((nc, GH, GDV, GDK), F32)],
        scratch_shapes=[pltpu.VMEM((GH, GDV, GDK), F32)],
        compiler_params=pltpu.CompilerParams(dimension_semantics=("arbitrary",)),
    )(proj, proj, proj, proj, proj, w2p, bg, gn)


def _gla_bwd(proj, w2p, bg, gn, o_gla, states, dbin, dproj):
    nc = T // GLA_C

    def body(q_ref, k_ref, v_ref, glr_ref, zg_ref, w2_ref, b_ref, gn_ref, o_ref, st_ref, dbin_ref, buf_ref,
             out_ref, dw2_ref, dbg_ref, dgn_ref, dstate, dlogit):
        del buf_ref
        dq_ref = out_ref.at[:, C_QG:C_KG]
        dk_ref = out_ref.at[:, C_KG:C_VG]
        dv_ref = out_ref.at[:, C_VG:C_ZG]
        dzg_ref = out_ref.at[:, C_ZG:C_GLR]
        dglr_ref = out_ref.at[:, C_GLR:C_GLR + GLR_W]
        first = pl.program_id(0) == 0

        @pl.when(first)
        def _():
            dstate[...] = jnp.zeros_like(dstate)

        logit, cum = _gla_gates(glr_ref, w2_ref, b_ref)
        is_last = lax.broadcasted_iota(jnp.int32, (GLA_C, 1), 0) == GLA_C - 1
        dgn = jnp.zeros((1, GDV), F32)
        for h in range(GH):
            _, last, e_pos, e_neg, e_end, qt, kt, kh = _gla_head(cum, q_ref, k_ref, h)
            cols = slice(h * GDK, (h + 1) * GDK)
            vcols = slice(h * GDV, (h + 1) * GDV)
            o = o_ref[:, vcols]
            r = lax.rsqrt(jnp.mean(o * o, axis=-1, keepdims=True) + EPS)
            zg = zg_ref[:, vcols]
            sg = _sigmoid(zg)
            db_ = dbin_ref[:, vcols].astype(F32)
            dlin = db_ * zg * sg
            dzg_ref[:, vcols] = (db_ * (o * r * gn_ref[...]) * sg * (1.0 + zg * (1.0 - sg))).astype(BF16)
            u = dlin * gn_ref[...]
            do = (r * u - o * (r * r * r) * jnp.mean(u * o, axis=-1, keepdims=True)).astype(BF16)
            dgn = dgn + jnp.sum(dlin * o * r, axis=0, keepdims=True)
            st = st_ref[0, h]
            dst = dstate[h]
            v = v_ref[:, vcols].astype(BF16)
            qb, kb, khb = qt.astype(BF16), kt.astype(BF16), kh.astype(BF16)
            dstb = dst.astype(BF16)
            causal = _causal(GLA_C)
            a = jnp.where(causal, _nt(qb, kb), 0.0).astype(BF16)
            da = jnp.where(causal, _nt(do, v), 0.0).astype(BF16)
            dqt = _nn(do, st.astype(BF16)) + _nn(da, kb)
            dkt = _tn(da, qb)
            dkh = _nn(v, dstb)
            dv_ref[:, vcols] = (_tn(a, do) + _nt(khb, dstb)).astype(BF16)
            lam = jnp.exp(last)
            dlam = jnp.sum(dst * st, axis=0, keepdims=True)
            dstate[h] = dst * lam + _tn(do, qb)
            dq_ref[:, cols] = (dqt * e_pos * (GDK ** -0.5)).astype(BF16)
            dk_ref[:, cols] = (dkt * e_neg + dkh * e_end).astype(BF16)
            dkh_kh = dkh * kh
            dcum = dqt * qt - dkt * kt - dkh_kh
            dlast = jnp.sum(dkh_kh, axis=0, keepdims=True) + dlam * lam
            dcum = jnp.where(is_last, dcum + dlast, dcum)
            dlg = _tri_sum(dcum, upper=True)
            dlogit[:, cols] = dlg * (1.0 / GLA_TAU) * (1.0 - _sigmoid(logit[:, cols]))

        dl = dlogit[...]
        dlb = dl.astype(BF16)
        dglr_ref[...] = _nt(dlb, w2_ref[...]).astype(BF16)
        dw2 = _tn(glr_ref[...].astype(BF16), dlb)
        dbg = jnp.sum(dl, axis=0, keepdims=True)

        @pl.when(first)
        def _():
            dw2_ref[...] = dw2
            dbg_ref[...] = dbg
            dgn_ref[...] = dgn

        @pl.when(jnp.logical_not(first))
        def _():
            dw2_ref[...] += dw2
            dbg_ref[...] += dbg
            dgn_ref[...] += dgn

    rev = lambda i: nc - 1 - i
    row = lambda width, cblk: pl.BlockSpec((GLA_C, width), functools.partial(lambda i, c: (rev(i), c), c=cblk))
    full = lambda a: pl.BlockSpec(a.shape, functools.partial(lambda i, nd: (0,) * nd, nd=a.ndim))
    keep = lambda shape: pl.BlockSpec(shape, functools.partial(lambda i, nd: (0,) * nd, nd=len(shape)))
    return pl.pallas_call(
        body, name="gla_bwd", grid=(nc,),
        in_specs=[row(512, C_QG // 512), row(512, C_KG // 512), row(1024, C_VG // 1024), row(GLR_W, C_GLR // GLR_W),
                  row(1024, C_ZG // 1024), full(w2p), full(bg), full(gn), row(GH * GDV, 0),
                  pl.BlockSpec((1, GH, GDV, GDK), lambda i: (rev(i), 0, 0, 0)), row(GH * GDV, 0),
                  pl.BlockSpec(memory_space=pl.ANY)],
        out_specs=[row(GLA_GROUP_W, 0), keep((GLR_W, 512)), keep((1, 512)), keep((1, GDV))],
        out_shape=[S(dproj.shape, dproj.dtype), S((GLR_W, 512), F32), S((1, 512), F32), S((1, GDV), F32)],
        input_output_aliases={11: 0},
        scratch_shapes=[pltpu.VMEM((GH, GDV, GDK), F32), pltpu.VMEM((GLA_C, GH * GDK), F32)],
        compiler_params=pltpu.CompilerParams(dimension_semantics=("arbitrary",)),
    )(proj, proj, proj, proj, proj, w2p, bg, gn, o_gla, states, dbin, dproj)


RT = 512


def _rowchain(body, name, ins, outs, scratch=()):
    in_specs, args = [], []
    for spec in ins:
        if spec[0] == "tok":
            _, arr, width, cblk = spec
            in_specs.append(pl.BlockSpec((RT, width), functools.partial(lambda i, c: (i, c), c=cblk)))
        else:
            arr = spec[1]
            in_specs.append(pl.BlockSpec(arr.shape, functools.partial(lambda i, nd: (0,) * nd, nd=arr.ndim)))
        args.append(arr)
    out_specs, out_shape = [], []
    for spec in outs:
        if spec[0] == "tok":
            _, shape, dtype, width, cblk = spec
            out_specs.append(pl.BlockSpec((RT, width), functools.partial(lambda i, c: (i, c), c=cblk)))
        else:
            _, shape, dtype = spec
            out_specs.append(pl.BlockSpec(shape, functools.partial(lambda i, nd: (0,) * nd, nd=len(shape))))
        out_shape.append(S(shape, dtype))
    return pl.pallas_call(
        body, name=name, grid=(T // RT,), in_specs=in_specs, out_specs=out_specs, out_shape=out_shape,
        scratch_shapes=list(scratch), compiler_params=pltpu.CompilerParams(dimension_semantics=("arbitrary",)),
    )(*args)


def _tok(arr, width=None, cblk=0):
    return ("tok", arr, arr.shape[1] if width is None else width, cblk)


def _tok_out(dtype, width=D):
    return ("tok", (T, width), dtype, width, 0)


def _branches_fwd(ain, bin_, proj, x, w_att, w_gla, w_out):
    def body(ain_ref, bin_ref, g_ref, x_ref, wa_ref, wg_ref, wo_ref, ya_ref, yb_ref, y_ref, x1_ref):
        ya = _nn(ain_ref[...], wa_ref[...]).astype(BF16)
        yb = _nn(bin_ref[...], wg_ref[...]).astype(BF16)
        ya_ref[...] = ya
        yb_ref[...] = yb
        y = (_sigmoid(g_ref[:, :D]) * ya.astype(F32) + _sigmoid(g_ref[:, D:]) * yb.astype(F32)).astype(BF16)
        y_ref[...] = y
        x1_ref[...] = x_ref[...] + _nn(y, wo_ref[...])

    return _rowchain(body, "branches_fwd",
                     [_tok(ain), _tok(bin_), _tok(proj, 2 * D, C_GA // (2 * D)), _tok(x), ("all", w_att),
                      ("all", w_gla), ("all", w_out)],
                     [_tok_out(BF16), _tok_out(BF16), _tok_out(BF16), _tok_out(F32)])


def _ple_loss(x1, p, target, g2, w_pg, w_ple):
    def body(x1_ref, p_ref, t_ref, g_ref, wpg_ref, wple_ref, n2_ref, loss_ref, dout_ref, de_ref, du_ref, acc):
        first = pl.program_id(0) == 0
        x1 = x1_ref[...]
        r = lax.rsqrt(jnp.mean(x1 * x1, axis=-1, keepdims=True) + EPS)
        n2 = (x1 * r * g_ref[...]).astype(BF16)
        n2_ref[...] = n2
        pg = _sigmoid(_nn(n2, wpg_ref[...]))
        e_ = _nn(p_ref[...].astype(BF16), wple_ref[...])
        diff = x1 + e_ * pg - t_ref[...]
        part = jnp.sum(diff * diff, axis=0, keepdims=True)

        @pl.when(first)
        def _():
            acc[...] = part

        @pl.when(jnp.logical_not(first))
        def _():
            acc[...] += part

        dout = diff * (1.0 / D)
        dout_ref[...] = dout
        de_ref[...] = (dout * pg).astype(BF16)
        du_ref[...] = (dout * e_ * pg * (1.0 - pg)).astype(BF16)
        loss_ref[...] = jnp.zeros((1, 128), F32) + jnp.sum(acc[...], axis=-1, keepdims=True) * (0.5 / D)

    return _rowchain(body, "ple_loss", [_tok(x1), _tok(p), _tok(target), ("all", g2), ("all", w_pg), ("all", w_ple)],
                     [_tok_out(BF16), ("acc", (1, 128), F32), _tok_out(F32), _tok_out(BF16), _tok_out(BF16)],
                     scratch=[pltpu.VMEM((1, D), F32)])


def _ple_bwd(du, x1, dout, g2, w_pg, w_out):
    def body(du_ref, x1_ref, dout_ref, g_ref, wpg_ref, wo_ref, dx_ref, dxb_ref, dy_ref, dg_ref):
        first = pl.program_id(0) == 0
        x1 = x1_ref[...]
        r = lax.rsqrt(jnp.mean(x1 * x1, axis=-1, keepdims=True) + EPS)
        dn = _nt(du_ref[...], wpg_ref[...])
        u = dn * g_ref[...]
        dx = dout_ref[...] + r * u - x1 * (r * r * r) * jnp.mean(u * x1, axis=-1, keepdims=True)
        dxb = dx.astype(BF16)
        dx_ref[...] = dx
        dxb_ref[...] = dxb
        dy_ref[...] = _nt(dxb, wo_ref[...]).astype(BF16)
        part = jnp.sum(dn * x1 * r, axis=0, keepdims=True)

        @pl.when(first)
        def _():
            dg_ref[...] = part

        @pl.when(jnp.logical_not(first))
        def _():
            dg_ref[...] += part

    return _rowchain(body, "ple_bwd", [_tok(du), _tok(x1), _tok(dout), ("all", g2), ("all", w_pg), ("all", w_out)],
                     [_tok_out(F32), _tok_out(BF16), _tok_out(BF16), ("acc", (1, D), F32)])


def _branches_bwd(dy, ya, yb, proj, w_att, w_gla):
    def body(dy_ref, ya_ref, yb_ref, g_ref, wa_ref, wg_ref, dg_ref, dya_ref, dyb_ref, dain_ref, dbin_ref):
        dy_ = dy_ref[...].astype(F32)
        sa, sb = _sigmoid(g_ref[:, :D]), _sigmoid(g_ref[:, D:])
        dg_ref[:, :D] = (dy_ * ya_ref[...].astype(F32) * sa * (1.0 - sa)).astype(BF16)
        dg_ref[:, D:] = (dy_ * yb_ref[...].astype(F32) * sb * (1.0 - sb)).astype(BF16)
        dya = (dy_ * sa).astype(BF16)
        dyb = (dy_ * sb).astype(BF16)
        dya_ref[...] = dya
        dyb_ref[...] = dyb
        dain_ref[...] = _nt(dya, wa_ref[...]).astype(BF16)
        dbin_ref[...] = _nt(dyb, wg_ref[...]).astype(BF16)

    gates = C_GA // (2 * D)
    return _rowchain(body, "branches_bwd",
                     [_tok(dy), _tok(ya), _tok(yb), _tok(proj, 2 * D, gates), ("all", w_att), ("all", w_gla)],
                     [("tok", (T, NCOL), BF16, 2 * D, gates), _tok_out(BF16), _tok_out(BF16), _tok_out(BF16, ATT_W),
                      _tok_out(BF16)])


def _peer(k):
    x, y, c = lax.axis_index("x"), lax.axis_index("y"), lax.axis_index("c")
    return (x ^ ((k >> 2) & 1), y ^ ((k >> 1) & 1), c ^ (k & 1))


def _my_index():
    return 4 * lax.axis_index("x") + 2 * lax.axis_index("y") + lax.axis_index("c")


def _peer_index(k):
    px, py, pc = _peer(k)
    return 4 * px + 2 * py + pc


def _pairwise_plan(src_of, dst_of, landed_of, own_src, own_dst):
    def plan(ins, outs, send, recv, local):
        n = len(ins)

        def own():
            return [pltpu.make_async_copy(own_src(ins[a]), own_dst(outs[a]), local.at[a]) for a in range(n)]

        def remote(k, a, src, dst):
            return pltpu.make_async_remote_copy(src_ref=src, dst_ref=dst, send_sem=send.at[k - 1, a],
                                                recv_sem=recv.at[k - 1, a], device_id=_peer(k), device_id_type=MESH)

        def sent():
            return [remote(k, a, src_of(ins[a], k), dst_of(outs[a])) for k in range(1, NDEV) for a in range(n)]

        def start():
            for cp in own() + sent():
                cp.start()

        def finish():
            for k in range(1, NDEV):
                for a in range(n):
                    remote(k, a, own_src(ins[a]), landed_of(outs[a], k)).wait_recv()
            for cp in sent():
                cp.wait_send()
            for cp in own():
                cp.wait()

        return start, finish

    return plan


def _pairwise_sems(n):
    return [pltpu.SemaphoreType.DMA((NDEV - 1, n)), pltpu.SemaphoreType.DMA((NDEV - 1, n)),
            pltpu.SemaphoreType.DMA((n,))]


def _gather_side(arrs):
    plan = _pairwise_plan(src_of=lambda i, k: i, dst_of=lambda o: o.at[_my_index()],
                          landed_of=lambda o, k: o.at[_peer_index(k)],
                          own_src=lambda i: i, own_dst=lambda o: o.at[_my_index()])
    return dict(arrs=arrs, out_shape=[S((NDEV,) + a.shape, a.dtype) for a in arrs],
                scratch=_pairwise_sems(len(arrs)), plan=plan)


def _exchange_side(arrs):
    plan = _pairwise_plan(src_of=lambda i, k: i.at[_peer_index(k)], dst_of=lambda o: o.at[_my_index()],
                          landed_of=lambda o, k: o.at[_peer_index(k)],
                          own_src=lambda i: i.at[_my_index()], own_dst=lambda o: o.at[_my_index()])
    return dict(arrs=arrs, out_shape=[S(a.shape, a.dtype) for a in arrs], scratch=_pairwise_sems(len(arrs)), plan=plan)


def _comm_call(side, name):
    n = len(side["arrs"])

    def body(*refs):
        start, finish = side["plan"](refs[:n], refs[n:2 * n], *refs[2 * n:])
        start()
        finish()

    hbm = pl.BlockSpec(memory_space=pl.ANY)
    return pl.pallas_call(body, name=name, in_specs=[hbm] * n, out_specs=[hbm] * n, out_shape=side["out_shape"],
                          scratch_shapes=side["scratch"])(*side["arrs"])


def _all_gather_by_chip(arrs, name):
    n = len(arrs)

    def body(*refs):
        ins, outs = refs[:n], refs[n:2 * n]
        send, recv, local = refs[2 * n:]
        x, y, c = lax.axis_index("x"), lax.axis_index("y"), lax.axis_index("c")
        me, sibling = (x, y, c), (x, y, 1 - c)
        chips = [(1 - x, y), (x, 1 - y), (1 - x, 1 - y)]

        def copy(k, a, block, to, src=None):
            px, py, pc = block
            slot = outs[a].at[4 * px + 2 * py + pc]
            return pltpu.make_async_remote_copy(
                src_ref=slot if src is None else src, dst_ref=slot, send_sem=send.at[k, a], recv_sem=recv.at[k, a],
                device_id=to, device_id_type=MESH)

        mine = [pltpu.make_async_copy(ins[a], outs[a].at[4 * x + 2 * y + c], local.at[a]) for a in range(n)]
        first = []
        for a in range(n):
            first.append(copy(0, a, me, sibling, src=ins[a]))
            first += [copy(1 + j, a, me, (*chip, c), src=ins[a]) for j, chip in enumerate(chips)]
        for cp in mine + first:
            cp.start()
        passed = []
        for j, chip in enumerate(chips):
            for a in range(n):
                copy(1 + j, a, (*chip, c), me).wait_recv()
                passed.append(copy(4 + j, a, (*chip, c), sibling))
                passed[-1].start()
        for a in range(n):
            copy(0, a, sibling, me).wait_recv()
        for j, chip in enumerate(chips):
            for a in range(n):
                copy(4 + j, a, (*chip, 1 - c), me).wait_recv()
        for cp in first + passed:
            cp.wait_send()
        for cp in mine:
            cp.wait()

    hbm = pl.BlockSpec(memory_space=pl.ANY)
    return pl.pallas_call(
        body, name=name, in_specs=[hbm] * n, out_specs=[hbm] * n,
        out_shape=[S((NDEV,) + a.shape, a.dtype) for a in arrs],
        scratch_shapes=[pltpu.SemaphoreType.DMA((NDEV - 1, n)), pltpu.SemaphoreType.DMA((NDEV - 1, n)),
                        pltpu.SemaphoreType.DMA((n,))],
    )(*arrs)


NCHIP = 4


def _exchange_sibling(arrs, name):
    n = len(arrs)

    def body(*refs):
        ins, outs = refs[:n], refs[n:2 * n]
        send, recv = refs[2 * n:]
        x, y, c = lax.axis_index("x"), lax.axis_index("y"), lax.axis_index("c")
        copies = []
        for q in range(NCHIP):
            for a in range(n):
                copies.append(pltpu.make_async_remote_copy(
                    src_ref=ins[a].at[2 * q + (1 - c)], dst_ref=outs[a].at[q], send_sem=send.at[q, a],
                    recv_sem=recv.at[q, a], device_id=(x, y, 1 - c), device_id_type=MESH))
        for cp in copies:
            cp.start()
        for cp in copies:
            cp.wait_recv()
        for cp in copies:
            cp.wait_send()

    hbm = pl.BlockSpec(memory_space=pl.ANY)
    return pl.pallas_call(
        body, name=name, in_specs=[hbm] * n, out_specs=[hbm] * n,
        out_shape=[S((NCHIP,) + a.shape[1:], a.dtype) for a in arrs],
        scratch_shapes=[pltpu.SemaphoreType.DMA((NCHIP, n)), pltpu.SemaphoreType.DMA((NCHIP, n))],
    )(*arrs)


def _pair_add(mine, got, core, name):
    _, rows, cols = mine.shape
    tc = 256
    assert cols % tc == 0

    def body(core_ref, a_ref, b_ref, o_ref):
        o_ref[...] = (a_ref[...].astype(F32) + b_ref[...].astype(F32)).astype(BF16)

    return pl.pallas_call(
        body, name=name,
        grid_spec=pltpu.PrefetchScalarGridSpec(
            num_scalar_prefetch=1, grid=(NCHIP, cols // tc),
            in_specs=[pl.BlockSpec((None, rows, tc), lambda q, i, core_ref: (2 * q + core_ref[0], 0, i)),
                      pl.BlockSpec((None, rows, tc), lambda q, i, core_ref: (q, 0, i))],
            out_specs=pl.BlockSpec((None, rows, tc), lambda q, i, core_ref: (q, 0, i))),
        out_shape=S((NCHIP, rows, cols), BF16),
    )(core, mine, got)


def _chips_side(arrs):
    def plan(ins, outs, send, recv, local):
        n = len(ins)

        def places():
            x, y, c = lax.axis_index("x"), lax.axis_index("y"), lax.axis_index("c")
            return 2 * x + y, c, [(1 - x, y), (x, 1 - y), (1 - x, 1 - y)]

        def own():
            here, _, _ = places()
            return [pltpu.make_async_copy(ins[a].at[here], outs[a].at[here], local.at[a]) for a in range(n)]

        def remote(j, a, src_slot, dst_slot):
            _, c, chips = places()
            cx, cy = chips[j]
            return pltpu.make_async_remote_copy(
                src_ref=ins[a].at[src_slot], dst_ref=outs[a].at[dst_slot], send_sem=send.at[j, a],
                recv_sem=recv.at[j, a], device_id=(cx, cy, c), device_id_type=MESH)

        def sent():
            here, _, chips = places()
            return [remote(j, a, 2 * cx + cy, here) for j, (cx, cy) in enumerate(chips) for a in range(n)]

        def start():
            for cp in own() + sent():
                cp.start()

        def finish():
            here, _, chips = places()
            for j, (cx, cy) in enumerate(chips):
                for a in range(n):
                    remote(j, a, here, 2 * cx + cy).wait_recv()
            for cp in sent():
                cp.wait_send()
            for cp in own():
                cp.wait()

        return start, finish

    n = len(arrs)
    return dict(arrs=arrs, out_shape=[S(a.shape, a.dtype) for a in arrs],
                scratch=[pltpu.SemaphoreType.DMA((NCHIP - 1, n)), pltpu.SemaphoreType.DMA((NCHIP - 1, n)),
                         pltpu.SemaphoreType.DMA((n,))], plan=plan)


def _adamw(parts, w, m, v, name, tr, tc=None):
    rows, cols = w.shape
    if tc is None:
        assert rows % tr == 0
        grid, shape, at = (rows // tr,), (tr, cols), (lambda i: (i, 0))
    else:
        assert cols % tc == 0
        grid, shape, at = (cols // tc,), (rows, tc), (lambda i: (0, i))
    c1 = 1.0 - ADAM_B1 ** ADAM_STEP
    c2 = 1.0 - ADAM_B2 ** ADAM_STEP

    nparts = parts.shape[0]

    def body(p_ref, w_ref, m_ref, v_ref, g_ref, d_ref, mo_ref, vo_ref):
        g = p_ref[0].astype(F32)
        for s in range(1, nparts):
            g = g + p_ref[s].astype(F32)
        m_new = ADAM_B1 * m_ref[...] + (1.0 - ADAM_B1) * g
        v_new = ADAM_B2 * v_ref[...] + (1.0 - ADAM_B2) * (g * g)
        g_ref[...] = g
        mo_ref[...] = m_new
        vo_ref[...] = v_new
        d_ref[...] = -ADAM_LR * ((m_new / c1) / (jnp.sqrt(v_new / c2) + ADAM_EPS) + ADAM_WD * w_ref[...])

    blk = pl.BlockSpec(shape, at)
    return pl.pallas_call(
        body, name=name, grid=grid,
        in_specs=[pl.BlockSpec((nparts,) + shape, lambda i: (0,) + at(i)), blk, blk, blk],
        out_specs=[blk] * 4, out_shape=[S((rows, cols), F32)] * 4,
        compiler_params=pltpu.CompilerParams(dimension_semantics=("parallel",)),
    )(parts, w, m, v)


def _adam_math(g, w, m, v):
    c1 = 1.0 - ADAM_B1 ** ADAM_STEP
    c2 = 1.0 - ADAM_B2 ** ADAM_STEP
    m_new = ADAM_B1 * m + (1.0 - ADAM_B1) * g
    v_new = ADAM_B2 * v + (1.0 - ADAM_B2) * (g * g)
    return -ADAM_LR * ((m_new / c1) / (jnp.sqrt(v_new / c2) + ADAM_EPS) + ADAM_WD * w), m_new, v_new


def _adamw_small(parts, params):
    n = len(params)

    def body(*refs):
        p_refs, rest = refs[:n], refs[n:]
        for j in range(n):
            w_ref, m_ref, v_ref = rest[3 * j:3 * j + 3]
            g_ref, d_ref, mo_ref, vo_ref = rest[3 * n + 4 * j:3 * n + 4 * j + 4]
            width = w_ref.shape[1]
            g = p_refs[j][0]
            for s in range(1, NDEV):
                g = g + p_refs[j][s]
            g = g[:, :width]
            delta, m_new, v_new = _adam_math(g, w_ref[...], m_ref[...], v_ref[...])
            g_ref[...] = g
            d_ref[...] = delta
            mo_ref[...] = m_new
            vo_ref[...] = v_new

    flat = [a for group in params for a in group]
    return pl.pallas_call(
        body, name="adam_small",
        out_shape=[S(group[0].shape, F32) for group in params for _ in range(4)],
    )(*parts, *flat)


def _adamw_rows(parts, w, m, v, name, tc=128):
    rows, _, cols = w.shape
    nparts = parts.shape[0]
    c1 = 1.0 - ADAM_B1 ** ADAM_STEP
    c2 = 1.0 - ADAM_B2 ** ADAM_STEP

    def body(p_ref, w_ref, m_ref, v_ref, g_ref, d_ref, mo_ref, vo_ref):
        flat = lambda ref: ref[...].reshape(rows, tc)
        g = p_ref[0].astype(F32)
        for s in range(1, nparts):
            g = g + p_ref[s].astype(F32)
        m_new = ADAM_B1 * flat(m_ref) + (1.0 - ADAM_B1) * g
        v_new = ADAM_B2 * flat(v_ref) + (1.0 - ADAM_B2) * (g * g)
        delta = -ADAM_LR * ((m_new / c1) / (jnp.sqrt(v_new / c2) + ADAM_EPS) + ADAM_WD * flat(w_ref))
        for ref, val in ((g_ref, g), (d_ref, delta), (mo_ref, m_new), (vo_ref, v_new)):
            ref[...] = val.reshape(rows, 1, tc)

    blk = pl.BlockSpec((rows, 1, tc), lambda i: (0, 0, i))
    return pl.pallas_call(
        body, name=name, grid=(cols // tc,),
        in_specs=[pl.BlockSpec((nparts, rows, tc), lambda i: (0, 0, i)), blk, blk, blk],
        out_specs=[blk] * 4, out_shape=[S((rows, 1, cols), F32)] * 4,
        compiler_params=pltpu.CompilerParams(dimension_semantics=("parallel",)),
    )(parts, w, m, v)


def _to_aligned(wt):
    pad = jnp.zeros((GLR_W - GLR_N, wt.shape[1]), wt.dtype)
    return jnp.concatenate([wt[O_QG:O_GLR], wt[O_ZG:O_GA], wt[O_GLR:O_ZG], pad, wt[O_ZA:O_QG], wt[O_GA:O_END],
                            wt[O_QA:O_ZA]], axis=0)


def _from_aligned(wt):
    return jnp.concatenate([wt[C_QA:], wt[C_ZA:C_GA], wt[C_QG:C_ZG], wt[C_GLR:C_GLR + GLR_N], wt[C_ZG:C_GLR],
                            wt[C_GA:C_QA]], axis=0)


def _col_blocks(w, width):
    return w.reshape(w.shape[0], NDEV, width).transpose(1, 0, 2)


def _from_col_blocks(w):
    return w.transpose(1, 0, 2).reshape(w.shape[1], NDEV * w.shape[2])


def _local_step(x2, p2, pos, tgt, norm_g, qk_norm_q, qk_norm_k, gla_gate_b, gla_norm_g, ple_norm_g, w_al,
                weights=None, proj_side=None, unpack=None, dw_side_of=None, dh_side_of=None):
    half = ROT_DIM // 2
    inv8 = jnp.power(jnp.float32(ROPE_THETA), -jnp.arange(half, dtype=F32) * 2.0 / ROT_DIM)
    inv = jnp.tile(jnp.concatenate([inv8, inv8, jnp.zeros((HD - ROT_DIM,), F32)]), 2).reshape(1, 128)
    gq = jnp.tile(qk_norm_q, (1, 2))
    gk = jnp.tile(qk_norm_k, (1, 2))

    h = _rms_fwd(x2, norm_g, "rms1_fwd")
    if proj_side is None:
        proj = _mm(h, w_al, mode="nt", name="proj", tm=1024, tn=1536, tk=D)
    else:
        proj, got = _mm(h, w_al, mode="nt", name="proj", tm=1024, tn=1536, tk=D, side=proj_side)
        weights = unpack(got)
    w2p, w_att_f, w_gla_f, w_out_f, w_pg_f, w_ple_f = weights
    qkv = _qk_prep(proj, pos, inv, gq, gk)
    fwd = [_att_fwd(qkv[g], qkv[3 + g], qkv[6 + g], g, f"att_fwd{g}") for g in range(3)]
    att, lse, ain = _att_merge([f[0] for f in fwd], [f[1] for f in fwd], proj)
    o_gla, bin_, states = _gla_fwd(proj, w2p, gla_gate_b, gla_norm_g)
    ya, yb, y, x1 = _branches_fwd(ain, bin_, proj, x2, w_att_f, w_gla_f, w_out_f)
    n2, loss_v, dout, de, du = _ple_loss(x1, p2, tgt, ple_norm_g, w_pg_f, w_ple_f)

    dw_ple = _mm(p2, de, mode="tn", name="dw_ple", tm=PLE, tn=D, tk=512)
    dw_pg = _mm(n2, du, mode="tn", name="dw_pg", tm=D, tn=D, tk=512)
    dx1, dx1b, dy, dg_ple = _ple_bwd(du, x1, dout, ple_norm_g, w_pg_f, w_out_f)
    dw_out = _mm(y, dx1b, mode="tn", name="dw_out", tm=D, tn=D, tk=512)
    dproj, dya, dyb, dain, dbin = _branches_bwd(dy, ya, yb, proj, w_att_f, w_gla_f)
    dw_att = _mm(ain, dya, mode="tn", name="dw_att", tm=512, tn=D, tk=512)
    dw_gla = _mm(bin_, dyb, mode="tn", name="dw_gla", tm=D, tn=D, tk=512)
    dproj, da0, da1, da2, at1, at2, ls1, ls2 = _att_gate_bwd(dain, att, lse, proj, dproj)
    datts, atts, lses = (da0, da1, da2), (att[None], at1, at2), (lse[None], ls1, ls2)
    dproj, dw2, dbg, dgn = _gla_bwd(proj, w2p, gla_gate_b, gla_norm_g, o_gla, states, dbin, dproj)
    bwd = [_att_bwd(qkv[g], qkv[3 + g], qkv[6 + g], datts[g], atts[g], lses[g], g, f"att_bwd{g}") for g in range(3)]
    dproj, dgq, dgk = _qk_bwd(proj, pos, inv, gq, gk, [b[0] for b in bwd], [b[1] for b in bwd],
                              [b[2] for b in bwd], dproj)
    out = dict(loss=loss_v, dw2=dw2, dw_att=dw_att, dw_gla=dw_gla, dw_out=dw_out, dw_pg=dw_pg, dw_ple=dw_ple,
               dgq=dgq, dgk=dgk, dbg=dbg, dgn=dgn, dg_ple=dg_ple)
    if dw_side_of is None:
        dw_al = _mm(dproj, h, mode="tn", name="dw_in", tm=1536, tn=D, tk=2048, out_dtype=BF16)
    else:
        dw_al, out["dw_side"] = _mm(dproj, h, mode="tn", name="dw_in", tm=1536, tn=D, tk=2048, out_dtype=BF16,
                                    side=dw_side_of(out))
    if dh_side_of is None:
        dh = _mm(dproj, w_al, mode="nn", name="dh", tm=1024, tn=D, tk=3584)
    else:
        dh, out["dh_side"] = _mm(dproj, w_al, mode="nn", name="dh", tm=1024, tn=D, tk=3584, side=dh_side_of(dw_al))
    grad_x, _, dg_norm = _rms_bwd(dh, x2, norm_g, dx1, "rms1_bwd")
    out.update(grad_x=grad_x, dw_al=dw_al, dg_norm=dg_norm)
    return out


def kernel(x, p, positions, norm_g, w_in, qk_norm_q, qk_norm_k, gla_gate_w2, gla_gate_b, gla_norm_g, w_att_proj, w_gla_proj, w_out, ple_norm_g, w_ple_gate, w_ple, loss_target, m_norm_g, m_w_in, m_qk_norm_q, m_qk_norm_k, m_gla_gate_w2, m_gla_gate_b, m_gla_norm_g, m_w_att_proj, m_w_gla_proj, m_w_out, m_ple_norm_g, m_w_ple_gate, m_w_ple, v_norm_g, v_w_in, v_qk_norm_q, v_qk_norm_k, v_gla_gate_w2, v_gla_gate_b, v_gla_norm_g, v_w_att_proj, v_w_gla_proj, v_w_out, v_ple_norm_g, v_w_ple_gate, v_w_ple):
    x2, p2, tgt = x[0], p[0, 0], loss_target[0]
    pos = positions.astype(F32).reshape(T, 1)

    rows3 = jnp.stack([w_gla_proj[0], w_out[0], w_ple_gate[0]]).astype(BF16)
    cols3 = jnp.concatenate([w_att_proj[0], w_ple[0], jnp.pad(gla_gate_w2[0], ((0, 0), (0, 64)))], axis=0).astype(BF16)
    (g_in,) = _all_gather_by_chip([w_in[0].T.astype(BF16)], "gather_w_in")
    w_al = _to_aligned(g_in.reshape(W_IN_COLS, D))

    def unpack(got):
        g_rows, g_cols = got
        w2_f = _from_col_blocks(g_cols[:, 768:784, :64])
        return (jnp.pad(w2_f, ((0, GLR_W - GLR_N), (0, 0))), _from_col_blocks(g_cols[:, :512]),
                g_rows[:, 0].reshape(D, D), g_rows[:, 1].reshape(D, D), g_rows[:, 2].reshape(D, D),
                _from_col_blocks(g_cols[:, 512:768]))

    def dw_side_of(g):
        s_rows = jnp.concatenate([g[k].reshape(NDEV, 128, D) for k in ("dw_gla", "dw_out", "dw_pg")], axis=1)
        s_cols = jnp.concatenate([_col_blocks(g["dw_att"], 128), _col_blocks(g["dw_ple"], 128),
                                  jnp.pad(_col_blocks(g["dw2"][:GLR_N], 64), ((0, 0), (0, 0), (0, 64)))], axis=1)
        return _exchange_side([s_rows.astype(BF16), s_cols.astype(BF16)])

    def dh_side_of(dw_al):
        s_in = _from_aligned(dw_al).astype(BF16).reshape(NDEV, W_IN_SHARD, D)
        (from_sibling,) = _exchange_sibling([s_in], "exchange_sibling")
        core = lax.axis_index("c").astype(jnp.int32).reshape(1)
        return _chips_side([_pair_add(s_in, from_sibling, core, "pair_add")])

    loc = _local_step(x2, p2, pos, tgt, norm_g, qk_norm_q, qk_norm_k, gla_gate_b, gla_norm_g, ple_norm_g, w_al,
                      proj_side=_gather_side([rows3, cols3]), unpack=unpack, dw_side_of=dw_side_of,
                      dh_side_of=dh_side_of)
    loss_v, grad_x = loc["loss"], loc["grad_x"]
    dg_norm, dgq, dgk, dbg, dgn, dg_ple = (loc[k] for k in ("dg_norm", "dgq", "dgk", "dbg", "dgn", "dg_ple"))
    r_rows, r_cols = loc["dw_side"]
    (r_in,) = loc["dh_side"]

    r_small = _comm_call(_gather_side([dg_norm, dgq, dgk, dbg, dgn, dg_ple]), "gather_small")

    outs = {}

    def adam(nm, parts, w, m, v, tr):
        outs[nm] = _adamw(parts, w, m, v, "adam_" + nm, tr)

    rows_of = lambda a: jnp.transpose(a, (2, 0, 1))
    outs["w_in"] = [jnp.transpose(o, (1, 2, 0))[0] for o in
                    _adamw_rows(r_in, rows_of(w_in), rows_of(m_w_in), rows_of(v_w_in), "adam_w_in")]
    adam("w_gla_proj", r_rows[:, :128], w_gla_proj[0], m_w_gla_proj[0], v_w_gla_proj[0], 128)
    adam("w_out", r_rows[:, 128:256], w_out[0], m_w_out[0], v_w_out[0], 128)
    adam("w_ple_gate", r_rows[:, 256:], w_ple_gate[0], m_w_ple_gate[0], v_w_ple_gate[0], 128)
    adam("w_att_proj", r_cols[:, :512], w_att_proj[0], m_w_att_proj[0], v_w_att_proj[0], 512)
    adam("w_ple", r_cols[:, 512:768], w_ple[0], m_w_ple[0], v_w_ple[0], 256)
    adam("gla_gate_w2", r_cols[:, 768:784, :64], gla_gate_w2[0], m_gla_gate_w2[0], v_gla_gate_w2[0], 16)
    small = ((norm_g, m_norm_g, v_norm_g), (qk_norm_q, m_qk_norm_q, v_qk_norm_q), (qk_norm_k, m_qk_norm_k, v_qk_norm_k),
             (gla_gate_b, m_gla_gate_b, v_gla_gate_b), (gla_norm_g, m_gla_norm_g, v_gla_norm_g),
             (ple_norm_g, m_ple_norm_g, v_ple_norm_g))
    sm = _adamw_small(r_small, small)
    for j, nm in enumerate(("norm_g", "qk_norm_q", "qk_norm_k", "gla_gate_b", "gla_norm_g", "ple_norm_g")):
        outs[nm] = [o[0] for o in sm[4 * j:4 * j + 4]]

    loss = lax.psum(loss_v[0, 0], ("x", "y", "c"))
    order = ["norm_g", "w_in", "qk_norm_q", "qk_norm_k", "gla_gate_w2", "gla_gate_b", "gla_norm_g", "w_att_proj",
             "w_gla_proj", "w_out", "ple_norm_g", "w_ple_gate", "w_ple"]
    result = [loss, grad_x[None]]
    for i in range(4):
        result += [outs[nm][i][None] for nm in order]
    return tuple(result)
```

```python
import functools

import jax
import jax.numpy as jnp
from jax import lax
from jax.experimental import pallas as pl
from jax.experimental.pallas import tpu as pltpu

F32 = jnp.float32
BF16 = jnp.bfloat16
S = jax.ShapeDtypeStruct

T = 4096
D = 1024
NDEV = 8
HD = 64
ATT_W = 512
ATT_QKV = 1536
DILATIONS = (1, 4, 16)
BLK = 128
GH, GDK, GDV = 4, 128, 256
GLA_C = 128
PLE = 256
EPS = 1e-6
ROT_DIM = 16
ROPE_THETA = 500000.0
GLA_TAU = 16.0
W_IN_COLS = 10256
W_IN_SHARD = 1282

C_QG, C_KG, C_VG, C_ZG, C_GLR, C_ZA, C_GA, C_GB, C_QA, C_KA, C_VA = (
    0, 512, 1024, 2048, 3072, 3584, 4096, 5120, 6144, 7680, 9216)
GLA_GROUP_W = 3584
GLR_W = 512
NCOL = 10752
GLR_N = 16
O_QA, O_ZA, O_QG, O_GLR, O_ZG, O_GA, O_END = 0, 4608, 5120, 7168, 7184, 8208, 10256

ADAM_LR, ADAM_B1, ADAM_B2, ADAM_EPS, ADAM_WD, ADAM_STEP = 0.001, 0.9, 0.999, 1e-08, 0.01, 10

MESH = pl.DeviceIdType.MESH


def _sigmoid(z):
    return 1.0 / (1.0 + jnp.exp(-z))


def _dot(a, b, dims):
    return lax.dot_general(a, b, (dims, ((), ())), preferred_element_type=F32)


def _nn(a, b):
    return _dot(a, b, ((1,), (0,)))


def _nt(a, b):
    return _dot(a, b, ((1,), (1,)))


def _tn(a, b):
    return _dot(a, b, ((0,), (0,)))


def _mm(a, b, *, mode, name, tm, tn, tk, out_dtype=F32, res=None, side=None):
    if mode == "nn":
        (m, k), n = a.shape, b.shape[1]
        a_spec = pl.BlockSpec((tm, tk), lambda i, j, l: (i, l))
        b_spec = pl.BlockSpec((tk, tn), lambda i, j, l: (l, j))
        dot = _nn
    elif mode == "nt":
        (m, k), n = a.shape, b.shape[0]
        a_spec = pl.BlockSpec((tm, tk), lambda i, j, l: (i, l))
        b_spec = pl.BlockSpec((tn, tk), lambda i, j, l: (j, l))
        dot = _nt
    else:
        (k, m), n = a.shape, b.shape[1]
        a_spec = pl.BlockSpec((tk, tm), lambda i, j, l: (l, i))
        b_spec = pl.BlockSpec((tk, tn), lambda i, j, l: (l, j))
        dot = _tn
    assert m % tm == 0 and n % tn == 0 and k % tk == 0, (name, m, n, k)
    grid = (m // tm, n // tn, k // tk)
    nk = grid[2]
    o_spec = pl.BlockSpec((tm, tn), lambda i, j, l: (i, j))
    in_specs = [a_spec, b_spec]
    args = [a, b]
    if res is not None:
        in_specs.append(o_spec)
        args.append(res)
    n_in = len(args)
    n_side = 0 if side is None else len(side["arrs"])
    hbm = pl.BlockSpec(memory_space=pl.ANY)

    def body(*refs):
        a_ref, b_ref = refs[0], refs[1]
        r_ref = refs[2] if res is not None else None
        o_ref = refs[n_in + n_side]
        scratch = refs[n_in + 2 * n_side + 1:]
        if side is not None:
            start, finish_side = side["plan"](refs[n_in:n_in + n_side], refs[n_in + n_side + 1:n_in + 2 * n_side + 1],
                                              *scratch[1 if nk > 1 else 0:])
            ids = [pl.program_id(d) for d in range(3)]

            @pl.when((ids[0] == 0) & (ids[1] == 0) & (ids[2] == 0))
            def _():
                start()

        part = dot(a_ref[...].astype(BF16), b_ref[...].astype(BF16))

        def finish(val):
            if r_ref is not None:
                val = val + r_ref[...]
            o_ref[...] = val.astype(out_dtype)

        if nk == 1:
            finish(part)
        else:
            acc = scratch[0]
            l = pl.program_id(2)

            @pl.when(l == 0)
            def _():
                acc[...] = part

            @pl.when(l > 0)
            def _():
                acc[...] += part

            @pl.when(l == nk - 1)
            def _():
                finish(acc[...])

        if side is not None:
            @pl.when((ids[0] == grid[0] - 1) & (ids[1] == grid[1] - 1) & (ids[2] == grid[2] - 1))
            def _():
                finish_side()

    sems = [] if side is None else side["scratch"]
    outs = pl.pallas_call(
        body, name=name, grid=grid,
        in_specs=in_specs + [hbm] * n_side, out_specs=[o_spec] + [hbm] * n_side,
        out_shape=[S((m, n), out_dtype)] + ([] if side is None else side["out_shape"]),
        scratch_shapes=([pltpu.VMEM((tm, tn), F32)] if nk > 1 else []) + sems,
        compiler_params=pltpu.CompilerParams(
            dimension_semantics=("arbitrary",) * 3 if side is not None else ("parallel", "parallel", "arbitrary")),
    )(*args, *([] if side is None else side["arrs"]))
    return outs[0] if side is None else (outs[0], outs[1:])


def _side_parts(side, refs, n_in, n_out):
    n_side = 0 if side is None else len(side["arrs"])
    scratch = refs[n_in + n_out + 2 * n_side:]
    if side is None:
        return (lambda: None), (lambda: None), scratch
    start, finish = side["plan"](refs[n_in:n_in + n_side], refs[n_in + n_side + n_out:n_in + n_out + 2 * n_side],
                                 *scratch[len(scratch) - len(side["scratch"]):])
    return start, finish, scratch


def _proj_rms(x, g, wt, side=None):
    tm, tn = 1024, 1536
    grid = (T // tm, NCOL // tn)
    n_side = 0 if side is None else len(side["arrs"])
    hbm = pl.BlockSpec(memory_space=pl.ANY)

    def body(*refs):
        x_ref, g_ref, w_ref = refs[:3]
        o_ref, h_ref = refs[3 + n_side], refs[4 + n_side]
        start, finish, _ = _side_parts(side, refs, 3, 2)
        i, j = pl.program_id(0), pl.program_id(1)

        @pl.when((i == 0) & (j == 0))
        def _():
            start()

        @pl.when(j == 0)
        def _():
            xf = x_ref[...]
            r = lax.rsqrt(jnp.mean(xf * xf, axis=-1, keepdims=True) + EPS)
            h_ref[...] = (xf * r * g_ref[...]).astype(BF16)

        o_ref[...] = _nt(h_ref[...], w_ref[...])

        @pl.when((i == grid[0] - 1) & (j == grid[1] - 1))
        def _():
            finish()

    outs = pl.pallas_call(
        body, name="proj", grid=grid,
        in_specs=[pl.BlockSpec((tm, D), lambda i, j: (i, 0)), pl.BlockSpec((1, D), lambda i, j: (0, 0)),
                  pl.BlockSpec((tn, D), lambda i, j: (j, 0))] + [hbm] * n_side,
        out_specs=[pl.BlockSpec((tm, tn), lambda i, j: (i, j)), pl.BlockSpec((tm, D), lambda i, j: (i, 0))] + [hbm] * n_side,
        out_shape=[S((T, NCOL), F32), S((T, D), BF16)] + ([] if side is None else side["out_shape"]),
        scratch_shapes=[] if side is None else side["scratch"],
        compiler_params=pltpu.CompilerParams(dimension_semantics=("arbitrary", "arbitrary")),
    )(x, g, wt, *([] if side is None else side["arrs"]))
    return outs[0], outs[1], outs[2:]


def _dh_rms(dproj, wt, x, g, skip, side=None):
    tm, tk = 1024, 1792
    grid = (T // tm, NCOL // tk)
    n_side = 0 if side is None else len(side["arrs"])
    hbm = pl.BlockSpec(memory_space=pl.ANY)

    def body(*refs):
        a_ref, w_ref, x_ref, g_ref, s_ref = refs[:5]
        dx_ref, dg_ref = refs[5 + n_side], refs[6 + n_side]
        start, finish, scratch = _side_parts(side, refs, 5, 2)
        acc = scratch[0]
        i, l = pl.program_id(0), pl.program_id(1)

        @pl.when((i == 0) & (l == 0))
        def _():
            start()

        part = _nn(a_ref[...], w_ref[...])

        @pl.when(l == 0)
        def _():
            acc[...] = part

        @pl.when(l > 0)
        def _():
            acc[...] += part

        @pl.when(l == grid[1] - 1)
        def _():
            xf = x_ref[...]
            r = lax.rsqrt(jnp.mean(xf * xf, axis=-1, keepdims=True) + EPS)
            dn = acc[...]
            u = dn * g_ref[...]
            dx_ref[...] = s_ref[...] + r * u - xf * (r * r * r) * jnp.mean(u * xf, axis=-1, keepdims=True)
            dg = jnp.sum(dn * xf * r, axis=0, keepdims=True)

            @pl.when(i == 0)
            def _():
                dg_ref[...] = dg

            @pl.when(i > 0)
            def _():
                dg_ref[...] += dg

        @pl.when((i == grid[0] - 1) & (l == grid[1] - 1))
        def _():
            finish()

    tok = pl.BlockSpec((tm, D), lambda i, l: (i, 0))
    outs = pl.pallas_call(
        body, name="dh", grid=grid,
        in_specs=[pl.BlockSpec((tm, tk), lambda i, l: (i, l)), pl.BlockSpec((tk, D), lambda i, l: (l, 0)), tok,
                  pl.BlockSpec((1, D), lambda i, l: (0, 0)), tok] + [hbm] * n_side,
        out_specs=[tok, pl.BlockSpec((1, D), lambda i, l: (0, 0))] + [hbm] * n_side,
        out_shape=[S((T, D), F32), S((1, D), F32)] + ([] if side is None else side["out_shape"]),
        scratch_shapes=[pltpu.VMEM((tm, D), F32)] + ([] if side is None else side["scratch"]),
        compiler_params=pltpu.CompilerParams(dimension_semantics=("arbitrary", "arbitrary")),
    )(dproj, wt, x, g, skip, *([] if side is None else side["arrs"]))
    return outs[0], outs[1], outs[2:]


def _rot_tables(pos_ref, inv_ref):
    lane = lax.broadcasted_iota(jnp.int32, (1, 128), 1) % HD
    ang = pos_ref[...] * inv_ref[...]
    cos, sin = jnp.cos(ang), jnp.sin(ang)
    c = jnp.where(lane < ROT_DIM, cos, 1.0)
    sp = jnp.where((lane >= ROT_DIM // 2) & (lane < ROT_DIM), sin, 0.0)
    sm = jnp.where(lane < ROT_DIM // 2, -sin, 0.0)
    return c, sp, sm


def _head_sums(v):
    same = (lax.broadcasted_iota(jnp.int32, (128, 128), 0) < HD) == (lax.broadcasted_iota(jnp.int32, (128, 128), 1) < HD)
    ones = jnp.where(same, 1.0, 0.0).astype(BF16)
    hi = v.astype(BF16)
    lo = (v - hi.astype(F32)).astype(BF16)
    return _nn(hi, ones) + _nn(lo, ones)


def _pair_norm(t):
    return lax.rsqrt(_head_sums(t * t) * (1.0 / HD) + EPS)


def _pair_mean(t):
    return _head_sums(t) * (1.0 / HD)


TT = 256
NCH = ATT_QKV // 128


def _res_shape(grp, dtype):
    return S((DILATIONS[grp], T // DILATIONS[grp], ATT_W), dtype)


def _res_spec(grp):
    dil = DILATIONS[grp]
    return pl.BlockSpec((dil, TT // dil, ATT_W), lambda i: (0, i, 0))


def _to_residues(sc, j, dst_ref, dil, cols):
    n = TT // dil
    for r in range(dil):
        rows = sc[j] if dil == 1 else sc.at[j][pl.ds(r, n, stride=dil), :]
        dst_ref[r, :, cols] = rows.astype(dst_ref.dtype)


def _from_residues(src_ref, cols, sc, j, dil):
    n = TT // dil
    for r in range(dil):
        if dil == 1:
            sc[j] = src_ref[r, :, cols]
        else:
            sc.at[j][pl.ds(r, n, stride=dil), :] = src_ref[r, :, cols]


def _tok_spec(width, cblk=0):
    return pl.BlockSpec((TT, width), functools.partial(lambda i, c: (i, c), c=cblk))


def _const_spec(arr_or_shape):
    shape = arr_or_shape if isinstance(arr_or_shape, tuple) else arr_or_shape.shape
    return pl.BlockSpec(shape, functools.partial(lambda i, nd: (0,) * nd, nd=len(shape)))


def _qk_prep(proj, pos, inv, gq, gk):
    def body(q_ref, k_ref, v_ref, pos_ref, inv_ref, gq_ref, gk_ref, *rest):
        outs, sc = rest[:9], rest[9]
        c, sp, sm = _rot_tables(pos_ref, inv_ref)
        for which, (src, g_ref) in enumerate(((q_ref, gq_ref), (k_ref, gk_ref), (v_ref, None))):
            if g_ref is not None:
                g = jnp.broadcast_to(g_ref[...] * ((HD ** -0.5) if which == 0 else 1.0), c.shape)
                cg, spg, smg = c * g, sp * pltpu.roll(g, 8, 1), sm * pltpu.roll(g, 120, 1)
            for j in range(NCH):
                t = src[:, j * 128:(j + 1) * 128]
                if g_ref is not None:
                    t = _pair_norm(t) * (t * cg + pltpu.roll(t, 8, 1) * spg + pltpu.roll(t, 120, 1) * smg)
                sc[j] = t
            for j in range(NCH):
                grp, sub = divmod(j * 128, ATT_W)
                _to_residues(sc, j, outs[which * 3 + grp], DILATIONS[grp], slice(sub, sub + 128))

    return pl.pallas_call(
        body, name="qk_prep", grid=(T // TT,),
        in_specs=[_tok_spec(ATT_QKV, C_QA // ATT_QKV), _tok_spec(ATT_QKV, C_KA // ATT_QKV),
                  _tok_spec(ATT_QKV, C_VA // ATT_QKV), _tok_spec(1), _const_spec(inv), _const_spec(gq), _const_spec(gk)],
        out_specs=[_res_spec(g) for _ in range(3) for g in range(3)],
        out_shape=[_res_shape(g, BF16) for _ in range(3) for g in range(3)],
        scratch_shapes=[pltpu.VMEM((NCH, TT, 128), F32)],
        compiler_params=pltpu.CompilerParams(dimension_semantics=("arbitrary",)),
    )(proj, proj, proj, pos, inv, gq, gk)


def _qk_bwd(proj, pos, inv, gq, gk, dqs, dks, dvs, dproj):
    const = lambda a: pl.BlockSpec(a.shape, functools.partial(lambda i, p, nd: (0,) * nd, nd=a.ndim))
    res = lambda g: pl.BlockSpec((DILATIONS[g], TT // DILATIONS[g], ATT_W), lambda i, p: (0, i, 0))
    base = C_QA // ATT_QKV

    def body(t_ref, pos_ref, inv_ref, gq_ref, gk_ref, dq0, dq1, dq2, dk0, dk1, dk2, dv0, dv1, dv2, buf_ref,
             out_ref, dgq_ref, dgk_ref, sc):
        del buf_ref
        part = pl.program_id(1)
        first = pl.program_id(0) == 0

        def gather(drefs):
            for j in range(NCH):
                grp, sub = divmod(j * 128, ATT_W)
                _from_residues(drefs[grp], slice(sub, sub + 128), sc, j, DILATIONS[grp])

        def normed(g_ref, drefs, dg_ref):
            c, sp, sm = _rot_tables(pos_ref, inv_ref)
            gather(drefs)
            dg = jnp.zeros((1, 128), F32)
            for j in range(NCH):
                cols = slice(j * 128, (j + 1) * 128)
                d_rot = sc[j]
                dn = d_rot * c + pltpu.roll(d_rot * sp, 120, 1) + pltpu.roll(d_rot * sm, 8, 1)
                t = t_ref[:, cols]
                r = _pair_norm(t)
                u = dn * g_ref[...]
                out_ref[:, cols] = (r * u - t * (r * r * r) * _pair_mean(u * t)).astype(BF16)
                dg = dg + jnp.sum(dn * t * r, axis=0, keepdims=True)
            dg = dg + pltpu.roll(dg, HD, 1)

            @pl.when(first)
            def _():
                dg_ref[...] = dg

            @pl.when(jnp.logical_not(first))
            def _():
                dg_ref[...] += dg

        @pl.when(part == 0)
        def _():
            gather((dv0, dv1, dv2))
            for j in range(NCH):
                out_ref[:, j * 128:(j + 1) * 128] = sc[j].astype(BF16)

        @pl.when(part == 1)
        def _():
            normed(gq_ref, (dq0, dq1, dq2), dgq_ref)

        @pl.when(part == 2)
        def _():
            normed(gk_ref, (dk0, dk1, dk2), dgk_ref)

    keep = pl.BlockSpec((1, 128), lambda i, p: (0, 0))
    return pl.pallas_call(
        body, name="qk_bwd", grid=(T // TT, 3),
        in_specs=[pl.BlockSpec((TT, ATT_QKV), lambda i, p: (i, base + jnp.maximum(p - 1, 0))),
                  pl.BlockSpec((TT, 1), lambda i, p: (i, 0)), const(inv), const(gq), const(gk)]
        + [res(g) for _ in range(3) for g in range(3)] + [pl.BlockSpec(memory_space=pl.ANY)],
        out_specs=[pl.BlockSpec((TT, ATT_QKV), lambda i, p: (i, base + jnp.where(p == 0, 2, p - 1))), keep, keep],
        out_shape=[S(dproj.shape, dproj.dtype), S((1, 128), F32), S((1, 128), F32)],
        input_output_aliases={14: 0},
        scratch_shapes=[pltpu.VMEM((NCH, TT, 128), F32)],
        compiler_params=pltpu.CompilerParams(dimension_semantics=("arbitrary", "arbitrary")),
    )(proj, pos, inv, gq, gk, *dqs, *dks, *dvs, dproj)


def _split_heads(t):
    low = lax.broadcasted_iota(jnp.int32, (1, 128), 1) < HD
    zero = jnp.zeros_like(t)
    return jnp.concatenate([jnp.where(low, t, zero), jnp.where(low, zero, t)], axis=0)


def _join_heads(t2):
    low = lax.broadcasted_iota(jnp.int32, (1, 128), 1) < HD
    n = t2.shape[0] // 2
    return jnp.where(low, t2[:n], t2[n:])


def _band_mask4(has_before, has_own):
    row = lax.broadcasted_iota(jnp.int32, (BLK, 4 * BLK), 0)
    lane = lax.broadcasted_iota(jnp.int32, (BLK, 4 * BLK), 1)
    key = lane & (BLK - 1)
    own = lane >= 2 * BLK
    return (own & (key <= row) & has_own) | (jnp.logical_not(own) & (key >= row) & has_before)


def _band_mask_before(has_before):
    row = lax.broadcasted_iota(jnp.int32, (BLK, 2 * BLK), 0)
    key = lax.broadcasted_iota(jnp.int32, (BLK, 2 * BLK), 1) & (BLK - 1)
    return (key >= row) & has_before


def _per_head(width, col_a, col_b):
    lane = lax.broadcasted_iota(jnp.int32, (1, width), 1)
    return jnp.where((lane & BLK) == 0, col_a, col_b)


NQ = ATT_W // 128


def _att_fwd(q, k, v, grp, name):
    dil = DILATIONS[grp]
    nb = T // dil // BLK

    def body(q_ref, kp_ref, kc_ref, vp_ref, vc_ref, o_ref, lse_ref, s_sc, p_sc):
        mask = _band_mask4(pl.program_id(1) > 0, True)
        low = lax.broadcasted_iota(jnp.int32, (1, 128), 1) < HD
        halves = lambda ref, j, h: (ref[j, :, h * BLK:(h + 1) * BLK], ref[j, :, (h + 2) * BLK:(h + 3) * BLK])
        for j in range(NQ):
            cols = slice(j * 128, (j + 1) * 128)
            k4 = jnp.concatenate([_split_heads(kp_ref[:, cols]), _split_heads(kc_ref[:, cols])], axis=0)
            s_sc[j] = jnp.where(mask, _nt(q_ref[:, cols], k4), -jnp.inf)
        mxs = [[jnp.maximum(*(jnp.max(t, axis=-1, keepdims=True) for t in halves(s_sc, j, h))) for h in range(2)]
               for j in range(NQ)]
        dens = []
        for j in range(NQ):
            p = jnp.exp(s_sc[j] - _per_head(4 * BLK, *mxs[j]))
            p_sc[j] = p.astype(BF16)
            dens.append([jnp.sum(p[:, h * BLK:(h + 1) * BLK], axis=-1, keepdims=True)
                         + jnp.sum(p[:, (h + 2) * BLK:(h + 3) * BLK], axis=-1, keepdims=True) for h in range(2)])
        for j in range(NQ):
            cols = slice(j * 128, (j + 1) * 128)
            v4 = jnp.concatenate([_split_heads(vp_ref[:, cols]), _split_heads(vc_ref[:, cols])], axis=0)
            o_ref[:, cols] = _nn(p_sc[j], v4) / jnp.where(low, dens[j][0], dens[j][1])
            lse_ref[:, cols] = jnp.where(low, mxs[j][0] + jnp.log(dens[j][0]), mxs[j][1] + jnp.log(dens[j][1]))

    cur = pl.BlockSpec((None, BLK, ATT_W), lambda r, i: (r, i, 0))
    prev = pl.BlockSpec((None, BLK, ATT_W), lambda r, i: (r, jnp.maximum(i - 1, 0), 0))
    return pl.pallas_call(
        body, name=name, grid=(dil, nb),
        in_specs=[cur, prev, cur, prev, cur],
        out_specs=[cur, cur], out_shape=[_res_shape(grp, F32)] * 2,
        scratch_shapes=[pltpu.VMEM((NQ, BLK, 4 * BLK), F32), pltpu.VMEM((NQ, BLK, 4 * BLK), BF16)],
        compiler_params=pltpu.CompilerParams(dimension_semantics=("parallel", "arbitrary")),
    )(q, k, k, v, v)


def _att_bwd(q, k, v, datt, att, lse, grp, name):
    dil = DILATIONS[grp]
    nb = T // dil // BLK
    scale = HD ** -0.5

    def body(q0_ref, q1_ref, kp_ref, kc_ref, vp_ref, vc_ref, do0_ref, do1_ref, o0_ref, o1_ref, l0_ref, l1_ref,
             dq_ref, dk_ref, dv_ref, k4_sc, v4_sc, s0_sc, s1_sc, dp0_sc, dp1_sc, p_sc, ds_sc):
        i = pl.program_id(1)
        mask_mine = _band_mask4(i > 0, True)
        mask_next = _band_mask_before(i < nb - 1)
        low = lax.broadcasted_iota(jnp.int32, (1, 128), 1) < HD
        for j in range(NQ):
            cols = slice(j * 128, (j + 1) * 128)
            k4_sc[j, :2 * BLK] = _split_heads(kp_ref[:, cols])
            k4_sc[j, 2 * BLK:] = _split_heads(kc_ref[:, cols])
            v4_sc[j, :2 * BLK] = _split_heads(vp_ref[:, cols])
            v4_sc[j, 2 * BLK:] = _split_heads(vc_ref[:, cols])
        for j in range(NQ):
            cols = slice(j * 128, (j + 1) * 128)
            s0_sc[j] = _nt(q0_ref[:, cols], k4_sc[j])
            s1_sc[j] = _nt(q1_ref[:, cols], k4_sc[j, 2 * BLK:])
            dp0_sc[j] = _nt(do0_ref[:, cols].astype(BF16), v4_sc[j])
            dp1_sc[j] = _nt(do1_ref[:, cols].astype(BF16), v4_sc[j, 2 * BLK:])
        stats = []
        for j in range(NQ):
            cols = slice(j * 128, (j + 1) * 128)
            for do_ref, o_ref, l_ref in ((do0_ref, o0_ref, l0_ref), (do1_ref, o1_ref, l1_ref)):
                prod = do_ref[:, cols].astype(F32) * o_ref[:, cols].astype(F32)
                d_all = jnp.sum(prod, axis=-1, keepdims=True)
                d_low = jnp.sum(jnp.where(low, prod, 0.0), axis=-1, keepdims=True)
                lse_t = l_ref[:, cols]
                stats.append((d_low, d_all - d_low, lse_t[:, 0:1], lse_t[:, HD:HD + 1]))
        for j in range(NQ):
            (da, db, la, lb), (da1, db1, la1, lb1) = stats[2 * j], stats[2 * j + 1]
            p0 = jnp.where(mask_mine, jnp.exp(s0_sc[j] - _per_head(4 * BLK, la, lb)), 0.0)
            ds0 = p0 * (dp0_sc[j] - _per_head(4 * BLK, da, db))
            p1 = jnp.where(mask_next, jnp.exp(s1_sc[j] - _per_head(2 * BLK, la1, lb1)), 0.0)
            ds1 = p1 * (dp1_sc[j] - _per_head(2 * BLK, da1, db1))
            p_sc[j, :BLK] = p0.astype(BF16)
            ds_sc[j, :BLK] = ds0.astype(BF16)
            p_sc[j, BLK:, 2 * BLK:] = p1.astype(BF16)
            ds_sc[j, BLK:, 2 * BLK:] = ds1.astype(BF16)
        for j in range(NQ):
            cols = slice(j * 128, (j + 1) * 128)
            dq_ref[:, cols] = _nn(ds_sc[j, :BLK], k4_sc[j]) * scale
            qq = jnp.concatenate([q0_ref[:, cols], q1_ref[:, cols]], axis=0)
            dd = jnp.concatenate([do0_ref[:, cols], do1_ref[:, cols]], axis=0).astype(BF16)
            dk_ref[:, cols] = _join_heads(_tn(ds_sc[j, :, 2 * BLK:], qq))
            dv_ref[:, cols] = _join_heads(_tn(p_sc[j, :, 2 * BLK:], dd))

    def spec(shift):
        return pl.BlockSpec((None, BLK, ATT_W), lambda r, i: (r, jnp.clip(i + shift, 0, nb - 1), 0))

    here, after, before = spec(0), spec(1), spec(-1)
    vm = pltpu.VMEM
    return pl.pallas_call(
        body, name=name, grid=(dil, nb),
        in_specs=[here, after, before, here, before, here, here, after, here, after, here, after],
        out_specs=[here] * 3, out_shape=[_res_shape(grp, F32)] * 3,
        scratch_shapes=[vm((NQ, 4 * BLK, 128), BF16), vm((NQ, 4 * BLK, 128), BF16), vm((NQ, BLK, 4 * BLK), F32),
                        vm((NQ, BLK, 2 * BLK), F32), vm((NQ, BLK, 4 * BLK), F32), vm((NQ, BLK, 2 * BLK), F32),
                        vm((NQ, 2 * BLK, 4 * BLK), BF16), vm((NQ, 2 * BLK, 4 * BLK), BF16)],
        compiler_params=pltpu.CompilerParams(dimension_semantics=("parallel", "arbitrary")),
    )(q, q, k, k, v, v, datt, datt, att, att, lse, lse)


def _att_merge(os_, lses, proj):
    nq = ATT_W // 128

    def body(o0, o1, o2, l0, l1, l2, za_ref, att_ref, lse_ref, ain_ref, sc):
        for a, ref in enumerate((o0, o1, o2, l0, l1, l2)):
            for j in range(nq):
                _from_residues(ref, slice(j * 128, (j + 1) * 128), sc, a * nq + j, DILATIONS[a % 3])
        for j in range(nq):
            cols = slice(j * 128, (j + 1) * 128)
            oa, ob, oc = (sc[a * nq + j] for a in range(3))
            la, lb, lc = (sc[(3 + a) * nq + j] for a in range(3))
            m = jnp.maximum(jnp.maximum(la, lb), lc)
            wa, wb, wc = jnp.exp(la - m), jnp.exp(lb - m), jnp.exp(lc - m)
            tot = wa + wb + wc
            att = (wa * oa + wb * ob + wc * oc) / tot
            att_ref[:, cols] = att
            lse_ref[:, cols] = m + jnp.log(tot)
            za = za_ref[:, cols]
            ain_ref[:, cols] = (att * za * _sigmoid(za)).astype(BF16)

    return pl.pallas_call(
        body, name="att_merge", grid=(T // TT,),
        in_specs=[_res_spec(g) for _ in range(2) for g in range(3)] + [_tok_spec(ATT_W, C_ZA // ATT_W)],
        out_specs=[_tok_spec(ATT_W)] * 3,
        out_shape=[S((T, ATT_W), F32), S((T, ATT_W), F32), S((T, ATT_W), BF16)],
        scratch_shapes=[pltpu.VMEM((6 * nq, TT, 128), F32)],
        compiler_params=pltpu.CompilerParams(dimension_semantics=("arbitrary",)),
    )(*os_, *lses, proj)


def _att_gate_bwd(dain, att, lse, proj, dproj):
    nq = ATT_W // 128

    def body(d_ref, att_ref, lse_ref, za_ref, buf_ref, dza_ref, da0, da1, da2, at1, at2, ls1, ls2, sc):
        del buf_ref
        for j in range(nq):
            cols = slice(j * 128, (j + 1) * 128)
            za = za_ref[:, cols]
            sg = _sigmoid(za)
            d = d_ref[:, cols].astype(F32)
            att_ = att_ref[:, cols]
            dza_ref[:, cols] = (d * att_ * sg * (1.0 + za * (1.0 - sg))).astype(BF16)
            sc[j] = d * za * sg
            sc[nq + j] = att_
            sc[2 * nq + j] = lse_ref[:, cols]
        for j in range(nq):
            cols = slice(j * 128, (j + 1) * 128)
            for grp, dst in enumerate((da0, da1, da2)):
                _to_residues(sc, j, dst, DILATIONS[grp], cols)
            for grp, dst in ((1, at1), (2, at2)):
                _to_residues(sc, nq + j, dst, DILATIONS[grp], cols)
            for grp, dst in ((1, ls1), (2, ls2)):
                _to_residues(sc, 2 * nq + j, dst, DILATIONS[grp], cols)

    res = (0, 1, 2, 1, 2, 1, 2)
    return pl.pallas_call(
        body, name="att_gate_bwd", grid=(T // TT,),
        in_specs=[_tok_spec(ATT_W)] * 3 + [_tok_spec(ATT_W, C_ZA // ATT_W), pl.BlockSpec(memory_space=pl.ANY)],
        out_specs=[_tok_spec(ATT_W, C_ZA // ATT_W)] + [_res_spec(g) for g in res],
        out_shape=[S(dproj.shape, dproj.dtype)] + [_res_shape(g, BF16) for g in res[:5]]
        + [_res_shape(g, F32) for g in res[5:]],
        input_output_aliases={4: 0},
        scratch_shapes=[pltpu.VMEM((3 * nq, TT, 128), F32)],
        compiler_params=pltpu.CompilerParams(dimension_semantics=("arbitrary",)),
    )(dain, att, lse, proj, dproj)


def _split3(v):
    hi = v.astype(BF16)
    r1 = v - hi.astype(F32)
    mid = r1.astype(BF16)
    lo = (r1 - mid.astype(F32)).astype(BF16)
    return hi, mid, lo


def _tri_sum(v, upper):
    n = v.shape[0]
    row = lax.broadcasted_iota(jnp.int32, (n, n), 0)
    col = lax.broadcasted_iota(jnp.int32, (n, n), 1)
    tri = jnp.where(col >= row if upper else col <= row, 1.0, 0.0).astype(BF16)
    hi, mid, lo = _split3(v)
    return _nn(tri, hi) + _nn(tri, mid) + _nn(tri, lo)


def _gla_gates(glr_ref, w2_ref, b_ref):
    logit = _nn(glr_ref[...].astype(BF16), w2_ref[...]) + b_ref[...]
    lg = (jnp.minimum(logit, 0.0) - jnp.log(1.0 + jnp.exp(-jnp.abs(logit)))) * (1.0 / GLA_TAU)
    return logit, _tri_sum(lg, upper=False)


def _gla_head(cum, q_ref, k_ref, h):
    cols = slice(h * GDK, (h + 1) * GDK)
    b = cum[:, cols]
    last = b[GLA_C - 1:GLA_C, :]
    e_pos = jnp.exp(b)
    e_neg = jnp.exp(-b)
    e_end = jnp.exp(last - b)
    qt = q_ref[:, cols] * (GDK ** -0.5) * e_pos
    kt = k_ref[:, cols] * e_neg
    kh = k_ref[:, cols] * e_end
    return b, last, e_pos, e_neg, e_end, qt, kt, kh


def _causal(n):
    return lax.broadcasted_iota(jnp.int32, (n, n), 1) <= lax.broadcasted_iota(jnp.int32, (n, n), 0)


def _gla_fwd(proj, w2p, bg, gn):
    nc = T // GLA_C

    def body(q_ref, k_ref, v_ref, glr_ref, zg_ref, w2_ref, b_ref, gn_ref, o_ref, bin_ref, st_ref, state):
        @pl.when(pl.program_id(0) == 0)
        def _():
            state[...] = jnp.zeros_like(state)

        _, cum = _gla_gates(glr_ref, w2_ref, b_ref)
        for h in range(GH):
            _, last, _, _, _, qt, kt, kh = _gla_head(cum, q_ref, k_ref, h)
            vcols = slice(h * GDV, (h + 1) * GDV)
            st = state[h]
            st_ref[0, h] = st
            v = v_ref[:, vcols].astype(BF16)
            qb = qt.astype(BF16)
            a = jnp.where(_causal(GLA_C), _nt(qb, kt.astype(BF16)), 0.0)
            o = _nt(qb, st.astype(BF16)) + _nn(a.astype(BF16), v)
            state[h] = st * jnp.exp(last) + _tn(v, kh.astype(BF16))
            o_ref[:, vcols] = o
            r = lax.rsqrt(jnp.mean(o * o, axis=-1, keepdims=True) + EPS)
            zg = zg_ref[:, vcols]
            bin_ref[:, vcols] = (o * r * gn_ref[...] * zg * _sigmoid(zg)).astype(BF16)

    row = lambda width, cblk: pl.BlockSpec((GLA_C, width), functools.partial(lambda i, c: (i, c), c=cblk))
    full = lambda a: pl.BlockSpec(a.shape, functools.partial(lambda i, nd: (0,) * nd, nd=a.ndim))
    return pl.pallas_call(
        body, name="gla_fwd", grid=(nc,),
        in_specs=[row(512, C_QG // 512), row(512, C_KG // 512), row(1024, C_VG // 1024), row(GLR_W, C_GLR // GLR_W),
                  row(1024, C_ZG // 1024), full(w2p), full(bg), full(gn)],
        out_specs=[pl.BlockSpec((GLA_C, GH * GDV), lambda i: (i, 0)), pl.BlockSpec((GLA_C, GH * GDV), lambda i: (i, 0)),
                   pl.BlockSpec((1, GH, GDV, GDK), lambda i: (i, 0, 0, 0))],
        out_shape=[S((T, GH * GDV), F32), S((T, GH * GDV), BF16), S((nc, GH, GDV, GDK), F32)],
        scratch_shapes=[pltpu.VMEM((GH, GDV, GDK), F32)],
        compiler_params=pltpu.CompilerParams(dimension_semantics=("arbitrary",)),
    )(proj, proj, proj, proj, proj, w2p, bg, gn)


def _gla_bwd(proj, w2p, bg, gn, o_gla, states, dbin, dproj):
    nc = T // GLA_C

    def body(q_ref, k_ref, v_ref, glr_ref, zg_ref, w2_ref, b_ref, gn_ref, o_ref, st_ref, dbin_ref, buf_ref,
             out_ref, dw2_ref, dbg_ref, dgn_ref, dstate, dlogit):
        del buf_ref
        dq_ref = out_ref.at[:, C_QG:C_KG]
        dk_ref = out_ref.at[:, C_KG:C_VG]
        dv_ref = out_ref.at[:, C_VG:C_ZG]
        dzg_ref = out_ref.at[:, C_ZG:C_GLR]
        dglr_ref = out_ref.at[:, C_GLR:C_GLR + GLR_W]
        first = pl.program_id(0) == 0

        @pl.when(first)
        def _():
            dstate[...] = jnp.zeros_like(dstate)

        logit, cum = _gla_gates(glr_ref, w2_ref, b_ref)
        is_last = lax.broadcasted_iota(jnp.int32, (GLA_C, 1), 0) == GLA_C - 1
        dgn = jnp.zeros((1, GDV), F32)
        for h in range(GH):
            _, last, e_pos, e_neg, e_end, qt, kt, kh = _gla_head(cum, q_ref, k_ref, h)
            cols = slice(h * GDK, (h + 1) * GDK)
            vcols = slice(h * GDV, (h + 1) * GDV)
            o = o_ref[:, vcols]
            r = lax.rsqrt(jnp.mean(o * o, axis=-1, keepdims=True) + EPS)
            zg = zg_ref[:, vcols]
            sg = _sigmoid(zg)
            db_ = dbin_ref[:, vcols].astype(F32)
            dlin = db_ * zg * sg
            dzg_ref[:, vcols] = (db_ * (o * r * gn_ref[...]) * sg * (1.0 + zg * (1.0 - sg))).astype(BF16)
            u = dlin * gn_ref[...]
            do = (r * u - o * (r * r * r) * jnp.mean(u * o, axis=-1, keepdims=True)).astype(BF16)
            dgn = dgn + jnp.sum(dlin * o * r, axis=0, keepdims=True)
            st = st_ref[0, h]
            dst = dstate[h]
            v = v_ref[:, vcols].astype(BF16)
            qb, kb, khb = qt.astype(BF16), kt.astype(BF16), kh.astype(BF16)
            dstb = dst.astype(BF16)
            causal = _causal(GLA_C)
            a = jnp.where(causal, _nt(qb, kb), 0.0).astype(BF16)
            da = jnp.where(causal, _nt(do, v), 0.0).astype(BF16)
            dqt = _nn(do, st.astype(BF16)) + _nn(da, kb)
            dkt = _tn(da, qb)
            dkh = _nn(v, dstb)
            dv_ref[:, vcols] = (_tn(a, do) + _nt(khb, dstb)).astype(BF16)
            lam = jnp.exp(last)
            dlam = jnp.sum(dst * st, axis=0, keepdims=True)
            dstate[h] = dst * lam + _tn(do, qb)
            dq_ref[:, cols] = (dqt * e_pos * (GDK ** -0.5)).astype(BF16)
            dk_ref[:, cols] = (dkt * e_neg + dkh * e_end).astype(BF16)
            dkh_kh = dkh * kh
            dcum = dqt * qt - dkt * kt - dkh_kh
            dlast = jnp.sum(dkh_kh, axis=0, keepdims=True) + dlam * lam
            dcum = jnp.where(is_last, dcum + dlast, dcum)
            dlg = _tri_sum(dcum, upper=True)
            dlogit[:, cols] = dlg * (1.0 / GLA_TAU) * (1.0 - _sigmoid(logit[:, cols]))

        dl = dlogit[...]
        dlb = dl.astype(BF16)
        dglr_ref[...] = _nt(dlb, w2_ref[...]).astype(BF16)
        dw2 = _tn(glr_ref[...].astype(BF16), dlb)
        dbg = jnp.sum(dl, axis=0, keepdims=True)

        @pl.when(first)
        def _():
            dw2_ref[...] = dw2
            dbg_ref[...] = dbg
            dgn_ref[...] = dgn

        @pl.when(jnp.logical_not(first))
        def _():
            dw2_ref[...] += dw2
            dbg_ref[...] += dbg
            dgn_ref[...] += dgn

    rev = lambda i: nc - 1 - i
    row = lambda width, cblk: pl.BlockSpec((GLA_C, width), functools.partial(lambda i, c: (rev(i), c), c=cblk))
    full = lambda a: pl.BlockSpec(a.shape, functools.partial(lambda i, nd: (0,) * nd, nd=a.ndim))
    keep = lambda shape: pl.BlockSpec(shape, functools.partial(lambda i, nd: (0,) * nd, nd=len(shape)))
    return pl.pallas_call(
        body, name="gla_bwd", grid=(nc,),
        in_specs=[row(512, C_QG // 512), row(512, C_KG // 512), row(1024, C_VG // 1024), row(GLR_W, C_GLR // GLR_W),
                  row(1024, C_ZG // 1024), full(w2p), full(bg), full(gn), row(GH * GDV, 0),
                  pl.BlockSpec((1, GH, GDV, GDK), lambda i: (rev(i), 0, 0, 0)), row(GH * GDV, 0),
                  pl.BlockSpec(memory_space=pl.ANY)],
        out_specs=[row(GLA_GROUP_W, 0), keep((GLR_W, 512)), keep((1, 512)), keep((1, GDV))],
        out_shape=[S(dproj.shape, dproj.dtype), S((GLR_W, 512), F32), S((1, 512), F32), S((1, GDV), F32)],
        input_output_aliases={11: 0},
        scratch_shapes=[pltpu.VMEM((GH, GDV, GDK), F32), pltpu.VMEM((GLA_C, GH * GDK), F32)],
        compiler_params=pltpu.CompilerParams(dimension_semantics=("arbitrary",)),
    )(proj, proj, proj, proj, proj, w2p, bg, gn, o_gla, states, dbin, dproj)


RT = 512


def _rowchain(body, name, ins, outs, scratch=()):
    in_specs, args = [], []
    for spec in ins:
        if spec[0] == "tok":
            _, arr, width, cblk = spec
            in_specs.append(pl.BlockSpec((RT, width), functools.partial(lambda i, c: (i, c), c=cblk)))
        else:
            arr = spec[1]
            in_specs.append(pl.BlockSpec(arr.shape, functools.partial(lambda i, nd: (0,) * nd, nd=arr.ndim)))
        args.append(arr)
    out_specs, out_shape = [], []
    for spec in outs:
        if spec[0] == "tok":
            _, shape, dtype, width, cblk = spec
            out_specs.append(pl.BlockSpec((RT, width), functools.partial(lambda i, c: (i, c), c=cblk)))
        else:
            _, shape, dtype = spec
            out_specs.append(pl.BlockSpec(shape, functools.partial(lambda i, nd: (0,) * nd, nd=len(shape))))
        out_shape.append(S(shape, dtype))
    return pl.pallas_call(
        body, name=name, grid=(T // RT,), in_specs=in_specs, out_specs=out_specs, out_shape=out_shape,
        scratch_shapes=list(scratch), compiler_params=pltpu.CompilerParams(dimension_semantics=("arbitrary",)),
    )(*args)


def _tok(arr, width=None, cblk=0):
    return ("tok", arr, arr.shape[1] if width is None else width, cblk)


def _tok_out(dtype, width=D):
    return ("tok", (T, width), dtype, width, 0)


def _branches_fwd(ain, bin_, proj, x, w_att, w_gla, w_out):
    def body(ain_ref, bin_ref, g_ref, x_ref, wa_ref, wg_ref, wo_ref, ya_ref, yb_ref, y_ref, x1_ref):
        ya = _nn(ain_ref[...], wa_ref[...]).astype(BF16)
        yb = _nn(bin_ref[...], wg_ref[...]).astype(BF16)
        ya_ref[...] = ya
        yb_ref[...] = yb
        y = (_sigmoid(g_ref[:, :D]) * ya.astype(F32) + _sigmoid(g_ref[:, D:]) * yb.astype(F32)).astype(BF16)
        y_ref[...] = y
        x1_ref[...] = x_ref[...] + _nn(y, wo_ref[...])

    return _rowchain(body, "branches_fwd",
                     [_tok(ain), _tok(bin_), _tok(proj, 2 * D, C_GA // (2 * D)), _tok(x), ("all", w_att),
                      ("all", w_gla), ("all", w_out)],
                     [_tok_out(BF16), _tok_out(BF16), _tok_out(BF16), _tok_out(F32)])


def _ple_loss(x1, p, target, g2, w_pg, w_ple):
    def body(x1_ref, p_ref, t_ref, g_ref, wpg_ref, wple_ref, n2_ref, loss_ref, dout_ref, de_ref, du_ref, acc):
        first = pl.program_id(0) == 0
        x1 = x1_ref[...]
        r = lax.rsqrt(jnp.mean(x1 * x1, axis=-1, keepdims=True) + EPS)
        n2 = (x1 * r * g_ref[...]).astype(BF16)
        n2_ref[...] = n2
        pg = _sigmoid(_nn(n2, wpg_ref[...]))
        e_ = _nn(p_ref[...].astype(BF16), wple_ref[...])
        diff = x1 + e_ * pg - t_ref[...]
        part = jnp.sum(diff * diff, axis=0, keepdims=True)

        @pl.when(first)
        def _():
            acc[...] = part

        @pl.when(jnp.logical_not(first))
        def _():
            acc[...] += part

        dout = diff * (1.0 / D)
        dout_ref[...] = dout
        de_ref[...] = (dout * pg).astype(BF16)
        du_ref[...] = (dout * e_ * pg * (1.0 - pg)).astype(BF16)
        loss_ref[...] = jnp.zeros((1, 128), F32) + jnp.sum(acc[...], axis=-1, keepdims=True) * (0.5 / D)

    return _rowchain(body, "ple_loss", [_tok(x1), _tok(p), _tok(target), ("all", g2), ("all", w_pg), ("all", w_ple)],
                     [_tok_out(BF16), ("acc", (1, 128), F32), _tok_out(F32), _tok_out(BF16), _tok_out(BF16)],
                     scratch=[pltpu.VMEM((1, D), F32)])


def _ple_bwd(du, x1, dout, g2, w_pg, w_out):
    def body(du_ref, x1_ref, dout_ref, g_ref, wpg_ref, wo_ref, dx_ref, dxb_ref, dy_ref, dg_ref):
        first = pl.program_id(0) == 0
        x1 = x1_ref[...]
        r = lax.rsqrt(jnp.mean(x1 * x1, axis=-1, keepdims=True) + EPS)
        dn = _nt(du_ref[...], wpg_ref[...])
        u = dn * g_ref[...]
        dx = dout_ref[...] + r * u - x1 * (r * r * r) * jnp.mean(u * x1, axis=-1, keepdims=True)
        dxb = dx.astype(BF16)
        dx_ref[...] = dx
        dxb_ref[...] = dxb
        dy_ref[...] = _nt(dxb, wo_ref[...]).astype(BF16)
        part = jnp.sum(dn * x1 * r, axis=0, keepdims=True)

        @pl.when(first)
        def _():
            dg_ref[...] = part

        @pl.when(jnp.logical_not(first))
        def _():
            dg_ref[...] += part

    return _rowchain(body, "ple_bwd", [_tok(du), _tok(x1), _tok(dout), ("all", g2), ("all", w_pg), ("all", w_out)],
                     [_tok_out(F32), _tok_out(BF16), _tok_out(BF16), ("acc", (1, D), F32)])


def _branches_bwd(dy, ya, yb, proj, w_att, w_gla):
    def body(dy_ref, ya_ref, yb_ref, g_ref, wa_ref, wg_ref, dg_ref, dya_ref, dyb_ref, dain_ref, dbin_ref):
        dy_ = dy_ref[...].astype(F32)
        sa, sb = _sigmoid(g_ref[:, :D]), _sigmoid(g_ref[:, D:])
        dg_ref[:, :D] = (dy_ * ya_ref[...].astype(F32) * sa * (1.0 - sa)).astype(BF16)
        dg_ref[:, D:] = (dy_ * yb_ref[...].astype(F32) * sb * (1.0 - sb)).astype(BF16)
        dya = (dy_ * sa).astype(BF16)
        dyb = (dy_ * sb).astype(BF16)
        dya_ref[...] = dya
        dyb_ref[...] = dyb
        dain_ref[...] = _nt(dya, wa_ref[...]).astype(BF16)
        dbin_ref[...] = _nt(dyb, wg_ref[...]).astype(BF16)

    gates = C_GA // (2 * D)
    return _rowchain(body, "branches_bwd",
                     [_tok(dy), _tok(ya), _tok(yb), _tok(proj, 2 * D, gates), ("all", w_att), ("all", w_gla)],
                     [("tok", (T, NCOL), BF16, 2 * D, gates), _tok_out(BF16), _tok_out(BF16), _tok_out(BF16, ATT_W),
                      _tok_out(BF16)])


def _peer(k):
    x, y, c = lax.axis_index("x"), lax.axis_index("y"), lax.axis_index("c")
    return (x ^ ((k >> 2) & 1), y ^ ((k >> 1) & 1), c ^ (k & 1))


def _my_index():
    return 4 * lax.axis_index("x") + 2 * lax.axis_index("y") + lax.axis_index("c")


def _peer_index(k):
    px, py, pc = _peer(k)
    return 4 * px + 2 * py + pc


def _pairwise_plan(src_of, dst_of, landed_of, own_src, own_dst):
    def plan(ins, outs, send, recv, local):
        n = len(ins)

        def own():
            return [pltpu.make_async_copy(own_src(ins[a]), own_dst(outs[a]), local.at[a]) for a in range(n)]

        def remote(k, a, src, dst):
            return pltpu.make_async_remote_copy(src_ref=src, dst_ref=dst, send_sem=send.at[k - 1, a],
                                                recv_sem=recv.at[k - 1, a], device_id=_peer(k), device_id_type=MESH)

        def sent():
            return [remote(k, a, src_of(ins[a], k), dst_of(outs[a])) for k in range(1, NDEV) for a in range(n)]

        def start():
            for cp in own() + sent():
                cp.start()

        def finish():
            for k in range(1, NDEV):
                for a in range(n):
                    remote(k, a, own_src(ins[a]), landed_of(outs[a], k)).wait_recv()
            for cp in sent():
                cp.wait_send()
            for cp in own():
                cp.wait()

        return start, finish

    return plan


def _pairwise_sems(n):
    return [pltpu.SemaphoreType.DMA((NDEV - 1, n)), pltpu.SemaphoreType.DMA((NDEV - 1, n)),
            pltpu.SemaphoreType.DMA((n,))]


def _gather_side(arrs):
    plan = _pairwise_plan(src_of=lambda i, k: i, dst_of=lambda o: o.at[_my_index()],
                          landed_of=lambda o, k: o.at[_peer_index(k)],
                          own_src=lambda i: i, own_dst=lambda o: o.at[_my_index()])
    return dict(arrs=arrs, out_shape=[S((NDEV,) + a.shape, a.dtype) for a in arrs],
                scratch=_pairwise_sems(len(arrs)), plan=plan)


def _exchange_side(arrs):
    plan = _pairwise_plan(src_of=lambda i, k: i.at[_peer_index(k)], dst_of=lambda o: o.at[_my_index()],
                          landed_of=lambda o, k: o.at[_peer_index(k)],
                          own_src=lambda i: i.at[_my_index()], own_dst=lambda o: o.at[_my_index()])
    return dict(arrs=arrs, out_shape=[S(a.shape, a.dtype) for a in arrs], scratch=_pairwise_sems(len(arrs)), plan=plan)


def _comm_call(side, name):
    n = len(side["arrs"])

    def body(*refs):
        start, finish = side["plan"](refs[:n], refs[n:2 * n], *refs[2 * n:])
        start()
        finish()

    hbm = pl.BlockSpec(memory_space=pl.ANY)
    return pl.pallas_call(body, name=name, in_specs=[hbm] * n, out_specs=[hbm] * n, out_shape=side["out_shape"],
                          scratch_shapes=side["scratch"])(*side["arrs"])


def _all_gather_by_chip(arrs, name):
    n = len(arrs)

    def body(*refs):
        ins, outs = refs[:n], refs[n:2 * n]
        send, recv, local = refs[2 * n:]
        x, y, c = lax.axis_index("x"), lax.axis_index("y"), lax.axis_index("c")
        me, sibling = (x, y, c), (x, y, 1 - c)
        chips = [(1 - x, y), (x, 1 - y), (1 - x, 1 - y)]

        def copy(k, a, block, to, src=None):
            px, py, pc = block
            slot = outs[a].at[4 * px + 2 * py + pc]
            return pltpu.make_async_remote_copy(
                src_ref=slot if src is None else src, dst_ref=slot, send_sem=send.at[k, a], recv_sem=recv.at[k, a],
                device_id=to, device_id_type=MESH)

        mine = [pltpu.make_async_copy(ins[a], outs[a].at[4 * x + 2 * y + c], local.at[a]) for a in range(n)]
        first = []
        for a in range(n):
            first.append(copy(0, a, me, sibling, src=ins[a]))
            first += [copy(1 + j, a, me, (*chip, c), src=ins[a]) for j, chip in enumerate(chips)]
        for cp in mine + first:
            cp.start()
        passed = []
        for j, chip in enumerate(chips):
            for a in range(n):
                copy(1 + j, a, (*chip, c), me).wait_recv()
                passed.append(copy(4 + j, a, (*chip, c), sibling))
                passed[-1].start()
        for a in range(n):
            copy(0, a, sibling, me).wait_recv()
        for j, chip in enumerate(chips):
            for a in range(n):
                copy(4 + j, a, (*chip, 1 - c), me).wait_recv()
        for cp in first + passed:
            cp.wait_send()
        for cp in mine:
            cp.wait()

    hbm = pl.BlockSpec(memory_space=pl.ANY)
    return pl.pallas_call(
        body, name=name, in_specs=[hbm] * n, out_specs=[hbm] * n,
        out_shape=[S((NDEV,) + a.shape, a.dtype) for a in arrs],
        scratch_shapes=[pltpu.SemaphoreType.DMA((NDEV - 1, n)), pltpu.SemaphoreType.DMA((NDEV - 1, n)),
                        pltpu.SemaphoreType.DMA((n,))],
    )(*arrs)


NCHIP = 4


def _exchange_sibling(arrs, name):
    n = len(arrs)

    def body(*refs):
        ins, outs = refs[:n], refs[n:2 * n]
        send, recv = refs[2 * n:]
        x, y, c = lax.axis_index("x"), lax.axis_index("y"), lax.axis_index("c")
        copies = []
        for q in range(NCHIP):
            for a in range(n):
                copies.append(pltpu.make_async_remote_copy(
                    src_ref=ins[a].at[2 * q + (1 - c)], dst_ref=outs[a].at[q], send_sem=send.at[q, a],
                    recv_sem=recv.at[q, a], device_id=(x, y, 1 - c), device_id_type=MESH))
        for cp in copies:
            cp.start()
        for cp in copies:
            cp.wait_recv()
        for cp in copies:
            cp.wait_send()

    hbm = pl.BlockSpec(memory_space=pl.ANY)
    return pl.pallas_call(
        body, name=name, in_specs=[hbm] * n, out_specs=[hbm] * n,
        out_shape=[S((NCHIP,) + a.shape[1:], a.dtype) for a in arrs],
        scratch_shapes=[pltpu.SemaphoreType.DMA((NCHIP, n)), pltpu.SemaphoreType.DMA((NCHIP, n))],
    )(*arrs)


def _pair_add(mine, got, core, name):
    _, rows, cols = mine.shape
    tc = 256
    assert cols % tc == 0

    def body(core_ref, a_ref, b_ref, o_ref):
        o_ref[...] = (a_ref[...].astype(F32) + b_ref[...].astype(F32)).astype(BF16)

    return pl.pallas_call(
        body, name=name,
        grid_spec=pltpu.PrefetchScalarGridSpec(
            num_scalar_prefetch=1, grid=(NCHIP, cols // tc),
            in_specs=[pl.BlockSpec((None, rows, tc), lambda q, i, core_ref: (2 * q + core_ref[0], 0, i)),
                      pl.BlockSpec((None, rows, tc), lambda q, i, core_ref: (q, 0, i))],
            out_specs=pl.BlockSpec((None, rows, tc), lambda q, i, core_ref: (q, 0, i))),
        out_shape=S((NCHIP, rows, cols), BF16),
    )(core, mine, got)


def _chips_side(arrs):
    def plan(ins, outs, send, recv, local):
        n = len(ins)

        def places():
            x, y, c = lax.axis_index("x"), lax.axis_index("y"), lax.axis_index("c")
            return 2 * x + y, c, [(1 - x, y), (x, 1 - y), (1 - x, 1 - y)]

        def own():
            here, _, _ = places()
            return [pltpu.make_async_copy(ins[a].at[here], outs[a].at[here], local.at[a]) for a in range(n)]

        def remote(j, a, src_slot, dst_slot):
            _, c, chips = places()
            cx, cy = chips[j]
            return pltpu.make_async_remote_copy(
                src_ref=ins[a].at[src_slot], dst_ref=outs[a].at[dst_slot], send_sem=send.at[j, a],
                recv_sem=recv.at[j, a], device_id=(cx, cy, c), device_id_type=MESH)

        def sent():
            here, _, chips = places()
            return [remote(j, a, 2 * cx + cy, here) for j, (cx, cy) in enumerate(chips) for a in range(n)]

        def start():
            for cp in own() + sent():
                cp.start()

        def finish():
            here, _, chips = places()
            for j, (cx, cy) in enumerate(chips):
                for a in range(n):
                    remote(j, a, here, 2 * cx + cy).wait_recv()
            for cp in sent():
                cp.wait_send()
            for cp in own():
                cp.wait()

        return start, finish

    n = len(arrs)
    return dict(arrs=arrs, out_shape=[S(a.shape, a.dtype) for a in arrs],
                scratch=[pltpu.SemaphoreType.DMA((NCHIP - 1, n)), pltpu.SemaphoreType.DMA((NCHIP - 1, n)),
                         pltpu.SemaphoreType.DMA((n,))], plan=plan)


def _adamw(parts, w, m, v, name, tr, tc=None):
    rows, cols = w.shape
    if tc is None:
        assert rows % tr == 0
        grid, shape, at = (rows // tr,), (tr, cols), (lambda i: (i, 0))
    else:
        assert cols % tc == 0
        grid, shape, at = (cols // tc,), (rows, tc), (lambda i: (0, i))
    c1 = 1.0 - ADAM_B1 ** ADAM_STEP
    c2 = 1.0 - ADAM_B2 ** ADAM_STEP

    nparts = parts.shape[0]

    def body(p_ref, w_ref, m_ref, v_ref, g_ref, d_ref, mo_ref, vo_ref):
        g = p_ref[0].astype(F32)
        for s in range(1, nparts):
            g = g + p_ref[s].astype(F32)
        m_new = ADAM_B1 * m_ref[...] + (1.0 - ADAM_B1) * g
        v_new = ADAM_B2 * v_ref[...] + (1.0 - ADAM_B2) * (g * g)
        g_ref[...] = g
        mo_ref[...] = m_new
        vo_ref[...] = v_new
        d_ref[...] = -ADAM_LR * ((m_new / c1) / (jnp.sqrt(v_new / c2) + ADAM_EPS) + ADAM_WD * w_ref[...])

    blk = pl.BlockSpec(shape, at)
    return pl.pallas_call(
        body, name=name, grid=grid,
        in_specs=[pl.BlockSpec((nparts,) + shape, lambda i: (0,) + at(i)), blk, blk, blk],
        out_specs=[blk] * 4, out_shape=[S((rows, cols), F32)] * 4,
        compiler_params=pltpu.CompilerParams(dimension_semantics=("parallel",)),
    )(parts, w, m, v)


def _adam_math(g, w, m, v):
    c1 = 1.0 - ADAM_B1 ** ADAM_STEP
    c2 = 1.0 - ADAM_B2 ** ADAM_STEP
    m_new = ADAM_B1 * m + (1.0 - ADAM_B1) * g
    v_new = ADAM_B2 * v + (1.0 - ADAM_B2) * (g * g)
    return -ADAM_LR * ((m_new / c1) / (jnp.sqrt(v_new / c2) + ADAM_EPS) + ADAM_WD * w), m_new, v_new


def _adamw_small(parts, params, loss_parts):
    n = len(params)

    def body(*refs):
        p_refs, rest = refs[:n], refs[n + 1:]
        total = refs[n][0]
        for s in range(1, NDEV):
            total = total + refs[n][s]
        refs[-1][...] = total
        for j in range(n):
            w_ref, m_ref, v_ref = rest[3 * j:3 * j + 3]
            g_ref, d_ref, mo_ref, vo_ref = rest[3 * n + 4 * j:3 * n + 4 * j + 4]
            width = w_ref.shape[1]
            g = p_refs[j][0]
            for s in range(1, NDEV):
                g = g + p_refs[j][s]
            g = g[:, :width]
            delta, m_new, v_new = _adam_math(g, w_ref[...], m_ref[...], v_ref[...])
            g_ref[...] = g
            d_ref[...] = delta
            mo_ref[...] = m_new
            vo_ref[...] = v_new

    flat = [a for group in params for a in group]
    return pl.pallas_call(
        body, name="adam_small",
        out_shape=[S(group[0].shape, F32) for group in params for _ in range(4)] + [S((1, 128), F32)],
    )(*parts, loss_parts, *flat)


def _adamw_rows(parts, w, m, v, name, tc=128):
    rows, _, cols = w.shape
    nparts = parts.shape[0]
    c1 = 1.0 - ADAM_B1 ** ADAM_STEP
    c2 = 1.0 - ADAM_B2 ** ADAM_STEP

    def body(p_ref, w_ref, m_ref, v_ref, g_ref, d_ref, mo_ref, vo_ref):
        flat = lambda ref: ref[...].reshape(rows, tc)
        g = p_ref[0].astype(F32)
        for s in range(1, nparts):
            g = g + p_ref[s].astype(F32)
        m_new = ADAM_B1 * flat(m_ref) + (1.0 - ADAM_B1) * g
        v_new = ADAM_B2 * flat(v_ref) + (1.0 - ADAM_B2) * (g * g)
        delta = -ADAM_LR * ((m_new / c1) / (jnp.sqrt(v_new / c2) + ADAM_EPS) + ADAM_WD * flat(w_ref))
        for ref, val in ((g_ref, g), (d_ref, delta), (mo_ref, m_new), (vo_ref, v_new)):
            ref[...] = val.reshape(rows, 1, tc)

    blk = pl.BlockSpec((rows, 1, tc), lambda i: (0, 0, i))
    return pl.pallas_call(
        body, name=name, grid=(cols // tc,),
        in_specs=[pl.BlockSpec((nparts, rows, tc), lambda i: (0, 0, i)), blk, blk, blk],
        out_specs=[blk] * 4, out_shape=[S((rows, 1, cols), F32)] * 4,
        compiler_params=pltpu.CompilerParams(dimension_semantics=("parallel",)),
    )(parts, w, m, v)


def _to_aligned(wt):
    pad = jnp.zeros((GLR_W - GLR_N, wt.shape[1]), wt.dtype)
    return jnp.concatenate([wt[O_QG:O_GLR], wt[O_ZG:O_GA], wt[O_GLR:O_ZG], pad, wt[O_ZA:O_QG], wt[O_GA:O_END],
                            wt[O_QA:O_ZA]], axis=0)


def _from_aligned(wt):
    return jnp.concatenate([wt[C_QA:], wt[C_ZA:C_GA], wt[C_QG:C_ZG], wt[C_GLR:C_GLR + GLR_N], wt[C_ZG:C_GLR],
                            wt[C_GA:C_QA]], axis=0)


def _col_blocks(w, width):
    return w.reshape(w.shape[0], NDEV, width).transpose(1, 0, 2)


def _from_col_blocks(w):
    return w.transpose(1, 0, 2).reshape(w.shape[1], NDEV * w.shape[2])


def _local_step(x2, p2, pos, tgt, norm_g, qk_norm_q, qk_norm_k, gla_gate_b, gla_norm_g, ple_norm_g, w_al,
                weights=None, proj_side=None, unpack=None, dw_side_of=None, dh_side_of=None):
    half = ROT_DIM // 2
    inv8 = jnp.power(jnp.float32(ROPE_THETA), -jnp.arange(half, dtype=F32) * 2.0 / ROT_DIM)
    inv = jnp.tile(jnp.concatenate([inv8, inv8, jnp.zeros((HD - ROT_DIM,), F32)]), 2).reshape(1, 128)
    gq = jnp.tile(qk_norm_q, (1, 2))
    gk = jnp.tile(qk_norm_k, (1, 2))

    proj, h, got = _proj_rms(x2, norm_g, w_al, proj_side)
    if proj_side is not None:
        weights = unpack(got)
    w2p, w_att_f, w_gla_f, w_out_f, w_pg_f, w_ple_f = weights
    qkv = _qk_prep(proj, pos, inv, gq, gk)
    fwd = [_att_fwd(qkv[g], qkv[3 + g], qkv[6 + g], g, f"att_fwd{g}") for g in range(3)]
    att, lse, ain = _att_merge([f[0] for f in fwd], [f[1] for f in fwd], proj)
    o_gla, bin_, states = _gla_fwd(proj, w2p, gla_gate_b, gla_norm_g)
    ya, yb, y, x1 = _branches_fwd(ain, bin_, proj, x2, w_att_f, w_gla_f, w_out_f)
    n2, loss_v, dout, de, du = _ple_loss(x1, p2, tgt, ple_norm_g, w_pg_f, w_ple_f)

    dw_ple = _mm(p2, de, mode="tn", name="dw_ple", tm=PLE, tn=D, tk=512)
    dw_pg = _mm(n2, du, mode="tn", name="dw_pg", tm=D, tn=D, tk=512)
    dx1, dx1b, dy, dg_ple = _ple_bwd(du, x1, dout, ple_norm_g, w_pg_f, w_out_f)
    dw_out = _mm(y, dx1b, mode="tn", name="dw_out", tm=D, tn=D, tk=512)
    dproj, dya, dyb, dain, dbin = _branches_bwd(dy, ya, yb, proj, w_att_f, w_gla_f)
    dw_att = _mm(ain, dya, mode="tn", name="dw_att", tm=512, tn=D, tk=512)
    dw_gla = _mm(bin_, dyb, mode="tn", name="dw_gla", tm=D, tn=D, tk=512)
    dproj, da0, da1, da2, at1, at2, ls1, ls2 = _att_gate_bwd(dain, att, lse, proj, dproj)
    datts, atts, lses = (da0, da1, da2), (att[None], at1, at2), (lse[None], ls1, ls2)
    dproj, dw2, dbg, dgn = _gla_bwd(proj, w2p, gla_gate_b, gla_norm_g, o_gla, states, dbin, dproj)
    bwd = [_att_bwd(qkv[g], qkv[3 + g], qkv[6 + g], datts[g], atts[g], lses[g], g, f"att_bwd{g}") for g in range(3)]
    dproj, dgq, dgk = _qk_bwd(proj, pos, inv, gq, gk, [b[0] for b in bwd], [b[1] for b in bwd],
                              [b[2] for b in bwd], dproj)
    out = dict(loss=loss_v, dw2=dw2, dw_att=dw_att, dw_gla=dw_gla, dw_out=dw_out, dw_pg=dw_pg, dw_ple=dw_ple,
               dgq=dgq, dgk=dgk, dbg=dbg, dgn=dgn, dg_ple=dg_ple)
    if dw_side_of is None:
        dw_al = _mm(dproj, h, mode="tn", name="dw_in", tm=1536, tn=D, tk=2048, out_dtype=BF16)
    else:
        dw_al, out["dw_side"] = _mm(dproj, h, mode="tn", name="dw_in", tm=1536, tn=D, tk=2048, out_dtype=BF16,
                                    side=dw_side_of(out))
    grad_x, dg_norm, out["dh_side"] = _dh_rms(dproj, w_al, x2, norm_g, dx1,
                                              None if dh_side_of is None else dh_side_of(dw_al))
    out.update(grad_x=grad_x, dw_al=dw_al, dg_norm=dg_norm)
    return out


def kernel(x, p, positions, norm_g, w_in, qk_norm_q, qk_norm_k, gla_gate_w2, gla_gate_b, gla_norm_g, w_att_proj, w_gla_proj, w_out, ple_norm_g, w_ple_gate, w_ple, loss_target, m_norm_g, m_w_in, m_qk_norm_q, m_qk_norm_k, m_gla_gate_w2, m_gla_gate_b, m_gla_norm_g, m_w_att_proj, m_w_gla_proj, m_w_out, m_ple_norm_g, m_w_ple_gate, m_w_ple, v_norm_g, v_w_in, v_qk_norm_q, v_qk_norm_k, v_gla_gate_w2, v_gla_gate_b, v_gla_norm_g, v_w_att_proj, v_w_gla_proj, v_w_out, v_ple_norm_g, v_w_ple_gate, v_w_ple):
    x2, p2, tgt = x[0], p[0, 0], loss_target[0]
    pos = positions.astype(F32).reshape(T, 1)

    rows3 = jnp.stack([w_gla_proj[0], w_out[0], w_ple_gate[0]]).astype(BF16)
    cols3 = jnp.concatenate([w_att_proj[0], w_ple[0], jnp.pad(gla_gate_w2[0], ((0, 0), (0, 64)))], axis=0).astype(BF16)
    (g_in,) = _all_gather_by_chip([w_in[0].T.astype(BF16)], "gather_w_in")
    w_al = _to_aligned(g_in.reshape(W_IN_COLS, D))

    def unpack(got):
        g_rows, g_cols = got
        w2_f = _from_col_blocks(g_cols[:, 768:784, :64])
        return (jnp.pad(w2_f, ((0, GLR_W - GLR_N), (0, 0))), _from_col_blocks(g_cols[:, :512]),
                g_rows[:, 0].reshape(D, D), g_rows[:, 1].reshape(D, D), g_rows[:, 2].reshape(D, D),
                _from_col_blocks(g_cols[:, 512:768]))

    def dw_side_of(g):
        s_rows = jnp.concatenate([g[k].reshape(NDEV, 128, D) for k in ("dw_gla", "dw_out", "dw_pg")], axis=1)
        s_cols = jnp.concatenate([_col_blocks(g["dw_att"], 128), _col_blocks(g["dw_ple"], 128),
                                  jnp.pad(_col_blocks(g["dw2"][:GLR_N], 64), ((0, 0), (0, 0), (0, 64)))], axis=1)
        return _exchange_side([s_rows.astype(BF16), s_cols.astype(BF16)])

    def dh_side_of(dw_al):
        s_in = _from_aligned(dw_al).astype(BF16).reshape(NDEV, W_IN_SHARD, D)
        (from_sibling,) = _exchange_sibling([s_in], "exchange_sibling")
        core = lax.axis_index("c").astype(jnp.int32).reshape(1)
        return _chips_side([_pair_add(s_in, from_sibling, core, "pair_add")])

    loc = _local_step(x2, p2, pos, tgt, norm_g, qk_norm_q, qk_norm_k, gla_gate_b, gla_norm_g, ple_norm_g, w_al,
                      proj_side=_gather_side([rows3, cols3]), unpack=unpack, dw_side_of=dw_side_of,
                      dh_side_of=dh_side_of)
    loss_v, grad_x = loc["loss"], loc["grad_x"]
    dg_norm, dgq, dgk, dbg, dgn, dg_ple = (loc[k] for k in ("dg_norm", "dgq", "dgk", "dbg", "dgn", "dg_ple"))
    r_rows, r_cols = loc["dw_side"]
    (r_in,) = loc["dh_side"]

    r_small = _comm_call(_gather_side([dg_norm, dgq, dgk, dbg, dgn, dg_ple, loss_v]), "gather_small")

    outs = {}

    def adam(nm, parts, w, m, v, tr):
        outs[nm] = _adamw(parts, w, m, v, "adam_" + nm, tr)

    rows_of = lambda a: jnp.transpose(a, (2, 0, 1))
    outs["w_in"] = [jnp.transpose(o, (1, 2, 0))[0] for o in
                    _adamw_rows(r_in, rows_of(w_in), rows_of(m_w_in), rows_of(v_w_in), "adam_w_in")]
    adam("w_gla_proj", r_rows[:, :128], w_gla_proj[0], m_w_gla_proj[0], v_w_gla_proj[0], 128)
    adam("w_out", r_rows[:, 128:256], w_out[0], m_w_out[0], v_w_out[0], 128)
    adam("w_ple_gate", r_rows[:, 256:], w_ple_gate[0], m_w_ple_gate[0], v_w_ple_gate[0], 128)
    adam("w_att_proj", r_cols[:, :512], w_att_proj[0], m_w_att_proj[0], v_w_att_proj[0], 512)
    adam("w_ple", r_cols[:, 512:768], w_ple[0], m_w_ple[0], v_w_ple[0], 256)
    adam("gla_gate_w2", r_cols[:, 768:784, :64], gla_gate_w2[0], m_gla_gate_w2[0], v_gla_gate_w2[0], 16)
    small = ((norm_g, m_norm_g, v_norm_g), (qk_norm_q, m_qk_norm_q, v_qk_norm_q), (qk_norm_k, m_qk_norm_k, v_qk_norm_k),
             (gla_gate_b, m_gla_gate_b, v_gla_gate_b), (gla_norm_g, m_gla_norm_g, v_gla_norm_g),
             (ple_norm_g, m_ple_norm_g, v_ple_norm_g))
    sm = _adamw_small(r_small[:6], small, r_small[6])
    for j, nm in enumerate(("norm_g", "qk_norm_q", "qk_norm_k", "gla_gate_b", "gla_norm_g", "ple_norm_g")):
        outs[nm] = [o[0] for o in sm[4 * j:4 * j + 4]]

    loss = sm[-1][0, 0]
    order = ["norm_g", "w_in", "qk_norm_q", "qk_norm_k", "gla_gate_w2", "gla_gate_b", "gla_norm_g", "w_att_proj",
             "w_gla_proj", "w_out", "ple_norm_g", "w_ple_gate", "w_ple"]
    result = [loss, grad_x[None]]
    for i in range(4):
        result += [outs[nm][i][None] for nm in order]
    return tuple(result)
```

```python
import functools

import jax
import jax.numpy as jnp
from jax import lax
from jax.experimental import pallas as pl
from jax.experimental.pallas import tpu as pltpu

F32 = jnp.float32
BF16 = jnp.bfloat16
S = jax.ShapeDtypeStruct

T = 4096
D = 1024
NDEV = 8
HD = 64
ATT_W = 512
ATT_QKV = 1536
DILATIONS = (1, 4, 16)
BLK = 128
GH, GDK, GDV = 4, 128, 256
GLA_C = 128
PLE = 256
EPS = 1e-6
ROT_DIM = 16
ROPE_THETA = 500000.0
GLA_TAU = 16.0
W_IN_COLS = 10256
W_IN_SHARD = 1282

C_QG, C_KG, C_VG, C_ZG, C_GLR, C_ZA, C_GA, C_GB, C_QA, C_KA, C_VA = (
    0, 512, 1024, 2048, 3072, 3584, 4096, 5120, 6144, 7680, 9216)
GLA_GROUP_W = 3584
GLR_W = 512
NCOL = 10752
GLR_N = 16
O_QA, O_ZA, O_QG, O_GLR, O_ZG, O_GA, O_END = 0, 4608, 5120, 7168, 7184, 8208, 10256

ADAM_LR, ADAM_B1, ADAM_B2, ADAM_EPS, ADAM_WD, ADAM_STEP = 0.001, 0.9, 0.999, 1e-08, 0.01, 10

MESH = pl.DeviceIdType.MESH


def _sigmoid(z):
    return 1.0 / (1.0 + jnp.exp(-z))


def _dot(a, b, dims):
    return lax.dot_general(a, b, (dims, ((), ())), preferred_element_type=F32)


def _nn(a, b):
    return _dot(a, b, ((1,), (0,)))


def _nt(a, b):
    return _dot(a, b, ((1,), (1,)))


def _tn(a, b):
    return _dot(a, b, ((0,), (0,)))


def _mm(a, b, *, mode, name, tm, tn, tk, out_dtype=F32, res=None, side=None):
    if mode == "nn":
        (m, k), n = a.shape, b.shape[1]
        a_spec = pl.BlockSpec((tm, tk), lambda i, j, l: (i, l))
        b_spec = pl.BlockSpec((tk, tn), lambda i, j, l: (l, j))
        dot = _nn
    elif mode == "nt":
        (m, k), n = a.shape, b.shape[0]
        a_spec = pl.BlockSpec((tm, tk), lambda i, j, l: (i, l))
        b_spec = pl.BlockSpec((tn, tk), lambda i, j, l: (j, l))
        dot = _nt
    else:
        (k, m), n = a.shape, b.shape[1]
        a_spec = pl.BlockSpec((tk, tm), lambda i, j, l: (l, i))
        b_spec = pl.BlockSpec((tk, tn), lambda i, j, l: (l, j))
        dot = _tn
    assert m % tm == 0 and n % tn == 0 and k % tk == 0, (name, m, n, k)
    grid = (m // tm, n // tn, k // tk)
    nk = grid[2]
    o_spec = pl.BlockSpec((tm, tn), lambda i, j, l: (i, j))
    in_specs = [a_spec, b_spec]
    args = [a, b]
    if res is not None:
        in_specs.append(o_spec)
        args.append(res)
    n_in = len(args)
    n_side = 0 if side is None else len(side["arrs"])
    hbm = pl.BlockSpec(memory_space=pl.ANY)

    def body(*refs):
        a_ref, b_ref = refs[0], refs[1]
        r_ref = refs[2] if res is not None else None
        o_ref = refs[n_in + n_side]
        scratch = refs[n_in + 2 * n_side + 1:]
        if side is not None:
            start, finish_side = side["plan"](refs[n_in:n_in + n_side], refs[n_in + n_side + 1:n_in + 2 * n_side + 1],
                                              *scratch[1 if nk > 1 else 0:])
            ids = [pl.program_id(d) for d in range(3)]

            @pl.when((ids[0] == 0) & (ids[1] == 0) & (ids[2] == 0))
            def _():
                start()

        part = dot(a_ref[...].astype(BF16), b_ref[...].astype(BF16))

        def finish(val):
            if r_ref is not None:
                val = val + r_ref[...]
            o_ref[...] = val.astype(out_dtype)

        if nk == 1:
            finish(part)
        else:
            acc = scratch[0]
            l = pl.program_id(2)

            @pl.when(l == 0)
            def _():
                acc[...] = part

            @pl.when(l > 0)
            def _():
                acc[...] += part

            @pl.when(l == nk - 1)
            def _():
                finish(acc[...])

        if side is not None:
            @pl.when((ids[0] == grid[0] - 1) & (ids[1] == grid[1] - 1) & (ids[2] == grid[2] - 1))
            def _():
                finish_side()

    sems = [] if side is None else side["scratch"]
    outs = pl.pallas_call(
        body, name=name, grid=grid,
        in_specs=in_specs + [hbm] * n_side, out_specs=[o_spec] + [hbm] * n_side,
        out_shape=[S((m, n), out_dtype)] + ([] if side is None else side["out_shape"]),
        scratch_shapes=([pltpu.VMEM((tm, tn), F32)] if nk > 1 else []) + sems,
        compiler_params=pltpu.CompilerParams(
            dimension_semantics=("arbitrary",) * 3 if side is not None else ("parallel", "parallel", "arbitrary")),
    )(*args, *([] if side is None else side["arrs"]))
    return outs[0] if side is None else (outs[0], outs[1:])


def _side_parts(side, refs, n_in, n_out):
    n_side = 0 if side is None else len(side["arrs"])
    scratch = refs[n_in + n_out + 2 * n_side:]
    if side is None:
        return (lambda: None), (lambda: None), scratch
    start, finish = side["plan"](refs[n_in:n_in + n_side], refs[n_in + n_side + n_out:n_in + n_out + 2 * n_side],
                                 *scratch[len(scratch) - len(side["scratch"]):])
    return start, finish, scratch


def _proj_rms(x, g, wt, side=None):
    tm, tn = 1024, 1536
    grid = (T // tm, NCOL // tn)
    n_side = 0 if side is None else len(side["arrs"])
    hbm = pl.BlockSpec(memory_space=pl.ANY)

    def body(*refs):
        x_ref, g_ref, w_ref = refs[:3]
        o_ref, h_ref = refs[3 + n_side], refs[4 + n_side]
        start, finish, _ = _side_parts(side, refs, 3, 2)
        i, j = pl.program_id(0), pl.program_id(1)

        @pl.when((i == 0) & (j == 0))
        def _():
            start()

        @pl.when(j == 0)
        def _():
            xf = x_ref[...]
            r = lax.rsqrt(jnp.mean(xf * xf, axis=-1, keepdims=True) + EPS)
            h_ref[...] = (xf * r * g_ref[...]).astype(BF16)

        o_ref[...] = _nt(h_ref[...], w_ref[...])

        @pl.when((i == grid[0] - 1) & (j == grid[1] - 1))
        def _():
            finish()

    outs = pl.pallas_call(
        body, name="proj", grid=grid,
        in_specs=[pl.BlockSpec((tm, D), lambda i, j: (i, 0)), pl.BlockSpec((1, D), lambda i, j: (0, 0)),
                  pl.BlockSpec((tn, D), lambda i, j: (j, 0))] + [hbm] * n_side,
        out_specs=[pl.BlockSpec((tm, tn), lambda i, j: (i, j)), pl.BlockSpec((tm, D), lambda i, j: (i, 0))] + [hbm] * n_side,
        out_shape=[S((T, NCOL), F32), S((T, D), BF16)] + ([] if side is None else side["out_shape"]),
        scratch_shapes=[] if side is None else side["scratch"],
        compiler_params=pltpu.CompilerParams(dimension_semantics=("arbitrary", "arbitrary")),
    )(x, g, wt, *([] if side is None else side["arrs"]))
    return outs[0], outs[1], outs[2:]


def _dh_rms(dproj, wt, x, g, skip, side=None):
    tm, tk = 1024, 1792
    grid = (T // tm, NCOL // tk)
    n_side = 0 if side is None else len(side["arrs"])
    hbm = pl.BlockSpec(memory_space=pl.ANY)

    def body(*refs):
        a_ref, w_ref, x_ref, g_ref, s_ref = refs[:5]
        dx_ref, dg_ref = refs[5 + n_side], refs[6 + n_side]
        start, finish, scratch = _side_parts(side, refs, 5, 2)
        acc = scratch[0]
        i, l = pl.program_id(0), pl.program_id(1)

        @pl.when((i == 0) & (l == 0))
        def _():
            start()

        part = _nn(a_ref[...], w_ref[...])

        @pl.when(l == 0)
        def _():
            acc[...] = part

        @pl.when(l > 0)
        def _():
            acc[...] += part

        @pl.when(l == grid[1] - 1)
        def _():
            xf = x_ref[...]
            r = lax.rsqrt(jnp.mean(xf * xf, axis=-1, keepdims=True) + EPS)
            dn = acc[...]
            u = dn * g_ref[...]
            dx_ref[...] = s_ref[...] + r * u - xf * (r * r * r) * jnp.mean(u * xf, axis=-1, keepdims=True)
            dg = jnp.sum(dn * xf * r, axis=0, keepdims=True)

            @pl.when(i == 0)
            def _():
                dg_ref[...] = dg

            @pl.when(i > 0)
            def _():
                dg_ref[...] += dg

        @pl.when((i == grid[0] - 1) & (l == grid[1] - 1))
        def _():
            finish()

    tok = pl.BlockSpec((tm, D), lambda i, l: (i, 0))
    outs = pl.pallas_call(
        body, name="dh", grid=grid,
        in_specs=[pl.BlockSpec((tm, tk), lambda i, l: (i, l)), pl.BlockSpec((tk, D), lambda i, l: (l, 0)), tok,
                  pl.BlockSpec((1, D), lambda i, l: (0, 0)), tok] + [hbm] * n_side,
        out_specs=[tok, pl.BlockSpec((1, D), lambda i, l: (0, 0))] + [hbm] * n_side,
        out_shape=[S((T, D), F32), S((1, D), F32)] + ([] if side is None else side["out_shape"]),
        scratch_shapes=[pltpu.VMEM((tm, D), F32)] + ([] if side is None else side["scratch"]),
        compiler_params=pltpu.CompilerParams(dimension_semantics=("arbitrary", "arbitrary")),
    )(dproj, wt, x, g, skip, *([] if side is None else side["arrs"]))
    return outs[0], outs[1], outs[2:]


def _rot_tables(pos_ref, inv_ref):
    lane = lax.broadcasted_iota(jnp.int32, (1, 128), 1) % HD
    ang = pos_ref[...] * inv_ref[...]
    cos, sin = jnp.cos(ang), jnp.sin(ang)
    c = jnp.where(lane < ROT_DIM, cos, 1.0)
    sp = jnp.where((lane >= ROT_DIM // 2) & (lane < ROT_DIM), sin, 0.0)
    sm = jnp.where(lane < ROT_DIM // 2, -sin, 0.0)
    return c, sp, sm


def _head_sums(v):
    same = (lax.broadcasted_iota(jnp.int32, (128, 128), 0) < HD) == (lax.broadcasted_iota(jnp.int32, (128, 128), 1) < HD)
    ones = jnp.where(same, 1.0, 0.0).astype(BF16)
    hi = v.astype(BF16)
    lo = (v - hi.astype(F32)).astype(BF16)
    return _nn(hi, ones) + _nn(lo, ones)


def _pair_norm(t):
    return lax.rsqrt(_head_sums(t * t) * (1.0 / HD) + EPS)


def _pair_mean(t):
    return _head_sums(t) * (1.0 / HD)


TT = 256
NCH = ATT_QKV // 128


def _res_shape(grp, dtype):
    return S((DILATIONS[grp], T // DILATIONS[grp], ATT_W), dtype)


def _res_spec(grp):
    dil = DILATIONS[grp]
    return pl.BlockSpec((dil, TT // dil, ATT_W), lambda i: (0, i, 0))


def _to_residues(sc, j, dst_ref, dil, cols):
    n = TT // dil
    for r in range(dil):
        rows = sc[j] if dil == 1 else sc.at[j][pl.ds(r, n, stride=dil), :]
        dst_ref[r, :, cols] = rows.astype(dst_ref.dtype)


def _from_residues(src_ref, cols, sc, j, dil):
    n = TT // dil
    for r in range(dil):
        if dil == 1:
            sc[j] = src_ref[r, :, cols]
        else:
            sc.at[j][pl.ds(r, n, stride=dil), :] = src_ref[r, :, cols]


def _tok_spec(width, cblk=0):
    return pl.BlockSpec((TT, width), functools.partial(lambda i, c: (i, c), c=cblk))


def _const_spec(arr_or_shape):
    shape = arr_or_shape if isinstance(arr_or_shape, tuple) else arr_or_shape.shape
    return pl.BlockSpec(shape, functools.partial(lambda i, nd: (0,) * nd, nd=len(shape)))


def _qk_prep(proj, pos, inv, gq, gk):
    def body(q_ref, k_ref, v_ref, pos_ref, inv_ref, gq_ref, gk_ref, *rest):
        outs, sc = rest[:9], rest[9]
        c, sp, sm = _rot_tables(pos_ref, inv_ref)
        for which, (src, g_ref) in enumerate(((q_ref, gq_ref), (k_ref, gk_ref), (v_ref, None))):
            if g_ref is not None:
                g = jnp.broadcast_to(g_ref[...] * ((HD ** -0.5) if which == 0 else 1.0), c.shape)
                cg, spg, smg = c * g, sp * pltpu.roll(g, 8, 1), sm * pltpu.roll(g, 120, 1)
            for j in range(NCH):
                t = src[:, j * 128:(j + 1) * 128]
                if g_ref is not None:
                    t = _pair_norm(t) * (t * cg + pltpu.roll(t, 8, 1) * spg + pltpu.roll(t, 120, 1) * smg)
                sc[j] = t
            for j in range(NCH):
                grp, sub = divmod(j * 128, ATT_W)
                _to_residues(sc, j, outs[which * 3 + grp], DILATIONS[grp], slice(sub, sub + 128))

    return pl.pallas_call(
        body, name="qk_prep", grid=(T // TT,),
        in_specs=[_tok_spec(ATT_QKV, C_QA // ATT_QKV), _tok_spec(ATT_QKV, C_KA // ATT_QKV),
                  _tok_spec(ATT_QKV, C_VA // ATT_QKV), _tok_spec(1), _const_spec(inv), _const_spec(gq), _const_spec(gk)],
        out_specs=[_res_spec(g) for _ in range(3) for g in range(3)],
        out_shape=[_res_shape(g, BF16) for _ in range(3) for g in range(3)],
        scratch_shapes=[pltpu.VMEM((NCH, TT, 128), F32)],
        compiler_params=pltpu.CompilerParams(dimension_semantics=("arbitrary",)),
    )(proj, proj, proj, pos, inv, gq, gk)


def _qk_bwd(proj, pos, inv, gq, gk, dqs, dks, dvs, dproj):
    const = lambda a: pl.BlockSpec(a.shape, functools.partial(lambda i, p, nd: (0,) * nd, nd=a.ndim))
    res = lambda g: pl.BlockSpec((DILATIONS[g], TT // DILATIONS[g], ATT_W), lambda i, p: (0, i, 0))
    base = C_QA // ATT_QKV

    def body(t_ref, pos_ref, inv_ref, gq_ref, gk_ref, dq0, dq1, dq2, dk0, dk1, dk2, dv0, dv1, dv2, buf_ref,
             out_ref, dgq_ref, dgk_ref, sc):
        del buf_ref
        part = pl.program_id(1)
        first = pl.program_id(0) == 0

        def gather(drefs):
            for j in range(NCH):
                grp, sub = divmod(j * 128, ATT_W)
                _from_residues(drefs[grp], slice(sub, sub + 128), sc, j, DILATIONS[grp])

        def normed(g_ref, drefs, dg_ref):
            c, sp, sm = _rot_tables(pos_ref, inv_ref)
            gather(drefs)
            dg = jnp.zeros((1, 128), F32)
            for j in range(NCH):
                cols = slice(j * 128, (j + 1) * 128)
                d_rot = sc[j]
                dn = d_rot * c + pltpu.roll(d_rot * sp, 120, 1) + pltpu.roll(d_rot * sm, 8, 1)
                t = t_ref[:, cols]
                r = _pair_norm(t)
                u = dn * g_ref[...]
                out_ref[:, cols] = (r * u - t * (r * r * r) * _pair_mean(u * t)).astype(BF16)
                dg = dg + jnp.sum(dn * t * r, axis=0, keepdims=True)
            dg = dg + pltpu.roll(dg, HD, 1)

            @pl.when(first)
            def _():
                dg_ref[...] = dg

            @pl.when(jnp.logical_not(first))
            def _():
                dg_ref[...] += dg

        @pl.when(part == 0)
        def _():
            gather((dv0, dv1, dv2))
            for j in range(NCH):
                out_ref[:, j * 128:(j + 1) * 128] = sc[j].astype(BF16)

        @pl.when(part == 1)
        def _():
            normed(gq_ref, (dq0, dq1, dq2), dgq_ref)

        @pl.when(part == 2)
        def _():
            normed(gk_ref, (dk0, dk1, dk2), dgk_ref)

    keep = pl.BlockSpec((1, 128), lambda i, p: (0, 0))
    return pl.pallas_call(
        body, name="qk_bwd", grid=(T // TT, 3),
        in_specs=[pl.BlockSpec((TT, ATT_QKV), lambda i, p: (i, base + jnp.maximum(p - 1, 0))),
                  pl.BlockSpec((TT, 1), lambda i, p: (i, 0)), const(inv), const(gq), const(gk)]
        + [res(g) for _ in range(3) for g in range(3)] + [pl.BlockSpec(memory_space=pl.ANY)],
        out_specs=[pl.BlockSpec((TT, ATT_QKV), lambda i, p: (i, base + jnp.where(p == 0, 2, p - 1))), keep, keep],
        out_shape=[S(dproj.shape, dproj.dtype), S((1, 128), F32), S((1, 128), F32)],
        input_output_aliases={14: 0},
        scratch_shapes=[pltpu.VMEM((NCH, TT, 128), F32)],
        compiler_params=pltpu.CompilerParams(dimension_semantics=("arbitrary", "arbitrary")),
    )(proj, pos, inv, gq, gk, *dqs, *dks, *dvs, dproj)


def _split_heads(t):
    low = lax.broadcasted_iota(jnp.int32, (1, 128), 1) < HD
    zero = jnp.zeros_like(t)
    return jnp.concatenate([jnp.where(low, t, zero), jnp.where(low, zero, t)], axis=0)


def _join_heads(t2):
    low = lax.broadcasted_iota(jnp.int32, (1, 128), 1) < HD
    n = t2.shape[0] // 2
    return jnp.where(low, t2[:n], t2[n:])


def _band_mask4(has_before, has_own):
    row = lax.broadcasted_iota(jnp.int32, (BLK, 4 * BLK), 0)
    lane = lax.broadcasted_iota(jnp.int32, (BLK, 4 * BLK), 1)
    key = lane & (BLK - 1)
    own = lane >= 2 * BLK
    return (own & (key <= row) & has_own) | (jnp.logical_not(own) & (key >= row) & has_before)


def _band_mask_before(has_before):
    row = lax.broadcasted_iota(jnp.int32, (BLK, 2 * BLK), 0)
    key = lax.broadcasted_iota(jnp.int32, (BLK, 2 * BLK), 1) & (BLK - 1)
    return (key >= row) & has_before


def _per_head(width, col_a, col_b):
    lane = lax.broadcasted_iota(jnp.int32, (1, width), 1)
    return jnp.where((lane & BLK) == 0, col_a, col_b)


NQ = ATT_W // 128


def _att_fwd(q, k, v, grp, name):
    dil = DILATIONS[grp]
    nb = T // dil // BLK

    def body(q_ref, kp_ref, kc_ref, vp_ref, vc_ref, o_ref, lse_ref, s_sc, p_sc):
        mask = _band_mask4(pl.program_id(1) > 0, True)
        low = lax.broadcasted_iota(jnp.int32, (1, 128), 1) < HD
        halves = lambda ref, j, h: (ref[j, :, h * BLK:(h + 1) * BLK], ref[j, :, (h + 2) * BLK:(h + 3) * BLK])
        for j in range(NQ):
            cols = slice(j * 128, (j + 1) * 128)
            k4 = jnp.concatenate([_split_heads(kp_ref[:, cols]), _split_heads(kc_ref[:, cols])], axis=0)
            s_sc[j] = jnp.where(mask, _nt(q_ref[:, cols], k4), -jnp.inf)
        mxs = [[jnp.maximum(*(jnp.max(t, axis=-1, keepdims=True) for t in halves(s_sc, j, h))) for h in range(2)]
               for j in range(NQ)]
        dens = []
        for j in range(NQ):
            p = jnp.exp(s_sc[j] - _per_head(4 * BLK, *mxs[j]))
            p_sc[j] = p.astype(BF16)
            dens.append([jnp.sum(p[:, h * BLK:(h + 1) * BLK], axis=-1, keepdims=True)
                         + jnp.sum(p[:, (h + 2) * BLK:(h + 3) * BLK], axis=-1, keepdims=True) for h in range(2)])
        for j in range(NQ):
            cols = slice(j * 128, (j + 1) * 128)
            v4 = jnp.concatenate([_split_heads(vp_ref[:, cols]), _split_heads(vc_ref[:, cols])], axis=0)
            o_ref[:, cols] = _nn(p_sc[j], v4) / jnp.where(low, dens[j][0], dens[j][1])
            lse_ref[:, cols] = jnp.where(low, mxs[j][0] + jnp.log(dens[j][0]), mxs[j][1] + jnp.log(dens[j][1]))

    cur = pl.BlockSpec((None, BLK, ATT_W), lambda r, i: (r, i, 0))
    prev = pl.BlockSpec((None, BLK, ATT_W), lambda r, i: (r, jnp.maximum(i - 1, 0), 0))
    return pl.pallas_call(
        body, name=name, grid=(dil, nb),
        in_specs=[cur, prev, cur, prev, cur],
        out_specs=[cur, cur], out_shape=[_res_shape(grp, F32)] * 2,
        scratch_shapes=[pltpu.VMEM((NQ, BLK, 4 * BLK), F32), pltpu.VMEM((NQ, BLK, 4 * BLK), BF16)],
        compiler_params=pltpu.CompilerParams(dimension_semantics=("parallel", "arbitrary")),
    )(q, k, k, v, v)


def _att_bwd(q, k, v, datt, att, lse, grp, name):
    dil = DILATIONS[grp]
    nb = T // dil // BLK
    scale = HD ** -0.5

    def body(q0_ref, q1_ref, kp_ref, kc_ref, vp_ref, vc_ref, do0_ref, do1_ref, o0_ref, o1_ref, l0_ref, l1_ref,
             dq_ref, dk_ref, dv_ref, k4_sc, v4_sc, s0_sc, s1_sc, dp0_sc, dp1_sc, p_sc, ds_sc):
        i = pl.program_id(1)
        mask_mine = _band_mask4(i > 0, True)
        mask_next = _band_mask_before(i < nb - 1)
        low = lax.broadcasted_iota(jnp.int32, (1, 128), 1) < HD
        for j in range(NQ):
            cols = slice(j * 128, (j + 1) * 128)
            k4_sc[j, :2 * BLK] = _split_heads(kp_ref[:, cols])
            k4_sc[j, 2 * BLK:] = _split_heads(kc_ref[:, cols])
            v4_sc[j, :2 * BLK] = _split_heads(vp_ref[:, cols])
            v4_sc[j, 2 * BLK:] = _split_heads(vc_ref[:, cols])
        for j in range(NQ):
            cols = slice(j * 128, (j + 1) * 128)
            s0_sc[j] = _nt(q0_ref[:, cols], k4_sc[j])
            s1_sc[j] = _nt(q1_ref[:, cols], k4_sc[j, 2 * BLK:])
            dp0_sc[j] = _nt(do0_ref[:, cols].astype(BF16), v4_sc[j])
            dp1_sc[j] = _nt(do1_ref[:, cols].astype(BF16), v4_sc[j, 2 * BLK:])
        stats = []
        for j in range(NQ):
            cols = slice(j * 128, (j + 1) * 128)
            for do_ref, o_ref, l_ref in ((do0_ref, o0_ref, l0_ref), (do1_ref, o1_ref, l1_ref)):
                prod = do_ref[:, cols].astype(F32) * o_ref[:, cols].astype(F32)
                d_all = jnp.sum(prod, axis=-1, keepdims=True)
                d_low = jnp.sum(jnp.where(low, prod, 0.0), axis=-1, keepdims=True)
                lse_t = l_ref[:, cols]
                stats.append((d_low, d_all - d_low, lse_t[:, 0:1], lse_t[:, HD:HD + 1]))
        for j in range(NQ):
            (da, db, la, lb), (da1, db1, la1, lb1) = stats[2 * j], stats[2 * j + 1]
            p0 = jnp.where(mask_mine, jnp.exp(s0_sc[j] - _per_head(4 * BLK, la, lb)), 0.0)
            ds0 = p0 * (dp0_sc[j] - _per_head(4 * BLK, da, db))
            p1 = jnp.where(mask_next, jnp.exp(s1_sc[j] - _per_head(2 * BLK, la1, lb1)), 0.0)
            ds1 = p1 * (dp1_sc[j] - _per_head(2 * BLK, da1, db1))
            p_sc[j, :BLK] = p0.astype(BF16)
            ds_sc[j, :BLK] = ds0.astype(BF16)
            p_sc[j, BLK:, 2 * BLK:] = p1.astype(BF16)
            ds_sc[j, BLK:, 2 * BLK:] = ds1.astype(BF16)
        for j in range(NQ):
            cols = slice(j * 128, (j + 1) * 128)
            dq_ref[:, cols] = _nn(ds_sc[j, :BLK], k4_sc[j]) * scale
            qq = jnp.concatenate([q0_ref[:, cols], q1_ref[:, cols]], axis=0)
            dd = jnp.concatenate([do0_ref[:, cols], do1_ref[:, cols]], axis=0).astype(BF16)
            dk_ref[:, cols] = _join_heads(_tn(ds_sc[j, :, 2 * BLK:], qq))
            dv_ref[:, cols] = _join_heads(_tn(p_sc[j, :, 2 * BLK:], dd))

    def spec(shift):
        return pl.BlockSpec((None, BLK, ATT_W), lambda r, i: (r, jnp.clip(i + shift, 0, nb - 1), 0))

    here, after, before = spec(0), spec(1), spec(-1)
    vm = pltpu.VMEM
    return pl.pallas_call(
        body, name=name, grid=(dil, nb),
        in_specs=[here, after, before, here, before, here, here, after, here, after, here, after],
        out_specs=[here] * 3, out_shape=[_res_shape(grp, F32)] * 3,
        scratch_shapes=[vm((NQ, 4 * BLK, 128), BF16), vm((NQ, 4 * BLK, 128), BF16), vm((NQ, BLK, 4 * BLK), F32),
                        vm((NQ, BLK, 2 * BLK), F32), vm((NQ, BLK, 4 * BLK), F32), vm((NQ, BLK, 2 * BLK), F32),
                        vm((NQ, 2 * BLK, 4 * BLK), BF16), vm((NQ, 2 * BLK, 4 * BLK), BF16)],
        compiler_params=pltpu.CompilerParams(dimension_semantics=("parallel", "arbitrary")),
    )(q, q, k, k, v, v, datt, datt, att, att, lse, lse)


def _att_merge(os_, lses, proj):
    nq = ATT_W // 128

    def body(o0, o1, o2, l0, l1, l2, za_ref, att_ref, lse_ref, ain_ref, sc):
        for a, ref in enumerate((o0, o1, o2, l0, l1, l2)):
            for j in range(nq):
                _from_residues(ref, slice(j * 128, (j + 1) * 128), sc, a * nq + j, DILATIONS[a % 3])
        for j in range(nq):
            cols = slice(j * 128, (j + 1) * 128)
            oa, ob, oc = (sc[a * nq + j] for a in range(3))
            la, lb, lc = (sc[(3 + a) * nq + j] for a in range(3))
            m = jnp.maximum(jnp.maximum(la, lb), lc)
            wa, wb, wc = jnp.exp(la - m), jnp.exp(lb - m), jnp.exp(lc - m)
            tot = wa + wb + wc
            att = (wa * oa + wb * ob + wc * oc) / tot
            att_ref[:, cols] = att
            lse_ref[:, cols] = m + jnp.log(tot)
            za = za_ref[:, cols]
            ain_ref[:, cols] = (att * za * _sigmoid(za)).astype(BF16)

    return pl.pallas_call(
        body, name="att_merge", grid=(T // TT,),
        in_specs=[_res_spec(g) for _ in range(2) for g in range(3)] + [_tok_spec(ATT_W, C_ZA // ATT_W)],
        out_specs=[_tok_spec(ATT_W)] * 3,
        out_shape=[S((T, ATT_W), F32), S((T, ATT_W), F32), S((T, ATT_W), BF16)],
        scratch_shapes=[pltpu.VMEM((6 * nq, TT, 128), F32)],
        compiler_params=pltpu.CompilerParams(dimension_semantics=("arbitrary",)),
    )(*os_, *lses, proj)


def _att_gate_bwd(dain, att, lse, proj, dproj):
    nq = ATT_W // 128

    def body(d_ref, att_ref, lse_ref, za_ref, buf_ref, dza_ref, da0, da1, da2, at1, at2, ls1, ls2, sc):
        del buf_ref
        for j in range(nq):
            cols = slice(j * 128, (j + 1) * 128)
            za = za_ref[:, cols]
            sg = _sigmoid(za)
            d = d_ref[:, cols].astype(F32)
            att_ = att_ref[:, cols]
            dza_ref[:, cols] = (d * att_ * sg * (1.0 + za * (1.0 - sg))).astype(BF16)
            sc[j] = d * za * sg
            sc[nq + j] = att_
            sc[2 * nq + j] = lse_ref[:, cols]
        for j in range(nq):
            cols = slice(j * 128, (j + 1) * 128)
            for grp, dst in enumerate((da0, da1, da2)):
                _to_residues(sc, j, dst, DILATIONS[grp], cols)
            for grp, dst in ((1, at1), (2, at2)):
                _to_residues(sc, nq + j, dst, DILATIONS[grp], cols)
            for grp, dst in ((1, ls1), (2, ls2)):
                _to_residues(sc, 2 * nq + j, dst, DILATIONS[grp], cols)

    res = (0, 1, 2, 1, 2, 1, 2)
    return pl.pallas_call(
        body, name="att_gate_bwd", grid=(T // TT,),
        in_specs=[_tok_spec(ATT_W)] * 3 + [_tok_spec(ATT_W, C_ZA // ATT_W), pl.BlockSpec(memory_space=pl.ANY)],
        out_specs=[_tok_spec(ATT_W, C_ZA // ATT_W)] + [_res_spec(g) for g in res],
        out_shape=[S(dproj.shape, dproj.dtype)] + [_res_shape(g, BF16) for g in res[:5]]
        + [_res_shape(g, F32) for g in res[5:]],
        input_output_aliases={4: 0},
        scratch_shapes=[pltpu.VMEM((3 * nq, TT, 128), F32)],
        compiler_params=pltpu.CompilerParams(dimension_semantics=("arbitrary",)),
    )(dain, att, lse, proj, dproj)


def _split3(v):
    hi = v.astype(BF16)
    r1 = v - hi.astype(F32)
    mid = r1.astype(BF16)
    lo = (r1 - mid.astype(F32)).astype(BF16)
    return hi, mid, lo


def _tri_sum(v, upper):
    n = v.shape[0]
    row = lax.broadcasted_iota(jnp.int32, (n, n), 0)
    col = lax.broadcasted_iota(jnp.int32, (n, n), 1)
    tri = jnp.where(col >= row if upper else col <= row, 1.0, 0.0).astype(BF16)
    hi, mid, lo = _split3(v)
    return _nn(tri, hi) + _nn(tri, mid) + _nn(tri, lo)


def _gla_gates(glr_ref, w2_ref, b_ref):
    logit = _nn(glr_ref[...].astype(BF16), w2_ref[...]) + b_ref[...]
    lg = (jnp.minimum(logit, 0.0) - jnp.log(1.0 + jnp.exp(-jnp.abs(logit)))) * (1.0 / GLA_TAU)
    return logit, _tri_sum(lg, upper=False)


def _gla_head(cum, q_ref, k_ref, h):
    cols = slice(h * GDK, (h + 1) * GDK)
    b = cum[:, cols]
    last = b[GLA_C - 1:GLA_C, :]
    e_pos = jnp.exp(b)
    e_neg = jnp.exp(-b)
    e_end = jnp.exp(last - b)
    qt = q_ref[:, cols] * (GDK ** -0.5) * e_pos
    kt = k_ref[:, cols] * e_neg
    kh = k_ref[:, cols] * e_end
    return b, last, e_pos, e_neg, e_end, qt, kt, kh


def _causal(n):
    return lax.broadcasted_iota(jnp.int32, (n, n), 1) <= lax.broadcasted_iota(jnp.int32, (n, n), 0)


def _gla_fwd(proj, w2p, bg, gn):
    nc = T // GLA_C

    def body(q_ref, k_ref, v_ref, glr_ref, zg_ref, w2_ref, b_ref, gn_ref, o_ref, bin_ref, st_ref, state):
        @pl.when(pl.program_id(0) == 0)
        def _():
            state[...] = jnp.zeros_like(state)

        _, cum = _gla_gates(glr_ref, w2_ref, b_ref)
        for h in range(GH):
            _, last, _, _, _, qt, kt, kh = _gla_head(cum, q_ref, k_ref, h)
            vcols = slice(h * GDV, (h + 1) * GDV)
            st = state[h]
            st_ref[0, h] = st
            v = v_ref[:, vcols].astype(BF16)
            qb = qt.astype(BF16)
            a = jnp.where(_causal(GLA_C), _nt(qb, kt.astype(BF16)), 0.0)
            o = _nt(qb, st.astype(BF16)) + _nn(a.astype(BF16), v)
            state[h] = st * jnp.exp(last) + _tn(v, kh.astype(BF16))
            o_ref[:, vcols] = o
            r = lax.rsqrt(jnp.mean(o * o, axis=-1, keepdims=True) + EPS)
            zg = zg_ref[:, vcols]
            bin_ref[:, vcols] = (o * r * gn_ref[...] * zg * _sigmoid(zg)).astype(BF16)

    row = lambda width, cblk: pl.BlockSpec((GLA_C, width), functools.partial(lambda i, c: (i, c), c=cblk))
    full = lambda a: pl.BlockSpec(a.shape, functools.partial(lambda i, nd: (0,) * nd, nd=a.ndim))
    return pl.pallas_call(
        body, name="gla_fwd", grid=(nc,),
        in_specs=[row(512, C_QG // 512), row(512, C_KG // 512), row(1024, C_VG // 1024), row(GLR_W, C_GLR // GLR_W),
                  row(1024, C_ZG // 1024), full(w2p), full(bg), full(gn)],
        out_specs=[pl.BlockSpec((GLA_C, GH * GDV), lambda i: (i, 0)), pl.BlockSpec((GLA_C, GH * GDV), lambda i: (i, 0)),
                   pl.BlockSpec((1, GH, GDV, GDK), lambda i: (i, 0, 0, 0))],
        out_shape=[S((T, GH * GDV), F32), S((T, GH * GDV), BF16), S((nc, GH, GDV, GDK), F32)],
        scratch_shapes=[pltpu.VMEM((GH, GDV, GDK), F32)],
        compiler_params=pltpu.CompilerParams(dimension_semantics=("arbitrary",)),
    )(proj, proj, proj, proj, proj, w2p, bg, gn)


def _gla_bwd(proj, w2p, bg, gn, o_gla, states, dbin, dproj):
    nc = T // GLA_C

    def body(q_ref, k_ref, v_ref, glr_ref, zg_ref, w2_ref, b_ref, gn_ref, o_ref, st_ref, dbin_ref, buf_ref,
             out_ref, dw2_ref, dbg_ref, dgn_ref, dstate, dlogit):
        del buf_ref
        dq_ref = out_ref.at[:, C_QG:C_KG]
        dk_ref = out_ref.at[:, C_KG:C_VG]
        dv_ref = out_ref.at[:, C_VG:C_ZG]
        dzg_ref = out_ref.at[:, C_ZG:C_GLR]
        dglr_ref = out_ref.at[:, C_GLR:C_GLR + GLR_W]
        first = pl.program_id(0) == 0

        @pl.when(first)
        def _():
            dstate[...] = jnp.zeros_like(dstate)

        logit, cum = _gla_gates(glr_ref, w2_ref, b_ref)
        is_last = lax.broadcasted_iota(jnp.int32, (GLA_C, 1), 0) == GLA_C - 1
        dgn = jnp.zeros((1, GDV), F32)
        for h in range(GH):
            _, last, e_pos, e_neg, e_end, qt, kt, kh = _gla_head(cum, q_ref, k_ref, h)
            cols = slice(h * GDK, (h + 1) * GDK)
            vcols = slice(h * GDV, (h + 1) * GDV)
            o = o_ref[:, vcols]
            r = lax.rsqrt(jnp.mean(o * o, axis=-1, keepdims=True) + EPS)
            zg = zg_ref[:, vcols]
            sg = _sigmoid(zg)
            db_ = dbin_ref[:, vcols].astype(F32)
            dlin = db_ * zg * sg
            dzg_ref[:, vcols] = (db_ * (o * r * gn_ref[...]) * sg * (1.0 + zg * (1.0 - sg))).astype(BF16)
            u = dlin * gn_ref[...]
            do = (r * u - o * (r * r * r) * jnp.mean(u * o, axis=-1, keepdims=True)).astype(BF16)
            dgn = dgn + jnp.sum(dlin * o * r, axis=0, keepdims=True)
            st = st_ref[0, h]
            dst = dstate[h]
            v = v_ref[:, vcols].astype(BF16)
            qb, kb, khb = qt.astype(BF16), kt.astype(BF16), kh.astype(BF16)
            dstb = dst.astype(BF16)
            causal = _causal(GLA_C)
            a = jnp.where(causal, _nt(qb, kb), 0.0).astype(BF16)
            da = jnp.where(causal, _nt(do, v), 0.0).astype(BF16)
            dqt = _nn(do, st.astype(BF16)) + _nn(da, kb)
            dkt = _tn(da, qb)
            dkh = _nn(v, dstb)
            dv_ref[:, vcols] = (_tn(a, do) + _nt(khb, dstb)).astype(BF16)
            lam = jnp.exp(last)
            dlam = jnp.sum(dst * st, axis=0, keepdims=True)
            dstate[h] = dst * lam + _tn(do, qb)
            dq_ref[:, cols] = (dqt * e_pos * (GDK ** -0.5)).astype(BF16)
            dk_ref[:, cols] = (dkt * e_neg + dkh * e_end).astype(BF16)
            dkh_kh = dkh * kh
            dcum = dqt * qt - dkt * kt - dkh_kh
            dlast = jnp.sum(dkh_kh, axis=0, keepdims=True) + dlam * lam
            dcum = jnp.where(is_last, dcum + dlast, dcum)
            dlg = _tri_sum(dcum, upper=True)
            dlogit[:, cols] = dlg * (1.0 / GLA_TAU) * (1.0 - _sigmoid(logit[:, cols]))

        dl = dlogit[...]
        dlb = dl.astype(BF16)
        dglr_ref[...] = _nt(dlb, w2_ref[...]).astype(BF16)
        dw2 = _tn(glr_ref[...].astype(BF16), dlb)
        dbg = jnp.sum(dl, axis=0, keepdims=True)

        @pl.when(first)
        def _():
            dw2_ref[...] = dw2
            dbg_ref[...] = dbg
            dgn_ref[...] = dgn

        @pl.when(jnp.logical_not(first))
        def _():
            dw2_ref[...] += dw2
            dbg_ref[...] += dbg
            dgn_ref[...] += dgn

    rev = lambda i: nc - 1 - i
    row = lambda width, cblk: pl.BlockSpec((GLA_C, width), functools.partial(lambda i, c: (rev(i), c), c=cblk))
    full = lambda a: pl.BlockSpec(a.shape, functools.partial(lambda i, nd: (0,) * nd, nd=a.ndim))
    keep = lambda shape: pl.BlockSpec(shape, functools.partial(lambda i, nd: (0,) * nd, nd=len(shape)))
    return pl.pallas_call(
        body, name="gla_bwd", grid=(nc,),
        in_specs=[row(512, C_QG // 512), row(512, C_KG // 512), row(1024, C_VG // 1024), row(GLR_W, C_GLR // GLR_W),
                  row(1024, C_ZG // 1024), full(w2p), full(bg), full(gn), row(GH * GDV, 0),
                  pl.BlockSpec((1, GH, GDV, GDK), lambda i: (rev(i), 0, 0, 0)), row(GH * GDV, 0),
                  pl.BlockSpec(memory_space=pl.ANY)],
        out_specs=[row(GLA_GROUP_W, 0), keep((GLR_W, 512)), keep((1, 512)), keep((1, GDV))],
        out_shape=[S(dproj.shape, dproj.dtype), S((GLR_W, 512), F32), S((1, 512), F32), S((1, GDV), F32)],
        input_output_aliases={11: 0},
        scratch_shapes=[pltpu.VMEM((GH, GDV, GDK), F32), pltpu.VMEM((GLA_C, GH * GDK), F32)],
        compiler_params=pltpu.CompilerParams(dimension_semantics=("arbitrary",)),
    )(proj, proj, proj, proj, proj, w2p, bg, gn, o_gla, states, dbin, dproj)


RT = 512


def _rowchain(body, name, ins, outs, scratch=()):
    in_specs, args = [], []
    for spec in ins:
        if spec[0] == "tok":
            _, arr, width, cblk = spec
            in_specs.append(pl.BlockSpec((RT, width), functools.partial(lambda i, c: (i, c), c=cblk)))
        else:
            arr = spec[1]
            in_specs.append(pl.BlockSpec(arr.shape, functools.partial(lambda i, nd: (0,) * nd, nd=arr.ndim)))
        args.append(arr)
    out_specs, out_shape = [], []
    for spec in outs:
        if spec[0] == "tok":
            _, shape, dtype, width, cblk = spec
            out_specs.append(pl.BlockSpec((RT, width), functools.partial(lambda i, c: (i, c), c=cblk)))
        else:
            _, shape, dtype = spec
            out_specs.append(pl.BlockSpec(shape, functools.partial(lambda i, nd: (0,) * nd, nd=len(shape))))
        out_shape.append(S(shape, dtype))
    return pl.pallas_call(
        body, name=name, grid=(T // RT,), in_specs=in_specs, out_specs=out_specs, out_shape=out_shape,
        scratch_shapes=list(scratch), compiler_params=pltpu.CompilerParams(dimension_semantics=("arbitrary",)),
    )(*args)


def _tok(arr, width=None, cblk=0):
    return ("tok", arr, arr.shape[1] if width is None else width, cblk)


def _tok_out(dtype, width=D):
    return ("tok", (T, width), dtype, width, 0)


def _branches_fwd(ain, bin_, proj, x, w_att, w_gla, w_out):
    def body(ain_ref, bin_ref, g_ref, x_ref, wa_ref, wg_ref, wo_ref, ya_ref, yb_ref, y_ref, x1_ref):
        ya = _nn(ain_ref[...], wa_ref[...]).astype(BF16)
        yb = _nn(bin_ref[...], wg_ref[...]).astype(BF16)
        ya_ref[...] = ya
        yb_ref[...] = yb
        y = (_sigmoid(g_ref[:, :D]) * ya.astype(F32) + _sigmoid(g_ref[:, D:]) * yb.astype(F32)).astype(BF16)
        y_ref[...] = y
        x1_ref[...] = x_ref[...] + _nn(y, wo_ref[...])

    return _rowchain(body, "branches_fwd",
                     [_tok(ain), _tok(bin_), _tok(proj, 2 * D, C_GA // (2 * D)), _tok(x), ("all", w_att),
                      ("all", w_gla), ("all", w_out)],
                     [_tok_out(BF16), _tok_out(BF16), _tok_out(BF16), _tok_out(F32)])


def _ple_loss(x1, p, target, g2, w_pg, w_ple):
    def body(x1_ref, p_ref, t_ref, g_ref, wpg_ref, wple_ref, n2_ref, loss_ref, dout_ref, de_ref, du_ref, acc):
        first = pl.program_id(0) == 0
        x1 = x1_ref[...]
        r = lax.rsqrt(jnp.mean(x1 * x1, axis=-1, keepdims=True) + EPS)
        n2 = (x1 * r * g_ref[...]).astype(BF16)
        n2_ref[...] = n2
        pg = _sigmoid(_nn(n2, wpg_ref[...]))
        e_ = _nn(p_ref[...].astype(BF16), wple_ref[...])
        diff = x1 + e_ * pg - t_ref[...]
        part = jnp.sum(diff * diff, axis=0, keepdims=True)

        @pl.when(first)
        def _():
            acc[...] = part

        @pl.when(jnp.logical_not(first))
        def _():
            acc[...] += part

        dout = diff * (1.0 / D)
        dout_ref[...] = dout
        de_ref[...] = (dout * pg).astype(BF16)
        du_ref[...] = (dout * e_ * pg * (1.0 - pg)).astype(BF16)
        loss_ref[...] = jnp.zeros((1, 128), F32) + jnp.sum(acc[...], axis=-1, keepdims=True) * (0.5 / D)

    return _rowchain(body, "ple_loss", [_tok(x1), _tok(p), _tok(target), ("all", g2), ("all", w_pg), ("all", w_ple)],
                     [_tok_out(BF16), ("acc", (1, 128), F32), _tok_out(F32), _tok_out(BF16), _tok_out(BF16)],
                     scratch=[pltpu.VMEM((1, D), F32)])


def _ple_bwd(du, x1, dout, g2, w_pg, w_out):
    def body(du_ref, x1_ref, dout_ref, g_ref, wpg_ref, wo_ref, dx_ref, dxb_ref, dy_ref, dg_ref):
        first = pl.program_id(0) == 0
        x1 = x1_ref[...]
        r = lax.rsqrt(jnp.mean(x1 * x1, axis=-1, keepdims=True) + EPS)
        dn = _nt(du_ref[...], wpg_ref[...])
        u = dn * g_ref[...]
        dx = dout_ref[...] + r * u - x1 * (r * r * r) * jnp.mean(u * x1, axis=-1, keepdims=True)
        dxb = dx.astype(BF16)
        dx_ref[...] = dx
        dxb_ref[...] = dxb
        dy_ref[...] = _nt(dxb, wo_ref[...]).astype(BF16)
        part = jnp.sum(dn * x1 * r, axis=0, keepdims=True)

        @pl.when(first)
        def _():
            dg_ref[...] = part

        @pl.when(jnp.logical_not(first))
        def _():
            dg_ref[...] += part

    return _rowchain(body, "ple_bwd", [_tok(du), _tok(x1), _tok(dout), ("all", g2), ("all", w_pg), ("all", w_out)],
                     [_tok_out(F32), _tok_out(BF16), _tok_out(BF16), ("acc", (1, D), F32)])


def _branches_bwd(dy, ya, yb, proj, w_att, w_gla):
    def body(dy_ref, ya_ref, yb_ref, g_ref, wa_ref, wg_ref, dg_ref, dya_ref, dyb_ref, dain_ref, dbin_ref):
        dy_ = dy_ref[...].astype(F32)
        sa, sb = _sigmoid(g_ref[:, :D]), _sigmoid(g_ref[:, D:])
        dg_ref[:, :D] = (dy_ * ya_ref[...].astype(F32) * sa * (1.0 - sa)).astype(BF16)
        dg_ref[:, D:] = (dy_ * yb_ref[...].astype(F32) * sb * (1.0 - sb)).astype(BF16)
        dya = (dy_ * sa).astype(BF16)
        dyb = (dy_ * sb).astype(BF16)
        dya_ref[...] = dya
        dyb_ref[...] = dyb
        dain_ref[...] = _nt(dya, wa_ref[...]).astype(BF16)
        dbin_ref[...] = _nt(dyb, wg_ref[...]).astype(BF16)

    gates = C_GA // (2 * D)
    return _rowchain(body, "branches_bwd",
                     [_tok(dy), _tok(ya), _tok(yb), _tok(proj, 2 * D, gates), ("all", w_att), ("all", w_gla)],
                     [("tok", (T, NCOL), BF16, 2 * D, gates), _tok_out(BF16), _tok_out(BF16), _tok_out(BF16, ATT_W),
                      _tok_out(BF16)])


def _peer(k):
    x, y, c = lax.axis_index("x"), lax.axis_index("y"), lax.axis_index("c")
    return (x ^ ((k >> 2) & 1), y ^ ((k >> 1) & 1), c ^ (k & 1))


def _my_index():
    return 4 * lax.axis_index("x") + 2 * lax.axis_index("y") + lax.axis_index("c")


def _peer_index(k):
    px, py, pc = _peer(k)
    return 4 * px + 2 * py + pc


def _pairwise_plan(src_of, dst_of, landed_of, own_src, own_dst):
    def plan(ins, outs, send, recv, local):
        n = len(ins)

        def own():
            return [pltpu.make_async_copy(own_src(ins[a]), own_dst(outs[a]), local.at[a]) for a in range(n)]

        def remote(k, a, src, dst):
            return pltpu.make_async_remote_copy(src_ref=src, dst_ref=dst, send_sem=send.at[k - 1, a],
                                                recv_sem=recv.at[k - 1, a], device_id=_peer(k), device_id_type=MESH)

        def sent():
            return [remote(k, a, src_of(ins[a], k), dst_of(outs[a])) for k in range(1, NDEV) for a in range(n)]

        def start():
            for cp in own() + sent():
                cp.start()

        def finish():
            for k in range(1, NDEV):
                for a in range(n):
                    remote(k, a, own_src(ins[a]), landed_of(outs[a], k)).wait_recv()
            for cp in sent():
                cp.wait_send()
            for cp in own():
                cp.wait()

        return start, finish

    return plan


def _pairwise_sems(n):
    return [pltpu.SemaphoreType.DMA((NDEV - 1, n)), pltpu.SemaphoreType.DMA((NDEV - 1, n)),
            pltpu.SemaphoreType.DMA((n,))]


def _gather_side(arrs):
    plan = _pairwise_plan(src_of=lambda i, k: i, dst_of=lambda o: o.at[_my_index()],
                          landed_of=lambda o, k: o.at[_peer_index(k)],
                          own_src=lambda i: i, own_dst=lambda o: o.at[_my_index()])
    return dict(arrs=arrs, out_shape=[S((NDEV,) + a.shape, a.dtype) for a in arrs],
                scratch=_pairwise_sems(len(arrs)), plan=plan)


def _exchange_side(arrs):
    plan = _pairwise_plan(src_of=lambda i, k: i.at[_peer_index(k)], dst_of=lambda o: o.at[_my_index()],
                          landed_of=lambda o, k: o.at[_peer_index(k)],
                          own_src=lambda i: i.at[_my_index()], own_dst=lambda o: o.at[_my_index()])
    return dict(arrs=arrs, out_shape=[S(a.shape, a.dtype) for a in arrs], scratch=_pairwise_sems(len(arrs)), plan=plan)


def _comm_call(side, name):
    n = len(side["arrs"])

    def body(*refs):
        start, finish = side["plan"](refs[:n], refs[n:2 * n], *refs[2 * n:])
        start()
        finish()

    hbm = pl.BlockSpec(memory_space=pl.ANY)
    return pl.pallas_call(body, name=name, in_specs=[hbm] * n, out_specs=[hbm] * n, out_shape=side["out_shape"],
                          scratch_shapes=side["scratch"])(*side["arrs"])


def _all_gather_by_chip(arrs, name):
    n = len(arrs)

    def body(*refs):
        ins, outs = refs[:n], refs[n:2 * n]
        send, recv, local = refs[2 * n:]
        x, y, c = lax.axis_index("x"), lax.axis_index("y"), lax.axis_index("c")
        me, sibling = (x, y, c), (x, y, 1 - c)
        chips = [(1 - x, y), (x, 1 - y), (1 - x, 1 - y)]

        def copy(k, a, block, to, src=None):
            px, py, pc = block
            slot = outs[a].at[4 * px + 2 * py + pc]
            return pltpu.make_async_remote_copy(
                src_ref=slot if src is None else src, dst_ref=slot, send_sem=send.at[k, a], recv_sem=recv.at[k, a],
                device_id=to, device_id_type=MESH)

        mine = [pltpu.make_async_copy(ins[a], outs[a].at[4 * x + 2 * y + c], local.at[a]) for a in range(n)]
        first = []
        for a in range(n):
            first.append(copy(0, a, me, sibling, src=ins[a]))
            first += [copy(1 + j, a, me, (*chip, c), src=ins[a]) for j, chip in enumerate(chips)]
        for cp in mine + first:
            cp.start()
        passed = []
        for j, chip in enumerate(chips):
            for a in range(n):
                copy(1 + j, a, (*chip, c), me).wait_recv()
                passed.append(copy(4 + j, a, (*chip, c), sibling))
                passed[-1].start()
        for a in range(n):
            copy(0, a, sibling, me).wait_recv()
        for j, chip in enumerate(chips):
            for a in range(n):
                copy(4 + j, a, (*chip, 1 - c), me).wait_recv()
        for cp in first + passed:
            cp.wait_send()
        for cp in mine:
            cp.wait()

    hbm = pl.BlockSpec(memory_space=pl.ANY)
    return pl.pallas_call(
        body, name=name, in_specs=[hbm] * n, out_specs=[hbm] * n,
        out_shape=[S((NDEV,) + a.shape, a.dtype) for a in arrs],
        scratch_shapes=[pltpu.SemaphoreType.DMA((NDEV - 1, n)), pltpu.SemaphoreType.DMA((NDEV - 1, n)),
                        pltpu.SemaphoreType.DMA((n,))],
    )(*arrs)


NCHIP = 4


def _exchange_sibling(arrs, name):
    n = len(arrs)

    def body(*refs):
        ins, outs = refs[:n], refs[n:2 * n]
        send, recv = refs[2 * n:]
        x, y, c = lax.axis_index("x"), lax.axis_index("y"), lax.axis_index("c")
        copies = []
        for q in range(NCHIP):
            for a in range(n):
                copies.append(pltpu.make_async_remote_copy(
                    src_ref=ins[a].at[2 * q + (1 - c)], dst_ref=outs[a].at[q], send_sem=send.at[q, a],
                    recv_sem=recv.at[q, a], device_id=(x, y, 1 - c), device_id_type=MESH))
        for cp in copies:
            cp.start()
        for cp in copies:
            cp.wait_recv()
        for cp in copies:
            cp.wait_send()

    hbm = pl.BlockSpec(memory_space=pl.ANY)
    return pl.pallas_call(
        body, name=name, in_specs=[hbm] * n, out_specs=[hbm] * n,
        out_shape=[S((NCHIP,) + a.shape[1:], a.dtype) for a in arrs],
        scratch_shapes=[pltpu.SemaphoreType.DMA((NCHIP, n)), pltpu.SemaphoreType.DMA((NCHIP, n))],
    )(*arrs)


def _pair_add(mine, got, core, name):
    _, rows, cols = mine.shape
    tc = 256
    assert cols % tc == 0

    def body(core_ref, a_ref, b_ref, o_ref):
        o_ref[...] = (a_ref[...].astype(F32) + b_ref[...].astype(F32)).astype(BF16)

    return pl.pallas_call(
        body, name=name,
        grid_spec=pltpu.PrefetchScalarGridSpec(
            num_scalar_prefetch=1, grid=(NCHIP, cols // tc),
            in_specs=[pl.BlockSpec((None, rows, tc), lambda q, i, core_ref: (2 * q + core_ref[0], 0, i)),
                      pl.BlockSpec((None, rows, tc), lambda q, i, core_ref: (q, 0, i))],
            out_specs=pl.BlockSpec((None, rows, tc), lambda q, i, core_ref: (q, 0, i))),
        out_shape=S((NCHIP, rows, cols), BF16),
    )(core, mine, got)


def _chips_side(arrs):
    def plan(ins, outs, send, recv, local):
        n = len(ins)

        def places():
            x, y, c = lax.axis_index("x"), lax.axis_index("y"), lax.axis_index("c")
            return 2 * x + y, c, [(1 - x, y), (x, 1 - y), (1 - x, 1 - y)]

        def own():
            here, _, _ = places()
            return [pltpu.make_async_copy(ins[a].at[here], outs[a].at[here], local.at[a]) for a in range(n)]

        def remote(j, a, src_slot, dst_slot):
            _, c, chips = places()
            cx, cy = chips[j]
            return pltpu.make_async_remote_copy(
                src_ref=ins[a].at[src_slot], dst_ref=outs[a].at[dst_slot], send_sem=send.at[j, a],
                recv_sem=recv.at[j, a], device_id=(cx, cy, c), device_id_type=MESH)

        def sent():
            here, _, chips = places()
            return [remote(j, a, 2 * cx + cy, here) for j, (cx, cy) in enumerate(chips) for a in range(n)]

        def start():
            for cp in own() + sent():
                cp.start()

        def finish():
            here, _, chips = places()
            for j, (cx, cy) in enumerate(chips):
                for a in range(n):
                    remote(j, a, here, 2 * cx + cy).wait_recv()
            for cp in sent():
                cp.wait_send()
            for cp in own():
                cp.wait()

        return start, finish

    n = len(arrs)
    return dict(arrs=arrs, out_shape=[S(a.shape, a.dtype) for a in arrs],
                scratch=[pltpu.SemaphoreType.DMA((NCHIP - 1, n)), pltpu.SemaphoreType.DMA((NCHIP - 1, n)),
                         pltpu.SemaphoreType.DMA((n,))], plan=plan)


def _adamw(parts, w, m, v, name, tr, tc=None):
    rows, cols = w.shape
    if tc is None:
        assert rows % tr == 0
        grid, shape, at = (rows // tr,), (tr, cols), (lambda i: (i, 0))
    else:
        assert cols % tc == 0
        grid, shape, at = (cols // tc,), (rows, tc), (lambda i: (0, i))
    c1 = 1.0 - ADAM_B1 ** ADAM_STEP
    c2 = 1.0 - ADAM_B2 ** ADAM_STEP

    nparts = parts.shape[0]

    def body(p_ref, w_ref, m_ref, v_ref, g_ref, d_ref, mo_ref, vo_ref):
        g = p_ref[0].astype(F32)
        for s in range(1, nparts):
            g = g + p_ref[s].astype(F32)
        m_new = ADAM_B1 * m_ref[...] + (1.0 - ADAM_B1) * g
        v_new = ADAM_B2 * v_ref[...] + (1.0 - ADAM_B2) * (g * g)
        g_ref[...] = g
        mo_ref[...] = m_new
        vo_ref[...] = v_new
        d_ref[...] = -ADAM_LR * ((m_new / c1) / (jnp.sqrt(v_new / c2) + ADAM_EPS) + ADAM_WD * w_ref[...])

    blk = pl.BlockSpec(shape, at)
    return pl.pallas_call(
        body, name=name, grid=grid,
        in_specs=[pl.BlockSpec((nparts,) + shape, lambda i: (0,) + at(i)), blk, blk, blk],
        out_specs=[blk] * 4, out_shape=[S((rows, cols), F32)] * 4,
        compiler_params=pltpu.CompilerParams(dimension_semantics=("parallel",)),
    )(parts, w, m, v)


def _adam_math(g, w, m, v):
    c1 = 1.0 - ADAM_B1 ** ADAM_STEP
    c2 = 1.0 - ADAM_B2 ** ADAM_STEP
    m_new = ADAM_B1 * m + (1.0 - ADAM_B1) * g
    v_new = ADAM_B2 * v + (1.0 - ADAM_B2) * (g * g)
    return -ADAM_LR * ((m_new / c1) / (jnp.sqrt(v_new / c2) + ADAM_EPS) + ADAM_WD * w), m_new, v_new


def _adamw_small(parts, params, loss_parts):
    n = len(params)

    def body(*refs):
        p_refs, rest = refs[:n], refs[n + 1:]
        total = refs[n][0]
        for s in range(1, NDEV):
            total = total + refs[n][s]
        refs[-1][...] = total
        for j in range(n):
            w_ref, m_ref, v_ref = rest[3 * j:3 * j + 3]
            g_ref, d_ref, mo_ref, vo_ref = rest[3 * n + 4 * j:3 * n + 4 * j + 4]
            width = w_ref.shape[1]
            g = p_refs[j][0]
            for s in range(1, NDEV):
                g = g + p_refs[j][s]
            g = g[:, :width]
            delta, m_new, v_new = _adam_math(g, w_ref[...], m_ref[...], v_ref[...])
            g_ref[...] = g
            d_ref[...] = delta
            mo_ref[...] = m_new
            vo_ref[...] = v_new

    flat = [a for group in params for a in group]
    return pl.pallas_call(
        body, name="adam_small",
        out_shape=[S(group[0].shape, F32) for group in params for _ in range(4)] + [S((1, 128), F32)],
    )(*parts, loss_parts, *flat)


def _adamw_rows(parts, w, m, v, name, tc=128):
    rows, _, cols = w.shape
    nparts = parts.shape[0]
    nsteps = cols // tc

    def body(p_ref, w_hbm, m_hbm, v_hbm, g_hbm, d_hbm, mo_hbm, vo_hbm, inbuf, outbuf, insem, outsem):
        i = pl.program_id(0)
        slot = i & 1

        def view(ref, step):
            return ref.at[:, 0, pl.ds(pl.multiple_of(step * tc, tc), tc)]

        def fetch(step, sl):
            return [pltpu.make_async_copy(view(src, step), inbuf.at[sl, k], insem.at[sl, k])
                    for k, src in enumerate((w_hbm, m_hbm, v_hbm))]

        def write(step, sl):
            return [pltpu.make_async_copy(outbuf.at[sl, k], view(dst, step), outsem.at[sl, k])
                    for k, dst in enumerate((g_hbm, d_hbm, mo_hbm, vo_hbm))]

        @pl.when(i == 0)
        def _():
            for cp in fetch(0, 0):
                cp.start()

        @pl.when(i + 1 < nsteps)
        def _():
            for cp in fetch(i + 1, 1 - slot):
                cp.start()

        for cp in fetch(i, slot):
            cp.wait()

        @pl.when(i >= 2)
        def _():
            for cp in write(i - 2, slot):
                cp.wait()

        g = p_ref[0].astype(F32)
        for s in range(1, nparts):
            g = g + p_ref[s].astype(F32)
        delta, m_new, v_new = _adam_math(g, inbuf[slot, 0], inbuf[slot, 1], inbuf[slot, 2])
        for k, val in enumerate((g, delta, m_new, v_new)):
            outbuf[slot, k] = val
        for cp in write(i, slot):
            cp.start()

        @pl.when(i == nsteps - 1)
        def _():
            for cp in write(i - 1, 1 - slot) + write(i, slot):
                cp.wait()

    hbm = pl.BlockSpec(memory_space=pl.ANY)
    assert nsteps >= 2
    return pl.pallas_call(
        body, name=name, grid=(nsteps,),
        in_specs=[pl.BlockSpec((nparts, rows, tc), lambda i: (0, 0, i)), hbm, hbm, hbm],
        out_specs=[hbm] * 4, out_shape=[S((rows, 1, cols), F32)] * 4,
        scratch_shapes=[pltpu.VMEM((2, 3, rows, tc), F32), pltpu.VMEM((2, 4, rows, tc), F32),
                        pltpu.SemaphoreType.DMA((2, 3)), pltpu.SemaphoreType.DMA((2, 4))],
        compiler_params=pltpu.CompilerParams(dimension_semantics=("arbitrary",)),
    )(parts, w, m, v)


def _to_aligned(wt):
    pad = jnp.zeros((GLR_W - GLR_N, wt.shape[1]), wt.dtype)
    return jnp.concatenate([wt[O_QG:O_GLR], wt[O_ZG:O_GA], wt[O_GLR:O_ZG], pad, wt[O_ZA:O_QG], wt[O_GA:O_END],
                            wt[O_QA:O_ZA]], axis=0)


def _from_aligned(wt):
    return jnp.concatenate([wt[C_QA:], wt[C_ZA:C_GA], wt[C_QG:C_ZG], wt[C_GLR:C_GLR + GLR_N], wt[C_ZG:C_GLR],
                            wt[C_GA:C_QA]], axis=0)


_PIECES = ((O_QA, O_ZA, C_QA), (O_ZA, O_QG, C_ZA), (O_QG, O_GLR, C_QG), (O_GLR, O_ZG, C_GLR), (O_ZG, O_GA, C_ZG),
           (O_GA, O_END, C_GA))


def _shards_to_aligned(g):
    out = []
    for o0, o1, a0 in sorted(_PIECES, key=lambda t: t[2]):
        pos = o0
        while pos < o1:
            d = pos // W_IN_SHARD
            end = min(o1, (d + 1) * W_IN_SHARD)
            out.append(g[d, pos - d * W_IN_SHARD:end - d * W_IN_SHARD])
            pos = end
        if a0 == C_GLR:
            out.append(jnp.zeros((GLR_W - GLR_N, g.shape[2]), g.dtype))
    return jnp.concatenate(out, axis=0)


def _aligned_to_shards(wt):
    shards = []
    for d in range(NDEV):
        lo, hi = d * W_IN_SHARD, (d + 1) * W_IN_SHARD
        parts = [wt[a0 + max(lo, o0) - o0:a0 + min(hi, o1) - o0] for o0, o1, a0 in _PIECES if max(lo, o0) < min(hi, o1)]
        shards.append(jnp.concatenate(parts, axis=0))
    return jnp.stack(shards)


def _col_blocks(w, width):
    return w.reshape(w.shape[0], NDEV, width).transpose(1, 0, 2)


def _from_col_blocks(w):
    return w.transpose(1, 0, 2).reshape(w.shape[1], NDEV * w.shape[2])


def _local_step(x2, p2, pos, tgt, norm_g, qk_norm_q, qk_norm_k, gla_gate_b, gla_norm_g, ple_norm_g, w_al,
                weights=None, proj_side=None, unpack=None, dw_side_of=None, dh_side_of=None):
    half = ROT_DIM // 2
    inv8 = jnp.power(jnp.float32(ROPE_THETA), -jnp.arange(half, dtype=F32) * 2.0 / ROT_DIM)
    inv = jnp.tile(jnp.concatenate([inv8, inv8, jnp.zeros((HD - ROT_DIM,), F32)]), 2).reshape(1, 128)
    gq = jnp.tile(qk_norm_q, (1, 2))
    gk = jnp.tile(qk_norm_k, (1, 2))

    proj, h, got = _proj_rms(x2, norm_g, w_al, proj_side)
    if proj_side is not None:
        weights = unpack(got)
    w2p, w_att_f, w_gla_f, w_out_f, w_pg_f, w_ple_f = weights
    qkv = _qk_prep(proj, pos, inv, gq, gk)
    fwd = [_att_fwd(qkv[g], qkv[3 + g], qkv[6 + g], g, f"att_fwd{g}") for g in range(3)]
    att, lse, ain = _att_merge([f[0] for f in fwd], [f[1] for f in fwd], proj)
    o_gla, bin_, states = _gla_fwd(proj, w2p, gla_gate_b, gla_norm_g)
    ya, yb, y, x1 = _branches_fwd(ain, bin_, proj, x2, w_att_f, w_gla_f, w_out_f)
    n2, loss_v, dout, de, du = _ple_loss(x1, p2, tgt, ple_norm_g, w_pg_f, w_ple_f)

    dw_ple = _mm(p2, de, mode="tn", name="dw_ple", tm=PLE, tn=D, tk=512)
    dw_pg = _mm(n2, du, mode="tn", name="dw_pg", tm=D, tn=D, tk=512)
    dx1, dx1b, dy, dg_ple = _ple_bwd(du, x1, dout, ple_norm_g, w_pg_f, w_out_f)
    dw_out = _mm(y, dx1b, mode="tn", name="dw_out", tm=D, tn=D, tk=512)
    dproj, dya, dyb, dain, dbin = _branches_bwd(dy, ya, yb, proj, w_att_f, w_gla_f)
    dw_att = _mm(ain, dya, mode="tn", name="dw_att", tm=512, tn=D, tk=512)
    dw_gla = _mm(bin_, dyb, mode="tn", name="dw_gla", tm=D, tn=D, tk=512)
    dproj, da0, da1, da2, at1, at2, ls1, ls2 = _att_gate_bwd(dain, att, lse, proj, dproj)
    datts, atts, lses = (da0, da1, da2), (att[None], at1, at2), (lse[None], ls1, ls2)
    dproj, dw2, dbg, dgn = _gla_bwd(proj, w2p, gla_gate_b, gla_norm_g, o_gla, states, dbin, dproj)
    bwd = [_att_bwd(qkv[g], qkv[3 + g], qkv[6 + g], datts[g], atts[g], lses[g], g, f"att_bwd{g}") for g in range(3)]
    dproj, dgq, dgk = _qk_bwd(proj, pos, inv, gq, gk, [b[0] for b in bwd], [b[1] for b in bwd],
                              [b[2] for b in bwd], dproj)
    out = dict(loss=loss_v, dw2=dw2, dw_att=dw_att, dw_gla=dw_gla, dw_out=dw_out, dw_pg=dw_pg, dw_ple=dw_ple,
               dgq=dgq, dgk=dgk, dbg=dbg, dgn=dgn, dg_ple=dg_ple)
    if dw_side_of is None:
        dw_al = _mm(dproj, h, mode="tn", name="dw_in", tm=1536, tn=D, tk=2048, out_dtype=BF16)
    else:
        dw_al, out["dw_side"] = _mm(dproj, h, mode="tn", name="dw_in", tm=1536, tn=D, tk=2048, out_dtype=BF16,
                                    side=dw_side_of(out))
    grad_x, dg_norm, out["dh_side"] = _dh_rms(dproj, w_al, x2, norm_g, dx1,
                                              None if dh_side_of is None else dh_side_of(dw_al))
    out.update(grad_x=grad_x, dw_al=dw_al, dg_norm=dg_norm)
    return out


def kernel(x, p, positions, norm_g, w_in, qk_norm_q, qk_norm_k, gla_gate_w2, gla_gate_b, gla_norm_g, w_att_proj, w_gla_proj, w_out, ple_norm_g, w_ple_gate, w_ple, loss_target, m_norm_g, m_w_in, m_qk_norm_q, m_qk_norm_k, m_gla_gate_w2, m_gla_gate_b, m_gla_norm_g, m_w_att_proj, m_w_gla_proj, m_w_out, m_ple_norm_g, m_w_ple_gate, m_w_ple, v_norm_g, v_w_in, v_qk_norm_q, v_qk_norm_k, v_gla_gate_w2, v_gla_gate_b, v_gla_norm_g, v_w_att_proj, v_w_gla_proj, v_w_out, v_ple_norm_g, v_w_ple_gate, v_w_ple):
    x2, p2, tgt = x[0], p[0, 0], loss_target[0]
    pos = positions.astype(F32).reshape(T, 1)

    rows3 = jnp.stack([w_gla_proj[0], w_out[0], w_ple_gate[0]]).astype(BF16)
    cols3 = jnp.concatenate([w_att_proj[0], w_ple[0], jnp.pad(gla_gate_w2[0], ((0, 0), (0, 64)))], axis=0).astype(BF16)
    (g_in,) = _all_gather_by_chip([w_in[0].T.astype(BF16)], "gather_w_in")
    w_al = _shards_to_aligned(g_in)

    def unpack(got):
        g_rows, g_cols = got
        w2_f = _from_col_blocks(g_cols[:, 768:784, :64])
        return (jnp.pad(w2_f, ((0, GLR_W - GLR_N), (0, 0))), _from_col_blocks(g_cols[:, :512]),
                g_rows[:, 0].reshape(D, D), g_rows[:, 1].reshape(D, D), g_rows[:, 2].reshape(D, D),
                _from_col_blocks(g_cols[:, 512:768]))

    def dw_side_of(g):
        s_rows = jnp.concatenate([g[k].reshape(NDEV, 128, D) for k in ("dw_gla", "dw_out", "dw_pg")], axis=1)
        s_cols = jnp.concatenate([_col_blocks(g["dw_att"], 128), _col_blocks(g["dw_ple"], 128),
                                  jnp.pad(_col_blocks(g["dw2"][:GLR_N], 64), ((0, 0), (0, 0), (0, 64)))], axis=1)
        return _exchange_side([s_rows.astype(BF16), s_cols.astype(BF16)])

    def dh_side_of(dw_al):
        s_in = _aligned_to_shards(dw_al)
        (from_sibling,) = _exchange_sibling([s_in], "exchange_sibling")
        core = lax.axis_index("c").astype(jnp.int32).reshape(1)
        return _chips_side([_pair_add(s_in, from_sibling, core, "pair_add")])

    loc = _local_step(x2, p2, pos, tgt, norm_g, qk_norm_q, qk_norm_k, gla_gate_b, gla_norm_g, ple_norm_g, w_al,
                      proj_side=_gather_side([rows3, cols3]), unpack=unpack, dw_side_of=dw_side_of,
                      dh_side_of=dh_side_of)
    loss_v, grad_x = loc["loss"], loc["grad_x"]
    dg_norm, dgq, dgk, dbg, dgn, dg_ple = (loc[k] for k in ("dg_norm", "dgq", "dgk", "dbg", "dgn", "dg_ple"))
    r_rows, r_cols = loc["dw_side"]
    (r_in,) = loc["dh_side"]

    r_small = _comm_call(_gather_side([dg_norm, dgq, dgk, dbg, dgn, dg_ple, loss_v]), "gather_small")

    outs = {}

    def adam(nm, parts, w, m, v, tr):
        outs[nm] = _adamw(parts, w, m, v, "adam_" + nm, tr)

    rows_of = lambda a: jnp.transpose(a, (2, 0, 1))
    outs["w_in"] = [jnp.transpose(o, (1, 2, 0))[0] for o in
                    _adamw_rows(r_in, rows_of(w_in), rows_of(m_w_in), rows_of(v_w_in), "adam_w_in")]
    adam("w_gla_proj", r_rows[:, :128], w_gla_proj[0], m_w_gla_proj[0], v_w_gla_proj[0], 128)
    adam("w_out", r_rows[:, 128:256], w_out[0], m_w_out[0], v_w_out[0], 128)
    adam("w_ple_gate", r_rows[:, 256:], w_ple_gate[0], m_w_ple_gate[0], v_w_ple_gate[0], 128)
    adam("w_att_proj", r_cols[:, :512], w_att_proj[0], m_w_att_proj[0], v_w_att_proj[0], 512)
    adam("w_ple", r_cols[:, 512:768], w_ple[0], m_w_ple[0], v_w_ple[0], 256)
    adam("gla_gate_w2", r_cols[:, 768:784, :64], gla_gate_w2[0], m_gla_gate_w2[0], v_gla_gate_w2[0], 16)
    small = ((norm_g, m_norm_g, v_norm_g), (qk_norm_q, m_qk_norm_q, v_qk_norm_q), (qk_norm_k, m_qk_norm_k, v_qk_norm_k),
             (gla_gate_b, m_gla_gate_b, v_gla_gate_b), (gla_norm_g, m_gla_norm_g, v_gla_norm_g),
             (ple_norm_g, m_ple_norm_g, v_ple_norm_g))
    sm = _adamw_small(r_small[:6], small, r_small[6])
    for j, nm in enumerate(("norm_g", "qk_norm_q", "qk_norm_k", "gla_gate_b", "gla_norm_g", "ple_norm_g")):
        outs[nm] = [o[0] for o in sm[4 * j:4 * j + 4]]

    loss = sm[-1][0, 0]
    order = ["norm_g", "w_in", "qk_norm_q", "qk_norm_k", "gla_gate_w2", "gla_gate_b", "gla_norm_g", "w_att_proj",
             "w_gla_proj", "w_out", "ple_norm_g", "w_ple_gate", "w_ple"]
    result = [loss, grad_x[None]]
    for i in range(4):
        result += [outs[nm][i][None] for nm in order]
    return tuple(result)
```

```python
import functools

import jax
import jax.numpy as jnp
from jax import lax
from jax.experimental import pallas as pl
from jax.experimental.pallas import tpu as pltpu

F32 = jnp.float32
BF16 = jnp.bfloat16
S = jax.ShapeDtypeStruct

T = 4096
D = 1024
NDEV = 8
HD = 64
ATT_W = 512
ATT_QKV = 1536
DILATIONS = (1, 4, 16)
BLK = 128
GH, GDK, GDV = 4, 128, 256
GLA_C = 128
PLE = 256
EPS = 1e-6
ROT_DIM = 16
ROPE_THETA = 500000.0
GLA_TAU = 16.0
W_IN_COLS = 10256
W_IN_SHARD = 1282

C_QG, C_KG, C_VG, C_ZG, C_GLR, C_ZA, C_GA, C_GB, C_QA, C_KA, C_VA = (
    0, 512, 1024, 2048, 3072, 3584, 4096, 5120, 6144, 7680, 9216)
GLA_GROUP_W = 3584
GLR_W = 512
NCOL = 10752
GLR_N = 16
O_QA, O_ZA, O_QG, O_GLR, O_ZG, O_GA, O_END = 0, 4608, 5120, 7168, 7184, 8208, 10256

ADAM_LR, ADAM_B1, ADAM_B2, ADAM_EPS, ADAM_WD, ADAM_STEP = 0.001, 0.9, 0.999, 1e-08, 0.01, 10

MESH = pl.DeviceIdType.MESH


def _sigmoid(z):
    return 1.0 / (1.0 + jnp.exp(-z))


def _dot(a, b, dims):
    return lax.dot_general(a, b, (dims, ((), ())), preferred_element_type=F32)


def _nn(a, b):
    return _dot(a, b, ((1,), (0,)))


def _nt(a, b):
    return _dot(a, b, ((1,), (1,)))


def _tn(a, b):
    return _dot(a, b, ((0,), (0,)))


def _mm(a, b, *, mode, name, tm, tn, tk, out_dtype=F32, res=None, side=None):
    if mode == "nn":
        (m, k), n = a.shape, b.shape[1]
        a_spec = pl.BlockSpec((tm, tk), lambda i, j, l: (i, l))
        b_spec = pl.BlockSpec((tk, tn), lambda i, j, l: (l, j))
        dot = _nn
    elif mode == "nt":
        (m, k), n = a.shape, b.shape[0]
        a_spec = pl.BlockSpec((tm, tk), lambda i, j, l: (i, l))
        b_spec = pl.BlockSpec((tn, tk), lambda i, j, l: (j, l))
        dot = _nt
    else:
        (k, m), n = a.shape, b.shape[1]
        a_spec = pl.BlockSpec((tk, tm), lambda i, j, l: (l, i))
        b_spec = pl.BlockSpec((tk, tn), lambda i, j, l: (l, j))
        dot = _tn
    assert m % tm == 0 and n % tn == 0 and k % tk == 0, (name, m, n, k)
    grid = (m // tm, n // tn, k // tk)
    nk = grid[2]
    o_spec = pl.BlockSpec((tm, tn), lambda i, j, l: (i, j))
    in_specs = [a_spec, b_spec]
    args = [a, b]
    if res is not None:
        in_specs.append(o_spec)
        args.append(res)
    n_in = len(args)
    n_side = 0 if side is None else len(side["arrs"])
    hbm = pl.BlockSpec(memory_space=pl.ANY)

    def body(*refs):
        a_ref, b_ref = refs[0], refs[1]
        r_ref = refs[2] if res is not None else None
        o_ref = refs[n_in + n_side]
        scratch = refs[n_in + 2 * n_side + 1:]
        if side is not None:
            start, finish_side = side["plan"](refs[n_in:n_in + n_side], refs[n_in + n_side + 1:n_in + 2 * n_side + 1],
                                              *scratch[1 if nk > 1 else 0:])
            ids = [pl.program_id(d) for d in range(3)]

            @pl.when((ids[0] == 0) & (ids[1] == 0) & (ids[2] == 0))
            def _():
                start()

        part = dot(a_ref[...].astype(BF16), b_ref[...].astype(BF16))

        def finish(val):
            if r_ref is not None:
                val = val + r_ref[...]
            o_ref[...] = val.astype(out_dtype)

        if nk == 1:
            finish(part)
        else:
            acc = scratch[0]
            l = pl.program_id(2)

            @pl.when(l == 0)
            def _():
                acc[...] = part

            @pl.when(l > 0)
            def _():
                acc[...] += part

            @pl.when(l == nk - 1)
            def _():
                finish(acc[...])

        if side is not None:
            @pl.when((ids[0] == grid[0] - 1) & (ids[1] == grid[1] - 1) & (ids[2] == grid[2] - 1))
            def _():
                finish_side()

    sems = [] if side is None else side["scratch"]
    outs = pl.pallas_call(
        body, name=name, grid=grid,
        in_specs=in_specs + [hbm] * n_side, out_specs=[o_spec] + [hbm] * n_side,
        out_shape=[S((m, n), out_dtype)] + ([] if side is None else side["out_shape"]),
        scratch_shapes=([pltpu.VMEM((tm, tn), F32)] if nk > 1 else []) + sems,
        compiler_params=pltpu.CompilerParams(
            dimension_semantics=("arbitrary",) * 3 if side is not None else ("parallel", "parallel", "arbitrary")),
    )(*args, *([] if side is None else side["arrs"]))
    return outs[0] if side is None else (outs[0], outs[1:])


def _side_parts(side, refs, n_in, n_out):
    n_side = 0 if side is None else len(side["arrs"])
    scratch = refs[n_in + n_out + 2 * n_side:]
    if side is None:
        return (lambda: None), (lambda: None), scratch
    start, finish = side["plan"](refs[n_in:n_in + n_side], refs[n_in + n_side + n_out:n_in + n_out + 2 * n_side],
                                 *scratch[len(scratch) - len(side["scratch"]):])
    return start, finish, scratch


def _proj_rms(x, g, wt, side=None):
    tm, tn = 1024, 1536
    grid = (T // tm, NCOL // tn)
    n_side = 0 if side is None else len(side["arrs"])
    hbm = pl.BlockSpec(memory_space=pl.ANY)

    def body(*refs):
        x_ref, g_ref, w_ref = refs[:3]
        o_ref, h_ref = refs[3 + n_side], refs[4 + n_side]
        start, finish, _ = _side_parts(side, refs, 3, 2)
        i, j = pl.program_id(0), pl.program_id(1)

        @pl.when((i == 0) & (j == 0))
        def _():
            start()

        @pl.when(j == 0)
        def _():
            xf = x_ref[...]
            r = lax.rsqrt(jnp.mean(xf * xf, axis=-1, keepdims=True) + EPS)
            h_ref[...] = (xf * r * g_ref[...]).astype(BF16)

        o_ref[...] = _nt(h_ref[...], w_ref[...])

        @pl.when((i == grid[0] - 1) & (j == grid[1] - 1))
        def _():
            finish()

    outs = pl.pallas_call(
        body, name="proj", grid=grid,
        in_specs=[pl.BlockSpec((tm, D), lambda i, j: (i, 0)), pl.BlockSpec((1, D), lambda i, j: (0, 0)),
                  pl.BlockSpec((tn, D), lambda i, j: (j, 0))] + [hbm] * n_side,
        out_specs=[pl.BlockSpec((tm, tn), lambda i, j: (i, j)), pl.BlockSpec((tm, D), lambda i, j: (i, 0))] + [hbm] * n_side,
        out_shape=[S((T, NCOL), F32), S((T, D), BF16)] + ([] if side is None else side["out_shape"]),
        scratch_shapes=[] if side is None else side["scratch"],
        compiler_params=pltpu.CompilerParams(dimension_semantics=("arbitrary", "arbitrary")),
    )(x, g, wt, *([] if side is None else side["arrs"]))
    return outs[0], outs[1], outs[2:]


def _dh_rms(dproj, wt, x, g, skip, side=None):
    tm, tk = 1024, 1792
    grid = (T // tm, NCOL // tk)
    n_side = 0 if side is None else len(side["arrs"])
    hbm = pl.BlockSpec(memory_space=pl.ANY)

    def body(*refs):
        a_ref, w_ref, x_ref, g_ref, s_ref = refs[:5]
        dx_ref, dg_ref = refs[5 + n_side], refs[6 + n_side]
        start, finish, scratch = _side_parts(side, refs, 5, 2)
        acc = scratch[0]
        i, l = pl.program_id(0), pl.program_id(1)

        @pl.when((i == 0) & (l == 0))
        def _():
            start()

        part = _nn(a_ref[...], w_ref[...])

        @pl.when(l == 0)
        def _():
            acc[...] = part

        @pl.when(l > 0)
        def _():
            acc[...] += part

        @pl.when(l == grid[1] - 1)
        def _():
            xf = x_ref[...]
            r = lax.rsqrt(jnp.mean(xf * xf, axis=-1, keepdims=True) + EPS)
            dn = acc[...]
            u = dn * g_ref[...]
            dx_ref[...] = s_ref[...] + r * u - xf * (r * r * r) * jnp.mean(u * xf, axis=-1, keepdims=True)
            dg = jnp.sum(dn * xf * r, axis=0, keepdims=True)

            @pl.when(i == 0)
            def _():
                dg_ref[...] = dg

            @pl.when(i > 0)
            def _():
                dg_ref[...] += dg

        @pl.when((i == grid[0] - 1) & (l == grid[1] - 1))
        def _():
            finish()

    tok = pl.BlockSpec((tm, D), lambda i, l: (i, 0))
    outs = pl.pallas_call(
        body, name="dh", grid=grid,
        in_specs=[pl.BlockSpec((tm, tk), lambda i, l: (i, l)), pl.BlockSpec((tk, D), lambda i, l: (l, 0)), tok,
                  pl.BlockSpec((1, D), lambda i, l: (0, 0)), tok] + [hbm] * n_side,
        out_specs=[tok, pl.BlockSpec((1, D), lambda i, l: (0, 0))] + [hbm] * n_side,
        out_shape=[S((T, D), F32), S((1, D), F32)] + ([] if side is None else side["out_shape"]),
        scratch_shapes=[pltpu.VMEM((tm, D), F32)] + ([] if side is None else side["scratch"]),
        compiler_params=pltpu.CompilerParams(dimension_semantics=("arbitrary", "arbitrary")),
    )(dproj, wt, x, g, skip, *([] if side is None else side["arrs"]))
    return outs[0], outs[1], outs[2:]


def _rot_tables(pos_ref, inv_ref):
    lane = lax.broadcasted_iota(jnp.int32, (1, 128), 1) % HD
    ang = pos_ref[...] * inv_ref[...]
    cos, sin = jnp.cos(ang), jnp.sin(ang)
    c = jnp.where(lane < ROT_DIM, cos, 1.0)
    sp = jnp.where((lane >= ROT_DIM // 2) & (lane < ROT_DIM), sin, 0.0)
    sm = jnp.where(lane < ROT_DIM // 2, -sin, 0.0)
    return c, sp, sm


def _head_sums(v):
    same = (lax.broadcasted_iota(jnp.int32, (128, 128), 0) < HD) == (lax.broadcasted_iota(jnp.int32, (128, 128), 1) < HD)
    ones = jnp.where(same, 1.0, 0.0).astype(BF16)
    hi = v.astype(BF16)
    lo = (v - hi.astype(F32)).astype(BF16)
    return _nn(hi, ones) + _nn(lo, ones)


def _pair_norm(t):
    return lax.rsqrt(_head_sums(t * t) * (1.0 / HD) + EPS)


def _pair_mean(t):
    return _head_sums(t) * (1.0 / HD)


TT = 256
NCH = ATT_QKV // 128


def _res_shape(grp, dtype):
    return S((DILATIONS[grp], T // DILATIONS[grp], ATT_W), dtype)


def _res_spec(grp):
    dil = DILATIONS[grp]
    return pl.BlockSpec((dil, TT // dil, ATT_W), lambda i: (0, i, 0))


def _to_residues(sc, j, dst_ref, dil, cols):
    n = TT // dil
    for r in range(dil):
        rows = sc[j] if dil == 1 else sc.at[j][pl.ds(r, n, stride=dil), :]
        dst_ref[r, :, cols] = rows.astype(dst_ref.dtype)


def _from_residues(src_ref, cols, sc, j, dil):
    n = TT // dil
    for r in range(dil):
        if dil == 1:
            sc[j] = src_ref[r, :, cols]
        else:
            sc.at[j][pl.ds(r, n, stride=dil), :] = src_ref[r, :, cols]


def _tok_spec(width, cblk=0):
    return pl.BlockSpec((TT, width), functools.partial(lambda i, c: (i, c), c=cblk))


def _const_spec(arr_or_shape):
    shape = arr_or_shape if isinstance(arr_or_shape, tuple) else arr_or_shape.shape
    return pl.BlockSpec(shape, functools.partial(lambda i, nd: (0,) * nd, nd=len(shape)))


def _qk_prep(proj, pos, inv, gq, gk):
    def body(q_ref, k_ref, v_ref, pos_ref, inv_ref, gq_ref, gk_ref, *rest):
        outs, sc = rest[:9], rest[9]
        c, sp, sm = _rot_tables(pos_ref, inv_ref)
        for which, (src, g_ref) in enumerate(((q_ref, gq_ref), (k_ref, gk_ref), (v_ref, None))):
            if g_ref is not None:
                g = jnp.broadcast_to(g_ref[...] * ((HD ** -0.5) if which == 0 else 1.0), c.shape)
                cg, spg, smg = c * g, sp * pltpu.roll(g, 8, 1), sm * pltpu.roll(g, 120, 1)
            for j in range(NCH):
                t = src[:, j * 128:(j + 1) * 128]
                if g_ref is not None:
                    t = _pair_norm(t) * (t * cg + pltpu.roll(t, 8, 1) * spg + pltpu.roll(t, 120, 1) * smg)
                sc[j] = t
            for j in range(NCH):
                grp, sub = divmod(j * 128, ATT_W)
                _to_residues(sc, j, outs[which * 3 + grp], DILATIONS[grp], slice(sub, sub + 128))

    return pl.pallas_call(
        body, name="qk_prep", grid=(T // TT,),
        in_specs=[_tok_spec(ATT_QKV, C_QA // ATT_QKV), _tok_spec(ATT_QKV, C_KA // ATT_QKV),
                  _tok_spec(ATT_QKV, C_VA // ATT_QKV), _tok_spec(1), _const_spec(inv), _const_spec(gq), _const_spec(gk)],
        out_specs=[_res_spec(g) for _ in range(3) for g in range(3)],
        out_shape=[_res_shape(g, BF16) for _ in range(3) for g in range(3)],
        scratch_shapes=[pltpu.VMEM((NCH, TT, 128), F32)],
        compiler_params=pltpu.CompilerParams(dimension_semantics=("arbitrary",)),
    )(proj, proj, proj, pos, inv, gq, gk)


def _qk_bwd(proj, pos, inv, gq, gk, dqs, dks, dvs, dproj):
    const = lambda a: pl.BlockSpec(a.shape, functools.partial(lambda i, p, nd: (0,) * nd, nd=a.ndim))
    res = lambda g: pl.BlockSpec((DILATIONS[g], TT // DILATIONS[g], ATT_W), lambda i, p: (0, i, 0))
    base = C_QA // ATT_QKV

    def body(t_ref, pos_ref, inv_ref, gq_ref, gk_ref, dq0, dq1, dq2, dk0, dk1, dk2, dv0, dv1, dv2, buf_ref,
             out_ref, dgq_ref, dgk_ref, sc):
        del buf_ref
        part = pl.program_id(1)
        first = pl.program_id(0) == 0

        def gather(drefs):
            for j in range(NCH):
                grp, sub = divmod(j * 128, ATT_W)
                _from_residues(drefs[grp], slice(sub, sub + 128), sc, j, DILATIONS[grp])

        def normed(g_ref, drefs, dg_ref):
            c, sp, sm = _rot_tables(pos_ref, inv_ref)
            gather(drefs)
            dg = jnp.zeros((1, 128), F32)
            for j in range(NCH):
                cols = slice(j * 128, (j + 1) * 128)
                d_rot = sc[j]
                dn = d_rot * c + pltpu.roll(d_rot * sp, 120, 1) + pltpu.roll(d_rot * sm, 8, 1)
                t = t_ref[:, cols]
                r = _pair_norm(t)
                u = dn * g_ref[...]
                out_ref[:, cols] = (r * u - t * (r * r * r) * _pair_mean(u * t)).astype(BF16)
                dg = dg + jnp.sum(dn * t * r, axis=0, keepdims=True)
            dg = dg + pltpu.roll(dg, HD, 1)

            @pl.when(first)
            def _():
                dg_ref[...] = dg

            @pl.when(jnp.logical_not(first))
            def _():
                dg_ref[...] += dg

        @pl.when(part == 0)
        def _():
            gather((dv0, dv1, dv2))
            for j in range(NCH):
                out_ref[:, j * 128:(j + 1) * 128] = sc[j].astype(BF16)

        @pl.when(part == 1)
        def _():
            normed(gq_ref, (dq0, dq1, dq2), dgq_ref)

        @pl.when(part == 2)
        def _():
            normed(gk_ref, (dk0, dk1, dk2), dgk_ref)

    keep = pl.BlockSpec((1, 128), lambda i, p: (0, 0))
    return pl.pallas_call(
        body, name="qk_bwd", grid=(T // TT, 3),
        in_specs=[pl.BlockSpec((TT, ATT_QKV), lambda i, p: (i, base + jnp.maximum(p - 1, 0))),
                  pl.BlockSpec((TT, 1), lambda i, p: (i, 0)), const(inv), const(gq), const(gk)]
        + [res(g) for _ in range(3) for g in range(3)] + [pl.BlockSpec(memory_space=pl.ANY)],
        out_specs=[pl.BlockSpec((TT, ATT_QKV), lambda i, p: (i, base + jnp.where(p == 0, 2, p - 1))), keep, keep],
        out_shape=[S(dproj.shape, dproj.dtype), S((1, 128), F32), S((1, 128), F32)],
        input_output_aliases={14: 0},
        scratch_shapes=[pltpu.VMEM((NCH, TT, 128), F32)],
        compiler_params=pltpu.CompilerParams(dimension_semantics=("arbitrary", "arbitrary")),
    )(proj, pos, inv, gq, gk, *dqs, *dks, *dvs, dproj)


def _split_heads(t):
    low = lax.broadcasted_iota(jnp.int32, (1, 128), 1) < HD
    zero = jnp.zeros_like(t)
    return jnp.concatenate([jnp.where(low, t, zero), jnp.where(low, zero, t)], axis=0)


def _join_heads(t2):
    low = lax.broadcasted_iota(jnp.int32, (1, 128), 1) < HD
    n = t2.shape[0] // 2
    return jnp.where(low, t2[:n], t2[n:])


def _band_mask4(has_before, has_own):
    row = lax.broadcasted_iota(jnp.int32, (BLK, 4 * BLK), 0)
    lane = lax.broadcasted_iota(jnp.int32, (BLK, 4 * BLK), 1)
    key = lane & (BLK - 1)
    own = lane >= 2 * BLK
    return (own & (key <= row) & has_own) | (jnp.logical_not(own) & (key >= row) & has_before)


def _band_mask_before(has_before):
    row = lax.broadcasted_iota(jnp.int32, (BLK, 2 * BLK), 0)
    key = lax.broadcasted_iota(jnp.int32, (BLK, 2 * BLK), 1) & (BLK - 1)
    return (key >= row) & has_before


def _per_head(width, col_a, col_b):
    lane = lax.broadcasted_iota(jnp.int32, (1, width), 1)
    return jnp.where((lane & BLK) == 0, col_a, col_b)


NQ = ATT_W // 128


def _att_fwd(q, k, v, grp, name):
    dil = DILATIONS[grp]
    nb = T // dil // BLK

    def body(q_ref, kp_ref, kc_ref, vp_ref, vc_ref, o_ref, lse_ref, s_sc, p_sc):
        mask = _band_mask4(pl.program_id(1) > 0, True)
        low = lax.broadcasted_iota(jnp.int32, (1, 128), 1) < HD
        halves = lambda ref, j, h: (ref[j, :, h * BLK:(h + 1) * BLK], ref[j, :, (h + 2) * BLK:(h + 3) * BLK])
        for j in range(NQ):
            cols = slice(j * 128, (j + 1) * 128)
            k4 = jnp.concatenate([_split_heads(kp_ref[:, cols]), _split_heads(kc_ref[:, cols])], axis=0)
            s_sc[j] = jnp.where(mask, _nt(q_ref[:, cols], k4), -jnp.inf)
        mxs = [[jnp.maximum(*(jnp.max(t, axis=-1, keepdims=True) for t in halves(s_sc, j, h))) for h in range(2)]
               for j in range(NQ)]
        dens = []
        for j in range(NQ):
            p = jnp.exp(s_sc[j] - _per_head(4 * BLK, *mxs[j]))
            p_sc[j] = p.astype(BF16)
            dens.append([jnp.sum(p[:, h * BLK:(h + 1) * BLK], axis=-1, keepdims=True)
                         + jnp.sum(p[:, (h + 2) * BLK:(h + 3) * BLK], axis=-1, keepdims=True) for h in range(2)])
        for j in range(NQ):
            cols = slice(j * 128, (j + 1) * 128)
            v4 = jnp.concatenate([_split_heads(vp_ref[:, cols]), _split_heads(vc_ref[:, cols])], axis=0)
            o_ref[:, cols] = _nn(p_sc[j], v4) / jnp.where(low, dens[j][0], dens[j][1])
            lse_ref[:, cols] = jnp.where(low, mxs[j][0] + jnp.log(dens[j][0]), mxs[j][1] + jnp.log(dens[j][1]))

    cur = pl.BlockSpec((None, BLK, ATT_W), lambda r, i: (r, i, 0))
    prev = pl.BlockSpec((None, BLK, ATT_W), lambda r, i: (r, jnp.maximum(i - 1, 0), 0))
    return pl.pallas_call(
        body, name=name, grid=(dil, nb),
        in_specs=[cur, prev, cur, prev, cur],
        out_specs=[cur, cur], out_shape=[_res_shape(grp, F32)] * 2,
        scratch_shapes=[pltpu.VMEM((NQ, BLK, 4 * BLK), F32), pltpu.VMEM((NQ, BLK, 4 * BLK), BF16)],
        compiler_params=pltpu.CompilerParams(dimension_semantics=("parallel", "arbitrary")),
    )(q, k, k, v, v)


def _att_bwd(q, k, v, datt, att, lse, grp, name):
    dil = DILATIONS[grp]
    nb = T // dil // BLK
    scale = HD ** -0.5

    def body(q0_ref, q1_ref, kp_ref, kc_ref, vp_ref, vc_ref, do0_ref, do1_ref, o0_ref, o1_ref, l0_ref, l1_ref,
             dq_ref, dk_ref, dv_ref, k4_sc, v4_sc, s0_sc, s1_sc, dp0_sc, dp1_sc, p_sc, ds_sc):
        i = pl.program_id(1)
        mask_mine = _band_mask4(i > 0, True)
        mask_next = _band_mask_before(i < nb - 1)
        low = lax.broadcasted_iota(jnp.int32, (1, 128), 1) < HD
        for j in range(NQ):
            cols = slice(j * 128, (j + 1) * 128)
            k4_sc[j, :2 * BLK] = _split_heads(kp_ref[:, cols])
            k4_sc[j, 2 * BLK:] = _split_heads(kc_ref[:, cols])
            v4_sc[j, :2 * BLK] = _split_heads(vp_ref[:, cols])
            v4_sc[j, 2 * BLK:] = _split_heads(vc_ref[:, cols])
        for j in range(NQ):
            cols = slice(j * 128, (j + 1) * 128)
            s0_sc[j] = _nt(q0_ref[:, cols], k4_sc[j])
            s1_sc[j] = _nt(q1_ref[:, cols], k4_sc[j, 2 * BLK:])
            dp0_sc[j] = _nt(do0_ref[:, cols].astype(BF16), v4_sc[j])
            dp1_sc[j] = _nt(do1_ref[:, cols].astype(BF16), v4_sc[j, 2 * BLK:])
        stats = []
        for j in range(NQ):
            cols = slice(j * 128, (j + 1) * 128)
            for do_ref, o_ref, l_ref in ((do0_ref, o0_ref, l0_ref), (do1_ref, o1_ref, l1_ref)):
                prod = do_ref[:, cols].astype(F32) * o_ref[:, cols].astype(F32)
                d_all = jnp.sum(prod, axis=-1, keepdims=True)
                d_low = jnp.sum(jnp.where(low, prod, 0.0), axis=-1, keepdims=True)
                lse_t = l_ref[:, cols]
                stats.append((d_low, d_all - d_low, lse_t[:, 0:1], lse_t[:, HD:HD + 1]))
        for j in range(NQ):
            (da, db, la, lb), (da1, db1, la1, lb1) = stats[2 * j], stats[2 * j + 1]
            p0 = jnp.where(mask_mine, jnp.exp(s0_sc[j] - _per_head(4 * BLK, la, lb)), 0.0)
            ds0 = p0 * (dp0_sc[j] - _per_head(4 * BLK, da, db))
            p1 = jnp.where(mask_next, jnp.exp(s1_sc[j] - _per_head(2 * BLK, la1, lb1)), 0.0)
            ds1 = p1 * (dp1_sc[j] - _per_head(2 * BLK, da1, db1))
            p_sc[j, :BLK] = p0.astype(BF16)
            ds_sc[j, :BLK] = ds0.astype(BF16)
            p_sc[j, BLK:, 2 * BLK:] = p1.astype(BF16)
            ds_sc[j, BLK:, 2 * BLK:] = ds1.astype(BF16)
        for j in range(NQ):
            cols = slice(j * 128, (j + 1) * 128)
            dq_ref[:, cols] = _nn(ds_sc[j, :BLK], k4_sc[j]) * scale
            qq = jnp.concatenate([q0_ref[:, cols], q1_ref[:, cols]], axis=0)
            dd = jnp.concatenate([do0_ref[:, cols], do1_ref[:, cols]], axis=0).astype(BF16)
            dk_ref[:, cols] = _join_heads(_tn(ds_sc[j, :, 2 * BLK:], qq))
            dv_ref[:, cols] = _join_heads(_tn(p_sc[j, :, 2 * BLK:], dd))

    def spec(shift):
        return pl.BlockSpec((None, BLK, ATT_W), lambda r, i: (r, jnp.clip(i + shift, 0, nb - 1), 0))

    here, after, before = spec(0), spec(1), spec(-1)
    vm = pltpu.VMEM
    return pl.pallas_call(
        body, name=name, grid=(dil, nb),
        in_specs=[here, after, before, here, before, here, here, after, here, after, here, after],
        out_specs=[here] * 3, out_shape=[_res_shape(grp, F32)] * 3,
        scratch_shapes=[vm((NQ, 4 * BLK, 128), BF16), vm((NQ, 4 * BLK, 128), BF16), vm((NQ, BLK, 4 * BLK), F32),
                        vm((NQ, BLK, 2 * BLK), F32), vm((NQ, BLK, 4 * BLK), F32), vm((NQ, BLK, 2 * BLK), F32),
                        vm((NQ, 2 * BLK, 4 * BLK), BF16), vm((NQ, 2 * BLK, 4 * BLK), BF16)],
        compiler_params=pltpu.CompilerParams(dimension_semantics=("parallel", "arbitrary")),
    )(q, q, k, k, v, v, datt, datt, att, att, lse, lse)


def _att_merge(os_, lses, proj):
    nq = ATT_W // 128

    def body(o0, o1, o2, l0, l1, l2, za_ref, att_ref, lse_ref, ain_ref, sc):
        for a, ref in enumerate((o0, o1, o2, l0, l1, l2)):
            for j in range(nq):
                _from_residues(ref, slice(j * 128, (j + 1) * 128), sc, a * nq + j, DILATIONS[a % 3])
        for j in range(nq):
            cols = slice(j * 128, (j + 1) * 128)
            oa, ob, oc = (sc[a * nq + j] for a in range(3))
            la, lb, lc = (sc[(3 + a) * nq + j] for a in range(3))
            m = jnp.maximum(jnp.maximum(la, lb), lc)
            wa, wb, wc = jnp.exp(la - m), jnp.exp(lb - m), jnp.exp(lc - m)
            tot = wa + wb + wc
            att = (wa * oa + wb * ob + wc * oc) / tot
            att_ref[:, cols] = att
            lse_ref[:, cols] = m + jnp.log(tot)
            za = za_ref[:, cols]
            ain_ref[:, cols] = (att * za * _sigmoid(za)).astype(BF16)

    return pl.pallas_call(
        body, name="att_merge", grid=(T // TT,),
        in_specs=[_res_spec(g) for _ in range(2) for g in range(3)] + [_tok_spec(ATT_W, C_ZA // ATT_W)],
        out_specs=[_tok_spec(ATT_W)] * 3,
        out_shape=[S((T, ATT_W), F32), S((T, ATT_W), F32), S((T, ATT_W), BF16)],
        scratch_shapes=[pltpu.VMEM((6 * nq, TT, 128), F32)],
        compiler_params=pltpu.CompilerParams(dimension_semantics=("arbitrary",)),
    )(*os_, *lses, proj)


def _att_gate_bwd(dain, att, lse, proj, dproj):
    nq = ATT_W // 128

    def body(d_ref, att_ref, lse_ref, za_ref, buf_ref, dza_ref, da0, da1, da2, at1, at2, ls1, ls2, sc):
        del buf_ref
        for j in range(nq):
            cols = slice(j * 128, (j + 1) * 128)
            za = za_ref[:, cols]
            sg = _sigmoid(za)
            d = d_ref[:, cols].astype(F32)
            att_ = att_ref[:, cols]
            dza_ref[:, cols] = (d * att_ * sg * (1.0 + za * (1.0 - sg))).astype(BF16)
            sc[j] = d * za * sg
            sc[nq + j] = att_
            sc[2 * nq + j] = lse_ref[:, cols]
        for j in range(nq):
            cols = slice(j * 128, (j + 1) * 128)
            for grp, dst in enumerate((da0, da1, da2)):
                _to_residues(sc, j, dst, DILATIONS[grp], cols)
            for grp, dst in ((1, at1), (2, at2)):
                _to_residues(sc, nq + j, dst, DILATIONS[grp], cols)
            for grp, dst in ((1, ls1), (2, ls2)):
                _to_residues(sc, 2 * nq + j, dst, DILATIONS[grp], cols)

    res = (0, 1, 2, 1, 2, 1, 2)
    return pl.pallas_call(
        body, name="att_gate_bwd", grid=(T // TT,),
        in_specs=[_tok_spec(ATT_W)] * 3 + [_tok_spec(ATT_W, C_ZA // ATT_W), pl.BlockSpec(memory_space=pl.ANY)],
        out_specs=[_tok_spec(ATT_W, C_ZA // ATT_W)] + [_res_spec(g) for g in res],
        out_shape=[S(dproj.shape, dproj.dtype)] + [_res_shape(g, BF16) for g in res[:5]]
        + [_res_shape(g, F32) for g in res[5:]],
        input_output_aliases={4: 0},
        scratch_shapes=[pltpu.VMEM((3 * nq, TT, 128), F32)],
        compiler_params=pltpu.CompilerParams(dimension_semantics=("arbitrary",)),
    )(dain, att, lse, proj, dproj)


def _split3(v):
    hi = v.astype(BF16)
    r1 = v - hi.astype(F32)
    mid = r1.astype(BF16)
    lo = (r1 - mid.astype(F32)).astype(BF16)
    return hi, mid, lo


def _tri_sum(v, upper):
    n = v.shape[0]
    row = lax.broadcasted_iota(jnp.int32, (n, n), 0)
    col = lax.broadcasted_iota(jnp.int32, (n, n), 1)
    tri = jnp.where(col >= row if upper else col <= row, 1.0, 0.0).astype(BF16)
    hi, mid, lo = _split3(v)
    return _nn(tri, hi) + _nn(tri, mid) + _nn(tri, lo)


def _gla_gates(glr_ref, w2_ref, b_ref):
    logit = _nn(glr_ref[...].astype(BF16), w2_ref[...]) + b_ref[...]
    lg = (jnp.minimum(logit, 0.0) - jnp.log(1.0 + jnp.exp(-jnp.abs(logit)))) * (1.0 / GLA_TAU)
    return logit, _tri_sum(lg, upper=False)


def _gla_head(cum, q_ref, k_ref, h):
    cols = slice(h * GDK, (h + 1) * GDK)
    b = cum[:, cols]
    last = b[GLA_C - 1:GLA_C, :]
    e_pos = jnp.exp(b)
    e_neg = jnp.exp(-b)
    e_end = jnp.exp(last - b)
    qt = q_ref[:, cols] * (GDK ** -0.5) * e_pos
    kt = k_ref[:, cols] * e_neg
    kh = k_ref[:, cols] * e_end
    return b, last, e_pos, e_neg, e_end, qt, kt, kh


def _causal(n):
    return lax.broadcasted_iota(jnp.int32, (n, n), 1) <= lax.broadcasted_iota(jnp.int32, (n, n), 0)


def _gla_fwd(proj, w2p, bg, gn):
    nc = T // GLA_C

    def body(q_ref, k_ref, v_ref, glr_ref, zg_ref, w2_ref, b_ref, gn_ref, o_ref, bin_ref, st_ref, state):
        @pl.when(pl.program_id(0) == 0)
        def _():
            state[...] = jnp.zeros_like(state)

        _, cum = _gla_gates(glr_ref, w2_ref, b_ref)
        for h in range(GH):
            _, last, _, _, _, qt, kt, kh = _gla_head(cum, q_ref, k_ref, h)
            vcols = slice(h * GDV, (h + 1) * GDV)
            st = state[h]
            st_ref[0, h] = st
            v = v_ref[:, vcols].astype(BF16)
            qb = qt.astype(BF16)
            a = jnp.where(_causal(GLA_C), _nt(qb, kt.astype(BF16)), 0.0)
            o = _nt(qb, st.astype(BF16)) + _nn(a.astype(BF16), v)
            state[h] = st * jnp.exp(last) + _tn(v, kh.astype(BF16))
            o_ref[:, vcols] = o
            r = lax.rsqrt(jnp.mean(o * o, axis=-1, keepdims=True) + EPS)
            zg = zg_ref[:, vcols]
            bin_ref[:, vcols] = (o * r * gn_ref[...] * zg * _sigmoid(zg)).astype(BF16)

    row = lambda width, cblk: pl.BlockSpec((GLA_C, width), functools.partial(lambda i, c: (i, c), c=cblk))
    full = lambda a: pl.BlockSpec(a.shape, functools.partial(lambda i, nd: (0,) * nd, nd=a.ndim))
    return pl.pallas_call(
        body, name="gla_fwd", grid=(nc,),
        in_specs=[row(512, C_QG // 512), row(512, C_KG // 512), row(1024, C_VG // 1024), row(GLR_W, C_GLR // GLR_W),
                  row(1024, C_ZG // 1024), full(w2p), full(bg), full(gn)],
        out_specs=[pl.BlockSpec((GLA_C, GH * GDV), lambda i: (i, 0)), pl.BlockSpec((GLA_C, GH * GDV), lambda i: (i, 0)),
                   pl.BlockSpec((1, GH, GDV, GDK), lambda i: (i, 0, 0, 0))],
        out_shape=[S((T, GH * GDV), F32), S((T, GH * GDV), BF16), S((nc, GH, GDV, GDK), F32)],
        scratch_shapes=[pltpu.VMEM((GH, GDV, GDK), F32)],
        compiler_params=pltpu.CompilerParams(dimension_semantics=("arbitrary",)),
    )(proj, proj, proj, proj, proj, w2p, bg, gn)


def _gla_bwd(proj, w2p, bg, gn, o_gla, states, dbin, dproj):
    nc = T // GLA_C

    def body(q_ref, k_ref, v_ref, glr_ref, zg_ref, w2_ref, b_ref, gn_ref, o_ref, st_ref, dbin_ref, buf_ref,
             out_ref, dw2_ref, dbg_ref, dgn_ref, dstate, dlogit):
        del buf_ref
        dq_ref = out_ref.at[:, C_QG:C_KG]
        dk_ref = out_ref.at[:, C_KG:C_VG]
        dv_ref = out_ref.at[:, C_VG:C_ZG]
        dzg_ref = out_ref.at[:, C_ZG:C_GLR]
        dglr_ref = out_ref.at[:, C_GLR:C_GLR + GLR_W]
        first = pl.program_id(0) == 0

        @pl.when(first)
        def _():
            dstate[...] = jnp.zeros_like(dstate)

        logit, cum = _gla_gates(glr_ref, w2_ref, b_ref)
        is_last = lax.broadcasted_iota(jnp.int32, (GLA_C, 1), 0) == GLA_C - 1
        dgn = jnp.zeros((1, GDV), F32)
        for h in range(GH):
            _, last, e_pos, e_neg, e_end, qt, kt, kh = _gla_head(cum, q_ref, k_ref, h)
            cols = slice(h * GDK, (h + 1) * GDK)
            vcols = slice(h * GDV, (h + 1) * GDV)
            o = o_ref[:, vcols]
            r = lax.rsqrt(jnp.mean(o * o, axis=-1, keepdims=True) + EPS)
            zg = zg_ref[:, vcols]
            sg = _sigmoid(zg)
            db_ = dbin_ref[:, vcols].astype(F32)
            dlin = db_ * zg * sg
            dzg_ref[:, vcols] = (db_ * (o * r * gn_ref[...]) * sg * (1.0 + zg * (1.0 - sg))).astype(BF16)
            u = dlin * gn_ref[...]
            do = (r * u - o * (r * r * r) * jnp.mean(u * o, axis=-1, keepdims=True)).astype(BF16)
            dgn = dgn + jnp.sum(dlin * o * r, axis=0, keepdims=True)
            st = st_ref[0, h]
            dst = dstate[h]
            v = v_ref[:, vcols].astype(BF16)
            qb, kb, khb = qt.astype(BF16), kt.astype(BF16), kh.astype(BF16)
            dstb = dst.astype(BF16)
            causal = _causal(GLA_C)
            a = jnp.where(causal, _nt(qb, kb), 0.0).astype(BF16)
            da = jnp.where(causal, _nt(do, v), 0.0).astype(BF16)
            dqt = _nn(do, st.astype(BF16)) + _nn(da, kb)
            dkt = _tn(da, qb)
            dkh = _nn(v, dstb)
            dv_ref[:, vcols] = (_tn(a, do) + _nt(khb, dstb)).astype(BF16)
            lam = jnp.exp(last)
            dlam = jnp.sum(dst * st, axis=0, keepdims=True)
            dstate[h] = dst * lam + _tn(do, qb)
            dq_ref[:, cols] = (dqt * e_pos * (GDK ** -0.5)).astype(BF16)
            dk_ref[:, cols] = (dkt * e_neg + dkh * e_end).astype(BF16)
            dkh_kh = dkh * kh
            dcum = dqt * qt - dkt * kt - dkh_kh
            dlast = jnp.sum(dkh_kh, axis=0, keepdims=True) + dlam * lam
            dcum = jnp.where(is_last, dcum + dlast, dcum)
            dlg = _tri_sum(dcum, upper=True)
            dlogit[:, cols] = dlg * (1.0 / GLA_TAU) * (1.0 - _sigmoid(logit[:, cols]))

        dl = dlogit[...]
        dlb = dl.astype(BF16)
        dglr_ref[...] = _nt(dlb, w2_ref[...]).astype(BF16)
        dw2 = _tn(glr_ref[...].astype(BF16), dlb)
        dbg = jnp.sum(dl, axis=0, keepdims=True)

        @pl.when(first)
        def _():
            dw2_ref[...] = dw2
            dbg_ref[...] = dbg
            dgn_ref[...] = dgn

        @pl.when(jnp.logical_not(first))
        def _():
            dw2_ref[...] += dw2
            dbg_ref[...] += dbg
            dgn_ref[...] += dgn

    rev = lambda i: nc - 1 - i
    row = lambda width, cblk: pl.BlockSpec((GLA_C, width), functools.partial(lambda i, c: (rev(i), c), c=cblk))
    full = lambda a: pl.BlockSpec(a.shape, functools.partial(lambda i, nd: (0,) * nd, nd=a.ndim))
    keep = lambda shape: pl.BlockSpec(shape, functools.partial(lambda i, nd: (0,) * nd, nd=len(shape)))
    return pl.pallas_call(
        body, name="gla_bwd", grid=(nc,),
        in_specs=[row(512, C_QG // 512), row(512, C_KG // 512), row(1024, C_VG // 1024), row(GLR_W, C_GLR // GLR_W),
                  row(1024, C_ZG // 1024), full(w2p), full(bg), full(gn), row(GH * GDV, 0),
                  pl.BlockSpec((1, GH, GDV, GDK), lambda i: (rev(i), 0, 0, 0)), row(GH * GDV, 0),
                  pl.BlockSpec(memory_space=pl.ANY)],
        out_specs=[row(GLA_GROUP_W, 0), keep((GLR_W, 512)), keep((1, 512)), keep((1, GDV))],
        out_shape=[S(dproj.shape, dproj.dtype), S((GLR_W, 512), F32), S((1, 512), F32), S((1, GDV), F32)],
        input_output_aliases={11: 0},
        scratch_shapes=[pltpu.VMEM((GH, GDV, GDK), F32), pltpu.VMEM((GLA_C, GH * GDK), F32)],
        compiler_params=pltpu.CompilerParams(dimension_semantics=("arbitrary",)),
    )(proj, proj, proj, proj, proj, w2p, bg, gn, o_gla, states, dbin, dproj)


RT = 512


def _rowchain(body, name, ins, outs, scratch=()):
    in_specs, args = [], []
    for spec in ins:
        if spec[0] == "tok":
            _, arr, width, cblk = spec
            in_specs.append(pl.BlockSpec((RT, width), functools.partial(lambda i, c: (i, c), c=cblk)))
        else:
            arr = spec[1]
            in_specs.append(pl.BlockSpec(arr.shape, functools.partial(lambda i, nd: (0,) * nd, nd=arr.ndim)))
        args.append(arr)
    out_specs, out_shape = [], []
    for spec in outs:
        if spec[0] == "tok":
            _, shape, dtype, width, cblk = spec
            out_specs.append(pl.BlockSpec((RT, width), functools.partial(lambda i, c: (i, c), c=cblk)))
        else:
            _, shape, dtype = spec
            out_specs.append(pl.BlockSpec(shape, functools.partial(lambda i, nd: (0,) * nd, nd=len(shape))))
        out_shape.append(S(shape, dtype))
    return pl.pallas_call(
        body, name=name, grid=(T // RT,), in_specs=in_specs, out_specs=out_specs, out_shape=out_shape,
        scratch_shapes=list(scratch), compiler_params=pltpu.CompilerParams(dimension_semantics=("arbitrary",)),
    )(*args)


def _tok(arr, width=None, cblk=0):
    return ("tok", arr, arr.shape[1] if width is None else width, cblk)


def _tok_out(dtype, width=D):
    return ("tok", (T, width), dtype, width, 0)


def _branches_fwd(ain, bin_, proj, x, w_att, w_gla, w_out):
    def body(ain_ref, bin_ref, g_ref, x_ref, wa_ref, wg_ref, wo_ref, ya_ref, yb_ref, y_ref, x1_ref):
        ya = _nn(ain_ref[...], wa_ref[...]).astype(BF16)
        yb = _nn(bin_ref[...], wg_ref[...]).astype(BF16)
        ya_ref[...] = ya
        yb_ref[...] = yb
        y = (_sigmoid(g_ref[:, :D]) * ya.astype(F32) + _sigmoid(g_ref[:, D:]) * yb.astype(F32)).astype(BF16)
        y_ref[...] = y
        x1_ref[...] = x_ref[...] + _nn(y, wo_ref[...])

    return _rowchain(body, "branches_fwd",
                     [_tok(ain), _tok(bin_), _tok(proj, 2 * D, C_GA // (2 * D)), _tok(x), ("all", w_att),
                      ("all", w_gla), ("all", w_out)],
                     [_tok_out(BF16), _tok_out(BF16), _tok_out(BF16), _tok_out(F32)])


def _accumulate(ref, part, first):
    @pl.when(first)
    def _():
        ref[...] = part

    @pl.when(jnp.logical_not(first))
    def _():
        ref[...] += part


def _ple_loss(x1, p, target, g2, w_pg, w_ple):
    def body(x1_ref, p_ref, t_ref, g_ref, wpg_ref, wple_ref, n2_ref, loss_ref, dout_ref, du_ref, dwple_ref, acc):
        first = pl.program_id(0) == 0
        x1 = x1_ref[...]
        r = lax.rsqrt(jnp.mean(x1 * x1, axis=-1, keepdims=True) + EPS)
        n2 = (x1 * r * g_ref[...]).astype(BF16)
        n2_ref[...] = n2
        pg = _sigmoid(_nn(n2, wpg_ref[...]))
        pb = p_ref[...].astype(BF16)
        e_ = _nn(pb, wple_ref[...])
        diff = x1 + e_ * pg - t_ref[...]
        _accumulate(acc, jnp.sum(diff * diff, axis=0, keepdims=True), first)
        dout = diff * (1.0 / D)
        dout_ref[...] = dout
        du_ref[...] = (dout * e_ * pg * (1.0 - pg)).astype(BF16)
        _accumulate(dwple_ref, _tn(pb, (dout * pg).astype(BF16)), first)
        loss_ref[...] = jnp.zeros((1, 128), F32) + jnp.sum(acc[...], axis=-1, keepdims=True) * (0.5 / D)

    return _rowchain(body, "ple_loss", [_tok(x1), _tok(p), _tok(target), ("all", g2), ("all", w_pg), ("all", w_ple)],
                     [_tok_out(BF16), ("acc", (1, 128), F32), _tok_out(F32), _tok_out(BF16), ("acc", (PLE, D), F32)],
                     scratch=[pltpu.VMEM((1, D), F32)])


def _ple_bwd(du, n2, y, x1, dout, g2, w_pg, w_out):
    def body(du_ref, n2_ref, y_ref, x1_ref, dout_ref, g_ref, wpg_ref, wo_ref, dx_ref, dy_ref, dg_ref, dwpg_ref,
             dwo_ref):
        first = pl.program_id(0) == 0
        x1 = x1_ref[...]
        r = lax.rsqrt(jnp.mean(x1 * x1, axis=-1, keepdims=True) + EPS)
        du_ = du_ref[...]
        dn = _nt(du_, wpg_ref[...])
        u = dn * g_ref[...]
        dx = dout_ref[...] + r * u - x1 * (r * r * r) * jnp.mean(u * x1, axis=-1, keepdims=True)
        dxb = dx.astype(BF16)
        dx_ref[...] = dx
        dy_ref[...] = _nt(dxb, wo_ref[...]).astype(BF16)
        _accumulate(dg_ref, jnp.sum(dn * x1 * r, axis=0, keepdims=True), first)
        _accumulate(dwpg_ref, _tn(n2_ref[...], du_), first)
        _accumulate(dwo_ref, _tn(y_ref[...], dxb), first)

    return _rowchain(body, "ple_bwd",
                     [_tok(du), _tok(n2), _tok(y), _tok(x1), _tok(dout), ("all", g2), ("all", w_pg), ("all", w_out)],
                     [_tok_out(F32), _tok_out(BF16), ("acc", (1, D), F32), ("acc", (D, D), F32), ("acc", (D, D), F32)])


def _branches_bwd(dy, ya, yb, ain, bin_, proj, w_att, w_gla):
    def body(dy_ref, ya_ref, yb_ref, ain_ref, bin_ref, g_ref, wa_ref, wg_ref, dg_ref, dain_ref, dbin_ref,
             dwa_ref, dwg_ref):
        first = pl.program_id(0) == 0
        dy_ = dy_ref[...].astype(F32)
        sa, sb = _sigmoid(g_ref[:, :D]), _sigmoid(g_ref[:, D:])
        dg_ref[:, :D] = (dy_ * ya_ref[...].astype(F32) * sa * (1.0 - sa)).astype(BF16)
        dg_ref[:, D:] = (dy_ * yb_ref[...].astype(F32) * sb * (1.0 - sb)).astype(BF16)
        dya = (dy_ * sa).astype(BF16)
        dyb = (dy_ * sb).astype(BF16)
        dain_ref[...] = _nt(dya, wa_ref[...]).astype(BF16)
        dbin_ref[...] = _nt(dyb, wg_ref[...]).astype(BF16)
        _accumulate(dwa_ref, _tn(ain_ref[...], dya), first)
        _accumulate(dwg_ref, _tn(bin_ref[...], dyb), first)

    gates = C_GA // (2 * D)
    return _rowchain(body, "branches_bwd",
                     [_tok(dy), _tok(ya), _tok(yb), _tok(ain), _tok(bin_), _tok(proj, 2 * D, gates), ("all", w_att),
                      ("all", w_gla)],
                     [("tok", (T, NCOL), BF16, 2 * D, gates), _tok_out(BF16, ATT_W), _tok_out(BF16),
                      ("acc", (ATT_W, D), F32), ("acc", (D, D), F32)])


def _peer(k):
    x, y, c = lax.axis_index("x"), lax.axis_index("y"), lax.axis_index("c")
    return (x ^ ((k >> 2) & 1), y ^ ((k >> 1) & 1), c ^ (k & 1))


def _my_index():
    return 4 * lax.axis_index("x") + 2 * lax.axis_index("y") + lax.axis_index("c")


def _peer_index(k):
    px, py, pc = _peer(k)
    return 4 * px + 2 * py + pc


def _pairwise_plan(src_of, dst_of, landed_of, own_src, own_dst):
    def plan(ins, outs, send, recv, local):
        n = len(ins)

        def own():
            return [pltpu.make_async_copy(own_src(ins[a]), own_dst(outs[a]), local.at[a]) for a in range(n)]

        def remote(k, a, src, dst):
            return pltpu.make_async_remote_copy(src_ref=src, dst_ref=dst, send_sem=send.at[k - 1, a],
                                                recv_sem=recv.at[k - 1, a], device_id=_peer(k), device_id_type=MESH)

        def sent():
            return [remote(k, a, src_of(ins[a], k), dst_of(outs[a])) for k in range(1, NDEV) for a in range(n)]

        def start():
            for cp in own() + sent():
                cp.start()

        def finish():
            for k in range(1, NDEV):
                for a in range(n):
                    remote(k, a, own_src(ins[a]), landed_of(outs[a], k)).wait_recv()
            for cp in sent():
                cp.wait_send()
            for cp in own():
                cp.wait()

        return start, finish

    return plan


def _pairwise_sems(n):
    return [pltpu.SemaphoreType.DMA((NDEV - 1, n)), pltpu.SemaphoreType.DMA((NDEV - 1, n)),
            pltpu.SemaphoreType.DMA((n,))]


def _gather_side(arrs):
    plan = _pairwise_plan(src_of=lambda i, k: i, dst_of=lambda o: o.at[_my_index()],
                          landed_of=lambda o, k: o.at[_peer_index(k)],
                          own_src=lambda i: i, own_dst=lambda o: o.at[_my_index()])
    return dict(arrs=arrs, out_shape=[S((NDEV,) + a.shape, a.dtype) for a in arrs],
                scratch=_pairwise_sems(len(arrs)), plan=plan)


def _exchange_side(arrs):
    plan = _pairwise_plan(src_of=lambda i, k: i.at[_peer_index(k)], dst_of=lambda o: o.at[_my_index()],
                          landed_of=lambda o, k: o.at[_peer_index(k)],
                          own_src=lambda i: i.at[_my_index()], own_dst=lambda o: o.at[_my_index()])
    return dict(arrs=arrs, out_shape=[S(a.shape, a.dtype) for a in arrs], scratch=_pairwise_sems(len(arrs)), plan=plan)


def _comm_call(side, name):
    n = len(side["arrs"])

    def body(*refs):
        start, finish = side["plan"](refs[:n], refs[n:2 * n], *refs[2 * n:])
        start()
        finish()

    hbm = pl.BlockSpec(memory_space=pl.ANY)
    return pl.pallas_call(body, name=name, in_specs=[hbm] * n, out_specs=[hbm] * n, out_shape=side["out_shape"],
                          scratch_shapes=side["scratch"])(*side["arrs"])


def _all_gather_by_chip(arrs, name):
    n = len(arrs)

    def body(*refs):
        ins, outs = refs[:n], refs[n:2 * n]
        send, recv, local = refs[2 * n:]
        x, y, c = lax.axis_index("x"), lax.axis_index("y"), lax.axis_index("c")
        me, sibling = (x, y, c), (x, y, 1 - c)
        chips = [(1 - x, y), (x, 1 - y), (1 - x, 1 - y)]

        def copy(k, a, block, to, src=None):
            px, py, pc = block
            slot = outs[a].at[4 * px + 2 * py + pc]
            return pltpu.make_async_remote_copy(
                src_ref=slot if src is None else src, dst_ref=slot, send_sem=send.at[k, a], recv_sem=recv.at[k, a],
                device_id=to, device_id_type=MESH)

        mine = [pltpu.make_async_copy(ins[a], outs[a].at[4 * x + 2 * y + c], local.at[a]) for a in range(n)]
        first = []
        for a in range(n):
            first.append(copy(0, a, me, sibling, src=ins[a]))
            first += [copy(1 + j, a, me, (*chip, c), src=ins[a]) for j, chip in enumerate(chips)]
        for cp in mine + first:
            cp.start()
        passed = []
        for j, chip in enumerate(chips):
            for a in range(n):
                copy(1 + j, a, (*chip, c), me).wait_recv()
                passed.append(copy(4 + j, a, (*chip, c), sibling))
                passed[-1].start()
        for a in range(n):
            copy(0, a, sibling, me).wait_recv()
        for j, chip in enumerate(chips):
            for a in range(n):
                copy(4 + j, a, (*chip, 1 - c), me).wait_recv()
        for cp in first + passed:
            cp.wait_send()
        for cp in mine:
            cp.wait()

    hbm = pl.BlockSpec(memory_space=pl.ANY)
    return pl.pallas_call(
        body, name=name, in_specs=[hbm] * n, out_specs=[hbm] * n,
        out_shape=[S((NDEV,) + a.shape, a.dtype) for a in arrs],
        scratch_shapes=[pltpu.SemaphoreType.DMA((NDEV - 1, n)), pltpu.SemaphoreType.DMA((NDEV - 1, n)),
                        pltpu.SemaphoreType.DMA((n,))],
    )(*arrs)


NCHIP = 4


def _exchange_sibling(arrs, name):
    n = len(arrs)

    def body(*refs):
        ins, outs = refs[:n], refs[n:2 * n]
        send, recv = refs[2 * n:]
        x, y, c = lax.axis_index("x"), lax.axis_index("y"), lax.axis_index("c")
        copies = []
        for q in range(NCHIP):
            for a in range(n):
                copies.append(pltpu.make_async_remote_copy(
                    src_ref=ins[a].at[2 * q + (1 - c)], dst_ref=outs[a].at[q], send_sem=send.at[q, a],
                    recv_sem=recv.at[q, a], device_id=(x, y, 1 - c), device_id_type=MESH))
        for cp in copies:
            cp.start()
        for cp in copies:
            cp.wait_recv()
        for cp in copies:
            cp.wait_send()

    hbm = pl.BlockSpec(memory_space=pl.ANY)
    return pl.pallas_call(
        body, name=name, in_specs=[hbm] * n, out_specs=[hbm] * n,
        out_shape=[S((NCHIP,) + a.shape[1:], a.dtype) for a in arrs],
        scratch_shapes=[pltpu.SemaphoreType.DMA((NCHIP, n)), pltpu.SemaphoreType.DMA((NCHIP, n))],
    )(*arrs)


def _pair_add(mine, got, core, name):
    _, rows, cols = mine.shape
    tc = 256
    assert cols % tc == 0

    def body(core_ref, a_ref, b_ref, o_ref):
        o_ref[...] = (a_ref[...].astype(F32) + b_ref[...].astype(F32)).astype(BF16)

    return pl.pallas_call(
        body, name=name,
        grid_spec=pltpu.PrefetchScalarGridSpec(
            num_scalar_prefetch=1, grid=(NCHIP, cols // tc),
            in_specs=[pl.BlockSpec((None, rows, tc), lambda q, i, core_ref: (2 * q + core_ref[0], 0, i)),
                      pl.BlockSpec((None, rows, tc), lambda q, i, core_ref: (q, 0, i))],
            out_specs=pl.BlockSpec((None, rows, tc), lambda q, i, core_ref: (q, 0, i))),
        out_shape=S((NCHIP, rows, cols), BF16),
    )(core, mine, got)


def _chips_side(arrs):
    def plan(ins, outs, send, recv, local):
        n = len(ins)

        def places():
            x, y, c = lax.axis_index("x"), lax.axis_index("y"), lax.axis_index("c")
            return 2 * x + y, c, [(1 - x, y), (x, 1 - y), (1 - x, 1 - y)]

        def own():
            here, _, _ = places()
            return [pltpu.make_async_copy(ins[a].at[here], outs[a].at[here], local.at[a]) for a in range(n)]

        def remote(j, a, src_slot, dst_slot):
            _, c, chips = places()
            cx, cy = chips[j]
            return pltpu.make_async_remote_copy(
                src_ref=ins[a].at[src_slot], dst_ref=outs[a].at[dst_slot], send_sem=send.at[j, a],
                recv_sem=recv.at[j, a], device_id=(cx, cy, c), device_id_type=MESH)

        def sent():
            here, _, chips = places()
            return [remote(j, a, 2 * cx + cy, here) for j, (cx, cy) in enumerate(chips) for a in range(n)]

        def start():
            for cp in own() + sent():
                cp.start()

        def finish():
            here, _, chips = places()
            for j, (cx, cy) in enumerate(chips):
                for a in range(n):
                    remote(j, a, here, 2 * cx + cy).wait_recv()
            for cp in sent():
                cp.wait_send()
            for cp in own():
                cp.wait()

        return start, finish

    n = len(arrs)
    return dict(arrs=arrs, out_shape=[S(a.shape, a.dtype) for a in arrs],
                scratch=[pltpu.SemaphoreType.DMA((NCHIP - 1, n)), pltpu.SemaphoreType.DMA((NCHIP - 1, n)),
                         pltpu.SemaphoreType.DMA((n,))], plan=plan)


def _adamw(parts, w, m, v, name, tr, tc=None):
    rows, cols = w.shape
    if tc is None:
        assert rows % tr == 0
        grid, shape, at = (rows // tr,), (tr, cols), (lambda i: (i, 0))
    else:
        assert cols % tc == 0
        grid, shape, at = (cols // tc,), (rows, tc), (lambda i: (0, i))
    c1 = 1.0 - ADAM_B1 ** ADAM_STEP
    c2 = 1.0 - ADAM_B2 ** ADAM_STEP

    nparts = parts.shape[0]

    def body(p_ref, w_ref, m_ref, v_ref, g_ref, d_ref, mo_ref, vo_ref):
        g = p_ref[0].astype(F32)
        for s in range(1, nparts):
            g = g + p_ref[s].astype(F32)
        m_new = ADAM_B1 * m_ref[...] + (1.0 - ADAM_B1) * g
        v_new = ADAM_B2 * v_ref[...] + (1.0 - ADAM_B2) * (g * g)
        g_ref[...] = g
        mo_ref[...] = m_new
        vo_ref[...] = v_new
        d_ref[...] = -ADAM_LR * ((m_new / c1) / (jnp.sqrt(v_new / c2) + ADAM_EPS) + ADAM_WD * w_ref[...])

    blk = pl.BlockSpec(shape, at)
    return pl.pallas_call(
        body, name=name, grid=grid,
        in_specs=[pl.BlockSpec((nparts,) + shape, lambda i: (0,) + at(i)), blk, blk, blk],
        out_specs=[blk] * 4, out_shape=[S((rows, cols), F32)] * 4,
        compiler_params=pltpu.CompilerParams(dimension_semantics=("parallel",)),
    )(parts, w, m, v)


def _adam_math(g, w, m, v):
    c1 = 1.0 - ADAM_B1 ** ADAM_STEP
    c2 = 1.0 - ADAM_B2 ** ADAM_STEP
    m_new = ADAM_B1 * m + (1.0 - ADAM_B1) * g
    v_new = ADAM_B2 * v + (1.0 - ADAM_B2) * (g * g)
    return -ADAM_LR * ((m_new / c1) / (jnp.sqrt(v_new / c2) + ADAM_EPS) + ADAM_WD * w), m_new, v_new


def _adamw_small(parts, params, loss_parts):
    n = len(params)

    def body(*refs):
        p_refs, rest = refs[:n], refs[n + 1:]
        total = refs[n][0]
        for s in range(1, NDEV):
            total = total + refs[n][s]
        refs[-1][...] = total
        for j in range(n):
            w_ref, m_ref, v_ref = rest[3 * j:3 * j + 3]
            g_ref, d_ref, mo_ref, vo_ref = rest[3 * n + 4 * j:3 * n + 4 * j + 4]
            width = w_ref.shape[1]
            g = p_refs[j][0]
            for s in range(1, NDEV):
                g = g + p_refs[j][s]
            g = g[:, :width]
            delta, m_new, v_new = _adam_math(g, w_ref[...], m_ref[...], v_ref[...])
            g_ref[...] = g
            d_ref[...] = delta
            mo_ref[...] = m_new
            vo_ref[...] = v_new

    flat = [a for group in params for a in group]
    return pl.pallas_call(
        body, name="adam_small",
        out_shape=[S(group[0].shape, F32) for group in params for _ in range(4)] + [S((1, 128), F32)],
    )(*parts, loss_parts, *flat)


def _adamw_rows(parts, w, m, v, name, tc=128):
    rows, _, cols = w.shape
    nparts = parts.shape[0]
    nsteps = cols // tc

    def body(p_ref, w_hbm, m_hbm, v_hbm, g_hbm, d_hbm, mo_hbm, vo_hbm, inbuf, outbuf, insem, outsem):
        i = pl.program_id(0)
        slot = i & 1

        def view(ref, step):
            return ref.at[:, 0, pl.ds(pl.multiple_of(step * tc, tc), tc)]

        def fetch(step, sl):
            return [pltpu.make_async_copy(view(src, step), inbuf.at[sl, k], insem.at[sl, k])
                    for k, src in enumerate((w_hbm, m_hbm, v_hbm))]

        def write(step, sl):
            return [pltpu.make_async_copy(outbuf.at[sl, k], view(dst, step), outsem.at[sl, k])
                    for k, dst in enumerate((g_hbm, d_hbm, mo_hbm, vo_hbm))]

        @pl.when(i == 0)
        def _():
            for cp in fetch(0, 0):
                cp.start()

        @pl.when(i + 1 < nsteps)
        def _():
            for cp in fetch(i + 1, 1 - slot):
                cp.start()

        for cp in fetch(i, slot):
            cp.wait()

        @pl.when(i >= 2)
        def _():
            for cp in write(i - 2, slot):
                cp.wait()

        g = p_ref[0].astype(F32)
        for s in range(1, nparts):
            g = g + p_ref[s].astype(F32)
        delta, m_new, v_new = _adam_math(g, inbuf[slot, 0], inbuf[slot, 1], inbuf[slot, 2])
        for k, val in enumerate((g, delta, m_new, v_new)):
            outbuf[slot, k] = val
        for cp in write(i, slot):
            cp.start()

        @pl.when(i == nsteps - 1)
        def _():
            for cp in write(i - 1, 1 - slot) + write(i, slot):
                cp.wait()

    hbm = pl.BlockSpec(memory_space=pl.ANY)
    assert nsteps >= 2
    return pl.pallas_call(
        body, name=name, grid=(nsteps,),
        in_specs=[pl.BlockSpec((nparts, rows, tc), lambda i: (0, 0, i)), hbm, hbm, hbm],
        out_specs=[hbm] * 4, out_shape=[S((rows, 1, cols), F32)] * 4,
        scratch_shapes=[pltpu.VMEM((2, 3, rows, tc), F32), pltpu.VMEM((2, 4, rows, tc), F32),
                        pltpu.SemaphoreType.DMA((2, 3)), pltpu.SemaphoreType.DMA((2, 4))],
        compiler_params=pltpu.CompilerParams(dimension_semantics=("arbitrary",)),
    )(parts, w, m, v)


def _to_aligned(wt):
    pad = jnp.zeros((GLR_W - GLR_N, wt.shape[1]), wt.dtype)
    return jnp.concatenate([wt[O_QG:O_GLR], wt[O_ZG:O_GA], wt[O_GLR:O_ZG], pad, wt[O_ZA:O_QG], wt[O_GA:O_END],
                            wt[O_QA:O_ZA]], axis=0)


def _from_aligned(wt):
    return jnp.concatenate([wt[C_QA:], wt[C_ZA:C_GA], wt[C_QG:C_ZG], wt[C_GLR:C_GLR + GLR_N], wt[C_ZG:C_GLR],
                            wt[C_GA:C_QA]], axis=0)


def _col_blocks(w, width):
    return w.reshape(w.shape[0], NDEV, width).transpose(1, 0, 2)


def _from_col_blocks(w):
    return w.transpose(1, 0, 2).reshape(w.shape[1], NDEV * w.shape[2])


def _local_step(x2, p2, pos, tgt, norm_g, qk_norm_q, qk_norm_k, gla_gate_b, gla_norm_g, ple_norm_g, w_al,
                weights=None, proj_side=None, unpack=None, dw_side_of=None, dh_side_of=None):
    half = ROT_DIM // 2
    inv8 = jnp.power(jnp.float32(ROPE_THETA), -jnp.arange(half, dtype=F32) * 2.0 / ROT_DIM)
    inv = jnp.tile(jnp.concatenate([inv8, inv8, jnp.zeros((HD - ROT_DIM,), F32)]), 2).reshape(1, 128)
    gq = jnp.tile(qk_norm_q, (1, 2))
    gk = jnp.tile(qk_norm_k, (1, 2))

    proj, h, got = _proj_rms(x2, norm_g, w_al, proj_side)
    if proj_side is not None:
        weights = unpack(got)
    w2p, w_att_f, w_gla_f, w_out_f, w_pg_f, w_ple_f = weights
    qkv = _qk_prep(proj, pos, inv, gq, gk)
    fwd = [_att_fwd(qkv[g], qkv[3 + g], qkv[6 + g], g, f"att_fwd{g}") for g in range(3)]
    att, lse, ain = _att_merge([f[0] for f in fwd], [f[1] for f in fwd], proj)
    o_gla, bin_, states = _gla_fwd(proj, w2p, gla_gate_b, gla_norm_g)
    ya, yb, y, x1 = _branches_fwd(ain, bin_, proj, x2, w_att_f, w_gla_f, w_out_f)
    n2, loss_v, dout, du, dw_ple = _ple_loss(x1, p2, tgt, ple_norm_g, w_pg_f, w_ple_f)

    dx1, dy, dg_ple, dw_pg, dw_out = _ple_bwd(du, n2, y, x1, dout, ple_norm_g, w_pg_f, w_out_f)
    dproj, dain, dbin, dw_att, dw_gla = _branches_bwd(dy, ya, yb, ain, bin_, proj, w_att_f, w_gla_f)
    dproj, da0, da1, da2, at1, at2, ls1, ls2 = _att_gate_bwd(dain, att, lse, proj, dproj)
    datts, atts, lses = (da0, da1, da2), (att[None], at1, at2), (lse[None], ls1, ls2)
    dproj, dw2, dbg, dgn = _gla_bwd(proj, w2p, gla_gate_b, gla_norm_g, o_gla, states, dbin, dproj)
    bwd = [_att_bwd(qkv[g], qkv[3 + g], qkv[6 + g], datts[g], atts[g], lses[g], g, f"att_bwd{g}") for g in range(3)]
    dproj, dgq, dgk = _qk_bwd(proj, pos, inv, gq, gk, [b[0] for b in bwd], [b[1] for b in bwd],
                              [b[2] for b in bwd], dproj)
    out = dict(loss=loss_v, dw2=dw2, dw_att=dw_att, dw_gla=dw_gla, dw_out=dw_out, dw_pg=dw_pg, dw_ple=dw_ple,
               dgq=dgq, dgk=dgk, dbg=dbg, dgn=dgn, dg_ple=dg_ple)
    if dw_side_of is None:
        dw_al = _mm(dproj, h, mode="tn", name="dw_in", tm=1536, tn=D, tk=2048, out_dtype=BF16)
    else:
        dw_al, out["dw_side"] = _mm(dproj, h, mode="tn", name="dw_in", tm=1536, tn=D, tk=2048, out_dtype=BF16,
                                    side=dw_side_of(out))
    grad_x, dg_norm, out["dh_side"] = _dh_rms(dproj, w_al, x2, norm_g, dx1,
                                              None if dh_side_of is None else dh_side_of(dw_al))
    out.update(grad_x=grad_x, dw_al=dw_al, dg_norm=dg_norm)
    return out


def kernel(x, p, positions, norm_g, w_in, qk_norm_q, qk_norm_k, gla_gate_w2, gla_gate_b, gla_norm_g, w_att_proj, w_gla_proj, w_out, ple_norm_g, w_ple_gate, w_ple, loss_target, m_norm_g, m_w_in, m_qk_norm_q, m_qk_norm_k, m_gla_gate_w2, m_gla_gate_b, m_gla_norm_g, m_w_att_proj, m_w_gla_proj, m_w_out, m_ple_norm_g, m_w_ple_gate, m_w_ple, v_norm_g, v_w_in, v_qk_norm_q, v_qk_norm_k, v_gla_gate_w2, v_gla_gate_b, v_gla_norm_g, v_w_att_proj, v_w_gla_proj, v_w_out, v_ple_norm_g, v_w_ple_gate, v_w_ple):
    x2, p2, tgt = x[0], p[0, 0], loss_target[0]
    pos = positions.astype(F32).reshape(T, 1)

    rows3 = jnp.stack([w_gla_proj[0], w_out[0], w_ple_gate[0]]).astype(BF16)
    cols3 = jnp.concatenate([w_att_proj[0], w_ple[0], jnp.pad(gla_gate_w2[0], ((0, 0), (0, 64)))], axis=0).astype(BF16)
    (g_in,) = _all_gather_by_chip([w_in[0].T.astype(BF16)], "gather_w_in")
    w_al = _to_aligned(g_in.reshape(W_IN_COLS, D))

    def unpack(got):
        g_rows, g_cols = got
        w2_f = _from_col_blocks(g_cols[:, 768:784, :64])
        return (jnp.pad(w2_f, ((0, GLR_W - GLR_N), (0, 0))), _from_col_blocks(g_cols[:, :512]),
                g_rows[:, 0].reshape(D, D), g_rows[:, 1].reshape(D, D), g_rows[:, 2].reshape(D, D),
                _from_col_blocks(g_cols[:, 512:768]))

    def dw_side_of(g):
        s_rows = jnp.concatenate([g[k].reshape(NDEV, 128, D) for k in ("dw_gla", "dw_out", "dw_pg")], axis=1)
        s_cols = jnp.concatenate([_col_blocks(g["dw_att"], 128), _col_blocks(g["dw_ple"], 128),
                                  jnp.pad(_col_blocks(g["dw2"][:GLR_N], 64), ((0, 0), (0, 0), (0, 64)))], axis=1)
        return _exchange_side([s_rows.astype(BF16), s_cols.astype(BF16)])

    def dh_side_of(dw_al):
        s_in = _from_aligned(dw_al).reshape(NDEV, W_IN_SHARD, D)
        (from_sibling,) = _exchange_sibling([s_in], "exchange_sibling")
        core = lax.axis_index("c").astype(jnp.int32).reshape(1)
        return _chips_side([_pair_add(s_in, from_sibling, core, "pair_add")])

    loc = _local_step(x2, p2, pos, tgt, norm_g, qk_norm_q, qk_norm_k, gla_gate_b, gla_norm_g, ple_norm_g, w_al,
                      proj_side=_gather_side([rows3, cols3]), unpack=unpack, dw_side_of=dw_side_of,
                      dh_side_of=dh_side_of)
    loss_v, grad_x = loc["loss"], loc["grad_x"]
    dg_norm, dgq, dgk, dbg, dgn, dg_ple = (loc[k] for k in ("dg_norm", "dgq", "dgk", "dbg", "dgn", "dg_ple"))
    r_rows, r_cols = loc["dw_side"]
    (r_in,) = loc["dh_side"]

    r_small = _comm_call(_gather_side([dg_norm, dgq, dgk, dbg, dgn, dg_ple, loss_v]), "gather_small")

    outs = {}

    def adam(nm, parts, w, m, v, tr):
        outs[nm] = _adamw(parts, w, m, v, "adam_" + nm, tr)

    rows_of = lambda a: jnp.transpose(a, (2, 0, 1))
    outs["w_in"] = [jnp.transpose(o, (1, 2, 0))[0] for o in
                    _adamw_rows(r_in, rows_of(w_in), rows_of(m_w_in), rows_of(v_w_in), "adam_w_in")]
    adam("w_gla_proj", r_rows[:, :128], w_gla_proj[0], m_w_gla_proj[0], v_w_gla_proj[0], 128)
    adam("w_out", r_rows[:, 128:256], w_out[0], m_w_out[0], v_w_out[0], 128)
    adam("w_ple_gate", r_rows[:, 256:], w_ple_gate[0], m_w_ple_gate[0], v_w_ple_gate[0], 128)
    adam("w_att_proj", r_cols[:, :512], w_att_proj[0], m_w_att_proj[0], v_w_att_proj[0], 512)
    adam("w_ple", r_cols[:, 512:768], w_ple[0], m_w_ple[0], v_w_ple[0], 256)
    adam("gla_gate_w2", r_cols[:, 768:784, :64], gla_gate_w2[0], m_gla_gate_w2[0], v_gla_gate_w2[0], 16)
    small = ((norm_g, m_norm_g, v_norm_g), (qk_norm_q, m_qk_norm_q, v_qk_norm_q), (qk_norm_k, m_qk_norm_k, v_qk_norm_k),
             (gla_gate_b, m_gla_gate_b, v_gla_gate_b), (gla_norm_g, m_gla_norm_g, v_gla_norm_g),
             (ple_norm_g, m_ple_norm_g, v_ple_norm_g))
    sm = _adamw_small(r_small[:6], small, r_small[6])
    for j, nm in enumerate(("norm_g", "qk_norm_q", "qk_norm_k", "gla_gate_b", "gla_norm_g", "ple_norm_g")):
        outs[nm] = [o[0] for o in sm[4 * j:4 * j + 4]]

    loss = sm[-1][0, 0]
    order = ["norm_g", "w_in", "qk_norm_q", "qk_norm_k", "gla_gate_w2", "gla_gate_b", "gla_norm_g", "w_att_proj",
             "w_gla_proj", "w_out", "ple_norm_g", "w_ple_gate", "w_ple"]
    result = [loss, grad_x[None]]
    for i in range(4):
        result += [outs[nm][i][None] for nm in order]
    return tuple(result)
```

```python
import functools

import jax
import jax.numpy as jnp
from jax import lax
from jax.experimental import pallas as pl
from jax.experimental.pallas import tpu as pltpu

F32 = jnp.float32
BF16 = jnp.bfloat16
S = jax.ShapeDtypeStruct

T = 4096
D = 1024
NDEV = 8
HD = 64
ATT_W = 512
ATT_QKV = 1536
DILATIONS = (1, 4, 16)
BLK = 128
GH, GDK, GDV = 4, 128, 256
GLA_C = 128
PLE = 256
EPS = 1e-6
ROT_DIM = 16
ROPE_THETA = 500000.0
GLA_TAU = 16.0
W_IN_COLS = 10256
W_IN_SHARD = 1282

C_QG, C_KG, C_VG, C_ZG, C_GLR, C_ZA, C_GA, C_GB, C_QA, C_KA, C_VA = (
    0, 512, 1024, 2048, 3072, 3584, 4096, 5120, 6144, 7680, 9216)
GLA_GROUP_W = 3584
GLR_W = 512
NCOL = 10752
GLR_N = 16
O_QA, O_ZA, O_QG, O_GLR, O_ZG, O_GA, O_END = 0, 4608, 5120, 7168, 7184, 8208, 10256

ADAM_LR, ADAM_B1, ADAM_B2, ADAM_EPS, ADAM_WD, ADAM_STEP = 0.001, 0.9, 0.999, 1e-08, 0.01, 10

MESH = pl.DeviceIdType.MESH


def _sigmoid(z):
    return 1.0 / (1.0 + jnp.exp(-z))


def _dot(a, b, dims):
    return lax.dot_general(a, b, (dims, ((), ())), preferred_element_type=F32)


def _nn(a, b):
    return _dot(a, b, ((1,), (0,)))


def _nt(a, b):
    return _dot(a, b, ((1,), (1,)))


def _tn(a, b):
    return _dot(a, b, ((0,), (0,)))


def _mm(a, b, *, mode, name, tm, tn, tk, out_dtype=F32, res=None, side=None):
    if mode == "nn":
        (m, k), n = a.shape, b.shape[1]
        a_spec = pl.BlockSpec((tm, tk), lambda i, j, l: (i, l))
        b_spec = pl.BlockSpec((tk, tn), lambda i, j, l: (l, j))
        dot = _nn
    elif mode == "nt":
        (m, k), n = a.shape, b.shape[0]
        a_spec = pl.BlockSpec((tm, tk), lambda i, j, l: (i, l))
        b_spec = pl.BlockSpec((tn, tk), lambda i, j, l: (j, l))
        dot = _nt
    else:
        (k, m), n = a.shape, b.shape[1]
        a_spec = pl.BlockSpec((tk, tm), lambda i, j, l: (l, i))
        b_spec = pl.BlockSpec((tk, tn), lambda i, j, l: (l, j))
        dot = _tn
    assert m % tm == 0 and n % tn == 0 and k % tk == 0, (name, m, n, k)
    grid = (m // tm, n // tn, k // tk)
    nk = grid[2]
    o_spec = pl.BlockSpec((tm, tn), lambda i, j, l: (i, j))
    in_specs = [a_spec, b_spec]
    args = [a, b]
    if res is not None:
        in_specs.append(o_spec)
        args.append(res)
    n_in = len(args)
    n_side = 0 if side is None else len(side["arrs"])
    hbm = pl.BlockSpec(memory_space=pl.ANY)

    def body(*refs):
        a_ref, b_ref = refs[0], refs[1]
        r_ref = refs[2] if res is not None else None
        o_ref = refs[n_in + n_side]
        scratch = refs[n_in + 2 * n_side + 1:]
        if side is not None:
            start, finish_side = side["plan"](refs[n_in:n_in + n_side], refs[n_in + n_side + 1:n_in + 2 * n_side + 1],
                                              *scratch[1 if nk > 1 else 0:])
            ids = [pl.program_id(d) for d in range(3)]

            @pl.when((ids[0] == 0) & (ids[1] == 0) & (ids[2] == 0))
            def _():
                start()

        part = dot(a_ref[...].astype(BF16), b_ref[...].astype(BF16))

        def finish(val):
            if r_ref is not None:
                val = val + r_ref[...]
            o_ref[...] = val.astype(out_dtype)

        if nk == 1:
            finish(part)
        else:
            acc = scratch[0]
            l = pl.program_id(2)

            @pl.when(l == 0)
            def _():
                acc[...] = part

            @pl.when(l > 0)
            def _():
                acc[...] += part

            @pl.when(l == nk - 1)
            def _():
                finish(acc[...])

        if side is not None:
            @pl.when((ids[0] == grid[0] - 1) & (ids[1] == grid[1] - 1) & (ids[2] == grid[2] - 1))
            def _():
                finish_side()

    sems = [] if side is None else side["scratch"]
    outs = pl.pallas_call(
        body, name=name, grid=grid,
        in_specs=in_specs + [hbm] * n_side, out_specs=[o_spec] + [hbm] * n_side,
        out_shape=[S((m, n), out_dtype)] + ([] if side is None else side["out_shape"]),
        scratch_shapes=([pltpu.VMEM((tm, tn), F32)] if nk > 1 else []) + sems,
        compiler_params=pltpu.CompilerParams(
            dimension_semantics=("arbitrary",) * 3 if side is not None else ("parallel", "parallel", "arbitrary")),
    )(*args, *([] if side is None else side["arrs"]))
    return outs[0] if side is None else (outs[0], outs[1:])


def _side_parts(side, refs, n_in, n_out):
    n_side = 0 if side is None else len(side["arrs"])
    scratch = refs[n_in + n_out + 2 * n_side:]
    if side is None:
        return (lambda: None), (lambda: None), scratch
    start, finish = side["plan"](refs[n_in:n_in + n_side], refs[n_in + n_side + n_out:n_in + n_out + 2 * n_side],
                                 *scratch[len(scratch) - len(side["scratch"]):])
    return start, finish, scratch


def _proj_rms(x, g, wt, side=None):
    tm, tn = 1024, 1536
    grid = (T // tm, NCOL // tn)
    n_side = 0 if side is None else len(side["arrs"])
    hbm = pl.BlockSpec(memory_space=pl.ANY)

    def body(*refs):
        x_ref, g_ref, w_ref = refs[:3]
        o_ref, h_ref = refs[3 + n_side], refs[4 + n_side]
        start, finish, _ = _side_parts(side, refs, 3, 2)
        i, j = pl.program_id(0), pl.program_id(1)

        @pl.when((i == 0) & (j == 0))
        def _():
            start()

        @pl.when(j == 0)
        def _():
            xf = x_ref[...]
            r = lax.rsqrt(jnp.mean(xf * xf, axis=-1, keepdims=True) + EPS)
            h_ref[...] = (xf * r * g_ref[...]).astype(BF16)

        o_ref[...] = _nt(h_ref[...], w_ref[...])

        @pl.when((i == grid[0] - 1) & (j == grid[1] - 1))
        def _():
            finish()

    outs = pl.pallas_call(
        body, name="proj", grid=grid,
        in_specs=[pl.BlockSpec((tm, D), lambda i, j: (i, 0)), pl.BlockSpec((1, D), lambda i, j: (0, 0)),
                  pl.BlockSpec((tn, D), lambda i, j: (j, 0))] + [hbm] * n_side,
        out_specs=[pl.BlockSpec((tm, tn), lambda i, j: (i, j)), pl.BlockSpec((tm, D), lambda i, j: (i, 0))] + [hbm] * n_side,
        out_shape=[S((T, NCOL), F32), S((T, D), BF16)] + ([] if side is None else side["out_shape"]),
        scratch_shapes=[] if side is None else side["scratch"],
        compiler_params=pltpu.CompilerParams(dimension_semantics=("arbitrary", "arbitrary")),
    )(x, g, wt, *([] if side is None else side["arrs"]))
    return outs[0], outs[1], outs[2:]


def _dh_rms(dproj, wt, x, g, skip, side=None):
    tm, tk = 1024, 1792
    grid = (T // tm, NCOL // tk)
    n_side = 0 if side is None else len(side["arrs"])
    hbm = pl.BlockSpec(memory_space=pl.ANY)

    def body(*refs):
        a_ref, w_ref, x_ref, g_ref, s_ref = refs[:5]
        dx_ref, dg_ref = refs[5 + n_side], refs[6 + n_side]
        start, finish, scratch = _side_parts(side, refs, 5, 2)
        acc = scratch[0]
        i, l = pl.program_id(0), pl.program_id(1)

        @pl.when((i == 0) & (l == 0))
        def _():
            start()

        part = _nn(a_ref[...], w_ref[...])

        @pl.when(l == 0)
        def _():
            acc[...] = part

        @pl.when(l > 0)
        def _():
            acc[...] += part

        @pl.when(l == grid[1] - 1)
        def _():
            xf = x_ref[...]
            r = lax.rsqrt(jnp.mean(xf * xf, axis=-1, keepdims=True) + EPS)
            dn = acc[...]
            u = dn * g_ref[...]
            dx_ref[...] = s_ref[...] + r * u - xf * (r * r * r) * jnp.mean(u * xf, axis=-1, keepdims=True)
            dg = jnp.sum(dn * xf * r, axis=0, keepdims=True)

            @pl.when(i == 0)
            def _():
                dg_ref[...] = dg

            @pl.when(i > 0)
            def _():
                dg_ref[...] += dg

        @pl.when((i == grid[0] - 1) & (l == grid[1] - 1))
        def _():
            finish()

    tok = pl.BlockSpec((tm, D), lambda i, l: (i, 0))
    outs = pl.pallas_call(
        body, name="dh", grid=grid,
        in_specs=[pl.BlockSpec((tm, tk), lambda i, l: (i, l)), pl.BlockSpec((tk, D), lambda i, l: (l, 0)), tok,
                  pl.BlockSpec((1, D), lambda i, l: (0, 0)), tok] + [hbm] * n_side,
        out_specs=[tok, pl.BlockSpec((1, D), lambda i, l: (0, 0))] + [hbm] * n_side,
        out_shape=[S((T, D), F32), S((1, D), F32)] + ([] if side is None else side["out_shape"]),
        scratch_shapes=[pltpu.VMEM((tm, D), F32)] + ([] if side is None else side["scratch"]),
        compiler_params=pltpu.CompilerParams(dimension_semantics=("arbitrary", "arbitrary")),
    )(dproj, wt, x, g, skip, *([] if side is None else side["arrs"]))
    return outs[0], outs[1], outs[2:]


def _rot_tables(pos_ref, inv_ref):
    lane = lax.broadcasted_iota(jnp.int32, (1, 128), 1) % HD
    ang = pos_ref[...] * inv_ref[...]
    cos, sin = jnp.cos(ang), jnp.sin(ang)
    c = jnp.where(lane < ROT_DIM, cos, 1.0)
    sp = jnp.where((lane >= ROT_DIM // 2) & (lane < ROT_DIM), sin, 0.0)
    sm = jnp.where(lane < ROT_DIM // 2, -sin, 0.0)
    return c, sp, sm


def _head_sums(v):
    same = (lax.broadcasted_iota(jnp.int32, (128, 128), 0) < HD) == (lax.broadcasted_iota(jnp.int32, (128, 128), 1) < HD)
    ones = jnp.where(same, 1.0, 0.0).astype(BF16)
    hi = v.astype(BF16)
    lo = (v - hi.astype(F32)).astype(BF16)
    return _nn(hi, ones) + _nn(lo, ones)


def _pair_norm(t):
    return lax.rsqrt(_head_sums(t * t) * (1.0 / HD) + EPS)


def _pair_mean(t):
    return _head_sums(t) * (1.0 / HD)


TT = 256
NCH = ATT_QKV // 128


def _res_shape(grp, dtype):
    return S((DILATIONS[grp], T // DILATIONS[grp], ATT_W), dtype)


def _res_spec(grp):
    dil = DILATIONS[grp]
    return pl.BlockSpec((dil, TT // dil, ATT_W), lambda i: (0, i, 0))


def _to_residues(sc, j, dst_ref, dil, cols):
    n = TT // dil
    for r in range(dil):
        rows = sc[j] if dil == 1 else sc.at[j][pl.ds(r, n, stride=dil), :]
        dst_ref[r, :, cols] = rows.astype(dst_ref.dtype)


def _from_residues(src_ref, cols, sc, j, dil):
    n = TT // dil
    for r in range(dil):
        if dil == 1:
            sc[j] = src_ref[r, :, cols]
        else:
            sc.at[j][pl.ds(r, n, stride=dil), :] = src_ref[r, :, cols]


def _tok_spec(width, cblk=0):
    return pl.BlockSpec((TT, width), functools.partial(lambda i, c: (i, c), c=cblk))


def _const_spec(arr_or_shape):
    shape = arr_or_shape if isinstance(arr_or_shape, tuple) else arr_or_shape.shape
    return pl.BlockSpec(shape, functools.partial(lambda i, nd: (0,) * nd, nd=len(shape)))


def _qk_prep(proj, pos, inv, gq, gk):
    def body(q_ref, k_ref, v_ref, pos_ref, inv_ref, gq_ref, gk_ref, *rest):
        outs, sc = rest[:9], rest[9]
        c, sp, sm = _rot_tables(pos_ref, inv_ref)
        for which, (src, g_ref) in enumerate(((q_ref, gq_ref), (k_ref, gk_ref), (v_ref, None))):
            if g_ref is not None:
                g = jnp.broadcast_to(g_ref[...] * ((HD ** -0.5) if which == 0 else 1.0), c.shape)
                cg, spg, smg = c * g, sp * pltpu.roll(g, 8, 1), sm * pltpu.roll(g, 120, 1)
            for j in range(NCH):
                t = src[:, j * 128:(j + 1) * 128]
                if g_ref is not None:
                    t = _pair_norm(t) * (t * cg + pltpu.roll(t, 8, 1) * spg + pltpu.roll(t, 120, 1) * smg)
                sc[j] = t
            for j in range(NCH):
                grp, sub = divmod(j * 128, ATT_W)
                _to_residues(sc, j, outs[which * 3 + grp], DILATIONS[grp], slice(sub, sub + 128))

    return pl.pallas_call(
        body, name="qk_prep", grid=(T // TT,),
        in_specs=[_tok_spec(ATT_QKV, C_QA // ATT_QKV), _tok_spec(ATT_QKV, C_KA // ATT_QKV),
                  _tok_spec(ATT_QKV, C_VA // ATT_QKV), _tok_spec(1), _const_spec(inv), _const_spec(gq), _const_spec(gk)],
        out_specs=[_res_spec(g) for _ in range(3) for g in range(3)],
        out_shape=[_res_shape(g, BF16) for _ in range(3) for g in range(3)],
        scratch_shapes=[pltpu.VMEM((NCH, TT, 128), F32)],
        compiler_params=pltpu.CompilerParams(dimension_semantics=("arbitrary",)),
    )(proj, proj, proj, pos, inv, gq, gk)


def _qk_bwd(proj, pos, inv, gq, gk, dqs, dks, dvs, dproj):
    const = lambda a: pl.BlockSpec(a.shape, functools.partial(lambda i, p, nd: (0,) * nd, nd=a.ndim))
    res = lambda g: pl.BlockSpec((DILATIONS[g], TT // DILATIONS[g], ATT_W), lambda i, p: (0, i, 0))
    base = C_QA // ATT_QKV

    def body(t_ref, pos_ref, inv_ref, gq_ref, gk_ref, dq0, dq1, dq2, dk0, dk1, dk2, dv0, dv1, dv2, buf_ref,
             out_ref, dgq_ref, dgk_ref, sc):
        del buf_ref
        part = pl.program_id(1)
        first = pl.program_id(0) == 0

        def gather(drefs):
            for j in range(NCH):
                grp, sub = divmod(j * 128, ATT_W)
                _from_residues(drefs[grp], slice(sub, sub + 128), sc, j, DILATIONS[grp])

        def normed(g_ref, drefs, dg_ref):
            c, sp, sm = _rot_tables(pos_ref, inv_ref)
            gather(drefs)
            dg = jnp.zeros((1, 128), F32)
            for j in range(NCH):
                cols = slice(j * 128, (j + 1) * 128)
                d_rot = sc[j]
                dn = d_rot * c + pltpu.roll(d_rot * sp, 120, 1) + pltpu.roll(d_rot * sm, 8, 1)
                t = t_ref[:, cols]
                r = _pair_norm(t)
                u = dn * g_ref[...]
                out_ref[:, cols] = (r * u - t * (r * r * r) * _pair_mean(u * t)).astype(BF16)
                dg = dg + jnp.sum(dn * t * r, axis=0, keepdims=True)
            dg = dg + pltpu.roll(dg, HD, 1)

            @pl.when(first)
            def _():
                dg_ref[...] = dg

            @pl.when(jnp.logical_not(first))
            def _():
                dg_ref[...] += dg

        @pl.when(part == 0)
        def _():
            gather((dv0, dv1, dv2))
            for j in range(NCH):
                out_ref[:, j * 128:(j + 1) * 128] = sc[j].astype(BF16)

        @pl.when(part == 1)
        def _():
            normed(gq_ref, (dq0, dq1, dq2), dgq_ref)

        @pl.when(part == 2)
        def _():
            normed(gk_ref, (dk0, dk1, dk2), dgk_ref)

    keep = pl.BlockSpec((1, 128), lambda i, p: (0, 0))
    return pl.pallas_call(
        body, name="qk_bwd", grid=(T // TT, 3),
        in_specs=[pl.BlockSpec((TT, ATT_QKV), lambda i, p: (i, base + jnp.maximum(p - 1, 0))),
                  pl.BlockSpec((TT, 1), lambda i, p: (i, 0)), const(inv), const(gq), const(gk)]
        + [res(g) for _ in range(3) for g in range(3)] + [pl.BlockSpec(memory_space=pl.ANY)],
        out_specs=[pl.BlockSpec((TT, ATT_QKV), lambda i, p: (i, base + jnp.where(p == 0, 2, p - 1))), keep, keep],
        out_shape=[S(dproj.shape, dproj.dtype), S((1, 128), F32), S((1, 128), F32)],
        input_output_aliases={14: 0},
        scratch_shapes=[pltpu.VMEM((NCH, TT, 128), F32)],
        compiler_params=pltpu.CompilerParams(dimension_semantics=("arbitrary", "arbitrary")),
    )(proj, pos, inv, gq, gk, *dqs, *dks, *dvs, dproj)


def _split_heads(t):
    low = lax.broadcasted_iota(jnp.int32, (1, 128), 1) < HD
    zero = jnp.zeros_like(t)
    return jnp.concatenate([jnp.where(low, t, zero), jnp.where(low, zero, t)], axis=0)


def _join_heads(t2):
    low = lax.broadcasted_iota(jnp.int32, (1, 128), 1) < HD
    n = t2.shape[0] // 2
    return jnp.where(low, t2[:n], t2[n:])


def _band_mask4(has_before, has_own):
    row = lax.broadcasted_iota(jnp.int32, (BLK, 4 * BLK), 0)
    lane = lax.broadcasted_iota(jnp.int32, (BLK, 4 * BLK), 1)
    key = lane & (BLK - 1)
    own = lane >= 2 * BLK
    return (own & (key <= row) & has_own) | (jnp.logical_not(own) & (key >= row) & has_before)


def _band_mask_before(has_before):
    row = lax.broadcasted_iota(jnp.int32, (BLK, 2 * BLK), 0)
    key = lax.broadcasted_iota(jnp.int32, (BLK, 2 * BLK), 1) & (BLK - 1)
    return (key >= row) & has_before


def _per_head(width, col_a, col_b):
    lane = lax.broadcasted_iota(jnp.int32, (1, width), 1)
    return jnp.where((lane & BLK) == 0, col_a, col_b)


NQ = ATT_W // 128


def _att_fwd(q, k, v, grp, name):
    dil = DILATIONS[grp]
    nb = T // dil // BLK

    def body(q_ref, kp_ref, kc_ref, vp_ref, vc_ref, o_ref, lse_ref, s_sc, p_sc):
        mask = _band_mask4(pl.program_id(1) > 0, True)
        low = lax.broadcasted_iota(jnp.int32, (1, 128), 1) < HD
        halves = lambda ref, j, h: (ref[j, :, h * BLK:(h + 1) * BLK], ref[j, :, (h + 2) * BLK:(h + 3) * BLK])
        for j in range(NQ):
            cols = slice(j * 128, (j + 1) * 128)
            k4 = jnp.concatenate([_split_heads(kp_ref[:, cols]), _split_heads(kc_ref[:, cols])], axis=0)
            s_sc[j] = jnp.where(mask, _nt(q_ref[:, cols], k4), -jnp.inf)
        mxs = [[jnp.maximum(*(jnp.max(t, axis=-1, keepdims=True) for t in halves(s_sc, j, h))) for h in range(2)]
               for j in range(NQ)]
        dens = []
        for j in range(NQ):
            p = jnp.exp(s_sc[j] - _per_head(4 * BLK, *mxs[j]))
            p_sc[j] = p.astype(BF16)
            dens.append([jnp.sum(p[:, h * BLK:(h + 1) * BLK], axis=-1, keepdims=True)
                         + jnp.sum(p[:, (h + 2) * BLK:(h + 3) * BLK], axis=-1, keepdims=True) for h in range(2)])
        for j in range(NQ):
            cols = slice(j * 128, (j + 1) * 128)
            v4 = jnp.concatenate([_split_heads(vp_ref[:, cols]), _split_heads(vc_ref[:, cols])], axis=0)
            o_ref[:, cols] = _nn(p_sc[j], v4) / jnp.where(low, dens[j][0], dens[j][1])
            lse_ref[:, cols] = jnp.where(low, mxs[j][0] + jnp.log(dens[j][0]), mxs[j][1] + jnp.log(dens[j][1]))

    cur = pl.BlockSpec((None, BLK, ATT_W), lambda r, i: (r, i, 0))
    prev = pl.BlockSpec((None, BLK, ATT_W), lambda r, i: (r, jnp.maximum(i - 1, 0), 0))
    return pl.pallas_call(
        body, name=name, grid=(dil, nb),
        in_specs=[cur, prev, cur, prev, cur],
        out_specs=[cur, cur], out_shape=[_res_shape(grp, F32)] * 2,
        scratch_shapes=[pltpu.VMEM((NQ, BLK, 4 * BLK), F32), pltpu.VMEM((NQ, BLK, 4 * BLK), BF16)],
        compiler_params=pltpu.CompilerParams(dimension_semantics=("parallel", "arbitrary")),
    )(q, k, k, v, v)


def _att_bwd(q, k, v, datt, att, lse, grp, name):
    dil = DILATIONS[grp]
    nb = T // dil // BLK
    scale = HD ** -0.5

    def body(q0_ref, q1_ref, kp_ref, kc_ref, vp_ref, vc_ref, do0_ref, do1_ref, o0_ref, o1_ref, l0_ref, l1_ref,
             dq_ref, dk_ref, dv_ref, k4_sc, v4_sc, s0_sc, s1_sc, dp0_sc, dp1_sc, p_sc, ds_sc):
        i = pl.program_id(1)
        mask_mine = _band_mask4(i > 0, True)
        mask_next = _band_mask_before(i < nb - 1)
        low = lax.broadcasted_iota(jnp.int32, (1, 128), 1) < HD
        for j in range(NQ):
            cols = slice(j * 128, (j + 1) * 128)
            k4_sc[j, :2 * BLK] = _split_heads(kp_ref[:, cols])
            k4_sc[j, 2 * BLK:] = _split_heads(kc_ref[:, cols])
            v4_sc[j, :2 * BLK] = _split_heads(vp_ref[:, cols])
            v4_sc[j, 2 * BLK:] = _split_heads(vc_ref[:, cols])
        for j in range(NQ):
            cols = slice(j * 128, (j + 1) * 128)
            s0_sc[j] = _nt(q0_ref[:, cols], k4_sc[j])
            s1_sc[j] = _nt(q1_ref[:, cols], k4_sc[j, 2 * BLK:])
            dp0_sc[j] = _nt(do0_ref[:, cols].astype(BF16), v4_sc[j])
            dp1_sc[j] = _nt(do1_ref[:, cols].astype(BF16), v4_sc[j, 2 * BLK:])
        stats = []
        for j in range(NQ):
            cols = slice(j * 128, (j + 1) * 128)
            for do_ref, o_ref, l_ref in ((do0_ref, o0_ref, l0_ref), (do1_ref, o1_ref, l1_ref)):
                prod = do_ref[:, cols].astype(F32) * o_ref[:, cols].astype(F32)
                d_all = jnp.sum(prod, axis=-1, keepdims=True)
                d_low = jnp.sum(jnp.where(low, prod, 0.0), axis=-1, keepdims=True)
                lse_t = l_ref[:, cols]
                stats.append((d_low, d_all - d_low, lse_t[:, 0:1], lse_t[:, HD:HD + 1]))
        for j in range(NQ):
            (da, db, la, lb), (da1, db1, la1, lb1) = stats[2 * j], stats[2 * j + 1]
            p0 = jnp.where(mask_mine, jnp.exp(s0_sc[j] - _per_head(4 * BLK, la, lb)), 0.0)
            ds0 = p0 * (dp0_sc[j] - _per_head(4 * BLK, da, db))
            p1 = jnp.where(mask_next, jnp.exp(s1_sc[j] - _per_head(2 * BLK, la1, lb1)), 0.0)
            ds1 = p1 * (dp1_sc[j] - _per_head(2 * BLK, da1, db1))
            p_sc[j, :BLK] = p0.astype(BF16)
            ds_sc[j, :BLK] = ds0.astype(BF16)
            p_sc[j, BLK:, 2 * BLK:] = p1.astype(BF16)
            ds_sc[j, BLK:, 2 * BLK:] = ds1.astype(BF16)
        for j in range(NQ):
            cols = slice(j * 128, (j + 1) * 128)
            dq_ref[:, cols] = _nn(ds_sc[j, :BLK], k4_sc[j]) * scale
            qq = jnp.concatenate([q0_ref[:, cols], q1_ref[:, cols]], axis=0)
            dd = jnp.concatenate([do0_ref[:, cols], do1_ref[:, cols]], axis=0).astype(BF16)
            dk_ref[:, cols] = _join_heads(_tn(ds_sc[j, :, 2 * BLK:], qq))
            dv_ref[:, cols] = _join_heads(_tn(p_sc[j, :, 2 * BLK:], dd))

    def spec(shift):
        return pl.BlockSpec((None, BLK, ATT_W), lambda r, i: (r, jnp.clip(i + shift, 0, nb - 1), 0))

    here, after, before = spec(0), spec(1), spec(-1)
    vm = pltpu.VMEM
    return pl.pallas_call(
        body, name=name, grid=(dil, nb),
        in_specs=[here, after, before, here, before, here, here, after, here, after, here, after],
        out_specs=[here] * 3, out_shape=[_res_shape(grp, F32)] * 3,
        scratch_shapes=[vm((NQ, 4 * BLK, 128), BF16), vm((NQ, 4 * BLK, 128), BF16), vm((NQ, BLK, 4 * BLK), F32),
                        vm((NQ, BLK, 2 * BLK), F32), vm((NQ, BLK, 4 * BLK), F32), vm((NQ, BLK, 2 * BLK), F32),
                        vm((NQ, 2 * BLK, 4 * BLK), BF16), vm((NQ, 2 * BLK, 4 * BLK), BF16)],
        compiler_params=pltpu.CompilerParams(dimension_semantics=("parallel", "arbitrary")),
    )(q, q, k, k, v, v, datt, datt, att, att, lse, lse)


def _att_merge(os_, lses, proj):
    nq = ATT_W // 128

    def body(o0, o1, o2, l0, l1, l2, za_ref, att_ref, lse_ref, ain_ref, sc):
        for a, ref in enumerate((o0, o1, o2, l0, l1, l2)):
            for j in range(nq):
                _from_residues(ref, slice(j * 128, (j + 1) * 128), sc, a * nq + j, DILATIONS[a % 3])
        for j in range(nq):
            cols = slice(j * 128, (j + 1) * 128)
            oa, ob, oc = (sc[a * nq + j] for a in range(3))
            la, lb, lc = (sc[(3 + a) * nq + j] for a in range(3))
            m = jnp.maximum(jnp.maximum(la, lb), lc)
            wa, wb, wc = jnp.exp(la - m), jnp.exp(lb - m), jnp.exp(lc - m)
            tot = wa + wb + wc
            att = (wa * oa + wb * ob + wc * oc) / tot
            att_ref[:, cols] = att
            lse_ref[:, cols] = m + jnp.log(tot)
            za = za_ref[:, cols]
            ain_ref[:, cols] = (att * za * _sigmoid(za)).astype(BF16)

    return pl.pallas_call(
        body, name="att_merge", grid=(T // TT,),
        in_specs=[_res_spec(g) for _ in range(2) for g in range(3)] + [_tok_spec(ATT_W, C_ZA // ATT_W)],
        out_specs=[_tok_spec(ATT_W)] * 3,
        out_shape=[S((T, ATT_W), F32), S((T, ATT_W), F32), S((T, ATT_W), BF16)],
        scratch_shapes=[pltpu.VMEM((6 * nq, TT, 128), F32)],
        compiler_params=pltpu.CompilerParams(dimension_semantics=("arbitrary",)),
    )(*os_, *lses, proj)


def _att_gate_bwd(dain, att, lse, proj, dproj):
    nq = ATT_W // 128

    def body(d_ref, att_ref, lse_ref, za_ref, buf_ref, dza_ref, da0, da1, da2, at1, at2, ls1, ls2, sc):
        del buf_ref
        for j in range(nq):
            cols = slice(j * 128, (j + 1) * 128)
            za = za_ref[:, cols]
            sg = _sigmoid(za)
            d = d_ref[:, cols].astype(F32)
            att_ = att_ref[:, cols]
            dza_ref[:, cols] = (d * att_ * sg * (1.0 + za * (1.0 - sg))).astype(BF16)
            sc[j] = d * za * sg
            sc[nq + j] = att_
            sc[2 * nq + j] = lse_ref[:, cols]
        for j in range(nq):
            cols = slice(j * 128, (j + 1) * 128)
            for grp, dst in enumerate((da0, da1, da2)):
                _to_residues(sc, j, dst, DILATIONS[grp], cols)
            for grp, dst in ((1, at1), (2, at2)):
                _to_residues(sc, nq + j, dst, DILATIONS[grp], cols)
            for grp, dst in ((1, ls1), (2, ls2)):
                _to_residues(sc, 2 * nq + j, dst, DILATIONS[grp], cols)

    res = (0, 1, 2, 1, 2, 1, 2)
    return pl.pallas_call(
        body, name="att_gate_bwd", grid=(T // TT,),
        in_specs=[_tok_spec(ATT_W)] * 3 + [_tok_spec(ATT_W, C_ZA // ATT_W), pl.BlockSpec(memory_space=pl.ANY)],
        out_specs=[_tok_spec(ATT_W, C_ZA // ATT_W)] + [_res_spec(g) for g in res],
        out_shape=[S(dproj.shape, dproj.dtype)] + [_res_shape(g, BF16) for g in res[:5]]
        + [_res_shape(g, F32) for g in res[5:]],
        input_output_aliases={4: 0},
        scratch_shapes=[pltpu.VMEM((3 * nq, TT, 128), F32)],
        compiler_params=pltpu.CompilerParams(dimension_semantics=("arbitrary",)),
    )(dain, att, lse, proj, dproj)


def _split3(v):
    hi = v.astype(BF16)
    r1 = v - hi.astype(F32)
    mid = r1.astype(BF16)
    lo = (r1 - mid.astype(F32)).astype(BF16)
    return hi, mid, lo


def _tri_sum(v, upper):
    n = v.shape[0]
    row = lax.broadcasted_iota(jnp.int32, (n, n), 0)
    col = lax.broadcasted_iota(jnp.int32, (n, n), 1)
    tri = jnp.where(col >= row if upper else col <= row, 1.0, 0.0).astype(BF16)
    hi, mid, lo = _split3(v)
    return _nn(tri, hi) + _nn(tri, mid) + _nn(tri, lo)


def _gla_gates(glr_ref, w2_ref, b_ref):
    logit = _nn(glr_ref[...].astype(BF16), w2_ref[...]) + b_ref[...]
    lg = (jnp.minimum(logit, 0.0) - jnp.log(1.0 + jnp.exp(-jnp.abs(logit)))) * (1.0 / GLA_TAU)
    return logit, _tri_sum(lg, upper=False)


def _gla_head(cum, q_ref, k_ref, h):
    cols = slice(h * GDK, (h + 1) * GDK)
    b = cum[:, cols]
    last = b[GLA_C - 1:GLA_C, :]
    e_pos = jnp.exp(b)
    e_neg = jnp.exp(-b)
    e_end = jnp.exp(last - b)
    qt = q_ref[:, cols] * (GDK ** -0.5) * e_pos
    kt = k_ref[:, cols] * e_neg
    kh = k_ref[:, cols] * e_end
    return b, last, e_pos, e_neg, e_end, qt, kt, kh


def _causal(n):
    return lax.broadcasted_iota(jnp.int32, (n, n), 1) <= lax.broadcasted_iota(jnp.int32, (n, n), 0)


def _gla_fwd(proj, w2p, bg, gn):
    nc = T // GLA_C

    def body(q_ref, k_ref, v_ref, glr_ref, zg_ref, w2_ref, b_ref, gn_ref, o_ref, bin_ref, st_ref, state):
        @pl.when(pl.program_id(0) == 0)
        def _():
            state[...] = jnp.zeros_like(state)

        _, cum = _gla_gates(glr_ref, w2_ref, b_ref)
        for h in range(GH):
            _, last, _, _, _, qt, kt, kh = _gla_head(cum, q_ref, k_ref, h)
            vcols = slice(h * GDV, (h + 1) * GDV)
            st = state[h]
            st_ref[0, h] = st
            v = v_ref[:, vcols].astype(BF16)
            qb = qt.astype(BF16)
            a = jnp.where(_causal(GLA_C), _nt(qb, kt.astype(BF16)), 0.0)
            o = _nt(qb, st.astype(BF16)) + _nn(a.astype(BF16), v)
            state[h] = st * jnp.exp(last) + _tn(v, kh.astype(BF16))
            o_ref[:, vcols] = o
            r = lax.rsqrt(jnp.mean(o * o, axis=-1, keepdims=True) + EPS)
            zg = zg_ref[:, vcols]
            bin_ref[:, vcols] = (o * r * gn_ref[...] * zg * _sigmoid(zg)).astype(BF16)

    row = lambda width, cblk: pl.BlockSpec((GLA_C, width), functools.partial(lambda i, c: (i, c), c=cblk))
    full = lambda a: pl.BlockSpec(a.shape, functools.partial(lambda i, nd: (0,) * nd, nd=a.ndim))
    return pl.pallas_call(
        body, name="gla_fwd", grid=(nc,),
        in_specs=[row(512, C_QG // 512), row(512, C_KG // 512), row(1024, C_VG // 1024), row(GLR_W, C_GLR // GLR_W),
                  row(1024, C_ZG // 1024), full(w2p), full(bg), full(gn)],
        out_specs=[pl.BlockSpec((GLA_C, GH * GDV), lambda i: (i, 0)), pl.BlockSpec((GLA_C, GH * GDV), lambda i: (i, 0)),
                   pl.BlockSpec((1, GH, GDV, GDK), lambda i: (i, 0, 0, 0))],
        out_shape=[S((T, GH * GDV), F32), S((T, GH * GDV), BF16), S((nc, GH, GDV, GDK), F32)],
        scratch_shapes=[pltpu.VMEM((GH, GDV, GDK), F32)],
        compiler_params=pltpu.CompilerParams(dimension_semantics=("arbitrary",)),
    )(proj, proj, proj, proj, proj, w2p, bg, gn)


def _gla_bwd(proj, w2p, bg, gn, o_gla, states, dbin, dproj):
    nc = T // GLA_C

    def body(q_ref, k_ref, v_ref, glr_ref, zg_ref, w2_ref, b_ref, gn_ref, o_ref, st_ref, dbin_ref, buf_ref,
             out_ref, dw2_ref, dbg_ref, dgn_ref, dstate, dlogit):
        del buf_ref
        dq_ref = out_ref.at[:, C_QG:C_KG]
        dk_ref = out_ref.at[:, C_KG:C_VG]
        dv_ref = out_ref.at[:, C_VG:C_ZG]
        dzg_ref = out_ref.at[:, C_ZG:C_GLR]
        dglr_ref = out_ref.at[:, C_GLR:C_GLR + GLR_W]
        first = pl.program_id(0) == 0

        @pl.when(first)
        def _():
            dstate[...] = jnp.zeros_like(dstate)

        logit, cum = _gla_gates(glr_ref, w2_ref, b_ref)
        is_last = lax.broadcasted_iota(jnp.int32, (GLA_C, 1), 0) == GLA_C - 1
        dgn = jnp.zeros((1, GDV), F32)
        for h in range(GH):
            _, last, e_pos, e_neg, e_end, qt, kt, kh = _gla_head(cum, q_ref, k_ref, h)
            cols = slice(h * GDK, (h + 1) * GDK)
            vcols = slice(h * GDV, (h + 1) * GDV)
            o = o_ref[:, vcols]
            r = lax.rsqrt(jnp.mean(o * o, axis=-1, keepdims=True) + EPS)
            zg = zg_ref[:, vcols]
            sg = _sigmoid(zg)
            db_ = dbin_ref[:, vcols].astype(F32)
            dlin = db_ * zg * sg
            dzg_ref[:, vcols] = (db_ * (o * r * gn_ref[...]) * sg * (1.0 + zg * (1.0 - sg))).astype(BF16)
            u = dlin * gn_ref[...]
            do = (r * u - o * (r * r * r) * jnp.mean(u * o, axis=-1, keepdims=True)).astype(BF16)
            dgn = dgn + jnp.sum(dlin * o * r, axis=0, keepdims=True)
            st = st_ref[0, h]
            dst = dstate[h]
            v = v_ref[:, vcols].astype(BF16)
            qb, kb, khb = qt.astype(BF16), kt.astype(BF16), kh.astype(BF16)
            dstb = dst.astype(BF16)
            causal = _causal(GLA_C)
            a = jnp.where(causal, _nt(qb, kb), 0.0).astype(BF16)
            da = jnp.where(causal, _nt(do, v), 0.0).astype(BF16)
            dqt = _nn(do, st.astype(BF16)) + _nn(da, kb)
            dkt = _tn(da, qb)
            dkh = _nn(v, dstb)
            dv_ref[:, vcols] = (_tn(a, do) + _nt(khb, dstb)).astype(BF16)
            lam = jnp.exp(last)
            dlam = jnp.sum(dst * st, axis=0, keepdims=True)
            dstate[h] = dst * lam + _tn(do, qb)
            dq_ref[:, cols] = (dqt * e_pos * (GDK ** -0.5)).astype(BF16)
            dk_ref[:, cols] = (dkt * e_neg + dkh * e_end).astype(BF16)
            dkh_kh = dkh * kh
            dcum = dqt * qt - dkt * kt - dkh_kh
            dlast = jnp.sum(dkh_kh, axis=0, keepdims=True) + dlam * lam
            dcum = jnp.where(is_last, dcum + dlast, dcum)
            dlg = _tri_sum(dcum, upper=True)
            dlogit[:, cols] = dlg * (1.0 / GLA_TAU) * (1.0 - _sigmoid(logit[:, cols]))

        dl = dlogit[...]
        dlb = dl.astype(BF16)
        dglr_ref[...] = _nt(dlb, w2_ref[...]).astype(BF16)
        dw2 = _tn(glr_ref[...].astype(BF16), dlb)
        dbg = jnp.sum(dl, axis=0, keepdims=True)

        @pl.when(first)
        def _():
            dw2_ref[...] = dw2
            dbg_ref[...] = dbg
            dgn_ref[...] = dgn

        @pl.when(jnp.logical_not(first))
        def _():
            dw2_ref[...] += dw2
            dbg_ref[...] += dbg
            dgn_ref[...] += dgn

    rev = lambda i: nc - 1 - i
    row = lambda width, cblk: pl.BlockSpec((GLA_C, width), functools.partial(lambda i, c: (rev(i), c), c=cblk))
    full = lambda a: pl.BlockSpec(a.shape, functools.partial(lambda i, nd: (0,) * nd, nd=a.ndim))
    keep = lambda shape: pl.BlockSpec(shape, functools.partial(lambda i, nd: (0,) * nd, nd=len(shape)))
    return pl.pallas_call(
        body, name="gla_bwd", grid=(nc,),
        in_specs=[row(512, C_QG // 512), row(512, C_KG // 512), row(1024, C_VG // 1024), row(GLR_W, C_GLR // GLR_W),
                  row(1024, C_ZG // 1024), full(w2p), full(bg), full(gn), row(GH * GDV, 0),
                  pl.BlockSpec((1, GH, GDV, GDK), lambda i: (rev(i), 0, 0, 0)), row(GH * GDV, 0),
                  pl.BlockSpec(memory_space=pl.ANY)],
        out_specs=[row(GLA_GROUP_W, 0), keep((GLR_W, 512)), keep((1, 512)), keep((1, GDV))],
        out_shape=[S(dproj.shape, dproj.dtype), S((GLR_W, 512), F32), S((1, 512), F32), S((1, GDV), F32)],
        input_output_aliases={11: 0},
        scratch_shapes=[pltpu.VMEM((GH, GDV, GDK), F32), pltpu.VMEM((GLA_C, GH * GDK), F32)],
        compiler_params=pltpu.CompilerParams(dimension_semantics=("arbitrary",)),
    )(proj, proj, proj, proj, proj, w2p, bg, gn, o_gla, states, dbin, dproj)


RT = 512


def _rowchain(body, name, ins, outs, scratch=()):
    in_specs, args = [], []
    for spec in ins:
        if spec[0] == "tok":
            _, arr, width, cblk = spec
            in_specs.append(pl.BlockSpec((RT, width), functools.partial(lambda i, c: (i, c), c=cblk)))
        else:
            arr = spec[1]
            in_specs.append(pl.BlockSpec(arr.shape, functools.partial(lambda i, nd: (0,) * nd, nd=arr.ndim)))
        args.append(arr)
    out_specs, out_shape = [], []
    for spec in outs:
        if spec[0] == "tok":
            _, shape, dtype, width, cblk = spec
            out_specs.append(pl.BlockSpec((RT, width), functools.partial(lambda i, c: (i, c), c=cblk)))
        else:
            _, shape, dtype = spec
            out_specs.append(pl.BlockSpec(shape, functools.partial(lambda i, nd: (0,) * nd, nd=len(shape))))
        out_shape.append(S(shape, dtype))
    return pl.pallas_call(
        body, name=name, grid=(T // RT,), in_specs=in_specs, out_specs=out_specs, out_shape=out_shape,
        scratch_shapes=list(scratch), compiler_params=pltpu.CompilerParams(dimension_semantics=("arbitrary",)),
    )(*args)


def _tok(arr, width=None, cblk=0):
    return ("tok", arr, arr.shape[1] if width is None else width, cblk)


def _tok_out(dtype, width=D):
    return ("tok", (T, width), dtype, width, 0)


def _branches_fwd(ain, bin_, proj, x, w_att, w_gla, w_out):
    def body(ain_ref, bin_ref, g_ref, x_ref, wa_ref, wg_ref, wo_ref, ya_ref, yb_ref, y_ref, x1_ref):
        ya = _nn(ain_ref[...], wa_ref[...]).astype(BF16)
        yb = _nn(bin_ref[...], wg_ref[...]).astype(BF16)
        ya_ref[...] = ya
        yb_ref[...] = yb
        y = (_sigmoid(g_ref[:, :D]) * ya.astype(F32) + _sigmoid(g_ref[:, D:]) * yb.astype(F32)).astype(BF16)
        y_ref[...] = y
        x1_ref[...] = x_ref[...] + _nn(y, wo_ref[...])

    return _rowchain(body, "branches_fwd",
                     [_tok(ain), _tok(bin_), _tok(proj, 2 * D, C_GA // (2 * D)), _tok(x), ("all", w_att),
                      ("all", w_gla), ("all", w_out)],
                     [_tok_out(BF16), _tok_out(BF16), _tok_out(BF16), _tok_out(F32)])


def _accumulate(ref, part, first):
    @pl.when(first)
    def _():
        ref[...] = part

    @pl.when(jnp.logical_not(first))
    def _():
        ref[...] += part


def _ple_loss(x1, p, target, g2, w_pg, w_ple):
    def body(x1_ref, p_ref, t_ref, g_ref, wpg_ref, wple_ref, n2_ref, loss_ref, dout_ref, du_ref, dwple_ref, acc):
        first = pl.program_id(0) == 0
        x1 = x1_ref[...]
        r = lax.rsqrt(jnp.mean(x1 * x1, axis=-1, keepdims=True) + EPS)
        n2 = (x1 * r * g_ref[...]).astype(BF16)
        n2_ref[...] = n2
        pg = _sigmoid(_nn(n2, wpg_ref[...]))
        pb = p_ref[...].astype(BF16)
        e_ = _nn(pb, wple_ref[...])
        diff = x1 + e_ * pg - t_ref[...]
        _accumulate(acc, jnp.sum(diff * diff, axis=0, keepdims=True), first)
        dout = diff * (1.0 / D)
        dout_ref[...] = dout
        du_ref[...] = (dout * e_ * pg * (1.0 - pg)).astype(BF16)
        _accumulate(dwple_ref, _tn(pb, (dout * pg).astype(BF16)), first)
        loss_ref[...] = jnp.zeros((1, 128), F32) + jnp.sum(acc[...], axis=-1, keepdims=True) * (0.5 / D)

    return _rowchain(body, "ple_loss", [_tok(x1), _tok(p), _tok(target), ("all", g2), ("all", w_pg), ("all", w_ple)],
                     [_tok_out(BF16), ("acc", (1, 128), F32), _tok_out(F32), _tok_out(BF16), ("acc", (PLE, D), F32)],
                     scratch=[pltpu.VMEM((1, D), F32)])


def _ple_bwd(du, n2, y, x1, dout, g2, w_pg, w_out):
    def body(du_ref, n2_ref, y_ref, x1_ref, dout_ref, g_ref, wpg_ref, wo_ref, dx_ref, dy_ref, dg_ref, dwpg_ref,
             dwo_ref):
        first = pl.program_id(0) == 0
        x1 = x1_ref[...]
        r = lax.rsqrt(jnp.mean(x1 * x1, axis=-1, keepdims=True) + EPS)
        du_ = du_ref[...]
        dn = _nt(du_, wpg_ref[...])
        u = dn * g_ref[...]
        dx = dout_ref[...] + r * u - x1 * (r * r * r) * jnp.mean(u * x1, axis=-1, keepdims=True)
        dxb = dx.astype(BF16)
        dx_ref[...] = dx
        dy_ref[...] = _nt(dxb, wo_ref[...]).astype(BF16)
        _accumulate(dg_ref, jnp.sum(dn * x1 * r, axis=0, keepdims=True), first)
        _accumulate(dwpg_ref, _tn(n2_ref[...], du_), first)
        _accumulate(dwo_ref, _tn(y_ref[...], dxb), first)

    return _rowchain(body, "ple_bwd",
                     [_tok(du), _tok(n2), _tok(y), _tok(x1), _tok(dout), ("all", g2), ("all", w_pg), ("all", w_out)],
                     [_tok_out(F32), _tok_out(BF16), ("acc", (1, D), F32), ("acc", (D, D), F32), ("acc", (D, D), F32)])


def _branches_bwd(dy, ya, yb, ain, bin_, proj, w_att, w_gla):
    def body(dy_ref, ya_ref, yb_ref, ain_ref, bin_ref, g_ref, wa_ref, wg_ref, dg_ref, dain_ref, dbin_ref,
             dwa_ref, dwg_ref):
        first = pl.program_id(0) == 0
        dy_ = dy_ref[...].astype(F32)
        sa, sb = _sigmoid(g_ref[:, :D]), _sigmoid(g_ref[:, D:])
        dg_ref[:, :D] = (dy_ * ya_ref[...].astype(F32) * sa * (1.0 - sa)).astype(BF16)
        dg_ref[:, D:] = (dy_ * yb_ref[...].astype(F32) * sb * (1.0 - sb)).astype(BF16)
        dya = (dy_ * sa).astype(BF16)
        dyb = (dy_ * sb).astype(BF16)
        dain_ref[...] = _nt(dya, wa_ref[...]).astype(BF16)
        dbin_ref[...] = _nt(dyb, wg_ref[...]).astype(BF16)
        _accumulate(dwa_ref, _tn(ain_ref[...], dya), first)
        _accumulate(dwg_ref, _tn(bin_ref[...], dyb), first)

    gates = C_GA // (2 * D)
    return _rowchain(body, "branches_bwd",
                     [_tok(dy), _tok(ya), _tok(yb), _tok(ain), _tok(bin_), _tok(proj, 2 * D, gates), ("all", w_att),
                      ("all", w_gla)],
                     [("tok", (T, NCOL), BF16, 2 * D, gates), _tok_out(BF16, ATT_W), _tok_out(BF16),
                      ("acc", (ATT_W, D), F32), ("acc", (D, D), F32)])


def _peer(k):
    x, y, c = lax.axis_index("x"), lax.axis_index("y"), lax.axis_index("c")
    return (x ^ ((k >> 2) & 1), y ^ ((k >> 1) & 1), c ^ (k & 1))


def _my_index():
    return 4 * lax.axis_index("x") + 2 * lax.axis_index("y") + lax.axis_index("c")


def _peer_index(k):
    px, py, pc = _peer(k)
    return 4 * px + 2 * py + pc


def _pairwise_plan(src_of, dst_of, landed_of, own_src, own_dst):
    def plan(ins, outs, send, recv, local):
        n = len(ins)

        def own():
            return [pltpu.make_async_copy(own_src(ins[a]), own_dst(outs[a]), local.at[a]) for a in range(n)]

        def remote(k, a, src, dst):
            return pltpu.make_async_remote_copy(src_ref=src, dst_ref=dst, send_sem=send.at[k - 1, a],
                                                recv_sem=recv.at[k - 1, a], device_id=_peer(k), device_id_type=MESH)

        def sent():
            return [remote(k, a, src_of(ins[a], k), dst_of(outs[a])) for k in range(1, NDEV) for a in range(n)]

        def start():
            for cp in own() + sent():
                cp.start()

        def finish():
            for k in range(1, NDEV):
                for a in range(n):
                    remote(k, a, own_src(ins[a]), landed_of(outs[a], k)).wait_recv()
            for cp in sent():
                cp.wait_send()
            for cp in own():
                cp.wait()

        return start, finish

    return plan


def _pairwise_sems(n):
    return [pltpu.SemaphoreType.DMA((NDEV - 1, n)), pltpu.SemaphoreType.DMA((NDEV - 1, n)),
            pltpu.SemaphoreType.DMA((n,))]


def _gather_side(arrs):
    plan = _pairwise_plan(src_of=lambda i, k: i, dst_of=lambda o: o.at[_my_index()],
                          landed_of=lambda o, k: o.at[_peer_index(k)],
                          own_src=lambda i: i, own_dst=lambda o: o.at[_my_index()])
    return dict(arrs=arrs, out_shape=[S((NDEV,) + a.shape, a.dtype) for a in arrs],
                scratch=_pairwise_sems(len(arrs)), plan=plan)


def _exchange_side(arrs):
    plan = _pairwise_plan(src_of=lambda i, k: i.at[_peer_index(k)], dst_of=lambda o: o.at[_my_index()],
                          landed_of=lambda o, k: o.at[_peer_index(k)],
                          own_src=lambda i: i.at[_my_index()], own_dst=lambda o: o.at[_my_index()])
    return dict(arrs=arrs, out_shape=[S(a.shape, a.dtype) for a in arrs], scratch=_pairwise_sems(len(arrs)), plan=plan)


def _comm_call(side, name):
    n = len(side["arrs"])

    def body(*refs):
        start, finish = side["plan"](refs[:n], refs[n:2 * n], *refs[2 * n:])
        start()
        finish()

    hbm = pl.BlockSpec(memory_space=pl.ANY)
    return pl.pallas_call(body, name=name, in_specs=[hbm] * n, out_specs=[hbm] * n, out_shape=side["out_shape"],
                          scratch_shapes=side["scratch"])(*side["arrs"])


def _all_gather_by_chip(arrs, name):
    n = len(arrs)

    def body(*refs):
        ins, outs = refs[:n], refs[n:2 * n]
        send, recv, local = refs[2 * n:]
        x, y, c = lax.axis_index("x"), lax.axis_index("y"), lax.axis_index("c")
        me, sibling = (x, y, c), (x, y, 1 - c)
        chips = [(1 - x, y), (x, 1 - y), (1 - x, 1 - y)]

        def copy(k, a, block, to, src=None):
            px, py, pc = block
            slot = outs[a].at[4 * px + 2 * py + pc]
            return pltpu.make_async_remote_copy(
                src_ref=slot if src is None else src, dst_ref=slot, send_sem=send.at[k, a], recv_sem=recv.at[k, a],
                device_id=to, device_id_type=MESH)

        north = c == 1
        via = (jnp.where(north, 1 - x, x), jnp.where(north, y, 1 - y))
        onward = (jnp.where(north, x, 1 - x), jnp.where(north, 1 - y, y), c)
        mine = [pltpu.make_async_copy(ins[a], outs[a].at[4 * x + 2 * y + c], local.at[a]) for a in range(n)]
        first = []
        for a in range(n):
            first.append(copy(0, a, me, sibling, src=ins[a]))
            first += [copy(1 + j, a, me, (*chips[j], c), src=ins[a]) for j in range(2)]
        for cp in mine + first:
            cp.start()
        passed = []
        for j in range(2):
            for a in range(n):
                copy(1 + j, a, (*chips[j], c), me).wait_recv()
                passed.append(copy(4 + j, a, (*chips[j], c), sibling))
                passed[-1].start()
        for a in range(n):
            passed.append(copy(3, a, (*via, c), onward))
            passed[-1].start()
        for a in range(n):
            copy(3, a, (*chips[2], c), me).wait_recv()
            passed.append(copy(6, a, (*chips[2], c), sibling))
            passed[-1].start()
        for a in range(n):
            copy(0, a, sibling, me).wait_recv()
        for j, chip in enumerate(chips):
            for a in range(n):
                copy(4 + j, a, (*chip, 1 - c), me).wait_recv()
        for cp in first + passed:
            cp.wait_send()
        for cp in mine:
            cp.wait()

    hbm = pl.BlockSpec(memory_space=pl.ANY)
    return pl.pallas_call(
        body, name=name, in_specs=[hbm] * n, out_specs=[hbm] * n,
        out_shape=[S((NDEV,) + a.shape, a.dtype) for a in arrs],
        scratch_shapes=[pltpu.SemaphoreType.DMA((NDEV - 1, n)), pltpu.SemaphoreType.DMA((NDEV - 1, n)),
                        pltpu.SemaphoreType.DMA((n,))],
    )(*arrs)


NCHIP = 4


def _exchange_sibling(arrs, name):
    n = len(arrs)

    def body(*refs):
        ins, outs = refs[:n], refs[n:2 * n]
        send, recv = refs[2 * n:]
        x, y, c = lax.axis_index("x"), lax.axis_index("y"), lax.axis_index("c")
        copies = []
        for q in range(NCHIP):
            for a in range(n):
                copies.append(pltpu.make_async_remote_copy(
                    src_ref=ins[a].at[2 * q + (1 - c)], dst_ref=outs[a].at[q], send_sem=send.at[q, a],
                    recv_sem=recv.at[q, a], device_id=(x, y, 1 - c), device_id_type=MESH))
        for cp in copies:
            cp.start()
        for cp in copies:
            cp.wait_recv()
        for cp in copies:
            cp.wait_send()

    hbm = pl.BlockSpec(memory_space=pl.ANY)
    return pl.pallas_call(
        body, name=name, in_specs=[hbm] * n, out_specs=[hbm] * n,
        out_shape=[S((NCHIP,) + a.shape[1:], a.dtype) for a in arrs],
        scratch_shapes=[pltpu.SemaphoreType.DMA((NCHIP, n)), pltpu.SemaphoreType.DMA((NCHIP, n))],
    )(*arrs)


def _pair_add(mine, got, core, name):
    _, rows, cols = mine.shape
    tc = 256
    assert cols % tc == 0

    def body(core_ref, a_ref, b_ref, o_ref):
        o_ref[...] = (a_ref[...].astype(F32) + b_ref[...].astype(F32)).astype(BF16)

    return pl.pallas_call(
        body, name=name,
        grid_spec=pltpu.PrefetchScalarGridSpec(
            num_scalar_prefetch=1, grid=(NCHIP, cols // tc),
            in_specs=[pl.BlockSpec((None, rows, tc), lambda q, i, core_ref: (2 * q + core_ref[0], 0, i)),
                      pl.BlockSpec((None, rows, tc), lambda q, i, core_ref: (q, 0, i))],
            out_specs=pl.BlockSpec((None, rows, tc), lambda q, i, core_ref: (q, 0, i))),
        out_shape=S((NCHIP, rows, cols), BF16),
    )(core, mine, got)


def _chips_side(arrs):
    def plan(ins, outs, send, recv, local):
        n = len(ins)

        def places():
            x, y, c = lax.axis_index("x"), lax.axis_index("y"), lax.axis_index("c")
            return 2 * x + y, c, [(1 - x, y), (x, 1 - y), (1 - x, 1 - y)]

        def own():
            here, _, _ = places()
            return [pltpu.make_async_copy(ins[a].at[here], outs[a].at[here], local.at[a]) for a in range(n)]

        def remote(j, a, src_slot, dst_slot):
            _, c, chips = places()
            cx, cy = chips[j]
            return pltpu.make_async_remote_copy(
                src_ref=ins[a].at[src_slot], dst_ref=outs[a].at[dst_slot], send_sem=send.at[j, a],
                recv_sem=recv.at[j, a], device_id=(cx, cy, c), device_id_type=MESH)

        def sent():
            here, _, chips = places()
            return [remote(j, a, 2 * cx + cy, here) for j, (cx, cy) in enumerate(chips) for a in range(n)]

        def start():
            for cp in own() + sent():
                cp.start()

        def finish():
            here, _, chips = places()
            for j, (cx, cy) in enumerate(chips):
                for a in range(n):
                    remote(j, a, here, 2 * cx + cy).wait_recv()
            for cp in sent():
                cp.wait_send()
            for cp in own():
                cp.wait()

        return start, finish

    n = len(arrs)
    return dict(arrs=arrs, out_shape=[S(a.shape, a.dtype) for a in arrs],
                scratch=[pltpu.SemaphoreType.DMA((NCHIP - 1, n)), pltpu.SemaphoreType.DMA((NCHIP - 1, n)),
                         pltpu.SemaphoreType.DMA((n,))], plan=plan)


def _adamw(parts, w, m, v, name, tr, tc=None):
    rows, cols = w.shape
    if tc is None:
        assert rows % tr == 0
        grid, shape, at = (rows // tr,), (tr, cols), (lambda i: (i, 0))
    else:
        assert cols % tc == 0
        grid, shape, at = (cols // tc,), (rows, tc), (lambda i: (0, i))
    c1 = 1.0 - ADAM_B1 ** ADAM_STEP
    c2 = 1.0 - ADAM_B2 ** ADAM_STEP

    nparts = parts.shape[0]

    def body(p_ref, w_ref, m_ref, v_ref, g_ref, d_ref, mo_ref, vo_ref):
        g = p_ref[0].astype(F32)
        for s in range(1, nparts):
            g = g + p_ref[s].astype(F32)
        m_new = ADAM_B1 * m_ref[...] + (1.0 - ADAM_B1) * g
        v_new = ADAM_B2 * v_ref[...] + (1.0 - ADAM_B2) * (g * g)
        g_ref[...] = g
        mo_ref[...] = m_new
        vo_ref[...] = v_new
        d_ref[...] = -ADAM_LR * ((m_new / c1) / (jnp.sqrt(v_new / c2) + ADAM_EPS) + ADAM_WD * w_ref[...])

    blk = pl.BlockSpec(shape, at)
    return pl.pallas_call(
        body, name=name, grid=grid,
        in_specs=[pl.BlockSpec((nparts,) + shape, lambda i: (0,) + at(i)), blk, blk, blk],
        out_specs=[blk] * 4, out_shape=[S((rows, cols), F32)] * 4,
        compiler_params=pltpu.CompilerParams(dimension_semantics=("parallel",)),
    )(parts, w, m, v)


def _adam_math(g, w, m, v):
    c1 = 1.0 - ADAM_B1 ** ADAM_STEP
    c2 = 1.0 - ADAM_B2 ** ADAM_STEP
    m_new = ADAM_B1 * m + (1.0 - ADAM_B1) * g
    v_new = ADAM_B2 * v + (1.0 - ADAM_B2) * (g * g)
    return -ADAM_LR * ((m_new / c1) / (jnp.sqrt(v_new / c2) + ADAM_EPS) + ADAM_WD * w), m_new, v_new


def _adamw_small(parts, params, loss_parts):
    n = len(params)

    def body(*refs):
        p_refs, rest = refs[:n], refs[n + 1:]
        total = refs[n][0]
        for s in range(1, NDEV):
            total = total + refs[n][s]
        refs[-1][...] = total
        for j in range(n):
            w_ref, m_ref, v_ref = rest[3 * j:3 * j + 3]
            g_ref, d_ref, mo_ref, vo_ref = rest[3 * n + 4 * j:3 * n + 4 * j + 4]
            width = w_ref.shape[1]
            g = p_refs[j][0]
            for s in range(1, NDEV):
                g = g + p_refs[j][s]
            g = g[:, :width]
            delta, m_new, v_new = _adam_math(g, w_ref[...], m_ref[...], v_ref[...])
            g_ref[...] = g
            d_ref[...] = delta
            mo_ref[...] = m_new
            vo_ref[...] = v_new

    flat = [a for group in params for a in group]
    return pl.pallas_call(
        body, name="adam_small",
        out_shape=[S(group[0].shape, F32) for group in params for _ in range(4)] + [S((1, 128), F32)],
    )(*parts, loss_parts, *flat)


def _adamw_rows(parts, w, m, v, name, tc=128):
    rows, _, cols = w.shape
    nparts = parts.shape[0]
    nsteps = cols // tc

    def body(p_ref, w_hbm, m_hbm, v_hbm, g_hbm, d_hbm, mo_hbm, vo_hbm, inbuf, outbuf, insem, outsem):
        i = pl.program_id(0)
        slot = i & 1

        def view(ref, step):
            return ref.at[:, 0, pl.ds(pl.multiple_of(step * tc, tc), tc)]

        def fetch(step, sl):
            return [pltpu.make_async_copy(view(src, step), inbuf.at[sl, k], insem.at[sl, k])
                    for k, src in enumerate((w_hbm, m_hbm, v_hbm))]

        def write(step, sl):
            return [pltpu.make_async_copy(outbuf.at[sl, k], view(dst, step), outsem.at[sl, k])
                    for k, dst in enumerate((g_hbm, d_hbm, mo_hbm, vo_hbm))]

        @pl.when(i == 0)
        def _():
            for cp in fetch(0, 0):
                cp.start()

        @pl.when(i + 1 < nsteps)
        def _():
            for cp in fetch(i + 1, 1 - slot):
                cp.start()

        for cp in fetch(i, slot):
            cp.wait()

        @pl.when(i >= 2)
        def _():
            for cp in write(i - 2, slot):
                cp.wait()

        g = p_ref[0].astype(F32)
        for s in range(1, nparts):
            g = g + p_ref[s].astype(F32)
        delta, m_new, v_new = _adam_math(g, inbuf[slot, 0], inbuf[slot, 1], inbuf[slot, 2])
        for k, val in enumerate((g, delta, m_new, v_new)):
            outbuf[slot, k] = val
        for cp in write(i, slot):
            cp.start()

        @pl.when(i == nsteps - 1)
        def _():
            for cp in write(i - 1, 1 - slot) + write(i, slot):
                cp.wait()

    hbm = pl.BlockSpec(memory_space=pl.ANY)
    assert nsteps >= 2
    return pl.pallas_call(
        body, name=name, grid=(nsteps,),
        in_specs=[pl.BlockSpec((nparts, rows, tc), lambda i: (0, 0, i)), hbm, hbm, hbm],
        out_specs=[hbm] * 4, out_shape=[S((rows, 1, cols), F32)] * 4,
        scratch_shapes=[pltpu.VMEM((2, 3, rows, tc), F32), pltpu.VMEM((2, 4, rows, tc), F32),
                        pltpu.SemaphoreType.DMA((2, 3)), pltpu.SemaphoreType.DMA((2, 4))],
        compiler_params=pltpu.CompilerParams(dimension_semantics=("arbitrary",)),
    )(parts, w, m, v)


def _to_aligned(wt):
    pad = jnp.zeros((GLR_W - GLR_N, wt.shape[1]), wt.dtype)
    return jnp.concatenate([wt[O_QG:O_GLR], wt[O_ZG:O_GA], wt[O_GLR:O_ZG], pad, wt[O_ZA:O_QG], wt[O_GA:O_END],
                            wt[O_QA:O_ZA]], axis=0)


def _from_aligned(wt):
    return jnp.concatenate([wt[C_QA:], wt[C_ZA:C_GA], wt[C_QG:C_ZG], wt[C_GLR:C_GLR + GLR_N], wt[C_ZG:C_GLR],
                            wt[C_GA:C_QA]], axis=0)


def _col_blocks(w, width):
    return w.reshape(w.shape[0], NDEV, width).transpose(1, 0, 2)


def _from_col_blocks(w):
    return w.transpose(1, 0, 2).reshape(w.shape[1], NDEV * w.shape[2])


def _local_step(x2, p2, pos, tgt, norm_g, qk_norm_q, qk_norm_k, gla_gate_b, gla_norm_g, ple_norm_g, w_al,
                weights=None, proj_side=None, unpack=None, dw_side_of=None, dh_side_of=None):
    half = ROT_DIM // 2
    inv8 = jnp.power(jnp.float32(ROPE_THETA), -jnp.arange(half, dtype=F32) * 2.0 / ROT_DIM)
    inv = jnp.tile(jnp.concatenate([inv8, inv8, jnp.zeros((HD - ROT_DIM,), F32)]), 2).reshape(1, 128)
    gq = jnp.tile(qk_norm_q, (1, 2))
    gk = jnp.tile(qk_norm_k, (1, 2))

    proj, h, got = _proj_rms(x2, norm_g, w_al, proj_side)
    if proj_side is not None:
        weights = unpack(got)
    w2p, w_att_f, w_gla_f, w_out_f, w_pg_f, w_ple_f = weights
    qkv = _qk_prep(proj, pos, inv, gq, gk)
    fwd = [_att_fwd(qkv[g], qkv[3 + g], qkv[6 + g], g, f"att_fwd{g}") for g in range(3)]
    att, lse, ain = _att_merge([f[0] for f in fwd], [f[1] for f in fwd], proj)
    o_gla, bin_, states = _gla_fwd(proj, w2p, gla_gate_b, gla_norm_g)
    ya, yb, y, x1 = _branches_fwd(ain, bin_, proj, x2, w_att_f, w_gla_f, w_out_f)
    n2, loss_v, dout, du, dw_ple = _ple_loss(x1, p2, tgt, ple_norm_g, w_pg_f, w_ple_f)

    dx1, dy, dg_ple, dw_pg, dw_out = _ple_bwd(du, n2, y, x1, dout, ple_norm_g, w_pg_f, w_out_f)
    dproj, dain, dbin, dw_att, dw_gla = _branches_bwd(dy, ya, yb, ain, bin_, proj, w_att_f, w_gla_f)
    dproj, da0, da1, da2, at1, at2, ls1, ls2 = _att_gate_bwd(dain, att, lse, proj, dproj)
    datts, atts, lses = (da0, da1, da2), (att[None], at1, at2), (lse[None], ls1, ls2)
    dproj, dw2, dbg, dgn = _gla_bwd(proj, w2p, gla_gate_b, gla_norm_g, o_gla, states, dbin, dproj)
    bwd = [_att_bwd(qkv[g], qkv[3 + g], qkv[6 + g], datts[g], atts[g], lses[g], g, f"att_bwd{g}") for g in range(3)]
    dproj, dgq, dgk = _qk_bwd(proj, pos, inv, gq, gk, [b[0] for b in bwd], [b[1] for b in bwd],
                              [b[2] for b in bwd], dproj)
    out = dict(loss=loss_v, dw2=dw2, dw_att=dw_att, dw_gla=dw_gla, dw_out=dw_out, dw_pg=dw_pg, dw_ple=dw_ple,
               dgq=dgq, dgk=dgk, dbg=dbg, dgn=dgn, dg_ple=dg_ple)
    if dw_side_of is None:
        dw_al = _mm(dproj, h, mode="tn", name="dw_in", tm=1536, tn=D, tk=2048, out_dtype=BF16)
    else:
        dw_al, out["dw_side"] = _mm(dproj, h, mode="tn", name="dw_in", tm=1536, tn=D, tk=2048, out_dtype=BF16,
                                    side=dw_side_of(out))
    grad_x, dg_norm, out["dh_side"] = _dh_rms(dproj, w_al, x2, norm_g, dx1,
                                              None if dh_side_of is None else dh_side_of(dw_al))
    out.update(grad_x=grad_x, dw_al=dw_al, dg_norm=dg_norm)
    return out


def kernel(x, p, positions, norm_g, w_in, qk_norm_q, qk_norm_k, gla_gate_w2, gla_gate_b, gla_norm_g, w_att_proj, w_gla_proj, w_out, ple_norm_g, w_ple_gate, w_ple, loss_target, m_norm_g, m_w_in, m_qk_norm_q, m_qk_norm_k, m_gla_gate_w2, m_gla_gate_b, m_gla_norm_g, m_w_att_proj, m_w_gla_proj, m_w_out, m_ple_norm_g, m_w_ple_gate, m_w_ple, v_norm_g, v_w_in, v_qk_norm_q, v_qk_norm_k, v_gla_gate_w2, v_gla_gate_b, v_gla_norm_g, v_w_att_proj, v_w_gla_proj, v_w_out, v_ple_norm_g, v_w_ple_gate, v_w_ple):
    x2, p2, tgt = x[0], p[0, 0], loss_target[0]
    pos = positions.astype(F32).reshape(T, 1)

    rows3 = jnp.stack([w_gla_proj[0], w_out[0], w_ple_gate[0]]).astype(BF16)
    cols3 = jnp.concatenate([w_att_proj[0], w_ple[0], jnp.pad(gla_gate_w2[0], ((0, 0), (0, 64)))], axis=0).astype(BF16)
    (g_in,) = _all_gather_by_chip([w_in[0].T.astype(BF16)], "gather_w_in")
    w_al = _to_aligned(g_in.reshape(W_IN_COLS, D))

    def unpack(got):
        g_rows, g_cols = got
        w2_f = _from_col_blocks(g_cols[:, 768:784, :64])
        return (jnp.pad(w2_f, ((0, GLR_W - GLR_N), (0, 0))), _from_col_blocks(g_cols[:, :512]),
                g_rows[:, 0].reshape(D, D), g_rows[:, 1].reshape(D, D), g_rows[:, 2].reshape(D, D),
                _from_col_blocks(g_cols[:, 512:768]))

    def dw_side_of(g):
        s_rows = jnp.concatenate([g[k].reshape(NDEV, 128, D) for k in ("dw_gla", "dw_out", "dw_pg")], axis=1)
        s_cols = jnp.concatenate([_col_blocks(g["dw_att"], 128), _col_blocks(g["dw_ple"], 128),
                                  jnp.pad(_col_blocks(g["dw2"][:GLR_N], 64), ((0, 0), (0, 0), (0, 64)))], axis=1)
        return _exchange_side([s_rows.astype(BF16), s_cols.astype(BF16)])

    def dh_side_of(dw_al):
        s_in = _from_aligned(dw_al).reshape(NDEV, W_IN_SHARD, D)
        (from_sibling,) = _exchange_sibling([s_in], "exchange_sibling")
        core = lax.axis_index("c").astype(jnp.int32).reshape(1)
        return _chips_side([_pair_add(s_in, from_sibling, core, "pair_add")])

    loc = _local_step(x2, p2, pos, tgt, norm_g, qk_norm_q, qk_norm_k, gla_gate_b, gla_norm_g, ple_norm_g, w_al,
                      proj_side=_gather_side([rows3, cols3]), unpack=unpack, dw_side_of=dw_side_of,
                      dh_side_of=dh_side_of)
    loss_v, grad_x = loc["loss"], loc["grad_x"]
    dg_norm, dgq, dgk, dbg, dgn, dg_ple = (loc[k] for k in ("dg_norm", "dgq", "dgk", "dbg", "dgn", "dg_ple"))
    r_rows, r_cols = loc["dw_side"]
    (r_in,) = loc["dh_side"]

    r_small = _comm_call(_gather_side([dg_norm, dgq, dgk, dbg, dgn, dg_ple, loss_v]), "gather_small")

    outs = {}

    def adam(nm, parts, w, m, v, tr):
        outs[nm] = _adamw(parts, w, m, v, "adam_" + nm, tr)

    rows_of = lambda a: jnp.transpose(a, (2, 0, 1))
    outs["w_in"] = [jnp.transpose(o, (1, 2, 0))[0] for o in
                    _adamw_rows(r_in, rows_of(w_in), rows_of(m_w_in), rows_of(v_w_in), "adam_w_in")]
    adam("w_gla_proj", r_rows[:, :128], w_gla_proj[0], m_w_gla_proj[0], v_w_gla_proj[0], 128)
    adam("w_out", r_rows[:, 128:256], w_out[0], m_w_out[0], v_w_out[0], 128)
    adam("w_ple_gate", r_rows[:, 256:], w_ple_gate[0], m_w_ple_gate[0], v_w_ple_gate[0], 128)
    adam("w_att_proj", r_cols[:, :512], w_att_proj[0], m_w_att_proj[0], v_w_att_proj[0], 512)
    adam("w_ple", r_cols[:, 512:768], w_ple[0], m_w_ple[0], v_w_ple[0], 256)
    adam("gla_gate_w2", r_cols[:, 768:784, :64], gla_gate_w2[0], m_gla_gate_w2[0], v_gla_gate_w2[0], 16)
    small = ((norm_g, m_norm_g, v_norm_g), (qk_norm_q, m_qk_norm_q, v_qk_norm_q), (qk_norm_k, m_qk_norm_k, v_qk_norm_k),
             (gla_gate_b, m_gla_gate_b, v_gla_gate_b), (gla_norm_g, m_gla_norm_g, v_gla_norm_g),
             (ple_norm_g, m_ple_norm_g, v_ple_norm_g))
    sm = _adamw_small(r_small[:6], small, r_small[6])
    for j, nm in enumerate(("norm_g", "qk_norm_q", "qk_norm_k", "gla_gate_b", "gla_norm_g", "ple_norm_g")):
        outs[nm] = [o[0] for o in sm[4 * j:4 * j + 4]]

    loss = sm[-1][0, 0]
    order = ["norm_g", "w_in", "qk_norm_q", "qk_norm_k", "gla_gate_w2", "gla_gate_b", "gla_norm_g", "w_att_proj",
             "w_gla_proj", "w_out", "ple_norm_g", "w_ple_gate", "w_ple"]
    result = [loss, grad_x[None]]
    for i in range(4):
        result += [outs[nm][i][None] for nm in order]
    return tuple(result)
```

```python
import functools

import jax
import jax.numpy as jnp
from jax import lax
from jax.experimental import pallas as pl
from jax.experimental.pallas import tpu as pltpu

F32 = jnp.float32
BF16 = jnp.bfloat16
S = jax.ShapeDtypeStruct

T = 4096
D = 1024
NDEV = 8
HD = 64
ATT_W = 512
ATT_QKV = 1536
DILATIONS = (1, 4, 16)
BLK = 128
GH, GDK, GDV = 4, 128, 256
GLA_C = 128
PLE = 256
EPS = 1e-6
ROT_DIM = 16
ROPE_THETA = 500000.0
GLA_TAU = 16.0
W_IN_COLS = 10256
W_IN_SHARD = 1282

C_QG, C_KG, C_VG, C_ZG, C_GLR, C_ZA, C_GA, C_GB, C_QA, C_KA, C_VA = (
    0, 512, 1024, 2048, 3072, 3584, 4096, 5120, 6144, 7680, 9216)
GLA_GROUP_W = 3584
GLR_W = 512
NCOL = 10752
GLR_N = 16
O_QA, O_ZA, O_QG, O_GLR, O_ZG, O_GA, O_END = 0, 4608, 5120, 7168, 7184, 8208, 10256

ADAM_LR, ADAM_B1, ADAM_B2, ADAM_EPS, ADAM_WD, ADAM_STEP = 0.001, 0.9, 0.999, 1e-08, 0.01, 10

MESH = pl.DeviceIdType.MESH


def _sigmoid(z):
    return 1.0 / (1.0 + jnp.exp(-z))


def _dot(a, b, dims):
    return lax.dot_general(a, b, (dims, ((), ())), preferred_element_type=F32)


def _nn(a, b):
    return _dot(a, b, ((1,), (0,)))


def _nt(a, b):
    return _dot(a, b, ((1,), (1,)))


def _tn(a, b):
    return _dot(a, b, ((0,), (0,)))


def _mm(a, b, *, mode, name, tm, tn, tk, out_dtype=F32, res=None, side=None):
    if mode == "nn":
        (m, k), n = a.shape, b.shape[1]
        a_spec = pl.BlockSpec((tm, tk), lambda i, j, l: (i, l))
        b_spec = pl.BlockSpec((tk, tn), lambda i, j, l: (l, j))
        dot = _nn
    elif mode == "nt":
        (m, k), n = a.shape, b.shape[0]
        a_spec = pl.BlockSpec((tm, tk), lambda i, j, l: (i, l))
        b_spec = pl.BlockSpec((tn, tk), lambda i, j, l: (j, l))
        dot = _nt
    else:
        (k, m), n = a.shape, b.shape[1]
        a_spec = pl.BlockSpec((tk, tm), lambda i, j, l: (l, i))
        b_spec = pl.BlockSpec((tk, tn), lambda i, j, l: (l, j))
        dot = _tn
    assert m % tm == 0 and n % tn == 0 and k % tk == 0, (name, m, n, k)
    grid = (m // tm, n // tn, k // tk)
    nk = grid[2]
    o_spec = pl.BlockSpec((tm, tn), lambda i, j, l: (i, j))
    in_specs = [a_spec, b_spec]
    args = [a, b]
    if res is not None:
        in_specs.append(o_spec)
        args.append(res)
    n_in = len(args)
    n_side = 0 if side is None else len(side["arrs"])
    hbm = pl.BlockSpec(memory_space=pl.ANY)

    def body(*refs):
        a_ref, b_ref = refs[0], refs[1]
        r_ref = refs[2] if res is not None else None
        o_ref = refs[n_in + n_side]
        scratch = refs[n_in + 2 * n_side + 1:]
        if side is not None:
            start, finish_side = side["plan"](refs[n_in:n_in + n_side], refs[n_in + n_side + 1:n_in + 2 * n_side + 1],
                                              *scratch[1 if nk > 1 else 0:])
            ids = [pl.program_id(d) for d in range(3)]

            @pl.when((ids[0] == 0) & (ids[1] == 0) & (ids[2] == 0))
            def _():
                start()

        part = dot(a_ref[...].astype(BF16), b_ref[...].astype(BF16))

        def finish(val):
            if r_ref is not None:
                val = val + r_ref[...]
            o_ref[...] = val.astype(out_dtype)

        if nk == 1:
            finish(part)
        else:
            acc = scratch[0]
            l = pl.program_id(2)

            @pl.when(l == 0)
            def _():
                acc[...] = part

            @pl.when(l > 0)
            def _():
                acc[...] += part

            @pl.when(l == nk - 1)
            def _():
                finish(acc[...])

        if side is not None:
            @pl.when((ids[0] == grid[0] - 1) & (ids[1] == grid[1] - 1) & (ids[2] == grid[2] - 1))
            def _():
                finish_side()

    sems = [] if side is None else side["scratch"]
    outs = pl.pallas_call(
        body, name=name, grid=grid,
        in_specs=in_specs + [hbm] * n_side, out_specs=[o_spec] + [hbm] * n_side,
        out_shape=[S((m, n), out_dtype)] + ([] if side is None else side["out_shape"]),
        scratch_shapes=([pltpu.VMEM((tm, tn), F32)] if nk > 1 else []) + sems,
        compiler_params=pltpu.CompilerParams(
            dimension_semantics=("arbitrary",) * 3 if side is not None else ("parallel", "parallel", "arbitrary")),
    )(*args, *([] if side is None else side["arrs"]))
    return outs[0] if side is None else (outs[0], outs[1:])


def _side_parts(side, refs, n_in, n_out):
    n_side = 0 if side is None else len(side["arrs"])
    scratch = refs[n_in + n_out + 2 * n_side:]
    if side is None:
        return (lambda: None), (lambda: None), scratch
    start, finish = side["plan"](refs[n_in:n_in + n_side], refs[n_in + n_side + n_out:n_in + n_out + 2 * n_side],
                                 *scratch[len(scratch) - len(side["scratch"]):])
    return start, finish, scratch


def _proj_rms(x, g, wt, side=None):
    tm, tn = 1024, 1536
    grid = (T // tm, NCOL // tn)
    n_side = 0 if side is None else len(side["arrs"])
    hbm = pl.BlockSpec(memory_space=pl.ANY)

    def body(*refs):
        x_ref, g_ref, w_ref = refs[:3]
        o_ref, h_ref = refs[3 + n_side], refs[4 + n_side]
        start, finish, _ = _side_parts(side, refs, 3, 2)
        i, j = pl.program_id(0), pl.program_id(1)

        @pl.when((i == 0) & (j == 0))
        def _():
            start()

        @pl.when(j == 0)
        def _():
            xf = x_ref[...]
            r = lax.rsqrt(jnp.mean(xf * xf, axis=-1, keepdims=True) + EPS)
            h_ref[...] = (xf * r * g_ref[...]).astype(BF16)

        o_ref[...] = _nt(h_ref[...], w_ref[...])

        @pl.when((i == grid[0] - 1) & (j == grid[1] - 1))
        def _():
            finish()

    outs = pl.pallas_call(
        body, name="proj", grid=grid,
        in_specs=[pl.BlockSpec((tm, D), lambda i, j: (i, 0)), pl.BlockSpec((1, D), lambda i, j: (0, 0)),
                  pl.BlockSpec((tn, D), lambda i, j: (j, 0))] + [hbm] * n_side,
        out_specs=[pl.BlockSpec((tm, tn), lambda i, j: (i, j)), pl.BlockSpec((tm, D), lambda i, j: (i, 0))] + [hbm] * n_side,
        out_shape=[S((T, NCOL), F32), S((T, D), BF16)] + ([] if side is None else side["out_shape"]),
        scratch_shapes=[] if side is None else side["scratch"],
        compiler_params=pltpu.CompilerParams(dimension_semantics=("arbitrary", "arbitrary")),
    )(x, g, wt, *([] if side is None else side["arrs"]))
    return outs[0], outs[1], outs[2:]


def _dh_rms(dproj, wt, x, g, skip, side=None):
    tm, tk = 1024, 1792
    grid = (T // tm, NCOL // tk)
    n_side = 0 if side is None else len(side["arrs"])
    hbm = pl.BlockSpec(memory_space=pl.ANY)

    def body(*refs):
        a_ref, w_ref, x_ref, g_ref, s_ref = refs[:5]
        dx_ref, dg_ref = refs[5 + n_side], refs[6 + n_side]
        start, finish, scratch = _side_parts(side, refs, 5, 2)
        acc = scratch[0]
        i, l = pl.program_id(0), pl.program_id(1)

        @pl.when((i == 0) & (l == 0))
        def _():
            start()

        part = _nn(a_ref[...], w_ref[...])

        @pl.when(l == 0)
        def _():
            acc[...] = part

        @pl.when(l > 0)
        def _():
            acc[...] += part

        @pl.when(l == grid[1] - 1)
        def _():
            xf = x_ref[...]
            r = lax.rsqrt(jnp.mean(xf * xf, axis=-1, keepdims=True) + EPS)
            dn = acc[...]
            u = dn * g_ref[...]
            dx_ref[...] = s_ref[...] + r * u - xf * (r * r * r) * jnp.mean(u * xf, axis=-1, keepdims=True)
            dg = jnp.sum(dn * xf * r, axis=0, keepdims=True)

            @pl.when(i == 0)
            def _():
                dg_ref[...] = dg

            @pl.when(i > 0)
            def _():
                dg_ref[...] += dg

        @pl.when((i == grid[0] - 1) & (l == grid[1] - 1))
        def _():
            finish()

    tok = pl.BlockSpec((tm, D), lambda i, l: (i, 0))
    outs = pl.pallas_call(
        body, name="dh", grid=grid,
        in_specs=[pl.BlockSpec((tm, tk), lambda i, l: (i, l)), pl.BlockSpec((tk, D), lambda i, l: (l, 0)), tok,
                  pl.BlockSpec((1, D), lambda i, l: (0, 0)), tok] + [hbm] * n_side,
        out_specs=[tok, pl.BlockSpec((1, D), lambda i, l: (0, 0))] + [hbm] * n_side,
        out_shape=[S((T, D), F32), S((1, D), F32)] + ([] if side is None else side["out_shape"]),
        scratch_shapes=[pltpu.VMEM((tm, D), F32)] + ([] if side is None else side["scratch"]),
        compiler_params=pltpu.CompilerParams(dimension_semantics=("arbitrary", "arbitrary")),
    )(dproj, wt, x, g, skip, *([] if side is None else side["arrs"]))
    return outs[0], outs[1], outs[2:]


def _rot_tables(pos_ref, inv_ref):
    lane = lax.broadcasted_iota(jnp.int32, (1, 128), 1) % HD
    ang = pos_ref[...] * inv_ref[...]
    cos, sin = jnp.cos(ang), jnp.sin(ang)
    c = jnp.where(lane < ROT_DIM, cos, 1.0)
    sp = jnp.where((lane >= ROT_DIM // 2) & (lane < ROT_DIM), sin, 0.0)
    sm = jnp.where(lane < ROT_DIM // 2, -sin, 0.0)
    return c, sp, sm


def _head_sums(v):
    same = (lax.broadcasted_iota(jnp.int32, (128, 128), 0) < HD) == (lax.broadcasted_iota(jnp.int32, (128, 128), 1) < HD)
    ones = jnp.where(same, 1.0, 0.0).astype(BF16)
    hi = v.astype(BF16)
    lo = (v - hi.astype(F32)).astype(BF16)
    return _nn(hi, ones) + _nn(lo, ones)


def _pair_norm(t):
    return lax.rsqrt(_head_sums(t * t) * (1.0 / HD) + EPS)


def _pair_mean(t):
    return _head_sums(t) * (1.0 / HD)


TT = 256
NCH = ATT_QKV // 128


def _res_shape(grp, dtype):
    return S((DILATIONS[grp], T // DILATIONS[grp], ATT_W), dtype)


def _res_spec(grp):
    dil = DILATIONS[grp]
    return pl.BlockSpec((dil, TT // dil, ATT_W), lambda i: (0, i, 0))


def _to_residues(sc, j, dst_ref, dil, cols):
    n = TT // dil
    for r in range(dil):
        rows = sc[j] if dil == 1 else sc.at[j][pl.ds(r, n, stride=dil), :]
        dst_ref[r, :, cols] = rows.astype(dst_ref.dtype)


def _from_residues(src_ref, cols, sc, j, dil):
    n = TT // dil
    for r in range(dil):
        if dil == 1:
            sc[j] = src_ref[r, :, cols]
        else:
            sc.at[j][pl.ds(r, n, stride=dil), :] = src_ref[r, :, cols]


def _tok_spec(width, cblk=0):
    return pl.BlockSpec((TT, width), functools.partial(lambda i, c: (i, c), c=cblk))


def _const_spec(arr_or_shape):
    shape = arr_or_shape if isinstance(arr_or_shape, tuple) else arr_or_shape.shape
    return pl.BlockSpec(shape, functools.partial(lambda i, nd: (0,) * nd, nd=len(shape)))


def _qk_prep(proj, pos, inv, gq, gk):
    def body(q_ref, k_ref, v_ref, pos_ref, inv_ref, gq_ref, gk_ref, *rest):
        outs, sc = rest[:9], rest[9]
        c, sp, sm = _rot_tables(pos_ref, inv_ref)
        for which, (src, g_ref) in enumerate(((q_ref, gq_ref), (k_ref, gk_ref), (v_ref, None))):
            if g_ref is not None:
                g = jnp.broadcast_to(g_ref[...] * ((HD ** -0.5) if which == 0 else 1.0), c.shape)
                cg, spg, smg = c * g, sp * pltpu.roll(g, 8, 1), sm * pltpu.roll(g, 120, 1)
            for j in range(NCH):
                t = src[:, j * 128:(j + 1) * 128]
                if g_ref is not None:
                    t = _pair_norm(t) * (t * cg + pltpu.roll(t, 8, 1) * spg + pltpu.roll(t, 120, 1) * smg)
                sc[j] = t
            for j in range(NCH):
                grp, sub = divmod(j * 128, ATT_W)
                _to_residues(sc, j, outs[which * 3 + grp], DILATIONS[grp], slice(sub, sub + 128))

    return pl.pallas_call(
        body, name="qk_prep", grid=(T // TT,),
        in_specs=[_tok_spec(ATT_QKV, C_QA // ATT_QKV), _tok_spec(ATT_QKV, C_KA // ATT_QKV),
                  _tok_spec(ATT_QKV, C_VA // ATT_QKV), _tok_spec(1), _const_spec(inv), _const_spec(gq), _const_spec(gk)],
        out_specs=[_res_spec(g) for _ in range(3) for g in range(3)],
        out_shape=[_res_shape(g, BF16) for _ in range(3) for g in range(3)],
        scratch_shapes=[pltpu.VMEM((NCH, TT, 128), F32)],
        compiler_params=pltpu.CompilerParams(dimension_semantics=("arbitrary",)),
    )(proj, proj, proj, pos, inv, gq, gk)


def _qk_bwd(proj, pos, inv, gq, gk, dqs, dks, dvs, dproj):
    const = lambda a: pl.BlockSpec(a.shape, functools.partial(lambda i, p, nd: (0,) * nd, nd=a.ndim))
    res = lambda g: pl.BlockSpec((DILATIONS[g], TT // DILATIONS[g], ATT_W), lambda i, p: (0, i, 0))
    base = C_QA // ATT_QKV

    def body(t_ref, pos_ref, inv_ref, gq_ref, gk_ref, dq0, dq1, dq2, dk0, dk1, dk2, dv0, dv1, dv2, buf_ref,
             out_ref, dgq_ref, dgk_ref, sc):
        del buf_ref
        part = pl.program_id(1)
        first = pl.program_id(0) == 0

        def gather(drefs):
            for j in range(NCH):
                grp, sub = divmod(j * 128, ATT_W)
                _from_residues(drefs[grp], slice(sub, sub + 128), sc, j, DILATIONS[grp])

        def normed(g_ref, drefs, dg_ref):
            c, sp, sm = _rot_tables(pos_ref, inv_ref)
            gather(drefs)
            dg = jnp.zeros((1, 128), F32)
            for j in range(NCH):
                cols = slice(j * 128, (j + 1) * 128)
                d_rot = sc[j]
                dn = d_rot * c + pltpu.roll(d_rot * sp, 120, 1) + pltpu.roll(d_rot * sm, 8, 1)
                t = t_ref[:, cols]
                r = _pair_norm(t)
                u = dn * g_ref[...]
                out_ref[:, cols] = (r * u - t * (r * r * r) * _pair_mean(u * t)).astype(BF16)
                dg = dg + jnp.sum(dn * t * r, axis=0, keepdims=True)
            dg = dg + pltpu.roll(dg, HD, 1)

            @pl.when(first)
            def _():
                dg_ref[...] = dg

            @pl.when(jnp.logical_not(first))
            def _():
                dg_ref[...] += dg

        @pl.when(part == 0)
        def _():
            gather((dv0, dv1, dv2))
            for j in range(NCH):
                out_ref[:, j * 128:(j + 1) * 128] = sc[j].astype(BF16)

        @pl.when(part == 1)
        def _():
            normed(gq_ref, (dq0, dq1, dq2), dgq_ref)

        @pl.when(part == 2)
        def _():
            normed(gk_ref, (dk0, dk1, dk2), dgk_ref)

    keep = pl.BlockSpec((1, 128), lambda i, p: (0, 0))
    return pl.pallas_call(
        body, name="qk_bwd", grid=(T // TT, 3),
        in_specs=[pl.BlockSpec((TT, ATT_QKV), lambda i, p: (i, base + jnp.maximum(p - 1, 0))),
                  pl.BlockSpec((TT, 1), lambda i, p: (i, 0)), const(inv), const(gq), const(gk)]
        + [res(g) for _ in range(3) for g in range(3)] + [pl.BlockSpec(memory_space=pl.ANY)],
        out_specs=[pl.BlockSpec((TT, ATT_QKV), lambda i, p: (i, base + jnp.where(p == 0, 2, p - 1))), keep, keep],
        out_shape=[S(dproj.shape, dproj.dtype), S((1, 128), F32), S((1, 128), F32)],
        input_output_aliases={14: 0},
        scratch_shapes=[pltpu.VMEM((NCH, TT, 128), F32)],
        compiler_params=pltpu.CompilerParams(dimension_semantics=("arbitrary", "arbitrary")),
    )(proj, pos, inv, gq, gk, *dqs, *dks, *dvs, dproj)


def _split_heads(t):
    low = lax.broadcasted_iota(jnp.int32, (1, 128), 1) < HD
    zero = jnp.zeros_like(t)
    return jnp.concatenate([jnp.where(low, t, zero), jnp.where(low, zero, t)], axis=0)


def _join_heads(t2):
    low = lax.broadcasted_iota(jnp.int32, (1, 128), 1) < HD
    n = t2.shape[0] // 2
    return jnp.where(low, t2[:n], t2[n:])


def _band_mask4(has_before, has_own):
    row = lax.broadcasted_iota(jnp.int32, (BLK, 4 * BLK), 0)
    lane = lax.broadcasted_iota(jnp.int32, (BLK, 4 * BLK), 1)
    key = lane & (BLK - 1)
    own = lane >= 2 * BLK
    return (own & (key <= row) & has_own) | (jnp.logical_not(own) & (key >= row) & has_before)


def _band_mask_before(has_before):
    row = lax.broadcasted_iota(jnp.int32, (BLK, 2 * BLK), 0)
    key = lax.broadcasted_iota(jnp.int32, (BLK, 2 * BLK), 1) & (BLK - 1)
    return (key >= row) & has_before


def _per_head(width, col_a, col_b):
    lane = lax.broadcasted_iota(jnp.int32, (1, width), 1)
    return jnp.where((lane & BLK) == 0, col_a, col_b)


NQ = ATT_W // 128


def _att_fwd(q, k, v, grp, name):
    dil = DILATIONS[grp]
    nb = T // dil // BLK

    def body(q_ref, kp_ref, kc_ref, vp_ref, vc_ref, o_ref, lse_ref, s_sc, p_sc):
        mask = _band_mask4(pl.program_id(1) > 0, True)
        low = lax.broadcasted_iota(jnp.int32, (1, 128), 1) < HD
        halves = lambda ref, j, h: (ref[j, :, h * BLK:(h + 1) * BLK], ref[j, :, (h + 2) * BLK:(h + 3) * BLK])
        for j in range(NQ):
            cols = slice(j * 128, (j + 1) * 128)
            k4 = jnp.concatenate([_split_heads(kp_ref[:, cols]), _split_heads(kc_ref[:, cols])], axis=0)
            s_sc[j] = jnp.where(mask, _nt(q_ref[:, cols], k4), -jnp.inf)
        mxs = [[jnp.maximum(*(jnp.max(t, axis=-1, keepdims=True) for t in halves(s_sc, j, h))) for h in range(2)]
               for j in range(NQ)]
        dens = []
        for j in range(NQ):
            p = jnp.exp(s_sc[j] - _per_head(4 * BLK, *mxs[j]))
            p_sc[j] = p.astype(BF16)
            dens.append([jnp.sum(p[:, h * BLK:(h + 1) * BLK], axis=-1, keepdims=True)
                         + jnp.sum(p[:, (h + 2) * BLK:(h + 3) * BLK], axis=-1, keepdims=True) for h in range(2)])
        for j in range(NQ):
            cols = slice(j * 128, (j + 1) * 128)
            v4 = jnp.concatenate([_split_heads(vp_ref[:, cols]), _split_heads(vc_ref[:, cols])], axis=0)
            o_ref[:, cols] = _nn(p_sc[j], v4) / jnp.where(low, dens[j][0], dens[j][1])
            lse_ref[:, cols] = jnp.where(low, mxs[j][0] + jnp.log(dens[j][0]), mxs[j][1] + jnp.log(dens[j][1]))

    cur = pl.BlockSpec((None, BLK, ATT_W), lambda r, i: (r, i, 0))
    prev = pl.BlockSpec((None, BLK, ATT_W), lambda r, i: (r, jnp.maximum(i - 1, 0), 0))
    return pl.pallas_call(
        body, name=name, grid=(dil, nb),
        in_specs=[cur, prev, cur, prev, cur],
        out_specs=[cur, cur], out_shape=[_res_shape(grp, F32)] * 2,
        scratch_shapes=[pltpu.VMEM((NQ, BLK, 4 * BLK), F32), pltpu.VMEM((NQ, BLK, 4 * BLK), BF16)],
        compiler_params=pltpu.CompilerParams(dimension_semantics=("parallel", "arbitrary")),
    )(q, k, k, v, v)


def _att_bwd(q, k, v, datt, att, lse, grp, name):
    dil = DILATIONS[grp]
    nb = T // dil // BLK
    scale = HD ** -0.5

    def body(q0_ref, q1_ref, kp_ref, kc_ref, vp_ref, vc_ref, do0_ref, do1_ref, o0_ref, o1_ref, l0_ref, l1_ref,
             dq_ref, dk_ref, dv_ref, k4_sc, v4_sc, s0_sc, s1_sc, dp0_sc, dp1_sc, p_sc, ds_sc):
        i = pl.program_id(1)
        mask_mine = _band_mask4(i > 0, True)
        mask_next = _band_mask_before(i < nb - 1)
        low = lax.broadcasted_iota(jnp.int32, (1, 128), 1) < HD
        for j in range(NQ):
            cols = slice(j * 128, (j + 1) * 128)
            k4_sc[j, :2 * BLK] = _split_heads(kp_ref[:, cols])
            k4_sc[j, 2 * BLK:] = _split_heads(kc_ref[:, cols])
            v4_sc[j, :2 * BLK] = _split_heads(vp_ref[:, cols])
            v4_sc[j, 2 * BLK:] = _split_heads(vc_ref[:, cols])
        for j in range(NQ):
            cols = slice(j * 128, (j + 1) * 128)
            s0_sc[j] = _nt(q0_ref[:, cols], k4_sc[j])
            s1_sc[j] = _nt(q1_ref[:, cols], k4_sc[j, 2 * BLK:])
            dp0_sc[j] = _nt(do0_ref[:, cols].astype(BF16), v4_sc[j])
            dp1_sc[j] = _nt(do1_ref[:, cols].astype(BF16), v4_sc[j, 2 * BLK:])
        stats = []
        for j in range(NQ):
            cols = slice(j * 128, (j + 1) * 128)
            for do_ref, o_ref, l_ref in ((do0_ref, o0_ref, l0_ref), (do1_ref, o1_ref, l1_ref)):
                prod = do_ref[:, cols].astype(F32) * o_ref[:, cols].astype(F32)
                d_all = jnp.sum(prod, axis=-1, keepdims=True)
                d_low = jnp.sum(jnp.where(low, prod, 0.0), axis=-1, keepdims=True)
                lse_t = l_ref[:, cols]
                stats.append((d_low, d_all - d_low, lse_t[:, 0:1], lse_t[:, HD:HD + 1]))
        for j in range(NQ):
            (da, db, la, lb), (da1, db1, la1, lb1) = stats[2 * j], stats[2 * j + 1]
            p0 = jnp.where(mask_mine, jnp.exp(s0_sc[j] - _per_head(4 * BLK, la, lb)), 0.0)
            ds0 = p0 * (dp0_sc[j] - _per_head(4 * BLK, da, db))
            p1 = jnp.where(mask_next, jnp.exp(s1_sc[j] - _per_head(2 * BLK, la1, lb1)), 0.0)
            ds1 = p1 * (dp1_sc[j] - _per_head(2 * BLK, da1, db1))
            p_sc[j, :BLK] = p0.astype(BF16)
            ds_sc[j, :BLK] = ds0.astype(BF16)
            p_sc[j, BLK:, 2 * BLK:] = p1.astype(BF16)
            ds_sc[j, BLK:, 2 * BLK:] = ds1.astype(BF16)
        for j in range(NQ):
            cols = slice(j * 128, (j + 1) * 128)
            dq_ref[:, cols] = _nn(ds_sc[j, :BLK], k4_sc[j]) * scale
            qq = jnp.concatenate([q0_ref[:, cols], q1_ref[:, cols]], axis=0)
            dd = jnp.concatenate([do0_ref[:, cols], do1_ref[:, cols]], axis=0).astype(BF16)
            dk_ref[:, cols] = _join_heads(_tn(ds_sc[j, :, 2 * BLK:], qq))
            dv_ref[:, cols] = _join_heads(_tn(p_sc[j, :, 2 * BLK:], dd))

    def spec(shift):
        return pl.BlockSpec((None, BLK, ATT_W), lambda r, i: (r, jnp.clip(i + shift, 0, nb - 1), 0))

    here, after, before = spec(0), spec(1), spec(-1)
    vm = pltpu.VMEM
    return pl.pallas_call(
        body, name=name, grid=(dil, nb),
        in_specs=[here, after, before, here, before, here, here, after, here, after, here, after],
        out_specs=[here] * 3, out_shape=[_res_shape(grp, F32)] * 3,
        scratch_shapes=[vm((NQ, 4 * BLK, 128), BF16), vm((NQ, 4 * BLK, 128), BF16), vm((NQ, BLK, 4 * BLK), F32),
                        vm((NQ, BLK, 2 * BLK), F32), vm((NQ, BLK, 4 * BLK), F32), vm((NQ, BLK, 2 * BLK), F32),
                        vm((NQ, 2 * BLK, 4 * BLK), BF16), vm((NQ, 2 * BLK, 4 * BLK), BF16)],
        compiler_params=pltpu.CompilerParams(dimension_semantics=("parallel", "arbitrary")),
    )(q, q, k, k, v, v, datt, datt, att, att, lse, lse)


def _att_merge(os_, lses, proj):
    nq = ATT_W // 128

    def body(o0, o1, o2, l0, l1, l2, za_ref, att_ref, lse_ref, ain_ref, sc):
        for a, ref in enumerate((o0, o1, o2, l0, l1, l2)):
            for j in range(nq):
                _from_residues(ref, slice(j * 128, (j + 1) * 128), sc, a * nq + j, DILATIONS[a % 3])
        for j in range(nq):
            cols = slice(j * 128, (j + 1) * 128)
            oa, ob, oc = (sc[a * nq + j] for a in range(3))
            la, lb, lc = (sc[(3 + a) * nq + j] for a in range(3))
            m = jnp.maximum(jnp.maximum(la, lb), lc)
            wa, wb, wc = jnp.exp(la - m), jnp.exp(lb - m), jnp.exp(lc - m)
            tot = wa + wb + wc
            att = (wa * oa + wb * ob + wc * oc) / tot
            att_ref[:, cols] = att
            lse_ref[:, cols] = m + jnp.log(tot)
            za = za_ref[:, cols]
            ain_ref[:, cols] = (att * za * _sigmoid(za)).astype(BF16)

    return pl.pallas_call(
        body, name="att_merge", grid=(T // TT,),
        in_specs=[_res_spec(g) for _ in range(2) for g in range(3)] + [_tok_spec(ATT_W, C_ZA // ATT_W)],
        out_specs=[_tok_spec(ATT_W)] * 3,
        out_shape=[S((T, ATT_W), F32), S((T, ATT_W), F32), S((T, ATT_W), BF16)],
        scratch_shapes=[pltpu.VMEM((6 * nq, TT, 128), F32)],
        compiler_params=pltpu.CompilerParams(dimension_semantics=("arbitrary",)),
    )(*os_, *lses, proj)


def _att_gate_bwd(dain, att, lse, proj, dproj):
    nq = ATT_W // 128

    def body(d_ref, att_ref, lse_ref, za_ref, buf_ref, dza_ref, da0, da1, da2, at1, at2, ls1, ls2, sc):
        del buf_ref
        for j in range(nq):
            cols = slice(j * 128, (j + 1) * 128)
            za = za_ref[:, cols]
            sg = _sigmoid(za)
            d = d_ref[:, cols].astype(F32)
            att_ = att_ref[:, cols]
            dza_ref[:, cols] = (d * att_ * sg * (1.0 + za * (1.0 - sg))).astype(BF16)
            sc[j] = d * za * sg
            sc[nq + j] = att_
            sc[2 * nq + j] = lse_ref[:, cols]
        for j in range(nq):
            cols = slice(j * 128, (j + 1) * 128)
            for grp, dst in enumerate((da0, da1, da2)):
                _to_residues(sc, j, dst, DILATIONS[grp], cols)
            for grp, dst in ((1, at1), (2, at2)):
                _to_residues(sc, nq + j, dst, DILATIONS[grp], cols)
            for grp, dst in ((1, ls1), (2, ls2)):
                _to_residues(sc, 2 * nq + j, dst, DILATIONS[grp], cols)

    res = (0, 1, 2, 1, 2, 1, 2)
    return pl.pallas_call(
        body, name="att_gate_bwd", grid=(T // TT,),
        in_specs=[_tok_spec(ATT_W)] * 3 + [_tok_spec(ATT_W, C_ZA // ATT_W), pl.BlockSpec(memory_space=pl.ANY)],
        out_specs=[_tok_spec(ATT_W, C_ZA // ATT_W)] + [_res_spec(g) for g in res],
        out_shape=[S(dproj.shape, dproj.dtype)] + [_res_shape(g, BF16) for g in res[:5]]
        + [_res_shape(g, F32) for g in res[5:]],
        input_output_aliases={4: 0},
        scratch_shapes=[pltpu.VMEM((3 * nq, TT, 128), F32)],
        compiler_params=pltpu.CompilerParams(dimension_semantics=("arbitrary",)),
    )(dain, att, lse, proj, dproj)


def _split3(v):
    hi = v.astype(BF16)
    r1 = v - hi.astype(F32)
    mid = r1.astype(BF16)
    lo = (r1 - mid.astype(F32)).astype(BF16)
    return hi, mid, lo


def _tri_sum(v, upper):
    n = v.shape[0]
    row = lax.broadcasted_iota(jnp.int32, (n, n), 0)
    col = lax.broadcasted_iota(jnp.int32, (n, n), 1)
    tri = jnp.where(col >= row if upper else col <= row, 1.0, 0.0).astype(BF16)
    hi, mid, lo = _split3(v)
    return _nn(tri, hi) + _nn(tri, mid) + _nn(tri, lo)


def _gla_gates(glr_ref, w2_ref, b_ref):
    logit = _nn(glr_ref[...].astype(BF16), w2_ref[...]) + b_ref[...]
    lg = (jnp.minimum(logit, 0.0) - jnp.log(1.0 + jnp.exp(-jnp.abs(logit)))) * (1.0 / GLA_TAU)
    return logit, _tri_sum(lg, upper=False)


def _gla_head(cum, q_ref, k_ref, h):
    cols = slice(h * GDK, (h + 1) * GDK)
    b = cum[:, cols]
    last = b[GLA_C - 1:GLA_C, :]
    e_pos = jnp.exp(b)
    e_neg = jnp.exp(-b)
    e_end = jnp.exp(last - b)
    qt = q_ref[:, cols] * (GDK ** -0.5) * e_pos
    kt = k_ref[:, cols] * e_neg
    kh = k_ref[:, cols] * e_end
    return b, last, e_pos, e_neg, e_end, qt, kt, kh


def _causal(n):
    return lax.broadcasted_iota(jnp.int32, (n, n), 1) <= lax.broadcasted_iota(jnp.int32, (n, n), 0)


def _gla_fwd(proj, w2p, bg, gn):
    nc = T // GLA_C

    def body(q_ref, k_ref, v_ref, glr_ref, zg_ref, w2_ref, b_ref, gn_ref, o_ref, bin_ref, st_ref, state):
        @pl.when(pl.program_id(0) == 0)
        def _():
            state[...] = jnp.zeros_like(state)

        _, cum = _gla_gates(glr_ref, w2_ref, b_ref)
        for h in range(GH):
            _, last, _, _, _, qt, kt, kh = _gla_head(cum, q_ref, k_ref, h)
            vcols = slice(h * GDV, (h + 1) * GDV)
            st = state[h]
            st_ref[0, h] = st
            v = v_ref[:, vcols].astype(BF16)
            qb = qt.astype(BF16)
            a = jnp.where(_causal(GLA_C), _nt(qb, kt.astype(BF16)), 0.0)
            o = _nt(qb, st.astype(BF16)) + _nn(a.astype(BF16), v)
            state[h] = st * jnp.exp(last) + _tn(v, kh.astype(BF16))
            o_ref[:, vcols] = o
            r = lax.rsqrt(jnp.mean(o * o, axis=-1, keepdims=True) + EPS)
            zg = zg_ref[:, vcols]
            bin_ref[:, vcols] = (o * r * gn_ref[...] * zg * _sigmoid(zg)).astype(BF16)

    row = lambda width, cblk: pl.BlockSpec((GLA_C, width), functools.partial(lambda i, c: (i, c), c=cblk))
    full = lambda a: pl.BlockSpec(a.shape, functools.partial(lambda i, nd: (0,) * nd, nd=a.ndim))
    return pl.pallas_call(
        body, name="gla_fwd", grid=(nc,),
        in_specs=[row(512, C_QG // 512), row(512, C_KG // 512), row(1024, C_VG // 1024), row(GLR_W, C_GLR // GLR_W),
                  row(1024, C_ZG // 1024), full(w2p), full(bg), full(gn)],
        out_specs=[pl.BlockSpec((GLA_C, GH * GDV), lambda i: (i, 0)), pl.BlockSpec((GLA_C, GH * GDV), lambda i: (i, 0)),
                   pl.BlockSpec((1, GH, GDV, GDK), lambda i: (i, 0, 0, 0))],
        out_shape=[S((T, GH * GDV), F32), S((T, GH * GDV), BF16), S((nc, GH, GDV, GDK), F32)],
        scratch_shapes=[pltpu.VMEM((GH, GDV, GDK), F32)],
        compiler_params=pltpu.CompilerParams(dimension_semantics=("arbitrary",)),
    )(proj, proj, proj, proj, proj, w2p, bg, gn)


def _gla_bwd(proj, w2p, bg, gn, o_gla, states, dbin, dproj):
    nc = T // GLA_C

    def body(q_ref, k_ref, v_ref, glr_ref, zg_ref, w2_ref, b_ref, gn_ref, o_ref, st_ref, dbin_ref, buf_ref,
             out_ref, dw2_ref, dbg_ref, dgn_ref, dstate, dlogit):
        del buf_ref
        dq_ref = out_ref.at[:, C_QG:C_KG]
        dk_ref = out_ref.at[:, C_KG:C_VG]
        dv_ref = out_ref.at[:, C_VG:C_ZG]
        dzg_ref = out_ref.at[:, C_ZG:C_GLR]
        dglr_ref = out_ref.at[:, C_GLR:C_GLR + GLR_W]
        first = pl.program_id(0) == 0

        @pl.when(first)
        def _():
            dstate[...] = jnp.zeros_like(dstate)

        logit, cum = _gla_gates(glr_ref, w2_ref, b_ref)
        is_last = lax.broadcasted_iota(jnp.int32, (GLA_C, 1), 0) == GLA_C - 1
        dgn = jnp.zeros((1, GDV), F32)
        for h in range(GH):
            _, last, e_pos, e_neg, e_end, qt, kt, kh = _gla_head(cum, q_ref, k_ref, h)
            cols = slice(h * GDK, (h + 1) * GDK)
            vcols = slice(h * GDV, (h + 1) * GDV)
            o = o_ref[:, vcols]
            r = lax.rsqrt(jnp.mean(o * o, axis=-1, keepdims=True) + EPS)
            zg = zg_ref[:, vcols]
            sg = _sigmoid(zg)
            db_ = dbin_ref[:, vcols].astype(F32)
            dlin = db_ * zg * sg
            dzg_ref[:, vcols] = (db_ * (o * r * gn_ref[...]) * sg * (1.0 + zg * (1.0 - sg))).astype(BF16)
            u = dlin * gn_ref[...]
            do = (r * u - o * (r * r * r) * jnp.mean(u * o, axis=-1, keepdims=True)).astype(BF16)
            dgn = dgn + jnp.sum(dlin * o * r, axis=0, keepdims=True)
            st = st_ref[0, h]
            dst = dstate[h]
            v = v_ref[:, vcols].astype(BF16)
            qb, kb, khb = qt.astype(BF16), kt.astype(BF16), kh.astype(BF16)
            dstb = dst.astype(BF16)
            causal = _causal(GLA_C)
            a = jnp.where(causal, _nt(qb, kb), 0.0).astype(BF16)
            da = jnp.where(causal, _nt(do, v), 0.0).astype(BF16)
            dqt = _nn(do, st.astype(BF16)) + _nn(da, kb)
            dkt = _tn(da, qb)
            dkh = _nn(v, dstb)
            dv_ref[:, vcols] = (_tn(a, do) + _nt(khb, dstb)).astype(BF16)
            lam = jnp.exp(last)
            dlam = jnp.sum(dst * st, axis=0, keepdims=True)
            dstate[h] = dst * lam + _tn(do, qb)
            dq_ref[:, cols] = (dqt * e_pos * (GDK ** -0.5)).astype(BF16)
            dk_ref[:, cols] = (dkt * e_neg + dkh * e_end).astype(BF16)
            dkh_kh = dkh * kh
            dcum = dqt * qt - dkt * kt - dkh_kh
            dlast = jnp.sum(dkh_kh, axis=0, keepdims=True) + dlam * lam
            dcum = jnp.where(is_last, dcum + dlast, dcum)
            dlg = _tri_sum(dcum, upper=True)
            dlogit[:, cols] = dlg * (1.0 / GLA_TAU) * (1.0 - _sigmoid(logit[:, cols]))

        dl = dlogit[...]
        dlb = dl.astype(BF16)
        dglr_ref[...] = _nt(dlb, w2_ref[...]).astype(BF16)
        dw2 = _tn(glr_ref[...].astype(BF16), dlb)
        dbg = jnp.sum(dl, axis=0, keepdims=True)

        @pl.when(first)
        def _():
            dw2_ref[...] = dw2
            dbg_ref[...] = dbg
            dgn_ref[...] = dgn

        @pl.when(jnp.logical_not(first))
        def _():
            dw2_ref[...] += dw2
            dbg_ref[...] += dbg
            dgn_ref[...] += dgn

    rev = lambda i: nc - 1 - i
    row = lambda width, cblk: pl.BlockSpec((GLA_C, width), functools.partial(lambda i, c: (rev(i), c), c=cblk))
    full = lambda a: pl.BlockSpec(a.shape, functools.partial(lambda i, nd: (0,) * nd, nd=a.ndim))
    keep = lambda shape: pl.BlockSpec(shape, functools.partial(lambda i, nd: (0,) * nd, nd=len(shape)))
    return pl.pallas_call(
        body, name="gla_bwd", grid=(nc,),
        in_specs=[row(512, C_QG // 512), row(512, C_KG // 512), row(1024, C_VG // 1024), row(GLR_W, C_GLR // GLR_W),
                  row(1024, C_ZG // 1024), full(w2p), full(bg), full(gn), row(GH * GDV, 0),
                  pl.BlockSpec((1, GH, GDV, GDK), lambda i: (rev(i), 0, 0, 0)), row(GH * GDV, 0),
                  pl.BlockSpec(memory_space=pl.ANY)],
        out_specs=[row(GLA_GROUP_W, 0), keep((GLR_W, 512)), keep((1, 512)), keep((1, GDV))],
        out_shape=[S(dproj.shape, dproj.dtype), S((GLR_W, 512), F32), S((1, 512), F32), S((1, GDV), F32)],
        input_output_aliases={11: 0},
        scratch_shapes=[pltpu.VMEM((GH, GDV, GDK), F32), pltpu.VMEM((GLA_C, GH * GDK), F32)],
        compiler_params=pltpu.CompilerParams(dimension_semantics=("arbitrary",)),
    )(proj, proj, proj, proj, proj, w2p, bg, gn, o_gla, states, dbin, dproj)


RT = 512


def _rowchain(body, name, ins, outs, scratch=()):
    in_specs, args = [], []
    for spec in ins:
        if spec[0] == "tok":
            _, arr, width, cblk = spec
            in_specs.append(pl.BlockSpec((RT, width), functools.partial(lambda i, c: (i, c), c=cblk)))
        else:
            arr = spec[1]
            in_specs.append(pl.BlockSpec(arr.shape, functools.partial(lambda i, nd: (0,) * nd, nd=arr.ndim)))
        args.append(arr)
    out_specs, out_shape = [], []
    for spec in outs:
        if spec[0] == "tok":
            _, shape, dtype, width, cblk = spec
            out_specs.append(pl.BlockSpec((RT, width), functools.partial(lambda i, c: (i, c), c=cblk)))
        else:
            _, shape, dtype = spec
            out_specs.append(pl.BlockSpec(shape, functools.partial(lambda i, nd: (0,) * nd, nd=len(shape))))
        out_shape.append(S(shape, dtype))
    return pl.pallas_call(
        body, name=name, grid=(T // RT,), in_specs=in_specs, out_specs=out_specs, out_shape=out_shape,
        scratch_shapes=list(scratch), compiler_params=pltpu.CompilerParams(dimension_semantics=("arbitrary",)),
    )(*args)


def _tok(arr, width=None, cblk=0):
    return ("tok", arr, arr.shape[1] if width is None else width, cblk)


def _tok_out(dtype, width=D):
    return ("tok", (T, width), dtype, width, 0)


def _branches_fwd(ain, bin_, proj, x, w_att, w_gla, w_out):
    def body(ain_ref, bin_ref, g_ref, x_ref, wa_ref, wg_ref, wo_ref, ya_ref, yb_ref, y_ref, x1_ref):
        ya = _nn(ain_ref[...], wa_ref[...]).astype(BF16)
        yb = _nn(bin_ref[...], wg_ref[...]).astype(BF16)
        ya_ref[...] = ya
        yb_ref[...] = yb
        y = (_sigmoid(g_ref[:, :D]) * ya.astype(F32) + _sigmoid(g_ref[:, D:]) * yb.astype(F32)).astype(BF16)
        y_ref[...] = y
        x1_ref[...] = x_ref[...] + _nn(y, wo_ref[...])

    return _rowchain(body, "branches_fwd",
                     [_tok(ain), _tok(bin_), _tok(proj, 2 * D, C_GA // (2 * D)), _tok(x), ("all", w_att),
                      ("all", w_gla), ("all", w_out)],
                     [_tok_out(BF16), _tok_out(BF16), _tok_out(BF16), _tok_out(F32)])


def _accumulate(ref, part, first):
    @pl.when(first)
    def _():
        ref[...] = part

    @pl.when(jnp.logical_not(first))
    def _():
        ref[...] += part


def _ple_loss(x1, p, target, g2, w_pg, w_ple):
    def body(x1_ref, p_ref, t_ref, g_ref, wpg_ref, wple_ref, n2_ref, loss_ref, dout_ref, du_ref, dwple_ref, acc):
        first = pl.program_id(0) == 0
        x1 = x1_ref[...]
        r = lax.rsqrt(jnp.mean(x1 * x1, axis=-1, keepdims=True) + EPS)
        n2 = (x1 * r * g_ref[...]).astype(BF16)
        n2_ref[...] = n2
        pg = _sigmoid(_nn(n2, wpg_ref[...]))
        pb = p_ref[...].astype(BF16)
        e_ = _nn(pb, wple_ref[...])
        diff = x1 + e_ * pg - t_ref[...]
        _accumulate(acc, jnp.sum(diff * diff, axis=0, keepdims=True), first)
        dout = diff * (1.0 / D)
        dout_ref[...] = dout
        du_ref[...] = (dout * e_ * pg * (1.0 - pg)).astype(BF16)
        _accumulate(dwple_ref, _tn(pb, (dout * pg).astype(BF16)), first)
        loss_ref[...] = jnp.zeros((1, 128), F32) + jnp.sum(acc[...], axis=-1, keepdims=True) * (0.5 / D)

    return _rowchain(body, "ple_loss", [_tok(x1), _tok(p), _tok(target), ("all", g2), ("all", w_pg), ("all", w_ple)],
                     [_tok_out(BF16), ("acc", (1, 128), F32), _tok_out(F32), _tok_out(BF16), ("acc", (PLE, D), F32)],
                     scratch=[pltpu.VMEM((1, D), F32)])


def _ple_bwd(du, n2, y, x1, dout, g2, w_pg, w_out):
    def body(du_ref, n2_ref, y_ref, x1_ref, dout_ref, g_ref, wpg_ref, wo_ref, dx_ref, dy_ref, dg_ref, dwpg_ref,
             dwo_ref):
        first = pl.program_id(0) == 0
        x1 = x1_ref[...]
        r = lax.rsqrt(jnp.mean(x1 * x1, axis=-1, keepdims=True) + EPS)
        du_ = du_ref[...]
        dn = _nt(du_, wpg_ref[...])
        u = dn * g_ref[...]
        dx = dout_ref[...] + r * u - x1 * (r * r * r) * jnp.mean(u * x1, axis=-1, keepdims=True)
        dxb = dx.astype(BF16)
        dx_ref[...] = dx
        dy_ref[...] = _nt(dxb, wo_ref[...]).astype(BF16)
        _accumulate(dg_ref, jnp.sum(dn * x1 * r, axis=0, keepdims=True), first)
        _accumulate(dwpg_ref, _tn(n2_ref[...], du_), first)
        _accumulate(dwo_ref, _tn(y_ref[...], dxb), first)

    return _rowchain(body, "ple_bwd",
                     [_tok(du), _tok(n2), _tok(y), _tok(x1), _tok(dout), ("all", g2), ("all", w_pg), ("all", w_out)],
                     [_tok_out(F32), _tok_out(BF16), ("acc", (1, D), F32), ("acc", (D, D), F32), ("acc", (D, D), F32)])


def _branches_bwd(dy, ya, yb, ain, bin_, proj, w_att, w_gla):
    def body(dy_ref, ya_ref, yb_ref, ain_ref, bin_ref, g_ref, wa_ref, wg_ref, dg_ref, dain_ref, dbin_ref,
             dwa_ref, dwg_ref):
        first = pl.program_id(0) == 0
        dy_ = dy_ref[...].astype(F32)
        sa, sb = _sigmoid(g_ref[:, :D]), _sigmoid(g_ref[:, D:])
        dg_ref[:, :D] = (dy_ * ya_ref[...].astype(F32) * sa * (1.0 - sa)).astype(BF16)
        dg_ref[:, D:] = (dy_ * yb_ref[...].astype(F32) * sb * (1.0 - sb)).astype(BF16)
        dya = (dy_ * sa).astype(BF16)
        dyb = (dy_ * sb).astype(BF16)
        dain_ref[...] = _nt(dya, wa_ref[...]).astype(BF16)
        dbin_ref[...] = _nt(dyb, wg_ref[...]).astype(BF16)
        _accumulate(dwa_ref, _tn(ain_ref[...], dya), first)
        _accumulate(dwg_ref, _tn(bin_ref[...], dyb), first)

    gates = C_GA // (2 * D)
    return _rowchain(body, "branches_bwd",
                     [_tok(dy), _tok(ya), _tok(yb), _tok(ain), _tok(bin_), _tok(proj, 2 * D, gates), ("all", w_att),
                      ("all", w_gla)],
                     [("tok", (T, NCOL), BF16, 2 * D, gates), _tok_out(BF16, ATT_W), _tok_out(BF16),
                      ("acc", (ATT_W, D), F32), ("acc", (D, D), F32)])


def _peer(k):
    x, y, c = lax.axis_index("x"), lax.axis_index("y"), lax.axis_index("c")
    return (x ^ ((k >> 2) & 1), y ^ ((k >> 1) & 1), c ^ (k & 1))


def _my_index():
    return 4 * lax.axis_index("x") + 2 * lax.axis_index("y") + lax.axis_index("c")


def _peer_index(k):
    px, py, pc = _peer(k)
    return 4 * px + 2 * py + pc


def _pairwise_plan(src_of, dst_of, landed_of, own_src, own_dst):
    def plan(ins, outs, send, recv, local):
        n = len(ins)

        def own():
            return [pltpu.make_async_copy(own_src(ins[a]), own_dst(outs[a]), local.at[a]) for a in range(n)]

        def remote(k, a, src, dst):
            return pltpu.make_async_remote_copy(src_ref=src, dst_ref=dst, send_sem=send.at[k - 1, a],
                                                recv_sem=recv.at[k - 1, a], device_id=_peer(k), device_id_type=MESH)

        def sent():
            return [remote(k, a, src_of(ins[a], k), dst_of(outs[a])) for k in range(1, NDEV) for a in range(n)]

        def start():
            for cp in own() + sent():
                cp.start()

        def finish():
            for k in range(1, NDEV):
                for a in range(n):
                    remote(k, a, own_src(ins[a]), landed_of(outs[a], k)).wait_recv()
            for cp in sent():
                cp.wait_send()
            for cp in own():
                cp.wait()

        return start, finish

    return plan


def _pairwise_sems(n):
    return [pltpu.SemaphoreType.DMA((NDEV - 1, n)), pltpu.SemaphoreType.DMA((NDEV - 1, n)),
            pltpu.SemaphoreType.DMA((n,))]


def _gather_side(arrs):
    plan = _pairwise_plan(src_of=lambda i, k: i, dst_of=lambda o: o.at[_my_index()],
                          landed_of=lambda o, k: o.at[_peer_index(k)],
                          own_src=lambda i: i, own_dst=lambda o: o.at[_my_index()])
    return dict(arrs=arrs, out_shape=[S((NDEV,) + a.shape, a.dtype) for a in arrs],
                scratch=_pairwise_sems(len(arrs)), plan=plan)


def _exchange_side(arrs):
    plan = _pairwise_plan(src_of=lambda i, k: i.at[_peer_index(k)], dst_of=lambda o: o.at[_my_index()],
                          landed_of=lambda o, k: o.at[_peer_index(k)],
                          own_src=lambda i: i.at[_my_index()], own_dst=lambda o: o.at[_my_index()])
    return dict(arrs=arrs, out_shape=[S(a.shape, a.dtype) for a in arrs], scratch=_pairwise_sems(len(arrs)), plan=plan)


def _comm_call(side, name):
    n = len(side["arrs"])

    def body(*refs):
        start, finish = side["plan"](refs[:n], refs[n:2 * n], *refs[2 * n:])
        start()
        finish()

    hbm = pl.BlockSpec(memory_space=pl.ANY)
    return pl.pallas_call(body, name=name, in_specs=[hbm] * n, out_specs=[hbm] * n, out_shape=side["out_shape"],
                          scratch_shapes=side["scratch"])(*side["arrs"])


def _all_gather_by_chip(arrs, name):
    n = len(arrs)

    def body(*refs):
        ins, outs = refs[:n], refs[n:2 * n]
        send, recv, local = refs[2 * n:]
        x, y, c = lax.axis_index("x"), lax.axis_index("y"), lax.axis_index("c")
        me, sibling = (x, y, c), (x, y, 1 - c)
        chips = [(1 - x, y), (x, 1 - y), (1 - x, 1 - y)]

        def copy(k, a, block, to, src=None):
            px, py, pc = block
            slot = outs[a].at[4 * px + 2 * py + pc]
            return pltpu.make_async_remote_copy(
                src_ref=slot if src is None else src, dst_ref=slot, send_sem=send.at[k, a], recv_sem=recv.at[k, a],
                device_id=to, device_id_type=MESH)

        north = c == 1
        via = (jnp.where(north, 1 - x, x), jnp.where(north, y, 1 - y))
        onward = (jnp.where(north, x, 1 - x), jnp.where(north, 1 - y, y), c)
        mine = [pltpu.make_async_copy(ins[a], outs[a].at[4 * x + 2 * y + c], local.at[a]) for a in range(n)]
        first = []
        for a in range(n):
            first.append(copy(0, a, me, sibling, src=ins[a]))
            first += [copy(1 + j, a, me, (*chips[j], c), src=ins[a]) for j in range(2)]
        for cp in mine + first:
            cp.start()
        passed = []
        for j in range(2):
            for a in range(n):
                copy(1 + j, a, (*chips[j], c), me).wait_recv()
                passed.append(copy(4 + j, a, (*chips[j], c), sibling))
                passed[-1].start()
        for a in range(n):
            passed.append(copy(3, a, (*via, c), onward))
            passed[-1].start()
        for a in range(n):
            copy(3, a, (*chips[2], c), me).wait_recv()
            passed.append(copy(6, a, (*chips[2], c), sibling))
            passed[-1].start()
        for a in range(n):
            copy(0, a, sibling, me).wait_recv()
        for j, chip in enumerate(chips):
            for a in range(n):
                copy(4 + j, a, (*chip, 1 - c), me).wait_recv()
        for cp in first + passed:
            cp.wait_send()
        for cp in mine:
            cp.wait()

    hbm = pl.BlockSpec(memory_space=pl.ANY)
    return pl.pallas_call(
        body, name=name, in_specs=[hbm] * n, out_specs=[hbm] * n,
        out_shape=[S((NDEV,) + a.shape, a.dtype) for a in arrs],
        scratch_shapes=[pltpu.SemaphoreType.DMA((NDEV - 1, n)), pltpu.SemaphoreType.DMA((NDEV - 1, n)),
                        pltpu.SemaphoreType.DMA((n,))],
    )(*arrs)


NCHIP = 4


def _exchange_sibling(arrs, name):
    n = len(arrs)

    def body(*refs):
        ins, outs = refs[:n], refs[n:2 * n]
        send, recv = refs[2 * n:]
        x, y, c = lax.axis_index("x"), lax.axis_index("y"), lax.axis_index("c")
        copies = []
        for q in range(NCHIP):
            for a in range(n):
                copies.append(pltpu.make_async_remote_copy(
                    src_ref=ins[a].at[2 * q + (1 - c)], dst_ref=outs[a].at[q], send_sem=send.at[q, a],
                    recv_sem=recv.at[q, a], device_id=(x, y, 1 - c), device_id_type=MESH))
        for cp in copies:
            cp.start()
        for cp in copies:
            cp.wait_recv()
        for cp in copies:
            cp.wait_send()

    hbm = pl.BlockSpec(memory_space=pl.ANY)
    return pl.pallas_call(
        body, name=name, in_specs=[hbm] * n, out_specs=[hbm] * n,
        out_shape=[S((NCHIP,) + a.shape[1:], a.dtype) for a in arrs],
        scratch_shapes=[pltpu.SemaphoreType.DMA((NCHIP, n)), pltpu.SemaphoreType.DMA((NCHIP, n))],
    )(*arrs)


def _pair_add(mine, got, core, name):
    _, rows, cols = mine.shape
    tc = 256
    assert cols % tc == 0

    def body(core_ref, a_ref, b_ref, o_ref):
        o_ref[...] = (a_ref[...].astype(F32) + b_ref[...].astype(F32)).astype(BF16)

    return pl.pallas_call(
        body, name=name,
        grid_spec=pltpu.PrefetchScalarGridSpec(
            num_scalar_prefetch=1, grid=(NCHIP, cols // tc),
            in_specs=[pl.BlockSpec((None, rows, tc), lambda q, i, core_ref: (2 * q + core_ref[0], 0, i)),
                      pl.BlockSpec((None, rows, tc), lambda q, i, core_ref: (q, 0, i))],
            out_specs=pl.BlockSpec((None, rows, tc), lambda q, i, core_ref: (q, 0, i))),
        out_shape=S((NCHIP, rows, cols), BF16),
    )(core, mine, got)


def _chips_side(arrs):
    def plan(ins, outs, send, recv, local):
        n = len(ins)

        def places():
            x, y, c = lax.axis_index("x"), lax.axis_index("y"), lax.axis_index("c")
            return 2 * x + y, c, [(1 - x, y), (x, 1 - y), (1 - x, 1 - y)]

        def own():
            here, _, _ = places()
            return [pltpu.make_async_copy(ins[a].at[here], outs[a].at[here], local.at[a]) for a in range(n)]

        def remote(j, a, src_slot, dst_slot):
            _, c, chips = places()
            cx, cy = chips[j]
            return pltpu.make_async_remote_copy(
                src_ref=ins[a].at[src_slot], dst_ref=outs[a].at[dst_slot], send_sem=send.at[j, a],
                recv_sem=recv.at[j, a], device_id=(cx, cy, c), device_id_type=MESH)

        def sent():
            here, _, chips = places()
            return [remote(j, a, 2 * cx + cy, here) for j, (cx, cy) in enumerate(chips) for a in range(n)]

        def start():
            for cp in own() + sent():
                cp.start()

        def finish():
            here, _, chips = places()
            for j, (cx, cy) in enumerate(chips):
                for a in range(n):
                    remote(j, a, here, 2 * cx + cy).wait_recv()
            for cp in sent():
                cp.wait_send()
            for cp in own():
                cp.wait()

        return start, finish

    n = len(arrs)
    return dict(arrs=arrs, out_shape=[S(a.shape, a.dtype) for a in arrs],
                scratch=[pltpu.SemaphoreType.DMA((NCHIP - 1, n)), pltpu.SemaphoreType.DMA((NCHIP - 1, n)),
                         pltpu.SemaphoreType.DMA((n,))], plan=plan)


def _adamw(parts, w, m, v, name, tr, tc=None):
    rows, cols = w.shape
    if tc is None:
        assert rows % tr == 0
        grid, shape, at = (rows // tr,), (tr, cols), (lambda i: (i, 0))
    else:
        assert cols % tc == 0
        grid, shape, at = (cols // tc,), (rows, tc), (lambda i: (0, i))
    c1 = 1.0 - ADAM_B1 ** ADAM_STEP
    c2 = 1.0 - ADAM_B2 ** ADAM_STEP

    nparts = parts.shape[0]

    def body(p_ref, w_ref, m_ref, v_ref, g_ref, d_ref, mo_ref, vo_ref):
        g = p_ref[0].astype(F32)
        for s in range(1, nparts):
            g = g + p_ref[s].astype(F32)
        m_new = ADAM_B1 * m_ref[...] + (1.0 - ADAM_B1) * g
        v_new = ADAM_B2 * v_ref[...] + (1.0 - ADAM_B2) * (g * g)
        g_ref[...] = g
        mo_ref[...] = m_new
        vo_ref[...] = v_new
        d_ref[...] = -ADAM_LR * ((m_new / c1) / (jnp.sqrt(v_new / c2) + ADAM_EPS) + ADAM_WD * w_ref[...])

    blk = pl.BlockSpec(shape, at)
    return pl.pallas_call(
        body, name=name, grid=grid,
        in_specs=[pl.BlockSpec((nparts,) + shape, lambda i: (0,) + at(i)), blk, blk, blk],
        out_specs=[blk] * 4, out_shape=[S((rows, cols), F32)] * 4,
        compiler_params=pltpu.CompilerParams(dimension_semantics=("parallel",)),
    )(parts, w, m, v)


def _adam_math(g, w, m, v):
    c1 = 1.0 - ADAM_B1 ** ADAM_STEP
    c2 = 1.0 - ADAM_B2 ** ADAM_STEP
    m_new = ADAM_B1 * m + (1.0 - ADAM_B1) * g
    v_new = ADAM_B2 * v + (1.0 - ADAM_B2) * (g * g)
    return -ADAM_LR * ((m_new / c1) / (jnp.sqrt(v_new / c2) + ADAM_EPS) + ADAM_WD * w), m_new, v_new


def _adamw_small(parts, params, loss_parts):
    n = len(params)

    def body(*refs):
        p_refs, rest = refs[:n], refs[n + 1:]
        total = refs[n][0]
        for s in range(1, NDEV):
            total = total + refs[n][s]
        refs[-1][...] = total
        for j in range(n):
            w_ref, m_ref, v_ref = rest[3 * j:3 * j + 3]
            g_ref, d_ref, mo_ref, vo_ref = rest[3 * n + 4 * j:3 * n + 4 * j + 4]
            width = w_ref.shape[1]
            g = p_refs[j][0]
            for s in range(1, NDEV):
                g = g + p_refs[j][s]
            g = g[:, :width]
            delta, m_new, v_new = _adam_math(g, w_ref[...], m_ref[...], v_ref[...])
            g_ref[...] = g
            d_ref[...] = delta
            mo_ref[...] = m_new
            vo_ref[...] = v_new

    flat = [a for group in params for a in group]
    return pl.pallas_call(
        body, name="adam_small",
        out_shape=[S(group[0].shape, F32) for group in params for _ in range(4)] + [S((1, 128), F32)],
    )(*parts, loss_parts, *flat)


def _adamw_rows(parts, w, m, v, name, tc=128):
    rows, _, cols = w.shape
    nparts = parts.shape[0]
    nsteps = cols // tc

    def body(p_ref, w_hbm, m_hbm, v_hbm, g_hbm, d_hbm, mo_hbm, vo_hbm, inbuf, outbuf, insem, outsem):
        i = pl.program_id(0)
        slot = i & 1

        def view(ref, step):
            return ref.at[:, 0, pl.ds(pl.multiple_of(step * tc, tc), tc)]

        def fetch(step, sl):
            return [pltpu.make_async_copy(view(src, step), inbuf.at[sl, k], insem.at[sl, k])
                    for k, src in enumerate((w_hbm, m_hbm, v_hbm))]

        def write(step, sl):
            return [pltpu.make_async_copy(outbuf.at[sl, k], view(dst, step), outsem.at[sl, k])
                    for k, dst in enumerate((g_hbm, d_hbm, mo_hbm, vo_hbm))]

        @pl.when(i == 0)
        def _():
            for cp in fetch(0, 0):
                cp.start()

        @pl.when(i + 1 < nsteps)
        def _():
            for cp in fetch(i + 1, 1 - slot):
                cp.start()

        for cp in fetch(i, slot):
            cp.wait()

        @pl.when(i >= 2)
        def _():
            for cp in write(i - 2, slot):
                cp.wait()

        g = p_ref[0].astype(F32)
        for s in range(1, nparts):
            g = g + p_ref[s].astype(F32)
        g = g[:rows]
        delta, m_new, v_new = _adam_math(g, inbuf[slot, 0], inbuf[slot, 1], inbuf[slot, 2])
        for k, val in enumerate((g, delta, m_new, v_new)):
            outbuf[slot, k] = val
        for cp in write(i, slot):
            cp.start()

        @pl.when(i == nsteps - 1)
        def _():
            for cp in write(i - 1, 1 - slot) + write(i, slot):
                cp.wait()

    hbm = pl.BlockSpec(memory_space=pl.ANY)
    assert nsteps >= 2
    return pl.pallas_call(
        body, name=name, grid=(nsteps,),
        in_specs=[pl.BlockSpec((nparts, parts.shape[1], tc), lambda i: (0, 0, i)), hbm, hbm, hbm],
        out_specs=[hbm] * 4, out_shape=[S((rows, 1, cols), F32)] * 4,
        scratch_shapes=[pltpu.VMEM((2, 3, rows, tc), F32), pltpu.VMEM((2, 4, rows, tc), F32),
                        pltpu.SemaphoreType.DMA((2, 3)), pltpu.SemaphoreType.DMA((2, 4))],
        compiler_params=pltpu.CompilerParams(dimension_semantics=("arbitrary",)),
    )(parts, w, m, v)


def _to_aligned(wt):
    pad = jnp.zeros((GLR_W - GLR_N, wt.shape[1]), wt.dtype)
    return jnp.concatenate([wt[O_QG:O_GLR], wt[O_ZG:O_GA], wt[O_GLR:O_ZG], pad, wt[O_ZA:O_QG], wt[O_GA:O_END],
                            wt[O_QA:O_ZA]], axis=0)


def _from_aligned(wt):
    return jnp.concatenate([wt[C_QA:], wt[C_ZA:C_GA], wt[C_QG:C_ZG], wt[C_GLR:C_GLR + GLR_N], wt[C_ZG:C_GLR],
                            wt[C_GA:C_QA]], axis=0)


SLAB = 1296
REMAP_RUNS = 3
_PIECES = ((O_QA, O_ZA, C_QA), (O_ZA, O_QG, C_ZA), (O_QG, O_GLR, C_QG), (O_GLR, O_ZG, C_GLR), (O_ZG, O_GA, C_ZG),
           (O_GA, O_END, C_GA))


def _slab_row_of_aligned(a):
    for o0, o1, a0 in _PIECES:
        if a0 <= a < a0 + o1 - o0:
            c = o0 + a - a0
            return SLAB * (c // W_IN_SHARD) + c % W_IN_SHARD
    return -1


def _aligned_row_of_slab(r):
    d, l = divmod(r, SLAB)
    if l >= W_IN_SHARD:
        return -1
    c = d * W_IN_SHARD + l
    for o0, o1, a0 in _PIECES:
        if o0 <= c < o1:
            return a0 + c - o0
    raise AssertionError(c)


def _remap_table(row_of, n_out, block, n_src):
    win = block + 16
    table = []
    for b in range(n_out // block):
        runs = []
        for i in range(block):
            s = row_of(b * block + i)
            if s < 0:
                continue
            if runs and runs[-1][0] + runs[-1][2] == s and runs[-1][1] + runs[-1][2] == i:
                runs[-1][2] += 1
            else:
                runs.append([s, i, 1])
        assert len(runs) <= REMAP_RUNS, (b, runs)
        row = []
        for s, i, n in runs:
            w = min(s // 16 * 16, n_src - win)
            assert 0 <= s - w and s - w + n <= win
            row += [w, s - w, i, n]
        table.append(row + [0] * (4 * REMAP_RUNS - len(row)))
    return table


def _remap_rows(src, row_of, n_out, block, name):
    n_src, cols = src.shape
    nb, win = n_out // block, block + 16
    table = jnp.asarray(_remap_table(row_of, n_out, block, n_src), jnp.int32)

    def body(t_ref, src_hbm, o_ref, buf, acc, sem):
        b = pl.program_id(0)
        slot = b & 1

        def each_run(step, sl, act):
            for k in range(REMAP_RUNS):
                @pl.when(t_ref[step, 4 * k + 3] > 0)
                def _():
                    start = pl.multiple_of(t_ref[step, 4 * k], 16)
                    act(pltpu.make_async_copy(src_hbm.at[pl.ds(start, win)], buf.at[sl, k], sem.at[sl, k]))

        @pl.when(b == 0)
        def _():
            each_run(0, 0, lambda cp: cp.start())

        @pl.when(b + 1 < nb)
        def _():
            each_run(b + 1, 1 - slot, lambda cp: cp.start())

        each_run(b, slot, lambda cp: cp.wait())
        acc[...] = jnp.zeros_like(acc)
        row = lax.broadcasted_iota(jnp.int32, (block, win), 0)
        col = lax.broadcasted_iota(jnp.int32, (block, win), 1)
        for k in range(REMAP_RUNS):
            shift, first, count = (t_ref[b, 4 * k + j] for j in (1, 2, 3))

            @pl.when(count > 0)
            def _():
                pick = (col == row - first + shift) & (row >= first) & (row < first + count)
                acc[...] += _nn(jnp.where(pick, 1.0, 0.0).astype(BF16), buf[slot, k])

        o_ref[...] = acc[...].astype(o_ref.dtype)

    return pl.pallas_call(
        body, name=name,
        grid_spec=pltpu.PrefetchScalarGridSpec(
            num_scalar_prefetch=1, grid=(nb,), in_specs=[pl.BlockSpec(memory_space=pl.ANY)],
            out_specs=pl.BlockSpec((block, cols), lambda b, t: (b, 0)),
            scratch_shapes=[pltpu.VMEM((2, REMAP_RUNS, win, cols), src.dtype), pltpu.VMEM((block, cols), F32),
                            pltpu.SemaphoreType.DMA((2, REMAP_RUNS))]),
        out_shape=S((n_out, cols), src.dtype),
        compiler_params=pltpu.CompilerParams(dimension_semantics=("arbitrary",)),
    )(table, src)


def _col_blocks(w, width):
    return w.reshape(w.shape[0], NDEV, width).transpose(1, 0, 2)


def _from_col_blocks(w):
    return w.transpose(1, 0, 2).reshape(w.shape[1], NDEV * w.shape[2])


def _local_step(x2, p2, pos, tgt, norm_g, qk_norm_q, qk_norm_k, gla_gate_b, gla_norm_g, ple_norm_g, w_al,
                weights=None, proj_side=None, unpack=None, dw_side_of=None, dh_side_of=None):
    half = ROT_DIM // 2
    inv8 = jnp.power(jnp.float32(ROPE_THETA), -jnp.arange(half, dtype=F32) * 2.0 / ROT_DIM)
    inv = jnp.tile(jnp.concatenate([inv8, inv8, jnp.zeros((HD - ROT_DIM,), F32)]), 2).reshape(1, 128)
    gq = jnp.tile(qk_norm_q, (1, 2))
    gk = jnp.tile(qk_norm_k, (1, 2))

    proj, h, got = _proj_rms(x2, norm_g, w_al, proj_side)
    if proj_side is not None:
        weights = unpack(got)
    w2p, w_att_f, w_gla_f, w_out_f, w_pg_f, w_ple_f = weights
    qkv = _qk_prep(proj, pos, inv, gq, gk)
    fwd = [_att_fwd(qkv[g], qkv[3 + g], qkv[6 + g], g, f"att_fwd{g}") for g in range(3)]
    att, lse, ain = _att_merge([f[0] for f in fwd], [f[1] for f in fwd], proj)
    o_gla, bin_, states = _gla_fwd(proj, w2p, gla_gate_b, gla_norm_g)
    ya, yb, y, x1 = _branches_fwd(ain, bin_, proj, x2, w_att_f, w_gla_f, w_out_f)
    n2, loss_v, dout, du, dw_ple = _ple_loss(x1, p2, tgt, ple_norm_g, w_pg_f, w_ple_f)

    dx1, dy, dg_ple, dw_pg, dw_out = _ple_bwd(du, n2, y, x1, dout, ple_norm_g, w_pg_f, w_out_f)
    dproj, dain, dbin, dw_att, dw_gla = _branches_bwd(dy, ya, yb, ain, bin_, proj, w_att_f, w_gla_f)
    dproj, da0, da1, da2, at1, at2, ls1, ls2 = _att_gate_bwd(dain, att, lse, proj, dproj)
    datts, atts, lses = (da0, da1, da2), (att[None], at1, at2), (lse[None], ls1, ls2)
    dproj, dw2, dbg, dgn = _gla_bwd(proj, w2p, gla_gate_b, gla_norm_g, o_gla, states, dbin, dproj)
    bwd = [_att_bwd(qkv[g], qkv[3 + g], qkv[6 + g], datts[g], atts[g], lses[g], g, f"att_bwd{g}") for g in range(3)]
    dproj, dgq, dgk = _qk_bwd(proj, pos, inv, gq, gk, [b[0] for b in bwd], [b[1] for b in bwd],
                              [b[2] for b in bwd], dproj)
    out = dict(loss=loss_v, dw2=dw2, dw_att=dw_att, dw_gla=dw_gla, dw_out=dw_out, dw_pg=dw_pg, dw_ple=dw_ple,
               dgq=dgq, dgk=dgk, dbg=dbg, dgn=dgn, dg_ple=dg_ple)
    if dw_side_of is None:
        dw_al = _mm(dproj, h, mode="tn", name="dw_in", tm=1536, tn=D, tk=2048, out_dtype=BF16)
    else:
        dw_al, out["dw_side"] = _mm(dproj, h, mode="tn", name="dw_in", tm=1536, tn=D, tk=2048, out_dtype=BF16,
                                    side=dw_side_of(out))
    grad_x, dg_norm, out["dh_side"] = _dh_rms(dproj, w_al, x2, norm_g, dx1,
                                              None if dh_side_of is None else dh_side_of(dw_al))
    out.update(grad_x=grad_x, dw_al=dw_al, dg_norm=dg_norm)
    return out


def kernel(x, p, positions, norm_g, w_in, qk_norm_q, qk_norm_k, gla_gate_w2, gla_gate_b, gla_norm_g, w_att_proj, w_gla_proj, w_out, ple_norm_g, w_ple_gate, w_ple, loss_target, m_norm_g, m_w_in, m_qk_norm_q, m_qk_norm_k, m_gla_gate_w2, m_gla_gate_b, m_gla_norm_g, m_w_att_proj, m_w_gla_proj, m_w_out, m_ple_norm_g, m_w_ple_gate, m_w_ple, v_norm_g, v_w_in, v_qk_norm_q, v_qk_norm_k, v_gla_gate_w2, v_gla_gate_b, v_gla_norm_g, v_w_att_proj, v_w_gla_proj, v_w_out, v_ple_norm_g, v_w_ple_gate, v_w_ple):
    x2, p2, tgt = x[0], p[0, 0], loss_target[0]
    pos = positions.astype(F32).reshape(T, 1)

    rows3 = jnp.stack([w_gla_proj[0], w_out[0], w_ple_gate[0]]).astype(BF16)
    cols3 = jnp.concatenate([w_att_proj[0], w_ple[0], jnp.pad(gla_gate_w2[0], ((0, 0), (0, 64)))], axis=0).astype(BF16)
    mine = jnp.pad(w_in[0].T.astype(BF16), ((0, SLAB - W_IN_SHARD), (0, 0)))
    (g_in,) = _all_gather_by_chip([mine], "gather_w_in")
    w_al = _remap_rows(g_in.reshape(NDEV * SLAB, D), _slab_row_of_aligned, NCOL, 256, "align_w_in")

    def unpack(got):
        g_rows, g_cols = got
        w2_f = _from_col_blocks(g_cols[:, 768:784, :64])
        return (jnp.pad(w2_f, ((0, GLR_W - GLR_N), (0, 0))), _from_col_blocks(g_cols[:, :512]),
                g_rows[:, 0].reshape(D, D), g_rows[:, 1].reshape(D, D), g_rows[:, 2].reshape(D, D),
                _from_col_blocks(g_cols[:, 512:768]))

    def dw_side_of(g):
        s_rows = jnp.concatenate([g[k].reshape(NDEV, 128, D) for k in ("dw_gla", "dw_out", "dw_pg")], axis=1)
        s_cols = jnp.concatenate([_col_blocks(g["dw_att"], 128), _col_blocks(g["dw_ple"], 128),
                                  jnp.pad(_col_blocks(g["dw2"][:GLR_N], 64), ((0, 0), (0, 0), (0, 64)))], axis=1)
        return _exchange_side([s_rows.astype(BF16), s_cols.astype(BF16)])

    def dh_side_of(dw_al):
        s_in = _remap_rows(dw_al, _aligned_row_of_slab, NDEV * SLAB, 432, "shard_dw_in").reshape(NDEV, SLAB, D)
        (from_sibling,) = _exchange_sibling([s_in], "exchange_sibling")
        core = lax.axis_index("c").astype(jnp.int32).reshape(1)
        return _chips_side([_pair_add(s_in, from_sibling, core, "pair_add")])

    loc = _local_step(x2, p2, pos, tgt, norm_g, qk_norm_q, qk_norm_k, gla_gate_b, gla_norm_g, ple_norm_g, w_al,
                      proj_side=_gather_side([rows3, cols3]), unpack=unpack, dw_side_of=dw_side_of,
                      dh_side_of=dh_side_of)
    loss_v, grad_x = loc["loss"], loc["grad_x"]
    dg_norm, dgq, dgk, dbg, dgn, dg_ple = (loc[k] for k in ("dg_norm", "dgq", "dgk", "dbg", "dgn", "dg_ple"))
    r_rows, r_cols = loc["dw_side"]
    (r_in,) = loc["dh_side"]

    r_small = _comm_call(_gather_side([dg_norm, dgq, dgk, dbg, dgn, dg_ple, loss_v]), "gather_small")

    outs = {}

    def adam(nm, parts, w, m, v, tr):
        outs[nm] = _adamw(parts, w, m, v, "adam_" + nm, tr)

    rows_of = lambda a: jnp.transpose(a, (2, 0, 1))
    outs["w_in"] = [jnp.transpose(o, (1, 2, 0))[0] for o in
                    _adamw_rows(r_in, rows_of(w_in), rows_of(m_w_in), rows_of(v_w_in), "adam_w_in")]
    adam("w_gla_proj", r_rows[:, :128], w_gla_proj[0], m_w_gla_proj[0], v_w_gla_proj[0], 128)
    adam("w_out", r_rows[:, 128:256], w_out[0], m_w_out[0], v_w_out[0], 128)
    adam("w_ple_gate", r_rows[:, 256:], w_ple_gate[0], m_w_ple_gate[0], v_w_ple_gate[0], 128)
    adam("w_att_proj", r_cols[:, :512], w_att_proj[0], m_w_att_proj[0], v_w_att_proj[0], 512)
    adam("w_ple", r_cols[:, 512:768], w_ple[0], m_w_ple[0], v_w_ple[0], 256)
    adam("gla_gate_w2", r_cols[:, 768:784, :64], gla_gate_w2[0], m_gla_gate_w2[0], v_gla_gate_w2[0], 16)
    small = ((norm_g, m_norm_g, v_norm_g), (qk_norm_q, m_qk_norm_q, v_qk_norm_q), (qk_norm_k, m_qk_norm_k, v_qk_norm_k),
             (gla_gate_b, m_gla_gate_b, v_gla_gate_b), (gla_norm_g, m_gla_norm_g, v_gla_norm_g),
             (ple_norm_g, m_ple_norm_g, v_ple_norm_g))
    sm = _adamw_small(r_small[:6], small, r_small[6])
    for j, nm in enumerate(("norm_g", "qk_norm_q", "qk_norm_k", "gla_gate_b", "gla_norm_g", "ple_norm_g")):
        outs[nm] = [o[0] for o in sm[4 * j:4 * j + 4]]

    loss = sm[-1][0, 0]
    order = ["norm_g", "w_in", "qk_norm_q", "qk_norm_k", "gla_gate_w2", "gla_gate_b", "gla_norm_g", "w_att_proj",
             "w_gla_proj", "w_out", "ple_norm_g", "w_ple_gate", "w_ple"]
    result = [loss, grad_x[None]]
    for i in range(4):
        result += [outs[nm][i][None] for nm in order]
    return tuple(result)
```

```python
import functools

import jax
import jax.numpy as jnp
from jax import lax
from jax.experimental import pallas as pl
from jax.experimental.pallas import tpu as pltpu

F32 = jnp.float32
BF16 = jnp.bfloat16
S = jax.ShapeDtypeStruct

T = 4096
D = 1024
NDEV = 8
HD = 64
ATT_W = 512
ATT_QKV = 1536
DILATIONS = (1, 4, 16)
BLK = 128
GH, GDK, GDV = 4, 128, 256
GLA_C = 128
PLE = 256
EPS = 1e-6
ROT_DIM = 16
ROPE_THETA = 500000.0
GLA_TAU = 16.0
W_IN_COLS = 10256
W_IN_SHARD = 1282

C_QG, C_KG, C_VG, C_ZG, C_GLR, C_ZA, C_GA, C_GB, C_QA, C_KA, C_VA = (
    0, 512, 1024, 2048, 3072, 3584, 4096, 5120, 6144, 7680, 9216)
GLA_GROUP_W = 3584
GLR_W = 512
NCOL = 10752
GLR_N = 16
O_QA, O_ZA, O_QG, O_GLR, O_ZG, O_GA, O_END = 0, 4608, 5120, 7168, 7184, 8208, 10256

ADAM_LR, ADAM_B1, ADAM_B2, ADAM_EPS, ADAM_WD, ADAM_STEP = 0.001, 0.9, 0.999, 1e-08, 0.01, 10

MESH = pl.DeviceIdType.MESH


def _sigmoid(z):
    return 1.0 / (1.0 + jnp.exp(-z))


def _dot(a, b, dims):
    return lax.dot_general(a, b, (dims, ((), ())), preferred_element_type=F32)


def _nn(a, b):
    return _dot(a, b, ((1,), (0,)))


def _nt(a, b):
    return _dot(a, b, ((1,), (1,)))


def _tn(a, b):
    return _dot(a, b, ((0,), (0,)))


def _mm(a, b, *, mode, name, tm, tn, tk, out_dtype=F32, res=None, side=None):
    if mode == "nn":
        (m, k), n = a.shape, b.shape[1]
        a_spec = pl.BlockSpec((tm, tk), lambda i, j, l: (i, l))
        b_spec = pl.BlockSpec((tk, tn), lambda i, j, l: (l, j))
        dot = _nn
    elif mode == "nt":
        (m, k), n = a.shape, b.shape[0]
        a_spec = pl.BlockSpec((tm, tk), lambda i, j, l: (i, l))
        b_spec = pl.BlockSpec((tn, tk), lambda i, j, l: (j, l))
        dot = _nt
    else:
        (k, m), n = a.shape, b.shape[1]
        a_spec = pl.BlockSpec((tk, tm), lambda i, j, l: (l, i))
        b_spec = pl.BlockSpec((tk, tn), lambda i, j, l: (l, j))
        dot = _tn
    assert m % tm == 0 and n % tn == 0 and k % tk == 0, (name, m, n, k)
    grid = (m // tm, n // tn, k // tk)
    nk = grid[2]
    o_spec = pl.BlockSpec((tm, tn), lambda i, j, l: (i, j))
    in_specs = [a_spec, b_spec]
    args = [a, b]
    if res is not None:
        in_specs.append(o_spec)
        args.append(res)
    n_in = len(args)
    n_side = 0 if side is None else len(side["arrs"])
    hbm = pl.BlockSpec(memory_space=pl.ANY)

    def body(*refs):
        a_ref, b_ref = refs[0], refs[1]
        r_ref = refs[2] if res is not None else None
        o_ref = refs[n_in + n_side]
        scratch = refs[n_in + 2 * n_side + 1:]
        if side is not None:
            start, finish_side = side["plan"](refs[n_in:n_in + n_side], refs[n_in + n_side + 1:n_in + 2 * n_side + 1],
                                              *scratch[1 if nk > 1 else 0:])
            ids = [pl.program_id(d) for d in range(3)]

            @pl.when((ids[0] == 0) & (ids[1] == 0) & (ids[2] == 0))
            def _():
                start()

        part = dot(a_ref[...].astype(BF16), b_ref[...].astype(BF16))

        def finish(val):
            if r_ref is not None:
                val = val + r_ref[...]
            o_ref[...] = val.astype(out_dtype)

        if nk == 1:
            finish(part)
        else:
            acc = scratch[0]
            l = pl.program_id(2)

            @pl.when(l == 0)
            def _():
                acc[...] = part

            @pl.when(l > 0)
            def _():
                acc[...] += part

            @pl.when(l == nk - 1)
            def _():
                finish(acc[...])

        if side is not None:
            @pl.when((ids[0] == grid[0] - 1) & (ids[1] == grid[1] - 1) & (ids[2] == grid[2] - 1))
            def _():
                finish_side()

    sems = [] if side is None else side["scratch"]
    outs = pl.pallas_call(
        body, name=name, grid=grid,
        in_specs=in_specs + [hbm] * n_side, out_specs=[o_spec] + [hbm] * n_side,
        out_shape=[S((m, n), out_dtype)] + ([] if side is None else side["out_shape"]),
        scratch_shapes=([pltpu.VMEM((tm, tn), F32)] if nk > 1 else []) + sems,
        compiler_params=pltpu.CompilerParams(
            dimension_semantics=("arbitrary",) * 3 if side is not None else ("parallel", "parallel", "arbitrary")),
    )(*args, *([] if side is None else side["arrs"]))
    return outs[0] if side is None else (outs[0], outs[1:])


def _side_parts(side, refs, n_in, n_out):
    n_side = 0 if side is None else len(side["arrs"])
    scratch = refs[n_in + n_out + 2 * n_side:]
    if side is None:
        return (lambda: None), (lambda: None), scratch
    start, finish = side["plan"](refs[n_in:n_in + n_side], refs[n_in + n_side + n_out:n_in + n_out + 2 * n_side],
                                 *scratch[len(scratch) - len(side["scratch"]):])
    return start, finish, scratch


def _proj_rms(x, g, wt, side=None):
    tm, tn = 1024, 1536
    grid = (T // tm, NCOL // tn)
    n_side = 0 if side is None else len(side["arrs"])
    hbm = pl.BlockSpec(memory_space=pl.ANY)

    def body(*refs):
        x_ref, g_ref, w_ref = refs[:3]
        o_ref, h_ref = refs[3 + n_side], refs[4 + n_side]
        start, finish, _ = _side_parts(side, refs, 3, 2)
        i, j = pl.program_id(0), pl.program_id(1)

        @pl.when((i == 0) & (j == 0))
        def _():
            start()

        @pl.when(j == 0)
        def _():
            xf = x_ref[...]
            r = lax.rsqrt(jnp.mean(xf * xf, axis=-1, keepdims=True) + EPS)
            h_ref[...] = (xf * r * g_ref[...]).astype(BF16)

        o_ref[...] = _nt(h_ref[...], w_ref[...])

        @pl.when((i == grid[0] - 1) & (j == grid[1] - 1))
        def _():
            finish()

    outs = pl.pallas_call(
        body, name="proj", grid=grid,
        in_specs=[pl.BlockSpec((tm, D), lambda i, j: (i, 0)), pl.BlockSpec((1, D), lambda i, j: (0, 0)),
                  pl.BlockSpec((tn, D), lambda i, j: (j, 0))] + [hbm] * n_side,
        out_specs=[pl.BlockSpec((tm, tn), lambda i, j: (i, j)), pl.BlockSpec((tm, D), lambda i, j: (i, 0))] + [hbm] * n_side,
        out_shape=[S((T, NCOL), F32), S((T, D), BF16)] + ([] if side is None else side["out_shape"]),
        scratch_shapes=[] if side is None else side["scratch"],
        compiler_params=pltpu.CompilerParams(dimension_semantics=("arbitrary", "arbitrary")),
    )(x, g, wt, *([] if side is None else side["arrs"]))
    return outs[0], outs[1], outs[2:]


def _dh_rms(dproj, wt, x, g, skip, side=None):
    tm, tk = 1024, 2688
    grid = (T // tm, NCOL // tk)
    n_side = 0 if side is None else len(side["arrs"])
    hbm = pl.BlockSpec(memory_space=pl.ANY)

    def body(*refs):
        a_ref, w_ref, x_ref, g_ref, s_ref = refs[:5]
        dx_ref, dg_ref = refs[5 + n_side], refs[6 + n_side]
        start, finish, scratch = _side_parts(side, refs, 5, 2)
        acc = scratch[0]
        i, l = pl.program_id(0), pl.program_id(1)

        @pl.when((i == 0) & (l == 0))
        def _():
            start()

        part = _nn(a_ref[...], w_ref[...])

        @pl.when(l == 0)
        def _():
            acc[...] = part

        @pl.when(l > 0)
        def _():
            acc[...] += part

        @pl.when(l == grid[1] - 1)
        def _():
            xf = x_ref[...]
            r = lax.rsqrt(jnp.mean(xf * xf, axis=-1, keepdims=True) + EPS)
            dn = acc[...]
            u = dn * g_ref[...]
            dx_ref[...] = s_ref[...] + r * u - xf * (r * r * r) * jnp.mean(u * xf, axis=-1, keepdims=True)
            dg = jnp.sum(dn * xf * r, axis=0, keepdims=True)

            @pl.when(i == 0)
            def _():
                dg_ref[...] = dg

            @pl.when(i > 0)
            def _():
                dg_ref[...] += dg

        @pl.when((i == grid[0] - 1) & (l == grid[1] - 1))
        def _():
            finish()

    tok = pl.BlockSpec((tm, D), lambda i, l: (i, 0))
    outs = pl.pallas_call(
        body, name="dh", grid=grid,
        in_specs=[pl.BlockSpec((tm, tk), lambda i, l: (i, l)), pl.BlockSpec((tk, D), lambda i, l: (l, 0)), tok,
                  pl.BlockSpec((1, D), lambda i, l: (0, 0)), tok] + [hbm] * n_side,
        out_specs=[tok, pl.BlockSpec((1, D), lambda i, l: (0, 0))] + [hbm] * n_side,
        out_shape=[S((T, D), F32), S((1, D), F32)] + ([] if side is None else side["out_shape"]),
        scratch_shapes=[pltpu.VMEM((tm, D), F32)] + ([] if side is None else side["scratch"]),
        compiler_params=pltpu.CompilerParams(dimension_semantics=("arbitrary", "arbitrary")),
    )(dproj, wt, x, g, skip, *([] if side is None else side["arrs"]))
    return outs[0], outs[1], outs[2:]


def _rot_tables(pos_ref, inv_ref):
    lane = lax.broadcasted_iota(jnp.int32, (1, 128), 1) % HD
    ang = pos_ref[...] * inv_ref[...]
    cos, sin = jnp.cos(ang), jnp.sin(ang)
    c = jnp.where(lane < ROT_DIM, cos, 1.0)
    sp = jnp.where((lane >= ROT_DIM // 2) & (lane < ROT_DIM), sin, 0.0)
    sm = jnp.where(lane < ROT_DIM // 2, -sin, 0.0)
    return c, sp, sm


def _head_sums(v):
    same = (lax.broadcasted_iota(jnp.int32, (128, 128), 0) < HD) == (lax.broadcasted_iota(jnp.int32, (128, 128), 1) < HD)
    ones = jnp.where(same, 1.0, 0.0).astype(BF16)
    hi = v.astype(BF16)
    lo = (v - hi.astype(F32)).astype(BF16)
    return _nn(hi, ones) + _nn(lo, ones)


def _pair_norm(t):
    return lax.rsqrt(_head_sums(t * t) * (1.0 / HD) + EPS)


def _pair_mean(t):
    return _head_sums(t) * (1.0 / HD)


TT = 256
NCH = ATT_QKV // 128


def _res_shape(grp, dtype):
    return S((DILATIONS[grp], T // DILATIONS[grp], ATT_W), dtype)


def _res_spec(grp):
    dil = DILATIONS[grp]
    return pl.BlockSpec((dil, TT // dil, ATT_W), lambda i: (0, i, 0))


def _to_residues(sc, j, dst_ref, dil, cols):
    n = TT // dil
    for r in range(dil):
        rows = sc[j] if dil == 1 else sc.at[j][pl.ds(r, n, stride=dil), :]
        dst_ref[r, :, cols] = rows.astype(dst_ref.dtype)


def _from_residues(src_ref, cols, sc, j, dil):
    n = TT // dil
    for r in range(dil):
        if dil == 1:
            sc[j] = src_ref[r, :, cols]
        else:
            sc.at[j][pl.ds(r, n, stride=dil), :] = src_ref[r, :, cols]


def _tok_spec(width, cblk=0):
    return pl.BlockSpec((TT, width), functools.partial(lambda i, c: (i, c), c=cblk))


def _const_spec(arr_or_shape):
    shape = arr_or_shape if isinstance(arr_or_shape, tuple) else arr_or_shape.shape
    return pl.BlockSpec(shape, functools.partial(lambda i, nd: (0,) * nd, nd=len(shape)))


def _qk_prep(proj, pos, inv, gq, gk):
    def body(q_ref, k_ref, v_ref, pos_ref, inv_ref, gq_ref, gk_ref, *rest):
        outs, sc = rest[:9], rest[9]
        c, sp, sm = _rot_tables(pos_ref, inv_ref)
        for which, (src, g_ref) in enumerate(((q_ref, gq_ref), (k_ref, gk_ref), (v_ref, None))):
            if g_ref is not None:
                g = jnp.broadcast_to(g_ref[...] * ((HD ** -0.5) if which == 0 else 1.0), c.shape)
                cg, spg, smg = c * g, sp * pltpu.roll(g, 8, 1), sm * pltpu.roll(g, 120, 1)
            for j in range(NCH):
                t = src[:, j * 128:(j + 1) * 128]
                if g_ref is not None:
                    t = _pair_norm(t) * (t * cg + pltpu.roll(t, 8, 1) * spg + pltpu.roll(t, 120, 1) * smg)
                sc[j] = t
            for j in range(NCH):
                grp, sub = divmod(j * 128, ATT_W)
                _to_residues(sc, j, outs[which * 3 + grp], DILATIONS[grp], slice(sub, sub + 128))

    return pl.pallas_call(
        body, name="qk_prep", grid=(T // TT,),
        in_specs=[_tok_spec(ATT_QKV, C_QA // ATT_QKV), _tok_spec(ATT_QKV, C_KA // ATT_QKV),
                  _tok_spec(ATT_QKV, C_VA // ATT_QKV), _tok_spec(1), _const_spec(inv), _const_spec(gq), _const_spec(gk)],
        out_specs=[_res_spec(g) for _ in range(3) for g in range(3)],
        out_shape=[_res_shape(g, BF16) for _ in range(3) for g in range(3)],
        scratch_shapes=[pltpu.VMEM((NCH, TT, 128), F32)],
        compiler_params=pltpu.CompilerParams(dimension_semantics=("arbitrary",)),
    )(proj, proj, proj, pos, inv, gq, gk)


def _qk_bwd(proj, pos, inv, gq, gk, dqs, dks, dvs, dproj):
    const = lambda a: pl.BlockSpec(a.shape, functools.partial(lambda i, p, nd: (0,) * nd, nd=a.ndim))
    res = lambda g: pl.BlockSpec((DILATIONS[g], TT // DILATIONS[g], ATT_W), lambda i, p: (0, i, 0))
    base = C_QA // ATT_QKV

    def body(t_ref, pos_ref, inv_ref, gq_ref, gk_ref, dq0, dq1, dq2, dk0, dk1, dk2, dv0, dv1, dv2, buf_ref,
             out_ref, dgq_ref, dgk_ref, sc):
        del buf_ref
        part = pl.program_id(1)
        first = pl.program_id(0) == 0

        def gather(drefs):
            for j in range(NCH):
                grp, sub = divmod(j * 128, ATT_W)
                _from_residues(drefs[grp], slice(sub, sub + 128), sc, j, DILATIONS[grp])

        def normed(g_ref, drefs, dg_ref):
            c, sp, sm = _rot_tables(pos_ref, inv_ref)
            gather(drefs)
            dg = jnp.zeros((1, 128), F32)
            for j in range(NCH):
                cols = slice(j * 128, (j + 1) * 128)
                d_rot = sc[j]
                dn = d_rot * c + pltpu.roll(d_rot * sp, 120, 1) + pltpu.roll(d_rot * sm, 8, 1)
                t = t_ref[:, cols]
                r = _pair_norm(t)
                u = dn * g_ref[...]
                out_ref[:, cols] = (r * u - t * (r * r * r) * _pair_mean(u * t)).astype(BF16)
                dg = dg + jnp.sum(dn * t * r, axis=0, keepdims=True)
            dg = dg + pltpu.roll(dg, HD, 1)

            @pl.when(first)
            def _():
                dg_ref[...] = dg

            @pl.when(jnp.logical_not(first))
            def _():
                dg_ref[...] += dg

        @pl.when(part == 0)
        def _():
            gather((dv0, dv1, dv2))
            for j in range(NCH):
                out_ref[:, j * 128:(j + 1) * 128] = sc[j].astype(BF16)

        @pl.when(part == 1)
        def _():
            normed(gq_ref, (dq0, dq1, dq2), dgq_ref)

        @pl.when(part == 2)
        def _():
            normed(gk_ref, (dk0, dk1, dk2), dgk_ref)

    keep = pl.BlockSpec((1, 128), lambda i, p: (0, 0))
    return pl.pallas_call(
        body, name="qk_bwd", grid=(T // TT, 3),
        in_specs=[pl.BlockSpec((TT, ATT_QKV), lambda i, p: (i, base + jnp.maximum(p - 1, 0))),
                  pl.BlockSpec((TT, 1), lambda i, p: (i, 0)), const(inv), const(gq), const(gk)]
        + [res(g) for _ in range(3) for g in range(3)] + [pl.BlockSpec(memory_space=pl.ANY)],
        out_specs=[pl.BlockSpec((TT, ATT_QKV), lambda i, p: (i, base + jnp.where(p == 0, 2, p - 1))), keep, keep],
        out_shape=[S(dproj.shape, dproj.dtype), S((1, 128), F32), S((1, 128), F32)],
        input_output_aliases={14: 0},
        scratch_shapes=[pltpu.VMEM((NCH, TT, 128), F32)],
        compiler_params=pltpu.CompilerParams(dimension_semantics=("arbitrary", "arbitrary")),
    )(proj, pos, inv, gq, gk, *dqs, *dks, *dvs, dproj)


def _split_heads(t):
    low = lax.broadcasted_iota(jnp.int32, (1, 128), 1) < HD
    zero = jnp.zeros_like(t)
    return jnp.concatenate([jnp.where(low, t, zero), jnp.where(low, zero, t)], axis=0)


def _join_heads(t2):
    low = lax.broadcasted_iota(jnp.int32, (1, 128), 1) < HD
    n = t2.shape[0] // 2
    return jnp.where(low, t2[:n], t2[n:])


def _band_mask4(has_before, has_own):
    row = lax.broadcasted_iota(jnp.int32, (BLK, 4 * BLK), 0)
    lane = lax.broadcasted_iota(jnp.int32, (BLK, 4 * BLK), 1)
    key = lane & (BLK - 1)
    own = lane >= 2 * BLK
    return (own & (key <= row) & has_own) | (jnp.logical_not(own) & (key >= row) & has_before)


def _band_mask_before(has_before):
    row = lax.broadcasted_iota(jnp.int32, (BLK, 2 * BLK), 0)
    key = lax.broadcasted_iota(jnp.int32, (BLK, 2 * BLK), 1) & (BLK - 1)
    return (key >= row) & has_before


def _per_head(width, col_a, col_b):
    lane = lax.broadcasted_iota(jnp.int32, (1, width), 1)
    return jnp.where((lane & BLK) == 0, col_a, col_b)


NQ = ATT_W // 128


def _att_fwd(q, k, v, grp, name):
    dil = DILATIONS[grp]
    nb = T // dil // BLK

    def body(q_ref, kp_ref, kc_ref, vp_ref, vc_ref, o_ref, lse_ref, s_sc, p_sc):
        mask = _band_mask4(pl.program_id(1) > 0, True)
        low = lax.broadcasted_iota(jnp.int32, (1, 128), 1) < HD
        halves = lambda ref, j, h: (ref[j, :, h * BLK:(h + 1) * BLK], ref[j, :, (h + 2) * BLK:(h + 3) * BLK])
        for j in range(NQ):
            cols = slice(j * 128, (j + 1) * 128)
            k4 = jnp.concatenate([_split_heads(kp_ref[:, cols]), _split_heads(kc_ref[:, cols])], axis=0)
            s_sc[j] = jnp.where(mask, _nt(q_ref[:, cols], k4), -jnp.inf)
        mxs = [[jnp.maximum(*(jnp.max(t, axis=-1, keepdims=True) for t in halves(s_sc, j, h))) for h in range(2)]
               for j in range(NQ)]
        dens = []
        for j in range(NQ):
            p = jnp.exp(s_sc[j] - _per_head(4 * BLK, *mxs[j]))
            p_sc[j] = p.astype(BF16)
            dens.append([jnp.sum(p[:, h * BLK:(h + 1) * BLK], axis=-1, keepdims=True)
                         + jnp.sum(p[:, (h + 2) * BLK:(h + 3) * BLK], axis=-1, keepdims=True) for h in range(2)])
        for j in range(NQ):
            cols = slice(j * 128, (j + 1) * 128)
            v4 = jnp.concatenate([_split_heads(vp_ref[:, cols]), _split_heads(vc_ref[:, cols])], axis=0)
            o_ref[:, cols] = _nn(p_sc[j], v4) / jnp.where(low, dens[j][0], dens[j][1])
            lse_ref[:, cols] = jnp.where(low, mxs[j][0] + jnp.log(dens[j][0]), mxs[j][1] + jnp.log(dens[j][1]))

    cur = pl.BlockSpec((None, BLK, ATT_W), lambda r, i: (r, i, 0))
    prev = pl.BlockSpec((None, BLK, ATT_W), lambda r, i: (r, jnp.maximum(i - 1, 0), 0))
    return pl.pallas_call(
        body, name=name, grid=(dil, nb),
        in_specs=[cur, prev, cur, prev, cur],
        out_specs=[cur, cur], out_shape=[_res_shape(grp, F32)] * 2,
        scratch_shapes=[pltpu.VMEM((NQ, BLK, 4 * BLK), F32), pltpu.VMEM((NQ, BLK, 4 * BLK), BF16)],
        compiler_params=pltpu.CompilerParams(dimension_semantics=("parallel", "arbitrary")),
    )(q, k, k, v, v)


def _att_bwd(q, k, v, datt, att, lse, grp, name):
    dil = DILATIONS[grp]
    nb = T // dil // BLK
    scale = HD ** -0.5

    def body(q0_ref, q1_ref, kp_ref, kc_ref, vp_ref, vc_ref, do0_ref, do1_ref, o0_ref, o1_ref, l0_ref, l1_ref,
             dq_ref, dk_ref, dv_ref, k4_sc, v4_sc, s0_sc, s1_sc, dp0_sc, dp1_sc, p_sc, ds_sc):
        i = pl.program_id(1)
        mask_mine = _band_mask4(i > 0, True)
        mask_next = _band_mask_before(i < nb - 1)
        low = lax.broadcasted_iota(jnp.int32, (1, 128), 1) < HD
        for j in range(NQ):
            cols = slice(j * 128, (j + 1) * 128)
            k4_sc[j, :2 * BLK] = _split_heads(kp_ref[:, cols])
            k4_sc[j, 2 * BLK:] = _split_heads(kc_ref[:, cols])
            v4_sc[j, :2 * BLK] = _split_heads(vp_ref[:, cols])
            v4_sc[j, 2 * BLK:] = _split_heads(vc_ref[:, cols])
        for j in range(NQ):
            cols = slice(j * 128, (j + 1) * 128)
            s0_sc[j] = _nt(q0_ref[:, cols], k4_sc[j])
            s1_sc[j] = _nt(q1_ref[:, cols], k4_sc[j, 2 * BLK:])
            dp0_sc[j] = _nt(do0_ref[:, cols].astype(BF16), v4_sc[j])
            dp1_sc[j] = _nt(do1_ref[:, cols].astype(BF16), v4_sc[j, 2 * BLK:])
        stats = []
        for j in range(NQ):
            cols = slice(j * 128, (j + 1) * 128)
            for do_ref, o_ref, l_ref in ((do0_ref, o0_ref, l0_ref), (do1_ref, o1_ref, l1_ref)):
                prod = do_ref[:, cols].astype(F32) * o_ref[:, cols].astype(F32)
                d_all = jnp.sum(prod, axis=-1, keepdims=True)
                d_low = jnp.sum(jnp.where(low, prod, 0.0), axis=-1, keepdims=True)
                lse_t = l_ref[:, cols]
                stats.append((d_low, d_all - d_low, lse_t[:, 0:1], lse_t[:, HD:HD + 1]))
        for j in range(NQ):
            (da, db, la, lb), (da1, db1, la1, lb1) = stats[2 * j], stats[2 * j + 1]
            p0 = jnp.where(mask_mine, jnp.exp(s0_sc[j] - _per_head(4 * BLK, la, lb)), 0.0)
            ds0 = p0 * (dp0_sc[j] - _per_head(4 * BLK, da, db))
            p1 = jnp.where(mask_next, jnp.exp(s1_sc[j] - _per_head(2 * BLK, la1, lb1)), 0.0)
            ds1 = p1 * (dp1_sc[j] - _per_head(2 * BLK, da1, db1))
            p_sc[j, :BLK] = p0.astype(BF16)
            ds_sc[j, :BLK] = ds0.astype(BF16)
            p_sc[j, BLK:, 2 * BLK:] = p1.astype(BF16)
            ds_sc[j, BLK:, 2 * BLK:] = ds1.astype(BF16)
        for j in range(NQ):
            cols = slice(j * 128, (j + 1) * 128)
            dq_ref[:, cols] = _nn(ds_sc[j, :BLK], k4_sc[j]) * scale
            qq = jnp.concatenate([q0_ref[:, cols], q1_ref[:, cols]], axis=0)
            dd = jnp.concatenate([do0_ref[:, cols], do1_ref[:, cols]], axis=0).astype(BF16)
            dk_ref[:, cols] = _join_heads(_tn(ds_sc[j, :, 2 * BLK:], qq))
            dv_ref[:, cols] = _join_heads(_tn(p_sc[j, :, 2 * BLK:], dd))

    def spec(shift):
        return pl.BlockSpec((None, BLK, ATT_W), lambda r, i: (r, jnp.clip(i + shift, 0, nb - 1), 0))

    here, after, before = spec(0), spec(1), spec(-1)
    vm = pltpu.VMEM
    return pl.pallas_call(
        body, name=name, grid=(dil, nb),
        in_specs=[here, after, before, here, before, here, here, after, here, after, here, after],
        out_specs=[here] * 3, out_shape=[_res_shape(grp, F32)] * 3,
        scratch_shapes=[vm((NQ, 4 * BLK, 128), BF16), vm((NQ, 4 * BLK, 128), BF16), vm((NQ, BLK, 4 * BLK), F32),
                        vm((NQ, BLK, 2 * BLK), F32), vm((NQ, BLK, 4 * BLK), F32), vm((NQ, BLK, 2 * BLK), F32),
                        vm((NQ, 2 * BLK, 4 * BLK), BF16), vm((NQ, 2 * BLK, 4 * BLK), BF16)],
        compiler_params=pltpu.CompilerParams(dimension_semantics=("parallel", "arbitrary")),
    )(q, q, k, k, v, v, datt, datt, att, att, lse, lse)


def _att_merge(os_, lses, proj):
    nq = ATT_W // 128

    def body(o0, o1, o2, l0, l1, l2, za_ref, att_ref, lse_ref, ain_ref, sc):
        for a, ref in enumerate((o0, o1, o2, l0, l1, l2)):
            for j in range(nq):
                _from_residues(ref, slice(j * 128, (j + 1) * 128), sc, a * nq + j, DILATIONS[a % 3])
        for j in range(nq):
            cols = slice(j * 128, (j + 1) * 128)
            oa, ob, oc = (sc[a * nq + j] for a in range(3))
            la, lb, lc = (sc[(3 + a) * nq + j] for a in range(3))
            m = jnp.maximum(jnp.maximum(la, lb), lc)
            wa, wb, wc = jnp.exp(la - m), jnp.exp(lb - m), jnp.exp(lc - m)
            tot = wa + wb + wc
            att = (wa * oa + wb * ob + wc * oc) / tot
            att_ref[:, cols] = att
            lse_ref[:, cols] = m + jnp.log(tot)
            za = za_ref[:, cols]
            ain_ref[:, cols] = (att * za * _sigmoid(za)).astype(BF16)

    return pl.pallas_call(
        body, name="att_merge", grid=(T // TT,),
        in_specs=[_res_spec(g) for _ in range(2) for g in range(3)] + [_tok_spec(ATT_W, C_ZA // ATT_W)],
        out_specs=[_tok_spec(ATT_W)] * 3,
        out_shape=[S((T, ATT_W), F32), S((T, ATT_W), F32), S((T, ATT_W), BF16)],
        scratch_shapes=[pltpu.VMEM((6 * nq, TT, 128), F32)],
        compiler_params=pltpu.CompilerParams(dimension_semantics=("arbitrary",)),
    )(*os_, *lses, proj)


def _att_gate_bwd(dain, att, lse, proj, dproj):
    nq = ATT_W // 128

    def body(d_ref, att_ref, lse_ref, za_ref, buf_ref, dza_ref, da0, da1, da2, at1, at2, ls1, ls2, sc):
        del buf_ref
        for j in range(nq):
            cols = slice(j * 128, (j + 1) * 128)
            za = za_ref[:, cols]
            sg = _sigmoid(za)
            d = d_ref[:, cols].astype(F32)
            att_ = att_ref[:, cols]
            dza_ref[:, cols] = (d * att_ * sg * (1.0 + za * (1.0 - sg))).astype(BF16)
            sc[j] = d * za * sg
            sc[nq + j] = att_
            sc[2 * nq + j] = lse_ref[:, cols]
        for j in range(nq):
            cols = slice(j * 128, (j + 1) * 128)
            for grp, dst in enumerate((da0, da1, da2)):
                _to_residues(sc, j, dst, DILATIONS[grp], cols)
            for grp, dst in ((1, at1), (2, at2)):
                _to_residues(sc, nq + j, dst, DILATIONS[grp], cols)
            for grp, dst in ((1, ls1), (2, ls2)):
                _to_residues(sc, 2 * nq + j, dst, DILATIONS[grp], cols)

    res = (0, 1, 2, 1, 2, 1, 2)
    return pl.pallas_call(
        body, name="att_gate_bwd", grid=(T // TT,),
        in_specs=[_tok_spec(ATT_W)] * 3 + [_tok_spec(ATT_W, C_ZA // ATT_W), pl.BlockSpec(memory_space=pl.ANY)],
        out_specs=[_tok_spec(ATT_W, C_ZA // ATT_W)] + [_res_spec(g) for g in res],
        out_shape=[S(dproj.shape, dproj.dtype)] + [_res_shape(g, BF16) for g in res[:5]]
        + [_res_shape(g, F32) for g in res[5:]],
        input_output_aliases={4: 0},
        scratch_shapes=[pltpu.VMEM((3 * nq, TT, 128), F32)],
        compiler_params=pltpu.CompilerParams(dimension_semantics=("arbitrary",)),
    )(dain, att, lse, proj, dproj)


def _split3(v):
    hi = v.astype(BF16)
    r1 = v - hi.astype(F32)
    mid = r1.astype(BF16)
    lo = (r1 - mid.astype(F32)).astype(BF16)
    return hi, mid, lo


def _chunk_scores(qt, kt, q_ref, k_ref, h):
    cols = slice(h * GDK, (h + 1) * GDK)
    own = jnp.sum(q_ref[:, cols] * (GDK ** -0.5) * k_ref[:, cols], axis=-1, keepdims=True)
    row = lax.broadcasted_iota(jnp.int32, (GLA_C, GLA_C), 0)
    col = lax.broadcasted_iota(jnp.int32, (GLA_C, GLA_C), 1)
    a = _nt(qt.astype(BF16), kt.astype(BF16))
    return jnp.where(col < row, a, jnp.where(col == row, own, 0.0))


def _tri_sum(v, upper):
    n = v.shape[0]
    row = lax.broadcasted_iota(jnp.int32, (n, n), 0)
    col = lax.broadcasted_iota(jnp.int32, (n, n), 1)
    tri = jnp.where(col >= row if upper else col <= row, 1.0, 0.0).astype(BF16)
    hi, mid, lo = _split3(v)
    return _nn(tri, hi) + _nn(tri, mid) + _nn(tri, lo)


def _gla_gates(glr_ref, w2_ref, b_ref):
    logit = _nn(glr_ref[...].astype(BF16), w2_ref[...]) + b_ref[...]
    lg = (jnp.minimum(logit, 0.0) - jnp.log(1.0 + jnp.exp(-jnp.abs(logit)))) * (1.0 / GLA_TAU)
    return logit, _tri_sum(lg, upper=False)


def _gla_head(cum, q_ref, k_ref, h):
    cols = slice(h * GDK, (h + 1) * GDK)
    b = cum[:, cols]
    last = b[GLA_C - 1:GLA_C, :]
    e_pos = jnp.exp(b)
    e_neg = jnp.exp(-b)
    e_end = jnp.exp(last - b)
    qt = q_ref[:, cols] * (GDK ** -0.5) * e_pos
    kt = k_ref[:, cols] * e_neg
    kh = k_ref[:, cols] * e_end
    return b, last, e_pos, e_neg, e_end, qt, kt, kh


def _causal(n):
    return lax.broadcasted_iota(jnp.int32, (n, n), 1) <= lax.broadcasted_iota(jnp.int32, (n, n), 0)


def _gla_fwd(proj, w2p, bg, gn):
    nc = T // GLA_C

    def body(q_ref, k_ref, v_ref, glr_ref, zg_ref, w2_ref, b_ref, gn_ref, o_ref, bin_ref, st_ref, state):
        @pl.when(pl.program_id(0) == 0)
        def _():
            state[...] = jnp.zeros_like(state)

        _, cum = _gla_gates(glr_ref, w2_ref, b_ref)
        for h in range(GH):
            _, last, _, _, _, qt, kt, kh = _gla_head(cum, q_ref, k_ref, h)
            vcols = slice(h * GDV, (h + 1) * GDV)
            st = state[h]
            st_ref[0, h] = st
            v = v_ref[:, vcols].astype(BF16)
            qb = qt.astype(BF16)
            a = _chunk_scores(qt, kt, q_ref, k_ref, h)
            o = _nt(qb, st.astype(BF16)) + _nn(a.astype(BF16), v)
            state[h] = st * jnp.exp(last) + _tn(v, kh.astype(BF16))
            o_ref[:, vcols] = o
            r = lax.rsqrt(jnp.mean(o * o, axis=-1, keepdims=True) + EPS)
            zg = zg_ref[:, vcols]
            bin_ref[:, vcols] = (o * r * gn_ref[...] * zg * _sigmoid(zg)).astype(BF16)

    row = lambda width, cblk: pl.BlockSpec((GLA_C, width), functools.partial(lambda i, c: (i, c), c=cblk))
    full = lambda a: pl.BlockSpec(a.shape, functools.partial(lambda i, nd: (0,) * nd, nd=a.ndim))
    return pl.pallas_call(
        body, name="gla_fwd", grid=(nc,),
        in_specs=[row(512, C_QG // 512), row(512, C_KG // 512), row(1024, C_VG // 1024), row(GLR_W, C_GLR // GLR_W),
                  row(1024, C_ZG // 1024), full(w2p), full(bg), full(gn)],
        out_specs=[pl.BlockSpec((GLA_C, GH * GDV), lambda i: (i, 0)), pl.BlockSpec((GLA_C, GH * GDV), lambda i: (i, 0)),
                   pl.BlockSpec((1, GH, GDV, GDK), lambda i: (i, 0, 0, 0))],
        out_shape=[S((T, GH * GDV), F32), S((T, GH * GDV), BF16), S((nc, GH, GDV, GDK), F32)],
        scratch_shapes=[pltpu.VMEM((GH, GDV, GDK), F32)],
        compiler_params=pltpu.CompilerParams(dimension_semantics=("arbitrary",)),
    )(proj, proj, proj, proj, proj, w2p, bg, gn)


def _gla_bwd(proj, w2p, bg, gn, o_gla, states, dbin, dproj):
    nc = T // GLA_C

    def body(q_ref, k_ref, v_ref, glr_ref, zg_ref, w2_ref, b_ref, gn_ref, o_ref, st_ref, dbin_ref, buf_ref,
             out_ref, dw2_ref, dbg_ref, dgn_ref, dstate, dlogit):
        del buf_ref
        dq_ref = out_ref.at[:, C_QG:C_KG]
        dk_ref = out_ref.at[:, C_KG:C_VG]
        dv_ref = out_ref.at[:, C_VG:C_ZG]
        dzg_ref = out_ref.at[:, C_ZG:C_GLR]
        dglr_ref = out_ref.at[:, C_GLR:C_GLR + GLR_W]
        first = pl.program_id(0) == 0

        @pl.when(first)
        def _():
            dstate[...] = jnp.zeros_like(dstate)

        logit, cum = _gla_gates(glr_ref, w2_ref, b_ref)
        is_last = lax.broadcasted_iota(jnp.int32, (GLA_C, 1), 0) == GLA_C - 1
        dgn = jnp.zeros((1, GDV), F32)
        for h in range(GH):
            _, last, e_pos, e_neg, e_end, qt, kt, kh = _gla_head(cum, q_ref, k_ref, h)
            cols = slice(h * GDK, (h + 1) * GDK)
            vcols = slice(h * GDV, (h + 1) * GDV)
            o = o_ref[:, vcols]
            r = lax.rsqrt(jnp.mean(o * o, axis=-1, keepdims=True) + EPS)
            zg = zg_ref[:, vcols]
            sg = _sigmoid(zg)
            db_ = dbin_ref[:, vcols].astype(F32)
            dlin = db_ * zg * sg
            dzg_ref[:, vcols] = (db_ * (o * r * gn_ref[...]) * sg * (1.0 + zg * (1.0 - sg))).astype(BF16)
            u = dlin * gn_ref[...]
            do = (r * u - o * (r * r * r) * jnp.mean(u * o, axis=-1, keepdims=True)).astype(BF16)
            dgn = dgn + jnp.sum(dlin * o * r, axis=0, keepdims=True)
            st = st_ref[0, h]
            dst = dstate[h]
            v = v_ref[:, vcols].astype(BF16)
            qb, kb, khb = qt.astype(BF16), kt.astype(BF16), kh.astype(BF16)
            dstb = dst.astype(BF16)
            causal = _causal(GLA_C)
            a = _chunk_scores(qt, kt, q_ref, k_ref, h).astype(BF16)
            da = jnp.where(causal, _nt(do, v), 0.0).astype(BF16)
            dqt = _nn(do, st.astype(BF16)) + _nn(da, kb)
            dkt = _tn(da, qb)
            dkh = _nn(v, dstb)
            dv_ref[:, vcols] = (_tn(a, do) + _nt(khb, dstb)).astype(BF16)
            lam = jnp.exp(last)
            dlam = jnp.sum(dst * st, axis=0, keepdims=True)
            dstate[h] = dst * lam + _tn(do, qb)
            dq_ref[:, cols] = (dqt * e_pos * (GDK ** -0.5)).astype(BF16)
            dk_ref[:, cols] = (dkt * e_neg + dkh * e_end).astype(BF16)
            dkh_kh = dkh * kh
            dcum = dqt * qt - dkt * kt - dkh_kh
            dlast = jnp.sum(dkh_kh, axis=0, keepdims=True) + dlam * lam
            dcum = jnp.where(is_last, dcum + dlast, dcum)
            dlg = _tri_sum(dcum, upper=True)
            dlogit[:, cols] = dlg * (1.0 / GLA_TAU) * (1.0 - _sigmoid(logit[:, cols]))

        dl = dlogit[...]
        dlb = dl.astype(BF16)
        dglr_ref[...] = _nt(dlb, w2_ref[...]).astype(BF16)
        dw2 = _tn(glr_ref[...].astype(BF16), dlb)
        dbg = jnp.sum(dl, axis=0, keepdims=True)

        @pl.when(first)
        def _():
            dw2_ref[...] = dw2
            dbg_ref[...] = dbg
            dgn_ref[...] = dgn

        @pl.when(jnp.logical_not(first))
        def _():
            dw2_ref[...] += dw2
            dbg_ref[...] += dbg
            dgn_ref[...] += dgn

    rev = lambda i: nc - 1 - i
    row = lambda width, cblk: pl.BlockSpec((GLA_C, width), functools.partial(lambda i, c: (rev(i), c), c=cblk))
    full = lambda a: pl.BlockSpec(a.shape, functools.partial(lambda i, nd: (0,) * nd, nd=a.ndim))
    keep = lambda shape: pl.BlockSpec(shape, functools.partial(lambda i, nd: (0,) * nd, nd=len(shape)))
    return pl.pallas_call(
        body, name="gla_bwd", grid=(nc,),
        in_specs=[row(512, C_QG // 512), row(512, C_KG // 512), row(1024, C_VG // 1024), row(GLR_W, C_GLR // GLR_W),
                  row(1024, C_ZG // 1024), full(w2p), full(bg), full(gn), row(GH * GDV, 0),
                  pl.BlockSpec((1, GH, GDV, GDK), lambda i: (rev(i), 0, 0, 0)), row(GH * GDV, 0),
                  pl.BlockSpec(memory_space=pl.ANY)],
        out_specs=[row(GLA_GROUP_W, 0), keep((GLR_W, 512)), keep((1, 512)), keep((1, GDV))],
        out_shape=[S(dproj.shape, dproj.dtype), S((GLR_W, 512), F32), S((1, 512), F32), S((1, GDV), F32)],
        input_output_aliases={11: 0},
        scratch_shapes=[pltpu.VMEM((GH, GDV, GDK), F32), pltpu.VMEM((GLA_C, GH * GDK), F32)],
        compiler_params=pltpu.CompilerParams(dimension_semantics=("arbitrary",)),
    )(proj, proj, proj, proj, proj, w2p, bg, gn, o_gla, states, dbin, dproj)


RT = 512


def _rowchain(body, name, ins, outs, scratch=()):
    in_specs, args = [], []
    for spec in ins:
        if spec[0] == "tok":
            _, arr, width, cblk = spec
            in_specs.append(pl.BlockSpec((RT, width), functools.partial(lambda i, c: (i, c), c=cblk)))
        else:
            arr = spec[1]
            in_specs.append(pl.BlockSpec(arr.shape, functools.partial(lambda i, nd: (0,) * nd, nd=arr.ndim)))
        args.append(arr)
    out_specs, out_shape = [], []
    for spec in outs:
        if spec[0] == "tok":
            _, shape, dtype, width, cblk = spec
            out_specs.append(pl.BlockSpec((RT, width), functools.partial(lambda i, c: (i, c), c=cblk)))
        else:
            _, shape, dtype = spec
            out_specs.append(pl.BlockSpec(shape, functools.partial(lambda i, nd: (0,) * nd, nd=len(shape))))
        out_shape.append(S(shape, dtype))
    return pl.pallas_call(
        body, name=name, grid=(T // RT,), in_specs=in_specs, out_specs=out_specs, out_shape=out_shape,
        scratch_shapes=list(scratch), compiler_params=pltpu.CompilerParams(dimension_semantics=("arbitrary",)),
    )(*args)


def _tok(arr, width=None, cblk=0):
    return ("tok", arr, arr.shape[1] if width is None else width, cblk)


def _tok_out(dtype, width=D):
    return ("tok", (T, width), dtype, width, 0)


def _branches_fwd(ain, bin_, proj, x, w_att, w_gla, w_out):
    def body(ain_ref, bin_ref, g_ref, x_ref, wa_ref, wg_ref, wo_ref, ya_ref, yb_ref, y_ref, x1_ref):
        ya = _nn(ain_ref[...], wa_ref[...]).astype(BF16)
        yb = _nn(bin_ref[...], wg_ref[...]).astype(BF16)
        ya_ref[...] = ya
        yb_ref[...] = yb
        y = (_sigmoid(g_ref[:, :D]) * ya.astype(F32) + _sigmoid(g_ref[:, D:]) * yb.astype(F32)).astype(BF16)
        y_ref[...] = y
        x1_ref[...] = x_ref[...] + _nn(y, wo_ref[...])

    return _rowchain(body, "branches_fwd",
                     [_tok(ain), _tok(bin_), _tok(proj, 2 * D, C_GA // (2 * D)), _tok(x), ("all", w_att),
                      ("all", w_gla), ("all", w_out)],
                     [_tok_out(BF16), _tok_out(BF16), _tok_out(BF16), _tok_out(F32)])


def _accumulate(ref, part, first):
    @pl.when(first)
    def _():
        ref[...] = part

    @pl.when(jnp.logical_not(first))
    def _():
        ref[...] += part


def _ple_loss(x1, p, target, g2, w_pg, w_ple):
    def body(x1_ref, p_ref, t_ref, g_ref, wpg_ref, wple_ref, n2_ref, loss_ref, dout_ref, du_ref, dwple_ref, acc):
        first = pl.program_id(0) == 0
        x1 = x1_ref[...]
        r = lax.rsqrt(jnp.mean(x1 * x1, axis=-1, keepdims=True) + EPS)
        n2 = (x1 * r * g_ref[...]).astype(BF16)
        n2_ref[...] = n2
        pg = _sigmoid(_nn(n2, wpg_ref[...]))
        pb = p_ref[...].astype(BF16)
        e_ = _nn(pb, wple_ref[...])
        diff = x1 + e_ * pg - t_ref[...]
        _accumulate(acc, jnp.sum(diff * diff, axis=0, keepdims=True), first)
        dout = diff * (1.0 / D)
        dout_ref[...] = dout
        du_ref[...] = (dout * e_ * pg * (1.0 - pg)).astype(BF16)
        _accumulate(dwple_ref, _tn(pb, (dout * pg).astype(BF16)), first)
        loss_ref[...] = jnp.zeros((1, 128), F32) + jnp.sum(acc[...], axis=-1, keepdims=True) * (0.5 / D)

    return _rowchain(body, "ple_loss", [_tok(x1), _tok(p), _tok(target), ("all", g2), ("all", w_pg), ("all", w_ple)],
                     [_tok_out(BF16), ("acc", (1, 128), F32), _tok_out(F32), _tok_out(BF16), ("acc", (PLE, D), F32)],
                     scratch=[pltpu.VMEM((1, D), F32)])


def _ple_bwd(du, n2, y, x1, dout, g2, w_pg, w_out):
    def body(du_ref, n2_ref, y_ref, x1_ref, dout_ref, g_ref, wpg_ref, wo_ref, dx_ref, dy_ref, dg_ref, dwpg_ref,
             dwo_ref):
        first = pl.program_id(0) == 0
        x1 = x1_ref[...]
        r = lax.rsqrt(jnp.mean(x1 * x1, axis=-1, keepdims=True) + EPS)
        du_ = du_ref[...]
        dn = _nt(du_, wpg_ref[...])
        u = dn * g_ref[...]
        dx = dout_ref[...] + r * u - x1 * (r * r * r) * jnp.mean(u * x1, axis=-1, keepdims=True)
        dxb = dx.astype(BF16)
        dx_ref[...] = dx
        dy_ref[...] = _nt(dxb, wo_ref[...]).astype(BF16)
        _accumulate(dg_ref, jnp.sum(dn * x1 * r, axis=0, keepdims=True), first)
        _accumulate(dwpg_ref, _tn(n2_ref[...], du_), first)
        _accumulate(dwo_ref, _tn(y_ref[...], dxb), first)

    return _rowchain(body, "ple_bwd",
                     [_tok(du), _tok(n2), _tok(y), _tok(x1), _tok(dout), ("all", g2), ("all", w_pg), ("all", w_out)],
                     [_tok_out(F32), _tok_out(BF16), ("acc", (1, D), F32), ("acc", (D, D), F32), ("acc", (D, D), F32)])


def _branches_bwd(dy, ya, yb, ain, bin_, proj, w_att, w_gla):
    def body(dy_ref, ya_ref, yb_ref, ain_ref, bin_ref, g_ref, wa_ref, wg_ref, dg_ref, dain_ref, dbin_ref,
             dwa_ref, dwg_ref):
        first = pl.program_id(0) == 0
        dy_ = dy_ref[...].astype(F32)
        sa, sb = _sigmoid(g_ref[:, :D]), _sigmoid(g_ref[:, D:])
        dg_ref[:, :D] = (dy_ * ya_ref[...].astype(F32) * sa * (1.0 - sa)).astype(BF16)
        dg_ref[:, D:] = (dy_ * yb_ref[...].astype(F32) * sb * (1.0 - sb)).astype(BF16)
        dya = (dy_ * sa).astype(BF16)
        dyb = (dy_ * sb).astype(BF16)
        dain_ref[...] = _nt(dya, wa_ref[...]).astype(BF16)
        dbin_ref[...] = _nt(dyb, wg_ref[...]).astype(BF16)
        _accumulate(dwa_ref, _tn(ain_ref[...], dya), first)
        _accumulate(dwg_ref, _tn(bin_ref[...], dyb), first)

    gates = C_GA // (2 * D)
    return _rowchain(body, "branches_bwd",
                     [_tok(dy), _tok(ya), _tok(yb), _tok(ain), _tok(bin_), _tok(proj, 2 * D, gates), ("all", w_att),
                      ("all", w_gla)],
                     [("tok", (T, NCOL), BF16, 2 * D, gates), _tok_out(BF16, ATT_W), _tok_out(BF16),
                      ("acc", (ATT_W, D), F32), ("acc", (D, D), F32)])


def _peer(k):
    x, y, c = lax.axis_index("x"), lax.axis_index("y"), lax.axis_index("c")
    return (x ^ ((k >> 2) & 1), y ^ ((k >> 1) & 1), c ^ (k & 1))


def _my_index():
    return 4 * lax.axis_index("x") + 2 * lax.axis_index("y") + lax.axis_index("c")


def _peer_index(k):
    px, py, pc = _peer(k)
    return 4 * px + 2 * py + pc


def _pairwise_plan(src_of, dst_of, landed_of, own_src, own_dst):
    def plan(ins, outs, send, recv, local):
        n = len(ins)

        def own():
            return [pltpu.make_async_copy(own_src(ins[a]), own_dst(outs[a]), local.at[a]) for a in range(n)]

        def remote(k, a, src, dst):
            return pltpu.make_async_remote_copy(src_ref=src, dst_ref=dst, send_sem=send.at[k - 1, a],
                                                recv_sem=recv.at[k - 1, a], device_id=_peer(k), device_id_type=MESH)

        def sent():
            return [remote(k, a, src_of(ins[a], k), dst_of(outs[a])) for k in range(1, NDEV) for a in range(n)]

        def start():
            for cp in own() + sent():
                cp.start()

        def finish():
            for k in range(1, NDEV):
                for a in range(n):
                    remote(k, a, own_src(ins[a]), landed_of(outs[a], k)).wait_recv()
            for cp in sent():
                cp.wait_send()
            for cp in own():
                cp.wait()

        return start, finish

    return plan


def _pairwise_sems(n):
    return [pltpu.SemaphoreType.DMA((NDEV - 1, n)), pltpu.SemaphoreType.DMA((NDEV - 1, n)),
            pltpu.SemaphoreType.DMA((n,))]


def _gather_side(arrs):
    plan = _pairwise_plan(src_of=lambda i, k: i, dst_of=lambda o: o.at[_my_index()],
                          landed_of=lambda o, k: o.at[_peer_index(k)],
                          own_src=lambda i: i, own_dst=lambda o: o.at[_my_index()])
    return dict(arrs=arrs, out_shape=[S((NDEV,) + a.shape, a.dtype) for a in arrs],
                scratch=_pairwise_sems(len(arrs)), plan=plan)


def _exchange_side(arrs):
    plan = _pairwise_plan(src_of=lambda i, k: i.at[_peer_index(k)], dst_of=lambda o: o.at[_my_index()],
                          landed_of=lambda o, k: o.at[_peer_index(k)],
                          own_src=lambda i: i.at[_my_index()], own_dst=lambda o: o.at[_my_index()])
    return dict(arrs=arrs, out_shape=[S(a.shape, a.dtype) for a in arrs], scratch=_pairwise_sems(len(arrs)), plan=plan)


def _comm_call(side, name):
    n = len(side["arrs"])

    def body(*refs):
        start, finish = side["plan"](refs[:n], refs[n:2 * n], *refs[2 * n:])
        start()
        finish()

    hbm = pl.BlockSpec(memory_space=pl.ANY)
    return pl.pallas_call(body, name=name, in_specs=[hbm] * n, out_specs=[hbm] * n, out_shape=side["out_shape"],
                          scratch_shapes=side["scratch"])(*side["arrs"])


def _all_gather_by_chip(arrs, name):
    n = len(arrs)

    def body(*refs):
        ins, outs = refs[:n], refs[n:2 * n]
        send, recv, local = refs[2 * n:]
        x, y, c = lax.axis_index("x"), lax.axis_index("y"), lax.axis_index("c")
        me, sibling = (x, y, c), (x, y, 1 - c)
        chips = [(1 - x, y), (x, 1 - y), (1 - x, 1 - y)]

        def copy(k, a, block, to, src=None):
            px, py, pc = block
            slot = outs[a].at[4 * px + 2 * py + pc]
            return pltpu.make_async_remote_copy(
                src_ref=slot if src is None else src, dst_ref=slot, send_sem=send.at[k, a], recv_sem=recv.at[k, a],
                device_id=to, device_id_type=MESH)

        north = c == 1
        via = (jnp.where(north, 1 - x, x), jnp.where(north, y, 1 - y))
        onward = (jnp.where(north, x, 1 - x), jnp.where(north, 1 - y, y), c)
        mine = [pltpu.make_async_copy(ins[a], outs[a].at[4 * x + 2 * y + c], local.at[a]) for a in range(n)]
        first = []
        for a in range(n):
            first.append(copy(0, a, me, sibling, src=ins[a]))
            first += [copy(1 + j, a, me, (*chips[j], c), src=ins[a]) for j in range(2)]
        for cp in mine + first:
            cp.start()
        passed = []
        for j in range(2):
            for a in range(n):
                copy(1 + j, a, (*chips[j], c), me).wait_recv()
                passed.append(copy(4 + j, a, (*chips[j], c), sibling))
                passed[-1].start()
        for a in range(n):
            passed.append(copy(3, a, (*via, c), onward))
            passed[-1].start()
        for a in range(n):
            copy(3, a, (*chips[2], c), me).wait_recv()
            passed.append(copy(6, a, (*chips[2], c), sibling))
            passed[-1].start()
        for a in range(n):
            copy(0, a, sibling, me).wait_recv()
        for j, chip in enumerate(chips):
            for a in range(n):
                copy(4 + j, a, (*chip, 1 - c), me).wait_recv()
        for cp in first + passed:
            cp.wait_send()
        for cp in mine:
            cp.wait()

    hbm = pl.BlockSpec(memory_space=pl.ANY)
    return pl.pallas_call(
        body, name=name, in_specs=[hbm] * n, out_specs=[hbm] * n,
        out_shape=[S((NDEV,) + a.shape, a.dtype) for a in arrs],
        scratch_shapes=[pltpu.SemaphoreType.DMA((NDEV - 1, n)), pltpu.SemaphoreType.DMA((NDEV - 1, n)),
                        pltpu.SemaphoreType.DMA((n,))],
    )(*arrs)


NCHIP = 4


def _exchange_sibling(arrs, name):
    n = len(arrs)

    def body(*refs):
        ins, outs = refs[:n], refs[n:2 * n]
        send, recv = refs[2 * n:]
        x, y, c = lax.axis_index("x"), lax.axis_index("y"), lax.axis_index("c")
        copies = []
        for q in range(NCHIP):
            for a in range(n):
                copies.append(pltpu.make_async_remote_copy(
                    src_ref=ins[a].at[2 * q + (1 - c)], dst_ref=outs[a].at[q], send_sem=send.at[q, a],
                    recv_sem=recv.at[q, a], device_id=(x, y, 1 - c), device_id_type=MESH))
        for cp in copies:
            cp.start()
        for cp in copies:
            cp.wait_recv()
        for cp in copies:
            cp.wait_send()

    hbm = pl.BlockSpec(memory_space=pl.ANY)
    return pl.pallas_call(
        body, name=name, in_specs=[hbm] * n, out_specs=[hbm] * n,
        out_shape=[S((NCHIP,) + a.shape[1:], a.dtype) for a in arrs],
        scratch_shapes=[pltpu.SemaphoreType.DMA((NCHIP, n)), pltpu.SemaphoreType.DMA((NCHIP, n))],
    )(*arrs)


def _pair_add(mine, got, core, name):
    _, rows, cols = mine.shape
    tc = 256
    assert cols % tc == 0

    def body(core_ref, a_ref, b_ref, o_ref):
        o_ref[...] = (a_ref[...].astype(F32) + b_ref[...].astype(F32)).astype(BF16)

    return pl.pallas_call(
        body, name=name,
        grid_spec=pltpu.PrefetchScalarGridSpec(
            num_scalar_prefetch=1, grid=(NCHIP, cols // tc),
            in_specs=[pl.BlockSpec((None, rows, tc), lambda q, i, core_ref: (2 * q + core_ref[0], 0, i)),
                      pl.BlockSpec((None, rows, tc), lambda q, i, core_ref: (q, 0, i))],
            out_specs=pl.BlockSpec((None, rows, tc), lambda q, i, core_ref: (q, 0, i))),
        out_shape=S((NCHIP, rows, cols), BF16),
    )(core, mine, got)


def _chips_side(arrs):
    def plan(ins, outs, send, recv, local):
        n = len(ins)

        def places():
            x, y, c = lax.axis_index("x"), lax.axis_index("y"), lax.axis_index("c")
            return 2 * x + y, c, [(1 - x, y), (x, 1 - y), (1 - x, 1 - y)]

        def own():
            here, _, _ = places()
            return [pltpu.make_async_copy(ins[a].at[here], outs[a].at[here], local.at[a]) for a in range(n)]

        def remote(j, a, src_slot, dst_slot):
            _, c, chips = places()
            cx, cy = chips[j]
            return pltpu.make_async_remote_copy(
                src_ref=ins[a].at[src_slot], dst_ref=outs[a].at[dst_slot], send_sem=send.at[j, a],
                recv_sem=recv.at[j, a], device_id=(cx, cy, c), device_id_type=MESH)

        def sent():
            here, _, chips = places()
            return [remote(j, a, 2 * cx + cy, here) for j, (cx, cy) in enumerate(chips) for a in range(n)]

        def start():
            for cp in own() + sent():
                cp.start()

        def finish():
            here, _, chips = places()
            for j, (cx, cy) in enumerate(chips):
                for a in range(n):
                    remote(j, a, here, 2 * cx + cy).wait_recv()
            for cp in sent():
                cp.wait_send()
            for cp in own():
                cp.wait()

        return start, finish

    n = len(arrs)
    return dict(arrs=arrs, out_shape=[S(a.shape, a.dtype) for a in arrs],
                scratch=[pltpu.SemaphoreType.DMA((NCHIP - 1, n)), pltpu.SemaphoreType.DMA((NCHIP - 1, n)),
                         pltpu.SemaphoreType.DMA((n,))], plan=plan)


def _adamw_shards(parts, places):
    n_src = len(parts)

    def body(*refs):
        srcs, rest = refs[:n_src], refs[n_src:]
        for j, (src, rows, cols, _) in enumerate(places):
            w_ref, m_ref, v_ref = rest[3 * j:3 * j + 3]
            outs = rest[3 * len(places) + 4 * j:3 * len(places) + 4 * j + 4]
            p_ref = srcs[src]
            g = p_ref[0, rows, cols].astype(F32)
            for s in range(1, p_ref.shape[0]):
                g = g + p_ref[s, rows, cols].astype(F32)
            delta, m_new, v_new = _adam_math(g, w_ref[0], m_ref[0], v_ref[0])
            for ref, val in zip(outs, (g, delta, m_new, v_new)):
                ref[0] = val

    flat = [a for place in places for a in place[3]]
    return pl.pallas_call(
        body, name="adam_shards",
        out_shape=[S(place[3][0].shape, F32) for place in places for _ in range(4)],
    )(*parts, *flat)


def _adam_math(g, w, m, v):
    c1 = 1.0 - ADAM_B1 ** ADAM_STEP
    c2 = 1.0 - ADAM_B2 ** ADAM_STEP
    m_new = ADAM_B1 * m + (1.0 - ADAM_B1) * g
    v_new = ADAM_B2 * v + (1.0 - ADAM_B2) * (g * g)
    return -ADAM_LR * ((m_new / c1) / (jnp.sqrt(v_new / c2) + ADAM_EPS) + ADAM_WD * w), m_new, v_new


def _adamw_small(parts, params, loss_parts):
    n = len(params)

    def body(*refs):
        p_refs, rest = refs[:n], refs[n + 1:]
        total = refs[n][0]
        for s in range(1, NDEV):
            total = total + refs[n][s]
        refs[-1][...] = total
        for j in range(n):
            w_ref, m_ref, v_ref = rest[3 * j:3 * j + 3]
            g_ref, d_ref, mo_ref, vo_ref = rest[3 * n + 4 * j:3 * n + 4 * j + 4]
            width = w_ref.shape[1]
            g = p_refs[j][0]
            for s in range(1, NDEV):
                g = g + p_refs[j][s]
            g = g[:, :width]
            delta, m_new, v_new = _adam_math(g, w_ref[...], m_ref[...], v_ref[...])
            g_ref[...] = g
            d_ref[...] = delta
            mo_ref[...] = m_new
            vo_ref[...] = v_new

    flat = [a for group in params for a in group]
    return pl.pallas_call(
        body, name="adam_small",
        out_shape=[S(group[0].shape, F32) for group in params for _ in range(4)] + [S((1, 128), F32)],
    )(*parts, loss_parts, *flat)


def _adamw_rows(parts, w, m, v, name, tc=128):
    rows, _, cols = w.shape
    nparts = parts.shape[0]
    nsteps = cols // tc

    def body(p_ref, w_hbm, m_hbm, v_hbm, g_hbm, d_hbm, mo_hbm, vo_hbm, inbuf, outbuf, insem, outsem):
        i = pl.program_id(0)
        slot = i & 1

        def view(ref, step):
            return ref.at[:, 0, pl.ds(pl.multiple_of(step * tc, tc), tc)]

        def fetch(step, sl):
            return [pltpu.make_async_copy(view(src, step), inbuf.at[sl, k], insem.at[sl, k])
                    for k, src in enumerate((w_hbm, m_hbm, v_hbm))]

        def write(step, sl):
            return [pltpu.make_async_copy(outbuf.at[sl, k], view(dst, step), outsem.at[sl, k])
                    for k, dst in enumerate((g_hbm, d_hbm, mo_hbm, vo_hbm))]

        @pl.when(i == 0)
        def _():
            for cp in fetch(0, 0):
                cp.start()

        @pl.when(i + 1 < nsteps)
        def _():
            for cp in fetch(i + 1, 1 - slot):
                cp.start()

        for cp in fetch(i, slot):
            cp.wait()

        @pl.when(i >= 2)
        def _():
            for cp in write(i - 2, slot):
                cp.wait()

        g = p_ref[0].astype(F32)
        for s in range(1, nparts):
            g = g + p_ref[s].astype(F32)
        g = g[:rows]
        delta, m_new, v_new = _adam_math(g, inbuf[slot, 0], inbuf[slot, 1], inbuf[slot, 2])
        for k, val in enumerate((g, delta, m_new, v_new)):
            outbuf[slot, k] = val
        for cp in write(i, slot):
            cp.start()

        @pl.when(i == nsteps - 1)
        def _():
            for cp in write(i - 1, 1 - slot) + write(i, slot):
                cp.wait()

    hbm = pl.BlockSpec(memory_space=pl.ANY)
    assert nsteps >= 2
    return pl.pallas_call(
        body, name=name, grid=(nsteps,),
        in_specs=[pl.BlockSpec((nparts, parts.shape[1], tc), lambda i: (0, 0, i)), hbm, hbm, hbm],
        out_specs=[hbm] * 4, out_shape=[S((rows, 1, cols), F32)] * 4,
        scratch_shapes=[pltpu.VMEM((2, 3, rows, tc), F32), pltpu.VMEM((2, 4, rows, tc), F32),
                        pltpu.SemaphoreType.DMA((2, 3)), pltpu.SemaphoreType.DMA((2, 4))],
        compiler_params=pltpu.CompilerParams(dimension_semantics=("arbitrary",)),
    )(parts, w, m, v)


def _to_aligned(wt):
    pad = jnp.zeros((GLR_W - GLR_N, wt.shape[1]), wt.dtype)
    return jnp.concatenate([wt[O_QG:O_GLR], wt[O_ZG:O_GA], wt[O_GLR:O_ZG], pad, wt[O_ZA:O_QG], wt[O_GA:O_END],
                            wt[O_QA:O_ZA]], axis=0)


def _from_aligned(wt):
    return jnp.concatenate([wt[C_QA:], wt[C_ZA:C_GA], wt[C_QG:C_ZG], wt[C_GLR:C_GLR + GLR_N], wt[C_ZG:C_GLR],
                            wt[C_GA:C_QA]], axis=0)


SLAB = 1296
REMAP_RUNS = 3
_PIECES = ((O_QA, O_ZA, C_QA), (O_ZA, O_QG, C_ZA), (O_QG, O_GLR, C_QG), (O_GLR, O_ZG, C_GLR), (O_ZG, O_GA, C_ZG),
           (O_GA, O_END, C_GA))


def _slab_row_of_aligned(a):
    for o0, o1, a0 in _PIECES:
        if a0 <= a < a0 + o1 - o0:
            c = o0 + a - a0
            return SLAB * (c // W_IN_SHARD) + c % W_IN_SHARD
    return -1


def _aligned_row_of_slab(r):
    d, l = divmod(r, SLAB)
    if l >= W_IN_SHARD:
        return -1
    c = d * W_IN_SHARD + l
    for o0, o1, a0 in _PIECES:
        if o0 <= c < o1:
            return a0 + c - o0
    raise AssertionError(c)


def _remap_table(row_of, n_out, block, n_src):
    win = block + 16
    table = []
    for b in range(n_out // block):
        runs = []
        for i in range(block):
            s = row_of(b * block + i)
            if s < 0:
                continue
            if runs and runs[-1][0] + runs[-1][2] == s and runs[-1][1] + runs[-1][2] == i:
                runs[-1][2] += 1
            else:
                runs.append([s, i, 1])
        assert len(runs) <= REMAP_RUNS, (b, runs)
        row = []
        for s, i, n in runs:
            w = min(s // 16 * 16, n_src - win)
            assert 0 <= s - w and s - w + n <= win
            row += [w, s - w, i, n]
        table.append(row + [0] * (4 * REMAP_RUNS - len(row)))
    return table


def _remap_rows(src, row_of, n_out, block, name):
    n_src, cols = src.shape
    nb, win = n_out // block, block + 16
    table = jnp.asarray(_remap_table(row_of, n_out, block, n_src), jnp.int32)

    def body(t_ref, src_hbm, o_ref, buf, acc, sem):
        b = pl.program_id(0)
        slot = b & 1

        def each_run(step, sl, act):
            for k in range(REMAP_RUNS):
                @pl.when(t_ref[step, 4 * k + 3] > 0)
                def _():
                    start = pl.multiple_of(t_ref[step, 4 * k], 16)
                    act(pltpu.make_async_copy(src_hbm.at[pl.ds(start, win)], buf.at[sl, k], sem.at[sl, k]))

        @pl.when(b == 0)
        def _():
            each_run(0, 0, lambda cp: cp.start())

        @pl.when(b + 1 < nb)
        def _():
            each_run(b + 1, 1 - slot, lambda cp: cp.start())

        each_run(b, slot, lambda cp: cp.wait())
        acc[...] = jnp.zeros_like(acc)
        row = lax.broadcasted_iota(jnp.int32, (block, win), 0)
        col = lax.broadcasted_iota(jnp.int32, (block, win), 1)
        for k in range(REMAP_RUNS):
            shift, first, count = (t_ref[b, 4 * k + j] for j in (1, 2, 3))

            @pl.when(count > 0)
            def _():
                pick = (col == row - first + shift) & (row >= first) & (row < first + count)
                acc[...] += _nn(jnp.where(pick, 1.0, 0.0).astype(BF16), buf[slot, k])

        o_ref[...] = acc[...].astype(o_ref.dtype)

    return pl.pallas_call(
        body, name=name,
        grid_spec=pltpu.PrefetchScalarGridSpec(
            num_scalar_prefetch=1, grid=(nb,), in_specs=[pl.BlockSpec(memory_space=pl.ANY)],
            out_specs=pl.BlockSpec((block, cols), lambda b, t: (b, 0)),
            scratch_shapes=[pltpu.VMEM((2, REMAP_RUNS, win, cols), src.dtype), pltpu.VMEM((block, cols), F32),
                            pltpu.SemaphoreType.DMA((2, REMAP_RUNS))]),
        out_shape=S((n_out, cols), src.dtype),
        compiler_params=pltpu.CompilerParams(dimension_semantics=("arbitrary",)),
    )(table, src)


def _col_blocks(w, width):
    return w.reshape(w.shape[0], NDEV, width).transpose(1, 0, 2)


def _from_col_blocks(w):
    return w.transpose(1, 0, 2).reshape(w.shape[1], NDEV * w.shape[2])


def _local_step(x2, p2, pos, tgt, norm_g, qk_norm_q, qk_norm_k, gla_gate_b, gla_norm_g, ple_norm_g, w_al,
                weights=None, proj_side=None, unpack=None, dw_side_of=None, dh_side_of=None):
    half = ROT_DIM // 2
    inv8 = jnp.power(jnp.float32(ROPE_THETA), -jnp.arange(half, dtype=F32) * 2.0 / ROT_DIM)
    inv = jnp.tile(jnp.concatenate([inv8, inv8, jnp.zeros((HD - ROT_DIM,), F32)]), 2).reshape(1, 128)
    gq = jnp.tile(qk_norm_q, (1, 2))
    gk = jnp.tile(qk_norm_k, (1, 2))

    proj, h, got = _proj_rms(x2, norm_g, w_al, proj_side)
    if proj_side is not None:
        weights = unpack(got)
    w2p, w_att_f, w_gla_f, w_out_f, w_pg_f, w_ple_f = weights
    qkv = _qk_prep(proj, pos, inv, gq, gk)
    fwd = [_att_fwd(qkv[g], qkv[3 + g], qkv[6 + g], g, f"att_fwd{g}") for g in range(3)]
    att, lse, ain = _att_merge([f[0] for f in fwd], [f[1] for f in fwd], proj)
    o_gla, bin_, states = _gla_fwd(proj, w2p, gla_gate_b, gla_norm_g)
    ya, yb, y, x1 = _branches_fwd(ain, bin_, proj, x2, w_att_f, w_gla_f, w_out_f)
    n2, loss_v, dout, du, dw_ple = _ple_loss(x1, p2, tgt, ple_norm_g, w_pg_f, w_ple_f)

    dx1, dy, dg_ple, dw_pg, dw_out = _ple_bwd(du, n2, y, x1, dout, ple_norm_g, w_pg_f, w_out_f)
    dproj, dain, dbin, dw_att, dw_gla = _branches_bwd(dy, ya, yb, ain, bin_, proj, w_att_f, w_gla_f)
    dproj, da0, da1, da2, at1, at2, ls1, ls2 = _att_gate_bwd(dain, att, lse, proj, dproj)
    datts, atts, lses = (da0, da1, da2), (att[None], at1, at2), (lse[None], ls1, ls2)
    dproj, dw2, dbg, dgn = _gla_bwd(proj, w2p, gla_gate_b, gla_norm_g, o_gla, states, dbin, dproj)
    bwd = [_att_bwd(qkv[g], qkv[3 + g], qkv[6 + g], datts[g], atts[g], lses[g], g, f"att_bwd{g}") for g in range(3)]
    dproj, dgq, dgk = _qk_bwd(proj, pos, inv, gq, gk, [b[0] for b in bwd], [b[1] for b in bwd],
                              [b[2] for b in bwd], dproj)
    out = dict(loss=loss_v, dw2=dw2, dw_att=dw_att, dw_gla=dw_gla, dw_out=dw_out, dw_pg=dw_pg, dw_ple=dw_ple,
               dgq=dgq, dgk=dgk, dbg=dbg, dgn=dgn, dg_ple=dg_ple)
    if dw_side_of is None:
        dw_al = _mm(dproj, h, mode="tn", name="dw_in", tm=1536, tn=D, tk=T, out_dtype=BF16)
    else:
        dw_al, out["dw_side"] = _mm(dproj, h, mode="tn", name="dw_in", tm=1536, tn=D, tk=T, out_dtype=BF16,
                                    side=dw_side_of(out))
    grad_x, dg_norm, out["dh_side"] = _dh_rms(dproj, w_al, x2, norm_g, dx1,
                                              None if dh_side_of is None else dh_side_of(dw_al))
    out.update(grad_x=grad_x, dw_al=dw_al, dg_norm=dg_norm)
    return out


def kernel(x, p, positions, norm_g, w_in, qk_norm_q, qk_norm_k, gla_gate_w2, gla_gate_b, gla_norm_g, w_att_proj, w_gla_proj, w_out, ple_norm_g, w_ple_gate, w_ple, loss_target, m_norm_g, m_w_in, m_qk_norm_q, m_qk_norm_k, m_gla_gate_w2, m_gla_gate_b, m_gla_norm_g, m_w_att_proj, m_w_gla_proj, m_w_out, m_ple_norm_g, m_w_ple_gate, m_w_ple, v_norm_g, v_w_in, v_qk_norm_q, v_qk_norm_k, v_gla_gate_w2, v_gla_gate_b, v_gla_norm_g, v_w_att_proj, v_w_gla_proj, v_w_out, v_ple_norm_g, v_w_ple_gate, v_w_ple):
    x2, p2, tgt = x[0], p[0, 0], loss_target[0]
    pos = positions.astype(F32).reshape(T, 1)

    rows3 = jnp.stack([w_gla_proj[0], w_out[0], w_ple_gate[0]]).astype(BF16)
    cols3 = jnp.concatenate([w_att_proj[0], w_ple[0], jnp.pad(gla_gate_w2[0], ((0, 0), (0, 64)))], axis=0).astype(BF16)
    mine = jnp.pad(w_in[0].T.astype(BF16), ((0, SLAB - W_IN_SHARD), (0, 0)))
    (g_in,) = _all_gather_by_chip([mine], "gather_w_in")
    w_al = _remap_rows(g_in.reshape(NDEV * SLAB, D), _slab_row_of_aligned, NCOL, 256, "align_w_in")

    def unpack(got):
        g_rows, g_cols = got
        w2_f = _from_col_blocks(g_cols[:, 768:784, :64])
        return (jnp.pad(w2_f, ((0, GLR_W - GLR_N), (0, 0))), _from_col_blocks(g_cols[:, :512]),
                g_rows[:, 0].reshape(D, D), g_rows[:, 1].reshape(D, D), g_rows[:, 2].reshape(D, D),
                _from_col_blocks(g_cols[:, 512:768]))

    def dw_side_of(g):
        s_rows = jnp.concatenate([g[k].reshape(NDEV, 128, D) for k in ("dw_gla", "dw_out", "dw_pg")], axis=1)
        s_cols = jnp.concatenate([_col_blocks(g["dw_att"], 128), _col_blocks(g["dw_ple"], 128),
                                  jnp.pad(_col_blocks(g["dw2"][:GLR_N], 64), ((0, 0), (0, 0), (0, 64)))], axis=1)
        return _exchange_side([s_rows.astype(BF16), s_cols.astype(BF16)])

    def dh_side_of(dw_al):
        s_in = _remap_rows(dw_al, _aligned_row_of_slab, NDEV * SLAB, 432, "shard_dw_in").reshape(NDEV, SLAB, D)
        (from_sibling,) = _exchange_sibling([s_in], "exchange_sibling")
        core = lax.axis_index("c").astype(jnp.int32).reshape(1)
        return _chips_side([_pair_add(s_in, from_sibling, core, "pair_add")])

    loc = _local_step(x2, p2, pos, tgt, norm_g, qk_norm_q, qk_norm_k, gla_gate_b, gla_norm_g, ple_norm_g, w_al,
                      proj_side=_gather_side([rows3, cols3]), unpack=unpack, dw_side_of=dw_side_of,
                      dh_side_of=dh_side_of)
    loss_v, grad_x = loc["loss"], loc["grad_x"]
    dg_norm, dgq, dgk, dbg, dgn, dg_ple = (loc[k] for k in ("dg_norm", "dgq", "dgk", "dbg", "dgn", "dg_ple"))
    r_rows, r_cols = loc["dw_side"]
    (r_in,) = loc["dh_side"]

    r_small = _comm_call(_gather_side([dg_norm, dgq, dgk, dbg, dgn, dg_ple, loss_v]), "gather_small")

    outs = {}

    rows_of = lambda a: jnp.transpose(a, (2, 0, 1))
    outs["w_in"] = [jnp.transpose(o, (1, 2, 0))[0] for o in
                    _adamw_rows(r_in, rows_of(w_in), rows_of(m_w_in), rows_of(v_w_in), "adam_w_in")]
    places = (("w_gla_proj", 0, slice(0, 128), slice(None), (w_gla_proj, m_w_gla_proj, v_w_gla_proj)),
              ("w_out", 0, slice(128, 256), slice(None), (w_out, m_w_out, v_w_out)),
              ("w_ple_gate", 0, slice(256, 384), slice(None), (w_ple_gate, m_w_ple_gate, v_w_ple_gate)),
              ("w_att_proj", 1, slice(0, 512), slice(None), (w_att_proj, m_w_att_proj, v_w_att_proj)),
              ("w_ple", 1, slice(512, 768), slice(None), (w_ple, m_w_ple, v_w_ple)),
              ("gla_gate_w2", 1, slice(768, 784), slice(0, 64), (gla_gate_w2, m_gla_gate_w2, v_gla_gate_w2)))
    res = _adamw_shards([r_rows, r_cols], [place[1:] for place in places])
    for j, place in enumerate(places):
        outs[place[0]] = [o[0] for o in res[4 * j:4 * j + 4]]
    small = ((norm_g, m_norm_g, v_norm_g), (qk_norm_q, m_qk_norm_q, v_qk_norm_q), (qk_norm_k, m_qk_norm_k, v_qk_norm_k),
             (gla_gate_b, m_gla_gate_b, v_gla_gate_b), (gla_norm_g, m_gla_norm_g, v_gla_norm_g),
             (ple_norm_g, m_ple_norm_g, v_ple_norm_g))
    sm = _adamw_small(r_small[:6], small, r_small[6])
    for j, nm in enumerate(("norm_g", "qk_norm_q", "qk_norm_k", "gla_gate_b", "gla_norm_g", "ple_norm_g")):
        outs[nm] = [o[0] for o in sm[4 * j:4 * j + 4]]

    loss = sm[-1][0, 0]
    order = ["norm_g", "w_in", "qk_norm_q", "qk_norm_k", "gla_gate_w2", "gla_gate_b", "gla_norm_g", "w_att_proj",
             "w_gla_proj", "w_out", "ple_norm_g", "w_ple_gate", "w_ple"]
    result = [loss, grad_x[None]]
    for i in range(4):
        result += [outs[nm][i][None] for nm in order]
    return tuple(result)
```

```python
import functools

import jax
import jax.numpy as jnp
from jax import lax
from jax.experimental import pallas as pl
from jax.experimental.pallas import tpu as pltpu

F32 = jnp.float32
BF16 = jnp.bfloat16
S = jax.ShapeDtypeStruct

T = 4096
D = 1024
NDEV = 8
HD = 64
ATT_W = 512
ATT_QKV = 1536
DILATIONS = (1, 4, 16)
BLK = 128
GH, GDK, GDV = 4, 128, 256
GLA_C = 128
PLE = 256
EPS = 1e-6
ROT_DIM = 16
ROPE_THETA = 500000.0
GLA_TAU = 16.0
W_IN_SHARD = 1282

C_QG, C_KG, C_VG, C_ZG, C_GLR, C_ZA, C_GA, C_GB, C_QA, C_KA, C_VA = (
    0, 512, 1024, 2048, 3072, 3584, 4096, 5120, 6144, 7680, 9216)
GLA_GROUP_W = 3584
GLR_W = 512
NCOL = 10752
GLR_N = 16
O_QA, O_ZA, O_QG, O_GLR, O_ZG, O_GA, O_END = 0, 4608, 5120, 7168, 7184, 8208, 10256

ADAM_LR, ADAM_B1, ADAM_B2, ADAM_EPS, ADAM_WD, ADAM_STEP = 0.001, 0.9, 0.999, 1e-08, 0.01, 10

MESH = pl.DeviceIdType.MESH


def _sigmoid(z):
    return 1.0 / (1.0 + jnp.exp(-z))


def _dot(a, b, dims):
    return lax.dot_general(a, b, (dims, ((), ())), preferred_element_type=F32)


def _nn(a, b):
    return _dot(a, b, ((1,), (0,)))


def _nt(a, b):
    return _dot(a, b, ((1,), (1,)))


def _tn(a, b):
    return _dot(a, b, ((0,), (0,)))


def _mm(a, b, *, mode, name, tm, tn, tk, out_dtype=F32, res=None, side=None):
    if mode == "nn":
        (m, k), n = a.shape, b.shape[1]
        a_spec = pl.BlockSpec((tm, tk), lambda i, j, l: (i, l))
        b_spec = pl.BlockSpec((tk, tn), lambda i, j, l: (l, j))
        dot = _nn
    elif mode == "nt":
        (m, k), n = a.shape, b.shape[0]
        a_spec = pl.BlockSpec((tm, tk), lambda i, j, l: (i, l))
        b_spec = pl.BlockSpec((tn, tk), lambda i, j, l: (j, l))
        dot = _nt
    else:
        (k, m), n = a.shape, b.shape[1]
        a_spec = pl.BlockSpec((tk, tm), lambda i, j, l: (l, i))
        b_spec = pl.BlockSpec((tk, tn), lambda i, j, l: (l, j))
        dot = _tn
    assert m % tm == 0 and n % tn == 0 and k % tk == 0, (name, m, n, k)
    grid = (m // tm, n // tn, k // tk)
    nk = grid[2]
    o_spec = pl.BlockSpec((tm, tn), lambda i, j, l: (i, j))
    in_specs = [a_spec, b_spec]
    args = [a, b]
    if res is not None:
        in_specs.append(o_spec)
        args.append(res)
    n_in = len(args)
    n_side = 0 if side is None else len(side["arrs"])
    hbm = pl.BlockSpec(memory_space=pl.ANY)

    def body(*refs):
        a_ref, b_ref = refs[0], refs[1]
        r_ref = refs[2] if res is not None else None
        o_ref = refs[n_in + n_side]
        scratch = refs[n_in + 2 * n_side + 1:]
        if side is not None:
            start, finish_side = side["plan"](refs[n_in:n_in + n_side], refs[n_in + n_side + 1:n_in + 2 * n_side + 1],
                                              *scratch[1 if nk > 1 else 0:])
            ids = [pl.program_id(d) for d in range(3)]

            @pl.when((ids[0] == 0) & (ids[1] == 0) & (ids[2] == 0))
            def _():
                start()

        part = dot(a_ref[...].astype(BF16), b_ref[...].astype(BF16))

        def finish(val):
            if r_ref is not None:
                val = val + r_ref[...]
            o_ref[...] = val.astype(out_dtype)

        if nk == 1:
            finish(part)
        else:
            acc = scratch[0]
            l = pl.program_id(2)

            @pl.when(l == 0)
            def _():
                acc[...] = part

            @pl.when(l > 0)
            def _():
                acc[...] += part

            @pl.when(l == nk - 1)
            def _():
                finish(acc[...])

        if side is not None:
            @pl.when((ids[0] == grid[0] - 1) & (ids[1] == grid[1] - 1) & (ids[2] == grid[2] - 1))
            def _():
                finish_side()

    sems = [] if side is None else side["scratch"]
    outs = pl.pallas_call(
        body, name=name, grid=grid,
        in_specs=in_specs + [hbm] * n_side, out_specs=[o_spec] + [hbm] * n_side,
        out_shape=[S((m, n), out_dtype)] + ([] if side is None else side["out_shape"]),
        scratch_shapes=([pltpu.VMEM((tm, tn), F32)] if nk > 1 else []) + sems,
        compiler_params=pltpu.CompilerParams(
            dimension_semantics=("arbitrary",) * 3 if side is not None else ("parallel", "parallel", "arbitrary")),
    )(*args, *([] if side is None else side["arrs"]))
    return outs[0] if side is None else (outs[0], outs[1:])


def _side_parts(side, refs, n_in, n_out):
    n_side = 0 if side is None else len(side["arrs"])
    scratch = refs[n_in + n_out + 2 * n_side:]
    if side is None:
        return (lambda: None), (lambda: None), scratch
    start, finish = side["plan"](refs[n_in:n_in + n_side], refs[n_in + n_side + n_out:n_in + n_out + 2 * n_side],
                                 *scratch[len(scratch) - len(side["scratch"]):])
    return start, finish, scratch


def _proj_rms(x, g, wt, side=None):
    tm, tn = 1024, 1536
    grid = (T // tm, NCOL // tn)
    n_side = 0 if side is None else len(side["arrs"])
    hbm = pl.BlockSpec(memory_space=pl.ANY)

    def body(*refs):
        x_ref, g_ref, w_ref = refs[:3]
        o_ref, h_ref = refs[3 + n_side], refs[4 + n_side]
        start, finish, _ = _side_parts(side, refs, 3, 2)
        i, j = pl.program_id(0), pl.program_id(1)

        @pl.when((i == 0) & (j == 0))
        def _():
            start()

        @pl.when(j == 0)
        def _():
            xf = x_ref[...]
            r = lax.rsqrt(jnp.mean(xf * xf, axis=-1, keepdims=True) + EPS)
            h_ref[...] = (xf * r * g_ref[...]).astype(BF16)

        o_ref[...] = _nt(h_ref[...], w_ref[...])

        @pl.when((i == grid[0] - 1) & (j == grid[1] - 1))
        def _():
            finish()

    outs = pl.pallas_call(
        body, name="proj", grid=grid,
        in_specs=[pl.BlockSpec((tm, D), lambda i, j: (i, 0)), pl.BlockSpec((1, D), lambda i, j: (0, 0)),
                  pl.BlockSpec((tn, D), lambda i, j: (j, 0))] + [hbm] * n_side,
        out_specs=[pl.BlockSpec((tm, tn), lambda i, j: (i, j)), pl.BlockSpec((tm, D), lambda i, j: (i, 0))] + [hbm] * n_side,
        out_shape=[S((T, NCOL), F32), S((T, D), BF16)] + ([] if side is None else side["out_shape"]),
        scratch_shapes=[] if side is None else side["scratch"],
        compiler_params=pltpu.CompilerParams(dimension_semantics=("arbitrary", "arbitrary")),
    )(x, g, wt, *([] if side is None else side["arrs"]))
    return outs[0], outs[1], outs[2:]


def _dh_rms(dproj, wt, x, g, skip, side=None):
    tm, tk = 1024, 2688
    grid = (T // tm, NCOL // tk)
    n_side = 0 if side is None else len(side["arrs"])
    hbm = pl.BlockSpec(memory_space=pl.ANY)

    def body(*refs):
        a_ref, w_ref, x_ref, g_ref, s_ref = refs[:5]
        dx_ref, dg_ref = refs[5 + n_side], refs[6 + n_side]
        start, finish, scratch = _side_parts(side, refs, 5, 2)
        acc = scratch[0]
        i, l = pl.program_id(0), pl.program_id(1)

        @pl.when((i == 0) & (l == 0))
        def _():
            start()

        part = _nn(a_ref[...], w_ref[...])

        @pl.when(l == 0)
        def _():
            acc[...] = part

        @pl.when(l > 0)
        def _():
            acc[...] += part

        @pl.when(l == grid[1] - 1)
        def _():
            xf = x_ref[...]
            r = lax.rsqrt(jnp.mean(xf * xf, axis=-1, keepdims=True) + EPS)
            dn = acc[...]
            u = dn * g_ref[...]
            dx_ref[...] = s_ref[...] + r * u - xf * (r * r * r) * jnp.mean(u * xf, axis=-1, keepdims=True)
            dg = jnp.sum(dn * xf * r, axis=0, keepdims=True)

            @pl.when(i == 0)
            def _():
                dg_ref[...] = dg

            @pl.when(i > 0)
            def _():
                dg_ref[...] += dg

        @pl.when((i == grid[0] - 1) & (l == grid[1] - 1))
        def _():
            finish()

    tok = pl.BlockSpec((tm, D), lambda i, l: (i, 0))
    outs = pl.pallas_call(
        body, name="dh", grid=grid,
        in_specs=[pl.BlockSpec((tm, tk), lambda i, l: (i, l)), pl.BlockSpec((tk, D), lambda i, l: (l, 0)), tok,
                  pl.BlockSpec((1, D), lambda i, l: (0, 0)), tok] + [hbm] * n_side,
        out_specs=[tok, pl.BlockSpec((1, D), lambda i, l: (0, 0))] + [hbm] * n_side,
        out_shape=[S((T, D), F32), S((1, D), F32)] + ([] if side is None else side["out_shape"]),
        scratch_shapes=[pltpu.VMEM((tm, D), F32)] + ([] if side is None else side["scratch"]),
        compiler_params=pltpu.CompilerParams(dimension_semantics=("arbitrary", "arbitrary")),
    )(dproj, wt, x, g, skip, *([] if side is None else side["arrs"]))
    return outs[0], outs[1], outs[2:]


def _rot_tables(pos_ref, inv_ref):
    lane = lax.broadcasted_iota(jnp.int32, (1, 128), 1) % HD
    ang = pos_ref[...] * inv_ref[...]
    cos, sin = jnp.cos(ang), jnp.sin(ang)
    c = jnp.where(lane < ROT_DIM, cos, 1.0)
    sp = jnp.where((lane >= ROT_DIM // 2) & (lane < ROT_DIM), sin, 0.0)
    sm = jnp.where(lane < ROT_DIM // 2, -sin, 0.0)
    return c, sp, sm


def _head_sums(v):
    same = (lax.broadcasted_iota(jnp.int32, (128, 128), 0) < HD) == (lax.broadcasted_iota(jnp.int32, (128, 128), 1) < HD)
    ones = jnp.where(same, 1.0, 0.0).astype(BF16)
    hi = v.astype(BF16)
    lo = (v - hi.astype(F32)).astype(BF16)
    return _nn(hi, ones) + _nn(lo, ones)


def _pair_norm(t):
    return lax.rsqrt(_head_sums(t * t) * (1.0 / HD) + EPS)


def _pair_mean(t):
    return _head_sums(t) * (1.0 / HD)


TT = 256
NCH = ATT_QKV // 128


def _res_shape(grp, dtype):
    return S((DILATIONS[grp], T // DILATIONS[grp], ATT_W), dtype)


def _res_spec(grp):
    dil = DILATIONS[grp]
    return pl.BlockSpec((dil, TT // dil, ATT_W), lambda i: (0, i, 0))


def _to_residues(sc, j, dst_ref, dil, cols):
    n = TT // dil
    for r in range(dil):
        rows = sc[j] if dil == 1 else sc.at[j][pl.ds(r, n, stride=dil), :]
        dst_ref[r, :, cols] = rows.astype(dst_ref.dtype)


def _from_residues(src_ref, cols, sc, j, dil):
    n = TT // dil
    for r in range(dil):
        if dil == 1:
            sc[j] = src_ref[r, :, cols]
        else:
            sc.at[j][pl.ds(r, n, stride=dil), :] = src_ref[r, :, cols]


def _tok_spec(width, cblk=0):
    return pl.BlockSpec((TT, width), functools.partial(lambda i, c: (i, c), c=cblk))


def _const_spec(arr_or_shape):
    shape = arr_or_shape if isinstance(arr_or_shape, tuple) else arr_or_shape.shape
    return pl.BlockSpec(shape, functools.partial(lambda i, nd: (0,) * nd, nd=len(shape)))


def _qk_prep(proj, pos, inv, gq, gk):
    def body(q_ref, k_ref, v_ref, pos_ref, inv_ref, gq_ref, gk_ref, *rest):
        outs, sc = rest[:9], rest[9]
        c, sp, sm = _rot_tables(pos_ref, inv_ref)
        for which, (src, g_ref) in enumerate(((q_ref, gq_ref), (k_ref, gk_ref), (v_ref, None))):
            if g_ref is not None:
                g = jnp.broadcast_to(g_ref[...] * ((HD ** -0.5) if which == 0 else 1.0), c.shape)
                cg, spg, smg = c * g, sp * pltpu.roll(g, 8, 1), sm * pltpu.roll(g, 120, 1)
            for j in range(NCH):
                t = src[:, j * 128:(j + 1) * 128]
                if g_ref is not None:
                    t = _pair_norm(t) * (t * cg + pltpu.roll(t, 8, 1) * spg + pltpu.roll(t, 120, 1) * smg)
                sc[j] = t
            for j in range(NCH):
                grp, sub = divmod(j * 128, ATT_W)
                _to_residues(sc, j, outs[which * 3 + grp], DILATIONS[grp], slice(sub, sub + 128))

    return pl.pallas_call(
        body, name="qk_prep", grid=(T // TT,),
        in_specs=[_tok_spec(ATT_QKV, C_QA // ATT_QKV), _tok_spec(ATT_QKV, C_KA // ATT_QKV),
                  _tok_spec(ATT_QKV, C_VA // ATT_QKV), _tok_spec(1), _const_spec(inv), _const_spec(gq), _const_spec(gk)],
        out_specs=[_res_spec(g) for _ in range(3) for g in range(3)],
        out_shape=[_res_shape(g, BF16) for _ in range(3) for g in range(3)],
        scratch_shapes=[pltpu.VMEM((NCH, TT, 128), F32)],
        compiler_params=pltpu.CompilerParams(dimension_semantics=("arbitrary",)),
    )(proj, proj, proj, pos, inv, gq, gk)


def _qk_bwd(proj, pos, inv, gq, gk, dqs, dks, dvs, dproj):
    const = lambda a: pl.BlockSpec(a.shape, functools.partial(lambda i, p, nd: (0,) * nd, nd=a.ndim))
    res = lambda g: pl.BlockSpec((DILATIONS[g], TT // DILATIONS[g], ATT_W), lambda i, p: (0, i, 0))
    base = C_QA // ATT_QKV

    def body(t_ref, pos_ref, inv_ref, gq_ref, gk_ref, dq0, dq1, dq2, dk0, dk1, dk2, dv0, dv1, dv2, buf_ref,
             out_ref, dgq_ref, dgk_ref, sc):
        del buf_ref
        part = pl.program_id(1)
        first = pl.program_id(0) == 0

        def gather(drefs):
            for j in range(NCH):
                grp, sub = divmod(j * 128, ATT_W)
                _from_residues(drefs[grp], slice(sub, sub + 128), sc, j, DILATIONS[grp])

        def normed(g_ref, drefs, dg_ref):
            c, sp, sm = _rot_tables(pos_ref, inv_ref)
            gather(drefs)
            dg = jnp.zeros((1, 128), F32)
            for j in range(NCH):
                cols = slice(j * 128, (j + 1) * 128)
                d_rot = sc[j]
                dn = d_rot * c + pltpu.roll(d_rot * sp, 120, 1) + pltpu.roll(d_rot * sm, 8, 1)
                t = t_ref[:, cols]
                r = _pair_norm(t)
                gain = g_ref[...]
                dn_t = dn * t
                out_ref[:, cols] = (r * (dn * gain - t * ((r * r) * _pair_mean(dn_t * gain)))).astype(BF16)
                dg = dg + jnp.sum(dn_t * r, axis=0, keepdims=True)
            dg = dg + pltpu.roll(dg, HD, 1)

            @pl.when(first)
            def _():
                dg_ref[...] = dg

            @pl.when(jnp.logical_not(first))
            def _():
                dg_ref[...] += dg

        @pl.when(part == 0)
        def _():
            gather((dv0, dv1, dv2))
            for j in range(NCH):
                out_ref[:, j * 128:(j + 1) * 128] = sc[j].astype(BF16)

        @pl.when(part == 1)
        def _():
            normed(gq_ref, (dq0, dq1, dq2), dgq_ref)

        @pl.when(part == 2)
        def _():
            normed(gk_ref, (dk0, dk1, dk2), dgk_ref)

    keep = pl.BlockSpec((1, 128), lambda i, p: (0, 0))
    return pl.pallas_call(
        body, name="qk_bwd", grid=(T // TT, 3),
        in_specs=[pl.BlockSpec((TT, ATT_QKV), lambda i, p: (i, base + jnp.maximum(p - 1, 0))),
                  pl.BlockSpec((TT, 1), lambda i, p: (i, 0)), const(inv), const(gq), const(gk)]
        + [res(g) for _ in range(3) for g in range(3)] + [pl.BlockSpec(memory_space=pl.ANY)],
        out_specs=[pl.BlockSpec((TT, ATT_QKV), lambda i, p: (i, base + jnp.where(p == 0, 2, p - 1))), keep, keep],
        out_shape=[S(dproj.shape, dproj.dtype), S((1, 128), F32), S((1, 128), F32)],
        input_output_aliases={14: 0},
        scratch_shapes=[pltpu.VMEM((NCH, TT, 128), F32)],
        compiler_params=pltpu.CompilerParams(dimension_semantics=("arbitrary", "arbitrary")),
    )(proj, pos, inv, gq, gk, *dqs, *dks, *dvs, dproj)


def _split_heads(t):
    low = lax.broadcasted_iota(jnp.int32, (1, 128), 1) < HD
    zero = jnp.zeros_like(t)
    return jnp.concatenate([jnp.where(low, t, zero), jnp.where(low, zero, t)], axis=0)


def _join_heads(t2):
    low = lax.broadcasted_iota(jnp.int32, (1, 128), 1) < HD
    n = t2.shape[0] // 2
    return jnp.where(low, t2[:n], t2[n:])


def _band_mask4(has_before, has_own):
    row = lax.broadcasted_iota(jnp.int32, (BLK, 4 * BLK), 0)
    lane = lax.broadcasted_iota(jnp.int32, (BLK, 4 * BLK), 1)
    key = lane & (BLK - 1)
    own = lane >= 2 * BLK
    return (own & (key <= row) & has_own) | (jnp.logical_not(own) & (key >= row) & has_before)


def _band_mask_before(has_before):
    row = lax.broadcasted_iota(jnp.int32, (BLK, 2 * BLK), 0)
    key = lax.broadcasted_iota(jnp.int32, (BLK, 2 * BLK), 1) & (BLK - 1)
    return (key >= row) & has_before


def _per_head(width, col_a, col_b):
    lane = lax.broadcasted_iota(jnp.int32, (1, width), 1)
    return jnp.where((lane & BLK) == 0, col_a, col_b)


NQ = ATT_W // 128


def _att_fwd(q, k, v, grp, name):
    dil = DILATIONS[grp]
    nb = T // dil // BLK

    def body(q_ref, kp_ref, kc_ref, vp_ref, vc_ref, o_ref, lse_ref, s_sc, p_sc):
        mask = _band_mask4(pl.program_id(1) > 0, True)
        low = lax.broadcasted_iota(jnp.int32, (1, 128), 1) < HD
        halves = lambda ref, j, h: (ref[j, :, h * BLK:(h + 1) * BLK], ref[j, :, (h + 2) * BLK:(h + 3) * BLK])
        for j in range(NQ):
            cols = slice(j * 128, (j + 1) * 128)
            k4 = jnp.concatenate([_split_heads(kp_ref[:, cols]), _split_heads(kc_ref[:, cols])], axis=0)
            s_sc[j] = jnp.where(mask, _nt(q_ref[:, cols], k4), -jnp.inf)
        mxs = [[jnp.maximum(*(jnp.max(t, axis=-1, keepdims=True) for t in halves(s_sc, j, h))) for h in range(2)]
               for j in range(NQ)]
        dens = []
        for j in range(NQ):
            p = jnp.exp(s_sc[j] - _per_head(4 * BLK, *mxs[j]))
            p_sc[j] = p.astype(BF16)
            dens.append([jnp.sum(p[:, h * BLK:(h + 1) * BLK], axis=-1, keepdims=True)
                         + jnp.sum(p[:, (h + 2) * BLK:(h + 3) * BLK], axis=-1, keepdims=True) for h in range(2)])
        for j in range(NQ):
            cols = slice(j * 128, (j + 1) * 128)
            v4 = jnp.concatenate([_split_heads(vp_ref[:, cols]), _split_heads(vc_ref[:, cols])], axis=0)
            o_ref[:, cols] = _nn(p_sc[j], v4) / jnp.where(low, dens[j][0], dens[j][1])
            lse_ref[:, cols] = jnp.where(low, mxs[j][0] + jnp.log(dens[j][0]), mxs[j][1] + jnp.log(dens[j][1]))

    cur = pl.BlockSpec((None, BLK, ATT_W), lambda r, i: (r, i, 0))
    prev = pl.BlockSpec((None, BLK, ATT_W), lambda r, i: (r, jnp.maximum(i - 1, 0), 0))
    return pl.pallas_call(
        body, name=name, grid=(dil, nb),
        in_specs=[cur, prev, cur, prev, cur],
        out_specs=[cur, cur], out_shape=[_res_shape(grp, F32)] * 2,
        scratch_shapes=[pltpu.VMEM((NQ, BLK, 4 * BLK), F32), pltpu.VMEM((NQ, BLK, 4 * BLK), BF16)],
        compiler_params=pltpu.CompilerParams(dimension_semantics=("parallel", "arbitrary")),
    )(q, k, k, v, v)


def _att_bwd(q, k, v, datt, att, lse, grp, name):
    dil = DILATIONS[grp]
    nb = T // dil // BLK
    scale = HD ** -0.5

    def body(q0_ref, q1_ref, kp_ref, kc_ref, vp_ref, vc_ref, do0_ref, do1_ref, o0_ref, o1_ref, l0_ref, l1_ref,
             dq_ref, dk_ref, dv_ref, k4_sc, v4_sc, s0_sc, s1_sc, dp0_sc, dp1_sc, p_sc, ds_sc):
        i = pl.program_id(1)
        mask_mine = _band_mask4(i > 0, True)
        mask_next = _band_mask_before(i < nb - 1)
        low = lax.broadcasted_iota(jnp.int32, (1, 128), 1) < HD
        for j in range(NQ):
            cols = slice(j * 128, (j + 1) * 128)
            k4_sc[j, :2 * BLK] = _split_heads(kp_ref[:, cols])
            k4_sc[j, 2 * BLK:] = _split_heads(kc_ref[:, cols])
            v4_sc[j, :2 * BLK] = _split_heads(vp_ref[:, cols])
            v4_sc[j, 2 * BLK:] = _split_heads(vc_ref[:, cols])
        for j in range(NQ):
            cols = slice(j * 128, (j + 1) * 128)
            s0_sc[j] = _nt(q0_ref[:, cols], k4_sc[j])
            s1_sc[j] = _nt(q1_ref[:, cols], k4_sc[j, 2 * BLK:])
            dp0_sc[j] = _nt(do0_ref[:, cols].astype(BF16), v4_sc[j])
            dp1_sc[j] = _nt(do1_ref[:, cols].astype(BF16), v4_sc[j, 2 * BLK:])
        stats = []
        for j in range(NQ):
            cols = slice(j * 128, (j + 1) * 128)
            for do_ref, o_ref, l_ref in ((do0_ref, o0_ref, l0_ref), (do1_ref, o1_ref, l1_ref)):
                prod = do_ref[:, cols].astype(F32) * o_ref[:, cols].astype(F32)
                d_all = jnp.sum(prod, axis=-1, keepdims=True)
                d_low = jnp.sum(jnp.where(low, prod, 0.0), axis=-1, keepdims=True)
                lse_t = l_ref[:, cols]
                stats.append((d_low, d_all - d_low, lse_t[:, 0:1], lse_t[:, HD:HD + 1]))
        for j in range(NQ):
            (da, db, la, lb), (da1, db1, la1, lb1) = stats[2 * j], stats[2 * j + 1]
            p0 = jnp.where(mask_mine, jnp.exp(s0_sc[j] - _per_head(4 * BLK, la, lb)), 0.0)
            ds0 = p0 * (dp0_sc[j] - _per_head(4 * BLK, da, db))
            p1 = jnp.where(mask_next, jnp.exp(s1_sc[j] - _per_head(2 * BLK, la1, lb1)), 0.0)
            ds1 = p1 * (dp1_sc[j] - _per_head(2 * BLK, da1, db1))
            p_sc[j, :BLK] = p0.astype(BF16)
            ds_sc[j, :BLK] = ds0.astype(BF16)
            p_sc[j, BLK:, 2 * BLK:] = p1.astype(BF16)
            ds_sc[j, BLK:, 2 * BLK:] = ds1.astype(BF16)
        for j in range(NQ):
            cols = slice(j * 128, (j + 1) * 128)
            dq_ref[:, cols] = _nn(ds_sc[j, :BLK], k4_sc[j]) * scale
            qq = jnp.concatenate([q0_ref[:, cols], q1_ref[:, cols]], axis=0)
            dd = jnp.concatenate([do0_ref[:, cols], do1_ref[:, cols]], axis=0).astype(BF16)
            dk_ref[:, cols] = _join_heads(_tn(ds_sc[j, :, 2 * BLK:], qq))
            dv_ref[:, cols] = _join_heads(_tn(p_sc[j, :, 2 * BLK:], dd))

    def spec(shift):
        return pl.BlockSpec((None, BLK, ATT_W), lambda r, i: (r, jnp.clip(i + shift, 0, nb - 1), 0))

    here, after, before = spec(0), spec(1), spec(-1)
    vm = pltpu.VMEM
    return pl.pallas_call(
        body, name=name, grid=(dil, nb),
        in_specs=[here, after, before, here, before, here, here, after, here, after, here, after],
        out_specs=[here] * 3, out_shape=[_res_shape(grp, F32)] * 3,
        scratch_shapes=[vm((NQ, 4 * BLK, 128), BF16), vm((NQ, 4 * BLK, 128), BF16), vm((NQ, BLK, 4 * BLK), F32),
                        vm((NQ, BLK, 2 * BLK), F32), vm((NQ, BLK, 4 * BLK), F32), vm((NQ, BLK, 2 * BLK), F32),
                        vm((NQ, 2 * BLK, 4 * BLK), BF16), vm((NQ, 2 * BLK, 4 * BLK), BF16)],
        compiler_params=pltpu.CompilerParams(dimension_semantics=("parallel", "arbitrary")),
    )(q, q, k, k, v, v, datt, datt, att, att, lse, lse)


def _att_merge(os_, lses, proj):
    nq = ATT_W // 128

    def body(o0, o1, o2, l0, l1, l2, za_ref, att_ref, lse_ref, ain_ref, sc):
        for a, ref in enumerate((o0, o1, o2, l0, l1, l2)):
            for j in range(nq):
                _from_residues(ref, slice(j * 128, (j + 1) * 128), sc, a * nq + j, DILATIONS[a % 3])
        for j in range(nq):
            cols = slice(j * 128, (j + 1) * 128)
            oa, ob, oc = (sc[a * nq + j] for a in range(3))
            la, lb, lc = (sc[(3 + a) * nq + j] for a in range(3))
            m = jnp.maximum(jnp.maximum(la, lb), lc)
            wa, wb, wc = jnp.exp(la - m), jnp.exp(lb - m), jnp.exp(lc - m)
            tot = wa + wb + wc
            att = (wa * oa + wb * ob + wc * oc) / tot
            att_ref[:, cols] = att
            lse_ref[:, cols] = m + jnp.log(tot)
            za = za_ref[:, cols]
            ain_ref[:, cols] = (att * za * _sigmoid(za)).astype(BF16)

    return pl.pallas_call(
        body, name="att_merge", grid=(T // TT,),
        in_specs=[_res_spec(g) for _ in range(2) for g in range(3)] + [_tok_spec(ATT_W, C_ZA // ATT_W)],
        out_specs=[_tok_spec(ATT_W)] * 3,
        out_shape=[S((T, ATT_W), F32), S((T, ATT_W), F32), S((T, ATT_W), BF16)],
        scratch_shapes=[pltpu.VMEM((6 * nq, TT, 128), F32)],
        compiler_params=pltpu.CompilerParams(dimension_semantics=("arbitrary",)),
    )(*os_, *lses, proj)


def _att_gate_bwd(dain, att, lse, proj, dproj):
    nq = ATT_W // 128

    def body(d_ref, att_ref, lse_ref, za_ref, buf_ref, dza_ref, da0, da1, da2, at1, at2, ls1, ls2, sc):
        del buf_ref
        for j in range(nq):
            cols = slice(j * 128, (j + 1) * 128)
            za = za_ref[:, cols]
            sg = _sigmoid(za)
            d = d_ref[:, cols].astype(F32)
            att_ = att_ref[:, cols]
            dza_ref[:, cols] = (d * att_ * sg * (1.0 + za * (1.0 - sg))).astype(BF16)
            sc[j] = d * za * sg
            sc[nq + j] = att_
            sc[2 * nq + j] = lse_ref[:, cols]
        for j in range(nq):
            cols = slice(j * 128, (j + 1) * 128)
            for grp, dst in enumerate((da0, da1, da2)):
                _to_residues(sc, j, dst, DILATIONS[grp], cols)
            for grp, dst in ((1, at1), (2, at2)):
                _to_residues(sc, nq + j, dst, DILATIONS[grp], cols)
            for grp, dst in ((1, ls1), (2, ls2)):
                _to_residues(sc, 2 * nq + j, dst, DILATIONS[grp], cols)

    res = (0, 1, 2, 1, 2, 1, 2)
    return pl.pallas_call(
        body, name="att_gate_bwd", grid=(T // TT,),
        in_specs=[_tok_spec(ATT_W)] * 3 + [_tok_spec(ATT_W, C_ZA // ATT_W), pl.BlockSpec(memory_space=pl.ANY)],
        out_specs=[_tok_spec(ATT_W, C_ZA // ATT_W)] + [_res_spec(g) for g in res],
        out_shape=[S(dproj.shape, dproj.dtype)] + [_res_shape(g, BF16) for g in res[:5]]
        + [_res_shape(g, F32) for g in res[5:]],
        input_output_aliases={4: 0},
        scratch_shapes=[pltpu.VMEM((3 * nq, TT, 128), F32)],
        compiler_params=pltpu.CompilerParams(dimension_semantics=("arbitrary",)),
    )(dain, att, lse, proj, dproj)


def _split3(v):
    hi = v.astype(BF16)
    r1 = v - hi.astype(F32)
    mid = r1.astype(BF16)
    lo = (r1 - mid.astype(F32)).astype(BF16)
    return hi, mid, lo


def _chunk_scores(qt, kt, q_ref, k_ref, h):
    cols = slice(h * GDK, (h + 1) * GDK)
    own = jnp.sum(q_ref[:, cols] * (GDK ** -0.5) * k_ref[:, cols], axis=-1, keepdims=True)
    row = lax.broadcasted_iota(jnp.int32, (GLA_C, GLA_C), 0)
    col = lax.broadcasted_iota(jnp.int32, (GLA_C, GLA_C), 1)
    a = _nt(qt.astype(BF16), kt.astype(BF16))
    return jnp.where(col < row, a, jnp.where(col == row, own, 0.0))


def _tri_sum(v, upper):
    n = v.shape[0]
    row = lax.broadcasted_iota(jnp.int32, (n, n), 0)
    col = lax.broadcasted_iota(jnp.int32, (n, n), 1)
    tri = jnp.where(col >= row if upper else col <= row, 1.0, 0.0).astype(BF16)
    hi, mid, lo = _split3(v)
    return _nn(tri, hi) + _nn(tri, mid) + _nn(tri, lo)


def _gla_gates(glr_ref, w2_ref, b_ref):
    logit = _nn(glr_ref[...].astype(BF16), w2_ref[...]) + b_ref[...]
    lg = (jnp.minimum(logit, 0.0) - jnp.log(1.0 + jnp.exp(-jnp.abs(logit)))) * (1.0 / GLA_TAU)
    return logit, _tri_sum(lg, upper=False)


def _gla_head(cum, q_ref, k_ref, h):
    cols = slice(h * GDK, (h + 1) * GDK)
    b = cum[:, cols]
    last = b[GLA_C - 1:GLA_C, :]
    e_pos = jnp.exp(b)
    e_neg = jnp.exp(-b)
    e_end = jnp.exp(last - b)
    qt = q_ref[:, cols] * (GDK ** -0.5) * e_pos
    kt = k_ref[:, cols] * e_neg
    kh = k_ref[:, cols] * e_end
    return b, last, e_pos, e_neg, e_end, qt, kt, kh


def _causal(n):
    return lax.broadcasted_iota(jnp.int32, (n, n), 1) <= lax.broadcasted_iota(jnp.int32, (n, n), 0)


def _gla_fwd(proj, w2p, bg, gn):
    nc = T // GLA_C

    def body(q_ref, k_ref, v_ref, glr_ref, zg_ref, w2_ref, b_ref, gn_ref, o_ref, bin_ref, st_ref, state):
        @pl.when(pl.program_id(0) == 0)
        def _():
            state[...] = jnp.zeros_like(state)

        _, cum = _gla_gates(glr_ref, w2_ref, b_ref)
        for h in range(GH):
            _, last, _, _, _, qt, kt, kh = _gla_head(cum, q_ref, k_ref, h)
            vcols = slice(h * GDV, (h + 1) * GDV)
            st = state[h]
            st_ref[0, h] = st
            v = v_ref[:, vcols].astype(BF16)
            qb = qt.astype(BF16)
            a = _chunk_scores(qt, kt, q_ref, k_ref, h)
            o = _nt(qb, st.astype(BF16)) + _nn(a.astype(BF16), v)
            state[h] = st * jnp.exp(last) + _tn(v, kh.astype(BF16))
            o_ref[:, vcols] = o
            r = lax.rsqrt(jnp.mean(o * o, axis=-1, keepdims=True) + EPS)
            zg = zg_ref[:, vcols]
            bin_ref[:, vcols] = (o * r * gn_ref[...] * zg * _sigmoid(zg)).astype(BF16)

    row = lambda width, cblk: pl.BlockSpec((GLA_C, width), functools.partial(lambda i, c: (i, c), c=cblk))
    full = lambda a: pl.BlockSpec(a.shape, functools.partial(lambda i, nd: (0,) * nd, nd=a.ndim))
    return pl.pallas_call(
        body, name="gla_fwd", grid=(nc,),
        in_specs=[row(512, C_QG // 512), row(512, C_KG // 512), row(1024, C_VG // 1024), row(GLR_W, C_GLR // GLR_W),
                  row(1024, C_ZG // 1024), full(w2p), full(bg), full(gn)],
        out_specs=[pl.BlockSpec((GLA_C, GH * GDV), lambda i: (i, 0)), pl.BlockSpec((GLA_C, GH * GDV), lambda i: (i, 0)),
                   pl.BlockSpec((1, GH, GDV, GDK), lambda i: (i, 0, 0, 0))],
        out_shape=[S((T, GH * GDV), F32), S((T, GH * GDV), BF16), S((nc, GH, GDV, GDK), F32)],
        scratch_shapes=[pltpu.VMEM((GH, GDV, GDK), F32)],
        compiler_params=pltpu.CompilerParams(dimension_semantics=("arbitrary",)),
    )(proj, proj, proj, proj, proj, w2p, bg, gn)


def _gla_bwd(proj, w2p, bg, gn, o_gla, states, dbin, dproj):
    nc = T // GLA_C

    def body(q_ref, k_ref, v_ref, glr_ref, zg_ref, w2_ref, b_ref, gn_ref, o_ref, st_ref, dbin_ref, buf_ref,
             out_ref, dw2_ref, dbg_ref, dgn_ref, dstate, dlogit):
        del buf_ref
        dq_ref = out_ref.at[:, C_QG:C_KG]
        dk_ref = out_ref.at[:, C_KG:C_VG]
        dv_ref = out_ref.at[:, C_VG:C_ZG]
        dzg_ref = out_ref.at[:, C_ZG:C_GLR]
        dglr_ref = out_ref.at[:, C_GLR:C_GLR + GLR_W]
        first = pl.program_id(0) == 0

        @pl.when(first)
        def _():
            dstate[...] = jnp.zeros_like(dstate)

        logit, cum = _gla_gates(glr_ref, w2_ref, b_ref)
        is_last = lax.broadcasted_iota(jnp.int32, (GLA_C, 1), 0) == GLA_C - 1
        dgn = jnp.zeros((1, GDV), F32)
        for h in range(GH):
            _, last, e_pos, e_neg, e_end, qt, kt, kh = _gla_head(cum, q_ref, k_ref, h)
            cols = slice(h * GDK, (h + 1) * GDK)
            vcols = slice(h * GDV, (h + 1) * GDV)
            o = o_ref[:, vcols]
            r = lax.rsqrt(jnp.mean(o * o, axis=-1, keepdims=True) + EPS)
            zg = zg_ref[:, vcols]
            sg = _sigmoid(zg)
            db_ = dbin_ref[:, vcols].astype(F32)
            dlin = db_ * zg * sg
            dzg_ref[:, vcols] = (db_ * (o * r * gn_ref[...]) * sg * (1.0 + zg * (1.0 - sg))).astype(BF16)
            u = dlin * gn_ref[...]
            do = (r * u - o * (r * r * r) * jnp.mean(u * o, axis=-1, keepdims=True)).astype(BF16)
            dgn = dgn + jnp.sum(dlin * o * r, axis=0, keepdims=True)
            st = st_ref[0, h]
            dst = dstate[h]
            v = v_ref[:, vcols].astype(BF16)
            qb, kb, khb = qt.astype(BF16), kt.astype(BF16), kh.astype(BF16)
            dstb = dst.astype(BF16)
            causal = _causal(GLA_C)
            a = _chunk_scores(qt, kt, q_ref, k_ref, h).astype(BF16)
            da = jnp.where(causal, _nt(do, v), 0.0).astype(BF16)
            dqt = _nn(do, st.astype(BF16)) + _nn(da, kb)
            dkt = _tn(da, qb)
            dkh = _nn(v, dstb)
            dv_ref[:, vcols] = (_tn(a, do) + _nt(khb, dstb)).astype(BF16)
            lam = jnp.exp(last)
            dlam = jnp.sum(dst * st, axis=0, keepdims=True)
            dstate[h] = dst * lam + _tn(do, qb)
            dq_ref[:, cols] = (dqt * e_pos * (GDK ** -0.5)).astype(BF16)
            dk_ref[:, cols] = (dkt * e_neg + dkh * e_end).astype(BF16)
            dkh_kh = dkh * kh
            dcum = dqt * qt - dkt * kt - dkh_kh
            dlast = jnp.sum(dkh_kh, axis=0, keepdims=True) + dlam * lam
            dcum = jnp.where(is_last, dcum + dlast, dcum)
            dlg = _tri_sum(dcum, upper=True)
            dlogit[:, cols] = dlg * (1.0 / GLA_TAU) * (1.0 - _sigmoid(logit[:, cols]))

        dl = dlogit[...]
        dlb = dl.astype(BF16)
        dglr_ref[...] = _nt(dlb, w2_ref[...]).astype(BF16)
        dw2 = _tn(glr_ref[...].astype(BF16), dlb)
        dbg = jnp.sum(dl, axis=0, keepdims=True)

        @pl.when(first)
        def _():
            dw2_ref[...] = dw2
            dbg_ref[...] = dbg
            dgn_ref[...] = dgn

        @pl.when(jnp.logical_not(first))
        def _():
            dw2_ref[...] += dw2
            dbg_ref[...] += dbg
            dgn_ref[...] += dgn

    rev = lambda i: nc - 1 - i
    row = lambda width, cblk: pl.BlockSpec((GLA_C, width), functools.partial(lambda i, c: (rev(i), c), c=cblk))
    full = lambda a: pl.BlockSpec(a.shape, functools.partial(lambda i, nd: (0,) * nd, nd=a.ndim))
    keep = lambda shape: pl.BlockSpec(shape, functools.partial(lambda i, nd: (0,) * nd, nd=len(shape)))
    return pl.pallas_call(
        body, name="gla_bwd", grid=(nc,),
        in_specs=[row(512, C_QG // 512), row(512, C_KG // 512), row(1024, C_VG // 1024), row(GLR_W, C_GLR // GLR_W),
                  row(1024, C_ZG // 1024), full(w2p), full(bg), full(gn), row(GH * GDV, 0),
                  pl.BlockSpec((1, GH, GDV, GDK), lambda i: (rev(i), 0, 0, 0)), row(GH * GDV, 0),
                  pl.BlockSpec(memory_space=pl.ANY)],
        out_specs=[row(GLA_GROUP_W, 0), keep((GLR_W, 512)), keep((1, 512)), keep((1, GDV))],
        out_shape=[S(dproj.shape, dproj.dtype), S((GLR_W, 512), F32), S((1, 512), F32), S((1, GDV), F32)],
        input_output_aliases={11: 0},
        scratch_shapes=[pltpu.VMEM((GH, GDV, GDK), F32), pltpu.VMEM((GLA_C, GH * GDK), F32)],
        compiler_params=pltpu.CompilerParams(dimension_semantics=("arbitrary",)),
    )(proj, proj, proj, proj, proj, w2p, bg, gn, o_gla, states, dbin, dproj)


RT = 512


def _rowchain(body, name, ins, outs, scratch=()):
    in_specs, args = [], []
    for spec in ins:
        if spec[0] == "tok":
            _, arr, width, cblk = spec
            in_specs.append(pl.BlockSpec((RT, width), functools.partial(lambda i, c: (i, c), c=cblk)))
        else:
            arr = spec[1]
            in_specs.append(pl.BlockSpec(arr.shape, functools.partial(lambda i, nd: (0,) * nd, nd=arr.ndim)))
        args.append(arr)
    out_specs, out_shape = [], []
    for spec in outs:
        if spec[0] == "tok":
            _, shape, dtype, width, cblk = spec
            out_specs.append(pl.BlockSpec((RT, width), functools.partial(lambda i, c: (i, c), c=cblk)))
        else:
            _, shape, dtype = spec
            out_specs.append(pl.BlockSpec(shape, functools.partial(lambda i, nd: (0,) * nd, nd=len(shape))))
        out_shape.append(S(shape, dtype))
    return pl.pallas_call(
        body, name=name, grid=(T // RT,), in_specs=in_specs, out_specs=out_specs, out_shape=out_shape,
        scratch_shapes=list(scratch), compiler_params=pltpu.CompilerParams(dimension_semantics=("arbitrary",)),
    )(*args)


def _tok(arr, width=None, cblk=0):
    return ("tok", arr, arr.shape[1] if width is None else width, cblk)


def _tok_out(dtype, width=D):
    return ("tok", (T, width), dtype, width, 0)


def _branches_fwd(ain, bin_, proj, x, w_att, w_gla, w_out):
    def body(ain_ref, bin_ref, g_ref, x_ref, wa_ref, wg_ref, wo_ref, ya_ref, yb_ref, y_ref, x1_ref):
        ya = _nn(ain_ref[...], wa_ref[...]).astype(BF16)
        yb = _nn(bin_ref[...], wg_ref[...]).astype(BF16)
        ya_ref[...] = ya
        yb_ref[...] = yb
        y = (_sigmoid(g_ref[:, :D]) * ya.astype(F32) + _sigmoid(g_ref[:, D:]) * yb.astype(F32)).astype(BF16)
        y_ref[...] = y
        x1_ref[...] = x_ref[...] + _nn(y, wo_ref[...])

    return _rowchain(body, "branches_fwd",
                     [_tok(ain), _tok(bin_), _tok(proj, 2 * D, C_GA // (2 * D)), _tok(x), ("all", w_att),
                      ("all", w_gla), ("all", w_out)],
                     [_tok_out(BF16), _tok_out(BF16), _tok_out(BF16), _tok_out(F32)])


def _accumulate(ref, part, first):
    @pl.when(first)
    def _():
        ref[...] = part

    @pl.when(jnp.logical_not(first))
    def _():
        ref[...] += part


def _ple_loss(x1, p, target, g2, w_pg, w_ple):
    def body(x1_ref, p_ref, t_ref, g_ref, wpg_ref, wple_ref, n2_ref, loss_ref, dout_ref, du_ref, dwple_ref, acc):
        first = pl.program_id(0) == 0
        x1 = x1_ref[...]
        r = lax.rsqrt(jnp.mean(x1 * x1, axis=-1, keepdims=True) + EPS)
        n2 = (x1 * r * g_ref[...]).astype(BF16)
        n2_ref[...] = n2
        pg = _sigmoid(_nn(n2, wpg_ref[...]))
        pb = p_ref[...].astype(BF16)
        e_ = _nn(pb, wple_ref[...])
        diff = x1 + e_ * pg - t_ref[...]
        _accumulate(acc, jnp.sum(diff * diff, axis=0, keepdims=True), first)
        dout = diff * (1.0 / D)
        dout_ref[...] = dout
        du_ref[...] = (dout * e_ * pg * (1.0 - pg)).astype(BF16)
        _accumulate(dwple_ref, _tn(pb, (dout * pg).astype(BF16)), first)
        loss_ref[...] = jnp.zeros((1, 128), F32) + jnp.sum(acc[...], axis=-1, keepdims=True) * (0.5 / D)

    return _rowchain(body, "ple_loss", [_tok(x1), _tok(p), _tok(target), ("all", g2), ("all", w_pg), ("all", w_ple)],
                     [_tok_out(BF16), ("acc", (1, 128), F32), _tok_out(F32), _tok_out(BF16), ("acc", (PLE, D), F32)],
                     scratch=[pltpu.VMEM((1, D), F32)])


def _ple_bwd(du, n2, y, x1, dout, g2, w_pg, w_out):
    def body(du_ref, n2_ref, y_ref, x1_ref, dout_ref, g_ref, wpg_ref, wo_ref, dx_ref, dy_ref, dg_ref, dwpg_ref,
             dwo_ref):
        first = pl.program_id(0) == 0
        x1 = x1_ref[...]
        r = lax.rsqrt(jnp.mean(x1 * x1, axis=-1, keepdims=True) + EPS)
        du_ = du_ref[...]
        dn = _nt(du_, wpg_ref[...])
        u = dn * g_ref[...]
        dx = dout_ref[...] + r * u - x1 * (r * r * r) * jnp.mean(u * x1, axis=-1, keepdims=True)
        dxb = dx.astype(BF16)
        dx_ref[...] = dx
        dy_ref[...] = _nt(dxb, wo_ref[...]).astype(BF16)
        _accumulate(dg_ref, jnp.sum(dn * x1 * r, axis=0, keepdims=True), first)
        _accumulate(dwpg_ref, _tn(n2_ref[...], du_), first)
        _accumulate(dwo_ref, _tn(y_ref[...], dxb), first)

    return _rowchain(body, "ple_bwd",
                     [_tok(du), _tok(n2), _tok(y), _tok(x1), _tok(dout), ("all", g2), ("all", w_pg), ("all", w_out)],
                     [_tok_out(F32), _tok_out(BF16), ("acc", (1, D), F32), ("acc", (D, D), F32), ("acc", (D, D), F32)])


def _branches_bwd(dy, ya, yb, ain, bin_, proj, w_att, w_gla):
    def body(dy_ref, ya_ref, yb_ref, ain_ref, bin_ref, g_ref, wa_ref, wg_ref, dg_ref, dain_ref, dbin_ref,
             dwa_ref, dwg_ref):
        first = pl.program_id(0) == 0
        dy_ = dy_ref[...].astype(F32)
        sa, sb = _sigmoid(g_ref[:, :D]), _sigmoid(g_ref[:, D:])
        dg_ref[:, :D] = (dy_ * ya_ref[...].astype(F32) * sa * (1.0 - sa)).astype(BF16)
        dg_ref[:, D:] = (dy_ * yb_ref[...].astype(F32) * sb * (1.0 - sb)).astype(BF16)
        dya = (dy_ * sa).astype(BF16)
        dyb = (dy_ * sb).astype(BF16)
        dain_ref[...] = _nt(dya, wa_ref[...]).astype(BF16)
        dbin_ref[...] = _nt(dyb, wg_ref[...]).astype(BF16)
        _accumulate(dwa_ref, _tn(ain_ref[...], dya), first)
        _accumulate(dwg_ref, _tn(bin_ref[...], dyb), first)

    gates = C_GA // (2 * D)
    return _rowchain(body, "branches_bwd",
                     [_tok(dy), _tok(ya), _tok(yb), _tok(ain), _tok(bin_), _tok(proj, 2 * D, gates), ("all", w_att),
                      ("all", w_gla)],
                     [("tok", (T, NCOL), BF16, 2 * D, gates), _tok_out(BF16, ATT_W), _tok_out(BF16),
                      ("acc", (ATT_W, D), F32), ("acc", (D, D), F32)])


def _peer(k):
    x, y, c = lax.axis_index("x"), lax.axis_index("y"), lax.axis_index("c")
    return (x ^ ((k >> 2) & 1), y ^ ((k >> 1) & 1), c ^ (k & 1))


def _my_index():
    return 4 * lax.axis_index("x") + 2 * lax.axis_index("y") + lax.axis_index("c")


def _peer_index(k):
    px, py, pc = _peer(k)
    return 4 * px + 2 * py + pc


def _pairwise_plan(src_of, dst_of, landed_of, own_src, own_dst):
    def plan(ins, outs, send, recv, local):
        n = len(ins)

        def own():
            return [pltpu.make_async_copy(own_src(ins[a]), own_dst(outs[a]), local.at[a]) for a in range(n)]

        def remote(k, a, src, dst):
            return pltpu.make_async_remote_copy(src_ref=src, dst_ref=dst, send_sem=send.at[k - 1, a],
                                                recv_sem=recv.at[k - 1, a], device_id=_peer(k), device_id_type=MESH)

        def sent():
            return [remote(k, a, src_of(ins[a], k), dst_of(outs[a])) for k in range(1, NDEV) for a in range(n)]

        def start():
            for cp in own() + sent():
                cp.start()

        def finish():
            for k in range(1, NDEV):
                for a in range(n):
                    remote(k, a, own_src(ins[a]), landed_of(outs[a], k)).wait_recv()
            for cp in sent():
                cp.wait_send()
            for cp in own():
                cp.wait()

        return start, finish

    return plan


def _pairwise_sems(n):
    return [pltpu.SemaphoreType.DMA((NDEV - 1, n)), pltpu.SemaphoreType.DMA((NDEV - 1, n)),
            pltpu.SemaphoreType.DMA((n,))]


def _gather_side(arrs):
    plan = _pairwise_plan(src_of=lambda i, k: i, dst_of=lambda o: o.at[_my_index()],
                          landed_of=lambda o, k: o.at[_peer_index(k)],
                          own_src=lambda i: i, own_dst=lambda o: o.at[_my_index()])
    return dict(arrs=arrs, out_shape=[S((NDEV,) + a.shape, a.dtype) for a in arrs],
                scratch=_pairwise_sems(len(arrs)), plan=plan)


def _exchange_side(arrs):
    plan = _pairwise_plan(src_of=lambda i, k: i.at[_peer_index(k)], dst_of=lambda o: o.at[_my_index()],
                          landed_of=lambda o, k: o.at[_peer_index(k)],
                          own_src=lambda i: i.at[_my_index()], own_dst=lambda o: o.at[_my_index()])
    return dict(arrs=arrs, out_shape=[S(a.shape, a.dtype) for a in arrs], scratch=_pairwise_sems(len(arrs)), plan=plan)


def _comm_call(side, name):
    n = len(side["arrs"])

    def body(*refs):
        start, finish = side["plan"](refs[:n], refs[n:2 * n], *refs[2 * n:])
        start()
        finish()

    hbm = pl.BlockSpec(memory_space=pl.ANY)
    return pl.pallas_call(body, name=name, in_specs=[hbm] * n, out_specs=[hbm] * n, out_shape=side["out_shape"],
                          scratch_shapes=side["scratch"])(*side["arrs"])


def _all_gather_by_chip(arrs, name):
    n = len(arrs)

    def body(*refs):
        ins, outs = refs[:n], refs[n:2 * n]
        send, recv, local = refs[2 * n:]
        x, y, c = lax.axis_index("x"), lax.axis_index("y"), lax.axis_index("c")
        me, sibling = (x, y, c), (x, y, 1 - c)
        chips = [(1 - x, y), (x, 1 - y), (1 - x, 1 - y)]

        def copy(k, a, block, to, src=None):
            px, py, pc = block
            slot = outs[a].at[4 * px + 2 * py + pc]
            return pltpu.make_async_remote_copy(
                src_ref=slot if src is None else src, dst_ref=slot, send_sem=send.at[k, a], recv_sem=recv.at[k, a],
                device_id=to, device_id_type=MESH)

        north = c == 1
        via = (jnp.where(north, 1 - x, x), jnp.where(north, y, 1 - y))
        onward = (jnp.where(north, x, 1 - x), jnp.where(north, 1 - y, y), c)
        mine = [pltpu.make_async_copy(ins[a], outs[a].at[4 * x + 2 * y + c], local.at[a]) for a in range(n)]
        first = []
        for a in range(n):
            first.append(copy(0, a, me, sibling, src=ins[a]))
            first += [copy(1 + j, a, me, (*chips[j], c), src=ins[a]) for j in range(2)]
        for cp in mine + first:
            cp.start()
        passed = []
        for j in range(2):
            for a in range(n):
                copy(1 + j, a, (*chips[j], c), me).wait_recv()
                passed.append(copy(4 + j, a, (*chips[j], c), sibling))
                passed[-1].start()
        for a in range(n):
            passed.append(copy(3, a, (*via, c), onward))
            passed[-1].start()
        for a in range(n):
            copy(3, a, (*chips[2], c), me).wait_recv()
            passed.append(copy(6, a, (*chips[2], c), sibling))
            passed[-1].start()
        for a in range(n):
            copy(0, a, sibling, me).wait_recv()
        for j, chip in enumerate(chips):
            for a in range(n):
                copy(4 + j, a, (*chip, 1 - c), me).wait_recv()
        for cp in first + passed:
            cp.wait_send()
        for cp in mine:
            cp.wait()

    hbm = pl.BlockSpec(memory_space=pl.ANY)
    return pl.pallas_call(
        body, name=name, in_specs=[hbm] * n, out_specs=[hbm] * n,
        out_shape=[S((NDEV,) + a.shape, a.dtype) for a in arrs],
        scratch_shapes=[pltpu.SemaphoreType.DMA((NDEV - 1, n)), pltpu.SemaphoreType.DMA((NDEV - 1, n)),
                        pltpu.SemaphoreType.DMA((n,))],
    )(*arrs)


NCHIP = 4


def _exchange_sibling(arrs, name):
    n = len(arrs)

    def body(*refs):
        ins, outs = refs[:n], refs[n:2 * n]
        send, recv = refs[2 * n:]
        x, y, c = lax.axis_index("x"), lax.axis_index("y"), lax.axis_index("c")
        copies = []
        for q in range(NCHIP):
            for a in range(n):
                copies.append(pltpu.make_async_remote_copy(
                    src_ref=ins[a].at[2 * q + (1 - c)], dst_ref=outs[a].at[q], send_sem=send.at[q, a],
                    recv_sem=recv.at[q, a], device_id=(x, y, 1 - c), device_id_type=MESH))
        for cp in copies:
            cp.start()
        for cp in copies:
            cp.wait_recv()
        for cp in copies:
            cp.wait_send()

    hbm = pl.BlockSpec(memory_space=pl.ANY)
    return pl.pallas_call(
        body, name=name, in_specs=[hbm] * n, out_specs=[hbm] * n,
        out_shape=[S((NCHIP,) + a.shape[1:], a.dtype) for a in arrs],
        scratch_shapes=[pltpu.SemaphoreType.DMA((NCHIP, n)), pltpu.SemaphoreType.DMA((NCHIP, n))],
    )(*arrs)


def _pair_add(mine, got, core, name):
    _, rows, cols = mine.shape
    tc = 256
    assert cols % tc == 0

    def body(core_ref, a_ref, b_ref, o_ref):
        o_ref[...] = (a_ref[...].astype(F32) + b_ref[...].astype(F32)).astype(BF16)

    return pl.pallas_call(
        body, name=name,
        grid_spec=pltpu.PrefetchScalarGridSpec(
            num_scalar_prefetch=1, grid=(NCHIP, cols // tc),
            in_specs=[pl.BlockSpec((None, rows, tc), lambda q, i, core_ref: (2 * q + core_ref[0], 0, i)),
                      pl.BlockSpec((None, rows, tc), lambda q, i, core_ref: (q, 0, i))],
            out_specs=pl.BlockSpec((None, rows, tc), lambda q, i, core_ref: (q, 0, i))),
        out_shape=S((NCHIP, rows, cols), BF16),
    )(core, mine, got)


def _chips_side(arrs):
    def plan(ins, outs, send, recv, local):
        n = len(ins)

        def places():
            x, y, c = lax.axis_index("x"), lax.axis_index("y"), lax.axis_index("c")
            return 2 * x + y, c, [(1 - x, y), (x, 1 - y), (1 - x, 1 - y)]

        def own():
            here, _, _ = places()
            return [pltpu.make_async_copy(ins[a].at[here], outs[a].at[here], local.at[a]) for a in range(n)]

        def remote(j, a, src_slot, dst_slot):
            _, c, chips = places()
            cx, cy = chips[j]
            return pltpu.make_async_remote_copy(
                src_ref=ins[a].at[src_slot], dst_ref=outs[a].at[dst_slot], send_sem=send.at[j, a],
                recv_sem=recv.at[j, a], device_id=(cx, cy, c), device_id_type=MESH)

        def sent():
            here, _, chips = places()
            return [remote(j, a, 2 * cx + cy, here) for j, (cx, cy) in enumerate(chips) for a in range(n)]

        def start():
            for cp in own() + sent():
                cp.start()

        def finish():
            here, _, chips = places()
            for j, (cx, cy) in enumerate(chips):
                for a in range(n):
                    remote(j, a, here, 2 * cx + cy).wait_recv()
            for cp in sent():
                cp.wait_send()
            for cp in own():
                cp.wait()

        return start, finish

    n = len(arrs)
    return dict(arrs=arrs, out_shape=[S(a.shape, a.dtype) for a in arrs],
                scratch=[pltpu.SemaphoreType.DMA((NCHIP - 1, n)), pltpu.SemaphoreType.DMA((NCHIP - 1, n)),
                         pltpu.SemaphoreType.DMA((n,))], plan=plan)


def _adamw_shards(parts, places):
    n_src = len(parts)

    def body(*refs):
        srcs, rest = refs[:n_src], refs[n_src:]
        for j, (src, rows, cols, _) in enumerate(places):
            w_ref, m_ref, v_ref = rest[3 * j:3 * j + 3]
            outs = rest[3 * len(places) + 4 * j:3 * len(places) + 4 * j + 4]
            p_ref = srcs[src]
            g = p_ref[0, rows, cols].astype(F32)
            for s in range(1, p_ref.shape[0]):
                g = g + p_ref[s, rows, cols].astype(F32)
            delta, m_new, v_new = _adam_math(g, w_ref[0], m_ref[0], v_ref[0])
            for ref, val in zip(outs, (g, delta, m_new, v_new)):
                ref[0] = val

    flat = [a for place in places for a in place[3]]
    return pl.pallas_call(
        body, name="adam_shards",
        out_shape=[S(place[3][0].shape, F32) for place in places for _ in range(4)],
    )(*parts, *flat)


def _adam_math(g, w, m, v):
    c1 = 1.0 - ADAM_B1 ** ADAM_STEP
    c2 = 1.0 - ADAM_B2 ** ADAM_STEP
    m_new = ADAM_B1 * m + (1.0 - ADAM_B1) * g
    v_new = ADAM_B2 * v + (1.0 - ADAM_B2) * (g * g)
    return -ADAM_LR * ((m_new / c1) / (jnp.sqrt(v_new / c2) + ADAM_EPS) + ADAM_WD * w), m_new, v_new


def _adamw_small(parts, params, loss_parts):
    n = len(params)

    def body(*refs):
        p_refs, rest = refs[:n], refs[n + 1:]
        total = refs[n][0]
        for s in range(1, NDEV):
            total = total + refs[n][s]
        refs[-1][...] = total
        for j in range(n):
            w_ref, m_ref, v_ref = rest[3 * j:3 * j + 3]
            g_ref, d_ref, mo_ref, vo_ref = rest[3 * n + 4 * j:3 * n + 4 * j + 4]
            width = w_ref.shape[1]
            g = p_refs[j][0]
            for s in range(1, NDEV):
                g = g + p_refs[j][s]
            g = g[:, :width]
            delta, m_new, v_new = _adam_math(g, w_ref[...], m_ref[...], v_ref[...])
            g_ref[...] = g
            d_ref[...] = delta
            mo_ref[...] = m_new
            vo_ref[...] = v_new

    flat = [a for group in params for a in group]
    return pl.pallas_call(
        body, name="adam_small",
        out_shape=[S(group[0].shape, F32) for group in params for _ in range(4)] + [S((1, 128), F32)],
    )(*parts, loss_parts, *flat)


def _adamw_rows(parts, w, m, v, name, tc=128):
    rows, _, cols = w.shape
    nparts = parts.shape[0]
    nsteps = cols // tc

    def body(p_ref, w_hbm, m_hbm, v_hbm, g_hbm, d_hbm, mo_hbm, vo_hbm, inbuf, outbuf, insem, outsem):
        i = pl.program_id(0)
        slot = i & 1

        def view(ref, step):
            return ref.at[:, 0, pl.ds(pl.multiple_of(step * tc, tc), tc)]

        def fetch(step, sl):
            return [pltpu.make_async_copy(view(src, step), inbuf.at[sl, k], insem.at[sl, k])
                    for k, src in enumerate((w_hbm, m_hbm, v_hbm))]

        def write(step, sl):
            return [pltpu.make_async_copy(outbuf.at[sl, k], view(dst, step), outsem.at[sl, k])
                    for k, dst in enumerate((g_hbm, d_hbm, mo_hbm, vo_hbm))]

        @pl.when(i == 0)
        def _():
            for cp in fetch(0, 0):
                cp.start()

        @pl.when(i + 1 < nsteps)
        def _():
            for cp in fetch(i + 1, 1 - slot):
                cp.start()

        for cp in fetch(i, slot):
            cp.wait()

        @pl.when(i >= 2)
        def _():
            for cp in write(i - 2, slot):
                cp.wait()

        g = p_ref[0].astype(F32)
        for s in range(1, nparts):
            g = g + p_ref[s].astype(F32)
        g = g[:rows]
        delta, m_new, v_new = _adam_math(g, inbuf[slot, 0], inbuf[slot, 1], inbuf[slot, 2])
        for k, val in enumerate((g, delta, m_new, v_new)):
            outbuf[slot, k] = val
        for cp in write(i, slot):
            cp.start()

        @pl.when(i == nsteps - 1)
        def _():
            for cp in write(i - 1, 1 - slot) + write(i, slot):
                cp.wait()

    hbm = pl.BlockSpec(memory_space=pl.ANY)
    assert nsteps >= 2
    return pl.pallas_call(
        body, name=name, grid=(nsteps,),
        in_specs=[pl.BlockSpec((nparts, parts.shape[1], tc), lambda i: (0, 0, i)), hbm, hbm, hbm],
        out_specs=[hbm] * 4, out_shape=[S((rows, 1, cols), F32)] * 4,
        scratch_shapes=[pltpu.VMEM((2, 3, rows, tc), F32), pltpu.VMEM((2, 4, rows, tc), F32),
                        pltpu.SemaphoreType.DMA((2, 3)), pltpu.SemaphoreType.DMA((2, 4))],
        compiler_params=pltpu.CompilerParams(dimension_semantics=("arbitrary",)),
    )(parts, w, m, v)


SLAB = 1296
REMAP_RUNS = 3
_PIECES = ((O_QA, O_ZA, C_QA), (O_ZA, O_QG, C_ZA), (O_QG, O_GLR, C_QG), (O_GLR, O_ZG, C_GLR), (O_ZG, O_GA, C_ZG),
           (O_GA, O_END, C_GA))


def _slab_row_of_aligned(a):
    for o0, o1, a0 in _PIECES:
        if a0 <= a < a0 + o1 - o0:
            c = o0 + a - a0
            return SLAB * (c // W_IN_SHARD) + c % W_IN_SHARD
    return -1


def _aligned_row_of_slab(r):
    d, l = divmod(r, SLAB)
    if l >= W_IN_SHARD:
        return -1
    c = d * W_IN_SHARD + l
    for o0, o1, a0 in _PIECES:
        if o0 <= c < o1:
            return a0 + c - o0
    raise AssertionError(c)


def _remap_table(row_of, n_out, block, n_src):
    win = block + 16
    table = []
    for b in range(n_out // block):
        runs = []
        for i in range(block):
            s = row_of(b * block + i)
            if s < 0:
                continue
            if runs and runs[-1][0] + runs[-1][2] == s and runs[-1][1] + runs[-1][2] == i:
                runs[-1][2] += 1
            else:
                runs.append([s, i, 1])
        assert len(runs) <= REMAP_RUNS, (b, runs)
        row = []
        for s, i, n in runs:
            w = min(s // 16 * 16, n_src - win)
            assert 0 <= s - w and s - w + n <= win
            row += [w, s - w, i, n]
        table.append(row + [0] * (4 * REMAP_RUNS - len(row)))
    return table


def _remap_rows(src, row_of, n_out, block, name):
    n_src, cols = src.shape
    nb, win = n_out // block, block + 16
    table = jnp.asarray(_remap_table(row_of, n_out, block, n_src), jnp.int32)

    def body(t_ref, src_hbm, o_ref, buf, acc, sem):
        b = pl.program_id(0)
        slot = b & 1

        def each_run(step, sl, act):
            for k in range(REMAP_RUNS):
                @pl.when(t_ref[step, 4 * k + 3] > 0)
                def _():
                    start = pl.multiple_of(t_ref[step, 4 * k], 16)
                    act(pltpu.make_async_copy(src_hbm.at[pl.ds(start, win)], buf.at[sl, k], sem.at[sl, k]))

        @pl.when(b == 0)
        def _():
            each_run(0, 0, lambda cp: cp.start())

        @pl.when(b + 1 < nb)
        def _():
            each_run(b + 1, 1 - slot, lambda cp: cp.start())

        each_run(b, slot, lambda cp: cp.wait())
        acc[...] = jnp.zeros_like(acc)
        row = lax.broadcasted_iota(jnp.int32, (block, win), 0)
        col = lax.broadcasted_iota(jnp.int32, (block, win), 1)
        for k in range(REMAP_RUNS):
            shift, first, count = (t_ref[b, 4 * k + j] for j in (1, 2, 3))

            @pl.when(count > 0)
            def _():
                pick = (col == row - first + shift) & (row >= first) & (row < first + count)
                acc[...] += _nn(jnp.where(pick, 1.0, 0.0).astype(BF16), buf[slot, k])

        o_ref[...] = acc[...].astype(o_ref.dtype)

    return pl.pallas_call(
        body, name=name,
        grid_spec=pltpu.PrefetchScalarGridSpec(
            num_scalar_prefetch=1, grid=(nb,), in_specs=[pl.BlockSpec(memory_space=pl.ANY)],
            out_specs=pl.BlockSpec((block, cols), lambda b, t: (b, 0)),
            scratch_shapes=[pltpu.VMEM((2, REMAP_RUNS, win, cols), src.dtype), pltpu.VMEM((block, cols), F32),
                            pltpu.SemaphoreType.DMA((2, REMAP_RUNS))]),
        out_shape=S((n_out, cols), src.dtype),
        compiler_params=pltpu.CompilerParams(dimension_semantics=("arbitrary",)),
    )(table, src)


def _col_blocks(w, width):
    return w.reshape(w.shape[0], NDEV, width).transpose(1, 0, 2)


def _from_col_blocks(w):
    return w.transpose(1, 0, 2).reshape(w.shape[1], NDEV * w.shape[2])


def _local_step(x2, p2, pos, tgt, norm_g, qk_norm_q, qk_norm_k, gla_gate_b, gla_norm_g, ple_norm_g, w_al,
                weights=None, proj_side=None, unpack=None, dw_side_of=None, dh_side_of=None):
    half = ROT_DIM // 2
    inv8 = jnp.power(jnp.float32(ROPE_THETA), -jnp.arange(half, dtype=F32) * 2.0 / ROT_DIM)
    inv = jnp.tile(jnp.concatenate([inv8, inv8, jnp.zeros((HD - ROT_DIM,), F32)]), 2).reshape(1, 128)
    gq = jnp.tile(qk_norm_q, (1, 2))
    gk = jnp.tile(qk_norm_k, (1, 2))

    proj, h, got = _proj_rms(x2, norm_g, w_al, proj_side)
    if proj_side is not None:
        weights = unpack(got)
    w2p, w_att_f, w_gla_f, w_out_f, w_pg_f, w_ple_f = weights
    qkv = _qk_prep(proj, pos, inv, gq, gk)
    fwd = [_att_fwd(qkv[g], qkv[3 + g], qkv[6 + g], g, f"att_fwd{g}") for g in range(3)]
    att, lse, ain = _att_merge([f[0] for f in fwd], [f[1] for f in fwd], proj)
    o_gla, bin_, states = _gla_fwd(proj, w2p, gla_gate_b, gla_norm_g)
    ya, yb, y, x1 = _branches_fwd(ain, bin_, proj, x2, w_att_f, w_gla_f, w_out_f)
    n2, loss_v, dout, du, dw_ple = _ple_loss(x1, p2, tgt, ple_norm_g, w_pg_f, w_ple_f)

    dx1, dy, dg_ple, dw_pg, dw_out = _ple_bwd(du, n2, y, x1, dout, ple_norm_g, w_pg_f, w_out_f)
    dproj, dain, dbin, dw_att, dw_gla = _branches_bwd(dy, ya, yb, ain, bin_, proj, w_att_f, w_gla_f)
    dproj, da0, da1, da2, at1, at2, ls1, ls2 = _att_gate_bwd(dain, att, lse, proj, dproj)
    datts, atts, lses = (da0, da1, da2), (att[None], at1, at2), (lse[None], ls1, ls2)
    dproj, dw2, dbg, dgn = _gla_bwd(proj, w2p, gla_gate_b, gla_norm_g, o_gla, states, dbin, dproj)
    bwd = [_att_bwd(qkv[g], qkv[3 + g], qkv[6 + g], datts[g], atts[g], lses[g], g, f"att_bwd{g}") for g in range(3)]
    dproj, dgq, dgk = _qk_bwd(proj, pos, inv, gq, gk, [b[0] for b in bwd], [b[1] for b in bwd],
                              [b[2] for b in bwd], dproj)
    out = dict(loss=loss_v, dw2=dw2, dw_att=dw_att, dw_gla=dw_gla, dw_out=dw_out, dw_pg=dw_pg, dw_ple=dw_ple,
               dgq=dgq, dgk=dgk, dbg=dbg, dgn=dgn, dg_ple=dg_ple)
    if dw_side_of is None:
        dw_al = _mm(dproj, h, mode="tn", name="dw_in", tm=1536, tn=D, tk=T, out_dtype=BF16)
    else:
        dw_al, out["dw_side"] = _mm(dproj, h, mode="tn", name="dw_in", tm=1536, tn=D, tk=T, out_dtype=BF16,
                                    side=dw_side_of(out))
    grad_x, dg_norm, out["dh_side"] = _dh_rms(dproj, w_al, x2, norm_g, dx1,
                                              None if dh_side_of is None else dh_side_of(dw_al))
    out.update(grad_x=grad_x, dw_al=dw_al, dg_norm=dg_norm)
    return out


def kernel(x, p, positions, norm_g, w_in, qk_norm_q, qk_norm_k, gla_gate_w2, gla_gate_b, gla_norm_g, w_att_proj, w_gla_proj, w_out, ple_norm_g, w_ple_gate, w_ple, loss_target, m_norm_g, m_w_in, m_qk_norm_q, m_qk_norm_k, m_gla_gate_w2, m_gla_gate_b, m_gla_norm_g, m_w_att_proj, m_w_gla_proj, m_w_out, m_ple_norm_g, m_w_ple_gate, m_w_ple, v_norm_g, v_w_in, v_qk_norm_q, v_qk_norm_k, v_gla_gate_w2, v_gla_gate_b, v_gla_norm_g, v_w_att_proj, v_w_gla_proj, v_w_out, v_ple_norm_g, v_w_ple_gate, v_w_ple):
    x2, p2, tgt = x[0], p[0, 0], loss_target[0]
    pos = positions.astype(F32).reshape(T, 1)

    rows3 = jnp.stack([w_gla_proj[0], w_out[0], w_ple_gate[0]]).astype(BF16)
    cols3 = jnp.concatenate([w_att_proj[0], w_ple[0], jnp.pad(gla_gate_w2[0], ((0, 0), (0, 64)))], axis=0).astype(BF16)
    mine = jnp.pad(w_in[0].T.astype(BF16), ((0, SLAB - W_IN_SHARD), (0, 0)))
    (g_in,) = _all_gather_by_chip([mine], "gather_w_in")
    w_al = _remap_rows(g_in.reshape(NDEV * SLAB, D), _slab_row_of_aligned, NCOL, 256, "align_w_in")

    def unpack(got):
        g_rows, g_cols = got
        w2_f = _from_col_blocks(g_cols[:, 768:784, :64])
        return (jnp.pad(w2_f, ((0, GLR_W - GLR_N), (0, 0))), _from_col_blocks(g_cols[:, :512]),
                g_rows[:, 0].reshape(D, D), g_rows[:, 1].reshape(D, D), g_rows[:, 2].reshape(D, D),
                _from_col_blocks(g_cols[:, 512:768]))

    def dw_side_of(g):
        s_rows = jnp.concatenate([g[k].reshape(NDEV, 128, D) for k in ("dw_gla", "dw_out", "dw_pg")], axis=1)
        s_cols = jnp.concatenate([_col_blocks(g["dw_att"], 128), _col_blocks(g["dw_ple"], 128),
                                  jnp.pad(_col_blocks(g["dw2"][:GLR_N], 64), ((0, 0), (0, 0), (0, 64)))], axis=1)
        return _exchange_side([s_rows.astype(BF16), s_cols.astype(BF16)])

    def dh_side_of(dw_al):
        s_in = _remap_rows(dw_al, _aligned_row_of_slab, NDEV * SLAB, 432, "shard_dw_in").reshape(NDEV, SLAB, D)
        (from_sibling,) = _exchange_sibling([s_in], "exchange_sibling")
        core = lax.axis_index("c").astype(jnp.int32).reshape(1)
        return _chips_side([_pair_add(s_in, from_sibling, core, "pair_add")])

    loc = _local_step(x2, p2, pos, tgt, norm_g, qk_norm_q, qk_norm_k, gla_gate_b, gla_norm_g, ple_norm_g, w_al,
                      proj_side=_gather_side([rows3, cols3]), unpack=unpack, dw_side_of=dw_side_of,
                      dh_side_of=dh_side_of)
    loss_v, grad_x = loc["loss"], loc["grad_x"]
    dg_norm, dgq, dgk, dbg, dgn, dg_ple = (loc[k] for k in ("dg_norm", "dgq", "dgk", "dbg", "dgn", "dg_ple"))
    r_rows, r_cols = loc["dw_side"]
    (r_in,) = loc["dh_side"]

    r_small = _comm_call(_gather_side([dg_norm, dgq, dgk, dbg, dgn, dg_ple, loss_v]), "gather_small")

    outs = {}

    rows_of = lambda a: jnp.transpose(a, (2, 0, 1))
    outs["w_in"] = [jnp.transpose(o, (1, 2, 0))[0] for o in
                    _adamw_rows(r_in, rows_of(w_in), rows_of(m_w_in), rows_of(v_w_in), "adam_w_in")]
    places = (("w_gla_proj", 0, slice(0, 128), slice(None), (w_gla_proj, m_w_gla_proj, v_w_gla_proj)),
              ("w_out", 0, slice(128, 256), slice(None), (w_out, m_w_out, v_w_out)),
              ("w_ple_gate", 0, slice(256, 384), slice(None), (w_ple_gate, m_w_ple_gate, v_w_ple_gate)),
              ("w_att_proj", 1, slice(0, 512), slice(None), (w_att_proj, m_w_att_proj, v_w_att_proj)),
              ("w_ple", 1, slice(512, 768), slice(None), (w_ple, m_w_ple, v_w_ple)),
              ("gla_gate_w2", 1, slice(768, 784), slice(0, 64), (gla_gate_w2, m_gla_gate_w2, v_gla_gate_w2)))
    res = _adamw_shards([r_rows, r_cols], [place[1:] for place in places])
    for j, place in enumerate(places):
        outs[place[0]] = [o[0] for o in res[4 * j:4 * j + 4]]
    small = ((norm_g, m_norm_g, v_norm_g), (qk_norm_q, m_qk_norm_q, v_qk_norm_q), (qk_norm_k, m_qk_norm_k, v_qk_norm_k),
             (gla_gate_b, m_gla_gate_b, v_gla_gate_b), (gla_norm_g, m_gla_norm_g, v_gla_norm_g),
             (ple_norm_g, m_ple_norm_g, v_ple_norm_g))
    sm = _adamw_small(r_small[:6], small, r_small[6])
    for j, nm in enumerate(("norm_g", "qk_norm_q", "qk_norm_k", "gla_gate_b", "gla_norm_g", "ple_norm_g")):
        outs[nm] = [o[0] for o in sm[4 * j:4 * j + 4]]

    loss = sm[-1][0, 0]
    order = ["norm_g", "w_in", "qk_norm_q", "qk_norm_k", "gla_gate_w2", "gla_gate_b", "gla_norm_g", "w_att_proj",
             "w_gla_proj", "w_out", "ple_norm_g", "w_ple_gate", "w_ple"]
    result = [loss, grad_x[None]]
    for i in range(4):
        result += [outs[nm][i][None] for nm in order]
    return tuple(result)
```

```python
import functools

import jax
import jax.numpy as jnp
from jax import lax
from jax.experimental import pallas as pl
from jax.experimental.pallas import tpu as pltpu

F32 = jnp.float32
BF16 = jnp.bfloat16
S = jax.ShapeDtypeStruct

T = 4096
D = 1024
NDEV = 8
HD = 64
ATT_W = 512
ATT_QKV = 1536
DILATIONS = (1, 4, 16)
BLK = 128
GH, GDK, GDV = 4, 128, 256
GLA_C = 128
PLE = 256
EPS = 1e-6
ROT_DIM = 16
ROPE_THETA = 500000.0
GLA_TAU = 16.0
W_IN_SHARD = 1282

C_QG, C_KG, C_VG, C_ZG, C_GLR, C_ZA, C_GA, C_GB, C_QA, C_KA, C_VA = (
    0, 512, 1024, 2048, 3072, 3584, 4096, 5120, 6144, 7680, 9216)
GLA_GROUP_W = 3584
GLR_W = 512
NCOL = 10752
GLR_N = 16
O_QA, O_ZA, O_QG, O_GLR, O_ZG, O_GA, O_END = 0, 4608, 5120, 7168, 7184, 8208, 10256

ADAM_LR, ADAM_B1, ADAM_B2, ADAM_EPS, ADAM_WD, ADAM_STEP = 0.001, 0.9, 0.999, 1e-08, 0.01, 10

MESH = pl.DeviceIdType.MESH


def _sigmoid(z):
    return 1.0 / (1.0 + jnp.exp(-z))


def _dot(a, b, dims):
    return lax.dot_general(a, b, (dims, ((), ())), preferred_element_type=F32)


def _nn(a, b):
    return _dot(a, b, ((1,), (0,)))


def _nt(a, b):
    return _dot(a, b, ((1,), (1,)))


def _tn(a, b):
    return _dot(a, b, ((0,), (0,)))


def _mm(a, b, *, mode, name, tm, tn, tk, out_dtype=F32, res=None, side=None):
    if mode == "nn":
        (m, k), n = a.shape, b.shape[1]
        a_spec = pl.BlockSpec((tm, tk), lambda i, j, l: (i, l))
        b_spec = pl.BlockSpec((tk, tn), lambda i, j, l: (l, j))
        dot = _nn
    elif mode == "nt":
        (m, k), n = a.shape, b.shape[0]
        a_spec = pl.BlockSpec((tm, tk), lambda i, j, l: (i, l))
        b_spec = pl.BlockSpec((tn, tk), lambda i, j, l: (j, l))
        dot = _nt
    else:
        (k, m), n = a.shape, b.shape[1]
        a_spec = pl.BlockSpec((tk, tm), lambda i, j, l: (l, i))
        b_spec = pl.BlockSpec((tk, tn), lambda i, j, l: (l, j))
        dot = _tn
    assert m % tm == 0 and n % tn == 0 and k % tk == 0, (name, m, n, k)
    grid = (m // tm, n // tn, k // tk)
    nk = grid[2]
    o_spec = pl.BlockSpec((tm, tn), lambda i, j, l: (i, j))
    in_specs = [a_spec, b_spec]
    args = [a, b]
    if res is not None:
        in_specs.append(o_spec)
        args.append(res)
    n_in = len(args)
    n_side = 0 if side is None else len(side["arrs"])
    hbm = pl.BlockSpec(memory_space=pl.ANY)

    def body(*refs):
        a_ref, b_ref = refs[0], refs[1]
        r_ref = refs[2] if res is not None else None
        o_ref = refs[n_in + n_side]
        scratch = refs[n_in + 2 * n_side + 1:]
        if side is not None:
            start, finish_side = side["plan"](refs[n_in:n_in + n_side], refs[n_in + n_side + 1:n_in + 2 * n_side + 1],
                                              *scratch[1 if nk > 1 else 0:])
            ids = [pl.program_id(d) for d in range(3)]

            @pl.when((ids[0] == 0) & (ids[1] == 0) & (ids[2] == 0))
            def _():
                start()

        part = dot(a_ref[...].astype(BF16), b_ref[...].astype(BF16))

        def finish(val):
            if r_ref is not None:
                val = val + r_ref[...]
            o_ref[...] = val.astype(out_dtype)

        if nk == 1:
            finish(part)
        else:
            acc = scratch[0]
            l = pl.program_id(2)

            @pl.when(l == 0)
            def _():
                acc[...] = part

            @pl.when(l > 0)
            def _():
                acc[...] += part

            @pl.when(l == nk - 1)
            def _():
                finish(acc[...])

        if side is not None:
            @pl.when((ids[0] == grid[0] - 1) & (ids[1] == grid[1] - 1) & (ids[2] == grid[2] - 1))
            def _():
                finish_side()

    sems = [] if side is None else side["scratch"]
    outs = pl.pallas_call(
        body, name=name, grid=grid,
        in_specs=in_specs + [hbm] * n_side, out_specs=[o_spec] + [hbm] * n_side,
        out_shape=[S((m, n), out_dtype)] + ([] if side is None else side["out_shape"]),
        scratch_shapes=([pltpu.VMEM((tm, tn), F32)] if nk > 1 else []) + sems,
        compiler_params=pltpu.CompilerParams(
            dimension_semantics=("arbitrary",) * 3 if side is not None else ("parallel", "parallel", "arbitrary")),
    )(*args, *([] if side is None else side["arrs"]))
    return outs[0] if side is None else (outs[0], outs[1:])


def _side_parts(side, refs, n_in, n_out):
    n_side = 0 if side is None else len(side["arrs"])
    scratch = refs[n_in + n_out + 2 * n_side:]
    if side is None:
        return (lambda: None), (lambda: None), scratch
    start, finish = side["plan"](refs[n_in:n_in + n_side], refs[n_in + n_side + n_out:n_in + n_out + 2 * n_side],
                                 *scratch[len(scratch) - len(side["scratch"]):])
    return start, finish, scratch


def _proj_rms(x, g, wt, side=None):
    tm, tn = 1024, 1536
    grid = (T // tm, NCOL // tn)
    n_side = 0 if side is None else len(side["arrs"])
    hbm = pl.BlockSpec(memory_space=pl.ANY)

    def body(*refs):
        x_ref, g_ref, w_ref = refs[:3]
        o_ref, h_ref = refs[3 + n_side], refs[4 + n_side]
        start, finish, _ = _side_parts(side, refs, 3, 2)
        i, j = pl.program_id(0), pl.program_id(1)

        @pl.when((i == 0) & (j == 0))
        def _():
            start()

        @pl.when(j == 0)
        def _():
            xf = x_ref[...]
            r = lax.rsqrt(jnp.mean(xf * xf, axis=-1, keepdims=True) + EPS)
            h_ref[...] = (xf * r * g_ref[...]).astype(BF16)

        o_ref[...] = _nt(h_ref[...], w_ref[...])

        @pl.when((i == grid[0] - 1) & (j == grid[1] - 1))
        def _():
            finish()

    outs = pl.pallas_call(
        body, name="proj", grid=grid,
        in_specs=[pl.BlockSpec((tm, D), lambda i, j: (i, 0)), pl.BlockSpec((1, D), lambda i, j: (0, 0)),
                  pl.BlockSpec((tn, D), lambda i, j: (j, 0))] + [hbm] * n_side,
        out_specs=[pl.BlockSpec((tm, tn), lambda i, j: (i, j)), pl.BlockSpec((tm, D), lambda i, j: (i, 0))] + [hbm] * n_side,
        out_shape=[S((T, NCOL), F32), S((T, D), BF16)] + ([] if side is None else side["out_shape"]),
        scratch_shapes=[] if side is None else side["scratch"],
        compiler_params=pltpu.CompilerParams(dimension_semantics=("arbitrary", "arbitrary")),
    )(x, g, wt, *([] if side is None else side["arrs"]))
    return outs[0], outs[1], outs[2:]


def _dh_rms(dproj, wt, x, g, skip, side=None):
    tm, tk = 1024, 2688
    grid = (T // tm, NCOL // tk)
    n_side = 0 if side is None else len(side["arrs"])
    hbm = pl.BlockSpec(memory_space=pl.ANY)

    def body(*refs):
        a_ref, w_ref, x_ref, g_ref, s_ref = refs[:5]
        dx_ref, dg_ref = refs[5 + n_side], refs[6 + n_side]
        start, finish, scratch = _side_parts(side, refs, 5, 2)
        acc = scratch[0]
        i, l = pl.program_id(0), pl.program_id(1)

        @pl.when((i == 0) & (l == 0))
        def _():
            start()

        part = _nn(a_ref[...], w_ref[...])

        @pl.when(l == 0)
        def _():
            acc[...] = part

        @pl.when(l > 0)
        def _():
            acc[...] += part

        @pl.when(l == grid[1] - 1)
        def _():
            xf = x_ref[...]
            r = lax.rsqrt(jnp.mean(xf * xf, axis=-1, keepdims=True) + EPS)
            dn = acc[...]
            u = dn * g_ref[...]
            dx_ref[...] = s_ref[...] + r * u - xf * (r * r * r) * jnp.mean(u * xf, axis=-1, keepdims=True)
            dg = jnp.sum(dn * xf * r, axis=0, keepdims=True)

            @pl.when(i == 0)
            def _():
                dg_ref[...] = dg

            @pl.when(i > 0)
            def _():
                dg_ref[...] += dg

        @pl.when((i == grid[0] - 1) & (l == grid[1] - 1))
        def _():
            finish()

    tok = pl.BlockSpec((tm, D), lambda i, l: (i, 0))
    outs = pl.pallas_call(
        body, name="dh", grid=grid,
        in_specs=[pl.BlockSpec((tm, tk), lambda i, l: (i, l)), pl.BlockSpec((tk, D), lambda i, l: (l, 0)), tok,
                  pl.BlockSpec((1, D), lambda i, l: (0, 0)), tok] + [hbm] * n_side,
        out_specs=[tok, pl.BlockSpec((1, D), lambda i, l: (0, 0))] + [hbm] * n_side,
        out_shape=[S((T, D), F32), S((1, D), F32)] + ([] if side is None else side["out_shape"]),
        scratch_shapes=[pltpu.VMEM((tm, D), F32)] + ([] if side is None else side["scratch"]),
        compiler_params=pltpu.CompilerParams(dimension_semantics=("arbitrary", "arbitrary")),
    )(dproj, wt, x, g, skip, *([] if side is None else side["arrs"]))
    return outs[0], outs[1], outs[2:]


def _rot_tables(pos_ref, inv_ref):
    lane = lax.broadcasted_iota(jnp.int32, (1, 128), 1) % HD
    ang = pos_ref[...] * inv_ref[...]
    cos, sin = jnp.cos(ang), jnp.sin(ang)
    c = jnp.where(lane < ROT_DIM, cos, 1.0)
    sp = jnp.where((lane >= ROT_DIM // 2) & (lane < ROT_DIM), sin, 0.0)
    sm = jnp.where(lane < ROT_DIM // 2, -sin, 0.0)
    return c, sp, sm


def _head_sums(v):
    same = (lax.broadcasted_iota(jnp.int32, (128, 128), 0) < HD) == (lax.broadcasted_iota(jnp.int32, (128, 128), 1) < HD)
    ones = jnp.where(same, 1.0, 0.0).astype(BF16)
    hi = v.astype(BF16)
    lo = (v - hi.astype(F32)).astype(BF16)
    return _nn(hi, ones) + _nn(lo, ones)


def _pair_norm(t):
    return lax.rsqrt(_head_sums(t * t) * (1.0 / HD) + EPS)


def _pair_mean(t):
    return _head_sums(t) * (1.0 / HD)


TT = 256
NCH = ATT_QKV // 128


def _res_shape(grp, dtype):
    return S((DILATIONS[grp], T // DILATIONS[grp], ATT_W), dtype)


def _res_spec(grp):
    dil = DILATIONS[grp]
    return pl.BlockSpec((dil, TT // dil, ATT_W), lambda i: (0, i, 0))


def _to_residues(sc, j, dst_ref, dil, cols):
    n = TT // dil
    for r in range(dil):
        rows = sc[j] if dil == 1 else sc.at[j][pl.ds(r, n, stride=dil), :]
        dst_ref[r, :, cols] = rows.astype(dst_ref.dtype)


def _from_residues(src_ref, cols, sc, j, dil):
    n = TT // dil
    for r in range(dil):
        if dil == 1:
            sc[j] = src_ref[r, :, cols]
        else:
            sc.at[j][pl.ds(r, n, stride=dil), :] = src_ref[r, :, cols]


def _tok_spec(width, cblk=0):
    return pl.BlockSpec((TT, width), functools.partial(lambda i, c: (i, c), c=cblk))


def _const_spec(arr_or_shape):
    shape = arr_or_shape if isinstance(arr_or_shape, tuple) else arr_or_shape.shape
    return pl.BlockSpec(shape, functools.partial(lambda i, nd: (0,) * nd, nd=len(shape)))


def _qk_prep(proj, pos, inv, gq, gk):
    def body(q_ref, k_ref, v_ref, pos_ref, inv_ref, gq_ref, gk_ref, *rest):
        outs, sc = rest[:9], rest[9]
        c, sp, sm = _rot_tables(pos_ref, inv_ref)
        for which, (src, g_ref) in enumerate(((q_ref, gq_ref), (k_ref, gk_ref), (v_ref, None))):
            if g_ref is not None:
                g = jnp.broadcast_to(g_ref[...] * ((HD ** -0.5) if which == 0 else 1.0), c.shape)
                cg, spg, smg = c * g, sp * pltpu.roll(g, 8, 1), sm * pltpu.roll(g, 120, 1)
            for j in range(NCH):
                t = src[:, j * 128:(j + 1) * 128]
                if g_ref is not None:
                    t = _pair_norm(t) * (t * cg + pltpu.roll(t, 8, 1) * spg + pltpu.roll(t, 120, 1) * smg)
                sc[j] = t
            for j in range(NCH):
                grp, sub = divmod(j * 128, ATT_W)
                _to_residues(sc, j, outs[which * 3 + grp], DILATIONS[grp], slice(sub, sub + 128))

    return pl.pallas_call(
        body, name="qk_prep", grid=(T // TT,),
        in_specs=[_tok_spec(ATT_QKV, C_QA // ATT_QKV), _tok_spec(ATT_QKV, C_KA // ATT_QKV),
                  _tok_spec(ATT_QKV, C_VA // ATT_QKV), _tok_spec(1), _const_spec(inv), _const_spec(gq), _const_spec(gk)],
        out_specs=[_res_spec(g) for _ in range(3) for g in range(3)],
        out_shape=[_res_shape(g, BF16) for _ in range(3) for g in range(3)],
        scratch_shapes=[pltpu.VMEM((NCH, TT, 128), F32)],
        compiler_params=pltpu.CompilerParams(dimension_semantics=("arbitrary",)),
    )(proj, proj, proj, pos, inv, gq, gk)


def _qk_bwd(proj, pos, inv, gq, gk, dqs, dks, dvs, dproj):
    const = lambda a: pl.BlockSpec(a.shape, functools.partial(lambda i, p, nd: (0,) * nd, nd=a.ndim))
    res = lambda g: pl.BlockSpec((DILATIONS[g], TT // DILATIONS[g], ATT_W), lambda i, p: (0, i, 0))
    base = C_QA // ATT_QKV

    def body(t_ref, pos_ref, inv_ref, gq_ref, gk_ref, dq0, dq1, dq2, dk0, dk1, dk2, dv0, dv1, dv2, buf_ref,
             out_ref, dgq_ref, dgk_ref, sc):
        del buf_ref
        part = pl.program_id(1)
        first = pl.program_id(0) == 0

        def gather(drefs):
            for j in range(NCH):
                grp, sub = divmod(j * 128, ATT_W)
                _from_residues(drefs[grp], slice(sub, sub + 128), sc, j, DILATIONS[grp])

        def normed(g_ref, drefs, dg_ref):
            c, sp, sm = _rot_tables(pos_ref, inv_ref)
            gather(drefs)
            dg = jnp.zeros((1, 128), F32)
            for j in range(NCH):
                cols = slice(j * 128, (j + 1) * 128)
                d_rot = sc[j]
                dn = d_rot * c + pltpu.roll(d_rot * sp, 120, 1) + pltpu.roll(d_rot * sm, 8, 1)
                t = t_ref[:, cols]
                r = _pair_norm(t)
                gain = g_ref[...]
                dn_t = dn * t
                out_ref[:, cols] = (r * (dn * gain - t * ((r * r) * _pair_mean(dn_t * gain)))).astype(BF16)
                dg = dg + jnp.sum(dn_t * r, axis=0, keepdims=True)
            dg = dg + pltpu.roll(dg, HD, 1)

            @pl.when(first)
            def _():
                dg_ref[...] = dg

            @pl.when(jnp.logical_not(first))
            def _():
                dg_ref[...] += dg

        @pl.when(part == 0)
        def _():
            gather((dv0, dv1, dv2))
            for j in range(NCH):
                out_ref[:, j * 128:(j + 1) * 128] = sc[j].astype(BF16)

        @pl.when(part == 1)
        def _():
            normed(gq_ref, (dq0, dq1, dq2), dgq_ref)

        @pl.when(part == 2)
        def _():
            normed(gk_ref, (dk0, dk1, dk2), dgk_ref)

    keep = pl.BlockSpec((1, 128), lambda i, p: (0, 0))
    return pl.pallas_call(
        body, name="qk_bwd", grid=(T // TT, 3),
        in_specs=[pl.BlockSpec((TT, ATT_QKV), lambda i, p: (i, base + jnp.maximum(p - 1, 0))),
                  pl.BlockSpec((TT, 1), lambda i, p: (i, 0)), const(inv), const(gq), const(gk)]
        + [res(g) for _ in range(3) for g in range(3)] + [pl.BlockSpec(memory_space=pl.ANY)],
        out_specs=[pl.BlockSpec((TT, ATT_QKV), lambda i, p: (i, base + jnp.where(p == 0, 2, p - 1))), keep, keep],
        out_shape=[S(dproj.shape, dproj.dtype), S((1, 128), F32), S((1, 128), F32)],
        input_output_aliases={14: 0},
        scratch_shapes=[pltpu.VMEM((NCH, TT, 128), F32)],
        compiler_params=pltpu.CompilerParams(dimension_semantics=("arbitrary", "arbitrary")),
    )(proj, pos, inv, gq, gk, *dqs, *dks, *dvs, dproj)


def _split_heads(t):
    low = lax.broadcasted_iota(jnp.int32, (1, 128), 1) < HD
    zero = jnp.zeros_like(t)
    return jnp.concatenate([jnp.where(low, t, zero), jnp.where(low, zero, t)], axis=0)


def _join_heads(t2):
    low = lax.broadcasted_iota(jnp.int32, (1, 128), 1) < HD
    n = t2.shape[0] // 2
    return jnp.where(low, t2[:n], t2[n:])


def _band_mask4(has_before, has_own):
    row = lax.broadcasted_iota(jnp.int32, (BLK, 4 * BLK), 0)
    lane = lax.broadcasted_iota(jnp.int32, (BLK, 4 * BLK), 1)
    key = lane & (BLK - 1)
    own = lane >= 2 * BLK
    return (own & (key <= row) & has_own) | (jnp.logical_not(own) & (key >= row) & has_before)


def _band_mask_before(has_before):
    row = lax.broadcasted_iota(jnp.int32, (BLK, 2 * BLK), 0)
    key = lax.broadcasted_iota(jnp.int32, (BLK, 2 * BLK), 1) & (BLK - 1)
    return (key >= row) & has_before


def _per_head(width, col_a, col_b):
    lane = lax.broadcasted_iota(jnp.int32, (1, width), 1)
    return jnp.where((lane & BLK) == 0, col_a, col_b)


NQ = ATT_W // 128


def _att_fwd(q, k, v, grp, name):
    dil = DILATIONS[grp]
    nb = T // dil // BLK

    def body(q_ref, kp_ref, kc_ref, vp_ref, vc_ref, o_ref, lse_ref, s_sc, p_sc):
        mask = _band_mask4(pl.program_id(1) > 0, True)
        low = lax.broadcasted_iota(jnp.int32, (1, 128), 1) < HD
        halves = lambda ref, j, h: (ref[j, :, h * BLK:(h + 1) * BLK], ref[j, :, (h + 2) * BLK:(h + 3) * BLK])
        for j in range(NQ):
            cols = slice(j * 128, (j + 1) * 128)
            k4 = jnp.concatenate([_split_heads(kp_ref[:, cols]), _split_heads(kc_ref[:, cols])], axis=0)
            s_sc[j] = jnp.where(mask, _nt(q_ref[:, cols], k4), -jnp.inf)
        mxs = [[jnp.maximum(*(jnp.max(t, axis=-1, keepdims=True) for t in halves(s_sc, j, h))) for h in range(2)]
               for j in range(NQ)]
        dens = []
        for j in range(NQ):
            p = jnp.exp(s_sc[j] - _per_head(4 * BLK, *mxs[j]))
            p_sc[j] = p.astype(BF16)
            dens.append([jnp.sum(p[:, h * BLK:(h + 1) * BLK], axis=-1, keepdims=True)
                         + jnp.sum(p[:, (h + 2) * BLK:(h + 3) * BLK], axis=-1, keepdims=True) for h in range(2)])
        for j in range(NQ):
            cols = slice(j * 128, (j + 1) * 128)
            v4 = jnp.concatenate([_split_heads(vp_ref[:, cols]), _split_heads(vc_ref[:, cols])], axis=0)
            o_ref[:, cols] = _nn(p_sc[j], v4) / jnp.where(low, dens[j][0], dens[j][1])
            lse_ref[:, cols] = jnp.where(low, mxs[j][0] + jnp.log(dens[j][0]), mxs[j][1] + jnp.log(dens[j][1]))

    cur = pl.BlockSpec((None, BLK, ATT_W), lambda r, i: (r, i, 0))
    prev = pl.BlockSpec((None, BLK, ATT_W), lambda r, i: (r, jnp.maximum(i - 1, 0), 0))
    return pl.pallas_call(
        body, name=name, grid=(dil, nb),
        in_specs=[cur, prev, cur, prev, cur],
        out_specs=[cur, cur], out_shape=[_res_shape(grp, F32)] * 2,
        scratch_shapes=[pltpu.VMEM((NQ, BLK, 4 * BLK), F32), pltpu.VMEM((NQ, BLK, 4 * BLK), BF16)],
        compiler_params=pltpu.CompilerParams(dimension_semantics=("parallel", "arbitrary")),
    )(q, k, k, v, v)


def _att_bwd(q, k, v, datt, att, lse, grp, name):
    dil = DILATIONS[grp]
    nb = T // dil // BLK
    scale = HD ** -0.5

    def body(q0_ref, q1_ref, kp_ref, kc_ref, vp_ref, vc_ref, do0_ref, do1_ref, o0_ref, o1_ref, l0_ref, l1_ref,
             dq_ref, dk_ref, dv_ref, k4_sc, v4_sc, s0_sc, s1_sc, dp0_sc, dp1_sc, p_sc, ds_sc):
        i = pl.program_id(1)
        mask_mine = _band_mask4(i > 0, True)
        mask_next = _band_mask_before(i < nb - 1)
        low = lax.broadcasted_iota(jnp.int32, (1, 128), 1) < HD
        for j in range(NQ):
            cols = slice(j * 128, (j + 1) * 128)
            k4_sc[j, :2 * BLK] = _split_heads(kp_ref[:, cols])
            k4_sc[j, 2 * BLK:] = _split_heads(kc_ref[:, cols])
            v4_sc[j, :2 * BLK] = _split_heads(vp_ref[:, cols])
            v4_sc[j, 2 * BLK:] = _split_heads(vc_ref[:, cols])
        for j in range(NQ):
            cols = slice(j * 128, (j + 1) * 128)
            s0_sc[j] = _nt(q0_ref[:, cols], k4_sc[j])
            s1_sc[j] = _nt(q1_ref[:, cols], k4_sc[j, 2 * BLK:])
            dp0_sc[j] = _nt(do0_ref[:, cols].astype(BF16), v4_sc[j])
            dp1_sc[j] = _nt(do1_ref[:, cols].astype(BF16), v4_sc[j, 2 * BLK:])
        stats = []
        for j in range(NQ):
            cols = slice(j * 128, (j + 1) * 128)
            for do_ref, o_ref, l_ref in ((do0_ref, o0_ref, l0_ref), (do1_ref, o1_ref, l1_ref)):
                prod = do_ref[:, cols].astype(F32) * o_ref[:, cols].astype(F32)
                d_all = jnp.sum(prod, axis=-1, keepdims=True)
                d_low = jnp.sum(jnp.where(low, prod, 0.0), axis=-1, keepdims=True)
                lse_t = l_ref[:, cols]
                stats.append((d_low, d_all - d_low, lse_t[:, 0:1], lse_t[:, HD:HD + 1]))
        for j in range(NQ):
            (da, db, la, lb), (da1, db1, la1, lb1) = stats[2 * j], stats[2 * j + 1]
            p0 = jnp.where(mask_mine, jnp.exp(s0_sc[j] - _per_head(4 * BLK, la, lb)), 0.0)
            ds0 = p0 * (dp0_sc[j] - _per_head(4 * BLK, da, db))
            p1 = jnp.where(mask_next, jnp.exp(s1_sc[j] - _per_head(2 * BLK, la1, lb1)), 0.0)
            ds1 = p1 * (dp1_sc[j] - _per_head(2 * BLK, da1, db1))
            p_sc[j, :BLK] = p0.astype(BF16)
            ds_sc[j, :BLK] = ds0.astype(BF16)
            p_sc[j, BLK:, 2 * BLK:] = p1.astype(BF16)
            ds_sc[j, BLK:, 2 * BLK:] = ds1.astype(BF16)
        for j in range(NQ):
            cols = slice(j * 128, (j + 1) * 128)
            dq_ref[:, cols] = _nn(ds_sc[j, :BLK], k4_sc[j]) * scale
            qq = jnp.concatenate([q0_ref[:, cols], q1_ref[:, cols]], axis=0)
            dd = jnp.concatenate([do0_ref[:, cols], do1_ref[:, cols]], axis=0).astype(BF16)
            dk_ref[:, cols] = _join_heads(_tn(ds_sc[j, :, 2 * BLK:], qq))
            dv_ref[:, cols] = _join_heads(_tn(p_sc[j, :, 2 * BLK:], dd))

    def spec(shift):
        return pl.BlockSpec((None, BLK, ATT_W), lambda r, i: (r, jnp.clip(i + shift, 0, nb - 1), 0))

    here, after, before = spec(0), spec(1), spec(-1)
    vm = pltpu.VMEM
    return pl.pallas_call(
        body, name=name, grid=(dil, nb),
        in_specs=[here, after, before, here, before, here, here, after, here, after, here, after],
        out_specs=[here] * 3, out_shape=[_res_shape(grp, F32)] * 3,
        scratch_shapes=[vm((NQ, 4 * BLK, 128), BF16), vm((NQ, 4 * BLK, 128), BF16), vm((NQ, BLK, 4 * BLK), F32),
                        vm((NQ, BLK, 2 * BLK), F32), vm((NQ, BLK, 4 * BLK), F32), vm((NQ, BLK, 2 * BLK), F32),
                        vm((NQ, 2 * BLK, 4 * BLK), BF16), vm((NQ, 2 * BLK, 4 * BLK), BF16)],
        compiler_params=pltpu.CompilerParams(dimension_semantics=("parallel", "arbitrary")),
    )(q, q, k, k, v, v, datt, datt, att, att, lse, lse)


def _att_merge(os_, lses, proj):
    nq = ATT_W // 128

    def body(o0, o1, o2, l0, l1, l2, za_ref, att_ref, lse_ref, ain_ref, sc):
        for a, ref in enumerate((o0, o1, o2, l0, l1, l2)):
            for j in range(nq):
                _from_residues(ref, slice(j * 128, (j + 1) * 128), sc, a * nq + j, DILATIONS[a % 3])
        for j in range(nq):
            cols = slice(j * 128, (j + 1) * 128)
            oa, ob, oc = (sc[a * nq + j] for a in range(3))
            la, lb, lc = (sc[(3 + a) * nq + j] for a in range(3))
            m = jnp.maximum(jnp.maximum(la, lb), lc)
            wa, wb, wc = jnp.exp(la - m), jnp.exp(lb - m), jnp.exp(lc - m)
            tot = wa + wb + wc
            att = (wa * oa + wb * ob + wc * oc) / tot
            att_ref[:, cols] = att
            lse_ref[:, cols] = m + jnp.log(tot)
            za = za_ref[:, cols]
            ain_ref[:, cols] = (att * za * _sigmoid(za)).astype(BF16)

    return pl.pallas_call(
        body, name="att_merge", grid=(T // TT,),
        in_specs=[_res_spec(g) for _ in range(2) for g in range(3)] + [_tok_spec(ATT_W, C_ZA // ATT_W)],
        out_specs=[_tok_spec(ATT_W)] * 3,
        out_shape=[S((T, ATT_W), F32), S((T, ATT_W), F32), S((T, ATT_W), BF16)],
        scratch_shapes=[pltpu.VMEM((6 * nq, TT, 128), F32)],
        compiler_params=pltpu.CompilerParams(dimension_semantics=("arbitrary",)),
    )(*os_, *lses, proj)


def _att_gate_bwd(dain, att, lse, proj, dproj):
    nq = ATT_W // 128

    def body(d_ref, att_ref, lse_ref, za_ref, buf_ref, dza_ref, da0, da1, da2, at1, at2, ls1, ls2, sc):
        del buf_ref
        for j in range(nq):
            cols = slice(j * 128, (j + 1) * 128)
            za = za_ref[:, cols]
            sg = _sigmoid(za)
            d = d_ref[:, cols].astype(F32)
            att_ = att_ref[:, cols]
            dza_ref[:, cols] = (d * att_ * sg * (1.0 + za * (1.0 - sg))).astype(BF16)
            sc[j] = d * za * sg
            sc[nq + j] = att_
            sc[2 * nq + j] = lse_ref[:, cols]
        for j in range(nq):
            cols = slice(j * 128, (j + 1) * 128)
            for grp, dst in enumerate((da0, da1, da2)):
                _to_residues(sc, j, dst, DILATIONS[grp], cols)
            for grp, dst in ((1, at1), (2, at2)):
                _to_residues(sc, nq + j, dst, DILATIONS[grp], cols)
            for grp, dst in ((1, ls1), (2, ls2)):
                _to_residues(sc, 2 * nq + j, dst, DILATIONS[grp], cols)

    res = (0, 1, 2, 1, 2, 1, 2)
    return pl.pallas_call(
        body, name="att_gate_bwd", grid=(T // TT,),
        in_specs=[_tok_spec(ATT_W)] * 3 + [_tok_spec(ATT_W, C_ZA // ATT_W), pl.BlockSpec(memory_space=pl.ANY)],
        out_specs=[_tok_spec(ATT_W, C_ZA // ATT_W)] + [_res_spec(g) for g in res],
        out_shape=[S(dproj.shape, dproj.dtype)] + [_res_shape(g, BF16) for g in res[:5]]
        + [_res_shape(g, F32) for g in res[5:]],
        input_output_aliases={4: 0},
        scratch_shapes=[pltpu.VMEM((3 * nq, TT, 128), F32)],
        compiler_params=pltpu.CompilerParams(dimension_semantics=("arbitrary",)),
    )(dain, att, lse, proj, dproj)


def _split3(v):
    hi = v.astype(BF16)
    r1 = v - hi.astype(F32)
    mid = r1.astype(BF16)
    lo = (r1 - mid.astype(F32)).astype(BF16)
    return hi, mid, lo


def _chunk_scores(qt, kt, q_ref, k_ref, h):
    cols = slice(h * GDK, (h + 1) * GDK)
    own = jnp.sum(q_ref[:, cols] * (GDK ** -0.5) * k_ref[:, cols], axis=-1, keepdims=True)
    row = lax.broadcasted_iota(jnp.int32, (GLA_C, GLA_C), 0)
    col = lax.broadcasted_iota(jnp.int32, (GLA_C, GLA_C), 1)
    a = _nt(qt.astype(BF16), kt.astype(BF16))
    return jnp.where(col < row, a, jnp.where(col == row, own, 0.0))


def _tri_sum(v, upper):
    n = v.shape[0]
    row = lax.broadcasted_iota(jnp.int32, (n, n), 0)
    col = lax.broadcasted_iota(jnp.int32, (n, n), 1)
    tri = jnp.where(col >= row if upper else col <= row, 1.0, 0.0).astype(BF16)
    hi, mid, lo = _split3(v)
    return _nn(tri, hi) + _nn(tri, mid) + _nn(tri, lo)


def _gla_gates(glr_ref, w2_ref, b_ref):
    logit = _nn(glr_ref[...].astype(BF16), w2_ref[...]) + b_ref[...]
    lg = (jnp.minimum(logit, 0.0) - jnp.log(1.0 + jnp.exp(-jnp.abs(logit)))) * (1.0 / GLA_TAU)
    return logit, _tri_sum(lg, upper=False)


def _gla_head(cum, q_ref, k_ref, h):
    cols = slice(h * GDK, (h + 1) * GDK)
    b = cum[:, cols]
    last = b[GLA_C - 1:GLA_C, :]
    e_pos = jnp.exp(b)
    e_neg = jnp.exp(-b)
    e_end = jnp.exp(last - b)
    qt = q_ref[:, cols] * (GDK ** -0.5) * e_pos
    kt = k_ref[:, cols] * e_neg
    kh = k_ref[:, cols] * e_end
    return b, last, e_pos, e_neg, e_end, qt, kt, kh


def _causal(n):
    return lax.broadcasted_iota(jnp.int32, (n, n), 1) <= lax.broadcasted_iota(jnp.int32, (n, n), 0)


def _gla_fwd(proj, w2p, bg, gn):
    nc = T // GLA_C

    def body(q_ref, k_ref, v_ref, glr_ref, zg_ref, w2_ref, b_ref, gn_ref, o_ref, bin_ref, st_ref, state):
        @pl.when(pl.program_id(0) == 0)
        def _():
            state[...] = jnp.zeros_like(state)

        _, cum = _gla_gates(glr_ref, w2_ref, b_ref)
        for h in range(GH):
            _, last, _, _, _, qt, kt, kh = _gla_head(cum, q_ref, k_ref, h)
            vcols = slice(h * GDV, (h + 1) * GDV)
            st = state[h]
            st_ref[0, h] = st
            v = v_ref[:, vcols].astype(BF16)
            qb = qt.astype(BF16)
            a = _chunk_scores(qt, kt, q_ref, k_ref, h)
            o = _nt(qb, st.astype(BF16)) + _nn(a.astype(BF16), v)
            state[h] = st * jnp.exp(last) + _tn(v, kh.astype(BF16))
            o_ref[:, vcols] = o
            r = lax.rsqrt(jnp.mean(o * o, axis=-1, keepdims=True) + EPS)
            zg = zg_ref[:, vcols]
            bin_ref[:, vcols] = (o * r * gn_ref[...] * zg * _sigmoid(zg)).astype(BF16)

    row = lambda width, cblk: pl.BlockSpec((GLA_C, width), functools.partial(lambda i, c: (i, c), c=cblk))
    full = lambda a: pl.BlockSpec(a.shape, functools.partial(lambda i, nd: (0,) * nd, nd=a.ndim))
    return pl.pallas_call(
        body, name="gla_fwd", grid=(nc,),
        in_specs=[row(512, C_QG // 512), row(512, C_KG // 512), row(1024, C_VG // 1024), row(GLR_W, C_GLR // GLR_W),
                  row(1024, C_ZG // 1024), full(w2p), full(bg), full(gn)],
        out_specs=[pl.BlockSpec((GLA_C, GH * GDV), lambda i: (i, 0)), pl.BlockSpec((GLA_C, GH * GDV), lambda i: (i, 0)),
                   pl.BlockSpec((1, GH, GDV, GDK), lambda i: (i, 0, 0, 0))],
        out_shape=[S((T, GH * GDV), F32), S((T, GH * GDV), BF16), S((nc, GH, GDV, GDK), F32)],
        scratch_shapes=[pltpu.VMEM((GH, GDV, GDK), F32)],
        compiler_params=pltpu.CompilerParams(dimension_semantics=("arbitrary",)),
    )(proj, proj, proj, proj, proj, w2p, bg, gn)


def _gla_bwd(proj, w2p, bg, gn, o_gla, states, dbin, dproj):
    nc = T // GLA_C

    def body(q_ref, k_ref, v_ref, glr_ref, zg_ref, w2_ref, b_ref, gn_ref, o_ref, st_ref, dbin_ref, buf_ref,
             out_ref, dw2_ref, dbg_ref, dgn_ref, dstate, dlogit):
        del buf_ref
        dq_ref = out_ref.at[:, C_QG:C_KG]
        dk_ref = out_ref.at[:, C_KG:C_VG]
        dv_ref = out_ref.at[:, C_VG:C_ZG]
        dzg_ref = out_ref.at[:, C_ZG:C_GLR]
        dglr_ref = out_ref.at[:, C_GLR:C_GLR + GLR_W]
        first = pl.program_id(0) == 0

        @pl.when(first)
        def _():
            dstate[...] = jnp.zeros_like(dstate)

        logit, cum = _gla_gates(glr_ref, w2_ref, b_ref)
        is_last = lax.broadcasted_iota(jnp.int32, (GLA_C, 1), 0) == GLA_C - 1
        dgn = jnp.zeros((1, GDV), F32)
        for h in range(GH):
            _, last, e_pos, e_neg, e_end, qt, kt, kh = _gla_head(cum, q_ref, k_ref, h)
            cols = slice(h * GDK, (h + 1) * GDK)
            vcols = slice(h * GDV, (h + 1) * GDV)
            o = o_ref[:, vcols]
            r = lax.rsqrt(jnp.mean(o * o, axis=-1, keepdims=True) + EPS)
            zg = zg_ref[:, vcols]
            sg = _sigmoid(zg)
            db_ = dbin_ref[:, vcols].astype(F32)
            dlin = db_ * zg * sg
            dzg_ref[:, vcols] = (db_ * (o * r * gn_ref[...]) * sg * (1.0 + zg * (1.0 - sg))).astype(BF16)
            u = dlin * gn_ref[...]
            do = (r * u - o * (r * r * r) * jnp.mean(u * o, axis=-1, keepdims=True)).astype(BF16)
            dgn = dgn + jnp.sum(dlin * o * r, axis=0, keepdims=True)
            st = st_ref[0, h]
            dst = dstate[h]
            v = v_ref[:, vcols].astype(BF16)
            qb, kb, khb = qt.astype(BF16), kt.astype(BF16), kh.astype(BF16)
            dstb = dst.astype(BF16)
            causal = _causal(GLA_C)
            a = _chunk_scores(qt, kt, q_ref, k_ref, h).astype(BF16)
            da = jnp.where(causal, _nt(do, v), 0.0).astype(BF16)
            dqt = _nn(do, st.astype(BF16)) + _nn(da, kb)
            dkt = _tn(da, qb)
            dkh = _nn(v, dstb)
            dv_ref[:, vcols] = (_tn(a, do) + _nt(khb, dstb)).astype(BF16)
            lam = jnp.exp(last)
            dlam = jnp.sum(dst * st, axis=0, keepdims=True)
            dstate[h] = dst * lam + _tn(do, qb)
            dq_ref[:, cols] = (dqt * e_pos * (GDK ** -0.5)).astype(BF16)
            dk_ref[:, cols] = (dkt * e_neg + dkh * e_end).astype(BF16)
            dkh_kh = dkh * kh
            dcum = dqt * qt - dkt * kt - dkh_kh
            dlast = jnp.sum(dkh_kh, axis=0, keepdims=True) + dlam * lam
            dcum = jnp.where(is_last, dcum + dlast, dcum)
            dlg = _tri_sum(dcum, upper=True)
            dlogit[:, cols] = dlg * (1.0 / GLA_TAU) * (1.0 - _sigmoid(logit[:, cols]))

        dl = dlogit[...]
        dlb = dl.astype(BF16)
        dglr_ref[...] = _nt(dlb, w2_ref[...]).astype(BF16)
        dw2 = _tn(glr_ref[...].astype(BF16), dlb)
        dbg = jnp.sum(dl, axis=0, keepdims=True)

        @pl.when(first)
        def _():
            dw2_ref[...] = dw2
            dbg_ref[...] = dbg
            dgn_ref[...] = dgn

        @pl.when(jnp.logical_not(first))
        def _():
            dw2_ref[...] += dw2
            dbg_ref[...] += dbg
            dgn_ref[...] += dgn

    rev = lambda i: nc - 1 - i
    row = lambda width, cblk: pl.BlockSpec((GLA_C, width), functools.partial(lambda i, c: (rev(i), c), c=cblk))
    full = lambda a: pl.BlockSpec(a.shape, functools.partial(lambda i, nd: (0,) * nd, nd=a.ndim))
    keep = lambda shape: pl.BlockSpec(shape, functools.partial(lambda i, nd: (0,) * nd, nd=len(shape)))
    return pl.pallas_call(
        body, name="gla_bwd", grid=(nc,),
        in_specs=[row(512, C_QG // 512), row(512, C_KG // 512), row(1024, C_VG // 1024), row(GLR_W, C_GLR // GLR_W),
                  row(1024, C_ZG // 1024), full(w2p), full(bg), full(gn), row(GH * GDV, 0),
                  pl.BlockSpec((1, GH, GDV, GDK), lambda i: (rev(i), 0, 0, 0)), row(GH * GDV, 0),
                  pl.BlockSpec(memory_space=pl.ANY)],
        out_specs=[row(GLA_GROUP_W, 0), keep((GLR_W, 512)), keep((1, 512)), keep((1, GDV))],
        out_shape=[S(dproj.shape, dproj.dtype), S((GLR_W, 512), F32), S((1, 512), F32), S((1, GDV), F32)],
        input_output_aliases={11: 0},
        scratch_shapes=[pltpu.VMEM((GH, GDV, GDK), F32), pltpu.VMEM((GLA_C, GH * GDK), F32)],
        compiler_params=pltpu.CompilerParams(dimension_semantics=("arbitrary",)),
    )(proj, proj, proj, proj, proj, w2p, bg, gn, o_gla, states, dbin, dproj)


RT = 512


def _rowchain(body, name, ins, outs, scratch=()):
    in_specs, args = [], []
    for spec in ins:
        if spec[0] == "tok":
            _, arr, width, cblk = spec
            in_specs.append(pl.BlockSpec((RT, width), functools.partial(lambda i, c: (i, c), c=cblk)))
        else:
            arr = spec[1]
            in_specs.append(pl.BlockSpec(arr.shape, functools.partial(lambda i, nd: (0,) * nd, nd=arr.ndim)))
        args.append(arr)
    out_specs, out_shape = [], []
    for spec in outs:
        if spec[0] == "tok":
            _, shape, dtype, width, cblk = spec
            out_specs.append(pl.BlockSpec((RT, width), functools.partial(lambda i, c: (i, c), c=cblk)))
        else:
            _, shape, dtype = spec
            out_specs.append(pl.BlockSpec(shape, functools.partial(lambda i, nd: (0,) * nd, nd=len(shape))))
        out_shape.append(S(shape, dtype))
    return pl.pallas_call(
        body, name=name, grid=(T // RT,), in_specs=in_specs, out_specs=out_specs, out_shape=out_shape,
        scratch_shapes=list(scratch), compiler_params=pltpu.CompilerParams(dimension_semantics=("arbitrary",)),
    )(*args)


def _tok(arr, width=None, cblk=0):
    return ("tok", arr, arr.shape[1] if width is None else width, cblk)


def _tok_out(dtype, width=D):
    return ("tok", (T, width), dtype, width, 0)


def _branches_fwd(ain, bin_, proj, x, w_att, w_gla, w_out):
    def body(ain_ref, bin_ref, g_ref, x_ref, wa_ref, wg_ref, wo_ref, ya_ref, yb_ref, y_ref, x1_ref):
        ya = _nn(ain_ref[...], wa_ref[...]).astype(BF16)
        yb = _nn(bin_ref[...], wg_ref[...]).astype(BF16)
        ya_ref[...] = ya
        yb_ref[...] = yb
        y = (_sigmoid(g_ref[:, :D]) * ya.astype(F32) + _sigmoid(g_ref[:, D:]) * yb.astype(F32)).astype(BF16)
        y_ref[...] = y
        x1_ref[...] = x_ref[...] + _nn(y, wo_ref[...])

    return _rowchain(body, "branches_fwd",
                     [_tok(ain), _tok(bin_), _tok(proj, 2 * D, C_GA // (2 * D)), _tok(x), ("all", w_att),
                      ("all", w_gla), ("all", w_out)],
                     [_tok_out(BF16), _tok_out(BF16), _tok_out(BF16), _tok_out(F32)])


def _accumulate(ref, part, first):
    @pl.when(first)
    def _():
        ref[...] = part

    @pl.when(jnp.logical_not(first))
    def _():
        ref[...] += part


def _ple_loss(x1, p, target, g2, w_pg, w_ple):
    def body(x1_ref, p_ref, t_ref, g_ref, wpg_ref, wple_ref, n2_ref, loss_ref, dout_ref, du_ref, dwple_ref, acc):
        first = pl.program_id(0) == 0
        x1 = x1_ref[...]
        r = lax.rsqrt(jnp.mean(x1 * x1, axis=-1, keepdims=True) + EPS)
        n2 = (x1 * r * g_ref[...]).astype(BF16)
        n2_ref[...] = n2
        pg = _sigmoid(_nn(n2, wpg_ref[...]))
        pb = p_ref[...].astype(BF16)
        e_ = _nn(pb, wple_ref[...])
        diff = x1 + e_ * pg - t_ref[...]
        _accumulate(acc, jnp.sum(diff * diff, axis=0, keepdims=True), first)
        dout = diff * (1.0 / D)
        dout_ref[...] = dout
        du_ref[...] = (dout * e_ * pg * (1.0 - pg)).astype(BF16)
        _accumulate(dwple_ref, _tn(pb, (dout * pg).astype(BF16)), first)
        loss_ref[...] = jnp.zeros((1, 128), F32) + jnp.sum(acc[...], axis=-1, keepdims=True) * (0.5 / D)

    return _rowchain(body, "ple_loss", [_tok(x1), _tok(p), _tok(target), ("all", g2), ("all", w_pg), ("all", w_ple)],
                     [_tok_out(BF16), ("acc", (1, 128), F32), _tok_out(F32), _tok_out(BF16), ("acc", (PLE, D), F32)],
                     scratch=[pltpu.VMEM((1, D), F32)])


def _ple_bwd(du, n2, y, x1, dout, g2, w_pg, w_out):
    def body(du_ref, n2_ref, y_ref, x1_ref, dout_ref, g_ref, wpg_ref, wo_ref, dx_ref, dy_ref, dg_ref, dwpg_ref,
             dwo_ref):
        first = pl.program_id(0) == 0
        x1 = x1_ref[...]
        r = lax.rsqrt(jnp.mean(x1 * x1, axis=-1, keepdims=True) + EPS)
        du_ = du_ref[...]
        dn = _nt(du_, wpg_ref[...])
        u = dn * g_ref[...]
        dx = dout_ref[...] + r * u - x1 * (r * r * r) * jnp.mean(u * x1, axis=-1, keepdims=True)
        dxb = dx.astype(BF16)
        dx_ref[...] = dx
        dy_ref[...] = _nt(dxb, wo_ref[...]).astype(BF16)
        _accumulate(dg_ref, jnp.sum(dn * x1 * r, axis=0, keepdims=True), first)
        _accumulate(dwpg_ref, _tn(n2_ref[...], du_), first)
        _accumulate(dwo_ref, _tn(y_ref[...], dxb), first)

    return _rowchain(body, "ple_bwd",
                     [_tok(du), _tok(n2), _tok(y), _tok(x1), _tok(dout), ("all", g2), ("all", w_pg), ("all", w_out)],
                     [_tok_out(F32), _tok_out(BF16), ("acc", (1, D), F32), ("acc", (D, D), F32), ("acc", (D, D), F32)])


def _branches_bwd(dy, ya, yb, ain, bin_, proj, w_att, w_gla):
    def body(dy_ref, ya_ref, yb_ref, ain_ref, bin_ref, g_ref, wa_ref, wg_ref, dg_ref, dain_ref, dbin_ref,
             dwa_ref, dwg_ref):
        first = pl.program_id(0) == 0
        dy_ = dy_ref[...].astype(F32)
        sa, sb = _sigmoid(g_ref[:, :D]), _sigmoid(g_ref[:, D:])
        dg_ref[:, :D] = (dy_ * ya_ref[...].astype(F32) * sa * (1.0 - sa)).astype(BF16)
        dg_ref[:, D:] = (dy_ * yb_ref[...].astype(F32) * sb * (1.0 - sb)).astype(BF16)
        dya = (dy_ * sa).astype(BF16)
        dyb = (dy_ * sb).astype(BF16)
        dain_ref[...] = _nt(dya, wa_ref[...]).astype(BF16)
        dbin_ref[...] = _nt(dyb, wg_ref[...]).astype(BF16)
        _accumulate(dwa_ref, _tn(ain_ref[...], dya), first)
        _accumulate(dwg_ref, _tn(bin_ref[...], dyb), first)

    gates = C_GA // (2 * D)
    return _rowchain(body, "branches_bwd",
                     [_tok(dy), _tok(ya), _tok(yb), _tok(ain), _tok(bin_), _tok(proj, 2 * D, gates), ("all", w_att),
                      ("all", w_gla)],
                     [("tok", (T, NCOL), BF16, 2 * D, gates), _tok_out(BF16, ATT_W), _tok_out(BF16),
                      ("acc", (ATT_W, D), F32), ("acc", (D, D), F32)])


def _peer(k):
    x, y, c = lax.axis_index("x"), lax.axis_index("y"), lax.axis_index("c")
    return (x ^ ((k >> 2) & 1), y ^ ((k >> 1) & 1), c ^ (k & 1))


def _my_index():
    return 4 * lax.axis_index("x") + 2 * lax.axis_index("y") + lax.axis_index("c")


def _peer_index(k):
    px, py, pc = _peer(k)
    return 4 * px + 2 * py + pc


def _pairwise_plan(src_of, dst_of, landed_of, own_src, own_dst):
    def plan(ins, outs, send, recv, local):
        n = len(ins)

        def own():
            return [pltpu.make_async_copy(own_src(ins[a]), own_dst(outs[a]), local.at[a]) for a in range(n)]

        def remote(k, a, src, dst):
            return pltpu.make_async_remote_copy(src_ref=src, dst_ref=dst, send_sem=send.at[k - 1, a],
                                                recv_sem=recv.at[k - 1, a], device_id=_peer(k), device_id_type=MESH)

        def sent():
            return [remote(k, a, src_of(ins[a], k), dst_of(outs[a])) for k in range(1, NDEV) for a in range(n)]

        def start():
            for cp in own() + sent():
                cp.start()

        def finish():
            for k in range(1, NDEV):
                for a in range(n):
                    remote(k, a, own_src(ins[a]), landed_of(outs[a], k)).wait_recv()
            for cp in sent():
                cp.wait_send()
            for cp in own():
                cp.wait()

        return start, finish

    return plan


def _pairwise_sems(n):
    return [pltpu.SemaphoreType.DMA((NDEV - 1, n)), pltpu.SemaphoreType.DMA((NDEV - 1, n)),
            pltpu.SemaphoreType.DMA((n,))]


def _gather_side(arrs):
    plan = _pairwise_plan(src_of=lambda i, k: i, dst_of=lambda o: o.at[_my_index()],
                          landed_of=lambda o, k: o.at[_peer_index(k)],
                          own_src=lambda i: i, own_dst=lambda o: o.at[_my_index()])
    return dict(arrs=arrs, out_shape=[S((NDEV,) + a.shape, a.dtype) for a in arrs],
                scratch=_pairwise_sems(len(arrs)), plan=plan)


def _exchange_side(arrs):
    plan = _pairwise_plan(src_of=lambda i, k: i.at[_peer_index(k)], dst_of=lambda o: o.at[_my_index()],
                          landed_of=lambda o, k: o.at[_peer_index(k)],
                          own_src=lambda i: i.at[_my_index()], own_dst=lambda o: o.at[_my_index()])
    return dict(arrs=arrs, out_shape=[S(a.shape, a.dtype) for a in arrs], scratch=_pairwise_sems(len(arrs)), plan=plan)


def _comm_call(side, name):
    n = len(side["arrs"])

    def body(*refs):
        start, finish = side["plan"](refs[:n], refs[n:2 * n], *refs[2 * n:])
        start()
        finish()

    hbm = pl.BlockSpec(memory_space=pl.ANY)
    return pl.pallas_call(body, name=name, in_specs=[hbm] * n, out_specs=[hbm] * n, out_shape=side["out_shape"],
                          scratch_shapes=side["scratch"])(*side["arrs"])


def _all_gather_by_chip(arrs, name):
    n = len(arrs)

    def body(*refs):
        ins, outs = refs[:n], refs[n:2 * n]
        send, recv, local = refs[2 * n:]
        x, y, c = lax.axis_index("x"), lax.axis_index("y"), lax.axis_index("c")
        me, sibling = (x, y, c), (x, y, 1 - c)
        chips = [(1 - x, y), (x, 1 - y), (1 - x, 1 - y)]

        def copy(k, a, block, to, src=None):
            px, py, pc = block
            slot = outs[a].at[4 * px + 2 * py + pc]
            return pltpu.make_async_remote_copy(
                src_ref=slot if src is None else src, dst_ref=slot, send_sem=send.at[k, a], recv_sem=recv.at[k, a],
                device_id=to, device_id_type=MESH)

        north = c == 1
        via = (jnp.where(north, 1 - x, x), jnp.where(north, y, 1 - y))
        onward = (jnp.where(north, x, 1 - x), jnp.where(north, 1 - y, y), c)
        mine = [pltpu.make_async_copy(ins[a], outs[a].at[4 * x + 2 * y + c], local.at[a]) for a in range(n)]
        first = []
        for a in range(n):
            first.append(copy(0, a, me, sibling, src=ins[a]))
            first += [copy(1 + j, a, me, (*chips[j], c), src=ins[a]) for j in range(2)]
        for cp in mine + first:
            cp.start()
        passed = []
        for j in range(2):
            for a in range(n):
                copy(1 + j, a, (*chips[j], c), me).wait_recv()
                passed.append(copy(4 + j, a, (*chips[j], c), sibling))
                passed[-1].start()
        for a in range(n):
            passed.append(copy(3, a, (*via, c), onward))
            passed[-1].start()
        for a in range(n):
            copy(3, a, (*chips[2], c), me).wait_recv()
            passed.append(copy(6, a, (*chips[2], c), sibling))
            passed[-1].start()
        for a in range(n):
            copy(0, a, sibling, me).wait_recv()
        for j, chip in enumerate(chips):
            for a in range(n):
                copy(4 + j, a, (*chip, 1 - c), me).wait_recv()
        for cp in first + passed:
            cp.wait_send()
        for cp in mine:
            cp.wait()

    hbm = pl.BlockSpec(memory_space=pl.ANY)
    return pl.pallas_call(
        body, name=name, in_specs=[hbm] * n, out_specs=[hbm] * n,
        out_shape=[S((NDEV,) + a.shape, a.dtype) for a in arrs],
        scratch_shapes=[pltpu.SemaphoreType.DMA((NDEV - 1, n)), pltpu.SemaphoreType.DMA((NDEV - 1, n)),
                        pltpu.SemaphoreType.DMA((n,))],
    )(*arrs)


NCHIP = 4


def _sibling_sum(src, name, tc=256):
    _, rows, cols = src.shape
    assert cols % tc == 0

    def body(src_ref, got_ref, out_ref, a_buf, b_buf, o_buf, send, recv, local):
        x, y, c = lax.axis_index("x"), lax.axis_index("y"), lax.axis_index("c")
        copies = [pltpu.make_async_remote_copy(
            src_ref=src_ref.at[2 * q + (1 - c)], dst_ref=got_ref.at[q], send_sem=send.at[q], recv_sem=recv.at[q],
            device_id=(x, y, 1 - c), device_id_type=MESH) for q in range(NCHIP)]
        for cp in copies:
            cp.start()
        for q in range(NCHIP):
            copies[q].wait_recv()
            for t in range(cols // tc):
                tile = pl.ds(t * tc, tc)
                loads = [pltpu.make_async_copy(src_ref.at[2 * q + c, :, tile], a_buf, local.at[0]),
                         pltpu.make_async_copy(got_ref.at[q, :, tile], b_buf, local.at[1])]
                for cp in loads:
                    cp.start()
                for cp in loads:
                    cp.wait()
                o_buf[...] = (a_buf[...].astype(F32) + b_buf[...].astype(F32)).astype(BF16)
                store = pltpu.make_async_copy(o_buf, out_ref.at[q, :, tile], local.at[2])
                store.start()
                store.wait()
        for cp in copies:
            cp.wait_send()

    hbm = pl.BlockSpec(memory_space=pl.ANY)
    block = S((NCHIP, rows, cols), BF16)
    return pl.pallas_call(
        body, name=name, in_specs=[hbm], out_specs=[hbm, hbm], out_shape=[block, block],
        scratch_shapes=[pltpu.VMEM((rows, tc), BF16)] * 3
        + [pltpu.SemaphoreType.DMA((NCHIP,)), pltpu.SemaphoreType.DMA((NCHIP,)), pltpu.SemaphoreType.DMA((3,))],
    )(src)[1]


def _chips_side(arrs):
    def plan(ins, outs, send, recv, local):
        n = len(ins)

        def places():
            x, y, c = lax.axis_index("x"), lax.axis_index("y"), lax.axis_index("c")
            return 2 * x + y, c, [(1 - x, y), (x, 1 - y), (1 - x, 1 - y)]

        def own():
            here, _, _ = places()
            return [pltpu.make_async_copy(ins[a].at[here], outs[a].at[here], local.at[a]) for a in range(n)]

        def remote(j, a, src_slot, dst_slot):
            _, c, chips = places()
            cx, cy = chips[j]
            return pltpu.make_async_remote_copy(
                src_ref=ins[a].at[src_slot], dst_ref=outs[a].at[dst_slot], send_sem=send.at[j, a],
                recv_sem=recv.at[j, a], device_id=(cx, cy, c), device_id_type=MESH)

        def sent():
            here, _, chips = places()
            return [remote(j, a, 2 * cx + cy, here) for j, (cx, cy) in enumerate(chips) for a in range(n)]

        def start():
            for cp in own() + sent():
                cp.start()

        def finish():
            here, _, chips = places()
            for j, (cx, cy) in enumerate(chips):
                for a in range(n):
                    remote(j, a, here, 2 * cx + cy).wait_recv()
            for cp in sent():
                cp.wait_send()
            for cp in own():
                cp.wait()

        return start, finish

    n = len(arrs)
    return dict(arrs=arrs, out_shape=[S(a.shape, a.dtype) for a in arrs],
                scratch=[pltpu.SemaphoreType.DMA((NCHIP - 1, n)), pltpu.SemaphoreType.DMA((NCHIP - 1, n)),
                         pltpu.SemaphoreType.DMA((n,))], plan=plan)


def _adamw_shards(parts, places):
    n_src = len(parts)

    def body(*refs):
        srcs, rest = refs[:n_src], refs[n_src:]
        for j, (src, rows, cols, _) in enumerate(places):
            w_ref, m_ref, v_ref = rest[3 * j:3 * j + 3]
            outs = rest[3 * len(places) + 4 * j:3 * len(places) + 4 * j + 4]
            p_ref = srcs[src]
            g = p_ref[0, rows, cols].astype(F32)
            for s in range(1, p_ref.shape[0]):
                g = g + p_ref[s, rows, cols].astype(F32)
            delta, m_new, v_new = _adam_math(g, w_ref[0], m_ref[0], v_ref[0])
            for ref, val in zip(outs, (g, delta, m_new, v_new)):
                ref[0] = val

    flat = [a for place in places for a in place[3]]
    return pl.pallas_call(
        body, name="adam_shards",
        out_shape=[S(place[3][0].shape, F32) for place in places for _ in range(4)],
    )(*parts, *flat)


def _adam_math(g, w, m, v):
    c1 = 1.0 - ADAM_B1 ** ADAM_STEP
    c2 = 1.0 - ADAM_B2 ** ADAM_STEP
    m_new = ADAM_B1 * m + (1.0 - ADAM_B1) * g
    v_new = ADAM_B2 * v + (1.0 - ADAM_B2) * (g * g)
    return -ADAM_LR * ((m_new / c1) / (jnp.sqrt(v_new / c2) + ADAM_EPS) + ADAM_WD * w), m_new, v_new


def _adamw_small(parts, params, loss_parts):
    n = len(params)

    def body(*refs):
        p_refs, rest = refs[:n], refs[n + 1:]
        total = refs[n][0]
        for s in range(1, NDEV):
            total = total + refs[n][s]
        refs[-1][...] = total
        for j in range(n):
            w_ref, m_ref, v_ref = rest[3 * j:3 * j + 3]
            g_ref, d_ref, mo_ref, vo_ref = rest[3 * n + 4 * j:3 * n + 4 * j + 4]
            width = w_ref.shape[1]
            g = p_refs[j][0]
            for s in range(1, NDEV):
                g = g + p_refs[j][s]
            g = g[:, :width]
            delta, m_new, v_new = _adam_math(g, w_ref[...], m_ref[...], v_ref[...])
            g_ref[...] = g
            d_ref[...] = delta
            mo_ref[...] = m_new
            vo_ref[...] = v_new

    flat = [a for group in params for a in group]
    return pl.pallas_call(
        body, name="adam_small",
        out_shape=[S(group[0].shape, F32) for group in params for _ in range(4)] + [S((1, 128), F32)],
    )(*parts, loss_parts, *flat)


def _adamw_rows(parts, w, m, v, name, tc=128):
    rows, _, cols = w.shape
    nparts = parts.shape[0]
    nsteps = cols // tc

    def body(p_ref, w_hbm, m_hbm, v_hbm, g_hbm, d_hbm, mo_hbm, vo_hbm, inbuf, outbuf, insem, outsem):
        i = pl.program_id(0)
        slot = i & 1

        def view(ref, step):
            return ref.at[:, 0, pl.ds(pl.multiple_of(step * tc, tc), tc)]

        def fetch(step, sl):
            return [pltpu.make_async_copy(view(src, step), inbuf.at[sl, k], insem.at[sl, k])
                    for k, src in enumerate((w_hbm, m_hbm, v_hbm))]

        def write(step, sl):
            return [pltpu.make_async_copy(outbuf.at[sl, k], view(dst, step), outsem.at[sl, k])
                    for k, dst in enumerate((g_hbm, d_hbm, mo_hbm, vo_hbm))]

        @pl.when(i == 0)
        def _():
            for cp in fetch(0, 0):
                cp.start()

        @pl.when(i + 1 < nsteps)
        def _():
            for cp in fetch(i + 1, 1 - slot):
                cp.start()

        for cp in fetch(i, slot):
            cp.wait()

        @pl.when(i >= 2)
        def _():
            for cp in write(i - 2, slot):
                cp.wait()

        g = p_ref[0].astype(F32)
        for s in range(1, nparts):
            g = g + p_ref[s].astype(F32)
        g = g[:rows]
        delta, m_new, v_new = _adam_math(g, inbuf[slot, 0], inbuf[slot, 1], inbuf[slot, 2])
        for k, val in enumerate((g, delta, m_new, v_new)):
            outbuf[slot, k] = val
        for cp in write(i, slot):
            cp.start()

        @pl.when(i == nsteps - 1)
        def _():
            for cp in write(i - 1, 1 - slot) + write(i, slot):
                cp.wait()

    hbm = pl.BlockSpec(memory_space=pl.ANY)
    assert nsteps >= 2
    return pl.pallas_call(
        body, name=name, grid=(nsteps,),
        in_specs=[pl.BlockSpec((nparts, parts.shape[1], tc), lambda i: (0, 0, i)), hbm, hbm, hbm],
        out_specs=[hbm] * 4, out_shape=[S((rows, 1, cols), F32)] * 4,
        scratch_shapes=[pltpu.VMEM((2, 3, rows, tc), F32), pltpu.VMEM((2, 4, rows, tc), F32),
                        pltpu.SemaphoreType.DMA((2, 3)), pltpu.SemaphoreType.DMA((2, 4))],
        compiler_params=pltpu.CompilerParams(dimension_semantics=("arbitrary",)),
    )(parts, w, m, v)


SLAB = 1296
REMAP_RUNS = 3
_PIECES = ((O_QA, O_ZA, C_QA), (O_ZA, O_QG, C_ZA), (O_QG, O_GLR, C_QG), (O_GLR, O_ZG, C_GLR), (O_ZG, O_GA, C_ZG),
           (O_GA, O_END, C_GA))


def _slab_row_of_aligned(a):
    for o0, o1, a0 in _PIECES:
        if a0 <= a < a0 + o1 - o0:
            c = o0 + a - a0
            return SLAB * (c // W_IN_SHARD) + c % W_IN_SHARD
    return -1


def _aligned_row_of_slab(r):
    d, l = divmod(r, SLAB)
    if l >= W_IN_SHARD:
        return -1
    c = d * W_IN_SHARD + l
    for o0, o1, a0 in _PIECES:
        if o0 <= c < o1:
            return a0 + c - o0
    raise AssertionError(c)


def _remap_table(row_of, n_out, block, n_src):
    win = block + 16
    table = []
    for b in range(n_out // block):
        runs = []
        for i in range(block):
            s = row_of(b * block + i)
            if s < 0:
                continue
            if runs and runs[-1][0] + runs[-1][2] == s and runs[-1][1] + runs[-1][2] == i:
                runs[-1][2] += 1
            else:
                runs.append([s, i, 1])
        assert len(runs) <= REMAP_RUNS, (b, runs)
        row = []
        for s, i, n in runs:
            w = min(s // 16 * 16, n_src - win)
            assert 0 <= s - w and s - w + n <= win
            row += [w, s - w, i, n]
        table.append(row + [0] * (4 * REMAP_RUNS - len(row)))
    return table


def _remap_rows(src, row_of, n_out, block, name):
    n_src, cols = src.shape
    nb, win = n_out // block, block + 16
    table = jnp.asarray(_remap_table(row_of, n_out, block, n_src), jnp.int32)

    def body(t_ref, src_hbm, o_ref, buf, acc, sem):
        b = pl.program_id(0)
        slot = b & 1

        def each_run(step, sl, act):
            for k in range(REMAP_RUNS):
                @pl.when(t_ref[step, 4 * k + 3] > 0)
                def _():
                    start = pl.multiple_of(t_ref[step, 4 * k], 16)
                    act(pltpu.make_async_copy(src_hbm.at[pl.ds(start, win)], buf.at[sl, k], sem.at[sl, k]))

        @pl.when(b == 0)
        def _():
            each_run(0, 0, lambda cp: cp.start())

        @pl.when(b + 1 < nb)
        def _():
            each_run(b + 1, 1 - slot, lambda cp: cp.start())

        each_run(b, slot, lambda cp: cp.wait())
        acc[...] = jnp.zeros_like(acc)
        row = lax.broadcasted_iota(jnp.int32, (block, win), 0)
        col = lax.broadcasted_iota(jnp.int32, (block, win), 1)
        for k in range(REMAP_RUNS):
            shift, first, count = (t_ref[b, 4 * k + j] for j in (1, 2, 3))

            @pl.when(count > 0)
            def _():
                pick = (col == row - first + shift) & (row >= first) & (row < first + count)
                acc[...] += _nn(jnp.where(pick, 1.0, 0.0).astype(BF16), buf[slot, k])

        o_ref[...] = acc[...].astype(o_ref.dtype)

    return pl.pallas_call(
        body, name=name,
        grid_spec=pltpu.PrefetchScalarGridSpec(
            num_scalar_prefetch=1, grid=(nb,), in_specs=[pl.BlockSpec(memory_space=pl.ANY)],
            out_specs=pl.BlockSpec((block, cols), lambda b, t: (b, 0)),
            scratch_shapes=[pltpu.VMEM((2, REMAP_RUNS, win, cols), src.dtype), pltpu.VMEM((block, cols), F32),
                            pltpu.SemaphoreType.DMA((2, REMAP_RUNS))]),
        out_shape=S((n_out, cols), src.dtype),
        compiler_params=pltpu.CompilerParams(dimension_semantics=("arbitrary",)),
    )(table, src)


def _col_blocks(w, width):
    return w.reshape(w.shape[0], NDEV, width).transpose(1, 0, 2)


def _from_col_blocks(w):
    return w.transpose(1, 0, 2).reshape(w.shape[1], NDEV * w.shape[2])


def _local_step(x2, p2, pos, tgt, norm_g, qk_norm_q, qk_norm_k, gla_gate_b, gla_norm_g, ple_norm_g, w_al,
                weights=None, proj_side=None, unpack=None, dw_side_of=None, dh_side_of=None):
    half = ROT_DIM // 2
    inv8 = jnp.power(jnp.float32(ROPE_THETA), -jnp.arange(half, dtype=F32) * 2.0 / ROT_DIM)
    inv = jnp.tile(jnp.concatenate([inv8, inv8, jnp.zeros((HD - ROT_DIM,), F32)]), 2).reshape(1, 128)
    gq = jnp.tile(qk_norm_q, (1, 2))
    gk = jnp.tile(qk_norm_k, (1, 2))

    proj, h, got = _proj_rms(x2, norm_g, w_al, proj_side)
    if proj_side is not None:
        weights = unpack(got)
    w2p, w_att_f, w_gla_f, w_out_f, w_pg_f, w_ple_f = weights
    qkv = _qk_prep(proj, pos, inv, gq, gk)
    fwd = [_att_fwd(qkv[g], qkv[3 + g], qkv[6 + g], g, f"att_fwd{g}") for g in range(3)]
    att, lse, ain = _att_merge([f[0] for f in fwd], [f[1] for f in fwd], proj)
    o_gla, bin_, states = _gla_fwd(proj, w2p, gla_gate_b, gla_norm_g)
    ya, yb, y, x1 = _branches_fwd(ain, bin_, proj, x2, w_att_f, w_gla_f, w_out_f)
    n2, loss_v, dout, du, dw_ple = _ple_loss(x1, p2, tgt, ple_norm_g, w_pg_f, w_ple_f)

    dx1, dy, dg_ple, dw_pg, dw_out = _ple_bwd(du, n2, y, x1, dout, ple_norm_g, w_pg_f, w_out_f)
    dproj, dain, dbin, dw_att, dw_gla = _branches_bwd(dy, ya, yb, ain, bin_, proj, w_att_f, w_gla_f)
    dproj, da0, da1, da2, at1, at2, ls1, ls2 = _att_gate_bwd(dain, att, lse, proj, dproj)
    datts, atts, lses = (da0, da1, da2), (att[None], at1, at2), (lse[None], ls1, ls2)
    dproj, dw2, dbg, dgn = _gla_bwd(proj, w2p, gla_gate_b, gla_norm_g, o_gla, states, dbin, dproj)
    bwd = [_att_bwd(qkv[g], qkv[3 + g], qkv[6 + g], datts[g], atts[g], lses[g], g, f"att_bwd{g}") for g in range(3)]
    dproj, dgq, dgk = _qk_bwd(proj, pos, inv, gq, gk, [b[0] for b in bwd], [b[1] for b in bwd],
                              [b[2] for b in bwd], dproj)
    out = dict(loss=loss_v, dw2=dw2, dw_att=dw_att, dw_gla=dw_gla, dw_out=dw_out, dw_pg=dw_pg, dw_ple=dw_ple,
               dgq=dgq, dgk=dgk, dbg=dbg, dgn=dgn, dg_ple=dg_ple)
    if dw_side_of is None:
        dw_al = _mm(dproj, h, mode="tn", name="dw_in", tm=1536, tn=D, tk=T, out_dtype=BF16)
    else:
        dw_al, out["dw_side"] = _mm(dproj, h, mode="tn", name="dw_in", tm=1536, tn=D, tk=T, out_dtype=BF16,
                                    side=dw_side_of(out))
    grad_x, dg_norm, out["dh_side"] = _dh_rms(dproj, w_al, x2, norm_g, dx1,
                                              None if dh_side_of is None else dh_side_of(dw_al))
    out.update(grad_x=grad_x, dw_al=dw_al, dg_norm=dg_norm)
    return out


def kernel(x, p, positions, norm_g, w_in, qk_norm_q, qk_norm_k, gla_gate_w2, gla_gate_b, gla_norm_g, w_att_proj, w_gla_proj, w_out, ple_norm_g, w_ple_gate, w_ple, loss_target, m_norm_g, m_w_in, m_qk_norm_q, m_qk_norm_k, m_gla_gate_w2, m_gla_gate_b, m_gla_norm_g, m_w_att_proj, m_w_gla_proj, m_w_out, m_ple_norm_g, m_w_ple_gate, m_w_ple, v_norm_g, v_w_in, v_qk_norm_q, v_qk_norm_k, v_gla_gate_w2, v_gla_gate_b, v_gla_norm_g, v_w_att_proj, v_w_gla_proj, v_w_out, v_ple_norm_g, v_w_ple_gate, v_w_ple):
    x2, p2, tgt = x[0], p[0, 0], loss_target[0]
    pos = positions.astype(F32).reshape(T, 1)

    rows3 = jnp.stack([w_gla_proj[0], w_out[0], w_ple_gate[0]]).astype(BF16)
    cols3 = jnp.concatenate([w_att_proj[0], w_ple[0], jnp.pad(gla_gate_w2[0], ((0, 0), (0, 64)))], axis=0).astype(BF16)
    mine = jnp.pad(w_in[0].T.astype(BF16), ((0, SLAB - W_IN_SHARD), (0, 0)))
    (g_in,) = _all_gather_by_chip([mine], "gather_w_in")
    w_al = _remap_rows(g_in.reshape(NDEV * SLAB, D), _slab_row_of_aligned, NCOL, 256, "align_w_in")

    def unpack(got):
        g_rows, g_cols = got
        w2_f = _from_col_blocks(g_cols[:, 768:784, :64])
        return (jnp.pad(w2_f, ((0, GLR_W - GLR_N), (0, 0))), _from_col_blocks(g_cols[:, :512]),
                g_rows[:, 0].reshape(D, D), g_rows[:, 1].reshape(D, D), g_rows[:, 2].reshape(D, D),
                _from_col_blocks(g_cols[:, 512:768]))

    def dw_side_of(g):
        s_rows = jnp.concatenate([g[k].reshape(NDEV, 128, D) for k in ("dw_gla", "dw_out", "dw_pg")], axis=1)
        s_cols = jnp.concatenate([_col_blocks(g["dw_att"], 128), _col_blocks(g["dw_ple"], 128),
                                  jnp.pad(_col_blocks(g["dw2"][:GLR_N], 64), ((0, 0), (0, 0), (0, 64)))], axis=1)
        return _exchange_side([s_rows.astype(BF16), s_cols.astype(BF16)])

    def dh_side_of(dw_al):
        s_in = _remap_rows(dw_al, _aligned_row_of_slab, NDEV * SLAB, 432, "shard_dw_in").reshape(NDEV, SLAB, D)
        return _chips_side([_sibling_sum(s_in, "sibling_sum")])

    loc = _local_step(x2, p2, pos, tgt, norm_g, qk_norm_q, qk_norm_k, gla_gate_b, gla_norm_g, ple_norm_g, w_al,
                      proj_side=_gather_side([rows3, cols3]), unpack=unpack, dw_side_of=dw_side_of,
                      dh_side_of=dh_side_of)
    loss_v, grad_x = loc["loss"], loc["grad_x"]
    dg_norm, dgq, dgk, dbg, dgn, dg_ple = (loc[k] for k in ("dg_norm", "dgq", "dgk", "dbg", "dgn", "dg_ple"))
    r_rows, r_cols = loc["dw_side"]
    (r_in,) = loc["dh_side"]

    r_small = _comm_call(_gather_side([dg_norm, dgq, dgk, dbg, dgn, dg_ple, loss_v]), "gather_small")

    outs = {}

    rows_of = lambda a: jnp.transpose(a, (2, 0, 1))
    outs["w_in"] = [jnp.transpose(o, (1, 2, 0))[0] for o in
                    _adamw_rows(r_in, rows_of(w_in), rows_of(m_w_in), rows_of(v_w_in), "adam_w_in")]
    places = (("w_gla_proj", 0, slice(0, 128), slice(None), (w_gla_proj, m_w_gla_proj, v_w_gla_proj)),
              ("w_out", 0, slice(128, 256), slice(None), (w_out, m_w_out, v_w_out)),
              ("w_ple_gate", 0, slice(256, 384), slice(None), (w_ple_gate, m_w_ple_gate, v_w_ple_gate)),
              ("w_att_proj", 1, slice(0, 512), slice(None), (w_att_proj, m_w_att_proj, v_w_att_proj)),
              ("w_ple", 1, slice(512, 768), slice(None), (w_ple, m_w_ple, v_w_ple)),
              ("gla_gate_w2", 1, slice(768, 784), slice(0, 64), (gla_gate_w2, m_gla_gate_w2, v_gla_gate_w2)))
    res = _adamw_shards([r_rows, r_cols], [place[1:] for place in places])
    for j, place in enumerate(places):
        outs[place[0]] = [o[0] for o in res[4 * j:4 * j + 4]]
    small = ((norm_g, m_norm_g, v_norm_g), (qk_norm_q, m_qk_norm_q, v_qk_norm_q), (qk_norm_k, m_qk_norm_k, v_qk_norm_k),
             (gla_gate_b, m_gla_gate_b, v_gla_gate_b), (gla_norm_g, m_gla_norm_g, v_gla_norm_g),
             (ple_norm_g, m_ple_norm_g, v_ple_norm_g))
    sm = _adamw_small(r_small[:6], small, r_small[6])
    for j, nm in enumerate(("norm_g", "qk_norm_q", "qk_norm_k", "gla_gate_b", "gla_norm_g", "ple_norm_g")):
        outs[nm] = [o[0] for o in sm[4 * j:4 * j + 4]]

    loss = sm[-1][0, 0]
    order = ["norm_g", "w_in", "qk_norm_q", "qk_norm_k", "gla_gate_w2", "gla_gate_b", "gla_norm_g", "w_att_proj",
             "w_gla_proj", "w_out", "ple_norm_g", "w_ple_gate", "w_ple"]
    result = [loss, grad_x[None]]
    for i in range(4):
        result += [outs[nm][i][None] for nm in order]
    return tuple(result)
```

```python
import functools

import jax
import jax.numpy as jnp
from jax import lax
from jax.experimental import pallas as pl
from jax.experimental.pallas import tpu as pltpu

F32 = jnp.float32
BF16 = jnp.bfloat16
S = jax.ShapeDtypeStruct

T = 4096
D = 1024
NDEV = 8
HD = 64
ATT_W = 512
ATT_QKV = 1536
DILATIONS = (1, 4, 16)
BLK = 128
GH, GDK, GDV = 4, 128, 256
GLA_C = 128
PLE = 256
EPS = 1e-6
ROT_DIM = 16
ROPE_THETA = 500000.0
GLA_TAU = 16.0
W_IN_SHARD = 1282

C_QG, C_KG, C_VG, C_ZG, C_GLR, C_ZA, C_GA, C_GB, C_QA, C_KA, C_VA = (
    0, 512, 1024, 2048, 3072, 3584, 4096, 5120, 6144, 7680, 9216)
GLA_GROUP_W = 3584
GLR_W = 512
NCOL = 10752
GLR_N = 16
O_QA, O_ZA, O_QG, O_GLR, O_ZG, O_GA, O_END = 0, 4608, 5120, 7168, 7184, 8208, 10256

ADAM_LR, ADAM_B1, ADAM_B2, ADAM_EPS, ADAM_WD, ADAM_STEP = 0.001, 0.9, 0.999, 1e-08, 0.01, 10

MESH = pl.DeviceIdType.MESH


def _sigmoid(z):
    return 1.0 / (1.0 + jnp.exp(-z))


def _dot(a, b, dims):
    return lax.dot_general(a, b, (dims, ((), ())), preferred_element_type=F32)


def _nn(a, b):
    return _dot(a, b, ((1,), (0,)))


def _nt(a, b):
    return _dot(a, b, ((1,), (1,)))


def _tn(a, b):
    return _dot(a, b, ((0,), (0,)))


def _mm(a, b, *, mode, name, tm, tn, tk, out_dtype=F32, res=None, side=None):
    if mode == "nn":
        (m, k), n = a.shape, b.shape[1]
        a_spec = pl.BlockSpec((tm, tk), lambda i, j, l: (i, l))
        b_spec = pl.BlockSpec((tk, tn), lambda i, j, l: (l, j))
        dot = _nn
    elif mode == "nt":
        (m, k), n = a.shape, b.shape[0]
        a_spec = pl.BlockSpec((tm, tk), lambda i, j, l: (i, l))
        b_spec = pl.BlockSpec((tn, tk), lambda i, j, l: (j, l))
        dot = _nt
    else:
        (k, m), n = a.shape, b.shape[1]
        a_spec = pl.BlockSpec((tk, tm), lambda i, j, l: (l, i))
        b_spec = pl.BlockSpec((tk, tn), lambda i, j, l: (l, j))
        dot = _tn
    assert m % tm == 0 and n % tn == 0 and k % tk == 0, (name, m, n, k)
    grid = (m // tm, n // tn, k // tk)
    nk = grid[2]
    o_spec = pl.BlockSpec((tm, tn), lambda i, j, l: (i, j))
    in_specs = [a_spec, b_spec]
    args = [a, b]
    if res is not None:
        in_specs.append(o_spec)
        args.append(res)
    n_in = len(args)
    n_side = 0 if side is None else len(side["arrs"])
    hbm = pl.BlockSpec(memory_space=pl.ANY)

    def body(*refs):
        a_ref, b_ref = refs[0], refs[1]
        r_ref = refs[2] if res is not None else None
        o_ref = refs[n_in + n_side]
        scratch = refs[n_in + 2 * n_side + 1:]
        if side is not None:
            start, finish_side = side["plan"](refs[n_in:n_in + n_side], refs[n_in + n_side + 1:n_in + 2 * n_side + 1],
                                              *scratch[1 if nk > 1 else 0:])
            ids = [pl.program_id(d) for d in range(3)]

            @pl.when((ids[0] == 0) & (ids[1] == 0) & (ids[2] == 0))
            def _():
                start()

        part = dot(a_ref[...].astype(BF16), b_ref[...].astype(BF16))

        def finish(val):
            if r_ref is not None:
                val = val + r_ref[...]
            o_ref[...] = val.astype(out_dtype)

        if nk == 1:
            finish(part)
        else:
            acc = scratch[0]
            l = pl.program_id(2)

            @pl.when(l == 0)
            def _():
                acc[...] = part

            @pl.when(l > 0)
            def _():
                acc[...] += part

            @pl.when(l == nk - 1)
            def _():
                finish(acc[...])

        if side is not None:
            @pl.when((ids[0] == grid[0] - 1) & (ids[1] == grid[1] - 1) & (ids[2] == grid[2] - 1))
            def _():
                finish_side()

    sems = [] if side is None else side["scratch"]
    outs = pl.pallas_call(
        body, name=name, grid=grid,
        in_specs=in_specs + [hbm] * n_side, out_specs=[o_spec] + [hbm] * n_side,
        out_shape=[S((m, n), out_dtype)] + ([] if side is None else side["out_shape"]),
        scratch_shapes=([pltpu.VMEM((tm, tn), F32)] if nk > 1 else []) + sems,
        compiler_params=pltpu.CompilerParams(
            dimension_semantics=("arbitrary",) * 3 if side is not None else ("parallel", "parallel", "arbitrary")),
    )(*args, *([] if side is None else side["arrs"]))
    return outs[0] if side is None else (outs[0], outs[1:])


def _side_parts(side, refs, n_in, n_out):
    n_side = 0 if side is None else len(side["arrs"])
    scratch = refs[n_in + n_out + 2 * n_side:]
    if side is None:
        return (lambda: None), (lambda: None), scratch
    start, finish = side["plan"](refs[n_in:n_in + n_side], refs[n_in + n_side + n_out:n_in + n_out + 2 * n_side],
                                 *scratch[len(scratch) - len(side["scratch"]):])
    return start, finish, scratch


def _proj_rms(x, g, wt, side=None):
    tm, tn = 1024, 1536
    grid = (T // tm, NCOL // tn)
    n_side = 0 if side is None else len(side["arrs"])
    hbm = pl.BlockSpec(memory_space=pl.ANY)

    def body(*refs):
        x_ref, g_ref, w_ref = refs[:3]
        o_ref, h_ref = refs[3 + n_side], refs[4 + n_side]
        start, finish, _ = _side_parts(side, refs, 3, 2)
        i, j = pl.program_id(0), pl.program_id(1)

        @pl.when((i == 0) & (j == 0))
        def _():
            start()

        @pl.when(j == 0)
        def _():
            xf = x_ref[...]
            r = lax.rsqrt(jnp.mean(xf * xf, axis=-1, keepdims=True) + EPS)
            h_ref[...] = (xf * r * g_ref[...]).astype(BF16)

        o_ref[...] = _nt(h_ref[...], w_ref[...])

        @pl.when((i == grid[0] - 1) & (j == grid[1] - 1))
        def _():
            finish()

    outs = pl.pallas_call(
        body, name="proj", grid=grid,
        in_specs=[pl.BlockSpec((tm, D), lambda i, j: (i, 0)), pl.BlockSpec((1, D), lambda i, j: (0, 0)),
                  pl.BlockSpec((tn, D), lambda i, j: (j, 0))] + [hbm] * n_side,
        out_specs=[pl.BlockSpec((tm, tn), lambda i, j: (i, j)), pl.BlockSpec((tm, D), lambda i, j: (i, 0))] + [hbm] * n_side,
        out_shape=[S((T, NCOL), F32), S((T, D), BF16)] + ([] if side is None else side["out_shape"]),
        scratch_shapes=[] if side is None else side["scratch"],
        compiler_params=pltpu.CompilerParams(dimension_semantics=("arbitrary", "arbitrary")),
    )(x, g, wt, *([] if side is None else side["arrs"]))
    return outs[0], outs[1], outs[2:]


def _dh_rms(dproj, wt, x, g, skip, side=None):
    tm, tk = 1024, 2688
    grid = (T // tm, NCOL // tk)
    n_side = 0 if side is None else len(side["arrs"])
    hbm = pl.BlockSpec(memory_space=pl.ANY)

    def body(*refs):
        a_ref, w_ref, x_ref, g_ref, s_ref = refs[:5]
        dx_ref, dg_ref = refs[5 + n_side], refs[6 + n_side]
        start, finish, scratch = _side_parts(side, refs, 5, 2)
        acc = scratch[0]
        i, l = pl.program_id(0), pl.program_id(1)

        @pl.when((i == 0) & (l == 0))
        def _():
            start()

        part = _nn(a_ref[...], w_ref[...])

        @pl.when(l == 0)
        def _():
            acc[...] = part

        @pl.when(l > 0)
        def _():
            acc[...] += part

        @pl.when(l == grid[1] - 1)
        def _():
            xf = x_ref[...]
            r = lax.rsqrt(jnp.mean(xf * xf, axis=-1, keepdims=True) + EPS)
            dn = acc[...]
            u = dn * g_ref[...]
            dx_ref[...] = s_ref[...] + r * u - xf * (r * r * r) * jnp.mean(u * xf, axis=-1, keepdims=True)
            dg = jnp.sum(dn * xf * r, axis=0, keepdims=True)

            @pl.when(i == 0)
            def _():
                dg_ref[...] = dg

            @pl.when(i > 0)
            def _():
                dg_ref[...] += dg

        @pl.when((i == grid[0] - 1) & (l == grid[1] - 1))
        def _():
            finish()

    tok = pl.BlockSpec((tm, D), lambda i, l: (i, 0))
    outs = pl.pallas_call(
        body, name="dh", grid=grid,
        in_specs=[pl.BlockSpec((tm, tk), lambda i, l: (i, l)), pl.BlockSpec((tk, D), lambda i, l: (l, 0)), tok,
                  pl.BlockSpec((1, D), lambda i, l: (0, 0)), tok] + [hbm] * n_side,
        out_specs=[tok, pl.BlockSpec((1, D), lambda i, l: (0, 0))] + [hbm] * n_side,
        out_shape=[S((T, D), F32), S((1, D), F32)] + ([] if side is None else side["out_shape"]),
        scratch_shapes=[pltpu.VMEM((tm, D), F32)] + ([] if side is None else side["scratch"]),
        compiler_params=pltpu.CompilerParams(dimension_semantics=("arbitrary", "arbitrary")),
    )(dproj, wt, x, g, skip, *([] if side is None else side["arrs"]))
    return outs[0], outs[1], outs[2:]


def _rot_tables(pos_ref, inv_ref):
    lane = lax.broadcasted_iota(jnp.int32, (1, 128), 1) % HD
    ang = pos_ref[...] * inv_ref[...]
    cos, sin = jnp.cos(ang), jnp.sin(ang)
    c = jnp.where(lane < ROT_DIM, cos, 1.0)
    sp = jnp.where((lane >= ROT_DIM // 2) & (lane < ROT_DIM), sin, 0.0)
    sm = jnp.where(lane < ROT_DIM // 2, -sin, 0.0)
    return c, sp, sm


def _head_sums(v):
    same = (lax.broadcasted_iota(jnp.int32, (128, 128), 0) < HD) == (lax.broadcasted_iota(jnp.int32, (128, 128), 1) < HD)
    ones = jnp.where(same, 1.0, 0.0).astype(BF16)
    hi = v.astype(BF16)
    lo = (v - hi.astype(F32)).astype(BF16)
    return _nn(hi, ones) + _nn(lo, ones)


def _pair_norm(t):
    return lax.rsqrt(_head_sums(t * t) * (1.0 / HD) + EPS)


def _pair_mean(t):
    return _head_sums(t) * (1.0 / HD)


TT = 256
NCH = ATT_QKV // 128


def _res_shape(grp, dtype):
    return S((DILATIONS[grp], T // DILATIONS[grp], ATT_W), dtype)


def _res_spec(grp):
    dil = DILATIONS[grp]
    return pl.BlockSpec((dil, TT // dil, ATT_W), lambda i: (0, i, 0))


def _to_residues(sc, j, dst_ref, dil, cols):
    n = TT // dil
    for r in range(dil):
        rows = sc[j] if dil == 1 else sc.at[j][pl.ds(r, n, stride=dil), :]
        dst_ref[r, :, cols] = rows.astype(dst_ref.dtype)


def _from_residues(src_ref, cols, sc, j, dil):
    n = TT // dil
    for r in range(dil):
        if dil == 1:
            sc[j] = src_ref[r, :, cols]
        else:
            sc.at[j][pl.ds(r, n, stride=dil), :] = src_ref[r, :, cols]


def _tok_spec(width, cblk=0):
    return pl.BlockSpec((TT, width), functools.partial(lambda i, c: (i, c), c=cblk))


def _const_spec(arr_or_shape):
    shape = arr_or_shape if isinstance(arr_or_shape, tuple) else arr_or_shape.shape
    return pl.BlockSpec(shape, functools.partial(lambda i, nd: (0,) * nd, nd=len(shape)))


def _qk_prep(proj, pos, inv, gq, gk):
    def body(q_ref, k_ref, v_ref, pos_ref, inv_ref, gq_ref, gk_ref, *rest):
        outs, sc = rest[:9], rest[9]
        c, sp, sm = _rot_tables(pos_ref, inv_ref)
        for which, (src, g_ref) in enumerate(((q_ref, gq_ref), (k_ref, gk_ref), (v_ref, None))):
            if g_ref is not None:
                g = jnp.broadcast_to(g_ref[...] * ((HD ** -0.5) if which == 0 else 1.0), c.shape)
                cg, spg, smg = c * g, sp * pltpu.roll(g, 8, 1), sm * pltpu.roll(g, 120, 1)
            for j in range(NCH):
                t = src[:, j * 128:(j + 1) * 128]
                if g_ref is not None:
                    t = _pair_norm(t) * (t * cg + pltpu.roll(t, 8, 1) * spg + pltpu.roll(t, 120, 1) * smg)
                sc[j] = t
            for j in range(NCH):
                grp, sub = divmod(j * 128, ATT_W)
                _to_residues(sc, j, outs[which * 3 + grp], DILATIONS[grp], slice(sub, sub + 128))

    return pl.pallas_call(
        body, name="qk_prep", grid=(T // TT,),
        in_specs=[_tok_spec(ATT_QKV, C_QA // ATT_QKV), _tok_spec(ATT_QKV, C_KA // ATT_QKV),
                  _tok_spec(ATT_QKV, C_VA // ATT_QKV), _tok_spec(1), _const_spec(inv), _const_spec(gq), _const_spec(gk)],
        out_specs=[_res_spec(g) for _ in range(3) for g in range(3)],
        out_shape=[_res_shape(g, BF16) for _ in range(3) for g in range(3)],
        scratch_shapes=[pltpu.VMEM((NCH, TT, 128), F32)],
        compiler_params=pltpu.CompilerParams(dimension_semantics=("arbitrary",)),
    )(proj, proj, proj, pos, inv, gq, gk)


def _qk_bwd(proj, pos, inv, gq, gk, dqs, dks, dvs, dproj):
    const = lambda a: pl.BlockSpec(a.shape, functools.partial(lambda i, p, nd: (0,) * nd, nd=a.ndim))
    res = lambda g: pl.BlockSpec((DILATIONS[g], TT // DILATIONS[g], ATT_W), lambda i, p: (0, i, 0))
    base = C_QA // ATT_QKV

    def body(t_ref, pos_ref, inv_ref, gq_ref, gk_ref, dq0, dq1, dq2, dk0, dk1, dk2, dv0, dv1, dv2, buf_ref,
             out_ref, dgq_ref, dgk_ref, sc):
        del buf_ref
        part = pl.program_id(1)
        first = pl.program_id(0) == 0

        def gather(drefs):
            for j in range(NCH):
                grp, sub = divmod(j * 128, ATT_W)
                _from_residues(drefs[grp], slice(sub, sub + 128), sc, j, DILATIONS[grp])

        def normed(g_ref, drefs, dg_ref):
            c, sp, sm = _rot_tables(pos_ref, inv_ref)
            gather(drefs)
            dg = jnp.zeros((1, 128), F32)
            for j in range(NCH):
                cols = slice(j * 128, (j + 1) * 128)
                d_rot = sc[j]
                dn = d_rot * c + pltpu.roll(d_rot * sp, 120, 1) + pltpu.roll(d_rot * sm, 8, 1)
                t = t_ref[:, cols]
                r = _pair_norm(t)
                gain = g_ref[...]
                dn_t = dn * t
                out_ref[:, cols] = (r * (dn * gain - t * ((r * r) * _pair_mean(dn_t * gain)))).astype(BF16)
                dg = dg + jnp.sum(dn_t * r, axis=0, keepdims=True)
            dg = dg + pltpu.roll(dg, HD, 1)

            @pl.when(first)
            def _():
                dg_ref[...] = dg

            @pl.when(jnp.logical_not(first))
            def _():
                dg_ref[...] += dg

        @pl.when(part == 0)
        def _():
            gather((dv0, dv1, dv2))
            for j in range(NCH):
                out_ref[:, j * 128:(j + 1) * 128] = sc[j].astype(BF16)

        @pl.when(part == 1)
        def _():
            normed(gq_ref, (dq0, dq1, dq2), dgq_ref)

        @pl.when(part == 2)
        def _():
            normed(gk_ref, (dk0, dk1, dk2), dgk_ref)

    keep = pl.BlockSpec((1, 128), lambda i, p: (0, 0))
    return pl.pallas_call(
        body, name="qk_bwd", grid=(T // TT, 3),
        in_specs=[pl.BlockSpec((TT, ATT_QKV), lambda i, p: (i, base + jnp.maximum(p - 1, 0))),
                  pl.BlockSpec((TT, 1), lambda i, p: (i, 0)), const(inv), const(gq), const(gk)]
        + [res(g) for _ in range(3) for g in range(3)] + [pl.BlockSpec(memory_space=pl.ANY)],
        out_specs=[pl.BlockSpec((TT, ATT_QKV), lambda i, p: (i, base + jnp.where(p == 0, 2, p - 1))), keep, keep],
        out_shape=[S(dproj.shape, dproj.dtype), S((1, 128), F32), S((1, 128), F32)],
        input_output_aliases={14: 0},
        scratch_shapes=[pltpu.VMEM((NCH, TT, 128), F32)],
        compiler_params=pltpu.CompilerParams(dimension_semantics=("arbitrary", "arbitrary")),
    )(proj, pos, inv, gq, gk, *dqs, *dks, *dvs, dproj)


def _split_heads(t):
    low = lax.broadcasted_iota(jnp.int32, (1, 128), 1) < HD
    zero = jnp.zeros_like(t)
    return jnp.concatenate([jnp.where(low, t, zero), jnp.where(low, zero, t)], axis=0)


def _join_heads(t2):
    low = lax.broadcasted_iota(jnp.int32, (1, 128), 1) < HD
    n = t2.shape[0] // 2
    return jnp.where(low, t2[:n], t2[n:])


def _band_mask4(has_before, has_own):
    row = lax.broadcasted_iota(jnp.int32, (BLK, 4 * BLK), 0)
    lane = lax.broadcasted_iota(jnp.int32, (BLK, 4 * BLK), 1)
    key = lane & (BLK - 1)
    own = lane >= 2 * BLK
    return (own & (key <= row) & has_own) | (jnp.logical_not(own) & (key >= row) & has_before)


def _band_mask_before(has_before):
    row = lax.broadcasted_iota(jnp.int32, (BLK, 2 * BLK), 0)
    key = lax.broadcasted_iota(jnp.int32, (BLK, 2 * BLK), 1) & (BLK - 1)
    return (key >= row) & has_before


def _per_head(width, col_a, col_b):
    lane = lax.broadcasted_iota(jnp.int32, (1, width), 1)
    return jnp.where((lane & BLK) == 0, col_a, col_b)


NQ = ATT_W // 128


def _att_fwd(q, k, v, grp, name):
    dil = DILATIONS[grp]
    nb = T // dil // BLK

    def body(q_ref, kp_ref, kc_ref, vp_ref, vc_ref, o_ref, lse_ref, s_sc, p_sc):
        mask = _band_mask4(pl.program_id(1) > 0, True)
        low = lax.broadcasted_iota(jnp.int32, (1, 128), 1) < HD
        halves = lambda ref, j, h: (ref[j, :, h * BLK:(h + 1) * BLK], ref[j, :, (h + 2) * BLK:(h + 3) * BLK])
        for j in range(NQ):
            cols = slice(j * 128, (j + 1) * 128)
            k4 = jnp.concatenate([_split_heads(kp_ref[:, cols]), _split_heads(kc_ref[:, cols])], axis=0)
            s_sc[j] = jnp.where(mask, _nt(q_ref[:, cols], k4), -jnp.inf)
        mxs = [[jnp.maximum(*(jnp.max(t, axis=-1, keepdims=True) for t in halves(s_sc, j, h))) for h in range(2)]
               for j in range(NQ)]
        dens = []
        for j in range(NQ):
            p = jnp.exp(s_sc[j] - _per_head(4 * BLK, *mxs[j]))
            p_sc[j] = p.astype(BF16)
            dens.append([jnp.sum(p[:, h * BLK:(h + 1) * BLK], axis=-1, keepdims=True)
                         + jnp.sum(p[:, (h + 2) * BLK:(h + 3) * BLK], axis=-1, keepdims=True) for h in range(2)])
        for j in range(NQ):
            cols = slice(j * 128, (j + 1) * 128)
            v4 = jnp.concatenate([_split_heads(vp_ref[:, cols]), _split_heads(vc_ref[:, cols])], axis=0)
            o_ref[:, cols] = _nn(p_sc[j], v4) / jnp.where(low, dens[j][0], dens[j][1])
            lse_ref[:, cols] = jnp.where(low, mxs[j][0] + jnp.log(dens[j][0]), mxs[j][1] + jnp.log(dens[j][1]))

    cur = pl.BlockSpec((None, BLK, ATT_W), lambda r, i: (r, i, 0))
    prev = pl.BlockSpec((None, BLK, ATT_W), lambda r, i: (r, jnp.maximum(i - 1, 0), 0))
    return pl.pallas_call(
        body, name=name, grid=(dil, nb),
        in_specs=[cur, prev, cur, prev, cur],
        out_specs=[cur, cur], out_shape=[_res_shape(grp, F32)] * 2,
        scratch_shapes=[pltpu.VMEM((NQ, BLK, 4 * BLK), F32), pltpu.VMEM((NQ, BLK, 4 * BLK), BF16)],
        compiler_params=pltpu.CompilerParams(dimension_semantics=("parallel", "arbitrary")),
    )(q, k, k, v, v)


def _att_bwd(q, k, v, datt, att, lse, grp, name):
    dil = DILATIONS[grp]
    nb = T // dil // BLK
    scale = HD ** -0.5

    def body(q0_ref, q1_ref, kp_ref, kc_ref, vp_ref, vc_ref, do0_ref, do1_ref, o0_ref, o1_ref, l0_ref, l1_ref,
             dq_ref, dk_ref, dv_ref, k4_sc, v4_sc, s0_sc, s1_sc, dp0_sc, dp1_sc, p_sc, ds_sc):
        i = pl.program_id(1)
        mask_mine = _band_mask4(i > 0, True)
        mask_next = _band_mask_before(i < nb - 1)
        low = lax.broadcasted_iota(jnp.int32, (1, 128), 1) < HD
        for j in range(NQ):
            cols = slice(j * 128, (j + 1) * 128)
            k4_sc[j, :2 * BLK] = _split_heads(kp_ref[:, cols])
            k4_sc[j, 2 * BLK:] = _split_heads(kc_ref[:, cols])
            v4_sc[j, :2 * BLK] = _split_heads(vp_ref[:, cols])
            v4_sc[j, 2 * BLK:] = _split_heads(vc_ref[:, cols])
        for j in range(NQ):
            cols = slice(j * 128, (j + 1) * 128)
            s0_sc[j] = _nt(q0_ref[:, cols], k4_sc[j])
            s1_sc[j] = _nt(q1_ref[:, cols], k4_sc[j, 2 * BLK:])
            dp0_sc[j] = _nt(do0_ref[:, cols].astype(BF16), v4_sc[j])
            dp1_sc[j] = _nt(do1_ref[:, cols].astype(BF16), v4_sc[j, 2 * BLK:])
        stats = []
        for j in range(NQ):
            cols = slice(j * 128, (j + 1) * 128)
            for do_ref, o_ref, l_ref in ((do0_ref, o0_ref, l0_ref), (do1_ref, o1_ref, l1_ref)):
                prod = do_ref[:, cols].astype(F32) * o_ref[:, cols].astype(F32)
                d_all = jnp.sum(prod, axis=-1, keepdims=True)
                d_low = jnp.sum(jnp.where(low, prod, 0.0), axis=-1, keepdims=True)
                lse_t = l_ref[:, cols]
                stats.append((d_low, d_all - d_low, lse_t[:, 0:1], lse_t[:, HD:HD + 1]))
        for j in range(NQ):
            (da, db, la, lb), (da1, db1, la1, lb1) = stats[2 * j], stats[2 * j + 1]
            p0 = jnp.where(mask_mine, jnp.exp(s0_sc[j] - _per_head(4 * BLK, la, lb)), 0.0)
            ds0 = p0 * (dp0_sc[j] - _per_head(4 * BLK, da, db))
            p1 = jnp.where(mask_next, jnp.exp(s1_sc[j] - _per_head(2 * BLK, la1, lb1)), 0.0)
            ds1 = p1 * (dp1_sc[j] - _per_head(2 * BLK, da1, db1))
            p_sc[j, :BLK] = p0.astype(BF16)
            ds_sc[j, :BLK] = ds0.astype(BF16)
            p_sc[j, BLK:, 2 * BLK:] = p1.astype(BF16)
            ds_sc[j, BLK:, 2 * BLK:] = ds1.astype(BF16)
        for j in range(NQ):
            cols = slice(j * 128, (j + 1) * 128)
            dq_ref[:, cols] = _nn(ds_sc[j, :BLK], k4_sc[j]) * scale
            qq = jnp.concatenate([q0_ref[:, cols], q1_ref[:, cols]], axis=0)
            dd = jnp.concatenate([do0_ref[:, cols], do1_ref[:, cols]], axis=0).astype(BF16)
            dk_ref[:, cols] = _join_heads(_tn(ds_sc[j, :, 2 * BLK:], qq))
            dv_ref[:, cols] = _join_heads(_tn(p_sc[j, :, 2 * BLK:], dd))

    def spec(shift):
        return pl.BlockSpec((None, BLK, ATT_W), lambda r, i: (r, jnp.clip(i + shift, 0, nb - 1), 0))

    here, after, before = spec(0), spec(1), spec(-1)
    vm = pltpu.VMEM
    return pl.pallas_call(
        body, name=name, grid=(dil, nb),
        in_specs=[here, after, before, here, before, here, here, after, here, after, here, after],
        out_specs=[here] * 3, out_shape=[_res_shape(grp, F32)] * 3,
        scratch_shapes=[vm((NQ, 4 * BLK, 128), BF16), vm((NQ, 4 * BLK, 128), BF16), vm((NQ, BLK, 4 * BLK), F32),
                        vm((NQ, BLK, 2 * BLK), F32), vm((NQ, BLK, 4 * BLK), F32), vm((NQ, BLK, 2 * BLK), F32),
                        vm((NQ, 2 * BLK, 4 * BLK), BF16), vm((NQ, 2 * BLK, 4 * BLK), BF16)],
        compiler_params=pltpu.CompilerParams(dimension_semantics=("parallel", "arbitrary")),
    )(q, q, k, k, v, v, datt, datt, att, att, lse, lse)


def _att_merge(os_, lses, proj):
    nq = ATT_W // 128

    def body(o0, o1, o2, l0, l1, l2, za_ref, att_ref, lse_ref, ain_ref, sc):
        for a, ref in enumerate((o0, o1, o2, l0, l1, l2)):
            for j in range(nq):
                _from_residues(ref, slice(j * 128, (j + 1) * 128), sc, a * nq + j, DILATIONS[a % 3])
        for j in range(nq):
            cols = slice(j * 128, (j + 1) * 128)
            oa, ob, oc = (sc[a * nq + j] for a in range(3))
            la, lb, lc = (sc[(3 + a) * nq + j] for a in range(3))
            m = jnp.maximum(jnp.maximum(la, lb), lc)
            wa, wb, wc = jnp.exp(la - m), jnp.exp(lb - m), jnp.exp(lc - m)
            tot = wa + wb + wc
            att = (wa * oa + wb * ob + wc * oc) / tot
            att_ref[:, cols] = att
            lse_ref[:, cols] = m + jnp.log(tot)
            za = za_ref[:, cols]
            ain_ref[:, cols] = (att * za * _sigmoid(za)).astype(BF16)

    return pl.pallas_call(
        body, name="att_merge", grid=(T // TT,),
        in_specs=[_res_spec(g) for _ in range(2) for g in range(3)] + [_tok_spec(ATT_W, C_ZA // ATT_W)],
        out_specs=[_tok_spec(ATT_W)] * 3,
        out_shape=[S((T, ATT_W), F32), S((T, ATT_W), F32), S((T, ATT_W), BF16)],
        scratch_shapes=[pltpu.VMEM((6 * nq, TT, 128), F32)],
        compiler_params=pltpu.CompilerParams(dimension_semantics=("arbitrary",)),
    )(*os_, *lses, proj)


def _att_gate_bwd(dain, att, lse, proj, dproj):
    nq = ATT_W // 128

    def body(d_ref, att_ref, lse_ref, za_ref, buf_ref, dza_ref, da0, da1, da2, at1, at2, ls1, ls2, sc):
        del buf_ref
        for j in range(nq):
            cols = slice(j * 128, (j + 1) * 128)
            za = za_ref[:, cols]
            sg = _sigmoid(za)
            d = d_ref[:, cols].astype(F32)
            att_ = att_ref[:, cols]
            dza_ref[:, cols] = (d * att_ * sg * (1.0 + za * (1.0 - sg))).astype(BF16)
            sc[j] = d * za * sg
            sc[nq + j] = att_
            sc[2 * nq + j] = lse_ref[:, cols]
        for j in range(nq):
            cols = slice(j * 128, (j + 1) * 128)
            for grp, dst in enumerate((da0, da1, da2)):
                _to_residues(sc, j, dst, DILATIONS[grp], cols)
            for grp, dst in ((1, at1), (2, at2)):
                _to_residues(sc, nq + j, dst, DILATIONS[grp], cols)
            for grp, dst in ((1, ls1), (2, ls2)):
                _to_residues(sc, 2 * nq + j, dst, DILATIONS[grp], cols)

    res = (0, 1, 2, 1, 2, 1, 2)
    return pl.pallas_call(
        body, name="att_gate_bwd", grid=(T // TT,),
        in_specs=[_tok_spec(ATT_W)] * 3 + [_tok_spec(ATT_W, C_ZA // ATT_W), pl.BlockSpec(memory_space=pl.ANY)],
        out_specs=[_tok_spec(ATT_W, C_ZA // ATT_W)] + [_res_spec(g) for g in res],
        out_shape=[S(dproj.shape, dproj.dtype)] + [_res_shape(g, BF16) for g in res[:5]]
        + [_res_shape(g, F32) for g in res[5:]],
        input_output_aliases={4: 0},
        scratch_shapes=[pltpu.VMEM((3 * nq, TT, 128), F32)],
        compiler_params=pltpu.CompilerParams(dimension_semantics=("arbitrary",)),
    )(dain, att, lse, proj, dproj)


def _split3(v):
    hi = v.astype(BF16)
    r1 = v - hi.astype(F32)
    mid = r1.astype(BF16)
    lo = (r1 - mid.astype(F32)).astype(BF16)
    return hi, mid, lo


def _chunk_scores(qt, kt, q_ref, k_ref, h):
    cols = slice(h * GDK, (h + 1) * GDK)
    own = jnp.sum(q_ref[:, cols] * (GDK ** -0.5) * k_ref[:, cols], axis=-1, keepdims=True)
    row = lax.broadcasted_iota(jnp.int32, (GLA_C, GLA_C), 0)
    col = lax.broadcasted_iota(jnp.int32, (GLA_C, GLA_C), 1)
    a = _nt(qt.astype(BF16), kt.astype(BF16))
    return jnp.where(col < row, a, jnp.where(col == row, own, 0.0))


def _tri_sum(v, upper):
    n = v.shape[0]
    row = lax.broadcasted_iota(jnp.int32, (n, n), 0)
    col = lax.broadcasted_iota(jnp.int32, (n, n), 1)
    tri = jnp.where(col >= row if upper else col <= row, 1.0, 0.0).astype(BF16)
    hi, mid, lo = _split3(v)
    return _nn(tri, hi) + _nn(tri, mid) + _nn(tri, lo)


def _gla_gates(glr_ref, w2_ref, b_ref):
    logit = _nn(glr_ref[...].astype(BF16), w2_ref[...]) + b_ref[...]
    lg = (jnp.minimum(logit, 0.0) - jnp.log(1.0 + jnp.exp(-jnp.abs(logit)))) * (1.0 / GLA_TAU)
    return logit, _tri_sum(lg, upper=False)


def _gla_head(cum, q_ref, k_ref, h):
    cols = slice(h * GDK, (h + 1) * GDK)
    b = cum[:, cols]
    last = b[GLA_C - 1:GLA_C, :]
    e_pos = jnp.exp(b)
    e_neg = jnp.exp(-b)
    e_end = jnp.exp(last - b)
    qt = q_ref[:, cols] * (GDK ** -0.5) * e_pos
    kt = k_ref[:, cols] * e_neg
    kh = k_ref[:, cols] * e_end
    return b, last, e_pos, e_neg, e_end, qt, kt, kh


def _causal(n):
    return lax.broadcasted_iota(jnp.int32, (n, n), 1) <= lax.broadcasted_iota(jnp.int32, (n, n), 0)


def _gla_fwd(proj, w2p, bg, gn):
    nc = T // GLA_C

    def body(q_ref, k_ref, v_ref, glr_ref, zg_ref, w2_ref, b_ref, gn_ref, o_ref, bin_ref, st_ref, state):
        @pl.when(pl.program_id(0) == 0)
        def _():
            state[...] = jnp.zeros_like(state)

        _, cum = _gla_gates(glr_ref, w2_ref, b_ref)
        for h in range(GH):
            _, last, _, _, _, qt, kt, kh = _gla_head(cum, q_ref, k_ref, h)
            vcols = slice(h * GDV, (h + 1) * GDV)
            st = state[h]
            st_ref[0, h] = st
            v = v_ref[:, vcols].astype(BF16)
            qb = qt.astype(BF16)
            a = _chunk_scores(qt, kt, q_ref, k_ref, h)
            o = _nt(qb, st.astype(BF16)) + _nn(a.astype(BF16), v)
            state[h] = st * jnp.exp(last) + _tn(v, kh.astype(BF16))
            o_ref[:, vcols] = o
            r = lax.rsqrt(jnp.mean(o * o, axis=-1, keepdims=True) + EPS)
            zg = zg_ref[:, vcols]
            bin_ref[:, vcols] = (o * r * gn_ref[...] * zg * _sigmoid(zg)).astype(BF16)

    row = lambda width, cblk: pl.BlockSpec((GLA_C, width), functools.partial(lambda i, c: (i, c), c=cblk))
    full = lambda a: pl.BlockSpec(a.shape, functools.partial(lambda i, nd: (0,) * nd, nd=a.ndim))
    return pl.pallas_call(
        body, name="gla_fwd", grid=(nc,),
        in_specs=[row(512, C_QG // 512), row(512, C_KG // 512), row(1024, C_VG // 1024), row(GLR_W, C_GLR // GLR_W),
                  row(1024, C_ZG // 1024), full(w2p), full(bg), full(gn)],
        out_specs=[pl.BlockSpec((GLA_C, GH * GDV), lambda i: (i, 0)), pl.BlockSpec((GLA_C, GH * GDV), lambda i: (i, 0)),
                   pl.BlockSpec((1, GH, GDV, GDK), lambda i: (i, 0, 0, 0))],
        out_shape=[S((T, GH * GDV), F32), S((T, GH * GDV), BF16), S((nc, GH, GDV, GDK), F32)],
        scratch_shapes=[pltpu.VMEM((GH, GDV, GDK), F32)],
        compiler_params=pltpu.CompilerParams(dimension_semantics=("arbitrary",)),
    )(proj, proj, proj, proj, proj, w2p, bg, gn)


def _gla_bwd(proj, w2p, bg, gn, o_gla, states, dbin, dproj):
    nc = T // GLA_C

    def body(q_ref, k_ref, v_ref, glr_ref, zg_ref, w2_ref, b_ref, gn_ref, o_ref, st_ref, dbin_ref, buf_ref,
             out_ref, dw2_ref, dbg_ref, dgn_ref, dstate, dlogit):
        del buf_ref
        dq_ref = out_ref.at[:, C_QG:C_KG]
        dk_ref = out_ref.at[:, C_KG:C_VG]
        dv_ref = out_ref.at[:, C_VG:C_ZG]
        dzg_ref = out_ref.at[:, C_ZG:C_GLR]
        dglr_ref = out_ref.at[:, C_GLR:C_GLR + GLR_W]
        first = pl.program_id(0) == 0

        @pl.when(first)
        def _():
            dstate[...] = jnp.zeros_like(dstate)

        logit, cum = _gla_gates(glr_ref, w2_ref, b_ref)
        is_last = lax.broadcasted_iota(jnp.int32, (GLA_C, 1), 0) == GLA_C - 1
        dgn = jnp.zeros((1, GDV), F32)
        for h in range(GH):
            _, last, e_pos, e_neg, e_end, qt, kt, kh = _gla_head(cum, q_ref, k_ref, h)
            cols = slice(h * GDK, (h + 1) * GDK)
            vcols = slice(h * GDV, (h + 1) * GDV)
            o = o_ref[:, vcols]
            r = lax.rsqrt(jnp.mean(o * o, axis=-1, keepdims=True) + EPS)
            zg = zg_ref[:, vcols]
            sg = _sigmoid(zg)
            db_ = dbin_ref[:, vcols].astype(F32)
            dlin = db_ * zg * sg
            dzg_ref[:, vcols] = (db_ * (o * r * gn_ref[...]) * sg * (1.0 + zg * (1.0 - sg))).astype(BF16)
            u = dlin * gn_ref[...]
            do = (r * u - o * (r * r * r) * jnp.mean(u * o, axis=-1, keepdims=True)).astype(BF16)
            dgn = dgn + jnp.sum(dlin * o * r, axis=0, keepdims=True)
            st = st_ref[0, h]
            dst = dstate[h]
            v = v_ref[:, vcols].astype(BF16)
            qb, kb, khb = qt.astype(BF16), kt.astype(BF16), kh.astype(BF16)
            dstb = dst.astype(BF16)
            causal = _causal(GLA_C)
            a = _chunk_scores(qt, kt, q_ref, k_ref, h).astype(BF16)
            da = jnp.where(causal, _nt(do, v), 0.0).astype(BF16)
            dqt = _nn(do, st.astype(BF16)) + _nn(da, kb)
            dkt = _tn(da, qb)
            dkh = _nn(v, dstb)
            dv_ref[:, vcols] = (_tn(a, do) + _nt(khb, dstb)).astype(BF16)
            lam = jnp.exp(last)
            dlam = jnp.sum(dst * st, axis=0, keepdims=True)
            dstate[h] = dst * lam + _tn(do, qb)
            dq_ref[:, cols] = (dqt * e_pos * (GDK ** -0.5)).astype(BF16)
            dk_ref[:, cols] = (dkt * e_neg + dkh * e_end).astype(BF16)
            dkh_kh = dkh * kh
            dcum = dqt * qt - dkt * kt - dkh_kh
            dlast = jnp.sum(dkh_kh, axis=0, keepdims=True) + dlam * lam
            dcum = jnp.where(is_last, dcum + dlast, dcum)
            dlg = _tri_sum(dcum, upper=True)
            dlogit[:, cols] = dlg * (1.0 / GLA_TAU) * (1.0 - _sigmoid(logit[:, cols]))

        dl = dlogit[...]
        dlb = dl.astype(BF16)
        dglr_ref[...] = _nt(dlb, w2_ref[...]).astype(BF16)
        dw2 = _tn(glr_ref[...].astype(BF16), dlb)
        dbg = jnp.sum(dl, axis=0, keepdims=True)

        @pl.when(first)
        def _():
            dw2_ref[...] = dw2
            dbg_ref[...] = dbg
            dgn_ref[...] = dgn

        @pl.when(jnp.logical_not(first))
        def _():
            dw2_ref[...] += dw2
            dbg_ref[...] += dbg
            dgn_ref[...] += dgn

    rev = lambda i: nc - 1 - i
    row = lambda width, cblk: pl.BlockSpec((GLA_C, width), functools.partial(lambda i, c: (rev(i), c), c=cblk))
    full = lambda a: pl.BlockSpec(a.shape, functools.partial(lambda i, nd: (0,) * nd, nd=a.ndim))
    keep = lambda shape: pl.BlockSpec(shape, functools.partial(lambda i, nd: (0,) * nd, nd=len(shape)))
    return pl.pallas_call(
        body, name="gla_bwd", grid=(nc,),
        in_specs=[row(512, C_QG // 512), row(512, C_KG // 512), row(1024, C_VG // 1024), row(GLR_W, C_GLR // GLR_W),
                  row(1024, C_ZG // 1024), full(w2p), full(bg), full(gn), row(GH * GDV, 0),
                  pl.BlockSpec((1, GH, GDV, GDK), lambda i: (rev(i), 0, 0, 0)), row(GH * GDV, 0),
                  pl.BlockSpec(memory_space=pl.ANY)],
        out_specs=[row(GLA_GROUP_W, 0), keep((GLR_W, 512)), keep((1, 512)), keep((1, GDV))],
        out_shape=[S(dproj.shape, dproj.dtype), S((GLR_W, 512), F32), S((1, 512), F32), S((1, GDV), F32)],
        input_output_aliases={11: 0},
        scratch_shapes=[pltpu.VMEM((GH, GDV, GDK), F32), pltpu.VMEM((GLA_C, GH * GDK), F32)],
        compiler_params=pltpu.CompilerParams(dimension_semantics=("arbitrary",)),
    )(proj, proj, proj, proj, proj, w2p, bg, gn, o_gla, states, dbin, dproj)


RT = 512


def _rowchain(body, name, ins, outs, scratch=()):
    in_specs, args = [], []
    for spec in ins:
        if spec[0] == "tok":
            _, arr, width, cblk = spec
            in_specs.append(pl.BlockSpec((RT, width), functools.partial(lambda i, c: (i, c), c=cblk)))
        else:
            arr = spec[1]
            in_specs.append(pl.BlockSpec(arr.shape, functools.partial(lambda i, nd: (0,) * nd, nd=arr.ndim)))
        args.append(arr)
    out_specs, out_shape = [], []
    for spec in outs:
        if spec[0] == "tok":
            _, shape, dtype, width, cblk = spec
            out_specs.append(pl.BlockSpec((RT, width), functools.partial(lambda i, c: (i, c), c=cblk)))
        else:
            _, shape, dtype = spec
            out_specs.append(pl.BlockSpec(shape, functools.partial(lambda i, nd: (0,) * nd, nd=len(shape))))
        out_shape.append(S(shape, dtype))
    return pl.pallas_call(
        body, name=name, grid=(T // RT,), in_specs=in_specs, out_specs=out_specs, out_shape=out_shape,
        scratch_shapes=list(scratch), compiler_params=pltpu.CompilerParams(dimension_semantics=("arbitrary",)),
    )(*args)


def _tok(arr, width=None, cblk=0):
    return ("tok", arr, arr.shape[1] if width is None else width, cblk)


def _tok_out(dtype, width=D):
    return ("tok", (T, width), dtype, width, 0)


def _branches_fwd(ain, bin_, proj, x, w_att, w_gla, w_out):
    def body(ain_ref, bin_ref, g_ref, x_ref, wa_ref, wg_ref, wo_ref, ya_ref, yb_ref, y_ref, x1_ref):
        ya = _nn(ain_ref[...], wa_ref[...]).astype(BF16)
        yb = _nn(bin_ref[...], wg_ref[...]).astype(BF16)
        ya_ref[...] = ya
        yb_ref[...] = yb
        y = (_sigmoid(g_ref[:, :D]) * ya.astype(F32) + _sigmoid(g_ref[:, D:]) * yb.astype(F32)).astype(BF16)
        y_ref[...] = y
        x1_ref[...] = x_ref[...] + _nn(y, wo_ref[...])

    return _rowchain(body, "branches_fwd",
                     [_tok(ain), _tok(bin_), _tok(proj, 2 * D, C_GA // (2 * D)), _tok(x), ("all", w_att),
                      ("all", w_gla), ("all", w_out)],
                     [_tok_out(BF16), _tok_out(BF16), _tok_out(BF16), _tok_out(F32)])


def _accumulate(ref, part, first):
    @pl.when(first)
    def _():
        ref[...] = part

    @pl.when(jnp.logical_not(first))
    def _():
        ref[...] += part


def _ple_loss(x1, p, target, g2, w_pg, w_ple):
    def body(x1_ref, p_ref, t_ref, g_ref, wpg_ref, wple_ref, n2_ref, loss_ref, dout_ref, du_ref, dwple_ref, acc):
        first = pl.program_id(0) == 0
        x1 = x1_ref[...]
        r = lax.rsqrt(jnp.mean(x1 * x1, axis=-1, keepdims=True) + EPS)
        n2 = (x1 * r * g_ref[...]).astype(BF16)
        n2_ref[...] = n2
        pg = _sigmoid(_nn(n2, wpg_ref[...]))
        pb = p_ref[...].astype(BF16)
        e_ = _nn(pb, wple_ref[...])
        diff = x1 + e_ * pg - t_ref[...]
        _accumulate(acc, jnp.sum(diff * diff, axis=0, keepdims=True), first)
        dout = diff * (1.0 / D)
        dout_ref[...] = dout
        du_ref[...] = (dout * e_ * pg * (1.0 - pg)).astype(BF16)
        _accumulate(dwple_ref, _tn(pb, (dout * pg).astype(BF16)), first)
        loss_ref[...] = jnp.zeros((1, 128), F32) + jnp.sum(acc[...], axis=-1, keepdims=True) * (0.5 / D)

    return _rowchain(body, "ple_loss", [_tok(x1), _tok(p), _tok(target), ("all", g2), ("all", w_pg), ("all", w_ple)],
                     [_tok_out(BF16), ("acc", (1, 128), F32), _tok_out(F32), _tok_out(BF16), ("acc", (PLE, D), F32)],
                     scratch=[pltpu.VMEM((1, D), F32)])


def _ple_bwd(du, n2, y, x1, dout, g2, w_pg, w_out):
    def body(du_ref, n2_ref, y_ref, x1_ref, dout_ref, g_ref, wpg_ref, wo_ref, dx_ref, dy_ref, dg_ref, dwpg_ref,
             dwo_ref):
        first = pl.program_id(0) == 0
        x1 = x1_ref[...]
        r = lax.rsqrt(jnp.mean(x1 * x1, axis=-1, keepdims=True) + EPS)
        du_ = du_ref[...]
        dn = _nt(du_, wpg_ref[...])
        u = dn * g_ref[...]
        dx = dout_ref[...] + r * u - x1 * (r * r * r) * jnp.mean(u * x1, axis=-1, keepdims=True)
        dxb = dx.astype(BF16)
        dx_ref[...] = dx
        dy_ref[...] = _nt(dxb, wo_ref[...]).astype(BF16)
        _accumulate(dg_ref, jnp.sum(dn * x1 * r, axis=0, keepdims=True), first)
        _accumulate(dwpg_ref, _tn(n2_ref[...], du_), first)
        _accumulate(dwo_ref, _tn(y_ref[...], dxb), first)

    return _rowchain(body, "ple_bwd",
                     [_tok(du), _tok(n2), _tok(y), _tok(x1), _tok(dout), ("all", g2), ("all", w_pg), ("all", w_out)],
                     [_tok_out(F32), _tok_out(BF16), ("acc", (1, D), F32), ("acc", (D, D), F32), ("acc", (D, D), F32)])


def _branches_bwd(dy, ya, yb, ain, bin_, proj, w_att, w_gla):
    def body(dy_ref, ya_ref, yb_ref, ain_ref, bin_ref, g_ref, wa_ref, wg_ref, dg_ref, dain_ref, dbin_ref,
             dwa_ref, dwg_ref):
        first = pl.program_id(0) == 0
        dy_ = dy_ref[...].astype(F32)
        sa, sb = _sigmoid(g_ref[:, :D]), _sigmoid(g_ref[:, D:])
        dg_ref[:, :D] = (dy_ * ya_ref[...].astype(F32) * sa * (1.0 - sa)).astype(BF16)
        dg_ref[:, D:] = (dy_ * yb_ref[...].astype(F32) * sb * (1.0 - sb)).astype(BF16)
        dya = (dy_ * sa).astype(BF16)
        dyb = (dy_ * sb).astype(BF16)
        dain_ref[...] = _nt(dya, wa_ref[...]).astype(BF16)
        dbin_ref[...] = _nt(dyb, wg_ref[...]).astype(BF16)
        _accumulate(dwa_ref, _tn(ain_ref[...], dya), first)
        _accumulate(dwg_ref, _tn(bin_ref[...], dyb), first)

    gates = C_GA // (2 * D)
    return _rowchain(body, "branches_bwd",
                     [_tok(dy), _tok(ya), _tok(yb), _tok(ain), _tok(bin_), _tok(proj, 2 * D, gates), ("all", w_att),
                      ("all", w_gla)],
                     [("tok", (T, NCOL), BF16, 2 * D, gates), _tok_out(BF16, ATT_W), _tok_out(BF16),
                      ("acc", (ATT_W, D), F32), ("acc", (D, D), F32)])


def _peer(k):
    x, y, c = lax.axis_index("x"), lax.axis_index("y"), lax.axis_index("c")
    return (x ^ ((k >> 2) & 1), y ^ ((k >> 1) & 1), c ^ (k & 1))


def _my_index():
    return 4 * lax.axis_index("x") + 2 * lax.axis_index("y") + lax.axis_index("c")


def _peer_index(k):
    px, py, pc = _peer(k)
    return 4 * px + 2 * py + pc


def _pairwise_plan(src_of, dst_of, landed_of, own_src, own_dst):
    def plan(ins, outs, send, recv, local):
        n = len(ins)

        def own():
            return [pltpu.make_async_copy(own_src(ins[a]), own_dst(outs[a]), local.at[a]) for a in range(n)]

        def remote(k, a, src, dst):
            return pltpu.make_async_remote_copy(src_ref=src, dst_ref=dst, send_sem=send.at[k - 1, a],
                                                recv_sem=recv.at[k - 1, a], device_id=_peer(k), device_id_type=MESH)

        def sent():
            return [remote(k, a, src_of(ins[a], k), dst_of(outs[a])) for k in range(1, NDEV) for a in range(n)]

        def start():
            for cp in own() + sent():
                cp.start()

        def finish():
            for k in range(1, NDEV):
                for a in range(n):
                    remote(k, a, own_src(ins[a]), landed_of(outs[a], k)).wait_recv()
            for cp in sent():
                cp.wait_send()
            for cp in own():
                cp.wait()

        return start, finish

    return plan


def _pairwise_sems(n):
    return [pltpu.SemaphoreType.DMA((NDEV - 1, n)), pltpu.SemaphoreType.DMA((NDEV - 1, n)),
            pltpu.SemaphoreType.DMA((n,))]


def _gather_side(arrs):
    plan = _pairwise_plan(src_of=lambda i, k: i, dst_of=lambda o: o.at[_my_index()],
                          landed_of=lambda o, k: o.at[_peer_index(k)],
                          own_src=lambda i: i, own_dst=lambda o: o.at[_my_index()])
    return dict(arrs=arrs, out_shape=[S((NDEV,) + a.shape, a.dtype) for a in arrs],
                scratch=_pairwise_sems(len(arrs)), plan=plan)


def _exchange_side(arrs):
    plan = _pairwise_plan(src_of=lambda i, k: i.at[_peer_index(k)], dst_of=lambda o: o.at[_my_index()],
                          landed_of=lambda o, k: o.at[_peer_index(k)],
                          own_src=lambda i: i.at[_my_index()], own_dst=lambda o: o.at[_my_index()])
    return dict(arrs=arrs, out_shape=[S(a.shape, a.dtype) for a in arrs], scratch=_pairwise_sems(len(arrs)), plan=plan)


def _comm_call(side, name):
    n = len(side["arrs"])

    def body(*refs):
        start, finish = side["plan"](refs[:n], refs[n:2 * n], *refs[2 * n:])
        start()
        finish()

    hbm = pl.BlockSpec(memory_space=pl.ANY)
    return pl.pallas_call(body, name=name, in_specs=[hbm] * n, out_specs=[hbm] * n, out_shape=side["out_shape"],
                          scratch_shapes=side["scratch"])(*side["arrs"])


def _all_gather_by_chip(arrs, name):
    n = len(arrs)

    def body(*refs):
        ins, outs = refs[:n], refs[n:2 * n]
        send, recv, local = refs[2 * n:]
        x, y, c = lax.axis_index("x"), lax.axis_index("y"), lax.axis_index("c")
        me, sibling = (x, y, c), (x, y, 1 - c)
        chips = [(1 - x, y), (x, 1 - y), (1 - x, 1 - y)]

        def copy(k, a, block, to, src=None):
            px, py, pc = block
            slot = outs[a].at[4 * px + 2 * py + pc]
            return pltpu.make_async_remote_copy(
                src_ref=slot if src is None else src, dst_ref=slot, send_sem=send.at[k, a], recv_sem=recv.at[k, a],
                device_id=to, device_id_type=MESH)

        north = c == 1
        via = (jnp.where(north, 1 - x, x), jnp.where(north, y, 1 - y))
        onward = (jnp.where(north, x, 1 - x), jnp.where(north, 1 - y, y), c)
        mine = [pltpu.make_async_copy(ins[a], outs[a].at[4 * x + 2 * y + c], local.at[a]) for a in range(n)]
        first = []
        for a in range(n):
            first.append(copy(0, a, me, sibling, src=ins[a]))
            first += [copy(1 + j, a, me, (*chips[j], c), src=ins[a]) for j in range(2)]
        for cp in mine + first:
            cp.start()
        passed = []
        for j in range(2):
            for a in range(n):
                copy(1 + j, a, (*chips[j], c), me).wait_recv()
                passed.append(copy(4 + j, a, (*chips[j], c), sibling))
                passed[-1].start()
        for a in range(n):
            passed.append(copy(3, a, (*via, c), onward))
            passed[-1].start()
        for a in range(n):
            copy(3, a, (*chips[2], c), me).wait_recv()
            passed.append(copy(6, a, (*chips[2], c), sibling))
            passed[-1].start()
        for a in range(n):
            copy(0, a, sibling, me).wait_recv()
        for j, chip in enumerate(chips):
            for a in range(n):
                copy(4 + j, a, (*chip, 1 - c), me).wait_recv()
        for cp in first + passed:
            cp.wait_send()
        for cp in mine:
            cp.wait()

    hbm = pl.BlockSpec(memory_space=pl.ANY)
    return pl.pallas_call(
        body, name=name, in_specs=[hbm] * n, out_specs=[hbm] * n,
        out_shape=[S((NDEV,) + a.shape, a.dtype) for a in arrs],
        scratch_shapes=[pltpu.SemaphoreType.DMA((NDEV - 1, n)), pltpu.SemaphoreType.DMA((NDEV - 1, n)),
                        pltpu.SemaphoreType.DMA((n,))],
    )(*arrs)


NCHIP = 4


def _sibling_sum(src, name, tc=256):
    _, rows, cols = src.shape
    assert cols % tc == 0

    def body(src_ref, got_ref, out_ref, a_buf, b_buf, o_buf, send, recv, local):
        x, y, c = lax.axis_index("x"), lax.axis_index("y"), lax.axis_index("c")
        copies = [pltpu.make_async_remote_copy(
            src_ref=src_ref.at[2 * q + (1 - c)], dst_ref=got_ref.at[q], send_sem=send.at[q], recv_sem=recv.at[q],
            device_id=(x, y, 1 - c), device_id_type=MESH) for q in range(NCHIP)]
        for cp in copies:
            cp.start()
        tiles = [(q, pl.ds(t * tc, tc)) for q in range(NCHIP) for t in range(cols // tc)]

        def loads(n):
            q, tile = tiles[n]
            return [pltpu.make_async_copy(src_ref.at[2 * q + c, :, tile], a_buf.at[n % 2], local.at[n % 2, 0]),
                    pltpu.make_async_copy(got_ref.at[q, :, tile], b_buf.at[n % 2], local.at[n % 2, 1])]

        def store(n):
            q, tile = tiles[n]
            return pltpu.make_async_copy(o_buf.at[n % 2], out_ref.at[q, :, tile], local.at[n % 2, 2])

        def fetch(n):
            if n == 0 or tiles[n][0] != tiles[n - 1][0]:
                copies[tiles[n][0]].wait_recv()
            for cp in loads(n):
                cp.start()

        fetch(0)
        for n in range(len(tiles)):
            if n + 1 < len(tiles):
                fetch(n + 1)
            for cp in loads(n):
                cp.wait()
            if n >= 2:
                store(n - 2).wait()
            o_buf[n % 2] = (a_buf[n % 2].astype(F32) + b_buf[n % 2].astype(F32)).astype(BF16)
            store(n).start()
        store(len(tiles) - 2).wait()
        store(len(tiles) - 1).wait()
        for cp in copies:
            cp.wait_send()

    hbm = pl.BlockSpec(memory_space=pl.ANY)
    block = S((NCHIP, rows, cols), BF16)
    return pl.pallas_call(
        body, name=name, in_specs=[hbm], out_specs=[hbm, hbm], out_shape=[block, block],
        scratch_shapes=[pltpu.VMEM((2, rows, tc), BF16)] * 3
        + [pltpu.SemaphoreType.DMA((NCHIP,)), pltpu.SemaphoreType.DMA((NCHIP,)), pltpu.SemaphoreType.DMA((2, 3))],
    )(src)[1]


def _chips_side(arrs):
    def plan(ins, outs, send, recv, local):
        n = len(ins)

        def places():
            x, y, c = lax.axis_index("x"), lax.axis_index("y"), lax.axis_index("c")
            return 2 * x + y, c, [(1 - x, y), (x, 1 - y), (1 - x, 1 - y)]

        def own():
            here, _, _ = places()
            return [pltpu.make_async_copy(ins[a].at[here], outs[a].at[here], local.at[a]) for a in range(n)]

        def remote(j, a, src_slot, dst_slot):
            _, c, chips = places()
            cx, cy = chips[j]
            return pltpu.make_async_remote_copy(
                src_ref=ins[a].at[src_slot], dst_ref=outs[a].at[dst_slot], send_sem=send.at[j, a],
                recv_sem=recv.at[j, a], device_id=(cx, cy, c), device_id_type=MESH)

        def sent():
            here, _, chips = places()
            return [remote(j, a, 2 * cx + cy, here) for j, (cx, cy) in enumerate(chips) for a in range(n)]

        def start():
            for cp in own() + sent():
                cp.start()

        def finish():
            here, _, chips = places()
            for j, (cx, cy) in enumerate(chips):
                for a in range(n):
                    remote(j, a, here, 2 * cx + cy).wait_recv()
            for cp in sent():
                cp.wait_send()
            for cp in own():
                cp.wait()

        return start, finish

    n = len(arrs)
    return dict(arrs=arrs, out_shape=[S(a.shape, a.dtype) for a in arrs],
                scratch=[pltpu.SemaphoreType.DMA((NCHIP - 1, n)), pltpu.SemaphoreType.DMA((NCHIP - 1, n)),
                         pltpu.SemaphoreType.DMA((n,))], plan=plan)


def _adamw_shards(parts, places):
    n_src = len(parts)

    def body(*refs):
        srcs, rest = refs[:n_src], refs[n_src:]
        for j, (src, rows, cols, _) in enumerate(places):
            w_ref, m_ref, v_ref = rest[3 * j:3 * j + 3]
            outs = rest[3 * len(places) + 4 * j:3 * len(places) + 4 * j + 4]
            p_ref = srcs[src]
            g = p_ref[0, rows, cols].astype(F32)
            for s in range(1, p_ref.shape[0]):
                g = g + p_ref[s, rows, cols].astype(F32)
            delta, m_new, v_new = _adam_math(g, w_ref[0], m_ref[0], v_ref[0])
            for ref, val in zip(outs, (g, delta, m_new, v_new)):
                ref[0] = val

    flat = [a for place in places for a in place[3]]
    return pl.pallas_call(
        body, name="adam_shards",
        out_shape=[S(place[3][0].shape, F32) for place in places for _ in range(4)],
    )(*parts, *flat)


def _adam_math(g, w, m, v):
    c1 = 1.0 - ADAM_B1 ** ADAM_STEP
    c2 = 1.0 - ADAM_B2 ** ADAM_STEP
    m_new = ADAM_B1 * m + (1.0 - ADAM_B1) * g
    v_new = ADAM_B2 * v + (1.0 - ADAM_B2) * (g * g)
    return -ADAM_LR * ((m_new / c1) / (jnp.sqrt(v_new / c2) + ADAM_EPS) + ADAM_WD * w), m_new, v_new


def _adamw_small(parts, params, loss_parts):
    n = len(params)

    def body(*refs):
        p_refs, rest = refs[:n], refs[n + 1:]
        total = refs[n][0]
        for s in range(1, NDEV):
            total = total + refs[n][s]
        refs[-1][...] = total
        for j in range(n):
            w_ref, m_ref, v_ref = rest[3 * j:3 * j + 3]
            g_ref, d_ref, mo_ref, vo_ref = rest[3 * n + 4 * j:3 * n + 4 * j + 4]
            width = w_ref.shape[1]
            g = p_refs[j][0]
            for s in range(1, NDEV):
                g = g + p_refs[j][s]
            g = g[:, :width]
            delta, m_new, v_new = _adam_math(g, w_ref[...], m_ref[...], v_ref[...])
            g_ref[...] = g
            d_ref[...] = delta
            mo_ref[...] = m_new
            vo_ref[...] = v_new

    flat = [a for group in params for a in group]
    return pl.pallas_call(
        body, name="adam_small",
        out_shape=[S(group[0].shape, F32) for group in params for _ in range(4)] + [S((1, 128), F32)],
    )(*parts, loss_parts, *flat)


def _adamw_rows(parts, w, m, v, name, tc=128):
    rows, _, cols = w.shape
    nparts = parts.shape[0]
    nsteps = cols // tc

    def body(p_ref, w_hbm, m_hbm, v_hbm, g_hbm, d_hbm, mo_hbm, vo_hbm, inbuf, outbuf, insem, outsem):
        i = pl.program_id(0)
        slot = i & 1

        def view(ref, step):
            return ref.at[:, 0, pl.ds(pl.multiple_of(step * tc, tc), tc)]

        def fetch(step, sl):
            return [pltpu.make_async_copy(view(src, step), inbuf.at[sl, k], insem.at[sl, k])
                    for k, src in enumerate((w_hbm, m_hbm, v_hbm))]

        def write(step, sl):
            return [pltpu.make_async_copy(outbuf.at[sl, k], view(dst, step), outsem.at[sl, k])
                    for k, dst in enumerate((g_hbm, d_hbm, mo_hbm, vo_hbm))]

        @pl.when(i == 0)
        def _():
            for cp in fetch(0, 0):
                cp.start()

        @pl.when(i + 1 < nsteps)
        def _():
            for cp in fetch(i + 1, 1 - slot):
                cp.start()

        for cp in fetch(i, slot):
            cp.wait()

        @pl.when(i >= 2)
        def _():
            for cp in write(i - 2, slot):
                cp.wait()

        g = p_ref[0].astype(F32)
        for s in range(1, nparts):
            g = g + p_ref[s].astype(F32)
        g = g[:rows]
        delta, m_new, v_new = _adam_math(g, inbuf[slot, 0], inbuf[slot, 1], inbuf[slot, 2])
        for k, val in enumerate((g, delta, m_new, v_new)):
            outbuf[slot, k] = val
        for cp in write(i, slot):
            cp.start()

        @pl.when(i == nsteps - 1)
        def _():
            for cp in write(i - 1, 1 - slot) + write(i, slot):
                cp.wait()

    hbm = pl.BlockSpec(memory_space=pl.ANY)
    assert nsteps >= 2
    return pl.pallas_call(
        body, name=name, grid=(nsteps,),
        in_specs=[pl.BlockSpec((nparts, parts.shape[1], tc), lambda i: (0, 0, i)), hbm, hbm, hbm],
        out_specs=[hbm] * 4, out_shape=[S((rows, 1, cols), F32)] * 4,
        scratch_shapes=[pltpu.VMEM((2, 3, rows, tc), F32), pltpu.VMEM((2, 4, rows, tc), F32),
                        pltpu.SemaphoreType.DMA((2, 3)), pltpu.SemaphoreType.DMA((2, 4))],
        compiler_params=pltpu.CompilerParams(dimension_semantics=("arbitrary",)),
    )(parts, w, m, v)


SLAB = 1296
REMAP_RUNS = 3
_PIECES = ((O_QA, O_ZA, C_QA), (O_ZA, O_QG, C_ZA), (O_QG, O_GLR, C_QG), (O_GLR, O_ZG, C_GLR), (O_ZG, O_GA, C_ZG),
           (O_GA, O_END, C_GA))


def _slab_row_of_aligned(a):
    for o0, o1, a0 in _PIECES:
        if a0 <= a < a0 + o1 - o0:
            c = o0 + a - a0
            return SLAB * (c // W_IN_SHARD) + c % W_IN_SHARD
    return -1


def _aligned_row_of_slab(r):
    d, l = divmod(r, SLAB)
    if l >= W_IN_SHARD:
        return -1
    c = d * W_IN_SHARD + l
    for o0, o1, a0 in _PIECES:
        if o0 <= c < o1:
            return a0 + c - o0
    raise AssertionError(c)


def _remap_table(row_of, n_out, block, n_src):
    win = block + 16
    table = []
    for b in range(n_out // block):
        runs = []
        for i in range(block):
            s = row_of(b * block + i)
            if s < 0:
                continue
            if runs and runs[-1][0] + runs[-1][2] == s and runs[-1][1] + runs[-1][2] == i:
                runs[-1][2] += 1
            else:
                runs.append([s, i, 1])
        assert len(runs) <= REMAP_RUNS, (b, runs)
        row = []
        for s, i, n in runs:
            w = min(s // 16 * 16, n_src - win)
            assert 0 <= s - w and s - w + n <= win
            row += [w, s - w, i, n]
        table.append(row + [0] * (4 * REMAP_RUNS - len(row)))
    return table


def _remap_rows(src, row_of, n_out, block, name):
    n_src, cols = src.shape
    nb, win = n_out // block, block + 16
    table = jnp.asarray(_remap_table(row_of, n_out, block, n_src), jnp.int32)

    def body(t_ref, src_hbm, o_ref, buf, acc, sem):
        b = pl.program_id(0)
        slot = b & 1

        def each_run(step, sl, act):
            for k in range(REMAP_RUNS):
                @pl.when(t_ref[step, 4 * k + 3] > 0)
                def _():
                    start = pl.multiple_of(t_ref[step, 4 * k], 16)
                    act(pltpu.make_async_copy(src_hbm.at[pl.ds(start, win)], buf.at[sl, k], sem.at[sl, k]))

        @pl.when(b == 0)
        def _():
            each_run(0, 0, lambda cp: cp.start())

        @pl.when(b + 1 < nb)
        def _():
            each_run(b + 1, 1 - slot, lambda cp: cp.start())

        each_run(b, slot, lambda cp: cp.wait())
        acc[...] = jnp.zeros_like(acc)
        row = lax.broadcasted_iota(jnp.int32, (block, win), 0)
        col = lax.broadcasted_iota(jnp.int32, (block, win), 1)
        for k in range(REMAP_RUNS):
            shift, first, count = (t_ref[b, 4 * k + j] for j in (1, 2, 3))

            @pl.when(count > 0)
            def _():
                pick = (col == row - first + shift) & (row >= first) & (row < first + count)
                acc[...] += _nn(jnp.where(pick, 1.0, 0.0).astype(BF16), buf[slot, k])

        o_ref[...] = acc[...].astype(o_ref.dtype)

    return pl.pallas_call(
        body, name=name,
        grid_spec=pltpu.PrefetchScalarGridSpec(
            num_scalar_prefetch=1, grid=(nb,), in_specs=[pl.BlockSpec(memory_space=pl.ANY)],
            out_specs=pl.BlockSpec((block, cols), lambda b, t: (b, 0)),
            scratch_shapes=[pltpu.VMEM((2, REMAP_RUNS, win, cols), src.dtype), pltpu.VMEM((block, cols), F32),
                            pltpu.SemaphoreType.DMA((2, REMAP_RUNS))]),
        out_shape=S((n_out, cols), src.dtype),
        compiler_params=pltpu.CompilerParams(dimension_semantics=("arbitrary",)),
    )(table, src)


def _col_blocks(w, width):
    return w.reshape(w.shape[0], NDEV, width).transpose(1, 0, 2)


def _from_col_blocks(w):
    return w.transpose(1, 0, 2).reshape(w.shape[1], NDEV * w.shape[2])


def _local_step(x2, p2, pos, tgt, norm_g, qk_norm_q, qk_norm_k, gla_gate_b, gla_norm_g, ple_norm_g, w_al,
                weights=None, proj_side=None, unpack=None, dw_side_of=None, dh_side_of=None):
    half = ROT_DIM // 2
    inv8 = jnp.power(jnp.float32(ROPE_THETA), -jnp.arange(half, dtype=F32) * 2.0 / ROT_DIM)
    inv = jnp.tile(jnp.concatenate([inv8, inv8, jnp.zeros((HD - ROT_DIM,), F32)]), 2).reshape(1, 128)
    gq = jnp.tile(qk_norm_q, (1, 2))
    gk = jnp.tile(qk_norm_k, (1, 2))

    proj, h, got = _proj_rms(x2, norm_g, w_al, proj_side)
    if proj_side is not None:
        weights = unpack(got)
    w2p, w_att_f, w_gla_f, w_out_f, w_pg_f, w_ple_f = weights
    qkv = _qk_prep(proj, pos, inv, gq, gk)
    fwd = [_att_fwd(qkv[g], qkv[3 + g], qkv[6 + g], g, f"att_fwd{g}") for g in range(3)]
    att, lse, ain = _att_merge([f[0] for f in fwd], [f[1] for f in fwd], proj)
    o_gla, bin_, states = _gla_fwd(proj, w2p, gla_gate_b, gla_norm_g)
    ya, yb, y, x1 = _branches_fwd(ain, bin_, proj, x2, w_att_f, w_gla_f, w_out_f)
    n2, loss_v, dout, du, dw_ple = _ple_loss(x1, p2, tgt, ple_norm_g, w_pg_f, w_ple_f)

    dx1, dy, dg_ple, dw_pg, dw_out = _ple_bwd(du, n2, y, x1, dout, ple_norm_g, w_pg_f, w_out_f)
    dproj, dain, dbin, dw_att, dw_gla = _branches_bwd(dy, ya, yb, ain, bin_, proj, w_att_f, w_gla_f)
    dproj, da0, da1, da2, at1, at2, ls1, ls2 = _att_gate_bwd(dain, att, lse, proj, dproj)
    datts, atts, lses = (da0, da1, da2), (att[None], at1, at2), (lse[None], ls1, ls2)
    dproj, dw2, dbg, dgn = _gla_bwd(proj, w2p, gla_gate_b, gla_norm_g, o_gla, states, dbin, dproj)
    bwd = [_att_bwd(qkv[g], qkv[3 + g], qkv[6 + g], datts[g], atts[g], lses[g], g, f"att_bwd{g}") for g in range(3)]
    dproj, dgq, dgk = _qk_bwd(proj, pos, inv, gq, gk, [b[0] for b in bwd], [b[1] for b in bwd],
                              [b[2] for b in bwd], dproj)
    out = dict(loss=loss_v, dw2=dw2, dw_att=dw_att, dw_gla=dw_gla, dw_out=dw_out, dw_pg=dw_pg, dw_ple=dw_ple,
               dgq=dgq, dgk=dgk, dbg=dbg, dgn=dgn, dg_ple=dg_ple)
    if dw_side_of is None:
        dw_al = _mm(dproj, h, mode="tn", name="dw_in", tm=1536, tn=D, tk=T, out_dtype=BF16)
    else:
        dw_al, out["dw_side"] = _mm(dproj, h, mode="tn", name="dw_in", tm=1536, tn=D, tk=T, out_dtype=BF16,
                                    side=dw_side_of(out))
    grad_x, dg_norm, out["dh_side"] = _dh_rms(dproj, w_al, x2, norm_g, dx1,
                                              None if dh_side_of is None else dh_side_of(dw_al))
    out.update(grad_x=grad_x, dw_al=dw_al, dg_norm=dg_norm)
    return out


def kernel(x, p, positions, norm_g, w_in, qk_norm_q, qk_norm_k, gla_gate_w2, gla_gate_b, gla_norm_g, w_att_proj, w_gla_proj, w_out, ple_norm_g, w_ple_gate, w_ple, loss_target, m_norm_g, m_w_in, m_qk_norm_q, m_qk_norm_k, m_gla_gate_w2, m_gla_gate_b, m_gla_norm_g, m_w_att_proj, m_w_gla_proj, m_w_out, m_ple_norm_g, m_w_ple_gate, m_w_ple, v_norm_g, v_w_in, v_qk_norm_q, v_qk_norm_k, v_gla_gate_w2, v_gla_gate_b, v_gla_norm_g, v_w_att_proj, v_w_gla_proj, v_w_out, v_ple_norm_g, v_w_ple_gate, v_w_ple):
    x2, p2, tgt = x[0], p[0, 0], loss_target[0]
    pos = positions.astype(F32).reshape(T, 1)

    rows3 = jnp.stack([w_gla_proj[0], w_out[0], w_ple_gate[0]]).astype(BF16)
    cols3 = jnp.concatenate([w_att_proj[0], w_ple[0], jnp.pad(gla_gate_w2[0], ((0, 0), (0, 64)))], axis=0).astype(BF16)
    mine = jnp.pad(w_in[0].T.astype(BF16), ((0, SLAB - W_IN_SHARD), (0, 0)))
    (g_in,) = _all_gather_by_chip([mine], "gather_w_in")
    w_al = _remap_rows(g_in.reshape(NDEV * SLAB, D), _slab_row_of_aligned, NCOL, 256, "align_w_in")

    def unpack(got):
        g_rows, g_cols = got
        w2_f = _from_col_blocks(g_cols[:, 768:784, :64])
        return (jnp.pad(w2_f, ((0, GLR_W - GLR_N), (0, 0))), _from_col_blocks(g_cols[:, :512]),
                g_rows[:, 0].reshape(D, D), g_rows[:, 1].reshape(D, D), g_rows[:, 2].reshape(D, D),
                _from_col_blocks(g_cols[:, 512:768]))

    def dw_side_of(g):
        s_rows = jnp.concatenate([g[k].reshape(NDEV, 128, D) for k in ("dw_gla", "dw_out", "dw_pg")], axis=1)
        s_cols = jnp.concatenate([_col_blocks(g["dw_att"], 128), _col_blocks(g["dw_ple"], 128),
                                  jnp.pad(_col_blocks(g["dw2"][:GLR_N], 64), ((0, 0), (0, 0), (0, 64)))], axis=1)
        return _exchange_side([s_rows.astype(BF16), s_cols.astype(BF16)])

    def dh_side_of(dw_al):
        s_in = _remap_rows(dw_al, _aligned_row_of_slab, NDEV * SLAB, 432, "shard_dw_in").reshape(NDEV, SLAB, D)
        return _chips_side([_sibling_sum(s_in, "sibling_sum")])

    loc = _local_step(x2, p2, pos, tgt, norm_g, qk_norm_q, qk_norm_k, gla_gate_b, gla_norm_g, ple_norm_g, w_al,
                      proj_side=_gather_side([rows3, cols3]), unpack=unpack, dw_side_of=dw_side_of,
                      dh_side_of=dh_side_of)
    loss_v, grad_x = loc["loss"], loc["grad_x"]
    dg_norm, dgq, dgk, dbg, dgn, dg_ple = (loc[k] for k in ("dg_norm", "dgq", "dgk", "dbg", "dgn", "dg_ple"))
    r_rows, r_cols = loc["dw_side"]
    (r_in,) = loc["dh_side"]

    r_small = _comm_call(_gather_side([dg_norm, dgq, dgk, dbg, dgn, dg_ple, loss_v]), "gather_small")

    outs = {}

    rows_of = lambda a: jnp.transpose(a, (2, 0, 1))
    outs["w_in"] = [jnp.transpose(o, (1, 2, 0))[0] for o in
                    _adamw_rows(r_in, rows_of(w_in), rows_of(m_w_in), rows_of(v_w_in), "adam_w_in")]
    places = (("w_gla_proj", 0, slice(0, 128), slice(None), (w_gla_proj, m_w_gla_proj, v_w_gla_proj)),
              ("w_out", 0, slice(128, 256), slice(None), (w_out, m_w_out, v_w_out)),
              ("w_ple_gate", 0, slice(256, 384), slice(None), (w_ple_gate, m_w_ple_gate, v_w_ple_gate)),
              ("w_att_proj", 1, slice(0, 512), slice(None), (w_att_proj, m_w_att_proj, v_w_att_proj)),
              ("w_ple", 1, slice(512, 768), slice(None), (w_ple, m_w_ple, v_w_ple)),
              ("gla_gate_w2", 1, slice(768, 784), slice(0, 64), (gla_gate_w2, m_gla_gate_w2, v_gla_gate_w2)))
    res = _adamw_shards([r_rows, r_cols], [place[1:] for place in places])
    for j, place in enumerate(places):
        outs[place[0]] = [o[0] for o in res[4 * j:4 * j + 4]]
    small = ((norm_g, m_norm_g, v_norm_g), (qk_norm_q, m_qk_norm_q, v_qk_norm_q), (qk_norm_k, m_qk_norm_k, v_qk_norm_k),
             (gla_gate_b, m_gla_gate_b, v_gla_gate_b), (gla_norm_g, m_gla_norm_g, v_gla_norm_g),
             (ple_norm_g, m_ple_norm_g, v_ple_norm_g))
    sm = _adamw_small(r_small[:6], small, r_small[6])
    for j, nm in enumerate(("norm_g", "qk_norm_q", "qk_norm_k", "gla_gate_b", "gla_norm_g", "ple_norm_g")):
        outs[nm] = [o[0] for o in sm[4 * j:4 * j + 4]]

    loss = sm[-1][0, 0]
    order = ["norm_g", "w_in", "qk_norm_q", "qk_norm_k", "gla_gate_w2", "gla_gate_b", "gla_norm_g", "w_att_proj",
             "w_gla_proj", "w_out", "ple_norm_g", "w_ple_gate", "w_ple"]
    result = [loss, grad_x[None]]
    for i in range(4):
        result += [outs[nm][i][None] for nm in order]
    return tuple(result)
```

```python
import functools

import jax
import jax.numpy as jnp
from jax import lax
from jax.experimental import pallas as pl
from jax.experimental.pallas import tpu as pltpu

F32 = jnp.float32
BF16 = jnp.bfloat16
S = jax.ShapeDtypeStruct

T = 4096
D = 1024
NDEV = 8
HD = 64
ATT_W = 512
ATT_QKV = 1536
DILATIONS = (1, 4, 16)
BLK = 128
GH, GDK, GDV = 4, 128, 256
GLA_C = 128
PLE = 256
EPS = 1e-6
ROT_DIM = 16
ROPE_THETA = 500000.0
GLA_TAU = 16.0
W_IN_SHARD = 1282

C_QG, C_KG, C_VG, C_ZG, C_GLR, C_ZA, C_GA, C_GB, C_QA, C_KA, C_VA = (
    0, 512, 1024, 2048, 3072, 3584, 4096, 5120, 6144, 7680, 9216)
GLA_GROUP_W = 3584
GLR_W = 512
NCOL = 10752
GLR_N = 16
O_QA, O_ZA, O_QG, O_GLR, O_ZG, O_GA, O_END = 0, 4608, 5120, 7168, 7184, 8208, 10256

ADAM_LR, ADAM_B1, ADAM_B2, ADAM_EPS, ADAM_WD, ADAM_STEP = 0.001, 0.9, 0.999, 1e-08, 0.01, 10

MESH = pl.DeviceIdType.MESH


def _sigmoid(z):
    return 1.0 / (1.0 + jnp.exp(-z))


def _dot(a, b, dims):
    return lax.dot_general(a, b, (dims, ((), ())), preferred_element_type=F32)


def _nn(a, b):
    return _dot(a, b, ((1,), (0,)))


def _nt(a, b):
    return _dot(a, b, ((1,), (1,)))


def _tn(a, b):
    return _dot(a, b, ((0,), (0,)))


def _mm(a, b, *, mode, name, tm, tn, tk, out_dtype=F32, res=None, side=None):
    if mode == "nn":
        (m, k), n = a.shape, b.shape[1]
        a_spec = pl.BlockSpec((tm, tk), lambda i, j, l: (i, l))
        b_spec = pl.BlockSpec((tk, tn), lambda i, j, l: (l, j))
        dot = _nn
    elif mode == "nt":
        (m, k), n = a.shape, b.shape[0]
        a_spec = pl.BlockSpec((tm, tk), lambda i, j, l: (i, l))
        b_spec = pl.BlockSpec((tn, tk), lambda i, j, l: (j, l))
        dot = _nt
    else:
        (k, m), n = a.shape, b.shape[1]
        a_spec = pl.BlockSpec((tk, tm), lambda i, j, l: (l, i))
        b_spec = pl.BlockSpec((tk, tn), lambda i, j, l: (l, j))
        dot = _tn
    assert m % tm == 0 and n % tn == 0 and k % tk == 0, (name, m, n, k)
    grid = (m // tm, n // tn, k // tk)
    nk = grid[2]
    o_spec = pl.BlockSpec((tm, tn), lambda i, j, l: (i, j))
    in_specs = [a_spec, b_spec]
    args = [a, b]
    if res is not None:
        in_specs.append(o_spec)
        args.append(res)
    n_in = len(args)
    n_side = 0 if side is None else len(side["arrs"])
    hbm = pl.BlockSpec(memory_space=pl.ANY)

    def body(*refs):
        a_ref, b_ref = refs[0], refs[1]
        r_ref = refs[2] if res is not None else None
        o_ref = refs[n_in + n_side]
        scratch = refs[n_in + 2 * n_side + 1:]
        if side is not None:
            start, finish_side = side["plan"](refs[n_in:n_in + n_side], refs[n_in + n_side + 1:n_in + 2 * n_side + 1],
                                              *scratch[1 if nk > 1 else 0:])
            ids = [pl.program_id(d) for d in range(3)]

            @pl.when((ids[0] == 0) & (ids[1] == 0) & (ids[2] == 0))
            def _():
                start()

        part = dot(a_ref[...].astype(BF16), b_ref[...].astype(BF16))

        def finish(val):
            if r_ref is not None:
                val = val + r_ref[...]
            o_ref[...] = val.astype(out_dtype)

        if nk == 1:
            finish(part)
        else:
            acc = scratch[0]
            l = pl.program_id(2)

            @pl.when(l == 0)
            def _():
                acc[...] = part

            @pl.when(l > 0)
            def _():
                acc[...] += part

            @pl.when(l == nk - 1)
            def _():
                finish(acc[...])

        if side is not None:
            @pl.when((ids[0] == grid[0] - 1) & (ids[1] == grid[1] - 1) & (ids[2] == grid[2] - 1))
            def _():
                finish_side()

    sems = [] if side is None else side["scratch"]
    outs = pl.pallas_call(
        body, name=name, grid=grid,
        in_specs=in_specs + [hbm] * n_side, out_specs=[o_spec] + [hbm] * n_side,
        out_shape=[S((m, n), out_dtype)] + ([] if side is None else side["out_shape"]),
        scratch_shapes=([pltpu.VMEM((tm, tn), F32)] if nk > 1 else []) + sems,
        compiler_params=pltpu.CompilerParams(
            dimension_semantics=("arbitrary",) * 3 if side is not None else ("parallel", "parallel", "arbitrary")),
    )(*args, *([] if side is None else side["arrs"]))
    return outs[0] if side is None else (outs[0], outs[1:])


def _side_parts(side, refs, n_in, n_out):
    n_side = 0 if side is None else len(side["arrs"])
    scratch = refs[n_in + n_out + 2 * n_side:]
    if side is None:
        return (lambda: None), (lambda: None), scratch
    start, finish = side["plan"](refs[n_in:n_in + n_side], refs[n_in + n_side + n_out:n_in + n_out + 2 * n_side],
                                 *scratch[len(scratch) - len(side["scratch"]):])
    return start, finish, scratch


def _proj_rms(x, g, wt, side=None):
    tm, tn = 1024, 1536
    grid = (T // tm, NCOL // tn)
    n_side = 0 if side is None else len(side["arrs"])
    hbm = pl.BlockSpec(memory_space=pl.ANY)

    def body(*refs):
        x_ref, g_ref, w_ref = refs[:3]
        o_ref, h_ref = refs[3 + n_side], refs[4 + n_side]
        start, finish, _ = _side_parts(side, refs, 3, 2)
        i, j = pl.program_id(0), pl.program_id(1)

        @pl.when((i == 0) & (j == 0))
        def _():
            start()

        @pl.when(j == 0)
        def _():
            xf = x_ref[...]
            r = lax.rsqrt(jnp.mean(xf * xf, axis=-1, keepdims=True) + EPS)
            h_ref[...] = (xf * r * g_ref[...]).astype(BF16)

        o_ref[...] = _nt(h_ref[...], w_ref[...])

        @pl.when((i == grid[0] - 1) & (j == grid[1] - 1))
        def _():
            finish()

    outs = pl.pallas_call(
        body, name="proj", grid=grid,
        in_specs=[pl.BlockSpec((tm, D), lambda i, j: (i, 0)), pl.BlockSpec((1, D), lambda i, j: (0, 0)),
                  pl.BlockSpec((tn, D), lambda i, j: (j, 0))] + [hbm] * n_side,
        out_specs=[pl.BlockSpec((tm, tn), lambda i, j: (i, j)), pl.BlockSpec((tm, D), lambda i, j: (i, 0))] + [hbm] * n_side,
        out_shape=[S((T, NCOL), F32), S((T, D), BF16)] + ([] if side is None else side["out_shape"]),
        scratch_shapes=[] if side is None else side["scratch"],
        compiler_params=pltpu.CompilerParams(dimension_semantics=("arbitrary", "arbitrary")),
    )(x, g, wt, *([] if side is None else side["arrs"]))
    return outs[0], outs[1], outs[2:]


def _dh_rms(dproj, wt, x, g, skip, side=None):
    tm, tk = 1024, 2688
    grid = (T // tm, NCOL // tk)
    n_side = 0 if side is None else len(side["arrs"])
    hbm = pl.BlockSpec(memory_space=pl.ANY)

    def body(*refs):
        a_ref, w_ref, x_ref, g_ref, s_ref = refs[:5]
        dx_ref, dg_ref = refs[5 + n_side], refs[6 + n_side]
        start, finish, scratch = _side_parts(side, refs, 5, 2)
        acc = scratch[0]
        i, l = pl.program_id(0), pl.program_id(1)

        @pl.when((i == 0) & (l == 0))
        def _():
            start()

        part = _nn(a_ref[...], w_ref[...])

        @pl.when(l == 0)
        def _():
            acc[...] = part

        @pl.when(l > 0)
        def _():
            acc[...] += part

        @pl.when(l == grid[1] - 1)
        def _():
            xf = x_ref[...]
            r = lax.rsqrt(jnp.mean(xf * xf, axis=-1, keepdims=True) + EPS)
            dn = acc[...]
            u = dn * g_ref[...]
            dx_ref[...] = s_ref[...] + r * u - xf * (r * r * r) * jnp.mean(u * xf, axis=-1, keepdims=True)
            dg = jnp.sum(dn * xf * r, axis=0, keepdims=True)

            @pl.when(i == 0)
            def _():
                dg_ref[...] = dg

            @pl.when(i > 0)
            def _():
                dg_ref[...] += dg

        @pl.when((i == grid[0] - 1) & (l == grid[1] - 1))
        def _():
            finish()

    tok = pl.BlockSpec((tm, D), lambda i, l: (i, 0))
    outs = pl.pallas_call(
        body, name="dh", grid=grid,
        in_specs=[pl.BlockSpec((tm, tk), lambda i, l: (i, l)), pl.BlockSpec((tk, D), lambda i, l: (l, 0)), tok,
                  pl.BlockSpec((1, D), lambda i, l: (0, 0)), tok] + [hbm] * n_side,
        out_specs=[tok, pl.BlockSpec((1, D), lambda i, l: (0, 0))] + [hbm] * n_side,
        out_shape=[S((T, D), F32), S((1, D), F32)] + ([] if side is None else side["out_shape"]),
        scratch_shapes=[pltpu.VMEM((tm, D), F32)] + ([] if side is None else side["scratch"]),
        compiler_params=pltpu.CompilerParams(dimension_semantics=("arbitrary", "arbitrary")),
    )(dproj, wt, x, g, skip, *([] if side is None else side["arrs"]))
    return outs[0], outs[1], outs[2:]


def _rot_tables(pos_ref, inv_ref):
    lane = lax.broadcasted_iota(jnp.int32, (1, 128), 1) % HD
    ang = pos_ref[...] * inv_ref[...]
    cos, sin = jnp.cos(ang), jnp.sin(ang)
    c = jnp.where(lane < ROT_DIM, cos, 1.0)
    sp = jnp.where((lane >= ROT_DIM // 2) & (lane < ROT_DIM), sin, 0.0)
    sm = jnp.where(lane < ROT_DIM // 2, -sin, 0.0)
    return c, sp, sm


def _head_sums(v):
    same = (lax.broadcasted_iota(jnp.int32, (128, 128), 0) < HD) == (lax.broadcasted_iota(jnp.int32, (128, 128), 1) < HD)
    ones = jnp.where(same, 1.0, 0.0).astype(BF16)
    hi = v.astype(BF16)
    lo = (v - hi.astype(F32)).astype(BF16)
    return _nn(hi, ones) + _nn(lo, ones)


def _pair_norm(t):
    return lax.rsqrt(_head_sums(t * t) * (1.0 / HD) + EPS)


def _pair_mean(t):
    return _head_sums(t) * (1.0 / HD)


TT = 256
NCH = ATT_QKV // 128


def _res_shape(grp, dtype):
    return S((DILATIONS[grp], T // DILATIONS[grp], ATT_W), dtype)


def _res_spec(grp):
    dil = DILATIONS[grp]
    return pl.BlockSpec((dil, TT // dil, ATT_W), lambda i: (0, i, 0))


def _to_residues(sc, j, dst_ref, dil, cols):
    n = TT // dil
    for r in range(dil):
        rows = sc[j] if dil == 1 else sc.at[j][pl.ds(r, n, stride=dil), :]
        dst_ref[r, :, cols] = rows.astype(dst_ref.dtype)


def _from_residues(src_ref, cols, sc, j, dil):
    n = TT // dil
    for r in range(dil):
        if dil == 1:
            sc[j] = src_ref[r, :, cols]
        else:
            sc.at[j][pl.ds(r, n, stride=dil), :] = src_ref[r, :, cols]


def _tok_spec(width, cblk=0):
    return pl.BlockSpec((TT, width), functools.partial(lambda i, c: (i, c), c=cblk))


def _const_spec(arr_or_shape):
    shape = arr_or_shape if isinstance(arr_or_shape, tuple) else arr_or_shape.shape
    return pl.BlockSpec(shape, functools.partial(lambda i, nd: (0,) * nd, nd=len(shape)))


def _qk_prep(proj, pos, inv, gq, gk):
    def body(q_ref, k_ref, v_ref, pos_ref, inv_ref, gq_ref, gk_ref, *rest):
        outs, sc = rest[:9], rest[9]
        c, sp, sm = _rot_tables(pos_ref, inv_ref)
        for which, (src, g_ref) in enumerate(((q_ref, gq_ref), (k_ref, gk_ref), (v_ref, None))):
            if g_ref is not None:
                g = jnp.broadcast_to(g_ref[...] * ((HD ** -0.5) if which == 0 else 1.0), c.shape)
                cg, spg, smg = c * g, sp * pltpu.roll(g, 8, 1), sm * pltpu.roll(g, 120, 1)
            for j in range(NCH):
                t = src[:, j * 128:(j + 1) * 128]
                if g_ref is not None:
                    t = _pair_norm(t) * (t * cg + pltpu.roll(t, 8, 1) * spg + pltpu.roll(t, 120, 1) * smg)
                sc[j] = t
            for j in range(NCH):
                grp, sub = divmod(j * 128, ATT_W)
                _to_residues(sc, j, outs[which * 3 + grp], DILATIONS[grp], slice(sub, sub + 128))

    return pl.pallas_call(
        body, name="qk_prep", grid=(T // TT,),
        in_specs=[_tok_spec(ATT_QKV, C_QA // ATT_QKV), _tok_spec(ATT_QKV, C_KA // ATT_QKV),
                  _tok_spec(ATT_QKV, C_VA // ATT_QKV), _tok_spec(1), _const_spec(inv), _const_spec(gq), _const_spec(gk)],
        out_specs=[_res_spec(g) for _ in range(3) for g in range(3)],
        out_shape=[_res_shape(g, BF16) for _ in range(3) for g in range(3)],
        scratch_shapes=[pltpu.VMEM((NCH, TT, 128), F32)],
        compiler_params=pltpu.CompilerParams(dimension_semantics=("arbitrary",)),
    )(proj, proj, proj, pos, inv, gq, gk)


def _qk_bwd(proj, pos, inv, gq, gk, dqs, dks, dvs, dproj):
    const = lambda a: pl.BlockSpec(a.shape, functools.partial(lambda i, p, nd: (0,) * nd, nd=a.ndim))
    res = lambda g: pl.BlockSpec((DILATIONS[g], TT // DILATIONS[g], ATT_W), lambda i, p: (0, i, 0))
    base = C_QA // ATT_QKV

    def body(t_ref, pos_ref, inv_ref, gq_ref, gk_ref, dq0, dq1, dq2, dk0, dk1, dk2, dv0, dv1, dv2, buf_ref,
             out_ref, dgq_ref, dgk_ref, sc):
        del buf_ref
        part = pl.program_id(1)
        first = pl.program_id(0) == 0

        def gather(drefs):
            for j in range(NCH):
                grp, sub = divmod(j * 128, ATT_W)
                _from_residues(drefs[grp], slice(sub, sub + 128), sc, j, DILATIONS[grp])

        def normed(g_ref, drefs, dg_ref):
            c, sp, sm = _rot_tables(pos_ref, inv_ref)
            gather(drefs)
            dg = jnp.zeros((1, 128), F32)
            for j in range(NCH):
                cols = slice(j * 128, (j + 1) * 128)
                d_rot = sc[j]
                dn = d_rot * c + pltpu.roll(d_rot * sp, 120, 1) + pltpu.roll(d_rot * sm, 8, 1)
                t = t_ref[:, cols]
                r = _pair_norm(t)
                gain = g_ref[...]
                dn_t = dn * t
                out_ref[:, cols] = (r * (dn * gain - t * ((r * r) * _pair_mean(dn_t * gain)))).astype(BF16)
                dg = dg + jnp.sum(dn_t * r, axis=0, keepdims=True)
            dg = dg + pltpu.roll(dg, HD, 1)

            @pl.when(first)
            def _():
                dg_ref[...] = dg

            @pl.when(jnp.logical_not(first))
            def _():
                dg_ref[...] += dg

        @pl.when(part == 0)
        def _():
            gather((dv0, dv1, dv2))
            for j in range(NCH):
                out_ref[:, j * 128:(j + 1) * 128] = sc[j].astype(BF16)

        @pl.when(part == 1)
        def _():
            normed(gq_ref, (dq0, dq1, dq2), dgq_ref)

        @pl.when(part == 2)
        def _():
            normed(gk_ref, (dk0, dk1, dk2), dgk_ref)

    keep = pl.BlockSpec((1, 128), lambda i, p: (0, 0))
    return pl.pallas_call(
        body, name="qk_bwd", grid=(T // TT, 3),
        in_specs=[pl.BlockSpec((TT, ATT_QKV), lambda i, p: (i, base + jnp.maximum(p - 1, 0))),
                  pl.BlockSpec((TT, 1), lambda i, p: (i, 0)), const(inv), const(gq), const(gk)]
        + [res(g) for _ in range(3) for g in range(3)] + [pl.BlockSpec(memory_space=pl.ANY)],
        out_specs=[pl.BlockSpec((TT, ATT_QKV), lambda i, p: (i, base + jnp.where(p == 0, 2, p - 1))), keep, keep],
        out_shape=[S(dproj.shape, dproj.dtype), S((1, 128), F32), S((1, 128), F32)],
        input_output_aliases={14: 0},
        scratch_shapes=[pltpu.VMEM((NCH, TT, 128), F32)],
        compiler_params=pltpu.CompilerParams(dimension_semantics=("arbitrary", "arbitrary")),
    )(proj, pos, inv, gq, gk, *dqs, *dks, *dvs, dproj)


def _split_heads(t):
    low = lax.broadcasted_iota(jnp.int32, (1, 128), 1) < HD
    zero = jnp.zeros_like(t)
    return jnp.concatenate([jnp.where(low, t, zero), jnp.where(low, zero, t)], axis=0)


def _join_heads(t2):
    low = lax.broadcasted_iota(jnp.int32, (1, 128), 1) < HD
    n = t2.shape[0] // 2
    return jnp.where(low, t2[:n], t2[n:])


def _band_mask4(has_before, has_own):
    row = lax.broadcasted_iota(jnp.int32, (BLK, 4 * BLK), 0)
    lane = lax.broadcasted_iota(jnp.int32, (BLK, 4 * BLK), 1)
    key = lane & (BLK - 1)
    own = lane >= 2 * BLK
    return (own & (key <= row) & has_own) | (jnp.logical_not(own) & (key >= row) & has_before)


def _band_mask_before(has_before):
    row = lax.broadcasted_iota(jnp.int32, (BLK, 2 * BLK), 0)
    key = lax.broadcasted_iota(jnp.int32, (BLK, 2 * BLK), 1) & (BLK - 1)
    return (key >= row) & has_before


def _per_head(width, col_a, col_b):
    lane = lax.broadcasted_iota(jnp.int32, (1, width), 1)
    return jnp.where((lane & BLK) == 0, col_a, col_b)


NQ = ATT_W // 128


def _att_fwd(q, k, v, grp, name):
    dil = DILATIONS[grp]
    nb = T // dil // BLK

    def body(q_ref, kp_ref, kc_ref, vp_ref, vc_ref, o_ref, lse_ref, s_sc, p_sc):
        mask = _band_mask4(pl.program_id(1) > 0, True)
        low = lax.broadcasted_iota(jnp.int32, (1, 128), 1) < HD
        halves = lambda ref, j, h: (ref[j, :, h * BLK:(h + 1) * BLK], ref[j, :, (h + 2) * BLK:(h + 3) * BLK])
        for j in range(NQ):
            cols = slice(j * 128, (j + 1) * 128)
            k4 = jnp.concatenate([_split_heads(kp_ref[:, cols]), _split_heads(kc_ref[:, cols])], axis=0)
            s_sc[j] = jnp.where(mask, _nt(q_ref[:, cols], k4), -jnp.inf)
        mxs = [[jnp.maximum(*(jnp.max(t, axis=-1, keepdims=True) for t in halves(s_sc, j, h))) for h in range(2)]
               for j in range(NQ)]
        dens = []
        for j in range(NQ):
            p = jnp.exp(s_sc[j] - _per_head(4 * BLK, *mxs[j]))
            p_sc[j] = p.astype(BF16)
            dens.append([jnp.sum(p[:, h * BLK:(h + 1) * BLK], axis=-1, keepdims=True)
                         + jnp.sum(p[:, (h + 2) * BLK:(h + 3) * BLK], axis=-1, keepdims=True) for h in range(2)])
        for j in range(NQ):
            cols = slice(j * 128, (j + 1) * 128)
            v4 = jnp.concatenate([_split_heads(vp_ref[:, cols]), _split_heads(vc_ref[:, cols])], axis=0)
            o_ref[:, cols] = _nn(p_sc[j], v4) / jnp.where(low, dens[j][0], dens[j][1])
            lse_ref[:, cols] = jnp.where(low, mxs[j][0] + jnp.log(dens[j][0]), mxs[j][1] + jnp.log(dens[j][1]))

    cur = pl.BlockSpec((None, BLK, ATT_W), lambda r, i: (r, i, 0))
    prev = pl.BlockSpec((None, BLK, ATT_W), lambda r, i: (r, jnp.maximum(i - 1, 0), 0))
    return pl.pallas_call(
        body, name=name, grid=(dil, nb),
        in_specs=[cur, prev, cur, prev, cur],
        out_specs=[cur, cur], out_shape=[_res_shape(grp, F32)] * 2,
        scratch_shapes=[pltpu.VMEM((NQ, BLK, 4 * BLK), F32), pltpu.VMEM((NQ, BLK, 4 * BLK), BF16)],
        compiler_params=pltpu.CompilerParams(dimension_semantics=("parallel", "arbitrary")),
    )(q, k, k, v, v)


def _att_bwd(q, k, v, datt, att, lse, grp, name):
    dil = DILATIONS[grp]
    nb = T // dil // BLK
    scale = HD ** -0.5

    def body(q0_ref, q1_ref, kp_ref, kc_ref, vp_ref, vc_ref, do0_ref, do1_ref, o0_ref, o1_ref, l0_ref, l1_ref,
             dq_ref, dk_ref, dv_ref, k4_sc, v4_sc, s0_sc, s1_sc, dp0_sc, dp1_sc, p_sc, ds_sc):
        i = pl.program_id(1)
        mask_mine = _band_mask4(i > 0, True)
        mask_next = _band_mask_before(i < nb - 1)
        low = lax.broadcasted_iota(jnp.int32, (1, 128), 1) < HD
        for j in range(NQ):
            cols = slice(j * 128, (j + 1) * 128)
            k4_sc[j, :2 * BLK] = _split_heads(kp_ref[:, cols])
            k4_sc[j, 2 * BLK:] = _split_heads(kc_ref[:, cols])
            v4_sc[j, :2 * BLK] = _split_heads(vp_ref[:, cols])
            v4_sc[j, 2 * BLK:] = _split_heads(vc_ref[:, cols])
        for j in range(NQ):
            cols = slice(j * 128, (j + 1) * 128)
            s0_sc[j] = _nt(q0_ref[:, cols], k4_sc[j])
            s1_sc[j] = _nt(q1_ref[:, cols], k4_sc[j, 2 * BLK:])
            dp0_sc[j] = _nt(do0_ref[:, cols].astype(BF16), v4_sc[j])
            dp1_sc[j] = _nt(do1_ref[:, cols].astype(BF16), v4_sc[j, 2 * BLK:])
        stats = []
        for j in range(NQ):
            cols = slice(j * 128, (j + 1) * 128)
            for do_ref, o_ref, l_ref in ((do0_ref, o0_ref, l0_ref), (do1_ref, o1_ref, l1_ref)):
                prod = do_ref[:, cols].astype(F32) * o_ref[:, cols].astype(F32)
                d_all = jnp.sum(prod, axis=-1, keepdims=True)
                d_low = jnp.sum(jnp.where(low, prod, 0.0), axis=-1, keepdims=True)
                lse_t = l_ref[:, cols]
                stats.append((d_low, d_all - d_low, lse_t[:, 0:1], lse_t[:, HD:HD + 1]))
        for j in range(NQ):
            (da, db, la, lb), (da1, db1, la1, lb1) = stats[2 * j], stats[2 * j + 1]
            p0 = jnp.where(mask_mine, jnp.exp(s0_sc[j] - _per_head(4 * BLK, la, lb)), 0.0)
            ds0 = p0 * (dp0_sc[j] - _per_head(4 * BLK, da, db))
            p1 = jnp.where(mask_next, jnp.exp(s1_sc[j] - _per_head(2 * BLK, la1, lb1)), 0.0)
            ds1 = p1 * (dp1_sc[j] - _per_head(2 * BLK, da1, db1))
            p_sc[j, :BLK] = p0.astype(BF16)
            ds_sc[j, :BLK] = ds0.astype(BF16)
            p_sc[j, BLK:, 2 * BLK:] = p1.astype(BF16)
            ds_sc[j, BLK:, 2 * BLK:] = ds1.astype(BF16)
        for j in range(NQ):
            cols = slice(j * 128, (j + 1) * 128)
            dq_ref[:, cols] = _nn(ds_sc[j, :BLK], k4_sc[j]) * scale
            qq = jnp.concatenate([q0_ref[:, cols], q1_ref[:, cols]], axis=0)
            dd = jnp.concatenate([do0_ref[:, cols], do1_ref[:, cols]], axis=0).astype(BF16)
            dk_ref[:, cols] = _join_heads(_tn(ds_sc[j, :, 2 * BLK:], qq))
            dv_ref[:, cols] = _join_heads(_tn(p_sc[j, :, 2 * BLK:], dd))

    def spec(shift):
        return pl.BlockSpec((None, BLK, ATT_W), lambda r, i: (r, jnp.clip(i + shift, 0, nb - 1), 0))

    here, after, before = spec(0), spec(1), spec(-1)
    vm = pltpu.VMEM
    return pl.pallas_call(
        body, name=name, grid=(dil, nb),
        in_specs=[here, after, before, here, before, here, here, after, here, after, here, after],
        out_specs=[here] * 3, out_shape=[_res_shape(grp, F32)] * 3,
        scratch_shapes=[vm((NQ, 4 * BLK, 128), BF16), vm((NQ, 4 * BLK, 128), BF16), vm((NQ, BLK, 4 * BLK), F32),
                        vm((NQ, BLK, 2 * BLK), F32), vm((NQ, BLK, 4 * BLK), F32), vm((NQ, BLK, 2 * BLK), F32),
                        vm((NQ, 2 * BLK, 4 * BLK), BF16), vm((NQ, 2 * BLK, 4 * BLK), BF16)],
        compiler_params=pltpu.CompilerParams(dimension_semantics=("parallel", "arbitrary")),
    )(q, q, k, k, v, v, datt, datt, att, att, lse, lse)


def _att_merge(os_, lses, proj):
    nq = ATT_W // 128

    def body(o0, o1, o2, l0, l1, l2, za_ref, att_ref, lse_ref, ain_ref, sc):
        for a, ref in enumerate((o0, o1, o2, l0, l1, l2)):
            for j in range(nq):
                _from_residues(ref, slice(j * 128, (j + 1) * 128), sc, a * nq + j, DILATIONS[a % 3])
        for j in range(nq):
            cols = slice(j * 128, (j + 1) * 128)
            oa, ob, oc = (sc[a * nq + j] for a in range(3))
            la, lb, lc = (sc[(3 + a) * nq + j] for a in range(3))
            m = jnp.maximum(jnp.maximum(la, lb), lc)
            wa, wb, wc = jnp.exp(la - m), jnp.exp(lb - m), jnp.exp(lc - m)
            tot = wa + wb + wc
            att = (wa * oa + wb * ob + wc * oc) / tot
            att_ref[:, cols] = att
            lse_ref[:, cols] = m + jnp.log(tot)
            za = za_ref[:, cols]
            ain_ref[:, cols] = (att * za * _sigmoid(za)).astype(BF16)

    return pl.pallas_call(
        body, name="att_merge", grid=(T // TT,),
        in_specs=[_res_spec(g) for _ in range(2) for g in range(3)] + [_tok_spec(ATT_W, C_ZA // ATT_W)],
        out_specs=[_tok_spec(ATT_W)] * 3,
        out_shape=[S((T, ATT_W), F32), S((T, ATT_W), F32), S((T, ATT_W), BF16)],
        scratch_shapes=[pltpu.VMEM((6 * nq, TT, 128), F32)],
        compiler_params=pltpu.CompilerParams(dimension_semantics=("arbitrary",)),
    )(*os_, *lses, proj)


def _att_gate_bwd(dain, att, lse, proj, dproj):
    nq = ATT_W // 128

    def body(d_ref, att_ref, lse_ref, za_ref, buf_ref, dza_ref, da0, da1, da2, at1, at2, ls1, ls2, sc):
        del buf_ref
        for j in range(nq):
            cols = slice(j * 128, (j + 1) * 128)
            za = za_ref[:, cols]
            sg = _sigmoid(za)
            d = d_ref[:, cols].astype(F32)
            att_ = att_ref[:, cols]
            dza_ref[:, cols] = (d * att_ * sg * (1.0 + za * (1.0 - sg))).astype(BF16)
            sc[j] = d * za * sg
            sc[nq + j] = att_
            sc[2 * nq + j] = lse_ref[:, cols]
        for j in range(nq):
            cols = slice(j * 128, (j + 1) * 128)
            for grp, dst in enumerate((da0, da1, da2)):
                _to_residues(sc, j, dst, DILATIONS[grp], cols)
            for grp, dst in ((1, at1), (2, at2)):
                _to_residues(sc, nq + j, dst, DILATIONS[grp], cols)
            for grp, dst in ((1, ls1), (2, ls2)):
                _to_residues(sc, 2 * nq + j, dst, DILATIONS[grp], cols)

    res = (0, 1, 2, 1, 2, 1, 2)
    return pl.pallas_call(
        body, name="att_gate_bwd", grid=(T // TT,),
        in_specs=[_tok_spec(ATT_W)] * 3 + [_tok_spec(ATT_W, C_ZA // ATT_W), pl.BlockSpec(memory_space=pl.ANY)],
        out_specs=[_tok_spec(ATT_W, C_ZA // ATT_W)] + [_res_spec(g) for g in res],
        out_shape=[S(dproj.shape, dproj.dtype)] + [_res_shape(g, BF16) for g in res[:5]]
        + [_res_shape(g, F32) for g in res[5:]],
        input_output_aliases={4: 0},
        scratch_shapes=[pltpu.VMEM((3 * nq, TT, 128), F32)],
        compiler_params=pltpu.CompilerParams(dimension_semantics=("arbitrary",)),
    )(dain, att, lse, proj, dproj)


def _split3(v):
    hi = v.astype(BF16)
    r1 = v - hi.astype(F32)
    mid = r1.astype(BF16)
    lo = (r1 - mid.astype(F32)).astype(BF16)
    return hi, mid, lo


def _chunk_scores(qt, kt, q_ref, k_ref, h):
    cols = slice(h * GDK, (h + 1) * GDK)
    own = jnp.sum(q_ref[:, cols] * (GDK ** -0.5) * k_ref[:, cols], axis=-1, keepdims=True)
    row = lax.broadcasted_iota(jnp.int32, (GLA_C, GLA_C), 0)
    col = lax.broadcasted_iota(jnp.int32, (GLA_C, GLA_C), 1)
    a = _nt(qt.astype(BF16), kt.astype(BF16))
    return jnp.where(col < row, a, jnp.where(col == row, own, 0.0))


def _tri_sum(v, upper):
    n = v.shape[0]
    row = lax.broadcasted_iota(jnp.int32, (n, n), 0)
    col = lax.broadcasted_iota(jnp.int32, (n, n), 1)
    tri = jnp.where(col >= row if upper else col <= row, 1.0, 0.0).astype(BF16)
    hi, mid, lo = _split3(v)
    return _nn(tri, hi) + _nn(tri, mid) + _nn(tri, lo)


def _gla_gates(glr_ref, w2_ref, b_ref):
    logit = _nn(glr_ref[...].astype(BF16), w2_ref[...]) + b_ref[...]
    lg = (jnp.minimum(logit, 0.0) - jnp.log(1.0 + jnp.exp(-jnp.abs(logit)))) * (1.0 / GLA_TAU)
    return logit, _tri_sum(lg, upper=False)


def _gla_head(cum, q_ref, k_ref, h):
    cols = slice(h * GDK, (h + 1) * GDK)
    b = cum[:, cols]
    last = b[GLA_C - 1:GLA_C, :]
    e_pos = jnp.exp(b)
    e_neg = jnp.exp(-b)
    e_end = jnp.exp(last - b)
    qt = q_ref[:, cols] * (GDK ** -0.5) * e_pos
    kt = k_ref[:, cols] * e_neg
    kh = k_ref[:, cols] * e_end
    return b, last, e_pos, e_neg, e_end, qt, kt, kh


def _causal(n):
    return lax.broadcasted_iota(jnp.int32, (n, n), 1) <= lax.broadcasted_iota(jnp.int32, (n, n), 0)


def _gla_fwd(proj, w2p, bg, gn):
    nc = T // GLA_C

    def body(q_ref, k_ref, v_ref, glr_ref, zg_ref, w2_ref, b_ref, gn_ref, o_ref, bin_ref, st_ref, state):
        @pl.when(pl.program_id(0) == 0)
        def _():
            state[...] = jnp.zeros_like(state)

        _, cum = _gla_gates(glr_ref, w2_ref, b_ref)
        for h in range(GH):
            _, last, _, _, _, qt, kt, kh = _gla_head(cum, q_ref, k_ref, h)
            vcols = slice(h * GDV, (h + 1) * GDV)
            st = state[h]
            st_ref[0, h] = st
            v = v_ref[:, vcols].astype(BF16)
            qb = qt.astype(BF16)
            a = _chunk_scores(qt, kt, q_ref, k_ref, h)
            o = _nt(qb, st.astype(BF16)) + _nn(a.astype(BF16), v)
            state[h] = st * jnp.exp(last) + _tn(v, kh.astype(BF16))
            o_ref[:, vcols] = o
            r = lax.rsqrt(jnp.mean(o * o, axis=-1, keepdims=True) + EPS)
            zg = zg_ref[:, vcols]
            bin_ref[:, vcols] = (o * r * gn_ref[...] * zg * _sigmoid(zg)).astype(BF16)

    row = lambda width, cblk: pl.BlockSpec((GLA_C, width), functools.partial(lambda i, c: (i, c), c=cblk))
    full = lambda a: pl.BlockSpec(a.shape, functools.partial(lambda i, nd: (0,) * nd, nd=a.ndim))
    return pl.pallas_call(
        body, name="gla_fwd", grid=(nc,),
        in_specs=[row(512, C_QG // 512), row(512, C_KG // 512), row(1024, C_VG // 1024), row(GLR_W, C_GLR // GLR_W),
                  row(1024, C_ZG // 1024), full(w2p), full(bg), full(gn)],
        out_specs=[pl.BlockSpec((GLA_C, GH * GDV), lambda i: (i, 0)), pl.BlockSpec((GLA_C, GH * GDV), lambda i: (i, 0)),
                   pl.BlockSpec((1, GH, GDV, GDK), lambda i: (i, 0, 0, 0))],
        out_shape=[S((T, GH * GDV), F32), S((T, GH * GDV), BF16), S((nc, GH, GDV, GDK), F32)],
        scratch_shapes=[pltpu.VMEM((GH, GDV, GDK), F32)],
        compiler_params=pltpu.CompilerParams(dimension_semantics=("arbitrary",)),
    )(proj, proj, proj, proj, proj, w2p, bg, gn)


def _gla_bwd(proj, w2p, bg, gn, o_gla, states, dbin, dproj):
    nc = T // GLA_C

    def body(q_ref, k_ref, v_ref, glr_ref, zg_ref, w2_ref, b_ref, gn_ref, o_ref, st_ref, dbin_ref, buf_ref,
             out_ref, dw2_ref, dbg_ref, dgn_ref, dstate, dlogit):
        del buf_ref
        dq_ref = out_ref.at[:, C_QG:C_KG]
        dk_ref = out_ref.at[:, C_KG:C_VG]
        dv_ref = out_ref.at[:, C_VG:C_ZG]
        dzg_ref = out_ref.at[:, C_ZG:C_GLR]
        dglr_ref = out_ref.at[:, C_GLR:C_GLR + GLR_W]
        first = pl.program_id(0) == 0

        @pl.when(first)
        def _():
            dstate[...] = jnp.zeros_like(dstate)

        logit, cum = _gla_gates(glr_ref, w2_ref, b_ref)
        is_last = lax.broadcasted_iota(jnp.int32, (GLA_C, 1), 0) == GLA_C - 1
        dgn = jnp.zeros((1, GDV), F32)
        for h in range(GH):
            _, last, e_pos, e_neg, e_end, qt, kt, kh = _gla_head(cum, q_ref, k_ref, h)
            cols = slice(h * GDK, (h + 1) * GDK)
            vcols = slice(h * GDV, (h + 1) * GDV)
            o = o_ref[:, vcols]
            r = lax.rsqrt(jnp.mean(o * o, axis=-1, keepdims=True) + EPS)
            zg = zg_ref[:, vcols]
            sg = _sigmoid(zg)
            db_ = dbin_ref[:, vcols].astype(F32)
            dlin = db_ * zg * sg
            dzg_ref[:, vcols] = (db_ * (o * r * gn_ref[...]) * sg * (1.0 + zg * (1.0 - sg))).astype(BF16)
            u = dlin * gn_ref[...]
            do = (r * u - o * (r * r * r) * jnp.mean(u * o, axis=-1, keepdims=True)).astype(BF16)
            dgn = dgn + jnp.sum(dlin * o * r, axis=0, keepdims=True)
            st = st_ref[0, h]
            dst = dstate[h]
            v = v_ref[:, vcols].astype(BF16)
            qb, kb, khb = qt.astype(BF16), kt.astype(BF16), kh.astype(BF16)
            dstb = dst.astype(BF16)
            causal = _causal(GLA_C)
            a = _chunk_scores(qt, kt, q_ref, k_ref, h).astype(BF16)
            da = jnp.where(causal, _nt(do, v), 0.0).astype(BF16)
            dqt = _nn(do, st.astype(BF16)) + _nn(da, kb)
            dkt = _tn(da, qb)
            dkh = _nn(v, dstb)
            dv_ref[:, vcols] = (_tn(a, do) + _nt(khb, dstb)).astype(BF16)
            lam = jnp.exp(last)
            dlam = jnp.sum(dst * st, axis=0, keepdims=True)
            dstate[h] = dst * lam + _tn(do, qb)
            dq_ref[:, cols] = (dqt * e_pos * (GDK ** -0.5)).astype(BF16)
            dk_ref[:, cols] = (dkt * e_neg + dkh * e_end).astype(BF16)
            dkh_kh = dkh * kh
            dcum = dqt * qt - dkt * kt - dkh_kh
            dlast = jnp.sum(dkh_kh, axis=0, keepdims=True) + dlam * lam
            dcum = jnp.where(is_last, dcum + dlast, dcum)
            dlg = _tri_sum(dcum, upper=True)
            dlogit[:, cols] = dlg * (1.0 / GLA_TAU) * (1.0 - _sigmoid(logit[:, cols]))

        dl = dlogit[...]
        dlb = dl.astype(BF16)
        dglr_ref[...] = _nt(dlb, w2_ref[...]).astype(BF16)
        dw2 = _tn(glr_ref[...].astype(BF16), dlb)
        dbg = jnp.sum(dl, axis=0, keepdims=True)

        @pl.when(first)
        def _():
            dw2_ref[...] = dw2
            dbg_ref[...] = dbg
            dgn_ref[...] = dgn

        @pl.when(jnp.logical_not(first))
        def _():
            dw2_ref[...] += dw2
            dbg_ref[...] += dbg
            dgn_ref[...] += dgn

    rev = lambda i: nc - 1 - i
    row = lambda width, cblk: pl.BlockSpec((GLA_C, width), functools.partial(lambda i, c: (rev(i), c), c=cblk))
    full = lambda a: pl.BlockSpec(a.shape, functools.partial(lambda i, nd: (0,) * nd, nd=a.ndim))
    keep = lambda shape: pl.BlockSpec(shape, functools.partial(lambda i, nd: (0,) * nd, nd=len(shape)))
    return pl.pallas_call(
        body, name="gla_bwd", grid=(nc,),
        in_specs=[row(512, C_QG // 512), row(512, C_KG // 512), row(1024, C_VG // 1024), row(GLR_W, C_GLR // GLR_W),
                  row(1024, C_ZG // 1024), full(w2p), full(bg), full(gn), row(GH * GDV, 0),
                  pl.BlockSpec((1, GH, GDV, GDK), lambda i: (rev(i), 0, 0, 0)), row(GH * GDV, 0),
                  pl.BlockSpec(memory_space=pl.ANY)],
        out_specs=[row(GLA_GROUP_W, 0), keep((GLR_W, 512)), keep((1, 512)), keep((1, GDV))],
        out_shape=[S(dproj.shape, dproj.dtype), S((GLR_W, 512), F32), S((1, 512), F32), S((1, GDV), F32)],
        input_output_aliases={11: 0},
        scratch_shapes=[pltpu.VMEM((GH, GDV, GDK), F32), pltpu.VMEM((GLA_C, GH * GDK), F32)],
        compiler_params=pltpu.CompilerParams(dimension_semantics=("arbitrary",)),
    )(proj, proj, proj, proj, proj, w2p, bg, gn, o_gla, states, dbin, dproj)


RT = 512


def _rowchain(body, name, ins, outs, scratch=()):
    in_specs, args = [], []
    for spec in ins:
        if spec[0] == "tok":
            _, arr, width, cblk = spec
            in_specs.append(pl.BlockSpec((RT, width), functools.partial(lambda i, c: (i, c), c=cblk)))
        else:
            arr = spec[1]
            in_specs.append(pl.BlockSpec(arr.shape, functools.partial(lambda i, nd: (0,) * nd, nd=arr.ndim)))
        args.append(arr)
    out_specs, out_shape = [], []
    for spec in outs:
        if spec[0] == "tok":
            _, shape, dtype, width, cblk = spec
            out_specs.append(pl.BlockSpec((RT, width), functools.partial(lambda i, c: (i, c), c=cblk)))
        else:
            _, shape, dtype = spec
            out_specs.append(pl.BlockSpec(shape, functools.partial(lambda i, nd: (0,) * nd, nd=len(shape))))
        out_shape.append(S(shape, dtype))
    return pl.pallas_call(
        body, name=name, grid=(T // RT,), in_specs=in_specs, out_specs=out_specs, out_shape=out_shape,
        scratch_shapes=list(scratch), compiler_params=pltpu.CompilerParams(dimension_semantics=("arbitrary",)),
    )(*args)


def _tok(arr, width=None, cblk=0):
    return ("tok", arr, arr.shape[1] if width is None else width, cblk)


def _tok_out(dtype, width=D):
    return ("tok", (T, width), dtype, width, 0)


def _branches_fwd(ain, bin_, proj, x, w_att, w_gla, w_out):
    def body(ain_ref, bin_ref, g_ref, x_ref, wa_ref, wg_ref, wo_ref, ya_ref, yb_ref, y_ref, x1_ref):
        ya = _nn(ain_ref[...], wa_ref[...]).astype(BF16)
        yb = _nn(bin_ref[...], wg_ref[...]).astype(BF16)
        ya_ref[...] = ya
        yb_ref[...] = yb
        y = (_sigmoid(g_ref[:, :D]) * ya.astype(F32) + _sigmoid(g_ref[:, D:]) * yb.astype(F32)).astype(BF16)
        y_ref[...] = y
        x1_ref[...] = x_ref[...] + _nn(y, wo_ref[...])

    return _rowchain(body, "branches_fwd",
                     [_tok(ain), _tok(bin_), _tok(proj, 2 * D, C_GA // (2 * D)), _tok(x), ("all", w_att),
                      ("all", w_gla), ("all", w_out)],
                     [_tok_out(BF16), _tok_out(BF16), _tok_out(BF16), _tok_out(F32)])


def _accumulate(ref, part, first):
    @pl.when(first)
    def _():
        ref[...] = part

    @pl.when(jnp.logical_not(first))
    def _():
        ref[...] += part


def _ple_loss(x1, p, target, g2, w_pg, w_ple):
    def body(x1_ref, p_ref, t_ref, g_ref, wpg_ref, wple_ref, n2_ref, loss_ref, dout_ref, du_ref, dwple_ref, acc):
        first = pl.program_id(0) == 0
        x1 = x1_ref[...]
        r = lax.rsqrt(jnp.mean(x1 * x1, axis=-1, keepdims=True) + EPS)
        n2 = (x1 * r * g_ref[...]).astype(BF16)
        n2_ref[...] = n2
        pg = _sigmoid(_nn(n2, wpg_ref[...]))
        pb = p_ref[...].astype(BF16)
        e_ = _nn(pb, wple_ref[...])
        diff = x1 + e_ * pg - t_ref[...]
        _accumulate(acc, jnp.sum(diff * diff, axis=0, keepdims=True), first)
        dout = diff * (1.0 / D)
        dout_ref[...] = dout
        du_ref[...] = (dout * e_ * pg * (1.0 - pg)).astype(BF16)
        _accumulate(dwple_ref, _tn(pb, (dout * pg).astype(BF16)), first)
        loss_ref[...] = jnp.zeros((1, 128), F32) + jnp.sum(acc[...], axis=-1, keepdims=True) * (0.5 / D)

    return _rowchain(body, "ple_loss", [_tok(x1), _tok(p), _tok(target), ("all", g2), ("all", w_pg), ("all", w_ple)],
                     [_tok_out(BF16), ("acc", (1, 128), F32), _tok_out(F32), _tok_out(BF16), ("acc", (PLE, D), F32)],
                     scratch=[pltpu.VMEM((1, D), F32)])


def _ple_bwd(du, n2, y, x1, dout, g2, w_pg, w_out):
    def body(du_ref, n2_ref, y_ref, x1_ref, dout_ref, g_ref, wpg_ref, wo_ref, dx_ref, dy_ref, dg_ref, dwpg_ref,
             dwo_ref):
        first = pl.program_id(0) == 0
        x1 = x1_ref[...]
        r = lax.rsqrt(jnp.mean(x1 * x1, axis=-1, keepdims=True) + EPS)
        du_ = du_ref[...]
        dn = _nt(du_, wpg_ref[...])
        u = dn * g_ref[...]
        dx = dout_ref[...] + r * u - x1 * (r * r * r) * jnp.mean(u * x1, axis=-1, keepdims=True)
        dxb = dx.astype(BF16)
        dx_ref[...] = dx
        dy_ref[...] = _nt(dxb, wo_ref[...]).astype(BF16)
        _accumulate(dg_ref, jnp.sum(dn * x1 * r, axis=0, keepdims=True), first)
        _accumulate(dwpg_ref, _tn(n2_ref[...], du_), first)
        _accumulate(dwo_ref, _tn(y_ref[...], dxb), first)

    return _rowchain(body, "ple_bwd",
                     [_tok(du), _tok(n2), _tok(y), _tok(x1), _tok(dout), ("all", g2), ("all", w_pg), ("all", w_out)],
                     [_tok_out(F32), _tok_out(BF16), ("acc", (1, D), F32), ("acc", (D, D), F32), ("acc", (D, D), F32)])


def _branches_bwd(dy, ya, yb, ain, bin_, proj, w_att, w_gla):
    def body(dy_ref, ya_ref, yb_ref, ain_ref, bin_ref, g_ref, wa_ref, wg_ref, dg_ref, dain_ref, dbin_ref,
             dwa_ref, dwg_ref):
        first = pl.program_id(0) == 0
        dy_ = dy_ref[...].astype(F32)
        sa, sb = _sigmoid(g_ref[:, :D]), _sigmoid(g_ref[:, D:])
        dg_ref[:, :D] = (dy_ * ya_ref[...].astype(F32) * sa * (1.0 - sa)).astype(BF16)
        dg_ref[:, D:] = (dy_ * yb_ref[...].astype(F32) * sb * (1.0 - sb)).astype(BF16)
        dya = (dy_ * sa).astype(BF16)
        dyb = (dy_ * sb).astype(BF16)
        dain_ref[...] = _nt(dya, wa_ref[...]).astype(BF16)
        dbin_ref[...] = _nt(dyb, wg_ref[...]).astype(BF16)
        _accumulate(dwa_ref, _tn(ain_ref[...], dya), first)
        _accumulate(dwg_ref, _tn(bin_ref[...], dyb), first)

    gates = C_GA // (2 * D)
    return _rowchain(body, "branches_bwd",
                     [_tok(dy), _tok(ya), _tok(yb), _tok(ain), _tok(bin_), _tok(proj, 2 * D, gates), ("all", w_att),
                      ("all", w_gla)],
                     [("tok", (T, NCOL), BF16, 2 * D, gates), _tok_out(BF16, ATT_W), _tok_out(BF16),
                      ("acc", (ATT_W, D), F32), ("acc", (D, D), F32)])


def _peer(k):
    x, y, c = lax.axis_index("x"), lax.axis_index("y"), lax.axis_index("c")
    return (x ^ ((k >> 2) & 1), y ^ ((k >> 1) & 1), c ^ (k & 1))


def _my_index():
    return 4 * lax.axis_index("x") + 2 * lax.axis_index("y") + lax.axis_index("c")


def _peer_index(k):
    px, py, pc = _peer(k)
    return 4 * px + 2 * py + pc


def _pairwise_plan(src_of, dst_of, landed_of, own_src, own_dst):
    def plan(ins, outs, send, recv, local):
        n = len(ins)

        def own():
            return [pltpu.make_async_copy(own_src(ins[a]), own_dst(outs[a]), local.at[a]) for a in range(n)]

        def remote(k, a, src, dst):
            return pltpu.make_async_remote_copy(src_ref=src, dst_ref=dst, send_sem=send.at[k - 1, a],
                                                recv_sem=recv.at[k - 1, a], device_id=_peer(k), device_id_type=MESH)

        def sent():
            return [remote(k, a, src_of(ins[a], k), dst_of(outs[a])) for k in range(1, NDEV) for a in range(n)]

        def start():
            for cp in own() + sent():
                cp.start()

        def finish():
            for k in range(1, NDEV):
                for a in range(n):
                    remote(k, a, own_src(ins[a]), landed_of(outs[a], k)).wait_recv()
            for cp in sent():
                cp.wait_send()
            for cp in own():
                cp.wait()

        return start, finish

    return plan


def _pairwise_sems(n):
    return [pltpu.SemaphoreType.DMA((NDEV - 1, n)), pltpu.SemaphoreType.DMA((NDEV - 1, n)),
            pltpu.SemaphoreType.DMA((n,))]


def _gather_side(arrs):
    plan = _pairwise_plan(src_of=lambda i, k: i, dst_of=lambda o: o.at[_my_index()],
                          landed_of=lambda o, k: o.at[_peer_index(k)],
                          own_src=lambda i: i, own_dst=lambda o: o.at[_my_index()])
    return dict(arrs=arrs, out_shape=[S((NDEV,) + a.shape, a.dtype) for a in arrs],
                scratch=_pairwise_sems(len(arrs)), plan=plan)


def _exchange_side(arrs):
    plan = _pairwise_plan(src_of=lambda i, k: i.at[_peer_index(k)], dst_of=lambda o: o.at[_my_index()],
                          landed_of=lambda o, k: o.at[_peer_index(k)],
                          own_src=lambda i: i.at[_my_index()], own_dst=lambda o: o.at[_my_index()])
    return dict(arrs=arrs, out_shape=[S(a.shape, a.dtype) for a in arrs], scratch=_pairwise_sems(len(arrs)), plan=plan)


def _comm_call(side, name):
    n = len(side["arrs"])

    def body(*refs):
        start, finish = side["plan"](refs[:n], refs[n:2 * n], *refs[2 * n:])
        start()
        finish()

    hbm = pl.BlockSpec(memory_space=pl.ANY)
    return pl.pallas_call(body, name=name, in_specs=[hbm] * n, out_specs=[hbm] * n, out_shape=side["out_shape"],
                          scratch_shapes=side["scratch"])(*side["arrs"])


def _all_gather_by_chip(arrs, name):
    n = len(arrs)

    def body(*refs):
        ins, outs = refs[:n], refs[n:2 * n]
        send, recv, local = refs[2 * n:]
        x, y, c = lax.axis_index("x"), lax.axis_index("y"), lax.axis_index("c")
        me, sibling = (x, y, c), (x, y, 1 - c)
        chips = [(1 - x, y), (x, 1 - y), (1 - x, 1 - y)]

        def copy(k, a, block, to, src=None):
            px, py, pc = block
            slot = outs[a].at[4 * px + 2 * py + pc]
            return pltpu.make_async_remote_copy(
                src_ref=slot if src is None else src, dst_ref=slot, send_sem=send.at[k, a], recv_sem=recv.at[k, a],
                device_id=to, device_id_type=MESH)

        north = c == 1
        via = (jnp.where(north, 1 - x, x), jnp.where(north, y, 1 - y))
        onward = (jnp.where(north, x, 1 - x), jnp.where(north, 1 - y, y), c)
        mine = [pltpu.make_async_copy(ins[a], outs[a].at[4 * x + 2 * y + c], local.at[a]) for a in range(n)]
        first = []
        for a in range(n):
            first.append(copy(0, a, me, sibling, src=ins[a]))
            first += [copy(1 + j, a, me, (*chips[j], c), src=ins[a]) for j in range(2)]
        for cp in mine + first:
            cp.start()
        passed = []
        for j in range(2):
            for a in range(n):
                copy(1 + j, a, (*chips[j], c), me).wait_recv()
                passed.append(copy(4 + j, a, (*chips[j], c), sibling))
                passed[-1].start()
        for a in range(n):
            passed.append(copy(3, a, (*via, c), onward))
            passed[-1].start()
        for a in range(n):
            copy(3, a, (*chips[2], c), me).wait_recv()
            passed.append(copy(6, a, (*chips[2], c), sibling))
            passed[-1].start()
        for a in range(n):
            copy(0, a, sibling, me).wait_recv()
        for j, chip in enumerate(chips):
            for a in range(n):
                copy(4 + j, a, (*chip, 1 - c), me).wait_recv()
        for cp in first + passed:
            cp.wait_send()
        for cp in mine:
            cp.wait()

    hbm = pl.BlockSpec(memory_space=pl.ANY)
    return pl.pallas_call(
        body, name=name, in_specs=[hbm] * n, out_specs=[hbm] * n,
        out_shape=[S((NDEV,) + a.shape, a.dtype) for a in arrs],
        scratch_shapes=[pltpu.SemaphoreType.DMA((NDEV - 1, n)), pltpu.SemaphoreType.DMA((NDEV - 1, n)),
                        pltpu.SemaphoreType.DMA((n,))],
    )(*arrs)


NCHIP = 4


def _sibling_sum(src, name, tc=256):
    _, rows, cols = src.shape
    assert cols % tc == 0

    def body(src_ref, got_ref, out_ref, a_buf, b_buf, o_buf, send, recv, local):
        x, y, c = lax.axis_index("x"), lax.axis_index("y"), lax.axis_index("c")
        copies = [pltpu.make_async_remote_copy(
            src_ref=src_ref.at[2 * q + (1 - c)], dst_ref=got_ref.at[q], send_sem=send.at[q], recv_sem=recv.at[q],
            device_id=(x, y, 1 - c), device_id_type=MESH) for q in range(NCHIP)]
        for cp in copies:
            cp.start()
        tiles = [(q, pl.ds(t * tc, tc)) for q in range(NCHIP) for t in range(cols // tc)]

        def loads(n):
            q, tile = tiles[n]
            return [pltpu.make_async_copy(src_ref.at[2 * q + c, :, tile], a_buf.at[n % 2], local.at[n % 2, 0]),
                    pltpu.make_async_copy(got_ref.at[q, :, tile], b_buf.at[n % 2], local.at[n % 2, 1])]

        def store(n):
            q, tile = tiles[n]
            return pltpu.make_async_copy(o_buf.at[n % 2], out_ref.at[q, :, tile], local.at[n % 2, 2])

        def fetch(n):
            if n == 0 or tiles[n][0] != tiles[n - 1][0]:
                copies[tiles[n][0]].wait_recv()
            for cp in loads(n):
                cp.start()

        fetch(0)
        for n in range(len(tiles)):
            if n + 1 < len(tiles):
                fetch(n + 1)
            for cp in loads(n):
                cp.wait()
            if n >= 2:
                store(n - 2).wait()
            o_buf[n % 2] = (a_buf[n % 2].astype(F32) + b_buf[n % 2].astype(F32)).astype(BF16)
            store(n).start()
        store(len(tiles) - 2).wait()
        store(len(tiles) - 1).wait()
        for cp in copies:
            cp.wait_send()

    hbm = pl.BlockSpec(memory_space=pl.ANY)
    block = S((NCHIP, rows, cols), BF16)
    return pl.pallas_call(
        body, name=name, in_specs=[hbm], out_specs=[hbm, hbm], out_shape=[block, block],
        scratch_shapes=[pltpu.VMEM((2, rows, tc), BF16)] * 3
        + [pltpu.SemaphoreType.DMA((NCHIP,)), pltpu.SemaphoreType.DMA((NCHIP,)), pltpu.SemaphoreType.DMA((2, 3))],
    )(src)[1]


def _chips_side(arrs):
    def plan(ins, outs, send, recv, local):
        n = len(ins)

        def places():
            x, y, c = lax.axis_index("x"), lax.axis_index("y"), lax.axis_index("c")
            return 2 * x + y, c, [(1 - x, y), (x, 1 - y), (1 - x, 1 - y)]

        def own():
            here, _, _ = places()
            return [pltpu.make_async_copy(ins[a].at[here], outs[a].at[here], local.at[a]) for a in range(n)]

        def remote(j, a, src_slot, dst_slot):
            _, c, chips = places()
            cx, cy = chips[j]
            return pltpu.make_async_remote_copy(
                src_ref=ins[a].at[src_slot], dst_ref=outs[a].at[dst_slot], send_sem=send.at[j, a],
                recv_sem=recv.at[j, a], device_id=(cx, cy, c), device_id_type=MESH)

        def sent():
            here, _, chips = places()
            return [remote(j, a, 2 * cx + cy, here) for j, (cx, cy) in enumerate(chips) for a in range(n)]

        def start():
            for cp in own() + sent():
                cp.start()

        def finish():
            here, _, chips = places()
            for j, (cx, cy) in enumerate(chips):
                for a in range(n):
                    remote(j, a, here, 2 * cx + cy).wait_recv()
            for cp in sent():
                cp.wait_send()
            for cp in own():
                cp.wait()

        return start, finish

    n = len(arrs)
    return dict(arrs=arrs, out_shape=[S(a.shape, a.dtype) for a in arrs],
                scratch=[pltpu.SemaphoreType.DMA((NCHIP - 1, n)), pltpu.SemaphoreType.DMA((NCHIP - 1, n)),
                         pltpu.SemaphoreType.DMA((n,))], plan=plan)


def _adamw_shards(parts, places):
    n_src = len(parts)

    def body(*refs):
        srcs, rest = refs[:n_src], refs[n_src:]
        for j, (src, rows, cols, _) in enumerate(places):
            w_ref, m_ref, v_ref = rest[3 * j:3 * j + 3]
            outs = rest[3 * len(places) + 4 * j:3 * len(places) + 4 * j + 4]
            p_ref = srcs[src]
            g = p_ref[0, rows, cols].astype(F32)
            for s in range(1, p_ref.shape[0]):
                g = g + p_ref[s, rows, cols].astype(F32)
            delta, m_new, v_new = _adam_math(g, w_ref[0], m_ref[0], v_ref[0])
            for ref, val in zip(outs, (g, delta, m_new, v_new)):
                ref[0] = val

    flat = [a for place in places for a in place[3]]
    return pl.pallas_call(
        body, name="adam_shards",
        out_shape=[S(place[3][0].shape, F32) for place in places for _ in range(4)],
    )(*parts, *flat)


def _adam_math(g, w, m, v):
    c1 = 1.0 - ADAM_B1 ** ADAM_STEP
    c2 = 1.0 - ADAM_B2 ** ADAM_STEP
    m_new = ADAM_B1 * m + (1.0 - ADAM_B1) * g
    v_new = ADAM_B2 * v + (1.0 - ADAM_B2) * (g * g)
    return -ADAM_LR * ((m_new / c1) / (jnp.sqrt(v_new / c2) + ADAM_EPS) + ADAM_WD * w), m_new, v_new


def _adamw_small(parts, params, loss_parts):
    n = len(params)

    def body(*refs):
        p_refs, rest = refs[:n], refs[n + 1:]
        total = refs[n][0]
        for s in range(1, NDEV):
            total = total + refs[n][s]
        refs[-1][...] = total
        for j in range(n):
            w_ref, m_ref, v_ref = rest[3 * j:3 * j + 3]
            g_ref, d_ref, mo_ref, vo_ref = rest[3 * n + 4 * j:3 * n + 4 * j + 4]
            width = w_ref.shape[1]
            g = p_refs[j][0]
            for s in range(1, NDEV):
                g = g + p_refs[j][s]
            g = g[:, :width]
            delta, m_new, v_new = _adam_math(g, w_ref[...], m_ref[...], v_ref[...])
            g_ref[...] = g
            d_ref[...] = delta
            mo_ref[...] = m_new
            vo_ref[...] = v_new

    flat = [a for group in params for a in group]
    return pl.pallas_call(
        body, name="adam_small",
        out_shape=[S(group[0].shape, F32) for group in params for _ in range(4)] + [S((1, 128), F32)],
    )(*parts, loss_parts, *flat)


def _adamw_rows(parts, w, m, v, name, tc=128):
    rows, _, cols = w.shape
    nparts = parts.shape[0]
    nsteps = cols // tc

    def body(p_ref, w_hbm, m_hbm, v_hbm, g_hbm, d_hbm, mo_hbm, vo_hbm, inbuf, outbuf, insem, outsem):
        i = pl.program_id(0)
        slot = i & 1

        def view(ref, step):
            return ref.at[:, 0, pl.ds(pl.multiple_of(step * tc, tc), tc)]

        def fetch(step, sl):
            return [pltpu.make_async_copy(view(src, step), inbuf.at[sl, k], insem.at[sl, k])
                    for k, src in enumerate((w_hbm, m_hbm, v_hbm))]

        def write(step, sl):
            return [pltpu.make_async_copy(outbuf.at[sl, k], view(dst, step), outsem.at[sl, k])
                    for k, dst in enumerate((g_hbm, d_hbm, mo_hbm, vo_hbm))]

        @pl.when(i == 0)
        def _():
            for cp in fetch(0, 0):
                cp.start()

        @pl.when(i + 1 < nsteps)
        def _():
            for cp in fetch(i + 1, 1 - slot):
                cp.start()

        for cp in fetch(i, slot):
            cp.wait()

        @pl.when(i >= 2)
        def _():
            for cp in write(i - 2, slot):
                cp.wait()

        g = p_ref[0].astype(F32)
        for s in range(1, nparts):
            g = g + p_ref[s].astype(F32)
        g = g[:rows]
        delta, m_new, v_new = _adam_math(g, inbuf[slot, 0], inbuf[slot, 1], inbuf[slot, 2])
        for k, val in enumerate((g, delta, m_new, v_new)):
            outbuf[slot, k] = val
        for cp in write(i, slot):
            cp.start()

        @pl.when(i == nsteps - 1)
        def _():
            for cp in write(i - 1, 1 - slot) + write(i, slot):
                cp.wait()

    hbm = pl.BlockSpec(memory_space=pl.ANY)
    assert nsteps >= 2
    return pl.pallas_call(
        body, name=name, grid=(nsteps,),
        in_specs=[pl.BlockSpec((nparts, parts.shape[1], tc), lambda i: (0, 0, i)), hbm, hbm, hbm],
        out_specs=[hbm] * 4, out_shape=[S((rows, 1, cols), F32)] * 4,
        scratch_shapes=[pltpu.VMEM((2, 3, rows, tc), F32), pltpu.VMEM((2, 4, rows, tc), F32),
                        pltpu.SemaphoreType.DMA((2, 3)), pltpu.SemaphoreType.DMA((2, 4))],
        compiler_params=pltpu.CompilerParams(dimension_semantics=("arbitrary",)),
    )(parts, w, m, v)


SLAB = 1296
REMAP_RUNS = 3
_PIECES = ((O_QA, O_ZA, C_QA), (O_ZA, O_QG, C_ZA), (O_QG, O_GLR, C_QG), (O_GLR, O_ZG, C_GLR), (O_ZG, O_GA, C_ZG),
           (O_GA, O_END, C_GA))


def _slab_row_of_aligned(a):
    for o0, o1, a0 in _PIECES:
        if a0 <= a < a0 + o1 - o0:
            c = o0 + a - a0
            return SLAB * (c // W_IN_SHARD) + c % W_IN_SHARD
    return -1


def _aligned_row_of_slab(r):
    d, l = divmod(r, SLAB)
    if l >= W_IN_SHARD:
        return -1
    c = d * W_IN_SHARD + l
    for o0, o1, a0 in _PIECES:
        if o0 <= c < o1:
            return a0 + c - o0
    raise AssertionError(c)


def _remap_table(row_of, n_out, block, n_src):
    win = block + 16
    table = []
    for b in range(n_out // block):
        runs = []
        for i in range(block):
            s = row_of(b * block + i)
            if s < 0:
                continue
            if runs and runs[-1][0] + runs[-1][2] == s and runs[-1][1] + runs[-1][2] == i:
                runs[-1][2] += 1
            else:
                runs.append([s, i, 1])
        assert len(runs) <= REMAP_RUNS, (b, runs)
        row = []
        for s, i, n in runs:
            w = min(s // 16 * 16, n_src - win)
            assert 0 <= s - w and s - w + n <= win
            row += [w, s - w, i, n]
        table.append(row + [0] * (4 * REMAP_RUNS - len(row)))
    return table


def _remap_rows(src, row_of, n_out, block, name):
    n_src, cols = src.shape
    nb, win = n_out // block, block + 16
    table = _remap_table(row_of, n_out, block, n_src)
    runs = [[tuple(row[4 * k:4 * k + 4]) for k in range(REMAP_RUNS) if row[4 * k + 3] > 0] for row in table]

    def body(src_hbm, out_hbm, wbuf, obuf, insem, outsem):
        def fetches(b):
            return [pltpu.make_async_copy(src_hbm.at[pl.ds(w, win)], wbuf.at[b % 2, k], insem.at[b % 2, k])
                    for k, (w, _, _, _) in enumerate(runs[b])]

        def store(b):
            return pltpu.make_async_copy(obuf.at[b % 2], out_hbm.at[pl.ds(b * block, block)], outsem.at[b % 2])

        for cp in fetches(0):
            cp.start()
        for b in range(nb):
            if b + 1 < nb:
                for cp in fetches(b + 1):
                    cp.start()
            for cp in fetches(b):
                cp.wait()
            if b >= 2:
                store(b - 2).wait()
            if sum(count for _, _, _, count in runs[b]) < block:
                obuf[b % 2] = jnp.zeros((block, cols), src.dtype)
            for k, (_, shift, first, count) in enumerate(runs[b]):
                obuf[b % 2, first:first + count, :] = wbuf[b % 2, k, shift:shift + count, :]
            store(b).start()
        store(nb - 2).wait()
        store(nb - 1).wait()

    hbm = pl.BlockSpec(memory_space=pl.ANY)
    return pl.pallas_call(
        body, name=name, in_specs=[hbm], out_specs=hbm, out_shape=S((n_out, cols), src.dtype),
        scratch_shapes=[pltpu.VMEM((2, REMAP_RUNS, win, cols), src.dtype), pltpu.VMEM((2, block, cols), src.dtype),
                        pltpu.SemaphoreType.DMA((2, REMAP_RUNS)), pltpu.SemaphoreType.DMA((2,))],
    )(src)


def _col_blocks(w, width):
    return w.reshape(w.shape[0], NDEV, width).transpose(1, 0, 2)


def _from_col_blocks(w):
    return w.transpose(1, 0, 2).reshape(w.shape[1], NDEV * w.shape[2])


def _local_step(x2, p2, pos, tgt, norm_g, qk_norm_q, qk_norm_k, gla_gate_b, gla_norm_g, ple_norm_g, w_al,
                weights=None, proj_side=None, unpack=None, dw_side_of=None, dh_side_of=None):
    half = ROT_DIM // 2
    inv8 = jnp.power(jnp.float32(ROPE_THETA), -jnp.arange(half, dtype=F32) * 2.0 / ROT_DIM)
    inv = jnp.tile(jnp.concatenate([inv8, inv8, jnp.zeros((HD - ROT_DIM,), F32)]), 2).reshape(1, 128)
    gq = jnp.tile(qk_norm_q, (1, 2))
    gk = jnp.tile(qk_norm_k, (1, 2))

    proj, h, got = _proj_rms(x2, norm_g, w_al, proj_side)
    if proj_side is not None:
        weights = unpack(got)
    w2p, w_att_f, w_gla_f, w_out_f, w_pg_f, w_ple_f = weights
    qkv = _qk_prep(proj, pos, inv, gq, gk)
    fwd = [_att_fwd(qkv[g], qkv[3 + g], qkv[6 + g], g, f"att_fwd{g}") for g in range(3)]
    att, lse, ain = _att_merge([f[0] for f in fwd], [f[1] for f in fwd], proj)
    o_gla, bin_, states = _gla_fwd(proj, w2p, gla_gate_b, gla_norm_g)
    ya, yb, y, x1 = _branches_fwd(ain, bin_, proj, x2, w_att_f, w_gla_f, w_out_f)
    n2, loss_v, dout, du, dw_ple = _ple_loss(x1, p2, tgt, ple_norm_g, w_pg_f, w_ple_f)

    dx1, dy, dg_ple, dw_pg, dw_out = _ple_bwd(du, n2, y, x1, dout, ple_norm_g, w_pg_f, w_out_f)
    dproj, dain, dbin, dw_att, dw_gla = _branches_bwd(dy, ya, yb, ain, bin_, proj, w_att_f, w_gla_f)
    dproj, da0, da1, da2, at1, at2, ls1, ls2 = _att_gate_bwd(dain, att, lse, proj, dproj)
    datts, atts, lses = (da0, da1, da2), (att[None], at1, at2), (lse[None], ls1, ls2)
    dproj, dw2, dbg, dgn = _gla_bwd(proj, w2p, gla_gate_b, gla_norm_g, o_gla, states, dbin, dproj)
    bwd = [_att_bwd(qkv[g], qkv[3 + g], qkv[6 + g], datts[g], atts[g], lses[g], g, f"att_bwd{g}") for g in range(3)]
    dproj, dgq, dgk = _qk_bwd(proj, pos, inv, gq, gk, [b[0] for b in bwd], [b[1] for b in bwd],
                              [b[2] for b in bwd], dproj)
    out = dict(loss=loss_v, dw2=dw2, dw_att=dw_att, dw_gla=dw_gla, dw_out=dw_out, dw_pg=dw_pg, dw_ple=dw_ple,
               dgq=dgq, dgk=dgk, dbg=dbg, dgn=dgn, dg_ple=dg_ple)
    if dw_side_of is None:
        dw_al = _mm(dproj, h, mode="tn", name="dw_in", tm=1536, tn=D, tk=T, out_dtype=BF16)
    else:
        dw_al, out["dw_side"] = _mm(dproj, h, mode="tn", name="dw_in", tm=1536, tn=D, tk=T, out_dtype=BF16,
                                    side=dw_side_of(out))
    grad_x, dg_norm, out["dh_side"] = _dh_rms(dproj, w_al, x2, norm_g, dx1,
                                              None if dh_side_of is None else dh_side_of(dw_al))
    out.update(grad_x=grad_x, dw_al=dw_al, dg_norm=dg_norm)
    return out


def kernel(x, p, positions, norm_g, w_in, qk_norm_q, qk_norm_k, gla_gate_w2, gla_gate_b, gla_norm_g, w_att_proj, w_gla_proj, w_out, ple_norm_g, w_ple_gate, w_ple, loss_target, m_norm_g, m_w_in, m_qk_norm_q, m_qk_norm_k, m_gla_gate_w2, m_gla_gate_b, m_gla_norm_g, m_w_att_proj, m_w_gla_proj, m_w_out, m_ple_norm_g, m_w_ple_gate, m_w_ple, v_norm_g, v_w_in, v_qk_norm_q, v_qk_norm_k, v_gla_gate_w2, v_gla_gate_b, v_gla_norm_g, v_w_att_proj, v_w_gla_proj, v_w_out, v_ple_norm_g, v_w_ple_gate, v_w_ple):
    x2, p2, tgt = x[0], p[0, 0], loss_target[0]
    pos = positions.astype(F32).reshape(T, 1)

    rows3 = jnp.stack([w_gla_proj[0], w_out[0], w_ple_gate[0]]).astype(BF16)
    cols3 = jnp.concatenate([w_att_proj[0], w_ple[0], jnp.pad(gla_gate_w2[0], ((0, 0), (0, 64)))], axis=0).astype(BF16)
    mine = jnp.pad(w_in[0].T.astype(BF16), ((0, SLAB - W_IN_SHARD), (0, 0)))
    (g_in,) = _all_gather_by_chip([mine], "gather_w_in")
    w_al = _remap_rows(g_in.reshape(NDEV * SLAB, D), _slab_row_of_aligned, NCOL, 256, "align_w_in")

    def unpack(got):
        g_rows, g_cols = got
        w2_f = _from_col_blocks(g_cols[:, 768:784, :64])
        return (jnp.pad(w2_f, ((0, GLR_W - GLR_N), (0, 0))), _from_col_blocks(g_cols[:, :512]),
                g_rows[:, 0].reshape(D, D), g_rows[:, 1].reshape(D, D), g_rows[:, 2].reshape(D, D),
                _from_col_blocks(g_cols[:, 512:768]))

    def dw_side_of(g):
        s_rows = jnp.concatenate([g[k].reshape(NDEV, 128, D) for k in ("dw_gla", "dw_out", "dw_pg")], axis=1)
        s_cols = jnp.concatenate([_col_blocks(g["dw_att"], 128), _col_blocks(g["dw_ple"], 128),
                                  jnp.pad(_col_blocks(g["dw2"][:GLR_N], 64), ((0, 0), (0, 0), (0, 64)))], axis=1)
        return _exchange_side([s_rows.astype(BF16), s_cols.astype(BF16)])

    def dh_side_of(dw_al):
        s_in = _remap_rows(dw_al, _aligned_row_of_slab, NDEV * SLAB, 432, "shard_dw_in").reshape(NDEV, SLAB, D)
        return _chips_side([_sibling_sum(s_in, "sibling_sum")])

    loc = _local_step(x2, p2, pos, tgt, norm_g, qk_norm_q, qk_norm_k, gla_gate_b, gla_norm_g, ple_norm_g, w_al,
                      proj_side=_gather_side([rows3, cols3]), unpack=unpack, dw_side_of=dw_side_of,
                      dh_side_of=dh_side_of)
    loss_v, grad_x = loc["loss"], loc["grad_x"]
    dg_norm, dgq, dgk, dbg, dgn, dg_ple = (loc[k] for k in ("dg_norm", "dgq", "dgk", "dbg", "dgn", "dg_ple"))
    r_rows, r_cols = loc["dw_side"]
    (r_in,) = loc["dh_side"]

    r_small = _comm_call(_gather_side([dg_norm, dgq, dgk, dbg, dgn, dg_ple, loss_v]), "gather_small")

    outs = {}

    rows_of = lambda a: jnp.transpose(a, (2, 0, 1))
    outs["w_in"] = [jnp.transpose(o, (1, 2, 0))[0] for o in
                    _adamw_rows(r_in, rows_of(w_in), rows_of(m_w_in), rows_of(v_w_in), "adam_w_in")]
    places = (("w_gla_proj", 0, slice(0, 128), slice(None), (w_gla_proj, m_w_gla_proj, v_w_gla_proj)),
              ("w_out", 0, slice(128, 256), slice(None), (w_out, m_w_out, v_w_out)),
              ("w_ple_gate", 0, slice(256, 384), slice(None), (w_ple_gate, m_w_ple_gate, v_w_ple_gate)),
              ("w_att_proj", 1, slice(0, 512), slice(None), (w_att_proj, m_w_att_proj, v_w_att_proj)),
              ("w_ple", 1, slice(512, 768), slice(None), (w_ple, m_w_ple, v_w_ple)),
              ("gla_gate_w2", 1, slice(768, 784), slice(0, 64), (gla_gate_w2, m_gla_gate_w2, v_gla_gate_w2)))
    res = _adamw_shards([r_rows, r_cols], [place[1:] for place in places])
    for j, place in enumerate(places):
        outs[place[0]] = [o[0] for o in res[4 * j:4 * j + 4]]
    small = ((norm_g, m_norm_g, v_norm_g), (qk_norm_q, m_qk_norm_q, v_qk_norm_q), (qk_norm_k, m_qk_norm_k, v_qk_norm_k),
             (gla_gate_b, m_gla_gate_b, v_gla_gate_b), (gla_norm_g, m_gla_norm_g, v_gla_norm_g),
             (ple_norm_g, m_ple_norm_g, v_ple_norm_g))
    sm = _adamw_small(r_small[:6], small, r_small[6])
    for j, nm in enumerate(("norm_g", "qk_norm_q", "qk_norm_k", "gla_gate_b", "gla_norm_g", "ple_norm_g")):
        outs[nm] = [o[0] for o in sm[4 * j:4 * j + 4]]

    loss = sm[-1][0, 0]
    order = ["norm_g", "w_in", "qk_norm_q", "qk_norm_k", "gla_gate_w2", "gla_gate_b", "gla_norm_g", "w_att_proj",
             "w_gla_proj", "w_out", "ple_norm_g", "w_ple_gate", "w_ple"]
    result = [loss, grad_x[None]]
    for i in range(4):
        result += [outs[nm][i][None] for nm in order]
    return tuple(result)
```

```python
import functools

import jax
import jax.numpy as jnp
from jax import lax
from jax.experimental import pallas as pl
from jax.experimental.pallas import tpu as pltpu

F32 = jnp.float32
BF16 = jnp.bfloat16
S = jax.ShapeDtypeStruct

T = 4096
D = 1024
NDEV = 8
HD = 64
ATT_W = 512
ATT_QKV = 1536
DILATIONS = (1, 4, 16)
BLK = 128
GH, GDK, GDV = 4, 128, 256
GLA_C = 128
PLE = 256
EPS = 1e-6
ROT_DIM = 16
ROPE_THETA = 500000.0
GLA_TAU = 16.0
W_IN_SHARD = 1282

C_QG, C_KG, C_VG, C_ZG, C_GLR, C_ZA, C_GA, C_GB, C_QA, C_KA, C_VA = (
    0, 512, 1024, 2048, 3072, 3584, 4096, 5120, 6144, 7680, 9216)
GLA_GROUP_W = 3584
GLR_W = 512
NCOL = 10752
GLR_N = 16
O_QA, O_ZA, O_QG, O_GLR, O_ZG, O_GA, O_END = 0, 4608, 5120, 7168, 7184, 8208, 10256

ADAM_LR, ADAM_B1, ADAM_B2, ADAM_EPS, ADAM_WD, ADAM_STEP = 0.001, 0.9, 0.999, 1e-08, 0.01, 10

MESH = pl.DeviceIdType.MESH


def _sigmoid(z):
    return 1.0 / (1.0 + jnp.exp(-z))


def _dot(a, b, dims):
    return lax.dot_general(a, b, (dims, ((), ())), preferred_element_type=F32)


def _nn(a, b):
    return _dot(a, b, ((1,), (0,)))


def _nt(a, b):
    return _dot(a, b, ((1,), (1,)))


def _tn(a, b):
    return _dot(a, b, ((0,), (0,)))


def _mm(a, b, *, mode, name, tm, tn, tk, out_dtype=F32, res=None, side=None):
    if mode == "nn":
        (m, k), n = a.shape, b.shape[1]
        a_spec = pl.BlockSpec((tm, tk), lambda i, j, l: (i, l))
        b_spec = pl.BlockSpec((tk, tn), lambda i, j, l: (l, j))
        dot = _nn
    elif mode == "nt":
        (m, k), n = a.shape, b.shape[0]
        a_spec = pl.BlockSpec((tm, tk), lambda i, j, l: (i, l))
        b_spec = pl.BlockSpec((tn, tk), lambda i, j, l: (j, l))
        dot = _nt
    else:
        (k, m), n = a.shape, b.shape[1]
        a_spec = pl.BlockSpec((tk, tm), lambda i, j, l: (l, i))
        b_spec = pl.BlockSpec((tk, tn), lambda i, j, l: (l, j))
        dot = _tn
    assert m % tm == 0 and n % tn == 0 and k % tk == 0, (name, m, n, k)
    grid = (m // tm, n // tn, k // tk)
    nk = grid[2]
    o_spec = pl.BlockSpec((tm, tn), lambda i, j, l: (i, j))
    in_specs = [a_spec, b_spec]
    args = [a, b]
    if res is not None:
        in_specs.append(o_spec)
        args.append(res)
    n_in = len(args)
    n_side = 0 if side is None else len(side["arrs"])
    hbm = pl.BlockSpec(memory_space=pl.ANY)

    def body(*refs):
        a_ref, b_ref = refs[0], refs[1]
        r_ref = refs[2] if res is not None else None
        o_ref = refs[n_in + n_side]
        scratch = refs[n_in + 2 * n_side + 1:]
        if side is not None:
            start, finish_side = side["plan"](refs[n_in:n_in + n_side], refs[n_in + n_side + 1:n_in + 2 * n_side + 1],
                                              *scratch[1 if nk > 1 else 0:])
            ids = [pl.program_id(d) for d in range(3)]

            @pl.when((ids[0] == 0) & (ids[1] == 0) & (ids[2] == 0))
            def _():
                start()

        part = dot(a_ref[...].astype(BF16), b_ref[...].astype(BF16))

        def finish(val):
            if r_ref is not None:
                val = val + r_ref[...]
            o_ref[...] = val.astype(out_dtype)

        if nk == 1:
            finish(part)
        else:
            acc = scratch[0]
            l = pl.program_id(2)

            @pl.when(l == 0)
            def _():
                acc[...] = part

            @pl.when(l > 0)
            def _():
                acc[...] += part

            @pl.when(l == nk - 1)
            def _():
                finish(acc[...])

        if side is not None:
            @pl.when((ids[0] == grid[0] - 1) & (ids[1] == grid[1] - 1) & (ids[2] == grid[2] - 1))
            def _():
                finish_side()

    sems = [] if side is None else side["scratch"]
    outs = pl.pallas_call(
        body, name=name, grid=grid,
        in_specs=in_specs + [hbm] * n_side, out_specs=[o_spec] + [hbm] * n_side,
        out_shape=[S((m, n), out_dtype)] + ([] if side is None else side["out_shape"]),
        scratch_shapes=([pltpu.VMEM((tm, tn), F32)] if nk > 1 else []) + sems,
        compiler_params=pltpu.CompilerParams(
            dimension_semantics=("arbitrary",) * 3 if side is not None else ("parallel", "parallel", "arbitrary")),
    )(*args, *([] if side is None else side["arrs"]))
    return outs[0] if side is None else (outs[0], outs[1:])


def _side_parts(side, refs, n_in, n_out):
    n_side = 0 if side is None else len(side["arrs"])
    scratch = refs[n_in + n_out + 2 * n_side:]
    if side is None:
        return (lambda: None), (lambda: None), scratch
    start, finish = side["plan"](refs[n_in:n_in + n_side], refs[n_in + n_side + n_out:n_in + n_out + 2 * n_side],
                                 *scratch[len(scratch) - len(side["scratch"]):])
    return start, finish, scratch


def _proj_rms(x, g, wt, side=None):
    tm, tn = 1024, 1536
    grid = (T // tm, NCOL // tn)
    n_side = 0 if side is None else len(side["arrs"])
    hbm = pl.BlockSpec(memory_space=pl.ANY)

    def body(*refs):
        x_ref, g_ref, w_ref = refs[:3]
        o_ref, h_ref = refs[3 + n_side], refs[4 + n_side]
        start, finish, _ = _side_parts(side, refs, 3, 2)
        i, j = pl.program_id(0), pl.program_id(1)

        @pl.when((i == 0) & (j == 0))
        def _():
            start()

        @pl.when(j == 0)
        def _():
            xf = x_ref[...]
            r = lax.rsqrt(jnp.mean(xf * xf, axis=-1, keepdims=True) + EPS)
            h_ref[...] = (xf * r * g_ref[...]).astype(BF16)

        o_ref[...] = _nt(h_ref[...], w_ref[...])

        @pl.when((i == grid[0] - 1) & (j == grid[1] - 1))
        def _():
            finish()

    outs = pl.pallas_call(
        body, name="proj", grid=grid,
        in_specs=[pl.BlockSpec((tm, D), lambda i, j: (i, 0)), pl.BlockSpec((1, D), lambda i, j: (0, 0)),
                  pl.BlockSpec((tn, D), lambda i, j: (j, 0))] + [hbm] * n_side,
        out_specs=[pl.BlockSpec((tm, tn), lambda i, j: (i, j)), pl.BlockSpec((tm, D), lambda i, j: (i, 0))] + [hbm] * n_side,
        out_shape=[S((T, NCOL), F32), S((T, D), BF16)] + ([] if side is None else side["out_shape"]),
        scratch_shapes=[] if side is None else side["scratch"],
        compiler_params=pltpu.CompilerParams(dimension_semantics=("arbitrary", "arbitrary")),
    )(x, g, wt, *([] if side is None else side["arrs"]))
    return outs[0], outs[1], outs[2:]


def _dh_rms(dproj, wt, x, g, skip, side=None):
    tm, tk = 1024, 2688
    grid = (T // tm, NCOL // tk)
    n_side = 0 if side is None else len(side["arrs"])
    hbm = pl.BlockSpec(memory_space=pl.ANY)

    def body(*refs):
        a_ref, w_ref, x_ref, g_ref, s_ref = refs[:5]
        dx_ref, dg_ref = refs[5 + n_side], refs[6 + n_side]
        start, finish, scratch = _side_parts(side, refs, 5, 2)
        acc = scratch[0]
        i, l = pl.program_id(0), pl.program_id(1)

        @pl.when((i == 0) & (l == 0))
        def _():
            start()

        part = _nn(a_ref[...], w_ref[...])

        @pl.when(l == 0)
        def _():
            acc[...] = part

        @pl.when(l > 0)
        def _():
            acc[...] += part

        @pl.when(l == grid[1] - 1)
        def _():
            xf = x_ref[...]
            r = lax.rsqrt(jnp.mean(xf * xf, axis=-1, keepdims=True) + EPS)
            dn = acc[...]
            u = dn * g_ref[...]
            dx_ref[...] = s_ref[...] + r * u - xf * (r * r * r) * jnp.mean(u * xf, axis=-1, keepdims=True)
            dg = jnp.sum(dn * xf * r, axis=0, keepdims=True)

            @pl.when(i == 0)
            def _():
                dg_ref[...] = dg

            @pl.when(i > 0)
            def _():
                dg_ref[...] += dg

        @pl.when((i == grid[0] - 1) & (l == grid[1] - 1))
        def _():
            finish()

    tok = pl.BlockSpec((tm, D), lambda i, l: (i, 0))
    outs = pl.pallas_call(
        body, name="dh", grid=grid,
        in_specs=[pl.BlockSpec((tm, tk), lambda i, l: (i, l)), pl.BlockSpec((tk, D), lambda i, l: (l, 0)), tok,
                  pl.BlockSpec((1, D), lambda i, l: (0, 0)), tok] + [hbm] * n_side,
        out_specs=[tok, pl.BlockSpec((1, D), lambda i, l: (0, 0))] + [hbm] * n_side,
        out_shape=[S((T, D), F32), S((1, D), F32)] + ([] if side is None else side["out_shape"]),
        scratch_shapes=[pltpu.VMEM((tm, D), F32)] + ([] if side is None else side["scratch"]),
        compiler_params=pltpu.CompilerParams(dimension_semantics=("arbitrary", "arbitrary")),
    )(dproj, wt, x, g, skip, *([] if side is None else side["arrs"]))
    return outs[0], outs[1], outs[2:]


def _rot_tables(pos_ref, inv_ref):
    lane = lax.broadcasted_iota(jnp.int32, (1, 128), 1) % HD
    ang = pos_ref[...] * inv_ref[...]
    cos, sin = jnp.cos(ang), jnp.sin(ang)
    c = jnp.where(lane < ROT_DIM, cos, 1.0)
    sp = jnp.where((lane >= ROT_DIM // 2) & (lane < ROT_DIM), sin, 0.0)
    sm = jnp.where(lane < ROT_DIM // 2, -sin, 0.0)
    return c, sp, sm


def _head_sums(v):
    same = (lax.broadcasted_iota(jnp.int32, (128, 128), 0) < HD) == (lax.broadcasted_iota(jnp.int32, (128, 128), 1) < HD)
    ones = jnp.where(same, 1.0, 0.0).astype(BF16)
    hi = v.astype(BF16)
    lo = (v - hi.astype(F32)).astype(BF16)
    return _nn(hi, ones) + _nn(lo, ones)


def _pair_norm(t):
    return lax.rsqrt(_head_sums(t * t) * (1.0 / HD) + EPS)


def _pair_mean(t):
    return _head_sums(t) * (1.0 / HD)


TT = 256
NCH = ATT_QKV // 128


def _res_shape(grp, dtype):
    return S((DILATIONS[grp], T // DILATIONS[grp], ATT_W), dtype)


def _res_spec(grp):
    dil = DILATIONS[grp]
    return pl.BlockSpec((dil, TT // dil, ATT_W), lambda i: (0, i, 0))


def _to_residues(sc, j, dst_ref, dil, cols):
    n = TT // dil
    for r in range(dil):
        rows = sc[j] if dil == 1 else sc.at[j][pl.ds(r, n, stride=dil), :]
        dst_ref[r, :, cols] = rows.astype(dst_ref.dtype)


def _from_residues(src_ref, cols, sc, j, dil):
    n = TT // dil
    for r in range(dil):
        if dil == 1:
            sc[j] = src_ref[r, :, cols]
        else:
            sc.at[j][pl.ds(r, n, stride=dil), :] = src_ref[r, :, cols]


def _tok_spec(width, cblk=0):
    return pl.BlockSpec((TT, width), functools.partial(lambda i, c: (i, c), c=cblk))


def _const_spec(arr_or_shape):
    shape = arr_or_shape if isinstance(arr_or_shape, tuple) else arr_or_shape.shape
    return pl.BlockSpec(shape, functools.partial(lambda i, nd: (0,) * nd, nd=len(shape)))


def _qk_prep(proj, pos, inv, gq, gk):
    def body(q_ref, k_ref, v_ref, pos_ref, inv_ref, gq_ref, gk_ref, *rest):
        outs, sc = rest[:9], rest[9]
        c, sp, sm = _rot_tables(pos_ref, inv_ref)
        for which, (src, g_ref) in enumerate(((q_ref, gq_ref), (k_ref, gk_ref), (v_ref, None))):
            if g_ref is not None:
                g = jnp.broadcast_to(g_ref[...] * ((HD ** -0.5) if which == 0 else 1.0), c.shape)
                cg, spg, smg = c * g, sp * pltpu.roll(g, 8, 1), sm * pltpu.roll(g, 120, 1)
            for j in range(NCH):
                t = src[:, j * 128:(j + 1) * 128]
                if g_ref is not None:
                    t = _pair_norm(t) * (t * cg + pltpu.roll(t, 8, 1) * spg + pltpu.roll(t, 120, 1) * smg)
                sc[j] = t
            for j in range(NCH):
                grp, sub = divmod(j * 128, ATT_W)
                _to_residues(sc, j, outs[which * 3 + grp], DILATIONS[grp], slice(sub, sub + 128))

    return pl.pallas_call(
        body, name="qk_prep", grid=(T // TT,),
        in_specs=[_tok_spec(ATT_QKV, C_QA // ATT_QKV), _tok_spec(ATT_QKV, C_KA // ATT_QKV),
                  _tok_spec(ATT_QKV, C_VA // ATT_QKV), _tok_spec(1), _const_spec(inv), _const_spec(gq), _const_spec(gk)],
        out_specs=[_res_spec(g) for _ in range(3) for g in range(3)],
        out_shape=[_res_shape(g, BF16) for _ in range(3) for g in range(3)],
        scratch_shapes=[pltpu.VMEM((NCH, TT, 128), F32)],
        compiler_params=pltpu.CompilerParams(dimension_semantics=("arbitrary",)),
    )(proj, proj, proj, pos, inv, gq, gk)


def _qk_bwd(proj, pos, inv, gq, gk, dqs, dks, dvs, dproj):
    const = lambda a: pl.BlockSpec(a.shape, functools.partial(lambda i, p, nd: (0,) * nd, nd=a.ndim))
    res = lambda g: pl.BlockSpec((DILATIONS[g], TT // DILATIONS[g], ATT_W), lambda i, p: (0, i, 0))
    base = C_QA // ATT_QKV

    def body(t_ref, pos_ref, inv_ref, gq_ref, gk_ref, dq0, dq1, dq2, dk0, dk1, dk2, dv0, dv1, dv2, buf_ref,
             out_ref, dgq_ref, dgk_ref, sc):
        del buf_ref
        part = pl.program_id(1)
        first = pl.program_id(0) == 0

        def gather(drefs):
            for j in range(NCH):
                grp, sub = divmod(j * 128, ATT_W)
                _from_residues(drefs[grp], slice(sub, sub + 128), sc, j, DILATIONS[grp])

        def normed(g_ref, drefs, dg_ref):
            c, sp, sm = _rot_tables(pos_ref, inv_ref)
            gather(drefs)
            dg = jnp.zeros((1, 128), F32)
            for j in range(NCH):
                cols = slice(j * 128, (j + 1) * 128)
                d_rot = sc[j]
                dn = d_rot * c + pltpu.roll(d_rot * sp, 120, 1) + pltpu.roll(d_rot * sm, 8, 1)
                t = t_ref[:, cols]
                r = _pair_norm(t)
                gain = g_ref[...]
                dn_t = dn * t
                out_ref[:, cols] = (r * (dn * gain - t * ((r * r) * _pair_mean(dn_t * gain)))).astype(BF16)
                dg = dg + jnp.sum(dn_t * r, axis=0, keepdims=True)
            dg = dg + pltpu.roll(dg, HD, 1)

            @pl.when(first)
            def _():
                dg_ref[...] = dg

            @pl.when(jnp.logical_not(first))
            def _():
                dg_ref[...] += dg

        @pl.when(part == 0)
        def _():
            gather((dv0, dv1, dv2))
            for j in range(NCH):
                out_ref[:, j * 128:(j + 1) * 128] = sc[j].astype(BF16)

        @pl.when(part == 1)
        def _():
            normed(gq_ref, (dq0, dq1, dq2), dgq_ref)

        @pl.when(part == 2)
        def _():
            normed(gk_ref, (dk0, dk1, dk2), dgk_ref)

    keep = pl.BlockSpec((1, 128), lambda i, p: (0, 0))
    return pl.pallas_call(
        body, name="qk_bwd", grid=(T // TT, 3),
        in_specs=[pl.BlockSpec((TT, ATT_QKV), lambda i, p: (i, base + jnp.maximum(p - 1, 0))),
                  pl.BlockSpec((TT, 1), lambda i, p: (i, 0)), const(inv), const(gq), const(gk)]
        + [res(g) for _ in range(3) for g in range(3)] + [pl.BlockSpec(memory_space=pl.ANY)],
        out_specs=[pl.BlockSpec((TT, ATT_QKV), lambda i, p: (i, base + jnp.where(p == 0, 2, p - 1))), keep, keep],
        out_shape=[S(dproj.shape, dproj.dtype), S((1, 128), F32), S((1, 128), F32)],
        input_output_aliases={14: 0},
        scratch_shapes=[pltpu.VMEM((NCH, TT, 128), F32)],
        compiler_params=pltpu.CompilerParams(dimension_semantics=("arbitrary", "arbitrary")),
    )(proj, pos, inv, gq, gk, *dqs, *dks, *dvs, dproj)


def _split_heads(t):
    low = lax.broadcasted_iota(jnp.int32, (1, 128), 1) < HD
    zero = jnp.zeros_like(t)
    return jnp.concatenate([jnp.where(low, t, zero), jnp.where(low, zero, t)], axis=0)


def _join_heads(t2):
    low = lax.broadcasted_iota(jnp.int32, (1, 128), 1) < HD
    n = t2.shape[0] // 2
    return jnp.where(low, t2[:n], t2[n:])


def _band_mask4(has_before, has_own):
    row = lax.broadcasted_iota(jnp.int32, (BLK, 4 * BLK), 0)
    lane = lax.broadcasted_iota(jnp.int32, (BLK, 4 * BLK), 1)
    key = lane & (BLK - 1)
    own = lane >= 2 * BLK
    return (own & (key <= row) & has_own) | (jnp.logical_not(own) & (key >= row) & has_before)


def _band_mask_before(has_before):
    row = lax.broadcasted_iota(jnp.int32, (BLK, 2 * BLK), 0)
    key = lax.broadcasted_iota(jnp.int32, (BLK, 2 * BLK), 1) & (BLK - 1)
    return (key >= row) & has_before


def _per_head(width, col_a, col_b):
    lane = lax.broadcasted_iota(jnp.int32, (1, width), 1)
    return jnp.where((lane & BLK) == 0, col_a, col_b)


NQ = ATT_W // 128


def _att_fwd(q, k, v, grp, name):
    dil = DILATIONS[grp]
    nb = T // dil // BLK

    def body(q_ref, kp_ref, kc_ref, vp_ref, vc_ref, o_ref, lse_ref, s_sc, p_sc):
        mask = _band_mask4(pl.program_id(1) > 0, True)
        low = lax.broadcasted_iota(jnp.int32, (1, 128), 1) < HD
        halves = lambda ref, j, h: (ref[j, :, h * BLK:(h + 1) * BLK], ref[j, :, (h + 2) * BLK:(h + 3) * BLK])
        for j in range(NQ):
            cols = slice(j * 128, (j + 1) * 128)
            k4 = jnp.concatenate([_split_heads(kp_ref[:, cols]), _split_heads(kc_ref[:, cols])], axis=0)
            s_sc[j] = jnp.where(mask, _nt(q_ref[:, cols], k4), -jnp.inf)
        mxs = [[jnp.maximum(*(jnp.max(t, axis=-1, keepdims=True) for t in halves(s_sc, j, h))) for h in range(2)]
               for j in range(NQ)]
        dens = []
        for j in range(NQ):
            p = jnp.exp(s_sc[j] - _per_head(4 * BLK, *mxs[j]))
            p_sc[j] = p.astype(BF16)
            dens.append([jnp.sum(p[:, h * BLK:(h + 1) * BLK], axis=-1, keepdims=True)
                         + jnp.sum(p[:, (h + 2) * BLK:(h + 3) * BLK], axis=-1, keepdims=True) for h in range(2)])
        for j in range(NQ):
            cols = slice(j * 128, (j + 1) * 128)
            v4 = jnp.concatenate([_split_heads(vp_ref[:, cols]), _split_heads(vc_ref[:, cols])], axis=0)
            o_ref[:, cols] = _nn(p_sc[j], v4) / jnp.where(low, dens[j][0], dens[j][1])
            lse_ref[:, cols] = jnp.where(low, mxs[j][0] + jnp.log(dens[j][0]), mxs[j][1] + jnp.log(dens[j][1]))

    cur = pl.BlockSpec((None, BLK, ATT_W), lambda r, i: (r, i, 0))
    prev = pl.BlockSpec((None, BLK, ATT_W), lambda r, i: (r, jnp.maximum(i - 1, 0), 0))
    return pl.pallas_call(
        body, name=name, grid=(dil, nb),
        in_specs=[cur, prev, cur, prev, cur],
        out_specs=[cur, cur], out_shape=[_res_shape(grp, F32)] * 2,
        scratch_shapes=[pltpu.VMEM((NQ, BLK, 4 * BLK), F32), pltpu.VMEM((NQ, BLK, 4 * BLK), BF16)],
        compiler_params=pltpu.CompilerParams(dimension_semantics=("parallel", "arbitrary")),
    )(q, k, k, v, v)


def _att_bwd(q, k, v, datt, att, lse, grp, name):
    dil = DILATIONS[grp]
    nb = T // dil // BLK
    scale = HD ** -0.5

    def body(q0_ref, q1_ref, kp_ref, kc_ref, vp_ref, vc_ref, do0_ref, do1_ref, o0_ref, o1_ref, l0_ref, l1_ref,
             dq_ref, dk_ref, dv_ref, k4_sc, v4_sc, s0_sc, s1_sc, dp0_sc, dp1_sc, p_sc, ds_sc):
        i = pl.program_id(1)
        mask_mine = _band_mask4(i > 0, True)
        mask_next = _band_mask_before(i < nb - 1)
        low = lax.broadcasted_iota(jnp.int32, (1, 128), 1) < HD
        for j in range(NQ):
            cols = slice(j * 128, (j + 1) * 128)
            k4_sc[j, :2 * BLK] = _split_heads(kp_ref[:, cols])
            k4_sc[j, 2 * BLK:] = _split_heads(kc_ref[:, cols])
            v4_sc[j, :2 * BLK] = _split_heads(vp_ref[:, cols])
            v4_sc[j, 2 * BLK:] = _split_heads(vc_ref[:, cols])
        for j in range(NQ):
            cols = slice(j * 128, (j + 1) * 128)
            s0_sc[j] = _nt(q0_ref[:, cols], k4_sc[j])
            s1_sc[j] = _nt(q1_ref[:, cols], k4_sc[j, 2 * BLK:])
            dp0_sc[j] = _nt(do0_ref[:, cols].astype(BF16), v4_sc[j])
            dp1_sc[j] = _nt(do1_ref[:, cols].astype(BF16), v4_sc[j, 2 * BLK:])
        stats = []
        for j in range(NQ):
            cols = slice(j * 128, (j + 1) * 128)
            for do_ref, o_ref, l_ref in ((do0_ref, o0_ref, l0_ref), (do1_ref, o1_ref, l1_ref)):
                prod = do_ref[:, cols].astype(F32) * o_ref[:, cols].astype(F32)
                d_all = jnp.sum(prod, axis=-1, keepdims=True)
                d_low = jnp.sum(jnp.where(low, prod, 0.0), axis=-1, keepdims=True)
                lse_t = l_ref[:, cols]
                stats.append((d_low, d_all - d_low, lse_t[:, 0:1], lse_t[:, HD:HD + 1]))
        for j in range(NQ):
            (da, db, la, lb), (da1, db1, la1, lb1) = stats[2 * j], stats[2 * j + 1]
            p0 = jnp.where(mask_mine, jnp.exp(s0_sc[j] - _per_head(4 * BLK, la, lb)), 0.0)
            ds0 = p0 * (dp0_sc[j] - _per_head(4 * BLK, da, db))
            p1 = jnp.where(mask_next, jnp.exp(s1_sc[j] - _per_head(2 * BLK, la1, lb1)), 0.0)
            ds1 = p1 * (dp1_sc[j] - _per_head(2 * BLK, da1, db1))
            p_sc[j, :BLK] = p0.astype(BF16)
            ds_sc[j, :BLK] = ds0.astype(BF16)
            p_sc[j, BLK:, 2 * BLK:] = p1.astype(BF16)
            ds_sc[j, BLK:, 2 * BLK:] = ds1.astype(BF16)
        for j in range(NQ):
            cols = slice(j * 128, (j + 1) * 128)
            dq_ref[:, cols] = _nn(ds_sc[j, :BLK], k4_sc[j]) * scale
            qq = jnp.concatenate([q0_ref[:, cols], q1_ref[:, cols]], axis=0)
            dd = jnp.concatenate([do0_ref[:, cols], do1_ref[:, cols]], axis=0).astype(BF16)
            dk_ref[:, cols] = _join_heads(_tn(ds_sc[j, :, 2 * BLK:], qq))
            dv_ref[:, cols] = _join_heads(_tn(p_sc[j, :, 2 * BLK:], dd))

    def spec(shift):
        return pl.BlockSpec((None, BLK, ATT_W), lambda r, i: (r, jnp.clip(i + shift, 0, nb - 1), 0))

    here, after, before = spec(0), spec(1), spec(-1)
    vm = pltpu.VMEM
    return pl.pallas_call(
        body, name=name, grid=(dil, nb),
        in_specs=[here, after, before, here, before, here, here, after, here, after, here, after],
        out_specs=[here] * 3, out_shape=[_res_shape(grp, F32)] * 3,
        scratch_shapes=[vm((NQ, 4 * BLK, 128), BF16), vm((NQ, 4 * BLK, 128), BF16), vm((NQ, BLK, 4 * BLK), F32),
                        vm((NQ, BLK, 2 * BLK), F32), vm((NQ, BLK, 4 * BLK), F32), vm((NQ, BLK, 2 * BLK), F32),
                        vm((NQ, 2 * BLK, 4 * BLK), BF16), vm((NQ, 2 * BLK, 4 * BLK), BF16)],
        compiler_params=pltpu.CompilerParams(dimension_semantics=("parallel", "arbitrary")),
    )(q, q, k, k, v, v, datt, datt, att, att, lse, lse)


def _att_merge(os_, lses, proj):
    nq = ATT_W // 128

    def body(o0, o1, o2, l0, l1, l2, za_ref, att_ref, lse_ref, ain_ref, sc):
        for a, ref in enumerate((o0, o1, o2, l0, l1, l2)):
            for j in range(nq):
                _from_residues(ref, slice(j * 128, (j + 1) * 128), sc, a * nq + j, DILATIONS[a % 3])
        for j in range(nq):
            cols = slice(j * 128, (j + 1) * 128)
            oa, ob, oc = (sc[a * nq + j] for a in range(3))
            la, lb, lc = (sc[(3 + a) * nq + j] for a in range(3))
            m = jnp.maximum(jnp.maximum(la, lb), lc)
            wa, wb, wc = jnp.exp(la - m), jnp.exp(lb - m), jnp.exp(lc - m)
            tot = wa + wb + wc
            att = (wa * oa + wb * ob + wc * oc) / tot
            att_ref[:, cols] = att
            lse_ref[:, cols] = m + jnp.log(tot)
            za = za_ref[:, cols]
            ain_ref[:, cols] = (att * za * _sigmoid(za)).astype(BF16)

    return pl.pallas_call(
        body, name="att_merge", grid=(T // TT,),
        in_specs=[_res_spec(g) for _ in range(2) for g in range(3)] + [_tok_spec(ATT_W, C_ZA // ATT_W)],
        out_specs=[_tok_spec(ATT_W)] * 3,
        out_shape=[S((T, ATT_W), F32), S((T, ATT_W), F32), S((T, ATT_W), BF16)],
        scratch_shapes=[pltpu.VMEM((6 * nq, TT, 128), F32)],
        compiler_params=pltpu.CompilerParams(dimension_semantics=("arbitrary",)),
    )(*os_, *lses, proj)


def _att_gate_bwd(dain, att, lse, proj, dproj):
    nq = ATT_W // 128

    def body(d_ref, att_ref, lse_ref, za_ref, buf_ref, dza_ref, da0, da1, da2, at1, at2, ls1, ls2, sc):
        del buf_ref
        for j in range(nq):
            cols = slice(j * 128, (j + 1) * 128)
            za = za_ref[:, cols]
            sg = _sigmoid(za)
            d = d_ref[:, cols].astype(F32)
            att_ = att_ref[:, cols]
            dza_ref[:, cols] = (d * att_ * sg * (1.0 + za * (1.0 - sg))).astype(BF16)
            sc[j] = d * za * sg
            sc[nq + j] = att_
            sc[2 * nq + j] = lse_ref[:, cols]
        for j in range(nq):
            cols = slice(j * 128, (j + 1) * 128)
            for grp, dst in enumerate((da0, da1, da2)):
                _to_residues(sc, j, dst, DILATIONS[grp], cols)
            for grp, dst in ((1, at1), (2, at2)):
                _to_residues(sc, nq + j, dst, DILATIONS[grp], cols)
            for grp, dst in ((1, ls1), (2, ls2)):
                _to_residues(sc, 2 * nq + j, dst, DILATIONS[grp], cols)

    res = (0, 1, 2, 1, 2, 1, 2)
    return pl.pallas_call(
        body, name="att_gate_bwd", grid=(T // TT,),
        in_specs=[_tok_spec(ATT_W)] * 3 + [_tok_spec(ATT_W, C_ZA // ATT_W), pl.BlockSpec(memory_space=pl.ANY)],
        out_specs=[_tok_spec(ATT_W, C_ZA // ATT_W)] + [_res_spec(g) for g in res],
        out_shape=[S(dproj.shape, dproj.dtype)] + [_res_shape(g, BF16) for g in res[:5]]
        + [_res_shape(g, F32) for g in res[5:]],
        input_output_aliases={4: 0},
        scratch_shapes=[pltpu.VMEM((3 * nq, TT, 128), F32)],
        compiler_params=pltpu.CompilerParams(dimension_semantics=("arbitrary",)),
    )(dain, att, lse, proj, dproj)


def _split3(v):
    hi = v.astype(BF16)
    r1 = v - hi.astype(F32)
    mid = r1.astype(BF16)
    lo = (r1 - mid.astype(F32)).astype(BF16)
    return hi, mid, lo


def _chunk_scores(qt, kt, q_ref, k_ref, h):
    cols = slice(h * GDK, (h + 1) * GDK)
    own = jnp.sum(q_ref[:, cols] * (GDK ** -0.5) * k_ref[:, cols], axis=-1, keepdims=True)
    row = lax.broadcasted_iota(jnp.int32, (GLA_C, GLA_C), 0)
    col = lax.broadcasted_iota(jnp.int32, (GLA_C, GLA_C), 1)
    a = _nt(qt.astype(BF16), kt.astype(BF16))
    return jnp.where(col < row, a, jnp.where(col == row, own, 0.0))


def _tri_sum(v, upper):
    n = v.shape[0]
    row = lax.broadcasted_iota(jnp.int32, (n, n), 0)
    col = lax.broadcasted_iota(jnp.int32, (n, n), 1)
    tri = jnp.where(col >= row if upper else col <= row, 1.0, 0.0).astype(BF16)
    hi, mid, lo = _split3(v)
    return _nn(tri, hi) + _nn(tri, mid) + _nn(tri, lo)


def _gla_gates(glr_ref, w2_ref, b_ref):
    logit = _nn(glr_ref[...].astype(BF16), w2_ref[...]) + b_ref[...]
    lg = (jnp.minimum(logit, 0.0) - jnp.log(1.0 + jnp.exp(-jnp.abs(logit)))) * (1.0 / GLA_TAU)
    return logit, _tri_sum(lg, upper=False)


def _gla_head(cum, q_ref, k_ref, h):
    cols = slice(h * GDK, (h + 1) * GDK)
    b = cum[:, cols]
    last = b[GLA_C - 1:GLA_C, :]
    e_pos = jnp.exp(b)
    e_neg = jnp.exp(-b)
    e_end = jnp.exp(last - b)
    qt = q_ref[:, cols] * (GDK ** -0.5) * e_pos
    kt = k_ref[:, cols] * e_neg
    kh = k_ref[:, cols] * e_end
    return b, last, e_pos, e_neg, e_end, qt, kt, kh


def _causal(n):
    return lax.broadcasted_iota(jnp.int32, (n, n), 1) <= lax.broadcasted_iota(jnp.int32, (n, n), 0)


def _gla_fwd(proj, w2p, bg, gn):
    nc = T // GLA_C

    def body(q_ref, k_ref, v_ref, glr_ref, zg_ref, w2_ref, b_ref, gn_ref, o_ref, bin_ref, st_ref, state):
        @pl.when(pl.program_id(0) == 0)
        def _():
            state[...] = jnp.zeros_like(state)

        _, cum = _gla_gates(glr_ref, w2_ref, b_ref)
        for h in range(GH):
            _, last, _, _, _, qt, kt, kh = _gla_head(cum, q_ref, k_ref, h)
            vcols = slice(h * GDV, (h + 1) * GDV)
            st = state[h]
            st_ref[0, h] = st
            v = v_ref[:, vcols].astype(BF16)
            qb = qt.astype(BF16)
            a = _chunk_scores(qt, kt, q_ref, k_ref, h)
            o = _nt(qb, st.astype(BF16)) + _nn(a.astype(BF16), v)
            state[h] = st * jnp.exp(last) + _tn(v, kh.astype(BF16))
            o_ref[:, vcols] = o
            r = lax.rsqrt(jnp.mean(o * o, axis=-1, keepdims=True) + EPS)
            zg = zg_ref[:, vcols]
            bin_ref[:, vcols] = (o * r * gn_ref[...] * zg * _sigmoid(zg)).astype(BF16)

    row = lambda width, cblk: pl.BlockSpec((GLA_C, width), functools.partial(lambda i, c: (i, c), c=cblk))
    full = lambda a: pl.BlockSpec(a.shape, functools.partial(lambda i, nd: (0,) * nd, nd=a.ndim))
    return pl.pallas_call(
        body, name="gla_fwd", grid=(nc,),
        in_specs=[row(512, C_QG // 512), row(512, C_KG // 512), row(1024, C_VG // 1024), row(GLR_W, C_GLR // GLR_W),
                  row(1024, C_ZG // 1024), full(w2p), full(bg), full(gn)],
        out_specs=[pl.BlockSpec((GLA_C, GH * GDV), lambda i: (i, 0)), pl.BlockSpec((GLA_C, GH * GDV), lambda i: (i, 0)),
                   pl.BlockSpec((1, GH, GDV, GDK), lambda i: (i, 0, 0, 0))],
        out_shape=[S((T, GH * GDV), F32), S((T, GH * GDV), BF16), S((nc, GH, GDV, GDK), F32)],
        scratch_shapes=[pltpu.VMEM((GH, GDV, GDK), F32)],
        compiler_params=pltpu.CompilerParams(dimension_semantics=("arbitrary",)),
    )(proj, proj, proj, proj, proj, w2p, bg, gn)


def _gla_bwd(proj, w2p, bg, gn, o_gla, states, dbin, dproj):
    nc = T // GLA_C

    def body(q_ref, k_ref, v_ref, glr_ref, zg_ref, w2_ref, b_ref, gn_ref, o_ref, st_ref, dbin_ref, buf_ref,
             out_ref, dw2_ref, dbg_ref, dgn_ref, dstate, dlogit):
        del buf_ref
        dq_ref = out_ref.at[:, C_QG:C_KG]
        dk_ref = out_ref.at[:, C_KG:C_VG]
        dv_ref = out_ref.at[:, C_VG:C_ZG]
        dzg_ref = out_ref.at[:, C_ZG:C_GLR]
        dglr_ref = out_ref.at[:, C_GLR:C_GLR + GLR_W]
        first = pl.program_id(0) == 0

        @pl.when(first)
        def _():
            dstate[...] = jnp.zeros_like(dstate)

        logit, cum = _gla_gates(glr_ref, w2_ref, b_ref)
        is_last = lax.broadcasted_iota(jnp.int32, (GLA_C, 1), 0) == GLA_C - 1
        dgn = jnp.zeros((1, GDV), F32)
        for h in range(GH):
            _, last, e_pos, e_neg, e_end, qt, kt, kh = _gla_head(cum, q_ref, k_ref, h)
            cols = slice(h * GDK, (h + 1) * GDK)
            vcols = slice(h * GDV, (h + 1) * GDV)
            o = o_ref[:, vcols]
            r = lax.rsqrt(jnp.mean(o * o, axis=-1, keepdims=True) + EPS)
            zg = zg_ref[:, vcols]
            sg = _sigmoid(zg)
            db_ = dbin_ref[:, vcols].astype(F32)
            dlin = db_ * zg * sg
            dzg_ref[:, vcols] = (db_ * (o * r * gn_ref[...]) * sg * (1.0 + zg * (1.0 - sg))).astype(BF16)
            u = dlin * gn_ref[...]
            do = (r * u - o * (r * r * r) * jnp.mean(u * o, axis=-1, keepdims=True)).astype(BF16)
            dgn = dgn + jnp.sum(dlin * o * r, axis=0, keepdims=True)
            st = st_ref[0, h]
            dst = dstate[h]
            v = v_ref[:, vcols].astype(BF16)
            qb, kb, khb = qt.astype(BF16), kt.astype(BF16), kh.astype(BF16)
            dstb = dst.astype(BF16)
            causal = _causal(GLA_C)
            a = _chunk_scores(qt, kt, q_ref, k_ref, h).astype(BF16)
            da = jnp.where(causal, _nt(do, v), 0.0).astype(BF16)
            dqt = _nn(do, st.astype(BF16)) + _nn(da, kb)
            dkt = _tn(da, qb)
            dkh = _nn(v, dstb)
            dv_ref[:, vcols] = (_tn(a, do) + _nt(khb, dstb)).astype(BF16)
            lam = jnp.exp(last)
            dlam = jnp.sum(dst * st, axis=0, keepdims=True)
            dstate[h] = dst * lam + _tn(do, qb)
            dq_ref[:, cols] = (dqt * e_pos * (GDK ** -0.5)).astype(BF16)
            dk_ref[:, cols] = (dkt * e_neg + dkh * e_end).astype(BF16)
            dkh_kh = dkh * kh
            dcum = dqt * qt - dkt * kt - dkh_kh
            dlast = jnp.sum(dkh_kh, axis=0, keepdims=True) + dlam * lam
            dcum = jnp.where(is_last, dcum + dlast, dcum)
            dlg = _tri_sum(dcum, upper=True)
            dlogit[:, cols] = dlg * (1.0 / GLA_TAU) * (1.0 - _sigmoid(logit[:, cols]))

        dl = dlogit[...]
        dlb = dl.astype(BF16)
        dglr_ref[...] = _nt(dlb, w2_ref[...]).astype(BF16)
        dw2 = _tn(glr_ref[...].astype(BF16), dlb)
        dbg = jnp.sum(dl, axis=0, keepdims=True)

        @pl.when(first)
        def _():
            dw2_ref[...] = dw2
            dbg_ref[...] = dbg
            dgn_ref[...] = dgn

        @pl.when(jnp.logical_not(first))
        def _():
            dw2_ref[...] += dw2
            dbg_ref[...] += dbg
            dgn_ref[...] += dgn

    rev = lambda i: nc - 1 - i
    row = lambda width, cblk: pl.BlockSpec((GLA_C, width), functools.partial(lambda i, c: (rev(i), c), c=cblk))
    full = lambda a: pl.BlockSpec(a.shape, functools.partial(lambda i, nd: (0,) * nd, nd=a.ndim))
    keep = lambda shape: pl.BlockSpec(shape, functools.partial(lambda i, nd: (0,) * nd, nd=len(shape)))
    return pl.pallas_call(
        body, name="gla_bwd", grid=(nc,),
        in_specs=[row(512, C_QG // 512), row(512, C_KG // 512), row(1024, C_VG // 1024), row(GLR_W, C_GLR // GLR_W),
                  row(1024, C_ZG // 1024), full(w2p), full(bg), full(gn), row(GH * GDV, 0),
                  pl.BlockSpec((1, GH, GDV, GDK), lambda i: (rev(i), 0, 0, 0)), row(GH * GDV, 0),
                  pl.BlockSpec(memory_space=pl.ANY)],
        out_specs=[row(GLA_GROUP_W, 0), keep((GLR_W, 512)), keep((1, 512)), keep((1, GDV))],
        out_shape=[S(dproj.shape, dproj.dtype), S((GLR_W, 512), F32), S((1, 512), F32), S((1, GDV), F32)],
        input_output_aliases={11: 0},
        scratch_shapes=[pltpu.VMEM((GH, GDV, GDK), F32), pltpu.VMEM((GLA_C, GH * GDK), F32)],
        compiler_params=pltpu.CompilerParams(dimension_semantics=("arbitrary",)),
    )(proj, proj, proj, proj, proj, w2p, bg, gn, o_gla, states, dbin, dproj)


RT = 512


def _rowchain(body, name, ins, outs, scratch=()):
    in_specs, args = [], []
    for spec in ins:
        if spec[0] == "tok":
            _, arr, width, cblk = spec
            in_specs.append(pl.BlockSpec((RT, width), functools.partial(lambda i, c: (i, c), c=cblk)))
        else:
            arr = spec[1]
            in_specs.append(pl.BlockSpec(arr.shape, functools.partial(lambda i, nd: (0,) * nd, nd=arr.ndim)))
        args.append(arr)
    out_specs, out_shape = [], []
    for spec in outs:
        if spec[0] == "tok":
            _, shape, dtype, width, cblk = spec
            out_specs.append(pl.BlockSpec((RT, width), functools.partial(lambda i, c: (i, c), c=cblk)))
        else:
            _, shape, dtype = spec
            out_specs.append(pl.BlockSpec(shape, functools.partial(lambda i, nd: (0,) * nd, nd=len(shape))))
        out_shape.append(S(shape, dtype))
    return pl.pallas_call(
        body, name=name, grid=(T // RT,), in_specs=in_specs, out_specs=out_specs, out_shape=out_shape,
        scratch_shapes=list(scratch), compiler_params=pltpu.CompilerParams(dimension_semantics=("arbitrary",)),
    )(*args)


def _tok(arr, width=None, cblk=0):
    return ("tok", arr, arr.shape[1] if width is None else width, cblk)


def _tok_out(dtype, width=D):
    return ("tok", (T, width), dtype, width, 0)


def _branches_fwd(ain, bin_, proj, x, w_att, w_gla, w_out):
    def body(ain_ref, bin_ref, g_ref, x_ref, wa_ref, wg_ref, wo_ref, ya_ref, yb_ref, y_ref, x1_ref):
        ya = _nn(ain_ref[...], wa_ref[...]).astype(BF16)
        yb = _nn(bin_ref[...], wg_ref[...]).astype(BF16)
        ya_ref[...] = ya
        yb_ref[...] = yb
        y = (_sigmoid(g_ref[:, :D]) * ya.astype(F32) + _sigmoid(g_ref[:, D:]) * yb.astype(F32)).astype(BF16)
        y_ref[...] = y
        x1_ref[...] = x_ref[...] + _nn(y, wo_ref[...])

    return _rowchain(body, "branches_fwd",
                     [_tok(ain), _tok(bin_), _tok(proj, 2 * D, C_GA // (2 * D)), _tok(x), ("all", w_att),
                      ("all", w_gla), ("all", w_out)],
                     [_tok_out(BF16), _tok_out(BF16), _tok_out(BF16), _tok_out(F32)])


def _accumulate(ref, part, first):
    @pl.when(first)
    def _():
        ref[...] = part

    @pl.when(jnp.logical_not(first))
    def _():
        ref[...] += part


def _ple_loss(x1, p, target, g2, w_pg, w_ple):
    def body(x1_ref, p_ref, t_ref, g_ref, wpg_ref, wple_ref, n2_ref, loss_ref, dout_ref, du_ref, dwple_ref, acc):
        first = pl.program_id(0) == 0
        x1 = x1_ref[...]
        r = lax.rsqrt(jnp.mean(x1 * x1, axis=-1, keepdims=True) + EPS)
        n2 = (x1 * r * g_ref[...]).astype(BF16)
        n2_ref[...] = n2
        pg = _sigmoid(_nn(n2, wpg_ref[...]))
        pb = p_ref[...].astype(BF16)
        e_ = _nn(pb, wple_ref[...])
        diff = x1 + e_ * pg - t_ref[...]
        _accumulate(acc, jnp.sum(diff * diff, axis=0, keepdims=True), first)
        dout = diff * (1.0 / D)
        dout_ref[...] = dout
        du_ref[...] = (dout * e_ * pg * (1.0 - pg)).astype(BF16)
        _accumulate(dwple_ref, _tn(pb, (dout * pg).astype(BF16)), first)
        loss_ref[...] = jnp.zeros((1, 128), F32) + jnp.sum(acc[...], axis=-1, keepdims=True) * (0.5 / D)

    return _rowchain(body, "ple_loss", [_tok(x1), _tok(p), _tok(target), ("all", g2), ("all", w_pg), ("all", w_ple)],
                     [_tok_out(BF16), ("acc", (1, 128), F32), _tok_out(F32), _tok_out(BF16), ("acc", (PLE, D), F32)],
                     scratch=[pltpu.VMEM((1, D), F32)])


def _ple_bwd(du, n2, y, x1, dout, g2, w_pg, w_out):
    def body(du_ref, n2_ref, y_ref, x1_ref, dout_ref, g_ref, wpg_ref, wo_ref, dx_ref, dy_ref, dg_ref, dwpg_ref,
             dwo_ref):
        first = pl.program_id(0) == 0
        x1 = x1_ref[...]
        r = lax.rsqrt(jnp.mean(x1 * x1, axis=-1, keepdims=True) + EPS)
        du_ = du_ref[...]
        dn = _nt(du_, wpg_ref[...])
        u = dn * g_ref[...]
        dx = dout_ref[...] + r * u - x1 * (r * r * r) * jnp.mean(u * x1, axis=-1, keepdims=True)
        dxb = dx.astype(BF16)
        dx_ref[...] = dx
        dy_ref[...] = _nt(dxb, wo_ref[...]).astype(BF16)
        _accumulate(dg_ref, jnp.sum(dn * x1 * r, axis=0, keepdims=True), first)
        _accumulate(dwpg_ref, _tn(n2_ref[...], du_), first)
        _accumulate(dwo_ref, _tn(y_ref[...], dxb), first)

    return _rowchain(body, "ple_bwd",
                     [_tok(du), _tok(n2), _tok(y), _tok(x1), _tok(dout), ("all", g2), ("all", w_pg), ("all", w_out)],
                     [_tok_out(F32), _tok_out(BF16), ("acc", (1, D), F32), ("acc", (D, D), F32), ("acc", (D, D), F32)])


def _branches_bwd(dy, ya, yb, ain, bin_, proj, w_att, w_gla):
    def body(dy_ref, ya_ref, yb_ref, ain_ref, bin_ref, g_ref, wa_ref, wg_ref, dg_ref, dain_ref, dbin_ref,
             dwa_ref, dwg_ref):
        first = pl.program_id(0) == 0
        dy_ = dy_ref[...].astype(F32)
        sa, sb = _sigmoid(g_ref[:, :D]), _sigmoid(g_ref[:, D:])
        dg_ref[:, :D] = (dy_ * ya_ref[...].astype(F32) * sa * (1.0 - sa)).astype(BF16)
        dg_ref[:, D:] = (dy_ * yb_ref[...].astype(F32) * sb * (1.0 - sb)).astype(BF16)
        dya = (dy_ * sa).astype(BF16)
        dyb = (dy_ * sb).astype(BF16)
        dain_ref[...] = _nt(dya, wa_ref[...]).astype(BF16)
        dbin_ref[...] = _nt(dyb, wg_ref[...]).astype(BF16)
        _accumulate(dwa_ref, _tn(ain_ref[...], dya), first)
        _accumulate(dwg_ref, _tn(bin_ref[...], dyb), first)

    gates = C_GA // (2 * D)
    return _rowchain(body, "branches_bwd",
                     [_tok(dy), _tok(ya), _tok(yb), _tok(ain), _tok(bin_), _tok(proj, 2 * D, gates), ("all", w_att),
                      ("all", w_gla)],
                     [("tok", (T, NCOL), BF16, 2 * D, gates), _tok_out(BF16, ATT_W), _tok_out(BF16),
                      ("acc", (ATT_W, D), F32), ("acc", (D, D), F32)])


def _peer(k):
    x, y, c = lax.axis_index("x"), lax.axis_index("y"), lax.axis_index("c")
    return (x ^ ((k >> 2) & 1), y ^ ((k >> 1) & 1), c ^ (k & 1))


def _my_index():
    return 4 * lax.axis_index("x") + 2 * lax.axis_index("y") + lax.axis_index("c")


def _peer_index(k):
    px, py, pc = _peer(k)
    return 4 * px + 2 * py + pc


def _pairwise_plan(src_of, dst_of, landed_of, own_src, own_dst):
    def plan(ins, outs, send, recv, local):
        n = len(ins)

        def own():
            return [pltpu.make_async_copy(own_src(ins[a]), own_dst(outs[a]), local.at[a]) for a in range(n)]

        def remote(k, a, src, dst):
            return pltpu.make_async_remote_copy(src_ref=src, dst_ref=dst, send_sem=send.at[k - 1, a],
                                                recv_sem=recv.at[k - 1, a], device_id=_peer(k), device_id_type=MESH)

        def sent():
            return [remote(k, a, src_of(ins[a], k), dst_of(outs[a])) for k in range(1, NDEV) for a in range(n)]

        def start():
            for cp in own() + sent():
                cp.start()

        def finish():
            for k in range(1, NDEV):
                for a in range(n):
                    remote(k, a, own_src(ins[a]), landed_of(outs[a], k)).wait_recv()
            for cp in sent():
                cp.wait_send()
            for cp in own():
                cp.wait()

        return start, finish

    return plan


def _pairwise_sems(n):
    return [pltpu.SemaphoreType.DMA((NDEV - 1, n)), pltpu.SemaphoreType.DMA((NDEV - 1, n)),
            pltpu.SemaphoreType.DMA((n,))]


def _gather_side(arrs):
    plan = _pairwise_plan(src_of=lambda i, k: i, dst_of=lambda o: o.at[_my_index()],
                          landed_of=lambda o, k: o.at[_peer_index(k)],
                          own_src=lambda i: i, own_dst=lambda o: o.at[_my_index()])
    return dict(arrs=arrs, out_shape=[S((NDEV,) + a.shape, a.dtype) for a in arrs],
                scratch=_pairwise_sems(len(arrs)), plan=plan)


def _exchange_side(arrs):
    plan = _pairwise_plan(src_of=lambda i, k: i.at[_peer_index(k)], dst_of=lambda o: o.at[_my_index()],
                          landed_of=lambda o, k: o.at[_peer_index(k)],
                          own_src=lambda i: i.at[_my_index()], own_dst=lambda o: o.at[_my_index()])
    return dict(arrs=arrs, out_shape=[S(a.shape, a.dtype) for a in arrs], scratch=_pairwise_sems(len(arrs)), plan=plan)


def _comm_call(side, name):
    n = len(side["arrs"])

    def body(*refs):
        start, finish = side["plan"](refs[:n], refs[n:2 * n], *refs[2 * n:])
        start()
        finish()

    hbm = pl.BlockSpec(memory_space=pl.ANY)
    return pl.pallas_call(body, name=name, in_specs=[hbm] * n, out_specs=[hbm] * n, out_shape=side["out_shape"],
                          scratch_shapes=side["scratch"])(*side["arrs"])


def _all_gather_by_chip(arrs, name):
    n = len(arrs)

    def body(*refs):
        ins, outs = refs[:n], refs[n:2 * n]
        send, recv, local = refs[2 * n:]
        x, y, c = lax.axis_index("x"), lax.axis_index("y"), lax.axis_index("c")
        me, sibling = (x, y, c), (x, y, 1 - c)
        chips = [(1 - x, y), (x, 1 - y), (1 - x, 1 - y)]

        def copy(k, a, block, to, src=None):
            px, py, pc = block
            slot = outs[a].at[4 * px + 2 * py + pc]
            return pltpu.make_async_remote_copy(
                src_ref=slot if src is None else src, dst_ref=slot, send_sem=send.at[k, a], recv_sem=recv.at[k, a],
                device_id=to, device_id_type=MESH)

        north = c == 1
        via = (jnp.where(north, 1 - x, x), jnp.where(north, y, 1 - y))
        onward = (jnp.where(north, x, 1 - x), jnp.where(north, 1 - y, y), c)
        mine = [pltpu.make_async_copy(ins[a], outs[a].at[4 * x + 2 * y + c], local.at[a]) for a in range(n)]
        first = []
        for a in range(n):
            first.append(copy(0, a, me, sibling, src=ins[a]))
            first += [copy(1 + j, a, me, (*chips[j], c), src=ins[a]) for j in range(2)]
        for cp in mine + first:
            cp.start()
        passed = []
        for j in range(2):
            for a in range(n):
                copy(1 + j, a, (*chips[j], c), me).wait_recv()
                passed.append(copy(4 + j, a, (*chips[j], c), sibling))
                passed[-1].start()
        for a in range(n):
            passed.append(copy(3, a, (*via, c), onward))
            passed[-1].start()
        for a in range(n):
            copy(3, a, (*chips[2], c), me).wait_recv()
            passed.append(copy(6, a, (*chips[2], c), sibling))
            passed[-1].start()
        for a in range(n):
            copy(0, a, sibling, me).wait_recv()
        for j, chip in enumerate(chips):
            for a in range(n):
                copy(4 + j, a, (*chip, 1 - c), me).wait_recv()
        for cp in first + passed:
            cp.wait_send()
        for cp in mine:
            cp.wait()

    hbm = pl.BlockSpec(memory_space=pl.ANY)
    return pl.pallas_call(
        body, name=name, in_specs=[hbm] * n, out_specs=[hbm] * n,
        out_shape=[S((NDEV,) + a.shape, a.dtype) for a in arrs],
        scratch_shapes=[pltpu.SemaphoreType.DMA((NDEV - 1, n)), pltpu.SemaphoreType.DMA((NDEV - 1, n)),
                        pltpu.SemaphoreType.DMA((n,))],
    )(*arrs)


NCHIP = 4


def _sibling_sum(src, name, tc=256):
    _, rows, cols = src.shape
    assert cols % tc == 0

    def body(src_ref, got_ref, out_ref, a_buf, b_buf, o_buf, send, recv, local):
        x, y, c = lax.axis_index("x"), lax.axis_index("y"), lax.axis_index("c")
        copies = [pltpu.make_async_remote_copy(
            src_ref=src_ref.at[2 * q + (1 - c)], dst_ref=got_ref.at[q], send_sem=send.at[q], recv_sem=recv.at[q],
            device_id=(x, y, 1 - c), device_id_type=MESH) for q in range(NCHIP)]
        for cp in copies:
            cp.start()
        tiles = [(q, pl.ds(t * tc, tc)) for q in range(NCHIP) for t in range(cols // tc)]

        def loads(n):
            q, tile = tiles[n]
            return [pltpu.make_async_copy(src_ref.at[2 * q + c, :, tile], a_buf.at[n % 2], local.at[n % 2, 0]),
                    pltpu.make_async_copy(got_ref.at[q, :, tile], b_buf.at[n % 2], local.at[n % 2, 1])]

        def store(n):
            q, tile = tiles[n]
            return pltpu.make_async_copy(o_buf.at[n % 2], out_ref.at[q, :, tile], local.at[n % 2, 2])

        def fetch(n):
            if n == 0 or tiles[n][0] != tiles[n - 1][0]:
                copies[tiles[n][0]].wait_recv()
            for cp in loads(n):
                cp.start()

        fetch(0)
        for n in range(len(tiles)):
            if n + 1 < len(tiles):
                fetch(n + 1)
            for cp in loads(n):
                cp.wait()
            if n >= 2:
                store(n - 2).wait()
            o_buf[n % 2] = (a_buf[n % 2].astype(F32) + b_buf[n % 2].astype(F32)).astype(BF16)
            store(n).start()
        store(len(tiles) - 2).wait()
        store(len(tiles) - 1).wait()
        for cp in copies:
            cp.wait_send()

    hbm = pl.BlockSpec(memory_space=pl.ANY)
    block = S((NCHIP, rows, cols), BF16)
    return pl.pallas_call(
        body, name=name, in_specs=[hbm], out_specs=[hbm, hbm], out_shape=[block, block],
        scratch_shapes=[pltpu.VMEM((2, rows, tc), BF16)] * 3
        + [pltpu.SemaphoreType.DMA((NCHIP,)), pltpu.SemaphoreType.DMA((NCHIP,)), pltpu.SemaphoreType.DMA((2, 3))],
    )(src)[1]


def _chips_side(arrs):
    def plan(ins, outs, send, recv, local):
        n = len(ins)

        def places():
            x, y, c = lax.axis_index("x"), lax.axis_index("y"), lax.axis_index("c")
            return 2 * x + y, c, [(1 - x, y), (x, 1 - y), (1 - x, 1 - y)]

        def own():
            here, _, _ = places()
            return [pltpu.make_async_copy(ins[a].at[here], outs[a].at[here], local.at[a]) for a in range(n)]

        def remote(j, a, src_slot, dst_slot):
            _, c, chips = places()
            cx, cy = chips[j]
            return pltpu.make_async_remote_copy(
                src_ref=ins[a].at[src_slot], dst_ref=outs[a].at[dst_slot], send_sem=send.at[j, a],
                recv_sem=recv.at[j, a], device_id=(cx, cy, c), device_id_type=MESH)

        def sent():
            here, _, chips = places()
            return [remote(j, a, 2 * cx + cy, here) for j, (cx, cy) in enumerate(chips) for a in range(n)]

        def start():
            for cp in own() + sent():
                cp.start()

        def finish():
            here, _, chips = places()
            for j, (cx, cy) in enumerate(chips):
                for a in range(n):
                    remote(j, a, here, 2 * cx + cy).wait_recv()
            for cp in sent():
                cp.wait_send()
            for cp in own():
                cp.wait()

        return start, finish

    n = len(arrs)
    return dict(arrs=arrs, out_shape=[S(a.shape, a.dtype) for a in arrs],
                scratch=[pltpu.SemaphoreType.DMA((NCHIP - 1, n)), pltpu.SemaphoreType.DMA((NCHIP - 1, n)),
                         pltpu.SemaphoreType.DMA((n,))], plan=plan)


def _adamw_shards(parts, places):
    n_src = len(parts)

    def body(*refs):
        srcs, rest = refs[:n_src], refs[n_src:]
        for j, (src, rows, cols, _) in enumerate(places):
            w_ref, m_ref, v_ref = rest[3 * j:3 * j + 3]
            outs = rest[3 * len(places) + 4 * j:3 * len(places) + 4 * j + 4]
            p_ref = srcs[src]
            g = p_ref[0, rows, cols].astype(F32)
            for s in range(1, p_ref.shape[0]):
                g = g + p_ref[s, rows, cols].astype(F32)
            delta, m_new, v_new = _adam_math(g, w_ref[0], m_ref[0], v_ref[0])
            for ref, val in zip(outs, (g, delta, m_new, v_new)):
                ref[0] = val

    flat = [a for place in places for a in place[3]]
    return pl.pallas_call(
        body, name="adam_shards",
        out_shape=[S(place[3][0].shape, F32) for place in places for _ in range(4)],
    )(*parts, *flat)


def _adam_math(g, w, m, v):
    c1 = 1.0 - ADAM_B1 ** ADAM_STEP
    c2 = 1.0 - ADAM_B2 ** ADAM_STEP
    m_new = ADAM_B1 * m + (1.0 - ADAM_B1) * g
    v_new = ADAM_B2 * v + (1.0 - ADAM_B2) * (g * g)
    return -ADAM_LR * ((m_new / c1) / (jnp.sqrt(v_new / c2) + ADAM_EPS) + ADAM_WD * w), m_new, v_new


def _adamw_small(parts, params, loss_parts):
    n = len(params)

    def body(*refs):
        p_refs, rest = refs[:n], refs[n + 1:]
        total = refs[n][0]
        for s in range(1, NDEV):
            total = total + refs[n][s]
        refs[-1][...] = total
        for j in range(n):
            w_ref, m_ref, v_ref = rest[3 * j:3 * j + 3]
            g_ref, d_ref, mo_ref, vo_ref = rest[3 * n + 4 * j:3 * n + 4 * j + 4]
            width = w_ref.shape[1]
            g = p_refs[j][0]
            for s in range(1, NDEV):
                g = g + p_refs[j][s]
            g = g[:, :width]
            delta, m_new, v_new = _adam_math(g, w_ref[...], m_ref[...], v_ref[...])
            g_ref[...] = g
            d_ref[...] = delta
            mo_ref[...] = m_new
            vo_ref[...] = v_new

    flat = [a for group in params for a in group]
    return pl.pallas_call(
        body, name="adam_small",
        out_shape=[S(group[0].shape, F32) for group in params for _ in range(4)] + [S((1, 128), F32)],
    )(*parts, loss_parts, *flat)


def _adamw_rows(parts, w, m, v, name, tc=128):
    rows, _, cols = w.shape
    nparts = parts.shape[0]
    nsteps = cols // tc

    def body(p_ref, w_hbm, m_hbm, v_hbm, g_hbm, d_hbm, mo_hbm, vo_hbm, inbuf, outbuf, insem, outsem):
        i = pl.program_id(0)
        slot = i & 1

        def view(ref, step):
            return ref.at[:, 0, pl.ds(pl.multiple_of(step * tc, tc), tc)]

        def fetch(step, sl):
            return [pltpu.make_async_copy(view(src, step), inbuf.at[sl, k], insem.at[sl, k])
                    for k, src in enumerate((w_hbm, m_hbm, v_hbm))]

        def write(step, sl):
            return [pltpu.make_async_copy(outbuf.at[sl, k], view(dst, step), outsem.at[sl, k])
                    for k, dst in enumerate((g_hbm, d_hbm, mo_hbm, vo_hbm))]

        @pl.when(i == 0)
        def _():
            for cp in fetch(0, 0):
                cp.start()

        @pl.when(i + 1 < nsteps)
        def _():
            for cp in fetch(i + 1, 1 - slot):
                cp.start()

        for cp in fetch(i, slot):
            cp.wait()

        @pl.when(i >= 2)
        def _():
            for cp in write(i - 2, slot):
                cp.wait()

        g = p_ref[0].astype(F32)
        for s in range(1, nparts):
            g = g + p_ref[s].astype(F32)
        g = g[:rows]
        delta, m_new, v_new = _adam_math(g, inbuf[slot, 0], inbuf[slot, 1], inbuf[slot, 2])
        for k, val in enumerate((g, delta, m_new, v_new)):
            outbuf[slot, k] = val
        for cp in write(i, slot):
            cp.start()

        @pl.when(i == nsteps - 1)
        def _():
            for cp in write(i - 1, 1 - slot) + write(i, slot):
                cp.wait()

    hbm = pl.BlockSpec(memory_space=pl.ANY)
    assert nsteps >= 2
    return pl.pallas_call(
        body, name=name, grid=(nsteps,),
        in_specs=[pl.BlockSpec((nparts, parts.shape[1], tc), lambda i: (0, 0, i)), hbm, hbm, hbm],
        out_specs=[hbm] * 4, out_shape=[S((rows, 1, cols), F32)] * 4,
        scratch_shapes=[pltpu.VMEM((2, 3, rows, tc), F32), pltpu.VMEM((2, 4, rows, tc), F32),
                        pltpu.SemaphoreType.DMA((2, 3)), pltpu.SemaphoreType.DMA((2, 4))],
        compiler_params=pltpu.CompilerParams(dimension_semantics=("arbitrary",)),
    )(parts, w, m, v)


SLAB = 1296
REMAP_RUNS = 3
_PIECES = ((O_QA, O_ZA, C_QA), (O_ZA, O_QG, C_ZA), (O_QG, O_GLR, C_QG), (O_GLR, O_ZG, C_GLR), (O_ZG, O_GA, C_ZG),
           (O_GA, O_END, C_GA))


def _slab_row_of_aligned(a):
    for o0, o1, a0 in _PIECES:
        if a0 <= a < a0 + o1 - o0:
            c = o0 + a - a0
            return SLAB * (c // W_IN_SHARD) + c % W_IN_SHARD
    return -1


def _aligned_row_of_slab(r):
    d, l = divmod(r, SLAB)
    if l >= W_IN_SHARD:
        return -1
    c = d * W_IN_SHARD + l
    for o0, o1, a0 in _PIECES:
        if o0 <= c < o1:
            return a0 + c - o0
    raise AssertionError(c)


def _remap_table(row_of, n_out, block, n_src):
    win = block + 16
    table = []
    for b in range(n_out // block):
        runs = []
        for i in range(block):
            s = row_of(b * block + i)
            if s < 0:
                continue
            if runs and runs[-1][0] + runs[-1][2] == s and runs[-1][1] + runs[-1][2] == i:
                runs[-1][2] += 1
            else:
                runs.append([s, i, 1])
        assert len(runs) <= REMAP_RUNS, (b, runs)
        row = []
        for s, i, n in runs:
            w = min(s // 16 * 16, n_src - win)
            assert 0 <= s - w and s - w + n <= win
            row += [w, s - w, i, n]
        table.append(row + [0] * (4 * REMAP_RUNS - len(row)))
    return table


def _remap_rows(src, row_of, n_out, block, name):
    n_src, cols = src.shape
    nb, win = n_out // block, block + 16
    table = _remap_table(row_of, n_out, block, n_src)
    runs = [[tuple(row[4 * k:4 * k + 4]) for k in range(REMAP_RUNS) if row[4 * k + 3] > 0] for row in table]

    def body(src_hbm, out_hbm, wbuf, obuf, insem, outsem):
        def fetches(b):
            return [pltpu.make_async_copy(src_hbm.at[pl.ds(w, win)], wbuf.at[b % 2, k], insem.at[b % 2, k])
                    for k, (w, _, _, _) in enumerate(runs[b])]

        def store(b):
            return pltpu.make_async_copy(obuf.at[b % 2], out_hbm.at[pl.ds(b * block, block)], outsem.at[b % 2])

        for cp in fetches(0):
            cp.start()
        for b in range(nb):
            if b + 1 < nb:
                for cp in fetches(b + 1):
                    cp.start()
            for cp in fetches(b):
                cp.wait()
            if b >= 2:
                store(b - 2).wait()
            if sum(count for _, _, _, count in runs[b]) < block:
                obuf[b % 2] = jnp.zeros((block, cols), src.dtype)
            for k, (_, shift, first, count) in enumerate(runs[b]):
                obuf[b % 2, first:first + count, :] = wbuf[b % 2, k, shift:shift + count, :]
            store(b).start()
        store(nb - 2).wait()
        store(nb - 1).wait()

    hbm = pl.BlockSpec(memory_space=pl.ANY)
    return pl.pallas_call(
        body, name=name, in_specs=[hbm], out_specs=hbm, out_shape=S((n_out, cols), src.dtype),
        scratch_shapes=[pltpu.VMEM((2, REMAP_RUNS, win, cols), src.dtype), pltpu.VMEM((2, block, cols), src.dtype),
                        pltpu.SemaphoreType.DMA((2, REMAP_RUNS)), pltpu.SemaphoreType.DMA((2,))],
    )(src)


def _col_blocks(w, width):
    return w.reshape(w.shape[0], NDEV, width).transpose(1, 0, 2)


def _from_col_blocks(w):
    return w.transpose(1, 0, 2).reshape(w.shape[1], NDEV * w.shape[2])


def _local_step(x2, p2, pos, tgt, norm_g, qk_norm_q, qk_norm_k, gla_gate_b, gla_norm_g, ple_norm_g, w_al,
                weights=None, proj_side=None, unpack=None, dw_side_of=None, dh_side_of=None):
    half = ROT_DIM // 2
    inv8 = jnp.power(jnp.float32(ROPE_THETA), -jnp.arange(half, dtype=F32) * 2.0 / ROT_DIM)
    inv = jnp.tile(jnp.concatenate([inv8, inv8, jnp.zeros((HD - ROT_DIM,), F32)]), 2).reshape(1, 128)
    gq = jnp.tile(qk_norm_q, (1, 2))
    gk = jnp.tile(qk_norm_k, (1, 2))

    proj, h, got = _proj_rms(x2, norm_g, w_al, proj_side)
    if proj_side is not None:
        weights = unpack(got)
    w2p, w_att_f, w_gla_f, w_out_f, w_pg_f, w_ple_f = weights
    qkv = _qk_prep(proj, pos, inv, gq, gk)
    fwd = [_att_fwd(qkv[g], qkv[3 + g], qkv[6 + g], g, f"att_fwd{g}") for g in range(3)]
    att, lse, ain = _att_merge([f[0] for f in fwd], [f[1] for f in fwd], proj)
    o_gla, bin_, states = _gla_fwd(proj, w2p, gla_gate_b, gla_norm_g)
    ya, yb, y, x1 = _branches_fwd(ain, bin_, proj, x2, w_att_f, w_gla_f, w_out_f)
    n2, loss_v, dout, du, dw_ple = _ple_loss(x1, p2, tgt, ple_norm_g, w_pg_f, w_ple_f)

    dx1, dy, dg_ple, dw_pg, dw_out = _ple_bwd(du, n2, y, x1, dout, ple_norm_g, w_pg_f, w_out_f)
    dproj, dain, dbin, dw_att, dw_gla = _branches_bwd(dy, ya, yb, ain, bin_, proj, w_att_f, w_gla_f)
    dproj, da0, da1, da2, at1, at2, ls1, ls2 = _att_gate_bwd(dain, att, lse, proj, dproj)
    datts, atts, lses = (da0, da1, da2), (att[None], at1, at2), (lse[None], ls1, ls2)
    dproj, dw2, dbg, dgn = _gla_bwd(proj, w2p, gla_gate_b, gla_norm_g, o_gla, states, dbin, dproj)
    bwd = [_att_bwd(qkv[g], qkv[3 + g], qkv[6 + g], datts[g], atts[g], lses[g], g, f"att_bwd{g}") for g in range(3)]
    dproj, dgq, dgk = _qk_bwd(proj, pos, inv, gq, gk, [b[0] for b in bwd], [b[1] for b in bwd],
                              [b[2] for b in bwd], dproj)
    out = dict(loss=loss_v, dw2=dw2, dw_att=dw_att, dw_gla=dw_gla, dw_out=dw_out, dw_pg=dw_pg, dw_ple=dw_ple,
               dgq=dgq, dgk=dgk, dbg=dbg, dgn=dgn, dg_ple=dg_ple)
    if dw_side_of is None:
        dw_al = _mm(dproj, h, mode="tn", name="dw_in", tm=1536, tn=D, tk=T, out_dtype=BF16)
    else:
        dw_al, out["dw_side"] = _mm(dproj, h, mode="tn", name="dw_in", tm=1536, tn=D, tk=T, out_dtype=BF16,
                                    side=dw_side_of(out))
    grad_x, dg_norm, out["dh_side"] = _dh_rms(dproj, w_al, x2, norm_g, dx1,
                                              None if dh_side_of is None else dh_side_of(dw_al))
    out.update(grad_x=grad_x, dw_al=dw_al, dg_norm=dg_norm)
    return out


def kernel(x, p, positions, norm_g, w_in, qk_norm_q, qk_norm_k, gla_gate_w2, gla_gate_b, gla_norm_g, w_att_proj, w_gla_proj, w_out, ple_norm_g, w_ple_gate, w_ple, loss_target, m_norm_g, m_w_in, m_qk_norm_q, m_qk_norm_k, m_gla_gate_w2, m_gla_gate_b, m_gla_norm_g, m_w_att_proj, m_w_gla_proj, m_w_out, m_ple_norm_g, m_w_ple_gate, m_w_ple, v_norm_g, v_w_in, v_qk_norm_q, v_qk_norm_k, v_gla_gate_w2, v_gla_gate_b, v_gla_norm_g, v_w_att_proj, v_w_gla_proj, v_w_out, v_ple_norm_g, v_w_ple_gate, v_w_ple):
    x2, p2, tgt = x[0], p[0, 0], loss_target[0]
    pos = positions.astype(F32).reshape(T, 1)

    rows3 = jnp.stack([w_gla_proj[0], w_out[0], w_ple_gate[0]]).astype(BF16)
    cols3 = jnp.concatenate([w_att_proj[0], w_ple[0], jnp.pad(gla_gate_w2[0], ((0, 0), (0, 64)))], axis=0).astype(BF16)
    mine = jnp.pad(w_in[0].T.astype(BF16), ((0, SLAB - W_IN_SHARD), (0, 0)))
    (g_in,) = _all_gather_by_chip([mine], "gather_w_in")
    w_al = _remap_rows(g_in.reshape(NDEV * SLAB, D), _slab_row_of_aligned, NCOL, 512, "align_w_in")

    def unpack(got):
        g_rows, g_cols = got
        w2_f = _from_col_blocks(g_cols[:, 768:784, :64])
        return (jnp.pad(w2_f, ((0, GLR_W - GLR_N), (0, 0))), _from_col_blocks(g_cols[:, :512]),
                g_rows[:, 0].reshape(D, D), g_rows[:, 1].reshape(D, D), g_rows[:, 2].reshape(D, D),
                _from_col_blocks(g_cols[:, 512:768]))

    def dw_side_of(g):
        s_rows = jnp.concatenate([g[k].reshape(NDEV, 128, D) for k in ("dw_gla", "dw_out", "dw_pg")], axis=1)
        s_cols = jnp.concatenate([_col_blocks(g["dw_att"], 128), _col_blocks(g["dw_ple"], 128),
                                  jnp.pad(_col_blocks(g["dw2"][:GLR_N], 64), ((0, 0), (0, 0), (0, 64)))], axis=1)
        return _exchange_side([s_rows.astype(BF16), s_cols.astype(BF16)])

    def dh_side_of(dw_al):
        s_in = _remap_rows(dw_al, _aligned_row_of_slab, NDEV * SLAB, SLAB, "shard_dw_in").reshape(NDEV, SLAB, D)
        return _chips_side([_sibling_sum(s_in, "sibling_sum")])

    loc = _local_step(x2, p2, pos, tgt, norm_g, qk_norm_q, qk_norm_k, gla_gate_b, gla_norm_g, ple_norm_g, w_al,
                      proj_side=_gather_side([rows3, cols3]), unpack=unpack, dw_side_of=dw_side_of,
                      dh_side_of=dh_side_of)
    loss_v, grad_x = loc["loss"], loc["grad_x"]
    dg_norm, dgq, dgk, dbg, dgn, dg_ple = (loc[k] for k in ("dg_norm", "dgq", "dgk", "dbg", "dgn", "dg_ple"))
    r_rows, r_cols = loc["dw_side"]
    (r_in,) = loc["dh_side"]

    r_small = _comm_call(_gather_side([dg_norm, dgq, dgk, dbg, dgn, dg_ple, loss_v]), "gather_small")

    outs = {}

    rows_of = lambda a: jnp.transpose(a, (2, 0, 1))
    outs["w_in"] = [jnp.transpose(o, (1, 2, 0))[0] for o in
                    _adamw_rows(r_in, rows_of(w_in), rows_of(m_w_in), rows_of(v_w_in), "adam_w_in")]
    places = (("w_gla_proj", 0, slice(0, 128), slice(None), (w_gla_proj, m_w_gla_proj, v_w_gla_proj)),
              ("w_out", 0, slice(128, 256), slice(None), (w_out, m_w_out, v_w_out)),
              ("w_ple_gate", 0, slice(256, 384), slice(None), (w_ple_gate, m_w_ple_gate, v_w_ple_gate)),
              ("w_att_proj", 1, slice(0, 512), slice(None), (w_att_proj, m_w_att_proj, v_w_att_proj)),
              ("w_ple", 1, slice(512, 768), slice(None), (w_ple, m_w_ple, v_w_ple)),
              ("gla_gate_w2", 1, slice(768, 784), slice(0, 64), (gla_gate_w2, m_gla_gate_w2, v_gla_gate_w2)))
    res = _adamw_shards([r_rows, r_cols], [place[1:] for place in places])
    for j, place in enumerate(places):
        outs[place[0]] = [o[0] for o in res[4 * j:4 * j + 4]]
    small = ((norm_g, m_norm_g, v_norm_g), (qk_norm_q, m_qk_norm_q, v_qk_norm_q), (qk_norm_k, m_qk_norm_k, v_qk_norm_k),
             (gla_gate_b, m_gla_gate_b, v_gla_gate_b), (gla_norm_g, m_gla_norm_g, v_gla_norm_g),
             (ple_norm_g, m_ple_norm_g, v_ple_norm_g))
    sm = _adamw_small(r_small[:6], small, r_small[6])
    for j, nm in enumerate(("norm_g", "qk_norm_q", "qk_norm_k", "gla_gate_b", "gla_norm_g", "ple_norm_g")):
        outs[nm] = [o[0] for o in sm[4 * j:4 * j + 4]]

    loss = sm[-1][0, 0]
    order = ["norm_g", "w_in", "qk_norm_q", "qk_norm_k", "gla_gate_w2", "gla_gate_b", "gla_norm_g", "w_att_proj",
             "w_gla_proj", "w_out", "ple_norm_g", "w_ple_gate", "w_ple"]
    result = [loss, grad_x[None]]
    for i in range(4):
        result += [outs[nm][i][None] for nm in order]
    return tuple(result)
```

```python
import functools

import jax
import jax.numpy as jnp
from jax import lax
from jax.experimental import pallas as pl
from jax.experimental.pallas import tpu as pltpu

F32 = jnp.float32
BF16 = jnp.bfloat16
S = jax.ShapeDtypeStruct

T = 4096
D = 1024
NDEV = 8
HD = 64
ATT_W = 512
ATT_QKV = 1536
DILATIONS = (1, 4, 16)
BLK = 128
GH, GDK, GDV = 4, 128, 256
GLA_C = 128
PLE = 256
EPS = 1e-6
ROT_DIM = 16
ROPE_THETA = 500000.0
GLA_TAU = 16.0
W_IN_SHARD = 1282

C_QG, C_KG, C_VG, C_ZG, C_GLR, C_ZA, C_GA, C_GB, C_QA, C_KA, C_VA = (
    0, 512, 1024, 2048, 3072, 3584, 4096, 5120, 6144, 7680, 9216)
GLA_GROUP_W = 3584
GLR_W = 512
NCOL = 10752
GLR_N = 16
O_QA, O_ZA, O_QG, O_GLR, O_ZG, O_GA, O_END = 0, 4608, 5120, 7168, 7184, 8208, 10256

ADAM_LR, ADAM_B1, ADAM_B2, ADAM_EPS, ADAM_WD, ADAM_STEP = 0.001, 0.9, 0.999, 1e-08, 0.01, 10

MESH = pl.DeviceIdType.MESH


def _sigmoid(z):
    return 1.0 / (1.0 + jnp.exp(-z))


def _dot(a, b, dims):
    return lax.dot_general(a, b, (dims, ((), ())), preferred_element_type=F32)


def _nn(a, b):
    return _dot(a, b, ((1,), (0,)))


def _nt(a, b):
    return _dot(a, b, ((1,), (1,)))


def _tn(a, b):
    return _dot(a, b, ((0,), (0,)))


def _mm(a, b, *, mode, name, tm, tn, tk, out_dtype=F32, res=None, side=None):
    if mode == "nn":
        (m, k), n = a.shape, b.shape[1]
        a_spec = pl.BlockSpec((tm, tk), lambda i, j, l: (i, l))
        b_spec = pl.BlockSpec((tk, tn), lambda i, j, l: (l, j))
        dot = _nn
    elif mode == "nt":
        (m, k), n = a.shape, b.shape[0]
        a_spec = pl.BlockSpec((tm, tk), lambda i, j, l: (i, l))
        b_spec = pl.BlockSpec((tn, tk), lambda i, j, l: (j, l))
        dot = _nt
    else:
        (k, m), n = a.shape, b.shape[1]
        a_spec = pl.BlockSpec((tk, tm), lambda i, j, l: (l, i))
        b_spec = pl.BlockSpec((tk, tn), lambda i, j, l: (l, j))
        dot = _tn
    assert m % tm == 0 and n % tn == 0 and k % tk == 0, (name, m, n, k)
    grid = (m // tm, n // tn, k // tk)
    nk = grid[2]
    o_spec = pl.BlockSpec((tm, tn), lambda i, j, l: (i, j))
    in_specs = [a_spec, b_spec]
    args = [a, b]
    if res is not None:
        in_specs.append(o_spec)
        args.append(res)
    n_in = len(args)
    n_side = 0 if side is None else len(side["arrs"])
    hbm = pl.BlockSpec(memory_space=pl.ANY)

    def body(*refs):
        a_ref, b_ref = refs[0], refs[1]
        r_ref = refs[2] if res is not None else None
        o_ref = refs[n_in + n_side]
        scratch = refs[n_in + 2 * n_side + 1:]
        if side is not None:
            start, finish_side = side["plan"](refs[n_in:n_in + n_side], refs[n_in + n_side + 1:n_in + 2 * n_side + 1],
                                              *scratch[1 if nk > 1 else 0:])
            ids = [pl.program_id(d) for d in range(3)]

            @pl.when((ids[0] == 0) & (ids[1] == 0) & (ids[2] == 0))
            def _():
                start()

        part = dot(a_ref[...].astype(BF16), b_ref[...].astype(BF16))

        def finish(val):
            if r_ref is not None:
                val = val + r_ref[...]
            o_ref[...] = val.astype(out_dtype)

        if nk == 1:
            finish(part)
        else:
            acc = scratch[0]
            l = pl.program_id(2)

            @pl.when(l == 0)
            def _():
                acc[...] = part

            @pl.when(l > 0)
            def _():
                acc[...] += part

            @pl.when(l == nk - 1)
            def _():
                finish(acc[...])

        if side is not None:
            @pl.when((ids[0] == grid[0] - 1) & (ids[1] == grid[1] - 1) & (ids[2] == grid[2] - 1))
            def _():
                finish_side()

    sems = [] if side is None else side["scratch"]
    outs = pl.pallas_call(
        body, name=name, grid=grid,
        in_specs=in_specs + [hbm] * n_side, out_specs=[o_spec] + [hbm] * n_side,
        out_shape=[S((m, n), out_dtype)] + ([] if side is None else side["out_shape"]),
        scratch_shapes=([pltpu.VMEM((tm, tn), F32)] if nk > 1 else []) + sems,
        compiler_params=pltpu.CompilerParams(
            dimension_semantics=("arbitrary",) * 3 if side is not None else ("parallel", "parallel", "arbitrary")),
    )(*args, *([] if side is None else side["arrs"]))
    return outs[0] if side is None else (outs[0], outs[1:])


def _side_parts(side, refs, n_in, n_out):
    n_side = 0 if side is None else len(side["arrs"])
    scratch = refs[n_in + n_out + 2 * n_side:]
    if side is None:
        return (lambda: None), (lambda: None), scratch
    start, finish = side["plan"](refs[n_in:n_in + n_side], refs[n_in + n_side + n_out:n_in + n_out + 2 * n_side],
                                 *scratch[len(scratch) - len(side["scratch"]):])
    return start, finish, scratch


def _proj_rms(x, g, wt, side=None):
    tm, tn = 1024, 1536
    grid = (T // tm, NCOL // tn)
    n_side = 0 if side is None else len(side["arrs"])
    hbm = pl.BlockSpec(memory_space=pl.ANY)

    def body(*refs):
        x_ref, g_ref, w_ref = refs[:3]
        o_ref, h_ref = refs[3 + n_side], refs[4 + n_side]
        start, finish, _ = _side_parts(side, refs, 3, 2)
        i, j = pl.program_id(0), pl.program_id(1)

        @pl.when((i == 0) & (j == 0))
        def _():
            start()

        @pl.when(j == 0)
        def _():
            xf = x_ref[...]
            r = lax.rsqrt(jnp.mean(xf * xf, axis=-1, keepdims=True) + EPS)
            h_ref[...] = (xf * r * g_ref[...]).astype(BF16)

        o_ref[...] = _nt(h_ref[...], w_ref[...])

        @pl.when((i == grid[0] - 1) & (j == grid[1] - 1))
        def _():
            finish()

    outs = pl.pallas_call(
        body, name="proj", grid=grid,
        in_specs=[pl.BlockSpec((tm, D), lambda i, j: (i, 0)), pl.BlockSpec((1, D), lambda i, j: (0, 0)),
                  pl.BlockSpec((tn, D), lambda i, j: (j, 0))] + [hbm] * n_side,
        out_specs=[pl.BlockSpec((tm, tn), lambda i, j: (i, j)), pl.BlockSpec((tm, D), lambda i, j: (i, 0))] + [hbm] * n_side,
        out_shape=[S((T, NCOL), F32), S((T, D), BF16)] + ([] if side is None else side["out_shape"]),
        scratch_shapes=[] if side is None else side["scratch"],
        compiler_params=pltpu.CompilerParams(dimension_semantics=("arbitrary", "arbitrary")),
    )(x, g, wt, *([] if side is None else side["arrs"]))
    return outs[0], outs[1], outs[2:]


def _dh_rms(dproj, wt, x, g, skip, side=None):
    tm, tk = 1024, 2688
    grid = (T // tm, NCOL // tk)
    n_side = 0 if side is None else len(side["arrs"])
    hbm = pl.BlockSpec(memory_space=pl.ANY)

    def body(*refs):
        a_ref, w_ref, x_ref, g_ref, s_ref = refs[:5]
        dx_ref, dg_ref = refs[5 + n_side], refs[6 + n_side]
        start, finish, scratch = _side_parts(side, refs, 5, 2)
        acc = scratch[0]
        i, l = pl.program_id(0), pl.program_id(1)

        @pl.when((i == 0) & (l == 0))
        def _():
            start()

        part = _nn(a_ref[...], w_ref[...])

        @pl.when(l == 0)
        def _():
            acc[...] = part

        @pl.when(l > 0)
        def _():
            acc[...] += part

        @pl.when(l == grid[1] - 1)
        def _():
            xf = x_ref[...]
            r = lax.rsqrt(jnp.mean(xf * xf, axis=-1, keepdims=True) + EPS)
            dn = acc[...]
            u = dn * g_ref[...]
            dx_ref[...] = s_ref[...] + r * u - xf * (r * r * r) * jnp.mean(u * xf, axis=-1, keepdims=True)
            dg = jnp.sum(dn * xf * r, axis=0, keepdims=True)

            @pl.when(i == 0)
            def _():
                dg_ref[...] = dg

            @pl.when(i > 0)
            def _():
                dg_ref[...] += dg

        @pl.when((i == grid[0] - 1) & (l == grid[1] - 1))
        def _():
            finish()

    tok = pl.BlockSpec((tm, D), lambda i, l: (i, 0))
    outs = pl.pallas_call(
        body, name="dh", grid=grid,
        in_specs=[pl.BlockSpec((tm, tk), lambda i, l: (i, l)), pl.BlockSpec((tk, D), lambda i, l: (l, 0)), tok,
                  pl.BlockSpec((1, D), lambda i, l: (0, 0)), tok] + [hbm] * n_side,
        out_specs=[tok, pl.BlockSpec((1, D), lambda i, l: (0, 0))] + [hbm] * n_side,
        out_shape=[S((T, D), F32), S((1, D), F32)] + ([] if side is None else side["out_shape"]),
        scratch_shapes=[pltpu.VMEM((tm, D), F32)] + ([] if side is None else side["scratch"]),
        compiler_params=pltpu.CompilerParams(dimension_semantics=("arbitrary", "arbitrary")),
    )(dproj, wt, x, g, skip, *([] if side is None else side["arrs"]))
    return outs[0], outs[1], outs[2:]


def _rot_tables(pos_ref, inv_ref):
    lane = lax.broadcasted_iota(jnp.int32, (1, 128), 1) % HD
    ang = pos_ref[...] * inv_ref[...]
    cos, sin = jnp.cos(ang), jnp.sin(ang)
    c = jnp.where(lane < ROT_DIM, cos, 1.0)
    sp = jnp.where((lane >= ROT_DIM // 2) & (lane < ROT_DIM), sin, 0.0)
    sm = jnp.where(lane < ROT_DIM // 2, -sin, 0.0)
    return c, sp, sm


def _head_sums(v):
    same = (lax.broadcasted_iota(jnp.int32, (128, 128), 0) < HD) == (lax.broadcasted_iota(jnp.int32, (128, 128), 1) < HD)
    ones = jnp.where(same, 1.0, 0.0).astype(BF16)
    hi = v.astype(BF16)
    lo = (v - hi.astype(F32)).astype(BF16)
    return _nn(hi, ones) + _nn(lo, ones)


def _pair_norm(t):
    return lax.rsqrt(_head_sums(t * t) * (1.0 / HD) + EPS)


def _pair_mean(t):
    return _head_sums(t) * (1.0 / HD)


TT = 256
NCH = ATT_QKV // 128


def _res_shape(grp, dtype):
    return S((DILATIONS[grp], T // DILATIONS[grp], ATT_W), dtype)


def _res_spec(grp):
    dil = DILATIONS[grp]
    return pl.BlockSpec((dil, TT // dil, ATT_W), lambda i: (0, i, 0))


def _to_residues(sc, j, dst_ref, dil, cols):
    n = TT // dil
    for r in range(dil):
        rows = sc[j] if dil == 1 else sc.at[j][pl.ds(r, n, stride=dil), :]
        dst_ref[r, :, cols] = rows.astype(dst_ref.dtype)


def _from_residues(src_ref, cols, sc, j, dil):
    n = TT // dil
    for r in range(dil):
        if dil == 1:
            sc[j] = src_ref[r, :, cols]
        else:
            sc.at[j][pl.ds(r, n, stride=dil), :] = src_ref[r, :, cols]


def _tok_spec(width, cblk=0):
    return pl.BlockSpec((TT, width), functools.partial(lambda i, c: (i, c), c=cblk))


def _const_spec(arr_or_shape):
    shape = arr_or_shape if isinstance(arr_or_shape, tuple) else arr_or_shape.shape
    return pl.BlockSpec(shape, functools.partial(lambda i, nd: (0,) * nd, nd=len(shape)))


def _qk_prep(proj, pos, inv, gq, gk):
    def body(q_ref, k_ref, v_ref, pos_ref, inv_ref, gq_ref, gk_ref, *rest):
        outs, sc = rest[:9], rest[9]
        c, sp, sm = _rot_tables(pos_ref, inv_ref)
        for which, (src, g_ref) in enumerate(((q_ref, gq_ref), (k_ref, gk_ref), (v_ref, None))):
            if g_ref is not None:
                g = jnp.broadcast_to(g_ref[...] * ((HD ** -0.5) if which == 0 else 1.0), c.shape)
                cg, spg, smg = c * g, sp * pltpu.roll(g, 8, 1), sm * pltpu.roll(g, 120, 1)
            for j in range(NCH):
                t = src[:, j * 128:(j + 1) * 128]
                if g_ref is not None:
                    t = _pair_norm(t) * (t * cg + pltpu.roll(t, 8, 1) * spg + pltpu.roll(t, 120, 1) * smg)
                sc[j] = t
            for j in range(NCH):
                grp, sub = divmod(j * 128, ATT_W)
                _to_residues(sc, j, outs[which * 3 + grp], DILATIONS[grp], slice(sub, sub + 128))

    return pl.pallas_call(
        body, name="qk_prep", grid=(T // TT,),
        in_specs=[_tok_spec(ATT_QKV, C_QA // ATT_QKV), _tok_spec(ATT_QKV, C_KA // ATT_QKV),
                  _tok_spec(ATT_QKV, C_VA // ATT_QKV), _tok_spec(1), _const_spec(inv), _const_spec(gq), _const_spec(gk)],
        out_specs=[_res_spec(g) for _ in range(3) for g in range(3)],
        out_shape=[_res_shape(g, BF16) for _ in range(3) for g in range(3)],
        scratch_shapes=[pltpu.VMEM((NCH, TT, 128), F32)],
        compiler_params=pltpu.CompilerParams(dimension_semantics=("arbitrary",)),
    )(proj, proj, proj, pos, inv, gq, gk)


def _qk_bwd(proj, pos, inv, gq, gk, dqs, dks, dvs, dproj):
    const = lambda a: pl.BlockSpec(a.shape, functools.partial(lambda i, p, nd: (0,) * nd, nd=a.ndim))
    res = lambda g: pl.BlockSpec((DILATIONS[g], TT // DILATIONS[g], ATT_W), lambda i, p: (0, i, 0))
    base = C_QA // ATT_QKV

    def body(t_ref, pos_ref, inv_ref, gq_ref, gk_ref, dq0, dq1, dq2, dk0, dk1, dk2, dv0, dv1, dv2, buf_ref,
             out_ref, dgq_ref, dgk_ref, sc):
        del buf_ref
        part = pl.program_id(1)
        first = pl.program_id(0) == 0

        def gather(drefs):
            for j in range(NCH):
                grp, sub = divmod(j * 128, ATT_W)
                _from_residues(drefs[grp], slice(sub, sub + 128), sc, j, DILATIONS[grp])

        def normed(g_ref, drefs, dg_ref):
            c, sp, sm = _rot_tables(pos_ref, inv_ref)
            gather(drefs)
            dg = jnp.zeros((1, 128), F32)
            for j in range(NCH):
                cols = slice(j * 128, (j + 1) * 128)
                d_rot = sc[j]
                dn = d_rot * c + pltpu.roll(d_rot * sp, 120, 1) + pltpu.roll(d_rot * sm, 8, 1)
                t = t_ref[:, cols]
                r = _pair_norm(t)
                gain = g_ref[...]
                dn_t = dn * t
                out_ref[:, cols] = (r * (dn * gain - t * ((r * r) * _pair_mean(dn_t * gain)))).astype(BF16)
                dg = dg + jnp.sum(dn_t * r, axis=0, keepdims=True)
            dg = dg + pltpu.roll(dg, HD, 1)

            @pl.when(first)
            def _():
                dg_ref[...] = dg

            @pl.when(jnp.logical_not(first))
            def _():
                dg_ref[...] += dg

        @pl.when(part == 0)
        def _():
            gather((dv0, dv1, dv2))
            for j in range(NCH):
                out_ref[:, j * 128:(j + 1) * 128] = sc[j].astype(BF16)

        @pl.when(part == 1)
        def _():
            normed(gq_ref, (dq0, dq1, dq2), dgq_ref)

        @pl.when(part == 2)
        def _():
            normed(gk_ref, (dk0, dk1, dk2), dgk_ref)

    keep = pl.BlockSpec((1, 128), lambda i, p: (0, 0))
    return pl.pallas_call(
        body, name="qk_bwd", grid=(T // TT, 3),
        in_specs=[pl.BlockSpec((TT, ATT_QKV), lambda i, p: (i, base + jnp.maximum(p - 1, 0))),
                  pl.BlockSpec((TT, 1), lambda i, p: (i, 0)), const(inv), const(gq), const(gk)]
        + [res(g) for _ in range(3) for g in range(3)] + [pl.BlockSpec(memory_space=pl.ANY)],
        out_specs=[pl.BlockSpec((TT, ATT_QKV), lambda i, p: (i, base + jnp.where(p == 0, 2, p - 1))), keep, keep],
        out_shape=[S(dproj.shape, dproj.dtype), S((1, 128), F32), S((1, 128), F32)],
        input_output_aliases={14: 0},
        scratch_shapes=[pltpu.VMEM((NCH, TT, 128), F32)],
        compiler_params=pltpu.CompilerParams(dimension_semantics=("arbitrary", "arbitrary")),
    )(proj, pos, inv, gq, gk, *dqs, *dks, *dvs, dproj)


def _split_heads(t):
    low = lax.broadcasted_iota(jnp.int32, (1, 128), 1) < HD
    zero = jnp.zeros_like(t)
    return jnp.concatenate([jnp.where(low, t, zero), jnp.where(low, zero, t)], axis=0)


def _join_heads(t2):
    low = lax.broadcasted_iota(jnp.int32, (1, 128), 1) < HD
    n = t2.shape[0] // 2
    return jnp.where(low, t2[:n], t2[n:])


def _band_mask4(has_before, has_own):
    row = lax.broadcasted_iota(jnp.int32, (BLK, 4 * BLK), 0)
    lane = lax.broadcasted_iota(jnp.int32, (BLK, 4 * BLK), 1)
    key = lane & (BLK - 1)
    own = lane >= 2 * BLK
    return (own & (key <= row) & has_own) | (jnp.logical_not(own) & (key >= row) & has_before)


def _band_mask_before(has_before):
    row = lax.broadcasted_iota(jnp.int32, (BLK, 2 * BLK), 0)
    key = lax.broadcasted_iota(jnp.int32, (BLK, 2 * BLK), 1) & (BLK - 1)
    return (key >= row) & has_before


def _per_head(width, col_a, col_b):
    lane = lax.broadcasted_iota(jnp.int32, (1, width), 1)
    return jnp.where((lane & BLK) == 0, col_a, col_b)


NQ = ATT_W // 128


def _att_fwd(q, k, v, grp, name):
    dil = DILATIONS[grp]
    nb = T // dil // BLK

    def body(q_ref, kp_ref, kc_ref, vp_ref, vc_ref, o_ref, lse_ref, s_sc, p_sc):
        mask = _band_mask4(pl.program_id(1) > 0, True)
        low = lax.broadcasted_iota(jnp.int32, (1, 128), 1) < HD
        halves = lambda ref, j, h: (ref[j, :, h * BLK:(h + 1) * BLK], ref[j, :, (h + 2) * BLK:(h + 3) * BLK])
        for j in range(NQ):
            cols = slice(j * 128, (j + 1) * 128)
            k4 = jnp.concatenate([_split_heads(kp_ref[:, cols]), _split_heads(kc_ref[:, cols])], axis=0)
            s_sc[j] = jnp.where(mask, _nt(q_ref[:, cols], k4), -jnp.inf)
        mxs = [[jnp.maximum(*(jnp.max(t, axis=-1, keepdims=True) for t in halves(s_sc, j, h))) for h in range(2)]
               for j in range(NQ)]
        dens = []
        for j in range(NQ):
            p = jnp.exp(s_sc[j] - _per_head(4 * BLK, *mxs[j]))
            p_sc[j] = p.astype(BF16)
            dens.append([jnp.sum(p[:, h * BLK:(h + 1) * BLK], axis=-1, keepdims=True)
                         + jnp.sum(p[:, (h + 2) * BLK:(h + 3) * BLK], axis=-1, keepdims=True) for h in range(2)])
        for j in range(NQ):
            cols = slice(j * 128, (j + 1) * 128)
            v4 = jnp.concatenate([_split_heads(vp_ref[:, cols]), _split_heads(vc_ref[:, cols])], axis=0)
            o_ref[:, cols] = _nn(p_sc[j], v4) / jnp.where(low, dens[j][0], dens[j][1])
            lse_ref[:, cols] = jnp.where(low, mxs[j][0] + jnp.log(dens[j][0]), mxs[j][1] + jnp.log(dens[j][1]))

    cur = pl.BlockSpec((None, BLK, ATT_W), lambda r, i: (r, i, 0))
    prev = pl.BlockSpec((None, BLK, ATT_W), lambda r, i: (r, jnp.maximum(i - 1, 0), 0))
    return pl.pallas_call(
        body, name=name, grid=(dil, nb),
        in_specs=[cur, prev, cur, prev, cur],
        out_specs=[cur, cur], out_shape=[_res_shape(grp, F32)] * 2,
        scratch_shapes=[pltpu.VMEM((NQ, BLK, 4 * BLK), F32), pltpu.VMEM((NQ, BLK, 4 * BLK), BF16)],
        compiler_params=pltpu.CompilerParams(dimension_semantics=("parallel", "arbitrary")),
    )(q, k, k, v, v)


def _att_bwd(q, k, v, datt, att, lse, grp, name):
    dil = DILATIONS[grp]
    nb = T // dil // BLK
    scale = HD ** -0.5

    def body(q0_ref, q1_ref, kp_ref, kc_ref, vp_ref, vc_ref, do0_ref, do1_ref, o0_ref, o1_ref, l0_ref, l1_ref,
             dq_ref, dk_ref, dv_ref, k4_sc, v4_sc, s0_sc, s1_sc, dp0_sc, dp1_sc, p_sc, ds_sc):
        i = pl.program_id(1)
        mask_mine = _band_mask4(i > 0, True)
        mask_next = _band_mask_before(i < nb - 1)
        low = lax.broadcasted_iota(jnp.int32, (1, 128), 1) < HD
        for j in range(NQ):
            cols = slice(j * 128, (j + 1) * 128)
            k4_sc[j, :2 * BLK] = _split_heads(kp_ref[:, cols])
            k4_sc[j, 2 * BLK:] = _split_heads(kc_ref[:, cols])
            v4_sc[j, :2 * BLK] = _split_heads(vp_ref[:, cols])
            v4_sc[j, 2 * BLK:] = _split_heads(vc_ref[:, cols])
        for j in range(NQ):
            cols = slice(j * 128, (j + 1) * 128)
            s0_sc[j] = _nt(q0_ref[:, cols], k4_sc[j])
            s1_sc[j] = _nt(q1_ref[:, cols], k4_sc[j, 2 * BLK:])
            dp0_sc[j] = _nt(do0_ref[:, cols].astype(BF16), v4_sc[j])
            dp1_sc[j] = _nt(do1_ref[:, cols].astype(BF16), v4_sc[j, 2 * BLK:])
        stats = []
        for j in range(NQ):
            cols = slice(j * 128, (j + 1) * 128)
            for do_ref, o_ref, l_ref in ((do0_ref, o0_ref, l0_ref), (do1_ref, o1_ref, l1_ref)):
                prod = do_ref[:, cols].astype(F32) * o_ref[:, cols].astype(F32)
                d_all = jnp.sum(prod, axis=-1, keepdims=True)
                d_low = jnp.sum(jnp.where(low, prod, 0.0), axis=-1, keepdims=True)
                lse_t = l_ref[:, cols]
                stats.append((d_low, d_all - d_low, lse_t[:, 0:1], lse_t[:, HD:HD + 1]))
        for j in range(NQ):
            (da, db, la, lb), (da1, db1, la1, lb1) = stats[2 * j], stats[2 * j + 1]
            p0 = jnp.where(mask_mine, jnp.exp(s0_sc[j] - _per_head(4 * BLK, la, lb)), 0.0)
            ds0 = p0 * (dp0_sc[j] - _per_head(4 * BLK, da, db))
            p1 = jnp.where(mask_next, jnp.exp(s1_sc[j] - _per_head(2 * BLK, la1, lb1)), 0.0)
            ds1 = p1 * (dp1_sc[j] - _per_head(2 * BLK, da1, db1))
            p_sc[j, :BLK] = p0.astype(BF16)
            ds_sc[j, :BLK] = ds0.astype(BF16)
            p_sc[j, BLK:, 2 * BLK:] = p1.astype(BF16)
            ds_sc[j, BLK:, 2 * BLK:] = ds1.astype(BF16)
        for j in range(NQ):
            cols = slice(j * 128, (j + 1) * 128)
            dq_ref[:, cols] = _nn(ds_sc[j, :BLK], k4_sc[j]) * scale
            qq = jnp.concatenate([q0_ref[:, cols], q1_ref[:, cols]], axis=0)
            dd = jnp.concatenate([do0_ref[:, cols], do1_ref[:, cols]], axis=0).astype(BF16)
            dk_ref[:, cols] = _join_heads(_tn(ds_sc[j, :, 2 * BLK:], qq))
            dv_ref[:, cols] = _join_heads(_tn(p_sc[j, :, 2 * BLK:], dd))

    def spec(shift):
        return pl.BlockSpec((None, BLK, ATT_W), lambda r, i: (r, jnp.clip(i + shift, 0, nb - 1), 0))

    here, after, before = spec(0), spec(1), spec(-1)
    vm = pltpu.VMEM
    return pl.pallas_call(
        body, name=name, grid=(dil, nb),
        in_specs=[here, after, before, here, before, here, here, after, here, after, here, after],
        out_specs=[here] * 3, out_shape=[_res_shape(grp, F32)] * 3,
        scratch_shapes=[vm((NQ, 4 * BLK, 128), BF16), vm((NQ, 4 * BLK, 128), BF16), vm((NQ, BLK, 4 * BLK), F32),
                        vm((NQ, BLK, 2 * BLK), F32), vm((NQ, BLK, 4 * BLK), F32), vm((NQ, BLK, 2 * BLK), F32),
                        vm((NQ, 2 * BLK, 4 * BLK), BF16), vm((NQ, 2 * BLK, 4 * BLK), BF16)],
        compiler_params=pltpu.CompilerParams(dimension_semantics=("parallel", "arbitrary")),
    )(q, q, k, k, v, v, datt, datt, att, att, lse, lse)


def _att_merge(os_, lses, proj):
    nq = ATT_W // 128

    def body(o0, o1, o2, l0, l1, l2, za_ref, att_ref, lse_ref, ain_ref, sc):
        for a, ref in enumerate((o0, o1, o2, l0, l1, l2)):
            for j in range(nq):
                _from_residues(ref, slice(j * 128, (j + 1) * 128), sc, a * nq + j, DILATIONS[a % 3])
        for j in range(nq):
            cols = slice(j * 128, (j + 1) * 128)
            oa, ob, oc = (sc[a * nq + j] for a in range(3))
            la, lb, lc = (sc[(3 + a) * nq + j] for a in range(3))
            m = jnp.maximum(jnp.maximum(la, lb), lc)
            wa, wb, wc = jnp.exp(la - m), jnp.exp(lb - m), jnp.exp(lc - m)
            tot = wa + wb + wc
            att = (wa * oa + wb * ob + wc * oc) / tot
            att_ref[:, cols] = att
            lse_ref[:, cols] = m + jnp.log(tot)
            za = za_ref[:, cols]
            ain_ref[:, cols] = (att * za * _sigmoid(za)).astype(BF16)

    return pl.pallas_call(
        body, name="att_merge", grid=(T // TT,),
        in_specs=[_res_spec(g) for _ in range(2) for g in range(3)] + [_tok_spec(ATT_W, C_ZA // ATT_W)],
        out_specs=[_tok_spec(ATT_W)] * 3,
        out_shape=[S((T, ATT_W), F32), S((T, ATT_W), F32), S((T, ATT_W), BF16)],
        scratch_shapes=[pltpu.VMEM((6 * nq, TT, 128), F32)],
        compiler_params=pltpu.CompilerParams(dimension_semantics=("arbitrary",)),
    )(*os_, *lses, proj)


def _att_gate_bwd(dain, att, lse, proj, dproj):
    nq = ATT_W // 128

    def body(d_ref, att_ref, lse_ref, za_ref, buf_ref, dza_ref, da0, da1, da2, at1, at2, ls1, ls2, sc):
        del buf_ref
        for j in range(nq):
            cols = slice(j * 128, (j + 1) * 128)
            za = za_ref[:, cols]
            sg = _sigmoid(za)
            d = d_ref[:, cols].astype(F32)
            att_ = att_ref[:, cols]
            dza_ref[:, cols] = (d * att_ * sg * (1.0 + za * (1.0 - sg))).astype(BF16)
            sc[j] = d * za * sg
            sc[nq + j] = att_
            sc[2 * nq + j] = lse_ref[:, cols]
        for j in range(nq):
            cols = slice(j * 128, (j + 1) * 128)
            for grp, dst in enumerate((da0, da1, da2)):
                _to_residues(sc, j, dst, DILATIONS[grp], cols)
            for grp, dst in ((1, at1), (2, at2)):
                _to_residues(sc, nq + j, dst, DILATIONS[grp], cols)
            for grp, dst in ((1, ls1), (2, ls2)):
                _to_residues(sc, 2 * nq + j, dst, DILATIONS[grp], cols)

    res = (0, 1, 2, 1, 2, 1, 2)
    return pl.pallas_call(
        body, name="att_gate_bwd", grid=(T // TT,),
        in_specs=[_tok_spec(ATT_W)] * 3 + [_tok_spec(ATT_W, C_ZA // ATT_W), pl.BlockSpec(memory_space=pl.ANY)],
        out_specs=[_tok_spec(ATT_W, C_ZA // ATT_W)] + [_res_spec(g) for g in res],
        out_shape=[S(dproj.shape, dproj.dtype)] + [_res_shape(g, BF16) for g in res[:5]]
        + [_res_shape(g, F32) for g in res[5:]],
        input_output_aliases={4: 0},
        scratch_shapes=[pltpu.VMEM((3 * nq, TT, 128), F32)],
        compiler_params=pltpu.CompilerParams(dimension_semantics=("arbitrary",)),
    )(dain, att, lse, proj, dproj)


def _split3(v):
    hi = v.astype(BF16)
    r1 = v - hi.astype(F32)
    mid = r1.astype(BF16)
    lo = (r1 - mid.astype(F32)).astype(BF16)
    return hi, mid, lo


def _chunk_scores(qt, kt, q_ref, k_ref, h):
    cols = slice(h * GDK, (h + 1) * GDK)
    own = jnp.sum(q_ref[:, cols] * (GDK ** -0.5) * k_ref[:, cols], axis=-1, keepdims=True)
    row = lax.broadcasted_iota(jnp.int32, (GLA_C, GLA_C), 0)
    col = lax.broadcasted_iota(jnp.int32, (GLA_C, GLA_C), 1)
    a = _nt(qt.astype(BF16), kt.astype(BF16))
    return jnp.where(col < row, a, jnp.where(col == row, own, 0.0))


def _tri_sum(v, upper):
    n = v.shape[0]
    row = lax.broadcasted_iota(jnp.int32, (n, n), 0)
    col = lax.broadcasted_iota(jnp.int32, (n, n), 1)
    tri = jnp.where(col >= row if upper else col <= row, 1.0, 0.0).astype(BF16)
    hi, mid, lo = _split3(v)
    return _nn(tri, hi) + _nn(tri, mid) + _nn(tri, lo)


def _gla_gates(glr_ref, w2_ref, b_ref):
    logit = _nn(glr_ref[...].astype(BF16), w2_ref[...]) + b_ref[...]
    lg = (jnp.minimum(logit, 0.0) - jnp.log(1.0 + jnp.exp(-jnp.abs(logit)))) * (1.0 / GLA_TAU)
    return logit, _tri_sum(lg, upper=False)


def _gla_head(cum, q_ref, k_ref, h):
    cols = slice(h * GDK, (h + 1) * GDK)
    b = cum[:, cols]
    last = b[GLA_C - 1:GLA_C, :]
    e_pos = jnp.exp(b)
    e_neg = jnp.exp(-b)
    e_end = jnp.exp(last - b)
    qt = q_ref[:, cols] * (GDK ** -0.5) * e_pos
    kt = k_ref[:, cols] * e_neg
    kh = k_ref[:, cols] * e_end
    return b, last, e_pos, e_neg, e_end, qt, kt, kh


def _causal(n):
    return lax.broadcasted_iota(jnp.int32, (n, n), 1) <= lax.broadcasted_iota(jnp.int32, (n, n), 0)


def _gla_fwd(proj, w2p, bg, gn):
    nc = T // GLA_C

    def body(q_ref, k_ref, v_ref, glr_ref, zg_ref, w2_ref, b_ref, gn_ref, o_ref, bin_ref, st_ref, state):
        @pl.when(pl.program_id(0) == 0)
        def _():
            state[...] = jnp.zeros_like(state)

        _, cum = _gla_gates(glr_ref, w2_ref, b_ref)
        for h in range(GH):
            _, last, _, _, _, qt, kt, kh = _gla_head(cum, q_ref, k_ref, h)
            vcols = slice(h * GDV, (h + 1) * GDV)
            st = state[h]
            st_ref[0, h] = st
            v = v_ref[:, vcols].astype(BF16)
            qb = qt.astype(BF16)
            a = _chunk_scores(qt, kt, q_ref, k_ref, h)
            o = _nt(qb, st.astype(BF16)) + _nn(a.astype(BF16), v)
            state[h] = st * jnp.exp(last) + _tn(v, kh.astype(BF16))
            o_ref[:, vcols] = o
            r = lax.rsqrt(jnp.mean(o * o, axis=-1, keepdims=True) + EPS)
            zg = zg_ref[:, vcols]
            bin_ref[:, vcols] = (o * r * gn_ref[...] * zg * _sigmoid(zg)).astype(BF16)

    row = lambda width, cblk: pl.BlockSpec((GLA_C, width), functools.partial(lambda i, c: (i, c), c=cblk))
    full = lambda a: pl.BlockSpec(a.shape, functools.partial(lambda i, nd: (0,) * nd, nd=a.ndim))
    return pl.pallas_call(
        body, name="gla_fwd", grid=(nc,),
        in_specs=[row(512, C_QG // 512), row(512, C_KG // 512), row(1024, C_VG // 1024), row(GLR_W, C_GLR // GLR_W),
                  row(1024, C_ZG // 1024), full(w2p), full(bg), full(gn)],
        out_specs=[pl.BlockSpec((GLA_C, GH * GDV), lambda i: (i, 0)), pl.BlockSpec((GLA_C, GH * GDV), lambda i: (i, 0)),
                   pl.BlockSpec((1, GH, GDV, GDK), lambda i: (i, 0, 0, 0))],
        out_shape=[S((T, GH * GDV), F32), S((T, GH * GDV), BF16), S((nc, GH, GDV, GDK), F32)],
        scratch_shapes=[pltpu.VMEM((GH, GDV, GDK), F32)],
        compiler_params=pltpu.CompilerParams(dimension_semantics=("arbitrary",)),
    )(proj, proj, proj, proj, proj, w2p, bg, gn)


def _gla_bwd(proj, w2p, bg, gn, o_gla, states, dbin, dproj):
    nc = T // GLA_C

    def body(q_ref, k_ref, v_ref, glr_ref, zg_ref, w2_ref, b_ref, gn_ref, o_ref, st_ref, dbin_ref, buf_ref,
             out_ref, dw2_ref, dbg_ref, dgn_ref, dstate, dlogit):
        del buf_ref
        dq_ref = out_ref.at[:, C_QG:C_KG]
        dk_ref = out_ref.at[:, C_KG:C_VG]
        dv_ref = out_ref.at[:, C_VG:C_ZG]
        dzg_ref = out_ref.at[:, C_ZG:C_GLR]
        dglr_ref = out_ref.at[:, C_GLR:C_GLR + GLR_W]
        first = pl.program_id(0) == 0

        @pl.when(first)
        def _():
            dstate[...] = jnp.zeros_like(dstate)

        logit, cum = _gla_gates(glr_ref, w2_ref, b_ref)
        is_last = lax.broadcasted_iota(jnp.int32, (GLA_C, 1), 0) == GLA_C - 1
        dgn = jnp.zeros((1, GDV), F32)
        for h in range(GH):
            _, last, e_pos, e_neg, e_end, qt, kt, kh = _gla_head(cum, q_ref, k_ref, h)
            cols = slice(h * GDK, (h + 1) * GDK)
            vcols = slice(h * GDV, (h + 1) * GDV)
            o = o_ref[:, vcols]
            r = lax.rsqrt(jnp.mean(o * o, axis=-1, keepdims=True) + EPS)
            zg = zg_ref[:, vcols]
            sg = _sigmoid(zg)
            db_ = dbin_ref[:, vcols].astype(F32)
            dlin = db_ * zg * sg
            dzg_ref[:, vcols] = (db_ * (o * r * gn_ref[...]) * sg * (1.0 + zg * (1.0 - sg))).astype(BF16)
            u = dlin * gn_ref[...]
            do = (r * u - o * (r * r * r) * jnp.mean(u * o, axis=-1, keepdims=True)).astype(BF16)
            dgn = dgn + jnp.sum(dlin * o * r, axis=0, keepdims=True)
            st = st_ref[0, h]
            dst = dstate[h]
            v = v_ref[:, vcols].astype(BF16)
            qb, kb, khb = qt.astype(BF16), kt.astype(BF16), kh.astype(BF16)
            dstb = dst.astype(BF16)
            causal = _causal(GLA_C)
            a = _chunk_scores(qt, kt, q_ref, k_ref, h).astype(BF16)
            da = jnp.where(causal, _nt(do, v), 0.0).astype(BF16)
            dqt = _nn(do, st.astype(BF16)) + _nn(da, kb)
            dkt = _tn(da, qb)
            dkh = _nn(v, dstb)
            dv_ref[:, vcols] = (_tn(a, do) + _nt(khb, dstb)).astype(BF16)
            lam = jnp.exp(last)
            dlam = jnp.sum(dst * st, axis=0, keepdims=True)
            dstate[h] = dst * lam + _tn(do, qb)
            dq_ref[:, cols] = (dqt * e_pos * (GDK ** -0.5)).astype(BF16)
            dk_ref[:, cols] = (dkt * e_neg + dkh * e_end).astype(BF16)
            dkh_kh = dkh * kh
            dcum = dqt * qt - dkt * kt - dkh_kh
            dlast = jnp.sum(dkh_kh, axis=0, keepdims=True) + dlam * lam
            dcum = jnp.where(is_last, dcum + dlast, dcum)
            dlg = _tri_sum(dcum, upper=True)
            dlogit[:, cols] = dlg * (1.0 / GLA_TAU) * (1.0 - _sigmoid(logit[:, cols]))

        dl = dlogit[...]
        dlb = dl.astype(BF16)
        dglr_ref[...] = _nt(dlb, w2_ref[...]).astype(BF16)
        dw2 = _tn(glr_ref[...].astype(BF16), dlb)
        dbg = jnp.sum(dl, axis=0, keepdims=True)

        @pl.when(first)
        def _():
            dw2_ref[...] = dw2
            dbg_ref[...] = dbg
            dgn_ref[...] = dgn

        @pl.when(jnp.logical_not(first))
        def _():
            dw2_ref[...] += dw2
            dbg_ref[...] += dbg
            dgn_ref[...] += dgn

    rev = lambda i: nc - 1 - i
    row = lambda width, cblk: pl.BlockSpec((GLA_C, width), functools.partial(lambda i, c: (rev(i), c), c=cblk))
    full = lambda a: pl.BlockSpec(a.shape, functools.partial(lambda i, nd: (0,) * nd, nd=a.ndim))
    keep = lambda shape: pl.BlockSpec(shape, functools.partial(lambda i, nd: (0,) * nd, nd=len(shape)))
    return pl.pallas_call(
        body, name="gla_bwd", grid=(nc,),
        in_specs=[row(512, C_QG // 512), row(512, C_KG // 512), row(1024, C_VG // 1024), row(GLR_W, C_GLR // GLR_W),
                  row(1024, C_ZG // 1024), full(w2p), full(bg), full(gn), row(GH * GDV, 0),
                  pl.BlockSpec((1, GH, GDV, GDK), lambda i: (rev(i), 0, 0, 0)), row(GH * GDV, 0),
                  pl.BlockSpec(memory_space=pl.ANY)],
        out_specs=[row(GLA_GROUP_W, 0), keep((GLR_W, 512)), keep((1, 512)), keep((1, GDV))],
        out_shape=[S(dproj.shape, dproj.dtype), S((GLR_W, 512), F32), S((1, 512), F32), S((1, GDV), F32)],
        input_output_aliases={11: 0},
        scratch_shapes=[pltpu.VMEM((GH, GDV, GDK), F32), pltpu.VMEM((GLA_C, GH * GDK), F32)],
        compiler_params=pltpu.CompilerParams(dimension_semantics=("arbitrary",)),
    )(proj, proj, proj, proj, proj, w2p, bg, gn, o_gla, states, dbin, dproj)


RT = 512


def _rowchain(body, name, ins, outs, scratch=()):
    in_specs, args = [], []
    for spec in ins:
        if spec[0] == "tok":
            _, arr, width, cblk = spec
            in_specs.append(pl.BlockSpec((RT, width), functools.partial(lambda i, c: (i, c), c=cblk)))
        else:
            arr = spec[1]
            in_specs.append(pl.BlockSpec(arr.shape, functools.partial(lambda i, nd: (0,) * nd, nd=arr.ndim)))
        args.append(arr)
    out_specs, out_shape = [], []
    for spec in outs:
        if spec[0] == "tok":
            _, shape, dtype, width, cblk = spec
            out_specs.append(pl.BlockSpec((RT, width), functools.partial(lambda i, c: (i, c), c=cblk)))
        else:
            _, shape, dtype = spec
            out_specs.append(pl.BlockSpec(shape, functools.partial(lambda i, nd: (0,) * nd, nd=len(shape))))
        out_shape.append(S(shape, dtype))
    return pl.pallas_call(
        body, name=name, grid=(T // RT,), in_specs=in_specs, out_specs=out_specs, out_shape=out_shape,
        scratch_shapes=list(scratch), compiler_params=pltpu.CompilerParams(dimension_semantics=("arbitrary",)),
    )(*args)


def _tok(arr, width=None, cblk=0):
    return ("tok", arr, arr.shape[1] if width is None else width, cblk)


def _tok_out(dtype, width=D):
    return ("tok", (T, width), dtype, width, 0)


def _branches_fwd(ain, bin_, proj, x, w_att, w_gla, w_out):
    def body(ain_ref, bin_ref, g_ref, x_ref, wa_ref, wg_ref, wo_ref, ya_ref, yb_ref, y_ref, x1_ref):
        ya = _nn(ain_ref[...], wa_ref[...]).astype(BF16)
        yb = _nn(bin_ref[...], wg_ref[...]).astype(BF16)
        ya_ref[...] = ya
        yb_ref[...] = yb
        y = (_sigmoid(g_ref[:, :D]) * ya.astype(F32) + _sigmoid(g_ref[:, D:]) * yb.astype(F32)).astype(BF16)
        y_ref[...] = y
        x1_ref[...] = x_ref[...] + _nn(y, wo_ref[...])

    return _rowchain(body, "branches_fwd",
                     [_tok(ain), _tok(bin_), _tok(proj, 2 * D, C_GA // (2 * D)), _tok(x), ("all", w_att),
                      ("all", w_gla), ("all", w_out)],
                     [_tok_out(BF16), _tok_out(BF16), _tok_out(BF16), _tok_out(F32)])


def _accumulate(ref, part, first):
    @pl.when(first)
    def _():
        ref[...] = part

    @pl.when(jnp.logical_not(first))
    def _():
        ref[...] += part


def _ple_loss(x1, p, target, g2, w_pg, w_ple):
    def body(x1_ref, p_ref, t_ref, g_ref, wpg_ref, wple_ref, n2_ref, loss_ref, dout_ref, du_ref, dwple_ref, acc):
        first = pl.program_id(0) == 0
        x1 = x1_ref[...]
        r = lax.rsqrt(jnp.mean(x1 * x1, axis=-1, keepdims=True) + EPS)
        n2 = (x1 * r * g_ref[...]).astype(BF16)
        n2_ref[...] = n2
        pg = _sigmoid(_nn(n2, wpg_ref[...]))
        pb = p_ref[...].astype(BF16)
        e_ = _nn(pb, wple_ref[...])
        diff = x1 + e_ * pg - t_ref[...]
        _accumulate(acc, jnp.sum(diff * diff, axis=0, keepdims=True), first)
        dout = diff * (1.0 / D)
        dout_ref[...] = dout
        du_ref[...] = (dout * e_ * pg * (1.0 - pg)).astype(BF16)
        _accumulate(dwple_ref, _tn(pb, (dout * pg).astype(BF16)), first)
        loss_ref[...] = jnp.zeros((1, 128), F32) + jnp.sum(acc[...], axis=-1, keepdims=True) * (0.5 / D)

    return _rowchain(body, "ple_loss", [_tok(x1), _tok(p), _tok(target), ("all", g2), ("all", w_pg), ("all", w_ple)],
                     [_tok_out(BF16), ("acc", (1, 128), F32), _tok_out(F32), _tok_out(BF16), ("acc", (PLE, D), F32)],
                     scratch=[pltpu.VMEM((1, D), F32)])


def _ple_bwd(du, n2, y, x1, dout, g2, w_pg, w_out):
    def body(du_ref, n2_ref, y_ref, x1_ref, dout_ref, g_ref, wpg_ref, wo_ref, dx_ref, dy_ref, dg_ref, dwpg_ref,
             dwo_ref):
        first = pl.program_id(0) == 0
        x1 = x1_ref[...]
        r = lax.rsqrt(jnp.mean(x1 * x1, axis=-1, keepdims=True) + EPS)
        du_ = du_ref[...]
        dn = _nt(du_, wpg_ref[...])
        u = dn * g_ref[...]
        dx = dout_ref[...] + r * u - x1 * (r * r * r) * jnp.mean(u * x1, axis=-1, keepdims=True)
        dxb = dx.astype(BF16)
        dx_ref[...] = dx
        dy_ref[...] = _nt(dxb, wo_ref[...]).astype(BF16)
        _accumulate(dg_ref, jnp.sum(dn * x1 * r, axis=0, keepdims=True), first)
        _accumulate(dwpg_ref, _tn(n2_ref[...], du_), first)
        _accumulate(dwo_ref, _tn(y_ref[...], dxb), first)

    return _rowchain(body, "ple_bwd",
                     [_tok(du), _tok(n2), _tok(y), _tok(x1), _tok(dout), ("all", g2), ("all", w_pg), ("all", w_out)],
                     [_tok_out(F32), _tok_out(BF16), ("acc", (1, D), F32), ("acc", (D, D), F32), ("acc", (D, D), F32)])


def _branches_bwd(dy, ya, yb, ain, bin_, proj, w_att, w_gla):
    def body(dy_ref, ya_ref, yb_ref, ain_ref, bin_ref, g_ref, wa_ref, wg_ref, dg_ref, dain_ref, dbin_ref,
             dwa_ref, dwg_ref):
        first = pl.program_id(0) == 0
        dy_ = dy_ref[...].astype(F32)
        sa, sb = _sigmoid(g_ref[:, :D]), _sigmoid(g_ref[:, D:])
        dg_ref[:, :D] = (dy_ * ya_ref[...].astype(F32) * sa * (1.0 - sa)).astype(BF16)
        dg_ref[:, D:] = (dy_ * yb_ref[...].astype(F32) * sb * (1.0 - sb)).astype(BF16)
        dya = (dy_ * sa).astype(BF16)
        dyb = (dy_ * sb).astype(BF16)
        dain_ref[...] = _nt(dya, wa_ref[...]).astype(BF16)
        dbin_ref[...] = _nt(dyb, wg_ref[...]).astype(BF16)
        _accumulate(dwa_ref, _tn(ain_ref[...], dya), first)
        _accumulate(dwg_ref, _tn(bin_ref[...], dyb), first)

    gates = C_GA // (2 * D)
    return _rowchain(body, "branches_bwd",
                     [_tok(dy), _tok(ya), _tok(yb), _tok(ain), _tok(bin_), _tok(proj, 2 * D, gates), ("all", w_att),
                      ("all", w_gla)],
                     [("tok", (T, NCOL), BF16, 2 * D, gates), _tok_out(BF16, ATT_W), _tok_out(BF16),
                      ("acc", (ATT_W, D), F32), ("acc", (D, D), F32)])


def _peer(k):
    x, y, c = lax.axis_index("x"), lax.axis_index("y"), lax.axis_index("c")
    return (x ^ ((k >> 2) & 1), y ^ ((k >> 1) & 1), c ^ (k & 1))


def _my_index():
    return 4 * lax.axis_index("x") + 2 * lax.axis_index("y") + lax.axis_index("c")


def _peer_index(k):
    px, py, pc = _peer(k)
    return 4 * px + 2 * py + pc


def _pairwise_plan(src_of, dst_of, landed_of, own_src, own_dst):
    def plan(ins, outs, send, recv, local):
        n = len(ins)

        def own():
            return [pltpu.make_async_copy(own_src(ins[a]), own_dst(outs[a]), local.at[a]) for a in range(n)]

        def remote(k, a, src, dst):
            return pltpu.make_async_remote_copy(src_ref=src, dst_ref=dst, send_sem=send.at[k - 1, a],
                                                recv_sem=recv.at[k - 1, a], device_id=_peer(k), device_id_type=MESH)

        def sent():
            return [remote(k, a, src_of(ins[a], k), dst_of(outs[a])) for k in range(1, NDEV) for a in range(n)]

        def start():
            for cp in own() + sent():
                cp.start()

        def finish():
            for k in range(1, NDEV):
                for a in range(n):
                    remote(k, a, own_src(ins[a]), landed_of(outs[a], k)).wait_recv()
            for cp in sent():
                cp.wait_send()
            for cp in own():
                cp.wait()

        return start, finish

    return plan


def _pairwise_sems(n):
    return [pltpu.SemaphoreType.DMA((NDEV - 1, n)), pltpu.SemaphoreType.DMA((NDEV - 1, n)),
            pltpu.SemaphoreType.DMA((n,))]


def _gather_side(arrs):
    plan = _pairwise_plan(src_of=lambda i, k: i, dst_of=lambda o: o.at[_my_index()],
                          landed_of=lambda o, k: o.at[_peer_index(k)],
                          own_src=lambda i: i, own_dst=lambda o: o.at[_my_index()])
    return dict(arrs=arrs, out_shape=[S((NDEV,) + a.shape, a.dtype) for a in arrs],
                scratch=_pairwise_sems(len(arrs)), plan=plan)


def _exchange_side(arrs):
    plan = _pairwise_plan(src_of=lambda i, k: i.at[_peer_index(k)], dst_of=lambda o: o.at[_my_index()],
                          landed_of=lambda o, k: o.at[_peer_index(k)],
                          own_src=lambda i: i.at[_my_index()], own_dst=lambda o: o.at[_my_index()])
    return dict(arrs=arrs, out_shape=[S(a.shape, a.dtype) for a in arrs], scratch=_pairwise_sems(len(arrs)), plan=plan)


def _comm_call(side, name):
    n = len(side["arrs"])

    def body(*refs):
        start, finish = side["plan"](refs[:n], refs[n:2 * n], *refs[2 * n:])
        start()
        finish()

    hbm = pl.BlockSpec(memory_space=pl.ANY)
    return pl.pallas_call(body, name=name, in_specs=[hbm] * n, out_specs=[hbm] * n, out_shape=side["out_shape"],
                          scratch_shapes=side["scratch"])(*side["arrs"])


def _all_gather_by_chip(arrs, name):
    n = len(arrs)

    def body(*refs):
        ins, outs = refs[:n], refs[n:2 * n]
        send, recv, local = refs[2 * n:]
        x, y, c = lax.axis_index("x"), lax.axis_index("y"), lax.axis_index("c")
        me, sibling = (x, y, c), (x, y, 1 - c)
        chips = [(1 - x, y), (x, 1 - y), (1 - x, 1 - y)]

        def copy(k, a, block, to, src=None):
            px, py, pc = block
            slot = outs[a].at[4 * px + 2 * py + pc]
            return pltpu.make_async_remote_copy(
                src_ref=slot if src is None else src, dst_ref=slot, send_sem=send.at[k, a], recv_sem=recv.at[k, a],
                device_id=to, device_id_type=MESH)

        north = c == 1
        via = (jnp.where(north, 1 - x, x), jnp.where(north, y, 1 - y))
        onward = (jnp.where(north, x, 1 - x), jnp.where(north, 1 - y, y), c)
        mine = [pltpu.make_async_copy(ins[a], outs[a].at[4 * x + 2 * y + c], local.at[a]) for a in range(n)]
        first = []
        for a in range(n):
            first.append(copy(0, a, me, sibling, src=ins[a]))
            first += [copy(1 + j, a, me, (*chips[j], c), src=ins[a]) for j in range(2)]
        for cp in mine + first:
            cp.start()
        passed = []
        for j in range(2):
            for a in range(n):
                copy(1 + j, a, (*chips[j], c), me).wait_recv()
                passed.append(copy(4 + j, a, (*chips[j], c), sibling))
                passed[-1].start()
        for a in range(n):
            passed.append(copy(3, a, (*via, c), onward))
            passed[-1].start()
        for a in range(n):
            copy(3, a, (*chips[2], c), me).wait_recv()
            passed.append(copy(6, a, (*chips[2], c), sibling))
            passed[-1].start()
        for a in range(n):
            copy(0, a, sibling, me).wait_recv()
        for j, chip in enumerate(chips):
            for a in range(n):
                copy(4 + j, a, (*chip, 1 - c), me).wait_recv()
        for cp in first + passed:
            cp.wait_send()
        for cp in mine:
            cp.wait()

    hbm = pl.BlockSpec(memory_space=pl.ANY)
    return pl.pallas_call(
        body, name=name, in_specs=[hbm] * n, out_specs=[hbm] * n,
        out_shape=[S((NDEV,) + a.shape, a.dtype) for a in arrs],
        scratch_shapes=[pltpu.SemaphoreType.DMA((NDEV - 1, n)), pltpu.SemaphoreType.DMA((NDEV - 1, n)),
                        pltpu.SemaphoreType.DMA((n,))],
    )(*arrs)


NCHIP = 4


def _sibling_sum(src, name, tc=512):
    _, rows, cols = src.shape
    assert cols % tc == 0

    def body(src_ref, got_ref, out_ref, a_buf, b_buf, o_buf, send, recv, local):
        x, y, c = lax.axis_index("x"), lax.axis_index("y"), lax.axis_index("c")
        copies = [pltpu.make_async_remote_copy(
            src_ref=src_ref.at[2 * q + (1 - c)], dst_ref=got_ref.at[q], send_sem=send.at[q], recv_sem=recv.at[q],
            device_id=(x, y, 1 - c), device_id_type=MESH) for q in range(NCHIP)]
        for cp in copies:
            cp.start()
        tiles = [(q, pl.ds(t * tc, tc)) for q in range(NCHIP) for t in range(cols // tc)]

        def loads(n):
            q, tile = tiles[n]
            return [pltpu.make_async_copy(src_ref.at[2 * q + c, :, tile], a_buf.at[n % 2], local.at[n % 2, 0]),
                    pltpu.make_async_copy(got_ref.at[q, :, tile], b_buf.at[n % 2], local.at[n % 2, 1])]

        def store(n):
            q, tile = tiles[n]
            return pltpu.make_async_copy(o_buf.at[n % 2], out_ref.at[q, :, tile], local.at[n % 2, 2])

        def fetch(n):
            if n == 0 or tiles[n][0] != tiles[n - 1][0]:
                copies[tiles[n][0]].wait_recv()
            for cp in loads(n):
                cp.start()

        fetch(0)
        for n in range(len(tiles)):
            if n + 1 < len(tiles):
                fetch(n + 1)
            for cp in loads(n):
                cp.wait()
            if n >= 2:
                store(n - 2).wait()
            o_buf[n % 2] = (a_buf[n % 2].astype(F32) + b_buf[n % 2].astype(F32)).astype(BF16)
            store(n).start()
        store(len(tiles) - 2).wait()
        store(len(tiles) - 1).wait()
        for cp in copies:
            cp.wait_send()

    hbm = pl.BlockSpec(memory_space=pl.ANY)
    block = S((NCHIP, rows, cols), BF16)
    return pl.pallas_call(
        body, name=name, in_specs=[hbm], out_specs=[hbm, hbm], out_shape=[block, block],
        scratch_shapes=[pltpu.VMEM((2, rows, tc), BF16)] * 3
        + [pltpu.SemaphoreType.DMA((NCHIP,)), pltpu.SemaphoreType.DMA((NCHIP,)), pltpu.SemaphoreType.DMA((2, 3))],
    )(src)[1]


def _chips_side(arrs):
    def plan(ins, outs, send, recv, local):
        n = len(ins)

        def places():
            x, y, c = lax.axis_index("x"), lax.axis_index("y"), lax.axis_index("c")
            return 2 * x + y, c, [(1 - x, y), (x, 1 - y), (1 - x, 1 - y)]

        def own():
            here, _, _ = places()
            return [pltpu.make_async_copy(ins[a].at[here], outs[a].at[here], local.at[a]) for a in range(n)]

        def remote(j, a, src_slot, dst_slot):
            _, c, chips = places()
            cx, cy = chips[j]
            return pltpu.make_async_remote_copy(
                src_ref=ins[a].at[src_slot], dst_ref=outs[a].at[dst_slot], send_sem=send.at[j, a],
                recv_sem=recv.at[j, a], device_id=(cx, cy, c), device_id_type=MESH)

        def sent():
            here, _, chips = places()
            return [remote(j, a, 2 * cx + cy, here) for j, (cx, cy) in enumerate(chips) for a in range(n)]

        def start():
            for cp in own() + sent():
                cp.start()

        def finish():
            here, _, chips = places()
            for j, (cx, cy) in enumerate(chips):
                for a in range(n):
                    remote(j, a, here, 2 * cx + cy).wait_recv()
            for cp in sent():
                cp.wait_send()
            for cp in own():
                cp.wait()

        return start, finish

    n = len(arrs)
    return dict(arrs=arrs, out_shape=[S(a.shape, a.dtype) for a in arrs],
                scratch=[pltpu.SemaphoreType.DMA((NCHIP - 1, n)), pltpu.SemaphoreType.DMA((NCHIP - 1, n)),
                         pltpu.SemaphoreType.DMA((n,))], plan=plan)


def _adamw_shards(parts, places):
    n_src = len(parts)

    def body(*refs):
        srcs, rest = refs[:n_src], refs[n_src:]
        for j, (src, rows, cols, _) in enumerate(places):
            w_ref, m_ref, v_ref = rest[3 * j:3 * j + 3]
            outs = rest[3 * len(places) + 4 * j:3 * len(places) + 4 * j + 4]
            p_ref = srcs[src]
            g = p_ref[0, rows, cols].astype(F32)
            for s in range(1, p_ref.shape[0]):
                g = g + p_ref[s, rows, cols].astype(F32)
            delta, m_new, v_new = _adam_math(g, w_ref[0], m_ref[0], v_ref[0])
            for ref, val in zip(outs, (g, delta, m_new, v_new)):
                ref[0] = val

    flat = [a for place in places for a in place[3]]
    return pl.pallas_call(
        body, name="adam_shards",
        out_shape=[S(place[3][0].shape, F32) for place in places for _ in range(4)],
    )(*parts, *flat)


def _adam_math(g, w, m, v):
    c1 = 1.0 - ADAM_B1 ** ADAM_STEP
    c2 = 1.0 - ADAM_B2 ** ADAM_STEP
    m_new = ADAM_B1 * m + (1.0 - ADAM_B1) * g
    v_new = ADAM_B2 * v + (1.0 - ADAM_B2) * (g * g)
    return -ADAM_LR * ((m_new / c1) / (jnp.sqrt(v_new / c2) + ADAM_EPS) + ADAM_WD * w), m_new, v_new


def _adamw_small(parts, params, loss_parts):
    n = len(params)

    def body(*refs):
        p_refs, rest = refs[:n], refs[n + 1:]
        total = refs[n][0]
        for s in range(1, NDEV):
            total = total + refs[n][s]
        refs[-1][...] = total
        for j in range(n):
            w_ref, m_ref, v_ref = rest[3 * j:3 * j + 3]
            g_ref, d_ref, mo_ref, vo_ref = rest[3 * n + 4 * j:3 * n + 4 * j + 4]
            width = w_ref.shape[1]
            g = p_refs[j][0]
            for s in range(1, NDEV):
                g = g + p_refs[j][s]
            g = g[:, :width]
            delta, m_new, v_new = _adam_math(g, w_ref[...], m_ref[...], v_ref[...])
            g_ref[...] = g
            d_ref[...] = delta
            mo_ref[...] = m_new
            vo_ref[...] = v_new

    flat = [a for group in params for a in group]
    return pl.pallas_call(
        body, name="adam_small",
        out_shape=[S(group[0].shape, F32) for group in params for _ in range(4)] + [S((1, 128), F32)],
    )(*parts, loss_parts, *flat)


def _adamw_rows(parts, w, m, v, name, tc=256):
    rows, _, cols = w.shape
    nparts = parts.shape[0]
    nsteps = cols // tc

    def body(p_ref, w_hbm, m_hbm, v_hbm, g_hbm, d_hbm, mo_hbm, vo_hbm, inbuf, outbuf, insem, outsem):
        i = pl.program_id(0)
        slot = i & 1

        def view(ref, step):
            return ref.at[:, 0, pl.ds(pl.multiple_of(step * tc, tc), tc)]

        def fetch(step, sl):
            return [pltpu.make_async_copy(view(src, step), inbuf.at[sl, k], insem.at[sl, k])
                    for k, src in enumerate((w_hbm, m_hbm, v_hbm))]

        def write(step, sl):
            return [pltpu.make_async_copy(outbuf.at[sl, k], view(dst, step), outsem.at[sl, k])
                    for k, dst in enumerate((g_hbm, d_hbm, mo_hbm, vo_hbm))]

        @pl.when(i == 0)
        def _():
            for cp in fetch(0, 0):
                cp.start()

        @pl.when(i + 1 < nsteps)
        def _():
            for cp in fetch(i + 1, 1 - slot):
                cp.start()

        for cp in fetch(i, slot):
            cp.wait()

        @pl.when(i >= 2)
        def _():
            for cp in write(i - 2, slot):
                cp.wait()

        g = p_ref[0].astype(F32)
        for s in range(1, nparts):
            g = g + p_ref[s].astype(F32)
        g = g[:rows]
        delta, m_new, v_new = _adam_math(g, inbuf[slot, 0], inbuf[slot, 1], inbuf[slot, 2])
        for k, val in enumerate((g, delta, m_new, v_new)):
            outbuf[slot, k] = val
        for cp in write(i, slot):
            cp.start()

        @pl.when(i == nsteps - 1)
        def _():
            for cp in write(i - 1, 1 - slot) + write(i, slot):
                cp.wait()

    hbm = pl.BlockSpec(memory_space=pl.ANY)
    assert nsteps >= 2
    return pl.pallas_call(
        body, name=name, grid=(nsteps,),
        in_specs=[pl.BlockSpec((nparts, parts.shape[1], tc), lambda i: (0, 0, i)), hbm, hbm, hbm],
        out_specs=[hbm] * 4, out_shape=[S((rows, 1, cols), F32)] * 4,
        scratch_shapes=[pltpu.VMEM((2, 3, rows, tc), F32), pltpu.VMEM((2, 4, rows, tc), F32),
                        pltpu.SemaphoreType.DMA((2, 3)), pltpu.SemaphoreType.DMA((2, 4))],
        compiler_params=pltpu.CompilerParams(dimension_semantics=("arbitrary",)),
    )(parts, w, m, v)


SLAB = 1296
REMAP_RUNS = 4
_PIECES = ((O_QA, O_ZA, C_QA), (O_ZA, O_QG, C_ZA), (O_QG, O_GLR, C_QG), (O_GLR, O_ZG, C_GLR), (O_ZG, O_GA, C_ZG),
           (O_GA, O_END, C_GA))


def _slab_row_of_aligned(a):
    for o0, o1, a0 in _PIECES:
        if a0 <= a < a0 + o1 - o0:
            c = o0 + a - a0
            return SLAB * (c // W_IN_SHARD) + c % W_IN_SHARD
    return -1


def _aligned_row_of_slab(r):
    d, l = divmod(r, SLAB)
    if l >= W_IN_SHARD:
        return -1
    c = d * W_IN_SHARD + l
    for o0, o1, a0 in _PIECES:
        if o0 <= c < o1:
            return a0 + c - o0
    raise AssertionError(c)


def _remap_table(row_of, n_out, block, n_src):
    win = block + 16
    table = []
    for b in range(n_out // block):
        runs = []
        for i in range(block):
            s = row_of(b * block + i)
            if s < 0:
                continue
            if runs and runs[-1][0] + runs[-1][2] == s and runs[-1][1] + runs[-1][2] == i:
                runs[-1][2] += 1
            else:
                runs.append([s, i, 1])
        assert len(runs) <= REMAP_RUNS, (b, runs)
        row = []
        for s, i, n in runs:
            w = min(s // 16 * 16, n_src - win)
            assert 0 <= s - w and s - w + n <= win
            row += [w, s - w, i, n]
        table.append(row + [0] * (4 * REMAP_RUNS - len(row)))
    return table


def _remap_rows(src, row_of, n_out, block, name):
    n_src, cols = src.shape
    nb, win = n_out // block, block + 16
    table = _remap_table(row_of, n_out, block, n_src)
    runs = [[tuple(row[4 * k:4 * k + 4]) for k in range(REMAP_RUNS) if row[4 * k + 3] > 0] for row in table]

    def body(src_hbm, out_hbm, wbuf, obuf, insem, outsem):
        def fetches(b):
            return [pltpu.make_async_copy(src_hbm.at[pl.ds(w, win)], wbuf.at[b % 2, k], insem.at[b % 2, k])
                    for k, (w, _, _, _) in enumerate(runs[b])]

        def store(b):
            return pltpu.make_async_copy(obuf.at[b % 2], out_hbm.at[pl.ds(b * block, block)], outsem.at[b % 2])

        for cp in fetches(0):
            cp.start()
        for b in range(nb):
            if b + 1 < nb:
                for cp in fetches(b + 1):
                    cp.start()
            for cp in fetches(b):
                cp.wait()
            if b >= 2:
                store(b - 2).wait()
            if sum(count for _, _, _, count in runs[b]) < block:
                obuf[b % 2] = jnp.zeros((block, cols), src.dtype)
            for k, (_, shift, first, count) in enumerate(runs[b]):
                obuf[b % 2, first:first + count, :] = wbuf[b % 2, k, shift:shift + count, :]
            store(b).start()
        store(nb - 2).wait()
        store(nb - 1).wait()

    hbm = pl.BlockSpec(memory_space=pl.ANY)
    return pl.pallas_call(
        body, name=name, in_specs=[hbm], out_specs=hbm, out_shape=S((n_out, cols), src.dtype),
        scratch_shapes=[pltpu.VMEM((2, REMAP_RUNS, win, cols), src.dtype), pltpu.VMEM((2, block, cols), src.dtype),
                        pltpu.SemaphoreType.DMA((2, REMAP_RUNS)), pltpu.SemaphoreType.DMA((2,))],
    )(src)


def _col_blocks(w, width):
    return w.reshape(w.shape[0], NDEV, width).transpose(1, 0, 2)


def _from_col_blocks(w):
    return w.transpose(1, 0, 2).reshape(w.shape[1], NDEV * w.shape[2])


def _local_step(x2, p2, pos, tgt, norm_g, qk_norm_q, qk_norm_k, gla_gate_b, gla_norm_g, ple_norm_g, w_al,
                weights=None, proj_side=None, unpack=None, dw_side_of=None, dh_side_of=None):
    half = ROT_DIM // 2
    inv8 = jnp.power(jnp.float32(ROPE_THETA), -jnp.arange(half, dtype=F32) * 2.0 / ROT_DIM)
    inv = jnp.tile(jnp.concatenate([inv8, inv8, jnp.zeros((HD - ROT_DIM,), F32)]), 2).reshape(1, 128)
    gq = jnp.tile(qk_norm_q, (1, 2))
    gk = jnp.tile(qk_norm_k, (1, 2))

    proj, h, got = _proj_rms(x2, norm_g, w_al, proj_side)
    if proj_side is not None:
        weights = unpack(got)
    w2p, w_att_f, w_gla_f, w_out_f, w_pg_f, w_ple_f = weights
    qkv = _qk_prep(proj, pos, inv, gq, gk)
    fwd = [_att_fwd(qkv[g], qkv[3 + g], qkv[6 + g], g, f"att_fwd{g}") for g in range(3)]
    att, lse, ain = _att_merge([f[0] for f in fwd], [f[1] for f in fwd], proj)
    o_gla, bin_, states = _gla_fwd(proj, w2p, gla_gate_b, gla_norm_g)
    ya, yb, y, x1 = _branches_fwd(ain, bin_, proj, x2, w_att_f, w_gla_f, w_out_f)
    n2, loss_v, dout, du, dw_ple = _ple_loss(x1, p2, tgt, ple_norm_g, w_pg_f, w_ple_f)

    dx1, dy, dg_ple, dw_pg, dw_out = _ple_bwd(du, n2, y, x1, dout, ple_norm_g, w_pg_f, w_out_f)
    dproj, dain, dbin, dw_att, dw_gla = _branches_bwd(dy, ya, yb, ain, bin_, proj, w_att_f, w_gla_f)
    dproj, da0, da1, da2, at1, at2, ls1, ls2 = _att_gate_bwd(dain, att, lse, proj, dproj)
    datts, atts, lses = (da0, da1, da2), (att[None], at1, at2), (lse[None], ls1, ls2)
    dproj, dw2, dbg, dgn = _gla_bwd(proj, w2p, gla_gate_b, gla_norm_g, o_gla, states, dbin, dproj)
    bwd = [_att_bwd(qkv[g], qkv[3 + g], qkv[6 + g], datts[g], atts[g], lses[g], g, f"att_bwd{g}") for g in range(3)]
    dproj, dgq, dgk = _qk_bwd(proj, pos, inv, gq, gk, [b[0] for b in bwd], [b[1] for b in bwd],
                              [b[2] for b in bwd], dproj)
    out = dict(loss=loss_v, dw2=dw2, dw_att=dw_att, dw_gla=dw_gla, dw_out=dw_out, dw_pg=dw_pg, dw_ple=dw_ple,
               dgq=dgq, dgk=dgk, dbg=dbg, dgn=dgn, dg_ple=dg_ple)
    if dw_side_of is None:
        dw_al = _mm(dproj, h, mode="tn", name="dw_in", tm=1536, tn=D, tk=T, out_dtype=BF16)
    else:
        dw_al, out["dw_side"] = _mm(dproj, h, mode="tn", name="dw_in", tm=1536, tn=D, tk=T, out_dtype=BF16,
                                    side=dw_side_of(out))
    grad_x, dg_norm, out["dh_side"] = _dh_rms(dproj, w_al, x2, norm_g, dx1,
                                              None if dh_side_of is None else dh_side_of(dw_al))
    out.update(grad_x=grad_x, dw_al=dw_al, dg_norm=dg_norm)
    return out


def kernel(x, p, positions, norm_g, w_in, qk_norm_q, qk_norm_k, gla_gate_w2, gla_gate_b, gla_norm_g, w_att_proj, w_gla_proj, w_out, ple_norm_g, w_ple_gate, w_ple, loss_target, m_norm_g, m_w_in, m_qk_norm_q, m_qk_norm_k, m_gla_gate_w2, m_gla_gate_b, m_gla_norm_g, m_w_att_proj, m_w_gla_proj, m_w_out, m_ple_norm_g, m_w_ple_gate, m_w_ple, v_norm_g, v_w_in, v_qk_norm_q, v_qk_norm_k, v_gla_gate_w2, v_gla_gate_b, v_gla_norm_g, v_w_att_proj, v_w_gla_proj, v_w_out, v_ple_norm_g, v_w_ple_gate, v_w_ple):
    x2, p2, tgt = x[0], p[0, 0], loss_target[0]
    pos = positions.astype(F32).reshape(T, 1)

    rows3 = jnp.stack([w_gla_proj[0], w_out[0], w_ple_gate[0]]).astype(BF16)
    cols3 = jnp.concatenate([w_att_proj[0], w_ple[0], jnp.pad(gla_gate_w2[0], ((0, 0), (0, 64)))], axis=0).astype(BF16)
    mine = jnp.pad(w_in[0].T.astype(BF16), ((0, SLAB - W_IN_SHARD), (0, 0)))
    (g_in,) = _all_gather_by_chip([mine], "gather_w_in")
    w_al = _remap_rows(g_in.reshape(NDEV * SLAB, D), _slab_row_of_aligned, NCOL, 1536, "align_w_in")

    def unpack(got):
        g_rows, g_cols = got
        w2_f = _from_col_blocks(g_cols[:, 768:784, :64])
        return (jnp.pad(w2_f, ((0, GLR_W - GLR_N), (0, 0))), _from_col_blocks(g_cols[:, :512]),
                g_rows[:, 0].reshape(D, D), g_rows[:, 1].reshape(D, D), g_rows[:, 2].reshape(D, D),
                _from_col_blocks(g_cols[:, 512:768]))

    def dw_side_of(g):
        s_rows = jnp.concatenate([g[k].reshape(NDEV, 128, D) for k in ("dw_gla", "dw_out", "dw_pg")], axis=1)
        s_cols = jnp.concatenate([_col_blocks(g["dw_att"], 128), _col_blocks(g["dw_ple"], 128),
                                  jnp.pad(_col_blocks(g["dw2"][:GLR_N], 64), ((0, 0), (0, 0), (0, 64)))], axis=1)
        return _exchange_side([s_rows.astype(BF16), s_cols.astype(BF16)])

    def dh_side_of(dw_al):
        s_in = _remap_rows(dw_al, _aligned_row_of_slab, NDEV * SLAB, SLAB, "shard_dw_in").reshape(NDEV, SLAB, D)
        return _chips_side([_sibling_sum(s_in, "sibling_sum")])

    loc = _local_step(x2, p2, pos, tgt, norm_g, qk_norm_q, qk_norm_k, gla_gate_b, gla_norm_g, ple_norm_g, w_al,
                      proj_side=_gather_side([rows3, cols3]), unpack=unpack, dw_side_of=dw_side_of,
                      dh_side_of=dh_side_of)
    loss_v, grad_x = loc["loss"], loc["grad_x"]
    dg_norm, dgq, dgk, dbg, dgn, dg_ple = (loc[k] for k in ("dg_norm", "dgq", "dgk", "dbg", "dgn", "dg_ple"))
    r_rows, r_cols = loc["dw_side"]
    (r_in,) = loc["dh_side"]

    r_small = _comm_call(_gather_side([dg_norm, dgq, dgk, dbg, dgn, dg_ple, loss_v]), "gather_small")

    outs = {}

    rows_of = lambda a: jnp.transpose(a, (2, 0, 1))
    outs["w_in"] = [jnp.transpose(o, (1, 2, 0))[0] for o in
                    _adamw_rows(r_in, rows_of(w_in), rows_of(m_w_in), rows_of(v_w_in), "adam_w_in")]
    places = (("w_gla_proj", 0, slice(0, 128), slice(None), (w_gla_proj, m_w_gla_proj, v_w_gla_proj)),
              ("w_out", 0, slice(128, 256), slice(None), (w_out, m_w_out, v_w_out)),
              ("w_ple_gate", 0, slice(256, 384), slice(None), (w_ple_gate, m_w_ple_gate, v_w_ple_gate)),
              ("w_att_proj", 1, slice(0, 512), slice(None), (w_att_proj, m_w_att_proj, v_w_att_proj)),
              ("w_ple", 1, slice(512, 768), slice(None), (w_ple, m_w_ple, v_w_ple)),
              ("gla_gate_w2", 1, slice(768, 784), slice(0, 64), (gla_gate_w2, m_gla_gate_w2, v_gla_gate_w2)))
    res = _adamw_shards([r_rows, r_cols], [place[1:] for place in places])
    for j, place in enumerate(places):
        outs[place[0]] = [o[0] for o in res[4 * j:4 * j + 4]]
    small = ((norm_g, m_norm_g, v_norm_g), (qk_norm_q, m_qk_norm_q, v_qk_norm_q), (qk_norm_k, m_qk_norm_k, v_qk_norm_k),
             (gla_gate_b, m_gla_gate_b, v_gla_gate_b), (gla_norm_g, m_gla_norm_g, v_gla_norm_g),
             (ple_norm_g, m_ple_norm_g, v_ple_norm_g))
    sm = _adamw_small(r_small[:6], small, r_small[6])
    for j, nm in enumerate(("norm_g", "qk_norm_q", "qk_norm_k", "gla_gate_b", "gla_norm_g", "ple_norm_g")):
        outs[nm] = [o[0] for o in sm[4 * j:4 * j + 4]]

    loss = sm[-1][0, 0]
    order = ["norm_g", "w_in", "qk_norm_q", "qk_norm_k", "gla_gate_w2", "gla_gate_b", "gla_norm_g", "w_att_proj",
             "w_gla_proj", "w_out", "ple_norm_g", "w_ple_gate", "w_ple"]
    result = [loss, grad_x[None]]
    for i in range(4):
        result += [outs[nm][i][None] for nm in order]
    return tuple(result)
```

```python
import functools

import jax
import jax.numpy as jnp
from jax import lax
from jax.experimental import pallas as pl
from jax.experimental.pallas import tpu as pltpu

F32 = jnp.float32
BF16 = jnp.bfloat16
S = jax.ShapeDtypeStruct

T = 4096
D = 1024
NDEV = 8
HD = 64
ATT_W = 512
ATT_QKV = 1536
DILATIONS = (1, 4, 16)
BLK = 128
GH, GDK, GDV = 4, 128, 256
GLA_C = 128
PLE = 256
EPS = 1e-6
ROT_DIM = 16
ROPE_THETA = 500000.0
GLA_TAU = 16.0
W_IN_SHARD = 1282

C_QG, C_KG, C_VG, C_ZG, C_GLR, C_ZA, C_GA, C_GB, C_QA, C_KA, C_VA = (
    0, 512, 1024, 2048, 3072, 3584, 4096, 5120, 6144, 7680, 9216)
GLA_GROUP_W = 3584
GLR_W = 512
NCOL = 10752
GLR_N = 16
O_QA, O_ZA, O_QG, O_GLR, O_ZG, O_GA, O_END = 0, 4608, 5120, 7168, 7184, 8208, 10256

ADAM_LR, ADAM_B1, ADAM_B2, ADAM_EPS, ADAM_WD, ADAM_STEP = 0.001, 0.9, 0.999, 1e-08, 0.01, 10

MESH = pl.DeviceIdType.MESH


def _sigmoid(z):
    return 1.0 / (1.0 + jnp.exp(-z))


def _dot(a, b, dims):
    return lax.dot_general(a, b, (dims, ((), ())), preferred_element_type=F32)


def _nn(a, b):
    return _dot(a, b, ((1,), (0,)))


def _nt(a, b):
    return _dot(a, b, ((1,), (1,)))


def _tn(a, b):
    return _dot(a, b, ((0,), (0,)))


def _mm(a, b, *, mode, name, tm, tn, tk, out_dtype=F32, res=None, side=None):
    if mode == "nn":
        (m, k), n = a.shape, b.shape[1]
        a_spec = pl.BlockSpec((tm, tk), lambda i, j, l: (i, l))
        b_spec = pl.BlockSpec((tk, tn), lambda i, j, l: (l, j))
        dot = _nn
    elif mode == "nt":
        (m, k), n = a.shape, b.shape[0]
        a_spec = pl.BlockSpec((tm, tk), lambda i, j, l: (i, l))
        b_spec = pl.BlockSpec((tn, tk), lambda i, j, l: (j, l))
        dot = _nt
    else:
        (k, m), n = a.shape, b.shape[1]
        a_spec = pl.BlockSpec((tk, tm), lambda i, j, l: (l, i))
        b_spec = pl.BlockSpec((tk, tn), lambda i, j, l: (l, j))
        dot = _tn
    assert m % tm == 0 and n % tn == 0 and k % tk == 0, (name, m, n, k)
    grid = (m // tm, n // tn, k // tk)
    nk = grid[2]
    o_spec = pl.BlockSpec((tm, tn), lambda i, j, l: (i, j))
    in_specs = [a_spec, b_spec]
    args = [a, b]
    if res is not None:
        in_specs.append(o_spec)
        args.append(res)
    n_in = len(args)
    n_side = 0 if side is None else len(side["arrs"])
    hbm = pl.BlockSpec(memory_space=pl.ANY)

    def body(*refs):
        a_ref, b_ref = refs[0], refs[1]
        r_ref = refs[2] if res is not None else None
        o_ref = refs[n_in + n_side]
        scratch = refs[n_in + 2 * n_side + 1:]
        if side is not None:
            start, finish_side = side["plan"](refs[n_in:n_in + n_side], refs[n_in + n_side + 1:n_in + 2 * n_side + 1],
                                              *scratch[1 if nk > 1 else 0:])
            ids = [pl.program_id(d) for d in range(3)]

            @pl.when((ids[0] == 0) & (ids[1] == 0) & (ids[2] == 0))
            def _():
                start()

        part = dot(a_ref[...].astype(BF16), b_ref[...].astype(BF16))

        def finish(val):
            if r_ref is not None:
                val = val + r_ref[...]
            o_ref[...] = val.astype(out_dtype)

        if nk == 1:
            finish(part)
        else:
            acc = scratch[0]
            l = pl.program_id(2)

            @pl.when(l == 0)
            def _():
                acc[...] = part

            @pl.when(l > 0)
            def _():
                acc[...] += part

            @pl.when(l == nk - 1)
            def _():
                finish(acc[...])

        if side is not None:
            @pl.when((ids[0] == grid[0] - 1) & (ids[1] == grid[1] - 1) & (ids[2] == grid[2] - 1))
            def _():
                finish_side()

    sems = [] if side is None else side["scratch"]
    outs = pl.pallas_call(
        body, name=name, grid=grid,
        in_specs=in_specs + [hbm] * n_side, out_specs=[o_spec] + [hbm] * n_side,
        out_shape=[S((m, n), out_dtype)] + ([] if side is None else side["out_shape"]),
        scratch_shapes=([pltpu.VMEM((tm, tn), F32)] if nk > 1 else []) + sems,
        compiler_params=pltpu.CompilerParams(
            dimension_semantics=("arbitrary",) * 3 if side is not None else ("parallel", "parallel", "arbitrary")),
    )(*args, *([] if side is None else side["arrs"]))
    return outs[0] if side is None else (outs[0], outs[1:])


def _side_parts(side, refs, n_in, n_out):
    n_side = 0 if side is None else len(side["arrs"])
    scratch = refs[n_in + n_out + 2 * n_side:]
    if side is None:
        return (lambda: None), (lambda: None), scratch
    start, finish = side["plan"](refs[n_in:n_in + n_side], refs[n_in + n_side + n_out:n_in + n_out + 2 * n_side],
                                 *scratch[len(scratch) - len(side["scratch"]):])
    return start, finish, scratch


def _proj_rms(x, g, wt, side=None):
    tm, tn = 1024, 1536
    grid = (T // tm, NCOL // tn)
    n_side = 0 if side is None else len(side["arrs"])
    hbm = pl.BlockSpec(memory_space=pl.ANY)

    def body(*refs):
        x_ref, g_ref, w_ref = refs[:3]
        o_ref, h_ref = refs[3 + n_side], refs[4 + n_side]
        start, finish, _ = _side_parts(side, refs, 3, 2)
        i, j = pl.program_id(0), pl.program_id(1)

        @pl.when((i == 0) & (j == 0))
        def _():
            start()

        @pl.when(j == 0)
        def _():
            xf = x_ref[...]
            r = lax.rsqrt(jnp.mean(xf * xf, axis=-1, keepdims=True) + EPS)
            h_ref[...] = (xf * r * g_ref[...]).astype(BF16)

        o_ref[...] = _nt(h_ref[...], w_ref[...])

        @pl.when((i == grid[0] - 1) & (j == grid[1] - 1))
        def _():
            finish()

    outs = pl.pallas_call(
        body, name="proj", grid=grid,
        in_specs=[pl.BlockSpec((tm, D), lambda i, j: (i, 0)), pl.BlockSpec((1, D), lambda i, j: (0, 0)),
                  pl.BlockSpec((tn, D), lambda i, j: (j, 0))] + [hbm] * n_side,
        out_specs=[pl.BlockSpec((tm, tn), lambda i, j: (i, j)), pl.BlockSpec((tm, D), lambda i, j: (i, 0))] + [hbm] * n_side,
        out_shape=[S((T, NCOL), F32), S((T, D), BF16)] + ([] if side is None else side["out_shape"]),
        scratch_shapes=[] if side is None else side["scratch"],
        compiler_params=pltpu.CompilerParams(dimension_semantics=("arbitrary", "arbitrary")),
    )(x, g, wt, *([] if side is None else side["arrs"]))
    return outs[0], outs[1], outs[2:]


def _dh_rms(dproj, wt, x, g, skip, side=None):
    tm, tk = 1024, 2688
    grid = (T // tm, NCOL // tk)
    n_side = 0 if side is None else len(side["arrs"])
    hbm = pl.BlockSpec(memory_space=pl.ANY)

    def body(*refs):
        a_ref, w_ref, x_ref, g_ref, s_ref = refs[:5]
        dx_ref, dg_ref = refs[5 + n_side], refs[6 + n_side]
        start, finish, scratch = _side_parts(side, refs, 5, 2)
        acc = scratch[0]
        i, l = pl.program_id(0), pl.program_id(1)

        @pl.when((i == 0) & (l == 0))
        def _():
            start()

        part = _nn(a_ref[...], w_ref[...])

        @pl.when(l == 0)
        def _():
            acc[...] = part

        @pl.when(l > 0)
        def _():
            acc[...] += part

        @pl.when(l == grid[1] - 1)
        def _():
            xf = x_ref[...]
            r = lax.rsqrt(jnp.mean(xf * xf, axis=-1, keepdims=True) + EPS)
            dn = acc[...]
            u = dn * g_ref[...]
            dx_ref[...] = s_ref[...] + r * u - xf * (r * r * r) * jnp.mean(u * xf, axis=-1, keepdims=True)
            dg = jnp.sum(dn * xf * r, axis=0, keepdims=True)

            @pl.when(i == 0)
            def _():
                dg_ref[...] = dg

            @pl.when(i > 0)
            def _():
                dg_ref[...] += dg

        @pl.when((i == grid[0] - 1) & (l == grid[1] - 1))
        def _():
            finish()

    tok = pl.BlockSpec((tm, D), lambda i, l: (i, 0))
    outs = pl.pallas_call(
        body, name="dh", grid=grid,
        in_specs=[pl.BlockSpec((tm, tk), lambda i, l: (i, l)), pl.BlockSpec((tk, D), lambda i, l: (l, 0)), tok,
                  pl.BlockSpec((1, D), lambda i, l: (0, 0)), tok] + [hbm] * n_side,
        out_specs=[tok, pl.BlockSpec((1, D), lambda i, l: (0, 0))] + [hbm] * n_side,
        out_shape=[S((T, D), F32), S((1, D), F32)] + ([] if side is None else side["out_shape"]),
        scratch_shapes=[pltpu.VMEM((tm, D), F32)] + ([] if side is None else side["scratch"]),
        compiler_params=pltpu.CompilerParams(dimension_semantics=("arbitrary", "arbitrary")),
    )(dproj, wt, x, g, skip, *([] if side is None else side["arrs"]))
    return outs[0], outs[1], outs[2:]


def _rot_tables(pos_ref, inv_ref):
    lane = lax.broadcasted_iota(jnp.int32, (1, 128), 1) % HD
    ang = pos_ref[...] * inv_ref[...]
    cos, sin = jnp.cos(ang), jnp.sin(ang)
    c = jnp.where(lane < ROT_DIM, cos, 1.0)
    sp = jnp.where((lane >= ROT_DIM // 2) & (lane < ROT_DIM), sin, 0.0)
    sm = jnp.where(lane < ROT_DIM // 2, -sin, 0.0)
    return c, sp, sm


def _head_sums(v):
    same = (lax.broadcasted_iota(jnp.int32, (128, 128), 0) < HD) == (lax.broadcasted_iota(jnp.int32, (128, 128), 1) < HD)
    ones = jnp.where(same, 1.0, 0.0).astype(BF16)
    hi = v.astype(BF16)
    lo = (v - hi.astype(F32)).astype(BF16)
    return _nn(hi, ones) + _nn(lo, ones)


def _pair_norm(t):
    return lax.rsqrt(_head_sums(t * t) * (1.0 / HD) + EPS)


def _pair_mean(t):
    return _head_sums(t) * (1.0 / HD)


TT = 256
NCH = ATT_QKV // 128


def _res_shape(grp, dtype):
    return S((DILATIONS[grp], T // DILATIONS[grp], ATT_W), dtype)


def _res_spec(grp):
    dil = DILATIONS[grp]
    return pl.BlockSpec((dil, TT // dil, ATT_W), lambda i: (0, i, 0))


def _to_residues(sc, j, dst_ref, dil, cols):
    n = TT // dil
    for r in range(dil):
        rows = sc[j] if dil == 1 else sc.at[j][pl.ds(r, n, stride=dil), :]
        dst_ref[r, :, cols] = rows.astype(dst_ref.dtype)


def _from_residues(src_ref, cols, sc, j, dil):
    n = TT // dil
    for r in range(dil):
        if dil == 1:
            sc[j] = src_ref[r, :, cols].astype(F32)
        else:
            sc.at[j][pl.ds(r, n, stride=dil), :] = src_ref[r, :, cols].astype(F32)


def _tok_spec(width, cblk=0):
    return pl.BlockSpec((TT, width), functools.partial(lambda i, c: (i, c), c=cblk))


def _const_spec(arr_or_shape):
    shape = arr_or_shape if isinstance(arr_or_shape, tuple) else arr_or_shape.shape
    return pl.BlockSpec(shape, functools.partial(lambda i, nd: (0,) * nd, nd=len(shape)))


def _qk_prep(proj, pos, inv, gq, gk):
    def body(q_ref, k_ref, v_ref, pos_ref, inv_ref, gq_ref, gk_ref, *rest):
        outs, sc = rest[:9], rest[9]
        c, sp, sm = _rot_tables(pos_ref, inv_ref)
        for which, (src, g_ref) in enumerate(((q_ref, gq_ref), (k_ref, gk_ref), (v_ref, None))):
            if g_ref is not None:
                g = jnp.broadcast_to(g_ref[...] * ((HD ** -0.5) if which == 0 else 1.0), c.shape)
                cg, spg, smg = c * g, sp * pltpu.roll(g, 8, 1), sm * pltpu.roll(g, 120, 1)
            for j in range(NCH):
                t = src[:, j * 128:(j + 1) * 128]
                if g_ref is not None:
                    t = _pair_norm(t) * (t * cg + pltpu.roll(t, 8, 1) * spg + pltpu.roll(t, 120, 1) * smg)
                sc[j] = t
            for j in range(NCH):
                grp, sub = divmod(j * 128, ATT_W)
                _to_residues(sc, j, outs[which * 3 + grp], DILATIONS[grp], slice(sub, sub + 128))

    return pl.pallas_call(
        body, name="qk_prep", grid=(T // TT,),
        in_specs=[_tok_spec(ATT_QKV, C_QA // ATT_QKV), _tok_spec(ATT_QKV, C_KA // ATT_QKV),
                  _tok_spec(ATT_QKV, C_VA // ATT_QKV), _tok_spec(1), _const_spec(inv), _const_spec(gq), _const_spec(gk)],
        out_specs=[_res_spec(g) for _ in range(3) for g in range(3)],
        out_shape=[_res_shape(g, BF16) for _ in range(3) for g in range(3)],
        scratch_shapes=[pltpu.VMEM((NCH, TT, 128), F32)],
        compiler_params=pltpu.CompilerParams(dimension_semantics=("arbitrary",)),
    )(proj, proj, proj, pos, inv, gq, gk)


def _qk_bwd(proj, pos, inv, gq, gk, dqs, dks, dvs, dproj):
    const = lambda a: pl.BlockSpec(a.shape, functools.partial(lambda i, p, nd: (0,) * nd, nd=a.ndim))
    res = lambda g: pl.BlockSpec((DILATIONS[g], TT // DILATIONS[g], ATT_W), lambda i, p: (0, i, 0))
    base = C_QA // ATT_QKV

    def body(t_ref, pos_ref, inv_ref, gq_ref, gk_ref, dq0, dq1, dq2, dk0, dk1, dk2, dv0, dv1, dv2, buf_ref,
             out_ref, dgq_ref, dgk_ref, sc):
        del buf_ref
        part = pl.program_id(1)
        first = pl.program_id(0) == 0

        def gather(drefs):
            for j in range(NCH):
                grp, sub = divmod(j * 128, ATT_W)
                _from_residues(drefs[grp], slice(sub, sub + 128), sc, j, DILATIONS[grp])

        def normed(g_ref, drefs, dg_ref):
            c, sp, sm = _rot_tables(pos_ref, inv_ref)
            gather(drefs)
            dg = jnp.zeros((1, 128), F32)
            for j in range(NCH):
                cols = slice(j * 128, (j + 1) * 128)
                d_rot = sc[j]
                dn = d_rot * c + pltpu.roll(d_rot * sp, 120, 1) + pltpu.roll(d_rot * sm, 8, 1)
                t = t_ref[:, cols]
                r = _pair_norm(t)
                gain = g_ref[...]
                dn_t = dn * t
                out_ref[:, cols] = (r * (dn * gain - t * ((r * r) * _pair_mean(dn_t * gain)))).astype(BF16)
                dg = dg + jnp.sum(dn_t * r, axis=0, keepdims=True)
            dg = dg + pltpu.roll(dg, HD, 1)

            @pl.when(first)
            def _():
                dg_ref[...] = dg

            @pl.when(jnp.logical_not(first))
            def _():
                dg_ref[...] += dg

        @pl.when(part == 0)
        def _():
            gather((dv0, dv1, dv2))
            for j in range(NCH):
                out_ref[:, j * 128:(j + 1) * 128] = sc[j].astype(BF16)

        @pl.when(part == 1)
        def _():
            normed(gq_ref, (dq0, dq1, dq2), dgq_ref)

        @pl.when(part == 2)
        def _():
            normed(gk_ref, (dk0, dk1, dk2), dgk_ref)

    keep = pl.BlockSpec((1, 128), lambda i, p: (0, 0))
    return pl.pallas_call(
        body, name="qk_bwd", grid=(T // TT, 3),
        in_specs=[pl.BlockSpec((TT, ATT_QKV), lambda i, p: (i, base + jnp.maximum(p - 1, 0))),
                  pl.BlockSpec((TT, 1), lambda i, p: (i, 0)), const(inv), const(gq), const(gk)]
        + [res(g) for _ in range(3) for g in range(3)] + [pl.BlockSpec(memory_space=pl.ANY)],
        out_specs=[pl.BlockSpec((TT, ATT_QKV), lambda i, p: (i, base + jnp.where(p == 0, 2, p - 1))), keep, keep],
        out_shape=[S(dproj.shape, dproj.dtype), S((1, 128), F32), S((1, 128), F32)],
        input_output_aliases={14: 0},
        scratch_shapes=[pltpu.VMEM((NCH, TT, 128), F32)],
        compiler_params=pltpu.CompilerParams(dimension_semantics=("arbitrary", "arbitrary")),
    )(proj, pos, inv, gq, gk, *dqs, *dks, *dvs, dproj)


def _split_heads(t):
    low = lax.broadcasted_iota(jnp.int32, (1, 128), 1) < HD
    zero = jnp.zeros_like(t)
    return jnp.concatenate([jnp.where(low, t, zero), jnp.where(low, zero, t)], axis=0)


def _join_heads(t2):
    low = lax.broadcasted_iota(jnp.int32, (1, 128), 1) < HD
    n = t2.shape[0] // 2
    return jnp.where(low, t2[:n], t2[n:])


def _band_mask4(has_before, has_own):
    row = lax.broadcasted_iota(jnp.int32, (BLK, 4 * BLK), 0)
    lane = lax.broadcasted_iota(jnp.int32, (BLK, 4 * BLK), 1)
    key = lane & (BLK - 1)
    own = lane >= 2 * BLK
    return (own & (key <= row) & has_own) | (jnp.logical_not(own) & (key >= row) & has_before)


def _band_mask_before(has_before):
    row = lax.broadcasted_iota(jnp.int32, (BLK, 2 * BLK), 0)
    key = lax.broadcasted_iota(jnp.int32, (BLK, 2 * BLK), 1) & (BLK - 1)
    return (key >= row) & has_before


def _per_head(width, col_a, col_b):
    lane = lax.broadcasted_iota(jnp.int32, (1, width), 1)
    return jnp.where((lane & BLK) == 0, col_a, col_b)


NQ = ATT_W // 128


def _att_fwd(q, k, v, grp, name):
    dil = DILATIONS[grp]
    nb = T // dil // BLK

    def body(q_ref, kp_ref, kc_ref, vp_ref, vc_ref, o_ref, lse_ref, s_sc, p_sc):
        mask = _band_mask4(pl.program_id(1) > 0, True)
        low = lax.broadcasted_iota(jnp.int32, (1, 128), 1) < HD
        halves = lambda ref, j, h: (ref[j, :, h * BLK:(h + 1) * BLK], ref[j, :, (h + 2) * BLK:(h + 3) * BLK])
        for j in range(NQ):
            cols = slice(j * 128, (j + 1) * 128)
            k4 = jnp.concatenate([_split_heads(kp_ref[:, cols]), _split_heads(kc_ref[:, cols])], axis=0)
            s_sc[j] = jnp.where(mask, _nt(q_ref[:, cols], k4), -jnp.inf)
        mxs = [[jnp.maximum(*(jnp.max(t, axis=-1, keepdims=True) for t in halves(s_sc, j, h))) for h in range(2)]
               for j in range(NQ)]
        dens = []
        for j in range(NQ):
            p = jnp.exp(s_sc[j] - _per_head(4 * BLK, *mxs[j]))
            p_sc[j] = p.astype(BF16)
            dens.append([jnp.sum(p[:, h * BLK:(h + 1) * BLK], axis=-1, keepdims=True)
                         + jnp.sum(p[:, (h + 2) * BLK:(h + 3) * BLK], axis=-1, keepdims=True) for h in range(2)])
        for j in range(NQ):
            cols = slice(j * 128, (j + 1) * 128)
            v4 = jnp.concatenate([_split_heads(vp_ref[:, cols]), _split_heads(vc_ref[:, cols])], axis=0)
            o_ref[:, cols] = (_nn(p_sc[j], v4) / jnp.where(low, dens[j][0], dens[j][1])).astype(BF16)
            lse_ref[:, cols] = jnp.where(low, mxs[j][0] + jnp.log(dens[j][0]), mxs[j][1] + jnp.log(dens[j][1]))

    cur = pl.BlockSpec((None, BLK, ATT_W), lambda r, i: (r, i, 0))
    prev = pl.BlockSpec((None, BLK, ATT_W), lambda r, i: (r, jnp.maximum(i - 1, 0), 0))
    return pl.pallas_call(
        body, name=name, grid=(dil, nb),
        in_specs=[cur, prev, cur, prev, cur],
        out_specs=[cur, cur], out_shape=[_res_shape(grp, BF16), _res_shape(grp, F32)],
        scratch_shapes=[pltpu.VMEM((NQ, BLK, 4 * BLK), F32), pltpu.VMEM((NQ, BLK, 4 * BLK), BF16)],
        compiler_params=pltpu.CompilerParams(dimension_semantics=("parallel", "arbitrary")),
    )(q, k, k, v, v)


def _att_bwd(q, k, v, datt, att, lse, grp, name):
    dil = DILATIONS[grp]
    nb = T // dil // BLK
    scale = HD ** -0.5

    def body(q0_ref, q1_ref, kp_ref, kc_ref, vp_ref, vc_ref, do0_ref, do1_ref, o0_ref, o1_ref, l0_ref, l1_ref,
             dq_ref, dk_ref, dv_ref, k4_sc, v4_sc, s0_sc, s1_sc, dp0_sc, dp1_sc, p_sc, ds_sc):
        i = pl.program_id(1)
        mask_mine = _band_mask4(i > 0, True)
        mask_next = _band_mask_before(i < nb - 1)
        low = lax.broadcasted_iota(jnp.int32, (1, 128), 1) < HD
        for j in range(NQ):
            cols = slice(j * 128, (j + 1) * 128)
            k4_sc[j, :2 * BLK] = _split_heads(kp_ref[:, cols])
            k4_sc[j, 2 * BLK:] = _split_heads(kc_ref[:, cols])
            v4_sc[j, :2 * BLK] = _split_heads(vp_ref[:, cols])
            v4_sc[j, 2 * BLK:] = _split_heads(vc_ref[:, cols])
        for j in range(NQ):
            cols = slice(j * 128, (j + 1) * 128)
            s0_sc[j] = _nt(q0_ref[:, cols], k4_sc[j])
            s1_sc[j] = _nt(q1_ref[:, cols], k4_sc[j, 2 * BLK:])
            dp0_sc[j] = _nt(do0_ref[:, cols].astype(BF16), v4_sc[j])
            dp1_sc[j] = _nt(do1_ref[:, cols].astype(BF16), v4_sc[j, 2 * BLK:])
        stats = []
        for j in range(NQ):
            cols = slice(j * 128, (j + 1) * 128)
            for do_ref, o_ref, l_ref in ((do0_ref, o0_ref, l0_ref), (do1_ref, o1_ref, l1_ref)):
                prod = do_ref[:, cols].astype(F32) * o_ref[:, cols].astype(F32)
                d_all = jnp.sum(prod, axis=-1, keepdims=True)
                d_low = jnp.sum(jnp.where(low, prod, 0.0), axis=-1, keepdims=True)
                lse_t = l_ref[:, cols]
                stats.append((d_low, d_all - d_low, lse_t[:, 0:1], lse_t[:, HD:HD + 1]))
        for j in range(NQ):
            (da, db, la, lb), (da1, db1, la1, lb1) = stats[2 * j], stats[2 * j + 1]
            p0 = jnp.where(mask_mine, jnp.exp(s0_sc[j] - _per_head(4 * BLK, la, lb)), 0.0)
            ds0 = p0 * (dp0_sc[j] - _per_head(4 * BLK, da, db))
            p1 = jnp.where(mask_next, jnp.exp(s1_sc[j] - _per_head(2 * BLK, la1, lb1)), 0.0)
            ds1 = p1 * (dp1_sc[j] - _per_head(2 * BLK, da1, db1))
            p_sc[j, :BLK] = p0.astype(BF16)
            ds_sc[j, :BLK] = ds0.astype(BF16)
            p_sc[j, BLK:, 2 * BLK:] = p1.astype(BF16)
            ds_sc[j, BLK:, 2 * BLK:] = ds1.astype(BF16)
        for j in range(NQ):
            cols = slice(j * 128, (j + 1) * 128)
            dq_ref[:, cols] = _nn(ds_sc[j, :BLK], k4_sc[j]) * scale
            qq = jnp.concatenate([q0_ref[:, cols], q1_ref[:, cols]], axis=0)
            dd = jnp.concatenate([do0_ref[:, cols], do1_ref[:, cols]], axis=0).astype(BF16)
            dk_ref[:, cols] = _join_heads(_tn(ds_sc[j, :, 2 * BLK:], qq))
            dv_ref[:, cols] = _join_heads(_tn(p_sc[j, :, 2 * BLK:], dd))

    def spec(shift):
        return pl.BlockSpec((None, BLK, ATT_W), lambda r, i: (r, jnp.clip(i + shift, 0, nb - 1), 0))

    here, after, before = spec(0), spec(1), spec(-1)
    vm = pltpu.VMEM
    return pl.pallas_call(
        body, name=name, grid=(dil, nb),
        in_specs=[here, after, before, here, before, here, here, after, here, after, here, after],
        out_specs=[here] * 3, out_shape=[_res_shape(grp, F32)] * 3,
        scratch_shapes=[vm((NQ, 4 * BLK, 128), BF16), vm((NQ, 4 * BLK, 128), BF16), vm((NQ, BLK, 4 * BLK), F32),
                        vm((NQ, BLK, 2 * BLK), F32), vm((NQ, BLK, 4 * BLK), F32), vm((NQ, BLK, 2 * BLK), F32),
                        vm((NQ, 2 * BLK, 4 * BLK), BF16), vm((NQ, 2 * BLK, 4 * BLK), BF16)],
        compiler_params=pltpu.CompilerParams(dimension_semantics=("parallel", "arbitrary")),
    )(q, q, k, k, v, v, datt, datt, att, att, lse, lse)


def _att_merge(os_, lses, proj):
    nq = ATT_W // 128

    def body(o0, o1, o2, l0, l1, l2, za_ref, att_ref, lse_ref, ain_ref, sc):
        for a, ref in enumerate((o0, o1, o2, l0, l1, l2)):
            for j in range(nq):
                _from_residues(ref, slice(j * 128, (j + 1) * 128), sc, a * nq + j, DILATIONS[a % 3])
        for j in range(nq):
            cols = slice(j * 128, (j + 1) * 128)
            oa, ob, oc = (sc[a * nq + j] for a in range(3))
            la, lb, lc = (sc[(3 + a) * nq + j] for a in range(3))
            m = jnp.maximum(jnp.maximum(la, lb), lc)
            wa, wb, wc = jnp.exp(la - m), jnp.exp(lb - m), jnp.exp(lc - m)
            tot = wa + wb + wc
            att = (wa * oa + wb * ob + wc * oc) / tot
            att_ref[:, cols] = att
            lse_ref[:, cols] = m + jnp.log(tot)
            za = za_ref[:, cols]
            ain_ref[:, cols] = (att * za * _sigmoid(za)).astype(BF16)

    return pl.pallas_call(
        body, name="att_merge", grid=(T // TT,),
        in_specs=[_res_spec(g) for _ in range(2) for g in range(3)] + [_tok_spec(ATT_W, C_ZA // ATT_W)],
        out_specs=[_tok_spec(ATT_W)] * 3,
        out_shape=[S((T, ATT_W), F32), S((T, ATT_W), F32), S((T, ATT_W), BF16)],
        scratch_shapes=[pltpu.VMEM((6 * nq, TT, 128), F32)],
        compiler_params=pltpu.CompilerParams(dimension_semantics=("arbitrary",)),
    )(*os_, *lses, proj)


def _att_gate_bwd(dain, att, lse, proj, dproj):
    nq = ATT_W // 128

    def body(d_ref, att_ref, lse_ref, za_ref, buf_ref, dza_ref, da0, da1, da2, at1, at2, ls1, ls2, sc):
        del buf_ref
        for j in range(nq):
            cols = slice(j * 128, (j + 1) * 128)
            za = za_ref[:, cols]
            sg = _sigmoid(za)
            d = d_ref[:, cols].astype(F32)
            att_ = att_ref[:, cols]
            dza_ref[:, cols] = (d * att_ * sg * (1.0 + za * (1.0 - sg))).astype(BF16)
            sc[j] = d * za * sg
            sc[nq + j] = att_
            sc[2 * nq + j] = lse_ref[:, cols]
        for j in range(nq):
            cols = slice(j * 128, (j + 1) * 128)
            for grp, dst in enumerate((da0, da1, da2)):
                _to_residues(sc, j, dst, DILATIONS[grp], cols)
            for grp, dst in ((1, at1), (2, at2)):
                _to_residues(sc, nq + j, dst, DILATIONS[grp], cols)
            for grp, dst in ((1, ls1), (2, ls2)):
                _to_residues(sc, 2 * nq + j, dst, DILATIONS[grp], cols)

    res = (0, 1, 2, 1, 2, 1, 2)
    return pl.pallas_call(
        body, name="att_gate_bwd", grid=(T // TT,),
        in_specs=[_tok_spec(ATT_W)] * 3 + [_tok_spec(ATT_W, C_ZA // ATT_W), pl.BlockSpec(memory_space=pl.ANY)],
        out_specs=[_tok_spec(ATT_W, C_ZA // ATT_W)] + [_res_spec(g) for g in res],
        out_shape=[S(dproj.shape, dproj.dtype)] + [_res_shape(g, BF16) for g in res[:5]]
        + [_res_shape(g, F32) for g in res[5:]],
        input_output_aliases={4: 0},
        scratch_shapes=[pltpu.VMEM((3 * nq, TT, 128), F32)],
        compiler_params=pltpu.CompilerParams(dimension_semantics=("arbitrary",)),
    )(dain, att, lse, proj, dproj)


def _split3(v):
    hi = v.astype(BF16)
    r1 = v - hi.astype(F32)
    mid = r1.astype(BF16)
    lo = (r1 - mid.astype(F32)).astype(BF16)
    return hi, mid, lo


def _chunk_scores(qt, kt, q_ref, k_ref, h):
    cols = slice(h * GDK, (h + 1) * GDK)
    own = jnp.sum(q_ref[:, cols] * (GDK ** -0.5) * k_ref[:, cols], axis=-1, keepdims=True)
    row = lax.broadcasted_iota(jnp.int32, (GLA_C, GLA_C), 0)
    col = lax.broadcasted_iota(jnp.int32, (GLA_C, GLA_C), 1)
    a = _nt(qt.astype(BF16), kt.astype(BF16))
    return jnp.where(col < row, a, jnp.where(col == row, own, 0.0))


def _tri_sum(v, upper):
    n = v.shape[0]
    row = lax.broadcasted_iota(jnp.int32, (n, n), 0)
    col = lax.broadcasted_iota(jnp.int32, (n, n), 1)
    tri = jnp.where(col >= row if upper else col <= row, 1.0, 0.0).astype(BF16)
    hi, mid, lo = _split3(v)
    return _nn(tri, hi) + _nn(tri, mid) + _nn(tri, lo)


def _gla_gates(glr_ref, w2_ref, b_ref):
    logit = _nn(glr_ref[...].astype(BF16), w2_ref[...]) + b_ref[...]
    lg = (jnp.minimum(logit, 0.0) - jnp.log(1.0 + jnp.exp(-jnp.abs(logit)))) * (1.0 / GLA_TAU)
    return logit, _tri_sum(lg, upper=False)


def _gla_head(cum, q_ref, k_ref, h):
    cols = slice(h * GDK, (h + 1) * GDK)
    b = cum[:, cols]
    last = b[GLA_C - 1:GLA_C, :]
    e_pos = jnp.exp(b)
    e_neg = jnp.exp(-b)
    e_end = jnp.exp(last - b)
    qt = q_ref[:, cols] * (GDK ** -0.5) * e_pos
    kt = k_ref[:, cols] * e_neg
    kh = k_ref[:, cols] * e_end
    return b, last, e_pos, e_neg, e_end, qt, kt, kh


def _causal(n):
    return lax.broadcasted_iota(jnp.int32, (n, n), 1) <= lax.broadcasted_iota(jnp.int32, (n, n), 0)


def _gla_fwd(proj, w2p, bg, gn):
    nc = T // GLA_C

    def body(q_ref, k_ref, v_ref, glr_ref, zg_ref, w2_ref, b_ref, gn_ref, o_ref, bin_ref, st_ref, state):
        @pl.when(pl.program_id(0) == 0)
        def _():
            state[...] = jnp.zeros_like(state)

        _, cum = _gla_gates(glr_ref, w2_ref, b_ref)
        for h in range(GH):
            _, last, _, _, _, qt, kt, kh = _gla_head(cum, q_ref, k_ref, h)
            vcols = slice(h * GDV, (h + 1) * GDV)
            st = state[h]
            st_ref[0, h] = st
            v = v_ref[:, vcols].astype(BF16)
            qb = qt.astype(BF16)
            a = _chunk_scores(qt, kt, q_ref, k_ref, h)
            o = _nt(qb, st.astype(BF16)) + _nn(a.astype(BF16), v)
            state[h] = st * jnp.exp(last) + _tn(v, kh.astype(BF16))
            o_ref[:, vcols] = o
            r = lax.rsqrt(jnp.mean(o * o, axis=-1, keepdims=True) + EPS)
            zg = zg_ref[:, vcols]
            bin_ref[:, vcols] = (o * r * gn_ref[...] * zg * _sigmoid(zg)).astype(BF16)

    row = lambda width, cblk: pl.BlockSpec((GLA_C, width), functools.partial(lambda i, c: (i, c), c=cblk))
    full = lambda a: pl.BlockSpec(a.shape, functools.partial(lambda i, nd: (0,) * nd, nd=a.ndim))
    return pl.pallas_call(
        body, name="gla_fwd", grid=(nc,),
        in_specs=[row(512, C_QG // 512), row(512, C_KG // 512), row(1024, C_VG // 1024), row(GLR_W, C_GLR // GLR_W),
                  row(1024, C_ZG // 1024), full(w2p), full(bg), full(gn)],
        out_specs=[pl.BlockSpec((GLA_C, GH * GDV), lambda i: (i, 0)), pl.BlockSpec((GLA_C, GH * GDV), lambda i: (i, 0)),
                   pl.BlockSpec((1, GH, GDV, GDK), lambda i: (i, 0, 0, 0))],
        out_shape=[S((T, GH * GDV), F32), S((T, GH * GDV), BF16), S((nc, GH, GDV, GDK), F32)],
        scratch_shapes=[pltpu.VMEM((GH, GDV, GDK), F32)],
        compiler_params=pltpu.CompilerParams(dimension_semantics=("arbitrary",)),
    )(proj, proj, proj, proj, proj, w2p, bg, gn)


def _gla_bwd(proj, w2p, bg, gn, o_gla, states, dbin, dproj):
    nc = T // GLA_C

    def body(q_ref, k_ref, v_ref, glr_ref, zg_ref, w2_ref, b_ref, gn_ref, o_ref, st_ref, dbin_ref, buf_ref,
             out_ref, dw2_ref, dbg_ref, dgn_ref, dstate, dlogit):
        del buf_ref
        dq_ref = out_ref.at[:, C_QG:C_KG]
        dk_ref = out_ref.at[:, C_KG:C_VG]
        dv_ref = out_ref.at[:, C_VG:C_ZG]
        dzg_ref = out_ref.at[:, C_ZG:C_GLR]
        dglr_ref = out_ref.at[:, C_GLR:C_GLR + GLR_W]
        first = pl.program_id(0) == 0

        @pl.when(first)
        def _():
            dstate[...] = jnp.zeros_like(dstate)

        logit, cum = _gla_gates(glr_ref, w2_ref, b_ref)
        is_last = lax.broadcasted_iota(jnp.int32, (GLA_C, 1), 0) == GLA_C - 1
        dgn = jnp.zeros((1, GDV), F32)
        for h in range(GH):
            _, last, e_pos, e_neg, e_end, qt, kt, kh = _gla_head(cum, q_ref, k_ref, h)
            cols = slice(h * GDK, (h + 1) * GDK)
            vcols = slice(h * GDV, (h + 1) * GDV)
            o = o_ref[:, vcols]
            r = lax.rsqrt(jnp.mean(o * o, axis=-1, keepdims=True) + EPS)
            zg = zg_ref[:, vcols]
            sg = _sigmoid(zg)
            db_ = dbin_ref[:, vcols].astype(F32)
            dlin = db_ * zg * sg
            dzg_ref[:, vcols] = (db_ * (o * r * gn_ref[...]) * sg * (1.0 + zg * (1.0 - sg))).astype(BF16)
            u = dlin * gn_ref[...]
            do = (r * u - o * (r * r * r) * jnp.mean(u * o, axis=-1, keepdims=True)).astype(BF16)
            dgn = dgn + jnp.sum(dlin * o * r, axis=0, keepdims=True)
            st = st_ref[0, h]
            dst = dstate[h]
            v = v_ref[:, vcols].astype(BF16)
            qb, kb, khb = qt.astype(BF16), kt.astype(BF16), kh.astype(BF16)
            dstb = dst.astype(BF16)
            causal = _causal(GLA_C)
            a = _chunk_scores(qt, kt, q_ref, k_ref, h).astype(BF16)
            da = jnp.where(causal, _nt(do, v), 0.0).astype(BF16)
            dqt = _nn(do, st.astype(BF16)) + _nn(da, kb)
            dkt = _tn(da, qb)
            dkh = _nn(v, dstb)
            dv_ref[:, vcols] = (_tn(a, do) + _nt(khb, dstb)).astype(BF16)
            lam = jnp.exp(last)
            dlam = jnp.sum(dst * st, axis=0, keepdims=True)
            dstate[h] = dst * lam + _tn(do, qb)
            dq_ref[:, cols] = (dqt * e_pos * (GDK ** -0.5)).astype(BF16)
            dk_ref[:, cols] = (dkt * e_neg + dkh * e_end).astype(BF16)
            dkh_kh = dkh * kh
            dcum = dqt * qt - dkt * kt - dkh_kh
            dlast = jnp.sum(dkh_kh, axis=0, keepdims=True) + dlam * lam
            dcum = jnp.where(is_last, dcum + dlast, dcum)
            dlg = _tri_sum(dcum, upper=True)
            dlogit[:, cols] = dlg * (1.0 / GLA_TAU) * (1.0 - _sigmoid(logit[:, cols]))

        dl = dlogit[...]
        dlb = dl.astype(BF16)
        dglr_ref[...] = _nt(dlb, w2_ref[...]).astype(BF16)
        dw2 = _tn(glr_ref[...].astype(BF16), dlb)
        dbg = jnp.sum(dl, axis=0, keepdims=True)

        @pl.when(first)
        def _():
            dw2_ref[...] = dw2
            dbg_ref[...] = dbg
            dgn_ref[...] = dgn

        @pl.when(jnp.logical_not(first))
        def _():
            dw2_ref[...] += dw2
            dbg_ref[...] += dbg
            dgn_ref[...] += dgn

    rev = lambda i: nc - 1 - i
    row = lambda width, cblk: pl.BlockSpec((GLA_C, width), functools.partial(lambda i, c: (rev(i), c), c=cblk))
    full = lambda a: pl.BlockSpec(a.shape, functools.partial(lambda i, nd: (0,) * nd, nd=a.ndim))
    keep = lambda shape: pl.BlockSpec(shape, functools.partial(lambda i, nd: (0,) * nd, nd=len(shape)))
    return pl.pallas_call(
        body, name="gla_bwd", grid=(nc,),
        in_specs=[row(512, C_QG // 512), row(512, C_KG // 512), row(1024, C_VG // 1024), row(GLR_W, C_GLR // GLR_W),
                  row(1024, C_ZG // 1024), full(w2p), full(bg), full(gn), row(GH * GDV, 0),
                  pl.BlockSpec((1, GH, GDV, GDK), lambda i: (rev(i), 0, 0, 0)), row(GH * GDV, 0),
                  pl.BlockSpec(memory_space=pl.ANY)],
        out_specs=[row(GLA_GROUP_W, 0), keep((GLR_W, 512)), keep((1, 512)), keep((1, GDV))],
        out_shape=[S(dproj.shape, dproj.dtype), S((GLR_W, 512), F32), S((1, 512), F32), S((1, GDV), F32)],
        input_output_aliases={11: 0},
        scratch_shapes=[pltpu.VMEM((GH, GDV, GDK), F32), pltpu.VMEM((GLA_C, GH * GDK), F32)],
        compiler_params=pltpu.CompilerParams(dimension_semantics=("arbitrary",)),
    )(proj, proj, proj, proj, proj, w2p, bg, gn, o_gla, states, dbin, dproj)


RT = 512


def _rowchain(body, name, ins, outs, scratch=()):
    in_specs, args = [], []
    for spec in ins:
        if spec[0] == "tok":
            _, arr, width, cblk = spec
            in_specs.append(pl.BlockSpec((RT, width), functools.partial(lambda i, c: (i, c), c=cblk)))
        else:
            arr = spec[1]
            in_specs.append(pl.BlockSpec(arr.shape, functools.partial(lambda i, nd: (0,) * nd, nd=arr.ndim)))
        args.append(arr)
    out_specs, out_shape = [], []
    for spec in outs:
        if spec[0] == "tok":
            _, shape, dtype, width, cblk = spec
            out_specs.append(pl.BlockSpec((RT, width), functools.partial(lambda i, c: (i, c), c=cblk)))
        else:
            _, shape, dtype = spec
            out_specs.append(pl.BlockSpec(shape, functools.partial(lambda i, nd: (0,) * nd, nd=len(shape))))
        out_shape.append(S(shape, dtype))
    return pl.pallas_call(
        body, name=name, grid=(T // RT,), in_specs=in_specs, out_specs=out_specs, out_shape=out_shape,
        scratch_shapes=list(scratch), compiler_params=pltpu.CompilerParams(dimension_semantics=("arbitrary",)),
    )(*args)


def _tok(arr, width=None, cblk=0):
    return ("tok", arr, arr.shape[1] if width is None else width, cblk)


def _tok_out(dtype, width=D):
    return ("tok", (T, width), dtype, width, 0)


def _branches_fwd(ain, bin_, proj, x, w_att, w_gla, w_out):
    def body(ain_ref, bin_ref, g_ref, x_ref, wa_ref, wg_ref, wo_ref, ya_ref, yb_ref, y_ref, x1_ref):
        ya = _nn(ain_ref[...], wa_ref[...]).astype(BF16)
        yb = _nn(bin_ref[...], wg_ref[...]).astype(BF16)
        ya_ref[...] = ya
        yb_ref[...] = yb
        y = (_sigmoid(g_ref[:, :D]) * ya.astype(F32) + _sigmoid(g_ref[:, D:]) * yb.astype(F32)).astype(BF16)
        y_ref[...] = y
        x1_ref[...] = x_ref[...] + _nn(y, wo_ref[...])

    return _rowchain(body, "branches_fwd",
                     [_tok(ain), _tok(bin_), _tok(proj, 2 * D, C_GA // (2 * D)), _tok(x), ("all", w_att),
                      ("all", w_gla), ("all", w_out)],
                     [_tok_out(BF16), _tok_out(BF16), _tok_out(BF16), _tok_out(F32)])


def _accumulate(ref, part, first):
    @pl.when(first)
    def _():
        ref[...] = part

    @pl.when(jnp.logical_not(first))
    def _():
        ref[...] += part


def _ple_loss(x1, p, target, g2, w_pg, w_ple):
    def body(x1_ref, p_ref, t_ref, g_ref, wpg_ref, wple_ref, n2_ref, loss_ref, dout_ref, du_ref, dwple_ref, acc):
        first = pl.program_id(0) == 0
        x1 = x1_ref[...]
        r = lax.rsqrt(jnp.mean(x1 * x1, axis=-1, keepdims=True) + EPS)
        n2 = (x1 * r * g_ref[...]).astype(BF16)
        n2_ref[...] = n2
        pg = _sigmoid(_nn(n2, wpg_ref[...]))
        pb = p_ref[...].astype(BF16)
        e_ = _nn(pb, wple_ref[...])
        diff = x1 + e_ * pg - t_ref[...]
        _accumulate(acc, jnp.sum(diff * diff, axis=0, keepdims=True), first)
        dout = diff * (1.0 / D)
        dout_ref[...] = dout
        du_ref[...] = (dout * e_ * pg * (1.0 - pg)).astype(BF16)
        _accumulate(dwple_ref, _tn(pb, (dout * pg).astype(BF16)), first)
        loss_ref[...] = jnp.zeros((1, 128), F32) + jnp.sum(acc[...], axis=-1, keepdims=True) * (0.5 / D)

    return _rowchain(body, "ple_loss", [_tok(x1), _tok(p), _tok(target), ("all", g2), ("all", w_pg), ("all", w_ple)],
                     [_tok_out(BF16), ("acc", (1, 128), F32), _tok_out(F32), _tok_out(BF16), ("acc", (PLE, D), F32)],
                     scratch=[pltpu.VMEM((1, D), F32)])


def _ple_bwd(du, n2, y, x1, dout, g2, w_pg, w_out):
    def body(du_ref, n2_ref, y_ref, x1_ref, dout_ref, g_ref, wpg_ref, wo_ref, dx_ref, dy_ref, dg_ref, dwpg_ref,
             dwo_ref):
        first = pl.program_id(0) == 0
        x1 = x1_ref[...]
        r = lax.rsqrt(jnp.mean(x1 * x1, axis=-1, keepdims=True) + EPS)
        du_ = du_ref[...]
        dn = _nt(du_, wpg_ref[...])
        u = dn * g_ref[...]
        dx = dout_ref[...] + r * u - x1 * (r * r * r) * jnp.mean(u * x1, axis=-1, keepdims=True)
        dxb = dx.astype(BF16)
        dx_ref[...] = dx
        dy_ref[...] = _nt(dxb, wo_ref[...]).astype(BF16)
        _accumulate(dg_ref, jnp.sum(dn * x1 * r, axis=0, keepdims=True), first)
        _accumulate(dwpg_ref, _tn(n2_ref[...], du_), first)
        _accumulate(dwo_ref, _tn(y_ref[...], dxb), first)

    return _rowchain(body, "ple_bwd",
                     [_tok(du), _tok(n2), _tok(y), _tok(x1), _tok(dout), ("all", g2), ("all", w_pg), ("all", w_out)],
                     [_tok_out(F32), _tok_out(BF16), ("acc", (1, D), F32), ("acc", (D, D), F32), ("acc", (D, D), F32)])


def _branches_bwd(dy, ya, yb, ain, bin_, proj, w_att, w_gla):
    def body(dy_ref, ya_ref, yb_ref, ain_ref, bin_ref, g_ref, wa_ref, wg_ref, dg_ref, dain_ref, dbin_ref,
             dwa_ref, dwg_ref):
        first = pl.program_id(0) == 0
        dy_ = dy_ref[...].astype(F32)
        sa, sb = _sigmoid(g_ref[:, :D]), _sigmoid(g_ref[:, D:])
        dg_ref[:, :D] = (dy_ * ya_ref[...].astype(F32) * sa * (1.0 - sa)).astype(BF16)
        dg_ref[:, D:] = (dy_ * yb_ref[...].astype(F32) * sb * (1.0 - sb)).astype(BF16)
        dya = (dy_ * sa).astype(BF16)
        dyb = (dy_ * sb).astype(BF16)
        dain_ref[...] = _nt(dya, wa_ref[...]).astype(BF16)
        dbin_ref[...] = _nt(dyb, wg_ref[...]).astype(BF16)
        _accumulate(dwa_ref, _tn(ain_ref[...], dya), first)
        _accumulate(dwg_ref, _tn(bin_ref[...], dyb), first)

    gates = C_GA // (2 * D)
    return _rowchain(body, "branches_bwd",
                     [_tok(dy), _tok(ya), _tok(yb), _tok(ain), _tok(bin_), _tok(proj, 2 * D, gates), ("all", w_att),
                      ("all", w_gla)],
                     [("tok", (T, NCOL), BF16, 2 * D, gates), _tok_out(BF16, ATT_W), _tok_out(BF16),
                      ("acc", (ATT_W, D), F32), ("acc", (D, D), F32)])


def _peer(k):
    x, y, c = lax.axis_index("x"), lax.axis_index("y"), lax.axis_index("c")
    return (x ^ ((k >> 2) & 1), y ^ ((k >> 1) & 1), c ^ (k & 1))


def _my_index():
    return 4 * lax.axis_index("x") + 2 * lax.axis_index("y") + lax.axis_index("c")


def _peer_index(k):
    px, py, pc = _peer(k)
    return 4 * px + 2 * py + pc


def _pairwise_plan(src_of, dst_of, landed_of, own_src, own_dst):
    def plan(ins, outs, send, recv, local):
        n = len(ins)

        def own():
            return [pltpu.make_async_copy(own_src(ins[a]), own_dst(outs[a]), local.at[a]) for a in range(n)]

        def remote(k, a, src, dst):
            return pltpu.make_async_remote_copy(src_ref=src, dst_ref=dst, send_sem=send.at[k - 1, a],
                                                recv_sem=recv.at[k - 1, a], device_id=_peer(k), device_id_type=MESH)

        def sent():
            return [remote(k, a, src_of(ins[a], k), dst_of(outs[a])) for k in range(1, NDEV) for a in range(n)]

        def start():
            for cp in own() + sent():
                cp.start()

        def finish():
            for k in range(1, NDEV):
                for a in range(n):
                    remote(k, a, own_src(ins[a]), landed_of(outs[a], k)).wait_recv()
            for cp in sent():
                cp.wait_send()
            for cp in own():
                cp.wait()

        return start, finish

    return plan


def _pairwise_sems(n):
    return [pltpu.SemaphoreType.DMA((NDEV - 1, n)), pltpu.SemaphoreType.DMA((NDEV - 1, n)),
            pltpu.SemaphoreType.DMA((n,))]


def _gather_side(arrs):
    plan = _pairwise_plan(src_of=lambda i, k: i, dst_of=lambda o: o.at[_my_index()],
                          landed_of=lambda o, k: o.at[_peer_index(k)],
                          own_src=lambda i: i, own_dst=lambda o: o.at[_my_index()])
    return dict(arrs=arrs, out_shape=[S((NDEV,) + a.shape, a.dtype) for a in arrs],
                scratch=_pairwise_sems(len(arrs)), plan=plan)


def _exchange_side(arrs):
    plan = _pairwise_plan(src_of=lambda i, k: i.at[_peer_index(k)], dst_of=lambda o: o.at[_my_index()],
                          landed_of=lambda o, k: o.at[_peer_index(k)],
                          own_src=lambda i: i.at[_my_index()], own_dst=lambda o: o.at[_my_index()])
    return dict(arrs=arrs, out_shape=[S(a.shape, a.dtype) for a in arrs], scratch=_pairwise_sems(len(arrs)), plan=plan)


def _comm_call(side, name):
    n = len(side["arrs"])

    def body(*refs):
        start, finish = side["plan"](refs[:n], refs[n:2 * n], *refs[2 * n:])
        start()
        finish()

    hbm = pl.BlockSpec(memory_space=pl.ANY)
    return pl.pallas_call(body, name=name, in_specs=[hbm] * n, out_specs=[hbm] * n, out_shape=side["out_shape"],
                          scratch_shapes=side["scratch"])(*side["arrs"])


def _all_gather_by_chip(arrs, name):
    n = len(arrs)

    def body(*refs):
        ins, outs = refs[:n], refs[n:2 * n]
        send, recv, local = refs[2 * n:]
        x, y, c = lax.axis_index("x"), lax.axis_index("y"), lax.axis_index("c")
        me, sibling = (x, y, c), (x, y, 1 - c)
        chips = [(1 - x, y), (x, 1 - y), (1 - x, 1 - y)]

        def copy(k, a, block, to, src=None):
            px, py, pc = block
            slot = outs[a].at[4 * px + 2 * py + pc]
            return pltpu.make_async_remote_copy(
                src_ref=slot if src is None else src, dst_ref=slot, send_sem=send.at[k, a], recv_sem=recv.at[k, a],
                device_id=to, device_id_type=MESH)

        north = c == 1
        via = (jnp.where(north, 1 - x, x), jnp.where(north, y, 1 - y))
        onward = (jnp.where(north, x, 1 - x), jnp.where(north, 1 - y, y), c)
        mine = [pltpu.make_async_copy(ins[a], outs[a].at[4 * x + 2 * y + c], local.at[a]) for a in range(n)]
        first = []
        for a in range(n):
            first.append(copy(0, a, me, sibling, src=ins[a]))
            first += [copy(1 + j, a, me, (*chips[j], c), src=ins[a]) for j in range(2)]
        for cp in mine + first:
            cp.start()
        passed = []
        for j in range(2):
            for a in range(n):
                copy(1 + j, a, (*chips[j], c), me).wait_recv()
                passed.append(copy(4 + j, a, (*chips[j], c), sibling))
                passed[-1].start()
        for a in range(n):
            passed.append(copy(3, a, (*via, c), onward))
            passed[-1].start()
        for a in range(n):
            copy(3, a, (*chips[2], c), me).wait_recv()
            passed.append(copy(6, a, (*chips[2], c), sibling))
            passed[-1].start()
        for a in range(n):
            copy(0, a, sibling, me).wait_recv()
        for j, chip in enumerate(chips):
            for a in range(n):
                copy(4 + j, a, (*chip, 1 - c), me).wait_recv()
        for cp in first + passed:
            cp.wait_send()
        for cp in mine:
            cp.wait()

    hbm = pl.BlockSpec(memory_space=pl.ANY)
    return pl.pallas_call(
        body, name=name, in_specs=[hbm] * n, out_specs=[hbm] * n,
        out_shape=[S((NDEV,) + a.shape, a.dtype) for a in arrs],
        scratch_shapes=[pltpu.SemaphoreType.DMA((NDEV - 1, n)), pltpu.SemaphoreType.DMA((NDEV - 1, n)),
                        pltpu.SemaphoreType.DMA((n,))],
    )(*arrs)


NCHIP = 4


def _sibling_sum(src, name, tc=512):
    _, rows, cols = src.shape
    assert cols % tc == 0

    def body(src_ref, got_ref, out_ref, a_buf, b_buf, o_buf, send, recv, local):
        x, y, c = lax.axis_index("x"), lax.axis_index("y"), lax.axis_index("c")
        copies = [pltpu.make_async_remote_copy(
            src_ref=src_ref.at[2 * q + (1 - c)], dst_ref=got_ref.at[q], send_sem=send.at[q], recv_sem=recv.at[q],
            device_id=(x, y, 1 - c), device_id_type=MESH) for q in range(NCHIP)]
        for cp in copies:
            cp.start()
        tiles = [(q, pl.ds(t * tc, tc)) for q in range(NCHIP) for t in range(cols // tc)]

        def loads(n):
            q, tile = tiles[n]
            return [pltpu.make_async_copy(src_ref.at[2 * q + c, :, tile], a_buf.at[n % 2], local.at[n % 2, 0]),
                    pltpu.make_async_copy(got_ref.at[q, :, tile], b_buf.at[n % 2], local.at[n % 2, 1])]

        def store(n):
            q, tile = tiles[n]
            return pltpu.make_async_copy(o_buf.at[n % 2], out_ref.at[q, :, tile], local.at[n % 2, 2])

        def fetch(n):
            if n == 0 or tiles[n][0] != tiles[n - 1][0]:
                copies[tiles[n][0]].wait_recv()
            for cp in loads(n):
                cp.start()

        fetch(0)
        for n in range(len(tiles)):
            if n + 1 < len(tiles):
                fetch(n + 1)
            for cp in loads(n):
                cp.wait()
            if n >= 2:
                store(n - 2).wait()
            o_buf[n % 2] = (a_buf[n % 2].astype(F32) + b_buf[n % 2].astype(F32)).astype(BF16)
            store(n).start()
        store(len(tiles) - 2).wait()
        store(len(tiles) - 1).wait()
        for cp in copies:
            cp.wait_send()

    hbm = pl.BlockSpec(memory_space=pl.ANY)
    block = S((NCHIP, rows, cols), BF16)
    return pl.pallas_call(
        body, name=name, in_specs=[hbm], out_specs=[hbm, hbm], out_shape=[block, block],
        scratch_shapes=[pltpu.VMEM((2, rows, tc), BF16)] * 3
        + [pltpu.SemaphoreType.DMA((NCHIP,)), pltpu.SemaphoreType.DMA((NCHIP,)), pltpu.SemaphoreType.DMA((2, 3))],
    )(src)[1]


def _chips_side(arrs):
    def plan(ins, outs, send, recv, local):
        n = len(ins)

        def places():
            x, y, c = lax.axis_index("x"), lax.axis_index("y"), lax.axis_index("c")
            return 2 * x + y, c, [(1 - x, y), (x, 1 - y), (1 - x, 1 - y)]

        def own():
            here, _, _ = places()
            return [pltpu.make_async_copy(ins[a].at[here], outs[a].at[here], local.at[a]) for a in range(n)]

        def remote(j, a, src_slot, dst_slot):
            _, c, chips = places()
            cx, cy = chips[j]
            return pltpu.make_async_remote_copy(
                src_ref=ins[a].at[src_slot], dst_ref=outs[a].at[dst_slot], send_sem=send.at[j, a],
                recv_sem=recv.at[j, a], device_id=(cx, cy, c), device_id_type=MESH)

        def sent():
            here, _, chips = places()
            return [remote(j, a, 2 * cx + cy, here) for j, (cx, cy) in enumerate(chips) for a in range(n)]

        def start():
            for cp in own() + sent():
                cp.start()

        def finish():
            here, _, chips = places()
            for j, (cx, cy) in enumerate(chips):
                for a in range(n):
                    remote(j, a, here, 2 * cx + cy).wait_recv()
            for cp in sent():
                cp.wait_send()
            for cp in own():
                cp.wait()

        return start, finish

    n = len(arrs)
    return dict(arrs=arrs, out_shape=[S(a.shape, a.dtype) for a in arrs],
                scratch=[pltpu.SemaphoreType.DMA((NCHIP - 1, n)), pltpu.SemaphoreType.DMA((NCHIP - 1, n)),
                         pltpu.SemaphoreType.DMA((n,))], plan=plan)


def _adamw_shards(parts, places):
    n_src = len(parts)

    def body(*refs):
        srcs, rest = refs[:n_src], refs[n_src:]
        for j, (src, rows, cols, _) in enumerate(places):
            w_ref, m_ref, v_ref = rest[3 * j:3 * j + 3]
            outs = rest[3 * len(places) + 4 * j:3 * len(places) + 4 * j + 4]
            p_ref = srcs[src]
            g = p_ref[0, rows, cols].astype(F32)
            for s in range(1, p_ref.shape[0]):
                g = g + p_ref[s, rows, cols].astype(F32)
            delta, m_new, v_new = _adam_math(g, w_ref[0], m_ref[0], v_ref[0])
            for ref, val in zip(outs, (g, delta, m_new, v_new)):
                ref[0] = val

    flat = [a for place in places for a in place[3]]
    return pl.pallas_call(
        body, name="adam_shards",
        out_shape=[S(place[3][0].shape, F32) for place in places for _ in range(4)],
    )(*parts, *flat)


def _adam_math(g, w, m, v):
    c1 = 1.0 - ADAM_B1 ** ADAM_STEP
    c2 = 1.0 - ADAM_B2 ** ADAM_STEP
    m_new = ADAM_B1 * m + (1.0 - ADAM_B1) * g
    v_new = ADAM_B2 * v + (1.0 - ADAM_B2) * (g * g)
    return -ADAM_LR * ((m_new / c1) / (jnp.sqrt(v_new / c2) + ADAM_EPS) + ADAM_WD * w), m_new, v_new


def _adamw_small(parts, params, loss_parts):
    n = len(params)

    def body(*refs):
        p_refs, rest = refs[:n], refs[n + 1:]
        total = refs[n][0]
        for s in range(1, NDEV):
            total = total + refs[n][s]
        refs[-1][...] = total
        for j in range(n):
            w_ref, m_ref, v_ref = rest[3 * j:3 * j + 3]
            g_ref, d_ref, mo_ref, vo_ref = rest[3 * n + 4 * j:3 * n + 4 * j + 4]
            width = w_ref.shape[1]
            g = p_refs[j][0]
            for s in range(1, NDEV):
                g = g + p_refs[j][s]
            g = g[:, :width]
            delta, m_new, v_new = _adam_math(g, w_ref[...], m_ref[...], v_ref[...])
            g_ref[...] = g
            d_ref[...] = delta
            mo_ref[...] = m_new
            vo_ref[...] = v_new

    flat = [a for group in params for a in group]
    return pl.pallas_call(
        body, name="adam_small",
        out_shape=[S(group[0].shape, F32) for group in params for _ in range(4)] + [S((1, 128), F32)],
    )(*parts, loss_parts, *flat)


def _adamw_rows(parts, w, m, v, name, tc=256):
    rows, _, cols = w.shape
    nparts = parts.shape[0]
    nsteps = cols // tc

    def body(p_ref, w_hbm, m_hbm, v_hbm, g_hbm, d_hbm, mo_hbm, vo_hbm, inbuf, outbuf, insem, outsem):
        i = pl.program_id(0)
        slot = i & 1

        def view(ref, step):
            return ref.at[:, 0, pl.ds(pl.multiple_of(step * tc, tc), tc)]

        def fetch(step, sl):
            return [pltpu.make_async_copy(view(src, step), inbuf.at[sl, k], insem.at[sl, k])
                    for k, src in enumerate((w_hbm, m_hbm, v_hbm))]

        def write(step, sl):
            return [pltpu.make_async_copy(outbuf.at[sl, k], view(dst, step), outsem.at[sl, k])
                    for k, dst in enumerate((g_hbm, d_hbm, mo_hbm, vo_hbm))]

        @pl.when(i == 0)
        def _():
            for cp in fetch(0, 0):
                cp.start()

        @pl.when(i + 1 < nsteps)
        def _():
            for cp in fetch(i + 1, 1 - slot):
                cp.start()

        for cp in fetch(i, slot):
            cp.wait()

        @pl.when(i >= 2)
        def _():
            for cp in write(i - 2, slot):
                cp.wait()

        g = p_ref[0].astype(F32)
        for s in range(1, nparts):
            g = g + p_ref[s].astype(F32)
        g = g[:rows]
        delta, m_new, v_new = _adam_math(g, inbuf[slot, 0], inbuf[slot, 1], inbuf[slot, 2])
        for k, val in enumerate((g, delta, m_new, v_new)):
            outbuf[slot, k] = val
        for cp in write(i, slot):
            cp.start()

        @pl.when(i == nsteps - 1)
        def _():
            for cp in write(i - 1, 1 - slot) + write(i, slot):
                cp.wait()

    hbm = pl.BlockSpec(memory_space=pl.ANY)
    assert nsteps >= 2
    return pl.pallas_call(
        body, name=name, grid=(nsteps,),
        in_specs=[pl.BlockSpec((nparts, parts.shape[1], tc), lambda i: (0, 0, i)), hbm, hbm, hbm],
        out_specs=[hbm] * 4, out_shape=[S((rows, 1, cols), F32)] * 4,
        scratch_shapes=[pltpu.VMEM((2, 3, rows, tc), F32), pltpu.VMEM((2, 4, rows, tc), F32),
                        pltpu.SemaphoreType.DMA((2, 3)), pltpu.SemaphoreType.DMA((2, 4))],
        compiler_params=pltpu.CompilerParams(dimension_semantics=("arbitrary",)),
    )(parts, w, m, v)


SLAB = 1296
REMAP_RUNS = 4
_PIECES = ((O_QA, O_ZA, C_QA), (O_ZA, O_QG, C_ZA), (O_QG, O_GLR, C_QG), (O_GLR, O_ZG, C_GLR), (O_ZG, O_GA, C_ZG),
           (O_GA, O_END, C_GA))


def _slab_row_of_aligned(a):
    for o0, o1, a0 in _PIECES:
        if a0 <= a < a0 + o1 - o0:
            c = o0 + a - a0
            return SLAB * (c // W_IN_SHARD) + c % W_IN_SHARD
    return -1


def _aligned_row_of_slab(r):
    d, l = divmod(r, SLAB)
    if l >= W_IN_SHARD:
        return -1
    c = d * W_IN_SHARD + l
    for o0, o1, a0 in _PIECES:
        if o0 <= c < o1:
            return a0 + c - o0
    raise AssertionError(c)


def _remap_table(row_of, n_out, block, n_src):
    win = block + 16
    table = []
    for b in range(n_out // block):
        runs = []
        for i in range(block):
            s = row_of(b * block + i)
            if s < 0:
                continue
            if runs and runs[-1][0] + runs[-1][2] == s and runs[-1][1] + runs[-1][2] == i:
                runs[-1][2] += 1
            else:
                runs.append([s, i, 1])
        assert len(runs) <= REMAP_RUNS, (b, runs)
        row = []
        for s, i, n in runs:
            w = min(s // 16 * 16, n_src - win)
            assert 0 <= s - w and s - w + n <= win
            row += [w, s - w, i, n]
        table.append(row + [0] * (4 * REMAP_RUNS - len(row)))
    return table


def _remap_rows(src, row_of, n_out, block, name):
    n_src, cols = src.shape
    nb, win = n_out // block, block + 16
    table = _remap_table(row_of, n_out, block, n_src)
    runs = [[tuple(row[4 * k:4 * k + 4]) for k in range(REMAP_RUNS) if row[4 * k + 3] > 0] for row in table]

    def body(src_hbm, out_hbm, wbuf, obuf, insem, outsem):
        def fetches(b):
            return [pltpu.make_async_copy(src_hbm.at[pl.ds(w, win)], wbuf.at[b % 2, k], insem.at[b % 2, k])
                    for k, (w, _, _, _) in enumerate(runs[b])]

        def store(b):
            return pltpu.make_async_copy(obuf.at[b % 2], out_hbm.at[pl.ds(b * block, block)], outsem.at[b % 2])

        for cp in fetches(0):
            cp.start()
        for b in range(nb):
            if b + 1 < nb:
                for cp in fetches(b + 1):
                    cp.start()
            for cp in fetches(b):
                cp.wait()
            if b >= 2:
                store(b - 2).wait()
            if sum(count for _, _, _, count in runs[b]) < block:
                obuf[b % 2] = jnp.zeros((block, cols), src.dtype)
            for k, (_, shift, first, count) in enumerate(runs[b]):
                obuf[b % 2, first:first + count, :] = wbuf[b % 2, k, shift:shift + count, :]
            store(b).start()
        store(nb - 2).wait()
        store(nb - 1).wait()

    hbm = pl.BlockSpec(memory_space=pl.ANY)
    return pl.pallas_call(
        body, name=name, in_specs=[hbm], out_specs=hbm, out_shape=S((n_out, cols), src.dtype),
        scratch_shapes=[pltpu.VMEM((2, REMAP_RUNS, win, cols), src.dtype), pltpu.VMEM((2, block, cols), src.dtype),
                        pltpu.SemaphoreType.DMA((2, REMAP_RUNS)), pltpu.SemaphoreType.DMA((2,))],
    )(src)


def _col_blocks(w, width):
    return w.reshape(w.shape[0], NDEV, width).transpose(1, 0, 2)


def _from_col_blocks(w):
    return w.transpose(1, 0, 2).reshape(w.shape[1], NDEV * w.shape[2])


def _local_step(x2, p2, pos, tgt, norm_g, qk_norm_q, qk_norm_k, gla_gate_b, gla_norm_g, ple_norm_g, w_al,
                weights=None, proj_side=None, unpack=None, dw_side_of=None, dh_side_of=None):
    half = ROT_DIM // 2
    inv8 = jnp.power(jnp.float32(ROPE_THETA), -jnp.arange(half, dtype=F32) * 2.0 / ROT_DIM)
    inv = jnp.tile(jnp.concatenate([inv8, inv8, jnp.zeros((HD - ROT_DIM,), F32)]), 2).reshape(1, 128)
    gq = jnp.tile(qk_norm_q, (1, 2))
    gk = jnp.tile(qk_norm_k, (1, 2))

    proj, h, got = _proj_rms(x2, norm_g, w_al, proj_side)
    if proj_side is not None:
        weights = unpack(got)
    w2p, w_att_f, w_gla_f, w_out_f, w_pg_f, w_ple_f = weights
    qkv = _qk_prep(proj, pos, inv, gq, gk)
    fwd = [_att_fwd(qkv[g], qkv[3 + g], qkv[6 + g], g, f"att_fwd{g}") for g in range(3)]
    att, lse, ain = _att_merge([f[0] for f in fwd], [f[1] for f in fwd], proj)
    o_gla, bin_, states = _gla_fwd(proj, w2p, gla_gate_b, gla_norm_g)
    ya, yb, y, x1 = _branches_fwd(ain, bin_, proj, x2, w_att_f, w_gla_f, w_out_f)
    n2, loss_v, dout, du, dw_ple = _ple_loss(x1, p2, tgt, ple_norm_g, w_pg_f, w_ple_f)

    dx1, dy, dg_ple, dw_pg, dw_out = _ple_bwd(du, n2, y, x1, dout, ple_norm_g, w_pg_f, w_out_f)
    dproj, dain, dbin, dw_att, dw_gla = _branches_bwd(dy, ya, yb, ain, bin_, proj, w_att_f, w_gla_f)
    dproj, da0, da1, da2, at1, at2, ls1, ls2 = _att_gate_bwd(dain, att, lse, proj, dproj)
    datts, atts, lses = (da0, da1, da2), (att[None], at1, at2), (lse[None], ls1, ls2)
    dproj, dw2, dbg, dgn = _gla_bwd(proj, w2p, gla_gate_b, gla_norm_g, o_gla, states, dbin, dproj)
    bwd = [_att_bwd(qkv[g], qkv[3 + g], qkv[6 + g], datts[g], atts[g], lses[g], g, f"att_bwd{g}") for g in range(3)]
    dproj, dgq, dgk = _qk_bwd(proj, pos, inv, gq, gk, [b[0] for b in bwd], [b[1] for b in bwd],
                              [b[2] for b in bwd], dproj)
    out = dict(loss=loss_v, dw2=dw2, dw_att=dw_att, dw_gla=dw_gla, dw_out=dw_out, dw_pg=dw_pg, dw_ple=dw_ple,
               dgq=dgq, dgk=dgk, dbg=dbg, dgn=dgn, dg_ple=dg_ple)
    if dw_side_of is None:
        dw_al = _mm(dproj, h, mode="tn", name="dw_in", tm=1536, tn=D, tk=T, out_dtype=BF16)
    else:
        dw_al, out["dw_side"] = _mm(dproj, h, mode="tn", name="dw_in", tm=1536, tn=D, tk=T, out_dtype=BF16,
                                    side=dw_side_of(out))
    grad_x, dg_norm, out["dh_side"] = _dh_rms(dproj, w_al, x2, norm_g, dx1,
                                              None if dh_side_of is None else dh_side_of(dw_al))
    out.update(grad_x=grad_x, dw_al=dw_al, dg_norm=dg_norm)
    return out


def kernel(x, p, positions, norm_g, w_in, qk_norm_q, qk_norm_k, gla_gate_w2, gla_gate_b, gla_norm_g, w_att_proj, w_gla_proj, w_out, ple_norm_g, w_ple_gate, w_ple, loss_target, m_norm_g, m_w_in, m_qk_norm_q, m_qk_norm_k, m_gla_gate_w2, m_gla_gate_b, m_gla_norm_g, m_w_att_proj, m_w_gla_proj, m_w_out, m_ple_norm_g, m_w_ple_gate, m_w_ple, v_norm_g, v_w_in, v_qk_norm_q, v_qk_norm_k, v_gla_gate_w2, v_gla_gate_b, v_gla_norm_g, v_w_att_proj, v_w_gla_proj, v_w_out, v_ple_norm_g, v_w_ple_gate, v_w_ple):
    x2, p2, tgt = x[0], p[0, 0], loss_target[0]
    pos = positions.astype(F32).reshape(T, 1)

    rows3 = jnp.stack([w_gla_proj[0], w_out[0], w_ple_gate[0]]).astype(BF16)
    cols3 = jnp.concatenate([w_att_proj[0], w_ple[0], jnp.pad(gla_gate_w2[0], ((0, 0), (0, 64)))], axis=0).astype(BF16)
    mine = jnp.pad(w_in[0].T.astype(BF16), ((0, SLAB - W_IN_SHARD), (0, 0)))
    (g_in,) = _all_gather_by_chip([mine], "gather_w_in")
    w_al = _remap_rows(g_in.reshape(NDEV * SLAB, D), _slab_row_of_aligned, NCOL, 1536, "align_w_in")

    def unpack(got):
        g_rows, g_cols = got
        w2_f = _from_col_blocks(g_cols[:, 768:784, :64])
        return (jnp.pad(w2_f, ((0, GLR_W - GLR_N), (0, 0))), _from_col_blocks(g_cols[:, :512]),
                g_rows[:, 0].reshape(D, D), g_rows[:, 1].reshape(D, D), g_rows[:, 2].reshape(D, D),
                _from_col_blocks(g_cols[:, 512:768]))

    def dw_side_of(g):
        s_rows = jnp.concatenate([g[k].reshape(NDEV, 128, D) for k in ("dw_gla", "dw_out", "dw_pg")], axis=1)
        s_cols = jnp.concatenate([_col_blocks(g["dw_att"], 128), _col_blocks(g["dw_ple"], 128),
                                  jnp.pad(_col_blocks(g["dw2"][:GLR_N], 64), ((0, 0), (0, 0), (0, 64)))], axis=1)
        return _exchange_side([s_rows.astype(BF16), s_cols.astype(BF16)])

    def dh_side_of(dw_al):
        s_in = _remap_rows(dw_al, _aligned_row_of_slab, NDEV * SLAB, SLAB, "shard_dw_in").reshape(NDEV, SLAB, D)
        return _chips_side([_sibling_sum(s_in, "sibling_sum")])

    loc = _local_step(x2, p2, pos, tgt, norm_g, qk_norm_q, qk_norm_k, gla_gate_b, gla_norm_g, ple_norm_g, w_al,
                      proj_side=_gather_side([rows3, cols3]), unpack=unpack, dw_side_of=dw_side_of,
                      dh_side_of=dh_side_of)
    loss_v, grad_x = loc["loss"], loc["grad_x"]
    dg_norm, dgq, dgk, dbg, dgn, dg_ple = (loc[k] for k in ("dg_norm", "dgq", "dgk", "dbg", "dgn", "dg_ple"))
    r_rows, r_cols = loc["dw_side"]
    (r_in,) = loc["dh_side"]

    r_small = _comm_call(_gather_side([dg_norm, dgq, dgk, dbg, dgn, dg_ple, loss_v]), "gather_small")

    outs = {}

    rows_of = lambda a: jnp.transpose(a, (2, 0, 1))
    outs["w_in"] = [jnp.transpose(o, (1, 2, 0))[0] for o in
                    _adamw_rows(r_in, rows_of(w_in), rows_of(m_w_in), rows_of(v_w_in), "adam_w_in")]
    places = (("w_gla_proj", 0, slice(0, 128), slice(None), (w_gla_proj, m_w_gla_proj, v_w_gla_proj)),
              ("w_out", 0, slice(128, 256), slice(None), (w_out, m_w_out, v_w_out)),
              ("w_ple_gate", 0, slice(256, 384), slice(None), (w_ple_gate, m_w_ple_gate, v_w_ple_gate)),
              ("w_att_proj", 1, slice(0, 512), slice(None), (w_att_proj, m_w_att_proj, v_w_att_proj)),
              ("w_ple", 1, slice(512, 768), slice(None), (w_ple, m_w_ple, v_w_ple)),
              ("gla_gate_w2", 1, slice(768, 784), slice(0, 64), (gla_gate_w2, m_gla_gate_w2, v_gla_gate_w2)))
    res = _adamw_shards([r_rows, r_cols], [place[1:] for place in places])
    for j, place in enumerate(places):
        outs[place[0]] = [o[0] for o in res[4 * j:4 * j + 4]]
    small = ((norm_g, m_norm_g, v_norm_g), (qk_norm_q, m_qk_norm_q, v_qk_norm_q), (qk_norm_k, m_qk_norm_k, v_qk_norm_k),
             (gla_gate_b, m_gla_gate_b, v_gla_gate_b), (gla_norm_g, m_gla_norm_g, v_gla_norm_g),
             (ple_norm_g, m_ple_norm_g, v_ple_norm_g))
    sm = _adamw_small(r_small[:6], small, r_small[6])
    for j, nm in enumerate(("norm_g", "qk_norm_q", "qk_norm_k", "gla_gate_b", "gla_norm_g", "ple_norm_g")):
        outs[nm] = [o[0] for o in sm[4 * j:4 * j + 4]]

    loss = sm[-1][0, 0]
    order = ["norm_g", "w_in", "qk_norm_q", "qk_norm_k", "gla_gate_w2", "gla_gate_b", "gla_norm_g", "w_att_proj",
             "w_gla_proj", "w_out", "ple_norm_g", "w_ple_gate", "w_ple"]
    result = [loss, grad_x[None]]
    for i in range(4):
        result += [outs[nm][i][None] for nm in order]
    return tuple(result)
```

```python
import functools

import jax
import jax.numpy as jnp
from jax import lax
from jax.experimental import pallas as pl
from jax.experimental.pallas import tpu as pltpu

F32 = jnp.float32
BF16 = jnp.bfloat16
S = jax.ShapeDtypeStruct

T = 4096
D = 1024
NDEV = 8
HD = 64
ATT_W = 512
ATT_QKV = 1536
DILATIONS = (1, 4, 16)
BLK = 128
GH, GDK, GDV = 4, 128, 256
GLA_C = 128
PLE = 256
EPS = 1e-6
ROT_DIM = 16
ROPE_THETA = 500000.0
GLA_TAU = 16.0
W_IN_SHARD = 1282

C_QG, C_KG, C_VG, C_ZG, C_GLR, C_ZA, C_GA, C_GB, C_QA, C_KA, C_VA = (
    0, 512, 1024, 2048, 3072, 3584, 4096, 5120, 6144, 7680, 9216)
GLA_GROUP_W = 3584
GLR_W = 512
NCOL = 10752
GLR_N = 16
O_QA, O_ZA, O_QG, O_GLR, O_ZG, O_GA, O_END = 0, 4608, 5120, 7168, 7184, 8208, 10256

ADAM_LR, ADAM_B1, ADAM_B2, ADAM_EPS, ADAM_WD, ADAM_STEP = 0.001, 0.9, 0.999, 1e-08, 0.01, 10

MESH = pl.DeviceIdType.MESH


def _sigmoid(z):
    return 1.0 / (1.0 + jnp.exp(-z))


def _dot(a, b, dims):
    return lax.dot_general(a, b, (dims, ((), ())), preferred_element_type=F32)


def _nn(a, b):
    return _dot(a, b, ((1,), (0,)))


def _nt(a, b):
    return _dot(a, b, ((1,), (1,)))


def _tn(a, b):
    return _dot(a, b, ((0,), (0,)))


def _mm(a, b, *, mode, name, tm, tn, tk, out_dtype=F32, res=None, side=None):
    if mode == "nn":
        (m, k), n = a.shape, b.shape[1]
        a_spec = pl.BlockSpec((tm, tk), lambda i, j, l: (i, l))
        b_spec = pl.BlockSpec((tk, tn), lambda i, j, l: (l, j))
        dot = _nn
    elif mode == "nt":
        (m, k), n = a.shape, b.shape[0]
        a_spec = pl.BlockSpec((tm, tk), lambda i, j, l: (i, l))
        b_spec = pl.BlockSpec((tn, tk), lambda i, j, l: (j, l))
        dot = _nt
    else:
        (k, m), n = a.shape, b.shape[1]
        a_spec = pl.BlockSpec((tk, tm), lambda i, j, l: (l, i))
        b_spec = pl.BlockSpec((tk, tn), lambda i, j, l: (l, j))
        dot = _tn
    assert m % tm == 0 and n % tn == 0 and k % tk == 0, (name, m, n, k)
    grid = (m // tm, n // tn, k // tk)
    nk = grid[2]
    o_spec = pl.BlockSpec((tm, tn), lambda i, j, l: (i, j))
    in_specs = [a_spec, b_spec]
    args = [a, b]
    if res is not None:
        in_specs.append(o_spec)
        args.append(res)
    n_in = len(args)
    n_side = 0 if side is None else len(side["arrs"])
    hbm = pl.BlockSpec(memory_space=pl.ANY)

    def body(*refs):
        a_ref, b_ref = refs[0], refs[1]
        r_ref = refs[2] if res is not None else None
        o_ref = refs[n_in + n_side]
        scratch = refs[n_in + 2 * n_side + 1:]
        if side is not None:
            start, finish_side = side["plan"](refs[n_in:n_in + n_side], refs[n_in + n_side + 1:n_in + 2 * n_side + 1],
                                              *scratch[1 if nk > 1 else 0:])
            ids = [pl.program_id(d) for d in range(3)]

            @pl.when((ids[0] == 0) & (ids[1] == 0) & (ids[2] == 0))
            def _():
                start()

        part = dot(a_ref[...].astype(BF16), b_ref[...].astype(BF16))

        def finish(val):
            if r_ref is not None:
                val = val + r_ref[...]
            o_ref[...] = val.astype(out_dtype)

        if nk == 1:
            finish(part)
        else:
            acc = scratch[0]
            l = pl.program_id(2)

            @pl.when(l == 0)
            def _():
                acc[...] = part

            @pl.when(l > 0)
            def _():
                acc[...] += part

            @pl.when(l == nk - 1)
            def _():
                finish(acc[...])

        if side is not None:
            @pl.when((ids[0] == grid[0] - 1) & (ids[1] == grid[1] - 1) & (ids[2] == grid[2] - 1))
            def _():
                finish_side()

    sems = [] if side is None else side["scratch"]
    outs = pl.pallas_call(
        body, name=name, grid=grid,
        in_specs=in_specs + [hbm] * n_side, out_specs=[o_spec] + [hbm] * n_side,
        out_shape=[S((m, n), out_dtype)] + ([] if side is None else side["out_shape"]),
        scratch_shapes=([pltpu.VMEM((tm, tn), F32)] if nk > 1 else []) + sems,
        compiler_params=pltpu.CompilerParams(
            dimension_semantics=("arbitrary",) * 3 if side is not None else ("parallel", "parallel", "arbitrary")),
    )(*args, *([] if side is None else side["arrs"]))
    return outs[0] if side is None else (outs[0], outs[1:])


def _side_parts(side, refs, n_in, n_out):
    n_side = 0 if side is None else len(side["arrs"])
    scratch = refs[n_in + n_out + 2 * n_side:]
    if side is None:
        return (lambda: None), (lambda: None), scratch
    start, finish = side["plan"](refs[n_in:n_in + n_side], refs[n_in + n_side + n_out:n_in + n_out + 2 * n_side],
                                 *scratch[len(scratch) - len(side["scratch"]):])
    return start, finish, scratch


def _proj_rms(x, g, wt, side=None):
    tm, tn = 1024, 1536
    grid = (T // tm, NCOL // tn)
    n_side = 0 if side is None else len(side["arrs"])
    hbm = pl.BlockSpec(memory_space=pl.ANY)

    def body(*refs):
        x_ref, g_ref, w_ref = refs[:3]
        o_ref, h_ref = refs[3 + n_side], refs[4 + n_side]
        start, finish, _ = _side_parts(side, refs, 3, 2)
        i, j = pl.program_id(0), pl.program_id(1)

        @pl.when((i == 0) & (j == 0))
        def _():
            start()

        @pl.when(j == 0)
        def _():
            xf = x_ref[...]
            r = lax.rsqrt(jnp.mean(xf * xf, axis=-1, keepdims=True) + EPS)
            h_ref[...] = (xf * r * g_ref[...]).astype(BF16)

        o_ref[...] = _nt(h_ref[...], w_ref[...])

        @pl.when((i == grid[0] - 1) & (j == grid[1] - 1))
        def _():
            finish()

    outs = pl.pallas_call(
        body, name="proj", grid=grid,
        in_specs=[pl.BlockSpec((tm, D), lambda i, j: (i, 0)), pl.BlockSpec((1, D), lambda i, j: (0, 0)),
                  pl.BlockSpec((tn, D), lambda i, j: (j, 0))] + [hbm] * n_side,
        out_specs=[pl.BlockSpec((tm, tn), lambda i, j: (i, j)), pl.BlockSpec((tm, D), lambda i, j: (i, 0))] + [hbm] * n_side,
        out_shape=[S((T, NCOL), F32), S((T, D), BF16)] + ([] if side is None else side["out_shape"]),
        scratch_shapes=[] if side is None else side["scratch"],
        compiler_params=pltpu.CompilerParams(dimension_semantics=("arbitrary", "arbitrary")),
    )(x, g, wt, *([] if side is None else side["arrs"]))
    return outs[0], outs[1], outs[2:]


def _dh_rms(dproj, wt, x, g, skip, side=None):
    tm, tk = 1024, 2688
    grid = (T // tm, NCOL // tk)
    n_side = 0 if side is None else len(side["arrs"])
    hbm = pl.BlockSpec(memory_space=pl.ANY)

    def body(*refs):
        a_ref, w_ref, x_ref, g_ref, s_ref = refs[:5]
        dx_ref, dg_ref = refs[5 + n_side], refs[6 + n_side]
        start, finish, scratch = _side_parts(side, refs, 5, 2)
        acc = scratch[0]
        i, l = pl.program_id(0), pl.program_id(1)

        @pl.when((i == 0) & (l == 0))
        def _():
            start()

        part = _nn(a_ref[...], w_ref[...])

        @pl.when(l == 0)
        def _():
            acc[...] = part

        @pl.when(l > 0)
        def _():
            acc[...] += part

        @pl.when(l == grid[1] - 1)
        def _():
            xf = x_ref[...]
            r = lax.rsqrt(jnp.mean(xf * xf, axis=-1, keepdims=True) + EPS)
            dn = acc[...]
            u = dn * g_ref[...]
            dx_ref[...] = s_ref[...] + r * u - xf * (r * r * r) * jnp.mean(u * xf, axis=-1, keepdims=True)
            dg = jnp.sum(dn * xf * r, axis=0, keepdims=True)

            @pl.when(i == 0)
            def _():
                dg_ref[...] = dg

            @pl.when(i > 0)
            def _():
                dg_ref[...] += dg

        @pl.when((i == grid[0] - 1) & (l == grid[1] - 1))
        def _():
            finish()

    tok = pl.BlockSpec((tm, D), lambda i, l: (i, 0))
    outs = pl.pallas_call(
        body, name="dh", grid=grid,
        in_specs=[pl.BlockSpec((tm, tk), lambda i, l: (i, l)), pl.BlockSpec((tk, D), lambda i, l: (l, 0)), tok,
                  pl.BlockSpec((1, D), lambda i, l: (0, 0)), tok] + [hbm] * n_side,
        out_specs=[tok, pl.BlockSpec((1, D), lambda i, l: (0, 0))] + [hbm] * n_side,
        out_shape=[S((T, D), F32), S((1, D), F32)] + ([] if side is None else side["out_shape"]),
        scratch_shapes=[pltpu.VMEM((tm, D), F32)] + ([] if side is None else side["scratch"]),
        compiler_params=pltpu.CompilerParams(dimension_semantics=("arbitrary", "arbitrary")),
    )(dproj, wt, x, g, skip, *([] if side is None else side["arrs"]))
    return outs[0], outs[1], outs[2:]


def _rot_tables(pos_ref, inv_ref):
    lane = lax.broadcasted_iota(jnp.int32, (1, 128), 1) % HD
    ang = pos_ref[...] * inv_ref[...]
    cos, sin = jnp.cos(ang), jnp.sin(ang)
    c = jnp.where(lane < ROT_DIM, cos, 1.0)
    sp = jnp.where((lane >= ROT_DIM // 2) & (lane < ROT_DIM), sin, 0.0)
    sm = jnp.where(lane < ROT_DIM // 2, -sin, 0.0)
    return c, sp, sm


def _head_sums(v):
    same = (lax.broadcasted_iota(jnp.int32, (128, 128), 0) < HD) == (lax.broadcasted_iota(jnp.int32, (128, 128), 1) < HD)
    return _nn(v.astype(BF16), jnp.where(same, 1.0, 0.0).astype(BF16))


def _pair_norm(t):
    return lax.rsqrt(_head_sums(t * t) * (1.0 / HD) + EPS)


def _pair_mean(t):
    return _head_sums(t) * (1.0 / HD)


TT = 256
NCH = ATT_QKV // 128


def _res_shape(grp, dtype):
    return S((DILATIONS[grp], T // DILATIONS[grp], ATT_W), dtype)


def _res_spec(grp):
    dil = DILATIONS[grp]
    return pl.BlockSpec((dil, TT // dil, ATT_W), lambda i: (0, i, 0))


def _to_residues(sc, j, dst_ref, dil, cols):
    n = TT // dil
    for r in range(dil):
        rows = sc[j] if dil == 1 else sc.at[j][pl.ds(r, n, stride=dil), :]
        dst_ref[r, :, cols] = rows.astype(dst_ref.dtype)


def _from_residues(src_ref, cols, sc, j, dil):
    n = TT // dil
    for r in range(dil):
        if dil == 1:
            sc[j] = src_ref[r, :, cols]
        else:
            sc.at[j][pl.ds(r, n, stride=dil), :] = src_ref[r, :, cols]


def _tok_spec(width, cblk=0):
    return pl.BlockSpec((TT, width), functools.partial(lambda i, c: (i, c), c=cblk))


def _const_spec(arr_or_shape):
    shape = arr_or_shape if isinstance(arr_or_shape, tuple) else arr_or_shape.shape
    return pl.BlockSpec(shape, functools.partial(lambda i, nd: (0,) * nd, nd=len(shape)))


def _qk_prep(proj, pos, inv, gq, gk):
    def body(q_ref, k_ref, v_ref, pos_ref, inv_ref, gq_ref, gk_ref, *rest):
        outs, sc = rest[:9], rest[9]
        c, sp, sm = _rot_tables(pos_ref, inv_ref)
        for which, (src, g_ref) in enumerate(((q_ref, gq_ref), (k_ref, gk_ref), (v_ref, None))):
            if g_ref is not None:
                g = jnp.broadcast_to(g_ref[...] * ((HD ** -0.5) if which == 0 else 1.0), c.shape)
                cg, spg, smg = c * g, sp * pltpu.roll(g, 8, 1), sm * pltpu.roll(g, 120, 1)
            for j in range(NCH):
                t = src[:, j * 128:(j + 1) * 128]
                if g_ref is not None:
                    t = _pair_norm(t) * (t * cg + pltpu.roll(t, 8, 1) * spg + pltpu.roll(t, 120, 1) * smg)
                sc[j] = t
            for j in range(NCH):
                grp, sub = divmod(j * 128, ATT_W)
                _to_residues(sc, j, outs[which * 3 + grp], DILATIONS[grp], slice(sub, sub + 128))

    return pl.pallas_call(
        body, name="qk_prep", grid=(T // TT,),
        in_specs=[_tok_spec(ATT_QKV, C_QA // ATT_QKV), _tok_spec(ATT_QKV, C_KA // ATT_QKV),
                  _tok_spec(ATT_QKV, C_VA // ATT_QKV), _tok_spec(1), _const_spec(inv), _const_spec(gq), _const_spec(gk)],
        out_specs=[_res_spec(g) for _ in range(3) for g in range(3)],
        out_shape=[_res_shape(g, BF16) for _ in range(3) for g in range(3)],
        scratch_shapes=[pltpu.VMEM((NCH, TT, 128), F32)],
        compiler_params=pltpu.CompilerParams(dimension_semantics=("arbitrary",)),
    )(proj, proj, proj, pos, inv, gq, gk)


def _qk_bwd(proj, pos, inv, gq, gk, dqs, dks, dvs, dproj):
    const = lambda a: pl.BlockSpec(a.shape, functools.partial(lambda i, p, nd: (0,) * nd, nd=a.ndim))
    res = lambda g: pl.BlockSpec((DILATIONS[g], TT // DILATIONS[g], ATT_W), lambda i, p: (0, i, 0))
    base = C_QA // ATT_QKV

    def body(t_ref, pos_ref, inv_ref, gq_ref, gk_ref, dq0, dq1, dq2, dk0, dk1, dk2, dv0, dv1, dv2, buf_ref,
             out_ref, dgq_ref, dgk_ref, sc):
        del buf_ref
        part = pl.program_id(1)
        first = pl.program_id(0) == 0

        def gather(drefs):
            for j in range(NCH):
                grp, sub = divmod(j * 128, ATT_W)
                _from_residues(drefs[grp], slice(sub, sub + 128), sc, j, DILATIONS[grp])

        def normed(g_ref, drefs, dg_ref):
            c, sp, sm = _rot_tables(pos_ref, inv_ref)
            gather(drefs)
            dg = jnp.zeros((1, 128), F32)
            for j in range(NCH):
                cols = slice(j * 128, (j + 1) * 128)
                d_rot = sc[j]
                dn = d_rot * c + pltpu.roll(d_rot * sp, 120, 1) + pltpu.roll(d_rot * sm, 8, 1)
                t = t_ref[:, cols]
                r = _pair_norm(t)
                gain = g_ref[...]
                dn_t = dn * t
                out_ref[:, cols] = (r * (dn * gain - t * ((r * r) * _pair_mean(dn_t * gain)))).astype(BF16)
                dg = dg + jnp.sum(dn_t * r, axis=0, keepdims=True)
            dg = dg + pltpu.roll(dg, HD, 1)

            @pl.when(first)
            def _():
                dg_ref[...] = dg

            @pl.when(jnp.logical_not(first))
            def _():
                dg_ref[...] += dg

        @pl.when(part == 0)
        def _():
            gather((dv0, dv1, dv2))
            for j in range(NCH):
                out_ref[:, j * 128:(j + 1) * 128] = sc[j].astype(BF16)

        @pl.when(part == 1)
        def _():
            normed(gq_ref, (dq0, dq1, dq2), dgq_ref)

        @pl.when(part == 2)
        def _():
            normed(gk_ref, (dk0, dk1, dk2), dgk_ref)

    keep = pl.BlockSpec((1, 128), lambda i, p: (0, 0))
    return pl.pallas_call(
        body, name="qk_bwd", grid=(T // TT, 3),
        in_specs=[pl.BlockSpec((TT, ATT_QKV), lambda i, p: (i, base + jnp.maximum(p - 1, 0))),
                  pl.BlockSpec((TT, 1), lambda i, p: (i, 0)), const(inv), const(gq), const(gk)]
        + [res(g) for _ in range(3) for g in range(3)] + [pl.BlockSpec(memory_space=pl.ANY)],
        out_specs=[pl.BlockSpec((TT, ATT_QKV), lambda i, p: (i, base + jnp.where(p == 0, 2, p - 1))), keep, keep],
        out_shape=[S(dproj.shape, dproj.dtype), S((1, 128), F32), S((1, 128), F32)],
        input_output_aliases={14: 0},
        scratch_shapes=[pltpu.VMEM((NCH, TT, 128), F32)],
        compiler_params=pltpu.CompilerParams(dimension_semantics=("arbitrary", "arbitrary")),
    )(proj, pos, inv, gq, gk, *dqs, *dks, *dvs, dproj)


def _split_heads(t):
    low = lax.broadcasted_iota(jnp.int32, (1, 128), 1) < HD
    zero = jnp.zeros_like(t)
    return jnp.concatenate([jnp.where(low, t, zero), jnp.where(low, zero, t)], axis=0)


def _join_heads(t2):
    low = lax.broadcasted_iota(jnp.int32, (1, 128), 1) < HD
    n = t2.shape[0] // 2
    return jnp.where(low, t2[:n], t2[n:])


def _band_mask4(has_before, has_own):
    row = lax.broadcasted_iota(jnp.int32, (BLK, 4 * BLK), 0)
    lane = lax.broadcasted_iota(jnp.int32, (BLK, 4 * BLK), 1)
    key = lane & (BLK - 1)
    own = lane >= 2 * BLK
    return (own & (key <= row) & has_own) | (jnp.logical_not(own) & (key >= row) & has_before)


def _band_mask_before(has_before):
    row = lax.broadcasted_iota(jnp.int32, (BLK, 2 * BLK), 0)
    key = lax.broadcasted_iota(jnp.int32, (BLK, 2 * BLK), 1) & (BLK - 1)
    return (key >= row) & has_before


def _per_head(width, col_a, col_b):
    lane = lax.broadcasted_iota(jnp.int32, (1, width), 1)
    return jnp.where((lane & BLK) == 0, col_a, col_b)


NQ = ATT_W // 128


def _att_fwd(q, k, v, grp, name):
    dil = DILATIONS[grp]
    nb = T // dil // BLK

    def body(q_ref, kp_ref, kc_ref, vp_ref, vc_ref, o_ref, lse_ref, s_sc, p_sc):
        mask = _band_mask4(pl.program_id(1) > 0, True)
        low = lax.broadcasted_iota(jnp.int32, (1, 128), 1) < HD
        halves = lambda ref, j, h: (ref[j, :, h * BLK:(h + 1) * BLK], ref[j, :, (h + 2) * BLK:(h + 3) * BLK])
        for j in range(NQ):
            cols = slice(j * 128, (j + 1) * 128)
            k4 = jnp.concatenate([_split_heads(kp_ref[:, cols]), _split_heads(kc_ref[:, cols])], axis=0)
            s_sc[j] = jnp.where(mask, _nt(q_ref[:, cols], k4), -jnp.inf)
        mxs = [[jnp.maximum(*(jnp.max(t, axis=-1, keepdims=True) for t in halves(s_sc, j, h))) for h in range(2)]
               for j in range(NQ)]
        dens = []
        for j in range(NQ):
            p = jnp.exp(s_sc[j] - _per_head(4 * BLK, *mxs[j]))
            p_sc[j] = p.astype(BF16)
            dens.append([jnp.sum(p[:, h * BLK:(h + 1) * BLK], axis=-1, keepdims=True)
                         + jnp.sum(p[:, (h + 2) * BLK:(h + 3) * BLK], axis=-1, keepdims=True) for h in range(2)])
        for j in range(NQ):
            cols = slice(j * 128, (j + 1) * 128)
            v4 = jnp.concatenate([_split_heads(vp_ref[:, cols]), _split_heads(vc_ref[:, cols])], axis=0)
            o_ref[:, cols] = _nn(p_sc[j], v4) / jnp.where(low, dens[j][0], dens[j][1])
            lse_ref[:, cols] = jnp.where(low, mxs[j][0] + jnp.log(dens[j][0]), mxs[j][1] + jnp.log(dens[j][1]))

    cur = pl.BlockSpec((None, BLK, ATT_W), lambda r, i: (r, i, 0))
    prev = pl.BlockSpec((None, BLK, ATT_W), lambda r, i: (r, jnp.maximum(i - 1, 0), 0))
    return pl.pallas_call(
        body, name=name, grid=(dil, nb),
        in_specs=[cur, prev, cur, prev, cur],
        out_specs=[cur, cur], out_shape=[_res_shape(grp, F32)] * 2,
        scratch_shapes=[pltpu.VMEM((NQ, BLK, 4 * BLK), F32), pltpu.VMEM((NQ, BLK, 4 * BLK), BF16)],
        compiler_params=pltpu.CompilerParams(dimension_semantics=("parallel", "arbitrary")),
    )(q, k, k, v, v)


def _att_bwd(q, k, v, datt, att, lse, grp, name):
    dil = DILATIONS[grp]
    nb = T // dil // BLK
    scale = HD ** -0.5

    def body(q0_ref, q1_ref, kp_ref, kc_ref, vp_ref, vc_ref, do0_ref, do1_ref, o0_ref, o1_ref, l0_ref, l1_ref,
             dq_ref, dk_ref, dv_ref, k4_sc, v4_sc, s0_sc, s1_sc, dp0_sc, dp1_sc, p_sc, ds_sc):
        i = pl.program_id(1)
        mask_mine = _band_mask4(i > 0, True)
        mask_next = _band_mask_before(i < nb - 1)
        low = lax.broadcasted_iota(jnp.int32, (1, 128), 1) < HD
        for j in range(NQ):
            cols = slice(j * 128, (j + 1) * 128)
            k4_sc[j, :2 * BLK] = _split_heads(kp_ref[:, cols])
            k4_sc[j, 2 * BLK:] = _split_heads(kc_ref[:, cols])
            v4_sc[j, :2 * BLK] = _split_heads(vp_ref[:, cols])
            v4_sc[j, 2 * BLK:] = _split_heads(vc_ref[:, cols])
        for j in range(NQ):
            cols = slice(j * 128, (j + 1) * 128)
            s0_sc[j] = _nt(q0_ref[:, cols], k4_sc[j])
            s1_sc[j] = _nt(q1_ref[:, cols], k4_sc[j, 2 * BLK:])
            dp0_sc[j] = _nt(do0_ref[:, cols].astype(BF16), v4_sc[j])
            dp1_sc[j] = _nt(do1_ref[:, cols].astype(BF16), v4_sc[j, 2 * BLK:])
        stats = []
        for j in range(NQ):
            cols = slice(j * 128, (j + 1) * 128)
            for do_ref, o_ref, l_ref in ((do0_ref, o0_ref, l0_ref), (do1_ref, o1_ref, l1_ref)):
                prod = do_ref[:, cols].astype(F32) * o_ref[:, cols].astype(F32)
                d_all = jnp.sum(prod, axis=-1, keepdims=True)
                d_low = jnp.sum(jnp.where(low, prod, 0.0), axis=-1, keepdims=True)
                lse_t = l_ref[:, cols]
                stats.append((d_low, d_all - d_low, lse_t[:, 0:1], lse_t[:, HD:HD + 1]))
        for j in range(NQ):
            (da, db, la, lb), (da1, db1, la1, lb1) = stats[2 * j], stats[2 * j + 1]
            p0 = jnp.where(mask_mine, jnp.exp(s0_sc[j] - _per_head(4 * BLK, la, lb)), 0.0)
            ds0 = p0 * (dp0_sc[j] - _per_head(4 * BLK, da, db))
            p1 = jnp.where(mask_next, jnp.exp(s1_sc[j] - _per_head(2 * BLK, la1, lb1)), 0.0)
            ds1 = p1 * (dp1_sc[j] - _per_head(2 * BLK, da1, db1))
            p_sc[j, :BLK] = p0.astype(BF16)
            ds_sc[j, :BLK] = ds0.astype(BF16)
            p_sc[j, BLK:, 2 * BLK:] = p1.astype(BF16)
            ds_sc[j, BLK:, 2 * BLK:] = ds1.astype(BF16)
        for j in range(NQ):
            cols = slice(j * 128, (j + 1) * 128)
            dq_ref[:, cols] = _nn(ds_sc[j, :BLK], k4_sc[j]) * scale
            qq = jnp.concatenate([q0_ref[:, cols], q1_ref[:, cols]], axis=0)
            dd = jnp.concatenate([do0_ref[:, cols], do1_ref[:, cols]], axis=0).astype(BF16)
            dk_ref[:, cols] = _join_heads(_tn(ds_sc[j, :, 2 * BLK:], qq))
            dv_ref[:, cols] = _join_heads(_tn(p_sc[j, :, 2 * BLK:], dd))

    def spec(shift):
        return pl.BlockSpec((None, BLK, ATT_W), lambda r, i: (r, jnp.clip(i + shift, 0, nb - 1), 0))

    here, after, before = spec(0), spec(1), spec(-1)
    vm = pltpu.VMEM
    return pl.pallas_call(
        body, name=name, grid=(dil, nb),
        in_specs=[here, after, before, here, before, here, here, after, here, after, here, after],
        out_specs=[here] * 3, out_shape=[_res_shape(grp, F32)] * 3,
        scratch_shapes=[vm((NQ, 4 * BLK, 128), BF16), vm((NQ, 4 * BLK, 128), BF16), vm((NQ, BLK, 4 * BLK), F32),
                        vm((NQ, BLK, 2 * BLK), F32), vm((NQ, BLK, 4 * BLK), F32), vm((NQ, BLK, 2 * BLK), F32),
                        vm((NQ, 2 * BLK, 4 * BLK), BF16), vm((NQ, 2 * BLK, 4 * BLK), BF16)],
        compiler_params=pltpu.CompilerParams(dimension_semantics=("parallel", "arbitrary")),
    )(q, q, k, k, v, v, datt, datt, att, att, lse, lse)


def _att_merge(os_, lses, proj):
    nq = ATT_W // 128

    def body(o0, o1, o2, l0, l1, l2, za_ref, att_ref, lse_ref, ain_ref, sc):
        for a, ref in enumerate((o0, o1, o2, l0, l1, l2)):
            for j in range(nq):
                _from_residues(ref, slice(j * 128, (j + 1) * 128), sc, a * nq + j, DILATIONS[a % 3])
        for j in range(nq):
            cols = slice(j * 128, (j + 1) * 128)
            oa, ob, oc = (sc[a * nq + j] for a in range(3))
            la, lb, lc = (sc[(3 + a) * nq + j] for a in range(3))
            m = jnp.maximum(jnp.maximum(la, lb), lc)
            wa, wb, wc = jnp.exp(la - m), jnp.exp(lb - m), jnp.exp(lc - m)
            tot = wa + wb + wc
            att = (wa * oa + wb * ob + wc * oc) / tot
            att_ref[:, cols] = att
            lse_ref[:, cols] = m + jnp.log(tot)
            za = za_ref[:, cols]
            ain_ref[:, cols] = (att * za * _sigmoid(za)).astype(BF16)

    return pl.pallas_call(
        body, name="att_merge", grid=(T // TT,),
        in_specs=[_res_spec(g) for _ in range(2) for g in range(3)] + [_tok_spec(ATT_W, C_ZA // ATT_W)],
        out_specs=[_tok_spec(ATT_W)] * 3,
        out_shape=[S((T, ATT_W), F32), S((T, ATT_W), F32), S((T, ATT_W), BF16)],
        scratch_shapes=[pltpu.VMEM((6 * nq, TT, 128), F32)],
        compiler_params=pltpu.CompilerParams(dimension_semantics=("arbitrary",)),
    )(*os_, *lses, proj)


def _att_gate_bwd(dain, att, lse, proj, dproj):
    nq = ATT_W // 128

    def body(d_ref, att_ref, lse_ref, za_ref, buf_ref, dza_ref, da0, da1, da2, at1, at2, ls1, ls2, sc):
        del buf_ref
        for j in range(nq):
            cols = slice(j * 128, (j + 1) * 128)
            za = za_ref[:, cols]
            sg = _sigmoid(za)
            d = d_ref[:, cols].astype(F32)
            att_ = att_ref[:, cols]
            dza_ref[:, cols] = (d * att_ * sg * (1.0 + za * (1.0 - sg))).astype(BF16)
            sc[j] = d * za * sg
            sc[nq + j] = att_
            sc[2 * nq + j] = lse_ref[:, cols]
        for j in range(nq):
            cols = slice(j * 128, (j + 1) * 128)
            for grp, dst in enumerate((da0, da1, da2)):
                _to_residues(sc, j, dst, DILATIONS[grp], cols)
            for grp, dst in ((1, at1), (2, at2)):
                _to_residues(sc, nq + j, dst, DILATIONS[grp], cols)
            for grp, dst in ((1, ls1), (2, ls2)):
                _to_residues(sc, 2 * nq + j, dst, DILATIONS[grp], cols)

    res = (0, 1, 2, 1, 2, 1, 2)
    return pl.pallas_call(
        body, name="att_gate_bwd", grid=(T // TT,),
        in_specs=[_tok_spec(ATT_W)] * 3 + [_tok_spec(ATT_W, C_ZA // ATT_W), pl.BlockSpec(memory_space=pl.ANY)],
        out_specs=[_tok_spec(ATT_W, C_ZA // ATT_W)] + [_res_spec(g) for g in res],
        out_shape=[S(dproj.shape, dproj.dtype)] + [_res_shape(g, BF16) for g in res[:5]]
        + [_res_shape(g, F32) for g in res[5:]],
        input_output_aliases={4: 0},
        scratch_shapes=[pltpu.VMEM((3 * nq, TT, 128), F32)],
        compiler_params=pltpu.CompilerParams(dimension_semantics=("arbitrary",)),
    )(dain, att, lse, proj, dproj)


def _split3(v):
    hi = v.astype(BF16)
    r1 = v - hi.astype(F32)
    mid = r1.astype(BF16)
    lo = (r1 - mid.astype(F32)).astype(BF16)
    return hi, mid, lo


def _chunk_scores(qt, kt, q_ref, k_ref, h):
    cols = slice(h * GDK, (h + 1) * GDK)
    own = jnp.sum(q_ref[:, cols] * (GDK ** -0.5) * k_ref[:, cols], axis=-1, keepdims=True)
    row = lax.broadcasted_iota(jnp.int32, (GLA_C, GLA_C), 0)
    col = lax.broadcasted_iota(jnp.int32, (GLA_C, GLA_C), 1)
    a = _nt(qt.astype(BF16), kt.astype(BF16))
    return jnp.where(col < row, a, jnp.where(col == row, own, 0.0))


def _tri_sum(v, upper):
    n = v.shape[0]
    row = lax.broadcasted_iota(jnp.int32, (n, n), 0)
    col = lax.broadcasted_iota(jnp.int32, (n, n), 1)
    tri = jnp.where(col >= row if upper else col <= row, 1.0, 0.0).astype(BF16)
    hi, mid, lo = _split3(v)
    return _nn(tri, hi) + _nn(tri, mid) + _nn(tri, lo)


def _gla_gates(glr_ref, w2_ref, b_ref):
    logit = _nn(glr_ref[...].astype(BF16), w2_ref[...]) + b_ref[...]
    lg = (jnp.minimum(logit, 0.0) - jnp.log(1.0 + jnp.exp(-jnp.abs(logit)))) * (1.0 / GLA_TAU)
    return logit, _tri_sum(lg, upper=False)


def _gla_head(cum, q_ref, k_ref, h):
    cols = slice(h * GDK, (h + 1) * GDK)
    b = cum[:, cols]
    last = b[GLA_C - 1:GLA_C, :]
    e_pos = jnp.exp(b)
    e_neg = jnp.exp(-b)
    e_end = jnp.exp(last - b)
    qt = q_ref[:, cols] * (GDK ** -0.5) * e_pos
    kt = k_ref[:, cols] * e_neg
    kh = k_ref[:, cols] * e_end
    return b, last, e_pos, e_neg, e_end, qt, kt, kh


def _causal(n):
    return lax.broadcasted_iota(jnp.int32, (n, n), 1) <= lax.broadcasted_iota(jnp.int32, (n, n), 0)


def _gla_fwd(proj, w2p, bg, gn):
    nc = T // GLA_C

    def body(q_ref, k_ref, v_ref, glr_ref, zg_ref, w2_ref, b_ref, gn_ref, o_ref, bin_ref, st_ref, state):
        @pl.when(pl.program_id(0) == 0)
        def _():
            state[...] = jnp.zeros_like(state)

        _, cum = _gla_gates(glr_ref, w2_ref, b_ref)
        for h in range(GH):
            _, last, _, _, _, qt, kt, kh = _gla_head(cum, q_ref, k_ref, h)
            vcols = slice(h * GDV, (h + 1) * GDV)
            st = state[h]
            st_ref[0, h] = st
            v = v_ref[:, vcols].astype(BF16)
            qb = qt.astype(BF16)
            a = _chunk_scores(qt, kt, q_ref, k_ref, h)
            o = _nt(qb, st.astype(BF16)) + _nn(a.astype(BF16), v)
            state[h] = st * jnp.exp(last) + _tn(v, kh.astype(BF16))
            o_ref[:, vcols] = o
            r = lax.rsqrt(jnp.mean(o * o, axis=-1, keepdims=True) + EPS)
            zg = zg_ref[:, vcols]
            bin_ref[:, vcols] = (o * r * gn_ref[...] * zg * _sigmoid(zg)).astype(BF16)

    row = lambda width, cblk: pl.BlockSpec((GLA_C, width), functools.partial(lambda i, c: (i, c), c=cblk))
    full = lambda a: pl.BlockSpec(a.shape, functools.partial(lambda i, nd: (0,) * nd, nd=a.ndim))
    return pl.pallas_call(
        body, name="gla_fwd", grid=(nc,),
        in_specs=[row(512, C_QG // 512), row(512, C_KG // 512), row(1024, C_VG // 1024), row(GLR_W, C_GLR // GLR_W),
                  row(1024, C_ZG // 1024), full(w2p), full(bg), full(gn)],
        out_specs=[pl.BlockSpec((GLA_C, GH * GDV), lambda i: (i, 0)), pl.BlockSpec((GLA_C, GH * GDV), lambda i: (i, 0)),
                   pl.BlockSpec((1, GH, GDV, GDK), lambda i: (i, 0, 0, 0))],
        out_shape=[S((T, GH * GDV), F32), S((T, GH * GDV), BF16), S((nc, GH, GDV, GDK), F32)],
        scratch_shapes=[pltpu.VMEM((GH, GDV, GDK), F32)],
        compiler_params=pltpu.CompilerParams(dimension_semantics=("arbitrary",)),
    )(proj, proj, proj, proj, proj, w2p, bg, gn)


def _gla_bwd(proj, w2p, bg, gn, o_gla, states, dbin, dproj):
    nc = T // GLA_C

    def body(q_ref, k_ref, v_ref, glr_ref, zg_ref, w2_ref, b_ref, gn_ref, o_ref, st_ref, dbin_ref, buf_ref,
             out_ref, dw2_ref, dbg_ref, dgn_ref, dstate, dlogit):
        del buf_ref
        dq_ref = out_ref.at[:, C_QG:C_KG]
        dk_ref = out_ref.at[:, C_KG:C_VG]
        dv_ref = out_ref.at[:, C_VG:C_ZG]
        dzg_ref = out_ref.at[:, C_ZG:C_GLR]
        dglr_ref = out_ref.at[:, C_GLR:C_GLR + GLR_W]
        first = pl.program_id(0) == 0

        @pl.when(first)
        def _():
            dstate[...] = jnp.zeros_like(dstate)

        logit, cum = _gla_gates(glr_ref, w2_ref, b_ref)
        is_last = lax.broadcasted_iota(jnp.int32, (GLA_C, 1), 0) == GLA_C - 1
        dgn = jnp.zeros((1, GDV), F32)
        for h in range(GH):
            _, last, e_pos, e_neg, e_end, qt, kt, kh = _gla_head(cum, q_ref, k_ref, h)
            cols = slice(h * GDK, (h + 1) * GDK)
            vcols = slice(h * GDV, (h + 1) * GDV)
            o = o_ref[:, vcols]
            r = lax.rsqrt(jnp.mean(o * o, axis=-1, keepdims=True) + EPS)
            zg = zg_ref[:, vcols]
            sg = _sigmoid(zg)
            db_ = dbin_ref[:, vcols].astype(F32)
            dlin = db_ * zg * sg
            dzg_ref[:, vcols] = (db_ * (o * r * gn_ref[...]) * sg * (1.0 + zg * (1.0 - sg))).astype(BF16)
            u = dlin * gn_ref[...]
            do = (r * u - o * (r * r * r) * jnp.mean(u * o, axis=-1, keepdims=True)).astype(BF16)
            dgn = dgn + jnp.sum(dlin * o * r, axis=0, keepdims=True)
            st = st_ref[0, h]
            dst = dstate[h]
            v = v_ref[:, vcols].astype(BF16)
            qb, kb, khb = qt.astype(BF16), kt.astype(BF16), kh.astype(BF16)
            dstb = dst.astype(BF16)
            causal = _causal(GLA_C)
            a = _chunk_scores(qt, kt, q_ref, k_ref, h).astype(BF16)
            da = jnp.where(causal, _nt(do, v), 0.0).astype(BF16)
            dqt = _nn(do, st.astype(BF16)) + _nn(da, kb)
            dkt = _tn(da, qb)
            dkh = _nn(v, dstb)
            dv_ref[:, vcols] = (_tn(a, do) + _nt(khb, dstb)).astype(BF16)
            lam = jnp.exp(last)
            dlam = jnp.sum(dst * st, axis=0, keepdims=True)
            dstate[h] = dst * lam + _tn(do, qb)
            dq_ref[:, cols] = (dqt * e_pos * (GDK ** -0.5)).astype(BF16)
            dk_ref[:, cols] = (dkt * e_neg + dkh * e_end).astype(BF16)
            dkh_kh = dkh * kh
            dcum = dqt * qt - dkt * kt - dkh_kh
            dlast = jnp.sum(dkh_kh, axis=0, keepdims=True) + dlam * lam
            dcum = jnp.where(is_last, dcum + dlast, dcum)
            dlg = _tri_sum(dcum, upper=True)
            dlogit[:, cols] = dlg * (1.0 / GLA_TAU) * (1.0 - _sigmoid(logit[:, cols]))

        dl = dlogit[...]
        dlb = dl.astype(BF16)
        dglr_ref[...] = _nt(dlb, w2_ref[...]).astype(BF16)
        dw2 = _tn(glr_ref[...].astype(BF16), dlb)
        dbg = jnp.sum(dl, axis=0, keepdims=True)

        @pl.when(first)
        def _():
            dw2_ref[...] = dw2
            dbg_ref[...] = dbg
            dgn_ref[...] = dgn

        @pl.when(jnp.logical_not(first))
        def _():
            dw2_ref[...] += dw2
            dbg_ref[...] += dbg
            dgn_ref[...] += dgn

    rev = lambda i: nc - 1 - i
    row = lambda width, cblk: pl.BlockSpec((GLA_C, width), functools.partial(lambda i, c: (rev(i), c), c=cblk))
    full = lambda a: pl.BlockSpec(a.shape, functools.partial(lambda i, nd: (0,) * nd, nd=a.ndim))
    keep = lambda shape: pl.BlockSpec(shape, functools.partial(lambda i, nd: (0,) * nd, nd=len(shape)))
    return pl.pallas_call(
        body, name="gla_bwd", grid=(nc,),
        in_specs=[row(512, C_QG // 512), row(512, C_KG // 512), row(1024, C_VG // 1024), row(GLR_W, C_GLR // GLR_W),
                  row(1024, C_ZG // 1024), full(w2p), full(bg), full(gn), row(GH * GDV, 0),
                  pl.BlockSpec((1, GH, GDV, GDK), lambda i: (rev(i), 0, 0, 0)), row(GH * GDV, 0),
                  pl.BlockSpec(memory_space=pl.ANY)],
        out_specs=[row(GLA_GROUP_W, 0), keep((GLR_W, 512)), keep((1, 512)), keep((1, GDV))],
        out_shape=[S(dproj.shape, dproj.dtype), S((GLR_W, 512), F32), S((1, 512), F32), S((1, GDV), F32)],
        input_output_aliases={11: 0},
        scratch_shapes=[pltpu.VMEM((GH, GDV, GDK), F32), pltpu.VMEM((GLA_C, GH * GDK), F32)],
        compiler_params=pltpu.CompilerParams(dimension_semantics=("arbitrary",)),
    )(proj, proj, proj, proj, proj, w2p, bg, gn, o_gla, states, dbin, dproj)


RT = 512


def _rowchain(body, name, ins, outs, scratch=()):
    in_specs, args = [], []
    for spec in ins:
        if spec[0] == "tok":
            _, arr, width, cblk = spec
            in_specs.append(pl.BlockSpec((RT, width), functools.partial(lambda i, c: (i, c), c=cblk)))
        else:
            arr = spec[1]
            in_specs.append(pl.BlockSpec(arr.shape, functools.partial(lambda i, nd: (0,) * nd, nd=arr.ndim)))
        args.append(arr)
    out_specs, out_shape = [], []
    for spec in outs:
        if spec[0] == "tok":
            _, shape, dtype, width, cblk = spec
            out_specs.append(pl.BlockSpec((RT, width), functools.partial(lambda i, c: (i, c), c=cblk)))
        else:
            _, shape, dtype = spec
            out_specs.append(pl.BlockSpec(shape, functools.partial(lambda i, nd: (0,) * nd, nd=len(shape))))
        out_shape.append(S(shape, dtype))
    return pl.pallas_call(
        body, name=name, grid=(T // RT,), in_specs=in_specs, out_specs=out_specs, out_shape=out_shape,
        scratch_shapes=list(scratch), compiler_params=pltpu.CompilerParams(dimension_semantics=("arbitrary",)),
    )(*args)


def _tok(arr, width=None, cblk=0):
    return ("tok", arr, arr.shape[1] if width is None else width, cblk)


def _tok_out(dtype, width=D):
    return ("tok", (T, width), dtype, width, 0)


def _branches_fwd(ain, bin_, proj, x, w_att, w_gla, w_out):
    def body(ain_ref, bin_ref, g_ref, x_ref, wa_ref, wg_ref, wo_ref, ya_ref, yb_ref, y_ref, x1_ref):
        ya = _nn(ain_ref[...], wa_ref[...]).astype(BF16)
        yb = _nn(bin_ref[...], wg_ref[...]).astype(BF16)
        ya_ref[...] = ya
        yb_ref[...] = yb
        y = (_sigmoid(g_ref[:, :D]) * ya.astype(F32) + _sigmoid(g_ref[:, D:]) * yb.astype(F32)).astype(BF16)
        y_ref[...] = y
        x1_ref[...] = x_ref[...] + _nn(y, wo_ref[...])

    return _rowchain(body, "branches_fwd",
                     [_tok(ain), _tok(bin_), _tok(proj, 2 * D, C_GA // (2 * D)), _tok(x), ("all", w_att),
                      ("all", w_gla), ("all", w_out)],
                     [_tok_out(BF16), _tok_out(BF16), _tok_out(BF16), _tok_out(F32)])


def _accumulate(ref, part, first):
    @pl.when(first)
    def _():
        ref[...] = part

    @pl.when(jnp.logical_not(first))
    def _():
        ref[...] += part


def _ple_loss(x1, p, target, g2, w_pg, w_ple):
    def body(x1_ref, p_ref, t_ref, g_ref, wpg_ref, wple_ref, n2_ref, loss_ref, dout_ref, du_ref, dwple_ref, acc):
        first = pl.program_id(0) == 0
        x1 = x1_ref[...]
        r = lax.rsqrt(jnp.mean(x1 * x1, axis=-1, keepdims=True) + EPS)
        n2 = (x1 * r * g_ref[...]).astype(BF16)
        n2_ref[...] = n2
        pg = _sigmoid(_nn(n2, wpg_ref[...]))
        pb = p_ref[...].astype(BF16)
        e_ = _nn(pb, wple_ref[...])
        diff = x1 + e_ * pg - t_ref[...]
        _accumulate(acc, jnp.sum(diff * diff, axis=0, keepdims=True), first)
        dout = diff * (1.0 / D)
        dout_ref[...] = dout
        du_ref[...] = (dout * e_ * pg * (1.0 - pg)).astype(BF16)
        _accumulate(dwple_ref, _tn(pb, (dout * pg).astype(BF16)), first)
        loss_ref[...] = jnp.zeros((1, 128), F32) + jnp.sum(acc[...], axis=-1, keepdims=True) * (0.5 / D)

    return _rowchain(body, "ple_loss", [_tok(x1), _tok(p), _tok(target), ("all", g2), ("all", w_pg), ("all", w_ple)],
                     [_tok_out(BF16), ("acc", (1, 128), F32), _tok_out(F32), _tok_out(BF16), ("acc", (PLE, D), F32)],
                     scratch=[pltpu.VMEM((1, D), F32)])


def _ple_bwd(du, n2, y, x1, dout, g2, w_pg, w_out):
    def body(du_ref, n2_ref, y_ref, x1_ref, dout_ref, g_ref, wpg_ref, wo_ref, dx_ref, dy_ref, dg_ref, dwpg_ref,
             dwo_ref):
        first = pl.program_id(0) == 0
        x1 = x1_ref[...]
        r = lax.rsqrt(jnp.mean(x1 * x1, axis=-1, keepdims=True) + EPS)
        du_ = du_ref[...]
        dn = _nt(du_, wpg_ref[...])
        u = dn * g_ref[...]
        dx = dout_ref[...] + r * u - x1 * (r * r * r) * jnp.mean(u * x1, axis=-1, keepdims=True)
        dxb = dx.astype(BF16)
        dx_ref[...] = dx
        dy_ref[...] = _nt(dxb, wo_ref[...]).astype(BF16)
        _accumulate(dg_ref, jnp.sum(dn * x1 * r, axis=0, keepdims=True), first)
        _accumulate(dwpg_ref, _tn(n2_ref[...], du_), first)
        _accumulate(dwo_ref, _tn(y_ref[...], dxb), first)

    return _rowchain(body, "ple_bwd",
                     [_tok(du), _tok(n2), _tok(y), _tok(x1), _tok(dout), ("all", g2), ("all", w_pg), ("all", w_out)],
                     [_tok_out(F32), _tok_out(BF16), ("acc", (1, D), F32), ("acc", (D, D), F32), ("acc", (D, D), F32)])


def _branches_bwd(dy, ya, yb, ain, bin_, proj, w_att, w_gla):
    def body(dy_ref, ya_ref, yb_ref, ain_ref, bin_ref, g_ref, wa_ref, wg_ref, dg_ref, dain_ref, dbin_ref,
             dwa_ref, dwg_ref):
        first = pl.program_id(0) == 0
        dy_ = dy_ref[...].astype(F32)
        sa, sb = _sigmoid(g_ref[:, :D]), _sigmoid(g_ref[:, D:])
        dg_ref[:, :D] = (dy_ * ya_ref[...].astype(F32) * sa * (1.0 - sa)).astype(BF16)
        dg_ref[:, D:] = (dy_ * yb_ref[...].astype(F32) * sb * (1.0 - sb)).astype(BF16)
        dya = (dy_ * sa).astype(BF16)
        dyb = (dy_ * sb).astype(BF16)
        dain_ref[...] = _nt(dya, wa_ref[...]).astype(BF16)
        dbin_ref[...] = _nt(dyb, wg_ref[...]).astype(BF16)
        _accumulate(dwa_ref, _tn(ain_ref[...], dya), first)
        _accumulate(dwg_ref, _tn(bin_ref[...], dyb), first)

    gates = C_GA // (2 * D)
    return _rowchain(body, "branches_bwd",
                     [_tok(dy), _tok(ya), _tok(yb), _tok(ain), _tok(bin_), _tok(proj, 2 * D, gates), ("all", w_att),
                      ("all", w_gla)],
                     [("tok", (T, NCOL), BF16, 2 * D, gates), _tok_out(BF16, ATT_W), _tok_out(BF16),
                      ("acc", (ATT_W, D), F32), ("acc", (D, D), F32)])


def _peer(k):
    x, y, c = lax.axis_index("x"), lax.axis_index("y"), lax.axis_index("c")
    return (x ^ ((k >> 2) & 1), y ^ ((k >> 1) & 1), c ^ (k & 1))


def _my_index():
    return 4 * lax.axis_index("x") + 2 * lax.axis_index("y") + lax.axis_index("c")


def _peer_index(k):
    px, py, pc = _peer(k)
    return 4 * px + 2 * py + pc


def _pairwise_plan(src_of, dst_of, landed_of, own_src, own_dst):
    def plan(ins, outs, send, recv, local):
        n = len(ins)

        def own():
            return [pltpu.make_async_copy(own_src(ins[a]), own_dst(outs[a]), local.at[a]) for a in range(n)]

        def remote(k, a, src, dst):
            return pltpu.make_async_remote_copy(src_ref=src, dst_ref=dst, send_sem=send.at[k - 1, a],
                                                recv_sem=recv.at[k - 1, a], device_id=_peer(k), device_id_type=MESH)

        def sent():
            return [remote(k, a, src_of(ins[a], k), dst_of(outs[a])) for k in range(1, NDEV) for a in range(n)]

        def start():
            for cp in own() + sent():
                cp.start()

        def finish():
            for k in range(1, NDEV):
                for a in range(n):
                    remote(k, a, own_src(ins[a]), landed_of(outs[a], k)).wait_recv()
            for cp in sent():
                cp.wait_send()
            for cp in own():
                cp.wait()

        return start, finish

    return plan


def _pairwise_sems(n):
    return [pltpu.SemaphoreType.DMA((NDEV - 1, n)), pltpu.SemaphoreType.DMA((NDEV - 1, n)),
            pltpu.SemaphoreType.DMA((n,))]


def _gather_side(arrs):
    plan = _pairwise_plan(src_of=lambda i, k: i, dst_of=lambda o: o.at[_my_index()],
                          landed_of=lambda o, k: o.at[_peer_index(k)],
                          own_src=lambda i: i, own_dst=lambda o: o.at[_my_index()])
    return dict(arrs=arrs, out_shape=[S((NDEV,) + a.shape, a.dtype) for a in arrs],
                scratch=_pairwise_sems(len(arrs)), plan=plan)


def _exchange_side(arrs):
    plan = _pairwise_plan(src_of=lambda i, k: i.at[_peer_index(k)], dst_of=lambda o: o.at[_my_index()],
                          landed_of=lambda o, k: o.at[_peer_index(k)],
                          own_src=lambda i: i.at[_my_index()], own_dst=lambda o: o.at[_my_index()])
    return dict(arrs=arrs, out_shape=[S(a.shape, a.dtype) for a in arrs], scratch=_pairwise_sems(len(arrs)), plan=plan)


def _comm_call(side, name):
    n = len(side["arrs"])

    def body(*refs):
        start, finish = side["plan"](refs[:n], refs[n:2 * n], *refs[2 * n:])
        start()
        finish()

    hbm = pl.BlockSpec(memory_space=pl.ANY)
    return pl.pallas_call(body, name=name, in_specs=[hbm] * n, out_specs=[hbm] * n, out_shape=side["out_shape"],
                          scratch_shapes=side["scratch"])(*side["arrs"])


def _all_gather_by_chip(arrs, name):
    n = len(arrs)

    def body(*refs):
        ins, outs = refs[:n], refs[n:2 * n]
        send, recv, local = refs[2 * n:]
        x, y, c = lax.axis_index("x"), lax.axis_index("y"), lax.axis_index("c")
        me, sibling = (x, y, c), (x, y, 1 - c)
        chips = [(1 - x, y), (x, 1 - y), (1 - x, 1 - y)]

        def copy(k, a, block, to, src=None):
            px, py, pc = block
            slot = outs[a].at[4 * px + 2 * py + pc]
            return pltpu.make_async_remote_copy(
                src_ref=slot if src is None else src, dst_ref=slot, send_sem=send.at[k, a], recv_sem=recv.at[k, a],
                device_id=to, device_id_type=MESH)

        north = c == 1
        via = (jnp.where(north, 1 - x, x), jnp.where(north, y, 1 - y))
        onward = (jnp.where(north, x, 1 - x), jnp.where(north, 1 - y, y), c)
        mine = [pltpu.make_async_copy(ins[a], outs[a].at[4 * x + 2 * y + c], local.at[a]) for a in range(n)]
        first = []
        for a in range(n):
            first.append(copy(0, a, me, sibling, src=ins[a]))
            first += [copy(1 + j, a, me, (*chips[j], c), src=ins[a]) for j in range(2)]
        for cp in mine + first:
            cp.start()
        passed = []
        for j in range(2):
            for a in range(n):
                copy(1 + j, a, (*chips[j], c), me).wait_recv()
                passed.append(copy(4 + j, a, (*chips[j], c), sibling))
                passed[-1].start()
        for a in range(n):
            passed.append(copy(3, a, (*via, c), onward))
            passed[-1].start()
        for a in range(n):
            copy(3, a, (*chips[2], c), me).wait_recv()
            passed.append(copy(6, a, (*chips[2], c), sibling))
            passed[-1].start()
        for a in range(n):
            copy(0, a, sibling, me).wait_recv()
        for j, chip in enumerate(chips):
            for a in range(n):
                copy(4 + j, a, (*chip, 1 - c), me).wait_recv()
        for cp in first + passed:
            cp.wait_send()
        for cp in mine:
            cp.wait()

    hbm = pl.BlockSpec(memory_space=pl.ANY)
    return pl.pallas_call(
        body, name=name, in_specs=[hbm] * n, out_specs=[hbm] * n,
        out_shape=[S((NDEV,) + a.shape, a.dtype) for a in arrs],
        scratch_shapes=[pltpu.SemaphoreType.DMA((NDEV - 1, n)), pltpu.SemaphoreType.DMA((NDEV - 1, n)),
                        pltpu.SemaphoreType.DMA((n,))],
    )(*arrs)


NCHIP = 4


def _sibling_sum(src, name, tc=512):
    _, rows, cols = src.shape
    assert cols % tc == 0

    def body(src_ref, got_ref, out_ref, a_buf, b_buf, o_buf, send, recv, local):
        x, y, c = lax.axis_index("x"), lax.axis_index("y"), lax.axis_index("c")
        copies = [pltpu.make_async_remote_copy(
            src_ref=src_ref.at[2 * q + (1 - c)], dst_ref=got_ref.at[q], send_sem=send.at[q], recv_sem=recv.at[q],
            device_id=(x, y, 1 - c), device_id_type=MESH) for q in range(NCHIP)]
        for cp in copies:
            cp.start()
        tiles = [(q, pl.ds(t * tc, tc)) for q in range(NCHIP) for t in range(cols // tc)]

        def loads(n):
            q, tile = tiles[n]
            return [pltpu.make_async_copy(src_ref.at[2 * q + c, :, tile], a_buf.at[n % 2], local.at[n % 2, 0]),
                    pltpu.make_async_copy(got_ref.at[q, :, tile], b_buf.at[n % 2], local.at[n % 2, 1])]

        def store(n):
            q, tile = tiles[n]
            return pltpu.make_async_copy(o_buf.at[n % 2], out_ref.at[q, :, tile], local.at[n % 2, 2])

        def fetch(n):
            if n == 0 or tiles[n][0] != tiles[n - 1][0]:
                copies[tiles[n][0]].wait_recv()
            for cp in loads(n):
                cp.start()

        fetch(0)
        for n in range(len(tiles)):
            if n + 1 < len(tiles):
                fetch(n + 1)
            for cp in loads(n):
                cp.wait()
            if n >= 2:
                store(n - 2).wait()
            o_buf[n % 2] = (a_buf[n % 2].astype(F32) + b_buf[n % 2].astype(F32)).astype(BF16)
            store(n).start()
        store(len(tiles) - 2).wait()
        store(len(tiles) - 1).wait()
        for cp in copies:
            cp.wait_send()

    hbm = pl.BlockSpec(memory_space=pl.ANY)
    block = S((NCHIP, rows, cols), BF16)
    return pl.pallas_call(
        body, name=name, in_specs=[hbm], out_specs=[hbm, hbm], out_shape=[block, block],
        scratch_shapes=[pltpu.VMEM((2, rows, tc), BF16)] * 3
        + [pltpu.SemaphoreType.DMA((NCHIP,)), pltpu.SemaphoreType.DMA((NCHIP,)), pltpu.SemaphoreType.DMA((2, 3))],
    )(src)[1]


def _chips_side(arrs):
    def plan(ins, outs, send, recv, local):
        n = len(ins)

        def places():
            x, y, c = lax.axis_index("x"), lax.axis_index("y"), lax.axis_index("c")
            return 2 * x + y, c, [(1 - x, y), (x, 1 - y), (1 - x, 1 - y)]

        def own():
            here, _, _ = places()
            return [pltpu.make_async_copy(ins[a].at[here], outs[a].at[here], local.at[a]) for a in range(n)]

        def remote(j, a, src_slot, dst_slot):
            _, c, chips = places()
            cx, cy = chips[j]
            return pltpu.make_async_remote_copy(
                src_ref=ins[a].at[src_slot], dst_ref=outs[a].at[dst_slot], send_sem=send.at[j, a],
                recv_sem=recv.at[j, a], device_id=(cx, cy, c), device_id_type=MESH)

        def sent():
            here, _, chips = places()
            return [remote(j, a, 2 * cx + cy, here) for j, (cx, cy) in enumerate(chips) for a in range(n)]

        def start():
            for cp in own() + sent():
                cp.start()

        def finish():
            here, _, chips = places()
            for j, (cx, cy) in enumerate(chips):
                for a in range(n):
                    remote(j, a, here, 2 * cx + cy).wait_recv()
            for cp in sent():
                cp.wait_send()
            for cp in own():
                cp.wait()

        return start, finish

    n = len(arrs)
    return dict(arrs=arrs, out_shape=[S(a.shape, a.dtype) for a in arrs],
                scratch=[pltpu.SemaphoreType.DMA((NCHIP - 1, n)), pltpu.SemaphoreType.DMA((NCHIP - 1, n)),
                         pltpu.SemaphoreType.DMA((n,))], plan=plan)


def _adamw_shards(parts, places):
    n_src = len(parts)

    def body(*refs):
        srcs, rest = refs[:n_src], refs[n_src:]
        for j, (src, rows, cols, _) in enumerate(places):
            w_ref, m_ref, v_ref = rest[3 * j:3 * j + 3]
            outs = rest[3 * len(places) + 4 * j:3 * len(places) + 4 * j + 4]
            p_ref = srcs[src]
            g = p_ref[0, rows, cols].astype(F32)
            for s in range(1, p_ref.shape[0]):
                g = g + p_ref[s, rows, cols].astype(F32)
            delta, m_new, v_new = _adam_math(g, w_ref[0], m_ref[0], v_ref[0])
            for ref, val in zip(outs, (g, delta, m_new, v_new)):
                ref[0] = val

    flat = [a for place in places for a in place[3]]
    return pl.pallas_call(
        body, name="adam_shards",
        out_shape=[S(place[3][0].shape, F32) for place in places for _ in range(4)],
    )(*parts, *flat)


def _adam_math(g, w, m, v):
    c1 = 1.0 - ADAM_B1 ** ADAM_STEP
    c2 = 1.0 - ADAM_B2 ** ADAM_STEP
    m_new = ADAM_B1 * m + (1.0 - ADAM_B1) * g
    v_new = ADAM_B2 * v + (1.0 - ADAM_B2) * (g * g)
    return -ADAM_LR * ((m_new / c1) / (jnp.sqrt(v_new / c2) + ADAM_EPS) + ADAM_WD * w), m_new, v_new


def _adamw_small(parts, params, loss_parts):
    n = len(params)

    def body(*refs):
        p_refs, rest = refs[:n], refs[n + 1:]
        total = refs[n][0]
        for s in range(1, NDEV):
            total = total + refs[n][s]
        refs[-1][...] = total
        for j in range(n):
            w_ref, m_ref, v_ref = rest[3 * j:3 * j + 3]
            g_ref, d_ref, mo_ref, vo_ref = rest[3 * n + 4 * j:3 * n + 4 * j + 4]
            width = w_ref.shape[1]
            g = p_refs[j][0]
            for s in range(1, NDEV):
                g = g + p_refs[j][s]
            g = g[:, :width]
            delta, m_new, v_new = _adam_math(g, w_ref[...], m_ref[...], v_ref[...])
            g_ref[...] = g
            d_ref[...] = delta
            mo_ref[...] = m_new
            vo_ref[...] = v_new

    flat = [a for group in params for a in group]
    return pl.pallas_call(
        body, name="adam_small",
        out_shape=[S(group[0].shape, F32) for group in params for _ in range(4)] + [S((1, 128), F32)],
    )(*parts, loss_parts, *flat)


def _adamw_rows(parts, w, m, v, name, tc=256):
    rows, _, cols = w.shape
    nparts = parts.shape[0]
    nsteps = cols // tc

    def body(p_ref, w_hbm, m_hbm, v_hbm, g_hbm, d_hbm, mo_hbm, vo_hbm, inbuf, outbuf, insem, outsem):
        i = pl.program_id(0)
        slot = i & 1

        def view(ref, step):
            return ref.at[:, 0, pl.ds(pl.multiple_of(step * tc, tc), tc)]

        def fetch(step, sl):
            return [pltpu.make_async_copy(view(src, step), inbuf.at[sl, k], insem.at[sl, k])
                    for k, src in enumerate((w_hbm, m_hbm, v_hbm))]

        def write(step, sl):
            return [pltpu.make_async_copy(outbuf.at[sl, k], view(dst, step), outsem.at[sl, k])
                    for k, dst in enumerate((g_hbm, d_hbm, mo_hbm, vo_hbm))]

        @pl.when(i == 0)
        def _():
            for cp in fetch(0, 0):
                cp.start()

        @pl.when(i + 1 < nsteps)
        def _():
            for cp in fetch(i + 1, 1 - slot):
                cp.start()

        for cp in fetch(i, slot):
            cp.wait()

        @pl.when(i >= 2)
        def _():
            for cp in write(i - 2, slot):
                cp.wait()

        g = p_ref[0].astype(F32)
        for s in range(1, nparts):
            g = g + p_ref[s].astype(F32)
        g = g[:rows]
        delta, m_new, v_new = _adam_math(g, inbuf[slot, 0], inbuf[slot, 1], inbuf[slot, 2])
        for k, val in enumerate((g, delta, m_new, v_new)):
            outbuf[slot, k] = val
        for cp in write(i, slot):
            cp.start()

        @pl.when(i == nsteps - 1)
        def _():
            for cp in write(i - 1, 1 - slot) + write(i, slot):
                cp.wait()

    hbm = pl.BlockSpec(memory_space=pl.ANY)
    assert nsteps >= 2
    return pl.pallas_call(
        body, name=name, grid=(nsteps,),
        in_specs=[pl.BlockSpec((nparts, parts.shape[1], tc), lambda i: (0, 0, i)), hbm, hbm, hbm],
        out_specs=[hbm] * 4, out_shape=[S((rows, 1, cols), F32)] * 4,
        scratch_shapes=[pltpu.VMEM((2, 3, rows, tc), F32), pltpu.VMEM((2, 4, rows, tc), F32),
                        pltpu.SemaphoreType.DMA((2, 3)), pltpu.SemaphoreType.DMA((2, 4))],
        compiler_params=pltpu.CompilerParams(dimension_semantics=("arbitrary",)),
    )(parts, w, m, v)


SLAB = 1296
REMAP_RUNS = 4
_PIECES = ((O_QA, O_ZA, C_QA), (O_ZA, O_QG, C_ZA), (O_QG, O_GLR, C_QG), (O_GLR, O_ZG, C_GLR), (O_ZG, O_GA, C_ZG),
           (O_GA, O_END, C_GA))


def _slab_row_of_aligned(a):
    for o0, o1, a0 in _PIECES:
        if a0 <= a < a0 + o1 - o0:
            c = o0 + a - a0
            return SLAB * (c // W_IN_SHARD) + c % W_IN_SHARD
    return -1


def _aligned_row_of_slab(r):
    d, l = divmod(r, SLAB)
    if l >= W_IN_SHARD:
        return -1
    c = d * W_IN_SHARD + l
    for o0, o1, a0 in _PIECES:
        if o0 <= c < o1:
            return a0 + c - o0
    raise AssertionError(c)


def _remap_table(row_of, n_out, block, n_src):
    win = block + 16
    table = []
    for b in range(n_out // block):
        runs = []
        for i in range(block):
            s = row_of(b * block + i)
            if s < 0:
                continue
            if runs and runs[-1][0] + runs[-1][2] == s and runs[-1][1] + runs[-1][2] == i:
                runs[-1][2] += 1
            else:
                runs.append([s, i, 1])
        assert len(runs) <= REMAP_RUNS, (b, runs)
        row = []
        for s, i, n in runs:
            w = min(s // 16 * 16, n_src - win)
            assert 0 <= s - w and s - w + n <= win
            row += [w, s - w, i, n]
        table.append(row + [0] * (4 * REMAP_RUNS - len(row)))
    return table


def _remap_rows(src, row_of, n_out, block, name):
    n_src, cols = src.shape
    nb, win = n_out // block, block + 16
    table = _remap_table(row_of, n_out, block, n_src)
    runs = [[tuple(row[4 * k:4 * k + 4]) for k in range(REMAP_RUNS) if row[4 * k + 3] > 0] for row in table]

    def body(src_hbm, out_hbm, wbuf, obuf, insem, outsem):
        def fetches(b):
            return [pltpu.make_async_copy(src_hbm.at[pl.ds(w, win)], wbuf.at[b % 2, k], insem.at[b % 2, k])
                    for k, (w, _, _, _) in enumerate(runs[b])]

        def store(b):
            return pltpu.make_async_copy(obuf.at[b % 2], out_hbm.at[pl.ds(b * block, block)], outsem.at[b % 2])

        for cp in fetches(0):
            cp.start()
        for b in range(nb):
            if b + 1 < nb:
                for cp in fetches(b + 1):
                    cp.start()
            for cp in fetches(b):
                cp.wait()
            if b >= 2:
                store(b - 2).wait()
            if sum(count for _, _, _, count in runs[b]) < block:
                obuf[b % 2] = jnp.zeros((block, cols), src.dtype)
            for k, (_, shift, first, count) in enumerate(runs[b]):
                obuf[b % 2, first:first + count, :] = wbuf[b % 2, k, shift:shift + count, :]
            store(b).start()
        store(nb - 2).wait()
        store(nb - 1).wait()

    hbm = pl.BlockSpec(memory_space=pl.ANY)
    return pl.pallas_call(
        body, name=name, in_specs=[hbm], out_specs=hbm, out_shape=S((n_out, cols), src.dtype),
        scratch_shapes=[pltpu.VMEM((2, REMAP_RUNS, win, cols), src.dtype), pltpu.VMEM((2, block, cols), src.dtype),
                        pltpu.SemaphoreType.DMA((2, REMAP_RUNS)), pltpu.SemaphoreType.DMA((2,))],
    )(src)


def _col_blocks(w, width):
    return w.reshape(w.shape[0], NDEV, width).transpose(1, 0, 2)


def _from_col_blocks(w):
    return w.transpose(1, 0, 2).reshape(w.shape[1], NDEV * w.shape[2])


def _local_step(x2, p2, pos, tgt, norm_g, qk_norm_q, qk_norm_k, gla_gate_b, gla_norm_g, ple_norm_g, w_al,
                weights=None, proj_side=None, unpack=None, dw_side_of=None, dh_side_of=None):
    half = ROT_DIM // 2
    inv8 = jnp.power(jnp.float32(ROPE_THETA), -jnp.arange(half, dtype=F32) * 2.0 / ROT_DIM)
    inv = jnp.tile(jnp.concatenate([inv8, inv8, jnp.zeros((HD - ROT_DIM,), F32)]), 2).reshape(1, 128)
    gq = jnp.tile(qk_norm_q, (1, 2))
    gk = jnp.tile(qk_norm_k, (1, 2))

    proj, h, got = _proj_rms(x2, norm_g, w_al, proj_side)
    if proj_side is not None:
        weights = unpack(got)
    w2p, w_att_f, w_gla_f, w_out_f, w_pg_f, w_ple_f = weights
    qkv = _qk_prep(proj, pos, inv, gq, gk)
    fwd = [_att_fwd(qkv[g], qkv[3 + g], qkv[6 + g], g, f"att_fwd{g}") for g in range(3)]
    att, lse, ain = _att_merge([f[0] for f in fwd], [f[1] for f in fwd], proj)
    o_gla, bin_, states = _gla_fwd(proj, w2p, gla_gate_b, gla_norm_g)
    ya, yb, y, x1 = _branches_fwd(ain, bin_, proj, x2, w_att_f, w_gla_f, w_out_f)
    n2, loss_v, dout, du, dw_ple = _ple_loss(x1, p2, tgt, ple_norm_g, w_pg_f, w_ple_f)

    dx1, dy, dg_ple, dw_pg, dw_out = _ple_bwd(du, n2, y, x1, dout, ple_norm_g, w_pg_f, w_out_f)
    dproj, dain, dbin, dw_att, dw_gla = _branches_bwd(dy, ya, yb, ain, bin_, proj, w_att_f, w_gla_f)
    dproj, da0, da1, da2, at1, at2, ls1, ls2 = _att_gate_bwd(dain, att, lse, proj, dproj)
    datts, atts, lses = (da0, da1, da2), (att[None], at1, at2), (lse[None], ls1, ls2)
    dproj, dw2, dbg, dgn = _gla_bwd(proj, w2p, gla_gate_b, gla_norm_g, o_gla, states, dbin, dproj)
    bwd = [_att_bwd(qkv[g], qkv[3 + g], qkv[6 + g], datts[g], atts[g], lses[g], g, f"att_bwd{g}") for g in range(3)]
    dproj, dgq, dgk = _qk_bwd(proj, pos, inv, gq, gk, [b[0] for b in bwd], [b[1] for b in bwd],
                              [b[2] for b in bwd], dproj)
    out = dict(loss=loss_v, dw2=dw2, dw_att=dw_att, dw_gla=dw_gla, dw_out=dw_out, dw_pg=dw_pg, dw_ple=dw_ple,
               dgq=dgq, dgk=dgk, dbg=dbg, dgn=dgn, dg_ple=dg_ple)
    if dw_side_of is None:
        dw_al = _mm(dproj, h, mode="tn", name="dw_in", tm=1536, tn=D, tk=T, out_dtype=BF16)
    else:
        dw_al, out["dw_side"] = _mm(dproj, h, mode="tn", name="dw_in", tm=1536, tn=D, tk=T, out_dtype=BF16,
                                    side=dw_side_of(out))
    grad_x, dg_norm, out["dh_side"] = _dh_rms(dproj, w_al, x2, norm_g, dx1,
                                              None if dh_side_of is None else dh_side_of(dw_al))
    out.update(grad_x=grad_x, dw_al=dw_al, dg_norm=dg_norm)
    return out


def kernel(x, p, positions, norm_g, w_in, qk_norm_q, qk_norm_k, gla_gate_w2, gla_gate_b, gla_norm_g, w_att_proj, w_gla_proj, w_out, ple_norm_g, w_ple_gate, w_ple, loss_target, m_norm_g, m_w_in, m_qk_norm_q, m_qk_norm_k, m_gla_gate_w2, m_gla_gate_b, m_gla_norm_g, m_w_att_proj, m_w_gla_proj, m_w_out, m_ple_norm_g, m_w_ple_gate, m_w_ple, v_norm_g, v_w_in, v_qk_norm_q, v_qk_norm_k, v_gla_gate_w2, v_gla_gate_b, v_gla_norm_g, v_w_att_proj, v_w_gla_proj, v_w_out, v_ple_norm_g, v_w_ple_gate, v_w_ple):
    x2, p2, tgt = x[0], p[0, 0], loss_target[0]
    pos = positions.astype(F32).reshape(T, 1)

    rows3 = jnp.stack([w_gla_proj[0], w_out[0], w_ple_gate[0]]).astype(BF16)
    cols3 = jnp.concatenate([w_att_proj[0], w_ple[0], jnp.pad(gla_gate_w2[0], ((0, 0), (0, 64)))], axis=0).astype(BF16)
    mine = jnp.pad(w_in[0].T.astype(BF16), ((0, SLAB - W_IN_SHARD), (0, 0)))
    (g_in,) = _all_gather_by_chip([mine], "gather_w_in")
    w_al = _remap_rows(g_in.reshape(NDEV * SLAB, D), _slab_row_of_aligned, NCOL, 1536, "align_w_in")

    def unpack(got):
        g_rows, g_cols = got
        w2_f = _from_col_blocks(g_cols[:, 768:784, :64])
        return (jnp.pad(w2_f, ((0, GLR_W - GLR_N), (0, 0))), _from_col_blocks(g_cols[:, :512]),
                g_rows[:, 0].reshape(D, D), g_rows[:, 1].reshape(D, D), g_rows[:, 2].reshape(D, D),
                _from_col_blocks(g_cols[:, 512:768]))

    def dw_side_of(g):
        s_rows = jnp.concatenate([g[k].reshape(NDEV, 128, D) for k in ("dw_gla", "dw_out", "dw_pg")], axis=1)
        s_cols = jnp.concatenate([_col_blocks(g["dw_att"], 128), _col_blocks(g["dw_ple"], 128),
                                  jnp.pad(_col_blocks(g["dw2"][:GLR_N], 64), ((0, 0), (0, 0), (0, 64)))], axis=1)
        return _exchange_side([s_rows.astype(BF16), s_cols.astype(BF16)])

    def dh_side_of(dw_al):
        s_in = _remap_rows(dw_al, _aligned_row_of_slab, NDEV * SLAB, SLAB, "shard_dw_in").reshape(NDEV, SLAB, D)
        return _chips_side([_sibling_sum(s_in, "sibling_sum")])

    loc = _local_step(x2, p2, pos, tgt, norm_g, qk_norm_q, qk_norm_k, gla_gate_b, gla_norm_g, ple_norm_g, w_al,
                      proj_side=_gather_side([rows3, cols3]), unpack=unpack, dw_side_of=dw_side_of,
                      dh_side_of=dh_side_of)
    loss_v, grad_x = loc["loss"], loc["grad_x"]
    dg_norm, dgq, dgk, dbg, dgn, dg_ple = (loc[k] for k in ("dg_norm", "dgq", "dgk", "dbg", "dgn", "dg_ple"))
    r_rows, r_cols = loc["dw_side"]
    (r_in,) = loc["dh_side"]

    r_small = _comm_call(_gather_side([dg_norm, dgq, dgk, dbg, dgn, dg_ple, loss_v]), "gather_small")

    outs = {}

    rows_of = lambda a: jnp.transpose(a, (2, 0, 1))
    outs["w_in"] = [jnp.transpose(o, (1, 2, 0))[0] for o in
                    _adamw_rows(r_in, rows_of(w_in), rows_of(m_w_in), rows_of(v_w_in), "adam_w_in")]
    places = (("w_gla_proj", 0, slice(0, 128), slice(None), (w_gla_proj, m_w_gla_proj, v_w_gla_proj)),
              ("w_out", 0, slice(128, 256), slice(None), (w_out, m_w_out, v_w_out)),
              ("w_ple_gate", 0, slice(256, 384), slice(None), (w_ple_gate, m_w_ple_gate, v_w_ple_gate)),
              ("w_att_proj", 1, slice(0, 512), slice(None), (w_att_proj, m_w_att_proj, v_w_att_proj)),
              ("w_ple", 1, slice(512, 768), slice(None), (w_ple, m_w_ple, v_w_ple)),
              ("gla_gate_w2", 1, slice(768, 784), slice(0, 64), (gla_gate_w2, m_gla_gate_w2, v_gla_gate_w2)))
    res = _adamw_shards([r_rows, r_cols], [place[1:] for place in places])
    for j, place in enumerate(places):
        outs[place[0]] = [o[0] for o in res[4 * j:4 * j + 4]]
    small = ((norm_g, m_norm_g, v_norm_g), (qk_norm_q, m_qk_norm_q, v_qk_norm_q), (qk_norm_k, m_qk_norm_k, v_qk_norm_k),
             (gla_gate_b, m_gla_gate_b, v_gla_gate_b), (gla_norm_g, m_gla_norm_g, v_gla_norm_g),
             (ple_norm_g, m_ple_norm_g, v_ple_norm_g))
    sm = _adamw_small(r_small[:6], small, r_small[6])
    for j, nm in enumerate(("norm_g", "qk_norm_q", "qk_norm_k", "gla_gate_b", "gla_norm_g", "ple_norm_g")):
        outs[nm] = [o[0] for o in sm[4 * j:4 * j + 4]]

    loss = sm[-1][0, 0]
    order = ["norm_g", "w_in", "qk_norm_q", "qk_norm_k", "gla_gate_w2", "gla_gate_b", "gla_norm_g", "w_att_proj",
             "w_gla_proj", "w_out", "ple_norm_g", "w_ple_gate", "w_ple"]
    result = [loss, grad_x[None]]
    for i in range(4):
        result += [outs[nm][i][None] for nm in order]
    return tuple(result)
```

```python
import functools

import jax
import jax.numpy as jnp
from jax import lax
from jax.experimental import pallas as pl
from jax.experimental.pallas import tpu as pltpu

F32 = jnp.float32
BF16 = jnp.bfloat16
S = jax.ShapeDtypeStruct

T = 4096
D = 1024
NDEV = 8
HD = 64
ATT_W = 512
ATT_QKV = 1536
DILATIONS = (1, 4, 16)
BLK = 128
GH, GDK, GDV = 4, 128, 256
GLA_C = 128
PLE = 256
EPS = 1e-6
ROT_DIM = 16
ROPE_THETA = 500000.0
GLA_TAU = 16.0
W_IN_SHARD = 1282

C_QG, C_KG, C_VG, C_ZG, C_GLR, C_ZA, C_GA, C_GB, C_QA, C_KA, C_VA = (
    0, 512, 1024, 2048, 3072, 3584, 4096, 5120, 6144, 7680, 9216)
GLA_GROUP_W = 3584
GLR_W = 512
NCOL = 10752
GLR_N = 16
O_QA, O_ZA, O_QG, O_GLR, O_ZG, O_GA, O_END = 0, 4608, 5120, 7168, 7184, 8208, 10256

ADAM_LR, ADAM_B1, ADAM_B2, ADAM_EPS, ADAM_WD, ADAM_STEP = 0.001, 0.9, 0.999, 1e-08, 0.01, 10

MESH = pl.DeviceIdType.MESH


def _sigmoid(z):
    return 1.0 / (1.0 + jnp.exp(-z))


def _dot(a, b, dims):
    return lax.dot_general(a, b, (dims, ((), ())), preferred_element_type=F32)


def _nn(a, b):
    return _dot(a, b, ((1,), (0,)))


def _nt(a, b):
    return _dot(a, b, ((1,), (1,)))


def _tn(a, b):
    return _dot(a, b, ((0,), (0,)))


def _mm(a, b, *, mode, name, tm, tn, tk, out_dtype=F32, res=None, side=None):
    if mode == "nn":
        (m, k), n = a.shape, b.shape[1]
        a_spec = pl.BlockSpec((tm, tk), lambda i, j, l: (i, l))
        b_spec = pl.BlockSpec((tk, tn), lambda i, j, l: (l, j))
        dot = _nn
    elif mode == "nt":
        (m, k), n = a.shape, b.shape[0]
        a_spec = pl.BlockSpec((tm, tk), lambda i, j, l: (i, l))
        b_spec = pl.BlockSpec((tn, tk), lambda i, j, l: (j, l))
        dot = _nt
    else:
        (k, m), n = a.shape, b.shape[1]
        a_spec = pl.BlockSpec((tk, tm), lambda i, j, l: (l, i))
        b_spec = pl.BlockSpec((tk, tn), lambda i, j, l: (l, j))
        dot = _tn
    assert m % tm == 0 and n % tn == 0 and k % tk == 0, (name, m, n, k)
    grid = (m // tm, n // tn, k // tk)
    nk = grid[2]
    o_spec = pl.BlockSpec((tm, tn), lambda i, j, l: (i, j))
    in_specs = [a_spec, b_spec]
    args = [a, b]
    if res is not None:
        in_specs.append(o_spec)
        args.append(res)
    n_in = len(args)
    n_side = 0 if side is None else len(side["arrs"])
    hbm = pl.BlockSpec(memory_space=pl.ANY)

    def body(*refs):
        a_ref, b_ref = refs[0], refs[1]
        r_ref = refs[2] if res is not None else None
        o_ref = refs[n_in + n_side]
        scratch = refs[n_in + 2 * n_side + 1:]
        if side is not None:
            start, finish_side = side["plan"](refs[n_in:n_in + n_side], refs[n_in + n_side + 1:n_in + 2 * n_side + 1],
                                              *scratch[1 if nk > 1 else 0:])
            ids = [pl.program_id(d) for d in range(3)]

            @pl.when((ids[0] == 0) & (ids[1] == 0) & (ids[2] == 0))
            def _():
                start()

        part = dot(a_ref[...].astype(BF16), b_ref[...].astype(BF16))

        def finish(val):
            if r_ref is not None:
                val = val + r_ref[...]
            o_ref[...] = val.astype(out_dtype)

        if nk == 1:
            finish(part)
        else:
            acc = scratch[0]
            l = pl.program_id(2)

            @pl.when(l == 0)
            def _():
                acc[...] = part

            @pl.when(l > 0)
            def _():
                acc[...] += part

            @pl.when(l == nk - 1)
            def _():
                finish(acc[...])

        if side is not None:
            @pl.when((ids[0] == grid[0] - 1) & (ids[1] == grid[1] - 1) & (ids[2] == grid[2] - 1))
            def _():
                finish_side()

    sems = [] if side is None else side["scratch"]
    outs = pl.pallas_call(
        body, name=name, grid=grid,
        in_specs=in_specs + [hbm] * n_side, out_specs=[o_spec] + [hbm] * n_side,
        out_shape=[S((m, n), out_dtype)] + ([] if side is None else side["out_shape"]),
        scratch_shapes=([pltpu.VMEM((tm, tn), F32)] if nk > 1 else []) + sems,
        compiler_params=pltpu.CompilerParams(
            dimension_semantics=("arbitrary",) * 3 if side is not None else ("parallel", "parallel", "arbitrary")),
    )(*args, *([] if side is None else side["arrs"]))
    return outs[0] if side is None else (outs[0], outs[1:])


def _side_parts(side, refs, n_in, n_out):
    n_side = 0 if side is None else len(side["arrs"])
    scratch = refs[n_in + n_out + 2 * n_side:]
    if side is None:
        return (lambda: None), (lambda: None), scratch
    start, finish = side["plan"](refs[n_in:n_in + n_side], refs[n_in + n_side + n_out:n_in + n_out + 2 * n_side],
                                 *scratch[len(scratch) - len(side["scratch"]):])
    return start, finish, scratch


def _proj_rms(x, g, wt, side=None):
    tm, tn = 1024, 1536
    grid = (T // tm, NCOL // tn)
    n_side = 0 if side is None else len(side["arrs"])
    hbm = pl.BlockSpec(memory_space=pl.ANY)

    def body(*refs):
        x_ref, g_ref, w_ref = refs[:3]
        o_ref, h_ref = refs[3 + n_side], refs[4 + n_side]
        start, finish, _ = _side_parts(side, refs, 3, 2)
        i, j = pl.program_id(0), pl.program_id(1)

        @pl.when((i == 0) & (j == 0))
        def _():
            start()

        @pl.when(j == 0)
        def _():
            xf = x_ref[...]
            r = lax.rsqrt(jnp.mean(xf * xf, axis=-1, keepdims=True) + EPS)
            h_ref[...] = (xf * r * g_ref[...]).astype(BF16)

        o_ref[...] = _nt(h_ref[...], w_ref[...])

        @pl.when((i == grid[0] - 1) & (j == grid[1] - 1))
        def _():
            finish()

    outs = pl.pallas_call(
        body, name="proj", grid=grid,
        in_specs=[pl.BlockSpec((tm, D), lambda i, j: (i, 0)), pl.BlockSpec((1, D), lambda i, j: (0, 0)),
                  pl.BlockSpec((tn, D), lambda i, j: (j, 0))] + [hbm] * n_side,
        out_specs=[pl.BlockSpec((tm, tn), lambda i, j: (i, j)), pl.BlockSpec((tm, D), lambda i, j: (i, 0))] + [hbm] * n_side,
        out_shape=[S((T, NCOL), F32), S((T, D), BF16)] + ([] if side is None else side["out_shape"]),
        scratch_shapes=[] if side is None else side["scratch"],
        compiler_params=pltpu.CompilerParams(dimension_semantics=("arbitrary", "arbitrary")),
    )(x, g, wt, *([] if side is None else side["arrs"]))
    return outs[0], outs[1], outs[2:]


def _dh_rms(dproj, wt, x, g, skip, side=None):
    tm, tk = 1024, 2688
    grid = (T // tm, NCOL // tk)
    n_side = 0 if side is None else len(side["arrs"])
    hbm = pl.BlockSpec(memory_space=pl.ANY)

    def body(*refs):
        a_ref, w_ref, x_ref, g_ref, s_ref = refs[:5]
        dx_ref, dg_ref = refs[5 + n_side], refs[6 + n_side]
        start, finish, scratch = _side_parts(side, refs, 5, 2)
        acc = scratch[0]
        i, l = pl.program_id(0), pl.program_id(1)

        @pl.when((i == 0) & (l == 0))
        def _():
            start()

        part = _nn(a_ref[...], w_ref[...])

        @pl.when(l == 0)
        def _():
            acc[...] = part

        @pl.when(l > 0)
        def _():
            acc[...] += part

        @pl.when(l == grid[1] - 1)
        def _():
            xf = x_ref[...]
            r = lax.rsqrt(jnp.mean(xf * xf, axis=-1, keepdims=True) + EPS)
            dn = acc[...]
            u = dn * g_ref[...]
            dx_ref[...] = s_ref[...] + r * u - xf * (r * r * r) * jnp.mean(u * xf, axis=-1, keepdims=True)
            dg = jnp.sum(dn * xf * r, axis=0, keepdims=True)

            @pl.when(i == 0)
            def _():
                dg_ref[...] = dg

            @pl.when(i > 0)
            def _():
                dg_ref[...] += dg

        @pl.when((i == grid[0] - 1) & (l == grid[1] - 1))
        def _():
            finish()

    tok = pl.BlockSpec((tm, D), lambda i, l: (i, 0))
    outs = pl.pallas_call(
        body, name="dh", grid=grid,
        in_specs=[pl.BlockSpec((tm, tk), lambda i, l: (i, l)), pl.BlockSpec((tk, D), lambda i, l: (l, 0)), tok,
                  pl.BlockSpec((1, D), lambda i, l: (0, 0)), tok] + [hbm] * n_side,
        out_specs=[tok, pl.BlockSpec((1, D), lambda i, l: (0, 0))] + [hbm] * n_side,
        out_shape=[S((T, D), F32), S((1, D), F32)] + ([] if side is None else side["out_shape"]),
        scratch_shapes=[pltpu.VMEM((tm, D), F32)] + ([] if side is None else side["scratch"]),
        compiler_params=pltpu.CompilerParams(dimension_semantics=("arbitrary", "arbitrary")),
    )(dproj, wt, x, g, skip, *([] if side is None else side["arrs"]))
    return outs[0], outs[1], outs[2:]


def _rot_tables(pos_ref, inv_ref):
    lane = lax.broadcasted_iota(jnp.int32, (1, 128), 1) % HD
    ang = pos_ref[...] * inv_ref[...]
    cos, sin = jnp.cos(ang), jnp.sin(ang)
    c = jnp.where(lane < ROT_DIM, cos, 1.0)
    sp = jnp.where((lane >= ROT_DIM // 2) & (lane < ROT_DIM), sin, 0.0)
    sm = jnp.where(lane < ROT_DIM // 2, -sin, 0.0)
    return c, sp, sm


def _head_sums(v):
    same = (lax.broadcasted_iota(jnp.int32, (128, 128), 0) < HD) == (lax.broadcasted_iota(jnp.int32, (128, 128), 1) < HD)
    ones = jnp.where(same, 1.0, 0.0).astype(BF16)
    hi = v.astype(BF16)
    lo = (v - hi.astype(F32)).astype(BF16)
    return _nn(hi, ones) + _nn(lo, ones)


def _pair_norm(t):
    return lax.rsqrt(_head_sums(t * t) * (1.0 / HD) + EPS)


def _pair_mean(t):
    return _head_sums(t) * (1.0 / HD)


TT = 256
NCH = ATT_QKV // 128


def _res_shape(grp, dtype):
    return S((DILATIONS[grp], T // DILATIONS[grp], ATT_W), dtype)


def _res_spec(grp):
    dil = DILATIONS[grp]
    return pl.BlockSpec((dil, TT // dil, ATT_W), lambda i: (0, i, 0))


def _to_residues(sc, j, dst_ref, dil, cols):
    n = TT // dil
    for r in range(dil):
        rows = sc[j] if dil == 1 else sc.at[j][pl.ds(r, n, stride=dil), :]
        dst_ref[r, :, cols] = rows.astype(dst_ref.dtype)


def _from_residues(src_ref, cols, sc, j, dil):
    n = TT // dil
    for r in range(dil):
        if dil == 1:
            sc[j] = src_ref[r, :, cols]
        else:
            sc.at[j][pl.ds(r, n, stride=dil), :] = src_ref[r, :, cols]


def _tok_spec(width, cblk=0):
    return pl.BlockSpec((TT, width), functools.partial(lambda i, c: (i, c), c=cblk))


def _const_spec(arr_or_shape):
    shape = arr_or_shape if isinstance(arr_or_shape, tuple) else arr_or_shape.shape
    return pl.BlockSpec(shape, functools.partial(lambda i, nd: (0,) * nd, nd=len(shape)))


def _qk_prep(proj, pos, inv, gq, gk):
    def body(q_ref, k_ref, v_ref, pos_ref, inv_ref, gq_ref, gk_ref, *rest):
        outs, sc = rest[:9], rest[9]
        c, sp, sm = _rot_tables(pos_ref, inv_ref)
        for which, (src, g_ref) in enumerate(((q_ref, gq_ref), (k_ref, gk_ref), (v_ref, None))):
            if g_ref is not None:
                g = jnp.broadcast_to(g_ref[...] * ((HD ** -0.5) if which == 0 else 1.0), c.shape)
                cg, spg, smg = c * g, sp * pltpu.roll(g, 8, 1), sm * pltpu.roll(g, 120, 1)
            for j in range(NCH):
                t = src[:, j * 128:(j + 1) * 128]
                if g_ref is not None:
                    t = _pair_norm(t) * (t * cg + pltpu.roll(t, 8, 1) * spg + pltpu.roll(t, 120, 1) * smg)
                sc[j] = t
            for j in range(NCH):
                grp, sub = divmod(j * 128, ATT_W)
                _to_residues(sc, j, outs[which * 3 + grp], DILATIONS[grp], slice(sub, sub + 128))

    return pl.pallas_call(
        body, name="qk_prep", grid=(T // TT,),
        in_specs=[_tok_spec(ATT_QKV, C_QA // ATT_QKV), _tok_spec(ATT_QKV, C_KA // ATT_QKV),
                  _tok_spec(ATT_QKV, C_VA // ATT_QKV), _tok_spec(1), _const_spec(inv), _const_spec(gq), _const_spec(gk)],
        out_specs=[_res_spec(g) for _ in range(3) for g in range(3)],
        out_shape=[_res_shape(g, BF16) for _ in range(3) for g in range(3)],
        scratch_shapes=[pltpu.VMEM((NCH, TT, 128), F32)],
        compiler_params=pltpu.CompilerParams(dimension_semantics=("arbitrary",)),
    )(proj, proj, proj, pos, inv, gq, gk)


def _qk_bwd(proj, pos, inv, gq, gk, dqs, dks, dvs, dproj):
    const = lambda a: pl.BlockSpec(a.shape, functools.partial(lambda i, p, nd: (0,) * nd, nd=a.ndim))
    res = lambda g: pl.BlockSpec((DILATIONS[g], TT // DILATIONS[g], ATT_W), lambda i, p: (0, i, 0))
    base = C_QA // ATT_QKV

    def body(t_ref, pos_ref, inv_ref, gq_ref, gk_ref, dq0, dq1, dq2, dk0, dk1, dk2, dv0, dv1, dv2, buf_ref,
             out_ref, dgq_ref, dgk_ref, sc):
        del buf_ref
        part = pl.program_id(1)
        first = pl.program_id(0) == 0

        def gather(drefs):
            for j in range(NCH):
                grp, sub = divmod(j * 128, ATT_W)
                _from_residues(drefs[grp], slice(sub, sub + 128), sc, j, DILATIONS[grp])

        def normed(g_ref, drefs, dg_ref):
            c, sp, sm = _rot_tables(pos_ref, inv_ref)
            gather(drefs)
            dg = jnp.zeros((1, 128), F32)
            for j in range(NCH):
                cols = slice(j * 128, (j + 1) * 128)
                d_rot = sc[j]
                dn = d_rot * c + pltpu.roll(d_rot * sp, 120, 1) + pltpu.roll(d_rot * sm, 8, 1)
                t = t_ref[:, cols]
                r = _pair_norm(t)
                gain = g_ref[...]
                dn_t = dn * t
                out_ref[:, cols] = (r * (dn * gain - t * ((r * r) * _pair_mean(dn_t * gain)))).astype(BF16)
                dg = dg + jnp.sum(dn_t * r, axis=0, keepdims=True)
            dg = dg + pltpu.roll(dg, HD, 1)

            @pl.when(first)
            def _():
                dg_ref[...] = dg

            @pl.when(jnp.logical_not(first))
            def _():
                dg_ref[...] += dg

        @pl.when(part == 0)
        def _():
            gather((dv0, dv1, dv2))
            for j in range(NCH):
                out_ref[:, j * 128:(j + 1) * 128] = sc[j].astype(BF16)

        @pl.when(part == 1)
        def _():
            normed(gq_ref, (dq0, dq1, dq2), dgq_ref)

        @pl.when(part == 2)
        def _():
            normed(gk_ref, (dk0, dk1, dk2), dgk_ref)

    keep = pl.BlockSpec((1, 128), lambda i, p: (0, 0))
    return pl.pallas_call(
        body, name="qk_bwd", grid=(T // TT, 3),
        in_specs=[pl.BlockSpec((TT, ATT_QKV), lambda i, p: (i, base + jnp.maximum(p - 1, 0))),
                  pl.BlockSpec((TT, 1), lambda i, p: (i, 0)), const(inv), const(gq), const(gk)]
        + [res(g) for _ in range(3) for g in range(3)] + [pl.BlockSpec(memory_space=pl.ANY)],
        out_specs=[pl.BlockSpec((TT, ATT_QKV), lambda i, p: (i, base + jnp.where(p == 0, 2, p - 1))), keep, keep],
        out_shape=[S(dproj.shape, dproj.dtype), S((1, 128), F32), S((1, 128), F32)],
        input_output_aliases={14: 0},
        scratch_shapes=[pltpu.VMEM((NCH, TT, 128), F32)],
        compiler_params=pltpu.CompilerParams(dimension_semantics=("arbitrary", "arbitrary")),
    )(proj, pos, inv, gq, gk, *dqs, *dks, *dvs, dproj)


def _split_heads(t):
    low = lax.broadcasted_iota(jnp.int32, (1, 128), 1) < HD
    zero = jnp.zeros_like(t)
    return jnp.concatenate([jnp.where(low, t, zero), jnp.where(low, zero, t)], axis=0)


def _join_heads(t2):
    low = lax.broadcasted_iota(jnp.int32, (1, 128), 1) < HD
    n = t2.shape[0] // 2
    return jnp.where(low, t2[:n], t2[n:])


def _band_mask4(has_before, has_own):
    row = lax.broadcasted_iota(jnp.int32, (BLK, 4 * BLK), 0)
    lane = lax.broadcasted_iota(jnp.int32, (BLK, 4 * BLK), 1)
    key = lane & (BLK - 1)
    own = lane >= 2 * BLK
    return (own & (key <= row) & has_own) | (jnp.logical_not(own) & (key >= row) & has_before)


def _band_mask_before(has_before):
    row = lax.broadcasted_iota(jnp.int32, (BLK, 2 * BLK), 0)
    key = lax.broadcasted_iota(jnp.int32, (BLK, 2 * BLK), 1) & (BLK - 1)
    return (key >= row) & has_before


def _per_head(width, col_a, col_b):
    lane = lax.broadcasted_iota(jnp.int32, (1, width), 1)
    return jnp.where((lane & BLK) == 0, col_a, col_b)


NQ = ATT_W // 128


def _att_fwd(q, k, v, grp, name):
    dil = DILATIONS[grp]
    nb = T // dil // BLK

    def body(q_ref, kp_ref, kc_ref, vp_ref, vc_ref, o_ref, lse_ref, s_sc, p_sc):
        mask = _band_mask4(pl.program_id(1) > 0, True)
        low = lax.broadcasted_iota(jnp.int32, (1, 128), 1) < HD
        halves = lambda ref, j, h: (ref[j, :, h * BLK:(h + 1) * BLK], ref[j, :, (h + 2) * BLK:(h + 3) * BLK])
        for j in range(NQ):
            cols = slice(j * 128, (j + 1) * 128)
            k4 = jnp.concatenate([_split_heads(kp_ref[:, cols]), _split_heads(kc_ref[:, cols])], axis=0)
            s_sc[j] = jnp.where(mask, _nt(q_ref[:, cols], k4), -jnp.inf)
        mxs = [[jnp.maximum(*(jnp.max(t, axis=-1, keepdims=True) for t in halves(s_sc, j, h))) for h in range(2)]
               for j in range(NQ)]
        dens = []
        for j in range(NQ):
            p = jnp.exp(s_sc[j] - _per_head(4 * BLK, *mxs[j]))
            p_sc[j] = p.astype(BF16)
            dens.append([jnp.sum(p[:, h * BLK:(h + 1) * BLK], axis=-1, keepdims=True)
                         + jnp.sum(p[:, (h + 2) * BLK:(h + 3) * BLK], axis=-1, keepdims=True) for h in range(2)])
        for j in range(NQ):
            cols = slice(j * 128, (j + 1) * 128)
            v4 = jnp.concatenate([_split_heads(vp_ref[:, cols]), _split_heads(vc_ref[:, cols])], axis=0)
            o_ref[:, cols] = _nn(p_sc[j], v4) / jnp.where(low, dens[j][0], dens[j][1])
            lse_ref[:, cols] = jnp.where(low, mxs[j][0] + jnp.log(dens[j][0]), mxs[j][1] + jnp.log(dens[j][1]))

    cur = pl.BlockSpec((None, BLK, ATT_W), lambda r, i: (r, i, 0))
    prev = pl.BlockSpec((None, BLK, ATT_W), lambda r, i: (r, jnp.maximum(i - 1, 0), 0))
    return pl.pallas_call(
        body, name=name, grid=(dil, nb),
        in_specs=[cur, prev, cur, prev, cur],
        out_specs=[cur, cur], out_shape=[_res_shape(grp, F32)] * 2,
        scratch_shapes=[pltpu.VMEM((NQ, BLK, 4 * BLK), F32), pltpu.VMEM((NQ, BLK, 4 * BLK), BF16)],
        compiler_params=pltpu.CompilerParams(dimension_semantics=("parallel", "arbitrary")),
    )(q, k, k, v, v)


def _att_bwd(q, k, v, datt, att, lse, grp, name):
    dil = DILATIONS[grp]
    nb = T // dil // BLK
    scale = HD ** -0.5

    def body(q0_ref, q1_ref, kp_ref, kc_ref, vp_ref, vc_ref, do0_ref, do1_ref, o0_ref, o1_ref, l0_ref, l1_ref,
             dq_ref, dk_ref, dv_ref, k4_sc, v4_sc, s0_sc, s1_sc, dp0_sc, dp1_sc, p_sc, ds_sc):
        i = pl.program_id(1)
        mask_mine = _band_mask4(i > 0, True)
        mask_next = _band_mask_before(i < nb - 1)
        low = lax.broadcasted_iota(jnp.int32, (1, 128), 1) < HD
        for j in range(NQ):
            cols = slice(j * 128, (j + 1) * 128)
            k4_sc[j, :2 * BLK] = _split_heads(kp_ref[:, cols])
            k4_sc[j, 2 * BLK:] = _split_heads(kc_ref[:, cols])
            v4_sc[j, :2 * BLK] = _split_heads(vp_ref[:, cols])
            v4_sc[j, 2 * BLK:] = _split_heads(vc_ref[:, cols])
        for j in range(NQ):
            cols = slice(j * 128, (j + 1) * 128)
            s0_sc[j] = _nt(q0_ref[:, cols], k4_sc[j])
            s1_sc[j] = _nt(q1_ref[:, cols], k4_sc[j, 2 * BLK:])
            dp0_sc[j] = _nt(do0_ref[:, cols].astype(BF16), v4_sc[j])
            dp1_sc[j] = _nt(do1_ref[:, cols].astype(BF16), v4_sc[j, 2 * BLK:])
        stats = []
        for j in range(NQ):
            cols = slice(j * 128, (j + 1) * 128)
            for do_ref, o_ref, l_ref in ((do0_ref, o0_ref, l0_ref), (do1_ref, o1_ref, l1_ref)):
                prod = do_ref[:, cols].astype(F32) * o_ref[:, cols].astype(F32)
                d_all = jnp.sum(prod, axis=-1, keepdims=True)
                d_low = jnp.sum(jnp.where(low, prod, 0.0), axis=-1, keepdims=True)
                lse_t = l_ref[:, cols]
                stats.append((d_low, d_all - d_low, lse_t[:, 0:1], lse_t[:, HD:HD + 1]))
        for j in range(NQ):
            (da, db, la, lb), (da1, db1, la1, lb1) = stats[2 * j], stats[2 * j + 1]
            p0 = jnp.where(mask_mine, jnp.exp(s0_sc[j] - _per_head(4 * BLK, la, lb)), 0.0)
            ds0 = p0 * (dp0_sc[j] - _per_head(4 * BLK, da, db))
            p1 = jnp.where(mask_next, jnp.exp(s1_sc[j] - _per_head(2 * BLK, la1, lb1)), 0.0)
            ds1 = p1 * (dp1_sc[j] - _per_head(2 * BLK, da1, db1))
            p_sc[j, :BLK] = p0.astype(BF16)
            ds_sc[j, :BLK] = ds0.astype(BF16)
            p_sc[j, BLK:, 2 * BLK:] = p1.astype(BF16)
            ds_sc[j, BLK:, 2 * BLK:] = ds1.astype(BF16)
        for j in range(NQ):
            cols = slice(j * 128, (j + 1) * 128)
            dq_ref[:, cols] = _nn(ds_sc[j, :BLK], k4_sc[j]) * scale
            qq = jnp.concatenate([q0_ref[:, cols], q1_ref[:, cols]], axis=0)
            dd = jnp.concatenate([do0_ref[:, cols], do1_ref[:, cols]], axis=0).astype(BF16)
            dk_ref[:, cols] = _join_heads(_tn(ds_sc[j, :, 2 * BLK:], qq))
            dv_ref[:, cols] = _join_heads(_tn(p_sc[j, :, 2 * BLK:], dd))

    def spec(shift):
        return pl.BlockSpec((None, BLK, ATT_W), lambda r, i: (r, jnp.clip(i + shift, 0, nb - 1), 0))

    here, after, before = spec(0), spec(1), spec(-1)
    vm = pltpu.VMEM
    return pl.pallas_call(
        body, name=name, grid=(dil, nb),
        in_specs=[here, after, before, here, before, here, here, after, here, after, here, after],
        out_specs=[here] * 3, out_shape=[_res_shape(grp, F32)] * 3,
        scratch_shapes=[vm((NQ, 4 * BLK, 128), BF16), vm((NQ, 4 * BLK, 128), BF16), vm((NQ, BLK, 4 * BLK), F32),
                        vm((NQ, BLK, 2 * BLK), F32), vm((NQ, BLK, 4 * BLK), F32), vm((NQ, BLK, 2 * BLK), F32),
                        vm((NQ, 2 * BLK, 4 * BLK), BF16), vm((NQ, 2 * BLK, 4 * BLK), BF16)],
        compiler_params=pltpu.CompilerParams(dimension_semantics=("parallel", "arbitrary")),
    )(q, q, k, k, v, v, datt, datt, att, att, lse, lse)


def _att_merge(os_, lses, proj):
    nq = ATT_W // 128

    def body(o0, o1, o2, l0, l1, l2, za_ref, att_ref, lse_ref, ain_ref, sc):
        for a, ref in enumerate((o0, o1, o2, l0, l1, l2)):
            for j in range(nq):
                _from_residues(ref, slice(j * 128, (j + 1) * 128), sc, a * nq + j, DILATIONS[a % 3])
        for j in range(nq):
            cols = slice(j * 128, (j + 1) * 128)
            oa, ob, oc = (sc[a * nq + j] for a in range(3))
            la, lb, lc = (sc[(3 + a) * nq + j] for a in range(3))
            m = jnp.maximum(jnp.maximum(la, lb), lc)
            wa, wb, wc = jnp.exp(la - m), jnp.exp(lb - m), jnp.exp(lc - m)
            tot = wa + wb + wc
            att = (wa * oa + wb * ob + wc * oc) / tot
            att_ref[:, cols] = att
            lse_ref[:, cols] = m + jnp.log(tot)
            za = za_ref[:, cols]
            ain_ref[:, cols] = (att * za * _sigmoid(za)).astype(BF16)

    return pl.pallas_call(
        body, name="att_merge", grid=(T // TT,),
        in_specs=[_res_spec(g) for _ in range(2) for g in range(3)] + [_tok_spec(ATT_W, C_ZA // ATT_W)],
        out_specs=[_tok_spec(ATT_W)] * 3,
        out_shape=[S((T, ATT_W), F32), S((T, ATT_W), F32), S((T, ATT_W), BF16)],
        scratch_shapes=[pltpu.VMEM((6 * nq, TT, 128), F32)],
        compiler_params=pltpu.CompilerParams(dimension_semantics=("arbitrary",)),
    )(*os_, *lses, proj)


def _att_gate_bwd(dain, att, lse, proj, dproj):
    nq = ATT_W // 128

    def body(d_ref, att_ref, lse_ref, za_ref, buf_ref, dza_ref, da0, da1, da2, at1, at2, ls1, ls2, sc):
        del buf_ref
        for j in range(nq):
            cols = slice(j * 128, (j + 1) * 128)
            za = za_ref[:, cols]
            sg = _sigmoid(za)
            d = d_ref[:, cols].astype(F32)
            att_ = att_ref[:, cols]
            dza_ref[:, cols] = (d * att_ * sg * (1.0 + za * (1.0 - sg))).astype(BF16)
            sc[j] = d * za * sg
            sc[nq + j] = att_
            sc[2 * nq + j] = lse_ref[:, cols]
        for j in range(nq):
            cols = slice(j * 128, (j + 1) * 128)
            for grp, dst in enumerate((da0, da1, da2)):
                _to_residues(sc, j, dst, DILATIONS[grp], cols)
            for grp, dst in ((1, at1), (2, at2)):
                _to_residues(sc, nq + j, dst, DILATIONS[grp], cols)
            for grp, dst in ((1, ls1), (2, ls2)):
                _to_residues(sc, 2 * nq + j, dst, DILATIONS[grp], cols)

    res = (0, 1, 2, 1, 2, 1, 2)
    return pl.pallas_call(
        body, name="att_gate_bwd", grid=(T // TT,),
        in_specs=[_tok_spec(ATT_W)] * 3 + [_tok_spec(ATT_W, C_ZA // ATT_W), pl.BlockSpec(memory_space=pl.ANY)],
        out_specs=[_tok_spec(ATT_W, C_ZA // ATT_W)] + [_res_spec(g) for g in res],
        out_shape=[S(dproj.shape, dproj.dtype)] + [_res_shape(g, BF16) for g in res[:5]]
        + [_res_shape(g, F32) for g in res[5:]],
        input_output_aliases={4: 0},
        scratch_shapes=[pltpu.VMEM((3 * nq, TT, 128), F32)],
        compiler_params=pltpu.CompilerParams(dimension_semantics=("arbitrary",)),
    )(dain, att, lse, proj, dproj)


def _split3(v):
    hi = v.astype(BF16)
    r1 = v - hi.astype(F32)
    mid = r1.astype(BF16)
    lo = (r1 - mid.astype(F32)).astype(BF16)
    return hi, mid, lo


def _chunk_scores(qt, kt, q_ref, k_ref, h):
    cols = slice(h * GDK, (h + 1) * GDK)
    own = jnp.sum(q_ref[:, cols] * (GDK ** -0.5) * k_ref[:, cols], axis=-1, keepdims=True)
    row = lax.broadcasted_iota(jnp.int32, (GLA_C, GLA_C), 0)
    col = lax.broadcasted_iota(jnp.int32, (GLA_C, GLA_C), 1)
    a = _nt(qt.astype(BF16), kt.astype(BF16))
    return jnp.where(col < row, a, jnp.where(col == row, own, 0.0))


def _tri_sum(v, upper):
    n = v.shape[0]
    row = lax.broadcasted_iota(jnp.int32, (n, n), 0)
    col = lax.broadcasted_iota(jnp.int32, (n, n), 1)
    tri = jnp.where(col >= row if upper else col <= row, 1.0, 0.0).astype(BF16)
    hi, mid, lo = _split3(v)
    return _nn(tri, hi) + _nn(tri, mid) + _nn(tri, lo)


def _gla_gates(glr_ref, w2_ref, b_ref):
    logit = _nn(glr_ref[...].astype(BF16), w2_ref[...]) + b_ref[...]
    lg = (jnp.minimum(logit, 0.0) - jnp.log(1.0 + jnp.exp(-jnp.abs(logit)))) * (1.0 / GLA_TAU)
    return logit, _tri_sum(lg, upper=False)


def _gla_head(cum, q_ref, k_ref, h):
    cols = slice(h * GDK, (h + 1) * GDK)
    b = cum[:, cols]
    last = b[GLA_C - 1:GLA_C, :]
    e_pos = jnp.exp(b)
    e_neg = jnp.exp(-b)
    e_end = jnp.exp(last - b)
    qt = q_ref[:, cols] * (GDK ** -0.5) * e_pos
    kt = k_ref[:, cols] * e_neg
    kh = k_ref[:, cols] * e_end
    return b, last, e_pos, e_neg, e_end, qt, kt, kh


def _causal(n):
    return lax.broadcasted_iota(jnp.int32, (n, n), 1) <= lax.broadcasted_iota(jnp.int32, (n, n), 0)


def _gla_fwd(proj, w2p, bg, gn):
    nc = T // GLA_C

    def body(q_ref, k_ref, v_ref, glr_ref, zg_ref, w2_ref, b_ref, gn_ref, o_ref, bin_ref, st_ref, state):
        @pl.when(pl.program_id(0) == 0)
        def _():
            state[...] = jnp.zeros_like(state)

        _, cum = _gla_gates(glr_ref, w2_ref, b_ref)
        for h in range(GH):
            _, last, _, _, _, qt, kt, kh = _gla_head(cum, q_ref, k_ref, h)
            vcols = slice(h * GDV, (h + 1) * GDV)
            st = state[h]
            st_ref[0, h] = st
            v = v_ref[:, vcols].astype(BF16)
            qb = qt.astype(BF16)
            a = _chunk_scores(qt, kt, q_ref, k_ref, h)
            o = _nt(qb, st.astype(BF16)) + _nn(a.astype(BF16), v)
            state[h] = st * jnp.exp(last) + _tn(v, kh.astype(BF16))
            o_ref[:, vcols] = o
            r = lax.rsqrt(jnp.mean(o * o, axis=-1, keepdims=True) + EPS)
            zg = zg_ref[:, vcols]
            bin_ref[:, vcols] = (o * r * gn_ref[...] * zg * _sigmoid(zg)).astype(BF16)

    row = lambda width, cblk: pl.BlockSpec((GLA_C, width), functools.partial(lambda i, c: (i, c), c=cblk))
    full = lambda a: pl.BlockSpec(a.shape, functools.partial(lambda i, nd: (0,) * nd, nd=a.ndim))
    return pl.pallas_call(
        body, name="gla_fwd", grid=(nc,),
        in_specs=[row(512, C_QG // 512), row(512, C_KG // 512), row(1024, C_VG // 1024), row(GLR_W, C_GLR // GLR_W),
                  row(1024, C_ZG // 1024), full(w2p), full(bg), full(gn)],
        out_specs=[pl.BlockSpec((GLA_C, GH * GDV), lambda i: (i, 0)), pl.BlockSpec((GLA_C, GH * GDV), lambda i: (i, 0)),
                   pl.BlockSpec((1, GH, GDV, GDK), lambda i: (i, 0, 0, 0))],
        out_shape=[S((T, GH * GDV), F32), S((T, GH * GDV), BF16), S((nc, GH, GDV, GDK), F32)],
        scratch_shapes=[pltpu.VMEM((GH, GDV, GDK), F32)],
        compiler_params=pltpu.CompilerParams(dimension_semantics=("arbitrary",)),
    )(proj, proj, proj, proj, proj, w2p, bg, gn)


def _gla_bwd(proj, w2p, bg, gn, o_gla, states, dbin, dproj):
    nc = T // GLA_C

    def body(q_ref, k_ref, v_ref, glr_ref, zg_ref, w2_ref, b_ref, gn_ref, o_ref, st_ref, dbin_ref, buf_ref,
             out_ref, dw2_ref, dbg_ref, dgn_ref, dstate, dlogit):
        del buf_ref
        dq_ref = out_ref.at[:, C_QG:C_KG]
        dk_ref = out_ref.at[:, C_KG:C_VG]
        dv_ref = out_ref.at[:, C_VG:C_ZG]
        dzg_ref = out_ref.at[:, C_ZG:C_GLR]
        dglr_ref = out_ref.at[:, C_GLR:C_GLR + GLR_W]
        first = pl.program_id(0) == 0

        @pl.when(first)
        def _():
            dstate[...] = jnp.zeros_like(dstate)

        logit, cum = _gla_gates(glr_ref, w2_ref, b_ref)
        is_last = lax.broadcasted_iota(jnp.int32, (GLA_C, 1), 0) == GLA_C - 1
        dgn = jnp.zeros((1, GDV), F32)
        for h in range(GH):
            _, last, e_pos, e_neg, e_end, qt, kt, kh = _gla_head(cum, q_ref, k_ref, h)
            cols = slice(h * GDK, (h + 1) * GDK)
            vcols = slice(h * GDV, (h + 1) * GDV)
            o = o_ref[:, vcols]
            r = lax.rsqrt(jnp.mean(o * o, axis=-1, keepdims=True) + EPS)
            zg = zg_ref[:, vcols]
            sg = _sigmoid(zg)
            db_ = dbin_ref[:, vcols].astype(F32)
            dlin = db_ * zg * sg
            dzg_ref[:, vcols] = (db_ * (o * r * gn_ref[...]) * sg * (1.0 + zg * (1.0 - sg))).astype(BF16)
            u = dlin * gn_ref[...]
            do = (r * u - o * (r * r * r) * jnp.mean(u * o, axis=-1, keepdims=True)).astype(BF16)
            dgn = dgn + jnp.sum(dlin * o * r, axis=0, keepdims=True)
            st = st_ref[0, h]
            dst = dstate[h]
            v = v_ref[:, vcols].astype(BF16)
            qb, kb, khb = qt.astype(BF16), kt.astype(BF16), kh.astype(BF16)
            dstb = dst.astype(BF16)
            causal = _causal(GLA_C)
            a = _chunk_scores(qt, kt, q_ref, k_ref, h).astype(BF16)
            da = jnp.where(causal, _nt(do, v), 0.0).astype(BF16)
            dqt = _nn(do, st.astype(BF16)) + _nn(da, kb)
            dkt = _tn(da, qb)
            dkh = _nn(v, dstb)
            dv_ref[:, vcols] = (_tn(a, do) + _nt(khb, dstb)).astype(BF16)
            lam = jnp.exp(last)
            dlam = jnp.sum(dst * st, axis=0, keepdims=True)
            dstate[h] = dst * lam + _tn(do, qb)
            dq_ref[:, cols] = (dqt * e_pos * (GDK ** -0.5)).astype(BF16)
            dk_ref[:, cols] = (dkt * e_neg + dkh * e_end).astype(BF16)
            dkh_kh = dkh * kh
            dcum = dqt * qt - dkt * kt - dkh_kh
            dlast = jnp.sum(dkh_kh, axis=0, keepdims=True) + dlam * lam
            dcum = jnp.where(is_last, dcum + dlast, dcum)
            dlg = _tri_sum(dcum, upper=True)
            dlogit[:, cols] = dlg * (1.0 / GLA_TAU) * (1.0 - _sigmoid(logit[:, cols]))

        dl = dlogit[...]
        dlb = dl.astype(BF16)
        dglr_ref[...] = _nt(dlb, w2_ref[...]).astype(BF16)
        dw2 = _tn(glr_ref[...].astype(BF16), dlb)
        dbg = jnp.sum(dl, axis=0, keepdims=True)

        @pl.when(first)
        def _():
            dw2_ref[...] = dw2
            dbg_ref[...] = dbg
            dgn_ref[...] = dgn

        @pl.when(jnp.logical_not(first))
        def _():
            dw2_ref[...] += dw2
            dbg_ref[...] += dbg
            dgn_ref[...] += dgn

    rev = lambda i: nc - 1 - i
    row = lambda width, cblk: pl.BlockSpec((GLA_C, width), functools.partial(lambda i, c: (rev(i), c), c=cblk))
    full = lambda a: pl.BlockSpec(a.shape, functools.partial(lambda i, nd: (0,) * nd, nd=a.ndim))
    keep = lambda shape: pl.BlockSpec(shape, functools.partial(lambda i, nd: (0,) * nd, nd=len(shape)))
    return pl.pallas_call(
        body, name="gla_bwd", grid=(nc,),
        in_specs=[row(512, C_QG // 512), row(512, C_KG // 512), row(1024, C_VG // 1024), row(GLR_W, C_GLR // GLR_W),
                  row(1024, C_ZG // 1024), full(w2p), full(bg), full(gn), row(GH * GDV, 0),
                  pl.BlockSpec((1, GH, GDV, GDK), lambda i: (rev(i), 0, 0, 0)), row(GH * GDV, 0),
                  pl.BlockSpec(memory_space=pl.ANY)],
        out_specs=[row(GLA_GROUP_W, 0), keep((GLR_W, 512)), keep((1, 512)), keep((1, GDV))],
        out_shape=[S(dproj.shape, dproj.dtype), S((GLR_W, 512), F32), S((1, 512), F32), S((1, GDV), F32)],
        input_output_aliases={11: 0},
        scratch_shapes=[pltpu.VMEM((GH, GDV, GDK), F32), pltpu.VMEM((GLA_C, GH * GDK), F32)],
        compiler_params=pltpu.CompilerParams(dimension_semantics=("arbitrary",)),
    )(proj, proj, proj, proj, proj, w2p, bg, gn, o_gla, states, dbin, dproj)


RT = 512


def _rowchain(body, name, ins, outs, scratch=()):
    in_specs, args = [], []
    for spec in ins:
        if spec[0] == "tok":
            _, arr, width, cblk = spec
            in_specs.append(pl.BlockSpec((RT, width), functools.partial(lambda i, c: (i, c), c=cblk)))
        else:
            arr = spec[1]
            in_specs.append(pl.BlockSpec(arr.shape, functools.partial(lambda i, nd: (0,) * nd, nd=arr.ndim)))
        args.append(arr)
    out_specs, out_shape = [], []
    for spec in outs:
        if spec[0] == "tok":
            _, shape, dtype, width, cblk = spec
            out_specs.append(pl.BlockSpec((RT, width), functools.partial(lambda i, c: (i, c), c=cblk)))
        else:
            _, shape, dtype = spec
            out_specs.append(pl.BlockSpec(shape, functools.partial(lambda i, nd: (0,) * nd, nd=len(shape))))
        out_shape.append(S(shape, dtype))
    return pl.pallas_call(
        body, name=name, grid=(T // RT,), in_specs=in_specs, out_specs=out_specs, out_shape=out_shape,
        scratch_shapes=list(scratch), compiler_params=pltpu.CompilerParams(dimension_semantics=("arbitrary",)),
    )(*args)


def _tok(arr, width=None, cblk=0):
    return ("tok", arr, arr.shape[1] if width is None else width, cblk)


def _tok_out(dtype, width=D):
    return ("tok", (T, width), dtype, width, 0)


def _branches_fwd(ain, bin_, proj, x, w_att, w_gla, w_out):
    def body(ain_ref, bin_ref, g_ref, x_ref, wa_ref, wg_ref, wo_ref, ya_ref, yb_ref, y_ref, x1_ref):
        ya = _nn(ain_ref[...], wa_ref[...]).astype(BF16)
        yb = _nn(bin_ref[...], wg_ref[...]).astype(BF16)
        ya_ref[...] = ya
        yb_ref[...] = yb
        y = (_sigmoid(g_ref[:, :D]) * ya.astype(F32) + _sigmoid(g_ref[:, D:]) * yb.astype(F32)).astype(BF16)
        y_ref[...] = y
        x1_ref[...] = x_ref[...] + _nn(y, wo_ref[...])

    return _rowchain(body, "branches_fwd",
                     [_tok(ain), _tok(bin_), _tok(proj, 2 * D, C_GA // (2 * D)), _tok(x), ("all", w_att),
                      ("all", w_gla), ("all", w_out)],
                     [_tok_out(BF16), _tok_out(BF16), _tok_out(BF16), _tok_out(F32)])


def _accumulate(ref, part, first):
    @pl.when(first)
    def _():
        ref[...] = part

    @pl.when(jnp.logical_not(first))
    def _():
        ref[...] += part


def _ple_loss(x1, p, target, g2, w_pg, w_ple):
    def body(x1_ref, p_ref, t_ref, g_ref, wpg_ref, wple_ref, n2_ref, loss_ref, dout_ref, du_ref, dwple_ref, acc):
        first = pl.program_id(0) == 0
        x1 = x1_ref[...]
        r = lax.rsqrt(jnp.mean(x1 * x1, axis=-1, keepdims=True) + EPS)
        n2 = (x1 * r * g_ref[...]).astype(BF16)
        n2_ref[...] = n2
        pg = _sigmoid(_nn(n2, wpg_ref[...]))
        pb = p_ref[...].astype(BF16)
        e_ = _nn(pb, wple_ref[...])
        diff = x1 + e_ * pg - t_ref[...]
        _accumulate(acc, jnp.sum(diff * diff, axis=0, keepdims=True), first)
        dout = diff * (1.0 / D)
        dout_ref[...] = dout
        du_ref[...] = (dout * e_ * pg * (1.0 - pg)).astype(BF16)
        _accumulate(dwple_ref, _tn(pb, (dout * pg).astype(BF16)), first)
        loss_ref[...] = jnp.zeros((1, 128), F32) + jnp.sum(acc[...], axis=-1, keepdims=True) * (0.5 / D)

    return _rowchain(body, "ple_loss", [_tok(x1), _tok(p), _tok(target), ("all", g2), ("all", w_pg), ("all", w_ple)],
                     [_tok_out(BF16), ("acc", (1, 128), F32), _tok_out(F32), _tok_out(BF16), ("acc", (PLE, D), F32)],
                     scratch=[pltpu.VMEM((1, D), F32)])


def _ple_bwd(du, n2, y, x1, dout, g2, w_pg, w_out):
    def body(du_ref, n2_ref, y_ref, x1_ref, dout_ref, g_ref, wpg_ref, wo_ref, dx_ref, dy_ref, dg_ref, dwpg_ref,
             dwo_ref):
        first = pl.program_id(0) == 0
        x1 = x1_ref[...]
        r = lax.rsqrt(jnp.mean(x1 * x1, axis=-1, keepdims=True) + EPS)
        du_ = du_ref[...]
        dn = _nt(du_, wpg_ref[...])
        u = dn * g_ref[...]
        dx = dout_ref[...] + r * u - x1 * (r * r * r) * jnp.mean(u * x1, axis=-1, keepdims=True)
        dxb = dx.astype(BF16)
        dx_ref[...] = dx
        dy_ref[...] = _nt(dxb, wo_ref[...]).astype(BF16)
        _accumulate(dg_ref, jnp.sum(dn * x1 * r, axis=0, keepdims=True), first)
        _accumulate(dwpg_ref, _tn(n2_ref[...], du_), first)
        _accumulate(dwo_ref, _tn(y_ref[...], dxb), first)

    return _rowchain(body, "ple_bwd",
                     [_tok(du), _tok(n2), _tok(y), _tok(x1), _tok(dout), ("all", g2), ("all", w_pg), ("all", w_out)],
                     [_tok_out(F32), _tok_out(BF16), ("acc", (1, D), F32), ("acc", (D, D), F32), ("acc", (D, D), F32)])


def _branches_bwd(dy, ya, yb, ain, bin_, proj, w_att, w_gla):
    def body(dy_ref, ya_ref, yb_ref, ain_ref, bin_ref, g_ref, wa_ref, wg_ref, dg_ref, dain_ref, dbin_ref,
             dwa_ref, dwg_ref):
        first = pl.program_id(0) == 0
        dy_ = dy_ref[...].astype(F32)
        sa, sb = _sigmoid(g_ref[:, :D]), _sigmoid(g_ref[:, D:])
        dg_ref[:, :D] = (dy_ * ya_ref[...].astype(F32) * sa * (1.0 - sa)).astype(BF16)
        dg_ref[:, D:] = (dy_ * yb_ref[...].astype(F32) * sb * (1.0 - sb)).astype(BF16)
        dya = (dy_ * sa).astype(BF16)
        dyb = (dy_ * sb).astype(BF16)
        dain_ref[...] = _nt(dya, wa_ref[...]).astype(BF16)
        dbin_ref[...] = _nt(dyb, wg_ref[...]).astype(BF16)
        _accumulate(dwa_ref, _tn(ain_ref[...], dya), first)
        _accumulate(dwg_ref, _tn(bin_ref[...], dyb), first)

    gates = C_GA // (2 * D)
    return _rowchain(body, "branches_bwd",
                     [_tok(dy), _tok(ya), _tok(yb), _tok(ain), _tok(bin_), _tok(proj, 2 * D, gates), ("all", w_att),
                      ("all", w_gla)],
                     [("tok", (T, NCOL), BF16, 2 * D, gates), _tok_out(BF16, ATT_W), _tok_out(BF16),
                      ("acc", (ATT_W, D), F32), ("acc", (D, D), F32)])


def _peer(k):
    x, y, c = lax.axis_index("x"), lax.axis_index("y"), lax.axis_index("c")
    return (x ^ ((k >> 2) & 1), y ^ ((k >> 1) & 1), c ^ (k & 1))


def _my_index():
    return 4 * lax.axis_index("x") + 2 * lax.axis_index("y") + lax.axis_index("c")


def _peer_index(k):
    px, py, pc = _peer(k)
    return 4 * px + 2 * py + pc


def _pairwise_plan(src_of, dst_of, landed_of, own_src, own_dst):
    def plan(ins, outs, send, recv, local):
        n = len(ins)

        def own():
            return [pltpu.make_async_copy(own_src(ins[a]), own_dst(outs[a]), local.at[a]) for a in range(n)]

        def remote(k, a, src, dst):
            return pltpu.make_async_remote_copy(src_ref=src, dst_ref=dst, send_sem=send.at[k - 1, a],
                                                recv_sem=recv.at[k - 1, a], device_id=_peer(k), device_id_type=MESH)

        def sent():
            return [remote(k, a, src_of(ins[a], k), dst_of(outs[a])) for k in range(1, NDEV) for a in range(n)]

        def start():
            for cp in own() + sent():
                cp.start()

        def finish():
            for k in range(1, NDEV):
                for a in range(n):
                    remote(k, a, own_src(ins[a]), landed_of(outs[a], k)).wait_recv()
            for cp in sent():
                cp.wait_send()
            for cp in own():
                cp.wait()

        return start, finish

    return plan


def _pairwise_sems(n):
    return [pltpu.SemaphoreType.DMA((NDEV - 1, n)), pltpu.SemaphoreType.DMA((NDEV - 1, n)),
            pltpu.SemaphoreType.DMA((n,))]


def _gather_side(arrs):
    plan = _pairwise_plan(src_of=lambda i, k: i, dst_of=lambda o: o.at[_my_index()],
                          landed_of=lambda o, k: o.at[_peer_index(k)],
                          own_src=lambda i: i, own_dst=lambda o: o.at[_my_index()])
    return dict(arrs=arrs, out_shape=[S((NDEV,) + a.shape, a.dtype) for a in arrs],
                scratch=_pairwise_sems(len(arrs)), plan=plan)


def _exchange_side(arrs):
    plan = _pairwise_plan(src_of=lambda i, k: i.at[_peer_index(k)], dst_of=lambda o: o.at[_my_index()],
                          landed_of=lambda o, k: o.at[_peer_index(k)],
                          own_src=lambda i: i.at[_my_index()], own_dst=lambda o: o.at[_my_index()])
    return dict(arrs=arrs, out_shape=[S(a.shape, a.dtype) for a in arrs], scratch=_pairwise_sems(len(arrs)), plan=plan)


def _comm_call(side, name):
    n = len(side["arrs"])

    def body(*refs):
        start, finish = side["plan"](refs[:n], refs[n:2 * n], *refs[2 * n:])
        start()
        finish()

    hbm = pl.BlockSpec(memory_space=pl.ANY)
    return pl.pallas_call(body, name=name, in_specs=[hbm] * n, out_specs=[hbm] * n, out_shape=side["out_shape"],
                          scratch_shapes=side["scratch"])(*side["arrs"])


def _all_gather_by_chip(arrs, name):
    n = len(arrs)

    def body(*refs):
        ins, outs = refs[:n], refs[n:2 * n]
        send, recv, local = refs[2 * n:]
        x, y, c = lax.axis_index("x"), lax.axis_index("y"), lax.axis_index("c")
        me, sibling = (x, y, c), (x, y, 1 - c)
        chips = [(1 - x, y), (x, 1 - y), (1 - x, 1 - y)]

        def copy(k, a, block, to, src=None):
            px, py, pc = block
            slot = outs[a].at[4 * px + 2 * py + pc]
            return pltpu.make_async_remote_copy(
                src_ref=slot if src is None else src, dst_ref=slot, send_sem=send.at[k, a], recv_sem=recv.at[k, a],
                device_id=to, device_id_type=MESH)

        north = c == 1
        via = (jnp.where(north, 1 - x, x), jnp.where(north, y, 1 - y))
        onward = (jnp.where(north, x, 1 - x), jnp.where(north, 1 - y, y), c)
        mine = [pltpu.make_async_copy(ins[a], outs[a].at[4 * x + 2 * y + c], local.at[a]) for a in range(n)]
        first = []
        for a in range(n):
            first.append(copy(0, a, me, sibling, src=ins[a]))
            first += [copy(1 + j, a, me, (*chips[j], c), src=ins[a]) for j in range(2)]
        for cp in mine + first:
            cp.start()
        passed = []
        for j in range(2):
            for a in range(n):
                copy(1 + j, a, (*chips[j], c), me).wait_recv()
                passed.append(copy(4 + j, a, (*chips[j], c), sibling))
                passed[-1].start()
        for a in range(n):
            passed.append(copy(3, a, (*via, c), onward))
            passed[-1].start()
        for a in range(n):
            copy(3, a, (*chips[2], c), me).wait_recv()
            passed.append(copy(6, a, (*chips[2], c), sibling))
            passed[-1].start()
        for a in range(n):
            copy(0, a, sibling, me).wait_recv()
        for j, chip in enumerate(chips):
            for a in range(n):
                copy(4 + j, a, (*chip, 1 - c), me).wait_recv()
        for cp in first + passed:
            cp.wait_send()
        for cp in mine:
            cp.wait()

    hbm = pl.BlockSpec(memory_space=pl.ANY)
    return pl.pallas_call(
        body, name=name, in_specs=[hbm] * n, out_specs=[hbm] * n,
        out_shape=[S((NDEV,) + a.shape, a.dtype) for a in arrs],
        scratch_shapes=[pltpu.SemaphoreType.DMA((NDEV - 1, n)), pltpu.SemaphoreType.DMA((NDEV - 1, n)),
                        pltpu.SemaphoreType.DMA((n,))],
    )(*arrs)


NCHIP = 4


def _sibling_sum(src, name, tc=512):
    _, rows, cols = src.shape
    assert cols % tc == 0

    def body(src_ref, got_ref, out_ref, a_buf, b_buf, o_buf, send, recv, local):
        x, y, c = lax.axis_index("x"), lax.axis_index("y"), lax.axis_index("c")
        copies = [pltpu.make_async_remote_copy(
            src_ref=src_ref.at[2 * q + (1 - c)], dst_ref=got_ref.at[q], send_sem=send.at[q], recv_sem=recv.at[q],
            device_id=(x, y, 1 - c), device_id_type=MESH) for q in range(NCHIP)]
        for cp in copies:
            cp.start()
        tiles = [(q, pl.ds(t * tc, tc)) for q in range(NCHIP) for t in range(cols // tc)]

        def loads(n):
            q, tile = tiles[n]
            return [pltpu.make_async_copy(src_ref.at[2 * q + c, :, tile], a_buf.at[n % 2], local.at[n % 2, 0]),
                    pltpu.make_async_copy(got_ref.at[q, :, tile], b_buf.at[n % 2], local.at[n % 2, 1])]

        def store(n):
            q, tile = tiles[n]
            return pltpu.make_async_copy(o_buf.at[n % 2], out_ref.at[q, :, tile], local.at[n % 2, 2])

        def fetch(n):
            if n == 0 or tiles[n][0] != tiles[n - 1][0]:
                copies[tiles[n][0]].wait_recv()
            for cp in loads(n):
                cp.start()

        fetch(0)
        for n in range(len(tiles)):
            if n + 1 < len(tiles):
                fetch(n + 1)
            for cp in loads(n):
                cp.wait()
            if n >= 2:
                store(n - 2).wait()
            o_buf[n % 2] = (a_buf[n % 2].astype(F32) + b_buf[n % 2].astype(F32)).astype(BF16)
            store(n).start()
        store(len(tiles) - 2).wait()
        store(len(tiles) - 1).wait()
        for cp in copies:
            cp.wait_send()

    hbm = pl.BlockSpec(memory_space=pl.ANY)
    block = S((NCHIP, rows, cols), BF16)
    return pl.pallas_call(
        body, name=name, in_specs=[hbm], out_specs=[hbm, hbm], out_shape=[block, block],
        scratch_shapes=[pltpu.VMEM((2, rows, tc), BF16)] * 3
        + [pltpu.SemaphoreType.DMA((NCHIP,)), pltpu.SemaphoreType.DMA((NCHIP,)), pltpu.SemaphoreType.DMA((2, 3))],
    )(src)[1]


def _chips_side(arrs):
    def plan(ins, outs, send, recv, local):
        n = len(ins)

        def places():
            x, y, c = lax.axis_index("x"), lax.axis_index("y"), lax.axis_index("c")
            return 2 * x + y, c, [(1 - x, y), (x, 1 - y), (1 - x, 1 - y)]

        def own():
            here, _, _ = places()
            return [pltpu.make_async_copy(ins[a].at[here], outs[a].at[here], local.at[a]) for a in range(n)]

        def remote(j, a, src_slot, dst_slot):
            _, c, chips = places()
            cx, cy = chips[j]
            return pltpu.make_async_remote_copy(
                src_ref=ins[a].at[src_slot], dst_ref=outs[a].at[dst_slot], send_sem=send.at[j, a],
                recv_sem=recv.at[j, a], device_id=(cx, cy, c), device_id_type=MESH)

        def sent():
            here, _, chips = places()
            return [remote(j, a, 2 * cx + cy, here) for j, (cx, cy) in enumerate(chips) for a in range(n)]

        def start():
            for cp in own() + sent():
                cp.start()

        def finish():
            here, _, chips = places()
            for j, (cx, cy) in enumerate(chips):
                for a in range(n):
                    remote(j, a, here, 2 * cx + cy).wait_recv()
            for cp in sent():
                cp.wait_send()
            for cp in own():
                cp.wait()

        return start, finish

    n = len(arrs)
    return dict(arrs=arrs, out_shape=[S(a.shape, a.dtype) for a in arrs],
                scratch=[pltpu.SemaphoreType.DMA((NCHIP - 1, n)), pltpu.SemaphoreType.DMA((NCHIP - 1, n)),
                         pltpu.SemaphoreType.DMA((n,))], plan=plan)


def _adamw_shards(parts, places, carry):
    n_src = len(parts)
    n_dma = 4
    rows_per_dma = carry.shape[0] // n_dma

    def body(*refs):
        carry_in, carry_out, sem = refs[n_src + 3 * len(places)], refs[-2], refs[-1]
        moves = [pltpu.make_async_copy(carry_in.at[pl.ds(k * rows_per_dma, rows_per_dma)],
                                       carry_out.at[pl.ds(k * rows_per_dma, rows_per_dma)], sem.at[k])
                 for k in range(n_dma)]
        for cp in moves:
            cp.start()
        _adam_places(refs)
        for cp in moves:
            cp.wait()

    def _adam_places(refs):
        srcs, rest = refs[:n_src], refs[n_src:n_src + 3 * len(places)] + refs[n_src + 3 * len(places) + 1:]
        for j, (src, rows, cols, _) in enumerate(places):
            w_ref, m_ref, v_ref = rest[3 * j:3 * j + 3]
            outs = rest[3 * len(places) + 4 * j:3 * len(places) + 4 * j + 4]
            p_ref = srcs[src]
            g = p_ref[0, rows, cols].astype(F32)
            for s in range(1, p_ref.shape[0]):
                g = g + p_ref[s, rows, cols].astype(F32)
            delta, m_new, v_new = _adam_math(g, w_ref[0], m_ref[0], v_ref[0])
            for ref, val in zip(outs, (g, delta, m_new, v_new)):
                ref[0] = val

    flat = [a for place in places for a in place[3]]
    vmem, hbm = pl.BlockSpec(memory_space=pltpu.VMEM), pl.BlockSpec(memory_space=pl.ANY)
    return pl.pallas_call(
        body, name="adam_shards",
        in_specs=[vmem] * (n_src + len(flat)) + [hbm], out_specs=[vmem] * (4 * len(places)) + [hbm],
        out_shape=[S(place[3][0].shape, F32) for place in places for _ in range(4)] + [S(carry.shape, carry.dtype)],
        scratch_shapes=[pltpu.SemaphoreType.DMA((n_dma,))],
    )(*parts, *flat, carry)


def _adam_math(g, w, m, v):
    c1 = 1.0 - ADAM_B1 ** ADAM_STEP
    c2 = 1.0 - ADAM_B2 ** ADAM_STEP
    m_new = ADAM_B1 * m + (1.0 - ADAM_B1) * g
    v_new = ADAM_B2 * v + (1.0 - ADAM_B2) * (g * g)
    return -ADAM_LR * ((m_new / c1) / (jnp.sqrt(v_new / c2) + ADAM_EPS) + ADAM_WD * w), m_new, v_new


def _adamw_small(parts, params, loss_parts):
    n = len(params)

    def body(*refs):
        p_refs, rest = refs[:n], refs[n + 1:]
        total = refs[n][0]
        for s in range(1, NDEV):
            total = total + refs[n][s]
        refs[-1][...] = total
        for j in range(n):
            w_ref, m_ref, v_ref = rest[3 * j:3 * j + 3]
            g_ref, d_ref, mo_ref, vo_ref = rest[3 * n + 4 * j:3 * n + 4 * j + 4]
            width = w_ref.shape[1]
            g = p_refs[j][0]
            for s in range(1, NDEV):
                g = g + p_refs[j][s]
            g = g[:, :width]
            delta, m_new, v_new = _adam_math(g, w_ref[...], m_ref[...], v_ref[...])
            g_ref[...] = g
            d_ref[...] = delta
            mo_ref[...] = m_new
            vo_ref[...] = v_new

    flat = [a for group in params for a in group]
    return pl.pallas_call(
        body, name="adam_small",
        out_shape=[S(group[0].shape, F32) for group in params for _ in range(4)] + [S((1, 128), F32)],
    )(*parts, loss_parts, *flat)


def _adamw_rows(parts, w, m, v, name, tc=256):
    rows, _, cols = w.shape
    nparts = parts.shape[0]
    nsteps = cols // tc

    def body(p_ref, w_hbm, m_hbm, v_hbm, g_hbm, d_hbm, mo_hbm, vo_hbm, inbuf, outbuf, insem, outsem):
        i = pl.program_id(0)
        slot = i & 1

        def view(ref, step):
            return ref.at[:, 0, pl.ds(pl.multiple_of(step * tc, tc), tc)]

        def fetch(step, sl):
            return [pltpu.make_async_copy(view(src, step), inbuf.at[sl, k], insem.at[sl, k])
                    for k, src in enumerate((w_hbm, m_hbm, v_hbm))]

        def write(step, sl):
            return [pltpu.make_async_copy(outbuf.at[sl, k], view(dst, step), outsem.at[sl, k])
                    for k, dst in enumerate((g_hbm, d_hbm, mo_hbm, vo_hbm))]

        @pl.when(i == 0)
        def _():
            for cp in fetch(0, 0):
                cp.start()

        @pl.when(i + 1 < nsteps)
        def _():
            for cp in fetch(i + 1, 1 - slot):
                cp.start()

        for cp in fetch(i, slot):
            cp.wait()

        @pl.when(i >= 2)
        def _():
            for cp in write(i - 2, slot):
                cp.wait()

        g = p_ref[0].astype(F32)
        for s in range(1, nparts):
            g = g + p_ref[s].astype(F32)
        g = g[:rows]
        delta, m_new, v_new = _adam_math(g, inbuf[slot, 0], inbuf[slot, 1], inbuf[slot, 2])
        for k, val in enumerate((g, delta, m_new, v_new)):
            outbuf[slot, k] = val
        for cp in write(i, slot):
            cp.start()

        @pl.when(i == nsteps - 1)
        def _():
            for cp in write(i - 1, 1 - slot) + write(i, slot):
                cp.wait()

    hbm = pl.BlockSpec(memory_space=pl.ANY)
    assert nsteps >= 2
    return pl.pallas_call(
        body, name=name, grid=(nsteps,),
        in_specs=[pl.BlockSpec((nparts, parts.shape[1], tc), lambda i: (0, 0, i)), hbm, hbm, hbm],
        out_specs=[hbm] * 4, out_shape=[S((rows, 1, cols), F32)] * 4,
        scratch_shapes=[pltpu.VMEM((2, 3, rows, tc), F32), pltpu.VMEM((2, 4, rows, tc), F32),
                        pltpu.SemaphoreType.DMA((2, 3)), pltpu.SemaphoreType.DMA((2, 4))],
        compiler_params=pltpu.CompilerParams(dimension_semantics=("arbitrary",)),
    )(parts, w, m, v)


SLAB = 1296
REMAP_RUNS = 4
_PIECES = ((O_QA, O_ZA, C_QA), (O_ZA, O_QG, C_ZA), (O_QG, O_GLR, C_QG), (O_GLR, O_ZG, C_GLR), (O_ZG, O_GA, C_ZG),
           (O_GA, O_END, C_GA))


def _slab_row_of_aligned(a):
    for o0, o1, a0 in _PIECES:
        if a0 <= a < a0 + o1 - o0:
            c = o0 + a - a0
            return SLAB * (c // W_IN_SHARD) + c % W_IN_SHARD
    return -1


def _aligned_row_of_slab(r):
    d, l = divmod(r, SLAB)
    if l >= W_IN_SHARD:
        return -1
    c = d * W_IN_SHARD + l
    for o0, o1, a0 in _PIECES:
        if o0 <= c < o1:
            return a0 + c - o0
    raise AssertionError(c)


def _remap_table(row_of, n_out, block, n_src):
    win = block + 16
    table = []
    for b in range(n_out // block):
        runs = []
        for i in range(block):
            s = row_of(b * block + i)
            if s < 0:
                continue
            if runs and runs[-1][0] + runs[-1][2] == s and runs[-1][1] + runs[-1][2] == i:
                runs[-1][2] += 1
            else:
                runs.append([s, i, 1])
        assert len(runs) <= REMAP_RUNS, (b, runs)
        row = []
        for s, i, n in runs:
            w = min(s // 16 * 16, n_src - win)
            assert 0 <= s - w and s - w + n <= win
            row += [w, s - w, i, n]
        table.append(row + [0] * (4 * REMAP_RUNS - len(row)))
    return table


def _remap_rows(src, row_of, n_out, block, name):
    n_src, cols = src.shape
    nb, win = n_out // block, block + 16
    table = _remap_table(row_of, n_out, block, n_src)
    runs = [[tuple(row[4 * k:4 * k + 4]) for k in range(REMAP_RUNS) if row[4 * k + 3] > 0] for row in table]

    def body(src_hbm, out_hbm, wbuf, obuf, insem, outsem):
        def fetches(b):
            return [pltpu.make_async_copy(src_hbm.at[pl.ds(w, win)], wbuf.at[b % 2, k], insem.at[b % 2, k])
                    for k, (w, _, _, _) in enumerate(runs[b])]

        def store(b):
            return pltpu.make_async_copy(obuf.at[b % 2], out_hbm.at[pl.ds(b * block, block)], outsem.at[b % 2])

        for cp in fetches(0):
            cp.start()
        for b in range(nb):
            if b + 1 < nb:
                for cp in fetches(b + 1):
                    cp.start()
            for cp in fetches(b):
                cp.wait()
            if b >= 2:
                store(b - 2).wait()
            if sum(count for _, _, _, count in runs[b]) < block:
                obuf[b % 2] = jnp.zeros((block, cols), src.dtype)
            for k, (_, shift, first, count) in enumerate(runs[b]):
                obuf[b % 2, first:first + count, :] = wbuf[b % 2, k, shift:shift + count, :]
            store(b).start()
        store(nb - 2).wait()
        store(nb - 1).wait()

    hbm = pl.BlockSpec(memory_space=pl.ANY)
    return pl.pallas_call(
        body, name=name, in_specs=[hbm], out_specs=hbm, out_shape=S((n_out, cols), src.dtype),
        scratch_shapes=[pltpu.VMEM((2, REMAP_RUNS, win, cols), src.dtype), pltpu.VMEM((2, block, cols), src.dtype),
                        pltpu.SemaphoreType.DMA((2, REMAP_RUNS)), pltpu.SemaphoreType.DMA((2,))],
    )(src)


def _col_blocks(w, width):
    return w.reshape(w.shape[0], NDEV, width).transpose(1, 0, 2)


def _from_col_blocks(w):
    return w.transpose(1, 0, 2).reshape(w.shape[1], NDEV * w.shape[2])


def _local_step(x2, p2, pos, tgt, norm_g, qk_norm_q, qk_norm_k, gla_gate_b, gla_norm_g, ple_norm_g, w_al,
                weights=None, proj_side=None, unpack=None, dw_side_of=None, dh_side_of=None):
    half = ROT_DIM // 2
    inv8 = jnp.power(jnp.float32(ROPE_THETA), -jnp.arange(half, dtype=F32) * 2.0 / ROT_DIM)
    inv = jnp.tile(jnp.concatenate([inv8, inv8, jnp.zeros((HD - ROT_DIM,), F32)]), 2).reshape(1, 128)
    gq = jnp.tile(qk_norm_q, (1, 2))
    gk = jnp.tile(qk_norm_k, (1, 2))

    proj, h, got = _proj_rms(x2, norm_g, w_al, proj_side)
    if proj_side is not None:
        weights = unpack(got)
    w2p, w_att_f, w_gla_f, w_out_f, w_pg_f, w_ple_f = weights
    qkv = _qk_prep(proj, pos, inv, gq, gk)
    fwd = [_att_fwd(qkv[g], qkv[3 + g], qkv[6 + g], g, f"att_fwd{g}") for g in range(3)]
    att, lse, ain = _att_merge([f[0] for f in fwd], [f[1] for f in fwd], proj)
    o_gla, bin_, states = _gla_fwd(proj, w2p, gla_gate_b, gla_norm_g)
    ya, yb, y, x1 = _branches_fwd(ain, bin_, proj, x2, w_att_f, w_gla_f, w_out_f)
    n2, loss_v, dout, du, dw_ple = _ple_loss(x1, p2, tgt, ple_norm_g, w_pg_f, w_ple_f)

    dx1, dy, dg_ple, dw_pg, dw_out = _ple_bwd(du, n2, y, x1, dout, ple_norm_g, w_pg_f, w_out_f)
    dproj, dain, dbin, dw_att, dw_gla = _branches_bwd(dy, ya, yb, ain, bin_, proj, w_att_f, w_gla_f)
    dproj, da0, da1, da2, at1, at2, ls1, ls2 = _att_gate_bwd(dain, att, lse, proj, dproj)
    datts, atts, lses = (da0, da1, da2), (att[None], at1, at2), (lse[None], ls1, ls2)
    dproj, dw2, dbg, dgn = _gla_bwd(proj, w2p, gla_gate_b, gla_norm_g, o_gla, states, dbin, dproj)
    bwd = [_att_bwd(qkv[g], qkv[3 + g], qkv[6 + g], datts[g], atts[g], lses[g], g, f"att_bwd{g}") for g in range(3)]
    dproj, dgq, dgk = _qk_bwd(proj, pos, inv, gq, gk, [b[0] for b in bwd], [b[1] for b in bwd],
                              [b[2] for b in bwd], dproj)
    out = dict(loss=loss_v, dw2=dw2, dw_att=dw_att, dw_gla=dw_gla, dw_out=dw_out, dw_pg=dw_pg, dw_ple=dw_ple,
               dgq=dgq, dgk=dgk, dbg=dbg, dgn=dgn, dg_ple=dg_ple)
    if dw_side_of is None:
        dw_al = _mm(dproj, h, mode="tn", name="dw_in", tm=1536, tn=D, tk=T, out_dtype=BF16)
    else:
        dw_al, out["dw_side"] = _mm(dproj, h, mode="tn", name="dw_in", tm=1536, tn=D, tk=T, out_dtype=BF16,
                                    side=dw_side_of(out))
    grad_x, dg_norm, out["dh_side"] = _dh_rms(dproj, w_al, x2, norm_g, dx1,
                                              None if dh_side_of is None else dh_side_of(dw_al))
    out.update(grad_x=grad_x, dw_al=dw_al, dg_norm=dg_norm)
    return out


def kernel(x, p, positions, norm_g, w_in, qk_norm_q, qk_norm_k, gla_gate_w2, gla_gate_b, gla_norm_g, w_att_proj, w_gla_proj, w_out, ple_norm_g, w_ple_gate, w_ple, loss_target, m_norm_g, m_w_in, m_qk_norm_q, m_qk_norm_k, m_gla_gate_w2, m_gla_gate_b, m_gla_norm_g, m_w_att_proj, m_w_gla_proj, m_w_out, m_ple_norm_g, m_w_ple_gate, m_w_ple, v_norm_g, v_w_in, v_qk_norm_q, v_qk_norm_k, v_gla_gate_w2, v_gla_gate_b, v_gla_norm_g, v_w_att_proj, v_w_gla_proj, v_w_out, v_ple_norm_g, v_w_ple_gate, v_w_ple):
    x2, p2, tgt = x[0], p[0, 0], loss_target[0]
    pos = positions.astype(F32).reshape(T, 1)

    rows3 = jnp.stack([w_gla_proj[0], w_out[0], w_ple_gate[0]]).astype(BF16)
    cols3 = jnp.concatenate([w_att_proj[0], w_ple[0], jnp.pad(gla_gate_w2[0], ((0, 0), (0, 64)))], axis=0).astype(BF16)
    mine = jnp.pad(w_in[0].T.astype(BF16), ((0, SLAB - W_IN_SHARD), (0, 0)))
    (g_in,) = _all_gather_by_chip([mine], "gather_w_in")
    w_al = _remap_rows(g_in.reshape(NDEV * SLAB, D), _slab_row_of_aligned, NCOL, 1536, "align_w_in")

    def unpack(got):
        g_rows, g_cols = got
        w2_f = _from_col_blocks(g_cols[:, 768:784, :64])
        return (jnp.pad(w2_f, ((0, GLR_W - GLR_N), (0, 0))), _from_col_blocks(g_cols[:, :512]),
                g_rows[:, 0].reshape(D, D), g_rows[:, 1].reshape(D, D), g_rows[:, 2].reshape(D, D),
                _from_col_blocks(g_cols[:, 512:768]))

    def dw_side_of(g):
        s_rows = jnp.concatenate([g[k].reshape(NDEV, 128, D) for k in ("dw_gla", "dw_out", "dw_pg")], axis=1)
        s_cols = jnp.concatenate([_col_blocks(g["dw_att"], 128), _col_blocks(g["dw_ple"], 128),
                                  jnp.pad(_col_blocks(g["dw2"][:GLR_N], 64), ((0, 0), (0, 0), (0, 64)))], axis=1)
        return _exchange_side([s_rows.astype(BF16), s_cols.astype(BF16)])

    def dh_side_of(dw_al):
        s_in = _remap_rows(dw_al, _aligned_row_of_slab, NDEV * SLAB, SLAB, "shard_dw_in").reshape(NDEV, SLAB, D)
        return _chips_side([_sibling_sum(s_in, "sibling_sum")])

    loc = _local_step(x2, p2, pos, tgt, norm_g, qk_norm_q, qk_norm_k, gla_gate_b, gla_norm_g, ple_norm_g, w_al,
                      proj_side=_gather_side([rows3, cols3]), unpack=unpack, dw_side_of=dw_side_of,
                      dh_side_of=dh_side_of)
    loss_v, grad_x = loc["loss"], loc["grad_x"]
    dg_norm, dgq, dgk, dbg, dgn, dg_ple = (loc[k] for k in ("dg_norm", "dgq", "dgk", "dbg", "dgn", "dg_ple"))
    r_rows, r_cols = loc["dw_side"]
    (r_in,) = loc["dh_side"]

    r_small = _comm_call(_gather_side([dg_norm, dgq, dgk, dbg, dgn, dg_ple, loss_v]), "gather_small")

    outs = {}

    rows_of = lambda a: jnp.transpose(a, (2, 0, 1))
    outs["w_in"] = [jnp.transpose(o, (1, 2, 0))[0] for o in
                    _adamw_rows(r_in, rows_of(w_in), rows_of(m_w_in), rows_of(v_w_in), "adam_w_in")]
    places = (("w_gla_proj", 0, slice(0, 128), slice(None), (w_gla_proj, m_w_gla_proj, v_w_gla_proj)),
              ("w_out", 0, slice(128, 256), slice(None), (w_out, m_w_out, v_w_out)),
              ("w_ple_gate", 0, slice(256, 384), slice(None), (w_ple_gate, m_w_ple_gate, v_w_ple_gate)),
              ("w_att_proj", 1, slice(0, 512), slice(None), (w_att_proj, m_w_att_proj, v_w_att_proj)),
              ("w_ple", 1, slice(512, 768), slice(None), (w_ple, m_w_ple, v_w_ple)),
              ("gla_gate_w2", 1, slice(768, 784), slice(0, 64), (gla_gate_w2, m_gla_gate_w2, v_gla_gate_w2)))
    res = _adamw_shards([r_rows, r_cols], [place[1:] for place in places], grad_x)
    grad_x = res[-1]
    for j, place in enumerate(places):
        outs[place[0]] = [o[0] for o in res[4 * j:4 * j + 4]]
    small = ((norm_g, m_norm_g, v_norm_g), (qk_norm_q, m_qk_norm_q, v_qk_norm_q), (qk_norm_k, m_qk_norm_k, v_qk_norm_k),
             (gla_gate_b, m_gla_gate_b, v_gla_gate_b), (gla_norm_g, m_gla_norm_g, v_gla_norm_g),
             (ple_norm_g, m_ple_norm_g, v_ple_norm_g))
    sm = _adamw_small(r_small[:6], small, r_small[6])
    for j, nm in enumerate(("norm_g", "qk_norm_q", "qk_norm_k", "gla_gate_b", "gla_norm_g", "ple_norm_g")):
        outs[nm] = [o[0] for o in sm[4 * j:4 * j + 4]]

    loss = sm[-1][0, 0]
    order = ["norm_g", "w_in", "qk_norm_q", "qk_norm_k", "gla_gate_w2", "gla_gate_b", "gla_norm_g", "w_att_proj",
             "w_gla_proj", "w_out", "ple_norm_g", "w_ple_gate", "w_ple"]
    result = [loss, grad_x[None]]
    for i in range(4):
        result += [outs[nm][i][None] for nm in order]
    return tuple(result)
```

```python
import functools

import jax
import jax.numpy as jnp
from jax import lax
from jax.experimental import pallas as pl
from jax.experimental.pallas import tpu as pltpu

F32 = jnp.float32
BF16 = jnp.bfloat16
S = jax.ShapeDtypeStruct

T = 4096
D = 1024
NDEV = 8
HD = 64
ATT_W = 512
ATT_QKV = 1536
DILATIONS = (1, 4, 16)
BLK = 128
GH, GDK, GDV = 4, 128, 256
GLA_C = 128
PLE = 256
EPS = 1e-6
ROT_DIM = 16
ROPE_THETA = 500000.0
GLA_TAU = 16.0
W_IN_SHARD = 1282

C_QG, C_KG, C_VG, C_ZG, C_GLR, C_ZA, C_GA, C_GB, C_QA, C_KA, C_VA = (
    0, 512, 1024, 2048, 3072, 3584, 4096, 5120, 6144, 7680, 9216)
GLA_GROUP_W = 3584
GLR_W = 512
NCOL = 10752
GLR_N = 16
O_QA, O_ZA, O_QG, O_GLR, O_ZG, O_GA, O_END = 0, 4608, 5120, 7168, 7184, 8208, 10256

ADAM_LR, ADAM_B1, ADAM_B2, ADAM_EPS, ADAM_WD, ADAM_STEP = 0.001, 0.9, 0.999, 1e-08, 0.01, 10

MESH = pl.DeviceIdType.MESH


def _sigmoid(z):
    return 1.0 / (1.0 + jnp.exp(-z))


def _dot(a, b, dims):
    return lax.dot_general(a, b, (dims, ((), ())), preferred_element_type=F32)


def _nn(a, b):
    return _dot(a, b, ((1,), (0,)))


def _nt(a, b):
    return _dot(a, b, ((1,), (1,)))


def _tn(a, b):
    return _dot(a, b, ((0,), (0,)))


def _mm(a, b, *, mode, name, tm, tn, tk, out_dtype=F32, res=None, side=None):
    if mode == "nn":
        (m, k), n = a.shape, b.shape[1]
        a_spec = pl.BlockSpec((tm, tk), lambda i, j, l: (i, l))
        b_spec = pl.BlockSpec((tk, tn), lambda i, j, l: (l, j))
        dot = _nn
    elif mode == "nt":
        (m, k), n = a.shape, b.shape[0]
        a_spec = pl.BlockSpec((tm, tk), lambda i, j, l: (i, l))
        b_spec = pl.BlockSpec((tn, tk), lambda i, j, l: (j, l))
        dot = _nt
    else:
        (k, m), n = a.shape, b.shape[1]
        a_spec = pl.BlockSpec((tk, tm), lambda i, j, l: (l, i))
        b_spec = pl.BlockSpec((tk, tn), lambda i, j, l: (l, j))
        dot = _tn
    assert m % tm == 0 and n % tn == 0 and k % tk == 0, (name, m, n, k)
    grid = (m // tm, n // tn, k // tk)
    nk = grid[2]
    o_spec = pl.BlockSpec((tm, tn), lambda i, j, l: (i, j))
    in_specs = [a_spec, b_spec]
    args = [a, b]
    if res is not None:
        in_specs.append(o_spec)
        args.append(res)
    n_in = len(args)
    n_side = 0 if side is None else len(side["arrs"])
    hbm = pl.BlockSpec(memory_space=pl.ANY)

    def body(*refs):
        a_ref, b_ref = refs[0], refs[1]
        r_ref = refs[2] if res is not None else None
        o_ref = refs[n_in + n_side]
        scratch = refs[n_in + 2 * n_side + 1:]
        if side is not None:
            start, finish_side = side["plan"](refs[n_in:n_in + n_side], refs[n_in + n_side + 1:n_in + 2 * n_side + 1],
                                              *scratch[1 if nk > 1 else 0:])
            ids = [pl.program_id(d) for d in range(3)]

            @pl.when((ids[0] == 0) & (ids[1] == 0) & (ids[2] == 0))
            def _():
                start()

        part = dot(a_ref[...].astype(BF16), b_ref[...].astype(BF16))

        def finish(val):
            if r_ref is not None:
                val = val + r_ref[...]
            o_ref[...] = val.astype(out_dtype)

        if nk == 1:
            finish(part)
        else:
            acc = scratch[0]
            l = pl.program_id(2)

            @pl.when(l == 0)
            def _():
                acc[...] = part

            @pl.when(l > 0)
            def _():
                acc[...] += part

            @pl.when(l == nk - 1)
            def _():
                finish(acc[...])

        if side is not None:
            @pl.when((ids[0] == grid[0] - 1) & (ids[1] == grid[1] - 1) & (ids[2] == grid[2] - 1))
            def _():
                finish_side()

    sems = [] if side is None else side["scratch"]
    outs = pl.pallas_call(
        body, name=name, grid=grid,
        in_specs=in_specs + [hbm] * n_side, out_specs=[o_spec] + [hbm] * n_side,
        out_shape=[S((m, n), out_dtype)] + ([] if side is None else side["out_shape"]),
        scratch_shapes=([pltpu.VMEM((tm, tn), F32)] if nk > 1 else []) + sems,
        compiler_params=pltpu.CompilerParams(
            dimension_semantics=("arbitrary",) * 3 if side is not None else ("parallel", "parallel", "arbitrary")),
    )(*args, *([] if side is None else side["arrs"]))
    return outs[0] if side is None else (outs[0], outs[1:])


def _side_parts(side, refs, n_in, n_out):
    n_side = 0 if side is None else len(side["arrs"])
    scratch = refs[n_in + n_out + 2 * n_side:]
    if side is None:
        return (lambda: None), (lambda: None), scratch
    start, finish = side["plan"](refs[n_in:n_in + n_side], refs[n_in + n_side + n_out:n_in + n_out + 2 * n_side],
                                 *scratch[len(scratch) - len(side["scratch"]):])
    return start, finish, scratch


def _proj_rms(x, g, wt, side=None):
    tm, tn = 1024, 1536
    grid = (T // tm, NCOL // tn)
    n_side = 0 if side is None else len(side["arrs"])
    hbm = pl.BlockSpec(memory_space=pl.ANY)

    def body(*refs):
        x_ref, g_ref, w_ref = refs[:3]
        o_ref, h_ref = refs[3 + n_side], refs[4 + n_side]
        start, finish, _ = _side_parts(side, refs, 3, 2)
        i, j = pl.program_id(0), pl.program_id(1)

        @pl.when((i == 0) & (j == 0))
        def _():
            start()

        @pl.when(j == 0)
        def _():
            xf = x_ref[...]
            r = lax.rsqrt(jnp.mean(xf * xf, axis=-1, keepdims=True) + EPS)
            h_ref[...] = (xf * r * g_ref[...]).astype(BF16)

        o_ref[...] = _nt(h_ref[...], w_ref[...])

        @pl.when((i == grid[0] - 1) & (j == grid[1] - 1))
        def _():
            finish()

    outs = pl.pallas_call(
        body, name="proj", grid=grid,
        in_specs=[pl.BlockSpec((tm, D), lambda i, j: (i, 0)), pl.BlockSpec((1, D), lambda i, j: (0, 0)),
                  pl.BlockSpec((tn, D), lambda i, j: (j, 0))] + [hbm] * n_side,
        out_specs=[pl.BlockSpec((tm, tn), lambda i, j: (i, j)), pl.BlockSpec((tm, D), lambda i, j: (i, 0))] + [hbm] * n_side,
        out_shape=[S((T, NCOL), F32), S((T, D), BF16)] + ([] if side is None else side["out_shape"]),
        scratch_shapes=[] if side is None else side["scratch"],
        compiler_params=pltpu.CompilerParams(dimension_semantics=("arbitrary", "arbitrary")),
    )(x, g, wt, *([] if side is None else side["arrs"]))
    return outs[0], outs[1], outs[2:]


def _dh_rms(dproj, wt, x, g, skip, side=None):
    tm, tk = 1024, 2688
    grid = (T // tm, NCOL // tk)
    n_side = 0 if side is None else len(side["arrs"])
    hbm = pl.BlockSpec(memory_space=pl.ANY)

    def body(*refs):
        a_ref, w_ref, x_ref, g_ref, s_ref = refs[:5]
        dx_ref, dg_ref = refs[5 + n_side], refs[6 + n_side]
        start, finish, scratch = _side_parts(side, refs, 5, 2)
        acc = scratch[0]
        i, l = pl.program_id(0), pl.program_id(1)

        @pl.when((i == 0) & (l == 0))
        def _():
            start()

        part = _nn(a_ref[...], w_ref[...])

        @pl.when(l == 0)
        def _():
            acc[...] = part

        @pl.when(l > 0)
        def _():
            acc[...] += part

        @pl.when(l == grid[1] - 1)
        def _():
            xf = x_ref[...]
            r = lax.rsqrt(jnp.mean(xf * xf, axis=-1, keepdims=True) + EPS)
            dn = acc[...]
            u = dn * g_ref[...]
            dx_ref[...] = s_ref[...] + r * u - xf * (r * r * r) * jnp.mean(u * xf, axis=-1, keepdims=True)
            dg = jnp.sum(dn * xf * r, axis=0, keepdims=True)

            @pl.when(i == 0)
            def _():
                dg_ref[...] = dg

            @pl.when(i > 0)
            def _():
                dg_ref[...] += dg

        @pl.when((i == grid[0] - 1) & (l == grid[1] - 1))
        def _():
            finish()

    tok = pl.BlockSpec((tm, D), lambda i, l: (i, 0))
    outs = pl.pallas_call(
        body, name="dh", grid=grid,
        in_specs=[pl.BlockSpec((tm, tk), lambda i, l: (i, l)), pl.BlockSpec((tk, D), lambda i, l: (l, 0)), tok,
                  pl.BlockSpec((1, D), lambda i, l: (0, 0)), tok] + [hbm] * n_side,
        out_specs=[tok, pl.BlockSpec((1, D), lambda i, l: (0, 0))] + [hbm] * n_side,
        out_shape=[S((T, D), F32), S((1, D), F32)] + ([] if side is None else side["out_shape"]),
        scratch_shapes=[pltpu.VMEM((tm, D), F32)] + ([] if side is None else side["scratch"]),
        compiler_params=pltpu.CompilerParams(dimension_semantics=("arbitrary", "arbitrary")),
    )(dproj, wt, x, g, skip, *([] if side is None else side["arrs"]))
    return outs[0], outs[1], outs[2:]


def _rot_tables(pos_ref, inv_ref):
    lane = lax.broadcasted_iota(jnp.int32, (1, 128), 1) % HD
    ang = pos_ref[...] * inv_ref[...]
    cos, sin = jnp.cos(ang), jnp.sin(ang)
    c = jnp.where(lane < ROT_DIM, cos, 1.0)
    sp = jnp.where((lane >= ROT_DIM // 2) & (lane < ROT_DIM), sin, 0.0)
    sm = jnp.where(lane < ROT_DIM // 2, -sin, 0.0)
    return c, sp, sm


def _head_sums(v):
    same = (lax.broadcasted_iota(jnp.int32, (128, 128), 0) < HD) == (lax.broadcasted_iota(jnp.int32, (128, 128), 1) < HD)
    ones = jnp.where(same, 1.0, 0.0).astype(BF16)
    hi = v.astype(BF16)
    lo = (v - hi.astype(F32)).astype(BF16)
    return _nn(hi, ones) + _nn(lo, ones)


def _pair_norm(t):
    return lax.rsqrt(_head_sums(t * t) * (1.0 / HD) + EPS)


def _pair_mean(t):
    return _head_sums(t) * (1.0 / HD)


TT = 256
NCH = ATT_QKV // 128


def _res_shape(grp, dtype):
    return S((DILATIONS[grp], T // DILATIONS[grp], ATT_W), dtype)


def _res_spec(grp):
    dil = DILATIONS[grp]
    return pl.BlockSpec((dil, TT // dil, ATT_W), lambda i: (0, i, 0))


def _to_residues(sc, j, dst_ref, dil, cols):
    n = TT // dil
    for r in range(dil):
        rows = sc[j] if dil == 1 else sc.at[j][pl.ds(r, n, stride=dil), :]
        dst_ref[r, :, cols] = rows.astype(dst_ref.dtype)


def _from_residues(src_ref, cols, sc, j, dil):
    n = TT // dil
    for r in range(dil):
        if dil == 1:
            sc[j] = src_ref[r, :, cols]
        else:
            sc.at[j][pl.ds(r, n, stride=dil), :] = src_ref[r, :, cols]


def _tok_spec(width, cblk=0):
    return pl.BlockSpec((TT, width), functools.partial(lambda i, c: (i, c), c=cblk))


def _const_spec(arr_or_shape):
    shape = arr_or_shape if isinstance(arr_or_shape, tuple) else arr_or_shape.shape
    return pl.BlockSpec(shape, functools.partial(lambda i, nd: (0,) * nd, nd=len(shape)))


def _qk_prep(proj, pos, inv, gq, gk):
    def body(q_ref, k_ref, v_ref, pos_ref, inv_ref, gq_ref, gk_ref, *rest):
        outs, sc = rest[:9], rest[9]
        c, sp, sm = _rot_tables(pos_ref, inv_ref)
        for which, (src, g_ref) in enumerate(((q_ref, gq_ref), (k_ref, gk_ref), (v_ref, None))):
            if g_ref is not None:
                g = jnp.broadcast_to(g_ref[...] * ((HD ** -0.5) if which == 0 else 1.0), c.shape)
                cg, spg, smg = c * g, sp * pltpu.roll(g, 8, 1), sm * pltpu.roll(g, 120, 1)
            for j in range(NCH):
                t = src[:, j * 128:(j + 1) * 128]
                if g_ref is not None:
                    t = _pair_norm(t) * (t * cg + pltpu.roll(t, 8, 1) * spg + pltpu.roll(t, 120, 1) * smg)
                sc[j] = t
            for j in range(NCH):
                grp, sub = divmod(j * 128, ATT_W)
                _to_residues(sc, j, outs[which * 3 + grp], DILATIONS[grp], slice(sub, sub + 128))

    return pl.pallas_call(
        body, name="qk_prep", grid=(T // TT,),
        in_specs=[_tok_spec(ATT_QKV, C_QA // ATT_QKV), _tok_spec(ATT_QKV, C_KA // ATT_QKV),
                  _tok_spec(ATT_QKV, C_VA // ATT_QKV), _tok_spec(1), _const_spec(inv), _const_spec(gq), _const_spec(gk)],
        out_specs=[_res_spec(g) for _ in range(3) for g in range(3)],
        out_shape=[_res_shape(g, BF16) for _ in range(3) for g in range(3)],
        scratch_shapes=[pltpu.VMEM((NCH, TT, 128), F32)],
        compiler_params=pltpu.CompilerParams(dimension_semantics=("arbitrary",)),
    )(proj, proj, proj, pos, inv, gq, gk)


def _qk_bwd(proj, pos, inv, gq, gk, dqs, dks, dvs, dproj):
    const = lambda a: pl.BlockSpec(a.shape, functools.partial(lambda i, p, nd: (0,) * nd, nd=a.ndim))
    res = lambda g: pl.BlockSpec((DILATIONS[g], TT // DILATIONS[g], ATT_W), lambda i, p: (0, i, 0))
    base = C_QA // ATT_QKV

    def body(t_ref, pos_ref, inv_ref, gq_ref, gk_ref, dq0, dq1, dq2, dk0, dk1, dk2, dv0, dv1, dv2, buf_ref,
             out_ref, dgq_ref, dgk_ref, sc):
        del buf_ref
        part = pl.program_id(1)
        first = pl.program_id(0) == 0

        def gather(drefs):
            for j in range(NCH):
                grp, sub = divmod(j * 128, ATT_W)
                _from_residues(drefs[grp], slice(sub, sub + 128), sc, j, DILATIONS[grp])

        def normed(g_ref, drefs, dg_ref):
            c, sp, sm = _rot_tables(pos_ref, inv_ref)
            gather(drefs)
            dg = jnp.zeros((1, 128), F32)
            for j in range(NCH):
                cols = slice(j * 128, (j + 1) * 128)
                d_rot = sc[j]
                dn = d_rot * c + pltpu.roll(d_rot * sp, 120, 1) + pltpu.roll(d_rot * sm, 8, 1)
                t = t_ref[:, cols]
                r = _pair_norm(t)
                gain = g_ref[...]
                dn_t = dn * t
                out_ref[:, cols] = (r * (dn * gain - t * ((r * r) * _pair_mean(dn_t * gain)))).astype(BF16)
                dg = dg + jnp.sum(dn_t * r, axis=0, keepdims=True)
            dg = dg + pltpu.roll(dg, HD, 1)

            @pl.when(first)
            def _():
                dg_ref[...] = dg

            @pl.when(jnp.logical_not(first))
            def _():
                dg_ref[...] += dg

        @pl.when(part == 0)
        def _():
            gather((dv0, dv1, dv2))
            for j in range(NCH):
                out_ref[:, j * 128:(j + 1) * 128] = sc[j].astype(BF16)

        @pl.when(part == 1)
        def _():
            normed(gq_ref, (dq0, dq1, dq2), dgq_ref)

        @pl.when(part == 2)
        def _():
            normed(gk_ref, (dk0, dk1, dk2), dgk_ref)

    keep = pl.BlockSpec((1, 128), lambda i, p: (0, 0))
    return pl.pallas_call(
        body, name="qk_bwd", grid=(T // TT, 3),
        in_specs=[pl.BlockSpec((TT, ATT_QKV), lambda i, p: (i, base + jnp.maximum(p - 1, 0))),
                  pl.BlockSpec((TT, 1), lambda i, p: (i, 0)), const(inv), const(gq), const(gk)]
        + [res(g) for _ in range(3) for g in range(3)] + [pl.BlockSpec(memory_space=pl.ANY)],
        out_specs=[pl.BlockSpec((TT, ATT_QKV), lambda i, p: (i, base + jnp.where(p == 0, 2, p - 1))), keep, keep],
        out_shape=[S(dproj.shape, dproj.dtype), S((1, 128), F32), S((1, 128), F32)],
        input_output_aliases={14: 0},
        scratch_shapes=[pltpu.VMEM((NCH, TT, 128), F32)],
        compiler_params=pltpu.CompilerParams(dimension_semantics=("arbitrary", "arbitrary")),
    )(proj, pos, inv, gq, gk, *dqs, *dks, *dvs, dproj)


def _split_heads(t):
    low = lax.broadcasted_iota(jnp.int32, (1, 128), 1) < HD
    zero = jnp.zeros_like(t)
    return jnp.concatenate([jnp.where(low, t, zero), jnp.where(low, zero, t)], axis=0)


def _join_heads(t2):
    low = lax.broadcasted_iota(jnp.int32, (1, 128), 1) < HD
    n = t2.shape[0] // 2
    return jnp.where(low, t2[:n], t2[n:])


def _band_mask4(has_before, has_own):
    row = lax.broadcasted_iota(jnp.int32, (BLK, 4 * BLK), 0)
    lane = lax.broadcasted_iota(jnp.int32, (BLK, 4 * BLK), 1)
    key = lane & (BLK - 1)
    own = lane >= 2 * BLK
    return (own & (key <= row) & has_own) | (jnp.logical_not(own) & (key >= row) & has_before)


def _band_mask_before(has_before):
    row = lax.broadcasted_iota(jnp.int32, (BLK, 2 * BLK), 0)
    key = lax.broadcasted_iota(jnp.int32, (BLK, 2 * BLK), 1) & (BLK - 1)
    return (key >= row) & has_before


def _per_head(width, col_a, col_b):
    lane = lax.broadcasted_iota(jnp.int32, (1, width), 1)
    return jnp.where((lane & BLK) == 0, col_a, col_b)


NQ = ATT_W // 128


def _att_fwd(q, k, v, grp, name):
    dil = DILATIONS[grp]
    nb = T // dil // BLK

    def body(q_ref, kp_ref, kc_ref, vp_ref, vc_ref, o_ref, lse_ref, s_sc, p_sc):
        mask = _band_mask4(pl.program_id(1) > 0, True)
        low = lax.broadcasted_iota(jnp.int32, (1, 128), 1) < HD
        halves = lambda ref, j, h: (ref[j, :, h * BLK:(h + 1) * BLK], ref[j, :, (h + 2) * BLK:(h + 3) * BLK])
        for j in range(NQ):
            cols = slice(j * 128, (j + 1) * 128)
            k4 = jnp.concatenate([_split_heads(kp_ref[:, cols]), _split_heads(kc_ref[:, cols])], axis=0)
            s_sc[j] = jnp.where(mask, _nt(q_ref[:, cols], k4), -jnp.inf)
        mxs = [[jnp.maximum(*(jnp.max(t, axis=-1, keepdims=True) for t in halves(s_sc, j, h))) for h in range(2)]
               for j in range(NQ)]
        dens = []
        for j in range(NQ):
            p = jnp.exp(s_sc[j] - _per_head(4 * BLK, *mxs[j]))
            p_sc[j] = p.astype(BF16)
            dens.append([jnp.sum(p[:, h * BLK:(h + 1) * BLK], axis=-1, keepdims=True)
                         + jnp.sum(p[:, (h + 2) * BLK:(h + 3) * BLK], axis=-1, keepdims=True) for h in range(2)])
        for j in range(NQ):
            cols = slice(j * 128, (j + 1) * 128)
            v4 = jnp.concatenate([_split_heads(vp_ref[:, cols]), _split_heads(vc_ref[:, cols])], axis=0)
            o_ref[:, cols] = _nn(p_sc[j], v4) / jnp.where(low, dens[j][0], dens[j][1])
            lse_ref[:, cols] = jnp.where(low, mxs[j][0] + jnp.log(dens[j][0]), mxs[j][1] + jnp.log(dens[j][1]))

    cur = pl.BlockSpec((None, BLK, ATT_W), lambda r, i: (r, i, 0))
    prev = pl.BlockSpec((None, BLK, ATT_W), lambda r, i: (r, jnp.maximum(i - 1, 0), 0))
    return pl.pallas_call(
        body, name=name, grid=(dil, nb),
        in_specs=[cur, prev, cur, prev, cur],
        out_specs=[cur, cur], out_shape=[_res_shape(grp, F32)] * 2,
        scratch_shapes=[pltpu.VMEM((NQ, BLK, 4 * BLK), F32), pltpu.VMEM((NQ, BLK, 4 * BLK), BF16)],
        compiler_params=pltpu.CompilerParams(dimension_semantics=("parallel", "arbitrary")),
    )(q, k, k, v, v)


def _att_bwd(q, k, v, datt, att, lse, grp, name):
    dil = DILATIONS[grp]
    nb = T // dil // BLK
    scale = HD ** -0.5

    def body(q0_ref, q1_ref, kp_ref, kc_ref, vp_ref, vc_ref, do0_ref, do1_ref, o0_ref, o1_ref, l0_ref, l1_ref,
             dq_ref, dk_ref, dv_ref, k4_sc, v4_sc, s0_sc, s1_sc, dp0_sc, dp1_sc, p_sc, ds_sc):
        i = pl.program_id(1)
        mask_mine = _band_mask4(i > 0, True)
        mask_next = _band_mask_before(i < nb - 1)
        low = lax.broadcasted_iota(jnp.int32, (1, 128), 1) < HD
        for j in range(NQ):
            cols = slice(j * 128, (j + 1) * 128)
            k4_sc[j, :2 * BLK] = _split_heads(kp_ref[:, cols])
            k4_sc[j, 2 * BLK:] = _split_heads(kc_ref[:, cols])
            v4_sc[j, :2 * BLK] = _split_heads(vp_ref[:, cols])
            v4_sc[j, 2 * BLK:] = _split_heads(vc_ref[:, cols])
        for j in range(NQ):
            cols = slice(j * 128, (j + 1) * 128)
            s0_sc[j] = _nt(q0_ref[:, cols], k4_sc[j])
            s1_sc[j] = _nt(q1_ref[:, cols], k4_sc[j, 2 * BLK:])
            dp0_sc[j] = _nt(do0_ref[:, cols].astype(BF16), v4_sc[j])
            dp1_sc[j] = _nt(do1_ref[:, cols].astype(BF16), v4_sc[j, 2 * BLK:])
        stats = []
        for j in range(NQ):
            cols = slice(j * 128, (j + 1) * 128)
            for do_ref, o_ref, l_ref in ((do0_ref, o0_ref, l0_ref), (do1_ref, o1_ref, l1_ref)):
                prod = do_ref[:, cols].astype(F32) * o_ref[:, cols].astype(F32)
                d_all = jnp.sum(prod, axis=-1, keepdims=True)
                d_low = jnp.sum(jnp.where(low, prod, 0.0), axis=-1, keepdims=True)
                lse_t = l_ref[:, cols]
                stats.append((d_low, d_all - d_low, lse_t[:, 0:1], lse_t[:, HD:HD + 1]))
        for j in range(NQ):
            (da, db, la, lb), (da1, db1, la1, lb1) = stats[2 * j], stats[2 * j + 1]
            p0 = jnp.where(mask_mine, jnp.exp(s0_sc[j] - _per_head(4 * BLK, la, lb)), 0.0)
            ds0 = p0 * (dp0_sc[j] - _per_head(4 * BLK, da, db))
            p1 = jnp.where(mask_next, jnp.exp(s1_sc[j] - _per_head(2 * BLK, la1, lb1)), 0.0)
            ds1 = p1 * (dp1_sc[j] - _per_head(2 * BLK, da1, db1))
            p_sc[j, :BLK] = p0.astype(BF16)
            ds_sc[j, :BLK] = ds0.astype(BF16)
            p_sc[j, BLK:, 2 * BLK:] = p1.astype(BF16)
            ds_sc[j, BLK:, 2 * BLK:] = ds1.astype(BF16)
        for j in range(NQ):
            cols = slice(j * 128, (j + 1) * 128)
            dq_ref[:, cols] = _nn(ds_sc[j, :BLK], k4_sc[j]) * scale
            qq = jnp.concatenate([q0_ref[:, cols], q1_ref[:, cols]], axis=0)
            dd = jnp.concatenate([do0_ref[:, cols], do1_ref[:, cols]], axis=0).astype(BF16)
            dk_ref[:, cols] = _join_heads(_tn(ds_sc[j, :, 2 * BLK:], qq))
            dv_ref[:, cols] = _join_heads(_tn(p_sc[j, :, 2 * BLK:], dd))

    def spec(shift):
        return pl.BlockSpec((None, BLK, ATT_W), lambda r, i: (r, jnp.clip(i + shift, 0, nb - 1), 0))

    here, after, before = spec(0), spec(1), spec(-1)
    vm = pltpu.VMEM
    return pl.pallas_call(
        body, name=name, grid=(dil, nb),
        in_specs=[here, after, before, here, before, here, here, after, here, after, here, after],
        out_specs=[here] * 3, out_shape=[_res_shape(grp, F32)] * 3,
        scratch_shapes=[vm((NQ, 4 * BLK, 128), BF16), vm((NQ, 4 * BLK, 128), BF16), vm((NQ, BLK, 4 * BLK), F32),
                        vm((NQ, BLK, 2 * BLK), F32), vm((NQ, BLK, 4 * BLK), F32), vm((NQ, BLK, 2 * BLK), F32),
                        vm((NQ, 2 * BLK, 4 * BLK), BF16), vm((NQ, 2 * BLK, 4 * BLK), BF16)],
        compiler_params=pltpu.CompilerParams(dimension_semantics=("parallel", "arbitrary")),
    )(q, q, k, k, v, v, datt, datt, att, att, lse, lse)


def _att_merge(os_, lses, proj):
    nq = ATT_W // 128

    def body(o0, o1, o2, l0, l1, l2, za_ref, att_ref, lse_ref, ain_ref, sc):
        for a, ref in enumerate((o0, o1, o2, l0, l1, l2)):
            for j in range(nq):
                _from_residues(ref, slice(j * 128, (j + 1) * 128), sc, a * nq + j, DILATIONS[a % 3])
        for j in range(nq):
            cols = slice(j * 128, (j + 1) * 128)
            oa, ob, oc = (sc[a * nq + j] for a in range(3))
            la, lb, lc = (sc[(3 + a) * nq + j] for a in range(3))
            m = jnp.maximum(jnp.maximum(la, lb), lc)
            wa, wb, wc = jnp.exp(la - m), jnp.exp(lb - m), jnp.exp(lc - m)
            tot = wa + wb + wc
            att = (wa * oa + wb * ob + wc * oc) / tot
            att_ref[:, cols] = att
            lse_ref[:, cols] = m + jnp.log(tot)
            za = za_ref[:, cols]
            ain_ref[:, cols] = (att * za * _sigmoid(za)).astype(BF16)

    return pl.pallas_call(
        body, name="att_merge", grid=(T // TT,),
        in_specs=[_res_spec(g) for _ in range(2) for g in range(3)] + [_tok_spec(ATT_W, C_ZA // ATT_W)],
        out_specs=[_tok_spec(ATT_W)] * 3,
        out_shape=[S((T, ATT_W), F32), S((T, ATT_W), F32), S((T, ATT_W), BF16)],
        scratch_shapes=[pltpu.VMEM((6 * nq, TT, 128), F32)],
        compiler_params=pltpu.CompilerParams(dimension_semantics=("arbitrary",)),
    )(*os_, *lses, proj)


def _att_gate_bwd(dain, att, lse, proj, dproj):
    nq = ATT_W // 128

    def body(d_ref, att_ref, lse_ref, za_ref, buf_ref, dza_ref, da0, da1, da2, at1, at2, ls1, ls2, sc):
        del buf_ref
        for j in range(nq):
            cols = slice(j * 128, (j + 1) * 128)
            za = za_ref[:, cols]
            sg = _sigmoid(za)
            d = d_ref[:, cols].astype(F32)
            att_ = att_ref[:, cols]
            dza_ref[:, cols] = (d * att_ * sg * (1.0 + za * (1.0 - sg))).astype(BF16)
            sc[j] = d * za * sg
            sc[nq + j] = att_
            sc[2 * nq + j] = lse_ref[:, cols]
        for j in range(nq):
            cols = slice(j * 128, (j + 1) * 128)
            for grp, dst in enumerate((da0, da1, da2)):
                _to_residues(sc, j, dst, DILATIONS[grp], cols)
            for grp, dst in ((1, at1), (2, at2)):
                _to_residues(sc, nq + j, dst, DILATIONS[grp], cols)
            for grp, dst in ((1, ls1), (2, ls2)):
                _to_residues(sc, 2 * nq + j, dst, DILATIONS[grp], cols)

    res = (0, 1, 2, 1, 2, 1, 2)
    return pl.pallas_call(
        body, name="att_gate_bwd", grid=(T // TT,),
        in_specs=[_tok_spec(ATT_W)] * 3 + [_tok_spec(ATT_W, C_ZA // ATT_W), pl.BlockSpec(memory_space=pl.ANY)],
        out_specs=[_tok_spec(ATT_W, C_ZA // ATT_W)] + [_res_spec(g) for g in res],
        out_shape=[S(dproj.shape, dproj.dtype)] + [_res_shape(g, BF16) for g in res[:5]]
        + [_res_shape(g, F32) for g in res[5:]],
        input_output_aliases={4: 0},
        scratch_shapes=[pltpu.VMEM((3 * nq, TT, 128), F32)],
        compiler_params=pltpu.CompilerParams(dimension_semantics=("arbitrary",)),
    )(dain, att, lse, proj, dproj)


def _split3(v):
    hi = v.astype(BF16)
    r1 = v - hi.astype(F32)
    mid = r1.astype(BF16)
    lo = (r1 - mid.astype(F32)).astype(BF16)
    return hi, mid, lo


def _chunk_scores(qt, kt, q_ref, k_ref, h):
    cols = slice(h * GDK, (h + 1) * GDK)
    own = jnp.sum(q_ref[:, cols] * (GDK ** -0.5) * k_ref[:, cols], axis=-1, keepdims=True)
    row = lax.broadcasted_iota(jnp.int32, (GLA_C, GLA_C), 0)
    col = lax.broadcasted_iota(jnp.int32, (GLA_C, GLA_C), 1)
    a = _nt(qt.astype(BF16), kt.astype(BF16))
    return jnp.where(col < row, a, jnp.where(col == row, own, 0.0))


def _tri_sum(v, upper):
    n = v.shape[0]
    row = lax.broadcasted_iota(jnp.int32, (n, n), 0)
    col = lax.broadcasted_iota(jnp.int32, (n, n), 1)
    tri = jnp.where(col >= row if upper else col <= row, 1.0, 0.0).astype(BF16)
    hi, mid, lo = _split3(v)
    return _nn(tri, hi) + _nn(tri, mid) + _nn(tri, lo)


def _gla_gates(glr_ref, w2_ref, b_ref):
    logit = _nn(glr_ref[...].astype(BF16), w2_ref[...]) + b_ref[...]
    lg = (jnp.minimum(logit, 0.0) - jnp.log(1.0 + jnp.exp(-jnp.abs(logit)))) * (1.0 / GLA_TAU)
    return logit, _tri_sum(lg, upper=False)


def _gla_head(cum, q_ref, k_ref, h):
    cols = slice(h * GDK, (h + 1) * GDK)
    b = cum[:, cols]
    last = b[GLA_C - 1:GLA_C, :]
    e_pos = jnp.exp(b)
    e_neg = jnp.exp(-b)
    e_end = jnp.exp(last - b)
    qt = q_ref[:, cols] * (GDK ** -0.5) * e_pos
    kt = k_ref[:, cols] * e_neg
    kh = k_ref[:, cols] * e_end
    return b, last, e_pos, e_neg, e_end, qt, kt, kh


def _causal(n):
    return lax.broadcasted_iota(jnp.int32, (n, n), 1) <= lax.broadcasted_iota(jnp.int32, (n, n), 0)


def _gla_fwd(proj, w2p, bg, gn):
    nc = T // GLA_C

    def body(q_ref, k_ref, v_ref, glr_ref, zg_ref, w2_ref, b_ref, gn_ref, o_ref, bin_ref, st_ref, state):
        @pl.when(pl.program_id(0) == 0)
        def _():
            state[...] = jnp.zeros_like(state)

        _, cum = _gla_gates(glr_ref, w2_ref, b_ref)
        for h in range(GH):
            _, last, _, _, _, qt, kt, kh = _gla_head(cum, q_ref, k_ref, h)
            vcols = slice(h * GDV, (h + 1) * GDV)
            st = state[h]
            st_ref[0, h] = st
            v = v_ref[:, vcols].astype(BF16)
            qb = qt.astype(BF16)
            a = _chunk_scores(qt, kt, q_ref, k_ref, h)
            o = _nt(qb, st.astype(BF16)) + _nn(a.astype(BF16), v)
            state[h] = st * jnp.exp(last) + _tn(v, kh.astype(BF16))
            o_ref[:, vcols] = o
            r = lax.rsqrt(jnp.mean(o * o, axis=-1, keepdims=True) + EPS)
            zg = zg_ref[:, vcols]
            bin_ref[:, vcols] = (o * r * gn_ref[...] * zg * _sigmoid(zg)).astype(BF16)

    row = lambda width, cblk: pl.BlockSpec((GLA_C, width), functools.partial(lambda i, c: (i, c), c=cblk))
    full = lambda a: pl.BlockSpec(a.shape, functools.partial(lambda i, nd: (0,) * nd, nd=a.ndim))
    return pl.pallas_call(
        body, name="gla_fwd", grid=(nc,),
        in_specs=[row(512, C_QG // 512), row(512, C_KG // 512), row(1024, C_VG // 1024), row(GLR_W, C_GLR // GLR_W),
                  row(1024, C_ZG // 1024), full(w2p), full(bg), full(gn)],
        out_specs=[pl.BlockSpec((GLA_C, GH * GDV), lambda i: (i, 0)), pl.BlockSpec((GLA_C, GH * GDV), lambda i: (i, 0)),
                   pl.BlockSpec((1, GH, GDV, GDK), lambda i: (i, 0, 0, 0))],
        out_shape=[S((T, GH * GDV), F32), S((T, GH * GDV), BF16), S((nc, GH, GDV, GDK), F32)],
        scratch_shapes=[pltpu.VMEM((GH, GDV, GDK), F32)],
        compiler_params=pltpu.CompilerParams(dimension_semantics=("arbitrary",)),
    )(proj, proj, proj, proj, proj, w2p, bg, gn)


def _gla_bwd(proj, w2p, bg, gn, o_gla, states, dbin, dproj):
    nc = T // GLA_C

    def body(q_ref, k_ref, v_ref, glr_ref, zg_ref, w2_ref, b_ref, gn_ref, o_ref, st_ref, dbin_ref, buf_ref,
             out_ref, dw2_ref, dbg_ref, dgn_ref, dstate, dlogit):
        del buf_ref
        dq_ref = out_ref.at[:, C_QG:C_KG]
        dk_ref = out_ref.at[:, C_KG:C_VG]
        dv_ref = out_ref.at[:, C_VG:C_ZG]
        dzg_ref = out_ref.at[:, C_ZG:C_GLR]
        dglr_ref = out_ref.at[:, C_GLR:C_GLR + GLR_W]
        first = pl.program_id(0) == 0

        @pl.when(first)
        def _():
            dstate[...] = jnp.zeros_like(dstate)

        logit, cum = _gla_gates(glr_ref, w2_ref, b_ref)
        is_last = lax.broadcasted_iota(jnp.int32, (GLA_C, 1), 0) == GLA_C - 1
        dgn = jnp.zeros((1, GDV), F32)
        for h in range(GH):
            _, last, e_pos, e_neg, e_end, qt, kt, kh = _gla_head(cum, q_ref, k_ref, h)
            cols = slice(h * GDK, (h + 1) * GDK)
            vcols = slice(h * GDV, (h + 1) * GDV)
            o = o_ref[:, vcols]
            r = lax.rsqrt(jnp.mean(o * o, axis=-1, keepdims=True) + EPS)
            zg = zg_ref[:, vcols]
            sg = _sigmoid(zg)
            db_ = dbin_ref[:, vcols].astype(F32)
            dlin = db_ * zg * sg
            dzg_ref[:, vcols] = (db_ * (o * r * gn_ref[...]) * sg * (1.0 + zg * (1.0 - sg))).astype(BF16)
            u = dlin * gn_ref[...]
            do = (r * u - o * (r * r * r) * jnp.mean(u * o, axis=-1, keepdims=True)).astype(BF16)
            dgn = dgn + jnp.sum(dlin * o * r, axis=0, keepdims=True)
            st = st_ref[0, h]
            dst = dstate[h]
            v = v_ref[:, vcols].astype(BF16)
            qb, kb, khb = qt.astype(BF16), kt.astype(BF16), kh.astype(BF16)
            dstb = dst.astype(BF16)
            causal = _causal(GLA_C)
            a = _chunk_scores(qt, kt, q_ref, k_ref, h).astype(BF16)
            da = jnp.where(causal, _nt(do, v), 0.0).astype(BF16)
            dqt = _nn(do, st.astype(BF16)) + _nn(da, kb)
            dkt = _tn(da, qb)
            dkh = _nn(v, dstb)
            dv_ref[:, vcols] = (_tn(a, do) + _nt(khb, dstb)).astype(BF16)
            lam = jnp.exp(last)
            dlam = jnp.sum(dst * st, axis=0, keepdims=True)
            dstate[h] = dst * lam + _tn(do, qb)
            dq_ref[:, cols] = (dqt * e_pos * (GDK ** -0.5)).astype(BF16)
            dk_ref[:, cols] = (dkt * e_neg + dkh * e_end).astype(BF16)
            dkh_kh = dkh * kh
            dcum = dqt * qt - dkt * kt - dkh_kh
            dlast = jnp.sum(dkh_kh, axis=0, keepdims=True) + dlam * lam
            dcum = jnp.where(is_last, dcum + dlast, dcum)
            dlg = _tri_sum(dcum, upper=True)
            dlogit[:, cols] = dlg * (1.0 / GLA_TAU) * (1.0 - _sigmoid(logit[:, cols]))

        dl = dlogit[...]
        dlb = dl.astype(BF16)
        dglr_ref[...] = _nt(dlb, w2_ref[...]).astype(BF16)
        dw2 = _tn(glr_ref[...].astype(BF16), dlb)
        dbg = jnp.sum(dl, axis=0, keepdims=True)

        @pl.when(first)
        def _():
            dw2_ref[...] = dw2
            dbg_ref[...] = dbg
            dgn_ref[...] = dgn

        @pl.when(jnp.logical_not(first))
        def _():
            dw2_ref[...] += dw2
            dbg_ref[...] += dbg
            dgn_ref[...] += dgn

    rev = lambda i: nc - 1 - i
    row = lambda width, cblk: pl.BlockSpec((GLA_C, width), functools.partial(lambda i, c: (rev(i), c), c=cblk))
    full = lambda a: pl.BlockSpec(a.shape, functools.partial(lambda i, nd: (0,) * nd, nd=a.ndim))
    keep = lambda shape: pl.BlockSpec(shape, functools.partial(lambda i, nd: (0,) * nd, nd=len(shape)))
    return pl.pallas_call(
        body, name="gla_bwd", grid=(nc,),
        in_specs=[row(512, C_QG // 512), row(512, C_KG // 512), row(1024, C_VG // 1024), row(GLR_W, C_GLR // GLR_W),
                  row(1024, C_ZG // 1024), full(w2p), full(bg), full(gn), row(GH * GDV, 0),
                  pl.BlockSpec((1, GH, GDV, GDK), lambda i: (rev(i), 0, 0, 0)), row(GH * GDV, 0),
                  pl.BlockSpec(memory_space=pl.ANY)],
        out_specs=[row(GLA_GROUP_W, 0), keep((GLR_W, 512)), keep((1, 512)), keep((1, GDV))],
        out_shape=[S(dproj.shape, dproj.dtype), S((GLR_W, 512), F32), S((1, 512), F32), S((1, GDV), F32)],
        input_output_aliases={11: 0},
        scratch_shapes=[pltpu.VMEM((GH, GDV, GDK), F32), pltpu.VMEM((GLA_C, GH * GDK), F32)],
        compiler_params=pltpu.CompilerParams(dimension_semantics=("arbitrary",)),
    )(proj, proj, proj, proj, proj, w2p, bg, gn, o_gla, states, dbin, dproj)


RT = 512


def _rowchain(body, name, ins, outs, scratch=()):
    in_specs, args = [], []
    for spec in ins:
        if spec[0] == "tok":
            _, arr, width, cblk = spec
            in_specs.append(pl.BlockSpec((RT, width), functools.partial(lambda i, c: (i, c), c=cblk)))
        else:
            arr = spec[1]
            in_specs.append(pl.BlockSpec(arr.shape, functools.partial(lambda i, nd: (0,) * nd, nd=arr.ndim)))
        args.append(arr)
    out_specs, out_shape = [], []
    for spec in outs:
        if spec[0] == "tok":
            _, shape, dtype, width, cblk = spec
            out_specs.append(pl.BlockSpec((RT, width), functools.partial(lambda i, c: (i, c), c=cblk)))
        else:
            _, shape, dtype = spec
            out_specs.append(pl.BlockSpec(shape, functools.partial(lambda i, nd: (0,) * nd, nd=len(shape))))
        out_shape.append(S(shape, dtype))
    return pl.pallas_call(
        body, name=name, grid=(T // RT,), in_specs=in_specs, out_specs=out_specs, out_shape=out_shape,
        scratch_shapes=list(scratch), compiler_params=pltpu.CompilerParams(dimension_semantics=("arbitrary",)),
    )(*args)


def _tok(arr, width=None, cblk=0):
    return ("tok", arr, arr.shape[1] if width is None else width, cblk)


def _tok_out(dtype, width=D):
    return ("tok", (T, width), dtype, width, 0)


def _branches_fwd(ain, bin_, proj, x, w_att, w_gla, w_out):
    def body(ain_ref, bin_ref, g_ref, x_ref, wa_ref, wg_ref, wo_ref, ya_ref, yb_ref, y_ref, x1_ref):
        ya = _nn(ain_ref[...], wa_ref[...]).astype(BF16)
        yb = _nn(bin_ref[...], wg_ref[...]).astype(BF16)
        ya_ref[...] = ya
        yb_ref[...] = yb
        y = (_sigmoid(g_ref[:, :D]) * ya.astype(F32) + _sigmoid(g_ref[:, D:]) * yb.astype(F32)).astype(BF16)
        y_ref[...] = y
        x1_ref[...] = x_ref[...] + _nn(y, wo_ref[...])

    return _rowchain(body, "branches_fwd",
                     [_tok(ain), _tok(bin_), _tok(proj, 2 * D, C_GA // (2 * D)), _tok(x), ("all", w_att),
                      ("all", w_gla), ("all", w_out)],
                     [_tok_out(BF16), _tok_out(BF16), _tok_out(BF16), _tok_out(F32)])


def _accumulate(ref, part, first):
    @pl.when(first)
    def _():
        ref[...] = part

    @pl.when(jnp.logical_not(first))
    def _():
        ref[...] += part


def _ple_loss(x1, p, target, g2, w_pg, w_ple):
    def body(x1_ref, p_ref, t_ref, g_ref, wpg_ref, wple_ref, n2_ref, loss_ref, dout_ref, du_ref, dwple_ref, acc):
        first = pl.program_id(0) == 0
        x1 = x1_ref[...]
        r = lax.rsqrt(jnp.mean(x1 * x1, axis=-1, keepdims=True) + EPS)
        n2 = (x1 * r * g_ref[...]).astype(BF16)
        n2_ref[...] = n2
        pg = _sigmoid(_nn(n2, wpg_ref[...]))
        pb = p_ref[...].astype(BF16)
        e_ = _nn(pb, wple_ref[...])
        diff = x1 + e_ * pg - t_ref[...]
        _accumulate(acc, jnp.sum(diff * diff, axis=0, keepdims=True), first)
        dout = diff * (1.0 / D)
        dout_ref[...] = dout
        du_ref[...] = (dout * e_ * pg * (1.0 - pg)).astype(BF16)
        _accumulate(dwple_ref, _tn(pb, (dout * pg).astype(BF16)), first)
        loss_ref[...] = jnp.zeros((1, 128), F32) + jnp.sum(acc[...], axis=-1, keepdims=True) * (0.5 / D)

    return _rowchain(body, "ple_loss", [_tok(x1), _tok(p), _tok(target), ("all", g2), ("all", w_pg), ("all", w_ple)],
                     [_tok_out(BF16), ("acc", (1, 128), F32), _tok_out(F32), _tok_out(BF16), ("acc", (PLE, D), F32)],
                     scratch=[pltpu.VMEM((1, D), F32)])


def _ple_bwd(du, n2, y, x1, dout, g2, w_pg, w_out):
    def body(du_ref, n2_ref, y_ref, x1_ref, dout_ref, g_ref, wpg_ref, wo_ref, dx_ref, dy_ref, dg_ref, dwpg_ref,
             dwo_ref):
        first = pl.program_id(0) == 0
        x1 = x1_ref[...]
        r = lax.rsqrt(jnp.mean(x1 * x1, axis=-1, keepdims=True) + EPS)
        du_ = du_ref[...]
        dn = _nt(du_, wpg_ref[...])
        u = dn * g_ref[...]
        dx = dout_ref[...] + r * u - x1 * (r * r * r) * jnp.mean(u * x1, axis=-1, keepdims=True)
        dxb = dx.astype(BF16)
        dx_ref[...] = dx
        dy_ref[...] = _nt(dxb, wo_ref[...]).astype(BF16)
        _accumulate(dg_ref, jnp.sum(dn * x1 * r, axis=0, keepdims=True), first)
        _accumulate(dwpg_ref, _tn(n2_ref[...], du_), first)
        _accumulate(dwo_ref, _tn(y_ref[...], dxb), first)

    return _rowchain(body, "ple_bwd",
                     [_tok(du), _tok(n2), _tok(y), _tok(x1), _tok(dout), ("all", g2), ("all", w_pg), ("all", w_out)],
                     [_tok_out(F32), _tok_out(BF16), ("acc", (1, D), F32), ("acc", (D, D), F32), ("acc", (D, D), F32)])


def _branches_bwd(dy, ya, yb, ain, bin_, proj, w_att, w_gla):
    def body(dy_ref, ya_ref, yb_ref, ain_ref, bin_ref, g_ref, wa_ref, wg_ref, dg_ref, dain_ref, dbin_ref,
             dwa_ref, dwg_ref):
        first = pl.program_id(0) == 0
        dy_ = dy_ref[...].astype(F32)
        sa, sb = _sigmoid(g_ref[:, :D]), _sigmoid(g_ref[:, D:])
        dg_ref[:, :D] = (dy_ * ya_ref[...].astype(F32) * sa * (1.0 - sa)).astype(BF16)
        dg_ref[:, D:] = (dy_ * yb_ref[...].astype(F32) * sb * (1.0 - sb)).astype(BF16)
        dya = (dy_ * sa).astype(BF16)
        dyb = (dy_ * sb).astype(BF16)
        dain_ref[...] = _nt(dya, wa_ref[...]).astype(BF16)
        dbin_ref[...] = _nt(dyb, wg_ref[...]).astype(BF16)
        _accumulate(dwa_ref, _tn(ain_ref[...], dya), first)
        _accumulate(dwg_ref, _tn(bin_ref[...], dyb), first)

    gates = C_GA // (2 * D)
    return _rowchain(body, "branches_bwd",
                     [_tok(dy), _tok(ya), _tok(yb), _tok(ain), _tok(bin_), _tok(proj, 2 * D, gates), ("all", w_att),
                      ("all", w_gla)],
                     [("tok", (T, NCOL), BF16, 2 * D, gates), _tok_out(BF16, ATT_W), _tok_out(BF16),
                      ("acc", (ATT_W, D), F32), ("acc", (D, D), F32)])


def _peer(k):
    x, y, c = lax.axis_index("x"), lax.axis_index("y"), lax.axis_index("c")
    return (x ^ ((k >> 2) & 1), y ^ ((k >> 1) & 1), c ^ (k & 1))


def _my_index():
    return 4 * lax.axis_index("x") + 2 * lax.axis_index("y") + lax.axis_index("c")


def _peer_index(k):
    px, py, pc = _peer(k)
    return 4 * px + 2 * py + pc


def _pairwise_plan(src_of, dst_of, landed_of, own_src, own_dst):
    def plan(ins, outs, send, recv, local):
        n = len(ins)

        def own():
            return [pltpu.make_async_copy(own_src(ins[a]), own_dst(outs[a]), local.at[a]) for a in range(n)]

        def remote(k, a, src, dst):
            return pltpu.make_async_remote_copy(src_ref=src, dst_ref=dst, send_sem=send.at[k - 1, a],
                                                recv_sem=recv.at[k - 1, a], device_id=_peer(k), device_id_type=MESH)

        def sent():
            return [remote(k, a, src_of(ins[a], k), dst_of(outs[a])) for k in range(1, NDEV) for a in range(n)]

        def start():
            for cp in own() + sent():
                cp.start()

        def finish():
            for k in range(1, NDEV):
                for a in range(n):
                    remote(k, a, own_src(ins[a]), landed_of(outs[a], k)).wait_recv()
            for cp in sent():
                cp.wait_send()
            for cp in own():
                cp.wait()

        return start, finish

    return plan


def _pairwise_sems(n):
    return [pltpu.SemaphoreType.DMA((NDEV - 1, n)), pltpu.SemaphoreType.DMA((NDEV - 1, n)),
            pltpu.SemaphoreType.DMA((n,))]


def _gather_side(arrs):
    plan = _pairwise_plan(src_of=lambda i, k: i, dst_of=lambda o: o.at[_my_index()],
                          landed_of=lambda o, k: o.at[_peer_index(k)],
                          own_src=lambda i: i, own_dst=lambda o: o.at[_my_index()])
    return dict(arrs=arrs, out_shape=[S((NDEV,) + a.shape, a.dtype) for a in arrs],
                scratch=_pairwise_sems(len(arrs)), plan=plan)


def _exchange_side(arrs):
    plan = _pairwise_plan(src_of=lambda i, k: i.at[_peer_index(k)], dst_of=lambda o: o.at[_my_index()],
                          landed_of=lambda o, k: o.at[_peer_index(k)],
                          own_src=lambda i: i.at[_my_index()], own_dst=lambda o: o.at[_my_index()])
    return dict(arrs=arrs, out_shape=[S(a.shape, a.dtype) for a in arrs], scratch=_pairwise_sems(len(arrs)), plan=plan)


def _comm_call(side, name):
    n = len(side["arrs"])

    def body(*refs):
        start, finish = side["plan"](refs[:n], refs[n:2 * n], *refs[2 * n:])
        start()
        finish()

    hbm = pl.BlockSpec(memory_space=pl.ANY)
    return pl.pallas_call(body, name=name, in_specs=[hbm] * n, out_specs=[hbm] * n, out_shape=side["out_shape"],
                          scratch_shapes=side["scratch"])(*side["arrs"])


def _all_gather_by_chip(arrs, name, beside):
    n = len(arrs)
    extra, extra_shapes, work = beside
    n_in, n_out = n + len(extra), n + len(extra_shapes)

    def body(*refs):
        ins, outs = refs[:n], refs[n_in:n_in + n]
        send, recv, local = refs[n_in + n_out:]
        x, y, c = lax.axis_index("x"), lax.axis_index("y"), lax.axis_index("c")
        me, sibling = (x, y, c), (x, y, 1 - c)
        chips = [(1 - x, y), (x, 1 - y), (1 - x, 1 - y)]

        def copy(k, a, block, to, src=None):
            px, py, pc = block
            slot = outs[a].at[4 * px + 2 * py + pc]
            return pltpu.make_async_remote_copy(
                src_ref=slot if src is None else src, dst_ref=slot, send_sem=send.at[k, a], recv_sem=recv.at[k, a],
                device_id=to, device_id_type=MESH)

        north = c == 1
        via = (jnp.where(north, 1 - x, x), jnp.where(north, y, 1 - y))
        onward = (jnp.where(north, x, 1 - x), jnp.where(north, 1 - y, y), c)
        mine = [pltpu.make_async_copy(ins[a], outs[a].at[4 * x + 2 * y + c], local.at[a]) for a in range(n)]
        first = []
        for a in range(n):
            first.append(copy(0, a, me, sibling, src=ins[a]))
            first += [copy(1 + j, a, me, (*chips[j], c), src=ins[a]) for j in range(2)]
        for cp in mine + first:
            cp.start()
        work(refs[n:n_in], refs[n_in + n:n_in + n_out])
        passed = []
        for j in range(2):
            for a in range(n):
                copy(1 + j, a, (*chips[j], c), me).wait_recv()
                passed.append(copy(4 + j, a, (*chips[j], c), sibling))
                passed[-1].start()
        for a in range(n):
            passed.append(copy(3, a, (*via, c), onward))
            passed[-1].start()
        for a in range(n):
            copy(3, a, (*chips[2], c), me).wait_recv()
            passed.append(copy(6, a, (*chips[2], c), sibling))
            passed[-1].start()
        for a in range(n):
            copy(0, a, sibling, me).wait_recv()
        for j, chip in enumerate(chips):
            for a in range(n):
                copy(4 + j, a, (*chip, 1 - c), me).wait_recv()
        for cp in first + passed:
            cp.wait_send()
        for cp in mine:
            cp.wait()

    hbm, vmem = pl.BlockSpec(memory_space=pl.ANY), pl.BlockSpec(memory_space=pltpu.VMEM)
    return pl.pallas_call(
        body, name=name, in_specs=[hbm] * n + [vmem] * len(extra), out_specs=[hbm] * n + [vmem] * len(extra_shapes),
        out_shape=[S((NDEV,) + a.shape, a.dtype) for a in arrs] + list(extra_shapes),
        scratch_shapes=[pltpu.SemaphoreType.DMA((NDEV - 1, n)), pltpu.SemaphoreType.DMA((NDEV - 1, n)),
                        pltpu.SemaphoreType.DMA((n,))],
    )(*arrs, *extra)


NCHIP = 4


def _sibling_sum(src, name, tc=512):
    _, rows, cols = src.shape
    assert cols % tc == 0

    def body(src_ref, got_ref, out_ref, a_buf, b_buf, o_buf, send, recv, local):
        x, y, c = lax.axis_index("x"), lax.axis_index("y"), lax.axis_index("c")
        copies = [pltpu.make_async_remote_copy(
            src_ref=src_ref.at[2 * q + (1 - c)], dst_ref=got_ref.at[q], send_sem=send.at[q], recv_sem=recv.at[q],
            device_id=(x, y, 1 - c), device_id_type=MESH) for q in range(NCHIP)]
        for cp in copies:
            cp.start()
        tiles = [(q, pl.ds(t * tc, tc)) for q in range(NCHIP) for t in range(cols // tc)]

        def loads(n):
            q, tile = tiles[n]
            return [pltpu.make_async_copy(src_ref.at[2 * q + c, :, tile], a_buf.at[n % 2], local.at[n % 2, 0]),
                    pltpu.make_async_copy(got_ref.at[q, :, tile], b_buf.at[n % 2], local.at[n % 2, 1])]

        def store(n):
            q, tile = tiles[n]
            return pltpu.make_async_copy(o_buf.at[n % 2], out_ref.at[q, :, tile], local.at[n % 2, 2])

        def fetch(n):
            if n == 0 or tiles[n][0] != tiles[n - 1][0]:
                copies[tiles[n][0]].wait_recv()
            for cp in loads(n):
                cp.start()

        fetch(0)
        for n in range(len(tiles)):
            if n + 1 < len(tiles):
                fetch(n + 1)
            for cp in loads(n):
                cp.wait()
            if n >= 2:
                store(n - 2).wait()
            o_buf[n % 2] = (a_buf[n % 2].astype(F32) + b_buf[n % 2].astype(F32)).astype(BF16)
            store(n).start()
        store(len(tiles) - 2).wait()
        store(len(tiles) - 1).wait()
        for cp in copies:
            cp.wait_send()

    hbm = pl.BlockSpec(memory_space=pl.ANY)
    block = S((NCHIP, rows, cols), BF16)
    return pl.pallas_call(
        body, name=name, in_specs=[hbm], out_specs=[hbm, hbm], out_shape=[block, block],
        scratch_shapes=[pltpu.VMEM((2, rows, tc), BF16)] * 3
        + [pltpu.SemaphoreType.DMA((NCHIP,)), pltpu.SemaphoreType.DMA((NCHIP,)), pltpu.SemaphoreType.DMA((2, 3))],
    )(src)[1]


def _chips_side(arrs):
    def plan(ins, outs, send, recv, local):
        n = len(ins)

        def places():
            x, y, c = lax.axis_index("x"), lax.axis_index("y"), lax.axis_index("c")
            return 2 * x + y, c, [(1 - x, y), (x, 1 - y), (1 - x, 1 - y)]

        def own():
            here, _, _ = places()
            return [pltpu.make_async_copy(ins[a].at[here], outs[a].at[here], local.at[a]) for a in range(n)]

        def remote(j, a, src_slot, dst_slot):
            _, c, chips = places()
            cx, cy = chips[j]
            return pltpu.make_async_remote_copy(
                src_ref=ins[a].at[src_slot], dst_ref=outs[a].at[dst_slot], send_sem=send.at[j, a],
                recv_sem=recv.at[j, a], device_id=(cx, cy, c), device_id_type=MESH)

        def sent():
            here, _, chips = places()
            return [remote(j, a, 2 * cx + cy, here) for j, (cx, cy) in enumerate(chips) for a in range(n)]

        def start():
            for cp in own() + sent():
                cp.start()

        def finish():
            here, _, chips = places()
            for j, (cx, cy) in enumerate(chips):
                for a in range(n):
                    remote(j, a, here, 2 * cx + cy).wait_recv()
            for cp in sent():
                cp.wait_send()
            for cp in own():
                cp.wait()

        return start, finish

    n = len(arrs)
    return dict(arrs=arrs, out_shape=[S(a.shape, a.dtype) for a in arrs],
                scratch=[pltpu.SemaphoreType.DMA((NCHIP - 1, n)), pltpu.SemaphoreType.DMA((NCHIP - 1, n)),
                         pltpu.SemaphoreType.DMA((n,))], plan=plan)


def _adamw_shards(parts, places):
    n_src = len(parts)

    def body(*refs):
        srcs, rest = refs[:n_src], refs[n_src:]
        for j, (src, rows, cols, _) in enumerate(places):
            w_ref, m_ref, v_ref = rest[3 * j:3 * j + 3]
            outs = rest[3 * len(places) + 4 * j:3 * len(places) + 4 * j + 4]
            p_ref = srcs[src]
            g = p_ref[0, rows, cols].astype(F32)
            for s in range(1, p_ref.shape[0]):
                g = g + p_ref[s, rows, cols].astype(F32)
            delta, m_new, v_new = _adam_math(g, w_ref[0], m_ref[0], v_ref[0])
            for ref, val in zip(outs, (g, delta, m_new, v_new)):
                ref[0] = val

    flat = [a for place in places for a in place[3]]
    return pl.pallas_call(
        body, name="adam_shards",
        out_shape=[S(place[3][0].shape, F32) for place in places for _ in range(4)],
    )(*parts, *flat)


def _adam_math(g, w, m, v):
    c1 = 1.0 - ADAM_B1 ** ADAM_STEP
    c2 = 1.0 - ADAM_B2 ** ADAM_STEP
    m_new = ADAM_B1 * m + (1.0 - ADAM_B1) * g
    v_new = ADAM_B2 * v + (1.0 - ADAM_B2) * (g * g)
    return -ADAM_LR * ((m_new / c1) / (jnp.sqrt(v_new / c2) + ADAM_EPS) + ADAM_WD * w), m_new, v_new


def _adamw_small(parts, params, loss_parts):
    n = len(params)

    def body(*refs):
        p_refs, rest = refs[:n], refs[n + 1:]
        total = refs[n][0]
        for s in range(1, NDEV):
            total = total + refs[n][s]
        refs[-1][...] = total
        for j in range(n):
            w_ref, m_ref, v_ref = rest[3 * j:3 * j + 3]
            g_ref, d_ref, mo_ref, vo_ref = rest[3 * n + 4 * j:3 * n + 4 * j + 4]
            width = w_ref.shape[1]
            g = p_refs[j][0]
            for s in range(1, NDEV):
                g = g + p_refs[j][s]
            g = g[:, :width]
            delta, m_new, v_new = _adam_math(g, w_ref[...], m_ref[...], v_ref[...])
            g_ref[...] = g
            d_ref[...] = delta
            mo_ref[...] = m_new
            vo_ref[...] = v_new

    flat = [a for group in params for a in group]
    return pl.pallas_call(
        body, name="adam_small",
        out_shape=[S(group[0].shape, F32) for group in params for _ in range(4)] + [S((1, 128), F32)],
    )(*parts, loss_parts, *flat)


def _adamw_rows(parts, w, m, v, name, tc=256):
    rows, _, cols = w.shape
    nparts = parts.shape[0]
    nsteps = cols // tc

    def body(p_ref, w_hbm, m_hbm, v_hbm, g_hbm, d_hbm, mo_hbm, vo_hbm, inbuf, outbuf, insem, outsem):
        i = pl.program_id(0)
        slot = i & 1

        def view(ref, step):
            return ref.at[:, 0, pl.ds(pl.multiple_of(step * tc, tc), tc)]

        def fetch(step, sl):
            return [pltpu.make_async_copy(view(src, step), inbuf.at[sl, k], insem.at[sl, k])
                    for k, src in enumerate((w_hbm, m_hbm, v_hbm))]

        def write(step, sl):
            return [pltpu.make_async_copy(outbuf.at[sl, k], view(dst, step), outsem.at[sl, k])
                    for k, dst in enumerate((g_hbm, d_hbm, mo_hbm, vo_hbm))]

        @pl.when(i == 0)
        def _():
            for cp in fetch(0, 0):
                cp.start()

        @pl.when(i + 1 < nsteps)
        def _():
            for cp in fetch(i + 1, 1 - slot):
                cp.start()

        for cp in fetch(i, slot):
            cp.wait()

        @pl.when(i >= 2)
        def _():
            for cp in write(i - 2, slot):
                cp.wait()

        g = p_ref[0].astype(F32)
        for s in range(1, nparts):
            g = g + p_ref[s].astype(F32)
        g = g[:rows]
        delta, m_new, v_new = _adam_math(g, inbuf[slot, 0], inbuf[slot, 1], inbuf[slot, 2])
        for k, val in enumerate((g, delta, m_new, v_new)):
            outbuf[slot, k] = val
        for cp in write(i, slot):
            cp.start()

        @pl.when(i == nsteps - 1)
        def _():
            for cp in write(i - 1, 1 - slot) + write(i, slot):
                cp.wait()

    hbm = pl.BlockSpec(memory_space=pl.ANY)
    assert nsteps >= 2
    return pl.pallas_call(
        body, name=name, grid=(nsteps,),
        in_specs=[pl.BlockSpec((nparts, parts.shape[1], tc), lambda i: (0, 0, i)), hbm, hbm, hbm],
        out_specs=[hbm] * 4, out_shape=[S((rows, 1, cols), F32)] * 4,
        scratch_shapes=[pltpu.VMEM((2, 3, rows, tc), F32), pltpu.VMEM((2, 4, rows, tc), F32),
                        pltpu.SemaphoreType.DMA((2, 3)), pltpu.SemaphoreType.DMA((2, 4))],
        compiler_params=pltpu.CompilerParams(dimension_semantics=("arbitrary",)),
    )(parts, w, m, v)


SLAB = 1296
REMAP_RUNS = 4
_PIECES = ((O_QA, O_ZA, C_QA), (O_ZA, O_QG, C_ZA), (O_QG, O_GLR, C_QG), (O_GLR, O_ZG, C_GLR), (O_ZG, O_GA, C_ZG),
           (O_GA, O_END, C_GA))


def _slab_row_of_aligned(a):
    for o0, o1, a0 in _PIECES:
        if a0 <= a < a0 + o1 - o0:
            c = o0 + a - a0
            return SLAB * (c // W_IN_SHARD) + c % W_IN_SHARD
    return -1


def _aligned_row_of_slab(r):
    d, l = divmod(r, SLAB)
    if l >= W_IN_SHARD:
        return -1
    c = d * W_IN_SHARD + l
    for o0, o1, a0 in _PIECES:
        if o0 <= c < o1:
            return a0 + c - o0
    raise AssertionError(c)


def _remap_table(row_of, n_out, block, n_src):
    win = block + 16
    table = []
    for b in range(n_out // block):
        runs = []
        for i in range(block):
            s = row_of(b * block + i)
            if s < 0:
                continue
            if runs and runs[-1][0] + runs[-1][2] == s and runs[-1][1] + runs[-1][2] == i:
                runs[-1][2] += 1
            else:
                runs.append([s, i, 1])
        assert len(runs) <= REMAP_RUNS, (b, runs)
        row = []
        for s, i, n in runs:
            w = min(s // 16 * 16, n_src - win)
            assert 0 <= s - w and s - w + n <= win
            row += [w, s - w, i, n]
        table.append(row + [0] * (4 * REMAP_RUNS - len(row)))
    return table


def _remap_rows(src, row_of, n_out, block, name):
    n_src, cols = src.shape
    nb, win = n_out // block, block + 16
    table = _remap_table(row_of, n_out, block, n_src)
    runs = [[tuple(row[4 * k:4 * k + 4]) for k in range(REMAP_RUNS) if row[4 * k + 3] > 0] for row in table]

    def body(src_hbm, out_hbm, wbuf, obuf, insem, outsem):
        def fetches(b):
            return [pltpu.make_async_copy(src_hbm.at[pl.ds(w, win)], wbuf.at[b % 2, k], insem.at[b % 2, k])
                    for k, (w, _, _, _) in enumerate(runs[b])]

        def store(b):
            return pltpu.make_async_copy(obuf.at[b % 2], out_hbm.at[pl.ds(b * block, block)], outsem.at[b % 2])

        for cp in fetches(0):
            cp.start()
        for b in range(nb):
            if b + 1 < nb:
                for cp in fetches(b + 1):
                    cp.start()
            for cp in fetches(b):
                cp.wait()
            if b >= 2:
                store(b - 2).wait()
            if sum(count for _, _, _, count in runs[b]) < block:
                obuf[b % 2] = jnp.zeros((block, cols), src.dtype)
            for k, (_, shift, first, count) in enumerate(runs[b]):
                obuf[b % 2, first:first + count, :] = wbuf[b % 2, k, shift:shift + count, :]
            store(b).start()
        store(nb - 2).wait()
        store(nb - 1).wait()

    hbm = pl.BlockSpec(memory_space=pl.ANY)
    return pl.pallas_call(
        body, name=name, in_specs=[hbm], out_specs=hbm, out_shape=S((n_out, cols), src.dtype),
        scratch_shapes=[pltpu.VMEM((2, REMAP_RUNS, win, cols), src.dtype), pltpu.VMEM((2, block, cols), src.dtype),
                        pltpu.SemaphoreType.DMA((2, REMAP_RUNS)), pltpu.SemaphoreType.DMA((2,))],
    )(src)


def _col_blocks(w, width):
    return w.reshape(w.shape[0], NDEV, width).transpose(1, 0, 2)


def _from_col_blocks(w):
    return w.transpose(1, 0, 2).reshape(w.shape[1], NDEV * w.shape[2])


def _local_step(x2, p2, pos, tgt, norm_g, qk_norm_q, qk_norm_k, gla_gate_b, gla_norm_g, ple_norm_g, w_al,
                weights=None, proj_side=None, unpack=None, dw_side_of=None, dh_side_of=None):
    half = ROT_DIM // 2
    inv8 = jnp.power(jnp.float32(ROPE_THETA), -jnp.arange(half, dtype=F32) * 2.0 / ROT_DIM)
    inv = jnp.tile(jnp.concatenate([inv8, inv8, jnp.zeros((HD - ROT_DIM,), F32)]), 2).reshape(1, 128)
    gq = jnp.tile(qk_norm_q, (1, 2))
    gk = jnp.tile(qk_norm_k, (1, 2))

    proj, h, got = _proj_rms(x2, norm_g, w_al, proj_side)
    if proj_side is not None:
        weights = unpack(got)
    w2p, w_att_f, w_gla_f, w_out_f, w_pg_f, w_ple_f = weights
    qkv = _qk_prep(proj, pos, inv, gq, gk)
    fwd = [_att_fwd(qkv[g], qkv[3 + g], qkv[6 + g], g, f"att_fwd{g}") for g in range(3)]
    att, lse, ain = _att_merge([f[0] for f in fwd], [f[1] for f in fwd], proj)
    o_gla, bin_, states = _gla_fwd(proj, w2p, gla_gate_b, gla_norm_g)
    ya, yb, y, x1 = _branches_fwd(ain, bin_, proj, x2, w_att_f, w_gla_f, w_out_f)
    n2, loss_v, dout, du, dw_ple = _ple_loss(x1, p2, tgt, ple_norm_g, w_pg_f, w_ple_f)

    dx1, dy, dg_ple, dw_pg, dw_out = _ple_bwd(du, n2, y, x1, dout, ple_norm_g, w_pg_f, w_out_f)
    dproj, dain, dbin, dw_att, dw_gla = _branches_bwd(dy, ya, yb, ain, bin_, proj, w_att_f, w_gla_f)
    dproj, da0, da1, da2, at1, at2, ls1, ls2 = _att_gate_bwd(dain, att, lse, proj, dproj)
    datts, atts, lses = (da0, da1, da2), (att[None], at1, at2), (lse[None], ls1, ls2)
    dproj, dw2, dbg, dgn = _gla_bwd(proj, w2p, gla_gate_b, gla_norm_g, o_gla, states, dbin, dproj)
    bwd = [_att_bwd(qkv[g], qkv[3 + g], qkv[6 + g], datts[g], atts[g], lses[g], g, f"att_bwd{g}") for g in range(3)]
    dproj, dgq, dgk = _qk_bwd(proj, pos, inv, gq, gk, [b[0] for b in bwd], [b[1] for b in bwd],
                              [b[2] for b in bwd], dproj)
    out = dict(loss=loss_v, dw2=dw2, dw_att=dw_att, dw_gla=dw_gla, dw_out=dw_out, dw_pg=dw_pg, dw_ple=dw_ple,
               dgq=dgq, dgk=dgk, dbg=dbg, dgn=dgn, dg_ple=dg_ple)
    if dw_side_of is None:
        dw_al = _mm(dproj, h, mode="tn", name="dw_in", tm=1536, tn=D, tk=T, out_dtype=BF16)
    else:
        dw_al, out["dw_side"] = _mm(dproj, h, mode="tn", name="dw_in", tm=1536, tn=D, tk=T, out_dtype=BF16,
                                    side=dw_side_of(out))
    grad_x, dg_norm, out["dh_side"] = _dh_rms(dproj, w_al, x2, norm_g, dx1,
                                              None if dh_side_of is None else dh_side_of(dw_al))
    out.update(grad_x=grad_x, dw_al=dw_al, dg_norm=dg_norm)
    return out


def kernel(x, p, positions, norm_g, w_in, qk_norm_q, qk_norm_k, gla_gate_w2, gla_gate_b, gla_norm_g, w_att_proj, w_gla_proj, w_out, ple_norm_g, w_ple_gate, w_ple, loss_target, m_norm_g, m_w_in, m_qk_norm_q, m_qk_norm_k, m_gla_gate_w2, m_gla_gate_b, m_gla_norm_g, m_w_att_proj, m_w_gla_proj, m_w_out, m_ple_norm_g, m_w_ple_gate, m_w_ple, v_norm_g, v_w_in, v_qk_norm_q, v_qk_norm_k, v_gla_gate_w2, v_gla_gate_b, v_gla_norm_g, v_w_att_proj, v_w_gla_proj, v_w_out, v_ple_norm_g, v_w_ple_gate, v_w_ple):
    x2, p2, tgt = x[0], p[0, 0], loss_target[0]
    pos = positions.astype(F32).reshape(T, 1)

    def pack_small(ins, outs):
        rows3_ref, cols3_ref = outs
        for j in range(3):
            rows3_ref[j] = ins[j][0].astype(BF16)
        cols3_ref[0:512, :] = ins[3][0].astype(BF16)
        cols3_ref[512:768, :] = ins[4][0].astype(BF16)
        cols3_ref[768:784, :] = jnp.zeros((GLR_N, 128), BF16)
        cols3_ref[768:784, 0:64] = ins[5][0].astype(BF16)

    mine = jnp.pad(w_in[0].T.astype(BF16), ((0, SLAB - W_IN_SHARD), (0, 0)))
    g_in, rows3, cols3 = _all_gather_by_chip(
        [mine], "gather_w_in", ([w_gla_proj, w_out, w_ple_gate, w_att_proj, w_ple, gla_gate_w2],
                                [S((3, 128, D), BF16), S((784, 128), BF16)], pack_small))
    w_al = _remap_rows(g_in.reshape(NDEV * SLAB, D), _slab_row_of_aligned, NCOL, 1536, "align_w_in")

    def unpack(got):
        g_rows, g_cols = got
        w2_f = _from_col_blocks(g_cols[:, 768:784, :64])
        return (jnp.pad(w2_f, ((0, GLR_W - GLR_N), (0, 0))), _from_col_blocks(g_cols[:, :512]),
                g_rows[:, 0].reshape(D, D), g_rows[:, 1].reshape(D, D), g_rows[:, 2].reshape(D, D),
                _from_col_blocks(g_cols[:, 512:768]))

    def dw_side_of(g):
        s_rows = jnp.concatenate([g[k].reshape(NDEV, 128, D) for k in ("dw_gla", "dw_out", "dw_pg")], axis=1)
        s_cols = jnp.concatenate([_col_blocks(g["dw_att"], 128), _col_blocks(g["dw_ple"], 128),
                                  jnp.pad(_col_blocks(g["dw2"][:GLR_N], 64), ((0, 0), (0, 0), (0, 64)))], axis=1)
        return _exchange_side([s_rows.astype(BF16), s_cols.astype(BF16)])

    def dh_side_of(dw_al):
        s_in = _remap_rows(dw_al, _aligned_row_of_slab, NDEV * SLAB, SLAB, "shard_dw_in").reshape(NDEV, SLAB, D)
        return _chips_side([_sibling_sum(s_in, "sibling_sum")])

    loc = _local_step(x2, p2, pos, tgt, norm_g, qk_norm_q, qk_norm_k, gla_gate_b, gla_norm_g, ple_norm_g, w_al,
                      proj_side=_gather_side([rows3, cols3]), unpack=unpack, dw_side_of=dw_side_of,
                      dh_side_of=dh_side_of)
    loss_v, grad_x = loc["loss"], loc["grad_x"]
    dg_norm, dgq, dgk, dbg, dgn, dg_ple = (loc[k] for k in ("dg_norm", "dgq", "dgk", "dbg", "dgn", "dg_ple"))
    r_rows, r_cols = loc["dw_side"]
    (r_in,) = loc["dh_side"]

    r_small = _comm_call(_gather_side([dg_norm, dgq, dgk, dbg, dgn, dg_ple, loss_v]), "gather_small")

    outs = {}

    rows_of = lambda a: jnp.transpose(a, (2, 0, 1))
    outs["w_in"] = [jnp.transpose(o, (1, 2, 0))[0] for o in
                    _adamw_rows(r_in, rows_of(w_in), rows_of(m_w_in), rows_of(v_w_in), "adam_w_in")]
    places = (("w_gla_proj", 0, slice(0, 128), slice(None), (w_gla_proj, m_w_gla_proj, v_w_gla_proj)),
              ("w_out", 0, slice(128, 256), slice(None), (w_out, m_w_out, v_w_out)),
              ("w_ple_gate", 0, slice(256, 384), slice(None), (w_ple_gate, m_w_ple_gate, v_w_ple_gate)),
              ("w_att_proj", 1, slice(0, 512), slice(None), (w_att_proj, m_w_att_proj, v_w_att_proj)),
              ("w_ple", 1, slice(512, 768), slice(None), (w_ple, m_w_ple, v_w_ple)),
              ("gla_gate_w2", 1, slice(768, 784), slice(0, 64), (gla_gate_w2, m_gla_gate_w2, v_gla_gate_w2)))
    res = _adamw_shards([r_rows, r_cols], [place[1:] for place in places])
    for j, place in enumerate(places):
        outs[place[0]] = [o[0] for o in res[4 * j:4 * j + 4]]
    small = ((norm_g, m_norm_g, v_norm_g), (qk_norm_q, m_qk_norm_q, v_qk_norm_q), (qk_norm_k, m_qk_norm_k, v_qk_norm_k),
             (gla_gate_b, m_gla_gate_b, v_gla_gate_b), (gla_norm_g, m_gla_norm_g, v_gla_norm_g),
             (ple_norm_g, m_ple_norm_g, v_ple_norm_g))
    sm = _adamw_small(r_small[:6], small, r_small[6])
    for j, nm in enumerate(("norm_g", "qk_norm_q", "qk_norm_k", "gla_gate_b", "gla_norm_g", "ple_norm_g")):
        outs[nm] = [o[0] for o in sm[4 * j:4 * j + 4]]

    loss = sm[-1][0, 0]
    order = ["norm_g", "w_in", "qk_norm_q", "qk_norm_k", "gla_gate_w2", "gla_gate_b", "gla_norm_g", "w_att_proj",
             "w_gla_proj", "w_out", "ple_norm_g", "w_ple_gate", "w_ple"]
    result = [loss, grad_x[None]]
    for i in range(4):
        result += [outs[nm][i][None] for nm in order]
    return tuple(result)
```

```python
import functools

import jax
import jax.numpy as jnp
from jax import lax
from jax.experimental import pallas as pl
from jax.experimental.pallas import tpu as pltpu

F32 = jnp.float32
BF16 = jnp.bfloat16
S = jax.ShapeDtypeStruct

T = 4096
D = 1024
NDEV = 8
HD = 64
ATT_W = 512
ATT_QKV = 1536
DILATIONS = (1, 4, 16)
BLK = 128
GH, GDK, GDV = 4, 128, 256
GLA_C = 128
PLE = 256
EPS = 1e-6
ROT_DIM = 16
ROPE_THETA = 500000.0
GLA_TAU = 16.0
W_IN_SHARD = 1282

C_QG, C_KG, C_VG, C_ZG, C_GLR, C_ZA, C_GA, C_GB, C_QA, C_KA, C_VA = (
    0, 512, 1024, 2048, 3072, 3584, 4096, 5120, 6144, 7680, 9216)
GLA_GROUP_W = 3584
GLR_W = 512
NCOL = 10752
GLR_N = 16
O_QA, O_ZA, O_QG, O_GLR, O_ZG, O_GA, O_END = 0, 4608, 5120, 7168, 7184, 8208, 10256

ADAM_LR, ADAM_B1, ADAM_B2, ADAM_EPS, ADAM_WD, ADAM_STEP = 0.001, 0.9, 0.999, 1e-08, 0.01, 10

MESH = pl.DeviceIdType.MESH


def _sigmoid(z):
    return 1.0 / (1.0 + jnp.exp(-z))


def _dot(a, b, dims):
    return lax.dot_general(a, b, (dims, ((), ())), preferred_element_type=F32)


def _nn(a, b):
    return _dot(a, b, ((1,), (0,)))


def _nt(a, b):
    return _dot(a, b, ((1,), (1,)))


def _tn(a, b):
    return _dot(a, b, ((0,), (0,)))


def _mm(a, b, *, mode, name, tm, tn, tk, out_dtype=F32, res=None, side=None):
    if mode == "nn":
        (m, k), n = a.shape, b.shape[1]
        a_spec = pl.BlockSpec((tm, tk), lambda i, j, l: (i, l))
        b_spec = pl.BlockSpec((tk, tn), lambda i, j, l: (l, j))
        dot = _nn
    elif mode == "nt":
        (m, k), n = a.shape, b.shape[0]
        a_spec = pl.BlockSpec((tm, tk), lambda i, j, l: (i, l))
        b_spec = pl.BlockSpec((tn, tk), lambda i, j, l: (j, l))
        dot = _nt
    else:
        (k, m), n = a.shape, b.shape[1]
        a_spec = pl.BlockSpec((tk, tm), lambda i, j, l: (l, i))
        b_spec = pl.BlockSpec((tk, tn), lambda i, j, l: (l, j))
        dot = _tn
    assert m % tm == 0 and n % tn == 0 and k % tk == 0, (name, m, n, k)
    grid = (m // tm, n // tn, k // tk)
    nk = grid[2]
    o_spec = pl.BlockSpec((tm, tn), lambda i, j, l: (i, j))
    in_specs = [a_spec, b_spec]
    args = [a, b]
    if res is not None:
        in_specs.append(o_spec)
        args.append(res)
    n_in = len(args)
    n_side = 0 if side is None else len(side["arrs"])
    hbm = pl.BlockSpec(memory_space=pl.ANY)

    def body(*refs):
        a_ref, b_ref = refs[0], refs[1]
        r_ref = refs[2] if res is not None else None
        o_ref = refs[n_in + n_side]
        scratch = refs[n_in + 2 * n_side + 1:]
        if side is not None:
            start, finish_side = side["plan"](refs[n_in:n_in + n_side], refs[n_in + n_side + 1:n_in + 2 * n_side + 1],
                                              *scratch[1 if nk > 1 else 0:])
            ids = [pl.program_id(d) for d in range(3)]

            @pl.when((ids[0] == 0) & (ids[1] == 0) & (ids[2] == 0))
            def _():
                start()

        part = dot(a_ref[...].astype(BF16), b_ref[...].astype(BF16))

        def finish(val):
            if r_ref is not None:
                val = val + r_ref[...]
            o_ref[...] = val.astype(out_dtype)

        if nk == 1:
            finish(part)
        else:
            acc = scratch[0]
            l = pl.program_id(2)

            @pl.when(l == 0)
            def _():
                acc[...] = part

            @pl.when(l > 0)
            def _():
                acc[...] += part

            @pl.when(l == nk - 1)
            def _():
                finish(acc[...])

        if side is not None:
            @pl.when((ids[0] == grid[0] - 1) & (ids[1] == grid[1] - 1) & (ids[2] == grid[2] - 1))
            def _():
                finish_side()

    sems = [] if side is None else side["scratch"]
    outs = pl.pallas_call(
        body, name=name, grid=grid,
        in_specs=in_specs + [hbm] * n_side, out_specs=[o_spec] + [hbm] * n_side,
        out_shape=[S((m, n), out_dtype)] + ([] if side is None else side["out_shape"]),
        scratch_shapes=([pltpu.VMEM((tm, tn), F32)] if nk > 1 else []) + sems,
        compiler_params=pltpu.CompilerParams(
            dimension_semantics=("arbitrary",) * 3 if side is not None else ("parallel", "parallel", "arbitrary")),
    )(*args, *([] if side is None else side["arrs"]))
    return outs[0] if side is None else (outs[0], outs[1:])


def _side_parts(side, refs, n_in, n_out):
    n_side = 0 if side is None else len(side["arrs"])
    scratch = refs[n_in + n_out + 2 * n_side:]
    if side is None:
        return (lambda: None), (lambda: None), scratch
    start, finish = side["plan"](refs[n_in:n_in + n_side], refs[n_in + n_side + n_out:n_in + n_out + 2 * n_side],
                                 *scratch[len(scratch) - len(side["scratch"]):])
    return start, finish, scratch


def _proj_rms(x, g, wt, side=None):
    tm, tn = 1024, 1536
    grid = (T // tm, NCOL // tn)
    n_side = 0 if side is None else len(side["arrs"])
    hbm = pl.BlockSpec(memory_space=pl.ANY)

    def body(*refs):
        x_ref, g_ref, w_ref = refs[:3]
        o_ref, h_ref = refs[3 + n_side], refs[4 + n_side]
        start, finish, _ = _side_parts(side, refs, 3, 2)
        i, j = pl.program_id(0), pl.program_id(1)

        @pl.when((i == 0) & (j == 0))
        def _():
            start()

        @pl.when(j == 0)
        def _():
            xf = x_ref[...]
            r = lax.rsqrt(jnp.mean(xf * xf, axis=-1, keepdims=True) + EPS)
            h_ref[...] = (xf * r * g_ref[...]).astype(BF16)

        o_ref[...] = _nt(h_ref[...], w_ref[...])

        @pl.when((i == grid[0] - 1) & (j == grid[1] - 1))
        def _():
            finish()

    outs = pl.pallas_call(
        body, name="proj", grid=grid,
        in_specs=[pl.BlockSpec((tm, D), lambda i, j: (i, 0)), pl.BlockSpec((1, D), lambda i, j: (0, 0)),
                  pl.BlockSpec((tn, D), lambda i, j: (j, 0))] + [hbm] * n_side,
        out_specs=[pl.BlockSpec((tm, tn), lambda i, j: (i, j)), pl.BlockSpec((tm, D), lambda i, j: (i, 0))] + [hbm] * n_side,
        out_shape=[S((T, NCOL), F32), S((T, D), BF16)] + ([] if side is None else side["out_shape"]),
        scratch_shapes=[] if side is None else side["scratch"],
        compiler_params=pltpu.CompilerParams(dimension_semantics=("arbitrary", "arbitrary")),
    )(x, g, wt, *([] if side is None else side["arrs"]))
    return outs[0], outs[1], outs[2:]


def _dh_rms(dproj, wt, x, g, skip, side=None):
    tm, tk = 1024, 2688
    grid = (T // tm, NCOL // tk)
    n_side = 0 if side is None else len(side["arrs"])
    hbm = pl.BlockSpec(memory_space=pl.ANY)

    def body(*refs):
        a_ref, w_ref, x_ref, g_ref, s_ref = refs[:5]
        dx_ref, dg_ref = refs[5 + n_side], refs[6 + n_side]
        start, finish, scratch = _side_parts(side, refs, 5, 2)
        acc = scratch[0]
        i, l = pl.program_id(0), pl.program_id(1)

        @pl.when((i == 0) & (l == 0))
        def _():
            start()

        part = _nn(a_ref[...], w_ref[...])

        @pl.when(l == 0)
        def _():
            acc[...] = part

        @pl.when(l > 0)
        def _():
            acc[...] += part

        @pl.when(l == grid[1] - 1)
        def _():
            xf = x_ref[...]
            r = lax.rsqrt(jnp.mean(xf * xf, axis=-1, keepdims=True) + EPS)
            dn = acc[...]
            u = dn * g_ref[...]
            dx_ref[...] = s_ref[...] + r * u - xf * (r * r * r) * jnp.mean(u * xf, axis=-1, keepdims=True)
            dg = jnp.sum(dn * xf * r, axis=0, keepdims=True)

            @pl.when(i == 0)
            def _():
                dg_ref[...] = dg

            @pl.when(i > 0)
            def _():
                dg_ref[...] += dg

        @pl.when((i == grid[0] - 1) & (l == grid[1] - 1))
        def _():
            finish()

    tok = pl.BlockSpec((tm, D), lambda i, l: (i, 0))
    outs = pl.pallas_call(
        body, name="dh", grid=grid,
        in_specs=[pl.BlockSpec((tm, tk), lambda i, l: (i, l)), pl.BlockSpec((tk, D), lambda i, l: (l, 0)), tok,
                  pl.BlockSpec((1, D), lambda i, l: (0, 0)), tok] + [hbm] * n_side,
        out_specs=[tok, pl.BlockSpec((1, D), lambda i, l: (0, 0))] + [hbm] * n_side,
        out_shape=[S((T, D), F32), S((1, D), F32)] + ([] if side is None else side["out_shape"]),
        scratch_shapes=[pltpu.VMEM((tm, D), F32)] + ([] if side is None else side["scratch"]),
        compiler_params=pltpu.CompilerParams(dimension_semantics=("arbitrary", "arbitrary")),
    )(dproj, wt, x, g, skip, *([] if side is None else side["arrs"]))
    return outs[0], outs[1], outs[2:]


def _rot_tables(pos_ref, inv_ref):
    lane = lax.broadcasted_iota(jnp.int32, (1, 128), 1) % HD
    ang = pos_ref[...] * inv_ref[...]
    cos, sin = jnp.cos(ang), jnp.sin(ang)
    c = jnp.where(lane < ROT_DIM, cos, 1.0)
    sp = jnp.where((lane >= ROT_DIM // 2) & (lane < ROT_DIM), sin, 0.0)
    sm = jnp.where(lane < ROT_DIM // 2, -sin, 0.0)
    return c, sp, sm


def _head_sums(v):
    same = (lax.broadcasted_iota(jnp.int32, (128, 128), 0) < HD) == (lax.broadcasted_iota(jnp.int32, (128, 128), 1) < HD)
    ones = jnp.where(same, 1.0, 0.0).astype(BF16)
    hi = v.astype(BF16)
    lo = (v - hi.astype(F32)).astype(BF16)
    return _nn(hi, ones) + _nn(lo, ones)


def _pair_norm(t):
    return lax.rsqrt(_head_sums(t * t) * (1.0 / HD) + EPS)


def _pair_mean(t):
    return _head_sums(t) * (1.0 / HD)


TT = 256
NCH = ATT_QKV // 128


def _res_shape(grp, dtype):
    return S((DILATIONS[grp], T // DILATIONS[grp], ATT_W), dtype)


def _res_spec(grp):
    dil = DILATIONS[grp]
    return pl.BlockSpec((dil, TT // dil, ATT_W), lambda i: (0, i, 0))


def _to_residues(sc, j, dst_ref, dil, cols):
    n = TT // dil
    for r in range(dil):
        rows = sc[j] if dil == 1 else sc.at[j][pl.ds(r, n, stride=dil), :]
        dst_ref[r, :, cols] = rows.astype(dst_ref.dtype)


def _from_residues(src_ref, cols, sc, j, dil):
    n = TT // dil
    for r in range(dil):
        if dil == 1:
            sc[j] = src_ref[r, :, cols]
        else:
            sc.at[j][pl.ds(r, n, stride=dil), :] = src_ref[r, :, cols]


def _tok_spec(width, cblk=0):
    return pl.BlockSpec((TT, width), functools.partial(lambda i, c: (i, c), c=cblk))


def _const_spec(arr_or_shape):
    shape = arr_or_shape if isinstance(arr_or_shape, tuple) else arr_or_shape.shape
    return pl.BlockSpec(shape, functools.partial(lambda i, nd: (0,) * nd, nd=len(shape)))


def _qk_prep(proj, pos, inv, gq, gk):
    def body(q_ref, k_ref, v_ref, pos_ref, inv_ref, gq_ref, gk_ref, *rest):
        outs, sc = rest[:9], rest[9]
        c, sp, sm = _rot_tables(pos_ref, inv_ref)
        for which, (src, g_ref) in enumerate(((q_ref, gq_ref), (k_ref, gk_ref), (v_ref, None))):
            if g_ref is not None:
                g = jnp.broadcast_to(g_ref[...] * ((HD ** -0.5) if which == 0 else 1.0), c.shape)
                cg, spg, smg = c * g, sp * pltpu.roll(g, 8, 1), sm * pltpu.roll(g, 120, 1)
            for j in range(NCH):
                t = src[:, j * 128:(j + 1) * 128]
                if g_ref is not None:
                    t = _pair_norm(t) * (t * cg + pltpu.roll(t, 8, 1) * spg + pltpu.roll(t, 120, 1) * smg)
                sc[j] = t
            for j in range(NCH):
                grp, sub = divmod(j * 128, ATT_W)
                _to_residues(sc, j, outs[which * 3 + grp], DILATIONS[grp], slice(sub, sub + 128))

    return pl.pallas_call(
        body, name="qk_prep", grid=(T // TT,),
        in_specs=[_tok_spec(ATT_QKV, C_QA // ATT_QKV), _tok_spec(ATT_QKV, C_KA // ATT_QKV),
                  _tok_spec(ATT_QKV, C_VA // ATT_QKV), _tok_spec(1), _const_spec(inv), _const_spec(gq), _const_spec(gk)],
        out_specs=[_res_spec(g) for _ in range(3) for g in range(3)],
        out_shape=[_res_shape(g, BF16) for _ in range(3) for g in range(3)],
        scratch_shapes=[pltpu.VMEM((NCH, TT, 128), F32)],
        compiler_params=pltpu.CompilerParams(dimension_semantics=("arbitrary",)),
    )(proj, proj, proj, pos, inv, gq, gk)


def _qk_bwd(proj, pos, inv, gq, gk, dqs, dks, dvs, dproj):
    const = lambda a: pl.BlockSpec(a.shape, functools.partial(lambda i, nd: (0,) * nd, nd=a.ndim))
    res = lambda g: pl.BlockSpec((DILATIONS[g], TT // DILATIONS[g], ATT_W), lambda i: (0, i, 0))
    steps = T // TT

    def body(t_ref, pos_ref, inv_ref, gq_ref, gk_ref, dq0, dq1, dq2, dk0, dk1, dk2, dv0, dv1, dv2, buf_ref,
             out_ref, dgq_ref, dgk_ref, sc, obuf, sem):
        del buf_ref
        i = pl.program_id(0)
        first = i == 0
        tile = obuf.at[i % 2]

        def store(step):
            return pltpu.make_async_copy(
                obuf.at[step % 2], out_ref.at[pl.ds(pl.multiple_of(step * TT, TT), TT), pl.ds(C_QA, 3 * ATT_QKV)],
                sem.at[step % 2])

        @pl.when(i >= 2)
        def _():
            store(i - 2).wait()

        def gather(drefs):
            for j in range(NCH):
                grp, sub = divmod(j * 128, ATT_W)
                _from_residues(drefs[grp], slice(sub, sub + 128), sc, j, DILATIONS[grp])

        def normed(g_ref, drefs, dg_ref, col0):
            c, sp, sm = _rot_tables(pos_ref, inv_ref)
            gather(drefs)
            dg = jnp.zeros((1, 128), F32)
            for j in range(NCH):
                cols = slice(col0 + j * 128, col0 + (j + 1) * 128)
                d_rot = sc[j]
                dn = d_rot * c + pltpu.roll(d_rot * sp, 120, 1) + pltpu.roll(d_rot * sm, 8, 1)
                t = t_ref[:, cols]
                r = _pair_norm(t)
                gain = g_ref[...]
                dn_t = dn * t
                tile[:, cols] = (r * (dn * gain - t * ((r * r) * _pair_mean(dn_t * gain)))).astype(BF16)
                dg = dg + jnp.sum(dn_t * r, axis=0, keepdims=True)
            dg = dg + pltpu.roll(dg, HD, 1)

            @pl.when(first)
            def _():
                dg_ref[...] = dg

            @pl.when(jnp.logical_not(first))
            def _():
                dg_ref[...] += dg

        normed(gq_ref, (dq0, dq1, dq2), dgq_ref, 0)
        normed(gk_ref, (dk0, dk1, dk2), dgk_ref, ATT_QKV)
        gather((dv0, dv1, dv2))
        for j in range(NCH):
            tile[:, 2 * ATT_QKV + j * 128:2 * ATT_QKV + (j + 1) * 128] = sc[j].astype(BF16)
        store(i).start()

        @pl.when(i == steps - 1)
        def _():
            store(i - 1).wait()
            store(i).wait()

    keep = pl.BlockSpec((1, 128), lambda i: (0, 0))
    hbm = pl.BlockSpec(memory_space=pl.ANY)
    return pl.pallas_call(
        body, name="qk_bwd", grid=(steps,),
        in_specs=[pl.BlockSpec((TT, 2 * ATT_QKV), lambda i: (i, C_QA // (2 * ATT_QKV))),
                  pl.BlockSpec((TT, 1), lambda i: (i, 0)), const(inv), const(gq), const(gk)]
        + [res(g) for _ in range(3) for g in range(3)] + [hbm],
        out_specs=[hbm, keep, keep],
        out_shape=[S(dproj.shape, dproj.dtype), S((1, 128), F32), S((1, 128), F32)],
        input_output_aliases={14: 0},
        scratch_shapes=[pltpu.VMEM((NCH, TT, 128), F32), pltpu.VMEM((2, TT, 3 * ATT_QKV), BF16),
                        pltpu.SemaphoreType.DMA((2,))],
        compiler_params=pltpu.CompilerParams(dimension_semantics=("arbitrary",)),
    )(proj, pos, inv, gq, gk, *dqs, *dks, *dvs, dproj)


def _split_heads(t):
    low = lax.broadcasted_iota(jnp.int32, (1, 128), 1) < HD
    zero = jnp.zeros_like(t)
    return jnp.concatenate([jnp.where(low, t, zero), jnp.where(low, zero, t)], axis=0)


def _join_heads(t2):
    low = lax.broadcasted_iota(jnp.int32, (1, 128), 1) < HD
    n = t2.shape[0] // 2
    return jnp.where(low, t2[:n], t2[n:])


def _band_mask4(has_before, has_own):
    row = lax.broadcasted_iota(jnp.int32, (BLK, 4 * BLK), 0)
    lane = lax.broadcasted_iota(jnp.int32, (BLK, 4 * BLK), 1)
    key = lane & (BLK - 1)
    own = lane >= 2 * BLK
    return (own & (key <= row) & has_own) | (jnp.logical_not(own) & (key >= row) & has_before)


def _band_mask_before(has_before):
    row = lax.broadcasted_iota(jnp.int32, (BLK, 2 * BLK), 0)
    key = lax.broadcasted_iota(jnp.int32, (BLK, 2 * BLK), 1) & (BLK - 1)
    return (key >= row) & has_before


def _per_head(width, col_a, col_b):
    lane = lax.broadcasted_iota(jnp.int32, (1, width), 1)
    return jnp.where((lane & BLK) == 0, col_a, col_b)


NQ = ATT_W // 128


def _att_fwd(q, k, v, grp, name):
    dil = DILATIONS[grp]
    nb = T // dil // BLK

    def body(q_ref, kp_ref, kc_ref, vp_ref, vc_ref, o_ref, lse_ref, s_sc, p_sc):
        mask = _band_mask4(pl.program_id(1) > 0, True)
        low = lax.broadcasted_iota(jnp.int32, (1, 128), 1) < HD
        halves = lambda ref, j, h: (ref[j, :, h * BLK:(h + 1) * BLK], ref[j, :, (h + 2) * BLK:(h + 3) * BLK])
        for j in range(NQ):
            cols = slice(j * 128, (j + 1) * 128)
            k4 = jnp.concatenate([_split_heads(kp_ref[:, cols]), _split_heads(kc_ref[:, cols])], axis=0)
            s_sc[j] = jnp.where(mask, _nt(q_ref[:, cols], k4), -jnp.inf)
        mxs = [[jnp.maximum(*(jnp.max(t, axis=-1, keepdims=True) for t in halves(s_sc, j, h))) for h in range(2)]
               for j in range(NQ)]
        dens = []
        for j in range(NQ):
            p = jnp.exp(s_sc[j] - _per_head(4 * BLK, *mxs[j]))
            p_sc[j] = p.astype(BF16)
            dens.append([jnp.sum(p[:, h * BLK:(h + 1) * BLK], axis=-1, keepdims=True)
                         + jnp.sum(p[:, (h + 2) * BLK:(h + 3) * BLK], axis=-1, keepdims=True) for h in range(2)])
        for j in range(NQ):
            cols = slice(j * 128, (j + 1) * 128)
            v4 = jnp.concatenate([_split_heads(vp_ref[:, cols]), _split_heads(vc_ref[:, cols])], axis=0)
            o_ref[:, cols] = _nn(p_sc[j], v4) / jnp.where(low, dens[j][0], dens[j][1])
            lse_ref[:, cols] = jnp.where(low, mxs[j][0] + jnp.log(dens[j][0]), mxs[j][1] + jnp.log(dens[j][1]))

    cur = pl.BlockSpec((None, BLK, ATT_W), lambda r, i: (r, i, 0))
    prev = pl.BlockSpec((None, BLK, ATT_W), lambda r, i: (r, jnp.maximum(i - 1, 0), 0))
    return pl.pallas_call(
        body, name=name, grid=(dil, nb),
        in_specs=[cur, prev, cur, prev, cur],
        out_specs=[cur, cur], out_shape=[_res_shape(grp, F32)] * 2,
        scratch_shapes=[pltpu.VMEM((NQ, BLK, 4 * BLK), F32), pltpu.VMEM((NQ, BLK, 4 * BLK), BF16)],
        compiler_params=pltpu.CompilerParams(dimension_semantics=("parallel", "arbitrary")),
    )(q, k, k, v, v)


def _att_bwd(q, k, v, datt, att, lse, grp, name):
    dil = DILATIONS[grp]
    nb = T // dil // BLK
    scale = HD ** -0.5

    def body(q0_ref, q1_ref, kp_ref, kc_ref, vp_ref, vc_ref, do0_ref, do1_ref, o0_ref, o1_ref, l0_ref, l1_ref,
             dq_ref, dk_ref, dv_ref, k4_sc, v4_sc, s0_sc, s1_sc, dp0_sc, dp1_sc, p_sc, ds_sc):
        i = pl.program_id(1)
        mask_mine = _band_mask4(i > 0, True)
        mask_next = _band_mask_before(i < nb - 1)
        low = lax.broadcasted_iota(jnp.int32, (1, 128), 1) < HD
        for j in range(NQ):
            cols = slice(j * 128, (j + 1) * 128)
            k4_sc[j, :2 * BLK] = _split_heads(kp_ref[:, cols])
            k4_sc[j, 2 * BLK:] = _split_heads(kc_ref[:, cols])
            v4_sc[j, :2 * BLK] = _split_heads(vp_ref[:, cols])
            v4_sc[j, 2 * BLK:] = _split_heads(vc_ref[:, cols])
        for j in range(NQ):
            cols = slice(j * 128, (j + 1) * 128)
            s0_sc[j] = _nt(q0_ref[:, cols], k4_sc[j])
            s1_sc[j] = _nt(q1_ref[:, cols], k4_sc[j, 2 * BLK:])
            dp0_sc[j] = _nt(do0_ref[:, cols].astype(BF16), v4_sc[j])
            dp1_sc[j] = _nt(do1_ref[:, cols].astype(BF16), v4_sc[j, 2 * BLK:])
        stats = []
        for j in range(NQ):
            cols = slice(j * 128, (j + 1) * 128)
            for do_ref, o_ref, l_ref in ((do0_ref, o0_ref, l0_ref), (do1_ref, o1_ref, l1_ref)):
                prod = do_ref[:, cols].astype(F32) * o_ref[:, cols].astype(F32)
                d_all = jnp.sum(prod, axis=-1, keepdims=True)
                d_low = jnp.sum(jnp.where(low, prod, 0.0), axis=-1, keepdims=True)
                lse_t = l_ref[:, cols]
                stats.append((d_low, d_all - d_low, lse_t[:, 0:1], lse_t[:, HD:HD + 1]))
        for j in range(NQ):
            (da, db, la, lb), (da1, db1, la1, lb1) = stats[2 * j], stats[2 * j + 1]
            p0 = jnp.where(mask_mine, jnp.exp(s0_sc[j] - _per_head(4 * BLK, la, lb)), 0.0)
            ds0 = p0 * (dp0_sc[j] - _per_head(4 * BLK, da, db))
            p1 = jnp.where(mask_next, jnp.exp(s1_sc[j] - _per_head(2 * BLK, la1, lb1)), 0.0)
            ds1 = p1 * (dp1_sc[j] - _per_head(2 * BLK, da1, db1))
            p_sc[j, :BLK] = p0.astype(BF16)
            ds_sc[j, :BLK] = ds0.astype(BF16)
            p_sc[j, BLK:, 2 * BLK:] = p1.astype(BF16)
            ds_sc[j, BLK:, 2 * BLK:] = ds1.astype(BF16)
        for j in range(NQ):
            cols = slice(j * 128, (j + 1) * 128)
            dq_ref[:, cols] = _nn(ds_sc[j, :BLK], k4_sc[j]) * scale
            qq = jnp.concatenate([q0_ref[:, cols], q1_ref[:, cols]], axis=0)
            dd = jnp.concatenate([do0_ref[:, cols], do1_ref[:, cols]], axis=0).astype(BF16)
            dk_ref[:, cols] = _join_heads(_tn(ds_sc[j, :, 2 * BLK:], qq))
            dv_ref[:, cols] = _join_heads(_tn(p_sc[j, :, 2 * BLK:], dd))

    def spec(shift):
        return pl.BlockSpec((None, BLK, ATT_W), lambda r, i: (r, jnp.clip(i + shift, 0, nb - 1), 0))

    here, after, before = spec(0), spec(1), spec(-1)
    vm = pltpu.VMEM
    return pl.pallas_call(
        body, name=name, grid=(dil, nb),
        in_specs=[here, after, before, here, before, here, here, after, here, after, here, after],
        out_specs=[here] * 3, out_shape=[_res_shape(grp, F32)] * 3,
        scratch_shapes=[vm((NQ, 4 * BLK, 128), BF16), vm((NQ, 4 * BLK, 128), BF16), vm((NQ, BLK, 4 * BLK), F32),
                        vm((NQ, BLK, 2 * BLK), F32), vm((NQ, BLK, 4 * BLK), F32), vm((NQ, BLK, 2 * BLK), F32),
                        vm((NQ, 2 * BLK, 4 * BLK), BF16), vm((NQ, 2 * BLK, 4 * BLK), BF16)],
        compiler_params=pltpu.CompilerParams(dimension_semantics=("parallel", "arbitrary")),
    )(q, q, k, k, v, v, datt, datt, att, att, lse, lse)


def _att_merge(os_, lses, proj):
    nq = ATT_W // 128

    def body(o0, o1, o2, l0, l1, l2, za_ref, att_ref, lse_ref, ain_ref, sc):
        for a, ref in enumerate((o0, o1, o2, l0, l1, l2)):
            for j in range(nq):
                _from_residues(ref, slice(j * 128, (j + 1) * 128), sc, a * nq + j, DILATIONS[a % 3])
        for j in range(nq):
            cols = slice(j * 128, (j + 1) * 128)
            oa, ob, oc = (sc[a * nq + j] for a in range(3))
            la, lb, lc = (sc[(3 + a) * nq + j] for a in range(3))
            m = jnp.maximum(jnp.maximum(la, lb), lc)
            wa, wb, wc = jnp.exp(la - m), jnp.exp(lb - m), jnp.exp(lc - m)
            tot = wa + wb + wc
            att = (wa * oa + wb * ob + wc * oc) / tot
            att_ref[:, cols] = att
            lse_ref[:, cols] = m + jnp.log(tot)
            za = za_ref[:, cols]
            ain_ref[:, cols] = (att * za * _sigmoid(za)).astype(BF16)

    return pl.pallas_call(
        body, name="att_merge", grid=(T // TT,),
        in_specs=[_res_spec(g) for _ in range(2) for g in range(3)] + [_tok_spec(ATT_W, C_ZA // ATT_W)],
        out_specs=[_tok_spec(ATT_W)] * 3,
        out_shape=[S((T, ATT_W), F32), S((T, ATT_W), F32), S((T, ATT_W), BF16)],
        scratch_shapes=[pltpu.VMEM((6 * nq, TT, 128), F32)],
        compiler_params=pltpu.CompilerParams(dimension_semantics=("arbitrary",)),
    )(*os_, *lses, proj)


def _att_gate_bwd(dain, att, lse, proj, dproj):
    nq = ATT_W // 128

    def body(d_ref, att_ref, lse_ref, za_ref, buf_ref, dza_ref, da0, da1, da2, at1, at2, ls1, ls2, sc):
        del buf_ref
        for j in range(nq):
            cols = slice(j * 128, (j + 1) * 128)
            za = za_ref[:, cols]
            sg = _sigmoid(za)
            d = d_ref[:, cols].astype(F32)
            att_ = att_ref[:, cols]
            dza_ref[:, cols] = (d * att_ * sg * (1.0 + za * (1.0 - sg))).astype(BF16)
            sc[j] = d * za * sg
            sc[nq + j] = att_
            sc[2 * nq + j] = lse_ref[:, cols]
        for j in range(nq):
            cols = slice(j * 128, (j + 1) * 128)
            for grp, dst in enumerate((da0, da1, da2)):
                _to_residues(sc, j, dst, DILATIONS[grp], cols)
            for grp, dst in ((1, at1), (2, at2)):
                _to_residues(sc, nq + j, dst, DILATIONS[grp], cols)
            for grp, dst in ((1, ls1), (2, ls2)):
                _to_residues(sc, 2 * nq + j, dst, DILATIONS[grp], cols)

    res = (0, 1, 2, 1, 2, 1, 2)
    return pl.pallas_call(
        body, name="att_gate_bwd", grid=(T // TT,),
        in_specs=[_tok_spec(ATT_W)] * 3 + [_tok_spec(ATT_W, C_ZA // ATT_W), pl.BlockSpec(memory_space=pl.ANY)],
        out_specs=[_tok_spec(ATT_W, C_ZA // ATT_W)] + [_res_spec(g) for g in res],
        out_shape=[S(dproj.shape, dproj.dtype)] + [_res_shape(g, BF16) for g in res[:5]]
        + [_res_shape(g, F32) for g in res[5:]],
        input_output_aliases={4: 0},
        scratch_shapes=[pltpu.VMEM((3 * nq, TT, 128), F32)],
        compiler_params=pltpu.CompilerParams(dimension_semantics=("arbitrary",)),
    )(dain, att, lse, proj, dproj)


def _split3(v):
    hi = v.astype(BF16)
    r1 = v - hi.astype(F32)
    mid = r1.astype(BF16)
    lo = (r1 - mid.astype(F32)).astype(BF16)
    return hi, mid, lo


def _chunk_scores(qt, kt, q_ref, k_ref, h):
    cols = slice(h * GDK, (h + 1) * GDK)
    own = jnp.sum(q_ref[:, cols] * (GDK ** -0.5) * k_ref[:, cols], axis=-1, keepdims=True)
    row = lax.broadcasted_iota(jnp.int32, (GLA_C, GLA_C), 0)
    col = lax.broadcasted_iota(jnp.int32, (GLA_C, GLA_C), 1)
    a = _nt(qt.astype(BF16), kt.astype(BF16))
    return jnp.where(col < row, a, jnp.where(col == row, own, 0.0))


def _tri_sum(v, upper):
    n = v.shape[0]
    row = lax.broadcasted_iota(jnp.int32, (n, n), 0)
    col = lax.broadcasted_iota(jnp.int32, (n, n), 1)
    tri = jnp.where(col >= row if upper else col <= row, 1.0, 0.0).astype(BF16)
    hi, mid, lo = _split3(v)
    return _nn(tri, hi) + _nn(tri, mid) + _nn(tri, lo)


def _gla_gates(glr_ref, w2_ref, b_ref):
    logit = _nn(glr_ref[...].astype(BF16), w2_ref[...]) + b_ref[...]
    lg = (jnp.minimum(logit, 0.0) - jnp.log(1.0 + jnp.exp(-jnp.abs(logit)))) * (1.0 / GLA_TAU)
    return logit, _tri_sum(lg, upper=False)


def _gla_head(cum, q_ref, k_ref, h):
    cols = slice(h * GDK, (h + 1) * GDK)
    b = cum[:, cols]
    last = b[GLA_C - 1:GLA_C, :]
    e_pos = jnp.exp(b)
    e_neg = jnp.exp(-b)
    e_end = jnp.exp(last - b)
    qt = q_ref[:, cols] * (GDK ** -0.5) * e_pos
    kt = k_ref[:, cols] * e_neg
    kh = k_ref[:, cols] * e_end
    return b, last, e_pos, e_neg, e_end, qt, kt, kh


def _causal(n):
    return lax.broadcasted_iota(jnp.int32, (n, n), 1) <= lax.broadcasted_iota(jnp.int32, (n, n), 0)


def _gla_fwd(proj, w2p, bg, gn):
    nc = T // GLA_C

    def body(q_ref, k_ref, v_ref, glr_ref, zg_ref, w2_ref, b_ref, gn_ref, o_ref, bin_ref, st_ref, state):
        @pl.when(pl.program_id(0) == 0)
        def _():
            state[...] = jnp.zeros_like(state)

        _, cum = _gla_gates(glr_ref, w2_ref, b_ref)
        for h in range(GH):
            _, last, _, _, _, qt, kt, kh = _gla_head(cum, q_ref, k_ref, h)
            vcols = slice(h * GDV, (h + 1) * GDV)
            st = state[h]
            st_ref[0, h] = st
            v = v_ref[:, vcols].astype(BF16)
            qb = qt.astype(BF16)
            a = _chunk_scores(qt, kt, q_ref, k_ref, h)
            o = _nt(qb, st.astype(BF16)) + _nn(a.astype(BF16), v)
            state[h] = st * jnp.exp(last) + _tn(v, kh.astype(BF16))
            o_ref[:, vcols] = o
            r = lax.rsqrt(jnp.mean(o * o, axis=-1, keepdims=True) + EPS)
            zg = zg_ref[:, vcols]
            bin_ref[:, vcols] = (o * r * gn_ref[...] * zg * _sigmoid(zg)).astype(BF16)

    row = lambda width, cblk: pl.BlockSpec((GLA_C, width), functools.partial(lambda i, c: (i, c), c=cblk))
    full = lambda a: pl.BlockSpec(a.shape, functools.partial(lambda i, nd: (0,) * nd, nd=a.ndim))
    return pl.pallas_call(
        body, name="gla_fwd", grid=(nc,),
        in_specs=[row(512, C_QG // 512), row(512, C_KG // 512), row(1024, C_VG // 1024), row(GLR_W, C_GLR // GLR_W),
                  row(1024, C_ZG // 1024), full(w2p), full(bg), full(gn)],
        out_specs=[pl.BlockSpec((GLA_C, GH * GDV), lambda i: (i, 0)), pl.BlockSpec((GLA_C, GH * GDV), lambda i: (i, 0)),
                   pl.BlockSpec((1, GH, GDV, GDK), lambda i: (i, 0, 0, 0))],
        out_shape=[S((T, GH * GDV), F32), S((T, GH * GDV), BF16), S((nc, GH, GDV, GDK), F32)],
        scratch_shapes=[pltpu.VMEM((GH, GDV, GDK), F32)],
        compiler_params=pltpu.CompilerParams(dimension_semantics=("arbitrary",)),
    )(proj, proj, proj, proj, proj, w2p, bg, gn)


def _gla_bwd(proj, w2p, bg, gn, o_gla, states, dbin, dproj):
    nc = T // GLA_C

    def body(q_ref, k_ref, v_ref, glr_ref, zg_ref, w2_ref, b_ref, gn_ref, o_ref, st_ref, dbin_ref, buf_ref,
             out_ref, dw2_ref, dbg_ref, dgn_ref, dstate, dlogit):
        del buf_ref
        dq_ref = out_ref.at[:, C_QG:C_KG]
        dk_ref = out_ref.at[:, C_KG:C_VG]
        dv_ref = out_ref.at[:, C_VG:C_ZG]
        dzg_ref = out_ref.at[:, C_ZG:C_GLR]
        dglr_ref = out_ref.at[:, C_GLR:C_GLR + GLR_W]
        first = pl.program_id(0) == 0

        @pl.when(first)
        def _():
            dstate[...] = jnp.zeros_like(dstate)

        logit, cum = _gla_gates(glr_ref, w2_ref, b_ref)
        is_last = lax.broadcasted_iota(jnp.int32, (GLA_C, 1), 0) == GLA_C - 1
        dgn = jnp.zeros((1, GDV), F32)
        for h in range(GH):
            _, last, e_pos, e_neg, e_end, qt, kt, kh = _gla_head(cum, q_ref, k_ref, h)
            cols = slice(h * GDK, (h + 1) * GDK)
            vcols = slice(h * GDV, (h + 1) * GDV)
            o = o_ref[:, vcols]
            r = lax.rsqrt(jnp.mean(o * o, axis=-1, keepdims=True) + EPS)
            zg = zg_ref[:, vcols]
            sg = _sigmoid(zg)
            db_ = dbin_ref[:, vcols].astype(F32)
            dlin = db_ * zg * sg
            dzg_ref[:, vcols] = (db_ * (o * r * gn_ref[...]) * sg * (1.0 + zg * (1.0 - sg))).astype(BF16)
            u = dlin * gn_ref[...]
            do = (r * u - o * (r * r * r) * jnp.mean(u * o, axis=-1, keepdims=True)).astype(BF16)
            dgn = dgn + jnp.sum(dlin * o * r, axis=0, keepdims=True)
            st = st_ref[0, h]
            dst = dstate[h]
            v = v_ref[:, vcols].astype(BF16)
            qb, kb, khb = qt.astype(BF16), kt.astype(BF16), kh.astype(BF16)
            dstb = dst.astype(BF16)
            causal = _causal(GLA_C)
            a = _chunk_scores(qt, kt, q_ref, k_ref, h).astype(BF16)
            da = jnp.where(causal, _nt(do, v), 0.0).astype(BF16)
            dqt = _nn(do, st.astype(BF16)) + _nn(da, kb)
            dkt = _tn(da, qb)
            dkh = _nn(v, dstb)
            dv_ref[:, vcols] = (_tn(a, do) + _nt(khb, dstb)).astype(BF16)
            lam = jnp.exp(last)
            dlam = jnp.sum(dst * st, axis=0, keepdims=True)
            dstate[h] = dst * lam + _tn(do, qb)
            dq_ref[:, cols] = (dqt * e_pos * (GDK ** -0.5)).astype(BF16)
            dk_ref[:, cols] = (dkt * e_neg + dkh * e_end).astype(BF16)
            dkh_kh = dkh * kh
            dcum = dqt * qt - dkt * kt - dkh_kh
            dlast = jnp.sum(dkh_kh, axis=0, keepdims=True) + dlam * lam
            dcum = jnp.where(is_last, dcum + dlast, dcum)
            dlg = _tri_sum(dcum, upper=True)
            dlogit[:, cols] = dlg * (1.0 / GLA_TAU) * (1.0 - _sigmoid(logit[:, cols]))

        dl = dlogit[...]
        dlb = dl.astype(BF16)
        dglr_ref[...] = _nt(dlb, w2_ref[...]).astype(BF16)
        dw2 = _tn(glr_ref[...].astype(BF16), dlb)
        dbg = jnp.sum(dl, axis=0, keepdims=True)

        @pl.when(first)
        def _():
            dw2_ref[...] = dw2
            dbg_ref[...] = dbg
            dgn_ref[...] = dgn

        @pl.when(jnp.logical_not(first))
        def _():
            dw2_ref[...] += dw2
            dbg_ref[...] += dbg
            dgn_ref[...] += dgn

    rev = lambda i: nc - 1 - i
    row = lambda width, cblk: pl.BlockSpec((GLA_C, width), functools.partial(lambda i, c: (rev(i), c), c=cblk))
    full = lambda a: pl.BlockSpec(a.shape, functools.partial(lambda i, nd: (0,) * nd, nd=a.ndim))
    keep = lambda shape: pl.BlockSpec(shape, functools.partial(lambda i, nd: (0,) * nd, nd=len(shape)))
    return pl.pallas_call(
        body, name="gla_bwd", grid=(nc,),
        in_specs=[row(512, C_QG // 512), row(512, C_KG // 512), row(1024, C_VG // 1024), row(GLR_W, C_GLR // GLR_W),
                  row(1024, C_ZG // 1024), full(w2p), full(bg), full(gn), row(GH * GDV, 0),
                  pl.BlockSpec((1, GH, GDV, GDK), lambda i: (rev(i), 0, 0, 0)), row(GH * GDV, 0),
                  pl.BlockSpec(memory_space=pl.ANY)],
        out_specs=[row(GLA_GROUP_W, 0), keep((GLR_W, 512)), keep((1, 512)), keep((1, GDV))],
        out_shape=[S(dproj.shape, dproj.dtype), S((GLR_W, 512), F32), S((1, 512), F32), S((1, GDV), F32)],
        input_output_aliases={11: 0},
        scratch_shapes=[pltpu.VMEM((GH, GDV, GDK), F32), pltpu.VMEM((GLA_C, GH * GDK), F32)],
        compiler_params=pltpu.CompilerParams(dimension_semantics=("arbitrary",)),
    )(proj, proj, proj, proj, proj, w2p, bg, gn, o_gla, states, dbin, dproj)


RT = 512


def _rowchain(body, name, ins, outs, scratch=()):
    in_specs, args = [], []
    for spec in ins:
        if spec[0] == "tok":
            _, arr, width, cblk = spec
            in_specs.append(pl.BlockSpec((RT, width), functools.partial(lambda i, c: (i, c), c=cblk)))
        else:
            arr = spec[1]
            in_specs.append(pl.BlockSpec(arr.shape, functools.partial(lambda i, nd: (0,) * nd, nd=arr.ndim)))
        args.append(arr)
    out_specs, out_shape = [], []
    for spec in outs:
        if spec[0] == "tok":
            _, shape, dtype, width, cblk = spec
            out_specs.append(pl.BlockSpec((RT, width), functools.partial(lambda i, c: (i, c), c=cblk)))
        else:
            _, shape, dtype = spec
            out_specs.append(pl.BlockSpec(shape, functools.partial(lambda i, nd: (0,) * nd, nd=len(shape))))
        out_shape.append(S(shape, dtype))
    return pl.pallas_call(
        body, name=name, grid=(T // RT,), in_specs=in_specs, out_specs=out_specs, out_shape=out_shape,
        scratch_shapes=list(scratch), compiler_params=pltpu.CompilerParams(dimension_semantics=("arbitrary",)),
    )(*args)


def _tok(arr, width=None, cblk=0):
    return ("tok", arr, arr.shape[1] if width is None else width, cblk)


def _tok_out(dtype, width=D):
    return ("tok", (T, width), dtype, width, 0)


def _branches_fwd(ain, bin_, proj, x, w_att, w_gla, w_out):
    def body(ain_ref, bin_ref, g_ref, x_ref, wa_ref, wg_ref, wo_ref, ya_ref, yb_ref, y_ref, x1_ref):
        ya = _nn(ain_ref[...], wa_ref[...]).astype(BF16)
        yb = _nn(bin_ref[...], wg_ref[...]).astype(BF16)
        ya_ref[...] = ya
        yb_ref[...] = yb
        y = (_sigmoid(g_ref[:, :D]) * ya.astype(F32) + _sigmoid(g_ref[:, D:]) * yb.astype(F32)).astype(BF16)
        y_ref[...] = y
        x1_ref[...] = x_ref[...] + _nn(y, wo_ref[...])

    return _rowchain(body, "branches_fwd",
                     [_tok(ain), _tok(bin_), _tok(proj, 2 * D, C_GA // (2 * D)), _tok(x), ("all", w_att),
                      ("all", w_gla), ("all", w_out)],
                     [_tok_out(BF16), _tok_out(BF16), _tok_out(BF16), _tok_out(F32)])


def _accumulate(ref, part, first):
    @pl.when(first)
    def _():
        ref[...] = part

    @pl.when(jnp.logical_not(first))
    def _():
        ref[...] += part


def _ple_loss(x1, p, target, g2, w_pg, w_ple):
    def body(x1_ref, p_ref, t_ref, g_ref, wpg_ref, wple_ref, n2_ref, loss_ref, dout_ref, du_ref, dwple_ref, acc):
        first = pl.program_id(0) == 0
        x1 = x1_ref[...]
        r = lax.rsqrt(jnp.mean(x1 * x1, axis=-1, keepdims=True) + EPS)
        n2 = (x1 * r * g_ref[...]).astype(BF16)
        n2_ref[...] = n2
        pg = _sigmoid(_nn(n2, wpg_ref[...]))
        pb = p_ref[...].astype(BF16)
        e_ = _nn(pb, wple_ref[...])
        diff = x1 + e_ * pg - t_ref[...]
        _accumulate(acc, jnp.sum(diff * diff, axis=0, keepdims=True), first)
        dout = diff * (1.0 / D)
        dout_ref[...] = dout
        du_ref[...] = (dout * e_ * pg * (1.0 - pg)).astype(BF16)
        _accumulate(dwple_ref, _tn(pb, (dout * pg).astype(BF16)), first)
        loss_ref[...] = jnp.zeros((1, 128), F32) + jnp.sum(acc[...], axis=-1, keepdims=True) * (0.5 / D)

    return _rowchain(body, "ple_loss", [_tok(x1), _tok(p), _tok(target), ("all", g2), ("all", w_pg), ("all", w_ple)],
                     [_tok_out(BF16), ("acc", (1, 128), F32), _tok_out(F32), _tok_out(BF16), ("acc", (PLE, D), F32)],
                     scratch=[pltpu.VMEM((1, D), F32)])


def _ple_bwd(du, n2, y, x1, dout, g2, w_pg, w_out):
    def body(du_ref, n2_ref, y_ref, x1_ref, dout_ref, g_ref, wpg_ref, wo_ref, dx_ref, dy_ref, dg_ref, dwpg_ref,
             dwo_ref):
        first = pl.program_id(0) == 0
        x1 = x1_ref[...]
        r = lax.rsqrt(jnp.mean(x1 * x1, axis=-1, keepdims=True) + EPS)
        du_ = du_ref[...]
        dn = _nt(du_, wpg_ref[...])
        u = dn * g_ref[...]
        dx = dout_ref[...] + r * u - x1 * (r * r * r) * jnp.mean(u * x1, axis=-1, keepdims=True)
        dxb = dx.astype(BF16)
        dx_ref[...] = dx
        dy_ref[...] = _nt(dxb, wo_ref[...]).astype(BF16)
        _accumulate(dg_ref, jnp.sum(dn * x1 * r, axis=0, keepdims=True), first)
        _accumulate(dwpg_ref, _tn(n2_ref[...], du_), first)
        _accumulate(dwo_ref, _tn(y_ref[...], dxb), first)

    return _rowchain(body, "ple_bwd",
                     [_tok(du), _tok(n2), _tok(y), _tok(x1), _tok(dout), ("all", g2), ("all", w_pg), ("all", w_out)],
                     [_tok_out(F32), _tok_out(BF16), ("acc", (1, D), F32), ("acc", (D, D), F32), ("acc", (D, D), F32)])


def _branches_bwd(dy, ya, yb, ain, bin_, proj, w_att, w_gla):
    def body(dy_ref, ya_ref, yb_ref, ain_ref, bin_ref, g_ref, wa_ref, wg_ref, dg_ref, dain_ref, dbin_ref,
             dwa_ref, dwg_ref):
        first = pl.program_id(0) == 0
        dy_ = dy_ref[...].astype(F32)
        sa, sb = _sigmoid(g_ref[:, :D]), _sigmoid(g_ref[:, D:])
        dg_ref[:, :D] = (dy_ * ya_ref[...].astype(F32) * sa * (1.0 - sa)).astype(BF16)
        dg_ref[:, D:] = (dy_ * yb_ref[...].astype(F32) * sb * (1.0 - sb)).astype(BF16)
        dya = (dy_ * sa).astype(BF16)
        dyb = (dy_ * sb).astype(BF16)
        dain_ref[...] = _nt(dya, wa_ref[...]).astype(BF16)
        dbin_ref[...] = _nt(dyb, wg_ref[...]).astype(BF16)
        _accumulate(dwa_ref, _tn(ain_ref[...], dya), first)
        _accumulate(dwg_ref, _tn(bin_ref[...], dyb), first)

    gates = C_GA // (2 * D)
    return _rowchain(body, "branches_bwd",
                     [_tok(dy), _tok(ya), _tok(yb), _tok(ain), _tok(bin_), _tok(proj, 2 * D, gates), ("all", w_att),
                      ("all", w_gla)],
                     [("tok", (T, NCOL), BF16, 2 * D, gates), _tok_out(BF16, ATT_W), _tok_out(BF16),
                      ("acc", (ATT_W, D), F32), ("acc", (D, D), F32)])


def _peer(k):
    x, y, c = lax.axis_index("x"), lax.axis_index("y"), lax.axis_index("c")
    return (x ^ ((k >> 2) & 1), y ^ ((k >> 1) & 1), c ^ (k & 1))


def _my_index():
    return 4 * lax.axis_index("x") + 2 * lax.axis_index("y") + lax.axis_index("c")


def _peer_index(k):
    px, py, pc = _peer(k)
    return 4 * px + 2 * py + pc


def _pairwise_plan(src_of, dst_of, landed_of, own_src, own_dst):
    def plan(ins, outs, send, recv, local):
        n = len(ins)

        def own():
            return [pltpu.make_async_copy(own_src(ins[a]), own_dst(outs[a]), local.at[a]) for a in range(n)]

        def remote(k, a, src, dst):
            return pltpu.make_async_remote_copy(src_ref=src, dst_ref=dst, send_sem=send.at[k - 1, a],
                                                recv_sem=recv.at[k - 1, a], device_id=_peer(k), device_id_type=MESH)

        def sent():
            return [remote(k, a, src_of(ins[a], k), dst_of(outs[a])) for k in range(1, NDEV) for a in range(n)]

        def start():
            for cp in own() + sent():
                cp.start()

        def finish():
            for k in range(1, NDEV):
                for a in range(n):
                    remote(k, a, own_src(ins[a]), landed_of(outs[a], k)).wait_recv()
            for cp in sent():
                cp.wait_send()
            for cp in own():
                cp.wait()

        return start, finish

    return plan


def _pairwise_sems(n):
    return [pltpu.SemaphoreType.DMA((NDEV - 1, n)), pltpu.SemaphoreType.DMA((NDEV - 1, n)),
            pltpu.SemaphoreType.DMA((n,))]


def _gather_side(arrs):
    plan = _pairwise_plan(src_of=lambda i, k: i, dst_of=lambda o: o.at[_my_index()],
                          landed_of=lambda o, k: o.at[_peer_index(k)],
                          own_src=lambda i: i, own_dst=lambda o: o.at[_my_index()])
    return dict(arrs=arrs, out_shape=[S((NDEV,) + a.shape, a.dtype) for a in arrs],
                scratch=_pairwise_sems(len(arrs)), plan=plan)


def _exchange_side(arrs):
    plan = _pairwise_plan(src_of=lambda i, k: i.at[_peer_index(k)], dst_of=lambda o: o.at[_my_index()],
                          landed_of=lambda o, k: o.at[_peer_index(k)],
                          own_src=lambda i: i.at[_my_index()], own_dst=lambda o: o.at[_my_index()])
    return dict(arrs=arrs, out_shape=[S(a.shape, a.dtype) for a in arrs], scratch=_pairwise_sems(len(arrs)), plan=plan)


def _comm_call(side, name):
    n = len(side["arrs"])

    def body(*refs):
        start, finish = side["plan"](refs[:n], refs[n:2 * n], *refs[2 * n:])
        start()
        finish()

    hbm = pl.BlockSpec(memory_space=pl.ANY)
    return pl.pallas_call(body, name=name, in_specs=[hbm] * n, out_specs=[hbm] * n, out_shape=side["out_shape"],
                          scratch_shapes=side["scratch"])(*side["arrs"])


def _all_gather_by_chip(arrs, name, beside):
    n = len(arrs)
    extra, extra_shapes, work = beside
    n_in, n_out = n + len(extra), n + len(extra_shapes)

    def body(*refs):
        ins, outs = refs[:n], refs[n_in:n_in + n]
        send, recv, local = refs[n_in + n_out:]
        x, y, c = lax.axis_index("x"), lax.axis_index("y"), lax.axis_index("c")
        me, sibling = (x, y, c), (x, y, 1 - c)
        chips = [(1 - x, y), (x, 1 - y), (1 - x, 1 - y)]

        def copy(k, a, block, to, src=None):
            px, py, pc = block
            slot = outs[a].at[4 * px + 2 * py + pc]
            return pltpu.make_async_remote_copy(
                src_ref=slot if src is None else src, dst_ref=slot, send_sem=send.at[k, a], recv_sem=recv.at[k, a],
                device_id=to, device_id_type=MESH)

        north = c == 1
        via = (jnp.where(north, 1 - x, x), jnp.where(north, y, 1 - y))
        onward = (jnp.where(north, x, 1 - x), jnp.where(north, 1 - y, y), c)
        mine = [pltpu.make_async_copy(ins[a], outs[a].at[4 * x + 2 * y + c], local.at[a]) for a in range(n)]
        first = []
        for a in range(n):
            first.append(copy(0, a, me, sibling, src=ins[a]))
            first += [copy(1 + j, a, me, (*chips[j], c), src=ins[a]) for j in range(2)]
        for cp in mine + first:
            cp.start()
        work(refs[n:n_in], refs[n_in + n:n_in + n_out])
        passed = []
        for j in range(2):
            for a in range(n):
                copy(1 + j, a, (*chips[j], c), me).wait_recv()
                passed.append(copy(4 + j, a, (*chips[j], c), sibling))
                passed[-1].start()
        for a in range(n):
            passed.append(copy(3, a, (*via, c), onward))
            passed[-1].start()
        for a in range(n):
            copy(3, a, (*chips[2], c), me).wait_recv()
            passed.append(copy(6, a, (*chips[2], c), sibling))
            passed[-1].start()
        for a in range(n):
            copy(0, a, sibling, me).wait_recv()
        for j, chip in enumerate(chips):
            for a in range(n):
                copy(4 + j, a, (*chip, 1 - c), me).wait_recv()
        for cp in first + passed:
            cp.wait_send()
        for cp in mine:
            cp.wait()

    hbm, vmem = pl.BlockSpec(memory_space=pl.ANY), pl.BlockSpec(memory_space=pltpu.VMEM)
    return pl.pallas_call(
        body, name=name, in_specs=[hbm] * n + [vmem] * len(extra), out_specs=[hbm] * n + [vmem] * len(extra_shapes),
        out_shape=[S((NDEV,) + a.shape, a.dtype) for a in arrs] + list(extra_shapes),
        scratch_shapes=[pltpu.SemaphoreType.DMA((NDEV - 1, n)), pltpu.SemaphoreType.DMA((NDEV - 1, n)),
                        pltpu.SemaphoreType.DMA((n,))],
    )(*arrs, *extra)


NCHIP = 4


def _sibling_sum(src, name, tc=512):
    _, rows, cols = src.shape
    assert cols % tc == 0

    def body(src_ref, got_ref, out_ref, a_buf, b_buf, o_buf, send, recv, local):
        x, y, c = lax.axis_index("x"), lax.axis_index("y"), lax.axis_index("c")
        copies = [pltpu.make_async_remote_copy(
            src_ref=src_ref.at[2 * q + (1 - c)], dst_ref=got_ref.at[q], send_sem=send.at[q], recv_sem=recv.at[q],
            device_id=(x, y, 1 - c), device_id_type=MESH) for q in range(NCHIP)]
        for cp in copies:
            cp.start()
        tiles = [(q, pl.ds(t * tc, tc)) for q in range(NCHIP) for t in range(cols // tc)]

        def loads(n):
            q, tile = tiles[n]
            return [pltpu.make_async_copy(src_ref.at[2 * q + c, :, tile], a_buf.at[n % 2], local.at[n % 2, 0]),
                    pltpu.make_async_copy(got_ref.at[q, :, tile], b_buf.at[n % 2], local.at[n % 2, 1])]

        def store(n):
            q, tile = tiles[n]
            return pltpu.make_async_copy(o_buf.at[n % 2], out_ref.at[q, :, tile], local.at[n % 2, 2])

        def fetch(n):
            if n == 0 or tiles[n][0] != tiles[n - 1][0]:
                copies[tiles[n][0]].wait_recv()
            for cp in loads(n):
                cp.start()

        fetch(0)
        for n in range(len(tiles)):
            if n + 1 < len(tiles):
                fetch(n + 1)
            for cp in loads(n):
                cp.wait()
            if n >= 2:
                store(n - 2).wait()
            o_buf[n % 2] = (a_buf[n % 2].astype(F32) + b_buf[n % 2].astype(F32)).astype(BF16)
            store(n).start()
        store(len(tiles) - 2).wait()
        store(len(tiles) - 1).wait()
        for cp in copies:
            cp.wait_send()

    hbm = pl.BlockSpec(memory_space=pl.ANY)
    block = S((NCHIP, rows, cols), BF16)
    return pl.pallas_call(
        body, name=name, in_specs=[hbm], out_specs=[hbm, hbm], out_shape=[block, block],
        scratch_shapes=[pltpu.VMEM((2, rows, tc), BF16)] * 3
        + [pltpu.SemaphoreType.DMA((NCHIP,)), pltpu.SemaphoreType.DMA((NCHIP,)), pltpu.SemaphoreType.DMA((2, 3))],
    )(src)[1]


def _chips_side(arrs):
    def plan(ins, outs, send, recv, local):
        n = len(ins)

        def places():
            x, y, c = lax.axis_index("x"), lax.axis_index("y"), lax.axis_index("c")
            return 2 * x + y, c, [(1 - x, y), (x, 1 - y), (1 - x, 1 - y)]

        def own():
            here, _, _ = places()
            return [pltpu.make_async_copy(ins[a].at[here], outs[a].at[here], local.at[a]) for a in range(n)]

        def remote(j, a, src_slot, dst_slot):
            _, c, chips = places()
            cx, cy = chips[j]
            return pltpu.make_async_remote_copy(
                src_ref=ins[a].at[src_slot], dst_ref=outs[a].at[dst_slot], send_sem=send.at[j, a],
                recv_sem=recv.at[j, a], device_id=(cx, cy, c), device_id_type=MESH)

        def sent():
            here, _, chips = places()
            return [remote(j, a, 2 * cx + cy, here) for j, (cx, cy) in enumerate(chips) for a in range(n)]

        def start():
            for cp in own() + sent():
                cp.start()

        def finish():
            here, _, chips = places()
            for j, (cx, cy) in enumerate(chips):
                for a in range(n):
                    remote(j, a, here, 2 * cx + cy).wait_recv()
            for cp in sent():
                cp.wait_send()
            for cp in own():
                cp.wait()

        return start, finish

    n = len(arrs)
    return dict(arrs=arrs, out_shape=[S(a.shape, a.dtype) for a in arrs],
                scratch=[pltpu.SemaphoreType.DMA((NCHIP - 1, n)), pltpu.SemaphoreType.DMA((NCHIP - 1, n)),
                         pltpu.SemaphoreType.DMA((n,))], plan=plan)


def _adamw_shards(parts, places):
    n_src = len(parts)

    def body(*refs):
        srcs, rest = refs[:n_src], refs[n_src:]
        for j, (src, rows, cols, _) in enumerate(places):
            w_ref, m_ref, v_ref = rest[3 * j:3 * j + 3]
            outs = rest[3 * len(places) + 4 * j:3 * len(places) + 4 * j + 4]
            p_ref = srcs[src]
            g = p_ref[0, rows, cols].astype(F32)
            for s in range(1, p_ref.shape[0]):
                g = g + p_ref[s, rows, cols].astype(F32)
            delta, m_new, v_new = _adam_math(g, w_ref[0], m_ref[0], v_ref[0])
            for ref, val in zip(outs, (g, delta, m_new, v_new)):
                ref[0] = val

    flat = [a for place in places for a in place[3]]
    return pl.pallas_call(
        body, name="adam_shards",
        out_shape=[S(place[3][0].shape, F32) for place in places for _ in range(4)],
    )(*parts, *flat)


def _adam_math(g, w, m, v):
    c1 = 1.0 - ADAM_B1 ** ADAM_STEP
    c2 = 1.0 - ADAM_B2 ** ADAM_STEP
    m_new = ADAM_B1 * m + (1.0 - ADAM_B1) * g
    v_new = ADAM_B2 * v + (1.0 - ADAM_B2) * (g * g)
    return -ADAM_LR * ((m_new / c1) / (jnp.sqrt(v_new / c2) + ADAM_EPS) + ADAM_WD * w), m_new, v_new


def _adamw_small(parts, params, loss_parts):
    n = len(params)

    def body(*refs):
        p_refs, rest = refs[:n], refs[n + 1:]
        total = refs[n][0]
        for s in range(1, NDEV):
            total = total + refs[n][s]
        refs[-1][...] = total
        for j in range(n):
            w_ref, m_ref, v_ref = rest[3 * j:3 * j + 3]
            g_ref, d_ref, mo_ref, vo_ref = rest[3 * n + 4 * j:3 * n + 4 * j + 4]
            width = w_ref.shape[1]
            g = p_refs[j][0]
            for s in range(1, NDEV):
                g = g + p_refs[j][s]
            g = g[:, :width]
            delta, m_new, v_new = _adam_math(g, w_ref[...], m_ref[...], v_ref[...])
            g_ref[...] = g
            d_ref[...] = delta
            mo_ref[...] = m_new
            vo_ref[...] = v_new

    flat = [a for group in params for a in group]
    return pl.pallas_call(
        body, name="adam_small",
        out_shape=[S(group[0].shape, F32) for group in params for _ in range(4)] + [S((1, 128), F32)],
    )(*parts, loss_parts, *flat)


def _adamw_rows(parts, w, m, v, name, tc=256):
    rows, _, cols = w.shape
    nparts = parts.shape[0]
    nsteps = cols // tc

    def body(p_ref, w_hbm, m_hbm, v_hbm, g_hbm, d_hbm, mo_hbm, vo_hbm, inbuf, outbuf, insem, outsem):
        i = pl.program_id(0)
        slot = i & 1

        def view(ref, step):
            return ref.at[:, 0, pl.ds(pl.multiple_of(step * tc, tc), tc)]

        def fetch(step, sl):
            return [pltpu.make_async_copy(view(src, step), inbuf.at[sl, k], insem.at[sl, k])
                    for k, src in enumerate((w_hbm, m_hbm, v_hbm))]

        def write(step, sl):
            return [pltpu.make_async_copy(outbuf.at[sl, k], view(dst, step), outsem.at[sl, k])
                    for k, dst in enumerate((g_hbm, d_hbm, mo_hbm, vo_hbm))]

        @pl.when(i == 0)
        def _():
            for cp in fetch(0, 0):
                cp.start()

        @pl.when(i + 1 < nsteps)
        def _():
            for cp in fetch(i + 1, 1 - slot):
                cp.start()

        for cp in fetch(i, slot):
            cp.wait()

        @pl.when(i >= 2)
        def _():
            for cp in write(i - 2, slot):
                cp.wait()

        g = p_ref[0].astype(F32)
        for s in range(1, nparts):
            g = g + p_ref[s].astype(F32)
        g = g[:rows]
        delta, m_new, v_new = _adam_math(g, inbuf[slot, 0], inbuf[slot, 1], inbuf[slot, 2])
        for k, val in enumerate((g, delta, m_new, v_new)):
            outbuf[slot, k] = val
        for cp in write(i, slot):
            cp.start()

        @pl.when(i == nsteps - 1)
        def _():
            for cp in write(i - 1, 1 - slot) + write(i, slot):
                cp.wait()

    hbm = pl.BlockSpec(memory_space=pl.ANY)
    assert nsteps >= 2
    return pl.pallas_call(
        body, name=name, grid=(nsteps,),
        in_specs=[pl.BlockSpec((nparts, parts.shape[1], tc), lambda i: (0, 0, i)), hbm, hbm, hbm],
        out_specs=[hbm] * 4, out_shape=[S((rows, 1, cols), F32)] * 4,
        scratch_shapes=[pltpu.VMEM((2, 3, rows, tc), F32), pltpu.VMEM((2, 4, rows, tc), F32),
                        pltpu.SemaphoreType.DMA((2, 3)), pltpu.SemaphoreType.DMA((2, 4))],
        compiler_params=pltpu.CompilerParams(dimension_semantics=("arbitrary",)),
    )(parts, w, m, v)


SLAB = 1296
REMAP_RUNS = 4
_PIECES = ((O_QA, O_ZA, C_QA), (O_ZA, O_QG, C_ZA), (O_QG, O_GLR, C_QG), (O_GLR, O_ZG, C_GLR), (O_ZG, O_GA, C_ZG),
           (O_GA, O_END, C_GA))


def _slab_row_of_aligned(a):
    for o0, o1, a0 in _PIECES:
        if a0 <= a < a0 + o1 - o0:
            c = o0 + a - a0
            return SLAB * (c // W_IN_SHARD) + c % W_IN_SHARD
    return -1


def _aligned_row_of_slab(r):
    d, l = divmod(r, SLAB)
    if l >= W_IN_SHARD:
        return -1
    c = d * W_IN_SHARD + l
    for o0, o1, a0 in _PIECES:
        if o0 <= c < o1:
            return a0 + c - o0
    raise AssertionError(c)


def _remap_table(row_of, n_out, block, n_src):
    win = block + 16
    table = []
    for b in range(n_out // block):
        runs = []
        for i in range(block):
            s = row_of(b * block + i)
            if s < 0:
                continue
            if runs and runs[-1][0] + runs[-1][2] == s and runs[-1][1] + runs[-1][2] == i:
                runs[-1][2] += 1
            else:
                runs.append([s, i, 1])
        assert len(runs) <= REMAP_RUNS, (b, runs)
        row = []
        for s, i, n in runs:
            w = min(s // 16 * 16, n_src - win)
            assert 0 <= s - w and s - w + n <= win
            row += [w, s - w, i, n]
        table.append(row + [0] * (4 * REMAP_RUNS - len(row)))
    return table


def _remap_rows(src, row_of, n_out, block, name):
    n_src, cols = src.shape
    nb, win = n_out // block, block + 16
    table = _remap_table(row_of, n_out, block, n_src)
    runs = [[tuple(row[4 * k:4 * k + 4]) for k in range(REMAP_RUNS) if row[4 * k + 3] > 0] for row in table]

    def body(src_hbm, out_hbm, wbuf, obuf, insem, outsem):
        def fetches(b):
            return [pltpu.make_async_copy(src_hbm.at[pl.ds(w, win)], wbuf.at[b % 2, k], insem.at[b % 2, k])
                    for k, (w, _, _, _) in enumerate(runs[b])]

        def store(b):
            return pltpu.make_async_copy(obuf.at[b % 2], out_hbm.at[pl.ds(b * block, block)], outsem.at[b % 2])

        for cp in fetches(0):
            cp.start()
        for b in range(nb):
            if b + 1 < nb:
                for cp in fetches(b + 1):
                    cp.start()
            for cp in fetches(b):
                cp.wait()
            if b >= 2:
                store(b - 2).wait()
            if sum(count for _, _, _, count in runs[b]) < block:
                obuf[b % 2] = jnp.zeros((block, cols), src.dtype)
            for k, (_, shift, first, count) in enumerate(runs[b]):
                obuf[b % 2, first:first + count, :] = wbuf[b % 2, k, shift:shift + count, :]
            store(b).start()
        store(nb - 2).wait()
        store(nb - 1).wait()

    hbm = pl.BlockSpec(memory_space=pl.ANY)
    return pl.pallas_call(
        body, name=name, in_specs=[hbm], out_specs=hbm, out_shape=S((n_out, cols), src.dtype),
        scratch_shapes=[pltpu.VMEM((2, REMAP_RUNS, win, cols), src.dtype), pltpu.VMEM((2, block, cols), src.dtype),
                        pltpu.SemaphoreType.DMA((2, REMAP_RUNS)), pltpu.SemaphoreType.DMA((2,))],
    )(src)


def _col_blocks(w, width):
    return w.reshape(w.shape[0], NDEV, width).transpose(1, 0, 2)


def _from_col_blocks(w):
    return w.transpose(1, 0, 2).reshape(w.shape[1], NDEV * w.shape[2])


def _local_step(x2, p2, pos, tgt, norm_g, qk_norm_q, qk_norm_k, gla_gate_b, gla_norm_g, ple_norm_g, w_al,
                weights=None, proj_side=None, unpack=None, dw_side_of=None, dh_side_of=None):
    half = ROT_DIM // 2
    inv8 = jnp.power(jnp.float32(ROPE_THETA), -jnp.arange(half, dtype=F32) * 2.0 / ROT_DIM)
    inv = jnp.tile(jnp.concatenate([inv8, inv8, jnp.zeros((HD - ROT_DIM,), F32)]), 2).reshape(1, 128)
    gq = jnp.tile(qk_norm_q, (1, 2))
    gk = jnp.tile(qk_norm_k, (1, 2))

    proj, h, got = _proj_rms(x2, norm_g, w_al, proj_side)
    if proj_side is not None:
        weights = unpack(got)
    w2p, w_att_f, w_gla_f, w_out_f, w_pg_f, w_ple_f = weights
    qkv = _qk_prep(proj, pos, inv, gq, gk)
    fwd = [_att_fwd(qkv[g], qkv[3 + g], qkv[6 + g], g, f"att_fwd{g}") for g in range(3)]
    att, lse, ain = _att_merge([f[0] for f in fwd], [f[1] for f in fwd], proj)
    o_gla, bin_, states = _gla_fwd(proj, w2p, gla_gate_b, gla_norm_g)
    ya, yb, y, x1 = _branches_fwd(ain, bin_, proj, x2, w_att_f, w_gla_f, w_out_f)
    n2, loss_v, dout, du, dw_ple = _ple_loss(x1, p2, tgt, ple_norm_g, w_pg_f, w_ple_f)

    dx1, dy, dg_ple, dw_pg, dw_out = _ple_bwd(du, n2, y, x1, dout, ple_norm_g, w_pg_f, w_out_f)
    dproj, dain, dbin, dw_att, dw_gla = _branches_bwd(dy, ya, yb, ain, bin_, proj, w_att_f, w_gla_f)
    dproj, da0, da1, da2, at1, at2, ls1, ls2 = _att_gate_bwd(dain, att, lse, proj, dproj)
    datts, atts, lses = (da0, da1, da2), (att[None], at1, at2), (lse[None], ls1, ls2)
    dproj, dw2, dbg, dgn = _gla_bwd(proj, w2p, gla_gate_b, gla_norm_g, o_gla, states, dbin, dproj)
    bwd = [_att_bwd(qkv[g], qkv[3 + g], qkv[6 + g], datts[g], atts[g], lses[g], g, f"att_bwd{g}") for g in range(3)]
    dproj, dgq, dgk = _qk_bwd(proj, pos, inv, gq, gk, [b[0] for b in bwd], [b[1] for b in bwd],
                              [b[2] for b in bwd], dproj)
    out = dict(loss=loss_v, dw2=dw2, dw_att=dw_att, dw_gla=dw_gla, dw_out=dw_out, dw_pg=dw_pg, dw_ple=dw_ple,
               dgq=dgq, dgk=dgk, dbg=dbg, dgn=dgn, dg_ple=dg_ple)
    if dw_side_of is None:
        dw_al = _mm(dproj, h, mode="tn", name="dw_in", tm=1536, tn=D, tk=T, out_dtype=BF16)
    else:
        dw_al, out["dw_side"] = _mm(dproj, h, mode="tn", name="dw_in", tm=1536, tn=D, tk=T, out_dtype=BF16,
                                    side=dw_side_of(out))
    grad_x, dg_norm, out["dh_side"] = _dh_rms(dproj, w_al, x2, norm_g, dx1,
                                              None if dh_side_of is None else dh_side_of(dw_al))
    out.update(grad_x=grad_x, dw_al=dw_al, dg_norm=dg_norm)
    return out


def kernel(x, p, positions, norm_g, w_in, qk_norm_q, qk_norm_k, gla_gate_w2, gla_gate_b, gla_norm_g, w_att_proj, w_gla_proj, w_out, ple_norm_g, w_ple_gate, w_ple, loss_target, m_norm_g, m_w_in, m_qk_norm_q, m_qk_norm_k, m_gla_gate_w2, m_gla_gate_b, m_gla_norm_g, m_w_att_proj, m_w_gla_proj, m_w_out, m_ple_norm_g, m_w_ple_gate, m_w_ple, v_norm_g, v_w_in, v_qk_norm_q, v_qk_norm_k, v_gla_gate_w2, v_gla_gate_b, v_gla_norm_g, v_w_att_proj, v_w_gla_proj, v_w_out, v_ple_norm_g, v_w_ple_gate, v_w_ple):
    x2, p2, tgt = x[0], p[0, 0], loss_target[0]
    pos = positions.astype(F32).reshape(T, 1)

    def pack_small(ins, outs):
        rows3_ref, cols3_ref = outs
        for j in range(3):
            rows3_ref[j] = ins[j][0].astype(BF16)
        cols3_ref[0:512, :] = ins[3][0].astype(BF16)
        cols3_ref[512:768, :] = ins[4][0].astype(BF16)
        cols3_ref[768:784, :] = jnp.zeros((GLR_N, 128), BF16)
        cols3_ref[768:784, 0:64] = ins[5][0].astype(BF16)

    mine = jnp.pad(w_in[0].T.astype(BF16), ((0, SLAB - W_IN_SHARD), (0, 0)))
    g_in, rows3, cols3 = _all_gather_by_chip(
        [mine], "gather_w_in", ([w_gla_proj, w_out, w_ple_gate, w_att_proj, w_ple, gla_gate_w2],
                                [S((3, 128, D), BF16), S((784, 128), BF16)], pack_small))
    w_al = _remap_rows(g_in.reshape(NDEV * SLAB, D), _slab_row_of_aligned, NCOL, 1536, "align_w_in")

    def unpack(got):
        g_rows, g_cols = got
        w2_f = _from_col_blocks(g_cols[:, 768:784, :64])
        return (jnp.pad(w2_f, ((0, GLR_W - GLR_N), (0, 0))), _from_col_blocks(g_cols[:, :512]),
                g_rows[:, 0].reshape(D, D), g_rows[:, 1].reshape(D, D), g_rows[:, 2].reshape(D, D),
                _from_col_blocks(g_cols[:, 512:768]))

    def dw_side_of(g):
        s_rows = jnp.concatenate([g[k].reshape(NDEV, 128, D) for k in ("dw_gla", "dw_out", "dw_pg")], axis=1)
        s_cols = jnp.concatenate([_col_blocks(g["dw_att"], 128), _col_blocks(g["dw_ple"], 128),
                                  jnp.pad(_col_blocks(g["dw2"][:GLR_N], 64), ((0, 0), (0, 0), (0, 64)))], axis=1)
        return _exchange_side([s_rows.astype(BF16), s_cols.astype(BF16)])

    def dh_side_of(dw_al):
        s_in = _remap_rows(dw_al, _aligned_row_of_slab, NDEV * SLAB, SLAB, "shard_dw_in").reshape(NDEV, SLAB, D)
        return _chips_side([_sibling_sum(s_in, "sibling_sum")])

    loc = _local_step(x2, p2, pos, tgt, norm_g, qk_norm_q, qk_norm_k, gla_gate_b, gla_norm_g, ple_norm_g, w_al,
                      proj_side=_gather_side([rows3, cols3]), unpack=unpack, dw_side_of=dw_side_of,
                      dh_side_of=dh_side_of)
    loss_v, grad_x = loc["loss"], loc["grad_x"]
    dg_norm, dgq, dgk, dbg, dgn, dg_ple = (loc[k] for k in ("dg_norm", "dgq", "dgk", "dbg", "dgn", "dg_ple"))
    r_rows, r_cols = loc["dw_side"]
    (r_in,) = loc["dh_side"]

    r_small = _comm_call(_gather_side([dg_norm, dgq, dgk, dbg, dgn, dg_ple, loss_v]), "gather_small")

    outs = {}

    rows_of = lambda a: jnp.transpose(a, (2, 0, 1))
    outs["w_in"] = [jnp.transpose(o, (1, 2, 0))[0] for o in
                    _adamw_rows(r_in, rows_of(w_in), rows_of(m_w_in), rows_of(v_w_in), "adam_w_in")]
    places = (("w_gla_proj", 0, slice(0, 128), slice(None), (w_gla_proj, m_w_gla_proj, v_w_gla_proj)),
              ("w_out", 0, slice(128, 256), slice(None), (w_out, m_w_out, v_w_out)),
              ("w_ple_gate", 0, slice(256, 384), slice(None), (w_ple_gate, m_w_ple_gate, v_w_ple_gate)),
              ("w_att_proj", 1, slice(0, 512), slice(None), (w_att_proj, m_w_att_proj, v_w_att_proj)),
              ("w_ple", 1, slice(512, 768), slice(None), (w_ple, m_w_ple, v_w_ple)),
              ("gla_gate_w2", 1, slice(768, 784), slice(0, 64), (gla_gate_w2, m_gla_gate_w2, v_gla_gate_w2)))
    res = _adamw_shards([r_rows, r_cols], [place[1:] for place in places])
    for j, place in enumerate(places):
        outs[place[0]] = [o[0] for o in res[4 * j:4 * j + 4]]
    small = ((norm_g, m_norm_g, v_norm_g), (qk_norm_q, m_qk_norm_q, v_qk_norm_q), (qk_norm_k, m_qk_norm_k, v_qk_norm_k),
             (gla_gate_b, m_gla_gate_b, v_gla_gate_b), (gla_norm_g, m_gla_norm_g, v_gla_norm_g),
             (ple_norm_g, m_ple_norm_g, v_ple_norm_g))
    sm = _adamw_small(r_small[:6], small, r_small[6])
    for j, nm in enumerate(("norm_g", "qk_norm_q", "qk_norm_k", "gla_gate_b", "gla_norm_g", "ple_norm_g")):
        outs[nm] = [o[0] for o in sm[4 * j:4 * j + 4]]

    loss = sm[-1][0, 0]
    order = ["norm_g", "w_in", "qk_norm_q", "qk_norm_k", "gla_gate_w2", "gla_gate_b", "gla_norm_g", "w_att_proj",
             "w_gla_proj", "w_out", "ple_norm_g", "w_ple_gate", "w_ple"]
    result = [loss, grad_x[None]]
    for i in range(4):
        result += [outs[nm][i][None] for nm in order]
    return tuple(result)
```

```python
import functools

import jax
import jax.numpy as jnp
from jax import lax
from jax.experimental import pallas as pl
from jax.experimental.pallas import tpu as pltpu

F32 = jnp.float32
BF16 = jnp.bfloat16
S = jax.ShapeDtypeStruct

T = 4096
D = 1024
NDEV = 8
HD = 64
ATT_W = 512
ATT_QKV = 1536
DILATIONS = (1, 4, 16)
BLK = 128
GH, GDK, GDV = 4, 128, 256
GLA_C = 128
PLE = 256
EPS = 1e-6
ROT_DIM = 16
ROPE_THETA = 500000.0
GLA_TAU = 16.0
W_IN_SHARD = 1282

C_QG, C_KG, C_VG, C_ZG, C_GLR, C_ZA, C_GA, C_GB, C_QA, C_KA, C_VA = (
    0, 512, 1024, 2048, 3072, 3584, 4096, 5120, 6144, 7680, 9216)
GLA_GROUP_W = 3584
GLR_W = 512
NCOL = 10752
GLR_N = 16
O_QA, O_ZA, O_QG, O_GLR, O_ZG, O_GA, O_END = 0, 4608, 5120, 7168, 7184, 8208, 10256

ADAM_LR, ADAM_B1, ADAM_B2, ADAM_EPS, ADAM_WD, ADAM_STEP = 0.001, 0.9, 0.999, 1e-08, 0.01, 10

MESH = pl.DeviceIdType.MESH


def _sigmoid(z):
    return 1.0 / (1.0 + jnp.exp(-z))


def _dot(a, b, dims):
    return lax.dot_general(a, b, (dims, ((), ())), preferred_element_type=F32)


def _nn(a, b):
    return _dot(a, b, ((1,), (0,)))


def _nt(a, b):
    return _dot(a, b, ((1,), (1,)))


def _tn(a, b):
    return _dot(a, b, ((0,), (0,)))


def _mm(a, b, *, mode, name, tm, tn, tk, out_dtype=F32, res=None, side=None):
    if mode == "nn":
        (m, k), n = a.shape, b.shape[1]
        a_spec = pl.BlockSpec((tm, tk), lambda i, j, l: (i, l))
        b_spec = pl.BlockSpec((tk, tn), lambda i, j, l: (l, j))
        dot = _nn
    elif mode == "nt":
        (m, k), n = a.shape, b.shape[0]
        a_spec = pl.BlockSpec((tm, tk), lambda i, j, l: (i, l))
        b_spec = pl.BlockSpec((tn, tk), lambda i, j, l: (j, l))
        dot = _nt
    else:
        (k, m), n = a.shape, b.shape[1]
        a_spec = pl.BlockSpec((tk, tm), lambda i, j, l: (l, i))
        b_spec = pl.BlockSpec((tk, tn), lambda i, j, l: (l, j))
        dot = _tn
    assert m % tm == 0 and n % tn == 0 and k % tk == 0, (name, m, n, k)
    grid = (m // tm, n // tn, k // tk)
    nk = grid[2]
    o_spec = pl.BlockSpec((tm, tn), lambda i, j, l: (i, j))
    in_specs = [a_spec, b_spec]
    args = [a, b]
    if res is not None:
        in_specs.append(o_spec)
        args.append(res)
    n_in = len(args)
    n_side = 0 if side is None else len(side["arrs"])
    hbm = pl.BlockSpec(memory_space=pl.ANY)

    def body(*refs):
        a_ref, b_ref = refs[0], refs[1]
        r_ref = refs[2] if res is not None else None
        o_ref = refs[n_in + n_side]
        scratch = refs[n_in + 2 * n_side + 1:]
        if side is not None:
            start, finish_side = side["plan"](refs[n_in:n_in + n_side], refs[n_in + n_side + 1:n_in + 2 * n_side + 1],
                                              *scratch[1 if nk > 1 else 0:])
            ids = [pl.program_id(d) for d in range(3)]

            @pl.when((ids[0] == 0) & (ids[1] == 0) & (ids[2] == 0))
            def _():
                start()

        part = dot(a_ref[...].astype(BF16), b_ref[...].astype(BF16))

        def finish(val):
            if r_ref is not None:
                val = val + r_ref[...]
            o_ref[...] = val.astype(out_dtype)

        if nk == 1:
            finish(part)
        else:
            acc = scratch[0]
            l = pl.program_id(2)

            @pl.when(l == 0)
            def _():
                acc[...] = part

            @pl.when(l > 0)
            def _():
                acc[...] += part

            @pl.when(l == nk - 1)
            def _():
                finish(acc[...])

        if side is not None:
            @pl.when((ids[0] == grid[0] - 1) & (ids[1] == grid[1] - 1) & (ids[2] == grid[2] - 1))
            def _():
                finish_side()

    sems = [] if side is None else side["scratch"]
    outs = pl.pallas_call(
        body, name=name, grid=grid,
        in_specs=in_specs + [hbm] * n_side, out_specs=[o_spec] + [hbm] * n_side,
        out_shape=[S((m, n), out_dtype)] + ([] if side is None else side["out_shape"]),
        scratch_shapes=([pltpu.VMEM((tm, tn), F32)] if nk > 1 else []) + sems,
        compiler_params=pltpu.CompilerParams(
            dimension_semantics=("arbitrary",) * 3 if side is not None else ("parallel", "parallel", "arbitrary")),
    )(*args, *([] if side is None else side["arrs"]))
    return outs[0] if side is None else (outs[0], outs[1:])


def _side_parts(side, refs, n_in, n_out):
    n_side = 0 if side is None else len(side["arrs"])
    scratch = refs[n_in + n_out + 2 * n_side:]
    if side is None:
        return (lambda: None), (lambda: None), scratch
    start, finish = side["plan"](refs[n_in:n_in + n_side], refs[n_in + n_side + n_out:n_in + n_out + 2 * n_side],
                                 *scratch[len(scratch) - len(side["scratch"]):])
    return start, finish, scratch


def _proj_rms(x, g, wt, side=None):
    tm, tn = 1024, 1536
    grid = (T // tm, NCOL // tn)
    n_side = 0 if side is None else len(side["arrs"])
    hbm = pl.BlockSpec(memory_space=pl.ANY)

    def body(*refs):
        x_ref, g_ref, w_ref = refs[:3]
        o_ref, h_ref = refs[3 + n_side], refs[4 + n_side]
        start, finish, _ = _side_parts(side, refs, 3, 2)
        i, j = pl.program_id(0), pl.program_id(1)

        @pl.when((i == 0) & (j == 0))
        def _():
            start()

        @pl.when(j == 0)
        def _():
            xf = x_ref[...]
            r = lax.rsqrt(jnp.mean(xf * xf, axis=-1, keepdims=True) + EPS)
            h_ref[...] = (xf * r * g_ref[...]).astype(BF16)

        o_ref[...] = _nt(h_ref[...], w_ref[...])

        @pl.when((i == grid[0] - 1) & (j == grid[1] - 1))
        def _():
            finish()

    outs = pl.pallas_call(
        body, name="proj", grid=grid,
        in_specs=[pl.BlockSpec((tm, D), lambda i, j: (i, 0)), pl.BlockSpec((1, D), lambda i, j: (0, 0)),
                  pl.BlockSpec((tn, D), lambda i, j: (j, 0))] + [hbm] * n_side,
        out_specs=[pl.BlockSpec((tm, tn), lambda i, j: (i, j)), pl.BlockSpec((tm, D), lambda i, j: (i, 0))] + [hbm] * n_side,
        out_shape=[S((T, NCOL), F32), S((T, D), BF16)] + ([] if side is None else side["out_shape"]),
        scratch_shapes=[] if side is None else side["scratch"],
        compiler_params=pltpu.CompilerParams(dimension_semantics=("arbitrary", "arbitrary")),
    )(x, g, wt, *([] if side is None else side["arrs"]))
    return outs[0], outs[1], outs[2:]


def _dh_rms(dproj, wt, x, g, skip, side=None):
    tm, tk = 1024, 2688
    grid = (T // tm, NCOL // tk)
    n_side = 0 if side is None else len(side["arrs"])
    hbm = pl.BlockSpec(memory_space=pl.ANY)

    def body(*refs):
        a_ref, w_ref, x_ref, g_ref, s_ref = refs[:5]
        dx_ref, dg_ref = refs[5 + n_side], refs[6 + n_side]
        start, finish, scratch = _side_parts(side, refs, 5, 2)
        acc = scratch[0]
        i, l = pl.program_id(0), pl.program_id(1)

        @pl.when((i == 0) & (l == 0))
        def _():
            start()

        part = _nn(a_ref[...], w_ref[...])

        @pl.when(l == 0)
        def _():
            acc[...] = part

        @pl.when(l > 0)
        def _():
            acc[...] += part

        @pl.when(l == grid[1] - 1)
        def _():
            xf = x_ref[...]
            r = lax.rsqrt(jnp.mean(xf * xf, axis=-1, keepdims=True) + EPS)
            dn = acc[...]
            u = dn * g_ref[...]
            dx_ref[...] = s_ref[...] + r * u - xf * (r * r * r) * jnp.mean(u * xf, axis=-1, keepdims=True)
            dg = jnp.sum(dn * xf * r, axis=0, keepdims=True)

            @pl.when(i == 0)
            def _():
                dg_ref[...] = dg

            @pl.when(i > 0)
            def _():
                dg_ref[...] += dg

        @pl.when((i == grid[0] - 1) & (l == grid[1] - 1))
        def _():
            finish()

    tok = pl.BlockSpec((tm, D), lambda i, l: (i, 0))
    outs = pl.pallas_call(
        body, name="dh", grid=grid,
        in_specs=[pl.BlockSpec((tm, tk), lambda i, l: (i, l)), pl.BlockSpec((tk, D), lambda i, l: (l, 0)), tok,
                  pl.BlockSpec((1, D), lambda i, l: (0, 0)), tok] + [hbm] * n_side,
        out_specs=[tok, pl.BlockSpec((1, D), lambda i, l: (0, 0))] + [hbm] * n_side,
        out_shape=[S((T, D), F32), S((1, D), F32)] + ([] if side is None else side["out_shape"]),
        scratch_shapes=[pltpu.VMEM((tm, D), F32)] + ([] if side is None else side["scratch"]),
        compiler_params=pltpu.CompilerParams(dimension_semantics=("arbitrary", "arbitrary")),
    )(dproj, wt, x, g, skip, *([] if side is None else side["arrs"]))
    return outs[0], outs[1], outs[2:]


def _rot_tables(pos_ref, inv_ref):
    lane = lax.broadcasted_iota(jnp.int32, (1, 128), 1) % HD
    ang = pos_ref[...] * inv_ref[...]
    cos, sin = jnp.cos(ang), jnp.sin(ang)
    c = jnp.where(lane < ROT_DIM, cos, 1.0)
    sp = jnp.where((lane >= ROT_DIM // 2) & (lane < ROT_DIM), sin, 0.0)
    sm = jnp.where(lane < ROT_DIM // 2, -sin, 0.0)
    return c, sp, sm


def _head_sums(v):
    same = (lax.broadcasted_iota(jnp.int32, (128, 128), 0) < HD) == (lax.broadcasted_iota(jnp.int32, (128, 128), 1) < HD)
    ones = jnp.where(same, 1.0, 0.0).astype(BF16)
    hi = v.astype(BF16)
    lo = (v - hi.astype(F32)).astype(BF16)
    return _nn(hi, ones) + _nn(lo, ones)


def _pair_norm(t):
    return lax.rsqrt(_head_sums(t * t) * (1.0 / HD) + EPS)


def _pair_mean(t):
    return _head_sums(t) * (1.0 / HD)


TT = 256
NCH = ATT_QKV // 128


def _res_shape(grp, dtype):
    return S((DILATIONS[grp], T // DILATIONS[grp], ATT_W), dtype)


def _res_spec(grp):
    dil = DILATIONS[grp]
    return pl.BlockSpec((dil, TT // dil, ATT_W), lambda i: (0, i, 0))


def _to_residues(sc, j, dst_ref, dil, cols):
    n = TT // dil
    for r in range(dil):
        rows = sc[j] if dil == 1 else sc.at[j][pl.ds(r, n, stride=dil), :]
        dst_ref[r, :, cols] = rows.astype(dst_ref.dtype)


def _from_residues(src_ref, cols, sc, j, dil):
    n = TT // dil
    for r in range(dil):
        if dil == 1:
            sc[j] = src_ref[r, :, cols]
        else:
            sc.at[j][pl.ds(r, n, stride=dil), :] = src_ref[r, :, cols]


def _tok_spec(width, cblk=0):
    return pl.BlockSpec((TT, width), functools.partial(lambda i, c: (i, c), c=cblk))


def _const_spec(arr_or_shape):
    shape = arr_or_shape if isinstance(arr_or_shape, tuple) else arr_or_shape.shape
    return pl.BlockSpec(shape, functools.partial(lambda i, nd: (0,) * nd, nd=len(shape)))


def _qk_prep(proj, pos, inv, gq, gk):
    def body(q_ref, k_ref, v_ref, pos_ref, inv_ref, gq_ref, gk_ref, *rest):
        outs, sc = rest[:9], rest[9]
        c, sp, sm = _rot_tables(pos_ref, inv_ref)
        for which, (src, g_ref) in enumerate(((q_ref, gq_ref), (k_ref, gk_ref), (v_ref, None))):
            if g_ref is not None:
                g = jnp.broadcast_to(g_ref[...] * ((HD ** -0.5) if which == 0 else 1.0), c.shape)
                cg, spg, smg = c * g, sp * pltpu.roll(g, 8, 1), sm * pltpu.roll(g, 120, 1)
            for j in range(NCH):
                t = src[:, j * 128:(j + 1) * 128]
                if g_ref is not None:
                    t = _pair_norm(t) * (t * cg + pltpu.roll(t, 8, 1) * spg + pltpu.roll(t, 120, 1) * smg)
                sc[j] = t
            for j in range(NCH):
                grp, sub = divmod(j * 128, ATT_W)
                _to_residues(sc, j, outs[which * 3 + grp], DILATIONS[grp], slice(sub, sub + 128))

    return pl.pallas_call(
        body, name="qk_prep", grid=(T // TT,),
        in_specs=[_tok_spec(ATT_QKV, C_QA // ATT_QKV), _tok_spec(ATT_QKV, C_KA // ATT_QKV),
                  _tok_spec(ATT_QKV, C_VA // ATT_QKV), _tok_spec(1), _const_spec(inv), _const_spec(gq), _const_spec(gk)],
        out_specs=[_res_spec(g) for _ in range(3) for g in range(3)],
        out_shape=[_res_shape(g, BF16) for _ in range(3) for g in range(3)],
        scratch_shapes=[pltpu.VMEM((NCH, TT, 128), F32)],
        compiler_params=pltpu.CompilerParams(dimension_semantics=("arbitrary",)),
    )(proj, proj, proj, pos, inv, gq, gk)


def _qk_bwd(proj, pos, inv, gq, gk, dqs, dks, dvs, dproj):
    const = lambda a: pl.BlockSpec(a.shape, functools.partial(lambda i, nd: (0,) * nd, nd=a.ndim))
    res = lambda g: pl.BlockSpec((DILATIONS[g], TT // DILATIONS[g], ATT_W), lambda i: (0, i, 0))
    steps = T // TT

    def body(t_ref, pos_ref, inv_ref, gq_ref, gk_ref, dq0, dq1, dq2, dk0, dk1, dk2, dv0, dv1, dv2, buf_ref,
             out_ref, dgq_ref, dgk_ref, sc, obuf, sem):
        del buf_ref
        i = pl.program_id(0)
        first = i == 0
        tile = obuf.at[i % 2]

        def store(step):
            return pltpu.make_async_copy(
                obuf.at[step % 2], out_ref.at[pl.ds(pl.multiple_of(step * TT, TT), TT), pl.ds(C_QA, 3 * ATT_QKV)],
                sem.at[step % 2])

        @pl.when(i >= 2)
        def _():
            store(i - 2).wait()

        def gather(drefs):
            for j in range(NCH):
                grp, sub = divmod(j * 128, ATT_W)
                _from_residues(drefs[grp], slice(sub, sub + 128), sc, j, DILATIONS[grp])

        def normed(g_ref, drefs, dg_ref, col0):
            c, sp, sm = _rot_tables(pos_ref, inv_ref)
            gather(drefs)
            dg = jnp.zeros((1, 128), F32)
            for j in range(NCH):
                cols = slice(col0 + j * 128, col0 + (j + 1) * 128)
                d_rot = sc[j]
                dn = d_rot * c + pltpu.roll(d_rot * sp, 120, 1) + pltpu.roll(d_rot * sm, 8, 1)
                t = t_ref[:, cols]
                r = _pair_norm(t)
                gain = g_ref[...]
                dn_t = dn * t
                tile[:, cols] = (r * (dn * gain - t * ((r * r) * _pair_mean(dn_t * gain)))).astype(BF16)
                dg = dg + jnp.sum(dn_t * r, axis=0, keepdims=True)
            dg = dg + pltpu.roll(dg, HD, 1)

            @pl.when(first)
            def _():
                dg_ref[...] = dg

            @pl.when(jnp.logical_not(first))
            def _():
                dg_ref[...] += dg

        normed(gq_ref, (dq0, dq1, dq2), dgq_ref, 0)
        normed(gk_ref, (dk0, dk1, dk2), dgk_ref, ATT_QKV)
        gather((dv0, dv1, dv2))
        for j in range(NCH):
            tile[:, 2 * ATT_QKV + j * 128:2 * ATT_QKV + (j + 1) * 128] = sc[j].astype(BF16)
        store(i).start()

        @pl.when(i == steps - 1)
        def _():
            store(i - 1).wait()
            store(i).wait()

    keep = pl.BlockSpec((1, 128), lambda i: (0, 0))
    hbm = pl.BlockSpec(memory_space=pl.ANY)
    return pl.pallas_call(
        body, name="qk_bwd", grid=(steps,),
        in_specs=[pl.BlockSpec((TT, 2 * ATT_QKV), lambda i: (i, C_QA // (2 * ATT_QKV))),
                  pl.BlockSpec((TT, 1), lambda i: (i, 0)), const(inv), const(gq), const(gk)]
        + [res(g) for _ in range(3) for g in range(3)] + [hbm],
        out_specs=[hbm, keep, keep],
        out_shape=[S(dproj.shape, dproj.dtype), S((1, 128), F32), S((1, 128), F32)],
        input_output_aliases={14: 0},
        scratch_shapes=[pltpu.VMEM((NCH, TT, 128), F32), pltpu.VMEM((2, TT, 3 * ATT_QKV), BF16),
                        pltpu.SemaphoreType.DMA((2,))],
        compiler_params=pltpu.CompilerParams(dimension_semantics=("arbitrary",)),
    )(proj, pos, inv, gq, gk, *dqs, *dks, *dvs, dproj)


def _split_heads(t):
    low = lax.broadcasted_iota(jnp.int32, (1, 128), 1) < HD
    zero = jnp.zeros_like(t)
    return jnp.concatenate([jnp.where(low, t, zero), jnp.where(low, zero, t)], axis=0)


def _join_heads(t2):
    low = lax.broadcasted_iota(jnp.int32, (1, 128), 1) < HD
    n = t2.shape[0] // 2
    return jnp.where(low, t2[:n], t2[n:])


def _band_mask4(has_before, has_own):
    row = lax.broadcasted_iota(jnp.int32, (BLK, 4 * BLK), 0)
    lane = lax.broadcasted_iota(jnp.int32, (BLK, 4 * BLK), 1)
    key = lane & (BLK - 1)
    own = lane >= 2 * BLK
    return (own & (key <= row) & has_own) | (jnp.logical_not(own) & (key >= row) & has_before)


def _band_mask_before(has_before):
    row = lax.broadcasted_iota(jnp.int32, (BLK, 2 * BLK), 0)
    key = lax.broadcasted_iota(jnp.int32, (BLK, 2 * BLK), 1) & (BLK - 1)
    return (key >= row) & has_before


def _per_head(width, col_a, col_b):
    lane = lax.broadcasted_iota(jnp.int32, (1, width), 1)
    return jnp.where((lane & BLK) == 0, col_a, col_b)


NQ = ATT_W // 128


def _att_fwd(q, k, v, grp, name):
    dil = DILATIONS[grp]
    nb = T // dil // BLK

    def body(q_ref, kp_ref, kc_ref, vp_ref, vc_ref, o_ref, lse_ref, s_sc, p_sc):
        mask = _band_mask4(pl.program_id(1) > 0, True)
        low = lax.broadcasted_iota(jnp.int32, (1, 128), 1) < HD
        halves = lambda ref, j, h: (ref[j, :, h * BLK:(h + 1) * BLK], ref[j, :, (h + 2) * BLK:(h + 3) * BLK])
        for j in range(NQ):
            cols = slice(j * 128, (j + 1) * 128)
            k4 = jnp.concatenate([_split_heads(kp_ref[:, cols]), _split_heads(kc_ref[:, cols])], axis=0)
            s_sc[j] = jnp.where(mask, _nt(q_ref[:, cols], k4), -jnp.inf)
        mxs = [[jnp.maximum(*(jnp.max(t, axis=-1, keepdims=True) for t in halves(s_sc, j, h))) for h in range(2)]
               for j in range(NQ)]
        dens = []
        for j in range(NQ):
            p = jnp.exp(s_sc[j] - _per_head(4 * BLK, *mxs[j]))
            p_sc[j] = p.astype(BF16)
            dens.append([jnp.sum(p[:, h * BLK:(h + 1) * BLK], axis=-1, keepdims=True)
                         + jnp.sum(p[:, (h + 2) * BLK:(h + 3) * BLK], axis=-1, keepdims=True) for h in range(2)])
        for j in range(NQ):
            cols = slice(j * 128, (j + 1) * 128)
            v4 = jnp.concatenate([_split_heads(vp_ref[:, cols]), _split_heads(vc_ref[:, cols])], axis=0)
            o_ref[:, cols] = _nn(p_sc[j], v4) / jnp.where(low, dens[j][0], dens[j][1])
            lse_ref[:, cols] = jnp.where(low, mxs[j][0] + jnp.log(dens[j][0]), mxs[j][1] + jnp.log(dens[j][1]))

    cur = pl.BlockSpec((None, BLK, ATT_W), lambda r, i: (r, i, 0))
    prev = pl.BlockSpec((None, BLK, ATT_W), lambda r, i: (r, jnp.maximum(i - 1, 0), 0))
    return pl.pallas_call(
        body, name=name, grid=(dil, nb),
        in_specs=[cur, prev, cur, prev, cur],
        out_specs=[cur, cur], out_shape=[_res_shape(grp, F32)] * 2,
        scratch_shapes=[pltpu.VMEM((NQ, BLK, 4 * BLK), F32), pltpu.VMEM((NQ, BLK, 4 * BLK), BF16)],
        compiler_params=pltpu.CompilerParams(dimension_semantics=("parallel", "arbitrary")),
    )(q, k, k, v, v)


def _att_bwd(q, k, v, datt, att, lse, grp, name):
    dil = DILATIONS[grp]
    nb = T // dil // BLK
    scale = HD ** -0.5

    def body(q0_ref, q1_ref, kp_ref, kc_ref, vp_ref, vc_ref, do0_ref, do1_ref, o0_ref, o1_ref, l0_ref, l1_ref,
             dq_ref, dk_ref, dv_ref, k4_sc, v4_sc, s0_sc, s1_sc, dp0_sc, dp1_sc, p_sc, ds_sc):
        i = pl.program_id(1)
        mask_mine = _band_mask4(i > 0, True)
        mask_next = _band_mask_before(i < nb - 1)
        low = lax.broadcasted_iota(jnp.int32, (1, 128), 1) < HD
        for j in range(NQ):
            cols = slice(j * 128, (j + 1) * 128)
            k4_sc[j, :2 * BLK] = _split_heads(kp_ref[:, cols])
            k4_sc[j, 2 * BLK:] = _split_heads(kc_ref[:, cols])
            v4_sc[j, :2 * BLK] = _split_heads(vp_ref[:, cols])
            v4_sc[j, 2 * BLK:] = _split_heads(vc_ref[:, cols])
        for j in range(NQ):
            cols = slice(j * 128, (j + 1) * 128)
            s0_sc[j] = _nt(q0_ref[:, cols], k4_sc[j])
            s1_sc[j] = _nt(q1_ref[:, cols], k4_sc[j, 2 * BLK:])
            dp0_sc[j] = _nt(do0_ref[:, cols].astype(BF16), v4_sc[j])
            dp1_sc[j] = _nt(do1_ref[:, cols].astype(BF16), v4_sc[j, 2 * BLK:])
        stats = []
        for j in range(NQ):
            cols = slice(j * 128, (j + 1) * 128)
            for do_ref, o_ref, l_ref in ((do0_ref, o0_ref, l0_ref), (do1_ref, o1_ref, l1_ref)):
                prod = do_ref[:, cols].astype(F32) * o_ref[:, cols].astype(F32)
                d_all = jnp.sum(prod, axis=-1, keepdims=True)
                d_low = jnp.sum(jnp.where(low, prod, 0.0), axis=-1, keepdims=True)
                lse_t = l_ref[:, cols]
                stats.append((d_low, d_all - d_low, lse_t[:, 0:1], lse_t[:, HD:HD + 1]))
        for j in range(NQ):
            (da, db, la, lb), (da1, db1, la1, lb1) = stats[2 * j], stats[2 * j + 1]
            p0 = jnp.where(mask_mine, jnp.exp(s0_sc[j] - _per_head(4 * BLK, la, lb)), 0.0)
            ds0 = p0 * (dp0_sc[j] - _per_head(4 * BLK, da, db))
            p1 = jnp.where(mask_next, jnp.exp(s1_sc[j] - _per_head(2 * BLK, la1, lb1)), 0.0)
            ds1 = p1 * (dp1_sc[j] - _per_head(2 * BLK, da1, db1))
            p_sc[j, :BLK] = p0.astype(BF16)
            ds_sc[j, :BLK] = ds0.astype(BF16)
            p_sc[j, BLK:, 2 * BLK:] = p1.astype(BF16)
            ds_sc[j, BLK:, 2 * BLK:] = ds1.astype(BF16)
        for j in range(NQ):
            cols = slice(j * 128, (j + 1) * 128)
            dq_ref[:, cols] = _nn(ds_sc[j, :BLK], k4_sc[j]) * scale
            qq = jnp.concatenate([q0_ref[:, cols], q1_ref[:, cols]], axis=0)
            dd = jnp.concatenate([do0_ref[:, cols], do1_ref[:, cols]], axis=0).astype(BF16)
            dk_ref[:, cols] = _join_heads(_tn(ds_sc[j, :, 2 * BLK:], qq))
            dv_ref[:, cols] = _join_heads(_tn(p_sc[j, :, 2 * BLK:], dd))

    def spec(shift):
        return pl.BlockSpec((None, BLK, ATT_W), lambda r, i: (r, jnp.clip(i + shift, 0, nb - 1), 0))

    here, after, before = spec(0), spec(1), spec(-1)
    vm = pltpu.VMEM
    return pl.pallas_call(
        body, name=name, grid=(dil, nb),
        in_specs=[here, after, before, here, before, here, here, after, here, after, here, after],
        out_specs=[here] * 3, out_shape=[_res_shape(grp, F32)] * 3,
        scratch_shapes=[vm((NQ, 4 * BLK, 128), BF16), vm((NQ, 4 * BLK, 128), BF16), vm((NQ, BLK, 4 * BLK), F32),
                        vm((NQ, BLK, 2 * BLK), F32), vm((NQ, BLK, 4 * BLK), F32), vm((NQ, BLK, 2 * BLK), F32),
                        vm((NQ, 2 * BLK, 4 * BLK), BF16), vm((NQ, 2 * BLK, 4 * BLK), BF16)],
        compiler_params=pltpu.CompilerParams(dimension_semantics=("parallel", "arbitrary")),
    )(q, q, k, k, v, v, datt, datt, att, att, lse, lse)


def _att_merge(os_, lses, proj):
    nq = ATT_W // 128

    def body(o0, o1, o2, l0, l1, l2, za_ref, att_ref, lse_ref, ain_ref, sc):
        for a, ref in enumerate((o0, o1, o2, l0, l1, l2)):
            for j in range(nq):
                _from_residues(ref, slice(j * 128, (j + 1) * 128), sc, a * nq + j, DILATIONS[a % 3])
        for j in range(nq):
            cols = slice(j * 128, (j + 1) * 128)
            oa, ob, oc = (sc[a * nq + j] for a in range(3))
            la, lb, lc = (sc[(3 + a) * nq + j] for a in range(3))
            m = jnp.maximum(jnp.maximum(la, lb), lc)
            wa, wb, wc = jnp.exp(la - m), jnp.exp(lb - m), jnp.exp(lc - m)
            tot = wa + wb + wc
            att = (wa * oa + wb * ob + wc * oc) / tot
            att_ref[:, cols] = att
            lse_ref[:, cols] = m + jnp.log(tot)
            za = za_ref[:, cols]
            ain_ref[:, cols] = (att * za * _sigmoid(za)).astype(BF16)

    return pl.pallas_call(
        body, name="att_merge", grid=(T // TT,),
        in_specs=[_res_spec(g) for _ in range(2) for g in range(3)] + [_tok_spec(ATT_W, C_ZA // ATT_W)],
        out_specs=[_tok_spec(ATT_W)] * 3,
        out_shape=[S((T, ATT_W), F32), S((T, ATT_W), F32), S((T, ATT_W), BF16)],
        scratch_shapes=[pltpu.VMEM((6 * nq, TT, 128), F32)],
        compiler_params=pltpu.CompilerParams(dimension_semantics=("arbitrary",)),
    )(*os_, *lses, proj)


def _att_gate_bwd(dain, att, lse, proj, dproj):
    nq = ATT_W // 128

    def body(d_ref, att_ref, lse_ref, za_ref, buf_ref, dza_ref, da0, da1, da2, at1, at2, ls1, ls2, sc):
        del buf_ref
        for j in range(nq):
            cols = slice(j * 128, (j + 1) * 128)
            za = za_ref[:, cols]
            sg = _sigmoid(za)
            d = d_ref[:, cols].astype(F32)
            att_ = att_ref[:, cols]
            dza_ref[:, cols] = (d * att_ * sg * (1.0 + za * (1.0 - sg))).astype(BF16)
            sc[j] = d * za * sg
            sc[nq + j] = att_
            sc[2 * nq + j] = lse_ref[:, cols]
        for j in range(nq):
            cols = slice(j * 128, (j + 1) * 128)
            for grp, dst in enumerate((da0, da1, da2)):
                _to_residues(sc, j, dst, DILATIONS[grp], cols)
            for grp, dst in ((1, at1), (2, at2)):
                _to_residues(sc, nq + j, dst, DILATIONS[grp], cols)
            for grp, dst in ((1, ls1), (2, ls2)):
                _to_residues(sc, 2 * nq + j, dst, DILATIONS[grp], cols)

    res = (0, 1, 2, 1, 2, 1, 2)
    return pl.pallas_call(
        body, name="att_gate_bwd", grid=(T // TT,),
        in_specs=[_tok_spec(ATT_W)] * 3 + [_tok_spec(ATT_W, C_ZA // ATT_W), pl.BlockSpec(memory_space=pl.ANY)],
        out_specs=[_tok_spec(ATT_W, C_ZA // ATT_W)] + [_res_spec(g) for g in res],
        out_shape=[S(dproj.shape, dproj.dtype)] + [_res_shape(g, BF16) for g in res[:5]]
        + [_res_shape(g, F32) for g in res[5:]],
        input_output_aliases={4: 0},
        scratch_shapes=[pltpu.VMEM((3 * nq, TT, 128), F32)],
        compiler_params=pltpu.CompilerParams(dimension_semantics=("arbitrary",)),
    )(dain, att, lse, proj, dproj)


def _split3(v):
    hi = v.astype(BF16)
    r1 = v - hi.astype(F32)
    mid = r1.astype(BF16)
    lo = (r1 - mid.astype(F32)).astype(BF16)
    return hi, mid, lo


def _chunk_scores(qt, kt, q_ref, k_ref, h):
    cols = slice(h * GDK, (h + 1) * GDK)
    own = jnp.sum(q_ref[:, cols] * (GDK ** -0.5) * k_ref[:, cols], axis=-1, keepdims=True)
    row = lax.broadcasted_iota(jnp.int32, (GLA_C, GLA_C), 0)
    col = lax.broadcasted_iota(jnp.int32, (GLA_C, GLA_C), 1)
    a = _nt(qt.astype(BF16), kt.astype(BF16))
    return jnp.where(col < row, a, jnp.where(col == row, own, 0.0))


def _tri_sum(v, upper):
    n = v.shape[0]
    row = lax.broadcasted_iota(jnp.int32, (n, n), 0)
    col = lax.broadcasted_iota(jnp.int32, (n, n), 1)
    tri = jnp.where(col >= row if upper else col <= row, 1.0, 0.0).astype(BF16)
    hi, mid, lo = _split3(v)
    return _nn(tri, hi) + _nn(tri, mid) + _nn(tri, lo)


def _gla_gates(glr_ref, w2_ref, b_ref):
    logit = _nn(glr_ref[...].astype(BF16), w2_ref[...]) + b_ref[...]
    lg = (jnp.minimum(logit, 0.0) - jnp.log(1.0 + jnp.exp(-jnp.abs(logit)))) * (1.0 / GLA_TAU)
    return logit, _tri_sum(lg, upper=False)


def _gla_head(cum, q_ref, k_ref, h):
    cols = slice(h * GDK, (h + 1) * GDK)
    b = cum[:, cols]
    last = b[GLA_C - 1:GLA_C, :]
    e_pos = jnp.exp(b)
    e_neg = jnp.exp(-b)
    e_end = jnp.exp(last - b)
    qt = q_ref[:, cols] * (GDK ** -0.5) * e_pos
    kt = k_ref[:, cols] * e_neg
    kh = k_ref[:, cols] * e_end
    return b, last, e_pos, e_neg, e_end, qt, kt, kh


def _causal(n):
    return lax.broadcasted_iota(jnp.int32, (n, n), 1) <= lax.broadcasted_iota(jnp.int32, (n, n), 0)


def _gla_fwd(proj, w2p, bg, gn):
    nc = T // GLA_C

    def body(q_ref, k_ref, v_ref, glr_ref, zg_ref, w2_ref, b_ref, gn_ref, o_ref, bin_ref, st_ref, state):
        @pl.when(pl.program_id(0) == 0)
        def _():
            state[...] = jnp.zeros_like(state)

        _, cum = _gla_gates(glr_ref, w2_ref, b_ref)
        for h in range(GH):
            _, last, _, _, _, qt, kt, kh = _gla_head(cum, q_ref, k_ref, h)
            vcols = slice(h * GDV, (h + 1) * GDV)
            st = state[h]
            st_ref[0, h] = st
            v = v_ref[:, vcols].astype(BF16)
            qb = qt.astype(BF16)
            a = _chunk_scores(qt, kt, q_ref, k_ref, h)
            o = _nt(qb, st.astype(BF16)) + _nn(a.astype(BF16), v)
            state[h] = st * jnp.exp(last) + _tn(v, kh.astype(BF16))
            o_ref[:, vcols] = o
            r = lax.rsqrt(jnp.mean(o * o, axis=-1, keepdims=True) + EPS)
            zg = zg_ref[:, vcols]
            bin_ref[:, vcols] = (o * r * gn_ref[...] * zg * _sigmoid(zg)).astype(BF16)

    row = lambda width, cblk: pl.BlockSpec((GLA_C, width), functools.partial(lambda i, c: (i, c), c=cblk))
    full = lambda a: pl.BlockSpec(a.shape, functools.partial(lambda i, nd: (0,) * nd, nd=a.ndim))
    return pl.pallas_call(
        body, name="gla_fwd", grid=(nc,),
        in_specs=[row(512, C_QG // 512), row(512, C_KG // 512), row(1024, C_VG // 1024), row(GLR_W, C_GLR // GLR_W),
                  row(1024, C_ZG // 1024), full(w2p), full(bg), full(gn)],
        out_specs=[pl.BlockSpec((GLA_C, GH * GDV), lambda i: (i, 0)), pl.BlockSpec((GLA_C, GH * GDV), lambda i: (i, 0)),
                   pl.BlockSpec((1, GH, GDV, GDK), lambda i: (i, 0, 0, 0))],
        out_shape=[S((T, GH * GDV), F32), S((T, GH * GDV), BF16), S((nc, GH, GDV, GDK), F32)],
        scratch_shapes=[pltpu.VMEM((GH, GDV, GDK), F32)],
        compiler_params=pltpu.CompilerParams(dimension_semantics=("arbitrary",)),
    )(proj, proj, proj, proj, proj, w2p, bg, gn)


def _gla_bwd(proj, w2p, bg, gn, o_gla, states, dbin, dproj):
    nc = T // GLA_C

    def body(q_ref, k_ref, v_ref, glr_ref, zg_ref, w2_ref, b_ref, gn_ref, o_ref, st_ref, dbin_ref, buf_ref,
             out_ref, dw2_ref, dbg_ref, dgn_ref, dstate, dlogit):
        del buf_ref
        dq_ref = out_ref.at[:, C_QG:C_KG]
        dk_ref = out_ref.at[:, C_KG:C_VG]
        dv_ref = out_ref.at[:, C_VG:C_ZG]
        dzg_ref = out_ref.at[:, C_ZG:C_GLR]
        dglr_ref = out_ref.at[:, C_GLR:C_GLR + GLR_W]
        first = pl.program_id(0) == 0

        @pl.when(first)
        def _():
            dstate[...] = jnp.zeros_like(dstate)

        logit, cum = _gla_gates(glr_ref, w2_ref, b_ref)
        is_last = lax.broadcasted_iota(jnp.int32, (GLA_C, 1), 0) == GLA_C - 1
        dgn = jnp.zeros((1, GDV), F32)
        for h in range(GH):
            _, last, e_pos, e_neg, e_end, qt, kt, kh = _gla_head(cum, q_ref, k_ref, h)
            cols = slice(h * GDK, (h + 1) * GDK)
            vcols = slice(h * GDV, (h + 1) * GDV)
            o = o_ref[:, vcols]
            r = lax.rsqrt(jnp.mean(o * o, axis=-1, keepdims=True) + EPS)
            zg = zg_ref[:, vcols]
            sg = _sigmoid(zg)
            db_ = dbin_ref[:, vcols].astype(F32)
            dlin = db_ * zg * sg
            dzg_ref[:, vcols] = (db_ * (o * r * gn_ref[...]) * sg * (1.0 + zg * (1.0 - sg))).astype(BF16)
            u = dlin * gn_ref[...]
            do = (r * u - o * (r * r * r) * jnp.mean(u * o, axis=-1, keepdims=True)).astype(BF16)
            dgn = dgn + jnp.sum(dlin * o * r, axis=0, keepdims=True)
            st = st_ref[0, h]
            dst = dstate[h]
            v = v_ref[:, vcols].astype(BF16)
            qb, kb, khb = qt.astype(BF16), kt.astype(BF16), kh.astype(BF16)
            dstb = dst.astype(BF16)
            causal = _causal(GLA_C)
            a = _chunk_scores(qt, kt, q_ref, k_ref, h).astype(BF16)
            da = jnp.where(causal, _nt(do, v), 0.0).astype(BF16)
            dqt = _nn(do, st.astype(BF16)) + _nn(da, kb)
            dkt = _tn(da, qb)
            dkh = _nn(v, dstb)
            dv_ref[:, vcols] = (_tn(a, do) + _nt(khb, dstb)).astype(BF16)
            lam = jnp.exp(last)
            dlam = jnp.sum(dst * st, axis=0, keepdims=True)
            dstate[h] = dst * lam + _tn(do, qb)
            dq_ref[:, cols] = (dqt * e_pos * (GDK ** -0.5)).astype(BF16)
            dk_ref[:, cols] = (dkt * e_neg + dkh * e_end).astype(BF16)
            dkh_kh = dkh * kh
            dcum = dqt * qt - dkt * kt - dkh_kh
            dlast = jnp.sum(dkh_kh, axis=0, keepdims=True) + dlam * lam
            dcum = jnp.where(is_last, dcum + dlast, dcum)
            dlg = _tri_sum(dcum, upper=True)
            dlogit[:, cols] = dlg * (1.0 / GLA_TAU) * (1.0 - _sigmoid(logit[:, cols]))

        dl = dlogit[...]
        dlb = dl.astype(BF16)
        dglr_ref[...] = _nt(dlb, w2_ref[...]).astype(BF16)
        dw2 = _tn(glr_ref[...].astype(BF16), dlb)
        dbg = jnp.sum(dl, axis=0, keepdims=True)

        @pl.when(first)
        def _():
            dw2_ref[...] = dw2
            dbg_ref[...] = dbg
            dgn_ref[...] = dgn

        @pl.when(jnp.logical_not(first))
        def _():
            dw2_ref[...] += dw2
            dbg_ref[...] += dbg
            dgn_ref[...] += dgn

    rev = lambda i: nc - 1 - i
    row = lambda width, cblk: pl.BlockSpec((GLA_C, width), functools.partial(lambda i, c: (rev(i), c), c=cblk))
    full = lambda a: pl.BlockSpec(a.shape, functools.partial(lambda i, nd: (0,) * nd, nd=a.ndim))
    keep = lambda shape: pl.BlockSpec(shape, functools.partial(lambda i, nd: (0,) * nd, nd=len(shape)))
    return pl.pallas_call(
        body, name="gla_bwd", grid=(nc,),
        in_specs=[row(512, C_QG // 512), row(512, C_KG // 512), row(1024, C_VG // 1024), row(GLR_W, C_GLR // GLR_W),
                  row(1024, C_ZG // 1024), full(w2p), full(bg), full(gn), row(GH * GDV, 0),
                  pl.BlockSpec((1, GH, GDV, GDK), lambda i: (rev(i), 0, 0, 0)), row(GH * GDV, 0),
                  pl.BlockSpec(memory_space=pl.ANY)],
        out_specs=[row(GLA_GROUP_W, 0), keep((GLR_W, 512)), keep((1, 512)), keep((1, GDV))],
        out_shape=[S(dproj.shape, dproj.dtype), S((GLR_W, 512), F32), S((1, 512), F32), S((1, GDV), F32)],
        input_output_aliases={11: 0},
        scratch_shapes=[pltpu.VMEM((GH, GDV, GDK), F32), pltpu.VMEM((GLA_C, GH * GDK), F32)],
        compiler_params=pltpu.CompilerParams(dimension_semantics=("arbitrary",)),
    )(proj, proj, proj, proj, proj, w2p, bg, gn, o_gla, states, dbin, dproj)


RT = 512


def _rowchain(body, name, ins, outs, scratch=()):
    in_specs, args = [], []
    for spec in ins:
        if spec[0] == "tok":
            _, arr, width, cblk = spec
            in_specs.append(pl.BlockSpec((RT, width), functools.partial(lambda i, c: (i, c), c=cblk)))
        else:
            arr = spec[1]
            in_specs.append(pl.BlockSpec(arr.shape, functools.partial(lambda i, nd: (0,) * nd, nd=arr.ndim)))
        args.append(arr)
    out_specs, out_shape = [], []
    for spec in outs:
        if spec[0] == "tok":
            _, shape, dtype, width, cblk = spec
            out_specs.append(pl.BlockSpec((RT, width), functools.partial(lambda i, c: (i, c), c=cblk)))
        else:
            _, shape, dtype = spec
            out_specs.append(pl.BlockSpec(shape, functools.partial(lambda i, nd: (0,) * nd, nd=len(shape))))
        out_shape.append(S(shape, dtype))
    return pl.pallas_call(
        body, name=name, grid=(T // RT,), in_specs=in_specs, out_specs=out_specs, out_shape=out_shape,
        scratch_shapes=list(scratch), compiler_params=pltpu.CompilerParams(dimension_semantics=("arbitrary",)),
    )(*args)


def _tok(arr, width=None, cblk=0):
    return ("tok", arr, arr.shape[1] if width is None else width, cblk)


def _tok_out(dtype, width=D):
    return ("tok", (T, width), dtype, width, 0)


def _branches_fwd(ain, bin_, proj, x, w_att, w_gla, w_out):
    def body(ain_ref, bin_ref, g_ref, x_ref, wa_ref, wg_ref, wo_ref, ya_ref, yb_ref, y_ref, x1_ref):
        ya = _nn(ain_ref[...], wa_ref[...]).astype(BF16)
        yb = _nn(bin_ref[...], wg_ref[...]).astype(BF16)
        ya_ref[...] = ya
        yb_ref[...] = yb
        y = (_sigmoid(g_ref[:, :D]) * ya.astype(F32) + _sigmoid(g_ref[:, D:]) * yb.astype(F32)).astype(BF16)
        y_ref[...] = y
        x1_ref[...] = x_ref[...] + _nn(y, wo_ref[...])

    return _rowchain(body, "branches_fwd",
                     [_tok(ain), _tok(bin_), _tok(proj, 2 * D, C_GA // (2 * D)), _tok(x), ("all", w_att),
                      ("all", w_gla), ("all", w_out)],
                     [_tok_out(BF16), _tok_out(BF16), _tok_out(BF16), _tok_out(F32)])


def _accumulate(ref, part, first):
    @pl.when(first)
    def _():
        ref[...] = part

    @pl.when(jnp.logical_not(first))
    def _():
        ref[...] += part


def _ple_loss(x1, p, target, g2, w_pg, w_ple):
    def body(x1_ref, p_ref, t_ref, g_ref, wpg_ref, wple_ref, n2_ref, loss_ref, dout_ref, du_ref, dwple_ref, acc):
        first = pl.program_id(0) == 0
        x1 = x1_ref[...]
        r = lax.rsqrt(jnp.mean(x1 * x1, axis=-1, keepdims=True) + EPS)
        n2 = (x1 * r * g_ref[...]).astype(BF16)
        n2_ref[...] = n2
        pg = _sigmoid(_nn(n2, wpg_ref[...]))
        pb = p_ref[...].astype(BF16)
        e_ = _nn(pb, wple_ref[...])
        diff = x1 + e_ * pg - t_ref[...]
        _accumulate(acc, jnp.sum(diff * diff, axis=0, keepdims=True), first)
        dout = diff * (1.0 / D)
        dout_ref[...] = dout
        du_ref[...] = (dout * e_ * pg * (1.0 - pg)).astype(BF16)
        _accumulate(dwple_ref, _tn(pb, (dout * pg).astype(BF16)), first)
        loss_ref[...] = jnp.zeros((1, 128), F32) + jnp.sum(acc[...], axis=-1, keepdims=True) * (0.5 / D)

    return _rowchain(body, "ple_loss", [_tok(x1), _tok(p), _tok(target), ("all", g2), ("all", w_pg), ("all", w_ple)],
                     [_tok_out(BF16), ("acc", (1, 128), F32), _tok_out(F32), _tok_out(BF16), ("acc", (PLE, D), F32)],
                     scratch=[pltpu.VMEM((1, D), F32)])


def _ple_bwd(du, n2, y, x1, dout, g2, w_pg, w_out):
    def body(du_ref, n2_ref, y_ref, x1_ref, dout_ref, g_ref, wpg_ref, wo_ref, dx_ref, dy_ref, dg_ref, dwpg_ref,
             dwo_ref):
        first = pl.program_id(0) == 0
        x1 = x1_ref[...]
        r = lax.rsqrt(jnp.mean(x1 * x1, axis=-1, keepdims=True) + EPS)
        du_ = du_ref[...]
        dn = _nt(du_, wpg_ref[...])
        u = dn * g_ref[...]
        dx = dout_ref[...] + r * u - x1 * (r * r * r) * jnp.mean(u * x1, axis=-1, keepdims=True)
        dxb = dx.astype(BF16)
        dx_ref[...] = dx
        dy_ref[...] = _nt(dxb, wo_ref[...]).astype(BF16)
        _accumulate(dg_ref, jnp.sum(dn * x1 * r, axis=0, keepdims=True), first)
        _accumulate(dwpg_ref, _tn(n2_ref[...], du_), first)
        _accumulate(dwo_ref, _tn(y_ref[...], dxb), first)

    return _rowchain(body, "ple_bwd",
                     [_tok(du), _tok(n2), _tok(y), _tok(x1), _tok(dout), ("all", g2), ("all", w_pg), ("all", w_out)],
                     [_tok_out(F32), _tok_out(BF16), ("acc", (1, D), F32), ("acc", (D, D), F32), ("acc", (D, D), F32)])


def _branches_bwd(dy, ya, yb, ain, bin_, proj, w_att, w_gla):
    def body(dy_ref, ya_ref, yb_ref, ain_ref, bin_ref, g_ref, wa_ref, wg_ref, dg_ref, dain_ref, dbin_ref,
             dwa_ref, dwg_ref):
        first = pl.program_id(0) == 0
        dy_ = dy_ref[...].astype(F32)
        sa, sb = _sigmoid(g_ref[:, :D]), _sigmoid(g_ref[:, D:])
        dg_ref[:, :D] = (dy_ * ya_ref[...].astype(F32) * sa * (1.0 - sa)).astype(BF16)
        dg_ref[:, D:] = (dy_ * yb_ref[...].astype(F32) * sb * (1.0 - sb)).astype(BF16)
        dya = (dy_ * sa).astype(BF16)
        dyb = (dy_ * sb).astype(BF16)
        dain_ref[...] = _nt(dya, wa_ref[...]).astype(BF16)
        dbin_ref[...] = _nt(dyb, wg_ref[...]).astype(BF16)
        _accumulate(dwa_ref, _tn(ain_ref[...], dya), first)
        _accumulate(dwg_ref, _tn(bin_ref[...], dyb), first)

    gates = C_GA // (2 * D)
    return _rowchain(body, "branches_bwd",
                     [_tok(dy), _tok(ya), _tok(yb), _tok(ain), _tok(bin_), _tok(proj, 2 * D, gates), ("all", w_att),
                      ("all", w_gla)],
                     [("tok", (T, NCOL), BF16, 2 * D, gates), _tok_out(BF16, ATT_W), _tok_out(BF16),
                      ("acc", (ATT_W, D), F32), ("acc", (D, D), F32)])


def _peer(k):
    x, y, c = lax.axis_index("x"), lax.axis_index("y"), lax.axis_index("c")
    return (x ^ ((k >> 2) & 1), y ^ ((k >> 1) & 1), c ^ (k & 1))


def _my_index():
    return 4 * lax.axis_index("x") + 2 * lax.axis_index("y") + lax.axis_index("c")


def _peer_index(k):
    px, py, pc = _peer(k)
    return 4 * px + 2 * py + pc


def _pairwise_plan(src_of, dst_of, landed_of, own_src, own_dst):
    def plan(ins, outs, send, recv, local):
        n = len(ins)

        def own():
            return [pltpu.make_async_copy(own_src(ins[a]), own_dst(outs[a]), local.at[a]) for a in range(n)]

        def remote(k, a, src, dst):
            return pltpu.make_async_remote_copy(src_ref=src, dst_ref=dst, send_sem=send.at[k - 1, a],
                                                recv_sem=recv.at[k - 1, a], device_id=_peer(k), device_id_type=MESH)

        def sent():
            return [remote(k, a, src_of(ins[a], k), dst_of(outs[a])) for k in range(1, NDEV) for a in range(n)]

        def start():
            for cp in own() + sent():
                cp.start()

        def finish():
            for k in range(1, NDEV):
                for a in range(n):
                    remote(k, a, own_src(ins[a]), landed_of(outs[a], k)).wait_recv()
            for cp in sent():
                cp.wait_send()
            for cp in own():
                cp.wait()

        return start, finish

    return plan


def _pairwise_sems(n):
    return [pltpu.SemaphoreType.DMA((NDEV - 1, n)), pltpu.SemaphoreType.DMA((NDEV - 1, n)),
            pltpu.SemaphoreType.DMA((n,))]


def _gather_side(arrs):
    plan = _pairwise_plan(src_of=lambda i, k: i, dst_of=lambda o: o.at[_my_index()],
                          landed_of=lambda o, k: o.at[_peer_index(k)],
                          own_src=lambda i: i, own_dst=lambda o: o.at[_my_index()])
    return dict(arrs=arrs, out_shape=[S((NDEV,) + a.shape, a.dtype) for a in arrs],
                scratch=_pairwise_sems(len(arrs)), plan=plan)


def _exchange_side(arrs):
    plan = _pairwise_plan(src_of=lambda i, k: i.at[_peer_index(k)], dst_of=lambda o: o.at[_my_index()],
                          landed_of=lambda o, k: o.at[_peer_index(k)],
                          own_src=lambda i: i.at[_my_index()], own_dst=lambda o: o.at[_my_index()])
    return dict(arrs=arrs, out_shape=[S(a.shape, a.dtype) for a in arrs], scratch=_pairwise_sems(len(arrs)), plan=plan)


def _comm_call(side, name):
    n = len(side["arrs"])

    def body(*refs):
        start, finish = side["plan"](refs[:n], refs[n:2 * n], *refs[2 * n:])
        start()
        finish()

    hbm = pl.BlockSpec(memory_space=pl.ANY)
    return pl.pallas_call(body, name=name, in_specs=[hbm] * n, out_specs=[hbm] * n, out_shape=side["out_shape"],
                          scratch_shapes=side["scratch"])(*side["arrs"])


def _all_gather_by_chip(arrs, name, beside):
    n = len(arrs)
    extra, extra_shapes, work = beside
    n_in, n_out = n + len(extra), n + len(extra_shapes)

    def body(*refs):
        ins, outs = refs[:n], refs[n_in:n_in + n]
        send, recv, local = refs[n_in + n_out:]
        x, y, c = lax.axis_index("x"), lax.axis_index("y"), lax.axis_index("c")
        me, sibling = (x, y, c), (x, y, 1 - c)
        chips = [(1 - x, y), (x, 1 - y), (1 - x, 1 - y)]

        def copy(k, a, block, to, src=None):
            px, py, pc = block
            slot = outs[a].at[4 * px + 2 * py + pc]
            return pltpu.make_async_remote_copy(
                src_ref=slot if src is None else src, dst_ref=slot, send_sem=send.at[k, a], recv_sem=recv.at[k, a],
                device_id=to, device_id_type=MESH)

        north = c == 1
        via = (jnp.where(north, 1 - x, x), jnp.where(north, y, 1 - y))
        onward = (jnp.where(north, x, 1 - x), jnp.where(north, 1 - y, y), c)
        mine = [pltpu.make_async_copy(ins[a], outs[a].at[4 * x + 2 * y + c], local.at[a]) for a in range(n)]
        first = []
        for a in range(n):
            first.append(copy(0, a, me, sibling, src=ins[a]))
            first += [copy(1 + j, a, me, (*chips[j], c), src=ins[a]) for j in range(2)]
        for cp in mine + first:
            cp.start()
        work(refs[n:n_in], refs[n_in + n:n_in + n_out])
        passed = []
        for j in range(2):
            for a in range(n):
                copy(1 + j, a, (*chips[j], c), me).wait_recv()
                passed.append(copy(4 + j, a, (*chips[j], c), sibling))
                passed[-1].start()
        for a in range(n):
            passed.append(copy(3, a, (*via, c), onward))
            passed[-1].start()
        for a in range(n):
            copy(3, a, (*chips[2], c), me).wait_recv()
            passed.append(copy(6, a, (*chips[2], c), sibling))
            passed[-1].start()
        for a in range(n):
            copy(0, a, sibling, me).wait_recv()
        for j, chip in enumerate(chips):
            for a in range(n):
                copy(4 + j, a, (*chip, 1 - c), me).wait_recv()
        for cp in first + passed:
            cp.wait_send()
        for cp in mine:
            cp.wait()

    hbm, vmem = pl.BlockSpec(memory_space=pl.ANY), pl.BlockSpec(memory_space=pltpu.VMEM)
    return pl.pallas_call(
        body, name=name, in_specs=[hbm] * n + [vmem] * len(extra), out_specs=[hbm] * n + [vmem] * len(extra_shapes),
        out_shape=[S((NDEV,) + a.shape, a.dtype) for a in arrs] + list(extra_shapes),
        scratch_shapes=[pltpu.SemaphoreType.DMA((NDEV - 1, n)), pltpu.SemaphoreType.DMA((NDEV - 1, n)),
                        pltpu.SemaphoreType.DMA((n,))],
    )(*arrs, *extra)


NCHIP = 4


def _sibling_sum(src, name, tc=512):
    _, rows, cols = src.shape
    assert cols % tc == 0

    def body(src_ref, got_ref, out_ref, a_buf, b_buf, o_buf, send, recv, local):
        x, y, c = lax.axis_index("x"), lax.axis_index("y"), lax.axis_index("c")
        copies = [pltpu.make_async_remote_copy(
            src_ref=src_ref.at[2 * q + (1 - c)], dst_ref=got_ref.at[q], send_sem=send.at[q], recv_sem=recv.at[q],
            device_id=(x, y, 1 - c), device_id_type=MESH) for q in range(NCHIP)]
        for cp in copies:
            cp.start()
        tiles = [(q, pl.ds(t * tc, tc)) for q in range(NCHIP) for t in range(cols // tc)]

        def loads(n):
            q, tile = tiles[n]
            return [pltpu.make_async_copy(src_ref.at[2 * q + c, :, tile], a_buf.at[n % 2], local.at[n % 2, 0]),
                    pltpu.make_async_copy(got_ref.at[q, :, tile], b_buf.at[n % 2], local.at[n % 2, 1])]

        def store(n):
            q, tile = tiles[n]
            return pltpu.make_async_copy(o_buf.at[n % 2], out_ref.at[q, :, tile], local.at[n % 2, 2])

        def fetch(n):
            if n == 0 or tiles[n][0] != tiles[n - 1][0]:
                copies[tiles[n][0]].wait_recv()
            for cp in loads(n):
                cp.start()

        fetch(0)
        for n in range(len(tiles)):
            if n + 1 < len(tiles):
                fetch(n + 1)
            for cp in loads(n):
                cp.wait()
            if n >= 2:
                store(n - 2).wait()
            o_buf[n % 2] = (a_buf[n % 2].astype(F32) + b_buf[n % 2].astype(F32)).astype(BF16)
            store(n).start()
        store(len(tiles) - 2).wait()
        store(len(tiles) - 1).wait()
        for cp in copies:
            cp.wait_send()

    hbm = pl.BlockSpec(memory_space=pl.ANY)
    block = S((NCHIP, rows, cols), BF16)
    return pl.pallas_call(
        body, name=name, in_specs=[hbm], out_specs=[hbm, hbm], out_shape=[block, block],
        scratch_shapes=[pltpu.VMEM((2, rows, tc), BF16)] * 3
        + [pltpu.SemaphoreType.DMA((NCHIP,)), pltpu.SemaphoreType.DMA((NCHIP,)), pltpu.SemaphoreType.DMA((2, 3))],
    )(src)[1]


def _chips_side(arrs):
    def plan(ins, outs, send, recv, local):
        n = len(ins)

        def places():
            x, y, c = lax.axis_index("x"), lax.axis_index("y"), lax.axis_index("c")
            return 2 * x + y, c, [(1 - x, y), (x, 1 - y), (1 - x, 1 - y)]

        def own():
            here, _, _ = places()
            return [pltpu.make_async_copy(ins[a].at[here], outs[a].at[here], local.at[a]) for a in range(n)]

        def remote(j, a, src_slot, dst_slot):
            _, c, chips = places()
            cx, cy = chips[j]
            return pltpu.make_async_remote_copy(
                src_ref=ins[a].at[src_slot], dst_ref=outs[a].at[dst_slot], send_sem=send.at[j, a],
                recv_sem=recv.at[j, a], device_id=(cx, cy, c), device_id_type=MESH)

        def sent():
            here, _, chips = places()
            return [remote(j, a, 2 * cx + cy, here) for j, (cx, cy) in enumerate(chips) for a in range(n)]

        def start():
            for cp in own() + sent():
                cp.start()

        def finish():
            here, _, chips = places()
            for j, (cx, cy) in enumerate(chips):
                for a in range(n):
                    remote(j, a, here, 2 * cx + cy).wait_recv()
            for cp in sent():
                cp.wait_send()
            for cp in own():
                cp.wait()

        return start, finish

    n = len(arrs)
    return dict(arrs=arrs, out_shape=[S(a.shape, a.dtype) for a in arrs],
                scratch=[pltpu.SemaphoreType.DMA((NCHIP - 1, n)), pltpu.SemaphoreType.DMA((NCHIP - 1, n)),
                         pltpu.SemaphoreType.DMA((n,))], plan=plan)


def _adamw_shards(parts, places):
    n_src = len(parts)

    def body(*refs):
        srcs, rest = refs[:n_src], refs[n_src:]
        for j, (src, rows, cols, _) in enumerate(places):
            w_ref, m_ref, v_ref = rest[3 * j:3 * j + 3]
            outs = rest[3 * len(places) + 4 * j:3 * len(places) + 4 * j + 4]
            p_ref = srcs[src]
            g = p_ref[0, rows, cols].astype(F32)
            for s in range(1, p_ref.shape[0]):
                g = g + p_ref[s, rows, cols].astype(F32)
            delta, m_new, v_new = _adam_math(g, w_ref[0], m_ref[0], v_ref[0])
            for ref, val in zip(outs, (g, delta, m_new, v_new)):
                ref[0] = val

    flat = [a for place in places for a in place[3]]
    return pl.pallas_call(
        body, name="adam_shards",
        out_shape=[S(place[3][0].shape, F32) for place in places for _ in range(4)],
    )(*parts, *flat)


def _adam_math(g, w, m, v):
    c1 = 1.0 - ADAM_B1 ** ADAM_STEP
    c2 = 1.0 - ADAM_B2 ** ADAM_STEP
    m_new = ADAM_B1 * m + (1.0 - ADAM_B1) * g
    v_new = ADAM_B2 * v + (1.0 - ADAM_B2) * (g * g)
    return -ADAM_LR * ((m_new / c1) / (jnp.sqrt(v_new / c2) + ADAM_EPS) + ADAM_WD * w), m_new, v_new


def _adamw_small(parts, params, loss_parts):
    n = len(params)

    def body(*refs):
        p_refs, rest = refs[:n], refs[n + 1:]
        total = refs[n][0]
        for s in range(1, NDEV):
            total = total + refs[n][s]
        refs[-1][...] = total
        for j in range(n):
            w_ref, m_ref, v_ref = rest[3 * j:3 * j + 3]
            g_ref, d_ref, mo_ref, vo_ref = rest[3 * n + 4 * j:3 * n + 4 * j + 4]
            width = w_ref.shape[1]
            g = p_refs[j][0]
            for s in range(1, NDEV):
                g = g + p_refs[j][s]
            g = g[:, :width]
            delta, m_new, v_new = _adam_math(g, w_ref[...], m_ref[...], v_ref[...])
            g_ref[...] = g
            d_ref[...] = delta
            mo_ref[...] = m_new
            vo_ref[...] = v_new

    flat = [a for group in params for a in group]
    return pl.pallas_call(
        body, name="adam_small",
        out_shape=[S(group[0].shape, F32) for group in params for _ in range(4)] + [S((1, 128), F32)],
    )(*parts, loss_parts, *flat)


def _adamw_rows(parts, w, m, v, name, tc=256):
    rows, _, cols = w.shape
    nparts = parts.shape[0]
    nsteps = cols // tc

    def body(p_ref, w_hbm, m_hbm, v_hbm, g_hbm, d_hbm, mo_hbm, vo_hbm, inbuf, outbuf, insem, outsem):
        i = pl.program_id(0)
        slot = i & 1

        def view(ref, step):
            return ref.at[:, 0, pl.ds(pl.multiple_of(step * tc, tc), tc)]

        def fetch(step, sl):
            return [pltpu.make_async_copy(view(src, step), inbuf.at[sl, k], insem.at[sl, k])
                    for k, src in enumerate((w_hbm, m_hbm, v_hbm))]

        def write(step, sl):
            return [pltpu.make_async_copy(outbuf.at[sl, k], view(dst, step), outsem.at[sl, k])
                    for k, dst in enumerate((g_hbm, d_hbm, mo_hbm, vo_hbm))]

        @pl.when(i == 0)
        def _():
            for cp in fetch(0, 0):
                cp.start()

        @pl.when(i + 1 < nsteps)
        def _():
            for cp in fetch(i + 1, 1 - slot):
                cp.start()

        for cp in fetch(i, slot):
            cp.wait()

        @pl.when(i >= 2)
        def _():
            for cp in write(i - 2, slot):
                cp.wait()

        g = p_ref[0].astype(F32)
        for s in range(1, nparts):
            g = g + p_ref[s].astype(F32)
        g = g[:rows]
        delta, m_new, v_new = _adam_math(g, inbuf[slot, 0], inbuf[slot, 1], inbuf[slot, 2])
        for k, val in enumerate((g, delta, m_new, v_new)):
            outbuf[slot, k] = val
        for cp in write(i, slot):
            cp.start()

        @pl.when(i == nsteps - 1)
        def _():
            for cp in write(i - 1, 1 - slot) + write(i, slot):
                cp.wait()

    hbm = pl.BlockSpec(memory_space=pl.ANY)
    assert nsteps >= 2
    return pl.pallas_call(
        body, name=name, grid=(nsteps,),
        in_specs=[pl.BlockSpec((nparts, parts.shape[1], tc), lambda i: (0, 0, i)), hbm, hbm, hbm],
        out_specs=[hbm] * 4, out_shape=[S((rows, 1, cols), F32)] * 4,
        scratch_shapes=[pltpu.VMEM((2, 3, rows, tc), F32), pltpu.VMEM((2, 4, rows, tc), F32),
                        pltpu.SemaphoreType.DMA((2, 3)), pltpu.SemaphoreType.DMA((2, 4))],
        compiler_params=pltpu.CompilerParams(dimension_semantics=("arbitrary",)),
    )(parts, w, m, v)


SLAB = 1296
REMAP_RUNS = 4
_PIECES = ((O_QA, O_ZA, C_QA), (O_ZA, O_QG, C_ZA), (O_QG, O_GLR, C_QG), (O_GLR, O_ZG, C_GLR), (O_ZG, O_GA, C_ZG),
           (O_GA, O_END, C_GA))


def _slab_row_of_aligned(a):
    for o0, o1, a0 in _PIECES:
        if a0 <= a < a0 + o1 - o0:
            c = o0 + a - a0
            return SLAB * (c // W_IN_SHARD) + c % W_IN_SHARD
    return -1


def _aligned_row_of_slab(r):
    d, l = divmod(r, SLAB)
    if l >= W_IN_SHARD:
        return -1
    c = d * W_IN_SHARD + l
    for o0, o1, a0 in _PIECES:
        if o0 <= c < o1:
            return a0 + c - o0
    raise AssertionError(c)


def _remap_table(row_of, n_out, block, n_src):
    win = block + 16
    table = []
    for b in range(n_out // block):
        runs = []
        for i in range(block):
            s = row_of(b * block + i)
            if s < 0:
                continue
            if runs and runs[-1][0] + runs[-1][2] == s and runs[-1][1] + runs[-1][2] == i:
                runs[-1][2] += 1
            else:
                runs.append([s, i, 1])
        assert len(runs) <= REMAP_RUNS, (b, runs)
        row = []
        for s, i, n in runs:
            w = min(s // 16 * 16, n_src - win)
            assert 0 <= s - w and s - w + n <= win
            row += [w, s - w, i, n]
        table.append(row + [0] * (4 * REMAP_RUNS - len(row)))
    return table


def _remap_rows(src, row_of, n_out, block, name):
    n_src, cols = src.shape
    nb, win = n_out // block, block + 16
    table = _remap_table(row_of, n_out, block, n_src)
    runs = [[tuple(row[4 * k:4 * k + 4]) for k in range(REMAP_RUNS) if row[4 * k + 3] > 0] for row in table]

    def body(src_hbm, out_hbm, wbuf, obuf, insem, outsem):
        def fetches(b):
            return [pltpu.make_async_copy(src_hbm.at[pl.ds(w, win)], wbuf.at[b % 2, k], insem.at[b % 2, k])
                    for k, (w, _, _, _) in enumerate(runs[b])]

        def store(b):
            return pltpu.make_async_copy(obuf.at[b % 2], out_hbm.at[pl.ds(b * block, block)], outsem.at[b % 2])

        for cp in fetches(0):
            cp.start()
        for b in range(nb):
            if b + 1 < nb:
                for cp in fetches(b + 1):
                    cp.start()
            for cp in fetches(b):
                cp.wait()
            if b >= 2:
                store(b - 2).wait()
            if sum(count for _, _, _, count in runs[b]) < block:
                obuf[b % 2] = jnp.zeros((block, cols), src.dtype)
            for k, (_, shift, first, count) in enumerate(runs[b]):
                obuf[b % 2, first:first + count, :] = wbuf[b % 2, k, shift:shift + count, :]
            store(b).start()
        store(nb - 2).wait()
        store(nb - 1).wait()

    hbm = pl.BlockSpec(memory_space=pl.ANY)
    return pl.pallas_call(
        body, name=name, in_specs=[hbm], out_specs=hbm, out_shape=S((n_out, cols), src.dtype),
        scratch_shapes=[pltpu.VMEM((2, REMAP_RUNS, win, cols), src.dtype), pltpu.VMEM((2, block, cols), src.dtype),
                        pltpu.SemaphoreType.DMA((2, REMAP_RUNS)), pltpu.SemaphoreType.DMA((2,))],
    )(src)


def _col_blocks(w, width):
    return w.reshape(w.shape[0], NDEV, width).transpose(1, 0, 2)


def _from_col_blocks(w):
    return w.transpose(1, 0, 2).reshape(w.shape[1], NDEV * w.shape[2])


def _local_step(x2, p2, pos, tgt, norm_g, qk_norm_q, qk_norm_k, gla_gate_b, gla_norm_g, ple_norm_g, w_al,
                weights=None, proj_side=None, unpack=None, dw_side_of=None, dh_side_of=None):
    half = ROT_DIM // 2
    inv8 = jnp.power(jnp.float32(ROPE_THETA), -jnp.arange(half, dtype=F32) * 2.0 / ROT_DIM)
    inv = jnp.tile(jnp.concatenate([inv8, inv8, jnp.zeros((HD - ROT_DIM,), F32)]), 2).reshape(1, 128)
    gq = jnp.tile(qk_norm_q, (1, 2))
    gk = jnp.tile(qk_norm_k, (1, 2))

    proj, h, got = _proj_rms(x2, norm_g, w_al, proj_side)
    if proj_side is not None:
        weights = unpack(got)
    w2p, w_att_f, w_gla_f, w_out_f, w_pg_f, w_ple_f = weights
    qkv = _qk_prep(proj, pos, inv, gq, gk)
    fwd = [_att_fwd(qkv[g], qkv[3 + g], qkv[6 + g], g, f"att_fwd{g}") for g in range(3)]
    att, lse, ain = _att_merge([f[0] for f in fwd], [f[1] for f in fwd], proj)
    o_gla, bin_, states = _gla_fwd(proj, w2p, gla_gate_b, gla_norm_g)
    ya, yb, y, x1 = _branches_fwd(ain, bin_, proj, x2, w_att_f, w_gla_f, w_out_f)
    n2, loss_v, dout, du, dw_ple = _ple_loss(x1, p2, tgt, ple_norm_g, w_pg_f, w_ple_f)

    dx1, dy, dg_ple, dw_pg, dw_out = _ple_bwd(du, n2, y, x1, dout, ple_norm_g, w_pg_f, w_out_f)
    dproj, dain, dbin, dw_att, dw_gla = _branches_bwd(dy, ya, yb, ain, bin_, proj, w_att_f, w_gla_f)
    dproj, da0, da1, da2, at1, at2, ls1, ls2 = _att_gate_bwd(dain, att, lse, proj, dproj)
    datts, atts, lses = (da0, da1, da2), (att[None], at1, at2), (lse[None], ls1, ls2)
    dproj, dw2, dbg, dgn = _gla_bwd(proj, w2p, gla_gate_b, gla_norm_g, o_gla, states, dbin, dproj)
    bwd = [_att_bwd(qkv[g], qkv[3 + g], qkv[6 + g], datts[g], atts[g], lses[g], g, f"att_bwd{g}") for g in range(3)]
    dproj, dgq, dgk = _qk_bwd(proj, pos, inv, gq, gk, [b[0] for b in bwd], [b[1] for b in bwd],
                              [b[2] for b in bwd], dproj)
    out = dict(loss=loss_v, dw2=dw2, dw_att=dw_att, dw_gla=dw_gla, dw_out=dw_out, dw_pg=dw_pg, dw_ple=dw_ple,
               dgq=dgq, dgk=dgk, dbg=dbg, dgn=dgn, dg_ple=dg_ple)
    if dw_side_of is None:
        dw_al = _mm(dproj, h, mode="tn", name="dw_in", tm=1536, tn=D, tk=T, out_dtype=BF16)
    else:
        dw_al, out["dw_side"] = _mm(dproj, h, mode="tn", name="dw_in", tm=1536, tn=D, tk=T, out_dtype=BF16,
                                    side=dw_side_of(out))
    grad_x, dg_norm, out["dh_side"] = _dh_rms(dproj, w_al, x2, norm_g, dx1,
                                              None if dh_side_of is None else dh_side_of(dw_al))
    out.update(grad_x=grad_x, dw_al=dw_al, dg_norm=dg_norm)
    return out


def kernel(x, p, positions, norm_g, w_in, qk_norm_q, qk_norm_k, gla_gate_w2, gla_gate_b, gla_norm_g, w_att_proj, w_gla_proj, w_out, ple_norm_g, w_ple_gate, w_ple, loss_target, m_norm_g, m_w_in, m_qk_norm_q, m_qk_norm_k, m_gla_gate_w2, m_gla_gate_b, m_gla_norm_g, m_w_att_proj, m_w_gla_proj, m_w_out, m_ple_norm_g, m_w_ple_gate, m_w_ple, v_norm_g, v_w_in, v_qk_norm_q, v_qk_norm_k, v_gla_gate_w2, v_gla_gate_b, v_gla_norm_g, v_w_att_proj, v_w_gla_proj, v_w_out, v_ple_norm_g, v_w_ple_gate, v_w_ple):
    x2, p2, tgt = x[0], p[0, 0], loss_target[0]
    pos = positions.astype(F32).reshape(T, 1)

    def pack_small(ins, outs):
        rows3_ref, cols3_ref = outs
        for j in range(3):
            rows3_ref[j] = ins[j][0].astype(BF16)
        cols3_ref[0:512, :] = ins[3][0].astype(BF16)
        cols3_ref[512:768, :] = ins[4][0].astype(BF16)
        cols3_ref[768:784, :] = jnp.zeros((GLR_N, 128), BF16)
        cols3_ref[768:784, 0:64] = ins[5][0].astype(BF16)

    mine = jnp.pad(w_in[0].T.astype(BF16), ((0, SLAB - W_IN_SHARD), (0, 0)))
    g_in, rows3, cols3 = _all_gather_by_chip(
        [mine], "gather_w_in", ([w_gla_proj, w_out, w_ple_gate, w_att_proj, w_ple, gla_gate_w2],
                                [S((3, 128, D), BF16), S((784, 128), BF16)], pack_small))
    w_al = _remap_rows(g_in.reshape(NDEV * SLAB, D), _slab_row_of_aligned, NCOL, 1536, "align_w_in")

    def unpack(got):
        g_rows, g_cols = got
        w2_f = _from_col_blocks(g_cols[:, 768:784, :64])
        return (jnp.pad(w2_f, ((0, GLR_W - GLR_N), (0, 0))), _from_col_blocks(g_cols[:, :512]),
                g_rows[:, 0].reshape(D, D), g_rows[:, 1].reshape(D, D), g_rows[:, 2].reshape(D, D),
                _from_col_blocks(g_cols[:, 512:768]))

    def dw_side_of(g):
        blocks = [g[k].reshape(NDEV, 128, D) for k in ("dw_gla", "dw_out", "dw_pg")]
        blocks += [_col_blocks(g["dw_att"], 128), _col_blocks(g["dw_ple"], 128),
                   jnp.pad(_col_blocks(g["dw2"][:GLR_N], 64), ((0, 0), (0, 0), (0, 64)))]
        return _exchange_side([b.astype(BF16) for b in blocks])

    def dh_side_of(dw_al):
        s_in = _remap_rows(dw_al, _aligned_row_of_slab, NDEV * SLAB, SLAB, "shard_dw_in").reshape(NDEV, SLAB, D)
        return _chips_side([_sibling_sum(s_in, "sibling_sum")])

    loc = _local_step(x2, p2, pos, tgt, norm_g, qk_norm_q, qk_norm_k, gla_gate_b, gla_norm_g, ple_norm_g, w_al,
                      proj_side=_gather_side([rows3, cols3]), unpack=unpack, dw_side_of=dw_side_of,
                      dh_side_of=dh_side_of)
    loss_v, grad_x = loc["loss"], loc["grad_x"]
    dg_norm, dgq, dgk, dbg, dgn, dg_ple = (loc[k] for k in ("dg_norm", "dgq", "dgk", "dbg", "dgn", "dg_ple"))
    (r_in,) = loc["dh_side"]

    r_small = _comm_call(_gather_side([dg_norm, dgq, dgk, dbg, dgn, dg_ple, loss_v]), "gather_small")

    outs = {}

    rows_of = lambda a: jnp.transpose(a, (2, 0, 1))
    outs["w_in"] = [jnp.transpose(o, (1, 2, 0))[0] for o in
                    _adamw_rows(r_in, rows_of(w_in), rows_of(m_w_in), rows_of(v_w_in), "adam_w_in")]
    places = (("w_gla_proj", 0, slice(None), slice(None), (w_gla_proj, m_w_gla_proj, v_w_gla_proj)),
              ("w_out", 1, slice(None), slice(None), (w_out, m_w_out, v_w_out)),
              ("w_ple_gate", 2, slice(None), slice(None), (w_ple_gate, m_w_ple_gate, v_w_ple_gate)),
              ("w_att_proj", 3, slice(None), slice(None), (w_att_proj, m_w_att_proj, v_w_att_proj)),
              ("w_ple", 4, slice(None), slice(None), (w_ple, m_w_ple, v_w_ple)),
              ("gla_gate_w2", 5, slice(None), slice(0, 64), (gla_gate_w2, m_gla_gate_w2, v_gla_gate_w2)))
    res = _adamw_shards(list(loc["dw_side"]), [place[1:] for place in places])
    for j, place in enumerate(places):
        outs[place[0]] = [o[0] for o in res[4 * j:4 * j + 4]]
    small = ((norm_g, m_norm_g, v_norm_g), (qk_norm_q, m_qk_norm_q, v_qk_norm_q), (qk_norm_k, m_qk_norm_k, v_qk_norm_k),
             (gla_gate_b, m_gla_gate_b, v_gla_gate_b), (gla_norm_g, m_gla_norm_g, v_gla_norm_g),
             (ple_norm_g, m_ple_norm_g, v_ple_norm_g))
    sm = _adamw_small(r_small[:6], small, r_small[6])
    for j, nm in enumerate(("norm_g", "qk_norm_q", "qk_norm_k", "gla_gate_b", "gla_norm_g", "ple_norm_g")):
        outs[nm] = [o[0] for o in sm[4 * j:4 * j + 4]]

    loss = sm[-1][0, 0]
    order = ["norm_g", "w_in", "qk_norm_q", "qk_norm_k", "gla_gate_w2", "gla_gate_b", "gla_norm_g", "w_att_proj",
             "w_gla_proj", "w_out", "ple_norm_g", "w_ple_gate", "w_ple"]
    result = [loss, grad_x[None]]
    for i in range(4):
        result += [outs[nm][i][None] for nm in order]
    return tuple(result)
```

```python
import functools

import jax
import jax.numpy as jnp
from jax import lax
from jax.experimental import pallas as pl
from jax.experimental.pallas import tpu as pltpu

F32 = jnp.float32
BF16 = jnp.bfloat16
S = jax.ShapeDtypeStruct

T = 4096
D = 1024
NDEV = 8
HD = 64
ATT_W = 512
ATT_QKV = 1536
DILATIONS = (1, 4, 16)
BLK = 128
GH, GDK, GDV = 4, 128, 256
GLA_C = 128
PLE = 256
EPS = 1e-6
ROT_DIM = 16
ROPE_THETA = 500000.0
GLA_TAU = 16.0
W_IN_SHARD = 1282

C_QG, C_KG, C_VG, C_ZG, C_GLR, C_ZA, C_GA, C_GB, C_QA, C_KA, C_VA = (
    0, 512, 1024, 2048, 3072, 3584, 4096, 5120, 6144, 7680, 9216)
GLA_GROUP_W = 3584
GLR_W = 512
NCOL = 10752
GLR_N = 16
O_QA, O_ZA, O_QG, O_GLR, O_ZG, O_GA, O_END = 0, 4608, 5120, 7168, 7184, 8208, 10256

ADAM_LR, ADAM_B1, ADAM_B2, ADAM_EPS, ADAM_WD, ADAM_STEP = 0.001, 0.9, 0.999, 1e-08, 0.01, 10

MESH = pl.DeviceIdType.MESH


def _sigmoid(z):
    return 1.0 / (1.0 + jnp.exp(-z))


def _dot(a, b, dims):
    return lax.dot_general(a, b, (dims, ((), ())), preferred_element_type=F32)


def _nn(a, b):
    return _dot(a, b, ((1,), (0,)))


def _nt(a, b):
    return _dot(a, b, ((1,), (1,)))


def _tn(a, b):
    return _dot(a, b, ((0,), (0,)))


def _mm(a, b, *, mode, name, tm, tn, tk, out_dtype=F32, res=None, side=None):
    if mode == "nn":
        (m, k), n = a.shape, b.shape[1]
        a_spec = pl.BlockSpec((tm, tk), lambda i, j, l: (i, l))
        b_spec = pl.BlockSpec((tk, tn), lambda i, j, l: (l, j))
        dot = _nn
    elif mode == "nt":
        (m, k), n = a.shape, b.shape[0]
        a_spec = pl.BlockSpec((tm, tk), lambda i, j, l: (i, l))
        b_spec = pl.BlockSpec((tn, tk), lambda i, j, l: (j, l))
        dot = _nt
    else:
        (k, m), n = a.shape, b.shape[1]
        a_spec = pl.BlockSpec((tk, tm), lambda i, j, l: (l, i))
        b_spec = pl.BlockSpec((tk, tn), lambda i, j, l: (l, j))
        dot = _tn
    assert m % tm == 0 and n % tn == 0 and k % tk == 0, (name, m, n, k)
    grid = (m // tm, n // tn, k // tk)
    nk = grid[2]
    o_spec = pl.BlockSpec((tm, tn), lambda i, j, l: (i, j))
    in_specs = [a_spec, b_spec]
    args = [a, b]
    if res is not None:
        in_specs.append(o_spec)
        args.append(res)
    n_in = len(args)
    n_side = 0 if side is None else len(side["arrs"])
    hbm = pl.BlockSpec(memory_space=pl.ANY)

    def body(*refs):
        a_ref, b_ref = refs[0], refs[1]
        r_ref = refs[2] if res is not None else None
        o_ref = refs[n_in + n_side]
        scratch = refs[n_in + 2 * n_side + 1:]
        if side is not None:
            start, finish_side = side["plan"](refs[n_in:n_in + n_side], refs[n_in + n_side + 1:n_in + 2 * n_side + 1],
                                              *scratch[1 if nk > 1 else 0:])
            ids = [pl.program_id(d) for d in range(3)]

            @pl.when((ids[0] == 0) & (ids[1] == 0) & (ids[2] == 0))
            def _():
                start()

        part = dot(a_ref[...].astype(BF16), b_ref[...].astype(BF16))

        def finish(val):
            if r_ref is not None:
                val = val + r_ref[...]
            o_ref[...] = val.astype(out_dtype)

        if nk == 1:
            finish(part)
        else:
            acc = scratch[0]
            l = pl.program_id(2)

            @pl.when(l == 0)
            def _():
                acc[...] = part

            @pl.when(l > 0)
            def _():
                acc[...] += part

            @pl.when(l == nk - 1)
            def _():
                finish(acc[...])

        if side is not None:
            @pl.when((ids[0] == grid[0] - 1) & (ids[1] == grid[1] - 1) & (ids[2] == grid[2] - 1))
            def _():
                finish_side()

    sems = [] if side is None else side["scratch"]
    outs = pl.pallas_call(
        body, name=name, grid=grid,
        in_specs=in_specs + [hbm] * n_side, out_specs=[o_spec] + [hbm] * n_side,
        out_shape=[S((m, n), out_dtype)] + ([] if side is None else side["out_shape"]),
        scratch_shapes=([pltpu.VMEM((tm, tn), F32)] if nk > 1 else []) + sems,
        compiler_params=pltpu.CompilerParams(
            dimension_semantics=("arbitrary",) * 3 if side is not None else ("parallel", "parallel", "arbitrary")),
    )(*args, *([] if side is None else side["arrs"]))
    return outs[0] if side is None else (outs[0], outs[1:])


def _side_parts(side, refs, n_in, n_out):
    n_side = 0 if side is None else len(side["arrs"])
    scratch = refs[n_in + n_out + 2 * n_side:]
    if side is None:
        return (lambda: None), (lambda: None), scratch
    start, finish = side["plan"](refs[n_in:n_in + n_side], refs[n_in + n_side + n_out:n_in + n_out + 2 * n_side],
                                 *scratch[len(scratch) - len(side["scratch"]):])
    return start, finish, scratch


def _proj_rms(x, g, wt, side=None):
    tm, tn = 1024, 1536
    grid = (T // tm, NCOL // tn)
    n_side = 0 if side is None else len(side["arrs"])
    hbm = pl.BlockSpec(memory_space=pl.ANY)

    def body(*refs):
        x_ref, g_ref, w_ref = refs[:3]
        o_ref, h_ref = refs[3 + n_side], refs[4 + n_side]
        start, finish, _ = _side_parts(side, refs, 3, 2)
        i, j = pl.program_id(0), pl.program_id(1)

        @pl.when((i == 0) & (j == 0))
        def _():
            start()

        @pl.when(j == 0)
        def _():
            xf = x_ref[...]
            r = lax.rsqrt(jnp.mean(xf * xf, axis=-1, keepdims=True) + EPS)
            h_ref[...] = (xf * r * g_ref[...]).astype(BF16)

        o_ref[...] = _nt(h_ref[...], w_ref[pl.ds(pl.multiple_of(j * tn, tn), tn), :])

        @pl.when((i == grid[0] - 1) & (j == grid[1] - 1))
        def _():
            finish()

    vmem_bytes = NCOL * D * 2 + 2 * (tm * D * 4 + tm * tn * 4 + tm * D * 2) + (4 << 20)
    outs = pl.pallas_call(
        body, name="proj", grid=grid,
        in_specs=[pl.BlockSpec((tm, D), lambda i, j: (i, 0)), pl.BlockSpec((1, D), lambda i, j: (0, 0)),
                  pl.BlockSpec((NCOL, D), lambda i, j: (0, 0), pipeline_mode=pl.Buffered(1))] + [hbm] * n_side,
        out_specs=[pl.BlockSpec((tm, tn), lambda i, j: (i, j)), pl.BlockSpec((tm, D), lambda i, j: (i, 0))] + [hbm] * n_side,
        out_shape=[S((T, NCOL), F32), S((T, D), BF16)] + ([] if side is None else side["out_shape"]),
        scratch_shapes=[] if side is None else side["scratch"],
        compiler_params=pltpu.CompilerParams(dimension_semantics=("arbitrary", "arbitrary"),
                                             vmem_limit_bytes=vmem_bytes),
    )(x, g, wt, *([] if side is None else side["arrs"]))
    return outs[0], outs[1], outs[2:]


def _dh_rms(dproj, wt, x, g, skip, side=None):
    tm, tk = 1024, 2688
    grid = (T // tm, NCOL // tk)
    n_side = 0 if side is None else len(side["arrs"])
    hbm = pl.BlockSpec(memory_space=pl.ANY)

    def body(*refs):
        a_ref, w_ref, x_ref, g_ref, s_ref = refs[:5]
        dx_ref, dg_ref = refs[5 + n_side], refs[6 + n_side]
        start, finish, scratch = _side_parts(side, refs, 5, 2)
        acc = scratch[0]
        i, l = pl.program_id(0), pl.program_id(1)

        @pl.when((i == 0) & (l == 0))
        def _():
            start()

        part = _nn(a_ref[...], w_ref[...])

        @pl.when(l == 0)
        def _():
            acc[...] = part

        @pl.when(l > 0)
        def _():
            acc[...] += part

        @pl.when(l == grid[1] - 1)
        def _():
            xf = x_ref[...]
            r = lax.rsqrt(jnp.mean(xf * xf, axis=-1, keepdims=True) + EPS)
            dn = acc[...]
            u = dn * g_ref[...]
            dx_ref[...] = s_ref[...] + r * u - xf * (r * r * r) * jnp.mean(u * xf, axis=-1, keepdims=True)
            dg = jnp.sum(dn * xf * r, axis=0, keepdims=True)

            @pl.when(i == 0)
            def _():
                dg_ref[...] = dg

            @pl.when(i > 0)
            def _():
                dg_ref[...] += dg

        @pl.when((i == grid[0] - 1) & (l == grid[1] - 1))
        def _():
            finish()

    tok = pl.BlockSpec((tm, D), lambda i, l: (i, 0))
    outs = pl.pallas_call(
        body, name="dh", grid=grid,
        in_specs=[pl.BlockSpec((tm, tk), lambda i, l: (i, l)), pl.BlockSpec((tk, D), lambda i, l: (l, 0)), tok,
                  pl.BlockSpec((1, D), lambda i, l: (0, 0)), tok] + [hbm] * n_side,
        out_specs=[tok, pl.BlockSpec((1, D), lambda i, l: (0, 0))] + [hbm] * n_side,
        out_shape=[S((T, D), F32), S((1, D), F32)] + ([] if side is None else side["out_shape"]),
        scratch_shapes=[pltpu.VMEM((tm, D), F32)] + ([] if side is None else side["scratch"]),
        compiler_params=pltpu.CompilerParams(dimension_semantics=("arbitrary", "arbitrary")),
    )(dproj, wt, x, g, skip, *([] if side is None else side["arrs"]))
    return outs[0], outs[1], outs[2:]


def _rot_tables(pos_ref, inv_ref):
    lane = lax.broadcasted_iota(jnp.int32, (1, 128), 1) % HD
    ang = pos_ref[...] * inv_ref[...]
    cos, sin = jnp.cos(ang), jnp.sin(ang)
    c = jnp.where(lane < ROT_DIM, cos, 1.0)
    sp = jnp.where((lane >= ROT_DIM // 2) & (lane < ROT_DIM), sin, 0.0)
    sm = jnp.where(lane < ROT_DIM // 2, -sin, 0.0)
    return c, sp, sm


def _head_sums(v):
    same = (lax.broadcasted_iota(jnp.int32, (128, 128), 0) < HD) == (lax.broadcasted_iota(jnp.int32, (128, 128), 1) < HD)
    ones = jnp.where(same, 1.0, 0.0).astype(BF16)
    hi = v.astype(BF16)
    lo = (v - hi.astype(F32)).astype(BF16)
    return _nn(hi, ones) + _nn(lo, ones)


def _pair_norm(t):
    return lax.rsqrt(_head_sums(t * t) * (1.0 / HD) + EPS)


def _pair_mean(t):
    return _head_sums(t) * (1.0 / HD)


TT = 256
NCH = ATT_QKV // 128


def _res_shape(grp, dtype):
    return S((DILATIONS[grp], T // DILATIONS[grp], ATT_W), dtype)


def _res_spec(grp):
    dil = DILATIONS[grp]
    return pl.BlockSpec((dil, TT // dil, ATT_W), lambda i: (0, i, 0))


def _to_residues(sc, j, dst_ref, dil, cols):
    n = TT // dil
    for r in range(dil):
        rows = sc[j] if dil == 1 else sc.at[j][pl.ds(r, n, stride=dil), :]
        dst_ref[r, :, cols] = rows.astype(dst_ref.dtype)


def _from_residues(src_ref, cols, sc, j, dil):
    n = TT // dil
    for r in range(dil):
        if dil == 1:
            sc[j] = src_ref[r, :, cols]
        else:
            sc.at[j][pl.ds(r, n, stride=dil), :] = src_ref[r, :, cols]


def _tok_spec(width, cblk=0):
    return pl.BlockSpec((TT, width), functools.partial(lambda i, c: (i, c), c=cblk))


def _const_spec(arr_or_shape):
    shape = arr_or_shape if isinstance(arr_or_shape, tuple) else arr_or_shape.shape
    return pl.BlockSpec(shape, functools.partial(lambda i, nd: (0,) * nd, nd=len(shape)))


def _qk_prep(proj, pos, inv, gq, gk):
    def body(q_ref, k_ref, v_ref, pos_ref, inv_ref, gq_ref, gk_ref, *rest):
        outs, sc = rest[:9], rest[9]
        c, sp, sm = _rot_tables(pos_ref, inv_ref)
        for which, (src, g_ref) in enumerate(((q_ref, gq_ref), (k_ref, gk_ref), (v_ref, None))):
            if g_ref is not None:
                g = jnp.broadcast_to(g_ref[...] * ((HD ** -0.5) if which == 0 else 1.0), c.shape)
                cg, spg, smg = c * g, sp * pltpu.roll(g, 8, 1), sm * pltpu.roll(g, 120, 1)
            for j in range(NCH):
                t = src[:, j * 128:(j + 1) * 128]
                if g_ref is not None:
                    t = _pair_norm(t) * (t * cg + pltpu.roll(t, 8, 1) * spg + pltpu.roll(t, 120, 1) * smg)
                sc[j] = t
            for j in range(NCH):
                grp, sub = divmod(j * 128, ATT_W)
                _to_residues(sc, j, outs[which * 3 + grp], DILATIONS[grp], slice(sub, sub + 128))

    return pl.pallas_call(
        body, name="qk_prep", grid=(T // TT,),
        in_specs=[_tok_spec(ATT_QKV, C_QA // ATT_QKV), _tok_spec(ATT_QKV, C_KA // ATT_QKV),
                  _tok_spec(ATT_QKV, C_VA // ATT_QKV), _tok_spec(1), _const_spec(inv), _const_spec(gq), _const_spec(gk)],
        out_specs=[_res_spec(g) for _ in range(3) for g in range(3)],
        out_shape=[_res_shape(g, BF16) for _ in range(3) for g in range(3)],
        scratch_shapes=[pltpu.VMEM((NCH, TT, 128), F32)],
        compiler_params=pltpu.CompilerParams(dimension_semantics=("arbitrary",)),
    )(proj, proj, proj, pos, inv, gq, gk)


def _qk_bwd(proj, pos, inv, gq, gk, dqs, dks, dvs, dproj):
    const = lambda a: pl.BlockSpec(a.shape, functools.partial(lambda i, nd: (0,) * nd, nd=a.ndim))
    res = lambda g: pl.BlockSpec((DILATIONS[g], TT // DILATIONS[g], ATT_W), lambda i: (0, i, 0))
    steps = T // TT

    def body(t_ref, pos_ref, inv_ref, gq_ref, gk_ref, dq0, dq1, dq2, dk0, dk1, dk2, dv0, dv1, dv2, buf_ref,
             out_ref, dgq_ref, dgk_ref, sc, obuf, sem):
        del buf_ref
        i = pl.program_id(0)
        first = i == 0
        tile = obuf.at[i % 2]

        def store(step):
            return pltpu.make_async_copy(
                obuf.at[step % 2], out_ref.at[pl.ds(pl.multiple_of(step * TT, TT), TT), pl.ds(C_QA, 3 * ATT_QKV)],
                sem.at[step % 2])

        @pl.when(i >= 2)
        def _():
            store(i - 2).wait()

        def gather(drefs):
            for j in range(NCH):
                grp, sub = divmod(j * 128, ATT_W)
                _from_residues(drefs[grp], slice(sub, sub + 128), sc, j, DILATIONS[grp])

        def normed(g_ref, drefs, dg_ref, col0):
            c, sp, sm = _rot_tables(pos_ref, inv_ref)
            gather(drefs)
            dg = jnp.zeros((1, 128), F32)
            for j in range(NCH):
                cols = slice(col0 + j * 128, col0 + (j + 1) * 128)
                d_rot = sc[j]
                dn = d_rot * c + pltpu.roll(d_rot * sp, 120, 1) + pltpu.roll(d_rot * sm, 8, 1)
                t = t_ref[:, cols]
                r = _pair_norm(t)
                gain = g_ref[...]
                dn_t = dn * t
                tile[:, cols] = (r * (dn * gain - t * ((r * r) * _pair_mean(dn_t * gain)))).astype(BF16)
                dg = dg + jnp.sum(dn_t * r, axis=0, keepdims=True)
            dg = dg + pltpu.roll(dg, HD, 1)

            @pl.when(first)
            def _():
                dg_ref[...] = dg

            @pl.when(jnp.logical_not(first))
            def _():
                dg_ref[...] += dg

        normed(gq_ref, (dq0, dq1, dq2), dgq_ref, 0)
        normed(gk_ref, (dk0, dk1, dk2), dgk_ref, ATT_QKV)
        gather((dv0, dv1, dv2))
        for j in range(NCH):
            tile[:, 2 * ATT_QKV + j * 128:2 * ATT_QKV + (j + 1) * 128] = sc[j].astype(BF16)
        store(i).start()

        @pl.when(i == steps - 1)
        def _():
            store(i - 1).wait()
            store(i).wait()

    keep = pl.BlockSpec((1, 128), lambda i: (0, 0))
    hbm = pl.BlockSpec(memory_space=pl.ANY)
    return pl.pallas_call(
        body, name="qk_bwd", grid=(steps,),
        in_specs=[pl.BlockSpec((TT, 2 * ATT_QKV), lambda i: (i, C_QA // (2 * ATT_QKV))),
                  pl.BlockSpec((TT, 1), lambda i: (i, 0)), const(inv), const(gq), const(gk)]
        + [res(g) for _ in range(3) for g in range(3)] + [hbm],
        out_specs=[hbm, keep, keep],
        out_shape=[S(dproj.shape, dproj.dtype), S((1, 128), F32), S((1, 128), F32)],
        input_output_aliases={14: 0},
        scratch_shapes=[pltpu.VMEM((NCH, TT, 128), F32), pltpu.VMEM((2, TT, 3 * ATT_QKV), BF16),
                        pltpu.SemaphoreType.DMA((2,))],
        compiler_params=pltpu.CompilerParams(dimension_semantics=("arbitrary",)),
    )(proj, pos, inv, gq, gk, *dqs, *dks, *dvs, dproj)


def _split_heads(t):
    low = lax.broadcasted_iota(jnp.int32, (1, 128), 1) < HD
    zero = jnp.zeros_like(t)
    return jnp.concatenate([jnp.where(low, t, zero), jnp.where(low, zero, t)], axis=0)


def _join_heads(t2):
    low = lax.broadcasted_iota(jnp.int32, (1, 128), 1) < HD
    n = t2.shape[0] // 2
    return jnp.where(low, t2[:n], t2[n:])


def _band_mask4(has_before, has_own):
    row = lax.broadcasted_iota(jnp.int32, (BLK, 4 * BLK), 0)
    lane = lax.broadcasted_iota(jnp.int32, (BLK, 4 * BLK), 1)
    key = lane & (BLK - 1)
    own = lane >= 2 * BLK
    return (own & (key <= row) & has_own) | (jnp.logical_not(own) & (key >= row) & has_before)


def _band_mask_before(has_before):
    row = lax.broadcasted_iota(jnp.int32, (BLK, 2 * BLK), 0)
    key = lax.broadcasted_iota(jnp.int32, (BLK, 2 * BLK), 1) & (BLK - 1)
    return (key >= row) & has_before


def _per_head(width, col_a, col_b):
    lane = lax.broadcasted_iota(jnp.int32, (1, width), 1)
    return jnp.where((lane & BLK) == 0, col_a, col_b)


NQ = ATT_W // 128


def _att_fwd(q, k, v, grp, name):
    dil = DILATIONS[grp]
    nb = T // dil // BLK

    def body(q_ref, kp_ref, kc_ref, vp_ref, vc_ref, o_ref, lse_ref, s_sc, p_sc):
        mask = _band_mask4(pl.program_id(1) > 0, True)
        low = lax.broadcasted_iota(jnp.int32, (1, 128), 1) < HD
        halves = lambda ref, j, h: (ref[j, :, h * BLK:(h + 1) * BLK], ref[j, :, (h + 2) * BLK:(h + 3) * BLK])
        for j in range(NQ):
            cols = slice(j * 128, (j + 1) * 128)
            k4 = jnp.concatenate([_split_heads(kp_ref[:, cols]), _split_heads(kc_ref[:, cols])], axis=0)
            s_sc[j] = jnp.where(mask, _nt(q_ref[:, cols], k4), -jnp.inf)
        mxs = [[jnp.maximum(*(jnp.max(t, axis=-1, keepdims=True) for t in halves(s_sc, j, h))) for h in range(2)]
               for j in range(NQ)]
        dens = []
        for j in range(NQ):
            p = jnp.exp(s_sc[j] - _per_head(4 * BLK, *mxs[j]))
            p_sc[j] = p.astype(BF16)
            dens.append([jnp.sum(p[:, h * BLK:(h + 1) * BLK], axis=-1, keepdims=True)
                         + jnp.sum(p[:, (h + 2) * BLK:(h + 3) * BLK], axis=-1, keepdims=True) for h in range(2)])
        for j in range(NQ):
            cols = slice(j * 128, (j + 1) * 128)
            v4 = jnp.concatenate([_split_heads(vp_ref[:, cols]), _split_heads(vc_ref[:, cols])], axis=0)
            o_ref[:, cols] = _nn(p_sc[j], v4) / jnp.where(low, dens[j][0], dens[j][1])
            lse_ref[:, cols] = jnp.where(low, mxs[j][0] + jnp.log(dens[j][0]), mxs[j][1] + jnp.log(dens[j][1]))

    cur = pl.BlockSpec((None, BLK, ATT_W), lambda r, i: (r, i, 0))
    prev = pl.BlockSpec((None, BLK, ATT_W), lambda r, i: (r, jnp.maximum(i - 1, 0), 0))
    return pl.pallas_call(
        body, name=name, grid=(dil, nb),
        in_specs=[cur, prev, cur, prev, cur],
        out_specs=[cur, cur], out_shape=[_res_shape(grp, F32)] * 2,
        scratch_shapes=[pltpu.VMEM((NQ, BLK, 4 * BLK), F32), pltpu.VMEM((NQ, BLK, 4 * BLK), BF16)],
        compiler_params=pltpu.CompilerParams(dimension_semantics=("parallel", "arbitrary")),
    )(q, k, k, v, v)


def _att_bwd(q, k, v, datt, att, lse, grp, name):
    dil = DILATIONS[grp]
    nb = T // dil // BLK
    scale = HD ** -0.5

    def body(q0_ref, q1_ref, kp_ref, kc_ref, vp_ref, vc_ref, do0_ref, do1_ref, o0_ref, o1_ref, l0_ref, l1_ref,
             dq_ref, dk_ref, dv_ref, k4_sc, v4_sc, s0_sc, s1_sc, dp0_sc, dp1_sc, p_sc, ds_sc):
        i = pl.program_id(1)
        mask_mine = _band_mask4(i > 0, True)
        mask_next = _band_mask_before(i < nb - 1)
        low = lax.broadcasted_iota(jnp.int32, (1, 128), 1) < HD
        for j in range(NQ):
            cols = slice(j * 128, (j + 1) * 128)
            k4_sc[j, :2 * BLK] = _split_heads(kp_ref[:, cols])
            k4_sc[j, 2 * BLK:] = _split_heads(kc_ref[:, cols])
            v4_sc[j, :2 * BLK] = _split_heads(vp_ref[:, cols])
            v4_sc[j, 2 * BLK:] = _split_heads(vc_ref[:, cols])
        for j in range(NQ):
            cols = slice(j * 128, (j + 1) * 128)
            s0_sc[j] = _nt(q0_ref[:, cols], k4_sc[j])
            s1_sc[j] = _nt(q1_ref[:, cols], k4_sc[j, 2 * BLK:])
            dp0_sc[j] = _nt(do0_ref[:, cols].astype(BF16), v4_sc[j])
            dp1_sc[j] = _nt(do1_ref[:, cols].astype(BF16), v4_sc[j, 2 * BLK:])
        stats = []
        for j in range(NQ):
            cols = slice(j * 128, (j + 1) * 128)
            for do_ref, o_ref, l_ref in ((do0_ref, o0_ref, l0_ref), (do1_ref, o1_ref, l1_ref)):
                prod = do_ref[:, cols].astype(F32) * o_ref[:, cols].astype(F32)
                d_all = jnp.sum(prod, axis=-1, keepdims=True)
                d_low = jnp.sum(jnp.where(low, prod, 0.0), axis=-1, keepdims=True)
                lse_t = l_ref[:, cols]
                stats.append((d_low, d_all - d_low, lse_t[:, 0:1], lse_t[:, HD:HD + 1]))
        for j in range(NQ):
            (da, db, la, lb), (da1, db1, la1, lb1) = stats[2 * j], stats[2 * j + 1]
            p0 = jnp.where(mask_mine, jnp.exp(s0_sc[j] - _per_head(4 * BLK, la, lb)), 0.0)
            ds0 = p0 * (dp0_sc[j] - _per_head(4 * BLK, da, db))
            p1 = jnp.where(mask_next, jnp.exp(s1_sc[j] - _per_head(2 * BLK, la1, lb1)), 0.0)
            ds1 = p1 * (dp1_sc[j] - _per_head(2 * BLK, da1, db1))
            p_sc[j, :BLK] = p0.astype(BF16)
            ds_sc[j, :BLK] = ds0.astype(BF16)
            p_sc[j, BLK:, 2 * BLK:] = p1.astype(BF16)
            ds_sc[j, BLK:, 2 * BLK:] = ds1.astype(BF16)
        for j in range(NQ):
            cols = slice(j * 128, (j + 1) * 128)
            dq_ref[:, cols] = _nn(ds_sc[j, :BLK], k4_sc[j]) * scale
            qq = jnp.concatenate([q0_ref[:, cols], q1_ref[:, cols]], axis=0)
            dd = jnp.concatenate([do0_ref[:, cols], do1_ref[:, cols]], axis=0).astype(BF16)
            dk_ref[:, cols] = _join_heads(_tn(ds_sc[j, :, 2 * BLK:], qq))
            dv_ref[:, cols] = _join_heads(_tn(p_sc[j, :, 2 * BLK:], dd))

    def spec(shift):
        return pl.BlockSpec((None, BLK, ATT_W), lambda r, i: (r, jnp.clip(i + shift, 0, nb - 1), 0))

    here, after, before = spec(0), spec(1), spec(-1)
    vm = pltpu.VMEM
    return pl.pallas_call(
        body, name=name, grid=(dil, nb),
        in_specs=[here, after, before, here, before, here, here, after, here, after, here, after],
        out_specs=[here] * 3, out_shape=[_res_shape(grp, F32)] * 3,
        scratch_shapes=[vm((NQ, 4 * BLK, 128), BF16), vm((NQ, 4 * BLK, 128), BF16), vm((NQ, BLK, 4 * BLK), F32),
                        vm((NQ, BLK, 2 * BLK), F32), vm((NQ, BLK, 4 * BLK), F32), vm((NQ, BLK, 2 * BLK), F32),
                        vm((NQ, 2 * BLK, 4 * BLK), BF16), vm((NQ, 2 * BLK, 4 * BLK), BF16)],
        compiler_params=pltpu.CompilerParams(dimension_semantics=("parallel", "arbitrary")),
    )(q, q, k, k, v, v, datt, datt, att, att, lse, lse)


def _att_merge(os_, lses, proj):
    nq = ATT_W // 128

    def body(o0, o1, o2, l0, l1, l2, za_ref, att_ref, lse_ref, ain_ref, sc):
        for a, ref in enumerate((o0, o1, o2, l0, l1, l2)):
            for j in range(nq):
                _from_residues(ref, slice(j * 128, (j + 1) * 128), sc, a * nq + j, DILATIONS[a % 3])
        for j in range(nq):
            cols = slice(j * 128, (j + 1) * 128)
            oa, ob, oc = (sc[a * nq + j] for a in range(3))
            la, lb, lc = (sc[(3 + a) * nq + j] for a in range(3))
            m = jnp.maximum(jnp.maximum(la, lb), lc)
            wa, wb, wc = jnp.exp(la - m), jnp.exp(lb - m), jnp.exp(lc - m)
            tot = wa + wb + wc
            att = (wa * oa + wb * ob + wc * oc) / tot
            att_ref[:, cols] = att
            lse_ref[:, cols] = m + jnp.log(tot)
            za = za_ref[:, cols]
            ain_ref[:, cols] = (att * za * _sigmoid(za)).astype(BF16)

    return pl.pallas_call(
        body, name="att_merge", grid=(T // TT,),
        in_specs=[_res_spec(g) for _ in range(2) for g in range(3)] + [_tok_spec(ATT_W, C_ZA // ATT_W)],
        out_specs=[_tok_spec(ATT_W)] * 3,
        out_shape=[S((T, ATT_W), F32), S((T, ATT_W), F32), S((T, ATT_W), BF16)],
        scratch_shapes=[pltpu.VMEM((6 * nq, TT, 128), F32)],
        compiler_params=pltpu.CompilerParams(dimension_semantics=("arbitrary",)),
    )(*os_, *lses, proj)


def _att_gate_bwd(dain, att, lse, proj, dproj):
    nq = ATT_W // 128

    def body(d_ref, att_ref, lse_ref, za_ref, buf_ref, dza_ref, da0, da1, da2, at1, at2, ls1, ls2, sc):
        del buf_ref
        for j in range(nq):
            cols = slice(j * 128, (j + 1) * 128)
            za = za_ref[:, cols]
            sg = _sigmoid(za)
            d = d_ref[:, cols].astype(F32)
            att_ = att_ref[:, cols]
            dza_ref[:, cols] = (d * att_ * sg * (1.0 + za * (1.0 - sg))).astype(BF16)
            sc[j] = d * za * sg
            sc[nq + j] = att_
            sc[2 * nq + j] = lse_ref[:, cols]
        for j in range(nq):
            cols = slice(j * 128, (j + 1) * 128)
            for grp, dst in enumerate((da0, da1, da2)):
                _to_residues(sc, j, dst, DILATIONS[grp], cols)
            for grp, dst in ((1, at1), (2, at2)):
                _to_residues(sc, nq + j, dst, DILATIONS[grp], cols)
            for grp, dst in ((1, ls1), (2, ls2)):
                _to_residues(sc, 2 * nq + j, dst, DILATIONS[grp], cols)

    res = (0, 1, 2, 1, 2, 1, 2)
    return pl.pallas_call(
        body, name="att_gate_bwd", grid=(T // TT,),
        in_specs=[_tok_spec(ATT_W)] * 3 + [_tok_spec(ATT_W, C_ZA // ATT_W), pl.BlockSpec(memory_space=pl.ANY)],
        out_specs=[_tok_spec(ATT_W, C_ZA // ATT_W)] + [_res_spec(g) for g in res],
        out_shape=[S(dproj.shape, dproj.dtype)] + [_res_shape(g, BF16) for g in res[:5]]
        + [_res_shape(g, F32) for g in res[5:]],
        input_output_aliases={4: 0},
        scratch_shapes=[pltpu.VMEM((3 * nq, TT, 128), F32)],
        compiler_params=pltpu.CompilerParams(dimension_semantics=("arbitrary",)),
    )(dain, att, lse, proj, dproj)


def _split3(v):
    hi = v.astype(BF16)
    r1 = v - hi.astype(F32)
    mid = r1.astype(BF16)
    lo = (r1 - mid.astype(F32)).astype(BF16)
    return hi, mid, lo


def _chunk_scores(qt, kt, q_ref, k_ref, h):
    cols = slice(h * GDK, (h + 1) * GDK)
    own = jnp.sum(q_ref[:, cols] * (GDK ** -0.5) * k_ref[:, cols], axis=-1, keepdims=True)
    row = lax.broadcasted_iota(jnp.int32, (GLA_C, GLA_C), 0)
    col = lax.broadcasted_iota(jnp.int32, (GLA_C, GLA_C), 1)
    a = _nt(qt.astype(BF16), kt.astype(BF16))
    return jnp.where(col < row, a, jnp.where(col == row, own, 0.0))


def _tri_sum(v, upper):
    n = v.shape[0]
    row = lax.broadcasted_iota(jnp.int32, (n, n), 0)
    col = lax.broadcasted_iota(jnp.int32, (n, n), 1)
    tri = jnp.where(col >= row if upper else col <= row, 1.0, 0.0).astype(BF16)
    hi, mid, lo = _split3(v)
    return _nn(tri, hi) + _nn(tri, mid) + _nn(tri, lo)


def _gla_gates(glr_ref, w2_ref, b_ref):
    logit = _nn(glr_ref[...].astype(BF16), w2_ref[...]) + b_ref[...]
    lg = (jnp.minimum(logit, 0.0) - jnp.log(1.0 + jnp.exp(-jnp.abs(logit)))) * (1.0 / GLA_TAU)
    return logit, _tri_sum(lg, upper=False)


def _gla_head(cum, q_ref, k_ref, h):
    cols = slice(h * GDK, (h + 1) * GDK)
    b = cum[:, cols]
    last = b[GLA_C - 1:GLA_C, :]
    e_pos = jnp.exp(b)
    e_neg = jnp.exp(-b)
    e_end = jnp.exp(last - b)
    qt = q_ref[:, cols] * (GDK ** -0.5) * e_pos
    kt = k_ref[:, cols] * e_neg
    kh = k_ref[:, cols] * e_end
    return b, last, e_pos, e_neg, e_end, qt, kt, kh


def _causal(n):
    return lax.broadcasted_iota(jnp.int32, (n, n), 1) <= lax.broadcasted_iota(jnp.int32, (n, n), 0)


def _gla_fwd(proj, w2p, bg, gn):
    nc = T // GLA_C

    def body(q_ref, k_ref, v_ref, glr_ref, zg_ref, w2_ref, b_ref, gn_ref, o_ref, bin_ref, st_ref, state):
        @pl.when(pl.program_id(0) == 0)
        def _():
            state[...] = jnp.zeros_like(state)

        _, cum = _gla_gates(glr_ref, w2_ref, b_ref)
        for h in range(GH):
            _, last, _, _, _, qt, kt, kh = _gla_head(cum, q_ref, k_ref, h)
            vcols = slice(h * GDV, (h + 1) * GDV)
            st = state[h]
            st_ref[0, h] = st
            v = v_ref[:, vcols].astype(BF16)
            qb = qt.astype(BF16)
            a = _chunk_scores(qt, kt, q_ref, k_ref, h)
            o = _nt(qb, st.astype(BF16)) + _nn(a.astype(BF16), v)
            state[h] = st * jnp.exp(last) + _tn(v, kh.astype(BF16))
            o_ref[:, vcols] = o
            r = lax.rsqrt(jnp.mean(o * o, axis=-1, keepdims=True) + EPS)
            zg = zg_ref[:, vcols]
            bin_ref[:, vcols] = (o * r * gn_ref[...] * zg * _sigmoid(zg)).astype(BF16)

    row = lambda width, cblk: pl.BlockSpec((GLA_C, width), functools.partial(lambda i, c: (i, c), c=cblk))
    full = lambda a: pl.BlockSpec(a.shape, functools.partial(lambda i, nd: (0,) * nd, nd=a.ndim))
    return pl.pallas_call(
        body, name="gla_fwd", grid=(nc,),
        in_specs=[row(512, C_QG // 512), row(512, C_KG // 512), row(1024, C_VG // 1024), row(GLR_W, C_GLR // GLR_W),
                  row(1024, C_ZG // 1024), full(w2p), full(bg), full(gn)],
        out_specs=[pl.BlockSpec((GLA_C, GH * GDV), lambda i: (i, 0)), pl.BlockSpec((GLA_C, GH * GDV), lambda i: (i, 0)),
                   pl.BlockSpec((1, GH, GDV, GDK), lambda i: (i, 0, 0, 0))],
        out_shape=[S((T, GH * GDV), F32), S((T, GH * GDV), BF16), S((nc, GH, GDV, GDK), F32)],
        scratch_shapes=[pltpu.VMEM((GH, GDV, GDK), F32)],
        compiler_params=pltpu.CompilerParams(dimension_semantics=("arbitrary",)),
    )(proj, proj, proj, proj, proj, w2p, bg, gn)


def _gla_bwd(proj, w2p, bg, gn, o_gla, states, dbin, dproj):
    nc = T // GLA_C

    def body(q_ref, k_ref, v_ref, glr_ref, zg_ref, w2_ref, b_ref, gn_ref, o_ref, st_ref, dbin_ref, buf_ref,
             out_ref, dw2_ref, dbg_ref, dgn_ref, dstate, dlogit):
        del buf_ref
        dq_ref = out_ref.at[:, C_QG:C_KG]
        dk_ref = out_ref.at[:, C_KG:C_VG]
        dv_ref = out_ref.at[:, C_VG:C_ZG]
        dzg_ref = out_ref.at[:, C_ZG:C_GLR]
        dglr_ref = out_ref.at[:, C_GLR:C_GLR + GLR_W]
        first = pl.program_id(0) == 0

        @pl.when(first)
        def _():
            dstate[...] = jnp.zeros_like(dstate)

        logit, cum = _gla_gates(glr_ref, w2_ref, b_ref)
        is_last = lax.broadcasted_iota(jnp.int32, (GLA_C, 1), 0) == GLA_C - 1
        dgn = jnp.zeros((1, GDV), F32)
        for h in range(GH):
            _, last, e_pos, e_neg, e_end, qt, kt, kh = _gla_head(cum, q_ref, k_ref, h)
            cols = slice(h * GDK, (h + 1) * GDK)
            vcols = slice(h * GDV, (h + 1) * GDV)
            o = o_ref[:, vcols]
            r = lax.rsqrt(jnp.mean(o * o, axis=-1, keepdims=True) + EPS)
            zg = zg_ref[:, vcols]
            sg = _sigmoid(zg)
            db_ = dbin_ref[:, vcols].astype(F32)
            dlin = db_ * zg * sg
            dzg_ref[:, vcols] = (db_ * (o * r * gn_ref[...]) * sg * (1.0 + zg * (1.0 - sg))).astype(BF16)
            u = dlin * gn_ref[...]
            do = (r * u - o * (r * r * r) * jnp.mean(u * o, axis=-1, keepdims=True)).astype(BF16)
            dgn = dgn + jnp.sum(dlin * o * r, axis=0, keepdims=True)
            st = st_ref[0, h]
            dst = dstate[h]
            v = v_ref[:, vcols].astype(BF16)
            qb, kb, khb = qt.astype(BF16), kt.astype(BF16), kh.astype(BF16)
            dstb = dst.astype(BF16)
            causal = _causal(GLA_C)
            a = _chunk_scores(qt, kt, q_ref, k_ref, h).astype(BF16)
            da = jnp.where(causal, _nt(do, v), 0.0).astype(BF16)
            dqt = _nn(do, st.astype(BF16)) + _nn(da, kb)
            dkt = _tn(da, qb)
            dkh = _nn(v, dstb)
            dv_ref[:, vcols] = (_tn(a, do) + _nt(khb, dstb)).astype(BF16)
            lam = jnp.exp(last)
            dlam = jnp.sum(dst * st, axis=0, keepdims=True)
            dstate[h] = dst * lam + _tn(do, qb)
            dq_ref[:, cols] = (dqt * e_pos * (GDK ** -0.5)).astype(BF16)
            dk_ref[:, cols] = (dkt * e_neg + dkh * e_end).astype(BF16)
            dkh_kh = dkh * kh
            dcum = dqt * qt - dkt * kt - dkh_kh
            dlast = jnp.sum(dkh_kh, axis=0, keepdims=True) + dlam * lam
            dcum = jnp.where(is_last, dcum + dlast, dcum)
            dlg = _tri_sum(dcum, upper=True)
            dlogit[:, cols] = dlg * (1.0 / GLA_TAU) * (1.0 - _sigmoid(logit[:, cols]))

        dl = dlogit[...]
        dlb = dl.astype(BF16)
        dglr_ref[...] = _nt(dlb, w2_ref[...]).astype(BF16)
        dw2 = _tn(glr_ref[...].astype(BF16), dlb)
        dbg = jnp.sum(dl, axis=0, keepdims=True)

        @pl.when(first)
        def _():
            dw2_ref[...] = dw2
            dbg_ref[...] = dbg
            dgn_ref[...] = dgn

        @pl.when(jnp.logical_not(first))
        def _():
            dw2_ref[...] += dw2
            dbg_ref[...] += dbg
            dgn_ref[...] += dgn

    rev = lambda i: nc - 1 - i
    row = lambda width, cblk: pl.BlockSpec((GLA_C, width), functools.partial(lambda i, c: (rev(i), c), c=cblk))
    full = lambda a: pl.BlockSpec(a.shape, functools.partial(lambda i, nd: (0,) * nd, nd=a.ndim))
    keep = lambda shape: pl.BlockSpec(shape, functools.partial(lambda i, nd: (0,) * nd, nd=len(shape)))
    return pl.pallas_call(
        body, name="gla_bwd", grid=(nc,),
        in_specs=[row(512, C_QG // 512), row(512, C_KG // 512), row(1024, C_VG // 1024), row(GLR_W, C_GLR // GLR_W),
                  row(1024, C_ZG // 1024), full(w2p), full(bg), full(gn), row(GH * GDV, 0),
                  pl.BlockSpec((1, GH, GDV, GDK), lambda i: (rev(i), 0, 0, 0)), row(GH * GDV, 0),
                  pl.BlockSpec(memory_space=pl.ANY)],
        out_specs=[row(GLA_GROUP_W, 0), keep((GLR_W, 512)), keep((1, 512)), keep((1, GDV))],
        out_shape=[S(dproj.shape, dproj.dtype), S((GLR_W, 512), F32), S((1, 512), F32), S((1, GDV), F32)],
        input_output_aliases={11: 0},
        scratch_shapes=[pltpu.VMEM((GH, GDV, GDK), F32), pltpu.VMEM((GLA_C, GH * GDK), F32)],
        compiler_params=pltpu.CompilerParams(dimension_semantics=("arbitrary",)),
    )(proj, proj, proj, proj, proj, w2p, bg, gn, o_gla, states, dbin, dproj)


RT = 512


def _rowchain(body, name, ins, outs, scratch=()):
    in_specs, args = [], []
    for spec in ins:
        if spec[0] == "tok":
            _, arr, width, cblk = spec
            in_specs.append(pl.BlockSpec((RT, width), functools.partial(lambda i, c: (i, c), c=cblk)))
        else:
            arr = spec[1]
            in_specs.append(pl.BlockSpec(arr.shape, functools.partial(lambda i, nd: (0,) * nd, nd=arr.ndim)))
        args.append(arr)
    out_specs, out_shape = [], []
    for spec in outs:
        if spec[0] == "tok":
            _, shape, dtype, width, cblk = spec
            out_specs.append(pl.BlockSpec((RT, width), functools.partial(lambda i, c: (i, c), c=cblk)))
        else:
            _, shape, dtype = spec
            out_specs.append(pl.BlockSpec(shape, functools.partial(lambda i, nd: (0,) * nd, nd=len(shape))))
        out_shape.append(S(shape, dtype))
    return pl.pallas_call(
        body, name=name, grid=(T // RT,), in_specs=in_specs, out_specs=out_specs, out_shape=out_shape,
        scratch_shapes=list(scratch), compiler_params=pltpu.CompilerParams(dimension_semantics=("arbitrary",)),
    )(*args)


def _tok(arr, width=None, cblk=0):
    return ("tok", arr, arr.shape[1] if width is None else width, cblk)


def _tok_out(dtype, width=D):
    return ("tok", (T, width), dtype, width, 0)


def _branches_fwd(ain, bin_, proj, x, w_att, w_gla, w_out):
    def body(ain_ref, bin_ref, g_ref, x_ref, wa_ref, wg_ref, wo_ref, ya_ref, yb_ref, y_ref, x1_ref):
        ya = _nn(ain_ref[...], wa_ref[...]).astype(BF16)
        yb = _nn(bin_ref[...], wg_ref[...]).astype(BF16)
        ya_ref[...] = ya
        yb_ref[...] = yb
        y = (_sigmoid(g_ref[:, :D]) * ya.astype(F32) + _sigmoid(g_ref[:, D:]) * yb.astype(F32)).astype(BF16)
        y_ref[...] = y
        x1_ref[...] = x_ref[...] + _nn(y, wo_ref[...])

    return _rowchain(body, "branches_fwd",
                     [_tok(ain), _tok(bin_), _tok(proj, 2 * D, C_GA // (2 * D)), _tok(x), ("all", w_att),
                      ("all", w_gla), ("all", w_out)],
                     [_tok_out(BF16), _tok_out(BF16), _tok_out(BF16), _tok_out(F32)])


def _accumulate(ref, part, first):
    @pl.when(first)
    def _():
        ref[...] = part

    @pl.when(jnp.logical_not(first))
    def _():
        ref[...] += part


def _ple_loss(x1, p, target, g2, w_pg, w_ple):
    def body(x1_ref, p_ref, t_ref, g_ref, wpg_ref, wple_ref, n2_ref, loss_ref, dout_ref, du_ref, dwple_ref, acc):
        first = pl.program_id(0) == 0
        x1 = x1_ref[...]
        r = lax.rsqrt(jnp.mean(x1 * x1, axis=-1, keepdims=True) + EPS)
        n2 = (x1 * r * g_ref[...]).astype(BF16)
        n2_ref[...] = n2
        pg = _sigmoid(_nn(n2, wpg_ref[...]))
        pb = p_ref[...].astype(BF16)
        e_ = _nn(pb, wple_ref[...])
        diff = x1 + e_ * pg - t_ref[...]
        _accumulate(acc, jnp.sum(diff * diff, axis=0, keepdims=True), first)
        dout = diff * (1.0 / D)
        dout_ref[...] = dout
        du_ref[...] = (dout * e_ * pg * (1.0 - pg)).astype(BF16)
        _accumulate(dwple_ref, _tn(pb, (dout * pg).astype(BF16)), first)
        loss_ref[...] = jnp.zeros((1, 128), F32) + jnp.sum(acc[...], axis=-1, keepdims=True) * (0.5 / D)

    return _rowchain(body, "ple_loss", [_tok(x1), _tok(p), _tok(target), ("all", g2), ("all", w_pg), ("all", w_ple)],
                     [_tok_out(BF16), ("acc", (1, 128), F32), _tok_out(F32), _tok_out(BF16), ("acc", (PLE, D), F32)],
                     scratch=[pltpu.VMEM((1, D), F32)])


def _ple_bwd(du, n2, y, x1, dout, g2, w_pg, w_out):
    def body(du_ref, n2_ref, y_ref, x1_ref, dout_ref, g_ref, wpg_ref, wo_ref, dx_ref, dy_ref, dg_ref, dwpg_ref,
             dwo_ref):
        first = pl.program_id(0) == 0
        x1 = x1_ref[...]
        r = lax.rsqrt(jnp.mean(x1 * x1, axis=-1, keepdims=True) + EPS)
        du_ = du_ref[...]
        dn = _nt(du_, wpg_ref[...])
        u = dn * g_ref[...]
        dx = dout_ref[...] + r * u - x1 * (r * r * r) * jnp.mean(u * x1, axis=-1, keepdims=True)
        dxb = dx.astype(BF16)
        dx_ref[...] = dx
        dy_ref[...] = _nt(dxb, wo_ref[...]).astype(BF16)
        _accumulate(dg_ref, jnp.sum(dn * x1 * r, axis=0, keepdims=True), first)
        _accumulate(dwpg_ref, _tn(n2_ref[...], du_), first)
        _accumulate(dwo_ref, _tn(y_ref[...], dxb), first)

    return _rowchain(body, "ple_bwd",
                     [_tok(du), _tok(n2), _tok(y), _tok(x1), _tok(dout), ("all", g2), ("all", w_pg), ("all", w_out)],
                     [_tok_out(F32), _tok_out(BF16), ("acc", (1, D), F32), ("acc", (D, D), F32), ("acc", (D, D), F32)])


def _branches_bwd(dy, ya, yb, ain, bin_, proj, w_att, w_gla):
    def body(dy_ref, ya_ref, yb_ref, ain_ref, bin_ref, g_ref, wa_ref, wg_ref, dg_ref, dain_ref, dbin_ref,
             dwa_ref, dwg_ref):
        first = pl.program_id(0) == 0
        dy_ = dy_ref[...].astype(F32)
        sa, sb = _sigmoid(g_ref[:, :D]), _sigmoid(g_ref[:, D:])
        dg_ref[:, :D] = (dy_ * ya_ref[...].astype(F32) * sa * (1.0 - sa)).astype(BF16)
        dg_ref[:, D:] = (dy_ * yb_ref[...].astype(F32) * sb * (1.0 - sb)).astype(BF16)
        dya = (dy_ * sa).astype(BF16)
        dyb = (dy_ * sb).astype(BF16)
        dain_ref[...] = _nt(dya, wa_ref[...]).astype(BF16)
        dbin_ref[...] = _nt(dyb, wg_ref[...]).astype(BF16)
        _accumulate(dwa_ref, _tn(ain_ref[...], dya), first)
        _accumulate(dwg_ref, _tn(bin_ref[...], dyb), first)

    gates = C_GA // (2 * D)
    return _rowchain(body, "branches_bwd",
                     [_tok(dy), _tok(ya), _tok(yb), _tok(ain), _tok(bin_), _tok(proj, 2 * D, gates), ("all", w_att),
                      ("all", w_gla)],
                     [("tok", (T, NCOL), BF16, 2 * D, gates), _tok_out(BF16, ATT_W), _tok_out(BF16),
                      ("acc", (ATT_W, D), F32), ("acc", (D, D), F32)])


def _peer(k):
    x, y, c = lax.axis_index("x"), lax.axis_index("y"), lax.axis_index("c")
    return (x ^ ((k >> 2) & 1), y ^ ((k >> 1) & 1), c ^ (k & 1))


def _my_index():
    return 4 * lax.axis_index("x") + 2 * lax.axis_index("y") + lax.axis_index("c")


def _peer_index(k):
    px, py, pc = _peer(k)
    return 4 * px + 2 * py + pc


def _pairwise_plan(src_of, dst_of, landed_of, own_src, own_dst):
    def plan(ins, outs, send, recv, local):
        n = len(ins)

        def own():
            return [pltpu.make_async_copy(own_src(ins[a]), own_dst(outs[a]), local.at[a]) for a in range(n)]

        def remote(k, a, src, dst):
            return pltpu.make_async_remote_copy(src_ref=src, dst_ref=dst, send_sem=send.at[k - 1, a],
                                                recv_sem=recv.at[k - 1, a], device_id=_peer(k), device_id_type=MESH)

        def sent():
            return [remote(k, a, src_of(ins[a], k), dst_of(outs[a])) for k in range(1, NDEV) for a in range(n)]

        def start():
            for cp in own() + sent():
                cp.start()

        def finish():
            for k in range(1, NDEV):
                for a in range(n):
                    remote(k, a, own_src(ins[a]), landed_of(outs[a], k)).wait_recv()
            for cp in sent():
                cp.wait_send()
            for cp in own():
                cp.wait()

        return start, finish

    return plan


def _pairwise_sems(n):
    return [pltpu.SemaphoreType.DMA((NDEV - 1, n)), pltpu.SemaphoreType.DMA((NDEV - 1, n)),
            pltpu.SemaphoreType.DMA((n,))]


def _gather_side(arrs):
    plan = _pairwise_plan(src_of=lambda i, k: i, dst_of=lambda o: o.at[_my_index()],
                          landed_of=lambda o, k: o.at[_peer_index(k)],
                          own_src=lambda i: i, own_dst=lambda o: o.at[_my_index()])
    return dict(arrs=arrs, out_shape=[S((NDEV,) + a.shape, a.dtype) for a in arrs],
                scratch=_pairwise_sems(len(arrs)), plan=plan)


def _exchange_side(arrs):
    plan = _pairwise_plan(src_of=lambda i, k: i.at[_peer_index(k)], dst_of=lambda o: o.at[_my_index()],
                          landed_of=lambda o, k: o.at[_peer_index(k)],
                          own_src=lambda i: i.at[_my_index()], own_dst=lambda o: o.at[_my_index()])
    return dict(arrs=arrs, out_shape=[S(a.shape, a.dtype) for a in arrs], scratch=_pairwise_sems(len(arrs)), plan=plan)


def _comm_call(side, name):
    n = len(side["arrs"])

    def body(*refs):
        start, finish = side["plan"](refs[:n], refs[n:2 * n], *refs[2 * n:])
        start()
        finish()

    hbm = pl.BlockSpec(memory_space=pl.ANY)
    return pl.pallas_call(body, name=name, in_specs=[hbm] * n, out_specs=[hbm] * n, out_shape=side["out_shape"],
                          scratch_shapes=side["scratch"])(*side["arrs"])


def _all_gather_by_chip(arrs, name, beside):
    n = len(arrs)
    extra, extra_shapes, work = beside
    n_in, n_out = n + len(extra), n + len(extra_shapes)

    def body(*refs):
        ins, outs = refs[:n], refs[n_in:n_in + n]
        send, recv, local = refs[n_in + n_out:]
        x, y, c = lax.axis_index("x"), lax.axis_index("y"), lax.axis_index("c")
        me, sibling = (x, y, c), (x, y, 1 - c)
        chips = [(1 - x, y), (x, 1 - y), (1 - x, 1 - y)]

        def copy(k, a, block, to, src=None):
            px, py, pc = block
            slot = outs[a].at[4 * px + 2 * py + pc]
            return pltpu.make_async_remote_copy(
                src_ref=slot if src is None else src, dst_ref=slot, send_sem=send.at[k, a], recv_sem=recv.at[k, a],
                device_id=to, device_id_type=MESH)

        north = c == 1
        via = (jnp.where(north, 1 - x, x), jnp.where(north, y, 1 - y))
        onward = (jnp.where(north, x, 1 - x), jnp.where(north, 1 - y, y), c)
        mine = [pltpu.make_async_copy(ins[a], outs[a].at[4 * x + 2 * y + c], local.at[a]) for a in range(n)]
        first = []
        for a in range(n):
            first.append(copy(0, a, me, sibling, src=ins[a]))
            first += [copy(1 + j, a, me, (*chips[j], c), src=ins[a]) for j in range(2)]
        for cp in mine + first:
            cp.start()
        work(refs[n:n_in], refs[n_in + n:n_in + n_out])
        passed = []
        for j in range(2):
            for a in range(n):
                copy(1 + j, a, (*chips[j], c), me).wait_recv()
                passed.append(copy(4 + j, a, (*chips[j], c), sibling))
                passed[-1].start()
        for a in range(n):
            passed.append(copy(3, a, (*via, c), onward))
            passed[-1].start()
        for a in range(n):
            copy(3, a, (*chips[2], c), me).wait_recv()
            passed.append(copy(6, a, (*chips[2], c), sibling))
            passed[-1].start()
        for a in range(n):
            copy(0, a, sibling, me).wait_recv()
        for j, chip in enumerate(chips):
            for a in range(n):
                copy(4 + j, a, (*chip, 1 - c), me).wait_recv()
        for cp in first + passed:
            cp.wait_send()
        for cp in mine:
            cp.wait()

    hbm, vmem = pl.BlockSpec(memory_space=pl.ANY), pl.BlockSpec(memory_space=pltpu.VMEM)
    return pl.pallas_call(
        body, name=name, in_specs=[hbm] * n + [vmem] * len(extra), out_specs=[hbm] * n + [vmem] * len(extra_shapes),
        out_shape=[S((NDEV,) + a.shape, a.dtype) for a in arrs] + list(extra_shapes),
        scratch_shapes=[pltpu.SemaphoreType.DMA((NDEV - 1, n)), pltpu.SemaphoreType.DMA((NDEV - 1, n)),
                        pltpu.SemaphoreType.DMA((n,))],
    )(*arrs, *extra)


NCHIP = 4


def _sibling_sum(src, name, tc=512):
    _, rows, cols = src.shape
    assert cols % tc == 0

    def body(src_ref, got_ref, out_ref, a_buf, b_buf, o_buf, send, recv, local):
        x, y, c = lax.axis_index("x"), lax.axis_index("y"), lax.axis_index("c")
        copies = [pltpu.make_async_remote_copy(
            src_ref=src_ref.at[2 * q + (1 - c)], dst_ref=got_ref.at[q], send_sem=send.at[q], recv_sem=recv.at[q],
            device_id=(x, y, 1 - c), device_id_type=MESH) for q in range(NCHIP)]
        for cp in copies:
            cp.start()
        tiles = [(q, pl.ds(t * tc, tc)) for q in range(NCHIP) for t in range(cols // tc)]

        def loads(n):
            q, tile = tiles[n]
            return [pltpu.make_async_copy(src_ref.at[2 * q + c, :, tile], a_buf.at[n % 2], local.at[n % 2, 0]),
                    pltpu.make_async_copy(got_ref.at[q, :, tile], b_buf.at[n % 2], local.at[n % 2, 1])]

        def store(n):
            q, tile = tiles[n]
            return pltpu.make_async_copy(o_buf.at[n % 2], out_ref.at[q, :, tile], local.at[n % 2, 2])

        def fetch(n):
            if n == 0 or tiles[n][0] != tiles[n - 1][0]:
                copies[tiles[n][0]].wait_recv()
            for cp in loads(n):
                cp.start()

        fetch(0)
        for n in range(len(tiles)):
            if n + 1 < len(tiles):
                fetch(n + 1)
            for cp in loads(n):
                cp.wait()
            if n >= 2:
                store(n - 2).wait()
            o_buf[n % 2] = (a_buf[n % 2].astype(F32) + b_buf[n % 2].astype(F32)).astype(BF16)
            store(n).start()
        store(len(tiles) - 2).wait()
        store(len(tiles) - 1).wait()
        for cp in copies:
            cp.wait_send()

    hbm = pl.BlockSpec(memory_space=pl.ANY)
    block = S((NCHIP, rows, cols), BF16)
    return pl.pallas_call(
        body, name=name, in_specs=[hbm], out_specs=[hbm, hbm], out_shape=[block, block],
        scratch_shapes=[pltpu.VMEM((2, rows, tc), BF16)] * 3
        + [pltpu.SemaphoreType.DMA((NCHIP,)), pltpu.SemaphoreType.DMA((NCHIP,)), pltpu.SemaphoreType.DMA((2, 3))],
    )(src)[1]


def _chips_side(arrs):
    def plan(ins, outs, send, recv, local):
        n = len(ins)

        def places():
            x, y, c = lax.axis_index("x"), lax.axis_index("y"), lax.axis_index("c")
            return 2 * x + y, c, [(1 - x, y), (x, 1 - y), (1 - x, 1 - y)]

        def own():
            here, _, _ = places()
            return [pltpu.make_async_copy(ins[a].at[here], outs[a].at[here], local.at[a]) for a in range(n)]

        def remote(j, a, src_slot, dst_slot):
            _, c, chips = places()
            cx, cy = chips[j]
            return pltpu.make_async_remote_copy(
                src_ref=ins[a].at[src_slot], dst_ref=outs[a].at[dst_slot], send_sem=send.at[j, a],
                recv_sem=recv.at[j, a], device_id=(cx, cy, c), device_id_type=MESH)

        def sent():
            here, _, chips = places()
            return [remote(j, a, 2 * cx + cy, here) for j, (cx, cy) in enumerate(chips) for a in range(n)]

        def start():
            for cp in own() + sent():
                cp.start()

        def finish():
            here, _, chips = places()
            for j, (cx, cy) in enumerate(chips):
                for a in range(n):
                    remote(j, a, here, 2 * cx + cy).wait_recv()
            for cp in sent():
                cp.wait_send()
            for cp in own():
                cp.wait()

        return start, finish

    n = len(arrs)
    return dict(arrs=arrs, out_shape=[S(a.shape, a.dtype) for a in arrs],
                scratch=[pltpu.SemaphoreType.DMA((NCHIP - 1, n)), pltpu.SemaphoreType.DMA((NCHIP - 1, n)),
                         pltpu.SemaphoreType.DMA((n,))], plan=plan)


def _adamw_shards(parts, places):
    n_src = len(parts)

    def body(*refs):
        srcs, rest = refs[:n_src], refs[n_src:]
        for j, (src, rows, cols, _) in enumerate(places):
            w_ref, m_ref, v_ref = rest[3 * j:3 * j + 3]
            outs = rest[3 * len(places) + 4 * j:3 * len(places) + 4 * j + 4]
            p_ref = srcs[src]
            g = p_ref[0, rows, cols].astype(F32)
            for s in range(1, p_ref.shape[0]):
                g = g + p_ref[s, rows, cols].astype(F32)
            delta, m_new, v_new = _adam_math(g, w_ref[0], m_ref[0], v_ref[0])
            for ref, val in zip(outs, (g, delta, m_new, v_new)):
                ref[0] = val

    flat = [a for place in places for a in place[3]]
    return pl.pallas_call(
        body, name="adam_shards",
        out_shape=[S(place[3][0].shape, F32) for place in places for _ in range(4)],
    )(*parts, *flat)


def _adam_math(g, w, m, v):
    c1 = 1.0 - ADAM_B1 ** ADAM_STEP
    c2 = 1.0 - ADAM_B2 ** ADAM_STEP
    m_new = ADAM_B1 * m + (1.0 - ADAM_B1) * g
    v_new = ADAM_B2 * v + (1.0 - ADAM_B2) * (g * g)
    return -ADAM_LR * ((m_new / c1) / (jnp.sqrt(v_new / c2) + ADAM_EPS) + ADAM_WD * w), m_new, v_new


def _adamw_small(parts, params, loss_parts):
    n = len(params)

    def body(*refs):
        p_refs, rest = refs[:n], refs[n + 1:]
        total = refs[n][0]
        for s in range(1, NDEV):
            total = total + refs[n][s]
        refs[-1][...] = total
        for j in range(n):
            w_ref, m_ref, v_ref = rest[3 * j:3 * j + 3]
            g_ref, d_ref, mo_ref, vo_ref = rest[3 * n + 4 * j:3 * n + 4 * j + 4]
            width = w_ref.shape[1]
            g = p_refs[j][0]
            for s in range(1, NDEV):
                g = g + p_refs[j][s]
            g = g[:, :width]
            delta, m_new, v_new = _adam_math(g, w_ref[...], m_ref[...], v_ref[...])
            g_ref[...] = g
            d_ref[...] = delta
            mo_ref[...] = m_new
            vo_ref[...] = v_new

    flat = [a for group in params for a in group]
    return pl.pallas_call(
        body, name="adam_small",
        out_shape=[S(group[0].shape, F32) for group in params for _ in range(4)] + [S((1, 128), F32)],
    )(*parts, loss_parts, *flat)


def _adamw_rows(parts, w, m, v, name, tc=256):
    rows, _, cols = w.shape
    nparts = parts.shape[0]
    nsteps = cols // tc

    def body(p_ref, w_hbm, m_hbm, v_hbm, g_hbm, d_hbm, mo_hbm, vo_hbm, inbuf, outbuf, insem, outsem):
        i = pl.program_id(0)
        slot = i & 1

        def view(ref, step):
            return ref.at[:, 0, pl.ds(pl.multiple_of(step * tc, tc), tc)]

        def fetch(step, sl):
            return [pltpu.make_async_copy(view(src, step), inbuf.at[sl, k], insem.at[sl, k])
                    for k, src in enumerate((w_hbm, m_hbm, v_hbm))]

        def write(step, sl):
            return [pltpu.make_async_copy(outbuf.at[sl, k], view(dst, step), outsem.at[sl, k])
                    for k, dst in enumerate((g_hbm, d_hbm, mo_hbm, vo_hbm))]

        @pl.when(i == 0)
        def _():
            for cp in fetch(0, 0):
                cp.start()

        @pl.when(i + 1 < nsteps)
        def _():
            for cp in fetch(i + 1, 1 - slot):
                cp.start()

        for cp in fetch(i, slot):
            cp.wait()

        @pl.when(i >= 2)
        def _():
            for cp in write(i - 2, slot):
                cp.wait()

        g = p_ref[0].astype(F32)
        for s in range(1, nparts):
            g = g + p_ref[s].astype(F32)
        g = g[:rows]
        delta, m_new, v_new = _adam_math(g, inbuf[slot, 0], inbuf[slot, 1], inbuf[slot, 2])
        for k, val in enumerate((g, delta, m_new, v_new)):
            outbuf[slot, k] = val
        for cp in write(i, slot):
            cp.start()

        @pl.when(i == nsteps - 1)
        def _():
            for cp in write(i - 1, 1 - slot) + write(i, slot):
                cp.wait()

    hbm = pl.BlockSpec(memory_space=pl.ANY)
    assert nsteps >= 2
    return pl.pallas_call(
        body, name=name, grid=(nsteps,),
        in_specs=[pl.BlockSpec((nparts, parts.shape[1], tc), lambda i: (0, 0, i)), hbm, hbm, hbm],
        out_specs=[hbm] * 4, out_shape=[S((rows, 1, cols), F32)] * 4,
        scratch_shapes=[pltpu.VMEM((2, 3, rows, tc), F32), pltpu.VMEM((2, 4, rows, tc), F32),
                        pltpu.SemaphoreType.DMA((2, 3)), pltpu.SemaphoreType.DMA((2, 4))],
        compiler_params=pltpu.CompilerParams(dimension_semantics=("arbitrary",)),
    )(parts, w, m, v)


SLAB = 1296
REMAP_RUNS = 4
_PIECES = ((O_QA, O_ZA, C_QA), (O_ZA, O_QG, C_ZA), (O_QG, O_GLR, C_QG), (O_GLR, O_ZG, C_GLR), (O_ZG, O_GA, C_ZG),
           (O_GA, O_END, C_GA))


def _slab_row_of_aligned(a):
    for o0, o1, a0 in _PIECES:
        if a0 <= a < a0 + o1 - o0:
            c = o0 + a - a0
            return SLAB * (c // W_IN_SHARD) + c % W_IN_SHARD
    return -1


def _aligned_row_of_slab(r):
    d, l = divmod(r, SLAB)
    if l >= W_IN_SHARD:
        return -1
    c = d * W_IN_SHARD + l
    for o0, o1, a0 in _PIECES:
        if o0 <= c < o1:
            return a0 + c - o0
    raise AssertionError(c)


def _remap_table(row_of, n_out, block, n_src):
    win = block + 16
    table = []
    for b in range(n_out // block):
        runs = []
        for i in range(block):
            s = row_of(b * block + i)
            if s < 0:
                continue
            if runs and runs[-1][0] + runs[-1][2] == s and runs[-1][1] + runs[-1][2] == i:
                runs[-1][2] += 1
            else:
                runs.append([s, i, 1])
        assert len(runs) <= REMAP_RUNS, (b, runs)
        row = []
        for s, i, n in runs:
            w = min(s // 16 * 16, n_src - win)
            assert 0 <= s - w and s - w + n <= win
            row += [w, s - w, i, n]
        table.append(row + [0] * (4 * REMAP_RUNS - len(row)))
    return table


def _remap_rows(src, row_of, n_out, block, name):
    n_src, cols = src.shape
    nb, win = n_out // block, block + 16
    table = _remap_table(row_of, n_out, block, n_src)
    runs = [[tuple(row[4 * k:4 * k + 4]) for k in range(REMAP_RUNS) if row[4 * k + 3] > 0] for row in table]

    def body(src_hbm, out_hbm, wbuf, obuf, insem, outsem):
        def fetches(b):
            return [pltpu.make_async_copy(src_hbm.at[pl.ds(w, win)], wbuf.at[b % 2, k], insem.at[b % 2, k])
                    for k, (w, _, _, _) in enumerate(runs[b])]

        def store(b):
            return pltpu.make_async_copy(obuf.at[b % 2], out_hbm.at[pl.ds(b * block, block)], outsem.at[b % 2])

        for cp in fetches(0):
            cp.start()
        for b in range(nb):
            if b + 1 < nb:
                for cp in fetches(b + 1):
                    cp.start()
            for cp in fetches(b):
                cp.wait()
            if b >= 2:
                store(b - 2).wait()
            if sum(count for _, _, _, count in runs[b]) < block:
                obuf[b % 2] = jnp.zeros((block, cols), src.dtype)
            for k, (_, shift, first, count) in enumerate(runs[b]):
                obuf[b % 2, first:first + count, :] = wbuf[b % 2, k, shift:shift + count, :]
            store(b).start()
        store(nb - 2).wait()
        store(nb - 1).wait()

    hbm = pl.BlockSpec(memory_space=pl.ANY)
    return pl.pallas_call(
        body, name=name, in_specs=[hbm], out_specs=hbm, out_shape=S((n_out, cols), src.dtype),
        scratch_shapes=[pltpu.VMEM((2, REMAP_RUNS, win, cols), src.dtype), pltpu.VMEM((2, block, cols), src.dtype),
                        pltpu.SemaphoreType.DMA((2, REMAP_RUNS)), pltpu.SemaphoreType.DMA((2,))],
    )(src)


def _col_blocks(w, width):
    return w.reshape(w.shape[0], NDEV, width).transpose(1, 0, 2)


def _from_col_blocks(w):
    return w.transpose(1, 0, 2).reshape(w.shape[1], NDEV * w.shape[2])


def _local_step(x2, p2, pos, tgt, norm_g, qk_norm_q, qk_norm_k, gla_gate_b, gla_norm_g, ple_norm_g, w_al,
                weights=None, proj_side=None, unpack=None, dw_side_of=None, dh_side_of=None):
    half = ROT_DIM // 2
    inv8 = jnp.power(jnp.float32(ROPE_THETA), -jnp.arange(half, dtype=F32) * 2.0 / ROT_DIM)
    inv = jnp.tile(jnp.concatenate([inv8, inv8, jnp.zeros((HD - ROT_DIM,), F32)]), 2).reshape(1, 128)
    gq = jnp.tile(qk_norm_q, (1, 2))
    gk = jnp.tile(qk_norm_k, (1, 2))

    proj, h, got = _proj_rms(x2, norm_g, w_al, proj_side)
    if proj_side is not None:
        weights = unpack(got)
    w2p, w_att_f, w_gla_f, w_out_f, w_pg_f, w_ple_f = weights
    qkv = _qk_prep(proj, pos, inv, gq, gk)
    fwd = [_att_fwd(qkv[g], qkv[3 + g], qkv[6 + g], g, f"att_fwd{g}") for g in range(3)]
    att, lse, ain = _att_merge([f[0] for f in fwd], [f[1] for f in fwd], proj)
    o_gla, bin_, states = _gla_fwd(proj, w2p, gla_gate_b, gla_norm_g)
    ya, yb, y, x1 = _branches_fwd(ain, bin_, proj, x2, w_att_f, w_gla_f, w_out_f)
    n2, loss_v, dout, du, dw_ple = _ple_loss(x1, p2, tgt, ple_norm_g, w_pg_f, w_ple_f)

    dx1, dy, dg_ple, dw_pg, dw_out = _ple_bwd(du, n2, y, x1, dout, ple_norm_g, w_pg_f, w_out_f)
    dproj, dain, dbin, dw_att, dw_gla = _branches_bwd(dy, ya, yb, ain, bin_, proj, w_att_f, w_gla_f)
    dproj, da0, da1, da2, at1, at2, ls1, ls2 = _att_gate_bwd(dain, att, lse, proj, dproj)
    datts, atts, lses = (da0, da1, da2), (att[None], at1, at2), (lse[None], ls1, ls2)
    dproj, dw2, dbg, dgn = _gla_bwd(proj, w2p, gla_gate_b, gla_norm_g, o_gla, states, dbin, dproj)
    bwd = [_att_bwd(qkv[g], qkv[3 + g], qkv[6 + g], datts[g], atts[g], lses[g], g, f"att_bwd{g}") for g in range(3)]
    dproj, dgq, dgk = _qk_bwd(proj, pos, inv, gq, gk, [b[0] for b in bwd], [b[1] for b in bwd],
                              [b[2] for b in bwd], dproj)
    out = dict(loss=loss_v, dw2=dw2, dw_att=dw_att, dw_gla=dw_gla, dw_out=dw_out, dw_pg=dw_pg, dw_ple=dw_ple,
               dgq=dgq, dgk=dgk, dbg=dbg, dgn=dgn, dg_ple=dg_ple)
    if dw_side_of is None:
        dw_al = _mm(dproj, h, mode="tn", name="dw_in", tm=1536, tn=D, tk=T, out_dtype=BF16)
    else:
        dw_al, out["dw_side"] = _mm(dproj, h, mode="tn", name="dw_in", tm=1536, tn=D, tk=T, out_dtype=BF16,
                                    side=dw_side_of(out))
    grad_x, dg_norm, out["dh_side"] = _dh_rms(dproj, w_al, x2, norm_g, dx1,
                                              None if dh_side_of is None else dh_side_of(dw_al))
    out.update(grad_x=grad_x, dw_al=dw_al, dg_norm=dg_norm)
    return out


def kernel(x, p, positions, norm_g, w_in, qk_norm_q, qk_norm_k, gla_gate_w2, gla_gate_b, gla_norm_g, w_att_proj, w_gla_proj, w_out, ple_norm_g, w_ple_gate, w_ple, loss_target, m_norm_g, m_w_in, m_qk_norm_q, m_qk_norm_k, m_gla_gate_w2, m_gla_gate_b, m_gla_norm_g, m_w_att_proj, m_w_gla_proj, m_w_out, m_ple_norm_g, m_w_ple_gate, m_w_ple, v_norm_g, v_w_in, v_qk_norm_q, v_qk_norm_k, v_gla_gate_w2, v_gla_gate_b, v_gla_norm_g, v_w_att_proj, v_w_gla_proj, v_w_out, v_ple_norm_g, v_w_ple_gate, v_w_ple):
    x2, p2, tgt = x[0], p[0, 0], loss_target[0]
    pos = positions.astype(F32).reshape(T, 1)

    def pack_small(ins, outs):
        rows3_ref, cols3_ref = outs
        for j in range(3):
            rows3_ref[j] = ins[j][0].astype(BF16)
        cols3_ref[0:512, :] = ins[3][0].astype(BF16)
        cols3_ref[512:768, :] = ins[4][0].astype(BF16)
        cols3_ref[768:784, :] = jnp.zeros((GLR_N, 128), BF16)
        cols3_ref[768:784, 0:64] = ins[5][0].astype(BF16)

    mine = jnp.pad(w_in[0].T.astype(BF16), ((0, SLAB - W_IN_SHARD), (0, 0)))
    g_in, rows3, cols3 = _all_gather_by_chip(
        [mine], "gather_w_in", ([w_gla_proj, w_out, w_ple_gate, w_att_proj, w_ple, gla_gate_w2],
                                [S((3, 128, D), BF16), S((784, 128), BF16)], pack_small))
    w_al = _remap_rows(g_in.reshape(NDEV * SLAB, D), _slab_row_of_aligned, NCOL, 1536, "align_w_in")

    def unpack(got):
        g_rows, g_cols = got
        w2_f = _from_col_blocks(g_cols[:, 768:784, :64])
        return (jnp.pad(w2_f, ((0, GLR_W - GLR_N), (0, 0))), _from_col_blocks(g_cols[:, :512]),
                g_rows[:, 0].reshape(D, D), g_rows[:, 1].reshape(D, D), g_rows[:, 2].reshape(D, D),
                _from_col_blocks(g_cols[:, 512:768]))

    def dw_side_of(g):
        s_rows = jnp.concatenate([g[k].reshape(NDEV, 128, D) for k in ("dw_gla", "dw_out", "dw_pg")], axis=1)
        s_cols = jnp.concatenate([_col_blocks(g["dw_att"], 128), _col_blocks(g["dw_ple"], 128),
                                  jnp.pad(_col_blocks(g["dw2"][:GLR_N], 64), ((0, 0), (0, 0), (0, 64)))], axis=1)
        return _exchange_side([s_rows.astype(BF16), s_cols.astype(BF16)])

    def dh_side_of(dw_al):
        s_in = _remap_rows(dw_al, _aligned_row_of_slab, NDEV * SLAB, SLAB, "shard_dw_in").reshape(NDEV, SLAB, D)
        return _chips_side([_sibling_sum(s_in, "sibling_sum")])

    loc = _local_step(x2, p2, pos, tgt, norm_g, qk_norm_q, qk_norm_k, gla_gate_b, gla_norm_g, ple_norm_g, w_al,
                      proj_side=_gather_side([rows3, cols3]), unpack=unpack, dw_side_of=dw_side_of,
                      dh_side_of=dh_side_of)
    loss_v, grad_x = loc["loss"], loc["grad_x"]
    dg_norm, dgq, dgk, dbg, dgn, dg_ple = (loc[k] for k in ("dg_norm", "dgq", "dgk", "dbg", "dgn", "dg_ple"))
    r_rows, r_cols = loc["dw_side"]
    (r_in,) = loc["dh_side"]

    r_small = _comm_call(_gather_side([dg_norm, dgq, dgk, dbg, dgn, dg_ple, loss_v]), "gather_small")

    outs = {}

    rows_of = lambda a: jnp.transpose(a, (2, 0, 1))
    outs["w_in"] = [jnp.transpose(o, (1, 2, 0))[0] for o in
                    _adamw_rows(r_in, rows_of(w_in), rows_of(m_w_in), rows_of(v_w_in), "adam_w_in")]
    places = (("w_gla_proj", 0, slice(0, 128), slice(None), (w_gla_proj, m_w_gla_proj, v_w_gla_proj)),
              ("w_out", 0, slice(128, 256), slice(None), (w_out, m_w_out, v_w_out)),
              ("w_ple_gate", 0, slice(256, 384), slice(None), (w_ple_gate, m_w_ple_gate, v_w_ple_gate)),
              ("w_att_proj", 1, slice(0, 512), slice(None), (w_att_proj, m_w_att_proj, v_w_att_proj)),
              ("w_ple", 1, slice(512, 768), slice(None), (w_ple, m_w_ple, v_w_ple)),
              ("gla_gate_w2", 1, slice(768, 784), slice(0, 64), (gla_gate_w2, m_gla_gate_w2, v_gla_gate_w2)))
    res = _adamw_shards([r_rows, r_cols], [place[1:] for place in places])
    for j, place in enumerate(places):
        outs[place[0]] = [o[0] for o in res[4 * j:4 * j + 4]]
    small = ((norm_g, m_norm_g, v_norm_g), (qk_norm_q, m_qk_norm_q, v_qk_norm_q), (qk_norm_k, m_qk_norm_k, v_qk_norm_k),
             (gla_gate_b, m_gla_gate_b, v_gla_gate_b), (gla_norm_g, m_gla_norm_g, v_gla_norm_g),
             (ple_norm_g, m_ple_norm_g, v_ple_norm_g))
    sm = _adamw_small(r_small[:6], small, r_small[6])
    for j, nm in enumerate(("norm_g", "qk_norm_q", "qk_norm_k", "gla_gate_b", "gla_norm_g", "ple_norm_g")):
        outs[nm] = [o[0] for o in sm[4 * j:4 * j + 4]]

    loss = sm[-1][0, 0]
    order = ["norm_g", "w_in", "qk_norm_q", "qk_norm_k", "gla_gate_w2", "gla_gate_b", "gla_norm_g", "w_att_proj",
             "w_gla_proj", "w_out", "ple_norm_g", "w_ple_gate", "w_ple"]
    result = [loss, grad_x[None]]
    for i in range(4):
        result += [outs[nm][i][None] for nm in order]
    return tuple(result)
```
